```python
import math
import jax, jax.numpy as jnp
from jax import lax
import numpy as np

D_MODEL = 1024
BATCH = 8
SEQ = 8192
DEPTH = 1

MEM_LEN = 256

SB_HEADS = 8
SB_HEAD_DIM = D_MODEL // 16
SB_WIDTH = SB_HEADS * SB_HEAD_DIM
SB_BLOCK = 128

RET_HEADS = 4
RET_QK_DIM = D_MODEL // 8
RET_V_DIM = 2 * RET_QK_DIM
RET_QK_WIDTH = RET_HEADS * RET_QK_DIM
RET_V_WIDTH = RET_HEADS * RET_V_DIM
RET_CHUNK = 128
ROPE_BASE = 10000.0

N_BRANCHES = 2

OFF_SB_Q = 0
OFF_SB_K = OFF_SB_Q + SB_WIDTH
OFF_SB_V = OFF_SB_K + SB_WIDTH
OFF_RET_Q = OFF_SB_V + SB_WIDTH
OFF_RET_K = OFF_RET_Q + RET_QK_WIDTH
OFF_RET_V = OFF_RET_K + RET_QK_WIDTH
OFF_RET_G = OFF_RET_V + RET_V_WIDTH
OFF_GATE = OFF_RET_G + RET_V_WIDTH
IN_WIDTH = OFF_GATE + N_BRANCHES * D_MODEL

MEM_HEADS = 4
MEM_HEAD_DIM = D_MODEL // MEM_HEADS

FFN_HIDDEN = -(-8 * D_MODEL // (3 * 256)) * 256

DN_ALPHA = (2.0 * DEPTH) ** 0.25
DN_BETA = (8.0 * DEPTH) ** -0.25
LN_EPS = 1e-5

kernel_name = "hybrid_stickbreak_retention_deepnorm"


def layer_norm(x, g, b):
    xf = x.astype(jnp.float32)
    mu = jnp.mean(xf, -1, keepdims=True)
    var = jnp.mean(jnp.square(xf - mu), -1, keepdims=True)
    return ((xf - mu) * lax.rsqrt(var + LN_EPS)).astype(x.dtype) * g + b


def head_norm(x):
    xf = x.astype(jnp.float32)
    mu = jnp.mean(xf, -1, keepdims=True)
    var = jnp.mean(jnp.square(xf - mu), -1, keepdims=True)
    return (xf - mu) * lax.rsqrt(var + LN_EPS)


def rotary(x):
    S, d = x.shape[1], x.shape[-1]
    half = d // 2
    inv = 1.0 / (ROPE_BASE ** (jnp.arange(half, dtype=jnp.float32) / half))
    ang = jnp.arange(S, dtype=jnp.float32)[:, None] * inv[None, :]
    cos = jnp.cos(ang)[None, :, None, :]
    sin = jnp.sin(ang)[None, :, None, :]
    xf = x.astype(jnp.float32)
    x1, x2 = xf[..., :half], xf[..., half:]
    return jnp.concatenate([x1 * cos - x2 * sin, x1 * sin + x2 * cos], -1).astype(x.dtype)


def stick_breaking_attention(q, k, v):
    B, S, H, d = q.shape
    nb = S // SB_BLOCK
    qb = (q.astype(jnp.float32) * (d ** -0.5)).reshape(B, nb, SB_BLOCK, H, d).transpose(1, 0, 3, 2, 4)
    kf = k.astype(jnp.float32).transpose(0, 2, 1, 3)
    vf = v.astype(jnp.float32).transpose(0, 2, 1, 3)
    key_pos = jnp.arange(S)

    def one_block(args):
        q_blk, blk = args
        q_pos = blk * SB_BLOCK + jnp.arange(SB_BLOCK)
        mask = key_pos[None, :] < q_pos[:, None]
        z = jnp.einsum('bhqd,bhkd->bhqk', q_blk, kf)
        log_beta = jax.nn.log_sigmoid(z)
        log_rem = jnp.where(mask, jax.nn.log_sigmoid(-z), 0.0)
        later = lax.cumsum(log_rem, axis=3, reverse=True) - log_rem
        w = jnp.where(mask, jnp.exp(log_beta + later), 0.0)
        return jnp.einsum('bhqk,bhkd->bhqd', w, vf)

    out = lax.map(one_block, (qb, jnp.arange(nb)))
    return out.transpose(1, 0, 3, 2, 4).reshape(B, S, H * d)


def retention(q, k, v):
    B, S, H, dk = q.shape
    dv = v.shape[-1]
    nc = S // RET_CHUNK
    log_gamma = jnp.log1p(-jnp.exp2(-5.0 - jnp.arange(H, dtype=jnp.float32)))
    qc = (q.astype(jnp.float32) * (dk ** -0.5)).reshape(B, nc, RET_CHUNK, H, dk)
    kc = k.astype(jnp.float32).reshape(B, nc, RET_CHUNK, H, dk)
    vc = v.astype(jnp.float32).reshape(B, nc, RET_CHUNK, H, dv)
    idx = jnp.arange(RET_CHUNK, dtype=jnp.float32)
    rel = idx[:, None] - idx[None, :]
    decay = jnp.where(rel[None] >= 0,
                      jnp.exp(log_gamma[:, None, None] * jnp.maximum(rel, 0.0)[None]), 0.0)
    scores = jnp.einsum('bnihd,bnjhd->bnhij', qc, kc) * decay
    inner = jnp.einsum('bnhij,bnjhe->bnihe', scores, vc)
    k_decay = jnp.exp(log_gamma[None, :] * (RET_CHUNK - 1.0 - idx)[:, None])
    chunk_kv = jnp.einsum('bnjhd,jh,bnjhe->nbhde', kc, k_decay, vc)
    chunk_decay = jnp.exp(log_gamma * RET_CHUNK)[None, :, None, None]

    def step(state, kv):
        return state * chunk_decay + kv, state

    _, states = lax.scan(step, jnp.zeros((B, H, dk, dv), jnp.float32), chunk_kv)
    q_decay = jnp.exp(log_gamma[None, :] * (idx + 1.0)[:, None])
    cross = jnp.einsum('bnihd,ih,nbhde->bnihe', qc, q_decay, states)
    return (inner + cross).reshape(B, S, H, dv)


def memory_cross_attention(x, mem, w_q, w_kv, w_o):
    B, S, _ = x.shape
    q = (x @ w_q).reshape(B, S, MEM_HEADS, MEM_HEAD_DIM)
    kv = (mem @ w_kv).reshape(B, MEM_LEN, 2, MEM_HEADS, MEM_HEAD_DIM)
    k, v = kv[:, :, 0], kv[:, :, 1]
    s = jnp.einsum('bshd,bmhd->bhsm', q.astype(jnp.float32), k.astype(jnp.float32)) * (MEM_HEAD_DIM ** -0.5)
    p = jax.nn.softmax(s, axis=-1)
    o = jnp.einsum('bhsm,bmhd->bshd', p, v.astype(jnp.float32)).reshape(B, S, D_MODEL)
    return o.astype(x.dtype) @ w_o


def _fwd_setup_inputs(seed: int = 0) -> dict:
    key = jax.random.key(seed)
    ks = jax.random.split(key, 24)
    nrm = lambda k, shape: jax.random.normal(k, shape, jnp.float32)

    def dense(k, fan_in, fan_out, scale=1.0):
        return nrm(k, (DEPTH, fan_in, fan_out)) * (fan_in ** -0.5) * scale

    gain = lambda k: 1.0 + 0.02 * nrm(k, (DEPTH, D_MODEL))
    bias = lambda k, n: 0.02 * nrm(k, (DEPTH, n))

    col_scale = jnp.concatenate([
        jnp.ones((OFF_SB_V,), jnp.float32),
        jnp.full((SB_WIDTH,), DN_BETA, jnp.float32),
        jnp.ones((2 * RET_QK_WIDTH,), jnp.float32),
        jnp.full((RET_V_WIDTH,), DN_BETA, jnp.float32),
        jnp.ones((RET_V_WIDTH + N_BRANCHES * D_MODEL,), jnp.float32)])
    mem_kv_scale = jnp.concatenate([jnp.ones((D_MODEL,), jnp.float32),
                                    jnp.full((D_MODEL,), DN_BETA, jnp.float32)])
    return {
        "x": nrm(ks[0], (BATCH, SEQ, D_MODEL)),
        "mem": nrm(ks[1], (BATCH, MEM_LEN, D_MODEL)),
        "w_in": dense(ks[2], D_MODEL, IN_WIDTH) * col_scale,
        "b_gate": bias(ks[3], N_BRANCHES * D_MODEL),
        "w_sb_o": dense(ks[4], SB_WIDTH, D_MODEL),
        "w_ret_o": dense(ks[5], RET_V_WIDTH, D_MODEL),
        "w_mix_o": dense(ks[6], D_MODEL, D_MODEL, DN_BETA),
        "ln1_g": gain(ks[7]),
        "ln1_b": bias(ks[8], D_MODEL),
        "w_mem_q": dense(ks[9], D_MODEL, D_MODEL),
        "w_mem_kv": dense(ks[10], D_MODEL, 2 * D_MODEL) * mem_kv_scale,
        "w_mem_o": dense(ks[11], D_MODEL, D_MODEL, DN_BETA),
        "ln2_g": gain(ks[12]),
        "ln2_b": bias(ks[13], D_MODEL),
        "w_ffn_in": dense(ks[14], D_MODEL, 2 * FFN_HIDDEN, DN_BETA),
        "w_ffn_out": dense(ks[15], FFN_HIDDEN, D_MODEL, DN_BETA),
        "ln3_g": gain(ks[16]),
        "ln3_b": bias(ks[17], D_MODEL),
    }


def _fwd_reference(x, mem, w_in, b_gate, w_sb_o, w_ret_o, w_mix_o, ln1_g, ln1_b,
              w_mem_q, w_mem_kv, w_mem_o, ln2_g, ln2_b, w_ffn_in, w_ffn_out, ln3_g, ln3_b):
    B, S, _ = x.shape
    for l in range(DEPTH):
        h = x @ w_in[l]
        sb_q = h[..., OFF_SB_Q:OFF_SB_K].reshape(B, S, SB_HEADS, SB_HEAD_DIM)
        sb_k = h[..., OFF_SB_K:OFF_SB_V].reshape(B, S, SB_HEADS, SB_HEAD_DIM)
        sb_v = h[..., OFF_SB_V:OFF_RET_Q].reshape(B, S, SB_HEADS, SB_HEAD_DIM)
        r_q = rotary(h[..., OFF_RET_Q:OFF_RET_K].reshape(B, S, RET_HEADS, RET_QK_DIM))
        r_k = rotary(h[..., OFF_RET_K:OFF_RET_V].reshape(B, S, RET_HEADS, RET_QK_DIM))
        r_v = h[..., OFF_RET_V:OFF_RET_G].reshape(B, S, RET_HEADS, RET_V_DIM)
        r_g = h[..., OFF_RET_G:OFF_GATE]
        gates = jax.nn.sigmoid(h[..., OFF_GATE:] + b_gate[l]).reshape(B, S, N_BRANCHES, D_MODEL)

        y_sb = stick_breaking_attention(sb_q, sb_k, sb_v).astype(x.dtype) @ w_sb_o[l]
        ret = head_norm(retention(r_q, r_k, r_v)).reshape(B, S, RET_V_WIDTH)
        y_ret = (jax.nn.silu(r_g) * ret.astype(x.dtype)) @ w_ret_o[l]
        mix = (gates[:, :, 0] * y_sb + gates[:, :, 1] * y_ret) @ w_mix_o[l]
        x = layer_norm(DN_ALPHA * x + mix, ln1_g[l], ln1_b[l])

        xa = memory_cross_attention(x, mem, w_mem_q[l], w_mem_kv[l], w_mem_o[l])
        x = layer_norm(DN_ALPHA * x + xa, ln2_g[l], ln2_b[l])

        f = x @ w_ffn_in[l]
        ff = (jax.nn.silu(f[..., :FFN_HIDDEN]) * f[..., FFN_HIDDEN:]) @ w_ffn_out[l]
        x = layer_norm(DN_ALPHA * x + ff, ln3_g[l], ln3_b[l])
    return x


import jax as _jax
import jax.numpy as _jnp

TWIN_FORMAT = 'train_step'
FWD_PARAMS = ['x', 'mem', 'w_in', 'b_gate', 'w_sb_o', 'w_ret_o', 'w_mix_o', 'ln1_g', 'ln1_b', 'w_mem_q', 'w_mem_kv', 'w_mem_o', 'ln2_g', 'ln2_b', 'w_ffn_in', 'w_ffn_out', 'ln3_g', 'ln3_b']
TWIN_WEIGHTS = ['w_in', 'b_gate', 'w_sb_o', 'w_ret_o', 'w_mix_o', 'ln1_g', 'ln1_b', 'w_mem_q', 'w_mem_kv', 'w_mem_o', 'ln2_g', 'ln2_b', 'w_ffn_in', 'w_ffn_out', 'ln3_g', 'ln3_b']
TWIN_DIFF_INPUT = 'x'
TWIN_INPUTS = ['x', 'mem', 'w_in', 'b_gate', 'w_sb_o', 'w_ret_o', 'w_mix_o', 'ln1_g', 'ln1_b', 'w_mem_q', 'w_mem_kv', 'w_mem_o', 'ln2_g', 'ln2_b', 'w_ffn_in', 'w_ffn_out', 'ln3_g', 'ln3_b', 'loss_target', 'm_w_in', 'm_b_gate', 'm_w_sb_o', 'm_w_ret_o', 'm_w_mix_o', 'm_ln1_g', 'm_ln1_b', 'm_w_mem_q', 'm_w_mem_kv', 'm_w_mem_o', 'm_ln2_g', 'm_ln2_b', 'm_w_ffn_in', 'm_w_ffn_out', 'm_ln3_g', 'm_ln3_b', 'v_w_in', 'v_b_gate', 'v_w_sb_o', 'v_w_ret_o', 'v_w_mix_o', 'v_ln1_g', 'v_ln1_b', 'v_w_mem_q', 'v_w_mem_kv', 'v_w_mem_o', 'v_ln2_g', 'v_ln2_b', 'v_w_ffn_in', 'v_w_ffn_out', 'v_ln3_g', 'v_ln3_b']
TWIN_OUTPUTS = ['loss', 'grad_x', 'grad_w_in', 'grad_b_gate', 'grad_w_sb_o', 'grad_w_ret_o', 'grad_w_mix_o', 'grad_ln1_g', 'grad_ln1_b', 'grad_w_mem_q', 'grad_w_mem_kv', 'grad_w_mem_o', 'grad_ln2_g', 'grad_ln2_b', 'grad_w_ffn_in', 'grad_w_ffn_out', 'grad_ln3_g', 'grad_ln3_b', 'delta_w_in', 'delta_b_gate', 'delta_w_sb_o', 'delta_w_ret_o', 'delta_w_mix_o', 'delta_ln1_g', 'delta_ln1_b', 'delta_w_mem_q', 'delta_w_mem_kv', 'delta_w_mem_o', 'delta_ln2_g', 'delta_ln2_b', 'delta_w_ffn_in', 'delta_w_ffn_out', 'delta_ln3_g', 'delta_ln3_b', 'new_m_w_in', 'new_m_b_gate', 'new_m_w_sb_o', 'new_m_w_ret_o', 'new_m_w_mix_o', 'new_m_ln1_g', 'new_m_ln1_b', 'new_m_w_mem_q', 'new_m_w_mem_kv', 'new_m_w_mem_o', 'new_m_ln2_g', 'new_m_ln2_b', 'new_m_w_ffn_in', 'new_m_w_ffn_out', 'new_m_ln3_g', 'new_m_ln3_b', 'new_v_w_in', 'new_v_b_gate', 'new_v_w_sb_o', 'new_v_w_ret_o', 'new_v_w_mix_o', 'new_v_ln1_g', 'new_v_ln1_b', 'new_v_w_mem_q', 'new_v_w_mem_kv', 'new_v_w_mem_o', 'new_v_ln2_g', 'new_v_ln2_b', 'new_v_w_ffn_in', 'new_v_w_ffn_out', 'new_v_ln3_g', 'new_v_ln3_b']
TWIN_LEAF_KINDS = {'loss': 'loss', 'grad_x': 'grad_x', 'grad_w_in': 'grad_w', 'grad_b_gate': 'grad_w', 'grad_w_sb_o': 'grad_w', 'grad_w_ret_o': 'grad_w', 'grad_w_mix_o': 'grad_w', 'grad_ln1_g': 'grad_w', 'grad_ln1_b': 'grad_w', 'grad_w_mem_q': 'grad_w', 'grad_w_mem_kv': 'grad_w', 'grad_w_mem_o': 'grad_w', 'grad_ln2_g': 'grad_w', 'grad_ln2_b': 'grad_w', 'grad_w_ffn_in': 'grad_w', 'grad_w_ffn_out': 'grad_w', 'grad_ln3_g': 'grad_w', 'grad_ln3_b': 'grad_w', 'delta_w_in': 'delta_w', 'delta_b_gate': 'delta_w', 'delta_w_sb_o': 'delta_w', 'delta_w_ret_o': 'delta_w', 'delta_w_mix_o': 'delta_w', 'delta_ln1_g': 'delta_w', 'delta_ln1_b': 'delta_w', 'delta_w_mem_q': 'delta_w', 'delta_w_mem_kv': 'delta_w', 'delta_w_mem_o': 'delta_w', 'delta_ln2_g': 'delta_w', 'delta_ln2_b': 'delta_w', 'delta_w_ffn_in': 'delta_w', 'delta_w_ffn_out': 'delta_w', 'delta_ln3_g': 'delta_w', 'delta_ln3_b': 'delta_w', 'new_m_w_in': 'new_m', 'new_m_b_gate': 'new_m', 'new_m_w_sb_o': 'new_m', 'new_m_w_ret_o': 'new_m', 'new_m_w_mix_o': 'new_m', 'new_m_ln1_g': 'new_m', 'new_m_ln1_b': 'new_m', 'new_m_w_mem_q': 'new_m', 'new_m_w_mem_kv': 'new_m', 'new_m_w_mem_o': 'new_m', 'new_m_ln2_g': 'new_m', 'new_m_ln2_b': 'new_m', 'new_m_w_ffn_in': 'new_m', 'new_m_w_ffn_out': 'new_m', 'new_m_ln3_g': 'new_m', 'new_m_ln3_b': 'new_m', 'new_v_w_in': 'new_v', 'new_v_b_gate': 'new_v', 'new_v_w_sb_o': 'new_v', 'new_v_w_ret_o': 'new_v', 'new_v_w_mix_o': 'new_v', 'new_v_ln1_g': 'new_v', 'new_v_ln1_b': 'new_v', 'new_v_w_mem_q': 'new_v', 'new_v_w_mem_kv': 'new_v', 'new_v_w_mem_o': 'new_v', 'new_v_ln2_g': 'new_v', 'new_v_ln2_b': 'new_v', 'new_v_w_ffn_in': 'new_v', 'new_v_w_ffn_out': 'new_v', 'new_v_ln3_g': 'new_v', 'new_v_ln3_b': 'new_v'}


def _forward(args):
    return _fwd_reference(*[args[k] for k in FWD_PARAMS])


def _output_shape():
    def fwd():
        inp = _fwd_setup_inputs(0)
        return _fwd_reference(*[inp[k] for k in FWD_PARAMS])
    out = _jax.eval_shape(fwd)
    return out.shape, out.dtype

N_MICROBATCH = 1
ADAM_LR = 0.001
ADAM_B1 = 0.9
ADAM_B2 = 0.999
ADAM_EPS = 1e-08
ADAM_WD = 0.01
ADAM_STEP = 10
PER_EXAMPLE_BATCH_AXIS = {'x': 0, 'mem': 0, 'loss_target': 0}
SHARED_INPUTS = []
_WEIGHT_DTYPES = {'w_in': _jnp.float32, 'b_gate': _jnp.float32, 'w_sb_o': _jnp.float32, 'w_ret_o': _jnp.float32, 'w_mix_o': _jnp.float32, 'ln1_g': _jnp.float32, 'ln1_b': _jnp.float32, 'w_mem_q': _jnp.float32, 'w_mem_kv': _jnp.float32, 'w_mem_o': _jnp.float32, 'ln2_g': _jnp.float32, 'ln2_b': _jnp.float32, 'w_ffn_in': _jnp.float32, 'w_ffn_out': _jnp.float32, 'ln3_g': _jnp.float32, 'ln3_b': _jnp.float32}
MOMENT_SCALE = {'w_in': 4.342598e-02, 'b_gate': 1.485520e-02, 'w_sb_o': 2.578735e-02, 'w_ret_o': 4.000361e-02, 'w_mix_o': 7.976976e-02, 'ln1_g': 2.256044e+00, 'ln1_b': 1.050846e+00, 'w_mem_q': 7.881317e-03, 'w_mem_kv': 1.216771e-02, 'w_mem_o': 1.514410e-02, 'ln2_g': 2.256756e+00, 'ln2_b': 1.044120e+00, 'w_ffn_in': 2.500902e-02, 'w_ffn_out': 4.092385e-02, 'ln3_g': 6.412045e+01, 'ln3_b': 1.751872e+00}


def _to_microbatches(a, axis):
    t = _jnp.moveaxis(a, axis, 0)
    t = t.reshape((N_MICROBATCH, t.shape[0] // N_MICROBATCH) + t.shape[1:])
    return _jnp.moveaxis(t, 1, axis + 1)


def setup_inputs(seed: int = 0) -> dict:
    inp = _fwd_setup_inputs(seed)
    key = _jax.random.fold_in(_jax.random.key(seed), 7919)
    shape, _ = _output_shape()
    out = dict(inp)
    out["loss_target"] = _jax.random.normal(_jax.random.fold_in(key, 0), shape, _jnp.float32)
    for i, name in enumerate(TWIN_WEIGHTS):
        w = inp[name].astype(_jnp.float32)
        if MOMENT_SCALE is None:
            s = _jnp.sqrt(_jnp.mean(_jnp.square(w)) + 1e-30)
        else:
            s = MOMENT_SCALE[name]
        km, kv = _jax.random.split(_jax.random.fold_in(key, i + 1))
        out[name] = w
        out["m_" + name] = s * _jax.random.normal(km, w.shape, _jnp.float32)
        out["v_" + name] = (s * s) * _jax.random.uniform(kv, w.shape, _jnp.float32, 0.5, 1.5)
    if N_MICROBATCH > 1:
        for name, axis in PER_EXAMPLE_BATCH_AXIS.items():
            out[name] = _to_microbatches(out[name], axis)
    return {'x': out['x'], 'mem': out['mem'], 'w_in': out['w_in'], 'b_gate': out['b_gate'], 'w_sb_o': out['w_sb_o'], 'w_ret_o': out['w_ret_o'], 'w_mix_o': out['w_mix_o'], 'ln1_g': out['ln1_g'], 'ln1_b': out['ln1_b'], 'w_mem_q': out['w_mem_q'], 'w_mem_kv': out['w_mem_kv'], 'w_mem_o': out['w_mem_o'], 'ln2_g': out['ln2_g'], 'ln2_b': out['ln2_b'], 'w_ffn_in': out['w_ffn_in'], 'w_ffn_out': out['w_ffn_out'], 'ln3_g': out['ln3_g'], 'ln3_b': out['ln3_b'], 'loss_target': out['loss_target'], 'm_w_in': out['m_w_in'], 'm_b_gate': out['m_b_gate'], 'm_w_sb_o': out['m_w_sb_o'], 'm_w_ret_o': out['m_w_ret_o'], 'm_w_mix_o': out['m_w_mix_o'], 'm_ln1_g': out['m_ln1_g'], 'm_ln1_b': out['m_ln1_b'], 'm_w_mem_q': out['m_w_mem_q'], 'm_w_mem_kv': out['m_w_mem_kv'], 'm_w_mem_o': out['m_w_mem_o'], 'm_ln2_g': out['m_ln2_g'], 'm_ln2_b': out['m_ln2_b'], 'm_w_ffn_in': out['m_w_ffn_in'], 'm_w_ffn_out': out['m_w_ffn_out'], 'm_ln3_g': out['m_ln3_g'], 'm_ln3_b': out['m_ln3_b'], 'v_w_in': out['v_w_in'], 'v_b_gate': out['v_b_gate'], 'v_w_sb_o': out['v_w_sb_o'], 'v_w_ret_o': out['v_w_ret_o'], 'v_w_mix_o': out['v_w_mix_o'], 'v_ln1_g': out['v_ln1_g'], 'v_ln1_b': out['v_ln1_b'], 'v_w_mem_q': out['v_w_mem_q'], 'v_w_mem_kv': out['v_w_mem_kv'], 'v_w_mem_o': out['v_w_mem_o'], 'v_ln2_g': out['v_ln2_g'], 'v_ln2_b': out['v_ln2_b'], 'v_w_ffn_in': out['v_w_ffn_in'], 'v_w_ffn_out': out['v_w_ffn_out'], 'v_ln3_g': out['v_ln3_g'], 'v_ln3_b': out['v_ln3_b']}


def _loss(weights, diff, rest, loss_target):
    with _jax.named_scope("forward"):
        args = {**rest, TWIN_DIFF_INPUT: diff, **{k: w.astype(_WEIGHT_DTYPES[k]) for k, w in weights.items()}}
        y = _forward(args)
    with _jax.named_scope("loss_head"):
        err = _jnp.square(y.astype(_jnp.float32) - loss_target)
        return 0.5 * _jnp.sum(_jnp.mean(err, axis=-1)) if err.ndim else 0.5 * err


def _adamw(w, g, m, v):
    m = ADAM_B1 * m + (1.0 - ADAM_B1) * g
    v = ADAM_B2 * v + (1.0 - ADAM_B2) * _jnp.square(g)
    m_hat = m / (1.0 - ADAM_B1 ** ADAM_STEP)
    v_hat = v / (1.0 - ADAM_B2 ** ADAM_STEP)
    delta = -ADAM_LR * (m_hat / (_jnp.sqrt(v_hat) + ADAM_EPS) + ADAM_WD * w)
    return delta, m, v


def reference(x, mem, w_in, b_gate, w_sb_o, w_ret_o, w_mix_o, ln1_g, ln1_b, w_mem_q, w_mem_kv, w_mem_o, ln2_g, ln2_b, w_ffn_in, w_ffn_out, ln3_g, ln3_b, loss_target, m_w_in, m_b_gate, m_w_sb_o, m_w_ret_o, m_w_mix_o, m_ln1_g, m_ln1_b, m_w_mem_q, m_w_mem_kv, m_w_mem_o, m_ln2_g, m_ln2_b, m_w_ffn_in, m_w_ffn_out, m_ln3_g, m_ln3_b, v_w_in, v_b_gate, v_w_sb_o, v_w_ret_o, v_w_mix_o, v_ln1_g, v_ln1_b, v_w_mem_q, v_w_mem_kv, v_w_mem_o, v_ln2_g, v_ln2_b, v_w_ffn_in, v_w_ffn_out, v_ln3_g, v_ln3_b):
    given = dict(x=x, mem=mem, w_in=w_in, b_gate=b_gate, w_sb_o=w_sb_o, w_ret_o=w_ret_o, w_mix_o=w_mix_o, ln1_g=ln1_g, ln1_b=ln1_b, w_mem_q=w_mem_q, w_mem_kv=w_mem_kv, w_mem_o=w_mem_o, ln2_g=ln2_g, ln2_b=ln2_b, w_ffn_in=w_ffn_in, w_ffn_out=w_ffn_out, ln3_g=ln3_g, ln3_b=ln3_b, loss_target=loss_target, m_w_in=m_w_in, m_b_gate=m_b_gate, m_w_sb_o=m_w_sb_o, m_w_ret_o=m_w_ret_o, m_w_mix_o=m_w_mix_o, m_ln1_g=m_ln1_g, m_ln1_b=m_ln1_b, m_w_mem_q=m_w_mem_q, m_w_mem_kv=m_w_mem_kv, m_w_mem_o=m_w_mem_o, m_ln2_g=m_ln2_g, m_ln2_b=m_ln2_b, m_w_ffn_in=m_w_ffn_in, m_w_ffn_out=m_w_ffn_out, m_ln3_g=m_ln3_g, m_ln3_b=m_ln3_b, v_w_in=v_w_in, v_b_gate=v_b_gate, v_w_sb_o=v_w_sb_o, v_w_ret_o=v_w_ret_o, v_w_mix_o=v_w_mix_o, v_ln1_g=v_ln1_g, v_ln1_b=v_ln1_b, v_w_mem_q=v_w_mem_q, v_w_mem_kv=v_w_mem_kv, v_w_mem_o=v_w_mem_o, v_ln2_g=v_ln2_g, v_ln2_b=v_ln2_b, v_w_ffn_in=v_w_ffn_in, v_w_ffn_out=v_w_ffn_out, v_ln3_g=v_ln3_g, v_ln3_b=v_ln3_b)
    weights = {n: given[n] for n in TWIN_WEIGHTS}
    shared = {n: given[n] for n in SHARED_INPUTS}
    per_example = {n: given[n] for n in ['x', 'mem']}
    grad_fn = _jax.value_and_grad(_loss, argnums=(0, 1))

    def one_microbatch(ex, loss_target):
        ex = dict(ex)
        diff = ex.pop(TWIN_DIFF_INPUT)
        return grad_fn(weights, diff, {**shared, **ex}, loss_target)

    if N_MICROBATCH == 1:
        loss, (grad_w, grad_x) = one_microbatch(per_example, given["loss_target"])
    else:
        def body(carry, xs):
            loss_sum, grad_sum = carry
            l_k, (gw_k, gx_k) = one_microbatch(xs[0], xs[1])
            with _jax.named_scope("update"):
                return (loss_sum + l_k, _jax.tree.map(_jnp.add, grad_sum, gw_k)), gx_k

        init = (_jnp.zeros((), _jnp.float32), _jax.tree.map(_jnp.zeros_like, weights))
        (loss, grad_w), grad_x = _jax.lax.scan(body, init, (per_example, given["loss_target"]))
    with _jax.named_scope("update"):
        delta_w, new_m, new_v = {}, {}, {}
        for n in TWIN_WEIGHTS:
            delta_w[n], new_m[n], new_v[n] = _adamw(weights[n], grad_w[n], given["m_" + n], given["v_" + n])
    return (loss, grad_x, *[grad_w[n] for n in TWIN_WEIGHTS], *[delta_w[n] for n in TWIN_WEIGHTS],
            *[new_m[n] for n in TWIN_WEIGHTS], *[new_v[n] for n in TWIN_WEIGHTS])
```

```python
import functools
import math

import jax
import jax.numpy as jnp
from jax import lax
from jax.experimental import pallas as pl
from jax.experimental.pallas import tpu as pltpu

F32 = jnp.float32
BF16 = jnp.bfloat16

N_DEV = 8
D_MODEL = 1024
SB_HEAD_DIM = 64
SB_WIDTH = 512
RET_HEADS = 4
RET_QK_DIM = 128
RET_V_DIM = 256
RET_QK_WIDTH = 512
RET_V_WIDTH = 1024
RET_CHUNK = 128
ROPE_BASE = 10000.0
MEM_HEADS = 4
MEM_HEAD_DIM = 256
FFN_HIDDEN = 2816
DN_ALPHA = 2.0 ** 0.25
LN_EPS = 1e-5
ADAM_LR = 0.001
ADAM_B1 = 0.9
ADAM_B2 = 0.999
ADAM_EPS = 1e-08
ADAM_WD = 0.01
ADAM_STEP = 10

VMEM_LIMIT_BYTES = 52 * 1024 * 1024
LANES = 128
SB_KEY_BLOCK = 128
SB_Q_BLOCK = 256

MESH_AXES = ("x", "y", "c")


def _pick(dim, prefs):
    for p in prefs:
        if dim % p == 0:
            return p
    return dim


def _params(sem):
    return pltpu.CompilerParams(dimension_semantics=sem, vmem_limit_bytes=VMEM_LIMIT_BYTES)


def _dot(a, b, dims):
    return lax.dot_general(a, b, (dims, ((), ())), preferred_element_type=F32)


_NN = ((1,), (0,))
_NT = ((1,), (1,))
_TN = ((0,), (0,))


def _mm(a, b, *, mode, out_dtype, name, res=None, res_scale=1.0):
    if mode == "nn":
        (M, K), (K2, N) = a.shape, b.shape
    elif mode == "nt":
        (M, K), (N, K2) = a.shape, b.shape
    else:
        (K, M), (K2, N) = a.shape, b.shape
    assert K == K2, (a.shape, b.shape, mode)
    tm = _pick(M, (1024, 512, 256, 128))
    tn = _pick(N, (512, 256, 128))
    tk = _pick(K, (1024, 512, 256, 128))
    nk = K // tk
    dims = {"nn": _NN, "nt": _NT, "tn": _TN}[mode]

    def body(*refs):
        if res is None:
            a_ref, b_ref, o_ref = refs[:3]
            r_ref = None
        else:
            a_ref, b_ref, r_ref, o_ref = refs[:4]
        acc_ref = refs[-1] if nk > 1 else None
        part = _dot(a_ref[...].astype(BF16), b_ref[...].astype(BF16), dims)

        def finish(total):
            if r_ref is not None:
                total = total + res_scale * r_ref[...]
            o_ref[...] = total.astype(out_dtype)

        if nk == 1:
            finish(part)
        else:
            k = pl.program_id(2)

            @pl.when(k == 0)
            def _():
                acc_ref[...] = part

            @pl.when(k > 0)
            def _():
                acc_ref[...] += part

            @pl.when(k == nk - 1)
            def _():
                finish(acc_ref[...])

    if mode == "nn":
        a_spec = pl.BlockSpec((tm, tk), lambda i, j, k: (i, k))
        b_spec = pl.BlockSpec((tk, tn), lambda i, j, k: (k, j))
    elif mode == "nt":
        a_spec = pl.BlockSpec((tm, tk), lambda i, j, k: (i, k))
        b_spec = pl.BlockSpec((tn, tk), lambda i, j, k: (j, k))
    else:
        a_spec = pl.BlockSpec((tk, tm), lambda i, j, k: (k, i))
        b_spec = pl.BlockSpec((tk, tn), lambda i, j, k: (k, j))
    o_spec = pl.BlockSpec((tm, tn), lambda i, j, k: (i, j))
    in_specs = [a_spec, b_spec] + ([o_spec] if res is not None else [])
    args = (a, b) + ((res,) if res is not None else ())
    return pl.pallas_call(
        body,
        name=name,
        grid=(M // tm, N // tn, nk),
        in_specs=in_specs,
        out_specs=o_spec,
        out_shape=jax.ShapeDtypeStruct((M, N), out_dtype),
        scratch_shapes=[pltpu.VMEM((tm, tn), F32)] if nk > 1 else [],
        compiler_params=_params(("parallel", "parallel", "arbitrary")),
    )(*args)


def _pair_rows(blk, lane_is_a):
    zero = jnp.zeros_like(blk)
    return jnp.concatenate([jnp.where(lane_is_a, blk, zero), jnp.where(lane_is_a, zero, blk)], axis=0)


def _pair_scan_matrix(strict_after):
    r = lax.broadcasted_iota(jnp.int32, (4 * LANES, 2 * LANES), 0) & (2 * LANES - 1)
    c = lax.broadcasted_iota(jnp.int32, (4 * LANES, 2 * LANES), 1)
    same = (r >= LANES) == (c >= LANES)
    rr, cc = r & (LANES - 1), c & (LANES - 1)
    tri = (rr > cc) if strict_after else (rr < cc)
    return jnp.where(same & tri, 1.0, 0.0).astype(BF16)


def _split_dot(val, mat):
    hi = val.astype(BF16)
    lo = (val - hi.astype(F32)).astype(BF16)
    return _dot(jnp.concatenate([hi, lo], axis=1), mat, _NN)


def _pair_cols(col_a, col_b, rows):
    return jnp.concatenate([jnp.broadcast_to(col_a, (rows, LANES)), jnp.broadcast_to(col_b, (rows, LANES))], axis=1)


def _sb_scores(q, kk, mask):
    z = _dot(q, kk, _NT) * (SB_HEAD_DIM ** -0.5)
    t = jnp.log1p(jnp.exp(-jnp.abs(z)))
    log_beta = jnp.minimum(z, 0.0) - t
    log_rem = -jnp.maximum(z, 0.0) - t
    if mask is not None:
        log_rem = jnp.where(mask, log_rem, 0.0)
    return log_beta, log_rem


def _sb_fwd(h_a):
    T = h_a.shape[0]
    tq = _pick(T, (SB_Q_BLOCK, SB_KEY_BLOCK))
    nq, per_q = T // tq, tq // SB_KEY_BLOCK
    assert T // SB_KEY_BLOCK <= LANES

    def body(q_ref, k_ref, v_ref, a_ref, r_ref):
        i = pl.program_id(1)
        q = q_ref[...]
        lane_is_a = lax.broadcasted_iota(jnp.int32, (SB_KEY_BLOCK, LANES), 1) < SB_HEAD_DIM
        after = _pair_scan_matrix(True)
        row = lax.broadcasted_iota(jnp.int32, (tq, 2 * LANES), 0)
        col = lax.broadcasted_iota(jnp.int32, (tq, 2 * LANES), 1) & (LANES - 1)
        blk_lane = lax.broadcasted_iota(jnp.int32, (tq, LANES), 1)

        def tile(kb, masked, carry):
            acc, ra, rb, rma, rmb = carry
            ks = pl.multiple_of(kb * SB_KEY_BLOCK, SB_KEY_BLOCK)
            kk = _pair_rows(k_ref[pl.ds(ks, SB_KEY_BLOCK), :], lane_is_a)
            vv = _pair_rows(v_ref[pl.ds(ks, SB_KEY_BLOCK), :], lane_is_a)
            mask = ((ks + col) < (i * tq + row)) if masked else None
            log_beta, log_rem = _sb_scores(q, kk, mask)
            later = _split_dot(log_rem, after) + _pair_cols(ra, rb, tq)
            w = jnp.exp(log_beta + later)
            if masked:
                w = jnp.where(mask, w, 0.0)
            acc = acc + _dot(w.astype(BF16), vv, _NN)
            here = (blk_lane == kb).astype(F32)
            rma = rma + ra * here
            rmb = rmb + rb * here
            ra = ra + jnp.sum(log_rem[:, :LANES], axis=1, keepdims=True)
            rb = rb + jnp.sum(log_rem[:, LANES:], axis=1, keepdims=True)
            return acc, ra, rb, rma, rmb

        carry = (jnp.zeros((tq, LANES), F32), jnp.zeros((tq, 1), F32), jnp.zeros((tq, 1), F32),
                 jnp.zeros((tq, LANES), F32), jnp.zeros((tq, LANES), F32))
        for d in range(per_q):
            carry = tile(i * per_q + (per_q - 1 - d), True, carry)
        carry = lax.fori_loop(0, i * per_q, lambda j, c: tile(i * per_q - 1 - j, False, c), carry)
        acc, _, _, rma, rmb = carry
        a_ref[...] = acc.astype(BF16)
        r_ref[0] = rma
        r_ref[1] = rmb

    return pl.pallas_call(
        body,
        name="sb_fwd",
        grid=(4, nq),
        in_specs=[pl.BlockSpec((tq, LANES), lambda p, i: (i, p)),
                  pl.BlockSpec((T, LANES), lambda p, i: (0, 4 + p)),
                  pl.BlockSpec((T, LANES), lambda p, i: (0, 8 + p))],
        out_specs=[pl.BlockSpec((tq, LANES), lambda p, i: (i, p)),
                   pl.BlockSpec((2, tq, LANES), lambda p, i: (p, i, 0))],
        out_shape=[jax.ShapeDtypeStruct((T, SB_WIDTH), BF16),
                   jax.ShapeDtypeStruct((8, T, LANES), F32)],
        compiler_params=_params(("parallel", "arbitrary")),
    )(h_a, h_a, h_a)


def _sb_bwd(h_a, d_out, r_mat):
    T = h_a.shape[0]
    tq = _pick(T, (SB_Q_BLOCK, SB_KEY_BLOCK))
    nq, per_q = T // tq, tq // SB_KEY_BLOCK

    def body(q_ref, k_ref, v_ref, do_ref, r_ref, dq_ref, dk_ref, dv_ref):
        i = pl.program_id(1)

        @pl.when(i == 0)
        def _():
            dk_ref[...] = jnp.zeros_like(dk_ref)
            dv_ref[...] = jnp.zeros_like(dv_ref)

        q = q_ref[...]
        d_o = do_ref[...]
        rma, rmb = r_ref[0], r_ref[1]
        lane_is_a = lax.broadcasted_iota(jnp.int32, (SB_KEY_BLOCK, LANES), 1) < SB_HEAD_DIM
        after = _pair_scan_matrix(True)
        before = _pair_scan_matrix(False)
        row = lax.broadcasted_iota(jnp.int32, (tq, 2 * LANES), 0)
        col = lax.broadcasted_iota(jnp.int32, (tq, 2 * LANES), 1) & (LANES - 1)
        blk_lane = lax.broadcasted_iota(jnp.int32, (tq, LANES), 1)

        def tile(kb, masked, carry):
            dq, ca, cb = carry
            ks = pl.multiple_of(kb * SB_KEY_BLOCK, SB_KEY_BLOCK)
            kk = _pair_rows(k_ref[pl.ds(ks, SB_KEY_BLOCK), :], lane_is_a)
            vv = _pair_rows(v_ref[pl.ds(ks, SB_KEY_BLOCK), :], lane_is_a)
            mask = ((ks + col) < (i * tq + row)) if masked else None
            log_beta, log_rem = _sb_scores(q, kk, mask)
            here = (blk_lane == kb).astype(F32)
            ra = jnp.sum(rma * here, axis=1, keepdims=True)
            rb = jnp.sum(rmb * here, axis=1, keepdims=True)
            later = _split_dot(log_rem, after) + _pair_cols(ra, rb, tq)
            w = jnp.exp(log_beta + later)
            if masked:
                w = jnp.where(mask, w, 0.0)
            da = _dot(d_o, vv, _NT) * w
            prefix = _split_dot(da, before) + _pair_cols(ca, cb, tq)
            sig = jnp.exp(log_beta)
            dz = da * (1.0 - sig) - prefix * sig
            if masked:
                dz = jnp.where(mask, dz, 0.0)
            dzb = (dz * (SB_HEAD_DIM ** -0.5)).astype(BF16)
            dq = dq + _dot(dzb, kk, _NN)
            dkk = _dot(dzb, q, _TN)
            dvv = _dot(w.astype(BF16), d_o, _TN)
            dk_ref[pl.ds(ks, SB_KEY_BLOCK), :] += jnp.where(lane_is_a, dkk[:SB_KEY_BLOCK], dkk[SB_KEY_BLOCK:])
            dv_ref[pl.ds(ks, SB_KEY_BLOCK), :] += jnp.where(lane_is_a, dvv[:SB_KEY_BLOCK], dvv[SB_KEY_BLOCK:])
            ca = ca + jnp.sum(da[:, :LANES], axis=1, keepdims=True)
            cb = cb + jnp.sum(da[:, LANES:], axis=1, keepdims=True)
            return dq, ca, cb

        carry = (jnp.zeros((tq, LANES), F32), jnp.zeros((tq, 1), F32), jnp.zeros((tq, 1), F32))
        carry = lax.fori_loop(0, i * per_q, lambda j, c: tile(j, False, c), carry)
        for d in range(per_q):
            carry = tile(i * per_q + d, True, carry)
        dq_ref[...] = carry[0].astype(BF16)

    return pl.pallas_call(
        body,
        name="sb_bwd",
        grid=(4, nq),
        in_specs=[pl.BlockSpec((tq, LANES), lambda p, i: (i, p)),
                  pl.BlockSpec((T, LANES), lambda p, i: (0, 4 + p)),
                  pl.BlockSpec((T, LANES), lambda p, i: (0, 8 + p)),
                  pl.BlockSpec((tq, LANES), lambda p, i: (i, p)),
                  pl.BlockSpec((2, tq, LANES), lambda p, i: (p, i, 0))],
        out_specs=[pl.BlockSpec((tq, LANES), lambda p, i: (i, p)),
                   pl.BlockSpec((T, LANES), lambda p, i: (0, p)),
                   pl.BlockSpec((T, LANES), lambda p, i: (0, p))],
        out_shape=[jax.ShapeDtypeStruct((T, SB_WIDTH), BF16),
                   jax.ShapeDtypeStruct((T, SB_WIDTH), F32),
                   jax.ShapeDtypeStruct((T, SB_WIDTH), F32)],
        compiler_params=_params(("parallel", "arbitrary")),
    )(h_a, h_a, h_a, d_out, r_mat)


def _ret_tables(T):
    half = RET_QK_DIM // 2
    inv = 1.0 / (ROPE_BASE ** (jnp.arange(half, dtype=F32) / half))
    ang = jnp.arange(T, dtype=F32)[:, None] * inv[None, :]
    cos, sin = jnp.cos(ang), jnp.sin(ang)
    cos_t = jnp.concatenate([cos, cos], axis=1)
    sin_t = jnp.concatenate([-sin, sin], axis=1)
    log_gamma = jnp.log1p(-jnp.exp2(-5.0 - jnp.arange(RET_HEADS, dtype=F32)))
    idx = jnp.arange(RET_CHUNK, dtype=F32)
    rel = idx[:, None] - idx[None, :]
    decay = jnp.where(rel[None] >= 0, jnp.exp(log_gamma[:, None, None] * jnp.maximum(rel, 0.0)[None]), 0.0)
    k_decay = jnp.exp(log_gamma[None, :] * (RET_CHUNK - 1.0 - idx)[:, None])
    q_decay = jnp.exp(log_gamma[None, :] * (idx + 1.0)[:, None])
    chunk_decay = jnp.exp(log_gamma * RET_CHUNK)
    k_dec = jnp.broadcast_to(k_decay.T[:, :, None], (RET_HEADS, RET_CHUNK, LANES))
    q_dec = jnp.broadcast_to(q_decay.T[:, :, None], (RET_HEADS, RET_CHUNK, LANES))
    c_dec = jnp.broadcast_to(chunk_decay[:, None, None], (RET_HEADS, 8, LANES))
    return cos_t, sin_t, decay, k_dec, q_dec, c_dec


def _rotary(x, cos_t, sin_t):
    return x * cos_t + pltpu.roll(x, RET_QK_DIM // 2, 1) * sin_t


def _rotary_transpose(dy, cos_t, sin_t):
    return dy * cos_t + pltpu.roll(dy * sin_t, RET_QK_DIM // 2, 1)


def _head_norm(o):
    mu = jnp.mean(o, axis=1, keepdims=True)
    cen = o - mu
    var = jnp.mean(cen * cen, axis=1, keepdims=True)
    rstd = lax.rsqrt(var + LN_EPS)
    return cen * rstd, rstd


def _ret_specs(nc, reverse):
    def n_of(n):
        return (nc - 1 - n) if reverse else n

    qk = pl.BlockSpec((RET_CHUNK, RET_QK_DIM), lambda h, n: (n_of(n), h))
    vv = pl.BlockSpec((RET_CHUNK, RET_V_DIM), lambda h, n: (n_of(n), h))
    pos = pl.BlockSpec((RET_CHUNK, LANES), lambda h, n: (n_of(n), 0))
    per_head = pl.BlockSpec((1, RET_CHUNK, LANES), lambda h, n: (h, 0, 0))
    c_dec = pl.BlockSpec((1, 8, LANES), lambda h, n: (h, 0, 0))
    state = pl.BlockSpec((1, 1, RET_QK_DIM, RET_V_DIM), lambda h, n: (h, n_of(n), 0, 0))
    return qk, vv, pos, per_head, c_dec, state


def _ret_fwd(h_b, h_c, h_d, tables):
    T = h_b.shape[0]
    nc = T // RET_CHUNK
    qk, vv, pos, per_head, c_dec, state = _ret_specs(nc, False)
    k_spec = pl.BlockSpec((RET_CHUNK, RET_QK_DIM), lambda h, n: (n, RET_HEADS + h))

    def body(q_ref, k_ref, v_ref, g_ref, cos_ref, sin_ref, dec_ref, kd_ref, qd_ref, cd_ref,
             y_ref, o_ref, st_ref, state_ref):
        @pl.when(pl.program_id(1) == 0)
        def _():
            state_ref[...] = jnp.zeros_like(state_ref)

        cos_t, sin_t = cos_ref[...], sin_ref[...]
        q = _rotary(q_ref[...], cos_t, sin_t) * (RET_QK_DIM ** -0.5)
        k = _rotary(k_ref[...], cos_t, sin_t)
        v = v_ref[...]
        state = state_ref[...]
        scores = _dot(q.astype(BF16), k.astype(BF16), _NT) * dec_ref[0]
        inner = _dot(scores.astype(BF16), v, _NN)
        cross = _dot((q * qd_ref[0]).astype(BF16), state.astype(BF16), _NN)
        o = inner + cross
        st_ref[0, 0] = state
        kv = _dot((k * kd_ref[0]).astype(BF16), v, _TN)
        state_ref[...] = state * cd_ref[0, 0:1, 0:1] + kv
        o_ref[...] = o
        normed, _ = _head_norm(o)
        gate = g_ref[...]
        y_ref[...] = (gate * jax.nn.sigmoid(gate) * normed).astype(BF16)

    return pl.pallas_call(
        body,
        name="ret_fwd",
        grid=(RET_HEADS, nc),
        in_specs=[qk, k_spec, vv, vv, pos, pos, per_head, per_head, per_head, c_dec],
        out_specs=[vv, vv, state],
        out_shape=[jax.ShapeDtypeStruct((T, RET_V_WIDTH), BF16),
                   jax.ShapeDtypeStruct((T, RET_V_WIDTH), F32),
                   jax.ShapeDtypeStruct((RET_HEADS, nc, RET_QK_DIM, RET_V_DIM), F32)],
        scratch_shapes=[pltpu.VMEM((RET_QK_DIM, RET_V_DIM), F32)],
        compiler_params=_params(("parallel", "arbitrary")),
    )(h_b, h_b, h_c, h_d, *tables)


def _ret_bwd(d_y, o_pre, states, h_b, h_c, h_d, tables):
    T = h_b.shape[0]
    nc = T // RET_CHUNK
    qk, vv, pos, per_head, c_dec, state = _ret_specs(nc, True)
    k_spec = pl.BlockSpec((RET_CHUNK, RET_QK_DIM), lambda h, n: (nc - 1 - n, RET_HEADS + h))

    def body(dy_ref, o_ref, st_ref, q_ref, k_ref, v_ref, g_ref, cos_ref, sin_ref, dec_ref, kd_ref, qd_ref, cd_ref,
             dq_ref, dk_ref, dv_ref, dg_ref, carry_ref):
        @pl.when(pl.program_id(1) == 0)
        def _():
            carry_ref[...] = jnp.zeros_like(carry_ref)

        cos_t, sin_t = cos_ref[...], sin_ref[...]
        scale = RET_QK_DIM ** -0.5
        q = _rotary(q_ref[...], cos_t, sin_t) * scale
        k = _rotary(k_ref[...], cos_t, sin_t)
        v = v_ref[...]
        decay, k_dec, q_dec = dec_ref[0], kd_ref[0], qd_ref[0]
        chunk_decay = cd_ref[0, 0:1, 0:1]
        state = st_ref[0, 0].astype(BF16)
        later = carry_ref[...]
        later_b = later.astype(BF16)

        gate = g_ref[...]
        sig = jax.nn.sigmoid(gate)
        silu = gate * sig
        normed, rstd = _head_norm(o_ref[...])
        d_y = dy_ref[...]
        dg_ref[...] = (d_y * normed * (sig * (1.0 + gate * (1.0 - sig)))).astype(BF16)
        d_n = d_y * silu
        d_o = rstd * (d_n - jnp.mean(d_n, axis=1, keepdims=True)
                      - normed * jnp.mean(d_n * normed, axis=1, keepdims=True))
        d_ob = d_o.astype(BF16)

        qb, kb = q.astype(BF16), k.astype(BF16)
        qd_b, kd_b = (q * q_dec).astype(BF16), (k * k_dec).astype(BF16)
        scores = _dot(qb, kb, _NT) * decay
        d_scores = (_dot(d_ob, v, _NT) * decay).astype(BF16)
        dq = _dot(d_scores, kb, _NN) + _dot(d_ob, state, _NT) * q_dec
        dk = _dot(d_scores, qb, _TN) + _dot(v, later_b, _NT) * k_dec
        dv = _dot(scores.astype(BF16), d_ob, _TN) + _dot(kd_b, later_b, _NN)
        carry_ref[...] = _dot(qd_b, d_ob, _TN) + chunk_decay * later
        dq_ref[...] = _rotary_transpose(dq * scale, cos_t, sin_t).astype(BF16)
        dk_ref[...] = _rotary_transpose(dk, cos_t, sin_t).astype(BF16)
        dv_ref[...] = dv.astype(BF16)

    return pl.pallas_call(
        body,
        name="ret_bwd",
        grid=(RET_HEADS, nc),
        in_specs=[vv, vv, state, qk, k_spec, vv, vv, pos, pos, per_head, per_head, per_head, c_dec],
        out_specs=[qk, qk, vv, vv],
        out_shape=[jax.ShapeDtypeStruct((T, RET_QK_WIDTH), BF16),
                   jax.ShapeDtypeStruct((T, RET_QK_WIDTH), BF16),
                   jax.ShapeDtypeStruct((T, RET_V_WIDTH), BF16),
                   jax.ShapeDtypeStruct((T, RET_V_WIDTH), BF16)],
        scratch_shapes=[pltpu.VMEM((RET_QK_DIM, RET_V_DIM), F32)],
        compiler_params=_params(("parallel", "arbitrary")),
    )(d_y, o_pre, states, h_b, h_b, h_c, h_d, *tables)


def _row_tile(T):
    return _pick(T, (256, 128))


def _gate_mix_fwd(h_e, b_gate, y_sb, y_ret):
    T = h_e.shape[0]
    tr = _row_tile(T)

    def body(g0_ref, g1_ref, b0_ref, b1_ref, ys_ref, yr_ref, o_ref):
        g0 = jax.nn.sigmoid(g0_ref[...] + b0_ref[...])
        g1 = jax.nn.sigmoid(g1_ref[...] + b1_ref[...])
        o_ref[...] = (g0 * ys_ref[...] + g1 * yr_ref[...]).astype(BF16)

    row = pl.BlockSpec((tr, D_MODEL), lambda i: (i, 0))
    return pl.pallas_call(
        body,
        name="gate_mix_fwd",
        grid=(T // tr,),
        in_specs=[row, pl.BlockSpec((tr, D_MODEL), lambda i: (i, 1)),
                  pl.BlockSpec((1, D_MODEL), lambda i: (0, 0)), pl.BlockSpec((1, D_MODEL), lambda i: (0, 1)),
                  row, row],
        out_specs=row,
        out_shape=jax.ShapeDtypeStruct((T, D_MODEL), BF16),
        compiler_params=_params(("parallel",)),
    )(h_e, h_e, b_gate, b_gate, y_sb, y_ret)


def _col_sum_update(acc_ref, val, first):
    part = jnp.sum(val.reshape(val.shape[0] // 8, 8, val.shape[1]), axis=0)

    @pl.when(first)
    def _():
        acc_ref[...] = part

    @pl.when(jnp.logical_not(first))
    def _():
        acc_ref[...] += part


def _gate_mix_bwd(d_mix, h_e, b_gate, y_sb, y_ret):
    T = h_e.shape[0]
    tr = _row_tile(T)
    steps = T // tr

    def body(dm_ref, g_ref, b_ref, ys_ref, yr_ref, dys_ref, dyr_ref, de_ref, db_ref, acc_ref):
        i = pl.program_id(0)
        dm = dm_ref[...]
        gates = jax.nn.sigmoid(g_ref[...] + b_ref[...])
        g0, g1 = gates[:, :D_MODEL], gates[:, D_MODEL:]
        dys_ref[...] = (dm * g0).astype(BF16)
        dyr_ref[...] = (dm * g1).astype(BF16)
        de = jnp.concatenate([dm * ys_ref[...] * g0 * (1.0 - g0), dm * yr_ref[...] * g1 * (1.0 - g1)], axis=1)
        de_ref[...] = de.astype(BF16)
        _col_sum_update(acc_ref, de, i == 0)

        @pl.when(i == steps - 1)
        def _():
            db_ref[...] = jnp.sum(acc_ref[...], axis=0, keepdims=True)

    row = pl.BlockSpec((tr, D_MODEL), lambda i: (i, 0))
    wide = pl.BlockSpec((tr, 2 * D_MODEL), lambda i: (i, 0))
    vec = pl.BlockSpec((1, 2 * D_MODEL), lambda i: (0, 0))
    return pl.pallas_call(
        body,
        name="gate_mix_bwd",
        grid=(steps,),
        in_specs=[row, wide, vec, row, row],
        out_specs=[row, row, wide, vec],
        out_shape=[jax.ShapeDtypeStruct((T, D_MODEL), BF16), jax.ShapeDtypeStruct((T, D_MODEL), BF16),
                   jax.ShapeDtypeStruct((T, 2 * D_MODEL), BF16), jax.ShapeDtypeStruct((1, 2 * D_MODEL), F32)],
        scratch_shapes=[pltpu.VMEM((8, 2 * D_MODEL), F32)],
        compiler_params=_params(("arbitrary",)),
    )(d_mix, h_e, b_gate, y_sb, y_ret)


def _ln_stats(u):
    mu = jnp.mean(u, axis=1, keepdims=True)
    cen = u - mu
    var = jnp.mean(cen * cen, axis=1, keepdims=True)
    rstd = lax.rsqrt(var + LN_EPS)
    return cen * rstd, rstd


def _ln_input_grad(d_out, gain, xhat, rstd):
    d_hat = d_out * gain
    return rstd * (d_hat - jnp.mean(d_hat, axis=1, keepdims=True)
                   - xhat * jnp.mean(d_hat * xhat, axis=1, keepdims=True))


def _ln_fwd(x_prev, sub, gain, bias, name):
    T = x_prev.shape[0]
    tr = _row_tile(T)

    def body(x_ref, s_ref, g_ref, b_ref, o_ref, xh_ref, rs_ref):
        xhat, rstd = _ln_stats(DN_ALPHA * x_ref[...] + s_ref[...])
        o_ref[...] = xhat * g_ref[...] + b_ref[...]
        xh_ref[...] = xhat
        rs_ref[...] = rstd

    row = pl.BlockSpec((tr, D_MODEL), lambda i: (i, 0))
    vec = pl.BlockSpec((1, D_MODEL), lambda i: (0, 0))
    return pl.pallas_call(
        body,
        name=name,
        grid=(T // tr,),
        in_specs=[row, row, vec, vec],
        out_specs=[row, row, pl.BlockSpec((tr, 1), lambda i: (i, 0))],
        out_shape=[jax.ShapeDtypeStruct((T, D_MODEL), F32), jax.ShapeDtypeStruct((T, D_MODEL), F32),
                   jax.ShapeDtypeStruct((T, 1), F32)],
        compiler_params=_params(("parallel",)),
    )(x_prev, sub, gain, bias)


def _ln_bwd(d_out, xhat, rstd, gain, name):
    T = d_out.shape[0]
    tr = _row_tile(T)
    steps = T // tr

    def body(d_ref, xh_ref, rs_ref, g_ref, du_ref, dg_ref, db_ref, accg_ref, accb_ref):
        i = pl.program_id(0)
        d_o, xh = d_ref[...], xh_ref[...]
        du_ref[...] = _ln_input_grad(d_o, g_ref[...], xh, rs_ref[...])
        _col_sum_update(accg_ref, d_o * xh, i == 0)
        _col_sum_update(accb_ref, d_o, i == 0)

        @pl.when(i == steps - 1)
        def _():
            dg_ref[...] = jnp.sum(accg_ref[...], axis=0, keepdims=True)
            db_ref[...] = jnp.sum(accb_ref[...], axis=0, keepdims=True)

    row = pl.BlockSpec((tr, D_MODEL), lambda i: (i, 0))
    vec = pl.BlockSpec((1, D_MODEL), lambda i: (0, 0))
    return pl.pallas_call(
        body,
        name=name,
        grid=(steps,),
        in_specs=[row, row, pl.BlockSpec((tr, 1), lambda i: (i, 0)), vec],
        out_specs=[row, vec, vec],
        out_shape=[jax.ShapeDtypeStruct((T, D_MODEL), F32), jax.ShapeDtypeStruct((1, D_MODEL), F32),
                   jax.ShapeDtypeStruct((1, D_MODEL), F32)],
        scratch_shapes=[pltpu.VMEM((8, D_MODEL), F32), pltpu.VMEM((8, D_MODEL), F32)],
        compiler_params=_params(("arbitrary",)),
    )(d_out, xhat, rstd, gain)


def _ln_loss(x_prev, sub, gain, bias, target):
    T = x_prev.shape[0]
    tr = _row_tile(T)
    steps = T // tr

    def body(x_ref, s_ref, g_ref, b_ref, t_ref, loss_ref, du_ref, dg_ref, db_ref, accl_ref, accg_ref, accb_ref):
        i = pl.program_id(0)
        gain_v = g_ref[...]
        xhat, rstd = _ln_stats(DN_ALPHA * x_ref[...] + s_ref[...])
        diff = xhat * gain_v + b_ref[...] - t_ref[...]
        d_o = diff * (1.0 / D_MODEL)
        du_ref[...] = _ln_input_grad(d_o, gain_v, xhat, rstd)
        _col_sum_update(accl_ref, diff * diff, i == 0)
        _col_sum_update(accg_ref, d_o * xhat, i == 0)
        _col_sum_update(accb_ref, d_o, i == 0)

        @pl.when(i == steps - 1)
        def _():
            per_col = jnp.sum(accl_ref[...], axis=0, keepdims=True)
            loss_ref[...] = jnp.sum(per_col, axis=1, keepdims=True) * (0.5 / D_MODEL)
            dg_ref[...] = jnp.sum(accg_ref[...], axis=0, keepdims=True)
            db_ref[...] = jnp.sum(accb_ref[...], axis=0, keepdims=True)

    row = pl.BlockSpec((tr, D_MODEL), lambda i: (i, 0))
    vec = pl.BlockSpec((1, D_MODEL), lambda i: (0, 0))
    return pl.pallas_call(
        body,
        name="ln3_loss",
        grid=(steps,),
        in_specs=[row, row, vec, vec, row],
        out_specs=[pl.BlockSpec((1, 1), lambda i: (0, 0)), row, vec, vec],
        out_shape=[jax.ShapeDtypeStruct((1, 1), F32), jax.ShapeDtypeStruct((T, D_MODEL), F32),
                   jax.ShapeDtypeStruct((1, D_MODEL), F32), jax.ShapeDtypeStruct((1, D_MODEL), F32)],
        scratch_shapes=[pltpu.VMEM((8, D_MODEL), F32)] * 3,
        compiler_params=_params(("arbitrary",)),
    )(x_prev, sub, gain, bias, target)


def _mem_probs(q_h, k_h):
    s = _dot(q_h, k_h, _NT) * (MEM_HEAD_DIM ** -0.5)
    e = jnp.exp(s - jnp.max(s, axis=1, keepdims=True))
    return e / jnp.sum(e, axis=1, keepdims=True)


def _xattn_fwd(q, kv):
    T, mem_len = q.shape[0], kv.shape[0]
    tq = _pick(T, (512, 256, 128))

    def body(q_ref, kv_ref, o_ref):
        for h in range(MEM_HEADS):
            cols = slice(h * MEM_HEAD_DIM, (h + 1) * MEM_HEAD_DIM)
            vcols = slice(D_MODEL + h * MEM_HEAD_DIM, D_MODEL + (h + 1) * MEM_HEAD_DIM)
            p = _mem_probs(q_ref[:, cols], kv_ref[:, cols])
            o_ref[:, cols] = _dot(p.astype(BF16), kv_ref[:, vcols], _NN).astype(BF16)

    return pl.pallas_call(
        body,
        name="xattn_fwd",
        grid=(T // tq,),
        in_specs=[pl.BlockSpec((tq, D_MODEL), lambda i: (i, 0)),
                  pl.BlockSpec((mem_len, 2 * D_MODEL), lambda i: (0, 0))],
        out_specs=pl.BlockSpec((tq, D_MODEL), lambda i: (i, 0)),
        out_shape=jax.ShapeDtypeStruct((T, D_MODEL), BF16),
        compiler_params=_params(("parallel",)),
    )(q, kv)


def _xattn_bwd(q, kv, d_o):
    T, mem_len = q.shape[0], kv.shape[0]
    tq = _pick(T, (512, 256, 128))

    def body(q_ref, kv_ref, do_ref, dq_ref, dkv_ref):
        @pl.when(pl.program_id(0) == 0)
        def _():
            dkv_ref[...] = jnp.zeros_like(dkv_ref)

        for h in range(MEM_HEADS):
            cols = slice(h * MEM_HEAD_DIM, (h + 1) * MEM_HEAD_DIM)
            vcols = slice(D_MODEL + h * MEM_HEAD_DIM, D_MODEL + (h + 1) * MEM_HEAD_DIM)
            q_h, k_h, do_h = q_ref[:, cols], kv_ref[:, cols], do_ref[:, cols]
            p = _mem_probs(q_h, k_h)
            dp = _dot(do_h, kv_ref[:, vcols], _NT)
            ds = p * (dp - jnp.sum(dp * p, axis=1, keepdims=True))
            dsb = (ds * (MEM_HEAD_DIM ** -0.5)).astype(BF16)
            dq_ref[:, cols] = _dot(dsb, k_h, _NN).astype(BF16)
            dkv_ref[:, cols] += _dot(dsb, q_h, _TN)
            dkv_ref[:, vcols] += _dot(p.astype(BF16), do_h, _TN)

    row = pl.BlockSpec((tq, D_MODEL), lambda i: (i, 0))
    full = pl.BlockSpec((mem_len, 2 * D_MODEL), lambda i: (0, 0))
    return pl.pallas_call(
        body,
        name="xattn_bwd",
        grid=(T // tq,),
        in_specs=[row, full, row],
        out_specs=[row, full],
        out_shape=[jax.ShapeDtypeStruct((T, D_MODEL), BF16), jax.ShapeDtypeStruct((mem_len, 2 * D_MODEL), F32)],
        compiler_params=_params(("arbitrary",)),
    )(q, kv, d_o)


def _swiglu_fwd(f):
    T = f.shape[0]
    tr = _row_tile(T)

    def body(f_ref, o_ref):
        a, b = f_ref[:, :FFN_HIDDEN], f_ref[:, FFN_HIDDEN:]
        o_ref[...] = (a * jax.nn.sigmoid(a) * b).astype(BF16)

    return pl.pallas_call(
        body,
        name="swiglu_fwd",
        grid=(T // tr,),
        in_specs=[pl.BlockSpec((tr, 2 * FFN_HIDDEN), lambda i: (i, 0))],
        out_specs=pl.BlockSpec((tr, FFN_HIDDEN), lambda i: (i, 0)),
        out_shape=jax.ShapeDtypeStruct((T, FFN_HIDDEN), BF16),
        compiler_params=_params(("parallel",)),
    )(f)


def _swiglu_bwd(d_hidden, f):
    T = f.shape[0]
    tr = _row_tile(T)

    def body(d_ref, f_ref, o_ref):
        a, b = f_ref[:, :FFN_HIDDEN], f_ref[:, FFN_HIDDEN:]
        d_h = d_ref[...]
        sig = jax.nn.sigmoid(a)
        o_ref[:, :FFN_HIDDEN] = (d_h * b * (sig * (1.0 + a * (1.0 - sig)))).astype(BF16)
        o_ref[:, FFN_HIDDEN:] = (d_h * (a * sig)).astype(BF16)

    wide = pl.BlockSpec((tr, 2 * FFN_HIDDEN), lambda i: (i, 0))
    return pl.pallas_call(
        body,
        name="swiglu_bwd",
        grid=(T // tr,),
        in_specs=[pl.BlockSpec((tr, FFN_HIDDEN), lambda i: (i, 0)), wide],
        out_specs=wide,
        out_shape=jax.ShapeDtypeStruct((T, 2 * FFN_HIDDEN), BF16),
        compiler_params=_params(("parallel",)),
    )(d_hidden, f)


def _local_step(x, mem, w, small, target):
    T = x.shape[0]
    tables = _ret_tables(T)
    xb, memb = x.astype(BF16), mem.astype(BF16)
    w_in = w["w_in"]

    h_a = _mm(xb, w_in[:, 0:1536], mode="nn", out_dtype=BF16, name="proj_sb")
    h_b = _mm(xb, w_in[:, 1536:2560], mode="nn", out_dtype=F32, name="proj_ret_qk")
    h_c = _mm(xb, w_in[:, 2560:3584], mode="nn", out_dtype=BF16, name="proj_ret_v")
    h_d = _mm(xb, w_in[:, 3584:4608], mode="nn", out_dtype=F32, name="proj_ret_g")
    h_e = _mm(xb, w_in[:, 4608:6656], mode="nn", out_dtype=F32, name="proj_gate")
    a_sb, r_mat = _sb_fwd(h_a)
    y_gated, o_pre, states = _ret_fwd(h_b, h_c, h_d, tables)
    y_sb = _mm(a_sb, w["w_sb_o"], mode="nn", out_dtype=F32, name="sb_out")
    y_ret = _mm(y_gated, w["w_ret_o"], mode="nn", out_dtype=F32, name="ret_out")
    mix_in = _gate_mix_fwd(h_e, small["b_gate"], y_sb, y_ret)
    mix = _mm(mix_in, w["w_mix_o"], mode="nn", out_dtype=F32, name="mix_out")
    x1, xhat1, rstd1 = _ln_fwd(x, mix, small["ln1_g"], small["ln1_b"], "ln1_fwd")
    x1b = x1.astype(BF16)
    q_m = _mm(x1b, w["w_mem_q"], mode="nn", out_dtype=BF16, name="mem_q")
    kv_m = _mm(memb, w["w_mem_kv"], mode="nn", out_dtype=BF16, name="mem_kv")
    o_m = _xattn_fwd(q_m, kv_m)
    xa = _mm(o_m, w["w_mem_o"], mode="nn", out_dtype=F32, name="mem_out")
    x2, xhat2, rstd2 = _ln_fwd(x1, xa, small["ln2_g"], small["ln2_b"], "ln2_fwd")
    x2b = x2.astype(BF16)
    f = _mm(x2b, w["w_ffn_in"], mode="nn", out_dtype=F32, name="ffn_in")
    hidden = _swiglu_fwd(f)
    ff = _mm(hidden, w["w_ffn_out"], mode="nn", out_dtype=F32, name="ffn_out")
    loss, du3, d_ln3_g, d_ln3_b = _ln_loss(x2, ff, small["ln3_g"], small["ln3_b"], target)

    du3b = du3.astype(BF16)
    g_ffn_out = _mm(hidden, du3b, mode="tn", out_dtype=F32, name="g_ffn_out")
    d_hidden = _mm(du3b, w["w_ffn_out"], mode="nt", out_dtype=F32, name="d_hidden")
    d_f = _swiglu_bwd(d_hidden, f)
    g_ffn_in = _mm(x2b, d_f, mode="tn", out_dtype=F32, name="g_ffn_in")
    d_x2 = _mm(d_f, w["w_ffn_in"], mode="nt", out_dtype=F32, name="d_x2", res=du3, res_scale=DN_ALPHA)
    du2, d_ln2_g, d_ln2_b = _ln_bwd(d_x2, xhat2, rstd2, small["ln2_g"], "ln2_bwd")
    du2b = du2.astype(BF16)
    g_mem_o = _mm(o_m, du2b, mode="tn", out_dtype=F32, name="g_mem_o")
    d_om = _mm(du2b, w["w_mem_o"], mode="nt", out_dtype=BF16, name="d_om")
    d_qm, d_kvm = _xattn_bwd(q_m, kv_m, d_om)
    g_mem_q = _mm(x1b, d_qm, mode="tn", out_dtype=F32, name="g_mem_q")
    g_mem_kv = _mm(memb, d_kvm.astype(BF16), mode="tn", out_dtype=F32, name="g_mem_kv")
    d_x1 = _mm(d_qm, w["w_mem_q"], mode="nt", out_dtype=F32, name="d_x1", res=du2, res_scale=DN_ALPHA)
    du1, d_ln1_g, d_ln1_b = _ln_bwd(d_x1, xhat1, rstd1, small["ln1_g"], "ln1_bwd")
    du1b = du1.astype(BF16)
    g_mix_o = _mm(mix_in, du1b, mode="tn", out_dtype=F32, name="g_mix_o")
    d_mix_in = _mm(du1b, w["w_mix_o"], mode="nt", out_dtype=F32, name="d_mix_in")
    d_ysb, d_yret, d_e, d_b_gate = _gate_mix_bwd(d_mix_in, h_e, small["b_gate"], y_sb, y_ret)
    g_sb_o = _mm(a_sb, d_ysb, mode="tn", out_dtype=F32, name="g_sb_o")
    g_ret_o = _mm(y_gated, d_yret, mode="tn", out_dtype=F32, name="g_ret_o")
    d_asb = _mm(d_ysb, w["w_sb_o"], mode="nt", out_dtype=BF16, name="d_asb")
    d_ygated = _mm(d_yret, w["w_ret_o"], mode="nt", out_dtype=F32, name="d_ygated")
    d_rq, d_rk, d_c, d_d = _ret_bwd(d_ygated, o_pre, states, h_b, h_c, h_d, tables)
    d_q, d_k, d_v = _sb_bwd(h_a, d_asb, r_mat)
    d_h = jnp.concatenate([d_q, d_k.astype(BF16), d_v.astype(BF16), d_rq, d_rk, d_c, d_d, d_e], axis=1)
    g_in = _mm(xb, d_h, mode="tn", out_dtype=F32, name="g_in")
    d_x = _mm(d_h, w_in, mode="nt", out_dtype=F32, name="d_x", res=du1, res_scale=DN_ALPHA)

    grads = {"w_in": g_in, "w_sb_o": g_sb_o, "w_ret_o": g_ret_o, "w_mix_o": g_mix_o, "w_mem_q": g_mem_q,
             "w_mem_kv": g_mem_kv, "w_mem_o": g_mem_o, "w_ffn_in": g_ffn_in, "w_ffn_out": g_ffn_out}
    small_grads = {"b_gate": d_b_gate, "ln1_g": d_ln1_g, "ln1_b": d_ln1_b, "ln2_g": d_ln2_g, "ln2_b": d_ln2_b,
                   "ln3_g": d_ln3_g, "ln3_b": d_ln3_b}
    return loss, d_x, grads, small_grads


def _my_index():
    return 4 * lax.axis_index("x") + 2 * lax.axis_index("y") + lax.axis_index("c")


def _peer(k):
    x, y, c = lax.axis_index("x"), lax.axis_index("y"), lax.axis_index("c")
    bx, by, bc = (k >> 2) & 1, (k >> 1) & 1, k & 1
    px = (1 - x) if bx else x
    py = (1 - y) if by else y
    pc = (1 - c) if bc else c
    return (px, py, pc), 4 * px + 2 * py + pc


def _exchange(bufs, scatter, name):
    n = len(bufs)

    def body(*refs):
        src, dst = refs[:n], refs[n:2 * n]
        send_sems, recv_sems, local_sems = refs[2 * n:]
        me = _my_index()
        copies = []
        for a in range(n):
            own = src[a].at[me] if scatter else src[a]
            local = pltpu.make_async_copy(own, dst[a].at[me], local_sems.at[a])
            local.start()
            copies.append(local)
            for k in range(1, N_DEV):
                peer, peer_idx = _peer(k)
                rc = pltpu.make_async_remote_copy(
                    src_ref=src[a].at[peer_idx] if scatter else src[a],
                    dst_ref=dst[a].at[me],
                    send_sem=send_sems.at[a, k - 1],
                    recv_sem=recv_sems.at[a, k - 1],
                    device_id=peer,
                    device_id_type=pl.DeviceIdType.MESH,
                )
                rc.start()
                copies.append(rc)
        for cp in copies:
            cp.wait()

    out_shapes = [jax.ShapeDtypeStruct(b.shape if scatter else (N_DEV,) + b.shape, b.dtype) for b in bufs]
    any_spec = pl.BlockSpec(memory_space=pl.ANY)
    return pl.pallas_call(
        body,
        name=name,
        in_specs=[any_spec] * n,
        out_specs=[any_spec] * n,
        out_shape=out_shapes,
        scratch_shapes=[pltpu.SemaphoreType.DMA((n, N_DEV - 1)), pltpu.SemaphoreType.DMA((n, N_DEV - 1)),
                        pltpu.SemaphoreType.DMA((n,))],
    )(*bufs)


def _adamw_math(w, g, m, v):
    m = ADAM_B1 * m + (1.0 - ADAM_B1) * g
    v = ADAM_B2 * v + (1.0 - ADAM_B2) * jnp.square(g)
    m_hat = m / (1.0 - ADAM_B1 ** ADAM_STEP)
    v_hat = v / (1.0 - ADAM_B2 ** ADAM_STEP)
    delta = -ADAM_LR * (m_hat / (jnp.sqrt(v_hat) + ADAM_EPS) + ADAM_WD * w)
    return delta, m, v


def _adamw(parts, w, m, v, name):
    R, C = w.shape
    tr = max(t for t in range(16, min(R, 256) + 1, 16) if R % t == 0) if R >= 16 else R

    def body(p_ref, w_ref, m_ref, v_ref, g_ref, d_ref, nm_ref, nv_ref):
        g = p_ref[0].astype(F32)
        for j in range(1, N_DEV):
            g = g + p_ref[j].astype(F32)
        delta, nm, nv = _adamw_math(w_ref[...], g, m_ref[...], v_ref[...])
        g_ref[...] = g
        d_ref[...] = delta
        nm_ref[...] = nm
        nv_ref[...] = nv

    blk = pl.BlockSpec((tr, C), lambda i: (i, 0))
    out = jax.ShapeDtypeStruct((R, C), F32)
    return pl.pallas_call(
        body,
        name=name,
        grid=(R // tr,),
        in_specs=[pl.BlockSpec((N_DEV, tr, C), lambda i: (0, i, 0)), blk, blk, blk],
        out_specs=[blk] * 4,
        out_shape=[out] * 4,
        compiler_params=_params(("parallel",)),
    )(parts, w, m, v)


_SHARD_AXIS = {"w_in": 1, "w_sb_o": 1, "w_ret_o": 0, "w_mix_o": 0, "w_mem_q": 0, "w_mem_kv": 1, "w_mem_o": 0,
               "w_ffn_in": 1, "w_ffn_out": 0}
_MATRICES = tuple(_SHARD_AXIS)
_SMALL = ("b_gate", "ln1_g", "ln1_b", "ln2_g", "ln2_b", "ln3_g", "ln3_b")
_WEIGHT_ORDER = ("w_in", "b_gate", "w_sb_o", "w_ret_o", "w_mix_o", "ln1_g", "ln1_b", "w_mem_q", "w_mem_kv", "w_mem_o",
                 "ln2_g", "ln2_b", "w_ffn_in", "w_ffn_out", "ln3_g", "ln3_b")


def _assemble(name, gathered):
    if _SHARD_AXIS[name] == 0:
        return gathered.reshape(-1, gathered.shape[2])
    return jnp.transpose(gathered, (1, 0, 2)).reshape(gathered.shape[1], -1)


def _to_slots(name, full):
    if _SHARD_AXIS[name] == 0:
        return full.reshape(N_DEV, full.shape[0] // N_DEV, full.shape[1])
    return jnp.transpose(full.reshape(full.shape[0], N_DEV, full.shape[1] // N_DEV), (1, 0, 2))


def _pack_small(vals):
    return jnp.concatenate([vals["b_gate"].reshape(2, D_MODEL)] + [vals[n] for n in _SMALL[1:]], axis=0)


def _unpack_small(packed):
    out = {"b_gate": packed[0:2].reshape(1, 2 * D_MODEL)}
    for i, n in enumerate(_SMALL[1:]):
        out[n] = packed[2 + i:3 + i]
    return out


def kernel(x, mem, w_in, b_gate, w_sb_o, w_ret_o, w_mix_o, ln1_g, ln1_b, w_mem_q, w_mem_kv, w_mem_o, ln2_g, ln2_b, w_ffn_in, w_ffn_out, ln3_g, ln3_b, loss_target, m_w_in, m_b_gate, m_w_sb_o, m_w_ret_o, m_w_mix_o, m_ln1_g, m_ln1_b, m_w_mem_q, m_w_mem_kv, m_w_mem_o, m_ln2_g, m_ln2_b, m_w_ffn_in, m_w_ffn_out, m_ln3_g, m_ln3_b, v_w_in, v_b_gate, v_w_sb_o, v_w_ret_o, v_w_mix_o, v_ln1_g, v_ln1_b, v_w_mem_q, v_w_mem_kv, v_w_mem_o, v_ln2_g, v_ln2_b, v_w_ffn_in, v_w_ffn_out, v_ln3_g, v_ln3_b):
    weights = dict(w_in=w_in, b_gate=b_gate, w_sb_o=w_sb_o, w_ret_o=w_ret_o, w_mix_o=w_mix_o, ln1_g=ln1_g, ln1_b=ln1_b,
                   w_mem_q=w_mem_q, w_mem_kv=w_mem_kv, w_mem_o=w_mem_o, ln2_g=ln2_g, ln2_b=ln2_b, w_ffn_in=w_ffn_in,
                   w_ffn_out=w_ffn_out, ln3_g=ln3_g, ln3_b=ln3_b)
    mom1 = dict(w_in=m_w_in, b_gate=m_b_gate, w_sb_o=m_w_sb_o, w_ret_o=m_w_ret_o, w_mix_o=m_w_mix_o, ln1_g=m_ln1_g,
                ln1_b=m_ln1_b, w_mem_q=m_w_mem_q, w_mem_kv=m_w_mem_kv, w_mem_o=m_w_mem_o, ln2_g=m_ln2_g, ln2_b=m_ln2_b,
                w_ffn_in=m_w_ffn_in, w_ffn_out=m_w_ffn_out, ln3_g=m_ln3_g, ln3_b=m_ln3_b)
    mom2 = dict(w_in=v_w_in, b_gate=v_b_gate, w_sb_o=v_w_sb_o, w_ret_o=v_w_ret_o, w_mix_o=v_w_mix_o, ln1_g=v_ln1_g,
                ln1_b=v_ln1_b, w_mem_q=v_w_mem_q, w_mem_kv=v_w_mem_kv, w_mem_o=v_w_mem_o, ln2_g=v_ln2_g, ln2_b=v_ln2_b,
                w_ffn_in=v_w_ffn_in, w_ffn_out=v_w_ffn_out, ln3_g=v_ln3_g, ln3_b=v_ln3_b)

    shards = [weights[n][0].astype(BF16) for n in _MATRICES]
    gathered = _exchange(shards, False, "gather_weights")
    full = {n: _assemble(n, g) for n, g in zip(_MATRICES, gathered)}
    small = {n: weights[n] for n in _SMALL}

    loss, d_x, grads, small_grads = _local_step(x[0], mem[0], full, small, loss_target[0])

    slots = [_to_slots(n, grads[n]).astype(BF16) for n in _MATRICES]
    small_part = _pack_small(small_grads)
    received = _exchange(slots + [jnp.broadcast_to(small_part[None], (N_DEV,) + small_part.shape)], True, "scatter_grads")

    new = {}
    for n, parts in zip(_MATRICES, received[:-1]):
        new[n] = _adamw(parts, weights[n][0], mom1[n][0], mom2[n][0], "adamw_" + n)
    packed = _adamw(received[-1], _pack_small({n: weights[n] for n in _SMALL}), _pack_small({n: mom1[n] for n in _SMALL}),
                    _pack_small({n: mom2[n] for n in _SMALL}), "adamw_small")
    small_new = [_unpack_small(p) for p in packed]

    outs = [lax.psum(loss[0, 0], MESH_AXES), d_x[None]]
    for slot in range(4):
        for n in _WEIGHT_ORDER:
            outs.append(new[n][slot][None] if n in new else small_new[slot][n])
    return tuple(outs)
```

```python
import functools
import math

import jax
import jax.numpy as jnp
from jax import lax
from jax.experimental import pallas as pl
from jax.experimental.pallas import tpu as pltpu

F32 = jnp.float32
BF16 = jnp.bfloat16

N_DEV = 8
D_MODEL = 1024
SB_HEAD_DIM = 64
SB_WIDTH = 512
RET_HEADS = 4
RET_QK_DIM = 128
RET_V_DIM = 256
RET_QK_WIDTH = 512
RET_V_WIDTH = 1024
RET_CHUNK = 128
ROPE_BASE = 10000.0
MEM_HEADS = 4
MEM_HEAD_DIM = 256
FFN_HIDDEN = 2816
DN_ALPHA = 2.0 ** 0.25
LN_EPS = 1e-5
ADAM_LR = 0.001
ADAM_B1 = 0.9
ADAM_B2 = 0.999
ADAM_EPS = 1e-08
ADAM_WD = 0.01
ADAM_STEP = 10

VMEM_LIMIT_BYTES = 52 * 1024 * 1024
LANES = 128
SB_KEY_BLOCK = 128
SB_Q_BLOCK = 256
SB_DEAD_LOG = -105.0

MESH_AXES = ("x", "y", "c")


def _pick(dim, prefs):
    for p in prefs:
        if dim % p == 0:
            return p
    return dim


def _params(sem):
    return pltpu.CompilerParams(dimension_semantics=sem, vmem_limit_bytes=VMEM_LIMIT_BYTES)


def _dot(a, b, dims):
    return lax.dot_general(a, b, (dims, ((), ())), preferred_element_type=F32)


_NN = ((1,), (0,))
_NT = ((1,), (1,))
_TN = ((0,), (0,))


def _mm(a, b, *, mode, out_dtype, name, res=None, res_scale=1.0):
    if mode == "nn":
        (M, K), (K2, N) = a.shape, b.shape
    elif mode == "nt":
        (M, K), (N, K2) = a.shape, b.shape
    else:
        (K, M), (K2, N) = a.shape, b.shape
    assert K == K2, (a.shape, b.shape, mode)
    tm = _pick(M, (1024, 512, 256, 128))
    tn = _pick(N, (512, 256, 128))
    tk = _pick(K, (1024, 512, 256, 128))
    nk = K // tk
    dims = {"nn": _NN, "nt": _NT, "tn": _TN}[mode]

    def body(*refs):
        if res is None:
            a_ref, b_ref, o_ref = refs[:3]
            r_ref = None
        else:
            a_ref, b_ref, r_ref, o_ref = refs[:4]
        acc_ref = refs[-1] if nk > 1 else None
        part = _dot(a_ref[...].astype(BF16), b_ref[...].astype(BF16), dims)

        def finish(total):
            if r_ref is not None:
                total = total + res_scale * r_ref[...]
            o_ref[...] = total.astype(out_dtype)

        if nk == 1:
            finish(part)
        else:
            k = pl.program_id(2)

            @pl.when(k == 0)
            def _():
                acc_ref[...] = part

            @pl.when(k > 0)
            def _():
                acc_ref[...] += part

            @pl.when(k == nk - 1)
            def _():
                finish(acc_ref[...])

    if mode == "nn":
        a_spec = pl.BlockSpec((tm, tk), lambda i, j, k: (i, k))
        b_spec = pl.BlockSpec((tk, tn), lambda i, j, k: (k, j))
    elif mode == "nt":
        a_spec = pl.BlockSpec((tm, tk), lambda i, j, k: (i, k))
        b_spec = pl.BlockSpec((tn, tk), lambda i, j, k: (j, k))
    else:
        a_spec = pl.BlockSpec((tk, tm), lambda i, j, k: (k, i))
        b_spec = pl.BlockSpec((tk, tn), lambda i, j, k: (k, j))
    o_spec = pl.BlockSpec((tm, tn), lambda i, j, k: (i, j))
    in_specs = [a_spec, b_spec] + ([o_spec] if res is not None else [])
    args = (a, b) + ((res,) if res is not None else ())
    return pl.pallas_call(
        body,
        name=name,
        grid=(M // tm, N // tn, nk),
        in_specs=in_specs,
        out_specs=o_spec,
        out_shape=jax.ShapeDtypeStruct((M, N), out_dtype),
        scratch_shapes=[pltpu.VMEM((tm, tn), F32)] if nk > 1 else [],
        compiler_params=_params(("parallel", "parallel", "arbitrary")),
    )(*args)


def _pair_rows(blk, lane_is_a):
    zero = jnp.zeros_like(blk)
    return jnp.concatenate([jnp.where(lane_is_a, blk, zero), jnp.where(lane_is_a, zero, blk)], axis=0)


def _pair_scan_matrix(strict_after):
    r = lax.broadcasted_iota(jnp.int32, (4 * LANES, 2 * LANES), 0) & (2 * LANES - 1)
    c = lax.broadcasted_iota(jnp.int32, (4 * LANES, 2 * LANES), 1)
    same = (r >= LANES) == (c >= LANES)
    rr, cc = r & (LANES - 1), c & (LANES - 1)
    tri = (rr > cc) if strict_after else (rr < cc)
    return jnp.where(same & tri, 1.0, 0.0).astype(BF16)


def _split_dot(val, mat):
    hi = val.astype(BF16)
    lo = (val - hi.astype(F32)).astype(BF16)
    return _dot(jnp.concatenate([hi, lo], axis=1), mat, _NN)


def _pair_cols(col_a, col_b, rows):
    return jnp.concatenate([jnp.broadcast_to(col_a, (rows, LANES)), jnp.broadcast_to(col_b, (rows, LANES))], axis=1)


def _sb_scores(q, kk, mask):
    z = _dot(q, kk, _NT) * (SB_HEAD_DIM ** -0.5)
    t = jnp.log1p(jnp.exp(-jnp.abs(z)))
    log_beta = jnp.minimum(z, 0.0) - t
    log_rem = -jnp.maximum(z, 0.0) - t
    if mask is not None:
        log_rem = jnp.where(mask, log_rem, 0.0)
    return log_beta, log_rem


def _sb_fwd(h_a):
    T = h_a.shape[0]
    tq = _pick(T, (SB_Q_BLOCK, SB_KEY_BLOCK))
    nq, per_q = T // tq, tq // SB_KEY_BLOCK
    assert T // SB_KEY_BLOCK <= LANES

    def body(q_ref, k_ref, v_ref, a_ref, r_ref):
        i = pl.program_id(1)
        q = q_ref[...]
        lane_is_a = lax.broadcasted_iota(jnp.int32, (SB_KEY_BLOCK, LANES), 1) < SB_HEAD_DIM
        after = _pair_scan_matrix(True)
        row = lax.broadcasted_iota(jnp.int32, (tq, 2 * LANES), 0)
        col = lax.broadcasted_iota(jnp.int32, (tq, 2 * LANES), 1) & (LANES - 1)
        blk_lane = lax.broadcasted_iota(jnp.int32, (tq, LANES), 1)

        def tile(kb, masked, carry):
            acc, ra, rb, rma, rmb = carry
            ks = pl.multiple_of(kb * SB_KEY_BLOCK, SB_KEY_BLOCK)
            kk = _pair_rows(k_ref[pl.ds(ks, SB_KEY_BLOCK), :], lane_is_a)
            vv = _pair_rows(v_ref[pl.ds(ks, SB_KEY_BLOCK), :], lane_is_a)
            mask = ((ks + col) < (i * tq + row)) if masked else None
            log_beta, log_rem = _sb_scores(q, kk, mask)
            later = _split_dot(log_rem, after) + _pair_cols(ra, rb, tq)
            w = jnp.exp(log_beta + later)
            if masked:
                w = jnp.where(mask, w, 0.0)
            acc = acc + _dot(w.astype(BF16), vv, _NN)
            here = (blk_lane == kb).astype(F32)
            rma = rma + ra * here
            rmb = rmb + rb * here
            ra = ra + jnp.sum(log_rem[:, :LANES], axis=1, keepdims=True)
            rb = rb + jnp.sum(log_rem[:, LANES:], axis=1, keepdims=True)
            return acc, ra, rb, rma, rmb

        carry = (jnp.zeros((tq, LANES), F32), jnp.zeros((tq, 1), F32), jnp.zeros((tq, 1), F32),
                 jnp.zeros((tq, LANES), F32), jnp.zeros((tq, LANES), F32))
        for d in range(per_q):
            carry = tile(i * per_q + (per_q - 1 - d), True, carry)
        n_full = i * per_q

        def alive(c):
            return jnp.logical_and(c[0] < n_full, jnp.max(jnp.maximum(c[2], c[3])) > SB_DEAD_LOG)

        def step(c):
            return (c[0] + 1,) + tile(n_full - 1 - c[0], False, c[1:])

        done, acc, ra, rb, rma, rmb = lax.while_loop(alive, step, (jnp.int32(0),) + carry)
        skipped = blk_lane < (n_full - done)
        rma = jnp.where(skipped, ra, rma)
        rmb = jnp.where(skipped, rb, rmb)
        a_ref[...] = acc.astype(BF16)
        r_ref[0] = rma
        r_ref[1] = rmb

    return pl.pallas_call(
        body,
        name="sb_fwd",
        grid=(4, nq),
        in_specs=[pl.BlockSpec((tq, LANES), lambda p, i: (i, p)),
                  pl.BlockSpec((T, LANES), lambda p, i: (0, 4 + p)),
                  pl.BlockSpec((T, LANES), lambda p, i: (0, 8 + p))],
        out_specs=[pl.BlockSpec((tq, LANES), lambda p, i: (i, p)),
                   pl.BlockSpec((2, tq, LANES), lambda p, i: (p, i, 0))],
        out_shape=[jax.ShapeDtypeStruct((T, SB_WIDTH), BF16),
                   jax.ShapeDtypeStruct((8, T, LANES), F32)],
        compiler_params=_params(("parallel", "arbitrary")),
    )(h_a, h_a, h_a)


def _sb_bwd(h_a, d_out, r_mat):
    T = h_a.shape[0]
    tq = _pick(T, (SB_Q_BLOCK, SB_KEY_BLOCK))
    nq, per_q = T // tq, tq // SB_KEY_BLOCK

    def body(q_ref, k_ref, v_ref, do_ref, r_ref, dq_ref, dk_ref, dv_ref):
        i = pl.program_id(1)

        @pl.when(i == 0)
        def _():
            dk_ref[...] = jnp.zeros_like(dk_ref)
            dv_ref[...] = jnp.zeros_like(dv_ref)

        q = q_ref[...]
        d_o = do_ref[...]
        rma, rmb = r_ref[0], r_ref[1]
        lane_is_a = lax.broadcasted_iota(jnp.int32, (SB_KEY_BLOCK, LANES), 1) < SB_HEAD_DIM
        after = _pair_scan_matrix(True)
        before = _pair_scan_matrix(False)
        row = lax.broadcasted_iota(jnp.int32, (tq, 2 * LANES), 0)
        col = lax.broadcasted_iota(jnp.int32, (tq, 2 * LANES), 1) & (LANES - 1)
        blk_lane = lax.broadcasted_iota(jnp.int32, (tq, LANES), 1)

        def tile(kb, masked, carry):
            dq, ca, cb = carry
            ks = pl.multiple_of(kb * SB_KEY_BLOCK, SB_KEY_BLOCK)
            kk = _pair_rows(k_ref[pl.ds(ks, SB_KEY_BLOCK), :], lane_is_a)
            vv = _pair_rows(v_ref[pl.ds(ks, SB_KEY_BLOCK), :], lane_is_a)
            mask = ((ks + col) < (i * tq + row)) if masked else None
            log_beta, log_rem = _sb_scores(q, kk, mask)
            here = (blk_lane == kb).astype(F32)
            ra = jnp.sum(rma * here, axis=1, keepdims=True)
            rb = jnp.sum(rmb * here, axis=1, keepdims=True)
            later = _split_dot(log_rem, after) + _pair_cols(ra, rb, tq)
            w = jnp.exp(log_beta + later)
            if masked:
                w = jnp.where(mask, w, 0.0)
            da = _dot(d_o, vv, _NT) * w
            prefix = _split_dot(da, before) + _pair_cols(ca, cb, tq)
            sig = jnp.exp(log_beta)
            dz = da * (1.0 - sig) - prefix * sig
            if masked:
                dz = jnp.where(mask, dz, 0.0)
            dzb = (dz * (SB_HEAD_DIM ** -0.5)).astype(BF16)
            dq = dq + _dot(dzb, kk, _NN)
            dkk = _dot(dzb, q, _TN)
            dvv = _dot(w.astype(BF16), d_o, _TN)
            dk_ref[pl.ds(ks, SB_KEY_BLOCK), :] += jnp.where(lane_is_a, dkk[:SB_KEY_BLOCK], dkk[SB_KEY_BLOCK:])
            dv_ref[pl.ds(ks, SB_KEY_BLOCK), :] += jnp.where(lane_is_a, dvv[:SB_KEY_BLOCK], dvv[SB_KEY_BLOCK:])
            ca = ca + jnp.sum(da[:, :LANES], axis=1, keepdims=True)
            cb = cb + jnp.sum(da[:, LANES:], axis=1, keepdims=True)
            return dq, ca, cb

        carry = (jnp.zeros((tq, LANES), F32), jnp.zeros((tq, 1), F32), jnp.zeros((tq, 1), F32))
        n_full = i * per_q
        col_max = jnp.max(jnp.maximum(rma, rmb), axis=0, keepdims=True)
        dead = jnp.logical_and(col_max <= SB_DEAD_LOG, blk_lane[0:1] < n_full)
        first = jnp.sum(dead.astype(F32)).astype(jnp.int32)
        carry = lax.fori_loop(first, n_full, lambda j, c: tile(j, False, c), carry)
        for d in range(per_q):
            carry = tile(i * per_q + d, True, carry)
        dq_ref[...] = carry[0].astype(BF16)

    return pl.pallas_call(
        body,
        name="sb_bwd",
        grid=(4, nq),
        in_specs=[pl.BlockSpec((tq, LANES), lambda p, i: (i, p)),
                  pl.BlockSpec((T, LANES), lambda p, i: (0, 4 + p)),
                  pl.BlockSpec((T, LANES), lambda p, i: (0, 8 + p)),
                  pl.BlockSpec((tq, LANES), lambda p, i: (i, p)),
                  pl.BlockSpec((2, tq, LANES), lambda p, i: (p, i, 0))],
        out_specs=[pl.BlockSpec((tq, LANES), lambda p, i: (i, p)),
                   pl.BlockSpec((T, LANES), lambda p, i: (0, p)),
                   pl.BlockSpec((T, LANES), lambda p, i: (0, p))],
        out_shape=[jax.ShapeDtypeStruct((T, SB_WIDTH), BF16),
                   jax.ShapeDtypeStruct((T, SB_WIDTH), F32),
                   jax.ShapeDtypeStruct((T, SB_WIDTH), F32)],
        compiler_params=_params(("parallel", "arbitrary")),
    )(h_a, h_a, h_a, d_out, r_mat)


def _ret_tables(T):
    half = RET_QK_DIM // 2
    inv = 1.0 / (ROPE_BASE ** (jnp.arange(half, dtype=F32) / half))
    ang = jnp.arange(T, dtype=F32)[:, None] * inv[None, :]
    cos, sin = jnp.cos(ang), jnp.sin(ang)
    cos_t = jnp.concatenate([cos, cos], axis=1)
    sin_t = jnp.concatenate([-sin, sin], axis=1)
    log_gamma = jnp.log1p(-jnp.exp2(-5.0 - jnp.arange(RET_HEADS, dtype=F32)))
    idx = jnp.arange(RET_CHUNK, dtype=F32)
    rel = idx[:, None] - idx[None, :]
    decay = jnp.where(rel[None] >= 0, jnp.exp(log_gamma[:, None, None] * jnp.maximum(rel, 0.0)[None]), 0.0)
    k_decay = jnp.exp(log_gamma[None, :] * (RET_CHUNK - 1.0 - idx)[:, None])
    q_decay = jnp.exp(log_gamma[None, :] * (idx + 1.0)[:, None])
    chunk_decay = jnp.exp(log_gamma * RET_CHUNK)
    k_dec = jnp.broadcast_to(k_decay.T[:, :, None], (RET_HEADS, RET_CHUNK, LANES))
    q_dec = jnp.broadcast_to(q_decay.T[:, :, None], (RET_HEADS, RET_CHUNK, LANES))
    c_dec = jnp.broadcast_to(chunk_decay[:, None, None], (RET_HEADS, 8, LANES))
    return cos_t, sin_t, decay, k_dec, q_dec, c_dec


def _rotary(x, cos_t, sin_t):
    return x * cos_t + pltpu.roll(x, RET_QK_DIM // 2, 1) * sin_t


def _rotary_transpose(dy, cos_t, sin_t):
    return dy * cos_t + pltpu.roll(dy * sin_t, RET_QK_DIM // 2, 1)


def _head_norm(o):
    mu = jnp.mean(o, axis=1, keepdims=True)
    cen = o - mu
    var = jnp.mean(cen * cen, axis=1, keepdims=True)
    rstd = lax.rsqrt(var + LN_EPS)
    return cen * rstd, rstd


def _ret_specs(nc, reverse):
    def n_of(n):
        return (nc - 1 - n) if reverse else n

    qk = pl.BlockSpec((RET_CHUNK, RET_QK_DIM), lambda h, n: (n_of(n), h))
    vv = pl.BlockSpec((RET_CHUNK, RET_V_DIM), lambda h, n: (n_of(n), h))
    pos = pl.BlockSpec((RET_CHUNK, LANES), lambda h, n: (n_of(n), 0))
    per_head = pl.BlockSpec((1, RET_CHUNK, LANES), lambda h, n: (h, 0, 0))
    c_dec = pl.BlockSpec((1, 8, LANES), lambda h, n: (h, 0, 0))
    state = pl.BlockSpec((1, 1, RET_QK_DIM, RET_V_DIM), lambda h, n: (h, n_of(n), 0, 0))
    return qk, vv, pos, per_head, c_dec, state


def _ret_fwd(h_b, h_c, h_d, tables):
    T = h_b.shape[0]
    nc = T // RET_CHUNK
    qk, vv, pos, per_head, c_dec, state = _ret_specs(nc, False)
    k_spec = pl.BlockSpec((RET_CHUNK, RET_QK_DIM), lambda h, n: (n, RET_HEADS + h))

    def body(q_ref, k_ref, v_ref, g_ref, cos_ref, sin_ref, dec_ref, kd_ref, qd_ref, cd_ref,
             y_ref, o_ref, st_ref, state_ref):
        @pl.when(pl.program_id(1) == 0)
        def _():
            state_ref[...] = jnp.zeros_like(state_ref)

        cos_t, sin_t = cos_ref[...], sin_ref[...]
        q = _rotary(q_ref[...], cos_t, sin_t) * (RET_QK_DIM ** -0.5)
        k = _rotary(k_ref[...], cos_t, sin_t)
        v = v_ref[...]
        state = state_ref[...]
        scores = _dot(q.astype(BF16), k.astype(BF16), _NT) * dec_ref[0]
        inner = _dot(scores.astype(BF16), v, _NN)
        cross = _dot((q * qd_ref[0]).astype(BF16), state.astype(BF16), _NN)
        o = inner + cross
        st_ref[0, 0] = state
        kv = _dot((k * kd_ref[0]).astype(BF16), v, _TN)
        state_ref[...] = state * cd_ref[0, 0:1, 0:1] + kv
        o_ref[...] = o
        normed, _ = _head_norm(o)
        gate = g_ref[...]
        y_ref[...] = (gate * jax.nn.sigmoid(gate) * normed).astype(BF16)

    return pl.pallas_call(
        body,
        name="ret_fwd",
        grid=(RET_HEADS, nc),
        in_specs=[qk, k_spec, vv, vv, pos, pos, per_head, per_head, per_head, c_dec],
        out_specs=[vv, vv, state],
        out_shape=[jax.ShapeDtypeStruct((T, RET_V_WIDTH), BF16),
                   jax.ShapeDtypeStruct((T, RET_V_WIDTH), F32),
                   jax.ShapeDtypeStruct((RET_HEADS, nc, RET_QK_DIM, RET_V_DIM), F32)],
        scratch_shapes=[pltpu.VMEM((RET_QK_DIM, RET_V_DIM), F32)],
        compiler_params=_params(("parallel", "arbitrary")),
    )(h_b, h_b, h_c, h_d, *tables)


def _ret_bwd(d_y, o_pre, states, h_b, h_c, h_d, tables):
    T = h_b.shape[0]
    nc = T // RET_CHUNK
    qk, vv, pos, per_head, c_dec, state = _ret_specs(nc, True)
    k_spec = pl.BlockSpec((RET_CHUNK, RET_QK_DIM), lambda h, n: (nc - 1 - n, RET_HEADS + h))

    def body(dy_ref, o_ref, st_ref, q_ref, k_ref, v_ref, g_ref, cos_ref, sin_ref, dec_ref, kd_ref, qd_ref, cd_ref,
             dq_ref, dk_ref, dv_ref, dg_ref, carry_ref):
        @pl.when(pl.program_id(1) == 0)
        def _():
            carry_ref[...] = jnp.zeros_like(carry_ref)

        cos_t, sin_t = cos_ref[...], sin_ref[...]
        scale = RET_QK_DIM ** -0.5
        q = _rotary(q_ref[...], cos_t, sin_t) * scale
        k = _rotary(k_ref[...], cos_t, sin_t)
        v = v_ref[...]
        decay, k_dec, q_dec = dec_ref[0], kd_ref[0], qd_ref[0]
        chunk_decay = cd_ref[0, 0:1, 0:1]
        state = st_ref[0, 0].astype(BF16)
        later = carry_ref[...]
        later_b = later.astype(BF16)

        gate = g_ref[...]
        sig = jax.nn.sigmoid(gate)
        silu = gate * sig
        normed, rstd = _head_norm(o_ref[...])
        d_y = dy_ref[...]
        dg_ref[...] = (d_y * normed * (sig * (1.0 + gate * (1.0 - sig)))).astype(BF16)
        d_n = d_y * silu
        d_o = rstd * (d_n - jnp.mean(d_n, axis=1, keepdims=True)
                      - normed * jnp.mean(d_n * normed, axis=1, keepdims=True))
        d_ob = d_o.astype(BF16)

        qb, kb = q.astype(BF16), k.astype(BF16)
        qd_b, kd_b = (q * q_dec).astype(BF16), (k * k_dec).astype(BF16)
        scores = _dot(qb, kb, _NT) * decay
        d_scores = (_dot(d_ob, v, _NT) * decay).astype(BF16)
        dq = _dot(d_scores, kb, _NN) + _dot(d_ob, state, _NT) * q_dec
        dk = _dot(d_scores, qb, _TN) + _dot(v, later_b, _NT) * k_dec
        dv = _dot(scores.astype(BF16), d_ob, _TN) + _dot(kd_b, later_b, _NN)
        carry_ref[...] = _dot(qd_b, d_ob, _TN) + chunk_decay * later
        dq_ref[...] = _rotary_transpose(dq * scale, cos_t, sin_t).astype(BF16)
        dk_ref[...] = _rotary_transpose(dk, cos_t, sin_t).astype(BF16)
        dv_ref[...] = dv.astype(BF16)

    return pl.pallas_call(
        body,
        name="ret_bwd",
        grid=(RET_HEADS, nc),
        in_specs=[vv, vv, state, qk, k_spec, vv, vv, pos, pos, per_head, per_head, per_head, c_dec],
        out_specs=[qk, qk, vv, vv],
        out_shape=[jax.ShapeDtypeStruct((T, RET_QK_WIDTH), BF16),
                   jax.ShapeDtypeStruct((T, RET_QK_WIDTH), BF16),
                   jax.ShapeDtypeStruct((T, RET_V_WIDTH), BF16),
                   jax.ShapeDtypeStruct((T, RET_V_WIDTH), BF16)],
        scratch_shapes=[pltpu.VMEM((RET_QK_DIM, RET_V_DIM), F32)],
        compiler_params=_params(("parallel", "arbitrary")),
    )(d_y, o_pre, states, h_b, h_b, h_c, h_d, *tables)


def _row_tile(T):
    return _pick(T, (256, 128))


def _gate_mix_fwd(h_e, b_gate, y_sb, y_ret):
    T = h_e.shape[0]
    tr = _row_tile(T)

    def body(g0_ref, g1_ref, b0_ref, b1_ref, ys_ref, yr_ref, o_ref):
        g0 = jax.nn.sigmoid(g0_ref[...] + b0_ref[...])
        g1 = jax.nn.sigmoid(g1_ref[...] + b1_ref[...])
        o_ref[...] = (g0 * ys_ref[...] + g1 * yr_ref[...]).astype(BF16)

    row = pl.BlockSpec((tr, D_MODEL), lambda i: (i, 0))
    return pl.pallas_call(
        body,
        name="gate_mix_fwd",
        grid=(T // tr,),
        in_specs=[row, pl.BlockSpec((tr, D_MODEL), lambda i: (i, 1)),
                  pl.BlockSpec((1, D_MODEL), lambda i: (0, 0)), pl.BlockSpec((1, D_MODEL), lambda i: (0, 1)),
                  row, row],
        out_specs=row,
        out_shape=jax.ShapeDtypeStruct((T, D_MODEL), BF16),
        compiler_params=_params(("parallel",)),
    )(h_e, h_e, b_gate, b_gate, y_sb, y_ret)


def _col_sum_update(acc_ref, val, first):
    part = jnp.sum(val.reshape(val.shape[0] // 8, 8, val.shape[1]), axis=0)

    @pl.when(first)
    def _():
        acc_ref[...] = part

    @pl.when(jnp.logical_not(first))
    def _():
        acc_ref[...] += part


def _gate_mix_bwd(d_mix, h_e, b_gate, y_sb, y_ret):
    T = h_e.shape[0]
    tr = _row_tile(T)
    steps = T // tr

    def body(dm_ref, g_ref, b_ref, ys_ref, yr_ref, dys_ref, dyr_ref, de_ref, db_ref, acc_ref):
        i = pl.program_id(0)
        dm = dm_ref[...]
        gates = jax.nn.sigmoid(g_ref[...] + b_ref[...])
        g0, g1 = gates[:, :D_MODEL], gates[:, D_MODEL:]
        dys_ref[...] = (dm * g0).astype(BF16)
        dyr_ref[...] = (dm * g1).astype(BF16)
        de = jnp.concatenate([dm * ys_ref[...] * g0 * (1.0 - g0), dm * yr_ref[...] * g1 * (1.0 - g1)], axis=1)
        de_ref[...] = de.astype(BF16)
        _col_sum_update(acc_ref, de, i == 0)

        @pl.when(i == steps - 1)
        def _():
            db_ref[...] = jnp.sum(acc_ref[...], axis=0, keepdims=True)

    row = pl.BlockSpec((tr, D_MODEL), lambda i: (i, 0))
    wide = pl.BlockSpec((tr, 2 * D_MODEL), lambda i: (i, 0))
    vec = pl.BlockSpec((1, 2 * D_MODEL), lambda i: (0, 0))
    return pl.pallas_call(
        body,
        name="gate_mix_bwd",
        grid=(steps,),
        in_specs=[row, wide, vec, row, row],
        out_specs=[row, row, wide, vec],
        out_shape=[jax.ShapeDtypeStruct((T, D_MODEL), BF16), jax.ShapeDtypeStruct((T, D_MODEL), BF16),
                   jax.ShapeDtypeStruct((T, 2 * D_MODEL), BF16), jax.ShapeDtypeStruct((1, 2 * D_MODEL), F32)],
        scratch_shapes=[pltpu.VMEM((8, 2 * D_MODEL), F32)],
        compiler_params=_params(("arbitrary",)),
    )(d_mix, h_e, b_gate, y_sb, y_ret)


def _ln_stats(u):
    mu = jnp.mean(u, axis=1, keepdims=True)
    cen = u - mu
    var = jnp.mean(cen * cen, axis=1, keepdims=True)
    rstd = lax.rsqrt(var + LN_EPS)
    return cen * rstd, rstd


def _ln_input_grad(d_out, gain, xhat, rstd):
    d_hat = d_out * gain
    return rstd * (d_hat - jnp.mean(d_hat, axis=1, keepdims=True)
                   - xhat * jnp.mean(d_hat * xhat, axis=1, keepdims=True))


def _ln_fwd(x_prev, sub, gain, bias, name):
    T = x_prev.shape[0]
    tr = _row_tile(T)

    def body(x_ref, s_ref, g_ref, b_ref, o_ref, xh_ref, rs_ref):
        xhat, rstd = _ln_stats(DN_ALPHA * x_ref[...] + s_ref[...])
        o_ref[...] = xhat * g_ref[...] + b_ref[...]
        xh_ref[...] = xhat
        rs_ref[...] = rstd

    row = pl.BlockSpec((tr, D_MODEL), lambda i: (i, 0))
    vec = pl.BlockSpec((1, D_MODEL), lambda i: (0, 0))
    return pl.pallas_call(
        body,
        name=name,
        grid=(T // tr,),
        in_specs=[row, row, vec, vec],
        out_specs=[row, row, pl.BlockSpec((tr, 1), lambda i: (i, 0))],
        out_shape=[jax.ShapeDtypeStruct((T, D_MODEL), F32), jax.ShapeDtypeStruct((T, D_MODEL), F32),
                   jax.ShapeDtypeStruct((T, 1), F32)],
        compiler_params=_params(("parallel",)),
    )(x_prev, sub, gain, bias)


def _ln_bwd(d_out, xhat, rstd, gain, name):
    T = d_out.shape[0]
    tr = _row_tile(T)
    steps = T // tr

    def body(d_ref, xh_ref, rs_ref, g_ref, du_ref, dg_ref, db_ref, accg_ref, accb_ref):
        i = pl.program_id(0)
        d_o, xh = d_ref[...], xh_ref[...]
        du_ref[...] = _ln_input_grad(d_o, g_ref[...], xh, rs_ref[...])
        _col_sum_update(accg_ref, d_o * xh, i == 0)
        _col_sum_update(accb_ref, d_o, i == 0)

        @pl.when(i == steps - 1)
        def _():
            dg_ref[...] = jnp.sum(accg_ref[...], axis=0, keepdims=True)
            db_ref[...] = jnp.sum(accb_ref[...], axis=0, keepdims=True)

    row = pl.BlockSpec((tr, D_MODEL), lambda i: (i, 0))
    vec = pl.BlockSpec((1, D_MODEL), lambda i: (0, 0))
    return pl.pallas_call(
        body,
        name=name,
        grid=(steps,),
        in_specs=[row, row, pl.BlockSpec((tr, 1), lambda i: (i, 0)), vec],
        out_specs=[row, vec, vec],
        out_shape=[jax.ShapeDtypeStruct((T, D_MODEL), F32), jax.ShapeDtypeStruct((1, D_MODEL), F32),
                   jax.ShapeDtypeStruct((1, D_MODEL), F32)],
        scratch_shapes=[pltpu.VMEM((8, D_MODEL), F32), pltpu.VMEM((8, D_MODEL), F32)],
        compiler_params=_params(("arbitrary",)),
    )(d_out, xhat, rstd, gain)


def _ln_loss(x_prev, sub, gain, bias, target):
    T = x_prev.shape[0]
    tr = _row_tile(T)
    steps = T // tr

    def body(x_ref, s_ref, g_ref, b_ref, t_ref, loss_ref, du_ref, dg_ref, db_ref, accl_ref, accg_ref, accb_ref):
        i = pl.program_id(0)
        gain_v = g_ref[...]
        xhat, rstd = _ln_stats(DN_ALPHA * x_ref[...] + s_ref[...])
        diff = xhat * gain_v + b_ref[...] - t_ref[...]
        d_o = diff * (1.0 / D_MODEL)
        du_ref[...] = _ln_input_grad(d_o, gain_v, xhat, rstd)
        _col_sum_update(accl_ref, diff * diff, i == 0)
        _col_sum_update(accg_ref, d_o * xhat, i == 0)
        _col_sum_update(accb_ref, d_o, i == 0)

        @pl.when(i == steps - 1)
        def _():
            per_col = jnp.sum(accl_ref[...], axis=0, keepdims=True)
            loss_ref[...] = jnp.sum(per_col, axis=1, keepdims=True) * (0.5 / D_MODEL)
            dg_ref[...] = jnp.sum(accg_ref[...], axis=0, keepdims=True)
            db_ref[...] = jnp.sum(accb_ref[...], axis=0, keepdims=True)

    row = pl.BlockSpec((tr, D_MODEL), lambda i: (i, 0))
    vec = pl.BlockSpec((1, D_MODEL), lambda i: (0, 0))
    return pl.pallas_call(
        body,
        name="ln3_loss",
        grid=(steps,),
        in_specs=[row, row, vec, vec, row],
        out_specs=[pl.BlockSpec((1, 1), lambda i: (0, 0)), row, vec, vec],
        out_shape=[jax.ShapeDtypeStruct((1, 1), F32), jax.ShapeDtypeStruct((T, D_MODEL), F32),
                   jax.ShapeDtypeStruct((1, D_MODEL), F32), jax.ShapeDtypeStruct((1, D_MODEL), F32)],
        scratch_shapes=[pltpu.VMEM((8, D_MODEL), F32)] * 3,
        compiler_params=_params(("arbitrary",)),
    )(x_prev, sub, gain, bias, target)


def _mem_probs(q_h, k_h):
    s = _dot(q_h, k_h, _NT) * (MEM_HEAD_DIM ** -0.5)
    e = jnp.exp(s - jnp.max(s, axis=1, keepdims=True))
    return e / jnp.sum(e, axis=1, keepdims=True)


def _xattn_fwd(q, kv):
    T, mem_len = q.shape[0], kv.shape[0]
    tq = _pick(T, (512, 256, 128))

    def body(q_ref, kv_ref, o_ref):
        for h in range(MEM_HEADS):
            cols = slice(h * MEM_HEAD_DIM, (h + 1) * MEM_HEAD_DIM)
            vcols = slice(D_MODEL + h * MEM_HEAD_DIM, D_MODEL + (h + 1) * MEM_HEAD_DIM)
            p = _mem_probs(q_ref[:, cols], kv_ref[:, cols])
            o_ref[:, cols] = _dot(p.astype(BF16), kv_ref[:, vcols], _NN).astype(BF16)

    return pl.pallas_call(
        body,
        name="xattn_fwd",
        grid=(T // tq,),
        in_specs=[pl.BlockSpec((tq, D_MODEL), lambda i: (i, 0)),
                  pl.BlockSpec((mem_len, 2 * D_MODEL), lambda i: (0, 0))],
        out_specs=pl.BlockSpec((tq, D_MODEL), lambda i: (i, 0)),
        out_shape=jax.ShapeDtypeStruct((T, D_MODEL), BF16),
        compiler_params=_params(("parallel",)),
    )(q, kv)


def _xattn_bwd(q, kv, d_o):
    T, mem_len = q.shape[0], kv.shape[0]
    tq = _pick(T, (512, 256, 128))

    def body(q_ref, kv_ref, do_ref, dq_ref, dkv_ref):
        @pl.when(pl.program_id(0) == 0)
        def _():
            dkv_ref[...] = jnp.zeros_like(dkv_ref)

        for h in range(MEM_HEADS):
            cols = slice(h * MEM_HEAD_DIM, (h + 1) * MEM_HEAD_DIM)
            vcols = slice(D_MODEL + h * MEM_HEAD_DIM, D_MODEL + (h + 1) * MEM_HEAD_DIM)
            q_h, k_h, do_h = q_ref[:, cols], kv_ref[:, cols], do_ref[:, cols]
            p = _mem_probs(q_h, k_h)
            dp = _dot(do_h, kv_ref[:, vcols], _NT)
            ds = p * (dp - jnp.sum(dp * p, axis=1, keepdims=True))
            dsb = (ds * (MEM_HEAD_DIM ** -0.5)).astype(BF16)
            dq_ref[:, cols] = _dot(dsb, k_h, _NN).astype(BF16)
            dkv_ref[:, cols] += _dot(dsb, q_h, _TN)
            dkv_ref[:, vcols] += _dot(p.astype(BF16), do_h, _TN)

    row = pl.BlockSpec((tq, D_MODEL), lambda i: (i, 0))
    full = pl.BlockSpec((mem_len, 2 * D_MODEL), lambda i: (0, 0))
    return pl.pallas_call(
        body,
        name="xattn_bwd",
        grid=(T // tq,),
        in_specs=[row, full, row],
        out_specs=[row, full],
        out_shape=[jax.ShapeDtypeStruct((T, D_MODEL), BF16), jax.ShapeDtypeStruct((mem_len, 2 * D_MODEL), F32)],
        compiler_params=_params(("arbitrary",)),
    )(q, kv, d_o)


def _swiglu_fwd(f):
    T = f.shape[0]
    tr = _row_tile(T)

    def body(f_ref, o_ref):
        a, b = f_ref[:, :FFN_HIDDEN], f_ref[:, FFN_HIDDEN:]
        o_ref[...] = (a * jax.nn.sigmoid(a) * b).astype(BF16)

    return pl.pallas_call(
        body,
        name="swiglu_fwd",
        grid=(T // tr,),
        in_specs=[pl.BlockSpec((tr, 2 * FFN_HIDDEN), lambda i: (i, 0))],
        out_specs=pl.BlockSpec((tr, FFN_HIDDEN), lambda i: (i, 0)),
        out_shape=jax.ShapeDtypeStruct((T, FFN_HIDDEN), BF16),
        compiler_params=_params(("parallel",)),
    )(f)


def _swiglu_bwd(d_hidden, f):
    T = f.shape[0]
    tr = _row_tile(T)

    def body(d_ref, f_ref, o_ref):
        a, b = f_ref[:, :FFN_HIDDEN], f_ref[:, FFN_HIDDEN:]
        d_h = d_ref[...]
        sig = jax.nn.sigmoid(a)
        o_ref[:, :FFN_HIDDEN] = (d_h * b * (sig * (1.0 + a * (1.0 - sig)))).astype(BF16)
        o_ref[:, FFN_HIDDEN:] = (d_h * (a * sig)).astype(BF16)

    wide = pl.BlockSpec((tr, 2 * FFN_HIDDEN), lambda i: (i, 0))
    return pl.pallas_call(
        body,
        name="swiglu_bwd",
        grid=(T // tr,),
        in_specs=[pl.BlockSpec((tr, FFN_HIDDEN), lambda i: (i, 0)), wide],
        out_specs=wide,
        out_shape=jax.ShapeDtypeStruct((T, 2 * FFN_HIDDEN), BF16),
        compiler_params=_params(("parallel",)),
    )(d_hidden, f)


def _local_step(x, mem, w, small, target):
    T = x.shape[0]
    tables = _ret_tables(T)
    xb, memb = x.astype(BF16), mem.astype(BF16)
    w_in = w["w_in"]

    h_a = _mm(xb, w_in[:, 0:1536], mode="nn", out_dtype=BF16, name="proj_sb")
    h_b = _mm(xb, w_in[:, 1536:2560], mode="nn", out_dtype=F32, name="proj_ret_qk")
    h_c = _mm(xb, w_in[:, 2560:3584], mode="nn", out_dtype=BF16, name="proj_ret_v")
    h_d = _mm(xb, w_in[:, 3584:4608], mode="nn", out_dtype=F32, name="proj_ret_g")
    h_e = _mm(xb, w_in[:, 4608:6656], mode="nn", out_dtype=F32, name="proj_gate")
    a_sb, r_mat = _sb_fwd(h_a)
    y_gated, o_pre, states = _ret_fwd(h_b, h_c, h_d, tables)
    y_sb = _mm(a_sb, w["w_sb_o"], mode="nn", out_dtype=F32, name="sb_out")
    y_ret = _mm(y_gated, w["w_ret_o"], mode="nn", out_dtype=F32, name="ret_out")
    mix_in = _gate_mix_fwd(h_e, small["b_gate"], y_sb, y_ret)
    mix = _mm(mix_in, w["w_mix_o"], mode="nn", out_dtype=F32, name="mix_out")
    x1, xhat1, rstd1 = _ln_fwd(x, mix, small["ln1_g"], small["ln1_b"], "ln1_fwd")
    x1b = x1.astype(BF16)
    q_m = _mm(x1b, w["w_mem_q"], mode="nn", out_dtype=BF16, name="mem_q")
    kv_m = _mm(memb, w["w_mem_kv"], mode="nn", out_dtype=BF16, name="mem_kv")
    o_m = _xattn_fwd(q_m, kv_m)
    xa = _mm(o_m, w["w_mem_o"], mode="nn", out_dtype=F32, name="mem_out")
    x2, xhat2, rstd2 = _ln_fwd(x1, xa, small["ln2_g"], small["ln2_b"], "ln2_fwd")
    x2b = x2.astype(BF16)
    f = _mm(x2b, w["w_ffn_in"], mode="nn", out_dtype=F32, name="ffn_in")
    hidden = _swiglu_fwd(f)
    ff = _mm(hidden, w["w_ffn_out"], mode="nn", out_dtype=F32, name="ffn_out")
    loss, du3, d_ln3_g, d_ln3_b = _ln_loss(x2, ff, small["ln3_g"], small["ln3_b"], target)

    du3b = du3.astype(BF16)
    g_ffn_out = _mm(hidden, du3b, mode="tn", out_dtype=F32, name="g_ffn_out")
    d_hidden = _mm(du3b, w["w_ffn_out"], mode="nt", out_dtype=F32, name="d_hidden")
    d_f = _swiglu_bwd(d_hidden, f)
    g_ffn_in = _mm(x2b, d_f, mode="tn", out_dtype=F32, name="g_ffn_in")
    d_x2 = _mm(d_f, w["w_ffn_in"], mode="nt", out_dtype=F32, name="d_x2", res=du3, res_scale=DN_ALPHA)
    du2, d_ln2_g, d_ln2_b = _ln_bwd(d_x2, xhat2, rstd2, small["ln2_g"], "ln2_bwd")
    du2b = du2.astype(BF16)
    g_mem_o = _mm(o_m, du2b, mode="tn", out_dtype=F32, name="g_mem_o")
    d_om = _mm(du2b, w["w_mem_o"], mode="nt", out_dtype=BF16, name="d_om")
    d_qm, d_kvm = _xattn_bwd(q_m, kv_m, d_om)
    g_mem_q = _mm(x1b, d_qm, mode="tn", out_dtype=F32, name="g_mem_q")
    g_mem_kv = _mm(memb, d_kvm.astype(BF16), mode="tn", out_dtype=F32, name="g_mem_kv")
    d_x1 = _mm(d_qm, w["w_mem_q"], mode="nt", out_dtype=F32, name="d_x1", res=du2, res_scale=DN_ALPHA)
    du1, d_ln1_g, d_ln1_b = _ln_bwd(d_x1, xhat1, rstd1, small["ln1_g"], "ln1_bwd")
    du1b = du1.astype(BF16)
    g_mix_o = _mm(mix_in, du1b, mode="tn", out_dtype=F32, name="g_mix_o")
    d_mix_in = _mm(du1b, w["w_mix_o"], mode="nt", out_dtype=F32, name="d_mix_in")
    d_ysb, d_yret, d_e, d_b_gate = _gate_mix_bwd(d_mix_in, h_e, small["b_gate"], y_sb, y_ret)
    g_sb_o = _mm(a_sb, d_ysb, mode="tn", out_dtype=F32, name="g_sb_o")
    g_ret_o = _mm(y_gated, d_yret, mode="tn", out_dtype=F32, name="g_ret_o")
    d_asb = _mm(d_ysb, w["w_sb_o"], mode="nt", out_dtype=BF16, name="d_asb")
    d_ygated = _mm(d_yret, w["w_ret_o"], mode="nt", out_dtype=F32, name="d_ygated")
    d_rq, d_rk, d_c, d_d = _ret_bwd(d_ygated, o_pre, states, h_b, h_c, h_d, tables)
    d_q, d_k, d_v = _sb_bwd(h_a, d_asb, r_mat)
    d_h = jnp.concatenate([d_q, d_k.astype(BF16), d_v.astype(BF16), d_rq, d_rk, d_c, d_d, d_e], axis=1)
    g_in = _mm(xb, d_h, mode="tn", out_dtype=F32, name="g_in")
    d_x = _mm(d_h, w_in, mode="nt", out_dtype=F32, name="d_x", res=du1, res_scale=DN_ALPHA)

    grads = {"w_in": g_in, "w_sb_o": g_sb_o, "w_ret_o": g_ret_o, "w_mix_o": g_mix_o, "w_mem_q": g_mem_q,
             "w_mem_kv": g_mem_kv, "w_mem_o": g_mem_o, "w_ffn_in": g_ffn_in, "w_ffn_out": g_ffn_out}
    small_grads = {"b_gate": d_b_gate, "ln1_g": d_ln1_g, "ln1_b": d_ln1_b, "ln2_g": d_ln2_g, "ln2_b": d_ln2_b,
                   "ln3_g": d_ln3_g, "ln3_b": d_ln3_b}
    return loss, d_x, grads, small_grads


def _my_index():
    return 4 * lax.axis_index("x") + 2 * lax.axis_index("y") + lax.axis_index("c")


def _peer(k):
    x, y, c = lax.axis_index("x"), lax.axis_index("y"), lax.axis_index("c")
    bx, by, bc = (k >> 2) & 1, (k >> 1) & 1, k & 1
    px = (1 - x) if bx else x
    py = (1 - y) if by else y
    pc = (1 - c) if bc else c
    return (px, py, pc), 4 * px + 2 * py + pc


def _exchange(bufs, scatter, name):
    n = len(bufs)

    def body(*refs):
        src, dst = refs[:n], refs[n:2 * n]
        send_sems, recv_sems, local_sems = refs[2 * n:]
        me = _my_index()
        copies = []
        for a in range(n):
            own = src[a].at[me] if scatter else src[a]
            local = pltpu.make_async_copy(own, dst[a].at[me], local_sems.at[a])
            local.start()
            copies.append(local)
            for k in range(1, N_DEV):
                peer, peer_idx = _peer(k)
                rc = pltpu.make_async_remote_copy(
                    src_ref=src[a].at[peer_idx] if scatter else src[a],
                    dst_ref=dst[a].at[me],
                    send_sem=send_sems.at[a, k - 1],
                    recv_sem=recv_sems.at[a, k - 1],
                    device_id=peer,
                    device_id_type=pl.DeviceIdType.MESH,
                )
                rc.start()
                copies.append(rc)
        for cp in copies:
            cp.wait()

    out_shapes = [jax.ShapeDtypeStruct(b.shape if scatter else (N_DEV,) + b.shape, b.dtype) for b in bufs]
    any_spec = pl.BlockSpec(memory_space=pl.ANY)
    return pl.pallas_call(
        body,
        name=name,
        in_specs=[any_spec] * n,
        out_specs=[any_spec] * n,
        out_shape=out_shapes,
        scratch_shapes=[pltpu.SemaphoreType.DMA((n, N_DEV - 1)), pltpu.SemaphoreType.DMA((n, N_DEV - 1)),
                        pltpu.SemaphoreType.DMA((n,))],
    )(*bufs)


def _adamw_math(w, g, m, v):
    m = ADAM_B1 * m + (1.0 - ADAM_B1) * g
    v = ADAM_B2 * v + (1.0 - ADAM_B2) * jnp.square(g)
    m_hat = m / (1.0 - ADAM_B1 ** ADAM_STEP)
    v_hat = v / (1.0 - ADAM_B2 ** ADAM_STEP)
    delta = -ADAM_LR * (m_hat / (jnp.sqrt(v_hat) + ADAM_EPS) + ADAM_WD * w)
    return delta, m, v


def _adamw(parts, w, m, v, name):
    R, C = w.shape
    tr = max(t for t in range(16, min(R, 256) + 1, 16) if R % t == 0) if R >= 16 else R

    def body(p_ref, w_ref, m_ref, v_ref, g_ref, d_ref, nm_ref, nv_ref):
        g = p_ref[0].astype(F32)
        for j in range(1, N_DEV):
            g = g + p_ref[j].astype(F32)
        delta, nm, nv = _adamw_math(w_ref[...], g, m_ref[...], v_ref[...])
        g_ref[...] = g
        d_ref[...] = delta
        nm_ref[...] = nm
        nv_ref[...] = nv

    blk = pl.BlockSpec((tr, C), lambda i: (i, 0))
    out = jax.ShapeDtypeStruct((R, C), F32)
    return pl.pallas_call(
        body,
        name=name,
        grid=(R // tr,),
        in_specs=[pl.BlockSpec((N_DEV, tr, C), lambda i: (0, i, 0)), blk, blk, blk],
        out_specs=[blk] * 4,
        out_shape=[out] * 4,
        compiler_params=_params(("parallel",)),
    )(parts, w, m, v)


_SHARD_AXIS = {"w_in": 1, "w_sb_o": 1, "w_ret_o": 0, "w_mix_o": 0, "w_mem_q": 0, "w_mem_kv": 1, "w_mem_o": 0,
               "w_ffn_in": 1, "w_ffn_out": 0}
_MATRICES = tuple(_SHARD_AXIS)
_SMALL = ("b_gate", "ln1_g", "ln1_b", "ln2_g", "ln2_b", "ln3_g", "ln3_b")
_WEIGHT_ORDER = ("w_in", "b_gate", "w_sb_o", "w_ret_o", "w_mix_o", "ln1_g", "ln1_b", "w_mem_q", "w_mem_kv", "w_mem_o",
                 "ln2_g", "ln2_b", "w_ffn_in", "w_ffn_out", "ln3_g", "ln3_b")


def _assemble(name, gathered):
    if _SHARD_AXIS[name] == 0:
        return gathered.reshape(-1, gathered.shape[2])
    return jnp.transpose(gathered, (1, 0, 2)).reshape(gathered.shape[1], -1)


def _to_slots(name, full):
    if _SHARD_AXIS[name] == 0:
        return full.reshape(N_DEV, full.shape[0] // N_DEV, full.shape[1])
    return jnp.transpose(full.reshape(full.shape[0], N_DEV, full.shape[1] // N_DEV), (1, 0, 2))


def _pack_small(vals):
    return jnp.concatenate([vals["b_gate"].reshape(2, D_MODEL)] + [vals[n] for n in _SMALL[1:]], axis=0)


def _unpack_small(packed):
    out = {"b_gate": packed[0:2].reshape(1, 2 * D_MODEL)}
    for i, n in enumerate(_SMALL[1:]):
        out[n] = packed[2 + i:3 + i]
    return out


def kernel(x, mem, w_in, b_gate, w_sb_o, w_ret_o, w_mix_o, ln1_g, ln1_b, w_mem_q, w_mem_kv, w_mem_o, ln2_g, ln2_b, w_ffn_in, w_ffn_out, ln3_g, ln3_b, loss_target, m_w_in, m_b_gate, m_w_sb_o, m_w_ret_o, m_w_mix_o, m_ln1_g, m_ln1_b, m_w_mem_q, m_w_mem_kv, m_w_mem_o, m_ln2_g, m_ln2_b, m_w_ffn_in, m_w_ffn_out, m_ln3_g, m_ln3_b, v_w_in, v_b_gate, v_w_sb_o, v_w_ret_o, v_w_mix_o, v_ln1_g, v_ln1_b, v_w_mem_q, v_w_mem_kv, v_w_mem_o, v_ln2_g, v_ln2_b, v_w_ffn_in, v_w_ffn_out, v_ln3_g, v_ln3_b):
    weights = dict(w_in=w_in, b_gate=b_gate, w_sb_o=w_sb_o, w_ret_o=w_ret_o, w_mix_o=w_mix_o, ln1_g=ln1_g, ln1_b=ln1_b,
                   w_mem_q=w_mem_q, w_mem_kv=w_mem_kv, w_mem_o=w_mem_o, ln2_g=ln2_g, ln2_b=ln2_b, w_ffn_in=w_ffn_in,
                   w_ffn_out=w_ffn_out, ln3_g=ln3_g, ln3_b=ln3_b)
    mom1 = dict(w_in=m_w_in, b_gate=m_b_gate, w_sb_o=m_w_sb_o, w_ret_o=m_w_ret_o, w_mix_o=m_w_mix_o, ln1_g=m_ln1_g,
                ln1_b=m_ln1_b, w_mem_q=m_w_mem_q, w_mem_kv=m_w_mem_kv, w_mem_o=m_w_mem_o, ln2_g=m_ln2_g, ln2_b=m_ln2_b,
                w_ffn_in=m_w_ffn_in, w_ffn_out=m_w_ffn_out, ln3_g=m_ln3_g, ln3_b=m_ln3_b)
    mom2 = dict(w_in=v_w_in, b_gate=v_b_gate, w_sb_o=v_w_sb_o, w_ret_o=v_w_ret_o, w_mix_o=v_w_mix_o, ln1_g=v_ln1_g,
                ln1_b=v_ln1_b, w_mem_q=v_w_mem_q, w_mem_kv=v_w_mem_kv, w_mem_o=v_w_mem_o, ln2_g=v_ln2_g, ln2_b=v_ln2_b,
                w_ffn_in=v_w_ffn_in, w_ffn_out=v_w_ffn_out, ln3_g=v_ln3_g, ln3_b=v_ln3_b)

    shards = [weights[n][0].astype(BF16) for n in _MATRICES]
    gathered = _exchange(shards, False, "gather_weights")
    full = {n: _assemble(n, g) for n, g in zip(_MATRICES, gathered)}
    small = {n: weights[n] for n in _SMALL}

    loss, d_x, grads, small_grads = _local_step(x[0], mem[0], full, small, loss_target[0])

    slots = [_to_slots(n, grads[n]).astype(BF16) for n in _MATRICES]
    small_part = _pack_small(small_grads)
    received = _exchange(slots + [jnp.broadcast_to(small_part[None], (N_DEV,) + small_part.shape)], True, "scatter_grads")

    new = {}
    for n, parts in zip(_MATRICES, received[:-1]):
        new[n] = _adamw(parts, weights[n][0], mom1[n][0], mom2[n][0], "adamw_" + n)
    packed = _adamw(received[-1], _pack_small({n: weights[n] for n in _SMALL}), _pack_small({n: mom1[n] for n in _SMALL}),
                    _pack_small({n: mom2[n] for n in _SMALL}), "adamw_small")
    small_new = [_unpack_small(p) for p in packed]

    outs = [lax.psum(loss[0, 0], MESH_AXES), d_x[None]]
    for slot in range(4):
        for n in _WEIGHT_ORDER:
            outs.append(new[n][slot][None] if n in new else small_new[slot][n])
    return tuple(outs)
```

```python
import functools
import math

import jax
import jax.numpy as jnp
from jax import lax
from jax.experimental import pallas as pl
from jax.experimental.pallas import tpu as pltpu

F32 = jnp.float32
BF16 = jnp.bfloat16

N_DEV = 8
D_MODEL = 1024
SB_HEAD_DIM = 64
SB_WIDTH = 512
RET_HEADS = 4
RET_QK_DIM = 128
RET_V_DIM = 256
RET_QK_WIDTH = 512
RET_V_WIDTH = 1024
RET_CHUNK = 128
ROPE_BASE = 10000.0
MEM_HEADS = 4
MEM_HEAD_DIM = 256
FFN_HIDDEN = 2816
DN_ALPHA = 2.0 ** 0.25
LN_EPS = 1e-5
ADAM_LR = 0.001
ADAM_B1 = 0.9
ADAM_B2 = 0.999
ADAM_EPS = 1e-08
ADAM_WD = 0.01
ADAM_STEP = 10

VMEM_LIMIT_BYTES = 52 * 1024 * 1024
LANES = 128
SB_KEY_BLOCK = 128
SB_Q_BLOCK = 256
SB_DEAD_LOG = -105.0

MESH_AXES = ("x", "y", "c")


def _pick(dim, prefs):
    for p in prefs:
        if dim % p == 0:
            return p
    return dim


def _params(sem):
    return pltpu.CompilerParams(dimension_semantics=sem, vmem_limit_bytes=VMEM_LIMIT_BYTES)


def _dot(a, b, dims):
    return lax.dot_general(a, b, (dims, ((), ())), preferred_element_type=F32)


_NN = ((1,), (0,))
_NT = ((1,), (1,))
_TN = ((0,), (0,))


def _my_index():
    return 4 * lax.axis_index("x") + 2 * lax.axis_index("y") + lax.axis_index("c")


def _peer(k):
    x, y, c = lax.axis_index("x"), lax.axis_index("y"), lax.axis_index("c")
    bx, by, bc = (k >> 2) & 1, (k >> 1) & 1, k & 1
    px = (1 - x) if bx else x
    py = (1 - y) if by else y
    pc = (1 - c) if bc else c
    return (px, py, pc), 4 * px + 2 * py + pc


class _Rider:
    def __init__(self, bufs, scatter):
        self.bufs, self.scatter, self.n = list(bufs), scatter, len(bufs)
        self.specs = [pl.BlockSpec(memory_space=pl.ANY)] * self.n
        self.out_shape = [jax.ShapeDtypeStruct(b.shape if scatter else (N_DEV,) + b.shape, b.dtype) for b in self.bufs]
        self.scratch = [pltpu.SemaphoreType.DMA((self.n, N_DEV - 1)), pltpu.SemaphoreType.DMA((self.n, N_DEV - 1)),
                        pltpu.SemaphoreType.DMA((self.n,))]

    def _copies(self, ride):
        src, dst, (send_sems, recv_sems, local_sems) = ride
        me = _my_index()
        copies = []
        for a in range(self.n):
            own = src[a].at[me] if self.scatter else src[a]
            copies.append(pltpu.make_async_copy(own, dst[a].at[me], local_sems.at[a]))
            for k in range(1, N_DEV):
                peer, peer_idx = _peer(k)
                copies.append(pltpu.make_async_remote_copy(
                    src_ref=src[a].at[peer_idx] if self.scatter else src[a],
                    dst_ref=dst[a].at[me],
                    send_sem=send_sems.at[a, k - 1],
                    recv_sem=recv_sems.at[a, k - 1],
                    device_id=peer,
                    device_id_type=pl.DeviceIdType.MESH,
                ))
        return copies

    def start_at_first(self, ids, ride):
        first = functools.reduce(jnp.logical_and, [i == 0 for i in ids])

        @pl.when(first)
        def _():
            for cp in self._copies(ride):
                cp.start()

    def wait_at_last(self, ids, grid, ride):
        last = functools.reduce(jnp.logical_and, [i == g - 1 for i, g in zip(ids, grid)])

        @pl.when(last)
        def _():
            for cp in self._copies(ride):
                cp.wait()


def _exchange(bufs, scatter, name):
    rider = _Rider(bufs, scatter)

    def body(*refs):
        ride = (refs[:rider.n], refs[rider.n:2 * rider.n], refs[2 * rider.n:])
        for cp in rider._copies(ride):
            cp.start()
        for cp in rider._copies(ride):
            cp.wait()

    return pl.pallas_call(
        body,
        name=name,
        in_specs=rider.specs,
        out_specs=rider.specs,
        out_shape=rider.out_shape,
        scratch_shapes=rider.scratch,
    )(*rider.bufs)


MM_RESIDENT_B_BYTES = 14 * 1024 * 1024
MM_A_TILE_BYTES = 4 * 1024 * 1024
MM_OUT_TILE_BYTES = 6 * 1024 * 1024


def _mm_tiles(mode, M, N, K, a_bytes, out_bytes):
    if mode != "tn" and K * N * 2 <= MM_RESIDENT_B_BYTES:
        for tm in (1024, 512, 256, 128):
            if M % tm == 0 and tm * K * a_bytes <= MM_A_TILE_BYTES and tm * N * out_bytes <= MM_OUT_TILE_BYTES:
                return tm, N, K
    if mode == "tn":
        return (_pick(M, (1024, 1408, 512, 256, 128)), _pick(N, (1024, 1664, 1408, 512, 256, 128)),
                _pick(K, (2048, 1024, 512, 256, 128)))
    return _pick(M, (1024, 512, 256, 128)), _pick(N, (512, 256, 128)), _pick(K, (1024, 512, 256, 128))


def _mm(a, b, *, mode, out_dtype, name, res=None, res_scale=1.0, rider=None):
    if mode == "nn":
        (M, K), (K2, N) = a.shape, b.shape
    elif mode == "nt":
        (M, K), (N, K2) = a.shape, b.shape
    else:
        (K, M), (K2, N) = a.shape, b.shape
    assert K == K2, (a.shape, b.shape, mode)
    out_bytes = jnp.dtype(out_dtype).itemsize + (4 if res is not None else 0)
    tm, tn, tk = _mm_tiles(mode, M, N, K, a.dtype.itemsize, out_bytes)
    grid = (M // tm, N // tn, K // tk)
    nk = grid[2]
    dims = {"nn": _NN, "nt": _NT, "tn": _TN}[mode]
    n_in = 2 + (res is not None)
    n_ride = rider.n if rider is not None else 0

    def body(*refs):
        a_ref, b_ref = refs[:2]
        r_ref = refs[2] if res is not None else None
        o_ref = refs[n_in + n_ride]
        rest = refs[n_in + 2 * n_ride + 1:]
        acc_ref = rest[0] if nk > 1 else None
        ids = [pl.program_id(d) for d in range(3)]
        if rider is not None:
            ride = (refs[n_in:n_in + n_ride], refs[n_in + n_ride + 1:n_in + 2 * n_ride + 1], rest[-3:])
            rider.start_at_first(ids, ride)
        part = _dot(a_ref[...].astype(BF16), b_ref[...].astype(BF16), dims)

        def finish(total):
            if r_ref is not None:
                total = total + res_scale * r_ref[...]
            o_ref[...] = total.astype(out_dtype)

        if nk == 1:
            finish(part)
        else:
            k = ids[2]

            @pl.when(k == 0)
            def _():
                acc_ref[...] = part

            @pl.when(k > 0)
            def _():
                acc_ref[...] += part

            @pl.when(k == nk - 1)
            def _():
                finish(acc_ref[...])

        if rider is not None:
            rider.wait_at_last(ids, grid, ride)

    if mode == "nn":
        a_spec = pl.BlockSpec((tm, tk), lambda i, j, k: (i, k))
        b_spec = pl.BlockSpec((tk, tn), lambda i, j, k: (k, j))
    elif mode == "nt":
        a_spec = pl.BlockSpec((tm, tk), lambda i, j, k: (i, k))
        b_spec = pl.BlockSpec((tn, tk), lambda i, j, k: (j, k))
    else:
        a_spec = pl.BlockSpec((tk, tm), lambda i, j, k: (k, i))
        b_spec = pl.BlockSpec((tk, tn), lambda i, j, k: (k, j))
    o_spec = pl.BlockSpec((tm, tn), lambda i, j, k: (i, j))
    in_specs = [a_spec, b_spec] + ([o_spec] if res is not None else [])
    args = (a, b) + ((res,) if res is not None else ())
    out_specs, out_shape = [o_spec], [jax.ShapeDtypeStruct((M, N), out_dtype)]
    scratch = [pltpu.VMEM((tm, tn), F32)] if nk > 1 else []
    sem = ("parallel", "parallel", "arbitrary")
    if rider is not None:
        in_specs, args = in_specs + rider.specs, args + tuple(rider.bufs)
        out_specs, out_shape = out_specs + rider.specs, out_shape + rider.out_shape
        scratch = scratch + rider.scratch
        sem = ("arbitrary",) * 3
    outs = pl.pallas_call(
        body,
        name=name,
        grid=grid,
        in_specs=in_specs,
        out_specs=out_specs,
        out_shape=out_shape,
        scratch_shapes=scratch,
        compiler_params=_params(sem),
    )(*args)
    return outs[0] if rider is None else (outs[0], list(outs[1:]))


def _mm_host(a, b, *, rider, **kw):
    out = _mm(a, b, rider=rider, **kw)
    return out if rider is not None else (out, [])


def _pair_rows(blk, lane_is_a):
    zero = jnp.zeros_like(blk)
    return jnp.concatenate([jnp.where(lane_is_a, blk, zero), jnp.where(lane_is_a, zero, blk)], axis=0)


def _pair_scan_matrix(strict_after):
    r = lax.broadcasted_iota(jnp.int32, (4 * LANES, 2 * LANES), 0) & (2 * LANES - 1)
    c = lax.broadcasted_iota(jnp.int32, (4 * LANES, 2 * LANES), 1)
    same = (r >= LANES) == (c >= LANES)
    rr, cc = r & (LANES - 1), c & (LANES - 1)
    tri = (rr > cc) if strict_after else (rr < cc)
    return jnp.where(same & tri, 1.0, 0.0).astype(BF16)


def _split_dot(val, mat):
    hi = val.astype(BF16)
    lo = (val - hi.astype(F32)).astype(BF16)
    return _dot(jnp.concatenate([hi, lo], axis=1), mat, _NN)


def _pair_cols(col_a, col_b, rows):
    return jnp.concatenate([jnp.broadcast_to(col_a, (rows, LANES)), jnp.broadcast_to(col_b, (rows, LANES))], axis=1)


def _sb_scores(q, kk, mask):
    z = _dot(q, kk, _NT) * (SB_HEAD_DIM ** -0.5)
    t = jnp.log1p(jnp.exp(-jnp.abs(z)))
    log_beta = jnp.minimum(z, 0.0) - t
    log_rem = -jnp.maximum(z, 0.0) - t
    if mask is not None:
        log_rem = jnp.where(mask, log_rem, 0.0)
    return log_beta, log_rem


def _sb_fwd(h_a, rider=None):
    T = h_a.shape[0]
    tq = _pick(T, (SB_Q_BLOCK, SB_KEY_BLOCK))
    nq, per_q = T // tq, tq // SB_KEY_BLOCK
    assert T // SB_KEY_BLOCK <= LANES
    n_ride = rider.n if rider is not None else 0

    def body(*refs):
        q_ref, k_ref, v_ref = refs[:3]
        a_ref, r_ref = refs[3 + n_ride:5 + n_ride]
        ids = [pl.program_id(0), pl.program_id(1)]
        if rider is not None:
            ride = (refs[3:3 + n_ride], refs[5 + n_ride:5 + 2 * n_ride], refs[-3:])
            rider.start_at_first(ids, ride)
        i = ids[1]
        q = q_ref[...]
        lane_is_a = lax.broadcasted_iota(jnp.int32, (SB_KEY_BLOCK, LANES), 1) < SB_HEAD_DIM
        after = _pair_scan_matrix(True)
        row = lax.broadcasted_iota(jnp.int32, (tq, 2 * LANES), 0)
        col = lax.broadcasted_iota(jnp.int32, (tq, 2 * LANES), 1) & (LANES - 1)
        blk_lane = lax.broadcasted_iota(jnp.int32, (tq, LANES), 1)

        def tile(kb, masked, carry):
            acc, ra, rb, rma, rmb = carry
            ks = pl.multiple_of(kb * SB_KEY_BLOCK, SB_KEY_BLOCK)
            kk = _pair_rows(k_ref[pl.ds(ks, SB_KEY_BLOCK), :], lane_is_a)
            vv = _pair_rows(v_ref[pl.ds(ks, SB_KEY_BLOCK), :], lane_is_a)
            mask = ((ks + col) < (i * tq + row)) if masked else None
            log_beta, log_rem = _sb_scores(q, kk, mask)
            later = _split_dot(log_rem, after) + _pair_cols(ra, rb, tq)
            w = jnp.exp(log_beta + later)
            if masked:
                w = jnp.where(mask, w, 0.0)
            acc = acc + _dot(w.astype(BF16), vv, _NN)
            here = (blk_lane == kb).astype(F32)
            rma = rma + ra * here
            rmb = rmb + rb * here
            ra = ra + jnp.sum(log_rem[:, :LANES], axis=1, keepdims=True)
            rb = rb + jnp.sum(log_rem[:, LANES:], axis=1, keepdims=True)
            return acc, ra, rb, rma, rmb

        carry = (jnp.zeros((tq, LANES), F32), jnp.zeros((tq, 1), F32), jnp.zeros((tq, 1), F32),
                 jnp.zeros((tq, LANES), F32), jnp.zeros((tq, LANES), F32))
        for d in range(per_q):
            carry = tile(i * per_q + (per_q - 1 - d), True, carry)
        n_full = i * per_q

        def alive(c):
            return jnp.logical_and(c[0] < n_full, jnp.max(jnp.maximum(c[2], c[3])) > SB_DEAD_LOG)

        def step(c):
            return (c[0] + 1,) + tile(n_full - 1 - c[0], False, c[1:])

        done, acc, ra, rb, rma, rmb = lax.while_loop(alive, step, (jnp.int32(0),) + carry)
        skipped = blk_lane < (n_full - done)
        rma = jnp.where(skipped, ra, rma)
        rmb = jnp.where(skipped, rb, rmb)
        a_ref[...] = acc.astype(BF16)
        r_ref[0] = rma
        r_ref[1] = rmb
        if rider is not None:
            rider.wait_at_last(ids, (4, nq), ride)

    in_specs = [pl.BlockSpec((tq, LANES), lambda p, i: (i, p)),
                pl.BlockSpec((T, LANES), lambda p, i: (0, 4 + p)),
                pl.BlockSpec((T, LANES), lambda p, i: (0, 8 + p))]
    out_specs = [pl.BlockSpec((tq, LANES), lambda p, i: (i, p)),
                 pl.BlockSpec((2, tq, LANES), lambda p, i: (p, i, 0))]
    out_shape = [jax.ShapeDtypeStruct((T, SB_WIDTH), BF16), jax.ShapeDtypeStruct((8, T, LANES), F32)]
    args = (h_a, h_a, h_a)
    if rider is not None:
        in_specs, args = in_specs + rider.specs, args + tuple(rider.bufs)
        out_specs, out_shape = out_specs + rider.specs, out_shape + rider.out_shape
    outs = pl.pallas_call(
        body,
        name="sb_fwd",
        grid=(4, nq),
        in_specs=in_specs,
        out_specs=out_specs,
        out_shape=out_shape,
        scratch_shapes=rider.scratch if rider is not None else [],
        compiler_params=_params(("arbitrary", "arbitrary") if rider is not None else ("parallel", "arbitrary")),
    )(*args)
    return outs[0], outs[1], list(outs[2:])


def _sb_bwd(h_a, d_out, r_mat):
    T = h_a.shape[0]
    tq = _pick(T, (SB_Q_BLOCK, SB_KEY_BLOCK))
    nq, per_q = T // tq, tq // SB_KEY_BLOCK

    def body(q_ref, k_ref, v_ref, do_ref, r_ref, dq_ref, dk_ref, dv_ref):
        i = pl.program_id(1)

        @pl.when(i == 0)
        def _():
            dk_ref[...] = jnp.zeros_like(dk_ref)
            dv_ref[...] = jnp.zeros_like(dv_ref)

        q = q_ref[...]
        d_o = do_ref[...]
        rma, rmb = r_ref[0], r_ref[1]
        lane_is_a = lax.broadcasted_iota(jnp.int32, (SB_KEY_BLOCK, LANES), 1) < SB_HEAD_DIM
        after = _pair_scan_matrix(True)
        before = _pair_scan_matrix(False)
        row = lax.broadcasted_iota(jnp.int32, (tq, 2 * LANES), 0)
        col = lax.broadcasted_iota(jnp.int32, (tq, 2 * LANES), 1) & (LANES - 1)
        blk_lane = lax.broadcasted_iota(jnp.int32, (tq, LANES), 1)

        def tile(kb, masked, carry):
            dq, ca, cb = carry
            ks = pl.multiple_of(kb * SB_KEY_BLOCK, SB_KEY_BLOCK)
            kk = _pair_rows(k_ref[pl.ds(ks, SB_KEY_BLOCK), :], lane_is_a)
            vv = _pair_rows(v_ref[pl.ds(ks, SB_KEY_BLOCK), :], lane_is_a)
            mask = ((ks + col) < (i * tq + row)) if masked else None
            log_beta, log_rem = _sb_scores(q, kk, mask)
            here = (blk_lane == kb).astype(F32)
            ra = jnp.sum(rma * here, axis=1, keepdims=True)
            rb = jnp.sum(rmb * here, axis=1, keepdims=True)
            later = _split_dot(log_rem, after) + _pair_cols(ra, rb, tq)
            w = jnp.exp(log_beta + later)
            if masked:
                w = jnp.where(mask, w, 0.0)
            da = _dot(d_o, vv, _NT) * w
            prefix = _split_dot(da, before) + _pair_cols(ca, cb, tq)
            sig = jnp.exp(log_beta)
            dz = da * (1.0 - sig) - prefix * sig
            if masked:
                dz = jnp.where(mask, dz, 0.0)
            dzb = (dz * (SB_HEAD_DIM ** -0.5)).astype(BF16)
            dq = dq + _dot(dzb, kk, _NN)
            dkk = _dot(dzb, q, _TN)
            dvv = _dot(w.astype(BF16), d_o, _TN)
            dk_ref[pl.ds(ks, SB_KEY_BLOCK), :] += jnp.where(lane_is_a, dkk[:SB_KEY_BLOCK], dkk[SB_KEY_BLOCK:])
            dv_ref[pl.ds(ks, SB_KEY_BLOCK), :] += jnp.where(lane_is_a, dvv[:SB_KEY_BLOCK], dvv[SB_KEY_BLOCK:])
            ca = ca + jnp.sum(da[:, :LANES], axis=1, keepdims=True)
            cb = cb + jnp.sum(da[:, LANES:], axis=1, keepdims=True)
            return dq, ca, cb

        carry = (jnp.zeros((tq, LANES), F32), jnp.zeros((tq, 1), F32), jnp.zeros((tq, 1), F32))
        n_full = i * per_q
        col_max = jnp.max(jnp.maximum(rma, rmb), axis=0, keepdims=True)
        dead = jnp.logical_and(col_max <= SB_DEAD_LOG, blk_lane[0:1] < n_full)
        first = jnp.sum(dead.astype(F32)).astype(jnp.int32)
        carry = lax.fori_loop(first, n_full, lambda j, c: tile(j, False, c), carry)
        for d in range(per_q):
            carry = tile(i * per_q + d, True, carry)
        dq_ref[...] = carry[0].astype(BF16)

    return pl.pallas_call(
        body,
        name="sb_bwd",
        grid=(4, nq),
        in_specs=[pl.BlockSpec((tq, LANES), lambda p, i: (i, p)),
                  pl.BlockSpec((T, LANES), lambda p, i: (0, 4 + p)),
                  pl.BlockSpec((T, LANES), lambda p, i: (0, 8 + p)),
                  pl.BlockSpec((tq, LANES), lambda p, i: (i, p)),
                  pl.BlockSpec((2, tq, LANES), lambda p, i: (p, i, 0))],
        out_specs=[pl.BlockSpec((tq, LANES), lambda p, i: (i, p)),
                   pl.BlockSpec((T, LANES), lambda p, i: (0, p)),
                   pl.BlockSpec((T, LANES), lambda p, i: (0, p))],
        out_shape=[jax.ShapeDtypeStruct((T, SB_WIDTH), BF16),
                   jax.ShapeDtypeStruct((T, SB_WIDTH), F32),
                   jax.ShapeDtypeStruct((T, SB_WIDTH), F32)],
        compiler_params=_params(("parallel", "arbitrary")),
    )(h_a, h_a, h_a, d_out, r_mat)


def _ret_tables(T):
    half = RET_QK_DIM // 2
    inv = 1.0 / (ROPE_BASE ** (jnp.arange(half, dtype=F32) / half))
    ang = jnp.arange(T, dtype=F32)[:, None] * inv[None, :]
    cos, sin = jnp.cos(ang), jnp.sin(ang)
    cos_t = jnp.concatenate([cos, cos], axis=1)
    sin_t = jnp.concatenate([-sin, sin], axis=1)
    log_gamma = jnp.log1p(-jnp.exp2(-5.0 - jnp.arange(RET_HEADS, dtype=F32)))
    idx = jnp.arange(RET_CHUNK, dtype=F32)
    rel = idx[:, None] - idx[None, :]
    decay = jnp.where(rel[None] >= 0, jnp.exp(log_gamma[:, None, None] * jnp.maximum(rel, 0.0)[None]), 0.0)
    k_decay = jnp.exp(log_gamma[None, :] * (RET_CHUNK - 1.0 - idx)[:, None])
    q_decay = jnp.exp(log_gamma[None, :] * (idx + 1.0)[:, None])
    chunk_decay = jnp.exp(log_gamma * RET_CHUNK)
    k_dec = jnp.broadcast_to(k_decay.T[:, :, None], (RET_HEADS, RET_CHUNK, LANES))
    q_dec = jnp.broadcast_to(q_decay.T[:, :, None], (RET_HEADS, RET_CHUNK, LANES))
    c_dec = jnp.broadcast_to(chunk_decay[:, None, None], (RET_HEADS, 8, LANES))
    return cos_t, sin_t, decay, k_dec, q_dec, c_dec


def _rotary(x, cos_t, sin_t):
    return x * cos_t + pltpu.roll(x, RET_QK_DIM // 2, 1) * sin_t


def _rotary_transpose(dy, cos_t, sin_t):
    return dy * cos_t + pltpu.roll(dy * sin_t, RET_QK_DIM // 2, 1)


def _head_norm(o):
    mu = jnp.mean(o, axis=1, keepdims=True)
    cen = o - mu
    var = jnp.mean(cen * cen, axis=1, keepdims=True)
    rstd = lax.rsqrt(var + LN_EPS)
    return cen * rstd, rstd


def _ret_specs(nc, reverse):
    def n_of(n):
        return (nc - 1 - n) if reverse else n

    qk = pl.BlockSpec((RET_CHUNK, RET_QK_DIM), lambda h, n: (n_of(n), h))
    vv = pl.BlockSpec((RET_CHUNK, RET_V_DIM), lambda h, n: (n_of(n), h))
    pos = pl.BlockSpec((RET_CHUNK, LANES), lambda h, n: (n_of(n), 0))
    per_head = pl.BlockSpec((1, RET_CHUNK, LANES), lambda h, n: (h, 0, 0))
    c_dec = pl.BlockSpec((1, 8, LANES), lambda h, n: (h, 0, 0))
    state = pl.BlockSpec((1, 1, RET_QK_DIM, RET_V_DIM), lambda h, n: (h, n_of(n), 0, 0))
    return qk, vv, pos, per_head, c_dec, state


def _ret_fwd(h_b, h_c, h_d, tables):
    T = h_b.shape[0]
    nc = T // RET_CHUNK
    qk, vv, pos, per_head, c_dec, state = _ret_specs(nc, False)
    k_spec = pl.BlockSpec((RET_CHUNK, RET_QK_DIM), lambda h, n: (n, RET_HEADS + h))

    def body(q_ref, k_ref, v_ref, g_ref, cos_ref, sin_ref, dec_ref, kd_ref, qd_ref, cd_ref,
             y_ref, o_ref, st_ref, state_ref):
        @pl.when(pl.program_id(1) == 0)
        def _():
            state_ref[...] = jnp.zeros_like(state_ref)

        cos_t, sin_t = cos_ref[...], sin_ref[...]
        q = _rotary(q_ref[...], cos_t, sin_t) * (RET_QK_DIM ** -0.5)
        k = _rotary(k_ref[...], cos_t, sin_t)
        v = v_ref[...]
        state = state_ref[...]
        scores = _dot(q.astype(BF16), k.astype(BF16), _NT) * dec_ref[0]
        inner = _dot(scores.astype(BF16), v, _NN)
        cross = _dot((q * qd_ref[0]).astype(BF16), state.astype(BF16), _NN)
        o = inner + cross
        st_ref[0, 0] = state
        kv = _dot((k * kd_ref[0]).astype(BF16), v, _TN)
        state_ref[...] = state * cd_ref[0, 0:1, 0:1] + kv
        o_ref[...] = o
        normed, _ = _head_norm(o)
        gate = g_ref[...]
        y_ref[...] = (gate * jax.nn.sigmoid(gate) * normed).astype(BF16)

    return pl.pallas_call(
        body,
        name="ret_fwd",
        grid=(RET_HEADS, nc),
        in_specs=[qk, k_spec, vv, vv, pos, pos, per_head, per_head, per_head, c_dec],
        out_specs=[vv, vv, state],
        out_shape=[jax.ShapeDtypeStruct((T, RET_V_WIDTH), BF16),
                   jax.ShapeDtypeStruct((T, RET_V_WIDTH), F32),
                   jax.ShapeDtypeStruct((RET_HEADS, nc, RET_QK_DIM, RET_V_DIM), F32)],
        scratch_shapes=[pltpu.VMEM((RET_QK_DIM, RET_V_DIM), F32)],
        compiler_params=_params(("parallel", "arbitrary")),
    )(h_b, h_b, h_c, h_d, *tables)


def _ret_bwd(d_y, o_pre, states, h_b, h_c, h_d, tables, rider=None):
    T = h_b.shape[0]
    nc = T // RET_CHUNK
    qk, vv, pos, per_head, c_dec, state = _ret_specs(nc, True)
    k_spec = pl.BlockSpec((RET_CHUNK, RET_QK_DIM), lambda h, n: (nc - 1 - n, RET_HEADS + h))
    n_ride = rider.n if rider is not None else 0

    def body(*refs):
        (dy_ref, o_ref, st_ref, q_ref, k_ref, v_ref, g_ref, cos_ref, sin_ref, dec_ref, kd_ref, qd_ref,
         cd_ref) = refs[:13]
        dq_ref, dk_ref, dv_ref, dg_ref = refs[13 + n_ride:17 + n_ride]
        carry_ref = refs[17 + 2 * n_ride]
        ids = [pl.program_id(0), pl.program_id(1)]
        if rider is not None:
            ride = (refs[13:13 + n_ride], refs[17 + n_ride:17 + 2 * n_ride], refs[-3:])
            rider.start_at_first(ids, ride)

        @pl.when(ids[1] == 0)
        def _():
            carry_ref[...] = jnp.zeros_like(carry_ref)

        cos_t, sin_t = cos_ref[...], sin_ref[...]
        scale = RET_QK_DIM ** -0.5
        q = _rotary(q_ref[...], cos_t, sin_t) * scale
        k = _rotary(k_ref[...], cos_t, sin_t)
        v = v_ref[...]
        decay, k_dec, q_dec = dec_ref[0], kd_ref[0], qd_ref[0]
        chunk_decay = cd_ref[0, 0:1, 0:1]
        state = st_ref[0, 0].astype(BF16)
        later = carry_ref[...]
        later_b = later.astype(BF16)

        gate = g_ref[...]
        sig = jax.nn.sigmoid(gate)
        silu = gate * sig
        normed, rstd = _head_norm(o_ref[...])
        d_y = dy_ref[...]
        dg_ref[...] = (d_y * normed * (sig * (1.0 + gate * (1.0 - sig)))).astype(BF16)
        d_n = d_y * silu
        d_o = rstd * (d_n - jnp.mean(d_n, axis=1, keepdims=True)
                      - normed * jnp.mean(d_n * normed, axis=1, keepdims=True))
        d_ob = d_o.astype(BF16)

        qb, kb = q.astype(BF16), k.astype(BF16)
        qd_b, kd_b = (q * q_dec).astype(BF16), (k * k_dec).astype(BF16)
        scores = _dot(qb, kb, _NT) * decay
        d_scores = (_dot(d_ob, v, _NT) * decay).astype(BF16)
        dq = _dot(d_scores, kb, _NN) + _dot(d_ob, state, _NT) * q_dec
        dk = _dot(d_scores, qb, _TN) + _dot(v, later_b, _NT) * k_dec
        dv = _dot(scores.astype(BF16), d_ob, _TN) + _dot(kd_b, later_b, _NN)
        carry_ref[...] = _dot(qd_b, d_ob, _TN) + chunk_decay * later
        dq_ref[...] = _rotary_transpose(dq * scale, cos_t, sin_t).astype(BF16)
        dk_ref[...] = _rotary_transpose(dk, cos_t, sin_t).astype(BF16)
        dv_ref[...] = dv.astype(BF16)
        if rider is not None:
            rider.wait_at_last(ids, (RET_HEADS, nc), ride)

    in_specs = [vv, vv, state, qk, k_spec, vv, vv, pos, pos, per_head, per_head, per_head, c_dec]
    out_specs = [qk, qk, vv, vv]
    out_shape = [jax.ShapeDtypeStruct((T, RET_QK_WIDTH), BF16), jax.ShapeDtypeStruct((T, RET_QK_WIDTH), BF16),
                 jax.ShapeDtypeStruct((T, RET_V_WIDTH), BF16), jax.ShapeDtypeStruct((T, RET_V_WIDTH), BF16)]
    args = (d_y, o_pre, states, h_b, h_b, h_c, h_d) + tuple(tables)
    scratch = [pltpu.VMEM((RET_QK_DIM, RET_V_DIM), F32)]
    if rider is not None:
        in_specs, args = in_specs + rider.specs, args + tuple(rider.bufs)
        out_specs, out_shape = out_specs + rider.specs, out_shape + rider.out_shape
        scratch = scratch + rider.scratch
    outs = pl.pallas_call(
        body,
        name="ret_bwd",
        grid=(RET_HEADS, nc),
        in_specs=in_specs,
        out_specs=out_specs,
        out_shape=out_shape,
        scratch_shapes=scratch,
        compiler_params=_params(("arbitrary", "arbitrary") if rider is not None else ("parallel", "arbitrary")),
    )(*args)
    return outs[0], outs[1], outs[2], outs[3], list(outs[4:])


def _row_tile(T):
    return _pick(T, (256, 128))


def _gate_mix_fwd(h_e, b_gate, y_sb, y_ret):
    T = h_e.shape[0]
    tr = _row_tile(T)

    def body(g0_ref, g1_ref, b0_ref, b1_ref, ys_ref, yr_ref, o_ref):
        g0 = jax.nn.sigmoid(g0_ref[...] + b0_ref[...])
        g1 = jax.nn.sigmoid(g1_ref[...] + b1_ref[...])
        o_ref[...] = (g0 * ys_ref[...] + g1 * yr_ref[...]).astype(BF16)

    row = pl.BlockSpec((tr, D_MODEL), lambda i: (i, 0))
    return pl.pallas_call(
        body,
        name="gate_mix_fwd",
        grid=(T // tr,),
        in_specs=[row, pl.BlockSpec((tr, D_MODEL), lambda i: (i, 1)),
                  pl.BlockSpec((1, D_MODEL), lambda i: (0, 0)), pl.BlockSpec((1, D_MODEL), lambda i: (0, 1)),
                  row, row],
        out_specs=row,
        out_shape=jax.ShapeDtypeStruct((T, D_MODEL), BF16),
        compiler_params=_params(("parallel",)),
    )(h_e, h_e, b_gate, b_gate, y_sb, y_ret)


def _col_sum_update(acc_ref, val, first):
    part = jnp.sum(val.reshape(val.shape[0] // 8, 8, val.shape[1]), axis=0)

    @pl.when(first)
    def _():
        acc_ref[...] = part

    @pl.when(jnp.logical_not(first))
    def _():
        acc_ref[...] += part


def _gate_mix_bwd(d_mix, h_e, b_gate, y_sb, y_ret):
    T = h_e.shape[0]
    tr = _row_tile(T)
    steps = T // tr

    def body(dm_ref, g_ref, b_ref, ys_ref, yr_ref, dys_ref, dyr_ref, de_ref, db_ref, acc_ref):
        i = pl.program_id(0)
        dm = dm_ref[...]
        gates = jax.nn.sigmoid(g_ref[...] + b_ref[...])
        g0, g1 = gates[:, :D_MODEL], gates[:, D_MODEL:]
        dys_ref[...] = (dm * g0).astype(BF16)
        dyr_ref[...] = (dm * g1).astype(BF16)
        de = jnp.concatenate([dm * ys_ref[...] * g0 * (1.0 - g0), dm * yr_ref[...] * g1 * (1.0 - g1)], axis=1)
        de_ref[...] = de.astype(BF16)
        _col_sum_update(acc_ref, de, i == 0)

        @pl.when(i == steps - 1)
        def _():
            db_ref[...] = jnp.sum(acc_ref[...], axis=0, keepdims=True)

    row = pl.BlockSpec((tr, D_MODEL), lambda i: (i, 0))
    wide = pl.BlockSpec((tr, 2 * D_MODEL), lambda i: (i, 0))
    vec = pl.BlockSpec((1, 2 * D_MODEL), lambda i: (0, 0))
    return pl.pallas_call(
        body,
        name="gate_mix_bwd",
        grid=(steps,),
        in_specs=[row, wide, vec, row, row],
        out_specs=[row, row, wide, vec],
        out_shape=[jax.ShapeDtypeStruct((T, D_MODEL), BF16), jax.ShapeDtypeStruct((T, D_MODEL), BF16),
                   jax.ShapeDtypeStruct((T, 2 * D_MODEL), BF16), jax.ShapeDtypeStruct((1, 2 * D_MODEL), F32)],
        scratch_shapes=[pltpu.VMEM((8, 2 * D_MODEL), F32)],
        compiler_params=_params(("arbitrary",)),
    )(d_mix, h_e, b_gate, y_sb, y_ret)


def _ln_stats(u):
    mu = jnp.mean(u, axis=1, keepdims=True)
    cen = u - mu
    var = jnp.mean(cen * cen, axis=1, keepdims=True)
    rstd = lax.rsqrt(var + LN_EPS)
    return cen * rstd, rstd


def _ln_input_grad(d_out, gain, xhat, rstd):
    d_hat = d_out * gain
    return rstd * (d_hat - jnp.mean(d_hat, axis=1, keepdims=True)
                   - xhat * jnp.mean(d_hat * xhat, axis=1, keepdims=True))


def _ln_fwd(x_prev, sub, gain, bias, name):
    T = x_prev.shape[0]
    tr = _row_tile(T)

    def body(x_ref, s_ref, g_ref, b_ref, o_ref, xh_ref, rs_ref):
        xhat, rstd = _ln_stats(DN_ALPHA * x_ref[...] + s_ref[...])
        o_ref[...] = xhat * g_ref[...] + b_ref[...]
        xh_ref[...] = xhat
        rs_ref[...] = rstd

    row = pl.BlockSpec((tr, D_MODEL), lambda i: (i, 0))
    vec = pl.BlockSpec((1, D_MODEL), lambda i: (0, 0))
    return pl.pallas_call(
        body,
        name=name,
        grid=(T // tr,),
        in_specs=[row, row, vec, vec],
        out_specs=[row, row, pl.BlockSpec((tr, 1), lambda i: (i, 0))],
        out_shape=[jax.ShapeDtypeStruct((T, D_MODEL), F32), jax.ShapeDtypeStruct((T, D_MODEL), F32),
                   jax.ShapeDtypeStruct((T, 1), F32)],
        compiler_params=_params(("parallel",)),
    )(x_prev, sub, gain, bias)


def _ln_bwd(d_out, xhat, rstd, gain, name):
    T = d_out.shape[0]
    tr = _row_tile(T)
    steps = T // tr

    def body(d_ref, xh_ref, rs_ref, g_ref, du_ref, dg_ref, db_ref, accg_ref, accb_ref):
        i = pl.program_id(0)
        d_o, xh = d_ref[...], xh_ref[...]
        du_ref[...] = _ln_input_grad(d_o, g_ref[...], xh, rs_ref[...])
        _col_sum_update(accg_ref, d_o * xh, i == 0)
        _col_sum_update(accb_ref, d_o, i == 0)

        @pl.when(i == steps - 1)
        def _():
            dg_ref[...] = jnp.sum(accg_ref[...], axis=0, keepdims=True)
            db_ref[...] = jnp.sum(accb_ref[...], axis=0, keepdims=True)

    row = pl.BlockSpec((tr, D_MODEL), lambda i: (i, 0))
    vec = pl.BlockSpec((1, D_MODEL), lambda i: (0, 0))
    return pl.pallas_call(
        body,
        name=name,
        grid=(steps,),
        in_specs=[row, row, pl.BlockSpec((tr, 1), lambda i: (i, 0)), vec],
        out_specs=[row, vec, vec],
        out_shape=[jax.ShapeDtypeStruct((T, D_MODEL), F32), jax.ShapeDtypeStruct((1, D_MODEL), F32),
                   jax.ShapeDtypeStruct((1, D_MODEL), F32)],
        scratch_shapes=[pltpu.VMEM((8, D_MODEL), F32), pltpu.VMEM((8, D_MODEL), F32)],
        compiler_params=_params(("arbitrary",)),
    )(d_out, xhat, rstd, gain)


def _ln_loss(x_prev, sub, gain, bias, target):
    T = x_prev.shape[0]
    tr = _row_tile(T)
    steps = T // tr

    def body(x_ref, s_ref, g_ref, b_ref, t_ref, loss_ref, du_ref, dg_ref, db_ref, accl_ref, accg_ref, accb_ref):
        i = pl.program_id(0)
        gain_v = g_ref[...]
        xhat, rstd = _ln_stats(DN_ALPHA * x_ref[...] + s_ref[...])
        diff = xhat * gain_v + b_ref[...] - t_ref[...]
        d_o = diff * (1.0 / D_MODEL)
        du_ref[...] = _ln_input_grad(d_o, gain_v, xhat, rstd)
        _col_sum_update(accl_ref, diff * diff, i == 0)
        _col_sum_update(accg_ref, d_o * xhat, i == 0)
        _col_sum_update(accb_ref, d_o, i == 0)

        @pl.when(i == steps - 1)
        def _():
            per_col = jnp.sum(accl_ref[...], axis=0, keepdims=True)
            loss_ref[...] = jnp.sum(per_col, axis=1, keepdims=True) * (0.5 / D_MODEL)
            dg_ref[...] = jnp.sum(accg_ref[...], axis=0, keepdims=True)
            db_ref[...] = jnp.sum(accb_ref[...], axis=0, keepdims=True)

    row = pl.BlockSpec((tr, D_MODEL), lambda i: (i, 0))
    vec = pl.BlockSpec((1, D_MODEL), lambda i: (0, 0))
    return pl.pallas_call(
        body,
        name="ln3_loss",
        grid=(steps,),
        in_specs=[row, row, vec, vec, row],
        out_specs=[pl.BlockSpec((1, 1), lambda i: (0, 0)), row, vec, vec],
        out_shape=[jax.ShapeDtypeStruct((1, 1), F32), jax.ShapeDtypeStruct((T, D_MODEL), F32),
                   jax.ShapeDtypeStruct((1, D_MODEL), F32), jax.ShapeDtypeStruct((1, D_MODEL), F32)],
        scratch_shapes=[pltpu.VMEM((8, D_MODEL), F32)] * 3,
        compiler_params=_params(("arbitrary",)),
    )(x_prev, sub, gain, bias, target)


def _mem_probs(q_h, k_h):
    s = _dot(q_h, k_h, _NT) * (MEM_HEAD_DIM ** -0.5)
    e = jnp.exp(s - jnp.max(s, axis=1, keepdims=True))
    return e / jnp.sum(e, axis=1, keepdims=True)


def _xattn_fwd(q, kv):
    T, mem_len = q.shape[0], kv.shape[0]
    tq = _pick(T, (512, 256, 128))

    def body(q_ref, kv_ref, o_ref):
        for h in range(MEM_HEADS):
            cols = slice(h * MEM_HEAD_DIM, (h + 1) * MEM_HEAD_DIM)
            vcols = slice(D_MODEL + h * MEM_HEAD_DIM, D_MODEL + (h + 1) * MEM_HEAD_DIM)
            p = _mem_probs(q_ref[:, cols], kv_ref[:, cols])
            o_ref[:, cols] = _dot(p.astype(BF16), kv_ref[:, vcols], _NN).astype(BF16)

    return pl.pallas_call(
        body,
        name="xattn_fwd",
        grid=(T // tq,),
        in_specs=[pl.BlockSpec((tq, D_MODEL), lambda i: (i, 0)),
                  pl.BlockSpec((mem_len, 2 * D_MODEL), lambda i: (0, 0))],
        out_specs=pl.BlockSpec((tq, D_MODEL), lambda i: (i, 0)),
        out_shape=jax.ShapeDtypeStruct((T, D_MODEL), BF16),
        compiler_params=_params(("parallel",)),
    )(q, kv)


def _xattn_bwd(q, kv, d_o):
    T, mem_len = q.shape[0], kv.shape[0]
    tq = _pick(T, (512, 256, 128))

    def body(q_ref, kv_ref, do_ref, dq_ref, dkv_ref):
        @pl.when(pl.program_id(0) == 0)
        def _():
            dkv_ref[...] = jnp.zeros_like(dkv_ref)

        for h in range(MEM_HEADS):
            cols = slice(h * MEM_HEAD_DIM, (h + 1) * MEM_HEAD_DIM)
            vcols = slice(D_MODEL + h * MEM_HEAD_DIM, D_MODEL + (h + 1) * MEM_HEAD_DIM)
            q_h, k_h, do_h = q_ref[:, cols], kv_ref[:, cols], do_ref[:, cols]
            p = _mem_probs(q_h, k_h)
            dp = _dot(do_h, kv_ref[:, vcols], _NT)
            ds = p * (dp - jnp.sum(dp * p, axis=1, keepdims=True))
            dsb = (ds * (MEM_HEAD_DIM ** -0.5)).astype(BF16)
            dq_ref[:, cols] = _dot(dsb, k_h, _NN).astype(BF16)
            dkv_ref[:, cols] += _dot(dsb, q_h, _TN)
            dkv_ref[:, vcols] += _dot(p.astype(BF16), do_h, _TN)

    row = pl.BlockSpec((tq, D_MODEL), lambda i: (i, 0))
    full = pl.BlockSpec((mem_len, 2 * D_MODEL), lambda i: (0, 0))
    return pl.pallas_call(
        body,
        name="xattn_bwd",
        grid=(T // tq,),
        in_specs=[row, full, row],
        out_specs=[row, full],
        out_shape=[jax.ShapeDtypeStruct((T, D_MODEL), BF16), jax.ShapeDtypeStruct((mem_len, 2 * D_MODEL), F32)],
        compiler_params=_params(("arbitrary",)),
    )(q, kv, d_o)


def _swiglu_fwd(f):
    T = f.shape[0]
    tr = _row_tile(T)

    def body(f_ref, o_ref):
        a, b = f_ref[:, :FFN_HIDDEN], f_ref[:, FFN_HIDDEN:]
        o_ref[...] = (a * jax.nn.sigmoid(a) * b).astype(BF16)

    return pl.pallas_call(
        body,
        name="swiglu_fwd",
        grid=(T // tr,),
        in_specs=[pl.BlockSpec((tr, 2 * FFN_HIDDEN), lambda i: (i, 0))],
        out_specs=pl.BlockSpec((tr, FFN_HIDDEN), lambda i: (i, 0)),
        out_shape=jax.ShapeDtypeStruct((T, FFN_HIDDEN), BF16),
        compiler_params=_params(("parallel",)),
    )(f)


def _swiglu_bwd(d_hidden, f):
    T = f.shape[0]
    tr = _row_tile(T)

    def body(d_ref, f_ref, o_ref):
        a, b = f_ref[:, :FFN_HIDDEN], f_ref[:, FFN_HIDDEN:]
        d_h = d_ref[...]
        sig = jax.nn.sigmoid(a)
        o_ref[:, :FFN_HIDDEN] = (d_h * b * (sig * (1.0 + a * (1.0 - sig)))).astype(BF16)
        o_ref[:, FFN_HIDDEN:] = (d_h * (a * sig)).astype(BF16)

    wide = pl.BlockSpec((tr, 2 * FFN_HIDDEN), lambda i: (i, 0))
    return pl.pallas_call(
        body,
        name="swiglu_bwd",
        grid=(T // tr,),
        in_specs=[pl.BlockSpec((tr, FFN_HIDDEN), lambda i: (i, 0)), wide],
        out_specs=wide,
        out_shape=jax.ShapeDtypeStruct((T, 2 * FFN_HIDDEN), BF16),
        compiler_params=_params(("parallel",)),
    )(d_hidden, f)


def _local_step(x, mem, w_in, small, target, fetch_rest, ship):
    T = x.shape[0]
    tables = _ret_tables(T)
    xb, memb = x.astype(BF16), mem.astype(BF16)

    h_a = _mm(xb, w_in[:, 0:1536], mode="nn", out_dtype=BF16, name="proj_sb")
    h_b = _mm(xb, w_in[:, 1536:2560], mode="nn", out_dtype=F32, name="proj_ret_qk")
    h_c = _mm(xb, w_in[:, 2560:3584], mode="nn", out_dtype=BF16, name="proj_ret_v")
    h_d = _mm(xb, w_in[:, 3584:4608], mode="nn", out_dtype=F32, name="proj_ret_g")
    h_e = _mm(xb, w_in[:, 4608:6656], mode="nn", out_dtype=F32, name="proj_gate")
    (a_sb, r_mat, _), w = fetch_rest(lambda rider: _sb_fwd(h_a, rider))
    y_gated, o_pre, states = _ret_fwd(h_b, h_c, h_d, tables)
    y_sb = _mm(a_sb, w["w_sb_o"], mode="nn", out_dtype=F32, name="sb_out")
    y_ret = _mm(y_gated, w["w_ret_o"], mode="nn", out_dtype=F32, name="ret_out")
    mix_in = _gate_mix_fwd(h_e, small["b_gate"], y_sb, y_ret)
    mix = _mm(mix_in, w["w_mix_o"], mode="nn", out_dtype=F32, name="mix_out")
    x1, xhat1, rstd1 = _ln_fwd(x, mix, small["ln1_g"], small["ln1_b"], "ln1_fwd")
    x1b = x1.astype(BF16)
    q_m = _mm(x1b, w["w_mem_q"], mode="nn", out_dtype=BF16, name="mem_q")
    kv_m = _mm(memb, w["w_mem_kv"], mode="nn", out_dtype=BF16, name="mem_kv")
    o_m = _xattn_fwd(q_m, kv_m)
    xa = _mm(o_m, w["w_mem_o"], mode="nn", out_dtype=F32, name="mem_out")
    x2, xhat2, rstd2 = _ln_fwd(x1, xa, small["ln2_g"], small["ln2_b"], "ln2_fwd")
    x2b = x2.astype(BF16)
    f = _mm(x2b, w["w_ffn_in"], mode="nn", out_dtype=F32, name="ffn_in")
    hidden = _swiglu_fwd(f)
    ff = _mm(hidden, w["w_ffn_out"], mode="nn", out_dtype=F32, name="ffn_out")
    loss, du3, d_ln3_g, d_ln3_b = _ln_loss(x2, ff, small["ln3_g"], small["ln3_b"], target)

    du3b = du3.astype(BF16)
    g_ffn_out = _mm(hidden, du3b, mode="tn", out_dtype=F32, name="g_ffn_out")
    d_hidden = _mm(du3b, w["w_ffn_out"], mode="nt", out_dtype=F32, name="d_hidden")
    d_f = _swiglu_bwd(d_hidden, f)
    g_ffn_in = _mm(x2b, d_f, mode="tn", out_dtype=F32, name="g_ffn_in")
    d_x2 = _mm(d_f, w["w_ffn_in"], mode="nt", out_dtype=F32, name="d_x2", res=du3, res_scale=DN_ALPHA)
    du2, d_ln2_g, d_ln2_b = _ln_bwd(d_x2, xhat2, rstd2, small["ln2_g"], "ln2_bwd")
    du2b = du2.astype(BF16)
    g_mem_o = _mm(o_m, du2b, mode="tn", out_dtype=F32, name="g_mem_o")
    d_om = _mm(du2b, w["w_mem_o"], mode="nt", out_dtype=BF16, name="d_om")
    d_qm, d_kvm = _xattn_bwd(q_m, kv_m, d_om)
    g_mem_q = _mm(x1b, d_qm, mode="tn", out_dtype=F32, name="g_mem_q")
    g_mem_kv = _mm(memb, d_kvm.astype(BF16), mode="tn", out_dtype=F32, name="g_mem_kv")
    d_x1 = _mm(d_qm, w["w_mem_q"], mode="nt", out_dtype=F32, name="d_x1", res=du2, res_scale=DN_ALPHA)
    du1, d_ln1_g, d_ln1_b = _ln_bwd(d_x1, xhat1, rstd1, small["ln1_g"], "ln1_bwd")
    du1b = du1.astype(BF16)
    g_mix_o = _mm(mix_in, du1b, mode="tn", out_dtype=F32, name="g_mix_o")
    d_mix_in = _mm(du1b, w["w_mix_o"], mode="nt", out_dtype=F32, name="d_mix_in")
    d_ysb, d_yret, d_e, d_b_gate = _gate_mix_bwd(d_mix_in, h_e, small["b_gate"], y_sb, y_ret)
    g_sb_o = _mm(a_sb, d_ysb, mode="tn", out_dtype=F32, name="g_sb_o")
    g_ret_o = _mm(y_gated, d_yret, mode="tn", out_dtype=F32, name="g_ret_o")
    d_asb = _mm(d_ysb, w["w_sb_o"], mode="nt", out_dtype=BF16, name="d_asb")
    d_ygated = _mm(d_yret, w["w_ret_o"], mode="nt", out_dtype=F32, name="d_ygated")
    small_grads = {"b_gate": d_b_gate, "ln1_g": d_ln1_g, "ln1_b": d_ln1_b, "ln2_g": d_ln2_g, "ln2_b": d_ln2_b,
                   "ln3_g": d_ln3_g, "ln3_b": d_ln3_b}
    early = {"w_ffn_out": g_ffn_out, "w_ffn_in": g_ffn_in, "w_mem_o": g_mem_o, "w_mem_q": g_mem_q,
             "w_mem_kv": g_mem_kv, "small": small_grads}
    d_rq, d_rk, d_c, d_d = ship(early, lambda rider: _ret_bwd(d_ygated, o_pre, states, h_b, h_c, h_d, tables, rider))
    d_q, d_k, d_v = _sb_bwd(h_a, d_asb, r_mat)
    d_h = jnp.concatenate([d_q, d_k.astype(BF16), d_v.astype(BF16), d_rq, d_rk, d_c, d_d, d_e], axis=1)
    (g_in,) = ship({"w_mix_o": g_mix_o, "w_sb_o": g_sb_o, "w_ret_o": g_ret_o},
                   lambda rider: _mm_host(xb, d_h, mode="tn", out_dtype=F32, name="g_in", rider=rider))
    (d_x,) = ship({"w_in": g_in},
                  lambda rider: _mm_host(d_h, w_in, mode="nt", out_dtype=F32, name="d_x", res=du1, res_scale=DN_ALPHA,
                                         rider=rider))
    return loss, d_x


def _adamw_math(w, g, m, v):
    m = ADAM_B1 * m + (1.0 - ADAM_B1) * g
    v = ADAM_B2 * v + (1.0 - ADAM_B2) * jnp.square(g)
    m_hat = m / (1.0 - ADAM_B1 ** ADAM_STEP)
    v_hat = v / (1.0 - ADAM_B2 ** ADAM_STEP)
    delta = -ADAM_LR * (m_hat / (jnp.sqrt(v_hat) + ADAM_EPS) + ADAM_WD * w)
    return delta, m, v


def _adamw(parts, w, m, v, name):
    R, C = w.shape
    tr = max(t for t in range(16, min(R, 256) + 1, 16) if R % t == 0) if R >= 16 else R

    def body(p_ref, w_ref, m_ref, v_ref, g_ref, d_ref, nm_ref, nv_ref):
        g = p_ref[0].astype(F32)
        for j in range(1, N_DEV):
            g = g + p_ref[j].astype(F32)
        delta, nm, nv = _adamw_math(w_ref[...], g, m_ref[...], v_ref[...])
        g_ref[...] = g
        d_ref[...] = delta
        nm_ref[...] = nm
        nv_ref[...] = nv

    blk = pl.BlockSpec((tr, C), lambda i: (i, 0))
    out = jax.ShapeDtypeStruct((R, C), F32)
    return pl.pallas_call(
        body,
        name=name,
        grid=(R // tr,),
        in_specs=[pl.BlockSpec((N_DEV, tr, C), lambda i: (0, i, 0)), blk, blk, blk],
        out_specs=[blk] * 4,
        out_shape=[out] * 4,
        compiler_params=_params(("parallel",)),
    )(parts, w, m, v)


_SHARD_AXIS = {"w_in": 1, "w_sb_o": 1, "w_ret_o": 0, "w_mix_o": 0, "w_mem_q": 0, "w_mem_kv": 1, "w_mem_o": 0,
               "w_ffn_in": 1, "w_ffn_out": 0}
_MATRICES = tuple(_SHARD_AXIS)
_SMALL = ("b_gate", "ln1_g", "ln1_b", "ln2_g", "ln2_b", "ln3_g", "ln3_b")
_WEIGHT_ORDER = ("w_in", "b_gate", "w_sb_o", "w_ret_o", "w_mix_o", "ln1_g", "ln1_b", "w_mem_q", "w_mem_kv", "w_mem_o",
                 "ln2_g", "ln2_b", "w_ffn_in", "w_ffn_out", "ln3_g", "ln3_b")


def _assemble(name, gathered):
    if _SHARD_AXIS[name] == 0:
        return gathered.reshape(-1, gathered.shape[2])
    return jnp.transpose(gathered, (1, 0, 2)).reshape(gathered.shape[1], -1)


def _to_slots(name, full):
    if _SHARD_AXIS[name] == 0:
        return full.reshape(N_DEV, full.shape[0] // N_DEV, full.shape[1])
    return jnp.transpose(full.reshape(full.shape[0], N_DEV, full.shape[1] // N_DEV), (1, 0, 2))


def _pack_small(vals):
    return jnp.concatenate([vals["b_gate"].reshape(2, D_MODEL)] + [vals[n] for n in _SMALL[1:]], axis=0)


def _unpack_small(packed):
    out = {"b_gate": packed[0:2].reshape(1, 2 * D_MODEL)}
    for i, n in enumerate(_SMALL[1:]):
        out[n] = packed[2 + i:3 + i]
    return out


def kernel(x, mem, w_in, b_gate, w_sb_o, w_ret_o, w_mix_o, ln1_g, ln1_b, w_mem_q, w_mem_kv, w_mem_o, ln2_g, ln2_b, w_ffn_in, w_ffn_out, ln3_g, ln3_b, loss_target, m_w_in, m_b_gate, m_w_sb_o, m_w_ret_o, m_w_mix_o, m_ln1_g, m_ln1_b, m_w_mem_q, m_w_mem_kv, m_w_mem_o, m_ln2_g, m_ln2_b, m_w_ffn_in, m_w_ffn_out, m_ln3_g, m_ln3_b, v_w_in, v_b_gate, v_w_sb_o, v_w_ret_o, v_w_mix_o, v_ln1_g, v_ln1_b, v_w_mem_q, v_w_mem_kv, v_w_mem_o, v_ln2_g, v_ln2_b, v_w_ffn_in, v_w_ffn_out, v_ln3_g, v_ln3_b):
    weights = dict(w_in=w_in, b_gate=b_gate, w_sb_o=w_sb_o, w_ret_o=w_ret_o, w_mix_o=w_mix_o, ln1_g=ln1_g, ln1_b=ln1_b,
                   w_mem_q=w_mem_q, w_mem_kv=w_mem_kv, w_mem_o=w_mem_o, ln2_g=ln2_g, ln2_b=ln2_b, w_ffn_in=w_ffn_in,
                   w_ffn_out=w_ffn_out, ln3_g=ln3_g, ln3_b=ln3_b)
    mom1 = dict(w_in=m_w_in, b_gate=m_b_gate, w_sb_o=m_w_sb_o, w_ret_o=m_w_ret_o, w_mix_o=m_w_mix_o, ln1_g=m_ln1_g,
                ln1_b=m_ln1_b, w_mem_q=m_w_mem_q, w_mem_kv=m_w_mem_kv, w_mem_o=m_w_mem_o, ln2_g=m_ln2_g, ln2_b=m_ln2_b,
                w_ffn_in=m_w_ffn_in, w_ffn_out=m_w_ffn_out, ln3_g=m_ln3_g, ln3_b=m_ln3_b)
    mom2 = dict(w_in=v_w_in, b_gate=v_b_gate, w_sb_o=v_w_sb_o, w_ret_o=v_w_ret_o, w_mix_o=v_w_mix_o, ln1_g=v_ln1_g,
                ln1_b=v_ln1_b, w_mem_q=v_w_mem_q, w_mem_kv=v_w_mem_kv, w_mem_o=v_w_mem_o, ln2_g=v_ln2_g, ln2_b=v_ln2_b,
                w_ffn_in=v_w_ffn_in, w_ffn_out=v_w_ffn_out, ln3_g=v_ln3_g, ln3_b=v_ln3_b)

    (gathered_in,) = _exchange([weights["w_in"][0].astype(BF16)], False, "gather_w_in")
    rest = [n for n in _MATRICES if n != "w_in"]
    received = {}

    def fetch_rest(host):
        res = host(_Rider([weights[n][0].astype(BF16) for n in rest], False))
        return res, {n: _assemble(n, g) for n, g in zip(rest, res[-1])}

    def ship(grads, host):
        names = list(grads)
        bufs = []
        for n in names:
            if n == "small":
                part = _pack_small(grads[n])
                bufs.append(jnp.broadcast_to(part[None], (N_DEV,) + part.shape))
            else:
                bufs.append(_to_slots(n, grads[n]).astype(BF16))
        res = host(_Rider(bufs, True))
        received.update(zip(names, res[-1]))
        return res[:-1]

    small = {n: weights[n] for n in _SMALL}
    loss, d_x = _local_step(x[0], mem[0], _assemble("w_in", gathered_in), small, loss_target[0], fetch_rest, ship)

    new = {}
    for n in _MATRICES:
        new[n] = _adamw(received[n], weights[n][0], mom1[n][0], mom2[n][0], "adamw_" + n)
    packed = _adamw(received["small"], _pack_small({n: weights[n] for n in _SMALL}),
                    _pack_small({n: mom1[n] for n in _SMALL}), _pack_small({n: mom2[n] for n in _SMALL}), "adamw_small")
    small_new = [_unpack_small(p) for p in packed]

    outs = [lax.psum(loss[0, 0], MESH_AXES), d_x[None]]
    for slot in range(4):
        for n in _WEIGHT_ORDER:
            outs.append(new[n][slot][None] if n in new else small_new[slot][n])
    return tuple(outs)
```

```python
import functools
import math

import jax
import jax.numpy as jnp
from jax import lax
from jax.experimental import pallas as pl
from jax.experimental.pallas import tpu as pltpu

F32 = jnp.float32
BF16 = jnp.bfloat16

N_DEV = 8
D_MODEL = 1024
SB_HEAD_DIM = 64
SB_WIDTH = 512
RET_HEADS = 4
RET_QK_DIM = 128
RET_V_DIM = 256
RET_QK_WIDTH = 512
RET_V_WIDTH = 1024
RET_CHUNK = 128
ROPE_BASE = 10000.0
MEM_HEADS = 4
MEM_HEAD_DIM = 256
FFN_HIDDEN = 2816
DN_ALPHA = 2.0 ** 0.25
LN_EPS = 1e-5
ADAM_LR = 0.001
ADAM_B1 = 0.9
ADAM_B2 = 0.999
ADAM_EPS = 1e-08
ADAM_WD = 0.01
ADAM_STEP = 10

VMEM_LIMIT_BYTES = 52 * 1024 * 1024
LANES = 128
SB_KEY_BLOCK = 128
SB_Q_BLOCK = 256
SB_DEAD_LOG = -105.0

MESH_AXES = ("x", "y", "c")


def _pick(dim, prefs):
    for p in prefs:
        if dim % p == 0:
            return p
    return dim


def _params(sem):
    return pltpu.CompilerParams(dimension_semantics=sem, vmem_limit_bytes=VMEM_LIMIT_BYTES)


def _dot(a, b, dims):
    return lax.dot_general(a, b, (dims, ((), ())), preferred_element_type=F32)


_NN = ((1,), (0,))
_NT = ((1,), (1,))
_TN = ((0,), (0,))


def _my_index():
    return 4 * lax.axis_index("x") + 2 * lax.axis_index("y") + lax.axis_index("c")


def _peer(k):
    x, y, c = lax.axis_index("x"), lax.axis_index("y"), lax.axis_index("c")
    bx, by, bc = (k >> 2) & 1, (k >> 1) & 1, k & 1
    px = (1 - x) if bx else x
    py = (1 - y) if by else y
    pc = (1 - c) if bc else c
    return (px, py, pc), 4 * px + 2 * py + pc


class _Rider:
    def __init__(self, bufs, scatter):
        self.bufs, self.scatter, self.n = list(bufs), scatter, len(bufs)
        self.specs = [pl.BlockSpec(memory_space=pl.ANY)] * self.n
        self.out_shape = [jax.ShapeDtypeStruct(b.shape if scatter else (N_DEV,) + b.shape, b.dtype) for b in self.bufs]
        self.scratch = [pltpu.SemaphoreType.DMA((self.n, N_DEV - 1)), pltpu.SemaphoreType.DMA((self.n, N_DEV - 1)),
                        pltpu.SemaphoreType.DMA((self.n,))]

    def _copies(self, ride):
        src, dst, (send_sems, recv_sems, local_sems) = ride
        me = _my_index()
        copies = []
        for a in range(self.n):
            own = src[a].at[me] if self.scatter else src[a]
            copies.append(pltpu.make_async_copy(own, dst[a].at[me], local_sems.at[a]))
            for k in range(1, N_DEV):
                peer, peer_idx = _peer(k)
                copies.append(pltpu.make_async_remote_copy(
                    src_ref=src[a].at[peer_idx] if self.scatter else src[a],
                    dst_ref=dst[a].at[me],
                    send_sem=send_sems.at[a, k - 1],
                    recv_sem=recv_sems.at[a, k - 1],
                    device_id=peer,
                    device_id_type=pl.DeviceIdType.MESH,
                ))
        return copies

    def start_at_first(self, ids, ride):
        first = functools.reduce(jnp.logical_and, [i == 0 for i in ids])

        @pl.when(first)
        def _():
            for cp in self._copies(ride):
                cp.start()

    def wait_at_last(self, ids, grid, ride):
        last = functools.reduce(jnp.logical_and, [i == g - 1 for i, g in zip(ids, grid)])

        @pl.when(last)
        def _():
            for cp in self._copies(ride):
                cp.wait()


def _exchange(bufs, scatter, name):
    rider = _Rider(bufs, scatter)

    def body(*refs):
        ride = (refs[:rider.n], refs[rider.n:2 * rider.n], refs[2 * rider.n:])
        for cp in rider._copies(ride):
            cp.start()
        for cp in rider._copies(ride):
            cp.wait()

    return pl.pallas_call(
        body,
        name=name,
        in_specs=rider.specs,
        out_specs=rider.specs,
        out_shape=rider.out_shape,
        scratch_shapes=rider.scratch,
    )(*rider.bufs)


MM_RESIDENT_B_BYTES = 14 * 1024 * 1024
MM_A_TILE_BYTES = 4 * 1024 * 1024
MM_OUT_TILE_BYTES = 6 * 1024 * 1024


def _mm_tiles(mode, M, N, K, a_bytes, out_bytes):
    if mode != "tn" and K * N * 2 <= MM_RESIDENT_B_BYTES:
        for tm in (1024, 512, 256, 128):
            if M % tm == 0 and tm * K * a_bytes <= MM_A_TILE_BYTES and tm * N * out_bytes <= MM_OUT_TILE_BYTES:
                return tm, N, K
    if mode == "tn":
        return (_pick(M, (1024, 1408, 512, 256, 128)), _pick(N, (1024, 1664, 1408, 512, 256, 128)),
                _pick(K, (2048, 1024, 512, 256, 128)))
    return _pick(M, (1024, 512, 256, 128)), _pick(N, (512, 256, 128)), _pick(K, (1024, 512, 256, 128))


def _mm(a, b, *, mode, out_dtype, name, res=None, res_scale=1.0, rider=None):
    if mode == "nn":
        (M, K), (K2, N) = a.shape, b.shape
    elif mode == "nt":
        (M, K), (N, K2) = a.shape, b.shape
    else:
        (K, M), (K2, N) = a.shape, b.shape
    assert K == K2, (a.shape, b.shape, mode)
    out_bytes = jnp.dtype(out_dtype).itemsize + (4 if res is not None else 0)
    tm, tn, tk = _mm_tiles(mode, M, N, K, a.dtype.itemsize, out_bytes)
    grid = (M // tm, N // tn, K // tk)
    nk = grid[2]
    dims = {"nn": _NN, "nt": _NT, "tn": _TN}[mode]
    n_in = 2 + (res is not None)
    n_ride = rider.n if rider is not None else 0

    def body(*refs):
        a_ref, b_ref = refs[:2]
        r_ref = refs[2] if res is not None else None
        o_ref = refs[n_in + n_ride]
        rest = refs[n_in + 2 * n_ride + 1:]
        acc_ref = rest[0] if nk > 1 else None
        ids = [pl.program_id(d) for d in range(3)]
        if rider is not None:
            ride = (refs[n_in:n_in + n_ride], refs[n_in + n_ride + 1:n_in + 2 * n_ride + 1], rest[-3:])
            rider.start_at_first(ids, ride)
        part = _dot(a_ref[...].astype(BF16), b_ref[...].astype(BF16), dims)

        def finish(total):
            if r_ref is not None:
                total = total + res_scale * r_ref[...]
            o_ref[...] = total.astype(out_dtype)

        if nk == 1:
            finish(part)
        else:
            k = ids[2]

            @pl.when(k == 0)
            def _():
                acc_ref[...] = part

            @pl.when(k > 0)
            def _():
                acc_ref[...] += part

            @pl.when(k == nk - 1)
            def _():
                finish(acc_ref[...])

        if rider is not None:
            rider.wait_at_last(ids, grid, ride)

    if mode == "nn":
        a_spec = pl.BlockSpec((tm, tk), lambda i, j, k: (i, k))
        b_spec = pl.BlockSpec((tk, tn), lambda i, j, k: (k, j))
    elif mode == "nt":
        a_spec = pl.BlockSpec((tm, tk), lambda i, j, k: (i, k))
        b_spec = pl.BlockSpec((tn, tk), lambda i, j, k: (j, k))
    else:
        a_spec = pl.BlockSpec((tk, tm), lambda i, j, k: (k, i))
        b_spec = pl.BlockSpec((tk, tn), lambda i, j, k: (k, j))
    o_spec = pl.BlockSpec((tm, tn), lambda i, j, k: (i, j))
    in_specs = [a_spec, b_spec] + ([o_spec] if res is not None else [])
    args = (a, b) + ((res,) if res is not None else ())
    out_specs, out_shape = [o_spec], [jax.ShapeDtypeStruct((M, N), out_dtype)]
    scratch = [pltpu.VMEM((tm, tn), F32)] if nk > 1 else []
    sem = ("parallel", "parallel", "arbitrary")
    if rider is not None:
        in_specs, args = in_specs + rider.specs, args + tuple(rider.bufs)
        out_specs, out_shape = out_specs + rider.specs, out_shape + rider.out_shape
        scratch = scratch + rider.scratch
        sem = ("arbitrary",) * 3
    outs = pl.pallas_call(
        body,
        name=name,
        grid=grid,
        in_specs=in_specs,
        out_specs=out_specs,
        out_shape=out_shape,
        scratch_shapes=scratch,
        compiler_params=_params(sem),
    )(*args)
    return outs[0] if rider is None else (outs[0], list(outs[1:]))


def _mm_host(a, b, *, rider, **kw):
    out = _mm(a, b, rider=rider, **kw)
    return out if rider is not None else (out, [])


MM_FUSED_MARGIN_BYTES = 10 * 1024 * 1024
MM_FUSED_MAX_ROWS = 512


def _mm_fused(a, b, *, mode, name, extras, outs, epilogue):
    if mode == "nn":
        (M, K), (K2, N) = a.shape, b.shape
        b_dims = _NN
    else:
        (M, K), (N, K2) = a.shape, b.shape
        b_dims = _NT
    assert K == K2, (a.shape, b.shape, mode)
    rows = [e for e in extras if e.shape[0] == M]
    per_row = 2 * (K * a.dtype.itemsize + sum(e.shape[1] * e.dtype.itemsize for e in rows)
                   + sum(c * jnp.dtype(d).itemsize for c, d in outs)) + 2 * N * 4
    budget = VMEM_LIMIT_BYTES - K * N * 2 - MM_FUSED_MARGIN_BYTES
    tm = next(t for t in (512, 256, 128, 64, 32, 16) if t <= MM_FUSED_MAX_ROWS and M % t == 0 and t * per_row <= budget)
    n_x = len(extras)

    def body(*refs):
        a_ref, b_ref = refs[:2]
        x_refs, o_refs = refs[2:2 + n_x], refs[2 + n_x:]
        prod = _dot(a_ref[...].astype(BF16), b_ref[...], b_dims)
        tiles = epilogue(prod, *[r[...] for r in x_refs])
        for o_ref, t in zip(o_refs, tiles):
            o_ref[...] = t.astype(o_ref.dtype)

    in_specs = [pl.BlockSpec((tm, K), lambda i: (i, 0)),
                pl.BlockSpec(b.shape, lambda i: (0, 0), pipeline_mode=pl.Buffered(1))]
    for e in extras:
        in_specs.append(pl.BlockSpec((tm, e.shape[1]), lambda i: (i, 0)) if e.shape[0] == M
                        else pl.BlockSpec(e.shape, lambda i: (0, 0)))
    return pl.pallas_call(
        body,
        name=name,
        grid=(M // tm,),
        in_specs=in_specs,
        out_specs=[pl.BlockSpec((tm, c), lambda i: (i, 0)) for c, _ in outs],
        out_shape=[jax.ShapeDtypeStruct((M, c), d) for c, d in outs],
        compiler_params=_params(("parallel",)),
    )(a, b, *extras)


def _pair_rows(blk, lane_is_a):
    zero = jnp.zeros_like(blk)
    return jnp.concatenate([jnp.where(lane_is_a, blk, zero), jnp.where(lane_is_a, zero, blk)], axis=0)


def _pair_scan_matrix(strict_after):
    r = lax.broadcasted_iota(jnp.int32, (4 * LANES, 2 * LANES), 0) & (2 * LANES - 1)
    c = lax.broadcasted_iota(jnp.int32, (4 * LANES, 2 * LANES), 1)
    same = (r >= LANES) == (c >= LANES)
    rr, cc = r & (LANES - 1), c & (LANES - 1)
    tri = (rr > cc) if strict_after else (rr < cc)
    return jnp.where(same & tri, 1.0, 0.0).astype(BF16)


def _split_dot(val, mat):
    hi = val.astype(BF16)
    lo = (val - hi.astype(F32)).astype(BF16)
    return _dot(jnp.concatenate([hi, lo], axis=1), mat, _NN)


def _pair_cols(col_a, col_b, rows):
    return jnp.concatenate([jnp.broadcast_to(col_a, (rows, LANES)), jnp.broadcast_to(col_b, (rows, LANES))], axis=1)


def _sb_scores(q, kk, mask):
    z = _dot(q, kk, _NT) * (SB_HEAD_DIM ** -0.5)
    t = jnp.log1p(jnp.exp(-jnp.abs(z)))
    log_beta = jnp.minimum(z, 0.0) - t
    log_rem = -jnp.maximum(z, 0.0) - t
    if mask is not None:
        log_rem = jnp.where(mask, log_rem, 0.0)
    return log_beta, log_rem


def _sb_fwd(h_a, rider=None):
    T = h_a.shape[0]
    tq = _pick(T, (SB_Q_BLOCK, SB_KEY_BLOCK))
    nq, per_q = T // tq, tq // SB_KEY_BLOCK
    assert T // SB_KEY_BLOCK <= LANES
    n_ride = rider.n if rider is not None else 0

    def body(*refs):
        q_ref, k_ref, v_ref = refs[:3]
        a_ref, r_ref = refs[3 + n_ride:5 + n_ride]
        ids = [pl.program_id(0), pl.program_id(1)]
        if rider is not None:
            ride = (refs[3:3 + n_ride], refs[5 + n_ride:5 + 2 * n_ride], refs[-3:])
            rider.start_at_first(ids, ride)
        i = ids[1]
        q = q_ref[...]
        lane_is_a = lax.broadcasted_iota(jnp.int32, (SB_KEY_BLOCK, LANES), 1) < SB_HEAD_DIM
        after = _pair_scan_matrix(True)
        row = lax.broadcasted_iota(jnp.int32, (tq, 2 * LANES), 0)
        col = lax.broadcasted_iota(jnp.int32, (tq, 2 * LANES), 1) & (LANES - 1)
        blk_lane = lax.broadcasted_iota(jnp.int32, (tq, LANES), 1)

        def tile(kb, masked, carry):
            acc, ra, rb, rma, rmb = carry
            ks = pl.multiple_of(kb * SB_KEY_BLOCK, SB_KEY_BLOCK)
            kk = _pair_rows(k_ref[pl.ds(ks, SB_KEY_BLOCK), :], lane_is_a)
            vv = _pair_rows(v_ref[pl.ds(ks, SB_KEY_BLOCK), :], lane_is_a)
            mask = ((ks + col) < (i * tq + row)) if masked else None
            log_beta, log_rem = _sb_scores(q, kk, mask)
            later = _split_dot(log_rem, after) + _pair_cols(ra, rb, tq)
            w = jnp.exp(log_beta + later)
            if masked:
                w = jnp.where(mask, w, 0.0)
            acc = acc + _dot(w.astype(BF16), vv, _NN)
            here = (blk_lane == kb).astype(F32)
            rma = rma + ra * here
            rmb = rmb + rb * here
            ra = ra + jnp.sum(log_rem[:, :LANES], axis=1, keepdims=True)
            rb = rb + jnp.sum(log_rem[:, LANES:], axis=1, keepdims=True)
            return acc, ra, rb, rma, rmb

        carry = (jnp.zeros((tq, LANES), F32), jnp.zeros((tq, 1), F32), jnp.zeros((tq, 1), F32),
                 jnp.zeros((tq, LANES), F32), jnp.zeros((tq, LANES), F32))
        for d in range(per_q):
            carry = tile(i * per_q + (per_q - 1 - d), True, carry)
        n_full = i * per_q

        def alive(c):
            return jnp.logical_and(c[0] < n_full, jnp.max(jnp.maximum(c[2], c[3])) > SB_DEAD_LOG)

        def step(c):
            return (c[0] + 1,) + tile(n_full - 1 - c[0], False, c[1:])

        done, acc, ra, rb, rma, rmb = lax.while_loop(alive, step, (jnp.int32(0),) + carry)
        skipped = blk_lane < (n_full - done)
        rma = jnp.where(skipped, ra, rma)
        rmb = jnp.where(skipped, rb, rmb)
        a_ref[...] = acc.astype(BF16)
        r_ref[0] = rma
        r_ref[1] = rmb
        if rider is not None:
            rider.wait_at_last(ids, (4, nq), ride)

    in_specs = [pl.BlockSpec((tq, LANES), lambda p, i: (i, p)),
                pl.BlockSpec((T, LANES), lambda p, i: (0, 4 + p)),
                pl.BlockSpec((T, LANES), lambda p, i: (0, 8 + p))]
    out_specs = [pl.BlockSpec((tq, LANES), lambda p, i: (i, p)),
                 pl.BlockSpec((2, tq, LANES), lambda p, i: (p, i, 0))]
    out_shape = [jax.ShapeDtypeStruct((T, SB_WIDTH), BF16), jax.ShapeDtypeStruct((8, T, LANES), F32)]
    args = (h_a, h_a, h_a)
    if rider is not None:
        in_specs, args = in_specs + rider.specs, args + tuple(rider.bufs)
        out_specs, out_shape = out_specs + rider.specs, out_shape + rider.out_shape
    outs = pl.pallas_call(
        body,
        name="sb_fwd",
        grid=(4, nq),
        in_specs=in_specs,
        out_specs=out_specs,
        out_shape=out_shape,
        scratch_shapes=rider.scratch if rider is not None else [],
        compiler_params=_params(("arbitrary", "arbitrary") if rider is not None else ("parallel", "arbitrary")),
    )(*args)
    return outs[0], outs[1], list(outs[2:])


def _sb_bwd(h_a, d_out, r_mat):
    T = h_a.shape[0]
    tq = _pick(T, (SB_Q_BLOCK, SB_KEY_BLOCK))
    nq, per_q = T // tq, tq // SB_KEY_BLOCK

    def body(q_ref, k_ref, v_ref, do_ref, r_ref, dq_ref, dk_ref, dv_ref):
        i = pl.program_id(1)

        @pl.when(i == 0)
        def _():
            dk_ref[...] = jnp.zeros_like(dk_ref)
            dv_ref[...] = jnp.zeros_like(dv_ref)

        q = q_ref[...]
        d_o = do_ref[...]
        rma, rmb = r_ref[0], r_ref[1]
        lane_is_a = lax.broadcasted_iota(jnp.int32, (SB_KEY_BLOCK, LANES), 1) < SB_HEAD_DIM
        after = _pair_scan_matrix(True)
        before = _pair_scan_matrix(False)
        row = lax.broadcasted_iota(jnp.int32, (tq, 2 * LANES), 0)
        col = lax.broadcasted_iota(jnp.int32, (tq, 2 * LANES), 1) & (LANES - 1)
        blk_lane = lax.broadcasted_iota(jnp.int32, (tq, LANES), 1)

        def tile(kb, masked, carry):
            dq, ca, cb = carry
            ks = pl.multiple_of(kb * SB_KEY_BLOCK, SB_KEY_BLOCK)
            kk = _pair_rows(k_ref[pl.ds(ks, SB_KEY_BLOCK), :], lane_is_a)
            vv = _pair_rows(v_ref[pl.ds(ks, SB_KEY_BLOCK), :], lane_is_a)
            mask = ((ks + col) < (i * tq + row)) if masked else None
            log_beta, log_rem = _sb_scores(q, kk, mask)
            here = (blk_lane == kb).astype(F32)
            ra = jnp.sum(rma * here, axis=1, keepdims=True)
            rb = jnp.sum(rmb * here, axis=1, keepdims=True)
            later = _split_dot(log_rem, after) + _pair_cols(ra, rb, tq)
            w = jnp.exp(log_beta + later)
            if masked:
                w = jnp.where(mask, w, 0.0)
            da = _dot(d_o, vv, _NT) * w
            prefix = _split_dot(da, before) + _pair_cols(ca, cb, tq)
            sig = jnp.exp(log_beta)
            dz = da * (1.0 - sig) - prefix * sig
            if masked:
                dz = jnp.where(mask, dz, 0.0)
            dzb = (dz * (SB_HEAD_DIM ** -0.5)).astype(BF16)
            dq = dq + _dot(dzb, kk, _NN)
            dkk = _dot(dzb, q, _TN)
            dvv = _dot(w.astype(BF16), d_o, _TN)
            dk_ref[pl.ds(ks, SB_KEY_BLOCK), :] += jnp.where(lane_is_a, dkk[:SB_KEY_BLOCK], dkk[SB_KEY_BLOCK:])
            dv_ref[pl.ds(ks, SB_KEY_BLOCK), :] += jnp.where(lane_is_a, dvv[:SB_KEY_BLOCK], dvv[SB_KEY_BLOCK:])
            ca = ca + jnp.sum(da[:, :LANES], axis=1, keepdims=True)
            cb = cb + jnp.sum(da[:, LANES:], axis=1, keepdims=True)
            return dq, ca, cb

        carry = (jnp.zeros((tq, LANES), F32), jnp.zeros((tq, 1), F32), jnp.zeros((tq, 1), F32))
        n_full = i * per_q
        col_max = jnp.max(jnp.maximum(rma, rmb), axis=0, keepdims=True)
        dead = jnp.logical_and(col_max <= SB_DEAD_LOG, blk_lane[0:1] < n_full)
        first = jnp.sum(dead.astype(F32)).astype(jnp.int32)
        carry = lax.fori_loop(first, n_full, lambda j, c: tile(j, False, c), carry)
        for d in range(per_q):
            carry = tile(i * per_q + d, True, carry)
        dq_ref[...] = carry[0].astype(BF16)

    return pl.pallas_call(
        body,
        name="sb_bwd",
        grid=(4, nq),
        in_specs=[pl.BlockSpec((tq, LANES), lambda p, i: (i, p)),
                  pl.BlockSpec((T, LANES), lambda p, i: (0, 4 + p)),
                  pl.BlockSpec((T, LANES), lambda p, i: (0, 8 + p)),
                  pl.BlockSpec((tq, LANES), lambda p, i: (i, p)),
                  pl.BlockSpec((2, tq, LANES), lambda p, i: (p, i, 0))],
        out_specs=[pl.BlockSpec((tq, LANES), lambda p, i: (i, p)),
                   pl.BlockSpec((T, LANES), lambda p, i: (0, p)),
                   pl.BlockSpec((T, LANES), lambda p, i: (0, p))],
        out_shape=[jax.ShapeDtypeStruct((T, SB_WIDTH), BF16),
                   jax.ShapeDtypeStruct((T, SB_WIDTH), F32),
                   jax.ShapeDtypeStruct((T, SB_WIDTH), F32)],
        compiler_params=_params(("parallel", "arbitrary")),
    )(h_a, h_a, h_a, d_out, r_mat)


def _ret_tables(T):
    half = RET_QK_DIM // 2
    inv = 1.0 / (ROPE_BASE ** (jnp.arange(half, dtype=F32) / half))
    ang = jnp.arange(T, dtype=F32)[:, None] * inv[None, :]
    cos, sin = jnp.cos(ang), jnp.sin(ang)
    cos_t = jnp.concatenate([cos, cos], axis=1)
    sin_t = jnp.concatenate([-sin, sin], axis=1)
    log_gamma = jnp.log1p(-jnp.exp2(-5.0 - jnp.arange(RET_HEADS, dtype=F32)))
    idx = jnp.arange(RET_CHUNK, dtype=F32)
    rel = idx[:, None] - idx[None, :]
    decay = jnp.where(rel[None] >= 0, jnp.exp(log_gamma[:, None, None] * jnp.maximum(rel, 0.0)[None]), 0.0)
    k_decay = jnp.exp(log_gamma[None, :] * (RET_CHUNK - 1.0 - idx)[:, None])
    q_decay = jnp.exp(log_gamma[None, :] * (idx + 1.0)[:, None])
    chunk_decay = jnp.exp(log_gamma * RET_CHUNK)
    k_dec = jnp.broadcast_to(k_decay.T[:, :, None], (RET_HEADS, RET_CHUNK, LANES))
    q_dec = jnp.broadcast_to(q_decay.T[:, :, None], (RET_HEADS, RET_CHUNK, LANES))
    c_dec = jnp.broadcast_to(chunk_decay[:, None, None], (RET_HEADS, 8, LANES))
    return cos_t, sin_t, decay, k_dec, q_dec, c_dec


def _rotary(x, cos_t, sin_t):
    return x * cos_t + pltpu.roll(x, RET_QK_DIM // 2, 1) * sin_t


def _rotary_transpose(dy, cos_t, sin_t):
    return dy * cos_t + pltpu.roll(dy * sin_t, RET_QK_DIM // 2, 1)


def _head_norm(o):
    mu = jnp.mean(o, axis=1, keepdims=True)
    cen = o - mu
    var = jnp.mean(cen * cen, axis=1, keepdims=True)
    rstd = lax.rsqrt(var + LN_EPS)
    return cen * rstd, rstd


def _ret_specs(nc, reverse):
    def n_of(n):
        return (nc - 1 - n) if reverse else n

    q_spec = pl.BlockSpec((RET_CHUNK, RET_QK_WIDTH), lambda n: (n_of(n), 0))
    k_spec = pl.BlockSpec((RET_CHUNK, RET_QK_WIDTH), lambda n: (n_of(n), 1))
    vv = pl.BlockSpec((RET_CHUNK, RET_V_WIDTH), lambda n: (n_of(n), 0))
    pos = pl.BlockSpec((RET_CHUNK, LANES), lambda n: (n_of(n), 0))
    per_head = pl.BlockSpec((RET_HEADS, RET_CHUNK, LANES), lambda n: (0, 0, 0))
    c_dec = pl.BlockSpec((RET_HEADS, 8, LANES), lambda n: (0, 0, 0))
    state = pl.BlockSpec((RET_HEADS, 1, RET_QK_DIM, RET_V_DIM), lambda n: (0, n_of(n), 0, 0))
    return q_spec, k_spec, vv, pos, per_head, c_dec, state


def _qk_cols(h):
    return slice(h * RET_QK_DIM, (h + 1) * RET_QK_DIM)


def _v_cols(h):
    return slice(h * RET_V_DIM, (h + 1) * RET_V_DIM)


def _ret_fwd(h_b, h_c, h_d, tables):
    T = h_b.shape[0]
    nc = T // RET_CHUNK
    q_spec, k_spec, vv, pos, per_head, c_dec, state = _ret_specs(nc, False)

    def body(q_ref, k_ref, v_ref, g_ref, cos_ref, sin_ref, dec_ref, kd_ref, qd_ref, cd_ref,
             y_ref, o_ref, st_ref, state_ref):
        @pl.when(pl.program_id(0) == 0)
        def _():
            state_ref[...] = jnp.zeros_like(state_ref)

        cos_t, sin_t = cos_ref[...], sin_ref[...]
        for h in range(RET_HEADS):
            q = _rotary(q_ref[:, _qk_cols(h)], cos_t, sin_t) * (RET_QK_DIM ** -0.5)
            k = _rotary(k_ref[:, _qk_cols(h)], cos_t, sin_t)
            v = v_ref[:, _v_cols(h)]
            prev = state_ref[h]
            scores = _dot(q.astype(BF16), k.astype(BF16), _NT) * dec_ref[h]
            inner = _dot(scores.astype(BF16), v, _NN)
            cross = _dot((q * qd_ref[h]).astype(BF16), prev.astype(BF16), _NN)
            o = inner + cross
            st_ref[h, 0] = prev
            kv = _dot((k * kd_ref[h]).astype(BF16), v, _TN)
            state_ref[h] = prev * cd_ref[h, 0:1, 0:1] + kv
            o_ref[:, _v_cols(h)] = o
            normed, _ = _head_norm(o)
            gate = g_ref[:, _v_cols(h)]
            y_ref[:, _v_cols(h)] = (gate * jax.nn.sigmoid(gate) * normed).astype(BF16)

    return pl.pallas_call(
        body,
        name="ret_fwd",
        grid=(nc,),
        in_specs=[q_spec, k_spec, vv, vv, pos, pos, per_head, per_head, per_head, c_dec],
        out_specs=[vv, vv, state],
        out_shape=[jax.ShapeDtypeStruct((T, RET_V_WIDTH), BF16),
                   jax.ShapeDtypeStruct((T, RET_V_WIDTH), F32),
                   jax.ShapeDtypeStruct((RET_HEADS, nc, RET_QK_DIM, RET_V_DIM), F32)],
        scratch_shapes=[pltpu.VMEM((RET_HEADS, RET_QK_DIM, RET_V_DIM), F32)],
        compiler_params=_params(("arbitrary",)),
    )(h_b, h_b, h_c, h_d, *tables)


def _ret_bwd(d_y, o_pre, states, h_b, h_c, h_d, tables, rider=None):
    T = h_b.shape[0]
    nc = T // RET_CHUNK
    q_spec, k_spec, vv, pos, per_head, c_dec, state = _ret_specs(nc, True)
    n_ride = rider.n if rider is not None else 0

    def body(*refs):
        (dy_ref, o_ref, st_ref, q_ref, k_ref, v_ref, g_ref, cos_ref, sin_ref, dec_ref, kd_ref, qd_ref,
         cd_ref) = refs[:13]
        dq_ref, dk_ref, dv_ref, dg_ref = refs[13 + n_ride:17 + n_ride]
        carry_ref = refs[17 + 2 * n_ride]
        ids = [pl.program_id(0)]
        if rider is not None:
            ride = (refs[13:13 + n_ride], refs[17 + n_ride:17 + 2 * n_ride], refs[-3:])
            rider.start_at_first(ids, ride)

        @pl.when(ids[0] == 0)
        def _():
            carry_ref[...] = jnp.zeros_like(carry_ref)

        cos_t, sin_t = cos_ref[...], sin_ref[...]
        scale = RET_QK_DIM ** -0.5
        for h in range(RET_HEADS):
            q = _rotary(q_ref[:, _qk_cols(h)], cos_t, sin_t) * scale
            k = _rotary(k_ref[:, _qk_cols(h)], cos_t, sin_t)
            v = v_ref[:, _v_cols(h)]
            decay, k_dec, q_dec = dec_ref[h], kd_ref[h], qd_ref[h]
            chunk_decay = cd_ref[h, 0:1, 0:1]
            state = st_ref[h, 0].astype(BF16)
            later = carry_ref[h]
            later_b = later.astype(BF16)

            gate = g_ref[:, _v_cols(h)]
            sig = jax.nn.sigmoid(gate)
            silu = gate * sig
            normed, rstd = _head_norm(o_ref[:, _v_cols(h)])
            d_y = dy_ref[:, _v_cols(h)]
            dg_ref[:, _v_cols(h)] = (d_y * normed * (sig * (1.0 + gate * (1.0 - sig)))).astype(BF16)
            d_n = d_y * silu
            d_o = rstd * (d_n - jnp.mean(d_n, axis=1, keepdims=True)
                          - normed * jnp.mean(d_n * normed, axis=1, keepdims=True))
            d_ob = d_o.astype(BF16)

            qb, kb = q.astype(BF16), k.astype(BF16)
            qd_b, kd_b = (q * q_dec).astype(BF16), (k * k_dec).astype(BF16)
            scores = _dot(qb, kb, _NT) * decay
            d_scores = (_dot(d_ob, v, _NT) * decay).astype(BF16)
            dq = _dot(d_scores, kb, _NN) + _dot(d_ob, state, _NT) * q_dec
            dk = _dot(d_scores, qb, _TN) + _dot(v, later_b, _NT) * k_dec
            dv = _dot(scores.astype(BF16), d_ob, _TN) + _dot(kd_b, later_b, _NN)
            carry_ref[h] = _dot(qd_b, d_ob, _TN) + chunk_decay * later
            dq_ref[:, _qk_cols(h)] = _rotary_transpose(dq * scale, cos_t, sin_t).astype(BF16)
            dk_ref[:, _qk_cols(h)] = _rotary_transpose(dk, cos_t, sin_t).astype(BF16)
            dv_ref[:, _v_cols(h)] = dv.astype(BF16)
        if rider is not None:
            rider.wait_at_last(ids, (nc,), ride)

    qk_out = pl.BlockSpec((RET_CHUNK, RET_QK_WIDTH), lambda n: (nc - 1 - n, 0))
    in_specs = [vv, vv, state, q_spec, k_spec, vv, vv, pos, pos, per_head, per_head, per_head, c_dec]
    out_specs = [qk_out, qk_out, vv, vv]
    out_shape = [jax.ShapeDtypeStruct((T, RET_QK_WIDTH), BF16), jax.ShapeDtypeStruct((T, RET_QK_WIDTH), BF16),
                 jax.ShapeDtypeStruct((T, RET_V_WIDTH), BF16), jax.ShapeDtypeStruct((T, RET_V_WIDTH), BF16)]
    args = (d_y, o_pre, states, h_b, h_b, h_c, h_d) + tuple(tables)
    scratch = [pltpu.VMEM((RET_HEADS, RET_QK_DIM, RET_V_DIM), F32)]
    if rider is not None:
        in_specs, args = in_specs + rider.specs, args + tuple(rider.bufs)
        out_specs, out_shape = out_specs + rider.specs, out_shape + rider.out_shape
        scratch = scratch + rider.scratch
    outs = pl.pallas_call(
        body,
        name="ret_bwd",
        grid=(nc,),
        in_specs=in_specs,
        out_specs=out_specs,
        out_shape=out_shape,
        scratch_shapes=scratch,
        compiler_params=_params(("arbitrary",)),
    )(*args)
    return outs[0], outs[1], outs[2], outs[3], list(outs[4:])


def _row_tile(T):
    return _pick(T, (256, 128))


def _gate_mix_tiles(y_ret, h_e, b_gate, y_sb):
    gates = jax.nn.sigmoid(h_e + b_gate)
    return y_ret, gates[:, :D_MODEL] * y_sb + gates[:, D_MODEL:] * y_ret


def _col_sum_update(acc_ref, val, first):
    part = jnp.sum(val.reshape(val.shape[0] // 8, 8, val.shape[1]), axis=0)

    @pl.when(first)
    def _():
        acc_ref[...] = part

    @pl.when(jnp.logical_not(first))
    def _():
        acc_ref[...] += part


def _gate_mix_bwd(d_mix, h_e, b_gate, y_sb, y_ret):
    T = h_e.shape[0]
    tr = _row_tile(T)
    steps = T // tr

    def body(dm_ref, g_ref, b_ref, ys_ref, yr_ref, dys_ref, dyr_ref, de_ref, db_ref, acc_ref):
        i = pl.program_id(0)
        dm = dm_ref[...]
        gates = jax.nn.sigmoid(g_ref[...] + b_ref[...])
        g0, g1 = gates[:, :D_MODEL], gates[:, D_MODEL:]
        dys_ref[...] = (dm * g0).astype(BF16)
        dyr_ref[...] = (dm * g1).astype(BF16)
        de = jnp.concatenate([dm * ys_ref[...] * g0 * (1.0 - g0), dm * yr_ref[...] * g1 * (1.0 - g1)], axis=1)
        de_ref[...] = de.astype(BF16)
        _col_sum_update(acc_ref, de, i == 0)

        @pl.when(i == steps - 1)
        def _():
            db_ref[...] = jnp.sum(acc_ref[...], axis=0, keepdims=True)

    row = pl.BlockSpec((tr, D_MODEL), lambda i: (i, 0))
    wide = pl.BlockSpec((tr, 2 * D_MODEL), lambda i: (i, 0))
    vec = pl.BlockSpec((1, 2 * D_MODEL), lambda i: (0, 0))
    return pl.pallas_call(
        body,
        name="gate_mix_bwd",
        grid=(steps,),
        in_specs=[row, wide, vec, row, row],
        out_specs=[row, row, wide, vec],
        out_shape=[jax.ShapeDtypeStruct((T, D_MODEL), BF16), jax.ShapeDtypeStruct((T, D_MODEL), BF16),
                   jax.ShapeDtypeStruct((T, 2 * D_MODEL), BF16), jax.ShapeDtypeStruct((1, 2 * D_MODEL), F32)],
        scratch_shapes=[pltpu.VMEM((8, 2 * D_MODEL), F32)],
        compiler_params=_params(("arbitrary",)),
    )(d_mix, h_e, b_gate, y_sb, y_ret)


def _ln_stats(u):
    mu = jnp.mean(u, axis=1, keepdims=True)
    cen = u - mu
    var = jnp.mean(cen * cen, axis=1, keepdims=True)
    rstd = lax.rsqrt(var + LN_EPS)
    return cen * rstd, rstd


def _ln_input_grad(d_out, gain, xhat, rstd):
    d_hat = d_out * gain
    return rstd * (d_hat - jnp.mean(d_hat, axis=1, keepdims=True)
                   - xhat * jnp.mean(d_hat * xhat, axis=1, keepdims=True))


def _ln_tiles(sub, x_prev, gain, bias):
    xhat, rstd = _ln_stats(DN_ALPHA * x_prev + sub)
    out = xhat * gain + bias
    return out, out, xhat, rstd


def _ln_bwd(d_out, xhat, rstd, gain, name):
    T = d_out.shape[0]
    tr = _row_tile(T)
    steps = T // tr

    def body(d_ref, xh_ref, rs_ref, g_ref, du_ref, dub_ref, dg_ref, db_ref, accg_ref, accb_ref):
        i = pl.program_id(0)
        d_o, xh = d_ref[...], xh_ref[...]
        du = _ln_input_grad(d_o, g_ref[...], xh, rs_ref[...])
        du_ref[...] = du
        dub_ref[...] = du.astype(BF16)
        _col_sum_update(accg_ref, d_o * xh, i == 0)
        _col_sum_update(accb_ref, d_o, i == 0)

        @pl.when(i == steps - 1)
        def _():
            dg_ref[...] = jnp.sum(accg_ref[...], axis=0, keepdims=True)
            db_ref[...] = jnp.sum(accb_ref[...], axis=0, keepdims=True)

    row = pl.BlockSpec((tr, D_MODEL), lambda i: (i, 0))
    vec = pl.BlockSpec((1, D_MODEL), lambda i: (0, 0))
    return pl.pallas_call(
        body,
        name=name,
        grid=(steps,),
        in_specs=[row, row, pl.BlockSpec((tr, 1), lambda i: (i, 0)), vec],
        out_specs=[row, row, vec, vec],
        out_shape=[jax.ShapeDtypeStruct((T, D_MODEL), F32), jax.ShapeDtypeStruct((T, D_MODEL), BF16),
                   jax.ShapeDtypeStruct((1, D_MODEL), F32), jax.ShapeDtypeStruct((1, D_MODEL), F32)],
        scratch_shapes=[pltpu.VMEM((8, D_MODEL), F32), pltpu.VMEM((8, D_MODEL), F32)],
        compiler_params=_params(("arbitrary",)),
    )(d_out, xhat, rstd, gain)


def _ln_loss(x_prev, sub, gain, bias, target):
    T = x_prev.shape[0]
    tr = _row_tile(T)
    steps = T // tr

    def body(x_ref, s_ref, g_ref, b_ref, t_ref, loss_ref, du_ref, dub_ref, dg_ref, db_ref, accl_ref, accg_ref,
             accb_ref):
        i = pl.program_id(0)
        gain_v = g_ref[...]
        xhat, rstd = _ln_stats(DN_ALPHA * x_ref[...] + s_ref[...])
        diff = xhat * gain_v + b_ref[...] - t_ref[...]
        d_o = diff * (1.0 / D_MODEL)
        du = _ln_input_grad(d_o, gain_v, xhat, rstd)
        du_ref[...] = du
        dub_ref[...] = du.astype(BF16)
        _col_sum_update(accl_ref, diff * diff, i == 0)
        _col_sum_update(accg_ref, d_o * xhat, i == 0)
        _col_sum_update(accb_ref, d_o, i == 0)

        @pl.when(i == steps - 1)
        def _():
            per_col = jnp.sum(accl_ref[...], axis=0, keepdims=True)
            loss_ref[...] = jnp.sum(per_col, axis=1, keepdims=True) * (0.5 / D_MODEL)
            dg_ref[...] = jnp.sum(accg_ref[...], axis=0, keepdims=True)
            db_ref[...] = jnp.sum(accb_ref[...], axis=0, keepdims=True)

    row = pl.BlockSpec((tr, D_MODEL), lambda i: (i, 0))
    vec = pl.BlockSpec((1, D_MODEL), lambda i: (0, 0))
    return pl.pallas_call(
        body,
        name="ln3_loss",
        grid=(steps,),
        in_specs=[row, row, vec, vec, row],
        out_specs=[pl.BlockSpec((1, 1), lambda i: (0, 0)), row, row, vec, vec],
        out_shape=[jax.ShapeDtypeStruct((1, 1), F32), jax.ShapeDtypeStruct((T, D_MODEL), F32),
                   jax.ShapeDtypeStruct((T, D_MODEL), BF16),
                   jax.ShapeDtypeStruct((1, D_MODEL), F32), jax.ShapeDtypeStruct((1, D_MODEL), F32)],
        scratch_shapes=[pltpu.VMEM((8, D_MODEL), F32)] * 3,
        compiler_params=_params(("arbitrary",)),
    )(x_prev, sub, gain, bias, target)


def _mem_probs(q_h, k_h):
    s = _dot(q_h, k_h, _NT) * (MEM_HEAD_DIM ** -0.5)
    e = jnp.exp(s - jnp.max(s, axis=1, keepdims=True))
    return e / jnp.sum(e, axis=1, keepdims=True)


def _xattn_fwd(q, kv):
    T, mem_len = q.shape[0], kv.shape[0]
    tq = _pick(T, (512, 256, 128))

    def body(q_ref, kv_ref, o_ref):
        for h in range(MEM_HEADS):
            cols = slice(h * MEM_HEAD_DIM, (h + 1) * MEM_HEAD_DIM)
            vcols = slice(D_MODEL + h * MEM_HEAD_DIM, D_MODEL + (h + 1) * MEM_HEAD_DIM)
            p = _mem_probs(q_ref[:, cols], kv_ref[:, cols])
            o_ref[:, cols] = _dot(p.astype(BF16), kv_ref[:, vcols], _NN).astype(BF16)

    return pl.pallas_call(
        body,
        name="xattn_fwd",
        grid=(T // tq,),
        in_specs=[pl.BlockSpec((tq, D_MODEL), lambda i: (i, 0)),
                  pl.BlockSpec((mem_len, 2 * D_MODEL), lambda i: (0, 0))],
        out_specs=pl.BlockSpec((tq, D_MODEL), lambda i: (i, 0)),
        out_shape=jax.ShapeDtypeStruct((T, D_MODEL), BF16),
        compiler_params=_params(("parallel",)),
    )(q, kv)


def _xattn_bwd(q, kv, d_o):
    T, mem_len = q.shape[0], kv.shape[0]
    tq = _pick(T, (512, 256, 128))

    def body(q_ref, kv_ref, do_ref, dq_ref, dkv_ref):
        @pl.when(pl.program_id(0) == 0)
        def _():
            dkv_ref[...] = jnp.zeros_like(dkv_ref)

        for h in range(MEM_HEADS):
            cols = slice(h * MEM_HEAD_DIM, (h + 1) * MEM_HEAD_DIM)
            vcols = slice(D_MODEL + h * MEM_HEAD_DIM, D_MODEL + (h + 1) * MEM_HEAD_DIM)
            q_h, k_h, do_h = q_ref[:, cols], kv_ref[:, cols], do_ref[:, cols]
            p = _mem_probs(q_h, k_h)
            dp = _dot(do_h, kv_ref[:, vcols], _NT)
            ds = p * (dp - jnp.sum(dp * p, axis=1, keepdims=True))
            dsb = (ds * (MEM_HEAD_DIM ** -0.5)).astype(BF16)
            dq_ref[:, cols] = _dot(dsb, k_h, _NN).astype(BF16)
            dkv_ref[:, cols] += _dot(dsb, q_h, _TN)
            dkv_ref[:, vcols] += _dot(p.astype(BF16), do_h, _TN)

    row = pl.BlockSpec((tq, D_MODEL), lambda i: (i, 0))
    full = pl.BlockSpec((mem_len, 2 * D_MODEL), lambda i: (0, 0))
    return pl.pallas_call(
        body,
        name="xattn_bwd",
        grid=(T // tq,),
        in_specs=[row, full, row],
        out_specs=[row, full],
        out_shape=[jax.ShapeDtypeStruct((T, D_MODEL), BF16), jax.ShapeDtypeStruct((mem_len, 2 * D_MODEL), F32)],
        compiler_params=_params(("arbitrary",)),
    )(q, kv, d_o)


def _swiglu_tiles(f):
    a, b = f[:, :FFN_HIDDEN], f[:, FFN_HIDDEN:]
    return f, a * jax.nn.sigmoid(a) * b


def _swiglu_grad_tiles(d_hidden, f):
    a, b = f[:, :FFN_HIDDEN], f[:, FFN_HIDDEN:]
    sig = jax.nn.sigmoid(a)
    return (jnp.concatenate([d_hidden * b * (sig * (1.0 + a * (1.0 - sig))), d_hidden * (a * sig)], axis=1),)


def _local_step(x, mem, w_in, small, target, fetch_rest, ship):
    T = x.shape[0]
    tables = _ret_tables(T)
    xb, memb = x.astype(BF16), mem.astype(BF16)

    h_a = _mm(xb, w_in[:, 0:1536], mode="nn", out_dtype=BF16, name="proj_sb")
    h_b = _mm(xb, w_in[:, 1536:2560], mode="nn", out_dtype=F32, name="proj_ret_qk")
    h_c = _mm(xb, w_in[:, 2560:3584], mode="nn", out_dtype=BF16, name="proj_ret_v")
    h_d = _mm(xb, w_in[:, 3584:4608], mode="nn", out_dtype=F32, name="proj_ret_g")
    h_e = _mm(xb, w_in[:, 4608:6656], mode="nn", out_dtype=F32, name="proj_gate")
    (a_sb, r_mat, _), w = fetch_rest(lambda rider: _sb_fwd(h_a, rider))
    y_gated, o_pre, states = _ret_fwd(h_b, h_c, h_d, tables)
    y_sb = _mm(a_sb, w["w_sb_o"], mode="nn", out_dtype=F32, name="sb_out")
    row_f32, row_bf16 = (D_MODEL, F32), (D_MODEL, BF16)
    ln_outs = [row_f32, row_bf16, row_f32, (1, F32)]
    y_ret, mix_in = _mm_fused(y_gated, w["w_ret_o"], mode="nn", name="ret_out", extras=[h_e, small["b_gate"], y_sb],
                              outs=[row_f32, row_bf16], epilogue=_gate_mix_tiles)
    x1, x1b, xhat1, rstd1 = _mm_fused(mix_in, w["w_mix_o"], mode="nn", name="mix_out",
                                      extras=[x, small["ln1_g"], small["ln1_b"]], outs=ln_outs, epilogue=_ln_tiles)
    q_m = _mm(x1b, w["w_mem_q"], mode="nn", out_dtype=BF16, name="mem_q")
    kv_m = _mm(memb, w["w_mem_kv"], mode="nn", out_dtype=BF16, name="mem_kv")
    o_m = _xattn_fwd(q_m, kv_m)
    x2, x2b, xhat2, rstd2 = _mm_fused(o_m, w["w_mem_o"], mode="nn", name="mem_out",
                                      extras=[x1, small["ln2_g"], small["ln2_b"]], outs=ln_outs, epilogue=_ln_tiles)
    f, hidden = _mm_fused(x2b, w["w_ffn_in"], mode="nn", name="ffn_in", extras=[],
                          outs=[(2 * FFN_HIDDEN, F32), (FFN_HIDDEN, BF16)], epilogue=_swiglu_tiles)
    ff = _mm(hidden, w["w_ffn_out"], mode="nn", out_dtype=F32, name="ffn_out")
    loss, du3, du3b, d_ln3_g, d_ln3_b = _ln_loss(x2, ff, small["ln3_g"], small["ln3_b"], target)

    g_ffn_out = _mm(hidden, du3b, mode="tn", out_dtype=F32, name="g_ffn_out")
    (d_f,) = _mm_fused(du3b, w["w_ffn_out"], mode="nt", name="d_hidden", extras=[f],
                       outs=[(2 * FFN_HIDDEN, BF16)], epilogue=_swiglu_grad_tiles)
    g_ffn_in = _mm(x2b, d_f, mode="tn", out_dtype=F32, name="g_ffn_in")
    d_x2 = _mm(d_f, w["w_ffn_in"], mode="nt", out_dtype=F32, name="d_x2", res=du3, res_scale=DN_ALPHA)
    du2, du2b, d_ln2_g, d_ln2_b = _ln_bwd(d_x2, xhat2, rstd2, small["ln2_g"], "ln2_bwd")
    g_mem_o = _mm(o_m, du2b, mode="tn", out_dtype=F32, name="g_mem_o")
    d_om = _mm(du2b, w["w_mem_o"], mode="nt", out_dtype=BF16, name="d_om")
    d_qm, d_kvm = _xattn_bwd(q_m, kv_m, d_om)
    g_mem_q = _mm(x1b, d_qm, mode="tn", out_dtype=F32, name="g_mem_q")
    g_mem_kv = _mm(memb, d_kvm.astype(BF16), mode="tn", out_dtype=F32, name="g_mem_kv")
    d_x1 = _mm(d_qm, w["w_mem_q"], mode="nt", out_dtype=F32, name="d_x1", res=du2, res_scale=DN_ALPHA)
    du1, du1b, d_ln1_g, d_ln1_b = _ln_bwd(d_x1, xhat1, rstd1, small["ln1_g"], "ln1_bwd")
    g_mix_o = _mm(mix_in, du1b, mode="tn", out_dtype=F32, name="g_mix_o")
    d_mix_in = _mm(du1b, w["w_mix_o"], mode="nt", out_dtype=F32, name="d_mix_in")
    d_ysb, d_yret, d_e, d_b_gate = _gate_mix_bwd(d_mix_in, h_e, small["b_gate"], y_sb, y_ret)
    g_sb_o = _mm(a_sb, d_ysb, mode="tn", out_dtype=F32, name="g_sb_o")
    g_ret_o = _mm(y_gated, d_yret, mode="tn", out_dtype=F32, name="g_ret_o")
    d_asb = _mm(d_ysb, w["w_sb_o"], mode="nt", out_dtype=BF16, name="d_asb")
    d_ygated = _mm(d_yret, w["w_ret_o"], mode="nt", out_dtype=F32, name="d_ygated")
    small_grads = {"b_gate": d_b_gate, "ln1_g": d_ln1_g, "ln1_b": d_ln1_b, "ln2_g": d_ln2_g, "ln2_b": d_ln2_b,
                   "ln3_g": d_ln3_g, "ln3_b": d_ln3_b}
    early = {"w_ffn_out": g_ffn_out, "w_ffn_in": g_ffn_in, "w_mem_o": g_mem_o, "w_mem_q": g_mem_q,
             "w_mem_kv": g_mem_kv, "small": small_grads}
    d_rq, d_rk, d_c, d_d = ship(early, lambda rider: _ret_bwd(d_ygated, o_pre, states, h_b, h_c, h_d, tables, rider))
    d_q, d_k, d_v = _sb_bwd(h_a, d_asb, r_mat)
    d_h = jnp.concatenate([d_q, d_k.astype(BF16), d_v.astype(BF16), d_rq, d_rk, d_c, d_d, d_e], axis=1)
    (g_in,) = ship({"w_mix_o": g_mix_o, "w_sb_o": g_sb_o, "w_ret_o": g_ret_o},
                   lambda rider: _mm_host(xb, d_h, mode="tn", out_dtype=F32, name="g_in", rider=rider))
    (d_x,) = ship({"w_in": g_in},
                  lambda rider: _mm_host(d_h, w_in, mode="nt", out_dtype=F32, name="d_x", res=du1, res_scale=DN_ALPHA,
                                         rider=rider))
    return loss, d_x


def _adamw_math(w, g, m, v):
    m = ADAM_B1 * m + (1.0 - ADAM_B1) * g
    v = ADAM_B2 * v + (1.0 - ADAM_B2) * jnp.square(g)
    m_hat = m / (1.0 - ADAM_B1 ** ADAM_STEP)
    v_hat = v / (1.0 - ADAM_B2 ** ADAM_STEP)
    delta = -ADAM_LR * (m_hat / (jnp.sqrt(v_hat) + ADAM_EPS) + ADAM_WD * w)
    return delta, m, v


def _adamw(parts, w, m, v, name):
    R, C = w.shape
    tr = max(t for t in range(16, min(R, 256) + 1, 16) if R % t == 0) if R >= 16 else R

    def body(p_ref, w_ref, m_ref, v_ref, g_ref, d_ref, nm_ref, nv_ref):
        g = p_ref[0].astype(F32)
        for j in range(1, N_DEV):
            g = g + p_ref[j].astype(F32)
        delta, nm, nv = _adamw_math(w_ref[...], g, m_ref[...], v_ref[...])
        g_ref[...] = g
        d_ref[...] = delta
        nm_ref[...] = nm
        nv_ref[...] = nv

    blk = pl.BlockSpec((tr, C), lambda i: (i, 0))
    out = jax.ShapeDtypeStruct((R, C), F32)
    return pl.pallas_call(
        body,
        name=name,
        grid=(R // tr,),
        in_specs=[pl.BlockSpec((N_DEV, tr, C), lambda i: (0, i, 0)), blk, blk, blk],
        out_specs=[blk] * 4,
        out_shape=[out] * 4,
        compiler_params=_params(("parallel",)),
    )(parts, w, m, v)


_SHARD_AXIS = {"w_in": 1, "w_sb_o": 1, "w_ret_o": 0, "w_mix_o": 0, "w_mem_q": 0, "w_mem_kv": 1, "w_mem_o": 0,
               "w_ffn_in": 1, "w_ffn_out": 0}
_MATRICES = tuple(_SHARD_AXIS)
_SMALL = ("b_gate", "ln1_g", "ln1_b", "ln2_g", "ln2_b", "ln3_g", "ln3_b")
_WEIGHT_ORDER = ("w_in", "b_gate", "w_sb_o", "w_ret_o", "w_mix_o", "ln1_g", "ln1_b", "w_mem_q", "w_mem_kv", "w_mem_o",
                 "ln2_g", "ln2_b", "w_ffn_in", "w_ffn_out", "ln3_g", "ln3_b")


def _assemble(name, gathered):
    if _SHARD_AXIS[name] == 0:
        return gathered.reshape(-1, gathered.shape[2])
    return jnp.transpose(gathered, (1, 0, 2)).reshape(gathered.shape[1], -1)


def _to_slots(name, full):
    if _SHARD_AXIS[name] == 0:
        return full.reshape(N_DEV, full.shape[0] // N_DEV, full.shape[1])
    return jnp.transpose(full.reshape(full.shape[0], N_DEV, full.shape[1] // N_DEV), (1, 0, 2))


def _pack_small(vals):
    return jnp.concatenate([vals["b_gate"].reshape(2, D_MODEL)] + [vals[n] for n in _SMALL[1:]], axis=0)


def _unpack_small(packed):
    out = {"b_gate": packed[0:2].reshape(1, 2 * D_MODEL)}
    for i, n in enumerate(_SMALL[1:]):
        out[n] = packed[2 + i:3 + i]
    return out


def kernel(x, mem, w_in, b_gate, w_sb_o, w_ret_o, w_mix_o, ln1_g, ln1_b, w_mem_q, w_mem_kv, w_mem_o, ln2_g, ln2_b, w_ffn_in, w_ffn_out, ln3_g, ln3_b, loss_target, m_w_in, m_b_gate, m_w_sb_o, m_w_ret_o, m_w_mix_o, m_ln1_g, m_ln1_b, m_w_mem_q, m_w_mem_kv, m_w_mem_o, m_ln2_g, m_ln2_b, m_w_ffn_in, m_w_ffn_out, m_ln3_g, m_ln3_b, v_w_in, v_b_gate, v_w_sb_o, v_w_ret_o, v_w_mix_o, v_ln1_g, v_ln1_b, v_w_mem_q, v_w_mem_kv, v_w_mem_o, v_ln2_g, v_ln2_b, v_w_ffn_in, v_w_ffn_out, v_ln3_g, v_ln3_b):
    weights = dict(w_in=w_in, b_gate=b_gate, w_sb_o=w_sb_o, w_ret_o=w_ret_o, w_mix_o=w_mix_o, ln1_g=ln1_g, ln1_b=ln1_b,
                   w_mem_q=w_mem_q, w_mem_kv=w_mem_kv, w_mem_o=w_mem_o, ln2_g=ln2_g, ln2_b=ln2_b, w_ffn_in=w_ffn_in,
                   w_ffn_out=w_ffn_out, ln3_g=ln3_g, ln3_b=ln3_b)
    mom1 = dict(w_in=m_w_in, b_gate=m_b_gate, w_sb_o=m_w_sb_o, w_ret_o=m_w_ret_o, w_mix_o=m_w_mix_o, ln1_g=m_ln1_g,
                ln1_b=m_ln1_b, w_mem_q=m_w_mem_q, w_mem_kv=m_w_mem_kv, w_mem_o=m_w_mem_o, ln2_g=m_ln2_g, ln2_b=m_ln2_b,
                w_ffn_in=m_w_ffn_in, w_ffn_out=m_w_ffn_out, ln3_g=m_ln3_g, ln3_b=m_ln3_b)
    mom2 = dict(w_in=v_w_in, b_gate=v_b_gate, w_sb_o=v_w_sb_o, w_ret_o=v_w_ret_o, w_mix_o=v_w_mix_o, ln1_g=v_ln1_g,
                ln1_b=v_ln1_b, w_mem_q=v_w_mem_q, w_mem_kv=v_w_mem_kv, w_mem_o=v_w_mem_o, ln2_g=v_ln2_g, ln2_b=v_ln2_b,
                w_ffn_in=v_w_ffn_in, w_ffn_out=v_w_ffn_out, ln3_g=v_ln3_g, ln3_b=v_ln3_b)

    (gathered_in,) = _exchange([weights["w_in"][0].astype(BF16)], False, "gather_w_in")
    rest = [n for n in _MATRICES if n != "w_in"]
    received = {}

    def fetch_rest(host):
        res = host(_Rider([weights[n][0].astype(BF16) for n in rest], False))
        return res, {n: _assemble(n, g) for n, g in zip(rest, res[-1])}

    def ship(grads, host):
        names = list(grads)
        bufs = []
        for n in names:
            if n == "small":
                part = _pack_small(grads[n])
                bufs.append(jnp.broadcast_to(part[None], (N_DEV,) + part.shape))
            else:
                bufs.append(_to_slots(n, grads[n]).astype(BF16))
        res = host(_Rider(bufs, True))
        received.update(zip(names, res[-1]))
        return res[:-1]

    small = {n: weights[n] for n in _SMALL}
    loss, d_x = _local_step(x[0], mem[0], _assemble("w_in", gathered_in), small, loss_target[0], fetch_rest, ship)

    new = {}
    for n in _MATRICES:
        new[n] = _adamw(received[n], weights[n][0], mom1[n][0], mom2[n][0], "adamw_" + n)
    packed = _adamw(received["small"], _pack_small({n: weights[n] for n in _SMALL}),
                    _pack_small({n: mom1[n] for n in _SMALL}), _pack_small({n: mom2[n] for n in _SMALL}), "adamw_small")
    small_new = [_unpack_small(p) for p in packed]

    outs = [lax.psum(loss[0, 0], MESH_AXES), d_x[None]]
    for slot in range(4):
        for n in _WEIGHT_ORDER:
            outs.append(new[n][slot][None] if n in new else small_new[slot][n])
    return tuple(outs)
```

```python
import functools
import math

import jax
import jax.numpy as jnp
from jax import lax
from jax.experimental import pallas as pl
from jax.experimental.pallas import tpu as pltpu

F32 = jnp.float32
BF16 = jnp.bfloat16

N_DEV = 8
D_MODEL = 1024
SB_HEAD_DIM = 64
SB_WIDTH = 512
RET_HEADS = 4
RET_QK_DIM = 128
RET_V_DIM = 256
RET_QK_WIDTH = 512
RET_V_WIDTH = 1024
RET_CHUNK = 128
ROPE_BASE = 10000.0
MEM_HEADS = 4
MEM_HEAD_DIM = 256
FFN_HIDDEN = 2816
DN_ALPHA = 2.0 ** 0.25
LN_EPS = 1e-5
ADAM_LR = 0.001
ADAM_B1 = 0.9
ADAM_B2 = 0.999
ADAM_EPS = 1e-08
ADAM_WD = 0.01
ADAM_STEP = 10

VMEM_LIMIT_BYTES = 52 * 1024 * 1024
LANES = 128
SB_KEY_BLOCK = 128
SB_Q_BLOCK = 256
SB_DEAD_LOG = -105.0

MESH_AXES = ("x", "y", "c")


def _pick(dim, prefs):
    for p in prefs:
        if dim % p == 0:
            return p
    return dim


def _params(sem):
    return pltpu.CompilerParams(dimension_semantics=sem, vmem_limit_bytes=VMEM_LIMIT_BYTES)


def _dot(a, b, dims):
    return lax.dot_general(a, b, (dims, ((), ())), preferred_element_type=F32)


_NN = ((1,), (0,))
_NT = ((1,), (1,))
_TN = ((0,), (0,))


def _my_index():
    return 4 * lax.axis_index("x") + 2 * lax.axis_index("y") + lax.axis_index("c")


def _peer(k):
    x, y, c = lax.axis_index("x"), lax.axis_index("y"), lax.axis_index("c")
    bx, by, bc = (k >> 2) & 1, (k >> 1) & 1, k & 1
    px = (1 - x) if bx else x
    py = (1 - y) if by else y
    pc = (1 - c) if bc else c
    return (px, py, pc), 4 * px + 2 * py + pc


class _Rider:
    def __init__(self, bufs, scatter):
        self.bufs, self.scatter, self.n = list(bufs), scatter, len(bufs)
        self.specs = [pl.BlockSpec(memory_space=pl.ANY)] * self.n
        self.out_shape = [jax.ShapeDtypeStruct(b.shape if scatter else (N_DEV,) + b.shape, b.dtype) for b in self.bufs]
        self.scratch = [pltpu.SemaphoreType.DMA((self.n, N_DEV - 1)), pltpu.SemaphoreType.DMA((self.n, N_DEV - 1)),
                        pltpu.SemaphoreType.DMA((self.n,))]

    def _remote(self, ride, a, k, src_ref, slot, to):
        _, dst, (send_sems, recv_sems, _) = ride
        return pltpu.make_async_remote_copy(src_ref=src_ref, dst_ref=dst[a].at[slot], send_sem=send_sems.at[a, k],
                                            recv_sem=recv_sems.at[a, k], device_id=to,
                                            device_id_type=pl.DeviceIdType.MESH)

    def _local(self, ride, a):
        src, dst, (_, _, local_sems) = ride
        me = _my_index()
        return pltpu.make_async_copy(src[a].at[me] if self.scatter else src[a], dst[a].at[me], local_sems.at[a])

    def _direct(self, ride, a):
        src = ride[0]
        me = _my_index()
        out = []
        for k in range(1, N_DEV):
            peer, peer_idx = _peer(k)
            out.append(self._remote(ride, a, k - 1, src[a].at[peer_idx], me, peer))
        return out

    def _two_level(self, ride, a):
        src, dst = ride[0], ride[1]
        x, y, c = lax.axis_index("x"), lax.axis_index("y"), lax.axis_index("c")
        me, sibling = _my_index(), (x, y, 1 - c)
        chips = [(1 - x, y), (x, 1 - y), (1 - x, 1 - y)]
        first = [self._remote(ride, a, 0, src[a], me, sibling)]
        passed, landing = [], [self._remote(ride, a, 0, src[a], me + 1 - 2 * c, sibling)]
        for j, (px, py) in enumerate(chips):
            first.append(self._remote(ride, a, 1 + j, src[a], me, (px, py, c)))
            theirs = 4 * px + 2 * py + c
            passed.append(self._remote(ride, a, 4 + j, dst[a].at[theirs], theirs, sibling))
            landing.append(self._remote(ride, a, 1 + j, src[a], theirs, (px, py, c)))
        for j, (px, py) in enumerate(chips):
            landing.append(self._remote(ride, a, 4 + j, src[a], 4 * px + 2 * py + 1 - c, sibling))
        return first, passed, landing

    def start(self, ride):
        for a in range(self.n):
            self._local(ride, a).start()
            for cp in (self._direct(ride, a) if self.scatter else self._two_level(ride, a)[0]):
                cp.start()

    def finish(self, ride):
        if self.scatter:
            for a in range(self.n):
                for cp in self._direct(ride, a):
                    cp.wait()
                self._local(ride, a).wait()
            return
        levels = [self._two_level(ride, a) for a in range(self.n)]
        for first, passed, landing in levels:
            for j, cp in enumerate(passed):
                landing[1 + j].wait_recv()
                cp.start()
        for a, (first, passed, landing) in enumerate(levels):
            landing[0].wait_recv()
            for cp in landing[4:]:
                cp.wait_recv()
            for cp in first + passed:
                cp.wait_send()
            self._local(ride, a).wait()

    def start_at_first(self, ids, ride):
        first = functools.reduce(jnp.logical_and, [i == 0 for i in ids])

        @pl.when(first)
        def _():
            self.start(ride)

    def wait_at_last(self, ids, grid, ride):
        last = functools.reduce(jnp.logical_and, [i == g - 1 for i, g in zip(ids, grid)])

        @pl.when(last)
        def _():
            self.finish(ride)


def _exchange(bufs, scatter, name):
    rider = _Rider(bufs, scatter)

    def body(*refs):
        ride = (refs[:rider.n], refs[rider.n:2 * rider.n], refs[2 * rider.n:])
        rider.start(ride)
        rider.finish(ride)

    return pl.pallas_call(
        body,
        name=name,
        in_specs=rider.specs,
        out_specs=rider.specs,
        out_shape=rider.out_shape,
        scratch_shapes=rider.scratch,
    )(*rider.bufs)


MM_RESIDENT_B_BYTES = 14 * 1024 * 1024
MM_A_TILE_BYTES = 4 * 1024 * 1024
MM_OUT_TILE_BYTES = 6 * 1024 * 1024


def _mm_tiles(mode, M, N, K, a_bytes, out_bytes):
    if mode != "tn" and K * N * 2 <= MM_RESIDENT_B_BYTES:
        for tm in (1024, 512, 256, 128):
            if M % tm == 0 and tm * K * a_bytes <= MM_A_TILE_BYTES and tm * N * out_bytes <= MM_OUT_TILE_BYTES:
                return tm, N, K
    if mode == "tn":
        return (_pick(M, (1024, 1408, 512, 256, 128)), _pick(N, (1024, 1664, 1408, 512, 256, 128)),
                _pick(K, (2048, 1024, 512, 256, 128)))
    return _pick(M, (1024, 512, 256, 128)), _pick(N, (512, 256, 128)), _pick(K, (1024, 512, 256, 128))


def _mm(a, b, *, mode, out_dtype, name, res=None, res_scale=1.0, rider=None):
    if mode == "nn":
        (M, K), (K2, N) = a.shape, b.shape
    elif mode == "nt":
        (M, K), (N, K2) = a.shape, b.shape
    else:
        (K, M), (K2, N) = a.shape, b.shape
    assert K == K2, (a.shape, b.shape, mode)
    out_bytes = jnp.dtype(out_dtype).itemsize + (4 if res is not None else 0)
    tm, tn, tk = _mm_tiles(mode, M, N, K, a.dtype.itemsize, out_bytes)
    grid = (M // tm, N // tn, K // tk)
    nk = grid[2]
    dims = {"nn": _NN, "nt": _NT, "tn": _TN}[mode]
    n_in = 2 + (res is not None)
    n_ride = rider.n if rider is not None else 0

    def body(*refs):
        a_ref, b_ref = refs[:2]
        r_ref = refs[2] if res is not None else None
        o_ref = refs[n_in + n_ride]
        rest = refs[n_in + 2 * n_ride + 1:]
        acc_ref = rest[0] if nk > 1 else None
        ids = [pl.program_id(d) for d in range(3)]
        if rider is not None:
            ride = (refs[n_in:n_in + n_ride], refs[n_in + n_ride + 1:n_in + 2 * n_ride + 1], rest[-3:])
            rider.start_at_first(ids, ride)
        part = _dot(a_ref[...].astype(BF16), b_ref[...].astype(BF16), dims)

        def finish(total):
            if r_ref is not None:
                total = total + res_scale * r_ref[...]
            o_ref[...] = total.astype(out_dtype)

        if nk == 1:
            finish(part)
        else:
            k = ids[2]

            @pl.when(k == 0)
            def _():
                acc_ref[...] = part

            @pl.when(k > 0)
            def _():
                acc_ref[...] += part

            @pl.when(k == nk - 1)
            def _():
                finish(acc_ref[...])

        if rider is not None:
            rider.wait_at_last(ids, grid, ride)

    if mode == "nn":
        a_spec = pl.BlockSpec((tm, tk), lambda i, j, k: (i, k))
        b_spec = pl.BlockSpec((tk, tn), lambda i, j, k: (k, j))
    elif mode == "nt":
        a_spec = pl.BlockSpec((tm, tk), lambda i, j, k: (i, k))
        b_spec = pl.BlockSpec((tn, tk), lambda i, j, k: (j, k))
    else:
        a_spec = pl.BlockSpec((tk, tm), lambda i, j, k: (k, i))
        b_spec = pl.BlockSpec((tk, tn), lambda i, j, k: (k, j))
    o_spec = pl.BlockSpec((tm, tn), lambda i, j, k: (i, j))
    in_specs = [a_spec, b_spec] + ([o_spec] if res is not None else [])
    args = (a, b) + ((res,) if res is not None else ())
    out_specs, out_shape = [o_spec], [jax.ShapeDtypeStruct((M, N), out_dtype)]
    scratch = [pltpu.VMEM((tm, tn), F32)] if nk > 1 else []
    sem = ("parallel", "parallel", "arbitrary")
    if rider is not None:
        in_specs, args = in_specs + rider.specs, args + tuple(rider.bufs)
        out_specs, out_shape = out_specs + rider.specs, out_shape + rider.out_shape
        scratch = scratch + rider.scratch
        sem = ("arbitrary",) * 3
    outs = pl.pallas_call(
        body,
        name=name,
        grid=grid,
        in_specs=in_specs,
        out_specs=out_specs,
        out_shape=out_shape,
        scratch_shapes=scratch,
        compiler_params=_params(sem),
    )(*args)
    return outs[0] if rider is None else (outs[0], list(outs[1:]))


def _mm_host(a, b, *, rider, **kw):
    out = _mm(a, b, rider=rider, **kw)
    return out if rider is not None else (out, [])


MM_FUSED_MARGIN_BYTES = 10 * 1024 * 1024
MM_FUSED_MAX_ROWS = 512


def _mm_fused(a, b, *, mode, name, extras, outs, epilogue):
    if mode == "nn":
        (M, K), (K2, N) = a.shape, b.shape
        b_dims = _NN
    else:
        (M, K), (N, K2) = a.shape, b.shape
        b_dims = _NT
    assert K == K2, (a.shape, b.shape, mode)
    rows = [e for e in extras if e.shape[0] == M]
    per_row = 2 * (K * a.dtype.itemsize + sum(e.shape[1] * e.dtype.itemsize for e in rows)
                   + sum(c * jnp.dtype(d).itemsize for c, d in outs)) + 2 * N * 4
    budget = VMEM_LIMIT_BYTES - K * N * 2 - MM_FUSED_MARGIN_BYTES
    tm = next(t for t in (512, 256, 128, 64, 32, 16) if t <= MM_FUSED_MAX_ROWS and M % t == 0 and t * per_row <= budget)
    n_x = len(extras)

    def body(*refs):
        a_ref, b_ref = refs[:2]
        x_refs, o_refs = refs[2:2 + n_x], refs[2 + n_x:]
        prod = _dot(a_ref[...].astype(BF16), b_ref[...], b_dims)
        tiles = epilogue(prod, *[r[...] for r in x_refs])
        for o_ref, t in zip(o_refs, tiles):
            o_ref[...] = t.astype(o_ref.dtype)

    in_specs = [pl.BlockSpec((tm, K), lambda i: (i, 0)),
                pl.BlockSpec(b.shape, lambda i: (0, 0), pipeline_mode=pl.Buffered(1))]
    for e in extras:
        in_specs.append(pl.BlockSpec((tm, e.shape[1]), lambda i: (i, 0)) if e.shape[0] == M
                        else pl.BlockSpec(e.shape, lambda i: (0, 0)))
    return pl.pallas_call(
        body,
        name=name,
        grid=(M // tm,),
        in_specs=in_specs,
        out_specs=[pl.BlockSpec((tm, c), lambda i: (i, 0)) for c, _ in outs],
        out_shape=[jax.ShapeDtypeStruct((M, c), d) for c, d in outs],
        compiler_params=_params(("parallel",)),
    )(a, b, *extras)


def _pair_rows(blk, lane_is_a):
    zero = jnp.zeros_like(blk)
    return jnp.concatenate([jnp.where(lane_is_a, blk, zero), jnp.where(lane_is_a, zero, blk)], axis=0)


def _pair_scan_matrix(strict_after):
    r = lax.broadcasted_iota(jnp.int32, (4 * LANES, 2 * LANES), 0) & (2 * LANES - 1)
    c = lax.broadcasted_iota(jnp.int32, (4 * LANES, 2 * LANES), 1)
    same = (r >= LANES) == (c >= LANES)
    rr, cc = r & (LANES - 1), c & (LANES - 1)
    tri = (rr > cc) if strict_after else (rr < cc)
    return jnp.where(same & tri, 1.0, 0.0).astype(BF16)


def _split_dot(val, mat):
    hi = val.astype(BF16)
    lo = (val - hi.astype(F32)).astype(BF16)
    return _dot(jnp.concatenate([hi, lo], axis=1), mat, _NN)


def _pair_cols(col_a, col_b, rows):
    return jnp.concatenate([jnp.broadcast_to(col_a, (rows, LANES)), jnp.broadcast_to(col_b, (rows, LANES))], axis=1)


def _sb_scores(q, kk, mask):
    z = _dot(q, kk, _NT) * (SB_HEAD_DIM ** -0.5)
    t = jnp.log1p(jnp.exp(-jnp.abs(z)))
    log_beta = jnp.minimum(z, 0.0) - t
    log_rem = -jnp.maximum(z, 0.0) - t
    if mask is not None:
        log_rem = jnp.where(mask, log_rem, 0.0)
    return log_beta, log_rem


def _sb_fwd(h_a, rider=None):
    T = h_a.shape[0]
    tq = _pick(T, (SB_Q_BLOCK, SB_KEY_BLOCK))
    nq, per_q = T // tq, tq // SB_KEY_BLOCK
    assert T // SB_KEY_BLOCK <= LANES
    n_ride = rider.n if rider is not None else 0

    def body(*refs):
        q_ref, k_ref, v_ref = refs[:3]
        a_ref, r_ref = refs[3 + n_ride:5 + n_ride]
        ids = [pl.program_id(0), pl.program_id(1)]
        if rider is not None:
            ride = (refs[3:3 + n_ride], refs[5 + n_ride:5 + 2 * n_ride], refs[-3:])
            rider.start_at_first(ids, ride)
        i = ids[1]
        q = q_ref[...]
        lane_is_a = lax.broadcasted_iota(jnp.int32, (SB_KEY_BLOCK, LANES), 1) < SB_HEAD_DIM
        after = _pair_scan_matrix(True)
        row = lax.broadcasted_iota(jnp.int32, (tq, 2 * LANES), 0)
        col = lax.broadcasted_iota(jnp.int32, (tq, 2 * LANES), 1) & (LANES - 1)
        blk_lane = lax.broadcasted_iota(jnp.int32, (tq, LANES), 1)

        def tile(kb, masked, carry):
            acc, ra, rb, rma, rmb = carry
            ks = pl.multiple_of(kb * SB_KEY_BLOCK, SB_KEY_BLOCK)
            kk = _pair_rows(k_ref[pl.ds(ks, SB_KEY_BLOCK), :], lane_is_a)
            vv = _pair_rows(v_ref[pl.ds(ks, SB_KEY_BLOCK), :], lane_is_a)
            mask = ((ks + col) < (i * tq + row)) if masked else None
            log_beta, log_rem = _sb_scores(q, kk, mask)
            later = _split_dot(log_rem, after) + _pair_cols(ra, rb, tq)
            w = jnp.exp(log_beta + later)
            if masked:
                w = jnp.where(mask, w, 0.0)
            acc = acc + _dot(w.astype(BF16), vv, _NN)
            here = (blk_lane == kb).astype(F32)
            rma = rma + ra * here
            rmb = rmb + rb * here
            ra = ra + jnp.sum(log_rem[:, :LANES], axis=1, keepdims=True)
            rb = rb + jnp.sum(log_rem[:, LANES:], axis=1, keepdims=True)
            return acc, ra, rb, rma, rmb

        carry = (jnp.zeros((tq, LANES), F32), jnp.zeros((tq, 1), F32), jnp.zeros((tq, 1), F32),
                 jnp.zeros((tq, LANES), F32), jnp.zeros((tq, LANES), F32))
        for d in range(per_q):
            carry = tile(i * per_q + (per_q - 1 - d), True, carry)
        n_full = i * per_q

        def alive(c):
            return jnp.logical_and(c[0] < n_full, jnp.max(jnp.maximum(c[2], c[3])) > SB_DEAD_LOG)

        def step(c):
            return (c[0] + 1,) + tile(n_full - 1 - c[0], False, c[1:])

        done, acc, ra, rb, rma, rmb = lax.while_loop(alive, step, (jnp.int32(0),) + carry)
        skipped = blk_lane < (n_full - done)
        rma = jnp.where(skipped, ra, rma)
        rmb = jnp.where(skipped, rb, rmb)
        a_ref[...] = acc.astype(BF16)
        r_ref[0] = rma
        r_ref[1] = rmb
        if rider is not None:
            rider.wait_at_last(ids, (4, nq), ride)

    in_specs = [pl.BlockSpec((tq, LANES), lambda p, i: (i, p)),
                pl.BlockSpec((T, LANES), lambda p, i: (0, 4 + p)),
                pl.BlockSpec((T, LANES), lambda p, i: (0, 8 + p))]
    out_specs = [pl.BlockSpec((tq, LANES), lambda p, i: (i, p)),
                 pl.BlockSpec((2, tq, LANES), lambda p, i: (p, i, 0))]
    out_shape = [jax.ShapeDtypeStruct((T, SB_WIDTH), BF16), jax.ShapeDtypeStruct((8, T, LANES), F32)]
    args = (h_a, h_a, h_a)
    if rider is not None:
        in_specs, args = in_specs + rider.specs, args + tuple(rider.bufs)
        out_specs, out_shape = out_specs + rider.specs, out_shape + rider.out_shape
    outs = pl.pallas_call(
        body,
        name="sb_fwd",
        grid=(4, nq),
        in_specs=in_specs,
        out_specs=out_specs,
        out_shape=out_shape,
        scratch_shapes=rider.scratch if rider is not None else [],
        compiler_params=_params(("arbitrary", "arbitrary") if rider is not None else ("parallel", "arbitrary")),
    )(*args)
    return outs[0], outs[1], list(outs[2:])


def _sb_bwd(h_a, d_out, r_mat):
    T = h_a.shape[0]
    tq = _pick(T, (SB_Q_BLOCK, SB_KEY_BLOCK))
    nq, per_q = T // tq, tq // SB_KEY_BLOCK

    def body(q_ref, k_ref, v_ref, do_ref, r_ref, dq_ref, dk_ref, dv_ref):
        i = pl.program_id(1)

        @pl.when(i == 0)
        def _():
            dk_ref[...] = jnp.zeros_like(dk_ref)
            dv_ref[...] = jnp.zeros_like(dv_ref)

        q = q_ref[...]
        d_o = do_ref[...]
        rma, rmb = r_ref[0], r_ref[1]
        lane_is_a = lax.broadcasted_iota(jnp.int32, (SB_KEY_BLOCK, LANES), 1) < SB_HEAD_DIM
        after = _pair_scan_matrix(True)
        before = _pair_scan_matrix(False)
        row = lax.broadcasted_iota(jnp.int32, (tq, 2 * LANES), 0)
        col = lax.broadcasted_iota(jnp.int32, (tq, 2 * LANES), 1) & (LANES - 1)
        blk_lane = lax.broadcasted_iota(jnp.int32, (tq, LANES), 1)

        def tile(kb, masked, carry):
            dq, ca, cb = carry
            ks = pl.multiple_of(kb * SB_KEY_BLOCK, SB_KEY_BLOCK)
            kk = _pair_rows(k_ref[pl.ds(ks, SB_KEY_BLOCK), :], lane_is_a)
            vv = _pair_rows(v_ref[pl.ds(ks, SB_KEY_BLOCK), :], lane_is_a)
            mask = ((ks + col) < (i * tq + row)) if masked else None
            log_beta, log_rem = _sb_scores(q, kk, mask)
            here = (blk_lane == kb).astype(F32)
            ra = jnp.sum(rma * here, axis=1, keepdims=True)
            rb = jnp.sum(rmb * here, axis=1, keepdims=True)
            later = _split_dot(log_rem, after) + _pair_cols(ra, rb, tq)
            w = jnp.exp(log_beta + later)
            if masked:
                w = jnp.where(mask, w, 0.0)
            da = _dot(d_o, vv, _NT) * w
            prefix = _split_dot(da, before) + _pair_cols(ca, cb, tq)
            sig = jnp.exp(log_beta)
            dz = da * (1.0 - sig) - prefix * sig
            if masked:
                dz = jnp.where(mask, dz, 0.0)
            dzb = (dz * (SB_HEAD_DIM ** -0.5)).astype(BF16)
            dq = dq + _dot(dzb, kk, _NN)
            dkk = _dot(dzb, q, _TN)
            dvv = _dot(w.astype(BF16), d_o, _TN)
            dk_ref[pl.ds(ks, SB_KEY_BLOCK), :] += jnp.where(lane_is_a, dkk[:SB_KEY_BLOCK], dkk[SB_KEY_BLOCK:])
            dv_ref[pl.ds(ks, SB_KEY_BLOCK), :] += jnp.where(lane_is_a, dvv[:SB_KEY_BLOCK], dvv[SB_KEY_BLOCK:])
            ca = ca + jnp.sum(da[:, :LANES], axis=1, keepdims=True)
            cb = cb + jnp.sum(da[:, LANES:], axis=1, keepdims=True)
            return dq, ca, cb

        carry = (jnp.zeros((tq, LANES), F32), jnp.zeros((tq, 1), F32), jnp.zeros((tq, 1), F32))
        n_full = i * per_q
        col_max = jnp.max(jnp.maximum(rma, rmb), axis=0, keepdims=True)
        dead = jnp.logical_and(col_max <= SB_DEAD_LOG, blk_lane[0:1] < n_full)
        first = jnp.sum(dead.astype(F32)).astype(jnp.int32)
        carry = lax.fori_loop(first, n_full, lambda j, c: tile(j, False, c), carry)
        for d in range(per_q):
            carry = tile(i * per_q + d, True, carry)
        dq_ref[...] = carry[0].astype(BF16)

    return pl.pallas_call(
        body,
        name="sb_bwd",
        grid=(4, nq),
        in_specs=[pl.BlockSpec((tq, LANES), lambda p, i: (i, p)),
                  pl.BlockSpec((T, LANES), lambda p, i: (0, 4 + p)),
                  pl.BlockSpec((T, LANES), lambda p, i: (0, 8 + p)),
                  pl.BlockSpec((tq, LANES), lambda p, i: (i, p)),
                  pl.BlockSpec((2, tq, LANES), lambda p, i: (p, i, 0))],
        out_specs=[pl.BlockSpec((tq, LANES), lambda p, i: (i, p)),
                   pl.BlockSpec((T, LANES), lambda p, i: (0, p)),
                   pl.BlockSpec((T, LANES), lambda p, i: (0, p))],
        out_shape=[jax.ShapeDtypeStruct((T, SB_WIDTH), BF16),
                   jax.ShapeDtypeStruct((T, SB_WIDTH), F32),
                   jax.ShapeDtypeStruct((T, SB_WIDTH), F32)],
        compiler_params=_params(("parallel", "arbitrary")),
    )(h_a, h_a, h_a, d_out, r_mat)


def _ret_tables(T):
    half = RET_QK_DIM // 2
    inv = 1.0 / (ROPE_BASE ** (jnp.arange(half, dtype=F32) / half))
    ang = jnp.arange(T, dtype=F32)[:, None] * inv[None, :]
    cos, sin = jnp.cos(ang), jnp.sin(ang)
    cos_t = jnp.concatenate([cos, cos], axis=1)
    sin_t = jnp.concatenate([-sin, sin], axis=1)
    log_gamma = jnp.log1p(-jnp.exp2(-5.0 - jnp.arange(RET_HEADS, dtype=F32)))
    idx = jnp.arange(RET_CHUNK, dtype=F32)
    rel = idx[:, None] - idx[None, :]
    decay = jnp.where(rel[None] >= 0, jnp.exp(log_gamma[:, None, None] * jnp.maximum(rel, 0.0)[None]), 0.0)
    k_decay = jnp.exp(log_gamma[None, :] * (RET_CHUNK - 1.0 - idx)[:, None])
    q_decay = jnp.exp(log_gamma[None, :] * (idx + 1.0)[:, None])
    chunk_decay = jnp.exp(log_gamma * RET_CHUNK)
    k_dec = jnp.broadcast_to(k_decay.T[:, :, None], (RET_HEADS, RET_CHUNK, LANES))
    q_dec = jnp.broadcast_to(q_decay.T[:, :, None], (RET_HEADS, RET_CHUNK, LANES))
    c_dec = jnp.broadcast_to(chunk_decay[:, None, None], (RET_HEADS, 8, LANES))
    return cos_t, sin_t, decay, k_dec, q_dec, c_dec


def _rotary(x, cos_t, sin_t):
    return x * cos_t + pltpu.roll(x, RET_QK_DIM // 2, 1) * sin_t


def _rotary_transpose(dy, cos_t, sin_t):
    return dy * cos_t + pltpu.roll(dy * sin_t, RET_QK_DIM // 2, 1)


def _head_norm(o):
    mu = jnp.mean(o, axis=1, keepdims=True)
    cen = o - mu
    var = jnp.mean(cen * cen, axis=1, keepdims=True)
    rstd = lax.rsqrt(var + LN_EPS)
    return cen * rstd, rstd


def _ret_specs(nc, reverse):
    def n_of(n):
        return (nc - 1 - n) if reverse else n

    q_spec = pl.BlockSpec((RET_CHUNK, RET_QK_WIDTH), lambda n: (n_of(n), 0))
    k_spec = pl.BlockSpec((RET_CHUNK, RET_QK_WIDTH), lambda n: (n_of(n), 1))
    vv = pl.BlockSpec((RET_CHUNK, RET_V_WIDTH), lambda n: (n_of(n), 0))
    pos = pl.BlockSpec((RET_CHUNK, LANES), lambda n: (n_of(n), 0))
    per_head = pl.BlockSpec((RET_HEADS, RET_CHUNK, LANES), lambda n: (0, 0, 0))
    c_dec = pl.BlockSpec((RET_HEADS, 8, LANES), lambda n: (0, 0, 0))
    state = pl.BlockSpec((RET_HEADS, 1, RET_QK_DIM, RET_V_DIM), lambda n: (0, n_of(n), 0, 0))
    return q_spec, k_spec, vv, pos, per_head, c_dec, state


def _qk_cols(h):
    return slice(h * RET_QK_DIM, (h + 1) * RET_QK_DIM)


def _v_cols(h):
    return slice(h * RET_V_DIM, (h + 1) * RET_V_DIM)


def _ret_fwd(h_b, h_c, h_d, tables):
    T = h_b.shape[0]
    nc = T // RET_CHUNK
    q_spec, k_spec, vv, pos, per_head, c_dec, state = _ret_specs(nc, False)

    def body(q_ref, k_ref, v_ref, g_ref, cos_ref, sin_ref, dec_ref, kd_ref, qd_ref, cd_ref,
             y_ref, o_ref, st_ref, state_ref):
        @pl.when(pl.program_id(0) == 0)
        def _():
            state_ref[...] = jnp.zeros_like(state_ref)

        cos_t, sin_t = cos_ref[...], sin_ref[...]
        for h in range(RET_HEADS):
            q = _rotary(q_ref[:, _qk_cols(h)], cos_t, sin_t) * (RET_QK_DIM ** -0.5)
            k = _rotary(k_ref[:, _qk_cols(h)], cos_t, sin_t)
            v = v_ref[:, _v_cols(h)]
            prev = state_ref[h]
            scores = _dot(q.astype(BF16), k.astype(BF16), _NT) * dec_ref[h]
            inner = _dot(scores.astype(BF16), v, _NN)
            cross = _dot((q * qd_ref[h]).astype(BF16), prev.astype(BF16), _NN)
            o = inner + cross
            st_ref[h, 0] = prev
            kv = _dot((k * kd_ref[h]).astype(BF16), v, _TN)
            state_ref[h] = prev * cd_ref[h, 0:1, 0:1] + kv
            o_ref[:, _v_cols(h)] = o
            normed, _ = _head_norm(o)
            gate = g_ref[:, _v_cols(h)]
            y_ref[:, _v_cols(h)] = (gate * jax.nn.sigmoid(gate) * normed).astype(BF16)

    return pl.pallas_call(
        body,
        name="ret_fwd",
        grid=(nc,),
        in_specs=[q_spec, k_spec, vv, vv, pos, pos, per_head, per_head, per_head, c_dec],
        out_specs=[vv, vv, state],
        out_shape=[jax.ShapeDtypeStruct((T, RET_V_WIDTH), BF16),
                   jax.ShapeDtypeStruct((T, RET_V_WIDTH), F32),
                   jax.ShapeDtypeStruct((RET_HEADS, nc, RET_QK_DIM, RET_V_DIM), F32)],
        scratch_shapes=[pltpu.VMEM((RET_HEADS, RET_QK_DIM, RET_V_DIM), F32)],
        compiler_params=_params(("arbitrary",)),
    )(h_b, h_b, h_c, h_d, *tables)


def _ret_bwd(d_y, o_pre, states, h_b, h_c, h_d, tables, rider=None):
    T = h_b.shape[0]
    nc = T // RET_CHUNK
    q_spec, k_spec, vv, pos, per_head, c_dec, state = _ret_specs(nc, True)
    n_ride = rider.n if rider is not None else 0

    def body(*refs):
        (dy_ref, o_ref, st_ref, q_ref, k_ref, v_ref, g_ref, cos_ref, sin_ref, dec_ref, kd_ref, qd_ref,
         cd_ref) = refs[:13]
        dq_ref, dk_ref, dv_ref, dg_ref = refs[13 + n_ride:17 + n_ride]
        carry_ref = refs[17 + 2 * n_ride]
        ids = [pl.program_id(0)]
        if rider is not None:
            ride = (refs[13:13 + n_ride], refs[17 + n_ride:17 + 2 * n_ride], refs[-3:])
            rider.start_at_first(ids, ride)

        @pl.when(ids[0] == 0)
        def _():
            carry_ref[...] = jnp.zeros_like(carry_ref)

        cos_t, sin_t = cos_ref[...], sin_ref[...]
        scale = RET_QK_DIM ** -0.5
        for h in range(RET_HEADS):
            q = _rotary(q_ref[:, _qk_cols(h)], cos_t, sin_t) * scale
            k = _rotary(k_ref[:, _qk_cols(h)], cos_t, sin_t)
            v = v_ref[:, _v_cols(h)]
            decay, k_dec, q_dec = dec_ref[h], kd_ref[h], qd_ref[h]
            chunk_decay = cd_ref[h, 0:1, 0:1]
            state = st_ref[h, 0].astype(BF16)
            later = carry_ref[h]
            later_b = later.astype(BF16)

            gate = g_ref[:, _v_cols(h)]
            sig = jax.nn.sigmoid(gate)
            silu = gate * sig
            normed, rstd = _head_norm(o_ref[:, _v_cols(h)])
            d_y = dy_ref[:, _v_cols(h)]
            dg_ref[:, _v_cols(h)] = (d_y * normed * (sig * (1.0 + gate * (1.0 - sig)))).astype(BF16)
            d_n = d_y * silu
            d_o = rstd * (d_n - jnp.mean(d_n, axis=1, keepdims=True)
                          - normed * jnp.mean(d_n * normed, axis=1, keepdims=True))
            d_ob = d_o.astype(BF16)

            qb, kb = q.astype(BF16), k.astype(BF16)
            qd_b, kd_b = (q * q_dec).astype(BF16), (k * k_dec).astype(BF16)
            scores = _dot(qb, kb, _NT) * decay
            d_scores = (_dot(d_ob, v, _NT) * decay).astype(BF16)
            dq = _dot(d_scores, kb, _NN) + _dot(d_ob, state, _NT) * q_dec
            dk = _dot(d_scores, qb, _TN) + _dot(v, later_b, _NT) * k_dec
            dv = _dot(scores.astype(BF16), d_ob, _TN) + _dot(kd_b, later_b, _NN)
            carry_ref[h] = _dot(qd_b, d_ob, _TN) + chunk_decay * later
            dq_ref[:, _qk_cols(h)] = _rotary_transpose(dq * scale, cos_t, sin_t).astype(BF16)
            dk_ref[:, _qk_cols(h)] = _rotary_transpose(dk, cos_t, sin_t).astype(BF16)
            dv_ref[:, _v_cols(h)] = dv.astype(BF16)
        if rider is not None:
            rider.wait_at_last(ids, (nc,), ride)

    qk_out = pl.BlockSpec((RET_CHUNK, RET_QK_WIDTH), lambda n: (nc - 1 - n, 0))
    in_specs = [vv, vv, state, q_spec, k_spec, vv, vv, pos, pos, per_head, per_head, per_head, c_dec]
    out_specs = [qk_out, qk_out, vv, vv]
    out_shape = [jax.ShapeDtypeStruct((T, RET_QK_WIDTH), BF16), jax.ShapeDtypeStruct((T, RET_QK_WIDTH), BF16),
                 jax.ShapeDtypeStruct((T, RET_V_WIDTH), BF16), jax.ShapeDtypeStruct((T, RET_V_WIDTH), BF16)]
    args = (d_y, o_pre, states, h_b, h_b, h_c, h_d) + tuple(tables)
    scratch = [pltpu.VMEM((RET_HEADS, RET_QK_DIM, RET_V_DIM), F32)]
    if rider is not None:
        in_specs, args = in_specs + rider.specs, args + tuple(rider.bufs)
        out_specs, out_shape = out_specs + rider.specs, out_shape + rider.out_shape
        scratch = scratch + rider.scratch
    outs = pl.pallas_call(
        body,
        name="ret_bwd",
        grid=(nc,),
        in_specs=in_specs,
        out_specs=out_specs,
        out_shape=out_shape,
        scratch_shapes=scratch,
        compiler_params=_params(("arbitrary",)),
    )(*args)
    return outs[0], outs[1], outs[2], outs[3], list(outs[4:])


def _row_tile(T):
    return _pick(T, (256, 128))


def _gate_mix_tiles(y_ret, h_e, b_gate, y_sb):
    gates = jax.nn.sigmoid(h_e + b_gate)
    return y_ret, gates[:, :D_MODEL] * y_sb + gates[:, D_MODEL:] * y_ret


def _col_sum_update(acc_ref, val, first):
    part = jnp.sum(val.reshape(val.shape[0] // 8, 8, val.shape[1]), axis=0)

    @pl.when(first)
    def _():
        acc_ref[...] = part

    @pl.when(jnp.logical_not(first))
    def _():
        acc_ref[...] += part


def _gate_mix_bwd(d_mix, h_e, b_gate, y_sb, y_ret):
    T = h_e.shape[0]
    tr = _row_tile(T)
    steps = T // tr

    def body(dm_ref, g_ref, b_ref, ys_ref, yr_ref, dys_ref, dyr_ref, de_ref, db_ref, acc_ref):
        i = pl.program_id(0)
        dm = dm_ref[...]
        gates = jax.nn.sigmoid(g_ref[...] + b_ref[...])
        g0, g1 = gates[:, :D_MODEL], gates[:, D_MODEL:]
        dys_ref[...] = (dm * g0).astype(BF16)
        dyr_ref[...] = (dm * g1).astype(BF16)
        de = jnp.concatenate([dm * ys_ref[...] * g0 * (1.0 - g0), dm * yr_ref[...] * g1 * (1.0 - g1)], axis=1)
        de_ref[...] = de.astype(BF16)
        _col_sum_update(acc_ref, de, i == 0)

        @pl.when(i == steps - 1)
        def _():
            db_ref[...] = jnp.sum(acc_ref[...], axis=0, keepdims=True)

    row = pl.BlockSpec((tr, D_MODEL), lambda i: (i, 0))
    wide = pl.BlockSpec((tr, 2 * D_MODEL), lambda i: (i, 0))
    vec = pl.BlockSpec((1, 2 * D_MODEL), lambda i: (0, 0))
    return pl.pallas_call(
        body,
        name="gate_mix_bwd",
        grid=(steps,),
        in_specs=[row, wide, vec, row, row],
        out_specs=[row, row, wide, vec],
        out_shape=[jax.ShapeDtypeStruct((T, D_MODEL), BF16), jax.ShapeDtypeStruct((T, D_MODEL), BF16),
                   jax.ShapeDtypeStruct((T, 2 * D_MODEL), BF16), jax.ShapeDtypeStruct((1, 2 * D_MODEL), F32)],
        scratch_shapes=[pltpu.VMEM((8, 2 * D_MODEL), F32)],
        compiler_params=_params(("arbitrary",)),
    )(d_mix, h_e, b_gate, y_sb, y_ret)


def _ln_stats(u):
    mu = jnp.mean(u, axis=1, keepdims=True)
    cen = u - mu
    var = jnp.mean(cen * cen, axis=1, keepdims=True)
    rstd = lax.rsqrt(var + LN_EPS)
    return cen * rstd, rstd


def _ln_input_grad(d_out, gain, xhat, rstd):
    d_hat = d_out * gain
    return rstd * (d_hat - jnp.mean(d_hat, axis=1, keepdims=True)
                   - xhat * jnp.mean(d_hat * xhat, axis=1, keepdims=True))


def _ln_tiles(sub, x_prev, gain, bias):
    xhat, rstd = _ln_stats(DN_ALPHA * x_prev + sub)
    out = xhat * gain + bias
    return out, out, xhat, rstd


def _ln_bwd(d_out, xhat, rstd, gain, name):
    T = d_out.shape[0]
    tr = _row_tile(T)
    steps = T // tr

    def body(d_ref, xh_ref, rs_ref, g_ref, du_ref, dub_ref, dg_ref, db_ref, accg_ref, accb_ref):
        i = pl.program_id(0)
        d_o, xh = d_ref[...], xh_ref[...]
        du = _ln_input_grad(d_o, g_ref[...], xh, rs_ref[...])
        du_ref[...] = du
        dub_ref[...] = du.astype(BF16)
        _col_sum_update(accg_ref, d_o * xh, i == 0)
        _col_sum_update(accb_ref, d_o, i == 0)

        @pl.when(i == steps - 1)
        def _():
            dg_ref[...] = jnp.sum(accg_ref[...], axis=0, keepdims=True)
            db_ref[...] = jnp.sum(accb_ref[...], axis=0, keepdims=True)

    row = pl.BlockSpec((tr, D_MODEL), lambda i: (i, 0))
    vec = pl.BlockSpec((1, D_MODEL), lambda i: (0, 0))
    return pl.pallas_call(
        body,
        name=name,
        grid=(steps,),
        in_specs=[row, row, pl.BlockSpec((tr, 1), lambda i: (i, 0)), vec],
        out_specs=[row, row, vec, vec],
        out_shape=[jax.ShapeDtypeStruct((T, D_MODEL), F32), jax.ShapeDtypeStruct((T, D_MODEL), BF16),
                   jax.ShapeDtypeStruct((1, D_MODEL), F32), jax.ShapeDtypeStruct((1, D_MODEL), F32)],
        scratch_shapes=[pltpu.VMEM((8, D_MODEL), F32), pltpu.VMEM((8, D_MODEL), F32)],
        compiler_params=_params(("arbitrary",)),
    )(d_out, xhat, rstd, gain)


def _ln_loss(x_prev, sub, gain, bias, target):
    T = x_prev.shape[0]
    tr = _row_tile(T)
    steps = T // tr

    def body(x_ref, s_ref, g_ref, b_ref, t_ref, loss_ref, du_ref, dub_ref, dg_ref, db_ref, accl_ref, accg_ref,
             accb_ref):
        i = pl.program_id(0)
        gain_v = g_ref[...]
        xhat, rstd = _ln_stats(DN_ALPHA * x_ref[...] + s_ref[...])
        diff = xhat * gain_v + b_ref[...] - t_ref[...]
        d_o = diff * (1.0 / D_MODEL)
        du = _ln_input_grad(d_o, gain_v, xhat, rstd)
        du_ref[...] = du
        dub_ref[...] = du.astype(BF16)
        _col_sum_update(accl_ref, diff * diff, i == 0)
        _col_sum_update(accg_ref, d_o * xhat, i == 0)
        _col_sum_update(accb_ref, d_o, i == 0)

        @pl.when(i == steps - 1)
        def _():
            per_col = jnp.sum(accl_ref[...], axis=0, keepdims=True)
            loss_ref[...] = jnp.sum(per_col, axis=1, keepdims=True) * (0.5 / D_MODEL)
            dg_ref[...] = jnp.sum(accg_ref[...], axis=0, keepdims=True)
            db_ref[...] = jnp.sum(accb_ref[...], axis=0, keepdims=True)

    row = pl.BlockSpec((tr, D_MODEL), lambda i: (i, 0))
    vec = pl.BlockSpec((1, D_MODEL), lambda i: (0, 0))
    return pl.pallas_call(
        body,
        name="ln3_loss",
        grid=(steps,),
        in_specs=[row, row, vec, vec, row],
        out_specs=[pl.BlockSpec((1, 1), lambda i: (0, 0)), row, row, vec, vec],
        out_shape=[jax.ShapeDtypeStruct((1, 1), F32), jax.ShapeDtypeStruct((T, D_MODEL), F32),
                   jax.ShapeDtypeStruct((T, D_MODEL), BF16),
                   jax.ShapeDtypeStruct((1, D_MODEL), F32), jax.ShapeDtypeStruct((1, D_MODEL), F32)],
        scratch_shapes=[pltpu.VMEM((8, D_MODEL), F32)] * 3,
        compiler_params=_params(("arbitrary",)),
    )(x_prev, sub, gain, bias, target)


def _mem_probs(q_h, k_h):
    s = _dot(q_h, k_h, _NT) * (MEM_HEAD_DIM ** -0.5)
    e = jnp.exp(s - jnp.max(s, axis=1, keepdims=True))
    return e / jnp.sum(e, axis=1, keepdims=True)


def _xattn_fwd(q, kv):
    T, mem_len = q.shape[0], kv.shape[0]
    tq = _pick(T, (512, 256, 128))

    def body(q_ref, kv_ref, o_ref):
        for h in range(MEM_HEADS):
            cols = slice(h * MEM_HEAD_DIM, (h + 1) * MEM_HEAD_DIM)
            vcols = slice(D_MODEL + h * MEM_HEAD_DIM, D_MODEL + (h + 1) * MEM_HEAD_DIM)
            p = _mem_probs(q_ref[:, cols], kv_ref[:, cols])
            o_ref[:, cols] = _dot(p.astype(BF16), kv_ref[:, vcols], _NN).astype(BF16)

    return pl.pallas_call(
        body,
        name="xattn_fwd",
        grid=(T // tq,),
        in_specs=[pl.BlockSpec((tq, D_MODEL), lambda i: (i, 0)),
                  pl.BlockSpec((mem_len, 2 * D_MODEL), lambda i: (0, 0))],
        out_specs=pl.BlockSpec((tq, D_MODEL), lambda i: (i, 0)),
        out_shape=jax.ShapeDtypeStruct((T, D_MODEL), BF16),
        compiler_params=_params(("parallel",)),
    )(q, kv)


def _xattn_bwd(q, kv, d_o):
    T, mem_len = q.shape[0], kv.shape[0]
    tq = _pick(T, (512, 256, 128))

    def body(q_ref, kv_ref, do_ref, dq_ref, dkv_ref):
        @pl.when(pl.program_id(0) == 0)
        def _():
            dkv_ref[...] = jnp.zeros_like(dkv_ref)

        for h in range(MEM_HEADS):
            cols = slice(h * MEM_HEAD_DIM, (h + 1) * MEM_HEAD_DIM)
            vcols = slice(D_MODEL + h * MEM_HEAD_DIM, D_MODEL + (h + 1) * MEM_HEAD_DIM)
            q_h, k_h, do_h = q_ref[:, cols], kv_ref[:, cols], do_ref[:, cols]
            p = _mem_probs(q_h, k_h)
            dp = _dot(do_h, kv_ref[:, vcols], _NT)
            ds = p * (dp - jnp.sum(dp * p, axis=1, keepdims=True))
            dsb = (ds * (MEM_HEAD_DIM ** -0.5)).astype(BF16)
            dq_ref[:, cols] = _dot(dsb, k_h, _NN).astype(BF16)
            dkv_ref[:, cols] += _dot(dsb, q_h, _TN)
            dkv_ref[:, vcols] += _dot(p.astype(BF16), do_h, _TN)

    row = pl.BlockSpec((tq, D_MODEL), lambda i: (i, 0))
    full = pl.BlockSpec((mem_len, 2 * D_MODEL), lambda i: (0, 0))
    return pl.pallas_call(
        body,
        name="xattn_bwd",
        grid=(T // tq,),
        in_specs=[row, full, row],
        out_specs=[row, full],
        out_shape=[jax.ShapeDtypeStruct((T, D_MODEL), BF16), jax.ShapeDtypeStruct((mem_len, 2 * D_MODEL), F32)],
        compiler_params=_params(("arbitrary",)),
    )(q, kv, d_o)


def _swiglu_tiles(f):
    a, b = f[:, :FFN_HIDDEN], f[:, FFN_HIDDEN:]
    return f, a * jax.nn.sigmoid(a) * b


def _swiglu_grad_tiles(d_hidden, f):
    a, b = f[:, :FFN_HIDDEN], f[:, FFN_HIDDEN:]
    sig = jax.nn.sigmoid(a)
    return (jnp.concatenate([d_hidden * b * (sig * (1.0 + a * (1.0 - sig))), d_hidden * (a * sig)], axis=1),)


def _local_step(x, mem, w_in, small, target, fetch_rest, ship):
    T = x.shape[0]
    tables = _ret_tables(T)
    xb, memb = x.astype(BF16), mem.astype(BF16)

    h_a = _mm(xb, w_in[:, 0:1536], mode="nn", out_dtype=BF16, name="proj_sb")
    h_b = _mm(xb, w_in[:, 1536:2560], mode="nn", out_dtype=F32, name="proj_ret_qk")
    h_c = _mm(xb, w_in[:, 2560:3584], mode="nn", out_dtype=BF16, name="proj_ret_v")
    h_d = _mm(xb, w_in[:, 3584:4608], mode="nn", out_dtype=F32, name="proj_ret_g")
    h_e = _mm(xb, w_in[:, 4608:6656], mode="nn", out_dtype=F32, name="proj_gate")
    (a_sb, r_mat, _), w = fetch_rest(lambda rider: _sb_fwd(h_a, rider))
    y_gated, o_pre, states = _ret_fwd(h_b, h_c, h_d, tables)
    y_sb = _mm(a_sb, w["w_sb_o"], mode="nn", out_dtype=F32, name="sb_out")
    row_f32, row_bf16 = (D_MODEL, F32), (D_MODEL, BF16)
    ln_outs = [row_f32, row_bf16, row_f32, (1, F32)]
    y_ret, mix_in = _mm_fused(y_gated, w["w_ret_o"], mode="nn", name="ret_out", extras=[h_e, small["b_gate"], y_sb],
                              outs=[row_f32, row_bf16], epilogue=_gate_mix_tiles)
    x1, x1b, xhat1, rstd1 = _mm_fused(mix_in, w["w_mix_o"], mode="nn", name="mix_out",
                                      extras=[x, small["ln1_g"], small["ln1_b"]], outs=ln_outs, epilogue=_ln_tiles)
    q_m = _mm(x1b, w["w_mem_q"], mode="nn", out_dtype=BF16, name="mem_q")
    kv_m = _mm(memb, w["w_mem_kv"], mode="nn", out_dtype=BF16, name="mem_kv")
    o_m = _xattn_fwd(q_m, kv_m)
    x2, x2b, xhat2, rstd2 = _mm_fused(o_m, w["w_mem_o"], mode="nn", name="mem_out",
                                      extras=[x1, small["ln2_g"], small["ln2_b"]], outs=ln_outs, epilogue=_ln_tiles)
    f, hidden = _mm_fused(x2b, w["w_ffn_in"], mode="nn", name="ffn_in", extras=[],
                          outs=[(2 * FFN_HIDDEN, F32), (FFN_HIDDEN, BF16)], epilogue=_swiglu_tiles)
    ff = _mm(hidden, w["w_ffn_out"], mode="nn", out_dtype=F32, name="ffn_out")
    loss, du3, du3b, d_ln3_g, d_ln3_b = _ln_loss(x2, ff, small["ln3_g"], small["ln3_b"], target)

    g_ffn_out = _mm(hidden, du3b, mode="tn", out_dtype=F32, name="g_ffn_out")
    (d_f,) = _mm_fused(du3b, w["w_ffn_out"], mode="nt", name="d_hidden", extras=[f],
                       outs=[(2 * FFN_HIDDEN, BF16)], epilogue=_swiglu_grad_tiles)
    g_ffn_in = _mm(x2b, d_f, mode="tn", out_dtype=F32, name="g_ffn_in")
    (d_x2,) = ship({"w_ffn_out": g_ffn_out},
                   lambda rider: _mm_host(d_f, w["w_ffn_in"], mode="nt", out_dtype=F32, name="d_x2", res=du3,
                                          res_scale=DN_ALPHA, rider=rider))
    du2, du2b, d_ln2_g, d_ln2_b = _ln_bwd(d_x2, xhat2, rstd2, small["ln2_g"], "ln2_bwd")
    g_mem_o = _mm(o_m, du2b, mode="tn", out_dtype=F32, name="g_mem_o")
    d_om = _mm(du2b, w["w_mem_o"], mode="nt", out_dtype=BF16, name="d_om")
    d_qm, d_kvm = _xattn_bwd(q_m, kv_m, d_om)
    g_mem_q = _mm(x1b, d_qm, mode="tn", out_dtype=F32, name="g_mem_q")
    g_mem_kv = _mm(memb, d_kvm.astype(BF16), mode="tn", out_dtype=F32, name="g_mem_kv")
    d_x1 = _mm(d_qm, w["w_mem_q"], mode="nt", out_dtype=F32, name="d_x1", res=du2, res_scale=DN_ALPHA)
    du1, du1b, d_ln1_g, d_ln1_b = _ln_bwd(d_x1, xhat1, rstd1, small["ln1_g"], "ln1_bwd")
    g_mix_o = _mm(mix_in, du1b, mode="tn", out_dtype=F32, name="g_mix_o")
    d_mix_in = _mm(du1b, w["w_mix_o"], mode="nt", out_dtype=F32, name="d_mix_in")
    d_ysb, d_yret, d_e, d_b_gate = _gate_mix_bwd(d_mix_in, h_e, small["b_gate"], y_sb, y_ret)
    g_sb_o = _mm(a_sb, d_ysb, mode="tn", out_dtype=F32, name="g_sb_o")
    g_ret_o = _mm(y_gated, d_yret, mode="tn", out_dtype=F32, name="g_ret_o")
    d_asb = _mm(d_ysb, w["w_sb_o"], mode="nt", out_dtype=BF16, name="d_asb")
    d_ygated = _mm(d_yret, w["w_ret_o"], mode="nt", out_dtype=F32, name="d_ygated")
    small_grads = {"b_gate": d_b_gate, "ln1_g": d_ln1_g, "ln1_b": d_ln1_b, "ln2_g": d_ln2_g, "ln2_b": d_ln2_b,
                   "ln3_g": d_ln3_g, "ln3_b": d_ln3_b}
    d_rq, d_rk, d_c, d_d = ship({"w_ffn_in": g_ffn_in},
                                lambda rider: _ret_bwd(d_ygated, o_pre, states, h_b, h_c, h_d, tables, rider))
    d_q, d_k, d_v = _sb_bwd(h_a, d_asb, r_mat)
    d_h = jnp.concatenate([d_q, d_k.astype(BF16), d_v.astype(BF16), d_rq, d_rk, d_c, d_d, d_e], axis=1)
    late = {"w_mem_kv": g_mem_kv, "w_mem_q": g_mem_q, "w_mem_o": g_mem_o, "w_mix_o": g_mix_o, "w_ret_o": g_ret_o,
            "w_sb_o": g_sb_o, "small": small_grads}
    (g_in,) = ship(late, lambda rider: _mm_host(xb, d_h, mode="tn", out_dtype=F32, name="g_in", rider=rider))
    (d_x,) = ship({"w_in": g_in},
                  lambda rider: _mm_host(d_h, w_in, mode="nt", out_dtype=F32, name="d_x", res=du1, res_scale=DN_ALPHA,
                                         rider=rider))
    return loss, d_x


def _adamw_math(w, g, m, v):
    m = ADAM_B1 * m + (1.0 - ADAM_B1) * g
    v = ADAM_B2 * v + (1.0 - ADAM_B2) * jnp.square(g)
    m_hat = m / (1.0 - ADAM_B1 ** ADAM_STEP)
    v_hat = v / (1.0 - ADAM_B2 ** ADAM_STEP)
    delta = -ADAM_LR * (m_hat / (jnp.sqrt(v_hat) + ADAM_EPS) + ADAM_WD * w)
    return delta, m, v


def _adamw(parts, w, m, v, name):
    R, C = w.shape
    tr = max(t for t in range(16, min(R, 256) + 1, 16) if R % t == 0) if R >= 16 else R

    def body(p_ref, w_ref, m_ref, v_ref, g_ref, d_ref, nm_ref, nv_ref):
        g = p_ref[0].astype(F32)
        for j in range(1, N_DEV):
            g = g + p_ref[j].astype(F32)
        delta, nm, nv = _adamw_math(w_ref[...], g, m_ref[...], v_ref[...])
        g_ref[...] = g
        d_ref[...] = delta
        nm_ref[...] = nm
        nv_ref[...] = nv

    blk = pl.BlockSpec((tr, C), lambda i: (i, 0))
    out = jax.ShapeDtypeStruct((R, C), F32)
    return pl.pallas_call(
        body,
        name=name,
        grid=(R // tr,),
        in_specs=[pl.BlockSpec((N_DEV, tr, C), lambda i: (0, i, 0)), blk, blk, blk],
        out_specs=[blk] * 4,
        out_shape=[out] * 4,
        compiler_params=_params(("parallel",)),
    )(parts, w, m, v)


_SHARD_AXIS = {"w_in": 1, "w_sb_o": 1, "w_ret_o": 0, "w_mix_o": 0, "w_mem_q": 0, "w_mem_kv": 1, "w_mem_o": 0,
               "w_ffn_in": 1, "w_ffn_out": 0}
_MATRICES = tuple(_SHARD_AXIS)
_SMALL = ("b_gate", "ln1_g", "ln1_b", "ln2_g", "ln2_b", "ln3_g", "ln3_b")
_WEIGHT_ORDER = ("w_in", "b_gate", "w_sb_o", "w_ret_o", "w_mix_o", "ln1_g", "ln1_b", "w_mem_q", "w_mem_kv", "w_mem_o",
                 "ln2_g", "ln2_b", "w_ffn_in", "w_ffn_out", "ln3_g", "ln3_b")


def _assemble(name, gathered):
    if _SHARD_AXIS[name] == 0:
        return gathered.reshape(-1, gathered.shape[2])
    return jnp.transpose(gathered, (1, 0, 2)).reshape(gathered.shape[1], -1)


def _to_slots(name, full):
    if _SHARD_AXIS[name] == 0:
        return full.reshape(N_DEV, full.shape[0] // N_DEV, full.shape[1])
    return jnp.transpose(full.reshape(full.shape[0], N_DEV, full.shape[1] // N_DEV), (1, 0, 2))


def _pack_small(vals):
    return jnp.concatenate([vals["b_gate"].reshape(2, D_MODEL)] + [vals[n] for n in _SMALL[1:]], axis=0)


def _unpack_small(packed):
    out = {"b_gate": packed[0:2].reshape(1, 2 * D_MODEL)}
    for i, n in enumerate(_SMALL[1:]):
        out[n] = packed[2 + i:3 + i]
    return out


def kernel(x, mem, w_in, b_gate, w_sb_o, w_ret_o, w_mix_o, ln1_g, ln1_b, w_mem_q, w_mem_kv, w_mem_o, ln2_g, ln2_b, w_ffn_in, w_ffn_out, ln3_g, ln3_b, loss_target, m_w_in, m_b_gate, m_w_sb_o, m_w_ret_o, m_w_mix_o, m_ln1_g, m_ln1_b, m_w_mem_q, m_w_mem_kv, m_w_mem_o, m_ln2_g, m_ln2_b, m_w_ffn_in, m_w_ffn_out, m_ln3_g, m_ln3_b, v_w_in, v_b_gate, v_w_sb_o, v_w_ret_o, v_w_mix_o, v_ln1_g, v_ln1_b, v_w_mem_q, v_w_mem_kv, v_w_mem_o, v_ln2_g, v_ln2_b, v_w_ffn_in, v_w_ffn_out, v_ln3_g, v_ln3_b):
    weights = dict(w_in=w_in, b_gate=b_gate, w_sb_o=w_sb_o, w_ret_o=w_ret_o, w_mix_o=w_mix_o, ln1_g=ln1_g, ln1_b=ln1_b,
                   w_mem_q=w_mem_q, w_mem_kv=w_mem_kv, w_mem_o=w_mem_o, ln2_g=ln2_g, ln2_b=ln2_b, w_ffn_in=w_ffn_in,
                   w_ffn_out=w_ffn_out, ln3_g=ln3_g, ln3_b=ln3_b)
    mom1 = dict(w_in=m_w_in, b_gate=m_b_gate, w_sb_o=m_w_sb_o, w_ret_o=m_w_ret_o, w_mix_o=m_w_mix_o, ln1_g=m_ln1_g,
                ln1_b=m_ln1_b, w_mem_q=m_w_mem_q, w_mem_kv=m_w_mem_kv, w_mem_o=m_w_mem_o, ln2_g=m_ln2_g, ln2_b=m_ln2_b,
                w_ffn_in=m_w_ffn_in, w_ffn_out=m_w_ffn_out, ln3_g=m_ln3_g, ln3_b=m_ln3_b)
    mom2 = dict(w_in=v_w_in, b_gate=v_b_gate, w_sb_o=v_w_sb_o, w_ret_o=v_w_ret_o, w_mix_o=v_w_mix_o, ln1_g=v_ln1_g,
                ln1_b=v_ln1_b, w_mem_q=v_w_mem_q, w_mem_kv=v_w_mem_kv, w_mem_o=v_w_mem_o, ln2_g=v_ln2_g, ln2_b=v_ln2_b,
                w_ffn_in=v_w_ffn_in, w_ffn_out=v_w_ffn_out, ln3_g=v_ln3_g, ln3_b=v_ln3_b)

    (gathered_in,) = _exchange([weights["w_in"][0].astype(BF16)], False, "gather_w_in")
    rest = [n for n in _MATRICES if n != "w_in"]
    received = {}

    def fetch_rest(host):
        res = host(_Rider([weights[n][0].astype(BF16) for n in rest], False))
        return res, {n: _assemble(n, g) for n, g in zip(rest, res[-1])}

    def ship(grads, host):
        names = list(grads)
        bufs = []
        for n in names:
            if n == "small":
                part = _pack_small(grads[n])
                bufs.append(jnp.broadcast_to(part[None], (N_DEV,) + part.shape))
            else:
                bufs.append(_to_slots(n, grads[n]).astype(BF16))
        res = host(_Rider(bufs, True))
        received.update(zip(names, res[-1]))
        return res[:-1]

    small = {n: weights[n] for n in _SMALL}
    loss, d_x = _local_step(x[0], mem[0], _assemble("w_in", gathered_in), small, loss_target[0], fetch_rest, ship)

    new = {}
    for n in _MATRICES:
        new[n] = _adamw(received[n], weights[n][0], mom1[n][0], mom2[n][0], "adamw_" + n)
    packed = _adamw(received["small"], _pack_small({n: weights[n] for n in _SMALL}),
                    _pack_small({n: mom1[n] for n in _SMALL}), _pack_small({n: mom2[n] for n in _SMALL}), "adamw_small")
    small_new = [_unpack_small(p) for p in packed]

    outs = [lax.psum(loss[0, 0], MESH_AXES), d_x[None]]
    for slot in range(4):
        for n in _WEIGHT_ORDER:
            outs.append(new[n][slot][None] if n in new else small_new[slot][n])
    return tuple(outs)
```

```python
import functools
import math

import jax
import jax.numpy as jnp
from jax import lax
from jax.experimental import pallas as pl
from jax.experimental.pallas import tpu as pltpu

F32 = jnp.float32
BF16 = jnp.bfloat16

N_DEV = 8
D_MODEL = 1024
SB_HEAD_DIM = 64
SB_WIDTH = 512
RET_HEADS = 4
RET_QK_DIM = 128
RET_V_DIM = 256
RET_QK_WIDTH = 512
RET_V_WIDTH = 1024
RET_CHUNK = 128
ROPE_BASE = 10000.0
MEM_HEADS = 4
MEM_HEAD_DIM = 256
FFN_HIDDEN = 2816
DN_ALPHA = 2.0 ** 0.25
LN_EPS = 1e-5
ADAM_LR = 0.001
ADAM_B1 = 0.9
ADAM_B2 = 0.999
ADAM_EPS = 1e-08
ADAM_WD = 0.01
ADAM_STEP = 10

VMEM_LIMIT_BYTES = 52 * 1024 * 1024
LANES = 128
SB_KEY_BLOCK = 128
SB_Q_BLOCK = 256
SB_DEAD_LOG = -105.0

MESH_AXES = ("x", "y", "c")


def _pick(dim, prefs):
    for p in prefs:
        if dim % p == 0:
            return p
    return dim


def _params(sem):
    return pltpu.CompilerParams(dimension_semantics=sem, vmem_limit_bytes=VMEM_LIMIT_BYTES)


def _dot(a, b, dims):
    return lax.dot_general(a, b, (dims, ((), ())), preferred_element_type=F32)


_NN = ((1,), (0,))
_NT = ((1,), (1,))
_TN = ((0,), (0,))


def _my_index():
    return 4 * lax.axis_index("x") + 2 * lax.axis_index("y") + lax.axis_index("c")


def _peer(k):
    x, y, c = lax.axis_index("x"), lax.axis_index("y"), lax.axis_index("c")
    bx, by, bc = (k >> 2) & 1, (k >> 1) & 1, k & 1
    px = (1 - x) if bx else x
    py = (1 - y) if by else y
    pc = (1 - c) if bc else c
    return (px, py, pc), 4 * px + 2 * py + pc


class _Rider:
    def __init__(self, bufs, scatter):
        self.bufs, self.scatter, self.n = list(bufs), scatter, len(bufs)
        self.specs = [pl.BlockSpec(memory_space=pl.ANY)] * self.n
        self.out_shape = [jax.ShapeDtypeStruct(b.shape if scatter else (N_DEV,) + b.shape, b.dtype) for b in self.bufs]
        self.scratch = [pltpu.SemaphoreType.DMA((self.n, N_DEV - 1)), pltpu.SemaphoreType.DMA((self.n, N_DEV - 1)),
                        pltpu.SemaphoreType.DMA((self.n,))]

    def _remote(self, ride, a, k, src_ref, slot, to):
        _, dst, (send_sems, recv_sems, _) = ride
        return pltpu.make_async_remote_copy(src_ref=src_ref, dst_ref=dst[a].at[slot], send_sem=send_sems.at[a, k],
                                            recv_sem=recv_sems.at[a, k], device_id=to,
                                            device_id_type=pl.DeviceIdType.MESH)

    def _local(self, ride, a):
        src, dst, (_, _, local_sems) = ride
        me = _my_index()
        return pltpu.make_async_copy(src[a].at[me] if self.scatter else src[a], dst[a].at[me], local_sems.at[a])

    def _direct(self, ride, a):
        src = ride[0]
        me = _my_index()
        out = []
        for k in range(1, N_DEV):
            peer, peer_idx = _peer(k)
            out.append(self._remote(ride, a, k - 1, src[a].at[peer_idx], me, peer))
        return out

    def _two_level(self, ride, a):
        src, dst = ride[0], ride[1]
        x, y, c = lax.axis_index("x"), lax.axis_index("y"), lax.axis_index("c")
        me, sibling = _my_index(), (x, y, 1 - c)
        chips = [(1 - x, y), (x, 1 - y), (1 - x, 1 - y)]
        first = [self._remote(ride, a, 0, src[a], me, sibling)]
        passed, landing = [], [self._remote(ride, a, 0, src[a], me + 1 - 2 * c, sibling)]
        for j, (px, py) in enumerate(chips):
            first.append(self._remote(ride, a, 1 + j, src[a], me, (px, py, c)))
            theirs = 4 * px + 2 * py + c
            passed.append(self._remote(ride, a, 4 + j, dst[a].at[theirs], theirs, sibling))
            landing.append(self._remote(ride, a, 1 + j, src[a], theirs, (px, py, c)))
        for j, (px, py) in enumerate(chips):
            landing.append(self._remote(ride, a, 4 + j, src[a], 4 * px + 2 * py + 1 - c, sibling))
        return first, passed, landing

    def start(self, ride):
        for a in range(self.n):
            self._local(ride, a).start()
            for cp in (self._direct(ride, a) if self.scatter else self._two_level(ride, a)[0]):
                cp.start()

    def finish(self, ride):
        if self.scatter:
            for a in range(self.n):
                for cp in self._direct(ride, a):
                    cp.wait()
                self._local(ride, a).wait()
            return
        levels = [self._two_level(ride, a) for a in range(self.n)]
        for first, passed, landing in levels:
            for j, cp in enumerate(passed):
                landing[1 + j].wait_recv()
                cp.start()
        for a, (first, passed, landing) in enumerate(levels):
            landing[0].wait_recv()
            for cp in landing[4:]:
                cp.wait_recv()
            for cp in first + passed:
                cp.wait_send()
            self._local(ride, a).wait()

    def start_at_first(self, ids, ride):
        first = functools.reduce(jnp.logical_and, [i == 0 for i in ids])

        @pl.when(first)
        def _():
            self.start(ride)

    def wait_at_last(self, ids, grid, ride):
        last = functools.reduce(jnp.logical_and, [i == g - 1 for i, g in zip(ids, grid)])

        @pl.when(last)
        def _():
            self.finish(ride)


def _exchange(bufs, scatter, name):
    rider = _Rider(bufs, scatter)

    def body(*refs):
        ride = (refs[:rider.n], refs[rider.n:2 * rider.n], refs[2 * rider.n:])
        rider.start(ride)
        rider.finish(ride)

    return pl.pallas_call(
        body,
        name=name,
        in_specs=rider.specs,
        out_specs=rider.specs,
        out_shape=rider.out_shape,
        scratch_shapes=rider.scratch,
    )(*rider.bufs)


MM_RESIDENT_B_BYTES = 14 * 1024 * 1024
MM_A_TILE_BYTES = 4 * 1024 * 1024
MM_OUT_TILE_BYTES = 6 * 1024 * 1024


def _mm_tiles(mode, M, N, K, a_bytes, out_bytes):
    if mode != "tn" and K * N * 2 <= MM_RESIDENT_B_BYTES:
        for tm in (1024, 512, 256, 128):
            if M % tm == 0 and tm * K * a_bytes <= MM_A_TILE_BYTES and tm * N * out_bytes <= MM_OUT_TILE_BYTES:
                return tm, N, K
    if mode == "tn":
        return (_pick(M, (1024, 1408, 512, 256, 128)), _pick(N, (1024, 1664, 1408, 512, 256, 128)),
                _pick(K, (2048, 1024, 512, 256, 128)))
    return _pick(M, (1024, 512, 256, 128)), _pick(N, (512, 256, 128)), _pick(K, (1024, 512, 256, 128))


def _mm(a, b, *, mode, out_dtype, name, res=None, res_scale=1.0, rider=None):
    if mode == "nn":
        (M, K), (K2, N) = a.shape, b.shape
    elif mode == "nt":
        (M, K), (N, K2) = a.shape, b.shape
    else:
        (K, M), (K2, N) = a.shape, b.shape
    assert K == K2, (a.shape, b.shape, mode)
    out_bytes = jnp.dtype(out_dtype).itemsize + (4 if res is not None else 0)
    tm, tn, tk = _mm_tiles(mode, M, N, K, a.dtype.itemsize, out_bytes)
    grid = (M // tm, N // tn, K // tk)
    nk = grid[2]
    dims = {"nn": _NN, "nt": _NT, "tn": _TN}[mode]
    n_in = 2 + (res is not None)
    n_ride = rider.n if rider is not None else 0

    def body(*refs):
        a_ref, b_ref = refs[:2]
        r_ref = refs[2] if res is not None else None
        o_ref = refs[n_in + n_ride]
        rest = refs[n_in + 2 * n_ride + 1:]
        acc_ref = rest[0] if nk > 1 else None
        ids = [pl.program_id(d) for d in range(3)]
        if rider is not None:
            ride = (refs[n_in:n_in + n_ride], refs[n_in + n_ride + 1:n_in + 2 * n_ride + 1], rest[-3:])
            rider.start_at_first(ids, ride)
        part = _dot(a_ref[...].astype(BF16), b_ref[...].astype(BF16), dims)

        def finish(total):
            if r_ref is not None:
                total = total + res_scale * r_ref[...]
            o_ref[...] = total.astype(out_dtype)

        if nk == 1:
            finish(part)
        else:
            k = ids[2]

            @pl.when(k == 0)
            def _():
                acc_ref[...] = part

            @pl.when(k > 0)
            def _():
                acc_ref[...] += part

            @pl.when(k == nk - 1)
            def _():
                finish(acc_ref[...])

        if rider is not None:
            rider.wait_at_last(ids, grid, ride)

    if mode == "nn":
        a_spec = pl.BlockSpec((tm, tk), lambda i, j, k: (i, k))
        b_spec = pl.BlockSpec((tk, tn), lambda i, j, k: (k, j))
    elif mode == "nt":
        a_spec = pl.BlockSpec((tm, tk), lambda i, j, k: (i, k))
        b_spec = pl.BlockSpec((tn, tk), lambda i, j, k: (j, k))
    else:
        a_spec = pl.BlockSpec((tk, tm), lambda i, j, k: (k, i))
        b_spec = pl.BlockSpec((tk, tn), lambda i, j, k: (k, j))
    o_spec = pl.BlockSpec((tm, tn), lambda i, j, k: (i, j))
    in_specs = [a_spec, b_spec] + ([o_spec] if res is not None else [])
    args = (a, b) + ((res,) if res is not None else ())
    out_specs, out_shape = [o_spec], [jax.ShapeDtypeStruct((M, N), out_dtype)]
    scratch = [pltpu.VMEM((tm, tn), F32)] if nk > 1 else []
    sem = ("parallel", "parallel", "arbitrary")
    if rider is not None:
        in_specs, args = in_specs + rider.specs, args + tuple(rider.bufs)
        out_specs, out_shape = out_specs + rider.specs, out_shape + rider.out_shape
        scratch = scratch + rider.scratch
        sem = ("arbitrary",) * 3
    outs = pl.pallas_call(
        body,
        name=name,
        grid=grid,
        in_specs=in_specs,
        out_specs=out_specs,
        out_shape=out_shape,
        scratch_shapes=scratch,
        compiler_params=_params(sem),
    )(*args)
    return outs[0] if rider is None else (outs[0], list(outs[1:]))


def _mm_host(a, b, *, rider, **kw):
    out = _mm(a, b, rider=rider, **kw)
    return out if rider is not None else (out, [])


MM_FUSED_MARGIN_BYTES = 10 * 1024 * 1024
MM_FUSED_MAX_ROWS = 512


def _mm_fused(a, b, *, mode, name, extras, outs, epilogue):
    if mode == "nn":
        (M, K), (K2, N) = a.shape, b.shape
        b_dims = _NN
    else:
        (M, K), (N, K2) = a.shape, b.shape
        b_dims = _NT
    assert K == K2, (a.shape, b.shape, mode)
    rows = [e for e in extras if e.shape[0] == M]
    per_row = 2 * (K * a.dtype.itemsize + sum(e.shape[1] * e.dtype.itemsize for e in rows)
                   + sum(c * jnp.dtype(d).itemsize for c, d in outs)) + 2 * N * 4
    budget = VMEM_LIMIT_BYTES - K * N * 2 - MM_FUSED_MARGIN_BYTES
    tm = next(t for t in (512, 256, 128, 64, 32, 16) if t <= MM_FUSED_MAX_ROWS and M % t == 0 and t * per_row <= budget)
    n_x = len(extras)

    def body(*refs):
        a_ref, b_ref = refs[:2]
        x_refs, o_refs = refs[2:2 + n_x], refs[2 + n_x:]
        prod = _dot(a_ref[...].astype(BF16), b_ref[...], b_dims)
        tiles = epilogue(prod, *[r[...] for r in x_refs])
        for o_ref, t in zip(o_refs, tiles):
            o_ref[...] = t.astype(o_ref.dtype)

    in_specs = [pl.BlockSpec((tm, K), lambda i: (i, 0)),
                pl.BlockSpec(b.shape, lambda i: (0, 0), pipeline_mode=pl.Buffered(1))]
    for e in extras:
        in_specs.append(pl.BlockSpec((tm, e.shape[1]), lambda i: (i, 0)) if e.shape[0] == M
                        else pl.BlockSpec(e.shape, lambda i: (0, 0)))
    return pl.pallas_call(
        body,
        name=name,
        grid=(M // tm,),
        in_specs=in_specs,
        out_specs=[pl.BlockSpec((tm, c), lambda i: (i, 0)) for c, _ in outs],
        out_shape=[jax.ShapeDtypeStruct((M, c), d) for c, d in outs],
        compiler_params=_params(("parallel",)),
    )(a, b, *extras)


def _pair_rows(blk, lane_is_a):
    zero = jnp.zeros_like(blk)
    return jnp.concatenate([jnp.where(lane_is_a, blk, zero), jnp.where(lane_is_a, zero, blk)], axis=0)


SB_STRIP = 32
SB_FWD_PAIRS = 4
SB_BWD_PAIRS = 2


def _pair_lanes(p):
    return slice(p * LANES, (p + 1) * LANES)


def _sb_scan_matrices():
    o = lax.broadcasted_iota(jnp.int32, (2 * LANES, 4 * LANES), 0)
    c = lax.broadcasted_iota(jnp.int32, (2 * LANES, 4 * LANES), 1) & (2 * LANES - 1)
    same = (o >= LANES) == (c >= LANES)
    oo, cc = o & (LANES - 1), c & (LANES - 1)
    return (jnp.where(same & (cc > oo), 1.0, 0.0).astype(BF16), jnp.where(same & (cc < oo), 1.0, 0.0).astype(BF16))


def _sb_log_terms(z):
    log_rem = -jnp.maximum(z, 0.0) - jnp.log(1.0 + jnp.exp(-jnp.abs(z)))
    return log_rem, log_rem + z


def _sb_store_split(ref, strip, val):
    hi = val.astype(BF16)
    ref[pl.ds(strip * SB_STRIP, SB_STRIP), :] = hi
    ref[pl.ds(2 * LANES + strip * SB_STRIP, SB_STRIP), :] = (val - hi.astype(F32)).astype(BF16)


def _sb_fwd(h_a, rider=None):
    assert SB_FWD_PAIRS == 4
    T = h_a.shape[0]
    tq = _pick(T, (SB_Q_BLOCK, SB_KEY_BLOCK))
    nq, per_q, nkb = T // tq, tq // SB_KEY_BLOCK, T // SB_KEY_BLOCK
    n_strips = 2 * LANES // SB_STRIP
    n_ride = rider.n if rider is not None else 0
    after_m, _ = _sb_scan_matrices()
    pairs = SB_FWD_PAIRS

    def body(*refs):
        q_ref, k_ref, v_ref, after_ref = refs[:4]
        a_ref, r_ref, n_ref = refs[4 + n_ride:7 + n_ride]
        z_ref, lb_ref, split_ref, w_ref = refs[7 + 2 * n_ride:11 + 2 * n_ride]
        ids = [pl.program_id(0)]
        if rider is not None:
            ride = (refs[4:4 + n_ride], refs[7 + n_ride:7 + 2 * n_ride], refs[-3:])
            rider.start_at_first(ids, ride)
        i = ids[0]
        q_t = [(q_ref[:, _pair_lanes(p)].astype(F32).T * (SB_HEAD_DIM ** -0.5)).astype(BF16) for p in range(pairs)]
        lane_is_a = lax.broadcasted_iota(jnp.int32, (SB_KEY_BLOCK, LANES), 1) < SB_HEAD_DIM
        q_idx = i * tq + lax.broadcasted_iota(jnp.int32, (1, tq), 1)
        key_off = lax.broadcasted_iota(jnp.int32, (SB_STRIP, 1), 0)

        def causal(ks, s):
            return (ks + (s * SB_STRIP) % SB_KEY_BLOCK + key_off) < q_idx

        def tile(kb, masked, carry):
            acc_t, ra, rb = [list(c) for c in carry]
            ks = pl.multiple_of(kb * SB_KEY_BLOCK, SB_KEY_BLOCK)
            vv = []
            for p in range(pairs):
                kk = _pair_rows(k_ref[pl.ds(ks, SB_KEY_BLOCK), _pair_lanes(p)], lane_is_a)
                vv.append(_pair_rows(v_ref[pl.ds(ks, SB_KEY_BLOCK), _pair_lanes(p)], lane_is_a))
                z_ref[p] = _dot(kk, q_t[p], _NN)
            sums = [[jnp.zeros((8, tq), F32), jnp.zeros((8, tq), F32)] for _ in range(pairs)]
            for p in range(pairs):
                for s in range(n_strips):
                    rows = pl.ds(s * SB_STRIP, SB_STRIP)
                    log_rem, log_beta = _sb_log_terms(z_ref[p, rows, :])
                    lb_ref[p, rows, :] = log_beta
                    if masked:
                        log_rem = jnp.where(causal(ks, s), log_rem, 0.0)
                    _sb_store_split(split_ref.at[p], s, log_rem)
                    head = (s * SB_STRIP) // SB_KEY_BLOCK
                    sums[p][head] = sums[p][head] + jnp.sum(log_rem.reshape(SB_STRIP // 8, 8, tq), axis=0)
            for p in range(pairs):
                z_ref[p] = _dot(after_ref[...], split_ref[p], _NN)
            for p in range(pairs):
                for s in range(n_strips):
                    rows = pl.ds(s * SB_STRIP, SB_STRIP)
                    start = ra[p] if (s * SB_STRIP) < SB_KEY_BLOCK else rb[p]
                    w = jnp.exp(lb_ref[p, rows, :] + z_ref[p, rows, :] + start)
                    if masked:
                        w = jnp.where(causal(ks, s), w, 0.0)
                    w_ref[p, rows, :] = w.astype(BF16)
            for p in range(pairs):
                acc_t[p] = acc_t[p] + _dot(vv[p], w_ref[p], _TN)
                r_ref[2 * p, kb] = ra[p]
                r_ref[2 * p + 1, kb] = rb[p]
                ra[p] = ra[p] + jnp.sum(sums[p][0], axis=0, keepdims=True)
                rb[p] = rb[p] + jnp.sum(sums[p][1], axis=0, keepdims=True)
            return tuple(acc_t), tuple(ra), tuple(rb)

        carry = (tuple(jnp.zeros((LANES, tq), F32) for _ in range(pairs)),
                 tuple(jnp.zeros((1, tq), F32) for _ in range(pairs)),
                 tuple(jnp.zeros((1, tq), F32) for _ in range(pairs)))
        for d in range(per_q):
            carry = tile(i * per_q + (per_q - 1 - d), True, carry)
        n_full = i * per_q

        def alive(c):
            top = functools.reduce(jnp.maximum, c[2] + c[3])
            return jnp.logical_and(c[0] < n_full, jnp.max(top) > SB_DEAD_LOG)

        def step(c):
            return (c[0] + 1,) + tile(n_full - 1 - c[0], False, c[1:])

        walked, acc_t, _, _ = lax.while_loop(alive, step, (jnp.int32(0),) + carry)
        for p in range(pairs):
            a_ref[:, _pair_lanes(p)] = acc_t[p].T.astype(BF16)
        n_ref[...] = jnp.zeros(n_ref.shape, F32) + walked.astype(F32)
        if rider is not None:
            rider.wait_at_last(ids, (nq,), ride)

    wide = pairs * LANES
    in_specs = [pl.BlockSpec((tq, wide), lambda i: (i, 0)),
                pl.BlockSpec((T, wide), lambda i: (0, 1), pipeline_mode=pl.Buffered(1)),
                pl.BlockSpec((T, wide), lambda i: (0, 2), pipeline_mode=pl.Buffered(1)),
                pl.BlockSpec(after_m.shape, lambda i: (0, 0), pipeline_mode=pl.Buffered(1))]
    out_specs = [pl.BlockSpec((tq, wide), lambda i: (i, 0)),
                 pl.BlockSpec((2 * pairs, nkb, 1, tq), lambda i: (0, 0, 0, i)),
                 pl.BlockSpec((1, 8, LANES), lambda i: (i, 0, 0))]
    out_shape = [jax.ShapeDtypeStruct((T, SB_WIDTH), BF16), jax.ShapeDtypeStruct((2 * pairs, nkb, 1, T), F32),
                 jax.ShapeDtypeStruct((nq, 8, LANES), F32)]
    args = (h_a, h_a, h_a, after_m)
    scratch = [pltpu.VMEM((pairs, 2 * LANES, tq), F32), pltpu.VMEM((pairs, 2 * LANES, tq), F32),
               pltpu.VMEM((pairs, 4 * LANES, tq), BF16), pltpu.VMEM((pairs, 2 * LANES, tq), BF16)]
    if rider is not None:
        in_specs, args = in_specs + rider.specs, args + tuple(rider.bufs)
        out_specs, out_shape = out_specs + rider.specs, out_shape + rider.out_shape
        scratch = scratch + rider.scratch
    outs = pl.pallas_call(
        body,
        name="sb_fwd",
        grid=(nq,),
        in_specs=in_specs,
        out_specs=out_specs,
        out_shape=out_shape,
        scratch_shapes=scratch,
        compiler_params=_params(("arbitrary",)),
    )(*args)
    return outs[0], (outs[1], outs[2]), list(outs[3:])


def _sb_bwd(h_a, d_out, saved):
    r_mat, walked_blocks = saved
    T = h_a.shape[0]
    tq = _pick(T, (SB_Q_BLOCK, SB_KEY_BLOCK))
    nq, per_q, nkb = T // tq, tq // SB_KEY_BLOCK, T // SB_KEY_BLOCK
    n_strips = 2 * LANES // SB_STRIP
    after_m, before_m = _sb_scan_matrices()
    pairs = SB_BWD_PAIRS
    groups = 4 // pairs

    def body(q_ref, k_ref, v_ref, do_ref, r_ref, n_ref, after_ref, before_ref, dq_ref, dk_ref, dv_ref,
             z_ref, lb_ref, split_ref, w_ref, da_ref, dz_ref):
        i = pl.program_id(1)

        @pl.when(i == 0)
        def _():
            dk_ref[...] = jnp.zeros_like(dk_ref)
            dv_ref[...] = jnp.zeros_like(dv_ref)

        scale = SB_HEAD_DIM ** -0.5
        q = [q_ref[:, _pair_lanes(p)] for p in range(pairs)]
        d_o = [do_ref[:, _pair_lanes(p)] for p in range(pairs)]
        q_t = [(x.astype(F32).T * scale).astype(BF16) for x in q]
        do_t = [x.astype(F32).T.astype(BF16) for x in d_o]
        lane_is_a = lax.broadcasted_iota(jnp.int32, (SB_KEY_BLOCK, LANES), 1) < SB_HEAD_DIM
        q_idx = i * tq + lax.broadcasted_iota(jnp.int32, (1, tq), 1)
        key_off = lax.broadcasted_iota(jnp.int32, (SB_STRIP, 1), 0)

        def causal(ks, s):
            return (ks + (s * SB_STRIP) % SB_KEY_BLOCK + key_off) < q_idx

        def tile(kb, masked, carry):
            dq_t, ca, cb = [list(c) for c in carry]
            ks = pl.multiple_of(kb * SB_KEY_BLOCK, SB_KEY_BLOCK)
            kk, vv = [], []
            for p in range(pairs):
                kk.append(_pair_rows(k_ref[pl.ds(ks, SB_KEY_BLOCK), _pair_lanes(p)], lane_is_a))
                vv.append(_pair_rows(v_ref[pl.ds(ks, SB_KEY_BLOCK), _pair_lanes(p)], lane_is_a))
                z_ref[p] = _dot(kk[p], q_t[p], _NN)
            for p in range(pairs):
                for s in range(n_strips):
                    rows = pl.ds(s * SB_STRIP, SB_STRIP)
                    log_rem, log_beta = _sb_log_terms(z_ref[p, rows, :])
                    lb_ref[p, rows, :] = log_beta
                    if masked:
                        log_rem = jnp.where(causal(ks, s), log_rem, 0.0)
                    _sb_store_split(split_ref.at[p], s, log_rem)
            for p in range(pairs):
                z_ref[p] = _dot(after_ref[...], split_ref[p], _NN)
                da_ref[p] = _dot(vv[p], do_t[p], _NN)
            sums = [[jnp.zeros((8, tq), F32), jnp.zeros((8, tq), F32)] for _ in range(pairs)]
            for p in range(pairs):
                for s in range(n_strips):
                    rows = pl.ds(s * SB_STRIP, SB_STRIP)
                    start = r_ref[2 * p + (s * SB_STRIP) // SB_KEY_BLOCK, kb]
                    w = jnp.exp(lb_ref[p, rows, :] + z_ref[p, rows, :] + start)
                    if masked:
                        w = jnp.where(causal(ks, s), w, 0.0)
                    w_ref[p, rows, :] = w.astype(BF16)
                    da = da_ref[p, rows, :] * w
                    da_ref[p, rows, :] = da
                    _sb_store_split(split_ref.at[p], s, da)
                    head = (s * SB_STRIP) // SB_KEY_BLOCK
                    sums[p][head] = sums[p][head] + jnp.sum(da.reshape(SB_STRIP // 8, 8, tq), axis=0)
            for p in range(pairs):
                z_ref[p] = _dot(before_ref[...], split_ref[p], _NN)
            for p in range(pairs):
                for s in range(n_strips):
                    rows = pl.ds(s * SB_STRIP, SB_STRIP)
                    base = ca[p] if (s * SB_STRIP) < SB_KEY_BLOCK else cb[p]
                    sig = jnp.exp(lb_ref[p, rows, :])
                    dz = da_ref[p, rows, :] * (1.0 - sig) - (z_ref[p, rows, :] + base) * sig
                    if masked:
                        dz = jnp.where(causal(ks, s), dz, 0.0)
                    dz_ref[p, rows, :] = (dz * scale).astype(BF16)
            for p in range(pairs):
                dq_t[p] = dq_t[p] + _dot(kk[p], dz_ref[p], _TN)
                dkk = _dot(dz_ref[p], q[p], _NN)
                dvv = _dot(w_ref[p], d_o[p], _NN)
                here = (pl.ds(ks, SB_KEY_BLOCK), _pair_lanes(p))
                dk_ref[here] += jnp.where(lane_is_a, dkk[:SB_KEY_BLOCK], dkk[SB_KEY_BLOCK:])
                dv_ref[here] += jnp.where(lane_is_a, dvv[:SB_KEY_BLOCK], dvv[SB_KEY_BLOCK:])
                ca[p] = ca[p] + jnp.sum(sums[p][0], axis=0, keepdims=True)
                cb[p] = cb[p] + jnp.sum(sums[p][1], axis=0, keepdims=True)
            return tuple(dq_t), tuple(ca), tuple(cb)

        n_full = i * per_q
        walked = jnp.clip(jnp.max(n_ref[...]).astype(jnp.int32), 0, n_full)
        carry = (tuple(jnp.zeros((LANES, tq), F32) for _ in range(pairs)),
                 tuple(jnp.zeros((1, tq), F32) for _ in range(pairs)),
                 tuple(jnp.zeros((1, tq), F32) for _ in range(pairs)))
        carry = lax.fori_loop(n_full - walked, n_full, lambda j, c: tile(j, False, c), carry)
        for d in range(per_q):
            carry = tile(i * per_q + d, True, carry)
        for p in range(pairs):
            dq_ref[:, _pair_lanes(p)] = carry[0][p].T.astype(BF16)

    wide = pairs * LANES
    mat = pl.BlockSpec(after_m.shape, lambda g, i: (0, 0), pipeline_mode=pl.Buffered(1))
    return pl.pallas_call(
        body,
        name="sb_bwd",
        grid=(groups, nq),
        in_specs=[pl.BlockSpec((tq, wide), lambda g, i: (i, g)),
                  pl.BlockSpec((T, wide), lambda g, i: (0, groups + g), pipeline_mode=pl.Buffered(1)),
                  pl.BlockSpec((T, wide), lambda g, i: (0, 2 * groups + g), pipeline_mode=pl.Buffered(1)),
                  pl.BlockSpec((tq, wide), lambda g, i: (i, g)),
                  pl.BlockSpec((2 * pairs, nkb, 1, tq), lambda g, i: (g, 0, 0, i)),
                  pl.BlockSpec((1, 8, LANES), lambda g, i: (i, 0, 0)),
                  mat, mat],
        out_specs=[pl.BlockSpec((tq, wide), lambda g, i: (i, g)),
                   pl.BlockSpec((T, wide), lambda g, i: (0, g)),
                   pl.BlockSpec((T, wide), lambda g, i: (0, g))],
        out_shape=[jax.ShapeDtypeStruct((T, SB_WIDTH), BF16),
                   jax.ShapeDtypeStruct((T, SB_WIDTH), F32),
                   jax.ShapeDtypeStruct((T, SB_WIDTH), F32)],
        scratch_shapes=[pltpu.VMEM((pairs, 2 * LANES, tq), F32), pltpu.VMEM((pairs, 2 * LANES, tq), F32),
                        pltpu.VMEM((pairs, 4 * LANES, tq), BF16), pltpu.VMEM((pairs, 2 * LANES, tq), BF16),
                        pltpu.VMEM((pairs, 2 * LANES, tq), F32), pltpu.VMEM((pairs, 2 * LANES, tq), BF16)],
        compiler_params=_params(("parallel", "arbitrary")),
    )(h_a, h_a, h_a, d_out, r_mat, walked_blocks, after_m, before_m)


def _ret_tables(T):
    half = RET_QK_DIM // 2
    inv = 1.0 / (ROPE_BASE ** (jnp.arange(half, dtype=F32) / half))
    ang = jnp.arange(T, dtype=F32)[:, None] * inv[None, :]
    cos, sin = jnp.cos(ang), jnp.sin(ang)
    cos_t = jnp.concatenate([cos, cos], axis=1)
    sin_t = jnp.concatenate([-sin, sin], axis=1)
    log_gamma = jnp.log1p(-jnp.exp2(-5.0 - jnp.arange(RET_HEADS, dtype=F32)))
    idx = jnp.arange(RET_CHUNK, dtype=F32)
    rel = idx[:, None] - idx[None, :]
    decay = jnp.where(rel[None] >= 0, jnp.exp(log_gamma[:, None, None] * jnp.maximum(rel, 0.0)[None]), 0.0)
    k_decay = jnp.exp(log_gamma[None, :] * (RET_CHUNK - 1.0 - idx)[:, None])
    q_decay = jnp.exp(log_gamma[None, :] * (idx + 1.0)[:, None])
    chunk_decay = jnp.exp(log_gamma * RET_CHUNK)
    k_dec = jnp.broadcast_to(k_decay.T[:, :, None], (RET_HEADS, RET_CHUNK, LANES))
    q_dec = jnp.broadcast_to(q_decay.T[:, :, None], (RET_HEADS, RET_CHUNK, LANES))
    c_dec = jnp.broadcast_to(chunk_decay[:, None, None], (RET_HEADS, 8, LANES))
    return cos_t, sin_t, decay, k_dec, q_dec, c_dec


def _rotary(x, cos_t, sin_t):
    return x * cos_t + pltpu.roll(x, RET_QK_DIM // 2, 1) * sin_t


def _rotary_transpose(dy, cos_t, sin_t):
    return dy * cos_t + pltpu.roll(dy * sin_t, RET_QK_DIM // 2, 1)


def _head_norm(o):
    mu = jnp.mean(o, axis=1, keepdims=True)
    cen = o - mu
    var = jnp.mean(cen * cen, axis=1, keepdims=True)
    rstd = lax.rsqrt(var + LN_EPS)
    return cen * rstd, rstd


def _ret_specs(nc, reverse):
    def n_of(n):
        return (nc - 1 - n) if reverse else n

    q_spec = pl.BlockSpec((RET_CHUNK, RET_QK_WIDTH), lambda n: (n_of(n), 0))
    k_spec = pl.BlockSpec((RET_CHUNK, RET_QK_WIDTH), lambda n: (n_of(n), 1))
    vv = pl.BlockSpec((RET_CHUNK, RET_V_WIDTH), lambda n: (n_of(n), 0))
    pos = pl.BlockSpec((RET_CHUNK, LANES), lambda n: (n_of(n), 0))
    per_head = pl.BlockSpec((RET_HEADS, RET_CHUNK, LANES), lambda n: (0, 0, 0))
    c_dec = pl.BlockSpec((RET_HEADS, 8, LANES), lambda n: (0, 0, 0))
    state = pl.BlockSpec((RET_HEADS, 1, RET_QK_DIM, RET_V_DIM), lambda n: (0, n_of(n), 0, 0))
    return q_spec, k_spec, vv, pos, per_head, c_dec, state


def _qk_cols(h):
    return slice(h * RET_QK_DIM, (h + 1) * RET_QK_DIM)


def _v_cols(h):
    return slice(h * RET_V_DIM, (h + 1) * RET_V_DIM)


def _ret_fwd(h_b, h_c, h_d, tables):
    T = h_b.shape[0]
    nc = T // RET_CHUNK
    q_spec, k_spec, vv, pos, per_head, c_dec, state = _ret_specs(nc, False)

    def body(q_ref, k_ref, v_ref, g_ref, cos_ref, sin_ref, dec_ref, kd_ref, qd_ref, cd_ref,
             y_ref, o_ref, st_ref, state_ref):
        @pl.when(pl.program_id(0) == 0)
        def _():
            state_ref[...] = jnp.zeros_like(state_ref)

        cos_t, sin_t = cos_ref[...], sin_ref[...]
        for h in range(RET_HEADS):
            q = _rotary(q_ref[:, _qk_cols(h)], cos_t, sin_t) * (RET_QK_DIM ** -0.5)
            k = _rotary(k_ref[:, _qk_cols(h)], cos_t, sin_t)
            v = v_ref[:, _v_cols(h)]
            prev = state_ref[h]
            scores = _dot(q.astype(BF16), k.astype(BF16), _NT) * dec_ref[h]
            inner = _dot(scores.astype(BF16), v, _NN)
            cross = _dot((q * qd_ref[h]).astype(BF16), prev.astype(BF16), _NN)
            o = inner + cross
            st_ref[h, 0] = prev
            kv = _dot((k * kd_ref[h]).astype(BF16), v, _TN)
            state_ref[h] = prev * cd_ref[h, 0:1, 0:1] + kv
            o_ref[:, _v_cols(h)] = o
            normed, _ = _head_norm(o)
            gate = g_ref[:, _v_cols(h)]
            y_ref[:, _v_cols(h)] = (gate * jax.nn.sigmoid(gate) * normed).astype(BF16)

    return pl.pallas_call(
        body,
        name="ret_fwd",
        grid=(nc,),
        in_specs=[q_spec, k_spec, vv, vv, pos, pos, per_head, per_head, per_head, c_dec],
        out_specs=[vv, vv, state],
        out_shape=[jax.ShapeDtypeStruct((T, RET_V_WIDTH), BF16),
                   jax.ShapeDtypeStruct((T, RET_V_WIDTH), F32),
                   jax.ShapeDtypeStruct((RET_HEADS, nc, RET_QK_DIM, RET_V_DIM), F32)],
        scratch_shapes=[pltpu.VMEM((RET_HEADS, RET_QK_DIM, RET_V_DIM), F32)],
        compiler_params=_params(("arbitrary",)),
    )(h_b, h_b, h_c, h_d, *tables)


def _ret_bwd(d_y, o_pre, states, h_b, h_c, h_d, tables, rider=None):
    T = h_b.shape[0]
    nc = T // RET_CHUNK
    q_spec, k_spec, vv, pos, per_head, c_dec, state = _ret_specs(nc, True)
    n_ride = rider.n if rider is not None else 0

    def body(*refs):
        (dy_ref, o_ref, st_ref, q_ref, k_ref, v_ref, g_ref, cos_ref, sin_ref, dec_ref, kd_ref, qd_ref,
         cd_ref) = refs[:13]
        dq_ref, dk_ref, dv_ref, dg_ref = refs[13 + n_ride:17 + n_ride]
        carry_ref = refs[17 + 2 * n_ride]
        ids = [pl.program_id(0)]
        if rider is not None:
            ride = (refs[13:13 + n_ride], refs[17 + n_ride:17 + 2 * n_ride], refs[-3:])
            rider.start_at_first(ids, ride)

        @pl.when(ids[0] == 0)
        def _():
            carry_ref[...] = jnp.zeros_like(carry_ref)

        cos_t, sin_t = cos_ref[...], sin_ref[...]
        scale = RET_QK_DIM ** -0.5
        for h in range(RET_HEADS):
            q = _rotary(q_ref[:, _qk_cols(h)], cos_t, sin_t) * scale
            k = _rotary(k_ref[:, _qk_cols(h)], cos_t, sin_t)
            v = v_ref[:, _v_cols(h)]
            decay, k_dec, q_dec = dec_ref[h], kd_ref[h], qd_ref[h]
            chunk_decay = cd_ref[h, 0:1, 0:1]
            state = st_ref[h, 0].astype(BF16)
            later = carry_ref[h]
            later_b = later.astype(BF16)

            gate = g_ref[:, _v_cols(h)]
            sig = jax.nn.sigmoid(gate)
            silu = gate * sig
            normed, rstd = _head_norm(o_ref[:, _v_cols(h)])
            d_y = dy_ref[:, _v_cols(h)]
            dg_ref[:, _v_cols(h)] = (d_y * normed * (sig * (1.0 + gate * (1.0 - sig)))).astype(BF16)
            d_n = d_y * silu
            d_o = rstd * (d_n - jnp.mean(d_n, axis=1, keepdims=True)
                          - normed * jnp.mean(d_n * normed, axis=1, keepdims=True))
            d_ob = d_o.astype(BF16)

            qb, kb = q.astype(BF16), k.astype(BF16)
            qd_b, kd_b = (q * q_dec).astype(BF16), (k * k_dec).astype(BF16)
            scores = _dot(qb, kb, _NT) * decay
            d_scores = (_dot(d_ob, v, _NT) * decay).astype(BF16)
            dq = _dot(d_scores, kb, _NN) + _dot(d_ob, state, _NT) * q_dec
            dk = _dot(d_scores, qb, _TN) + _dot(v, later_b, _NT) * k_dec
            dv = _dot(scores.astype(BF16), d_ob, _TN) + _dot(kd_b, later_b, _NN)
            carry_ref[h] = _dot(qd_b, d_ob, _TN) + chunk_decay * later
            dq_ref[:, _qk_cols(h)] = _rotary_transpose(dq * scale, cos_t, sin_t).astype(BF16)
            dk_ref[:, _qk_cols(h)] = _rotary_transpose(dk, cos_t, sin_t).astype(BF16)
            dv_ref[:, _v_cols(h)] = dv.astype(BF16)
        if rider is not None:
            rider.wait_at_last(ids, (nc,), ride)

    qk_out = pl.BlockSpec((RET_CHUNK, RET_QK_WIDTH), lambda n: (nc - 1 - n, 0))
    in_specs = [vv, vv, state, q_spec, k_spec, vv, vv, pos, pos, per_head, per_head, per_head, c_dec]
    out_specs = [qk_out, qk_out, vv, vv]
    out_shape = [jax.ShapeDtypeStruct((T, RET_QK_WIDTH), BF16), jax.ShapeDtypeStruct((T, RET_QK_WIDTH), BF16),
                 jax.ShapeDtypeStruct((T, RET_V_WIDTH), BF16), jax.ShapeDtypeStruct((T, RET_V_WIDTH), BF16)]
    args = (d_y, o_pre, states, h_b, h_b, h_c, h_d) + tuple(tables)
    scratch = [pltpu.VMEM((RET_HEADS, RET_QK_DIM, RET_V_DIM), F32)]
    if rider is not None:
        in_specs, args = in_specs + rider.specs, args + tuple(rider.bufs)
        out_specs, out_shape = out_specs + rider.specs, out_shape + rider.out_shape
        scratch = scratch + rider.scratch
    outs = pl.pallas_call(
        body,
        name="ret_bwd",
        grid=(nc,),
        in_specs=in_specs,
        out_specs=out_specs,
        out_shape=out_shape,
        scratch_shapes=scratch,
        compiler_params=_params(("arbitrary",)),
    )(*args)
    return outs[0], outs[1], outs[2], outs[3], list(outs[4:])


def _row_tile(T):
    return _pick(T, (256, 128))


def _gate_mix_tiles(y_ret, h_e, b_gate, y_sb):
    gates = jax.nn.sigmoid(h_e + b_gate)
    return y_ret, gates[:, :D_MODEL] * y_sb + gates[:, D_MODEL:] * y_ret


def _col_sum_update(acc_ref, val, first):
    part = jnp.sum(val.reshape(val.shape[0] // 8, 8, val.shape[1]), axis=0)

    @pl.when(first)
    def _():
        acc_ref[...] = part

    @pl.when(jnp.logical_not(first))
    def _():
        acc_ref[...] += part


def _gate_mix_bwd(d_mix, h_e, b_gate, y_sb, y_ret):
    T = h_e.shape[0]
    tr = _row_tile(T)
    steps = T // tr

    def body(dm_ref, g_ref, b_ref, ys_ref, yr_ref, dys_ref, dyr_ref, de_ref, db_ref, acc_ref):
        i = pl.program_id(0)
        dm = dm_ref[...]
        gates = jax.nn.sigmoid(g_ref[...] + b_ref[...])
        g0, g1 = gates[:, :D_MODEL], gates[:, D_MODEL:]
        dys_ref[...] = (dm * g0).astype(BF16)
        dyr_ref[...] = (dm * g1).astype(BF16)
        de = jnp.concatenate([dm * ys_ref[...] * g0 * (1.0 - g0), dm * yr_ref[...] * g1 * (1.0 - g1)], axis=1)
        de_ref[...] = de.astype(BF16)
        _col_sum_update(acc_ref, de, i == 0)

        @pl.when(i == steps - 1)
        def _():
            db_ref[...] = jnp.sum(acc_ref[...], axis=0, keepdims=True)

    row = pl.BlockSpec((tr, D_MODEL), lambda i: (i, 0))
    wide = pl.BlockSpec((tr, 2 * D_MODEL), lambda i: (i, 0))
    vec = pl.BlockSpec((1, 2 * D_MODEL), lambda i: (0, 0))
    return pl.pallas_call(
        body,
        name="gate_mix_bwd",
        grid=(steps,),
        in_specs=[row, wide, vec, row, row],
        out_specs=[row, row, wide, vec],
        out_shape=[jax.ShapeDtypeStruct((T, D_MODEL), BF16), jax.ShapeDtypeStruct((T, D_MODEL), BF16),
                   jax.ShapeDtypeStruct((T, 2 * D_MODEL), BF16), jax.ShapeDtypeStruct((1, 2 * D_MODEL), F32)],
        scratch_shapes=[pltpu.VMEM((8, 2 * D_MODEL), F32)],
        compiler_params=_params(("arbitrary",)),
    )(d_mix, h_e, b_gate, y_sb, y_ret)


def _ln_stats(u):
    mu = jnp.mean(u, axis=1, keepdims=True)
    cen = u - mu
    var = jnp.mean(cen * cen, axis=1, keepdims=True)
    rstd = lax.rsqrt(var + LN_EPS)
    return cen * rstd, rstd


def _ln_input_grad(d_out, gain, xhat, rstd):
    d_hat = d_out * gain
    return rstd * (d_hat - jnp.mean(d_hat, axis=1, keepdims=True)
                   - xhat * jnp.mean(d_hat * xhat, axis=1, keepdims=True))


def _ln_tiles(sub, x_prev, gain, bias):
    xhat, rstd = _ln_stats(DN_ALPHA * x_prev + sub)
    out = xhat * gain + bias
    return out, out, xhat, rstd


def _ln_bwd(d_out, xhat, rstd, gain, name):
    T = d_out.shape[0]
    tr = _row_tile(T)
    steps = T // tr

    def body(d_ref, xh_ref, rs_ref, g_ref, du_ref, dub_ref, dg_ref, db_ref, accg_ref, accb_ref):
        i = pl.program_id(0)
        d_o, xh = d_ref[...], xh_ref[...]
        du = _ln_input_grad(d_o, g_ref[...], xh, rs_ref[...])
        du_ref[...] = du
        dub_ref[...] = du.astype(BF16)
        _col_sum_update(accg_ref, d_o * xh, i == 0)
        _col_sum_update(accb_ref, d_o, i == 0)

        @pl.when(i == steps - 1)
        def _():
            dg_ref[...] = jnp.sum(accg_ref[...], axis=0, keepdims=True)
            db_ref[...] = jnp.sum(accb_ref[...], axis=0, keepdims=True)

    row = pl.BlockSpec((tr, D_MODEL), lambda i: (i, 0))
    vec = pl.BlockSpec((1, D_MODEL), lambda i: (0, 0))
    return pl.pallas_call(
        body,
        name=name,
        grid=(steps,),
        in_specs=[row, row, pl.BlockSpec((tr, 1), lambda i: (i, 0)), vec],
        out_specs=[row, row, vec, vec],
        out_shape=[jax.ShapeDtypeStruct((T, D_MODEL), F32), jax.ShapeDtypeStruct((T, D_MODEL), BF16),
                   jax.ShapeDtypeStruct((1, D_MODEL), F32), jax.ShapeDtypeStruct((1, D_MODEL), F32)],
        scratch_shapes=[pltpu.VMEM((8, D_MODEL), F32), pltpu.VMEM((8, D_MODEL), F32)],
        compiler_params=_params(("arbitrary",)),
    )(d_out, xhat, rstd, gain)


def _ln_loss(x_prev, sub, gain, bias, target):
    T = x_prev.shape[0]
    tr = _row_tile(T)
    steps = T // tr

    def body(x_ref, s_ref, g_ref, b_ref, t_ref, loss_ref, du_ref, dub_ref, dg_ref, db_ref, accl_ref, accg_ref,
             accb_ref):
        i = pl.program_id(0)
        gain_v = g_ref[...]
        xhat, rstd = _ln_stats(DN_ALPHA * x_ref[...] + s_ref[...])
        diff = xhat * gain_v + b_ref[...] - t_ref[...]
        d_o = diff * (1.0 / D_MODEL)
        du = _ln_input_grad(d_o, gain_v, xhat, rstd)
        du_ref[...] = du
        dub_ref[...] = du.astype(BF16)
        _col_sum_update(accl_ref, diff * diff, i == 0)
        _col_sum_update(accg_ref, d_o * xhat, i == 0)
        _col_sum_update(accb_ref, d_o, i == 0)

        @pl.when(i == steps - 1)
        def _():
            per_col = jnp.sum(accl_ref[...], axis=0, keepdims=True)
            loss_ref[...] = jnp.sum(per_col, axis=1, keepdims=True) * (0.5 / D_MODEL)
            dg_ref[...] = jnp.sum(accg_ref[...], axis=0, keepdims=True)
            db_ref[...] = jnp.sum(accb_ref[...], axis=0, keepdims=True)

    row = pl.BlockSpec((tr, D_MODEL), lambda i: (i, 0))
    vec = pl.BlockSpec((1, D_MODEL), lambda i: (0, 0))
    return pl.pallas_call(
        body,
        name="ln3_loss",
        grid=(steps,),
        in_specs=[row, row, vec, vec, row],
        out_specs=[pl.BlockSpec((1, 1), lambda i: (0, 0)), row, row, vec, vec],
        out_shape=[jax.ShapeDtypeStruct((1, 1), F32), jax.ShapeDtypeStruct((T, D_MODEL), F32),
                   jax.ShapeDtypeStruct((T, D_MODEL), BF16),
                   jax.ShapeDtypeStruct((1, D_MODEL), F32), jax.ShapeDtypeStruct((1, D_MODEL), F32)],
        scratch_shapes=[pltpu.VMEM((8, D_MODEL), F32)] * 3,
        compiler_params=_params(("arbitrary",)),
    )(x_prev, sub, gain, bias, target)


def _mem_probs(q_h, k_h):
    s = _dot(q_h, k_h, _NT) * (MEM_HEAD_DIM ** -0.5)
    e = jnp.exp(s - jnp.max(s, axis=1, keepdims=True))
    return e / jnp.sum(e, axis=1, keepdims=True)


def _xattn_fwd(q, kv):
    T, mem_len = q.shape[0], kv.shape[0]
    tq = _pick(T, (512, 256, 128))

    def body(q_ref, kv_ref, o_ref):
        for h in range(MEM_HEADS):
            cols = slice(h * MEM_HEAD_DIM, (h + 1) * MEM_HEAD_DIM)
            vcols = slice(D_MODEL + h * MEM_HEAD_DIM, D_MODEL + (h + 1) * MEM_HEAD_DIM)
            p = _mem_probs(q_ref[:, cols], kv_ref[:, cols])
            o_ref[:, cols] = _dot(p.astype(BF16), kv_ref[:, vcols], _NN).astype(BF16)

    return pl.pallas_call(
        body,
        name="xattn_fwd",
        grid=(T // tq,),
        in_specs=[pl.BlockSpec((tq, D_MODEL), lambda i: (i, 0)),
                  pl.BlockSpec((mem_len, 2 * D_MODEL), lambda i: (0, 0))],
        out_specs=pl.BlockSpec((tq, D_MODEL), lambda i: (i, 0)),
        out_shape=jax.ShapeDtypeStruct((T, D_MODEL), BF16),
        compiler_params=_params(("parallel",)),
    )(q, kv)


def _xattn_bwd(q, kv, d_o):
    T, mem_len = q.shape[0], kv.shape[0]
    tq = _pick(T, (512, 256, 128))

    def body(q_ref, kv_ref, do_ref, dq_ref, dkv_ref):
        @pl.when(pl.program_id(0) == 0)
        def _():
            dkv_ref[...] = jnp.zeros_like(dkv_ref)

        for h in range(MEM_HEADS):
            cols = slice(h * MEM_HEAD_DIM, (h + 1) * MEM_HEAD_DIM)
            vcols = slice(D_MODEL + h * MEM_HEAD_DIM, D_MODEL + (h + 1) * MEM_HEAD_DIM)
            q_h, k_h, do_h = q_ref[:, cols], kv_ref[:, cols], do_ref[:, cols]
            p = _mem_probs(q_h, k_h)
            dp = _dot(do_h, kv_ref[:, vcols], _NT)
            ds = p * (dp - jnp.sum(dp * p, axis=1, keepdims=True))
            dsb = (ds * (MEM_HEAD_DIM ** -0.5)).astype(BF16)
            dq_ref[:, cols] = _dot(dsb, k_h, _NN).astype(BF16)
            dkv_ref[:, cols] += _dot(dsb, q_h, _TN)
            dkv_ref[:, vcols] += _dot(p.astype(BF16), do_h, _TN)

    row = pl.BlockSpec((tq, D_MODEL), lambda i: (i, 0))
    full = pl.BlockSpec((mem_len, 2 * D_MODEL), lambda i: (0, 0))
    return pl.pallas_call(
        body,
        name="xattn_bwd",
        grid=(T // tq,),
        in_specs=[row, full, row],
        out_specs=[row, full],
        out_shape=[jax.ShapeDtypeStruct((T, D_MODEL), BF16), jax.ShapeDtypeStruct((mem_len, 2 * D_MODEL), F32)],
        compiler_params=_params(("arbitrary",)),
    )(q, kv, d_o)


def _swiglu_tiles(f):
    a, b = f[:, :FFN_HIDDEN], f[:, FFN_HIDDEN:]
    return f, a * jax.nn.sigmoid(a) * b


def _swiglu_grad_tiles(d_hidden, f):
    a, b = f[:, :FFN_HIDDEN], f[:, FFN_HIDDEN:]
    sig = jax.nn.sigmoid(a)
    return (jnp.concatenate([d_hidden * b * (sig * (1.0 + a * (1.0 - sig))), d_hidden * (a * sig)], axis=1),)


def _local_step(x, mem, w_in, small, target, fetch_rest, ship):
    T = x.shape[0]
    tables = _ret_tables(T)
    xb, memb = x.astype(BF16), mem.astype(BF16)

    h_a = _mm(xb, w_in[:, 0:1536], mode="nn", out_dtype=BF16, name="proj_sb")
    h_b = _mm(xb, w_in[:, 1536:2560], mode="nn", out_dtype=F32, name="proj_ret_qk")
    h_c = _mm(xb, w_in[:, 2560:3584], mode="nn", out_dtype=BF16, name="proj_ret_v")
    h_d = _mm(xb, w_in[:, 3584:4608], mode="nn", out_dtype=F32, name="proj_ret_g")
    h_e = _mm(xb, w_in[:, 4608:6656], mode="nn", out_dtype=F32, name="proj_gate")
    (a_sb, r_mat, _), w = fetch_rest(lambda rider: _sb_fwd(h_a, rider))
    y_gated, o_pre, states = _ret_fwd(h_b, h_c, h_d, tables)
    y_sb = _mm(a_sb, w["w_sb_o"], mode="nn", out_dtype=F32, name="sb_out")
    row_f32, row_bf16 = (D_MODEL, F32), (D_MODEL, BF16)
    ln_outs = [row_f32, row_bf16, row_f32, (1, F32)]
    y_ret, mix_in = _mm_fused(y_gated, w["w_ret_o"], mode="nn", name="ret_out", extras=[h_e, small["b_gate"], y_sb],
                              outs=[row_f32, row_bf16], epilogue=_gate_mix_tiles)
    x1, x1b, xhat1, rstd1 = _mm_fused(mix_in, w["w_mix_o"], mode="nn", name="mix_out",
                                      extras=[x, small["ln1_g"], small["ln1_b"]], outs=ln_outs, epilogue=_ln_tiles)
    q_m = _mm(x1b, w["w_mem_q"], mode="nn", out_dtype=BF16, name="mem_q")
    kv_m = _mm(memb, w["w_mem_kv"], mode="nn", out_dtype=BF16, name="mem_kv")
    o_m = _xattn_fwd(q_m, kv_m)
    x2, x2b, xhat2, rstd2 = _mm_fused(o_m, w["w_mem_o"], mode="nn", name="mem_out",
                                      extras=[x1, small["ln2_g"], small["ln2_b"]], outs=ln_outs, epilogue=_ln_tiles)
    f, hidden = _mm_fused(x2b, w["w_ffn_in"], mode="nn", name="ffn_in", extras=[],
                          outs=[(2 * FFN_HIDDEN, F32), (FFN_HIDDEN, BF16)], epilogue=_swiglu_tiles)
    ff = _mm(hidden, w["w_ffn_out"], mode="nn", out_dtype=F32, name="ffn_out")
    loss, du3, du3b, d_ln3_g, d_ln3_b = _ln_loss(x2, ff, small["ln3_g"], small["ln3_b"], target)

    g_ffn_out = _mm(hidden, du3b, mode="tn", out_dtype=F32, name="g_ffn_out")
    (d_f,) = _mm_fused(du3b, w["w_ffn_out"], mode="nt", name="d_hidden", extras=[f],
                       outs=[(2 * FFN_HIDDEN, BF16)], epilogue=_swiglu_grad_tiles)
    g_ffn_in = _mm(x2b, d_f, mode="tn", out_dtype=F32, name="g_ffn_in")
    (d_x2,) = ship({"w_ffn_out": g_ffn_out},
                   lambda rider: _mm_host(d_f, w["w_ffn_in"], mode="nt", out_dtype=F32, name="d_x2", res=du3,
                                          res_scale=DN_ALPHA, rider=rider))
    du2, du2b, d_ln2_g, d_ln2_b = _ln_bwd(d_x2, xhat2, rstd2, small["ln2_g"], "ln2_bwd")
    g_mem_o = _mm(o_m, du2b, mode="tn", out_dtype=F32, name="g_mem_o")
    d_om = _mm(du2b, w["w_mem_o"], mode="nt", out_dtype=BF16, name="d_om")
    d_qm, d_kvm = _xattn_bwd(q_m, kv_m, d_om)
    g_mem_q = _mm(x1b, d_qm, mode="tn", out_dtype=F32, name="g_mem_q")
    g_mem_kv = _mm(memb, d_kvm.astype(BF16), mode="tn", out_dtype=F32, name="g_mem_kv")
    d_x1 = _mm(d_qm, w["w_mem_q"], mode="nt", out_dtype=F32, name="d_x1", res=du2, res_scale=DN_ALPHA)
    du1, du1b, d_ln1_g, d_ln1_b = _ln_bwd(d_x1, xhat1, rstd1, small["ln1_g"], "ln1_bwd")
    g_mix_o = _mm(mix_in, du1b, mode="tn", out_dtype=F32, name="g_mix_o")
    d_mix_in = _mm(du1b, w["w_mix_o"], mode="nt", out_dtype=F32, name="d_mix_in")
    d_ysb, d_yret, d_e, d_b_gate = _gate_mix_bwd(d_mix_in, h_e, small["b_gate"], y_sb, y_ret)
    g_sb_o = _mm(a_sb, d_ysb, mode="tn", out_dtype=F32, name="g_sb_o")
    g_ret_o = _mm(y_gated, d_yret, mode="tn", out_dtype=F32, name="g_ret_o")
    d_asb = _mm(d_ysb, w["w_sb_o"], mode="nt", out_dtype=BF16, name="d_asb")
    d_ygated = _mm(d_yret, w["w_ret_o"], mode="nt", out_dtype=F32, name="d_ygated")
    small_grads = {"b_gate": d_b_gate, "ln1_g": d_ln1_g, "ln1_b": d_ln1_b, "ln2_g": d_ln2_g, "ln2_b": d_ln2_b,
                   "ln3_g": d_ln3_g, "ln3_b": d_ln3_b}
    d_rq, d_rk, d_c, d_d = ship({"w_ffn_in": g_ffn_in},
                                lambda rider: _ret_bwd(d_ygated, o_pre, states, h_b, h_c, h_d, tables, rider))
    d_q, d_k, d_v = _sb_bwd(h_a, d_asb, r_mat)
    d_h = jnp.concatenate([d_q, d_k.astype(BF16), d_v.astype(BF16), d_rq, d_rk, d_c, d_d, d_e], axis=1)
    late = {"w_mem_kv": g_mem_kv, "w_mem_q": g_mem_q, "w_mem_o": g_mem_o, "w_mix_o": g_mix_o, "w_ret_o": g_ret_o,
            "w_sb_o": g_sb_o, "small": small_grads}
    (g_in,) = ship(late, lambda rider: _mm_host(xb, d_h, mode="tn", out_dtype=F32, name="g_in", rider=rider))
    (d_x,) = ship({"w_in": g_in},
                  lambda rider: _mm_host(d_h, w_in, mode="nt", out_dtype=F32, name="d_x", res=du1, res_scale=DN_ALPHA,
                                         rider=rider))
    return loss, d_x


def _adamw_math(w, g, m, v):
    m = ADAM_B1 * m + (1.0 - ADAM_B1) * g
    v = ADAM_B2 * v + (1.0 - ADAM_B2) * jnp.square(g)
    m_hat = m / (1.0 - ADAM_B1 ** ADAM_STEP)
    v_hat = v / (1.0 - ADAM_B2 ** ADAM_STEP)
    delta = -ADAM_LR * (m_hat / (jnp.sqrt(v_hat) + ADAM_EPS) + ADAM_WD * w)
    return delta, m, v


def _adamw(parts, w, m, v, name):
    R, C = w.shape
    tr = max(t for t in range(16, min(R, 256) + 1, 16) if R % t == 0) if R >= 16 else R

    def body(p_ref, w_ref, m_ref, v_ref, g_ref, d_ref, nm_ref, nv_ref):
        g = p_ref[0].astype(F32)
        for j in range(1, N_DEV):
            g = g + p_ref[j].astype(F32)
        delta, nm, nv = _adamw_math(w_ref[...], g, m_ref[...], v_ref[...])
        g_ref[...] = g
        d_ref[...] = delta
        nm_ref[...] = nm
        nv_ref[...] = nv

    blk = pl.BlockSpec((tr, C), lambda i: (i, 0))
    out = jax.ShapeDtypeStruct((R, C), F32)
    return pl.pallas_call(
        body,
        name=name,
        grid=(R // tr,),
        in_specs=[pl.BlockSpec((N_DEV, tr, C), lambda i: (0, i, 0)), blk, blk, blk],
        out_specs=[blk] * 4,
        out_shape=[out] * 4,
        compiler_params=_params(("parallel",)),
    )(parts, w, m, v)


_SHARD_AXIS = {"w_in": 1, "w_sb_o": 1, "w_ret_o": 0, "w_mix_o": 0, "w_mem_q": 0, "w_mem_kv": 1, "w_mem_o": 0,
               "w_ffn_in": 1, "w_ffn_out": 0}
_MATRICES = tuple(_SHARD_AXIS)
_SMALL = ("b_gate", "ln1_g", "ln1_b", "ln2_g", "ln2_b", "ln3_g", "ln3_b")
_WEIGHT_ORDER = ("w_in", "b_gate", "w_sb_o", "w_ret_o", "w_mix_o", "ln1_g", "ln1_b", "w_mem_q", "w_mem_kv", "w_mem_o",
                 "ln2_g", "ln2_b", "w_ffn_in", "w_ffn_out", "ln3_g", "ln3_b")


def _assemble(name, gathered):
    if _SHARD_AXIS[name] == 0:
        return gathered.reshape(-1, gathered.shape[2])
    return jnp.transpose(gathered, (1, 0, 2)).reshape(gathered.shape[1], -1)


def _to_slots(name, full):
    if _SHARD_AXIS[name] == 0:
        return full.reshape(N_DEV, full.shape[0] // N_DEV, full.shape[1])
    return jnp.transpose(full.reshape(full.shape[0], N_DEV, full.shape[1] // N_DEV), (1, 0, 2))


def _pack_small(vals):
    return jnp.concatenate([vals["b_gate"].reshape(2, D_MODEL)] + [vals[n] for n in _SMALL[1:]], axis=0)


def _unpack_small(packed):
    out = {"b_gate": packed[0:2].reshape(1, 2 * D_MODEL)}
    for i, n in enumerate(_SMALL[1:]):
        out[n] = packed[2 + i:3 + i]
    return out


def kernel(x, mem, w_in, b_gate, w_sb_o, w_ret_o, w_mix_o, ln1_g, ln1_b, w_mem_q, w_mem_kv, w_mem_o, ln2_g, ln2_b, w_ffn_in, w_ffn_out, ln3_g, ln3_b, loss_target, m_w_in, m_b_gate, m_w_sb_o, m_w_ret_o, m_w_mix_o, m_ln1_g, m_ln1_b, m_w_mem_q, m_w_mem_kv, m_w_mem_o, m_ln2_g, m_ln2_b, m_w_ffn_in, m_w_ffn_out, m_ln3_g, m_ln3_b, v_w_in, v_b_gate, v_w_sb_o, v_w_ret_o, v_w_mix_o, v_ln1_g, v_ln1_b, v_w_mem_q, v_w_mem_kv, v_w_mem_o, v_ln2_g, v_ln2_b, v_w_ffn_in, v_w_ffn_out, v_ln3_g, v_ln3_b):
    weights = dict(w_in=w_in, b_gate=b_gate, w_sb_o=w_sb_o, w_ret_o=w_ret_o, w_mix_o=w_mix_o, ln1_g=ln1_g, ln1_b=ln1_b,
                   w_mem_q=w_mem_q, w_mem_kv=w_mem_kv, w_mem_o=w_mem_o, ln2_g=ln2_g, ln2_b=ln2_b, w_ffn_in=w_ffn_in,
                   w_ffn_out=w_ffn_out, ln3_g=ln3_g, ln3_b=ln3_b)
    mom1 = dict(w_in=m_w_in, b_gate=m_b_gate, w_sb_o=m_w_sb_o, w_ret_o=m_w_ret_o, w_mix_o=m_w_mix_o, ln1_g=m_ln1_g,
                ln1_b=m_ln1_b, w_mem_q=m_w_mem_q, w_mem_kv=m_w_mem_kv, w_mem_o=m_w_mem_o, ln2_g=m_ln2_g, ln2_b=m_ln2_b,
                w_ffn_in=m_w_ffn_in, w_ffn_out=m_w_ffn_out, ln3_g=m_ln3_g, ln3_b=m_ln3_b)
    mom2 = dict(w_in=v_w_in, b_gate=v_b_gate, w_sb_o=v_w_sb_o, w_ret_o=v_w_ret_o, w_mix_o=v_w_mix_o, ln1_g=v_ln1_g,
                ln1_b=v_ln1_b, w_mem_q=v_w_mem_q, w_mem_kv=v_w_mem_kv, w_mem_o=v_w_mem_o, ln2_g=v_ln2_g, ln2_b=v_ln2_b,
                w_ffn_in=v_w_ffn_in, w_ffn_out=v_w_ffn_out, ln3_g=v_ln3_g, ln3_b=v_ln3_b)

    (gathered_in,) = _exchange([weights["w_in"][0].astype(BF16)], False, "gather_w_in")
    rest = [n for n in _MATRICES if n != "w_in"]
    received = {}

    def fetch_rest(host):
        res = host(_Rider([weights[n][0].astype(BF16) for n in rest], False))
        return res, {n: _assemble(n, g) for n, g in zip(rest, res[-1])}

    def ship(grads, host):
        names = list(grads)
        bufs = []
        for n in names:
            if n == "small":
                part = _pack_small(grads[n])
                bufs.append(jnp.broadcast_to(part[None], (N_DEV,) + part.shape))
            else:
                bufs.append(_to_slots(n, grads[n]).astype(BF16))
        res = host(_Rider(bufs, True))
        received.update(zip(names, res[-1]))
        return res[:-1]

    small = {n: weights[n] for n in _SMALL}
    loss, d_x = _local_step(x[0], mem[0], _assemble("w_in", gathered_in), small, loss_target[0], fetch_rest, ship)

    new = {}
    for n in _MATRICES:
        new[n] = _adamw(received[n], weights[n][0], mom1[n][0], mom2[n][0], "adamw_" + n)
    packed = _adamw(received["small"], _pack_small({n: weights[n] for n in _SMALL}),
                    _pack_small({n: mom1[n] for n in _SMALL}), _pack_small({n: mom2[n] for n in _SMALL}), "adamw_small")
    small_new = [_unpack_small(p) for p in packed]

    outs = [lax.psum(loss[0, 0], MESH_AXES), d_x[None]]
    for slot in range(4):
        for n in _WEIGHT_ORDER:
            outs.append(new[n][slot][None] if n in new else small_new[slot][n])
    return tuple(outs)
```

```python
import functools
import math

import jax
import jax.numpy as jnp
from jax import lax
from jax.experimental import pallas as pl
from jax.experimental.pallas import tpu as pltpu

F32 = jnp.float32
BF16 = jnp.bfloat16

N_DEV = 8
D_MODEL = 1024
SB_HEAD_DIM = 64
SB_WIDTH = 512
RET_HEADS = 4
RET_QK_DIM = 128
RET_V_DIM = 256
RET_QK_WIDTH = 512
RET_V_WIDTH = 1024
RET_CHUNK = 128
ROPE_BASE = 10000.0
MEM_HEADS = 4
MEM_HEAD_DIM = 256
FFN_HIDDEN = 2816
DN_ALPHA = 2.0 ** 0.25
LN_EPS = 1e-5
ADAM_LR = 0.001
ADAM_B1 = 0.9
ADAM_B2 = 0.999
ADAM_EPS = 1e-08
ADAM_WD = 0.01
ADAM_STEP = 10

VMEM_LIMIT_BYTES = 52 * 1024 * 1024
LANES = 128
SB_KEY_BLOCK = 128
SB_Q_BLOCK = 256
SB_DEAD_LOG = -105.0

MESH_AXES = ("x", "y", "c")


def _pick(dim, prefs):
    for p in prefs:
        if dim % p == 0:
            return p
    return dim


def _params(sem):
    return pltpu.CompilerParams(dimension_semantics=sem, vmem_limit_bytes=VMEM_LIMIT_BYTES)


def _dot(a, b, dims):
    return lax.dot_general(a, b, (dims, ((), ())), preferred_element_type=F32)


_NN = ((1,), (0,))
_NT = ((1,), (1,))
_TN = ((0,), (0,))


def _my_index():
    return 4 * lax.axis_index("x") + 2 * lax.axis_index("y") + lax.axis_index("c")


def _peer(k):
    x, y, c = lax.axis_index("x"), lax.axis_index("y"), lax.axis_index("c")
    bx, by, bc = (k >> 2) & 1, (k >> 1) & 1, k & 1
    px = (1 - x) if bx else x
    py = (1 - y) if by else y
    pc = (1 - c) if bc else c
    return (px, py, pc), 4 * px + 2 * py + pc


class _Rider:
    def __init__(self, bufs, scatter):
        self.bufs, self.scatter, self.n = list(bufs), scatter, len(bufs)
        self.specs = [pl.BlockSpec(memory_space=pl.ANY)] * self.n
        self.out_shape = [jax.ShapeDtypeStruct(b.shape if scatter else (N_DEV,) + b.shape, b.dtype) for b in self.bufs]
        self.scratch = [pltpu.SemaphoreType.DMA((self.n, N_DEV - 1)), pltpu.SemaphoreType.DMA((self.n, N_DEV - 1)),
                        pltpu.SemaphoreType.DMA((self.n,))]

    def _remote(self, ride, a, k, src_ref, slot, to):
        _, dst, (send_sems, recv_sems, _) = ride
        return pltpu.make_async_remote_copy(src_ref=src_ref, dst_ref=dst[a].at[slot], send_sem=send_sems.at[a, k],
                                            recv_sem=recv_sems.at[a, k], device_id=to,
                                            device_id_type=pl.DeviceIdType.MESH)

    def _local(self, ride, a):
        src, dst, (_, _, local_sems) = ride
        me = _my_index()
        return pltpu.make_async_copy(src[a].at[me] if self.scatter else src[a], dst[a].at[me], local_sems.at[a])

    def _direct(self, ride, a):
        src = ride[0]
        me = _my_index()
        out = []
        for k in range(1, N_DEV):
            peer, peer_idx = _peer(k)
            out.append(self._remote(ride, a, k - 1, src[a].at[peer_idx], me, peer))
        return out

    def _two_level(self, ride, a):
        src, dst = ride[0], ride[1]
        x, y, c = lax.axis_index("x"), lax.axis_index("y"), lax.axis_index("c")
        me, sibling = _my_index(), (x, y, 1 - c)
        chips = [(1 - x, y), (x, 1 - y), (1 - x, 1 - y)]
        first = [self._remote(ride, a, 0, src[a], me, sibling)]
        passed, landing = [], [self._remote(ride, a, 0, src[a], me + 1 - 2 * c, sibling)]
        for j, (px, py) in enumerate(chips):
            first.append(self._remote(ride, a, 1 + j, src[a], me, (px, py, c)))
            theirs = 4 * px + 2 * py + c
            passed.append(self._remote(ride, a, 4 + j, dst[a].at[theirs], theirs, sibling))
            landing.append(self._remote(ride, a, 1 + j, src[a], theirs, (px, py, c)))
        for j, (px, py) in enumerate(chips):
            landing.append(self._remote(ride, a, 4 + j, src[a], 4 * px + 2 * py + 1 - c, sibling))
        return first, passed, landing

    def start(self, ride):
        for a in range(self.n):
            self._local(ride, a).start()
            for cp in (self._direct(ride, a) if self.scatter else self._two_level(ride, a)[0]):
                cp.start()

    def finish(self, ride):
        if self.scatter:
            for a in range(self.n):
                for cp in self._direct(ride, a):
                    cp.wait()
                self._local(ride, a).wait()
            return
        levels = [self._two_level(ride, a) for a in range(self.n)]
        for first, passed, landing in levels:
            for j, cp in enumerate(passed):
                landing[1 + j].wait_recv()
                cp.start()
        for a, (first, passed, landing) in enumerate(levels):
            landing[0].wait_recv()
            for cp in landing[4:]:
                cp.wait_recv()
            for cp in first + passed:
                cp.wait_send()
            self._local(ride, a).wait()

    def start_at_first(self, ids, ride):
        first = functools.reduce(jnp.logical_and, [i == 0 for i in ids])

        @pl.when(first)
        def _():
            self.start(ride)

    def wait_at_last(self, ids, grid, ride):
        last = functools.reduce(jnp.logical_and, [i == g - 1 for i, g in zip(ids, grid)])

        @pl.when(last)
        def _():
            self.finish(ride)


def _exchange(bufs, scatter, name):
    rider = _Rider(bufs, scatter)

    def body(*refs):
        ride = (refs[:rider.n], refs[rider.n:2 * rider.n], refs[2 * rider.n:])
        rider.start(ride)
        rider.finish(ride)

    return pl.pallas_call(
        body,
        name=name,
        in_specs=rider.specs,
        out_specs=rider.specs,
        out_shape=rider.out_shape,
        scratch_shapes=rider.scratch,
    )(*rider.bufs)


MM_RESIDENT_B_BYTES = 14 * 1024 * 1024
MM_A_TILE_BYTES = 4 * 1024 * 1024
MM_OUT_TILE_BYTES = 6 * 1024 * 1024


def _mm_tiles(mode, M, N, K, a_bytes, out_bytes):
    if mode != "tn" and K * N * 2 <= MM_RESIDENT_B_BYTES:
        for tm in (1024, 512, 256, 128):
            if M % tm == 0 and tm * K * a_bytes <= MM_A_TILE_BYTES and tm * N * out_bytes <= MM_OUT_TILE_BYTES:
                return tm, N, K
    if mode == "tn":
        return (_pick(M, (1024, 1408, 512, 256, 128)), _pick(N, (1024, 1664, 1408, 512, 256, 128)),
                _pick(K, (2048, 1024, 512, 256, 128)))
    return _pick(M, (1024, 512, 256, 128)), _pick(N, (512, 256, 128)), _pick(K, (1024, 512, 256, 128))


def _mm(a, b, *, mode, out_dtype, name, res=None, res_scale=1.0, rider=None):
    if mode == "nn":
        (M, K), (K2, N) = a.shape, b.shape
    elif mode == "nt":
        (M, K), (N, K2) = a.shape, b.shape
    else:
        (K, M), (K2, N) = a.shape, b.shape
    assert K == K2, (a.shape, b.shape, mode)
    out_bytes = jnp.dtype(out_dtype).itemsize + (4 if res is not None else 0)
    tm, tn, tk = _mm_tiles(mode, M, N, K, a.dtype.itemsize, out_bytes)
    grid = (M // tm, N // tn, K // tk)
    nk = grid[2]
    dims = {"nn": _NN, "nt": _NT, "tn": _TN}[mode]
    n_in = 2 + (res is not None)
    n_ride = rider.n if rider is not None else 0

    def body(*refs):
        a_ref, b_ref = refs[:2]
        r_ref = refs[2] if res is not None else None
        o_ref = refs[n_in + n_ride]
        rest = refs[n_in + 2 * n_ride + 1:]
        acc_ref = rest[0] if nk > 1 else None
        ids = [pl.program_id(d) for d in range(3)]
        if rider is not None:
            ride = (refs[n_in:n_in + n_ride], refs[n_in + n_ride + 1:n_in + 2 * n_ride + 1], rest[-3:])
            rider.start_at_first(ids, ride)
        part = _dot(a_ref[...].astype(BF16), b_ref[...].astype(BF16), dims)

        def finish(total):
            if r_ref is not None:
                total = total + res_scale * r_ref[...]
            o_ref[...] = total.astype(out_dtype)

        if nk == 1:
            finish(part)
        else:
            k = ids[2]

            @pl.when(k == 0)
            def _():
                acc_ref[...] = part

            @pl.when(k > 0)
            def _():
                acc_ref[...] += part

            @pl.when(k == nk - 1)
            def _():
                finish(acc_ref[...])

        if rider is not None:
            rider.wait_at_last(ids, grid, ride)

    if mode == "nn":
        a_spec = pl.BlockSpec((tm, tk), lambda i, j, k: (i, k))
        b_spec = pl.BlockSpec((tk, tn), lambda i, j, k: (k, j))
    elif mode == "nt":
        a_spec = pl.BlockSpec((tm, tk), lambda i, j, k: (i, k))
        b_spec = pl.BlockSpec((tn, tk), lambda i, j, k: (j, k))
    else:
        a_spec = pl.BlockSpec((tk, tm), lambda i, j, k: (k, i))
        b_spec = pl.BlockSpec((tk, tn), lambda i, j, k: (k, j))
    o_spec = pl.BlockSpec((tm, tn), lambda i, j, k: (i, j))
    in_specs = [a_spec, b_spec] + ([o_spec] if res is not None else [])
    args = (a, b) + ((res,) if res is not None else ())
    out_specs, out_shape = [o_spec], [jax.ShapeDtypeStruct((M, N), out_dtype)]
    scratch = [pltpu.VMEM((tm, tn), F32)] if nk > 1 else []
    sem = ("parallel", "parallel", "arbitrary")
    if rider is not None:
        in_specs, args = in_specs + rider.specs, args + tuple(rider.bufs)
        out_specs, out_shape = out_specs + rider.specs, out_shape + rider.out_shape
        scratch = scratch + rider.scratch
        sem = ("arbitrary",) * 3
    outs = pl.pallas_call(
        body,
        name=name,
        grid=grid,
        in_specs=in_specs,
        out_specs=out_specs,
        out_shape=out_shape,
        scratch_shapes=scratch,
        compiler_params=_params(sem),
    )(*args)
    return outs[0] if rider is None else (outs[0], list(outs[1:]))


def _mm_host(a, b, *, rider, **kw):
    out = _mm(a, b, rider=rider, **kw)
    return out if rider is not None else (out, [])


def _as_host(rider, results):
    return results if rider is not None else tuple(results) + ([],)


MM_FUSED_MARGIN_BYTES = 10 * 1024 * 1024
MM_FUSED_MAX_ROWS = 512


def _col_sum_update(acc_ref, val, first):
    part = jnp.sum(val.reshape(val.shape[0] // 8, 8, val.shape[1]), axis=0)

    @pl.when(first)
    def _():
        acc_ref[...] = part

    @pl.when(jnp.logical_not(first))
    def _():
        acc_ref[...] += part


def _mm_fused(a, b, *, mode, name, extras, outs, epilogue, sums=(), rider=None, max_rows=MM_FUSED_MAX_ROWS):
    if mode == "nn":
        (M, K), (K2, N) = a.shape, b.shape
        b_dims = _NN
    else:
        (M, K), (N, K2) = a.shape, b.shape
        b_dims = _NT
    assert K == K2, (a.shape, b.shape, mode)
    rows = [e for e in extras if e.shape[0] == M]
    per_row = 2 * (K * a.dtype.itemsize + sum(e.shape[1] * e.dtype.itemsize for e in rows)
                   + sum(c * jnp.dtype(d).itemsize for c, d in outs)) + 2 * N * 4
    budget = VMEM_LIMIT_BYTES - K * N * 2 - MM_FUSED_MARGIN_BYTES
    tm = next(t for t in (512, 256, 128, 64, 32, 16) if t <= max_rows and M % t == 0 and t * per_row <= budget)
    steps = M // tm
    n_x, n_o, n_s = len(extras), len(outs), len(sums)
    n_ride = rider.n if rider is not None else 0

    def body(*refs):
        a_ref, b_ref = refs[:2]
        x_refs = refs[2:2 + n_x]
        base = 2 + n_x + n_ride
        o_refs, s_refs = refs[base:base + n_o], refs[base + n_o:base + n_o + n_s]
        acc_refs = refs[base + n_o + n_s + n_ride:base + n_o + 2 * n_s + n_ride]
        ids = [pl.program_id(0)]
        if rider is not None:
            ride = (refs[2 + n_x:base], refs[base + n_o + n_s:base + n_o + n_s + n_ride], refs[-3:])
            rider.start_at_first(ids, ride)
        prod = _dot(a_ref[...].astype(BF16), b_ref[...], b_dims)
        tiles = epilogue(prod, *[r[...] for r in x_refs])
        for o_ref, t in zip(o_refs, tiles[:n_o]):
            o_ref[...] = t.astype(o_ref.dtype)
        for acc_ref, t in zip(acc_refs, tiles[n_o:]):
            _col_sum_update(acc_ref, t, ids[0] == 0)
        if n_s:
            @pl.when(ids[0] == steps - 1)
            def _():
                for s_ref, acc_ref in zip(s_refs, acc_refs):
                    s_ref[...] = jnp.sum(acc_ref[...], axis=0, keepdims=True)
        if rider is not None:
            rider.wait_at_last(ids, (steps,), ride)

    in_specs = [pl.BlockSpec((tm, K), lambda i: (i, 0)),
                pl.BlockSpec(b.shape, lambda i: (0, 0), pipeline_mode=pl.Buffered(1))]
    for e in extras:
        in_specs.append(pl.BlockSpec((tm, e.shape[1]), lambda i: (i, 0)) if e.shape[0] == M
                        else pl.BlockSpec(e.shape, lambda i: (0, 0)))
    out_specs = ([pl.BlockSpec((tm, c), lambda i: (i, 0)) for c, _ in outs]
                 + [pl.BlockSpec((1, c), lambda i: (0, 0)) for c in sums])
    out_shape = ([jax.ShapeDtypeStruct((M, c), d) for c, d in outs]
                 + [jax.ShapeDtypeStruct((1, c), F32) for c in sums])
    args = (a, b) + tuple(extras)
    scratch = [pltpu.VMEM((8, c), F32) for c in sums]
    if rider is not None:
        in_specs, args = in_specs + rider.specs, args + tuple(rider.bufs)
        out_specs, out_shape = out_specs + rider.specs, out_shape + rider.out_shape
        scratch = scratch + rider.scratch
    res = pl.pallas_call(
        body,
        name=name,
        grid=(steps,),
        in_specs=in_specs,
        out_specs=out_specs,
        out_shape=out_shape,
        scratch_shapes=scratch,
        compiler_params=_params(("arbitrary",) if (n_s or rider is not None) else ("parallel",)),
    )(*args)
    return tuple(res[:n_o + n_s]) + ((list(res[n_o + n_s:]),) if rider is not None else ())


def _pair_rows(blk, lane_is_a):
    zero = jnp.zeros_like(blk)
    return jnp.concatenate([jnp.where(lane_is_a, blk, zero), jnp.where(lane_is_a, zero, blk)], axis=0)


SB_STRIP = 32
SB_FWD_PAIRS = 4
SB_BWD_PAIRS = 2


def _pair_lanes(p):
    return slice(p * LANES, (p + 1) * LANES)


def _sb_scan_matrices():
    o = lax.broadcasted_iota(jnp.int32, (2 * LANES, 4 * LANES), 0)
    c = lax.broadcasted_iota(jnp.int32, (2 * LANES, 4 * LANES), 1) & (2 * LANES - 1)
    same = (o >= LANES) == (c >= LANES)
    oo, cc = o & (LANES - 1), c & (LANES - 1)
    return (jnp.where(same & (cc > oo), 1.0, 0.0).astype(BF16), jnp.where(same & (cc < oo), 1.0, 0.0).astype(BF16))


def _sb_log_terms(z):
    log_rem = -jnp.maximum(z, 0.0) - jnp.log(1.0 + jnp.exp(-jnp.abs(z)))
    return log_rem, log_rem + z


def _sb_store_split(ref, strip, val):
    hi = val.astype(BF16)
    ref[pl.ds(strip * SB_STRIP, SB_STRIP), :] = hi
    ref[pl.ds(2 * LANES + strip * SB_STRIP, SB_STRIP), :] = (val - hi.astype(F32)).astype(BF16)


def _sb_fwd(h_a, rider=None):
    assert SB_FWD_PAIRS == 4
    T = h_a.shape[0]
    tq = _pick(T, (SB_Q_BLOCK, SB_KEY_BLOCK))
    nq, per_q, nkb = T // tq, tq // SB_KEY_BLOCK, T // SB_KEY_BLOCK
    n_strips = 2 * LANES // SB_STRIP
    n_ride = rider.n if rider is not None else 0
    after_m, _ = _sb_scan_matrices()
    pairs = SB_FWD_PAIRS

    def body(*refs):
        q_ref, k_ref, v_ref, after_ref = refs[:4]
        a_ref, r_ref, n_ref = refs[4 + n_ride:7 + n_ride]
        z_ref, lb_ref, split_ref, w_ref = refs[7 + 2 * n_ride:11 + 2 * n_ride]
        ids = [pl.program_id(0)]
        if rider is not None:
            ride = (refs[4:4 + n_ride], refs[7 + n_ride:7 + 2 * n_ride], refs[-3:])
            rider.start_at_first(ids, ride)
        i = ids[0]
        q_t = [(q_ref[:, _pair_lanes(p)].astype(F32).T * (SB_HEAD_DIM ** -0.5)).astype(BF16) for p in range(pairs)]
        lane_is_a = lax.broadcasted_iota(jnp.int32, (SB_KEY_BLOCK, LANES), 1) < SB_HEAD_DIM
        q_idx = i * tq + lax.broadcasted_iota(jnp.int32, (1, tq), 1)
        key_off = lax.broadcasted_iota(jnp.int32, (SB_STRIP, 1), 0)

        def causal(ks, s):
            return (ks + (s * SB_STRIP) % SB_KEY_BLOCK + key_off) < q_idx

        def tile(kb, masked, carry):
            acc_t, ra, rb = [list(c) for c in carry]
            ks = pl.multiple_of(kb * SB_KEY_BLOCK, SB_KEY_BLOCK)
            vv = []
            for p in range(pairs):
                kk = _pair_rows(k_ref[pl.ds(ks, SB_KEY_BLOCK), _pair_lanes(p)], lane_is_a)
                vv.append(_pair_rows(v_ref[pl.ds(ks, SB_KEY_BLOCK), _pair_lanes(p)], lane_is_a))
                z_ref[p] = _dot(kk, q_t[p], _NN)
            sums = [[jnp.zeros((8, tq), F32), jnp.zeros((8, tq), F32)] for _ in range(pairs)]
            for p in range(pairs):
                for s in range(n_strips):
                    rows = pl.ds(s * SB_STRIP, SB_STRIP)
                    log_rem, log_beta = _sb_log_terms(z_ref[p, rows, :])
                    lb_ref[p, rows, :] = log_beta
                    if masked:
                        log_rem = jnp.where(causal(ks, s), log_rem, 0.0)
                    _sb_store_split(split_ref.at[p], s, log_rem)
                    head = (s * SB_STRIP) // SB_KEY_BLOCK
                    sums[p][head] = sums[p][head] + jnp.sum(log_rem.reshape(SB_STRIP // 8, 8, tq), axis=0)
            for p in range(pairs):
                z_ref[p] = _dot(after_ref[...], split_ref[p], _NN)
            for p in range(pairs):
                for s in range(n_strips):
                    rows = pl.ds(s * SB_STRIP, SB_STRIP)
                    start = ra[p] if (s * SB_STRIP) < SB_KEY_BLOCK else rb[p]
                    w = jnp.exp(lb_ref[p, rows, :] + z_ref[p, rows, :] + start)
                    if masked:
                        w = jnp.where(causal(ks, s), w, 0.0)
                    w_ref[p, rows, :] = w.astype(BF16)
            for p in range(pairs):
                acc_t[p] = acc_t[p] + _dot(vv[p], w_ref[p], _TN)
                r_ref[2 * p, kb] = ra[p]
                r_ref[2 * p + 1, kb] = rb[p]
                ra[p] = ra[p] + jnp.sum(sums[p][0], axis=0, keepdims=True)
                rb[p] = rb[p] + jnp.sum(sums[p][1], axis=0, keepdims=True)
            return tuple(acc_t), tuple(ra), tuple(rb)

        carry = (tuple(jnp.zeros((LANES, tq), F32) for _ in range(pairs)),
                 tuple(jnp.zeros((1, tq), F32) for _ in range(pairs)),
                 tuple(jnp.zeros((1, tq), F32) for _ in range(pairs)))
        for d in range(per_q):
            carry = tile(i * per_q + (per_q - 1 - d), True, carry)
        n_full = i * per_q

        def alive(c):
            top = functools.reduce(jnp.maximum, c[2] + c[3])
            return jnp.logical_and(c[0] < n_full, jnp.max(top) > SB_DEAD_LOG)

        def step(c):
            return (c[0] + 1,) + tile(n_full - 1 - c[0], False, c[1:])

        walked, acc_t, _, _ = lax.while_loop(alive, step, (jnp.int32(0),) + carry)
        for p in range(pairs):
            a_ref[:, _pair_lanes(p)] = acc_t[p].T.astype(BF16)
        n_ref[...] = jnp.zeros(n_ref.shape, F32) + walked.astype(F32)
        if rider is not None:
            rider.wait_at_last(ids, (nq,), ride)

    wide = pairs * LANES
    in_specs = [pl.BlockSpec((tq, wide), lambda i: (i, 0)),
                pl.BlockSpec((T, wide), lambda i: (0, 1), pipeline_mode=pl.Buffered(1)),
                pl.BlockSpec((T, wide), lambda i: (0, 2), pipeline_mode=pl.Buffered(1)),
                pl.BlockSpec(after_m.shape, lambda i: (0, 0), pipeline_mode=pl.Buffered(1))]
    out_specs = [pl.BlockSpec((tq, wide), lambda i: (i, 0)),
                 pl.BlockSpec((2 * pairs, nkb, 1, tq), lambda i: (0, 0, 0, i)),
                 pl.BlockSpec((1, 8, LANES), lambda i: (i, 0, 0))]
    out_shape = [jax.ShapeDtypeStruct((T, SB_WIDTH), BF16), jax.ShapeDtypeStruct((2 * pairs, nkb, 1, T), F32),
                 jax.ShapeDtypeStruct((nq, 8, LANES), F32)]
    args = (h_a, h_a, h_a, after_m)
    scratch = [pltpu.VMEM((pairs, 2 * LANES, tq), F32), pltpu.VMEM((pairs, 2 * LANES, tq), F32),
               pltpu.VMEM((pairs, 4 * LANES, tq), BF16), pltpu.VMEM((pairs, 2 * LANES, tq), BF16)]
    if rider is not None:
        in_specs, args = in_specs + rider.specs, args + tuple(rider.bufs)
        out_specs, out_shape = out_specs + rider.specs, out_shape + rider.out_shape
        scratch = scratch + rider.scratch
    outs = pl.pallas_call(
        body,
        name="sb_fwd",
        grid=(nq,),
        in_specs=in_specs,
        out_specs=out_specs,
        out_shape=out_shape,
        scratch_shapes=scratch,
        compiler_params=_params(("arbitrary",)),
    )(*args)
    return outs[0], (outs[1], outs[2]), list(outs[3:])


def _sb_bwd(h_a, d_out, saved):
    r_mat, walked_blocks = saved
    T = h_a.shape[0]
    tq = _pick(T, (SB_Q_BLOCK, SB_KEY_BLOCK))
    nq, per_q, nkb = T // tq, tq // SB_KEY_BLOCK, T // SB_KEY_BLOCK
    n_strips = 2 * LANES // SB_STRIP
    after_m, before_m = _sb_scan_matrices()
    pairs = SB_BWD_PAIRS
    groups = 4 // pairs

    def body(q_ref, k_ref, v_ref, do_ref, r_ref, n_ref, after_ref, before_ref, dq_ref, dk_ref, dv_ref,
             z_ref, lb_ref, split_ref, w_ref, da_ref, dz_ref):
        i = pl.program_id(1)

        @pl.when(i == 0)
        def _():
            dk_ref[...] = jnp.zeros_like(dk_ref)
            dv_ref[...] = jnp.zeros_like(dv_ref)

        scale = SB_HEAD_DIM ** -0.5
        q = [q_ref[:, _pair_lanes(p)] for p in range(pairs)]
        d_o = [do_ref[:, _pair_lanes(p)] for p in range(pairs)]
        q_t = [(x.astype(F32).T * scale).astype(BF16) for x in q]
        do_t = [x.astype(F32).T.astype(BF16) for x in d_o]
        lane_is_a = lax.broadcasted_iota(jnp.int32, (SB_KEY_BLOCK, LANES), 1) < SB_HEAD_DIM
        q_idx = i * tq + lax.broadcasted_iota(jnp.int32, (1, tq), 1)
        key_off = lax.broadcasted_iota(jnp.int32, (SB_STRIP, 1), 0)

        def causal(ks, s):
            return (ks + (s * SB_STRIP) % SB_KEY_BLOCK + key_off) < q_idx

        def tile(kb, masked, carry):
            dq_t, ca, cb = [list(c) for c in carry]
            ks = pl.multiple_of(kb * SB_KEY_BLOCK, SB_KEY_BLOCK)
            kk, vv = [], []
            for p in range(pairs):
                kk.append(_pair_rows(k_ref[pl.ds(ks, SB_KEY_BLOCK), _pair_lanes(p)], lane_is_a))
                vv.append(_pair_rows(v_ref[pl.ds(ks, SB_KEY_BLOCK), _pair_lanes(p)], lane_is_a))
                z_ref[p] = _dot(kk[p], q_t[p], _NN)
            for p in range(pairs):
                for s in range(n_strips):
                    rows = pl.ds(s * SB_STRIP, SB_STRIP)
                    log_rem, log_beta = _sb_log_terms(z_ref[p, rows, :])
                    lb_ref[p, rows, :] = log_beta
                    if masked:
                        log_rem = jnp.where(causal(ks, s), log_rem, 0.0)
                    _sb_store_split(split_ref.at[p], s, log_rem)
            for p in range(pairs):
                z_ref[p] = _dot(after_ref[...], split_ref[p], _NN)
                da_ref[p] = _dot(vv[p], do_t[p], _NN)
            sums = [[jnp.zeros((8, tq), F32), jnp.zeros((8, tq), F32)] for _ in range(pairs)]
            for p in range(pairs):
                for s in range(n_strips):
                    rows = pl.ds(s * SB_STRIP, SB_STRIP)
                    start = r_ref[2 * p + (s * SB_STRIP) // SB_KEY_BLOCK, kb]
                    w = jnp.exp(lb_ref[p, rows, :] + z_ref[p, rows, :] + start)
                    if masked:
                        w = jnp.where(causal(ks, s), w, 0.0)
                    w_ref[p, rows, :] = w.astype(BF16)
                    da = da_ref[p, rows, :] * w
                    da_ref[p, rows, :] = da
                    _sb_store_split(split_ref.at[p], s, da)
                    head = (s * SB_STRIP) // SB_KEY_BLOCK
                    sums[p][head] = sums[p][head] + jnp.sum(da.reshape(SB_STRIP // 8, 8, tq), axis=0)
            for p in range(pairs):
                z_ref[p] = _dot(before_ref[...], split_ref[p], _NN)
            for p in range(pairs):
                for s in range(n_strips):
                    rows = pl.ds(s * SB_STRIP, SB_STRIP)
                    base = ca[p] if (s * SB_STRIP) < SB_KEY_BLOCK else cb[p]
                    sig = jnp.exp(lb_ref[p, rows, :])
                    dz = da_ref[p, rows, :] * (1.0 - sig) - (z_ref[p, rows, :] + base) * sig
                    if masked:
                        dz = jnp.where(causal(ks, s), dz, 0.0)
                    dz_ref[p, rows, :] = (dz * scale).astype(BF16)
            for p in range(pairs):
                dq_t[p] = dq_t[p] + _dot(kk[p], dz_ref[p], _TN)
                dkk = _dot(dz_ref[p], q[p], _NN)
                dvv = _dot(w_ref[p], d_o[p], _NN)
                here = (pl.ds(ks, SB_KEY_BLOCK), _pair_lanes(p))
                dk_ref[here] += jnp.where(lane_is_a, dkk[:SB_KEY_BLOCK], dkk[SB_KEY_BLOCK:])
                dv_ref[here] += jnp.where(lane_is_a, dvv[:SB_KEY_BLOCK], dvv[SB_KEY_BLOCK:])
                ca[p] = ca[p] + jnp.sum(sums[p][0], axis=0, keepdims=True)
                cb[p] = cb[p] + jnp.sum(sums[p][1], axis=0, keepdims=True)
            return tuple(dq_t), tuple(ca), tuple(cb)

        n_full = i * per_q
        walked = jnp.clip(jnp.max(n_ref[...]).astype(jnp.int32), 0, n_full)
        carry = (tuple(jnp.zeros((LANES, tq), F32) for _ in range(pairs)),
                 tuple(jnp.zeros((1, tq), F32) for _ in range(pairs)),
                 tuple(jnp.zeros((1, tq), F32) for _ in range(pairs)))
        carry = lax.fori_loop(n_full - walked, n_full, lambda j, c: tile(j, False, c), carry)
        for d in range(per_q):
            carry = tile(i * per_q + d, True, carry)
        for p in range(pairs):
            dq_ref[:, _pair_lanes(p)] = carry[0][p].T.astype(BF16)

    wide = pairs * LANES
    mat = pl.BlockSpec(after_m.shape, lambda g, i: (0, 0), pipeline_mode=pl.Buffered(1))
    return pl.pallas_call(
        body,
        name="sb_bwd",
        grid=(groups, nq),
        in_specs=[pl.BlockSpec((tq, wide), lambda g, i: (i, g)),
                  pl.BlockSpec((T, wide), lambda g, i: (0, groups + g), pipeline_mode=pl.Buffered(1)),
                  pl.BlockSpec((T, wide), lambda g, i: (0, 2 * groups + g), pipeline_mode=pl.Buffered(1)),
                  pl.BlockSpec((tq, wide), lambda g, i: (i, g)),
                  pl.BlockSpec((2 * pairs, nkb, 1, tq), lambda g, i: (g, 0, 0, i)),
                  pl.BlockSpec((1, 8, LANES), lambda g, i: (i, 0, 0)),
                  mat, mat],
        out_specs=[pl.BlockSpec((tq, wide), lambda g, i: (i, g)),
                   pl.BlockSpec((T, wide), lambda g, i: (0, g)),
                   pl.BlockSpec((T, wide), lambda g, i: (0, g))],
        out_shape=[jax.ShapeDtypeStruct((T, SB_WIDTH), BF16),
                   jax.ShapeDtypeStruct((T, SB_WIDTH), F32),
                   jax.ShapeDtypeStruct((T, SB_WIDTH), F32)],
        scratch_shapes=[pltpu.VMEM((pairs, 2 * LANES, tq), F32), pltpu.VMEM((pairs, 2 * LANES, tq), F32),
                        pltpu.VMEM((pairs, 4 * LANES, tq), BF16), pltpu.VMEM((pairs, 2 * LANES, tq), BF16),
                        pltpu.VMEM((pairs, 2 * LANES, tq), F32), pltpu.VMEM((pairs, 2 * LANES, tq), BF16)],
        compiler_params=_params(("parallel", "arbitrary")),
    )(h_a, h_a, h_a, d_out, r_mat, walked_blocks, after_m, before_m)


def _ret_tables(T):
    half = RET_QK_DIM // 2
    inv = 1.0 / (ROPE_BASE ** (jnp.arange(half, dtype=F32) / half))
    ang = jnp.arange(T, dtype=F32)[:, None] * inv[None, :]
    cos, sin = jnp.cos(ang), jnp.sin(ang)
    cos_t = jnp.concatenate([cos, cos], axis=1)
    sin_t = jnp.concatenate([-sin, sin], axis=1)
    log_gamma = jnp.log1p(-jnp.exp2(-5.0 - jnp.arange(RET_HEADS, dtype=F32)))
    idx = jnp.arange(RET_CHUNK, dtype=F32)
    rel = idx[:, None] - idx[None, :]
    decay = jnp.where(rel[None] >= 0, jnp.exp(log_gamma[:, None, None] * jnp.maximum(rel, 0.0)[None]), 0.0)
    k_decay = jnp.exp(log_gamma[None, :] * (RET_CHUNK - 1.0 - idx)[:, None])
    q_decay = jnp.exp(log_gamma[None, :] * (idx + 1.0)[:, None])
    chunk_decay = jnp.exp(log_gamma * RET_CHUNK)
    k_dec = jnp.broadcast_to(k_decay.T[:, :, None], (RET_HEADS, RET_CHUNK, LANES))
    q_dec = jnp.broadcast_to(q_decay.T[:, :, None], (RET_HEADS, RET_CHUNK, LANES))
    c_dec = jnp.broadcast_to(chunk_decay[:, None, None], (RET_HEADS, 8, LANES))
    return cos_t, sin_t, decay, k_dec, q_dec, c_dec


def _rotary(x, cos_t, sin_t):
    return x * cos_t + pltpu.roll(x, RET_QK_DIM // 2, 1) * sin_t


def _rotary_transpose(dy, cos_t, sin_t):
    return dy * cos_t + pltpu.roll(dy * sin_t, RET_QK_DIM // 2, 1)


def _head_norm(o):
    mu = jnp.mean(o, axis=1, keepdims=True)
    cen = o - mu
    var = jnp.mean(cen * cen, axis=1, keepdims=True)
    rstd = lax.rsqrt(var + LN_EPS)
    return cen * rstd, rstd


def _ret_specs(nc, reverse):
    def n_of(n):
        return (nc - 1 - n) if reverse else n

    q_spec = pl.BlockSpec((RET_CHUNK, RET_QK_WIDTH), lambda n: (n_of(n), 0))
    k_spec = pl.BlockSpec((RET_CHUNK, RET_QK_WIDTH), lambda n: (n_of(n), 1))
    vv = pl.BlockSpec((RET_CHUNK, RET_V_WIDTH), lambda n: (n_of(n), 0))
    pos = pl.BlockSpec((RET_CHUNK, LANES), lambda n: (n_of(n), 0))
    per_head = pl.BlockSpec((RET_HEADS, RET_CHUNK, LANES), lambda n: (0, 0, 0))
    c_dec = pl.BlockSpec((RET_HEADS, 8, LANES), lambda n: (0, 0, 0))
    state = pl.BlockSpec((RET_HEADS, 1, RET_QK_DIM, RET_V_DIM), lambda n: (0, n_of(n), 0, 0))
    return q_spec, k_spec, vv, pos, per_head, c_dec, state


def _qk_cols(h):
    return slice(h * RET_QK_DIM, (h + 1) * RET_QK_DIM)


def _v_cols(h):
    return slice(h * RET_V_DIM, (h + 1) * RET_V_DIM)


def _ret_fwd(h_b, h_c, h_d, tables):
    T = h_b.shape[0]
    nc = T // RET_CHUNK
    q_spec, k_spec, vv, pos, per_head, c_dec, state = _ret_specs(nc, False)

    def body(q_ref, k_ref, v_ref, g_ref, cos_ref, sin_ref, dec_ref, kd_ref, qd_ref, cd_ref,
             y_ref, o_ref, st_ref, state_ref):
        @pl.when(pl.program_id(0) == 0)
        def _():
            state_ref[...] = jnp.zeros_like(state_ref)

        cos_t, sin_t = cos_ref[...], sin_ref[...]
        for h in range(RET_HEADS):
            q = _rotary(q_ref[:, _qk_cols(h)], cos_t, sin_t) * (RET_QK_DIM ** -0.5)
            k = _rotary(k_ref[:, _qk_cols(h)], cos_t, sin_t)
            v = v_ref[:, _v_cols(h)]
            prev = state_ref[h]
            scores = _dot(q.astype(BF16), k.astype(BF16), _NT) * dec_ref[h]
            inner = _dot(scores.astype(BF16), v, _NN)
            cross = _dot((q * qd_ref[h]).astype(BF16), prev.astype(BF16), _NN)
            o = inner + cross
            st_ref[h, 0] = prev
            kv = _dot((k * kd_ref[h]).astype(BF16), v, _TN)
            state_ref[h] = prev * cd_ref[h, 0:1, 0:1] + kv
            o_ref[:, _v_cols(h)] = o
            normed, _ = _head_norm(o)
            gate = g_ref[:, _v_cols(h)]
            y_ref[:, _v_cols(h)] = (gate * jax.nn.sigmoid(gate) * normed).astype(BF16)

    return pl.pallas_call(
        body,
        name="ret_fwd",
        grid=(nc,),
        in_specs=[q_spec, k_spec, vv, vv, pos, pos, per_head, per_head, per_head, c_dec],
        out_specs=[vv, vv, state],
        out_shape=[jax.ShapeDtypeStruct((T, RET_V_WIDTH), BF16),
                   jax.ShapeDtypeStruct((T, RET_V_WIDTH), F32),
                   jax.ShapeDtypeStruct((RET_HEADS, nc, RET_QK_DIM, RET_V_DIM), F32)],
        scratch_shapes=[pltpu.VMEM((RET_HEADS, RET_QK_DIM, RET_V_DIM), F32)],
        compiler_params=_params(("arbitrary",)),
    )(h_b, h_b, h_c, h_d, *tables)


def _ret_bwd(d_y, o_pre, states, h_b, h_c, h_d, tables, rider=None):
    T = h_b.shape[0]
    nc = T // RET_CHUNK
    q_spec, k_spec, vv, pos, per_head, c_dec, state = _ret_specs(nc, True)
    n_ride = rider.n if rider is not None else 0

    def body(*refs):
        (dy_ref, o_ref, st_ref, q_ref, k_ref, v_ref, g_ref, cos_ref, sin_ref, dec_ref, kd_ref, qd_ref,
         cd_ref) = refs[:13]
        dq_ref, dk_ref, dv_ref, dg_ref = refs[13 + n_ride:17 + n_ride]
        carry_ref = refs[17 + 2 * n_ride]
        ids = [pl.program_id(0)]
        if rider is not None:
            ride = (refs[13:13 + n_ride], refs[17 + n_ride:17 + 2 * n_ride], refs[-3:])
            rider.start_at_first(ids, ride)

        @pl.when(ids[0] == 0)
        def _():
            carry_ref[...] = jnp.zeros_like(carry_ref)

        cos_t, sin_t = cos_ref[...], sin_ref[...]
        scale = RET_QK_DIM ** -0.5
        for h in range(RET_HEADS):
            q = _rotary(q_ref[:, _qk_cols(h)], cos_t, sin_t) * scale
            k = _rotary(k_ref[:, _qk_cols(h)], cos_t, sin_t)
            v = v_ref[:, _v_cols(h)]
            decay, k_dec, q_dec = dec_ref[h], kd_ref[h], qd_ref[h]
            chunk_decay = cd_ref[h, 0:1, 0:1]
            state = st_ref[h, 0].astype(BF16)
            later = carry_ref[h]
            later_b = later.astype(BF16)

            gate = g_ref[:, _v_cols(h)]
            sig = jax.nn.sigmoid(gate)
            silu = gate * sig
            normed, rstd = _head_norm(o_ref[:, _v_cols(h)])
            d_y = dy_ref[:, _v_cols(h)]
            dg_ref[:, _v_cols(h)] = (d_y * normed * (sig * (1.0 + gate * (1.0 - sig)))).astype(BF16)
            d_n = d_y * silu
            d_o = rstd * (d_n - jnp.mean(d_n, axis=1, keepdims=True)
                          - normed * jnp.mean(d_n * normed, axis=1, keepdims=True))
            d_ob = d_o.astype(BF16)

            qb, kb = q.astype(BF16), k.astype(BF16)
            qd_b, kd_b = (q * q_dec).astype(BF16), (k * k_dec).astype(BF16)
            scores = _dot(qb, kb, _NT) * decay
            d_scores = (_dot(d_ob, v, _NT) * decay).astype(BF16)
            dq = _dot(d_scores, kb, _NN) + _dot(d_ob, state, _NT) * q_dec
            dk = _dot(d_scores, qb, _TN) + _dot(v, later_b, _NT) * k_dec
            dv = _dot(scores.astype(BF16), d_ob, _TN) + _dot(kd_b, later_b, _NN)
            carry_ref[h] = _dot(qd_b, d_ob, _TN) + chunk_decay * later
            dq_ref[:, _qk_cols(h)] = _rotary_transpose(dq * scale, cos_t, sin_t).astype(BF16)
            dk_ref[:, _qk_cols(h)] = _rotary_transpose(dk, cos_t, sin_t).astype(BF16)
            dv_ref[:, _v_cols(h)] = dv.astype(BF16)
        if rider is not None:
            rider.wait_at_last(ids, (nc,), ride)

    qk_out = pl.BlockSpec((RET_CHUNK, RET_QK_WIDTH), lambda n: (nc - 1 - n, 0))
    in_specs = [vv, vv, state, q_spec, k_spec, vv, vv, pos, pos, per_head, per_head, per_head, c_dec]
    out_specs = [qk_out, qk_out, vv, vv]
    out_shape = [jax.ShapeDtypeStruct((T, RET_QK_WIDTH), BF16), jax.ShapeDtypeStruct((T, RET_QK_WIDTH), BF16),
                 jax.ShapeDtypeStruct((T, RET_V_WIDTH), BF16), jax.ShapeDtypeStruct((T, RET_V_WIDTH), BF16)]
    args = (d_y, o_pre, states, h_b, h_b, h_c, h_d) + tuple(tables)
    scratch = [pltpu.VMEM((RET_HEADS, RET_QK_DIM, RET_V_DIM), F32)]
    if rider is not None:
        in_specs, args = in_specs + rider.specs, args + tuple(rider.bufs)
        out_specs, out_shape = out_specs + rider.specs, out_shape + rider.out_shape
        scratch = scratch + rider.scratch
    outs = pl.pallas_call(
        body,
        name="ret_bwd",
        grid=(nc,),
        in_specs=in_specs,
        out_specs=out_specs,
        out_shape=out_shape,
        scratch_shapes=scratch,
        compiler_params=_params(("arbitrary",)),
    )(*args)
    return outs[0], outs[1], outs[2], outs[3], list(outs[4:])


def _gate_mix_tiles(y_ret, h_e, b_gate, y_sb):
    gates = jax.nn.sigmoid(h_e + b_gate)
    return y_ret, gates[:, :D_MODEL] * y_sb + gates[:, D_MODEL:] * y_ret


def _gate_mix_grad_tiles(d_mix, h_e, b_gate, y_sb, y_ret):
    gates = jax.nn.sigmoid(h_e + b_gate)
    g0, g1 = gates[:, :D_MODEL], gates[:, D_MODEL:]
    d_e = jnp.concatenate([d_mix * y_sb * g0 * (1.0 - g0), d_mix * y_ret * g1 * (1.0 - g1)], axis=1)
    return d_mix * g0, d_mix * g1, d_e, d_e


def _ln_stats(u):
    mu = jnp.mean(u, axis=1, keepdims=True)
    cen = u - mu
    var = jnp.mean(cen * cen, axis=1, keepdims=True)
    rstd = lax.rsqrt(var + LN_EPS)
    return cen * rstd, rstd


def _ln_input_grad(d_out, gain, xhat, rstd):
    d_hat = d_out * gain
    return rstd * (d_hat - jnp.mean(d_hat, axis=1, keepdims=True)
                   - xhat * jnp.mean(d_hat * xhat, axis=1, keepdims=True))


def _ln_tiles(sub, x_prev, gain, bias):
    xhat, rstd = _ln_stats(DN_ALPHA * x_prev + sub)
    out = xhat * gain + bias
    return out, out, xhat, rstd


def _ln_grad_tiles(d_sub, res, xhat, rstd, gain):
    d_out = d_sub + DN_ALPHA * res
    du = _ln_input_grad(d_out, gain, xhat, rstd)
    return du, du, d_out * xhat, d_out


def _ln_loss_tiles(sub, x_prev, gain, bias, target):
    xhat, rstd = _ln_stats(DN_ALPHA * x_prev + sub)
    diff = xhat * gain + bias - target
    d_out = diff * (1.0 / D_MODEL)
    du = _ln_input_grad(d_out, gain, xhat, rstd)
    return du, du, diff * diff, d_out * xhat, d_out


def _mem_probs(q_h, k_h):
    s = _dot(q_h, k_h, _NT) * (MEM_HEAD_DIM ** -0.5)
    e = jnp.exp(s - jnp.max(s, axis=1, keepdims=True))
    return e / jnp.sum(e, axis=1, keepdims=True)


def _xattn_fwd(q, kv):
    T, mem_len = q.shape[0], kv.shape[0]
    tq = _pick(T, (512, 256, 128))

    def body(q_ref, kv_ref, o_ref):
        for h in range(MEM_HEADS):
            cols = slice(h * MEM_HEAD_DIM, (h + 1) * MEM_HEAD_DIM)
            vcols = slice(D_MODEL + h * MEM_HEAD_DIM, D_MODEL + (h + 1) * MEM_HEAD_DIM)
            p = _mem_probs(q_ref[:, cols], kv_ref[:, cols])
            o_ref[:, cols] = _dot(p.astype(BF16), kv_ref[:, vcols], _NN).astype(BF16)

    return pl.pallas_call(
        body,
        name="xattn_fwd",
        grid=(T // tq,),
        in_specs=[pl.BlockSpec((tq, D_MODEL), lambda i: (i, 0)),
                  pl.BlockSpec((mem_len, 2 * D_MODEL), lambda i: (0, 0))],
        out_specs=pl.BlockSpec((tq, D_MODEL), lambda i: (i, 0)),
        out_shape=jax.ShapeDtypeStruct((T, D_MODEL), BF16),
        compiler_params=_params(("parallel",)),
    )(q, kv)


def _xattn_bwd(q, kv, d_o):
    T, mem_len = q.shape[0], kv.shape[0]
    tq = _pick(T, (512, 256, 128))

    def body(q_ref, kv_ref, do_ref, dq_ref, dkv_ref):
        @pl.when(pl.program_id(0) == 0)
        def _():
            dkv_ref[...] = jnp.zeros_like(dkv_ref)

        for h in range(MEM_HEADS):
            cols = slice(h * MEM_HEAD_DIM, (h + 1) * MEM_HEAD_DIM)
            vcols = slice(D_MODEL + h * MEM_HEAD_DIM, D_MODEL + (h + 1) * MEM_HEAD_DIM)
            q_h, k_h, do_h = q_ref[:, cols], kv_ref[:, cols], do_ref[:, cols]
            p = _mem_probs(q_h, k_h)
            dp = _dot(do_h, kv_ref[:, vcols], _NT)
            ds = p * (dp - jnp.sum(dp * p, axis=1, keepdims=True))
            dsb = (ds * (MEM_HEAD_DIM ** -0.5)).astype(BF16)
            dq_ref[:, cols] = _dot(dsb, k_h, _NN).astype(BF16)
            dkv_ref[:, cols] += _dot(dsb, q_h, _TN)
            dkv_ref[:, vcols] += _dot(p.astype(BF16), do_h, _TN)

    row = pl.BlockSpec((tq, D_MODEL), lambda i: (i, 0))
    full = pl.BlockSpec((mem_len, 2 * D_MODEL), lambda i: (0, 0))
    return pl.pallas_call(
        body,
        name="xattn_bwd",
        grid=(T // tq,),
        in_specs=[row, full, row],
        out_specs=[row, full],
        out_shape=[jax.ShapeDtypeStruct((T, D_MODEL), BF16), jax.ShapeDtypeStruct((mem_len, 2 * D_MODEL), F32)],
        compiler_params=_params(("arbitrary",)),
    )(q, kv, d_o)


def _swiglu_tiles(f):
    a, b = f[:, :FFN_HIDDEN], f[:, FFN_HIDDEN:]
    return f, a * jax.nn.sigmoid(a) * b


def _swiglu_grad_tiles(d_hidden, f):
    a, b = f[:, :FFN_HIDDEN], f[:, FFN_HIDDEN:]
    sig = jax.nn.sigmoid(a)
    return (jnp.concatenate([d_hidden * b * (sig * (1.0 + a * (1.0 - sig))), d_hidden * (a * sig)], axis=1),)


def _local_step(x, mem, w_in, small, target, fetch_rest, ship):
    T = x.shape[0]
    tables = _ret_tables(T)
    xb, memb = x.astype(BF16), mem.astype(BF16)

    h_a = _mm(xb, w_in[:, 0:1536], mode="nn", out_dtype=BF16, name="proj_sb")
    h_b = _mm(xb, w_in[:, 1536:2560], mode="nn", out_dtype=F32, name="proj_ret_qk")
    h_c = _mm(xb, w_in[:, 2560:3584], mode="nn", out_dtype=BF16, name="proj_ret_v")
    h_d = _mm(xb, w_in[:, 3584:4608], mode="nn", out_dtype=F32, name="proj_ret_g")
    h_e = _mm(xb, w_in[:, 4608:6656], mode="nn", out_dtype=F32, name="proj_gate")
    (a_sb, r_mat, _), w = fetch_rest(lambda rider: _sb_fwd(h_a, rider))
    y_gated, o_pre, states = _ret_fwd(h_b, h_c, h_d, tables)
    y_sb = _mm(a_sb, w["w_sb_o"], mode="nn", out_dtype=F32, name="sb_out")
    row_f32, row_bf16 = (D_MODEL, F32), (D_MODEL, BF16)
    ln_outs = [row_f32, row_bf16, row_f32, (1, F32)]
    y_ret, mix_in = _mm_fused(y_gated, w["w_ret_o"], mode="nn", name="ret_out", extras=[h_e, small["b_gate"], y_sb],
                              outs=[row_f32, row_bf16], epilogue=_gate_mix_tiles)
    x1, x1b, xhat1, rstd1 = _mm_fused(mix_in, w["w_mix_o"], mode="nn", name="mix_out",
                                      extras=[x, small["ln1_g"], small["ln1_b"]], outs=ln_outs, epilogue=_ln_tiles)
    q_m = _mm(x1b, w["w_mem_q"], mode="nn", out_dtype=BF16, name="mem_q")
    kv_m = _mm(memb, w["w_mem_kv"], mode="nn", out_dtype=BF16, name="mem_kv")
    o_m = _xattn_fwd(q_m, kv_m)
    x2, x2b, xhat2, rstd2 = _mm_fused(o_m, w["w_mem_o"], mode="nn", name="mem_out",
                                      extras=[x1, small["ln2_g"], small["ln2_b"]], outs=ln_outs, epilogue=_ln_tiles)
    f, hidden = _mm_fused(x2b, w["w_ffn_in"], mode="nn", name="ffn_in", extras=[],
                          outs=[(2 * FFN_HIDDEN, F32), (FFN_HIDDEN, BF16)], epilogue=_swiglu_tiles)
    du_outs, col = [row_f32, row_bf16], D_MODEL
    du3, du3b, loss_cols, d_ln3_g, d_ln3_b = _mm_fused(
        hidden, w["w_ffn_out"], mode="nn", name="ffn_out", extras=[x2, small["ln3_g"], small["ln3_b"], target],
        outs=du_outs, sums=[col, col, col], epilogue=_ln_loss_tiles, max_rows=256)
    loss = jnp.sum(loss_cols) * (0.5 / D_MODEL)

    g_ffn_out = _mm(hidden, du3b, mode="tn", out_dtype=BF16, name="g_ffn_out")
    (d_f,) = _mm_fused(du3b, w["w_ffn_out"], mode="nt", name="d_hidden", extras=[f],
                       outs=[(2 * FFN_HIDDEN, BF16)], epilogue=_swiglu_grad_tiles)
    g_ffn_in = _mm(x2b, d_f, mode="tn", out_dtype=BF16, name="g_ffn_in")
    du2, du2b, d_ln2_g, d_ln2_b = ship(
        {"w_ffn_out": g_ffn_out},
        lambda rider: _as_host(rider, _mm_fused(
            d_f, w["w_ffn_in"], mode="nt", name="d_x2", extras=[du3, xhat2, rstd2, small["ln2_g"]], outs=du_outs,
            sums=[col, col], epilogue=_ln_grad_tiles, rider=rider, max_rows=256)))
    g_mem_o = _mm(o_m, du2b, mode="tn", out_dtype=BF16, name="g_mem_o")
    d_om = _mm(du2b, w["w_mem_o"], mode="nt", out_dtype=BF16, name="d_om")
    d_qm, d_kvm = _xattn_bwd(q_m, kv_m, d_om)
    g_mem_q = _mm(x1b, d_qm, mode="tn", out_dtype=BF16, name="g_mem_q")
    g_mem_kv = _mm(memb, d_kvm.astype(BF16), mode="tn", out_dtype=BF16, name="g_mem_kv")
    du1, du1b, d_ln1_g, d_ln1_b = _mm_fused(
        d_qm, w["w_mem_q"], mode="nt", name="d_x1", extras=[du2, xhat1, rstd1, small["ln1_g"]], outs=du_outs,
        sums=[col, col], epilogue=_ln_grad_tiles, max_rows=256)
    g_mix_o = _mm(mix_in, du1b, mode="tn", out_dtype=BF16, name="g_mix_o")
    d_ysb, d_yret, d_e, d_b_gate = _mm_fused(
        du1b, w["w_mix_o"], mode="nt", name="d_mix_in", extras=[h_e, small["b_gate"], y_sb, y_ret],
        outs=[row_bf16, row_bf16, (2 * D_MODEL, BF16)], sums=[2 * D_MODEL], epilogue=_gate_mix_grad_tiles,
        max_rows=256)
    g_sb_o = _mm(a_sb, d_ysb, mode="tn", out_dtype=BF16, name="g_sb_o")
    g_ret_o = _mm(y_gated, d_yret, mode="tn", out_dtype=BF16, name="g_ret_o")
    d_asb = _mm(d_ysb, w["w_sb_o"], mode="nt", out_dtype=BF16, name="d_asb")
    d_ygated = _mm(d_yret, w["w_ret_o"], mode="nt", out_dtype=F32, name="d_ygated")
    small_grads = {"b_gate": d_b_gate, "ln1_g": d_ln1_g, "ln1_b": d_ln1_b, "ln2_g": d_ln2_g, "ln2_b": d_ln2_b,
                   "ln3_g": d_ln3_g, "ln3_b": d_ln3_b}
    d_rq, d_rk, d_c, d_d = ship({"w_ffn_in": g_ffn_in},
                                lambda rider: _ret_bwd(d_ygated, o_pre, states, h_b, h_c, h_d, tables, rider))
    d_q, d_k, d_v = _sb_bwd(h_a, d_asb, r_mat)
    d_h = jnp.concatenate([d_q, d_k.astype(BF16), d_v.astype(BF16), d_rq, d_rk, d_c, d_d, d_e], axis=1)
    late = {"w_mem_kv": g_mem_kv, "w_mem_q": g_mem_q, "w_mem_o": g_mem_o, "w_mix_o": g_mix_o, "w_ret_o": g_ret_o,
            "w_sb_o": g_sb_o, "small": small_grads}
    (g_in,) = ship(late, lambda rider: _mm_host(xb, d_h, mode="tn", out_dtype=BF16, name="g_in", rider=rider))
    (d_x,) = ship({"w_in": g_in},
                  lambda rider: _mm_host(d_h, w_in, mode="nt", out_dtype=F32, name="d_x", res=du1, res_scale=DN_ALPHA,
                                         rider=rider))
    return loss, d_x


def _adamw_math(w, g, m, v):
    m = ADAM_B1 * m + (1.0 - ADAM_B1) * g
    v = ADAM_B2 * v + (1.0 - ADAM_B2) * jnp.square(g)
    m_hat = m / (1.0 - ADAM_B1 ** ADAM_STEP)
    v_hat = v / (1.0 - ADAM_B2 ** ADAM_STEP)
    delta = -ADAM_LR * (m_hat / (jnp.sqrt(v_hat) + ADAM_EPS) + ADAM_WD * w)
    return delta, m, v


def _adamw(parts, w, m, v, name):
    R, C = w.shape
    tr = max(t for t in range(16, min(R, 256) + 1, 16) if R % t == 0) if R >= 16 else R

    def body(p_ref, w_ref, m_ref, v_ref, g_ref, d_ref, nm_ref, nv_ref):
        g = p_ref[0].astype(F32)
        for j in range(1, N_DEV):
            g = g + p_ref[j].astype(F32)
        delta, nm, nv = _adamw_math(w_ref[...], g, m_ref[...], v_ref[...])
        g_ref[...] = g
        d_ref[...] = delta
        nm_ref[...] = nm
        nv_ref[...] = nv

    blk = pl.BlockSpec((tr, C), lambda i: (i, 0))
    out = jax.ShapeDtypeStruct((R, C), F32)
    return pl.pallas_call(
        body,
        name=name,
        grid=(R // tr,),
        in_specs=[pl.BlockSpec((N_DEV, tr, C), lambda i: (0, i, 0)), blk, blk, blk],
        out_specs=[blk] * 4,
        out_shape=[out] * 4,
        compiler_params=_params(("parallel",)),
    )(parts, w, m, v)


_SHARD_AXIS = {"w_in": 1, "w_sb_o": 1, "w_ret_o": 0, "w_mix_o": 0, "w_mem_q": 0, "w_mem_kv": 1, "w_mem_o": 0,
               "w_ffn_in": 1, "w_ffn_out": 0}
_MATRICES = tuple(_SHARD_AXIS)
_SMALL = ("b_gate", "ln1_g", "ln1_b", "ln2_g", "ln2_b", "ln3_g", "ln3_b")
_WEIGHT_ORDER = ("w_in", "b_gate", "w_sb_o", "w_ret_o", "w_mix_o", "ln1_g", "ln1_b", "w_mem_q", "w_mem_kv", "w_mem_o",
                 "ln2_g", "ln2_b", "w_ffn_in", "w_ffn_out", "ln3_g", "ln3_b")


def _assemble(name, gathered):
    if _SHARD_AXIS[name] == 0:
        return gathered.reshape(-1, gathered.shape[2])
    return jnp.transpose(gathered, (1, 0, 2)).reshape(gathered.shape[1], -1)


def _to_slots(name, full):
    if _SHARD_AXIS[name] == 0:
        return full.reshape(N_DEV, full.shape[0] // N_DEV, full.shape[1])
    return jnp.transpose(full.reshape(full.shape[0], N_DEV, full.shape[1] // N_DEV), (1, 0, 2))


def _pack_small(vals):
    return jnp.concatenate([vals["b_gate"].reshape(2, D_MODEL)] + [vals[n] for n in _SMALL[1:]], axis=0)


def _unpack_small(packed):
    out = {"b_gate": packed[0:2].reshape(1, 2 * D_MODEL)}
    for i, n in enumerate(_SMALL[1:]):
        out[n] = packed[2 + i:3 + i]
    return out


def kernel(x, mem, w_in, b_gate, w_sb_o, w_ret_o, w_mix_o, ln1_g, ln1_b, w_mem_q, w_mem_kv, w_mem_o, ln2_g, ln2_b, w_ffn_in, w_ffn_out, ln3_g, ln3_b, loss_target, m_w_in, m_b_gate, m_w_sb_o, m_w_ret_o, m_w_mix_o, m_ln1_g, m_ln1_b, m_w_mem_q, m_w_mem_kv, m_w_mem_o, m_ln2_g, m_ln2_b, m_w_ffn_in, m_w_ffn_out, m_ln3_g, m_ln3_b, v_w_in, v_b_gate, v_w_sb_o, v_w_ret_o, v_w_mix_o, v_ln1_g, v_ln1_b, v_w_mem_q, v_w_mem_kv, v_w_mem_o, v_ln2_g, v_ln2_b, v_w_ffn_in, v_w_ffn_out, v_ln3_g, v_ln3_b):
    weights = dict(w_in=w_in, b_gate=b_gate, w_sb_o=w_sb_o, w_ret_o=w_ret_o, w_mix_o=w_mix_o, ln1_g=ln1_g, ln1_b=ln1_b,
                   w_mem_q=w_mem_q, w_mem_kv=w_mem_kv, w_mem_o=w_mem_o, ln2_g=ln2_g, ln2_b=ln2_b, w_ffn_in=w_ffn_in,
                   w_ffn_out=w_ffn_out, ln3_g=ln3_g, ln3_b=ln3_b)
    mom1 = dict(w_in=m_w_in, b_gate=m_b_gate, w_sb_o=m_w_sb_o, w_ret_o=m_w_ret_o, w_mix_o=m_w_mix_o, ln1_g=m_ln1_g,
                ln1_b=m_ln1_b, w_mem_q=m_w_mem_q, w_mem_kv=m_w_mem_kv, w_mem_o=m_w_mem_o, ln2_g=m_ln2_g, ln2_b=m_ln2_b,
                w_ffn_in=m_w_ffn_in, w_ffn_out=m_w_ffn_out, ln3_g=m_ln3_g, ln3_b=m_ln3_b)
    mom2 = dict(w_in=v_w_in, b_gate=v_b_gate, w_sb_o=v_w_sb_o, w_ret_o=v_w_ret_o, w_mix_o=v_w_mix_o, ln1_g=v_ln1_g,
                ln1_b=v_ln1_b, w_mem_q=v_w_mem_q, w_mem_kv=v_w_mem_kv, w_mem_o=v_w_mem_o, ln2_g=v_ln2_g, ln2_b=v_ln2_b,
                w_ffn_in=v_w_ffn_in, w_ffn_out=v_w_ffn_out, ln3_g=v_ln3_g, ln3_b=v_ln3_b)

    (gathered_in,) = _exchange([weights["w_in"][0].astype(BF16)], False, "gather_w_in")
    rest = [n for n in _MATRICES if n != "w_in"]
    received = {}

    def fetch_rest(host):
        res = host(_Rider([weights[n][0].astype(BF16) for n in rest], False))
        return res, {n: _assemble(n, g) for n, g in zip(rest, res[-1])}

    def ship(grads, host):
        names = list(grads)
        bufs = []
        for n in names:
            if n == "small":
                part = _pack_small(grads[n])
                bufs.append(jnp.broadcast_to(part[None], (N_DEV,) + part.shape))
            else:
                bufs.append(_to_slots(n, grads[n]).astype(BF16))
        res = host(_Rider(bufs, True))
        received.update(zip(names, res[-1]))
        return res[:-1]

    small = {n: weights[n] for n in _SMALL}
    loss, d_x = _local_step(x[0], mem[0], _assemble("w_in", gathered_in), small, loss_target[0], fetch_rest, ship)

    new = {}
    for n in _MATRICES:
        new[n] = _adamw(received[n], weights[n][0], mom1[n][0], mom2[n][0], "adamw_" + n)
    packed = _adamw(received["small"], _pack_small({n: weights[n] for n in _SMALL}),
                    _pack_small({n: mom1[n] for n in _SMALL}), _pack_small({n: mom2[n] for n in _SMALL}), "adamw_small")
    small_new = [_unpack_small(p) for p in packed]

    outs = [lax.psum(loss, MESH_AXES), d_x[None]]
    for slot in range(4):
        for n in _WEIGHT_ORDER:
            outs.append(new[n][slot][None] if n in new else small_new[slot][n])
    return tuple(outs)
```

```python
import functools
import math

import jax
import jax.numpy as jnp
from jax import lax
from jax.experimental import pallas as pl
from jax.experimental.pallas import tpu as pltpu

F32 = jnp.float32
BF16 = jnp.bfloat16

N_DEV = 8
D_MODEL = 1024
SB_HEAD_DIM = 64
SB_WIDTH = 512
RET_HEADS = 4
RET_QK_DIM = 128
RET_V_DIM = 256
RET_QK_WIDTH = 512
RET_V_WIDTH = 1024
RET_CHUNK = 128
ROPE_BASE = 10000.0
MEM_HEADS = 4
MEM_HEAD_DIM = 256
FFN_HIDDEN = 2816
DN_ALPHA = 2.0 ** 0.25
LN_EPS = 1e-5
ADAM_LR = 0.001
ADAM_B1 = 0.9
ADAM_B2 = 0.999
ADAM_EPS = 1e-08
ADAM_WD = 0.01
ADAM_STEP = 10

VMEM_LIMIT_BYTES = 52 * 1024 * 1024
LANES = 128
SB_KEY_BLOCK = 128
SB_Q_BLOCK = 256
SB_DEAD_LOG = -105.0

MESH_AXES = ("x", "y", "c")


def _pick(dim, prefs):
    for p in prefs:
        if dim % p == 0:
            return p
    return dim


def _params(sem):
    return pltpu.CompilerParams(dimension_semantics=sem, vmem_limit_bytes=VMEM_LIMIT_BYTES)


def _dot(a, b, dims):
    return lax.dot_general(a, b, (dims, ((), ())), preferred_element_type=F32)


_NN = ((1,), (0,))
_NT = ((1,), (1,))
_TN = ((0,), (0,))


def _my_index():
    return 4 * lax.axis_index("x") + 2 * lax.axis_index("y") + lax.axis_index("c")


def _peer(k):
    x, y, c = lax.axis_index("x"), lax.axis_index("y"), lax.axis_index("c")
    bx, by, bc = (k >> 2) & 1, (k >> 1) & 1, k & 1
    px = (1 - x) if bx else x
    py = (1 - y) if by else y
    pc = (1 - c) if bc else c
    return (px, py, pc), 4 * px + 2 * py + pc


class _Rider:
    def __init__(self, bufs, scatter):
        self.bufs, self.scatter, self.n = list(bufs), scatter, len(bufs)
        self.specs = [pl.BlockSpec(memory_space=pl.ANY)] * self.n
        self.out_shape = [jax.ShapeDtypeStruct(b.shape if scatter else (N_DEV,) + b.shape, b.dtype) for b in self.bufs]
        self.scratch = [pltpu.SemaphoreType.DMA((self.n, N_DEV - 1)), pltpu.SemaphoreType.DMA((self.n, N_DEV - 1)),
                        pltpu.SemaphoreType.DMA((self.n,))]

    def _remote(self, ride, a, k, src_ref, slot, to):
        _, dst, (send_sems, recv_sems, _) = ride
        return pltpu.make_async_remote_copy(src_ref=src_ref, dst_ref=dst[a].at[slot], send_sem=send_sems.at[a, k],
                                            recv_sem=recv_sems.at[a, k], device_id=to,
                                            device_id_type=pl.DeviceIdType.MESH)

    def _local(self, ride, a):
        src, dst, (_, _, local_sems) = ride
        me = _my_index()
        return pltpu.make_async_copy(src[a].at[me] if self.scatter else src[a], dst[a].at[me], local_sems.at[a])

    def _direct(self, ride, a):
        src = ride[0]
        me = _my_index()
        out = []
        for k in range(1, N_DEV):
            peer, peer_idx = _peer(k)
            out.append(self._remote(ride, a, k - 1, src[a].at[peer_idx], me, peer))
        return out

    def _two_level(self, ride, a):
        src, dst = ride[0], ride[1]
        x, y, c = lax.axis_index("x"), lax.axis_index("y"), lax.axis_index("c")
        me, sibling = _my_index(), (x, y, 1 - c)
        chips = [(1 - x, y), (x, 1 - y), (1 - x, 1 - y)]
        first = [self._remote(ride, a, 0, src[a], me, sibling)]
        passed, landing = [], [self._remote(ride, a, 0, src[a], me + 1 - 2 * c, sibling)]
        for j, (px, py) in enumerate(chips):
            first.append(self._remote(ride, a, 1 + j, src[a], me, (px, py, c)))
            theirs = 4 * px + 2 * py + c
            passed.append(self._remote(ride, a, 4 + j, dst[a].at[theirs], theirs, sibling))
            landing.append(self._remote(ride, a, 1 + j, src[a], theirs, (px, py, c)))
        for j, (px, py) in enumerate(chips):
            landing.append(self._remote(ride, a, 4 + j, src[a], 4 * px + 2 * py + 1 - c, sibling))
        return first, passed, landing

    def start(self, ride):
        for a in range(self.n):
            self._local(ride, a).start()
            for cp in (self._direct(ride, a) if self.scatter else self._two_level(ride, a)[0]):
                cp.start()

    def finish(self, ride):
        if self.scatter:
            for a in range(self.n):
                for cp in self._direct(ride, a):
                    cp.wait()
                self._local(ride, a).wait()
            return
        levels = [self._two_level(ride, a) for a in range(self.n)]
        for first, passed, landing in levels:
            for j, cp in enumerate(passed):
                landing[1 + j].wait_recv()
                cp.start()
        for a, (first, passed, landing) in enumerate(levels):
            landing[0].wait_recv()
            for cp in landing[4:]:
                cp.wait_recv()
            for cp in first + passed:
                cp.wait_send()
            self._local(ride, a).wait()

    def start_at_first(self, ids, ride):
        first = functools.reduce(jnp.logical_and, [i == 0 for i in ids])

        @pl.when(first)
        def _():
            self.start(ride)

    def wait_at_last(self, ids, grid, ride):
        last = functools.reduce(jnp.logical_and, [i == g - 1 for i, g in zip(ids, grid)])

        @pl.when(last)
        def _():
            self.finish(ride)


def _exchange(bufs, scatter, name):
    rider = _Rider(bufs, scatter)

    def body(*refs):
        ride = (refs[:rider.n], refs[rider.n:2 * rider.n], refs[2 * rider.n:])
        rider.start(ride)
        rider.finish(ride)

    return pl.pallas_call(
        body,
        name=name,
        in_specs=rider.specs,
        out_specs=rider.specs,
        out_shape=rider.out_shape,
        scratch_shapes=rider.scratch,
    )(*rider.bufs)


MM_RESIDENT_B_BYTES = 14 * 1024 * 1024
MM_A_TILE_BYTES = 4 * 1024 * 1024
MM_OUT_TILE_BYTES = 6 * 1024 * 1024


def _mm_tiles(mode, M, N, K, a_bytes, out_bytes):
    if mode != "tn" and K * N * 2 <= MM_RESIDENT_B_BYTES:
        for tm in (1024, 512, 256, 128):
            if M % tm == 0 and tm * K * a_bytes <= MM_A_TILE_BYTES and tm * N * out_bytes <= MM_OUT_TILE_BYTES:
                return tm, N, K
    if mode == "tn":
        return (_pick(M, (1024, 1408, 512, 256, 128)), _pick(N, (1024, 1664, 1408, 512, 256, 128)),
                _pick(K, (2048, 1024, 512, 256, 128)))
    return _pick(M, (1024, 512, 256, 128)), _pick(N, (512, 256, 128)), _pick(K, (1024, 512, 256, 128))


def _mm(a, b, *, mode, out_dtype, name, res=None, res_scale=1.0, rider=None):
    if mode == "nn":
        (M, K), (K2, N) = a.shape, b.shape
    elif mode == "nt":
        (M, K), (N, K2) = a.shape, b.shape
    else:
        (K, M), (K2, N) = a.shape, b.shape
    assert K == K2, (a.shape, b.shape, mode)
    out_bytes = jnp.dtype(out_dtype).itemsize + (4 if res is not None else 0)
    tm, tn, tk = _mm_tiles(mode, M, N, K, a.dtype.itemsize, out_bytes)
    grid = (M // tm, N // tn, K // tk)
    nk = grid[2]
    dims = {"nn": _NN, "nt": _NT, "tn": _TN}[mode]
    n_in = 2 + (res is not None)
    n_ride = rider.n if rider is not None else 0

    def body(*refs):
        a_ref, b_ref = refs[:2]
        r_ref = refs[2] if res is not None else None
        o_ref = refs[n_in + n_ride]
        rest = refs[n_in + 2 * n_ride + 1:]
        acc_ref = rest[0] if nk > 1 else None
        ids = [pl.program_id(d) for d in range(3)]
        if rider is not None:
            ride = (refs[n_in:n_in + n_ride], refs[n_in + n_ride + 1:n_in + 2 * n_ride + 1], rest[-3:])
            rider.start_at_first(ids, ride)
        part = _dot(a_ref[...].astype(BF16), b_ref[...].astype(BF16), dims)

        def finish(total):
            if r_ref is not None:
                total = total + res_scale * r_ref[...]
            o_ref[...] = total.astype(out_dtype)

        if nk == 1:
            finish(part)
        else:
            k = ids[2]

            @pl.when(k == 0)
            def _():
                acc_ref[...] = part

            @pl.when(k > 0)
            def _():
                acc_ref[...] += part

            @pl.when(k == nk - 1)
            def _():
                finish(acc_ref[...])

        if rider is not None:
            rider.wait_at_last(ids, grid, ride)

    if mode == "nn":
        a_spec = pl.BlockSpec((tm, tk), lambda i, j, k: (i, k))
        b_spec = pl.BlockSpec((tk, tn), lambda i, j, k: (k, j))
    elif mode == "nt":
        a_spec = pl.BlockSpec((tm, tk), lambda i, j, k: (i, k))
        b_spec = pl.BlockSpec((tn, tk), lambda i, j, k: (j, k))
    else:
        a_spec = pl.BlockSpec((tk, tm), lambda i, j, k: (k, i))
        b_spec = pl.BlockSpec((tk, tn), lambda i, j, k: (k, j))
    o_spec = pl.BlockSpec((tm, tn), lambda i, j, k: (i, j))
    in_specs = [a_spec, b_spec] + ([o_spec] if res is not None else [])
    args = (a, b) + ((res,) if res is not None else ())
    out_specs, out_shape = [o_spec], [jax.ShapeDtypeStruct((M, N), out_dtype)]
    scratch = [pltpu.VMEM((tm, tn), F32)] if nk > 1 else []
    sem = ("parallel", "parallel", "arbitrary")
    if rider is not None:
        in_specs, args = in_specs + rider.specs, args + tuple(rider.bufs)
        out_specs, out_shape = out_specs + rider.specs, out_shape + rider.out_shape
        scratch = scratch + rider.scratch
        sem = ("arbitrary",) * 3
    outs = pl.pallas_call(
        body,
        name=name,
        grid=grid,
        in_specs=in_specs,
        out_specs=out_specs,
        out_shape=out_shape,
        scratch_shapes=scratch,
        compiler_params=_params(sem),
    )(*args)
    return outs[0] if rider is None else (outs[0], list(outs[1:]))


def _mm_host(a, b, *, rider, **kw):
    out = _mm(a, b, rider=rider, **kw)
    return out if rider is not None else (out, [])


def _as_host(rider, results):
    return results if rider is not None else tuple(results) + ([],)


MM_FUSED_MARGIN_BYTES = 10 * 1024 * 1024
MM_FUSED_MAX_ROWS = 512


def _col_sum_update(acc_ref, val, first):
    part = jnp.sum(val.reshape(val.shape[0] // 8, 8, val.shape[1]), axis=0)

    @pl.when(first)
    def _():
        acc_ref[...] = part

    @pl.when(jnp.logical_not(first))
    def _():
        acc_ref[...] += part


def _mm_fused(a, b, *, mode, name, extras, outs, epilogue, sums=(), rider=None, max_rows=MM_FUSED_MAX_ROWS,
              pass_a=False):
    if mode == "nn":
        (M, K), (K2, N) = a.shape, b.shape
        b_dims = _NN
    else:
        (M, K), (N, K2) = a.shape, b.shape
        b_dims = _NT
    assert K == K2, (a.shape, b.shape, mode)
    rows = [e for e in extras if e.shape[0] == M]
    per_row = 2 * (K * a.dtype.itemsize + sum(e.shape[1] * e.dtype.itemsize for e in rows)
                   + sum(c * jnp.dtype(d).itemsize for c, d in outs)) + 2 * N * 4
    budget = VMEM_LIMIT_BYTES - K * N * 2 - MM_FUSED_MARGIN_BYTES
    tm = next(t for t in (512, 256, 128, 64, 32, 16) if t <= max_rows and M % t == 0 and t * per_row <= budget)
    steps = M // tm
    n_x, n_o, n_s = len(extras), len(outs), len(sums)
    n_ride = rider.n if rider is not None else 0

    def body(*refs):
        a_ref, b_ref = refs[:2]
        x_refs = refs[2:2 + n_x]
        base = 2 + n_x + n_ride
        o_refs, s_refs = refs[base:base + n_o], refs[base + n_o:base + n_o + n_s]
        acc_refs = refs[base + n_o + n_s + n_ride:base + n_o + 2 * n_s + n_ride]
        ids = [pl.program_id(0)]
        if rider is not None:
            ride = (refs[2 + n_x:base], refs[base + n_o + n_s:base + n_o + n_s + n_ride], refs[-3:])
            rider.start_at_first(ids, ride)
        a_tile = a_ref[...]
        prod = _dot(a_tile.astype(BF16), b_ref[...], b_dims)
        tiles = epilogue(prod, *([a_tile] if pass_a else []), *[r[...] for r in x_refs])
        for o_ref, t in zip(o_refs, tiles[:n_o]):
            o_ref[...] = t.astype(o_ref.dtype)
        for acc_ref, t in zip(acc_refs, tiles[n_o:]):
            _col_sum_update(acc_ref, t, ids[0] == 0)
        if n_s:
            @pl.when(ids[0] == steps - 1)
            def _():
                for s_ref, acc_ref in zip(s_refs, acc_refs):
                    s_ref[...] = jnp.sum(acc_ref[...], axis=0, keepdims=True)
        if rider is not None:
            rider.wait_at_last(ids, (steps,), ride)

    in_specs = [pl.BlockSpec((tm, K), lambda i: (i, 0)),
                pl.BlockSpec(b.shape, lambda i: (0, 0), pipeline_mode=pl.Buffered(1))]
    for e in extras:
        in_specs.append(pl.BlockSpec((tm, e.shape[1]), lambda i: (i, 0)) if e.shape[0] == M
                        else pl.BlockSpec(e.shape, lambda i: (0, 0)))
    out_specs = ([pl.BlockSpec((tm, c), lambda i: (i, 0)) for c, _ in outs]
                 + [pl.BlockSpec((1, c), lambda i: (0, 0)) for c in sums])
    out_shape = ([jax.ShapeDtypeStruct((M, c), d) for c, d in outs]
                 + [jax.ShapeDtypeStruct((1, c), F32) for c in sums])
    args = (a, b) + tuple(extras)
    scratch = [pltpu.VMEM((8, c), F32) for c in sums]
    if rider is not None:
        in_specs, args = in_specs + rider.specs, args + tuple(rider.bufs)
        out_specs, out_shape = out_specs + rider.specs, out_shape + rider.out_shape
        scratch = scratch + rider.scratch
    res = pl.pallas_call(
        body,
        name=name,
        grid=(steps,),
        in_specs=in_specs,
        out_specs=out_specs,
        out_shape=out_shape,
        scratch_shapes=scratch,
        compiler_params=_params(("arbitrary",) if (n_s or rider is not None) else ("parallel",)),
    )(*args)
    return tuple(res[:n_o + n_s]) + ((list(res[n_o + n_s:]),) if rider is not None else ())


def _pair_rows(blk, lane_is_a):
    zero = jnp.zeros_like(blk)
    return jnp.concatenate([jnp.where(lane_is_a, blk, zero), jnp.where(lane_is_a, zero, blk)], axis=0)


SB_STRIP = 32
SB_FWD_PAIRS = 4
SB_BWD_PAIRS = 2


def _pair_lanes(p):
    return slice(p * LANES, (p + 1) * LANES)


def _sb_scan_matrices():
    o = lax.broadcasted_iota(jnp.int32, (2 * LANES, 4 * LANES), 0)
    c = lax.broadcasted_iota(jnp.int32, (2 * LANES, 4 * LANES), 1) & (2 * LANES - 1)
    same = (o >= LANES) == (c >= LANES)
    oo, cc = o & (LANES - 1), c & (LANES - 1)
    return (jnp.where(same & (cc > oo), 1.0, 0.0).astype(BF16), jnp.where(same & (cc < oo), 1.0, 0.0).astype(BF16))


def _sb_log_terms(z):
    log_rem = -jnp.maximum(z, 0.0) - jnp.log(1.0 + jnp.exp(-jnp.abs(z)))
    return log_rem, log_rem + z


def _sb_store_split(ref, strip, val):
    hi = val.astype(BF16)
    ref[pl.ds(strip * SB_STRIP, SB_STRIP), :] = hi
    ref[pl.ds(2 * LANES + strip * SB_STRIP, SB_STRIP), :] = (val - hi.astype(F32)).astype(BF16)


def _sb_fwd(h_a, rider=None):
    assert SB_FWD_PAIRS == 4
    T = h_a.shape[0]
    tq = _pick(T, (SB_Q_BLOCK, SB_KEY_BLOCK))
    nq, per_q, nkb = T // tq, tq // SB_KEY_BLOCK, T // SB_KEY_BLOCK
    n_strips = 2 * LANES // SB_STRIP
    n_ride = rider.n if rider is not None else 0
    after_m, _ = _sb_scan_matrices()
    pairs = SB_FWD_PAIRS

    def body(*refs):
        q_ref, k_ref, v_ref, after_ref = refs[:4]
        a_ref, r_ref, n_ref = refs[4 + n_ride:7 + n_ride]
        z_ref, lb_ref, split_ref, w_ref = refs[7 + 2 * n_ride:11 + 2 * n_ride]
        ids = [pl.program_id(0)]
        if rider is not None:
            ride = (refs[4:4 + n_ride], refs[7 + n_ride:7 + 2 * n_ride], refs[-3:])
            rider.start_at_first(ids, ride)
        i = ids[0]
        q_t = [(q_ref[:, _pair_lanes(p)].astype(F32).T * (SB_HEAD_DIM ** -0.5)).astype(BF16) for p in range(pairs)]
        lane_is_a = lax.broadcasted_iota(jnp.int32, (SB_KEY_BLOCK, LANES), 1) < SB_HEAD_DIM
        q_idx = i * tq + lax.broadcasted_iota(jnp.int32, (1, tq), 1)
        key_off = lax.broadcasted_iota(jnp.int32, (SB_STRIP, 1), 0)

        def causal(ks, s):
            return (ks + (s * SB_STRIP) % SB_KEY_BLOCK + key_off) < q_idx

        def tile(kb, masked, carry):
            acc_t, ra, rb = [list(c) for c in carry]
            ks = pl.multiple_of(kb * SB_KEY_BLOCK, SB_KEY_BLOCK)
            vv = []
            for p in range(pairs):
                kk = _pair_rows(k_ref[pl.ds(ks, SB_KEY_BLOCK), _pair_lanes(p)], lane_is_a)
                vv.append(_pair_rows(v_ref[pl.ds(ks, SB_KEY_BLOCK), _pair_lanes(p)], lane_is_a))
                z_ref[p] = _dot(kk, q_t[p], _NN)
            sums = [[jnp.zeros((8, tq), F32), jnp.zeros((8, tq), F32)] for _ in range(pairs)]
            for p in range(pairs):
                for s in range(n_strips):
                    rows = pl.ds(s * SB_STRIP, SB_STRIP)
                    log_rem, log_beta = _sb_log_terms(z_ref[p, rows, :])
                    lb_ref[p, rows, :] = log_beta
                    if masked:
                        log_rem = jnp.where(causal(ks, s), log_rem, 0.0)
                    _sb_store_split(split_ref.at[p], s, log_rem)
                    head = (s * SB_STRIP) // SB_KEY_BLOCK
                    sums[p][head] = sums[p][head] + jnp.sum(log_rem.reshape(SB_STRIP // 8, 8, tq), axis=0)
            for p in range(pairs):
                z_ref[p] = _dot(after_ref[...], split_ref[p], _NN)
            for p in range(pairs):
                for s in range(n_strips):
                    rows = pl.ds(s * SB_STRIP, SB_STRIP)
                    start = ra[p] if (s * SB_STRIP) < SB_KEY_BLOCK else rb[p]
                    w = jnp.exp(lb_ref[p, rows, :] + z_ref[p, rows, :] + start)
                    if masked:
                        w = jnp.where(causal(ks, s), w, 0.0)
                    w_ref[p, rows, :] = w.astype(BF16)
            for p in range(pairs):
                acc_t[p] = acc_t[p] + _dot(vv[p], w_ref[p], _TN)
                r_ref[2 * p, kb] = ra[p]
                r_ref[2 * p + 1, kb] = rb[p]
                ra[p] = ra[p] + jnp.sum(sums[p][0], axis=0, keepdims=True)
                rb[p] = rb[p] + jnp.sum(sums[p][1], axis=0, keepdims=True)
            return tuple(acc_t), tuple(ra), tuple(rb)

        carry = (tuple(jnp.zeros((LANES, tq), F32) for _ in range(pairs)),
                 tuple(jnp.zeros((1, tq), F32) for _ in range(pairs)),
                 tuple(jnp.zeros((1, tq), F32) for _ in range(pairs)))
        for d in range(per_q):
            carry = tile(i * per_q + (per_q - 1 - d), True, carry)
        n_full = i * per_q

        def alive(c):
            top = functools.reduce(jnp.maximum, c[2] + c[3])
            return jnp.logical_and(c[0] < n_full, jnp.max(top) > SB_DEAD_LOG)

        def step(c):
            return (c[0] + 1,) + tile(n_full - 1 - c[0], False, c[1:])

        walked, acc_t, _, _ = lax.while_loop(alive, step, (jnp.int32(0),) + carry)
        for p in range(pairs):
            a_ref[:, _pair_lanes(p)] = acc_t[p].T.astype(BF16)
        n_ref[...] = jnp.zeros(n_ref.shape, F32) + walked.astype(F32)
        if rider is not None:
            rider.wait_at_last(ids, (nq,), ride)

    wide = pairs * LANES
    in_specs = [pl.BlockSpec((tq, wide), lambda i: (i, 0)),
                pl.BlockSpec((T, wide), lambda i: (0, 1), pipeline_mode=pl.Buffered(1)),
                pl.BlockSpec((T, wide), lambda i: (0, 2), pipeline_mode=pl.Buffered(1)),
                pl.BlockSpec(after_m.shape, lambda i: (0, 0), pipeline_mode=pl.Buffered(1))]
    out_specs = [pl.BlockSpec((tq, wide), lambda i: (i, 0)),
                 pl.BlockSpec((2 * pairs, nkb, 1, tq), lambda i: (0, 0, 0, i)),
                 pl.BlockSpec((1, 8, LANES), lambda i: (i, 0, 0))]
    out_shape = [jax.ShapeDtypeStruct((T, SB_WIDTH), BF16), jax.ShapeDtypeStruct((2 * pairs, nkb, 1, T), F32),
                 jax.ShapeDtypeStruct((nq, 8, LANES), F32)]
    args = (h_a, h_a, h_a, after_m)
    scratch = [pltpu.VMEM((pairs, 2 * LANES, tq), F32), pltpu.VMEM((pairs, 2 * LANES, tq), F32),
               pltpu.VMEM((pairs, 4 * LANES, tq), BF16), pltpu.VMEM((pairs, 2 * LANES, tq), BF16)]
    if rider is not None:
        in_specs, args = in_specs + rider.specs, args + tuple(rider.bufs)
        out_specs, out_shape = out_specs + rider.specs, out_shape + rider.out_shape
        scratch = scratch + rider.scratch
    outs = pl.pallas_call(
        body,
        name="sb_fwd",
        grid=(nq,),
        in_specs=in_specs,
        out_specs=out_specs,
        out_shape=out_shape,
        scratch_shapes=scratch,
        compiler_params=_params(("arbitrary",)),
    )(*args)
    return outs[0], (outs[1], outs[2]), list(outs[3:])


def _sb_bwd(h_a, d_out, saved, rider=None):
    r_mat, walked_blocks = saved
    T = h_a.shape[0]
    tq = _pick(T, (SB_Q_BLOCK, SB_KEY_BLOCK))
    nq, per_q, nkb = T // tq, tq // SB_KEY_BLOCK, T // SB_KEY_BLOCK
    n_strips = 2 * LANES // SB_STRIP
    after_m, before_m = _sb_scan_matrices()
    pairs = SB_BWD_PAIRS
    groups = 4 // pairs
    n_ride = rider.n if rider is not None else 0

    def body(*refs):
        q_ref, k_ref, v_ref, do_ref, r_ref, n_ref, after_ref, before_ref = refs[:8]
        dq_ref, dk_ref, dv_ref = refs[8 + n_ride:11 + n_ride]
        z_ref, lb_ref, split_ref, w_ref, da_ref, dz_ref = refs[11 + 2 * n_ride:17 + 2 * n_ride]
        ids = [pl.program_id(0), pl.program_id(1)]
        if rider is not None:
            ride = (refs[8:8 + n_ride], refs[11 + n_ride:11 + 2 * n_ride], refs[-3:])
            rider.start_at_first(ids, ride)
        i = ids[1]

        @pl.when(i == 0)
        def _():
            dk_ref[...] = jnp.zeros_like(dk_ref)
            dv_ref[...] = jnp.zeros_like(dv_ref)

        scale = SB_HEAD_DIM ** -0.5
        q = [q_ref[:, _pair_lanes(p)] for p in range(pairs)]
        d_o = [do_ref[:, _pair_lanes(p)] for p in range(pairs)]
        q_t = [(x.astype(F32).T * scale).astype(BF16) for x in q]
        do_t = [x.astype(F32).T.astype(BF16) for x in d_o]
        lane_is_a = lax.broadcasted_iota(jnp.int32, (SB_KEY_BLOCK, LANES), 1) < SB_HEAD_DIM
        q_idx = i * tq + lax.broadcasted_iota(jnp.int32, (1, tq), 1)
        key_off = lax.broadcasted_iota(jnp.int32, (SB_STRIP, 1), 0)

        def causal(ks, s):
            return (ks + (s * SB_STRIP) % SB_KEY_BLOCK + key_off) < q_idx

        def tile(kb, masked, carry):
            dq_t, ca, cb = [list(c) for c in carry]
            ks = pl.multiple_of(kb * SB_KEY_BLOCK, SB_KEY_BLOCK)
            kk, vv = [], []
            for p in range(pairs):
                kk.append(_pair_rows(k_ref[pl.ds(ks, SB_KEY_BLOCK), _pair_lanes(p)], lane_is_a))
                vv.append(_pair_rows(v_ref[pl.ds(ks, SB_KEY_BLOCK), _pair_lanes(p)], lane_is_a))
                z_ref[p] = _dot(kk[p], q_t[p], _NN)
            for p in range(pairs):
                for s in range(n_strips):
                    rows = pl.ds(s * SB_STRIP, SB_STRIP)
                    log_rem, log_beta = _sb_log_terms(z_ref[p, rows, :])
                    lb_ref[p, rows, :] = log_beta
                    if masked:
                        log_rem = jnp.where(causal(ks, s), log_rem, 0.0)
                    _sb_store_split(split_ref.at[p], s, log_rem)
            for p in range(pairs):
                z_ref[p] = _dot(after_ref[...], split_ref[p], _NN)
                da_ref[p] = _dot(vv[p], do_t[p], _NN)
            sums = [[jnp.zeros((8, tq), F32), jnp.zeros((8, tq), F32)] for _ in range(pairs)]
            for p in range(pairs):
                for s in range(n_strips):
                    rows = pl.ds(s * SB_STRIP, SB_STRIP)
                    start = r_ref[2 * p + (s * SB_STRIP) // SB_KEY_BLOCK, kb]
                    w = jnp.exp(lb_ref[p, rows, :] + z_ref[p, rows, :] + start)
                    if masked:
                        w = jnp.where(causal(ks, s), w, 0.0)
                    w_ref[p, rows, :] = w.astype(BF16)
                    da = da_ref[p, rows, :] * w
                    da_ref[p, rows, :] = da
                    _sb_store_split(split_ref.at[p], s, da)
                    head = (s * SB_STRIP) // SB_KEY_BLOCK
                    sums[p][head] = sums[p][head] + jnp.sum(da.reshape(SB_STRIP // 8, 8, tq), axis=0)
            for p in range(pairs):
                z_ref[p] = _dot(before_ref[...], split_ref[p], _NN)
            for p in range(pairs):
                for s in range(n_strips):
                    rows = pl.ds(s * SB_STRIP, SB_STRIP)
                    base = ca[p] if (s * SB_STRIP) < SB_KEY_BLOCK else cb[p]
                    sig = jnp.exp(lb_ref[p, rows, :])
                    dz = da_ref[p, rows, :] * (1.0 - sig) - (z_ref[p, rows, :] + base) * sig
                    if masked:
                        dz = jnp.where(causal(ks, s), dz, 0.0)
                    dz_ref[p, rows, :] = (dz * scale).astype(BF16)
            for p in range(pairs):
                dq_t[p] = dq_t[p] + _dot(kk[p], dz_ref[p], _TN)
                dkk = _dot(dz_ref[p], q[p], _NN)
                dvv = _dot(w_ref[p], d_o[p], _NN)
                here = (pl.ds(ks, SB_KEY_BLOCK), _pair_lanes(p))
                dk_ref[here] += jnp.where(lane_is_a, dkk[:SB_KEY_BLOCK], dkk[SB_KEY_BLOCK:])
                dv_ref[here] += jnp.where(lane_is_a, dvv[:SB_KEY_BLOCK], dvv[SB_KEY_BLOCK:])
                ca[p] = ca[p] + jnp.sum(sums[p][0], axis=0, keepdims=True)
                cb[p] = cb[p] + jnp.sum(sums[p][1], axis=0, keepdims=True)
            return tuple(dq_t), tuple(ca), tuple(cb)

        n_full = i * per_q
        walked = jnp.clip(jnp.max(n_ref[...]).astype(jnp.int32), 0, n_full)
        carry = (tuple(jnp.zeros((LANES, tq), F32) for _ in range(pairs)),
                 tuple(jnp.zeros((1, tq), F32) for _ in range(pairs)),
                 tuple(jnp.zeros((1, tq), F32) for _ in range(pairs)))
        carry = lax.fori_loop(n_full - walked, n_full, lambda j, c: tile(j, False, c), carry)
        for d in range(per_q):
            carry = tile(i * per_q + d, True, carry)
        for p in range(pairs):
            dq_ref[:, _pair_lanes(p)] = carry[0][p].T.astype(BF16)
        if rider is not None:
            rider.wait_at_last(ids, (groups, nq), ride)

    wide = pairs * LANES
    mat = pl.BlockSpec(after_m.shape, lambda g, i: (0, 0), pipeline_mode=pl.Buffered(1))
    in_specs = [pl.BlockSpec((tq, wide), lambda g, i: (i, g)),
                pl.BlockSpec((T, wide), lambda g, i: (0, groups + g), pipeline_mode=pl.Buffered(1)),
                pl.BlockSpec((T, wide), lambda g, i: (0, 2 * groups + g), pipeline_mode=pl.Buffered(1)),
                pl.BlockSpec((tq, wide), lambda g, i: (i, g)),
                pl.BlockSpec((2 * pairs, nkb, 1, tq), lambda g, i: (g, 0, 0, i)),
                pl.BlockSpec((1, 8, LANES), lambda g, i: (i, 0, 0)),
                mat, mat]
    out_specs = [pl.BlockSpec((tq, wide), lambda g, i: (i, g)),
                 pl.BlockSpec((T, wide), lambda g, i: (0, g)),
                 pl.BlockSpec((T, wide), lambda g, i: (0, g))]
    out_shape = [jax.ShapeDtypeStruct((T, SB_WIDTH), BF16), jax.ShapeDtypeStruct((T, SB_WIDTH), F32),
                 jax.ShapeDtypeStruct((T, SB_WIDTH), F32)]
    args = (h_a, h_a, h_a, d_out, r_mat, walked_blocks, after_m, before_m)
    scratch = [pltpu.VMEM((pairs, 2 * LANES, tq), F32), pltpu.VMEM((pairs, 2 * LANES, tq), F32),
               pltpu.VMEM((pairs, 4 * LANES, tq), BF16), pltpu.VMEM((pairs, 2 * LANES, tq), BF16),
               pltpu.VMEM((pairs, 2 * LANES, tq), F32), pltpu.VMEM((pairs, 2 * LANES, tq), BF16)]
    if rider is not None:
        in_specs, args = in_specs + rider.specs, args + tuple(rider.bufs)
        out_specs, out_shape = out_specs + rider.specs, out_shape + rider.out_shape
        scratch = scratch + rider.scratch
    outs = pl.pallas_call(
        body,
        name="sb_bwd",
        grid=(groups, nq),
        in_specs=in_specs,
        out_specs=out_specs,
        out_shape=out_shape,
        scratch_shapes=scratch,
        compiler_params=_params(("arbitrary", "arbitrary") if rider is not None else ("parallel", "arbitrary")),
    )(*args)
    return outs[0], outs[1], outs[2], list(outs[3:])


def _ret_tables(T):
    half = RET_QK_DIM // 2
    inv = 1.0 / (ROPE_BASE ** (jnp.arange(half, dtype=F32) / half))
    ang = jnp.arange(T, dtype=F32)[:, None] * inv[None, :]
    cos, sin = jnp.cos(ang), jnp.sin(ang)
    cos_t = jnp.concatenate([cos, cos], axis=1)
    sin_t = jnp.concatenate([-sin, sin], axis=1)
    log_gamma = jnp.log1p(-jnp.exp2(-5.0 - jnp.arange(RET_HEADS, dtype=F32)))
    idx = jnp.arange(RET_CHUNK, dtype=F32)
    rel = idx[:, None] - idx[None, :]
    decay = jnp.where(rel[None] >= 0, jnp.exp(log_gamma[:, None, None] * jnp.maximum(rel, 0.0)[None]), 0.0)
    k_decay = jnp.exp(log_gamma[None, :] * (RET_CHUNK - 1.0 - idx)[:, None])
    q_decay = jnp.exp(log_gamma[None, :] * (idx + 1.0)[:, None])
    chunk_decay = jnp.exp(log_gamma * RET_CHUNK)
    k_dec = jnp.broadcast_to(k_decay.T[:, :, None], (RET_HEADS, RET_CHUNK, LANES))
    q_dec = jnp.broadcast_to(q_decay.T[:, :, None], (RET_HEADS, RET_CHUNK, LANES))
    c_dec = jnp.broadcast_to(chunk_decay[:, None, None], (RET_HEADS, 8, LANES))
    return cos_t, sin_t, decay, k_dec, q_dec, c_dec


def _rotary(x, cos_t, sin_t):
    return x * cos_t + pltpu.roll(x, RET_QK_DIM // 2, 1) * sin_t


def _rotary_transpose(dy, cos_t, sin_t):
    return dy * cos_t + pltpu.roll(dy * sin_t, RET_QK_DIM // 2, 1)


def _head_norm(o):
    mu = jnp.mean(o, axis=1, keepdims=True)
    cen = o - mu
    var = jnp.mean(cen * cen, axis=1, keepdims=True)
    rstd = lax.rsqrt(var + LN_EPS)
    return cen * rstd, rstd


def _ret_specs(nc, reverse):
    def n_of(n):
        return (nc - 1 - n) if reverse else n

    q_spec = pl.BlockSpec((RET_CHUNK, RET_QK_WIDTH), lambda n: (n_of(n), 0))
    k_spec = pl.BlockSpec((RET_CHUNK, RET_QK_WIDTH), lambda n: (n_of(n), 1))
    vv = pl.BlockSpec((RET_CHUNK, RET_V_WIDTH), lambda n: (n_of(n), 0))
    pos = pl.BlockSpec((RET_CHUNK, LANES), lambda n: (n_of(n), 0))
    per_head = pl.BlockSpec((RET_HEADS, RET_CHUNK, LANES), lambda n: (0, 0, 0))
    c_dec = pl.BlockSpec((RET_HEADS, 8, LANES), lambda n: (0, 0, 0))
    state = pl.BlockSpec((RET_HEADS, 1, RET_QK_DIM, RET_V_DIM), lambda n: (0, n_of(n), 0, 0))
    return q_spec, k_spec, vv, pos, per_head, c_dec, state


def _qk_cols(h):
    return slice(h * RET_QK_DIM, (h + 1) * RET_QK_DIM)


def _v_cols(h):
    return slice(h * RET_V_DIM, (h + 1) * RET_V_DIM)


def _ret_fwd(h_b, h_c, h_d, tables):
    T = h_b.shape[0]
    nc = T // RET_CHUNK
    q_spec, k_spec, vv, pos, per_head, c_dec, state = _ret_specs(nc, False)

    def body(q_ref, k_ref, v_ref, g_ref, cos_ref, sin_ref, dec_ref, kd_ref, qd_ref, cd_ref,
             y_ref, o_ref, st_ref, state_ref):
        @pl.when(pl.program_id(0) == 0)
        def _():
            state_ref[...] = jnp.zeros_like(state_ref)

        cos_t, sin_t = cos_ref[...], sin_ref[...]
        for h in range(RET_HEADS):
            q = _rotary(q_ref[:, _qk_cols(h)], cos_t, sin_t) * (RET_QK_DIM ** -0.5)
            k = _rotary(k_ref[:, _qk_cols(h)], cos_t, sin_t)
            v = v_ref[:, _v_cols(h)]
            prev = state_ref[h]
            scores = _dot(q.astype(BF16), k.astype(BF16), _NT) * dec_ref[h]
            inner = _dot(scores.astype(BF16), v, _NN)
            cross = _dot((q * qd_ref[h]).astype(BF16), prev.astype(BF16), _NN)
            o = inner + cross
            st_ref[h, 0] = prev
            kv = _dot((k * kd_ref[h]).astype(BF16), v, _TN)
            state_ref[h] = prev * cd_ref[h, 0:1, 0:1] + kv
            o_ref[:, _v_cols(h)] = o
            normed, _ = _head_norm(o)
            gate = g_ref[:, _v_cols(h)]
            y_ref[:, _v_cols(h)] = (gate * jax.nn.sigmoid(gate) * normed).astype(BF16)

    return pl.pallas_call(
        body,
        name="ret_fwd",
        grid=(nc,),
        in_specs=[q_spec, k_spec, vv, vv, pos, pos, per_head, per_head, per_head, c_dec],
        out_specs=[vv, vv, state],
        out_shape=[jax.ShapeDtypeStruct((T, RET_V_WIDTH), BF16),
                   jax.ShapeDtypeStruct((T, RET_V_WIDTH), F32),
                   jax.ShapeDtypeStruct((RET_HEADS, nc, RET_QK_DIM, RET_V_DIM), F32)],
        scratch_shapes=[pltpu.VMEM((RET_HEADS, RET_QK_DIM, RET_V_DIM), F32)],
        compiler_params=_params(("arbitrary",)),
    )(h_b, h_b, h_c, h_d, *tables)


def _ret_bwd(d_y, o_pre, states, h_b, h_c, h_d, tables, rider=None):
    T = h_b.shape[0]
    nc = T // RET_CHUNK
    q_spec, k_spec, vv, pos, per_head, c_dec, state = _ret_specs(nc, True)
    n_ride = rider.n if rider is not None else 0

    def body(*refs):
        (dy_ref, o_ref, st_ref, q_ref, k_ref, v_ref, g_ref, cos_ref, sin_ref, dec_ref, kd_ref, qd_ref,
         cd_ref) = refs[:13]
        dq_ref, dk_ref, dv_ref, dg_ref = refs[13 + n_ride:17 + n_ride]
        carry_ref = refs[17 + 2 * n_ride]
        ids = [pl.program_id(0)]
        if rider is not None:
            ride = (refs[13:13 + n_ride], refs[17 + n_ride:17 + 2 * n_ride], refs[-3:])
            rider.start_at_first(ids, ride)

        @pl.when(ids[0] == 0)
        def _():
            carry_ref[...] = jnp.zeros_like(carry_ref)

        cos_t, sin_t = cos_ref[...], sin_ref[...]
        scale = RET_QK_DIM ** -0.5
        for h in range(RET_HEADS):
            q = _rotary(q_ref[:, _qk_cols(h)], cos_t, sin_t) * scale
            k = _rotary(k_ref[:, _qk_cols(h)], cos_t, sin_t)
            v = v_ref[:, _v_cols(h)]
            decay, k_dec, q_dec = dec_ref[h], kd_ref[h], qd_ref[h]
            chunk_decay = cd_ref[h, 0:1, 0:1]
            state = st_ref[h, 0].astype(BF16)
            later = carry_ref[h]
            later_b = later.astype(BF16)

            gate = g_ref[:, _v_cols(h)]
            sig = jax.nn.sigmoid(gate)
            silu = gate * sig
            normed, rstd = _head_norm(o_ref[:, _v_cols(h)])
            d_y = dy_ref[:, _v_cols(h)]
            dg_ref[:, _v_cols(h)] = (d_y * normed * (sig * (1.0 + gate * (1.0 - sig)))).astype(BF16)
            d_n = d_y * silu
            d_o = rstd * (d_n - jnp.mean(d_n, axis=1, keepdims=True)
                          - normed * jnp.mean(d_n * normed, axis=1, keepdims=True))
            d_ob = d_o.astype(BF16)

            qb, kb = q.astype(BF16), k.astype(BF16)
            qd_b, kd_b = (q * q_dec).astype(BF16), (k * k_dec).astype(BF16)
            scores = _dot(qb, kb, _NT) * decay
            d_scores = (_dot(d_ob, v, _NT) * decay).astype(BF16)
            dq = _dot(d_scores, kb, _NN) + _dot(d_ob, state, _NT) * q_dec
            dk = _dot(d_scores, qb, _TN) + _dot(v, later_b, _NT) * k_dec
            dv = _dot(scores.astype(BF16), d_ob, _TN) + _dot(kd_b, later_b, _NN)
            carry_ref[h] = _dot(qd_b, d_ob, _TN) + chunk_decay * later
            dq_ref[:, _qk_cols(h)] = _rotary_transpose(dq * scale, cos_t, sin_t).astype(BF16)
            dk_ref[:, _qk_cols(h)] = _rotary_transpose(dk, cos_t, sin_t).astype(BF16)
            dv_ref[:, _v_cols(h)] = dv.astype(BF16)
        if rider is not None:
            rider.wait_at_last(ids, (nc,), ride)

    qk_out = pl.BlockSpec((RET_CHUNK, RET_QK_WIDTH), lambda n: (nc - 1 - n, 0))
    in_specs = [vv, vv, state, q_spec, k_spec, vv, vv, pos, pos, per_head, per_head, per_head, c_dec]
    out_specs = [qk_out, qk_out, vv, vv]
    out_shape = [jax.ShapeDtypeStruct((T, RET_QK_WIDTH), BF16), jax.ShapeDtypeStruct((T, RET_QK_WIDTH), BF16),
                 jax.ShapeDtypeStruct((T, RET_V_WIDTH), BF16), jax.ShapeDtypeStruct((T, RET_V_WIDTH), BF16)]
    args = (d_y, o_pre, states, h_b, h_b, h_c, h_d) + tuple(tables)
    scratch = [pltpu.VMEM((RET_HEADS, RET_QK_DIM, RET_V_DIM), F32)]
    if rider is not None:
        in_specs, args = in_specs + rider.specs, args + tuple(rider.bufs)
        out_specs, out_shape = out_specs + rider.specs, out_shape + rider.out_shape
        scratch = scratch + rider.scratch
    outs = pl.pallas_call(
        body,
        name="ret_bwd",
        grid=(nc,),
        in_specs=in_specs,
        out_specs=out_specs,
        out_shape=out_shape,
        scratch_shapes=scratch,
        compiler_params=_params(("arbitrary",)),
    )(*args)
    return outs[0], outs[1], outs[2], outs[3], list(outs[4:])


def _proj_tiles(h, x):
    return h[:, 0:1536], h[:, 1536:2560], h[:, 2560:3584], h[:, 3584:4608], h[:, 4608:6656], x


def _gate_mix_tiles(y_ret, h_e, b_gate, y_sb):
    gates = jax.nn.sigmoid(h_e + b_gate)
    return y_ret, gates[:, :D_MODEL] * y_sb + gates[:, D_MODEL:] * y_ret


def _gate_mix_grad_tiles(d_mix, h_e, b_gate, y_sb, y_ret):
    gates = jax.nn.sigmoid(h_e + b_gate)
    g0, g1 = gates[:, :D_MODEL], gates[:, D_MODEL:]
    d_e = jnp.concatenate([d_mix * y_sb * g0 * (1.0 - g0), d_mix * y_ret * g1 * (1.0 - g1)], axis=1)
    return d_mix * g0, d_mix * g1, d_e, d_e


def _ln_stats(u):
    mu = jnp.mean(u, axis=1, keepdims=True)
    cen = u - mu
    var = jnp.mean(cen * cen, axis=1, keepdims=True)
    rstd = lax.rsqrt(var + LN_EPS)
    return cen * rstd, rstd


def _ln_input_grad(d_out, gain, xhat, rstd):
    d_hat = d_out * gain
    return rstd * (d_hat - jnp.mean(d_hat, axis=1, keepdims=True)
                   - xhat * jnp.mean(d_hat * xhat, axis=1, keepdims=True))


def _ln_tiles(sub, x_prev, gain, bias):
    xhat, rstd = _ln_stats(DN_ALPHA * x_prev + sub)
    out = xhat * gain + bias
    return out, out, xhat, rstd


def _ln_grad_tiles(d_sub, res, xhat, rstd, gain):
    d_out = d_sub + DN_ALPHA * res
    du = _ln_input_grad(d_out, gain, xhat, rstd)
    return du, du, d_out * xhat, d_out


def _ln_loss_tiles(sub, x_prev, gain, bias, target):
    xhat, rstd = _ln_stats(DN_ALPHA * x_prev + sub)
    diff = xhat * gain + bias - target
    d_out = diff * (1.0 / D_MODEL)
    du = _ln_input_grad(d_out, gain, xhat, rstd)
    return du, du, diff * diff, d_out * xhat, d_out


def _mem_probs(q_h, k_h):
    s = _dot(q_h, k_h, _NT) * (MEM_HEAD_DIM ** -0.5)
    e = jnp.exp(s - jnp.max(s, axis=1, keepdims=True))
    return e / jnp.sum(e, axis=1, keepdims=True)


def _xattn_fwd(q, kv):
    T, mem_len = q.shape[0], kv.shape[0]
    tq = _pick(T, (512, 256, 128))

    def body(q_ref, kv_ref, o_ref):
        for h in range(MEM_HEADS):
            cols = slice(h * MEM_HEAD_DIM, (h + 1) * MEM_HEAD_DIM)
            vcols = slice(D_MODEL + h * MEM_HEAD_DIM, D_MODEL + (h + 1) * MEM_HEAD_DIM)
            p = _mem_probs(q_ref[:, cols], kv_ref[:, cols])
            o_ref[:, cols] = _dot(p.astype(BF16), kv_ref[:, vcols], _NN).astype(BF16)

    return pl.pallas_call(
        body,
        name="xattn_fwd",
        grid=(T // tq,),
        in_specs=[pl.BlockSpec((tq, D_MODEL), lambda i: (i, 0)),
                  pl.BlockSpec((mem_len, 2 * D_MODEL), lambda i: (0, 0))],
        out_specs=pl.BlockSpec((tq, D_MODEL), lambda i: (i, 0)),
        out_shape=jax.ShapeDtypeStruct((T, D_MODEL), BF16),
        compiler_params=_params(("parallel",)),
    )(q, kv)


def _xattn_bwd(q, kv, d_o):
    T, mem_len = q.shape[0], kv.shape[0]
    tq = _pick(T, (512, 256, 128))

    def body(q_ref, kv_ref, do_ref, dq_ref, dkv_ref):
        @pl.when(pl.program_id(0) == 0)
        def _():
            dkv_ref[...] = jnp.zeros_like(dkv_ref)

        for h in range(MEM_HEADS):
            cols = slice(h * MEM_HEAD_DIM, (h + 1) * MEM_HEAD_DIM)
            vcols = slice(D_MODEL + h * MEM_HEAD_DIM, D_MODEL + (h + 1) * MEM_HEAD_DIM)
            q_h, k_h, do_h = q_ref[:, cols], kv_ref[:, cols], do_ref[:, cols]
            p = _mem_probs(q_h, k_h)
            dp = _dot(do_h, kv_ref[:, vcols], _NT)
            ds = p * (dp - jnp.sum(dp * p, axis=1, keepdims=True))
            dsb = (ds * (MEM_HEAD_DIM ** -0.5)).astype(BF16)
            dq_ref[:, cols] = _dot(dsb, k_h, _NN).astype(BF16)
            dkv_ref[:, cols] += _dot(dsb, q_h, _TN)
            dkv_ref[:, vcols] += _dot(p.astype(BF16), do_h, _TN)

    row = pl.BlockSpec((tq, D_MODEL), lambda i: (i, 0))
    full = pl.BlockSpec((mem_len, 2 * D_MODEL), lambda i: (0, 0))
    return pl.pallas_call(
        body,
        name="xattn_bwd",
        grid=(T // tq,),
        in_specs=[row, full, row],
        out_specs=[row, full],
        out_shape=[jax.ShapeDtypeStruct((T, D_MODEL), BF16), jax.ShapeDtypeStruct((mem_len, 2 * D_MODEL), F32)],
        compiler_params=_params(("arbitrary",)),
    )(q, kv, d_o)


def _swiglu_tiles(f):
    a, b = f[:, :FFN_HIDDEN], f[:, FFN_HIDDEN:]
    return f, a * jax.nn.sigmoid(a) * b


def _swiglu_grad_tiles(d_hidden, f):
    a, b = f[:, :FFN_HIDDEN], f[:, FFN_HIDDEN:]
    sig = jax.nn.sigmoid(a)
    return (jnp.concatenate([d_hidden * b * (sig * (1.0 + a * (1.0 - sig))), d_hidden * (a * sig)], axis=1),)


def _local_step(x, mem, w_in, small, target, fetch_rest, ship):
    T = x.shape[0]
    tables = _ret_tables(T)
    memb = mem.astype(BF16)

    h_a, h_b, h_c, h_d, h_e, xb = _mm_fused(
        x, w_in, mode="nn", name="proj_in", extras=[], pass_a=True,
        outs=[(1536, BF16), (1024, F32), (1024, BF16), (1024, F32), (2048, F32), (D_MODEL, BF16)],
        epilogue=_proj_tiles, max_rows=256)
    (a_sb, r_mat, _), w = fetch_rest(lambda rider: _sb_fwd(h_a, rider))
    y_gated, o_pre, states = _ret_fwd(h_b, h_c, h_d, tables)
    y_sb = _mm(a_sb, w["w_sb_o"], mode="nn", out_dtype=F32, name="sb_out")
    row_f32, row_bf16 = (D_MODEL, F32), (D_MODEL, BF16)
    ln_outs = [row_f32, row_bf16, row_f32, (1, F32)]
    y_ret, mix_in = _mm_fused(y_gated, w["w_ret_o"], mode="nn", name="ret_out", extras=[h_e, small["b_gate"], y_sb],
                              outs=[row_f32, row_bf16], epilogue=_gate_mix_tiles)
    x1, x1b, xhat1, rstd1 = _mm_fused(mix_in, w["w_mix_o"], mode="nn", name="mix_out",
                                      extras=[x, small["ln1_g"], small["ln1_b"]], outs=ln_outs, epilogue=_ln_tiles)
    q_m = _mm(x1b, w["w_mem_q"], mode="nn", out_dtype=BF16, name="mem_q")
    kv_m = _mm(memb, w["w_mem_kv"], mode="nn", out_dtype=BF16, name="mem_kv")
    o_m = _xattn_fwd(q_m, kv_m)
    x2, x2b, xhat2, rstd2 = _mm_fused(o_m, w["w_mem_o"], mode="nn", name="mem_out",
                                      extras=[x1, small["ln2_g"], small["ln2_b"]], outs=ln_outs, epilogue=_ln_tiles)
    f, hidden = _mm_fused(x2b, w["w_ffn_in"], mode="nn", name="ffn_in", extras=[],
                          outs=[(2 * FFN_HIDDEN, F32), (FFN_HIDDEN, BF16)], epilogue=_swiglu_tiles)
    du_outs, col = [row_f32, row_bf16], D_MODEL
    du3, du3b, loss_cols, d_ln3_g, d_ln3_b = _mm_fused(
        hidden, w["w_ffn_out"], mode="nn", name="ffn_out", extras=[x2, small["ln3_g"], small["ln3_b"], target],
        outs=du_outs, sums=[col, col, col], epilogue=_ln_loss_tiles, max_rows=256)
    loss = jnp.sum(loss_cols) * (0.5 / D_MODEL)

    g_ffn_out = _mm(hidden, du3b, mode="tn", out_dtype=BF16, name="g_ffn_out")
    (d_f,) = _mm_fused(du3b, w["w_ffn_out"], mode="nt", name="d_hidden", extras=[f],
                       outs=[(2 * FFN_HIDDEN, BF16)], epilogue=_swiglu_grad_tiles)
    g_ffn_in = _mm(x2b, d_f, mode="tn", out_dtype=BF16, name="g_ffn_in")
    du2, du2b, d_ln2_g, d_ln2_b = ship(
        {"w_ffn_out": g_ffn_out},
        lambda rider: _as_host(rider, _mm_fused(
            d_f, w["w_ffn_in"], mode="nt", name="d_x2", extras=[du3, xhat2, rstd2, small["ln2_g"]], outs=du_outs,
            sums=[col, col], epilogue=_ln_grad_tiles, rider=rider, max_rows=256)))
    g_mem_o = _mm(o_m, du2b, mode="tn", out_dtype=BF16, name="g_mem_o")
    d_om = _mm(du2b, w["w_mem_o"], mode="nt", out_dtype=BF16, name="d_om")
    d_qm, d_kvm = _xattn_bwd(q_m, kv_m, d_om)
    g_mem_q = _mm(x1b, d_qm, mode="tn", out_dtype=BF16, name="g_mem_q")
    g_mem_kv = _mm(memb, d_kvm.astype(BF16), mode="tn", out_dtype=BF16, name="g_mem_kv")
    du1, du1b, d_ln1_g, d_ln1_b = _mm_fused(
        d_qm, w["w_mem_q"], mode="nt", name="d_x1", extras=[du2, xhat1, rstd1, small["ln1_g"]], outs=du_outs,
        sums=[col, col], epilogue=_ln_grad_tiles, max_rows=256)
    g_mix_o = _mm(mix_in, du1b, mode="tn", out_dtype=BF16, name="g_mix_o")
    d_ysb, d_yret, d_e, d_b_gate = _mm_fused(
        du1b, w["w_mix_o"], mode="nt", name="d_mix_in", extras=[h_e, small["b_gate"], y_sb, y_ret],
        outs=[row_bf16, row_bf16, (2 * D_MODEL, BF16)], sums=[2 * D_MODEL], epilogue=_gate_mix_grad_tiles,
        max_rows=256)
    g_sb_o = _mm(a_sb, d_ysb, mode="tn", out_dtype=BF16, name="g_sb_o")
    g_ret_o = _mm(y_gated, d_yret, mode="tn", out_dtype=BF16, name="g_ret_o")
    d_asb = _mm(d_ysb, w["w_sb_o"], mode="nt", out_dtype=BF16, name="d_asb")
    d_ygated = _mm(d_yret, w["w_ret_o"], mode="nt", out_dtype=F32, name="d_ygated")
    small_grads = {"b_gate": d_b_gate, "ln1_g": d_ln1_g, "ln1_b": d_ln1_b, "ln2_g": d_ln2_g, "ln2_b": d_ln2_b,
                   "ln3_g": d_ln3_g, "ln3_b": d_ln3_b}
    d_rq, d_rk, d_c, d_d, _ = _ret_bwd(d_ygated, o_pre, states, h_b, h_c, h_d, tables)
    ready = {"w_ffn_in": g_ffn_in, "w_mem_kv": g_mem_kv, "w_mem_q": g_mem_q, "w_mem_o": g_mem_o, "w_mix_o": g_mix_o,
             "w_ret_o": g_ret_o, "w_sb_o": g_sb_o, "small": small_grads}
    d_q, d_k, d_v = ship(ready, lambda rider: _sb_bwd(h_a, d_asb, r_mat, rider))
    d_h = jnp.concatenate([d_q, d_k.astype(BF16), d_v.astype(BF16), d_rq, d_rk, d_c, d_d, d_e], axis=1)
    g_in = _mm(xb, d_h, mode="tn", out_dtype=BF16, name="g_in")
    (d_x,) = ship({"w_in": g_in},
                  lambda rider: _mm_host(d_h, w_in, mode="nt", out_dtype=F32, name="d_x", res=du1, res_scale=DN_ALPHA,
                                         rider=rider))
    return loss, d_x


def _adamw_math(w, g, m, v):
    m = ADAM_B1 * m + (1.0 - ADAM_B1) * g
    v = ADAM_B2 * v + (1.0 - ADAM_B2) * jnp.square(g)
    m_hat = m / (1.0 - ADAM_B1 ** ADAM_STEP)
    v_hat = v / (1.0 - ADAM_B2 ** ADAM_STEP)
    delta = -ADAM_LR * (m_hat / (jnp.sqrt(v_hat) + ADAM_EPS) + ADAM_WD * w)
    return delta, m, v


def _adamw(parts, w, m, v, name):
    R, C = w.shape
    tr = max(t for t in range(16, min(R, 256) + 1, 16) if R % t == 0) if R >= 16 else R

    def body(p_ref, w_ref, m_ref, v_ref, g_ref, d_ref, nm_ref, nv_ref):
        g = p_ref[0].astype(F32)
        for j in range(1, N_DEV):
            g = g + p_ref[j].astype(F32)
        delta, nm, nv = _adamw_math(w_ref[...], g, m_ref[...], v_ref[...])
        g_ref[...] = g
        d_ref[...] = delta
        nm_ref[...] = nm
        nv_ref[...] = nv

    blk = pl.BlockSpec((tr, C), lambda i: (i, 0))
    out = jax.ShapeDtypeStruct((R, C), F32)
    return pl.pallas_call(
        body,
        name=name,
        grid=(R // tr,),
        in_specs=[pl.BlockSpec((N_DEV, tr, C), lambda i: (0, i, 0)), blk, blk, blk],
        out_specs=[blk] * 4,
        out_shape=[out] * 4,
        compiler_params=_params(("parallel",)),
    )(parts, w, m, v)


_SHARD_AXIS = {"w_in": 1, "w_sb_o": 1, "w_ret_o": 0, "w_mix_o": 0, "w_mem_q": 0, "w_mem_kv": 1, "w_mem_o": 0,
               "w_ffn_in": 1, "w_ffn_out": 0}
_MATRICES = tuple(_SHARD_AXIS)
_SMALL = ("b_gate", "ln1_g", "ln1_b", "ln2_g", "ln2_b", "ln3_g", "ln3_b")
_WEIGHT_ORDER = ("w_in", "b_gate", "w_sb_o", "w_ret_o", "w_mix_o", "ln1_g", "ln1_b", "w_mem_q", "w_mem_kv", "w_mem_o",
                 "ln2_g", "ln2_b", "w_ffn_in", "w_ffn_out", "ln3_g", "ln3_b")


def _assemble(name, gathered):
    if _SHARD_AXIS[name] == 0:
        return gathered.reshape(-1, gathered.shape[2])
    return jnp.transpose(gathered, (1, 0, 2)).reshape(gathered.shape[1], -1)


def _to_slots(name, full):
    if _SHARD_AXIS[name] == 0:
        return full.reshape(N_DEV, full.shape[0] // N_DEV, full.shape[1])
    return jnp.transpose(full.reshape(full.shape[0], N_DEV, full.shape[1] // N_DEV), (1, 0, 2))


def _pack_small(vals):
    return jnp.concatenate([vals["b_gate"].reshape(2, D_MODEL)] + [vals[n] for n in _SMALL[1:]], axis=0)


def _unpack_small(packed):
    out = {"b_gate": packed[0:2].reshape(1, 2 * D_MODEL)}
    for i, n in enumerate(_SMALL[1:]):
        out[n] = packed[2 + i:3 + i]
    return out


def kernel(x, mem, w_in, b_gate, w_sb_o, w_ret_o, w_mix_o, ln1_g, ln1_b, w_mem_q, w_mem_kv, w_mem_o, ln2_g, ln2_b, w_ffn_in, w_ffn_out, ln3_g, ln3_b, loss_target, m_w_in, m_b_gate, m_w_sb_o, m_w_ret_o, m_w_mix_o, m_ln1_g, m_ln1_b, m_w_mem_q, m_w_mem_kv, m_w_mem_o, m_ln2_g, m_ln2_b, m_w_ffn_in, m_w_ffn_out, m_ln3_g, m_ln3_b, v_w_in, v_b_gate, v_w_sb_o, v_w_ret_o, v_w_mix_o, v_ln1_g, v_ln1_b, v_w_mem_q, v_w_mem_kv, v_w_mem_o, v_ln2_g, v_ln2_b, v_w_ffn_in, v_w_ffn_out, v_ln3_g, v_ln3_b):
    weights = dict(w_in=w_in, b_gate=b_gate, w_sb_o=w_sb_o, w_ret_o=w_ret_o, w_mix_o=w_mix_o, ln1_g=ln1_g, ln1_b=ln1_b,
                   w_mem_q=w_mem_q, w_mem_kv=w_mem_kv, w_mem_o=w_mem_o, ln2_g=ln2_g, ln2_b=ln2_b, w_ffn_in=w_ffn_in,
                   w_ffn_out=w_ffn_out, ln3_g=ln3_g, ln3_b=ln3_b)
    mom1 = dict(w_in=m_w_in, b_gate=m_b_gate, w_sb_o=m_w_sb_o, w_ret_o=m_w_ret_o, w_mix_o=m_w_mix_o, ln1_g=m_ln1_g,
                ln1_b=m_ln1_b, w_mem_q=m_w_mem_q, w_mem_kv=m_w_mem_kv, w_mem_o=m_w_mem_o, ln2_g=m_ln2_g, ln2_b=m_ln2_b,
                w_ffn_in=m_w_ffn_in, w_ffn_out=m_w_ffn_out, ln3_g=m_ln3_g, ln3_b=m_ln3_b)
    mom2 = dict(w_in=v_w_in, b_gate=v_b_gate, w_sb_o=v_w_sb_o, w_ret_o=v_w_ret_o, w_mix_o=v_w_mix_o, ln1_g=v_ln1_g,
                ln1_b=v_ln1_b, w_mem_q=v_w_mem_q, w_mem_kv=v_w_mem_kv, w_mem_o=v_w_mem_o, ln2_g=v_ln2_g, ln2_b=v_ln2_b,
                w_ffn_in=v_w_ffn_in, w_ffn_out=v_w_ffn_out, ln3_g=v_ln3_g, ln3_b=v_ln3_b)

    (gathered_in,) = _exchange([weights["w_in"][0].astype(BF16)], False, "gather_w_in")
    rest = [n for n in _MATRICES if n != "w_in"]
    received = {}

    def fetch_rest(host):
        res = host(_Rider([weights[n][0].astype(BF16) for n in rest], False))
        return res, {n: _assemble(n, g) for n, g in zip(rest, res[-1])}

    def ship(grads, host):
        names = list(grads)
        bufs = []
        for n in names:
            if n == "small":
                part = _pack_small(grads[n])
                bufs.append(jnp.broadcast_to(part[None], (N_DEV,) + part.shape))
            else:
                bufs.append(_to_slots(n, grads[n]).astype(BF16))
        res = host(_Rider(bufs, True))
        received.update(zip(names, res[-1]))
        return res[:-1]

    small = {n: weights[n] for n in _SMALL}
    loss, d_x = _local_step(x[0], mem[0], _assemble("w_in", gathered_in), small, loss_target[0], fetch_rest, ship)

    new = {}
    for n in _MATRICES:
        new[n] = _adamw(received[n], weights[n][0], mom1[n][0], mom2[n][0], "adamw_" + n)
    packed = _adamw(received["small"], _pack_small({n: weights[n] for n in _SMALL}),
                    _pack_small({n: mom1[n] for n in _SMALL}), _pack_small({n: mom2[n] for n in _SMALL}), "adamw_small")
    small_new = [_unpack_small(p) for p in packed]

    outs = [lax.psum(loss, MESH_AXES), d_x[None]]
    for slot in range(4):
        for n in _WEIGHT_ORDER:
            outs.append(new[n][slot][None] if n in new else small_new[slot][n])
    return tuple(outs)
```

```python
import functools
import math

import jax
import jax.numpy as jnp
from jax import lax
from jax.experimental import pallas as pl
from jax.experimental.pallas import tpu as pltpu

F32 = jnp.float32
BF16 = jnp.bfloat16

N_DEV = 8
D_MODEL = 1024
SB_HEAD_DIM = 64
SB_WIDTH = 512
RET_HEADS = 4
RET_QK_DIM = 128
RET_V_DIM = 256
RET_QK_WIDTH = 512
RET_V_WIDTH = 1024
RET_CHUNK = 128
ROPE_BASE = 10000.0
MEM_HEADS = 4
MEM_HEAD_DIM = 256
FFN_HIDDEN = 2816
DN_ALPHA = 2.0 ** 0.25
LN_EPS = 1e-5
ADAM_LR = 0.001
ADAM_B1 = 0.9
ADAM_B2 = 0.999
ADAM_EPS = 1e-08
ADAM_WD = 0.01
ADAM_STEP = 10

VMEM_LIMIT_BYTES = 52 * 1024 * 1024
LANES = 128
SB_KEY_BLOCK = 128
SB_Q_BLOCK = 256
SB_DEAD_LOG = -105.0

MESH_AXES = ("x", "y", "c")


def _pick(dim, prefs):
    for p in prefs:
        if dim % p == 0:
            return p
    return dim


def _params(sem):
    return pltpu.CompilerParams(dimension_semantics=sem, vmem_limit_bytes=VMEM_LIMIT_BYTES)


def _dot(a, b, dims):
    return lax.dot_general(a, b, (dims, ((), ())), preferred_element_type=F32)


_NN = ((1,), (0,))
_NT = ((1,), (1,))
_TN = ((0,), (0,))


def _my_index():
    return 4 * lax.axis_index("x") + 2 * lax.axis_index("y") + lax.axis_index("c")


def _peer(k):
    x, y, c = lax.axis_index("x"), lax.axis_index("y"), lax.axis_index("c")
    bx, by, bc = (k >> 2) & 1, (k >> 1) & 1, k & 1
    px = (1 - x) if bx else x
    py = (1 - y) if by else y
    pc = (1 - c) if bc else c
    return (px, py, pc), 4 * px + 2 * py + pc


class _Rider:
    def __init__(self, bufs, scatter):
        self.bufs, self.scatter, self.n = list(bufs), scatter, len(bufs)
        self.specs = [pl.BlockSpec(memory_space=pl.ANY)] * self.n
        self.out_shape = [jax.ShapeDtypeStruct(b.shape if scatter else (N_DEV,) + b.shape, b.dtype) for b in self.bufs]
        self.scratch = [pltpu.SemaphoreType.DMA((self.n, N_DEV - 1)), pltpu.SemaphoreType.DMA((self.n, N_DEV - 1)),
                        pltpu.SemaphoreType.DMA((self.n,))]

    def _remote(self, ride, a, k, src_ref, slot, to):
        _, dst, (send_sems, recv_sems, _) = ride
        return pltpu.make_async_remote_copy(src_ref=src_ref, dst_ref=dst[a].at[slot], send_sem=send_sems.at[a, k],
                                            recv_sem=recv_sems.at[a, k], device_id=to,
                                            device_id_type=pl.DeviceIdType.MESH)

    def _local(self, ride, a):
        src, dst, (_, _, local_sems) = ride
        me = _my_index()
        return pltpu.make_async_copy(src[a].at[me] if self.scatter else src[a], dst[a].at[me], local_sems.at[a])

    def _direct(self, ride, a):
        src = ride[0]
        me = _my_index()
        out = []
        for k in range(1, N_DEV):
            peer, peer_idx = _peer(k)
            out.append(self._remote(ride, a, k - 1, src[a].at[peer_idx], me, peer))
        return out

    def _two_level(self, ride, a):
        src, dst = ride[0], ride[1]
        x, y, c = lax.axis_index("x"), lax.axis_index("y"), lax.axis_index("c")
        me, sibling = _my_index(), (x, y, 1 - c)
        chips = [(1 - x, y), (x, 1 - y), (1 - x, 1 - y)]
        first = [self._remote(ride, a, 0, src[a], me, sibling)]
        passed, landing = [], [self._remote(ride, a, 0, src[a], me + 1 - 2 * c, sibling)]
        for j, (px, py) in enumerate(chips):
            first.append(self._remote(ride, a, 1 + j, src[a], me, (px, py, c)))
            theirs = 4 * px + 2 * py + c
            passed.append(self._remote(ride, a, 4 + j, dst[a].at[theirs], theirs, sibling))
            landing.append(self._remote(ride, a, 1 + j, src[a], theirs, (px, py, c)))
        for j, (px, py) in enumerate(chips):
            landing.append(self._remote(ride, a, 4 + j, src[a], 4 * px + 2 * py + 1 - c, sibling))
        return first, passed, landing

    def start(self, ride):
        for a in range(self.n):
            self._local(ride, a).start()
            for cp in (self._direct(ride, a) if self.scatter else self._two_level(ride, a)[0]):
                cp.start()

    def finish(self, ride):
        if self.scatter:
            for a in range(self.n):
                for cp in self._direct(ride, a):
                    cp.wait()
                self._local(ride, a).wait()
            return
        levels = [self._two_level(ride, a) for a in range(self.n)]
        for first, passed, landing in levels:
            for j, cp in enumerate(passed):
                landing[1 + j].wait_recv()
                cp.start()
        for a, (first, passed, landing) in enumerate(levels):
            landing[0].wait_recv()
            for cp in landing[4:]:
                cp.wait_recv()
            for cp in first + passed:
                cp.wait_send()
            self._local(ride, a).wait()

    def start_at_first(self, ids, ride):
        first = functools.reduce(jnp.logical_and, [i == 0 for i in ids])

        @pl.when(first)
        def _():
            self.start(ride)

    def wait_at_last(self, ids, grid, ride):
        last = functools.reduce(jnp.logical_and, [i == g - 1 for i, g in zip(ids, grid)])

        @pl.when(last)
        def _():
            self.finish(ride)


def _exchange(bufs, scatter, name):
    rider = _Rider(bufs, scatter)

    def body(*refs):
        ride = (refs[:rider.n], refs[rider.n:2 * rider.n], refs[2 * rider.n:])
        rider.start(ride)
        rider.finish(ride)

    return pl.pallas_call(
        body,
        name=name,
        in_specs=rider.specs,
        out_specs=rider.specs,
        out_shape=rider.out_shape,
        scratch_shapes=rider.scratch,
    )(*rider.bufs)


MM_RESIDENT_B_BYTES = 14 * 1024 * 1024
MM_A_TILE_BYTES = 4 * 1024 * 1024
MM_OUT_TILE_BYTES = 6 * 1024 * 1024


def _mm_tiles(mode, M, N, K, a_bytes, out_bytes):
    if mode != "tn" and K * N * 2 <= MM_RESIDENT_B_BYTES:
        for tm in (1024, 512, 256, 128):
            if M % tm == 0 and tm * K * a_bytes <= MM_A_TILE_BYTES and tm * N * out_bytes <= MM_OUT_TILE_BYTES:
                return tm, N, K
    if mode == "tn":
        return (_pick(M, (1024, 1408, 512, 256, 128)), _pick(N, (1024, 1664, 1408, 512, 256, 128)),
                _pick(K, (2048, 1024, 512, 256, 128)))
    return _pick(M, (1024, 512, 256, 128)), _pick(N, (512, 256, 128)), _pick(K, (1024, 512, 256, 128))


def _mm(a, b, *, mode, out_dtype, name, res=None, res_scale=1.0, rider=None):
    if mode == "nn":
        (M, K), (K2, N) = a.shape, b.shape
    elif mode == "nt":
        (M, K), (N, K2) = a.shape, b.shape
    else:
        (K, M), (K2, N) = a.shape, b.shape
    assert K == K2, (a.shape, b.shape, mode)
    out_bytes = jnp.dtype(out_dtype).itemsize + (4 if res is not None else 0)
    tm, tn, tk = _mm_tiles(mode, M, N, K, a.dtype.itemsize, out_bytes)
    grid = (M // tm, N // tn, K // tk)
    nk = grid[2]
    dims = {"nn": _NN, "nt": _NT, "tn": _TN}[mode]
    n_in = 2 + (res is not None)
    n_ride = rider.n if rider is not None else 0

    def body(*refs):
        a_ref, b_ref = refs[:2]
        r_ref = refs[2] if res is not None else None
        o_ref = refs[n_in + n_ride]
        rest = refs[n_in + 2 * n_ride + 1:]
        acc_ref = rest[0] if nk > 1 else None
        ids = [pl.program_id(d) for d in range(3)]
        if rider is not None:
            ride = (refs[n_in:n_in + n_ride], refs[n_in + n_ride + 1:n_in + 2 * n_ride + 1], rest[-3:])
            rider.start_at_first(ids, ride)
        part = _dot(a_ref[...].astype(BF16), b_ref[...].astype(BF16), dims)

        def finish(total):
            if r_ref is not None:
                total = total + res_scale * r_ref[...]
            o_ref[...] = total.astype(out_dtype)

        if nk == 1:
            finish(part)
        else:
            k = ids[2]

            @pl.when(k == 0)
            def _():
                acc_ref[...] = part

            @pl.when(k > 0)
            def _():
                acc_ref[...] += part

            @pl.when(k == nk - 1)
            def _():
                finish(acc_ref[...])

        if rider is not None:
            rider.wait_at_last(ids, grid, ride)

    if mode == "nn":
        a_spec = pl.BlockSpec((tm, tk), lambda i, j, k: (i, k))
        b_spec = pl.BlockSpec((tk, tn), lambda i, j, k: (k, j))
    elif mode == "nt":
        a_spec = pl.BlockSpec((tm, tk), lambda i, j, k: (i, k))
        b_spec = pl.BlockSpec((tn, tk), lambda i, j, k: (j, k))
    else:
        a_spec = pl.BlockSpec((tk, tm), lambda i, j, k: (k, i))
        b_spec = pl.BlockSpec((tk, tn), lambda i, j, k: (k, j))
    o_spec = pl.BlockSpec((tm, tn), lambda i, j, k: (i, j))
    in_specs = [a_spec, b_spec] + ([o_spec] if res is not None else [])
    args = (a, b) + ((res,) if res is not None else ())
    out_specs, out_shape = [o_spec], [jax.ShapeDtypeStruct((M, N), out_dtype)]
    scratch = [pltpu.VMEM((tm, tn), F32)] if nk > 1 else []
    sem = ("parallel", "parallel", "arbitrary")
    if rider is not None:
        in_specs, args = in_specs + rider.specs, args + tuple(rider.bufs)
        out_specs, out_shape = out_specs + rider.specs, out_shape + rider.out_shape
        scratch = scratch + rider.scratch
        sem = ("arbitrary",) * 3
    outs = pl.pallas_call(
        body,
        name=name,
        grid=grid,
        in_specs=in_specs,
        out_specs=out_specs,
        out_shape=out_shape,
        scratch_shapes=scratch,
        compiler_params=_params(sem),
    )(*args)
    return outs[0] if rider is None else (outs[0], list(outs[1:]))


def _mm_host(a, b, *, rider, **kw):
    out = _mm(a, b, rider=rider, **kw)
    return out if rider is not None else (out, [])


def _as_host(rider, results):
    return results if rider is not None else tuple(results) + ([],)


MM_FUSED_MARGIN_BYTES = 10 * 1024 * 1024
MM_FUSED_MAX_ROWS = 512


def _col_sum_update(acc_ref, val, first):
    part = jnp.sum(val.reshape(val.shape[0] // 8, 8, val.shape[1]), axis=0)

    @pl.when(first)
    def _():
        acc_ref[...] = part

    @pl.when(jnp.logical_not(first))
    def _():
        acc_ref[...] += part


def _mm_fused(a, b, *, mode, name, extras, outs, epilogue, sums=(), rider=None, max_rows=MM_FUSED_MAX_ROWS,
              pass_a=False):
    parts = list(a) if isinstance(a, (list, tuple)) else [a]
    M, K = parts[0].shape[0], sum(p.shape[1] for p in parts)
    if mode == "nn":
        (K2, N), b_dims = b.shape, _NN
    else:
        (N, K2), b_dims = b.shape, _NT
    assert K == K2, (K, b.shape, mode)
    rows = parts + [e for e in extras if e.shape[0] == M]
    per_row = 2 * (sum(e.shape[1] * e.dtype.itemsize for e in rows)
                   + sum(c * jnp.dtype(d).itemsize for c, d in outs)) + 2 * N * 4
    budget = VMEM_LIMIT_BYTES - K * N * 2 - MM_FUSED_MARGIN_BYTES
    tm = next(t for t in (512, 256, 128, 64, 32, 16) if t <= max_rows and M % t == 0 and t * per_row <= budget)
    steps = M // tm
    n_a, n_x, n_o, n_s = len(parts), len(extras), len(outs), len(sums)
    n_ride = rider.n if rider is not None else 0

    def body(*refs):
        a_refs, b_ref = refs[:n_a], refs[n_a]
        x_refs = refs[n_a + 1:n_a + 1 + n_x]
        base = n_a + 1 + n_x + n_ride
        o_refs, s_refs = refs[base:base + n_o], refs[base + n_o:base + n_o + n_s]
        acc_refs = refs[base + n_o + n_s + n_ride:base + n_o + 2 * n_s + n_ride]
        ids = [pl.program_id(0)]
        if rider is not None:
            ride = (refs[n_a + 1 + n_x:base], refs[base + n_o + n_s:base + n_o + n_s + n_ride], refs[-3:])
            rider.start_at_first(ids, ride)
        a_tile = a_refs[0][...]
        a_bf16 = a_tile.astype(BF16) if n_a == 1 else jnp.concatenate([r[...].astype(BF16) for r in a_refs], axis=1)
        prod = _dot(a_bf16, b_ref[...], b_dims)
        tiles = epilogue(prod, *([a_tile] if pass_a else []), *[r[...] for r in x_refs])
        for o_ref, t in zip(o_refs, tiles[:n_o]):
            o_ref[...] = t.astype(o_ref.dtype)
        for acc_ref, t in zip(acc_refs, tiles[n_o:]):
            _col_sum_update(acc_ref, t, ids[0] == 0)
        if n_s:
            @pl.when(ids[0] == steps - 1)
            def _():
                for s_ref, acc_ref in zip(s_refs, acc_refs):
                    s_ref[...] = jnp.sum(acc_ref[...], axis=0, keepdims=True)
        if rider is not None:
            rider.wait_at_last(ids, (steps,), ride)

    in_specs = [pl.BlockSpec((tm, p.shape[1]), lambda i: (i, 0)) for p in parts]
    in_specs.append(pl.BlockSpec(b.shape, lambda i: (0, 0), pipeline_mode=pl.Buffered(1)))
    for e in extras:
        in_specs.append(pl.BlockSpec((tm, e.shape[1]), lambda i: (i, 0)) if e.shape[0] == M
                        else pl.BlockSpec(e.shape, lambda i: (0, 0)))
    out_specs = ([pl.BlockSpec((tm, c), lambda i: (i, 0)) for c, _ in outs]
                 + [pl.BlockSpec((1, c), lambda i: (0, 0)) for c in sums])
    out_shape = ([jax.ShapeDtypeStruct((M, c), d) for c, d in outs]
                 + [jax.ShapeDtypeStruct((1, c), F32) for c in sums])
    args = tuple(parts) + (b,) + tuple(extras)
    scratch = [pltpu.VMEM((8, c), F32) for c in sums]
    if rider is not None:
        in_specs, args = in_specs + rider.specs, args + tuple(rider.bufs)
        out_specs, out_shape = out_specs + rider.specs, out_shape + rider.out_shape
        scratch = scratch + rider.scratch
    res = pl.pallas_call(
        body,
        name=name,
        grid=(steps,),
        in_specs=in_specs,
        out_specs=out_specs,
        out_shape=out_shape,
        scratch_shapes=scratch,
        compiler_params=_params(("arbitrary",) if (n_s or rider is not None) else ("parallel",)),
    )(*args)
    return tuple(res[:n_o + n_s]) + ((list(res[n_o + n_s:]),) if rider is not None else ())


def _pair_rows(blk, lane_is_a):
    zero = jnp.zeros_like(blk)
    return jnp.concatenate([jnp.where(lane_is_a, blk, zero), jnp.where(lane_is_a, zero, blk)], axis=0)


SB_STRIP = 32
SB_FWD_PAIRS = 4
SB_BWD_PAIRS = 2


def _pair_lanes(p):
    return slice(p * LANES, (p + 1) * LANES)


def _sb_scan_matrices():
    o = lax.broadcasted_iota(jnp.int32, (2 * LANES, 4 * LANES), 0)
    c = lax.broadcasted_iota(jnp.int32, (2 * LANES, 4 * LANES), 1) & (2 * LANES - 1)
    same = (o >= LANES) == (c >= LANES)
    oo, cc = o & (LANES - 1), c & (LANES - 1)
    return (jnp.where(same & (cc > oo), 1.0, 0.0).astype(BF16), jnp.where(same & (cc < oo), 1.0, 0.0).astype(BF16))


def _sb_causal_masks(tq):
    d = lax.broadcasted_iota(jnp.int32, (tq // SB_KEY_BLOCK, SB_KEY_BLOCK, tq), 0)
    k = lax.broadcasted_iota(jnp.int32, (tq // SB_KEY_BLOCK, SB_KEY_BLOCK, tq), 1)
    t = lax.broadcasted_iota(jnp.int32, (tq // SB_KEY_BLOCK, SB_KEY_BLOCK, tq), 2)
    return jnp.where(d * SB_KEY_BLOCK + k < t, 1.0, 0.0).astype(F32)


def _sb_log_terms(z):
    log_rem = -jnp.maximum(z, 0.0) - jnp.log(1.0 + jnp.exp(-jnp.abs(z)))
    return log_rem, log_rem + z


def _sb_store_split(ref, strip, val):
    hi = val.astype(BF16)
    ref[pl.ds(strip * SB_STRIP, SB_STRIP), :] = hi
    ref[pl.ds(2 * LANES + strip * SB_STRIP, SB_STRIP), :] = (val - hi.astype(F32)).astype(BF16)


def _sb_fwd(h_a, rider=None):
    assert SB_FWD_PAIRS == 4
    T = h_a.shape[0]
    tq = _pick(T, (SB_Q_BLOCK, SB_KEY_BLOCK))
    nq, per_q, nkb = T // tq, tq // SB_KEY_BLOCK, T // SB_KEY_BLOCK
    n_strips = 2 * LANES // SB_STRIP
    n_ride = rider.n if rider is not None else 0
    after_m, _ = _sb_scan_matrices()
    causal_m = _sb_causal_masks(tq)
    pairs = SB_FWD_PAIRS

    def body(*refs):
        q_ref, k_ref, v_ref, after_ref, causal_ref = refs[:5]
        a_ref, r_ref, n_ref = refs[5 + n_ride:8 + n_ride]
        z_ref, lb_ref, split_ref, w_ref = refs[8 + 2 * n_ride:12 + 2 * n_ride]
        ids = [pl.program_id(0)]
        if rider is not None:
            ride = (refs[5:5 + n_ride], refs[8 + n_ride:8 + 2 * n_ride], refs[-3:])
            rider.start_at_first(ids, ride)
        i = ids[0]
        q_t = [(q_ref[:, _pair_lanes(p)].astype(F32).T * (SB_HEAD_DIM ** -0.5)).astype(BF16) for p in range(pairs)]
        lane_is_a = lax.broadcasted_iota(jnp.int32, (SB_KEY_BLOCK, LANES), 1) < SB_HEAD_DIM

        def causal(diag, s):
            return causal_ref[diag, pl.ds((s * SB_STRIP) % SB_KEY_BLOCK, SB_STRIP), :]

        def tile(kb, diag, carry):
            masked = diag is not None
            acc_t, ra, rb = [list(c) for c in carry]
            ks = pl.multiple_of(kb * SB_KEY_BLOCK, SB_KEY_BLOCK)
            vv = []
            for p in range(pairs):
                kk = _pair_rows(k_ref[pl.ds(ks, SB_KEY_BLOCK), _pair_lanes(p)], lane_is_a)
                vv.append(_pair_rows(v_ref[pl.ds(ks, SB_KEY_BLOCK), _pair_lanes(p)], lane_is_a))
                z_ref[p] = _dot(kk, q_t[p], _NN)
            sums = [[jnp.zeros((8, tq), F32), jnp.zeros((8, tq), F32)] for _ in range(pairs)]
            for p in range(pairs):
                for s in range(n_strips):
                    rows = pl.ds(s * SB_STRIP, SB_STRIP)
                    log_rem, log_beta = _sb_log_terms(z_ref[p, rows, :])
                    lb_ref[p, rows, :] = log_beta
                    if masked:
                        log_rem = log_rem * causal(diag, s)
                    _sb_store_split(split_ref.at[p], s, log_rem)
                    head = (s * SB_STRIP) // SB_KEY_BLOCK
                    sums[p][head] = sums[p][head] + jnp.sum(log_rem.reshape(SB_STRIP // 8, 8, tq), axis=0)
            for p in range(pairs):
                z_ref[p] = _dot(after_ref[...], split_ref[p], _NN)
            for p in range(pairs):
                for s in range(n_strips):
                    rows = pl.ds(s * SB_STRIP, SB_STRIP)
                    start = ra[p] if (s * SB_STRIP) < SB_KEY_BLOCK else rb[p]
                    w = jnp.exp(lb_ref[p, rows, :] + z_ref[p, rows, :] + start)
                    if masked:
                        w = w * causal(diag, s)
                    w_ref[p, rows, :] = w.astype(BF16)
            for p in range(pairs):
                acc_t[p] = acc_t[p] + _dot(vv[p], w_ref[p], _TN)
                r_ref[2 * p, kb] = ra[p]
                r_ref[2 * p + 1, kb] = rb[p]
                ra[p] = ra[p] + jnp.sum(sums[p][0], axis=0, keepdims=True)
                rb[p] = rb[p] + jnp.sum(sums[p][1], axis=0, keepdims=True)
            return tuple(acc_t), tuple(ra), tuple(rb)

        carry = (tuple(jnp.zeros((LANES, tq), F32) for _ in range(pairs)),
                 tuple(jnp.zeros((1, tq), F32) for _ in range(pairs)),
                 tuple(jnp.zeros((1, tq), F32) for _ in range(pairs)))
        for d in range(per_q):
            carry = tile(i * per_q + (per_q - 1 - d), per_q - 1 - d, carry)
        n_full = i * per_q

        def alive(c):
            top = functools.reduce(jnp.maximum, c[2] + c[3])
            return jnp.logical_and(c[0] < n_full, jnp.max(top) > SB_DEAD_LOG)

        def step(c):
            return (c[0] + 1,) + tile(n_full - 1 - c[0], None, c[1:])

        walked, acc_t, _, _ = lax.while_loop(alive, step, (jnp.int32(0),) + carry)
        for p in range(pairs):
            a_ref[:, _pair_lanes(p)] = acc_t[p].T.astype(BF16)
        n_ref[...] = jnp.zeros(n_ref.shape, F32) + walked.astype(F32)
        if rider is not None:
            rider.wait_at_last(ids, (nq,), ride)

    wide = pairs * LANES
    in_specs = [pl.BlockSpec((tq, wide), lambda i: (i, 0)),
                pl.BlockSpec((T, wide), lambda i: (0, 1), pipeline_mode=pl.Buffered(1)),
                pl.BlockSpec((T, wide), lambda i: (0, 2), pipeline_mode=pl.Buffered(1)),
                pl.BlockSpec(after_m.shape, lambda i: (0, 0), pipeline_mode=pl.Buffered(1)),
                pl.BlockSpec(causal_m.shape, lambda i: (0, 0, 0), pipeline_mode=pl.Buffered(1))]
    out_specs = [pl.BlockSpec((tq, wide), lambda i: (i, 0)),
                 pl.BlockSpec((2 * pairs, nkb, 1, tq), lambda i: (0, 0, 0, i)),
                 pl.BlockSpec((1, 8, LANES), lambda i: (i, 0, 0))]
    out_shape = [jax.ShapeDtypeStruct((T, SB_WIDTH), BF16), jax.ShapeDtypeStruct((2 * pairs, nkb, 1, T), F32),
                 jax.ShapeDtypeStruct((nq, 8, LANES), F32)]
    args = (h_a, h_a, h_a, after_m, causal_m)
    scratch = [pltpu.VMEM((pairs, 2 * LANES, tq), F32), pltpu.VMEM((pairs, 2 * LANES, tq), F32),
               pltpu.VMEM((pairs, 4 * LANES, tq), BF16), pltpu.VMEM((pairs, 2 * LANES, tq), BF16)]
    if rider is not None:
        in_specs, args = in_specs + rider.specs, args + tuple(rider.bufs)
        out_specs, out_shape = out_specs + rider.specs, out_shape + rider.out_shape
        scratch = scratch + rider.scratch
    outs = pl.pallas_call(
        body,
        name="sb_fwd",
        grid=(nq,),
        in_specs=in_specs,
        out_specs=out_specs,
        out_shape=out_shape,
        scratch_shapes=scratch,
        compiler_params=_params(("arbitrary",)),
    )(*args)
    return outs[0], (outs[1], outs[2]), list(outs[3:])


def _sb_bwd(h_a, d_out, saved, rider=None):
    r_mat, walked_blocks = saved
    T = h_a.shape[0]
    tq = _pick(T, (SB_Q_BLOCK, SB_KEY_BLOCK))
    nq, per_q, nkb = T // tq, tq // SB_KEY_BLOCK, T // SB_KEY_BLOCK
    n_strips = 2 * LANES // SB_STRIP
    after_m, before_m = _sb_scan_matrices()
    causal_m = _sb_causal_masks(tq)
    pairs = SB_BWD_PAIRS
    groups = 4 // pairs
    n_ride = rider.n if rider is not None else 0

    def body(*refs):
        q_ref, k_ref, v_ref, do_ref, r_ref, n_ref, after_ref, before_ref, causal_ref = refs[:9]
        dq_ref, dk_ref, dv_ref = refs[9 + n_ride:12 + n_ride]
        z_ref, lb_ref, split_ref, w_ref, da_ref, dz_ref = refs[12 + 2 * n_ride:18 + 2 * n_ride]
        ids = [pl.program_id(0), pl.program_id(1)]
        if rider is not None:
            ride = (refs[9:9 + n_ride], refs[12 + n_ride:12 + 2 * n_ride], refs[-3:])
            rider.start_at_first(ids, ride)
        i = ids[1]

        @pl.when(i == 0)
        def _():
            dk_ref[...] = jnp.zeros_like(dk_ref)
            dv_ref[...] = jnp.zeros_like(dv_ref)

        scale = SB_HEAD_DIM ** -0.5
        q = [q_ref[:, _pair_lanes(p)] for p in range(pairs)]
        d_o = [do_ref[:, _pair_lanes(p)] for p in range(pairs)]
        q_t = [(x.astype(F32).T * scale).astype(BF16) for x in q]
        do_t = [x.astype(F32).T.astype(BF16) for x in d_o]
        lane_is_a = lax.broadcasted_iota(jnp.int32, (SB_KEY_BLOCK, LANES), 1) < SB_HEAD_DIM

        def causal(diag, s):
            return causal_ref[diag, pl.ds((s * SB_STRIP) % SB_KEY_BLOCK, SB_STRIP), :]

        def tile(kb, diag, carry):
            masked = diag is not None
            dq_t, ca, cb = [list(c) for c in carry]
            ks = pl.multiple_of(kb * SB_KEY_BLOCK, SB_KEY_BLOCK)
            kk, vv = [], []
            for p in range(pairs):
                kk.append(_pair_rows(k_ref[pl.ds(ks, SB_KEY_BLOCK), _pair_lanes(p)], lane_is_a))
                vv.append(_pair_rows(v_ref[pl.ds(ks, SB_KEY_BLOCK), _pair_lanes(p)], lane_is_a))
                z_ref[p] = _dot(kk[p], q_t[p], _NN)
            for p in range(pairs):
                for s in range(n_strips):
                    rows = pl.ds(s * SB_STRIP, SB_STRIP)
                    log_rem, log_beta = _sb_log_terms(z_ref[p, rows, :])
                    lb_ref[p, rows, :] = log_beta
                    if masked:
                        log_rem = log_rem * causal(diag, s)
                    _sb_store_split(split_ref.at[p], s, log_rem)
            for p in range(pairs):
                z_ref[p] = _dot(after_ref[...], split_ref[p], _NN)
                da_ref[p] = _dot(vv[p], do_t[p], _NN)
            sums = [[jnp.zeros((8, tq), F32), jnp.zeros((8, tq), F32)] for _ in range(pairs)]
            for p in range(pairs):
                for s in range(n_strips):
                    rows = pl.ds(s * SB_STRIP, SB_STRIP)
                    start = r_ref[2 * p + (s * SB_STRIP) // SB_KEY_BLOCK, kb]
                    w = jnp.exp(lb_ref[p, rows, :] + z_ref[p, rows, :] + start)
                    if masked:
                        w = w * causal(diag, s)
                    w_ref[p, rows, :] = w.astype(BF16)
                    da = da_ref[p, rows, :] * w
                    da_ref[p, rows, :] = da
                    _sb_store_split(split_ref.at[p], s, da)
                    head = (s * SB_STRIP) // SB_KEY_BLOCK
                    sums[p][head] = sums[p][head] + jnp.sum(da.reshape(SB_STRIP // 8, 8, tq), axis=0)
            for p in range(pairs):
                z_ref[p] = _dot(before_ref[...], split_ref[p], _NN)
            for p in range(pairs):
                for s in range(n_strips):
                    rows = pl.ds(s * SB_STRIP, SB_STRIP)
                    base = ca[p] if (s * SB_STRIP) < SB_KEY_BLOCK else cb[p]
                    sig = jnp.exp(lb_ref[p, rows, :])
                    dz = da_ref[p, rows, :] * (1.0 - sig) - (z_ref[p, rows, :] + base) * sig
                    if masked:
                        dz = dz * causal(diag, s)
                    dz_ref[p, rows, :] = (dz * scale).astype(BF16)
            for p in range(pairs):
                dq_t[p] = dq_t[p] + _dot(kk[p], dz_ref[p], _TN)
                dkk = _dot(dz_ref[p], q[p], _NN)
                dvv = _dot(w_ref[p], d_o[p], _NN)
                here = (pl.ds(ks, SB_KEY_BLOCK), _pair_lanes(p))
                dk_ref[here] += jnp.where(lane_is_a, dkk[:SB_KEY_BLOCK], dkk[SB_KEY_BLOCK:])
                dv_ref[here] += jnp.where(lane_is_a, dvv[:SB_KEY_BLOCK], dvv[SB_KEY_BLOCK:])
                ca[p] = ca[p] + jnp.sum(sums[p][0], axis=0, keepdims=True)
                cb[p] = cb[p] + jnp.sum(sums[p][1], axis=0, keepdims=True)
            return tuple(dq_t), tuple(ca), tuple(cb)

        n_full = i * per_q
        walked = jnp.clip(jnp.max(n_ref[...]).astype(jnp.int32), 0, n_full)
        carry = (tuple(jnp.zeros((LANES, tq), F32) for _ in range(pairs)),
                 tuple(jnp.zeros((1, tq), F32) for _ in range(pairs)),
                 tuple(jnp.zeros((1, tq), F32) for _ in range(pairs)))
        carry = lax.fori_loop(n_full - walked, n_full, lambda j, c: tile(j, None, c), carry)
        for d in range(per_q):
            carry = tile(i * per_q + d, d, carry)
        for p in range(pairs):
            dq_ref[:, _pair_lanes(p)] = carry[0][p].T.astype(BF16)
        if rider is not None:
            rider.wait_at_last(ids, (groups, nq), ride)

    wide = pairs * LANES
    mat = pl.BlockSpec(after_m.shape, lambda g, i: (0, 0), pipeline_mode=pl.Buffered(1))
    in_specs = [pl.BlockSpec((tq, wide), lambda g, i: (i, g)),
                pl.BlockSpec((T, wide), lambda g, i: (0, groups + g), pipeline_mode=pl.Buffered(1)),
                pl.BlockSpec((T, wide), lambda g, i: (0, 2 * groups + g), pipeline_mode=pl.Buffered(1)),
                pl.BlockSpec((tq, wide), lambda g, i: (i, g)),
                pl.BlockSpec((2 * pairs, nkb, 1, tq), lambda g, i: (g, 0, 0, i)),
                pl.BlockSpec((1, 8, LANES), lambda g, i: (i, 0, 0)),
                mat, mat,
                pl.BlockSpec(causal_m.shape, lambda g, i: (0, 0, 0), pipeline_mode=pl.Buffered(1))]
    out_specs = [pl.BlockSpec((tq, wide), lambda g, i: (i, g)),
                 pl.BlockSpec((T, wide), lambda g, i: (0, g)),
                 pl.BlockSpec((T, wide), lambda g, i: (0, g))]
    out_shape = [jax.ShapeDtypeStruct((T, SB_WIDTH), BF16), jax.ShapeDtypeStruct((T, SB_WIDTH), F32),
                 jax.ShapeDtypeStruct((T, SB_WIDTH), F32)]
    args = (h_a, h_a, h_a, d_out, r_mat, walked_blocks, after_m, before_m, causal_m)
    scratch = [pltpu.VMEM((pairs, 2 * LANES, tq), F32), pltpu.VMEM((pairs, 2 * LANES, tq), F32),
               pltpu.VMEM((pairs, 4 * LANES, tq), BF16), pltpu.VMEM((pairs, 2 * LANES, tq), BF16),
               pltpu.VMEM((pairs, 2 * LANES, tq), F32), pltpu.VMEM((pairs, 2 * LANES, tq), BF16)]
    if rider is not None:
        in_specs, args = in_specs + rider.specs, args + tuple(rider.bufs)
        out_specs, out_shape = out_specs + rider.specs, out_shape + rider.out_shape
        scratch = scratch + rider.scratch
    outs = pl.pallas_call(
        body,
        name="sb_bwd",
        grid=(groups, nq),
        in_specs=in_specs,
        out_specs=out_specs,
        out_shape=out_shape,
        scratch_shapes=scratch,
        compiler_params=_params(("arbitrary", "arbitrary") if rider is not None else ("parallel", "arbitrary")),
    )(*args)
    return outs[0], outs[1], outs[2], list(outs[3:])


def _ret_tables(T):
    half = RET_QK_DIM // 2
    inv = 1.0 / (ROPE_BASE ** (jnp.arange(half, dtype=F32) / half))
    ang = jnp.arange(T, dtype=F32)[:, None] * inv[None, :]
    cos, sin = jnp.cos(ang), jnp.sin(ang)
    cos_t = jnp.concatenate([cos, cos], axis=1)
    sin_t = jnp.concatenate([-sin, sin], axis=1)
    log_gamma = jnp.log1p(-jnp.exp2(-5.0 - jnp.arange(RET_HEADS, dtype=F32)))
    idx = jnp.arange(RET_CHUNK, dtype=F32)
    rel = idx[:, None] - idx[None, :]
    decay = jnp.where(rel[None] >= 0, jnp.exp(log_gamma[:, None, None] * jnp.maximum(rel, 0.0)[None]), 0.0)
    k_decay = jnp.exp(log_gamma[None, :] * (RET_CHUNK - 1.0 - idx)[:, None])
    q_decay = jnp.exp(log_gamma[None, :] * (idx + 1.0)[:, None])
    chunk_decay = jnp.exp(log_gamma * RET_CHUNK)
    k_dec = jnp.broadcast_to(k_decay.T[:, :, None], (RET_HEADS, RET_CHUNK, LANES))
    q_dec = jnp.broadcast_to(q_decay.T[:, :, None], (RET_HEADS, RET_CHUNK, LANES))
    c_dec = jnp.broadcast_to(chunk_decay[:, None, None], (RET_HEADS, 8, LANES))
    return cos_t, sin_t, decay, k_dec, q_dec, c_dec


def _rotary(x, cos_t, sin_t):
    return x * cos_t + pltpu.roll(x, RET_QK_DIM // 2, 1) * sin_t


def _rotary_transpose(dy, cos_t, sin_t):
    return dy * cos_t + pltpu.roll(dy * sin_t, RET_QK_DIM // 2, 1)


def _head_norm(o):
    mu = jnp.mean(o, axis=1, keepdims=True)
    cen = o - mu
    var = jnp.mean(cen * cen, axis=1, keepdims=True)
    rstd = lax.rsqrt(var + LN_EPS)
    return cen * rstd, rstd


def _ret_specs(nc, reverse):
    def n_of(n):
        return (nc - 1 - n) if reverse else n

    q_spec = pl.BlockSpec((RET_CHUNK, RET_QK_WIDTH), lambda n: (n_of(n), 0))
    k_spec = pl.BlockSpec((RET_CHUNK, RET_QK_WIDTH), lambda n: (n_of(n), 1))
    vv = pl.BlockSpec((RET_CHUNK, RET_V_WIDTH), lambda n: (n_of(n), 0))
    pos = pl.BlockSpec((RET_CHUNK, LANES), lambda n: (n_of(n), 0))
    per_head = pl.BlockSpec((RET_HEADS, RET_CHUNK, LANES), lambda n: (0, 0, 0))
    c_dec = pl.BlockSpec((RET_HEADS, 8, LANES), lambda n: (0, 0, 0))
    state = pl.BlockSpec((RET_HEADS, 1, RET_QK_DIM, RET_V_DIM), lambda n: (0, n_of(n), 0, 0))
    return q_spec, k_spec, vv, pos, per_head, c_dec, state


def _qk_cols(h):
    return slice(h * RET_QK_DIM, (h + 1) * RET_QK_DIM)


def _v_cols(h):
    return slice(h * RET_V_DIM, (h + 1) * RET_V_DIM)


def _ret_fwd(h_b, h_c, h_d, tables):
    T = h_b.shape[0]
    nc = T // RET_CHUNK
    q_spec, k_spec, vv, pos, per_head, c_dec, state = _ret_specs(nc, False)

    def body(q_ref, k_ref, v_ref, g_ref, cos_ref, sin_ref, dec_ref, kd_ref, qd_ref, cd_ref,
             y_ref, o_ref, st_ref, state_ref):
        @pl.when(pl.program_id(0) == 0)
        def _():
            state_ref[...] = jnp.zeros_like(state_ref)

        cos_t, sin_t = cos_ref[...], sin_ref[...]
        for h in range(RET_HEADS):
            q = _rotary(q_ref[:, _qk_cols(h)], cos_t, sin_t) * (RET_QK_DIM ** -0.5)
            k = _rotary(k_ref[:, _qk_cols(h)], cos_t, sin_t)
            v = v_ref[:, _v_cols(h)]
            prev = state_ref[h]
            scores = _dot(q.astype(BF16), k.astype(BF16), _NT) * dec_ref[h]
            inner = _dot(scores.astype(BF16), v, _NN)
            cross = _dot((q * qd_ref[h]).astype(BF16), prev.astype(BF16), _NN)
            o = inner + cross
            st_ref[h, 0] = prev
            kv = _dot((k * kd_ref[h]).astype(BF16), v, _TN)
            state_ref[h] = prev * cd_ref[h, 0:1, 0:1] + kv
            o_ref[:, _v_cols(h)] = o
            normed, _ = _head_norm(o)
            gate = g_ref[:, _v_cols(h)]
            y_ref[:, _v_cols(h)] = (gate * jax.nn.sigmoid(gate) * normed).astype(BF16)

    return pl.pallas_call(
        body,
        name="ret_fwd",
        grid=(nc,),
        in_specs=[q_spec, k_spec, vv, vv, pos, pos, per_head, per_head, per_head, c_dec],
        out_specs=[vv, vv, state],
        out_shape=[jax.ShapeDtypeStruct((T, RET_V_WIDTH), BF16),
                   jax.ShapeDtypeStruct((T, RET_V_WIDTH), F32),
                   jax.ShapeDtypeStruct((RET_HEADS, nc, RET_QK_DIM, RET_V_DIM), F32)],
        scratch_shapes=[pltpu.VMEM((RET_HEADS, RET_QK_DIM, RET_V_DIM), F32)],
        compiler_params=_params(("arbitrary",)),
    )(h_b, h_b, h_c, h_d, *tables)


def _ret_bwd(d_y, o_pre, states, h_b, h_c, h_d, tables, rider=None):
    T = h_b.shape[0]
    nc = T // RET_CHUNK
    q_spec, k_spec, vv, pos, per_head, c_dec, state = _ret_specs(nc, True)
    n_ride = rider.n if rider is not None else 0

    def body(*refs):
        (dy_ref, o_ref, st_ref, q_ref, k_ref, v_ref, g_ref, cos_ref, sin_ref, dec_ref, kd_ref, qd_ref,
         cd_ref) = refs[:13]
        dq_ref, dk_ref, dv_ref, dg_ref = refs[13 + n_ride:17 + n_ride]
        carry_ref = refs[17 + 2 * n_ride]
        ids = [pl.program_id(0)]
        if rider is not None:
            ride = (refs[13:13 + n_ride], refs[17 + n_ride:17 + 2 * n_ride], refs[-3:])
            rider.start_at_first(ids, ride)

        @pl.when(ids[0] == 0)
        def _():
            carry_ref[...] = jnp.zeros_like(carry_ref)

        cos_t, sin_t = cos_ref[...], sin_ref[...]
        scale = RET_QK_DIM ** -0.5
        for h in range(RET_HEADS):
            q = _rotary(q_ref[:, _qk_cols(h)], cos_t, sin_t) * scale
            k = _rotary(k_ref[:, _qk_cols(h)], cos_t, sin_t)
            v = v_ref[:, _v_cols(h)]
            decay, k_dec, q_dec = dec_ref[h], kd_ref[h], qd_ref[h]
            chunk_decay = cd_ref[h, 0:1, 0:1]
            state = st_ref[h, 0].astype(BF16)
            later = carry_ref[h]
            later_b = later.astype(BF16)

            gate = g_ref[:, _v_cols(h)]
            sig = jax.nn.sigmoid(gate)
            silu = gate * sig
            normed, rstd = _head_norm(o_ref[:, _v_cols(h)])
            d_y = dy_ref[:, _v_cols(h)]
            dg_ref[:, _v_cols(h)] = (d_y * normed * (sig * (1.0 + gate * (1.0 - sig)))).astype(BF16)
            d_n = d_y * silu
            d_o = rstd * (d_n - jnp.mean(d_n, axis=1, keepdims=True)
                          - normed * jnp.mean(d_n * normed, axis=1, keepdims=True))
            d_ob = d_o.astype(BF16)

            qb, kb = q.astype(BF16), k.astype(BF16)
            qd_b, kd_b = (q * q_dec).astype(BF16), (k * k_dec).astype(BF16)
            scores = _dot(qb, kb, _NT) * decay
            d_scores = (_dot(d_ob, v, _NT) * decay).astype(BF16)
            dq = _dot(d_scores, kb, _NN) + _dot(d_ob, state, _NT) * q_dec
            dk = _dot(d_scores, qb, _TN) + _dot(v, later_b, _NT) * k_dec
            dv = _dot(scores.astype(BF16), d_ob, _TN) + _dot(kd_b, later_b, _NN)
            carry_ref[h] = _dot(qd_b, d_ob, _TN) + chunk_decay * later
            dq_ref[:, _qk_cols(h)] = _rotary_transpose(dq * scale, cos_t, sin_t).astype(BF16)
            dk_ref[:, _qk_cols(h)] = _rotary_transpose(dk, cos_t, sin_t).astype(BF16)
            dv_ref[:, _v_cols(h)] = dv.astype(BF16)
        if rider is not None:
            rider.wait_at_last(ids, (nc,), ride)

    qk_out = pl.BlockSpec((RET_CHUNK, RET_QK_WIDTH), lambda n: (nc - 1 - n, 0))
    in_specs = [vv, vv, state, q_spec, k_spec, vv, vv, pos, pos, per_head, per_head, per_head, c_dec]
    out_specs = [qk_out, qk_out, vv, vv]
    out_shape = [jax.ShapeDtypeStruct((T, RET_QK_WIDTH), BF16), jax.ShapeDtypeStruct((T, RET_QK_WIDTH), BF16),
                 jax.ShapeDtypeStruct((T, RET_V_WIDTH), BF16), jax.ShapeDtypeStruct((T, RET_V_WIDTH), BF16)]
    args = (d_y, o_pre, states, h_b, h_b, h_c, h_d) + tuple(tables)
    scratch = [pltpu.VMEM((RET_HEADS, RET_QK_DIM, RET_V_DIM), F32)]
    if rider is not None:
        in_specs, args = in_specs + rider.specs, args + tuple(rider.bufs)
        out_specs, out_shape = out_specs + rider.specs, out_shape + rider.out_shape
        scratch = scratch + rider.scratch
    outs = pl.pallas_call(
        body,
        name="ret_bwd",
        grid=(nc,),
        in_specs=in_specs,
        out_specs=out_specs,
        out_shape=out_shape,
        scratch_shapes=scratch,
        compiler_params=_params(("arbitrary",)),
    )(*args)
    return outs[0], outs[1], outs[2], outs[3], list(outs[4:])


def _proj_tiles(h, x):
    return h[:, 0:1536], h[:, 1536:2560], h[:, 2560:3584], h[:, 3584:4608], h[:, 4608:6656], x


def _gate_mix_tiles(y_ret, h_e, b_gate, y_sb):
    gates = jax.nn.sigmoid(h_e + b_gate)
    return y_ret, gates[:, :D_MODEL] * y_sb + gates[:, D_MODEL:] * y_ret


def _gate_mix_grad_tiles(d_mix, h_e, b_gate, y_sb, y_ret):
    gates = jax.nn.sigmoid(h_e + b_gate)
    g0, g1 = gates[:, :D_MODEL], gates[:, D_MODEL:]
    d_e = jnp.concatenate([d_mix * y_sb * g0 * (1.0 - g0), d_mix * y_ret * g1 * (1.0 - g1)], axis=1)
    return d_mix * g0, d_mix * g1, d_e, d_e


def _ln_stats(u):
    mu = jnp.mean(u, axis=1, keepdims=True)
    cen = u - mu
    var = jnp.mean(cen * cen, axis=1, keepdims=True)
    rstd = lax.rsqrt(var + LN_EPS)
    return cen * rstd, rstd


def _ln_input_grad(d_out, gain, xhat, rstd):
    d_hat = d_out * gain
    return rstd * (d_hat - jnp.mean(d_hat, axis=1, keepdims=True)
                   - xhat * jnp.mean(d_hat * xhat, axis=1, keepdims=True))


def _ln_tiles(sub, x_prev, gain, bias):
    xhat, rstd = _ln_stats(DN_ALPHA * x_prev + sub)
    out = xhat * gain + bias
    return out, out, xhat, rstd


def _residual_tiles(d_sub, res):
    return (d_sub + DN_ALPHA * res,)


def _ln_grad_tiles(d_sub, res, xhat, rstd, gain):
    d_out = d_sub + DN_ALPHA * res
    du = _ln_input_grad(d_out, gain, xhat, rstd)
    return du, du, d_out * xhat, d_out


def _ln_loss_tiles(sub, x_prev, gain, bias, target):
    xhat, rstd = _ln_stats(DN_ALPHA * x_prev + sub)
    diff = xhat * gain + bias - target
    d_out = diff * (1.0 / D_MODEL)
    du = _ln_input_grad(d_out, gain, xhat, rstd)
    return du, du, diff * diff, d_out * xhat, d_out


def _mem_probs(q_h, k_h):
    s = _dot(q_h, k_h, _NT) * (MEM_HEAD_DIM ** -0.5)
    e = jnp.exp(s - jnp.max(s, axis=1, keepdims=True))
    return e / jnp.sum(e, axis=1, keepdims=True)


def _xattn_fwd(q, kv):
    T, mem_len = q.shape[0], kv.shape[0]
    tq = _pick(T, (512, 256, 128))

    def body(q_ref, kv_ref, o_ref):
        for h in range(MEM_HEADS):
            cols = slice(h * MEM_HEAD_DIM, (h + 1) * MEM_HEAD_DIM)
            vcols = slice(D_MODEL + h * MEM_HEAD_DIM, D_MODEL + (h + 1) * MEM_HEAD_DIM)
            p = _mem_probs(q_ref[:, cols], kv_ref[:, cols])
            o_ref[:, cols] = _dot(p.astype(BF16), kv_ref[:, vcols], _NN).astype(BF16)

    return pl.pallas_call(
        body,
        name="xattn_fwd",
        grid=(T // tq,),
        in_specs=[pl.BlockSpec((tq, D_MODEL), lambda i: (i, 0)),
                  pl.BlockSpec((mem_len, 2 * D_MODEL), lambda i: (0, 0))],
        out_specs=pl.BlockSpec((tq, D_MODEL), lambda i: (i, 0)),
        out_shape=jax.ShapeDtypeStruct((T, D_MODEL), BF16),
        compiler_params=_params(("parallel",)),
    )(q, kv)


def _xattn_bwd(q, kv, d_o):
    T, mem_len = q.shape[0], kv.shape[0]
    tq = _pick(T, (512, 256, 128))

    def body(q_ref, kv_ref, do_ref, dq_ref, dkv_ref):
        @pl.when(pl.program_id(0) == 0)
        def _():
            dkv_ref[...] = jnp.zeros_like(dkv_ref)

        for h in range(MEM_HEADS):
            cols = slice(h * MEM_HEAD_DIM, (h + 1) * MEM_HEAD_DIM)
            vcols = slice(D_MODEL + h * MEM_HEAD_DIM, D_MODEL + (h + 1) * MEM_HEAD_DIM)
            q_h, k_h, do_h = q_ref[:, cols], kv_ref[:, cols], do_ref[:, cols]
            p = _mem_probs(q_h, k_h)
            dp = _dot(do_h, kv_ref[:, vcols], _NT)
            ds = p * (dp - jnp.sum(dp * p, axis=1, keepdims=True))
            dsb = (ds * (MEM_HEAD_DIM ** -0.5)).astype(BF16)
            dq_ref[:, cols] = _dot(dsb, k_h, _NN).astype(BF16)
            dkv_ref[:, cols] += _dot(dsb, q_h, _TN)
            dkv_ref[:, vcols] += _dot(p.astype(BF16), do_h, _TN)

    row = pl.BlockSpec((tq, D_MODEL), lambda i: (i, 0))
    full = pl.BlockSpec((mem_len, 2 * D_MODEL), lambda i: (0, 0))
    return pl.pallas_call(
        body,
        name="xattn_bwd",
        grid=(T // tq,),
        in_specs=[row, full, row],
        out_specs=[row, full],
        out_shape=[jax.ShapeDtypeStruct((T, D_MODEL), BF16), jax.ShapeDtypeStruct((mem_len, 2 * D_MODEL), F32)],
        compiler_params=_params(("arbitrary",)),
    )(q, kv, d_o)


def _swiglu_tiles(f):
    a, b = f[:, :FFN_HIDDEN], f[:, FFN_HIDDEN:]
    return f, a * jax.nn.sigmoid(a) * b


def _swiglu_grad_tiles(d_hidden, f):
    a, b = f[:, :FFN_HIDDEN], f[:, FFN_HIDDEN:]
    sig = jax.nn.sigmoid(a)
    return (jnp.concatenate([d_hidden * b * (sig * (1.0 + a * (1.0 - sig))), d_hidden * (a * sig)], axis=1),)


def _local_step(x, mem, w_in, small, target, fetch_rest, ship):
    T = x.shape[0]
    tables = _ret_tables(T)
    memb = mem.astype(BF16)

    h_a, h_b, h_c, h_d, h_e, xb = _mm_fused(
        x, w_in, mode="nn", name="proj_in", extras=[], pass_a=True,
        outs=[(1536, BF16), (1024, F32), (1024, BF16), (1024, F32), (2048, F32), (D_MODEL, BF16)],
        epilogue=_proj_tiles, max_rows=256)
    (a_sb, r_mat, _), w = fetch_rest(lambda rider: _sb_fwd(h_a, rider))
    y_gated, o_pre, states = _ret_fwd(h_b, h_c, h_d, tables)
    y_sb = _mm(a_sb, w["w_sb_o"], mode="nn", out_dtype=F32, name="sb_out")
    row_f32, row_bf16 = (D_MODEL, F32), (D_MODEL, BF16)
    ln_outs = [row_f32, row_bf16, row_f32, (1, F32)]
    y_ret, mix_in = _mm_fused(y_gated, w["w_ret_o"], mode="nn", name="ret_out", extras=[h_e, small["b_gate"], y_sb],
                              outs=[row_f32, row_bf16], epilogue=_gate_mix_tiles)
    x1, x1b, xhat1, rstd1 = _mm_fused(mix_in, w["w_mix_o"], mode="nn", name="mix_out",
                                      extras=[x, small["ln1_g"], small["ln1_b"]], outs=ln_outs, epilogue=_ln_tiles)
    q_m = _mm(x1b, w["w_mem_q"], mode="nn", out_dtype=BF16, name="mem_q")
    kv_m = _mm(memb, w["w_mem_kv"], mode="nn", out_dtype=BF16, name="mem_kv")
    o_m = _xattn_fwd(q_m, kv_m)
    x2, x2b, xhat2, rstd2 = _mm_fused(o_m, w["w_mem_o"], mode="nn", name="mem_out",
                                      extras=[x1, small["ln2_g"], small["ln2_b"]], outs=ln_outs, epilogue=_ln_tiles)
    f, hidden = _mm_fused(x2b, w["w_ffn_in"], mode="nn", name="ffn_in", extras=[],
                          outs=[(2 * FFN_HIDDEN, F32), (FFN_HIDDEN, BF16)], epilogue=_swiglu_tiles)
    du_outs, col = [row_f32, row_bf16], D_MODEL
    du3, du3b, loss_cols, d_ln3_g, d_ln3_b = _mm_fused(
        hidden, w["w_ffn_out"], mode="nn", name="ffn_out", extras=[x2, small["ln3_g"], small["ln3_b"], target],
        outs=du_outs, sums=[col, col, col], epilogue=_ln_loss_tiles, max_rows=256)
    loss = jnp.sum(loss_cols) * (0.5 / D_MODEL)

    g_ffn_out = _mm(hidden, du3b, mode="tn", out_dtype=BF16, name="g_ffn_out")
    (d_f,) = _mm_fused(du3b, w["w_ffn_out"], mode="nt", name="d_hidden", extras=[f],
                       outs=[(2 * FFN_HIDDEN, BF16)], epilogue=_swiglu_grad_tiles)
    g_ffn_in = _mm(x2b, d_f, mode="tn", out_dtype=BF16, name="g_ffn_in")
    du2, du2b, d_ln2_g, d_ln2_b = ship(
        {"w_ffn_out": g_ffn_out},
        lambda rider: _as_host(rider, _mm_fused(
            d_f, w["w_ffn_in"], mode="nt", name="d_x2", extras=[du3, xhat2, rstd2, small["ln2_g"]], outs=du_outs,
            sums=[col, col], epilogue=_ln_grad_tiles, rider=rider, max_rows=256)))
    g_mem_o = _mm(o_m, du2b, mode="tn", out_dtype=BF16, name="g_mem_o")
    d_om = _mm(du2b, w["w_mem_o"], mode="nt", out_dtype=BF16, name="d_om")
    d_qm, d_kvm = _xattn_bwd(q_m, kv_m, d_om)
    g_mem_q = _mm(x1b, d_qm, mode="tn", out_dtype=BF16, name="g_mem_q")
    g_mem_kv = _mm(memb, d_kvm.astype(BF16), mode="tn", out_dtype=BF16, name="g_mem_kv")
    du1, du1b, d_ln1_g, d_ln1_b = _mm_fused(
        d_qm, w["w_mem_q"], mode="nt", name="d_x1", extras=[du2, xhat1, rstd1, small["ln1_g"]], outs=du_outs,
        sums=[col, col], epilogue=_ln_grad_tiles, max_rows=256)
    g_mix_o = _mm(mix_in, du1b, mode="tn", out_dtype=BF16, name="g_mix_o")
    d_ysb, d_yret, d_e, d_b_gate = _mm_fused(
        du1b, w["w_mix_o"], mode="nt", name="d_mix_in", extras=[h_e, small["b_gate"], y_sb, y_ret],
        outs=[row_bf16, row_bf16, (2 * D_MODEL, BF16)], sums=[2 * D_MODEL], epilogue=_gate_mix_grad_tiles,
        max_rows=256)
    g_sb_o = _mm(a_sb, d_ysb, mode="tn", out_dtype=BF16, name="g_sb_o")
    g_ret_o = _mm(y_gated, d_yret, mode="tn", out_dtype=BF16, name="g_ret_o")
    d_asb = _mm(d_ysb, w["w_sb_o"], mode="nt", out_dtype=BF16, name="d_asb")
    d_ygated = _mm(d_yret, w["w_ret_o"], mode="nt", out_dtype=F32, name="d_ygated")
    small_grads = {"b_gate": d_b_gate, "ln1_g": d_ln1_g, "ln1_b": d_ln1_b, "ln2_g": d_ln2_g, "ln2_b": d_ln2_b,
                   "ln3_g": d_ln3_g, "ln3_b": d_ln3_b}
    d_rq, d_rk, d_c, d_d, _ = _ret_bwd(d_ygated, o_pre, states, h_b, h_c, h_d, tables)
    ready = {"w_ffn_in": g_ffn_in, "w_mem_kv": g_mem_kv, "w_mem_q": g_mem_q, "w_mem_o": g_mem_o, "w_mix_o": g_mix_o,
             "w_ret_o": g_ret_o, "w_sb_o": g_sb_o, "small": small_grads}
    d_q, d_k, d_v = ship(ready, lambda rider: _sb_bwd(h_a, d_asb, r_mat, rider))
    d_h = [("sb_q", d_q), ("sb_k", d_k), ("sb_v", d_v), ("ret_q", d_rq), ("ret_k", d_rk), ("ret_v", d_c),
           ("ret_g", d_d), ("gate", d_e)]
    g_in = jnp.concatenate([_mm(xb, piece, mode="tn", out_dtype=BF16, name="g_in_" + tag) for tag, piece in d_h],
                           axis=1)
    (d_x,) = ship({"w_in": g_in},
                  lambda rider: _as_host(rider, _mm_fused(
                      [piece for _, piece in d_h], w_in, mode="nt", name="d_x", extras=[du1], outs=[(D_MODEL, F32)],
                      epilogue=_residual_tiles, rider=rider, max_rows=256)))
    return loss, d_x


def _adamw_math(w, g, m, v):
    m = ADAM_B1 * m + (1.0 - ADAM_B1) * g
    v = ADAM_B2 * v + (1.0 - ADAM_B2) * jnp.square(g)
    m_hat = m / (1.0 - ADAM_B1 ** ADAM_STEP)
    v_hat = v / (1.0 - ADAM_B2 ** ADAM_STEP)
    delta = -ADAM_LR * (m_hat / (jnp.sqrt(v_hat) + ADAM_EPS) + ADAM_WD * w)
    return delta, m, v


def _adamw(parts, w, m, v, name):
    R, C = w.shape
    tr = max(t for t in range(16, min(R, 256) + 1, 16) if R % t == 0) if R >= 16 else R

    def body(p_ref, w_ref, m_ref, v_ref, g_ref, d_ref, nm_ref, nv_ref):
        g = p_ref[0].astype(F32)
        for j in range(1, N_DEV):
            g = g + p_ref[j].astype(F32)
        delta, nm, nv = _adamw_math(w_ref[...], g, m_ref[...], v_ref[...])
        g_ref[...] = g
        d_ref[...] = delta
        nm_ref[...] = nm
        nv_ref[...] = nv

    blk = pl.BlockSpec((tr, C), lambda i: (i, 0))
    out = jax.ShapeDtypeStruct((R, C), F32)
    return pl.pallas_call(
        body,
        name=name,
        grid=(R // tr,),
        in_specs=[pl.BlockSpec((N_DEV, tr, C), lambda i: (0, i, 0)), blk, blk, blk],
        out_specs=[blk] * 4,
        out_shape=[out] * 4,
        compiler_params=_params(("parallel",)),
    )(parts, w, m, v)


_SHARD_AXIS = {"w_in": 1, "w_sb_o": 1, "w_ret_o": 0, "w_mix_o": 0, "w_mem_q": 0, "w_mem_kv": 1, "w_mem_o": 0,
               "w_ffn_in": 1, "w_ffn_out": 0}
_MATRICES = tuple(_SHARD_AXIS)
_SMALL = ("b_gate", "ln1_g", "ln1_b", "ln2_g", "ln2_b", "ln3_g", "ln3_b")
_WEIGHT_ORDER = ("w_in", "b_gate", "w_sb_o", "w_ret_o", "w_mix_o", "ln1_g", "ln1_b", "w_mem_q", "w_mem_kv", "w_mem_o",
                 "ln2_g", "ln2_b", "w_ffn_in", "w_ffn_out", "ln3_g", "ln3_b")


def _assemble(name, gathered):
    if _SHARD_AXIS[name] == 0:
        return gathered.reshape(-1, gathered.shape[2])
    return jnp.transpose(gathered, (1, 0, 2)).reshape(gathered.shape[1], -1)


def _to_slots(name, full):
    if _SHARD_AXIS[name] == 0:
        return full.reshape(N_DEV, full.shape[0] // N_DEV, full.shape[1])
    return jnp.transpose(full.reshape(full.shape[0], N_DEV, full.shape[1] // N_DEV), (1, 0, 2))


def _pack_small(vals):
    return jnp.concatenate([vals["b_gate"].reshape(2, D_MODEL)] + [vals[n] for n in _SMALL[1:]], axis=0)


def _unpack_small(packed):
    out = {"b_gate": packed[0:2].reshape(1, 2 * D_MODEL)}
    for i, n in enumerate(_SMALL[1:]):
        out[n] = packed[2 + i:3 + i]
    return out


def kernel(x, mem, w_in, b_gate, w_sb_o, w_ret_o, w_mix_o, ln1_g, ln1_b, w_mem_q, w_mem_kv, w_mem_o, ln2_g, ln2_b, w_ffn_in, w_ffn_out, ln3_g, ln3_b, loss_target, m_w_in, m_b_gate, m_w_sb_o, m_w_ret_o, m_w_mix_o, m_ln1_g, m_ln1_b, m_w_mem_q, m_w_mem_kv, m_w_mem_o, m_ln2_g, m_ln2_b, m_w_ffn_in, m_w_ffn_out, m_ln3_g, m_ln3_b, v_w_in, v_b_gate, v_w_sb_o, v_w_ret_o, v_w_mix_o, v_ln1_g, v_ln1_b, v_w_mem_q, v_w_mem_kv, v_w_mem_o, v_ln2_g, v_ln2_b, v_w_ffn_in, v_w_ffn_out, v_ln3_g, v_ln3_b):
    weights = dict(w_in=w_in, b_gate=b_gate, w_sb_o=w_sb_o, w_ret_o=w_ret_o, w_mix_o=w_mix_o, ln1_g=ln1_g, ln1_b=ln1_b,
                   w_mem_q=w_mem_q, w_mem_kv=w_mem_kv, w_mem_o=w_mem_o, ln2_g=ln2_g, ln2_b=ln2_b, w_ffn_in=w_ffn_in,
                   w_ffn_out=w_ffn_out, ln3_g=ln3_g, ln3_b=ln3_b)
    mom1 = dict(w_in=m_w_in, b_gate=m_b_gate, w_sb_o=m_w_sb_o, w_ret_o=m_w_ret_o, w_mix_o=m_w_mix_o, ln1_g=m_ln1_g,
                ln1_b=m_ln1_b, w_mem_q=m_w_mem_q, w_mem_kv=m_w_mem_kv, w_mem_o=m_w_mem_o, ln2_g=m_ln2_g, ln2_b=m_ln2_b,
                w_ffn_in=m_w_ffn_in, w_ffn_out=m_w_ffn_out, ln3_g=m_ln3_g, ln3_b=m_ln3_b)
    mom2 = dict(w_in=v_w_in, b_gate=v_b_gate, w_sb_o=v_w_sb_o, w_ret_o=v_w_ret_o, w_mix_o=v_w_mix_o, ln1_g=v_ln1_g,
                ln1_b=v_ln1_b, w_mem_q=v_w_mem_q, w_mem_kv=v_w_mem_kv, w_mem_o=v_w_mem_o, ln2_g=v_ln2_g, ln2_b=v_ln2_b,
                w_ffn_in=v_w_ffn_in, w_ffn_out=v_w_ffn_out, ln3_g=v_ln3_g, ln3_b=v_ln3_b)

    (gathered_in,) = _exchange([weights["w_in"][0].astype(BF16)], False, "gather_w_in")
    rest = [n for n in _MATRICES if n != "w_in"]
    received = {}

    def fetch_rest(host):
        res = host(_Rider([weights[n][0].astype(BF16) for n in rest], False))
        return res, {n: _assemble(n, g) for n, g in zip(rest, res[-1])}

    def ship(grads, host):
        names = list(grads)
        bufs = []
        for n in names:
            if n == "small":
                part = _pack_small(grads[n])
                bufs.append(jnp.broadcast_to(part[None], (N_DEV,) + part.shape))
            else:
                bufs.append(_to_slots(n, grads[n]).astype(BF16))
        res = host(_Rider(bufs, True))
        received.update(zip(names, res[-1]))
        return res[:-1]

    small = {n: weights[n] for n in _SMALL}
    loss, d_x = _local_step(x[0], mem[0], _assemble("w_in", gathered_in), small, loss_target[0], fetch_rest, ship)

    new = {}
    for n in _MATRICES:
        new[n] = _adamw(received[n], weights[n][0], mom1[n][0], mom2[n][0], "adamw_" + n)
    packed = _adamw(received["small"], _pack_small({n: weights[n] for n in _SMALL}),
                    _pack_small({n: mom1[n] for n in _SMALL}), _pack_small({n: mom2[n] for n in _SMALL}), "adamw_small")
    small_new = [_unpack_small(p) for p in packed]

    outs = [lax.psum(loss, MESH_AXES), d_x[None]]
    for slot in range(4):
        for n in _WEIGHT_ORDER:
            outs.append(new[n][slot][None] if n in new else small_new[slot][n])
    return tuple(outs)
```

```python
import functools
import math

import jax
import jax.numpy as jnp
from jax import lax
from jax.experimental import pallas as pl
from jax.experimental.pallas import tpu as pltpu

F32 = jnp.float32
BF16 = jnp.bfloat16

N_DEV = 8
D_MODEL = 1024
SB_HEAD_DIM = 64
SB_WIDTH = 512
RET_HEADS = 4
RET_QK_DIM = 128
RET_V_DIM = 256
RET_QK_WIDTH = 512
RET_V_WIDTH = 1024
RET_CHUNK = 128
RET_STEP_CHUNKS = 2
ROPE_BASE = 10000.0
MEM_HEADS = 4
MEM_HEAD_DIM = 256
FFN_HIDDEN = 2816
DN_ALPHA = 2.0 ** 0.25
LN_EPS = 1e-5
ADAM_LR = 0.001
ADAM_B1 = 0.9
ADAM_B2 = 0.999
ADAM_EPS = 1e-08
ADAM_WD = 0.01
ADAM_STEP = 10

VMEM_LIMIT_BYTES = 52 * 1024 * 1024
LANES = 128
SB_KEY_BLOCK = 128
SB_Q_BLOCK = 256
SB_DEAD_LOG = -105.0

MESH_AXES = ("x", "y", "c")


def _pick(dim, prefs):
    for p in prefs:
        if dim % p == 0:
            return p
    return dim


def _params(sem):
    return pltpu.CompilerParams(dimension_semantics=sem, vmem_limit_bytes=VMEM_LIMIT_BYTES)


def _dot(a, b, dims):
    return lax.dot_general(a, b, (dims, ((), ())), preferred_element_type=F32)


_NN = ((1,), (0,))
_NT = ((1,), (1,))
_TN = ((0,), (0,))


def _my_index():
    return 4 * lax.axis_index("x") + 2 * lax.axis_index("y") + lax.axis_index("c")


def _peer(k):
    x, y, c = lax.axis_index("x"), lax.axis_index("y"), lax.axis_index("c")
    bx, by, bc = (k >> 2) & 1, (k >> 1) & 1, k & 1
    px = (1 - x) if bx else x
    py = (1 - y) if by else y
    pc = (1 - c) if bc else c
    return (px, py, pc), 4 * px + 2 * py + pc


class _Rider:
    def __init__(self, bufs, scatter):
        self.bufs, self.scatter, self.n = list(bufs), scatter, len(bufs)
        self.specs = [pl.BlockSpec(memory_space=pl.ANY)] * self.n
        self.out_shape = [jax.ShapeDtypeStruct(b.shape if scatter else (N_DEV,) + b.shape, b.dtype) for b in self.bufs]
        self.scratch = [pltpu.SemaphoreType.DMA((self.n, N_DEV - 1)), pltpu.SemaphoreType.DMA((self.n, N_DEV - 1)),
                        pltpu.SemaphoreType.DMA((self.n,))]

    def _remote(self, ride, a, k, src_ref, slot, to):
        _, dst, (send_sems, recv_sems, _) = ride
        return pltpu.make_async_remote_copy(src_ref=src_ref, dst_ref=dst[a].at[slot], send_sem=send_sems.at[a, k],
                                            recv_sem=recv_sems.at[a, k], device_id=to,
                                            device_id_type=pl.DeviceIdType.MESH)

    def _local(self, ride, a):
        src, dst, (_, _, local_sems) = ride
        me = _my_index()
        return pltpu.make_async_copy(src[a].at[me] if self.scatter else src[a], dst[a].at[me], local_sems.at[a])

    def _direct(self, ride, a):
        src = ride[0]
        me = _my_index()
        out = []
        for k in range(1, N_DEV):
            peer, peer_idx = _peer(k)
            out.append(self._remote(ride, a, k - 1, src[a].at[peer_idx], me, peer))
        return out

    def _two_level(self, ride, a):
        src, dst = ride[0], ride[1]
        x, y, c = lax.axis_index("x"), lax.axis_index("y"), lax.axis_index("c")
        me, sibling = _my_index(), (x, y, 1 - c)
        chips = [(1 - x, y), (x, 1 - y), (1 - x, 1 - y)]
        first = [self._remote(ride, a, 0, src[a], me, sibling)]
        passed, landing = [], [self._remote(ride, a, 0, src[a], me + 1 - 2 * c, sibling)]
        for j, (px, py) in enumerate(chips):
            first.append(self._remote(ride, a, 1 + j, src[a], me, (px, py, c)))
            theirs = 4 * px + 2 * py + c
            passed.append(self._remote(ride, a, 4 + j, dst[a].at[theirs], theirs, sibling))
            landing.append(self._remote(ride, a, 1 + j, src[a], theirs, (px, py, c)))
        for j, (px, py) in enumerate(chips):
            landing.append(self._remote(ride, a, 4 + j, src[a], 4 * px + 2 * py + 1 - c, sibling))
        return first, passed, landing

    def start(self, ride):
        for a in range(self.n):
            self._local(ride, a).start()
            for cp in (self._direct(ride, a) if self.scatter else self._two_level(ride, a)[0]):
                cp.start()

    def finish(self, ride):
        if self.scatter:
            for a in range(self.n):
                for cp in self._direct(ride, a):
                    cp.wait()
                self._local(ride, a).wait()
            return
        levels = [self._two_level(ride, a) for a in range(self.n)]
        for first, passed, landing in levels:
            for j, cp in enumerate(passed):
                landing[1 + j].wait_recv()
                cp.start()
        for a, (first, passed, landing) in enumerate(levels):
            landing[0].wait_recv()
            for cp in landing[4:]:
                cp.wait_recv()
            for cp in first + passed:
                cp.wait_send()
            self._local(ride, a).wait()

    def start_at_first(self, ids, ride):
        first = functools.reduce(jnp.logical_and, [i == 0 for i in ids])

        @pl.when(first)
        def _():
            self.start(ride)

    def wait_at_last(self, ids, grid, ride):
        last = functools.reduce(jnp.logical_and, [i == g - 1 for i, g in zip(ids, grid)])

        @pl.when(last)
        def _():
            self.finish(ride)


def _exchange(bufs, scatter, name):
    rider = _Rider(bufs, scatter)

    def body(*refs):
        ride = (refs[:rider.n], refs[rider.n:2 * rider.n], refs[2 * rider.n:])
        rider.start(ride)
        rider.finish(ride)

    return pl.pallas_call(
        body,
        name=name,
        in_specs=rider.specs,
        out_specs=rider.specs,
        out_shape=rider.out_shape,
        scratch_shapes=rider.scratch,
    )(*rider.bufs)


MM_RESIDENT_B_BYTES = 14 * 1024 * 1024
MM_A_TILE_BYTES = 4 * 1024 * 1024
MM_OUT_TILE_BYTES = 6 * 1024 * 1024


def _mm_tiles(mode, M, N, K, a_bytes, out_bytes):
    if mode != "tn" and K * N * 2 <= MM_RESIDENT_B_BYTES:
        for tm in (1024, 512, 256, 128):
            if M % tm == 0 and tm * K * a_bytes <= MM_A_TILE_BYTES and tm * N * out_bytes <= MM_OUT_TILE_BYTES:
                return tm, N, K
    if mode == "tn":
        return (_pick(M, (1024, 1408, 512, 256, 128)), _pick(N, (1024, 1664, 1408, 512, 256, 128)),
                _pick(K, (2048, 1024, 512, 256, 128)))
    return _pick(M, (1024, 512, 256, 128)), _pick(N, (512, 256, 128)), _pick(K, (1024, 512, 256, 128))


def _mm(a, b, *, mode, out_dtype, name, res=None, res_scale=1.0, rider=None):
    if mode == "nn":
        (M, K), (K2, N) = a.shape, b.shape
    elif mode == "nt":
        (M, K), (N, K2) = a.shape, b.shape
    else:
        (K, M), (K2, N) = a.shape, b.shape
    assert K == K2, (a.shape, b.shape, mode)
    out_bytes = jnp.dtype(out_dtype).itemsize + (4 if res is not None else 0)
    tm, tn, tk = _mm_tiles(mode, M, N, K, a.dtype.itemsize, out_bytes)
    grid = (M // tm, N // tn, K // tk)
    nk = grid[2]
    dims = {"nn": _NN, "nt": _NT, "tn": _TN}[mode]
    n_in = 2 + (res is not None)
    n_ride = rider.n if rider is not None else 0

    def body(*refs):
        a_ref, b_ref = refs[:2]
        r_ref = refs[2] if res is not None else None
        o_ref = refs[n_in + n_ride]
        rest = refs[n_in + 2 * n_ride + 1:]
        acc_ref = rest[0] if nk > 1 else None
        ids = [pl.program_id(d) for d in range(3)]
        if rider is not None:
            ride = (refs[n_in:n_in + n_ride], refs[n_in + n_ride + 1:n_in + 2 * n_ride + 1], rest[-3:])
            rider.start_at_first(ids, ride)
        part = _dot(a_ref[...].astype(BF16), b_ref[...].astype(BF16), dims)

        def finish(total):
            if r_ref is not None:
                total = total + res_scale * r_ref[...]
            o_ref[...] = total.astype(out_dtype)

        if nk == 1:
            finish(part)
        else:
            k = ids[2]

            @pl.when(k == 0)
            def _():
                acc_ref[...] = part

            @pl.when(k > 0)
            def _():
                acc_ref[...] += part

            @pl.when(k == nk - 1)
            def _():
                finish(acc_ref[...])

        if rider is not None:
            rider.wait_at_last(ids, grid, ride)

    if mode == "nn":
        a_spec = pl.BlockSpec((tm, tk), lambda i, j, k: (i, k))
        b_spec = pl.BlockSpec((tk, tn), lambda i, j, k: (k, j))
    elif mode == "nt":
        a_spec = pl.BlockSpec((tm, tk), lambda i, j, k: (i, k))
        b_spec = pl.BlockSpec((tn, tk), lambda i, j, k: (j, k))
    else:
        a_spec = pl.BlockSpec((tk, tm), lambda i, j, k: (k, i))
        b_spec = pl.BlockSpec((tk, tn), lambda i, j, k: (k, j))
    o_spec = pl.BlockSpec((tm, tn), lambda i, j, k: (i, j))
    in_specs = [a_spec, b_spec] + ([o_spec] if res is not None else [])
    args = (a, b) + ((res,) if res is not None else ())
    out_specs, out_shape = [o_spec], [jax.ShapeDtypeStruct((M, N), out_dtype)]
    scratch = [pltpu.VMEM((tm, tn), F32)] if nk > 1 else []
    sem = ("parallel", "parallel", "arbitrary")
    if rider is not None:
        in_specs, args = in_specs + rider.specs, args + tuple(rider.bufs)
        out_specs, out_shape = out_specs + rider.specs, out_shape + rider.out_shape
        scratch = scratch + rider.scratch
        sem = ("arbitrary",) * 3
    outs = pl.pallas_call(
        body,
        name=name,
        grid=grid,
        in_specs=in_specs,
        out_specs=out_specs,
        out_shape=out_shape,
        scratch_shapes=scratch,
        compiler_params=_params(sem),
    )(*args)
    return outs[0] if rider is None else (outs[0], list(outs[1:]))


def _mm_host(a, b, *, rider, **kw):
    out = _mm(a, b, rider=rider, **kw)
    return out if rider is not None else (out, [])


def _as_host(rider, results):
    return results if rider is not None else tuple(results) + ([],)


MM_FUSED_MARGIN_BYTES = 10 * 1024 * 1024
MM_FUSED_MAX_ROWS = 512


def _col_sum_update(acc_ref, val, first):
    part = jnp.sum(val.reshape(val.shape[0] // 8, 8, val.shape[1]), axis=0)

    @pl.when(first)
    def _():
        acc_ref[...] = part

    @pl.when(jnp.logical_not(first))
    def _():
        acc_ref[...] += part


def _mm_fused(a, b, *, mode, name, extras, outs, epilogue, sums=(), rider=None, max_rows=MM_FUSED_MAX_ROWS,
              pass_a=False):
    parts = list(a) if isinstance(a, (list, tuple)) else [a]
    M, K = parts[0].shape[0], sum(p.shape[1] for p in parts)
    if mode == "nn":
        (K2, N), b_dims = b.shape, _NN
    else:
        (N, K2), b_dims = b.shape, _NT
    assert K == K2, (K, b.shape, mode)
    rows = parts + [e for e in extras if e.shape[0] == M]
    per_row = 2 * (sum(e.shape[1] * e.dtype.itemsize for e in rows)
                   + sum(c * jnp.dtype(d).itemsize for c, d in outs)) + 2 * N * 4
    budget = VMEM_LIMIT_BYTES - K * N * 2 - MM_FUSED_MARGIN_BYTES
    tm = next(t for t in (512, 256, 128, 64, 32, 16) if t <= max_rows and M % t == 0 and t * per_row <= budget)
    steps = M // tm
    n_a, n_x, n_o, n_s = len(parts), len(extras), len(outs), len(sums)
    n_ride = rider.n if rider is not None else 0

    def body(*refs):
        a_refs, b_ref = refs[:n_a], refs[n_a]
        x_refs = refs[n_a + 1:n_a + 1 + n_x]
        base = n_a + 1 + n_x + n_ride
        o_refs, s_refs = refs[base:base + n_o], refs[base + n_o:base + n_o + n_s]
        acc_refs = refs[base + n_o + n_s + n_ride:base + n_o + 2 * n_s + n_ride]
        ids = [pl.program_id(0)]
        if rider is not None:
            ride = (refs[n_a + 1 + n_x:base], refs[base + n_o + n_s:base + n_o + n_s + n_ride], refs[-3:])
            rider.start_at_first(ids, ride)
        a_tile = a_refs[0][...]
        a_bf16 = a_tile.astype(BF16) if n_a == 1 else jnp.concatenate([r[...].astype(BF16) for r in a_refs], axis=1)
        prod = _dot(a_bf16, b_ref[...], b_dims)
        tiles = epilogue(prod, *([a_tile] if pass_a else []), *[r[...] for r in x_refs])
        for o_ref, t in zip(o_refs, tiles[:n_o]):
            o_ref[...] = t.astype(o_ref.dtype)
        for acc_ref, t in zip(acc_refs, tiles[n_o:]):
            _col_sum_update(acc_ref, t, ids[0] == 0)
        if n_s:
            @pl.when(ids[0] == steps - 1)
            def _():
                for s_ref, acc_ref in zip(s_refs, acc_refs):
                    s_ref[...] = jnp.sum(acc_ref[...], axis=0, keepdims=True)
        if rider is not None:
            rider.wait_at_last(ids, (steps,), ride)

    in_specs = [pl.BlockSpec((tm, p.shape[1]), lambda i: (i, 0)) for p in parts]
    in_specs.append(pl.BlockSpec(b.shape, lambda i: (0, 0), pipeline_mode=pl.Buffered(1)))
    for e in extras:
        in_specs.append(pl.BlockSpec((tm, e.shape[1]), lambda i: (i, 0)) if e.shape[0] == M
                        else pl.BlockSpec(e.shape, lambda i: (0, 0)))
    out_specs = ([pl.BlockSpec((tm, c), lambda i: (i, 0)) for c, _ in outs]
                 + [pl.BlockSpec((1, c), lambda i: (0, 0)) for c in sums])
    out_shape = ([jax.ShapeDtypeStruct((M, c), d) for c, d in outs]
                 + [jax.ShapeDtypeStruct((1, c), F32) for c in sums])
    args = tuple(parts) + (b,) + tuple(extras)
    scratch = [pltpu.VMEM((8, c), F32) for c in sums]
    if rider is not None:
        in_specs, args = in_specs + rider.specs, args + tuple(rider.bufs)
        out_specs, out_shape = out_specs + rider.specs, out_shape + rider.out_shape
        scratch = scratch + rider.scratch
    res = pl.pallas_call(
        body,
        name=name,
        grid=(steps,),
        in_specs=in_specs,
        out_specs=out_specs,
        out_shape=out_shape,
        scratch_shapes=scratch,
        compiler_params=_params(("arbitrary",) if (n_s or rider is not None) else ("parallel",)),
    )(*args)
    return tuple(res[:n_o + n_s]) + ((list(res[n_o + n_s:]),) if rider is not None else ())


def _pair_rows(blk, lane_is_a):
    zero = jnp.zeros_like(blk)
    return jnp.concatenate([jnp.where(lane_is_a, blk, zero), jnp.where(lane_is_a, zero, blk)], axis=0)


SB_STRIP = 32
SB_FWD_PAIRS = 4
SB_BWD_PAIRS = 2


def _pair_lanes(p):
    return slice(p * LANES, (p + 1) * LANES)


def _sb_scan_matrices():
    o = lax.broadcasted_iota(jnp.int32, (2 * LANES, 4 * LANES), 0)
    c = lax.broadcasted_iota(jnp.int32, (2 * LANES, 4 * LANES), 1) & (2 * LANES - 1)
    same = (o >= LANES) == (c >= LANES)
    oo, cc = o & (LANES - 1), c & (LANES - 1)
    return (jnp.where(same & (cc > oo), 1.0, 0.0).astype(BF16), jnp.where(same & (cc < oo), 1.0, 0.0).astype(BF16))


def _sb_causal_masks(tq):
    d = lax.broadcasted_iota(jnp.int32, (tq // SB_KEY_BLOCK, SB_KEY_BLOCK, tq), 0)
    k = lax.broadcasted_iota(jnp.int32, (tq // SB_KEY_BLOCK, SB_KEY_BLOCK, tq), 1)
    t = lax.broadcasted_iota(jnp.int32, (tq // SB_KEY_BLOCK, SB_KEY_BLOCK, tq), 2)
    return jnp.where(d * SB_KEY_BLOCK + k < t, 1.0, 0.0).astype(F32)


def _sb_log_terms(z):
    log_rem = -jnp.maximum(z, 0.0) - jnp.log(1.0 + jnp.exp(-jnp.abs(z)))
    return log_rem, log_rem + z


def _sb_store_split(ref, strip, val):
    hi = val.astype(BF16)
    ref[pl.ds(strip * SB_STRIP, SB_STRIP), :] = hi
    ref[pl.ds(2 * LANES + strip * SB_STRIP, SB_STRIP), :] = (val - hi.astype(F32)).astype(BF16)


def _sb_fwd(h_a, rider=None):
    assert SB_FWD_PAIRS == 4
    T = h_a.shape[0]
    tq = _pick(T, (SB_Q_BLOCK, SB_KEY_BLOCK))
    nq, per_q, nkb = T // tq, tq // SB_KEY_BLOCK, T // SB_KEY_BLOCK
    n_strips = 2 * LANES // SB_STRIP
    n_ride = rider.n if rider is not None else 0
    after_m, _ = _sb_scan_matrices()
    causal_m = _sb_causal_masks(tq)
    pairs = SB_FWD_PAIRS

    def body(*refs):
        q_ref, k_ref, v_ref, after_ref, causal_ref = refs[:5]
        a_ref, r_ref, n_ref = refs[5 + n_ride:8 + n_ride]
        z_ref, lb_ref, split_ref, w_ref = refs[8 + 2 * n_ride:12 + 2 * n_ride]
        ids = [pl.program_id(0)]
        if rider is not None:
            ride = (refs[5:5 + n_ride], refs[8 + n_ride:8 + 2 * n_ride], refs[-3:])
            rider.start_at_first(ids, ride)
        i = ids[0]
        q_t = [(q_ref[:, _pair_lanes(p)].astype(F32).T * (SB_HEAD_DIM ** -0.5)).astype(BF16) for p in range(pairs)]
        lane_is_a = lax.broadcasted_iota(jnp.int32, (SB_KEY_BLOCK, LANES), 1) < SB_HEAD_DIM

        def causal(diag, s):
            return causal_ref[diag, pl.ds((s * SB_STRIP) % SB_KEY_BLOCK, SB_STRIP), :]

        def tile(kb, diag, carry):
            masked = diag is not None
            acc_t, ra, rb = [list(c) for c in carry]
            ks = pl.multiple_of(kb * SB_KEY_BLOCK, SB_KEY_BLOCK)
            vv = []
            for p in range(pairs):
                kk = _pair_rows(k_ref[pl.ds(ks, SB_KEY_BLOCK), _pair_lanes(p)], lane_is_a)
                vv.append(_pair_rows(v_ref[pl.ds(ks, SB_KEY_BLOCK), _pair_lanes(p)], lane_is_a))
                z_ref[p] = _dot(kk, q_t[p], _NN)
            sums = [[jnp.zeros((8, tq), F32), jnp.zeros((8, tq), F32)] for _ in range(pairs)]
            for p in range(pairs):
                for s in range(n_strips):
                    rows = pl.ds(s * SB_STRIP, SB_STRIP)
                    log_rem, log_beta = _sb_log_terms(z_ref[p, rows, :])
                    lb_ref[p, rows, :] = log_beta
                    if masked:
                        log_rem = log_rem * causal(diag, s)
                    _sb_store_split(split_ref.at[p], s, log_rem)
                    head = (s * SB_STRIP) // SB_KEY_BLOCK
                    sums[p][head] = sums[p][head] + jnp.sum(log_rem.reshape(SB_STRIP // 8, 8, tq), axis=0)
            for p in range(pairs):
                z_ref[p] = _dot(after_ref[...], split_ref[p], _NN)
            for p in range(pairs):
                for s in range(n_strips):
                    rows = pl.ds(s * SB_STRIP, SB_STRIP)
                    start = ra[p] if (s * SB_STRIP) < SB_KEY_BLOCK else rb[p]
                    w = jnp.exp(lb_ref[p, rows, :] + z_ref[p, rows, :] + start)
                    if masked:
                        w = w * causal(diag, s)
                    w_ref[p, rows, :] = w.astype(BF16)
            for p in range(pairs):
                acc_t[p] = acc_t[p] + _dot(vv[p], w_ref[p], _TN)
                r_ref[2 * p, kb] = ra[p]
                r_ref[2 * p + 1, kb] = rb[p]
                ra[p] = ra[p] + jnp.sum(sums[p][0], axis=0, keepdims=True)
                rb[p] = rb[p] + jnp.sum(sums[p][1], axis=0, keepdims=True)
            return tuple(acc_t), tuple(ra), tuple(rb)

        carry = (tuple(jnp.zeros((LANES, tq), F32) for _ in range(pairs)),
                 tuple(jnp.zeros((1, tq), F32) for _ in range(pairs)),
                 tuple(jnp.zeros((1, tq), F32) for _ in range(pairs)))
        for d in range(per_q):
            carry = tile(i * per_q + (per_q - 1 - d), per_q - 1 - d, carry)
        n_full = i * per_q

        def alive(c):
            top = functools.reduce(jnp.maximum, c[2] + c[3])
            return jnp.logical_and(c[0] < n_full, jnp.max(top) > SB_DEAD_LOG)

        def step(c):
            return (c[0] + 1,) + tile(n_full - 1 - c[0], None, c[1:])

        walked, acc_t, _, _ = lax.while_loop(alive, step, (jnp.int32(0),) + carry)
        for p in range(pairs):
            a_ref[:, _pair_lanes(p)] = acc_t[p].T.astype(BF16)
        n_ref[...] = jnp.zeros(n_ref.shape, F32) + walked.astype(F32)
        if rider is not None:
            rider.wait_at_last(ids, (nq,), ride)

    wide = pairs * LANES
    in_specs = [pl.BlockSpec((tq, wide), lambda i: (i, 0)),
                pl.BlockSpec((T, wide), lambda i: (0, 1), pipeline_mode=pl.Buffered(1)),
                pl.BlockSpec((T, wide), lambda i: (0, 2), pipeline_mode=pl.Buffered(1)),
                pl.BlockSpec(after_m.shape, lambda i: (0, 0), pipeline_mode=pl.Buffered(1)),
                pl.BlockSpec(causal_m.shape, lambda i: (0, 0, 0), pipeline_mode=pl.Buffered(1))]
    out_specs = [pl.BlockSpec((tq, wide), lambda i: (i, 0)),
                 pl.BlockSpec((2 * pairs, nkb, 1, tq), lambda i: (0, 0, 0, i)),
                 pl.BlockSpec((1, 8, LANES), lambda i: (i, 0, 0))]
    out_shape = [jax.ShapeDtypeStruct((T, SB_WIDTH), BF16), jax.ShapeDtypeStruct((2 * pairs, nkb, 1, T), F32),
                 jax.ShapeDtypeStruct((nq, 8, LANES), F32)]
    args = (h_a, h_a, h_a, after_m, causal_m)
    scratch = [pltpu.VMEM((pairs, 2 * LANES, tq), F32), pltpu.VMEM((pairs, 2 * LANES, tq), F32),
               pltpu.VMEM((pairs, 4 * LANES, tq), BF16), pltpu.VMEM((pairs, 2 * LANES, tq), BF16)]
    if rider is not None:
        in_specs, args = in_specs + rider.specs, args + tuple(rider.bufs)
        out_specs, out_shape = out_specs + rider.specs, out_shape + rider.out_shape
        scratch = scratch + rider.scratch
    outs = pl.pallas_call(
        body,
        name="sb_fwd",
        grid=(nq,),
        in_specs=in_specs,
        out_specs=out_specs,
        out_shape=out_shape,
        scratch_shapes=scratch,
        compiler_params=_params(("arbitrary",)),
    )(*args)
    return outs[0], (outs[1], outs[2]), list(outs[3:])


def _sb_bwd(h_a, d_out, saved, rider=None):
    r_mat, walked_blocks = saved
    T = h_a.shape[0]
    tq = _pick(T, (SB_Q_BLOCK, SB_KEY_BLOCK))
    nq, per_q, nkb = T // tq, tq // SB_KEY_BLOCK, T // SB_KEY_BLOCK
    n_strips = 2 * LANES // SB_STRIP
    after_m, before_m = _sb_scan_matrices()
    causal_m = _sb_causal_masks(tq)
    pairs = SB_BWD_PAIRS
    groups = 4 // pairs
    n_ride = rider.n if rider is not None else 0

    def body(*refs):
        q_ref, k_ref, v_ref, do_ref, r_ref, n_ref, after_ref, before_ref, causal_ref = refs[:9]
        dq_ref, dk_ref, dv_ref = refs[9 + n_ride:12 + n_ride]
        z_ref, lb_ref, split_ref, w_ref, da_ref, dz_ref = refs[12 + 2 * n_ride:18 + 2 * n_ride]
        ids = [pl.program_id(0), pl.program_id(1)]
        if rider is not None:
            ride = (refs[9:9 + n_ride], refs[12 + n_ride:12 + 2 * n_ride], refs[-3:])
            rider.start_at_first(ids, ride)
        i = ids[1]

        @pl.when(i == 0)
        def _():
            dk_ref[...] = jnp.zeros_like(dk_ref)
            dv_ref[...] = jnp.zeros_like(dv_ref)

        scale = SB_HEAD_DIM ** -0.5
        q = [q_ref[:, _pair_lanes(p)] for p in range(pairs)]
        d_o = [do_ref[:, _pair_lanes(p)] for p in range(pairs)]
        q_t = [(x.astype(F32).T * scale).astype(BF16) for x in q]
        do_t = [x.astype(F32).T.astype(BF16) for x in d_o]
        lane_is_a = lax.broadcasted_iota(jnp.int32, (SB_KEY_BLOCK, LANES), 1) < SB_HEAD_DIM

        def causal(diag, s):
            return causal_ref[diag, pl.ds((s * SB_STRIP) % SB_KEY_BLOCK, SB_STRIP), :]

        def tile(kb, diag, carry):
            masked = diag is not None
            dq_t, ca, cb = [list(c) for c in carry]
            ks = pl.multiple_of(kb * SB_KEY_BLOCK, SB_KEY_BLOCK)
            kk, vv = [], []
            for p in range(pairs):
                kk.append(_pair_rows(k_ref[pl.ds(ks, SB_KEY_BLOCK), _pair_lanes(p)], lane_is_a))
                vv.append(_pair_rows(v_ref[pl.ds(ks, SB_KEY_BLOCK), _pair_lanes(p)], lane_is_a))
                z_ref[p] = _dot(kk[p], q_t[p], _NN)
            for p in range(pairs):
                for s in range(n_strips):
                    rows = pl.ds(s * SB_STRIP, SB_STRIP)
                    log_rem, log_beta = _sb_log_terms(z_ref[p, rows, :])
                    lb_ref[p, rows, :] = log_beta
                    if masked:
                        log_rem = log_rem * causal(diag, s)
                    _sb_store_split(split_ref.at[p], s, log_rem)
            for p in range(pairs):
                z_ref[p] = _dot(after_ref[...], split_ref[p], _NN)
                da_ref[p] = _dot(vv[p], do_t[p], _NN)
            sums = [[jnp.zeros((8, tq), F32), jnp.zeros((8, tq), F32)] for _ in range(pairs)]
            for p in range(pairs):
                for s in range(n_strips):
                    rows = pl.ds(s * SB_STRIP, SB_STRIP)
                    start = r_ref[2 * p + (s * SB_STRIP) // SB_KEY_BLOCK, kb]
                    w = jnp.exp(lb_ref[p, rows, :] + z_ref[p, rows, :] + start)
                    if masked:
                        w = w * causal(diag, s)
                    w_ref[p, rows, :] = w.astype(BF16)
                    da = da_ref[p, rows, :] * w
                    da_ref[p, rows, :] = da
                    _sb_store_split(split_ref.at[p], s, da)
                    head = (s * SB_STRIP) // SB_KEY_BLOCK
                    sums[p][head] = sums[p][head] + jnp.sum(da.reshape(SB_STRIP // 8, 8, tq), axis=0)
            for p in range(pairs):
                z_ref[p] = _dot(before_ref[...], split_ref[p], _NN)
            for p in range(pairs):
                for s in range(n_strips):
                    rows = pl.ds(s * SB_STRIP, SB_STRIP)
                    base = ca[p] if (s * SB_STRIP) < SB_KEY_BLOCK else cb[p]
                    sig = jnp.exp(lb_ref[p, rows, :])
                    dz = da_ref[p, rows, :] * (1.0 - sig) - (z_ref[p, rows, :] + base) * sig
                    if masked:
                        dz = dz * causal(diag, s)
                    dz_ref[p, rows, :] = (dz * scale).astype(BF16)
            for p in range(pairs):
                dq_t[p] = dq_t[p] + _dot(kk[p], dz_ref[p], _TN)
                dkk = _dot(dz_ref[p], q[p], _NN)
                dvv = _dot(w_ref[p], d_o[p], _NN)
                here = (pl.ds(ks, SB_KEY_BLOCK), _pair_lanes(p))
                dk_ref[here] += jnp.where(lane_is_a, dkk[:SB_KEY_BLOCK], dkk[SB_KEY_BLOCK:])
                dv_ref[here] += jnp.where(lane_is_a, dvv[:SB_KEY_BLOCK], dvv[SB_KEY_BLOCK:])
                ca[p] = ca[p] + jnp.sum(sums[p][0], axis=0, keepdims=True)
                cb[p] = cb[p] + jnp.sum(sums[p][1], axis=0, keepdims=True)
            return tuple(dq_t), tuple(ca), tuple(cb)

        n_full = i * per_q
        walked = jnp.clip(jnp.max(n_ref[...]).astype(jnp.int32), 0, n_full)
        carry = (tuple(jnp.zeros((LANES, tq), F32) for _ in range(pairs)),
                 tuple(jnp.zeros((1, tq), F32) for _ in range(pairs)),
                 tuple(jnp.zeros((1, tq), F32) for _ in range(pairs)))
        carry = lax.fori_loop(n_full - walked, n_full, lambda j, c: tile(j, None, c), carry)
        for d in range(per_q):
            carry = tile(i * per_q + d, d, carry)
        for p in range(pairs):
            dq_ref[:, _pair_lanes(p)] = carry[0][p].T.astype(BF16)
        if rider is not None:
            rider.wait_at_last(ids, (groups, nq), ride)

    wide = pairs * LANES
    mat = pl.BlockSpec(after_m.shape, lambda g, i: (0, 0), pipeline_mode=pl.Buffered(1))
    in_specs = [pl.BlockSpec((tq, wide), lambda g, i: (i, g)),
                pl.BlockSpec((T, wide), lambda g, i: (0, groups + g), pipeline_mode=pl.Buffered(1)),
                pl.BlockSpec((T, wide), lambda g, i: (0, 2 * groups + g), pipeline_mode=pl.Buffered(1)),
                pl.BlockSpec((tq, wide), lambda g, i: (i, g)),
                pl.BlockSpec((2 * pairs, nkb, 1, tq), lambda g, i: (g, 0, 0, i)),
                pl.BlockSpec((1, 8, LANES), lambda g, i: (i, 0, 0)),
                mat, mat,
                pl.BlockSpec(causal_m.shape, lambda g, i: (0, 0, 0), pipeline_mode=pl.Buffered(1))]
    out_specs = [pl.BlockSpec((tq, wide), lambda g, i: (i, g)),
                 pl.BlockSpec((T, wide), lambda g, i: (0, g)),
                 pl.BlockSpec((T, wide), lambda g, i: (0, g))]
    out_shape = [jax.ShapeDtypeStruct((T, SB_WIDTH), BF16), jax.ShapeDtypeStruct((T, SB_WIDTH), F32),
                 jax.ShapeDtypeStruct((T, SB_WIDTH), F32)]
    args = (h_a, h_a, h_a, d_out, r_mat, walked_blocks, after_m, before_m, causal_m)
    scratch = [pltpu.VMEM((pairs, 2 * LANES, tq), F32), pltpu.VMEM((pairs, 2 * LANES, tq), F32),
               pltpu.VMEM((pairs, 4 * LANES, tq), BF16), pltpu.VMEM((pairs, 2 * LANES, tq), BF16),
               pltpu.VMEM((pairs, 2 * LANES, tq), F32), pltpu.VMEM((pairs, 2 * LANES, tq), BF16)]
    if rider is not None:
        in_specs, args = in_specs + rider.specs, args + tuple(rider.bufs)
        out_specs, out_shape = out_specs + rider.specs, out_shape + rider.out_shape
        scratch = scratch + rider.scratch
    outs = pl.pallas_call(
        body,
        name="sb_bwd",
        grid=(groups, nq),
        in_specs=in_specs,
        out_specs=out_specs,
        out_shape=out_shape,
        scratch_shapes=scratch,
        compiler_params=_params(("arbitrary", "arbitrary") if rider is not None else ("parallel", "arbitrary")),
    )(*args)
    return outs[0], outs[1], outs[2], list(outs[3:])


def _ret_tables(T):
    half = RET_QK_DIM // 2
    inv = 1.0 / (ROPE_BASE ** (jnp.arange(half, dtype=F32) / half))
    ang = jnp.arange(T, dtype=F32)[:, None] * inv[None, :]
    cos, sin = jnp.cos(ang), jnp.sin(ang)
    cos_t = jnp.concatenate([cos, cos], axis=1)
    sin_t = jnp.concatenate([-sin, sin], axis=1)
    log_gamma = jnp.log1p(-jnp.exp2(-5.0 - jnp.arange(RET_HEADS, dtype=F32)))
    idx = jnp.arange(RET_CHUNK, dtype=F32)
    rel = idx[:, None] - idx[None, :]
    decay = jnp.where(rel[None] >= 0, jnp.exp(log_gamma[:, None, None] * jnp.maximum(rel, 0.0)[None]), 0.0)
    k_decay = jnp.exp(log_gamma[None, :] * (RET_CHUNK - 1.0 - idx)[:, None])
    q_decay = jnp.exp(log_gamma[None, :] * (idx + 1.0)[:, None])
    chunk_decay = jnp.exp(log_gamma * RET_CHUNK)
    k_dec = jnp.broadcast_to(k_decay.T[:, :, None], (RET_HEADS, RET_CHUNK, LANES))
    q_dec = jnp.broadcast_to(q_decay.T[:, :, None], (RET_HEADS, RET_CHUNK, LANES))
    c_dec = jnp.broadcast_to(chunk_decay[:, None, None], (RET_HEADS, 8, LANES))
    return cos_t, sin_t, decay, k_dec, q_dec, c_dec


def _rotary(x, cos_t, sin_t):
    return x * cos_t + pltpu.roll(x, RET_QK_DIM // 2, 1) * sin_t


def _rotary_transpose(dy, cos_t, sin_t):
    return dy * cos_t + pltpu.roll(dy * sin_t, RET_QK_DIM // 2, 1)


def _head_norm(o):
    mu = jnp.mean(o, axis=1, keepdims=True)
    cen = o - mu
    var = jnp.mean(cen * cen, axis=1, keepdims=True)
    rstd = lax.rsqrt(var + LN_EPS)
    return cen * rstd, rstd


def _ret_specs(steps, per_step, reverse):
    def n_of(n):
        return (steps - 1 - n) if reverse else n

    rows = per_step * RET_CHUNK
    q_spec = pl.BlockSpec((rows, RET_QK_WIDTH), lambda n: (n_of(n), 0))
    k_spec = pl.BlockSpec((rows, RET_QK_WIDTH), lambda n: (n_of(n), 1))
    vv = pl.BlockSpec((rows, RET_V_WIDTH), lambda n: (n_of(n), 0))
    pos = pl.BlockSpec((rows, LANES), lambda n: (n_of(n), 0))
    per_head = pl.BlockSpec((RET_HEADS, RET_CHUNK, LANES), lambda n: (0, 0, 0))
    c_dec = pl.BlockSpec((RET_HEADS, 8, LANES), lambda n: (0, 0, 0))
    state = pl.BlockSpec((RET_HEADS, per_step, RET_QK_DIM, RET_V_DIM), lambda n: (0, n_of(n), 0, 0))
    return q_spec, k_spec, vv, pos, per_head, c_dec, state


def _qk_cols(h):
    return slice(h * RET_QK_DIM, (h + 1) * RET_QK_DIM)


def _v_cols(h):
    return slice(h * RET_V_DIM, (h + 1) * RET_V_DIM)


def _ret_fwd(h_b, h_c, h_d, tables):
    T = h_b.shape[0]
    nc = T // RET_CHUNK
    per_step = _pick(nc, (RET_STEP_CHUNKS, 1))
    steps = nc // per_step
    q_spec, k_spec, vv, pos, per_head, c_dec, state = _ret_specs(steps, per_step, False)

    def body(q_ref, k_ref, v_ref, g_ref, cos_ref, sin_ref, dec_ref, kd_ref, qd_ref, cd_ref,
             y_ref, o_ref, st_ref, state_ref):
        @pl.when(pl.program_id(0) == 0)
        def _():
            state_ref[...] = jnp.zeros_like(state_ref)

        for c in range(per_step):
            rows = pl.ds(c * RET_CHUNK, RET_CHUNK)
            cos_t, sin_t = cos_ref[rows, :], sin_ref[rows, :]
            for h in range(RET_HEADS):
                q = _rotary(q_ref[rows, _qk_cols(h)], cos_t, sin_t) * (RET_QK_DIM ** -0.5)
                k = _rotary(k_ref[rows, _qk_cols(h)], cos_t, sin_t)
                v = v_ref[rows, _v_cols(h)]
                prev = state_ref[h]
                scores = _dot(q.astype(BF16), k.astype(BF16), _NT) * dec_ref[h]
                inner = _dot(scores.astype(BF16), v, _NN)
                cross = _dot((q * qd_ref[h]).astype(BF16), prev.astype(BF16), _NN)
                o = inner + cross
                st_ref[h, c] = prev
                kv = _dot((k * kd_ref[h]).astype(BF16), v, _TN)
                state_ref[h] = prev * cd_ref[h, 0:1, 0:1] + kv
                o_ref[rows, _v_cols(h)] = o
                normed, _ = _head_norm(o)
                gate = g_ref[rows, _v_cols(h)]
                y_ref[rows, _v_cols(h)] = (gate * jax.nn.sigmoid(gate) * normed).astype(BF16)

    return pl.pallas_call(
        body,
        name="ret_fwd",
        grid=(steps,),
        in_specs=[q_spec, k_spec, vv, vv, pos, pos, per_head, per_head, per_head, c_dec],
        out_specs=[vv, vv, state],
        out_shape=[jax.ShapeDtypeStruct((T, RET_V_WIDTH), BF16),
                   jax.ShapeDtypeStruct((T, RET_V_WIDTH), F32),
                   jax.ShapeDtypeStruct((RET_HEADS, nc, RET_QK_DIM, RET_V_DIM), F32)],
        scratch_shapes=[pltpu.VMEM((RET_HEADS, RET_QK_DIM, RET_V_DIM), F32)],
        compiler_params=_params(("arbitrary",)),
    )(h_b, h_b, h_c, h_d, *tables)


def _ret_bwd(d_y, o_pre, states, h_b, h_c, h_d, tables, rider=None):
    T = h_b.shape[0]
    nc = T // RET_CHUNK
    per_step = _pick(nc, (RET_STEP_CHUNKS, 1))
    steps = nc // per_step
    q_spec, k_spec, vv, pos, per_head, c_dec, state = _ret_specs(steps, per_step, True)
    n_ride = rider.n if rider is not None else 0

    def body(*refs):
        (dy_ref, o_ref, st_ref, q_ref, k_ref, v_ref, g_ref, cos_ref, sin_ref, dec_ref, kd_ref, qd_ref,
         cd_ref) = refs[:13]
        dq_ref, dk_ref, dv_ref, dg_ref = refs[13 + n_ride:17 + n_ride]
        carry_ref = refs[17 + 2 * n_ride]
        ids = [pl.program_id(0)]
        if rider is not None:
            ride = (refs[13:13 + n_ride], refs[17 + n_ride:17 + 2 * n_ride], refs[-3:])
            rider.start_at_first(ids, ride)

        @pl.when(ids[0] == 0)
        def _():
            carry_ref[...] = jnp.zeros_like(carry_ref)

        scale = RET_QK_DIM ** -0.5
        for c in reversed(range(per_step)):
            rows = pl.ds(c * RET_CHUNK, RET_CHUNK)
            cos_t, sin_t = cos_ref[rows, :], sin_ref[rows, :]
            for h in range(RET_HEADS):
                q = _rotary(q_ref[rows, _qk_cols(h)], cos_t, sin_t) * scale
                k = _rotary(k_ref[rows, _qk_cols(h)], cos_t, sin_t)
                v = v_ref[rows, _v_cols(h)]
                decay, k_dec, q_dec = dec_ref[h], kd_ref[h], qd_ref[h]
                chunk_decay = cd_ref[h, 0:1, 0:1]
                state = st_ref[h, c].astype(BF16)
                later = carry_ref[h]
                later_b = later.astype(BF16)

                gate = g_ref[rows, _v_cols(h)]
                sig = jax.nn.sigmoid(gate)
                silu = gate * sig
                normed, rstd = _head_norm(o_ref[rows, _v_cols(h)])
                d_y = dy_ref[rows, _v_cols(h)]
                dg_ref[rows, _v_cols(h)] = (d_y * normed * (sig * (1.0 + gate * (1.0 - sig)))).astype(BF16)
                d_n = d_y * silu
                d_o = rstd * (d_n - jnp.mean(d_n, axis=1, keepdims=True)
                              - normed * jnp.mean(d_n * normed, axis=1, keepdims=True))
                d_ob = d_o.astype(BF16)

                qb, kb = q.astype(BF16), k.astype(BF16)
                qd_b, kd_b = (q * q_dec).astype(BF16), (k * k_dec).astype(BF16)
                scores = _dot(qb, kb, _NT) * decay
                d_scores = (_dot(d_ob, v, _NT) * decay).astype(BF16)
                dq = _dot(d_scores, kb, _NN) + _dot(d_ob, state, _NT) * q_dec
                dk = _dot(d_scores, qb, _TN) + _dot(v, later_b, _NT) * k_dec
                dv = _dot(scores.astype(BF16), d_ob, _TN) + _dot(kd_b, later_b, _NN)
                carry_ref[h] = _dot(qd_b, d_ob, _TN) + chunk_decay * later
                dq_ref[rows, _qk_cols(h)] = _rotary_transpose(dq * scale, cos_t, sin_t).astype(BF16)
                dk_ref[rows, _qk_cols(h)] = _rotary_transpose(dk, cos_t, sin_t).astype(BF16)
                dv_ref[rows, _v_cols(h)] = dv.astype(BF16)
        if rider is not None:
            rider.wait_at_last(ids, (steps,), ride)

    qk_out = pl.BlockSpec((per_step * RET_CHUNK, RET_QK_WIDTH), lambda n: (steps - 1 - n, 0))
    in_specs = [vv, vv, state, q_spec, k_spec, vv, vv, pos, pos, per_head, per_head, per_head, c_dec]
    out_specs = [qk_out, qk_out, vv, vv]
    out_shape = [jax.ShapeDtypeStruct((T, RET_QK_WIDTH), BF16), jax.ShapeDtypeStruct((T, RET_QK_WIDTH), BF16),
                 jax.ShapeDtypeStruct((T, RET_V_WIDTH), BF16), jax.ShapeDtypeStruct((T, RET_V_WIDTH), BF16)]
    args = (d_y, o_pre, states, h_b, h_b, h_c, h_d) + tuple(tables)
    scratch = [pltpu.VMEM((RET_HEADS, RET_QK_DIM, RET_V_DIM), F32)]
    if rider is not None:
        in_specs, args = in_specs + rider.specs, args + tuple(rider.bufs)
        out_specs, out_shape = out_specs + rider.specs, out_shape + rider.out_shape
        scratch = scratch + rider.scratch
    outs = pl.pallas_call(
        body,
        name="ret_bwd",
        grid=(steps,),
        in_specs=in_specs,
        out_specs=out_specs,
        out_shape=out_shape,
        scratch_shapes=scratch,
        compiler_params=_params(("arbitrary",)),
    )(*args)
    return outs[0], outs[1], outs[2], outs[3], list(outs[4:])


def _proj_tiles(h, x):
    return h[:, 0:1536], h[:, 1536:2560], h[:, 2560:3584], h[:, 3584:4608], h[:, 4608:6656], x


def _gate_mix_tiles(y_ret, h_e, b_gate, y_sb):
    gates = jax.nn.sigmoid(h_e + b_gate)
    return y_ret, gates[:, :D_MODEL] * y_sb + gates[:, D_MODEL:] * y_ret


def _gate_mix_grad_tiles(d_mix, h_e, b_gate, y_sb, y_ret):
    gates = jax.nn.sigmoid(h_e + b_gate)
    g0, g1 = gates[:, :D_MODEL], gates[:, D_MODEL:]
    d_e = jnp.concatenate([d_mix * y_sb * g0 * (1.0 - g0), d_mix * y_ret * g1 * (1.0 - g1)], axis=1)
    return d_mix * g0, d_mix * g1, d_e, d_e


def _ln_stats(u):
    mu = jnp.mean(u, axis=1, keepdims=True)
    cen = u - mu
    var = jnp.mean(cen * cen, axis=1, keepdims=True)
    rstd = lax.rsqrt(var + LN_EPS)
    return cen * rstd, rstd


def _ln_input_grad(d_out, gain, xhat, rstd):
    d_hat = d_out * gain
    return rstd * (d_hat - jnp.mean(d_hat, axis=1, keepdims=True)
                   - xhat * jnp.mean(d_hat * xhat, axis=1, keepdims=True))


def _ln_tiles(sub, x_prev, gain, bias):
    xhat, rstd = _ln_stats(DN_ALPHA * x_prev + sub)
    out = xhat * gain + bias
    return out, out, xhat, rstd


def _residual_tiles(d_sub, res):
    return (d_sub + DN_ALPHA * res,)


def _ln_grad_tiles(d_sub, res, xhat, rstd, gain):
    d_out = d_sub + DN_ALPHA * res
    du = _ln_input_grad(d_out, gain, xhat, rstd)
    return du, du, d_out * xhat, d_out


def _ln_loss_tiles(sub, x_prev, gain, bias, target):
    xhat, rstd = _ln_stats(DN_ALPHA * x_prev + sub)
    diff = xhat * gain + bias - target
    d_out = diff * (1.0 / D_MODEL)
    du = _ln_input_grad(d_out, gain, xhat, rstd)
    return du, du, diff * diff, d_out * xhat, d_out


def _mem_probs(q_h, k_h):
    s = _dot(q_h, k_h, _NT) * (MEM_HEAD_DIM ** -0.5)
    e = jnp.exp(s - jnp.max(s, axis=1, keepdims=True))
    return e / jnp.sum(e, axis=1, keepdims=True)


def _xattn_fwd(q, kv):
    T, mem_len = q.shape[0], kv.shape[0]
    tq = _pick(T, (512, 256, 128))

    def body(q_ref, kv_ref, o_ref):
        for h in range(MEM_HEADS):
            cols = slice(h * MEM_HEAD_DIM, (h + 1) * MEM_HEAD_DIM)
            vcols = slice(D_MODEL + h * MEM_HEAD_DIM, D_MODEL + (h + 1) * MEM_HEAD_DIM)
            p = _mem_probs(q_ref[:, cols], kv_ref[:, cols])
            o_ref[:, cols] = _dot(p.astype(BF16), kv_ref[:, vcols], _NN).astype(BF16)

    return pl.pallas_call(
        body,
        name="xattn_fwd",
        grid=(T // tq,),
        in_specs=[pl.BlockSpec((tq, D_MODEL), lambda i: (i, 0)),
                  pl.BlockSpec((mem_len, 2 * D_MODEL), lambda i: (0, 0))],
        out_specs=pl.BlockSpec((tq, D_MODEL), lambda i: (i, 0)),
        out_shape=jax.ShapeDtypeStruct((T, D_MODEL), BF16),
        compiler_params=_params(("parallel",)),
    )(q, kv)


def _xattn_bwd(q, kv, d_o):
    T, mem_len = q.shape[0], kv.shape[0]
    tq = _pick(T, (512, 256, 128))

    def body(q_ref, kv_ref, do_ref, dq_ref, dkv_ref):
        @pl.when(pl.program_id(0) == 0)
        def _():
            dkv_ref[...] = jnp.zeros_like(dkv_ref)

        for h in range(MEM_HEADS):
            cols = slice(h * MEM_HEAD_DIM, (h + 1) * MEM_HEAD_DIM)
            vcols = slice(D_MODEL + h * MEM_HEAD_DIM, D_MODEL + (h + 1) * MEM_HEAD_DIM)
            q_h, k_h, do_h = q_ref[:, cols], kv_ref[:, cols], do_ref[:, cols]
            p = _mem_probs(q_h, k_h)
            dp = _dot(do_h, kv_ref[:, vcols], _NT)
            ds = p * (dp - jnp.sum(dp * p, axis=1, keepdims=True))
            dsb = (ds * (MEM_HEAD_DIM ** -0.5)).astype(BF16)
            dq_ref[:, cols] = _dot(dsb, k_h, _NN).astype(BF16)
            dkv_ref[:, cols] += _dot(dsb, q_h, _TN)
            dkv_ref[:, vcols] += _dot(p.astype(BF16), do_h, _TN)

    row = pl.BlockSpec((tq, D_MODEL), lambda i: (i, 0))
    full = pl.BlockSpec((mem_len, 2 * D_MODEL), lambda i: (0, 0))
    return pl.pallas_call(
        body,
        name="xattn_bwd",
        grid=(T // tq,),
        in_specs=[row, full, row],
        out_specs=[row, full],
        out_shape=[jax.ShapeDtypeStruct((T, D_MODEL), BF16), jax.ShapeDtypeStruct((mem_len, 2 * D_MODEL), F32)],
        compiler_params=_params(("arbitrary",)),
    )(q, kv, d_o)


def _swiglu_tiles(f):
    a, b = f[:, :FFN_HIDDEN], f[:, FFN_HIDDEN:]
    return f, a * jax.nn.sigmoid(a) * b


def _swiglu_grad_tiles(d_hidden, f):
    a, b = f[:, :FFN_HIDDEN], f[:, FFN_HIDDEN:]
    sig = jax.nn.sigmoid(a)
    return (jnp.concatenate([d_hidden * b * (sig * (1.0 + a * (1.0 - sig))), d_hidden * (a * sig)], axis=1),)


def _local_step(x, mem, w_in, small, target, fetch_rest, ship):
    T = x.shape[0]
    tables = _ret_tables(T)
    memb = mem.astype(BF16)

    h_a, h_b, h_c, h_d, h_e, xb = _mm_fused(
        x, w_in, mode="nn", name="proj_in", extras=[], pass_a=True,
        outs=[(1536, BF16), (1024, F32), (1024, BF16), (1024, F32), (2048, F32), (D_MODEL, BF16)],
        epilogue=_proj_tiles, max_rows=256)
    (a_sb, r_mat, _), w = fetch_rest(lambda rider: _sb_fwd(h_a, rider))
    y_gated, o_pre, states = _ret_fwd(h_b, h_c, h_d, tables)
    y_sb = _mm(a_sb, w["w_sb_o"], mode="nn", out_dtype=F32, name="sb_out")
    row_f32, row_bf16 = (D_MODEL, F32), (D_MODEL, BF16)
    ln_outs = [row_f32, row_bf16, row_f32, (1, F32)]
    y_ret, mix_in = _mm_fused(y_gated, w["w_ret_o"], mode="nn", name="ret_out", extras=[h_e, small["b_gate"], y_sb],
                              outs=[row_f32, row_bf16], epilogue=_gate_mix_tiles)
    x1, x1b, xhat1, rstd1 = _mm_fused(mix_in, w["w_mix_o"], mode="nn", name="mix_out",
                                      extras=[x, small["ln1_g"], small["ln1_b"]], outs=ln_outs, epilogue=_ln_tiles)
    q_m = _mm(x1b, w["w_mem_q"], mode="nn", out_dtype=BF16, name="mem_q")
    kv_m = _mm(memb, w["w_mem_kv"], mode="nn", out_dtype=BF16, name="mem_kv")
    o_m = _xattn_fwd(q_m, kv_m)
    x2, x2b, xhat2, rstd2 = _mm_fused(o_m, w["w_mem_o"], mode="nn", name="mem_out",
                                      extras=[x1, small["ln2_g"], small["ln2_b"]], outs=ln_outs, epilogue=_ln_tiles)
    f, hidden = _mm_fused(x2b, w["w_ffn_in"], mode="nn", name="ffn_in", extras=[],
                          outs=[(2 * FFN_HIDDEN, F32), (FFN_HIDDEN, BF16)], epilogue=_swiglu_tiles)
    du_outs, col = [row_f32, row_bf16], D_MODEL
    du3, du3b, loss_cols, d_ln3_g, d_ln3_b = _mm_fused(
        hidden, w["w_ffn_out"], mode="nn", name="ffn_out", extras=[x2, small["ln3_g"], small["ln3_b"], target],
        outs=du_outs, sums=[col, col, col], epilogue=_ln_loss_tiles, max_rows=256)

    g_ffn_out = _mm(hidden, du3b, mode="tn", out_dtype=BF16, name="g_ffn_out")
    (d_f,) = _mm_fused(du3b, w["w_ffn_out"], mode="nt", name="d_hidden", extras=[f],
                       outs=[(2 * FFN_HIDDEN, BF16)], epilogue=_swiglu_grad_tiles)
    g_ffn_in = _mm(x2b, d_f, mode="tn", out_dtype=BF16, name="g_ffn_in")
    du2, du2b, d_ln2_g, d_ln2_b = ship(
        {"w_ffn_out": g_ffn_out},
        lambda rider: _as_host(rider, _mm_fused(
            d_f, w["w_ffn_in"], mode="nt", name="d_x2", extras=[du3, xhat2, rstd2, small["ln2_g"]], outs=du_outs,
            sums=[col, col], epilogue=_ln_grad_tiles, rider=rider, max_rows=256)))
    g_mem_o = _mm(o_m, du2b, mode="tn", out_dtype=BF16, name="g_mem_o")
    d_om = _mm(du2b, w["w_mem_o"], mode="nt", out_dtype=BF16, name="d_om")
    d_qm, d_kvm = _xattn_bwd(q_m, kv_m, d_om)
    g_mem_q = _mm(x1b, d_qm, mode="tn", out_dtype=BF16, name="g_mem_q")
    g_mem_kv = _mm(memb, d_kvm.astype(BF16), mode="tn", out_dtype=BF16, name="g_mem_kv")
    du1, du1b, d_ln1_g, d_ln1_b = _mm_fused(
        d_qm, w["w_mem_q"], mode="nt", name="d_x1", extras=[du2, xhat1, rstd1, small["ln1_g"]], outs=du_outs,
        sums=[col, col], epilogue=_ln_grad_tiles, max_rows=256)
    g_mix_o = _mm(mix_in, du1b, mode="tn", out_dtype=BF16, name="g_mix_o")
    d_ysb, d_yret, d_e, d_b_gate = _mm_fused(
        du1b, w["w_mix_o"], mode="nt", name="d_mix_in", extras=[h_e, small["b_gate"], y_sb, y_ret],
        outs=[row_bf16, row_bf16, (2 * D_MODEL, BF16)], sums=[2 * D_MODEL], epilogue=_gate_mix_grad_tiles,
        max_rows=256)
    g_sb_o = _mm(a_sb, d_ysb, mode="tn", out_dtype=BF16, name="g_sb_o")
    g_ret_o = _mm(y_gated, d_yret, mode="tn", out_dtype=BF16, name="g_ret_o")
    d_asb = _mm(d_ysb, w["w_sb_o"], mode="nt", out_dtype=BF16, name="d_asb")
    d_ygated = _mm(d_yret, w["w_ret_o"], mode="nt", out_dtype=F32, name="d_ygated")
    small_grads = {"b_gate": d_b_gate, "ln1_g": d_ln1_g, "ln1_b": d_ln1_b, "ln2_g": d_ln2_g, "ln2_b": d_ln2_b,
                   "ln3_g": d_ln3_g, "ln3_b": d_ln3_b, "loss_cols": loss_cols}
    d_rq, d_rk, d_c, d_d, _ = _ret_bwd(d_ygated, o_pre, states, h_b, h_c, h_d, tables)
    ready = {"w_ffn_in": g_ffn_in, "w_mem_kv": g_mem_kv, "w_mem_q": g_mem_q, "w_mem_o": g_mem_o, "w_mix_o": g_mix_o,
             "w_ret_o": g_ret_o, "w_sb_o": g_sb_o, "small": small_grads}
    d_q, d_k, d_v = ship(ready, lambda rider: _sb_bwd(h_a, d_asb, r_mat, rider))
    d_h = [("sb_q", d_q), ("sb_k", d_k), ("sb_v", d_v), ("ret_q", d_rq), ("ret_k", d_rk), ("ret_v", d_c),
           ("ret_g", d_d), ("gate", d_e)]
    g_in = jnp.concatenate([_mm(xb, piece, mode="tn", out_dtype=BF16, name="g_in_" + tag) for tag, piece in d_h],
                           axis=1)
    (d_x,) = ship({"w_in": g_in},
                  lambda rider: _as_host(rider, _mm_fused(
                      [piece for _, piece in d_h], w_in, mode="nt", name="d_x", extras=[du1], outs=[(D_MODEL, F32)],
                      epilogue=_residual_tiles, rider=rider, max_rows=256)))
    return d_x


def _adamw_math(w, g, m, v):
    m = ADAM_B1 * m + (1.0 - ADAM_B1) * g
    v = ADAM_B2 * v + (1.0 - ADAM_B2) * jnp.square(g)
    m_hat = m / (1.0 - ADAM_B1 ** ADAM_STEP)
    v_hat = v / (1.0 - ADAM_B2 ** ADAM_STEP)
    delta = -ADAM_LR * (m_hat / (jnp.sqrt(v_hat) + ADAM_EPS) + ADAM_WD * w)
    return delta, m, v


def _adamw(parts, w, m, v, name):
    R, C = w.shape
    tr = max(t for t in range(16, min(R, 256) + 1, 16) if R % t == 0) if R >= 16 else R

    def body(p_ref, w_ref, m_ref, v_ref, g_ref, d_ref, nm_ref, nv_ref):
        g = p_ref[0].astype(F32)
        for j in range(1, N_DEV):
            g = g + p_ref[j].astype(F32)
        delta, nm, nv = _adamw_math(w_ref[...], g, m_ref[...], v_ref[...])
        g_ref[...] = g
        d_ref[...] = delta
        nm_ref[...] = nm
        nv_ref[...] = nv

    blk = pl.BlockSpec((tr, C), lambda i: (i, 0))
    out = jax.ShapeDtypeStruct((R, C), F32)
    return pl.pallas_call(
        body,
        name=name,
        grid=(R // tr,),
        in_specs=[pl.BlockSpec((N_DEV, tr, C), lambda i: (0, i, 0)), blk, blk, blk],
        out_specs=[blk] * 4,
        out_shape=[out] * 4,
        compiler_params=_params(("parallel",)),
    )(parts, w, m, v)


_SHARD_AXIS = {"w_in": 1, "w_sb_o": 1, "w_ret_o": 0, "w_mix_o": 0, "w_mem_q": 0, "w_mem_kv": 1, "w_mem_o": 0,
               "w_ffn_in": 1, "w_ffn_out": 0}
_MATRICES = tuple(_SHARD_AXIS)
_SMALL = ("b_gate", "ln1_g", "ln1_b", "ln2_g", "ln2_b", "ln3_g", "ln3_b")
_WEIGHT_ORDER = ("w_in", "b_gate", "w_sb_o", "w_ret_o", "w_mix_o", "ln1_g", "ln1_b", "w_mem_q", "w_mem_kv", "w_mem_o",
                 "ln2_g", "ln2_b", "w_ffn_in", "w_ffn_out", "ln3_g", "ln3_b")


def _assemble(name, gathered):
    if _SHARD_AXIS[name] == 0:
        return gathered.reshape(-1, gathered.shape[2])
    return jnp.transpose(gathered, (1, 0, 2)).reshape(gathered.shape[1], -1)


def _to_slots(name, full):
    if _SHARD_AXIS[name] == 0:
        return full.reshape(N_DEV, full.shape[0] // N_DEV, full.shape[1])
    return jnp.transpose(full.reshape(full.shape[0], N_DEV, full.shape[1] // N_DEV), (1, 0, 2))


SMALL_ROWS = 16


def _pack_small(vals):
    return jnp.concatenate([vals["b_gate"].reshape(2, D_MODEL)] + [vals[n] for n in _SMALL[1:]], axis=0)


def _unpack_small(packed):
    out = {"b_gate": packed[0:2].reshape(1, 2 * D_MODEL)}
    for i, n in enumerate(_SMALL[1:]):
        out[n] = packed[2 + i:3 + i]
    return out


def kernel(x, mem, w_in, b_gate, w_sb_o, w_ret_o, w_mix_o, ln1_g, ln1_b, w_mem_q, w_mem_kv, w_mem_o, ln2_g, ln2_b, w_ffn_in, w_ffn_out, ln3_g, ln3_b, loss_target, m_w_in, m_b_gate, m_w_sb_o, m_w_ret_o, m_w_mix_o, m_ln1_g, m_ln1_b, m_w_mem_q, m_w_mem_kv, m_w_mem_o, m_ln2_g, m_ln2_b, m_w_ffn_in, m_w_ffn_out, m_ln3_g, m_ln3_b, v_w_in, v_b_gate, v_w_sb_o, v_w_ret_o, v_w_mix_o, v_ln1_g, v_ln1_b, v_w_mem_q, v_w_mem_kv, v_w_mem_o, v_ln2_g, v_ln2_b, v_w_ffn_in, v_w_ffn_out, v_ln3_g, v_ln3_b):
    weights = dict(w_in=w_in, b_gate=b_gate, w_sb_o=w_sb_o, w_ret_o=w_ret_o, w_mix_o=w_mix_o, ln1_g=ln1_g, ln1_b=ln1_b,
                   w_mem_q=w_mem_q, w_mem_kv=w_mem_kv, w_mem_o=w_mem_o, ln2_g=ln2_g, ln2_b=ln2_b, w_ffn_in=w_ffn_in,
                   w_ffn_out=w_ffn_out, ln3_g=ln3_g, ln3_b=ln3_b)
    mom1 = dict(w_in=m_w_in, b_gate=m_b_gate, w_sb_o=m_w_sb_o, w_ret_o=m_w_ret_o, w_mix_o=m_w_mix_o, ln1_g=m_ln1_g,
                ln1_b=m_ln1_b, w_mem_q=m_w_mem_q, w_mem_kv=m_w_mem_kv, w_mem_o=m_w_mem_o, ln2_g=m_ln2_g, ln2_b=m_ln2_b,
                w_ffn_in=m_w_ffn_in, w_ffn_out=m_w_ffn_out, ln3_g=m_ln3_g, ln3_b=m_ln3_b)
    mom2 = dict(w_in=v_w_in, b_gate=v_b_gate, w_sb_o=v_w_sb_o, w_ret_o=v_w_ret_o, w_mix_o=v_w_mix_o, ln1_g=v_ln1_g,
                ln1_b=v_ln1_b, w_mem_q=v_w_mem_q, w_mem_kv=v_w_mem_kv, w_mem_o=v_w_mem_o, ln2_g=v_ln2_g, ln2_b=v_ln2_b,
                w_ffn_in=v_w_ffn_in, w_ffn_out=v_w_ffn_out, ln3_g=v_ln3_g, ln3_b=v_ln3_b)

    (gathered_in,) = _exchange([weights["w_in"][0].astype(BF16)], False, "gather_w_in")
    rest = [n for n in _MATRICES if n != "w_in"]
    received = {}

    def fetch_rest(host):
        res = host(_Rider([weights[n][0].astype(BF16) for n in rest], False))
        return res, {n: _assemble(n, g) for n, g in zip(rest, res[-1])}

    def ship(grads, host):
        names = list(grads)
        bufs = []
        for n in names:
            if n == "small":
                part = jnp.concatenate([_pack_small(grads[n]), grads[n]["loss_cols"],
                                        jnp.zeros((SMALL_ROWS - 9, D_MODEL), F32)], axis=0)
                bufs.append(jnp.broadcast_to(part[None], (N_DEV,) + part.shape))
            else:
                bufs.append(_to_slots(n, grads[n]).astype(BF16))
        res = host(_Rider(bufs, True))
        received.update(zip(names, res[-1]))
        return res[:-1]

    small = {n: weights[n] for n in _SMALL}
    d_x = _local_step(x[0], mem[0], _assemble("w_in", gathered_in), small, loss_target[0], fetch_rest, ship)

    new = {}
    for n in _MATRICES:
        new[n] = _adamw(received[n], weights[n][0], mom1[n][0], mom2[n][0], "adamw_" + n)
    packed = _adamw(received["small"][:, :8], _pack_small({n: weights[n] for n in _SMALL}),
                    _pack_small({n: mom1[n] for n in _SMALL}), _pack_small({n: mom2[n] for n in _SMALL}), "adamw_small")
    small_new = [_unpack_small(p) for p in packed]
    loss = jnp.sum(received["small"][:, 8]) * (0.5 / D_MODEL)

    outs = [loss, d_x[None]]
    for slot in range(4):
        for n in _WEIGHT_ORDER:
            outs.append(new[n][slot][None] if n in new else small_new[slot][n])
    return tuple(outs)
```

```python
import functools
import math

import jax
import jax.numpy as jnp
from jax import lax
from jax.experimental import pallas as pl
from jax.experimental.pallas import tpu as pltpu

F32 = jnp.float32
BF16 = jnp.bfloat16

N_DEV = 8
D_MODEL = 1024
SB_HEAD_DIM = 64
SB_WIDTH = 512
RET_HEADS = 4
RET_QK_DIM = 128
RET_V_DIM = 256
RET_QK_WIDTH = 512
RET_V_WIDTH = 1024
RET_CHUNK = 128
RET_STEP_CHUNKS = 2
ROPE_BASE = 10000.0
MEM_HEADS = 4
MEM_HEAD_DIM = 256
FFN_HIDDEN = 2816
DN_ALPHA = 2.0 ** 0.25
LN_EPS = 1e-5
ADAM_LR = 0.001
ADAM_B1 = 0.9
ADAM_B2 = 0.999
ADAM_EPS = 1e-08
ADAM_WD = 0.01
ADAM_STEP = 10

VMEM_LIMIT_BYTES = 52 * 1024 * 1024
LANES = 128
SB_KEY_BLOCK = 128
SB_Q_BLOCK = 256
SB_DEAD_LOG = -105.0

MESH_AXES = ("x", "y", "c")


def _pick(dim, prefs):
    for p in prefs:
        if dim % p == 0:
            return p
    return dim


def _params(sem):
    return pltpu.CompilerParams(dimension_semantics=sem, vmem_limit_bytes=VMEM_LIMIT_BYTES)


def _dot(a, b, dims):
    return lax.dot_general(a, b, (dims, ((), ())), preferred_element_type=F32)


_NN = ((1,), (0,))
_NT = ((1,), (1,))
_TN = ((0,), (0,))


def _my_index():
    return 4 * lax.axis_index("x") + 2 * lax.axis_index("y") + lax.axis_index("c")


def _peer(k):
    x, y, c = lax.axis_index("x"), lax.axis_index("y"), lax.axis_index("c")
    bx, by, bc = (k >> 2) & 1, (k >> 1) & 1, k & 1
    px = (1 - x) if bx else x
    py = (1 - y) if by else y
    pc = (1 - c) if bc else c
    return (px, py, pc), 4 * px + 2 * py + pc


class _Rider:
    def __init__(self, bufs, scatter):
        self.bufs, self.scatter, self.n = list(bufs), scatter, len(bufs)
        self.specs = [pl.BlockSpec(memory_space=pl.ANY)] * self.n
        self.out_shape = [jax.ShapeDtypeStruct(b.shape if scatter else (N_DEV,) + b.shape, b.dtype) for b in self.bufs]
        self.scratch = [pltpu.SemaphoreType.DMA((self.n, N_DEV - 1)), pltpu.SemaphoreType.DMA((self.n, N_DEV - 1)),
                        pltpu.SemaphoreType.DMA((self.n,))]

    def _remote(self, ride, a, k, src_ref, slot, to):
        _, dst, (send_sems, recv_sems, _) = ride
        return pltpu.make_async_remote_copy(src_ref=src_ref, dst_ref=dst[a].at[slot], send_sem=send_sems.at[a, k],
                                            recv_sem=recv_sems.at[a, k], device_id=to,
                                            device_id_type=pl.DeviceIdType.MESH)

    def _local(self, ride, a):
        src, dst, (_, _, local_sems) = ride
        me = _my_index()
        return pltpu.make_async_copy(src[a].at[me] if self.scatter else src[a], dst[a].at[me], local_sems.at[a])

    def _direct(self, ride, a):
        src = ride[0]
        me = _my_index()
        out = []
        for k in range(1, N_DEV):
            peer, peer_idx = _peer(k)
            out.append(self._remote(ride, a, k - 1, src[a].at[peer_idx], me, peer))
        return out

    def _two_level(self, ride, a):
        src, dst = ride[0], ride[1]
        x, y, c = lax.axis_index("x"), lax.axis_index("y"), lax.axis_index("c")
        me, sibling = _my_index(), (x, y, 1 - c)
        chips = [(1 - x, y), (x, 1 - y), (1 - x, 1 - y)]
        first = [self._remote(ride, a, 0, src[a], me, sibling)]
        passed, landing = [], [self._remote(ride, a, 0, src[a], me + 1 - 2 * c, sibling)]
        for j, (px, py) in enumerate(chips):
            first.append(self._remote(ride, a, 1 + j, src[a], me, (px, py, c)))
            theirs = 4 * px + 2 * py + c
            passed.append(self._remote(ride, a, 4 + j, dst[a].at[theirs], theirs, sibling))
            landing.append(self._remote(ride, a, 1 + j, src[a], theirs, (px, py, c)))
        for j, (px, py) in enumerate(chips):
            landing.append(self._remote(ride, a, 4 + j, src[a], 4 * px + 2 * py + 1 - c, sibling))
        return first, passed, landing

    def start(self, ride):
        for a in range(self.n):
            self._local(ride, a).start()
            for cp in (self._direct(ride, a) if self.scatter else self._two_level(ride, a)[0]):
                cp.start()

    def finish(self, ride):
        if self.scatter:
            for a in range(self.n):
                for cp in self._direct(ride, a):
                    cp.wait()
                self._local(ride, a).wait()
            return
        levels = [self._two_level(ride, a) for a in range(self.n)]
        for first, passed, landing in levels:
            for j, cp in enumerate(passed):
                landing[1 + j].wait_recv()
                cp.start()
        for a, (first, passed, landing) in enumerate(levels):
            landing[0].wait_recv()
            for cp in landing[4:]:
                cp.wait_recv()
            for cp in first + passed:
                cp.wait_send()
            self._local(ride, a).wait()

    def start_at_first(self, ids, ride):
        first = functools.reduce(jnp.logical_and, [i == 0 for i in ids])

        @pl.when(first)
        def _():
            self.start(ride)

    def wait_at_last(self, ids, grid, ride):
        last = functools.reduce(jnp.logical_and, [i == g - 1 for i, g in zip(ids, grid)])

        @pl.when(last)
        def _():
            self.finish(ride)


def _exchange(bufs, scatter, name):
    rider = _Rider(bufs, scatter)

    def body(*refs):
        ride = (refs[:rider.n], refs[rider.n:2 * rider.n], refs[2 * rider.n:])
        rider.start(ride)
        rider.finish(ride)

    return pl.pallas_call(
        body,
        name=name,
        in_specs=rider.specs,
        out_specs=rider.specs,
        out_shape=rider.out_shape,
        scratch_shapes=rider.scratch,
    )(*rider.bufs)


MM_RESIDENT_B_BYTES = 14 * 1024 * 1024
MM_A_TILE_BYTES = 4 * 1024 * 1024
MM_OUT_TILE_BYTES = 6 * 1024 * 1024


def _mm_tiles(mode, M, N, K, a_bytes, out_bytes):
    if mode != "tn" and K * N * 2 <= MM_RESIDENT_B_BYTES:
        for tm in (1024, 512, 256, 128):
            if M % tm == 0 and tm * K * a_bytes <= MM_A_TILE_BYTES and tm * N * out_bytes <= MM_OUT_TILE_BYTES:
                return tm, N, K
    if mode == "tn":
        return (_pick(M, (1024, 1408, 512, 256, 128)), _pick(N, (1024, 1664, 1408, 512, 256, 128)),
                _pick(K, (2048, 1024, 512, 256, 128)))
    return _pick(M, (1024, 512, 256, 128)), _pick(N, (512, 256, 128)), _pick(K, (1024, 512, 256, 128))


def _mm(a, b, *, mode, out_dtype, name, res=None, res_scale=1.0, rider=None):
    if mode == "nn":
        (M, K), (K2, N) = a.shape, b.shape
    elif mode == "nt":
        (M, K), (N, K2) = a.shape, b.shape
    else:
        (K, M), (K2, N) = a.shape, b.shape
    assert K == K2, (a.shape, b.shape, mode)
    out_bytes = jnp.dtype(out_dtype).itemsize + (4 if res is not None else 0)
    tm, tn, tk = _mm_tiles(mode, M, N, K, a.dtype.itemsize, out_bytes)
    grid = (M // tm, N // tn, K // tk)
    nk = grid[2]
    dims = {"nn": _NN, "nt": _NT, "tn": _TN}[mode]
    n_in = 2 + (res is not None)
    n_ride = rider.n if rider is not None else 0

    def body(*refs):
        a_ref, b_ref = refs[:2]
        r_ref = refs[2] if res is not None else None
        o_ref = refs[n_in + n_ride]
        rest = refs[n_in + 2 * n_ride + 1:]
        acc_ref = rest[0] if nk > 1 else None
        ids = [pl.program_id(d) for d in range(3)]
        if rider is not None:
            ride = (refs[n_in:n_in + n_ride], refs[n_in + n_ride + 1:n_in + 2 * n_ride + 1], rest[-3:])
            rider.start_at_first(ids, ride)
        part = _dot(a_ref[...].astype(BF16), b_ref[...].astype(BF16), dims)

        def finish(total):
            if r_ref is not None:
                total = total + res_scale * r_ref[...]
            o_ref[...] = total.astype(out_dtype)

        if nk == 1:
            finish(part)
        else:
            k = ids[2]

            @pl.when(k == 0)
            def _():
                acc_ref[...] = part

            @pl.when(k > 0)
            def _():
                acc_ref[...] += part

            @pl.when(k == nk - 1)
            def _():
                finish(acc_ref[...])

        if rider is not None:
            rider.wait_at_last(ids, grid, ride)

    if mode == "nn":
        a_spec = pl.BlockSpec((tm, tk), lambda i, j, k: (i, k))
        b_spec = pl.BlockSpec((tk, tn), lambda i, j, k: (k, j))
    elif mode == "nt":
        a_spec = pl.BlockSpec((tm, tk), lambda i, j, k: (i, k))
        b_spec = pl.BlockSpec((tn, tk), lambda i, j, k: (j, k))
    else:
        a_spec = pl.BlockSpec((tk, tm), lambda i, j, k: (k, i))
        b_spec = pl.BlockSpec((tk, tn), lambda i, j, k: (k, j))
    o_spec = pl.BlockSpec((tm, tn), lambda i, j, k: (i, j))
    in_specs = [a_spec, b_spec] + ([o_spec] if res is not None else [])
    args = (a, b) + ((res,) if res is not None else ())
    out_specs, out_shape = [o_spec], [jax.ShapeDtypeStruct((M, N), out_dtype)]
    scratch = [pltpu.VMEM((tm, tn), F32)] if nk > 1 else []
    sem = ("parallel", "parallel", "arbitrary")
    if rider is not None:
        in_specs, args = in_specs + rider.specs, args + tuple(rider.bufs)
        out_specs, out_shape = out_specs + rider.specs, out_shape + rider.out_shape
        scratch = scratch + rider.scratch
        sem = ("arbitrary",) * 3
    outs = pl.pallas_call(
        body,
        name=name,
        grid=grid,
        in_specs=in_specs,
        out_specs=out_specs,
        out_shape=out_shape,
        scratch_shapes=scratch,
        compiler_params=_params(sem),
    )(*args)
    return outs[0] if rider is None else (outs[0], list(outs[1:]))


def _mm_host(a, b, *, rider, **kw):
    out = _mm(a, b, rider=rider, **kw)
    return out if rider is not None else (out, [])


def _as_host(rider, results):
    return results if rider is not None else tuple(results) + ([],)


MM_FUSED_MARGIN_BYTES = 10 * 1024 * 1024
MM_FUSED_MAX_ROWS = 512


def _col_sum_update(acc_ref, val, first):
    part = jnp.sum(val.reshape(val.shape[0] // 8, 8, val.shape[1]), axis=0)

    @pl.when(first)
    def _():
        acc_ref[...] = part

    @pl.when(jnp.logical_not(first))
    def _():
        acc_ref[...] += part


def _mm_fused(a, b, *, mode, name, extras, outs, epilogue, sums=(), rider=None, max_rows=MM_FUSED_MAX_ROWS,
              pass_a=False):
    parts = list(a) if isinstance(a, (list, tuple)) else [a]
    M, K = parts[0].shape[0], sum(p.shape[1] for p in parts)
    if mode == "nn":
        (K2, N), b_dims = b.shape, _NN
    else:
        (N, K2), b_dims = b.shape, _NT
    assert K == K2, (K, b.shape, mode)
    rows = parts + [e for e in extras if e.shape[0] == M]
    per_row = 2 * (sum(e.shape[1] * e.dtype.itemsize for e in rows)
                   + sum(c * jnp.dtype(d).itemsize for c, d in outs)) + 2 * N * 4
    budget = VMEM_LIMIT_BYTES - K * N * 2 - MM_FUSED_MARGIN_BYTES
    tm = next(t for t in (512, 256, 128, 64, 32, 16) if t <= max_rows and M % t == 0 and t * per_row <= budget)
    steps = M // tm
    n_a, n_x, n_o, n_s = len(parts), len(extras), len(outs), len(sums)
    n_ride = rider.n if rider is not None else 0

    def body(*refs):
        a_refs, b_ref = refs[:n_a], refs[n_a]
        x_refs = refs[n_a + 1:n_a + 1 + n_x]
        base = n_a + 1 + n_x + n_ride
        o_refs, s_refs = refs[base:base + n_o], refs[base + n_o:base + n_o + n_s]
        acc_refs = refs[base + n_o + n_s + n_ride:base + n_o + 2 * n_s + n_ride]
        ids = [pl.program_id(0)]
        if rider is not None:
            ride = (refs[n_a + 1 + n_x:base], refs[base + n_o + n_s:base + n_o + n_s + n_ride], refs[-3:])
            rider.start_at_first(ids, ride)
        a_tile = a_refs[0][...]
        a_bf16 = a_tile.astype(BF16) if n_a == 1 else jnp.concatenate([r[...].astype(BF16) for r in a_refs], axis=1)
        prod = _dot(a_bf16, b_ref[...], b_dims)
        tiles = epilogue(prod, *([a_tile] if pass_a else []), *[r[...] for r in x_refs])
        for o_ref, t in zip(o_refs, tiles[:n_o]):
            o_ref[...] = t.astype(o_ref.dtype)
        for acc_ref, t in zip(acc_refs, tiles[n_o:]):
            _col_sum_update(acc_ref, t, ids[0] == 0)
        if n_s:
            @pl.when(ids[0] == steps - 1)
            def _():
                for s_ref, acc_ref in zip(s_refs, acc_refs):
                    s_ref[...] = jnp.sum(acc_ref[...], axis=0, keepdims=True)
        if rider is not None:
            rider.wait_at_last(ids, (steps,), ride)

    in_specs = [pl.BlockSpec((tm, p.shape[1]), lambda i: (i, 0)) for p in parts]
    in_specs.append(pl.BlockSpec(b.shape, lambda i: (0, 0), pipeline_mode=pl.Buffered(1)))
    for e in extras:
        in_specs.append(pl.BlockSpec((tm, e.shape[1]), lambda i: (i, 0)) if e.shape[0] == M
                        else pl.BlockSpec(e.shape, lambda i: (0, 0)))
    out_specs = ([pl.BlockSpec((tm, c), lambda i: (i, 0)) for c, _ in outs]
                 + [pl.BlockSpec((1, c), lambda i: (0, 0)) for c in sums])
    out_shape = ([jax.ShapeDtypeStruct((M, c), d) for c, d in outs]
                 + [jax.ShapeDtypeStruct((1, c), F32) for c in sums])
    args = tuple(parts) + (b,) + tuple(extras)
    scratch = [pltpu.VMEM((8, c), F32) for c in sums]
    if rider is not None:
        in_specs, args = in_specs + rider.specs, args + tuple(rider.bufs)
        out_specs, out_shape = out_specs + rider.specs, out_shape + rider.out_shape
        scratch = scratch + rider.scratch
    res = pl.pallas_call(
        body,
        name=name,
        grid=(steps,),
        in_specs=in_specs,
        out_specs=out_specs,
        out_shape=out_shape,
        scratch_shapes=scratch,
        compiler_params=_params(("arbitrary",) if (n_s or rider is not None) else ("parallel",)),
    )(*args)
    return tuple(res[:n_o + n_s]) + ((list(res[n_o + n_s:]),) if rider is not None else ())


def _pair_rows(blk, lane_is_a):
    zero = jnp.zeros_like(blk)
    return jnp.concatenate([jnp.where(lane_is_a, blk, zero), jnp.where(lane_is_a, zero, blk)], axis=0)


SB_STRIP = 32
SB_FWD_PAIRS = 4
SB_BWD_PAIRS = 2


def _pair_lanes(p):
    return slice(p * LANES, (p + 1) * LANES)


def _sb_scan_matrices():
    o = lax.broadcasted_iota(jnp.int32, (2 * LANES, 4 * LANES), 0)
    c = lax.broadcasted_iota(jnp.int32, (2 * LANES, 4 * LANES), 1) & (2 * LANES - 1)
    same = (o >= LANES) == (c >= LANES)
    oo, cc = o & (LANES - 1), c & (LANES - 1)
    return (jnp.where(same & (cc > oo), 1.0, 0.0).astype(BF16), jnp.where(same & (cc < oo), 1.0, 0.0).astype(BF16))


def _sb_causal_masks(tq):
    d = lax.broadcasted_iota(jnp.int32, (tq // SB_KEY_BLOCK, SB_KEY_BLOCK, tq), 0)
    k = lax.broadcasted_iota(jnp.int32, (tq // SB_KEY_BLOCK, SB_KEY_BLOCK, tq), 1)
    t = lax.broadcasted_iota(jnp.int32, (tq // SB_KEY_BLOCK, SB_KEY_BLOCK, tq), 2)
    return jnp.where(d * SB_KEY_BLOCK + k < t, 1.0, 0.0).astype(F32)


def _sb_log_terms(z):
    log_rem = -jnp.maximum(z, 0.0) - jnp.log(1.0 + jnp.exp(-jnp.abs(z)))
    return log_rem, log_rem + z


def _sb_store_split(ref, strip, val, cols):
    hi = val.astype(BF16)
    ref[pl.ds(strip * SB_STRIP, SB_STRIP), cols] = hi
    ref[pl.ds(2 * LANES + strip * SB_STRIP, SB_STRIP), cols] = (val - hi.astype(F32)).astype(BF16)


def _sb_lanes(tq, diag):
    if diag == "left":
        return 0, tq // 2
    first = 0 if diag is None else diag * SB_KEY_BLOCK
    return first, tq - first


def _lane_add(full, part, lanes):
    first, width = lanes
    pieces = [full[:, :first]] if first else []
    pieces.append(full[:, first:first + width] + part)
    if first + width < full.shape[1]:
        pieces.append(full[:, first + width:])
    return pieces[0] if len(pieces) == 1 else jnp.concatenate(pieces, axis=1)


def _sb_fwd(h_a, rider=None):
    assert SB_FWD_PAIRS == 4
    T = h_a.shape[0]
    tq = _pick(T, (SB_Q_BLOCK, SB_KEY_BLOCK))
    nq, per_q, nkb = T // tq, tq // SB_KEY_BLOCK, T // SB_KEY_BLOCK
    n_strips = 2 * LANES // SB_STRIP
    n_ride = rider.n if rider is not None else 0
    after_m, _ = _sb_scan_matrices()
    causal_m = _sb_causal_masks(tq)
    pairs = SB_FWD_PAIRS

    def body(*refs):
        q_ref, k_ref, v_ref, after_ref, causal_ref = refs[:5]
        a_ref, r_ref, n_ref = refs[5 + n_ride:8 + n_ride]
        z_ref, lb_ref, split_ref, w_ref = refs[8 + 2 * n_ride:12 + 2 * n_ride]
        ids = [pl.program_id(0)]
        if rider is not None:
            ride = (refs[5:5 + n_ride], refs[8 + n_ride:8 + 2 * n_ride], refs[-3:])
            rider.start_at_first(ids, ride)
        i = ids[0]
        q_t = [(q_ref[:, _pair_lanes(p)].astype(F32).T * (SB_HEAD_DIM ** -0.5)).astype(BF16) for p in range(pairs)]
        lane_is_a = lax.broadcasted_iota(jnp.int32, (SB_KEY_BLOCK, LANES), 1) < SB_HEAD_DIM

        def tile(kb, diag, carry):
            masked = isinstance(diag, int)
            lanes = _sb_lanes(tq, diag)
            cols = slice(lanes[0], lanes[0] + lanes[1])
            acc_t, ra, rb = [list(c) for c in carry]
            ks = pl.multiple_of(kb * SB_KEY_BLOCK, SB_KEY_BLOCK)

            def causal(s):
                return causal_ref[diag, pl.ds((s * SB_STRIP) % SB_KEY_BLOCK, SB_STRIP), cols]

            vv = []
            for p in range(pairs):
                kk = _pair_rows(k_ref[pl.ds(ks, SB_KEY_BLOCK), _pair_lanes(p)], lane_is_a)
                vv.append(_pair_rows(v_ref[pl.ds(ks, SB_KEY_BLOCK), _pair_lanes(p)], lane_is_a))
                z_ref[p, :, cols] = _dot(kk, q_t[p][:, cols], _NN)
            sums = [[jnp.zeros((8, lanes[1]), F32), jnp.zeros((8, lanes[1]), F32)] for _ in range(pairs)]
            for p in range(pairs):
                for s in range(n_strips):
                    rows = pl.ds(s * SB_STRIP, SB_STRIP)
                    log_rem, log_beta = _sb_log_terms(z_ref[p, rows, cols])
                    lb_ref[p, rows, cols] = log_beta
                    if masked:
                        log_rem = log_rem * causal(s)
                    _sb_store_split(split_ref.at[p], s, log_rem, cols)
                    head = (s * SB_STRIP) // SB_KEY_BLOCK
                    sums[p][head] = sums[p][head] + jnp.sum(log_rem.reshape(SB_STRIP // 8, 8, lanes[1]), axis=0)
            for p in range(pairs):
                z_ref[p, :, cols] = _dot(after_ref[...], split_ref[p, :, cols], _NN)
            for p in range(pairs):
                for s in range(n_strips):
                    rows = pl.ds(s * SB_STRIP, SB_STRIP)
                    start = (ra[p] if (s * SB_STRIP) < SB_KEY_BLOCK else rb[p])[:, cols]
                    w = jnp.exp(lb_ref[p, rows, cols] + z_ref[p, rows, cols] + start)
                    if masked:
                        w = w * causal(s)
                    w_ref[p, rows, cols] = w.astype(BF16)
            for p in range(pairs):
                acc_t[p] = _lane_add(acc_t[p], _dot(vv[p], w_ref[p, :, cols], _TN), lanes)
                r_ref[2 * p, kb] = ra[p]
                r_ref[2 * p + 1, kb] = rb[p]
                ra[p] = _lane_add(ra[p], jnp.sum(sums[p][0], axis=0, keepdims=True), lanes)
                rb[p] = _lane_add(rb[p], jnp.sum(sums[p][1], axis=0, keepdims=True), lanes)
            return tuple(acc_t), tuple(ra), tuple(rb)

        carry = (tuple(jnp.zeros((LANES, tq), F32) for _ in range(pairs)),
                 tuple(jnp.zeros((1, tq), F32) for _ in range(pairs)),
                 tuple(jnp.zeros((1, tq), F32) for _ in range(pairs)))
        for d in range(per_q):
            carry = tile(i * per_q + (per_q - 1 - d), per_q - 1 - d, carry)
        n_full = i * per_q

        def top_of(sums_a, sums_b, first):
            return jnp.max(functools.reduce(jnp.maximum, [r[:, first:] for r in sums_a + sums_b]))

        def alive(c):
            return jnp.logical_and(c[0] < n_full, top_of(c[2], c[3], 0) > SB_DEAD_LOG)

        def step(c):
            kb = n_full - 1 - c[0]
            return (c[0] + 1,) + lax.cond(top_of(c[2], c[3], tq // 2) > SB_DEAD_LOG,
                                          lambda cc: tile(kb, None, cc), lambda cc: tile(kb, "left", cc), c[1:])

        walked, acc_t, _, _ = lax.while_loop(alive, step, (jnp.int32(0),) + carry)
        for p in range(pairs):
            a_ref[:, _pair_lanes(p)] = acc_t[p].T.astype(BF16)
        n_ref[...] = jnp.zeros(n_ref.shape, F32) + walked.astype(F32)
        if rider is not None:
            rider.wait_at_last(ids, (nq,), ride)

    wide = pairs * LANES
    in_specs = [pl.BlockSpec((tq, wide), lambda i: (i, 0)),
                pl.BlockSpec((T, wide), lambda i: (0, 1), pipeline_mode=pl.Buffered(1)),
                pl.BlockSpec((T, wide), lambda i: (0, 2), pipeline_mode=pl.Buffered(1)),
                pl.BlockSpec(after_m.shape, lambda i: (0, 0), pipeline_mode=pl.Buffered(1)),
                pl.BlockSpec(causal_m.shape, lambda i: (0, 0, 0), pipeline_mode=pl.Buffered(1))]
    out_specs = [pl.BlockSpec((tq, wide), lambda i: (i, 0)),
                 pl.BlockSpec((2 * pairs, nkb, 1, tq), lambda i: (0, 0, 0, i)),
                 pl.BlockSpec((1, 8, LANES), lambda i: (i, 0, 0))]
    out_shape = [jax.ShapeDtypeStruct((T, SB_WIDTH), BF16), jax.ShapeDtypeStruct((2 * pairs, nkb, 1, T), F32),
                 jax.ShapeDtypeStruct((nq, 8, LANES), F32)]
    args = (h_a, h_a, h_a, after_m, causal_m)
    scratch = [pltpu.VMEM((pairs, 2 * LANES, tq), F32), pltpu.VMEM((pairs, 2 * LANES, tq), F32),
               pltpu.VMEM((pairs, 4 * LANES, tq), BF16), pltpu.VMEM((pairs, 2 * LANES, tq), BF16)]
    if rider is not None:
        in_specs, args = in_specs + rider.specs, args + tuple(rider.bufs)
        out_specs, out_shape = out_specs + rider.specs, out_shape + rider.out_shape
        scratch = scratch + rider.scratch
    outs = pl.pallas_call(
        body,
        name="sb_fwd",
        grid=(nq,),
        in_specs=in_specs,
        out_specs=out_specs,
        out_shape=out_shape,
        scratch_shapes=scratch,
        compiler_params=_params(("arbitrary",)),
    )(*args)
    return outs[0], (outs[1], outs[2]), list(outs[3:])


def _sb_bwd(h_a, d_out, saved, rider=None):
    r_mat, walked_blocks = saved
    T = h_a.shape[0]
    tq = _pick(T, (SB_Q_BLOCK, SB_KEY_BLOCK))
    nq, per_q, nkb = T // tq, tq // SB_KEY_BLOCK, T // SB_KEY_BLOCK
    n_strips = 2 * LANES // SB_STRIP
    after_m, before_m = _sb_scan_matrices()
    causal_m = _sb_causal_masks(tq)
    pairs = SB_BWD_PAIRS
    groups = 4 // pairs
    n_ride = rider.n if rider is not None else 0

    def body(*refs):
        q_ref, k_ref, v_ref, do_ref, r_ref, n_ref, after_ref, before_ref, causal_ref = refs[:9]
        dq_ref, dk_ref, dv_ref = refs[9 + n_ride:12 + n_ride]
        z_ref, lb_ref, split_ref, w_ref, da_ref, dz_ref = refs[12 + 2 * n_ride:18 + 2 * n_ride]
        ids = [pl.program_id(0), pl.program_id(1)]
        if rider is not None:
            ride = (refs[9:9 + n_ride], refs[12 + n_ride:12 + 2 * n_ride], refs[-3:])
            rider.start_at_first(ids, ride)
        i = ids[1]

        @pl.when(i == 0)
        def _():
            dk_ref[...] = jnp.zeros_like(dk_ref)
            dv_ref[...] = jnp.zeros_like(dv_ref)

        scale = SB_HEAD_DIM ** -0.5
        q = [q_ref[:, _pair_lanes(p)] for p in range(pairs)]
        d_o = [do_ref[:, _pair_lanes(p)] for p in range(pairs)]
        q_t = [(x.astype(F32).T * scale).astype(BF16) for x in q]
        do_t = [x.astype(F32).T.astype(BF16) for x in d_o]
        lane_is_a = lax.broadcasted_iota(jnp.int32, (SB_KEY_BLOCK, LANES), 1) < SB_HEAD_DIM

        def tile(kb, diag, carry):
            masked = isinstance(diag, int)
            lanes = _sb_lanes(tq, diag)
            cols = slice(lanes[0], lanes[0] + lanes[1])
            dq_t, ca, cb = [list(c) for c in carry]
            ks = pl.multiple_of(kb * SB_KEY_BLOCK, SB_KEY_BLOCK)

            def causal(s):
                return causal_ref[diag, pl.ds((s * SB_STRIP) % SB_KEY_BLOCK, SB_STRIP), cols]

            kk, vv = [], []
            for p in range(pairs):
                kk.append(_pair_rows(k_ref[pl.ds(ks, SB_KEY_BLOCK), _pair_lanes(p)], lane_is_a))
                vv.append(_pair_rows(v_ref[pl.ds(ks, SB_KEY_BLOCK), _pair_lanes(p)], lane_is_a))
                z_ref[p, :, cols] = _dot(kk[p], q_t[p][:, cols], _NN)
            for p in range(pairs):
                for s in range(n_strips):
                    rows = pl.ds(s * SB_STRIP, SB_STRIP)
                    log_rem, log_beta = _sb_log_terms(z_ref[p, rows, cols])
                    lb_ref[p, rows, cols] = log_beta
                    if masked:
                        log_rem = log_rem * causal(s)
                    _sb_store_split(split_ref.at[p], s, log_rem, cols)
            for p in range(pairs):
                z_ref[p, :, cols] = _dot(after_ref[...], split_ref[p, :, cols], _NN)
                da_ref[p, :, cols] = _dot(vv[p], do_t[p][:, cols], _NN)
            sums = [[jnp.zeros((8, lanes[1]), F32), jnp.zeros((8, lanes[1]), F32)] for _ in range(pairs)]
            for p in range(pairs):
                for s in range(n_strips):
                    rows = pl.ds(s * SB_STRIP, SB_STRIP)
                    start = r_ref[2 * p + (s * SB_STRIP) // SB_KEY_BLOCK, kb][:, cols]
                    w = jnp.exp(lb_ref[p, rows, cols] + z_ref[p, rows, cols] + start)
                    if masked:
                        w = w * causal(s)
                    w_ref[p, rows, cols] = w.astype(BF16)
                    da = da_ref[p, rows, cols] * w
                    da_ref[p, rows, cols] = da
                    _sb_store_split(split_ref.at[p], s, da, cols)
                    head = (s * SB_STRIP) // SB_KEY_BLOCK
                    sums[p][head] = sums[p][head] + jnp.sum(da.reshape(SB_STRIP // 8, 8, lanes[1]), axis=0)
            for p in range(pairs):
                z_ref[p, :, cols] = _dot(before_ref[...], split_ref[p, :, cols], _NN)
            for p in range(pairs):
                for s in range(n_strips):
                    rows = pl.ds(s * SB_STRIP, SB_STRIP)
                    base = (ca[p] if (s * SB_STRIP) < SB_KEY_BLOCK else cb[p])[:, cols]
                    sig = jnp.exp(lb_ref[p, rows, cols])
                    dz = da_ref[p, rows, cols] * (1.0 - sig) - (z_ref[p, rows, cols] + base) * sig
                    if masked:
                        dz = dz * causal(s)
                    dz_ref[p, rows, cols] = (dz * scale).astype(BF16)
            for p in range(pairs):
                dq_t[p] = _lane_add(dq_t[p], _dot(kk[p], dz_ref[p, :, cols], _TN), lanes)
                dkk = _dot(dz_ref[p, :, cols], q[p][cols, :], _NN)
                dvv = _dot(w_ref[p, :, cols], d_o[p][cols, :], _NN)
                here = (pl.ds(ks, SB_KEY_BLOCK), _pair_lanes(p))
                dk_ref[here] += jnp.where(lane_is_a, dkk[:SB_KEY_BLOCK], dkk[SB_KEY_BLOCK:])
                dv_ref[here] += jnp.where(lane_is_a, dvv[:SB_KEY_BLOCK], dvv[SB_KEY_BLOCK:])
                ca[p] = _lane_add(ca[p], jnp.sum(sums[p][0], axis=0, keepdims=True), lanes)
                cb[p] = _lane_add(cb[p], jnp.sum(sums[p][1], axis=0, keepdims=True), lanes)
            return tuple(dq_t), tuple(ca), tuple(cb)

        n_full = i * per_q
        walked = jnp.clip(jnp.max(n_ref[...]).astype(jnp.int32), 0, n_full)
        carry = (tuple(jnp.zeros((LANES, tq), F32) for _ in range(pairs)),
                 tuple(jnp.zeros((1, tq), F32) for _ in range(pairs)),
                 tuple(jnp.zeros((1, tq), F32) for _ in range(pairs)))

        def below(kb, c):
            starts = [r_ref[h, kb][:, tq // 2:] for h in range(2 * pairs)]
            reaches = jnp.max(functools.reduce(jnp.maximum, starts)) > SB_DEAD_LOG
            return lax.cond(reaches, lambda cc: tile(kb, None, cc), lambda cc: tile(kb, "left", cc), c)

        carry = lax.fori_loop(n_full - walked, n_full, below, carry)
        for d in range(per_q):
            carry = tile(i * per_q + d, d, carry)
        for p in range(pairs):
            dq_ref[:, _pair_lanes(p)] = carry[0][p].T.astype(BF16)
        if rider is not None:
            rider.wait_at_last(ids, (groups, nq), ride)

    wide = pairs * LANES
    mat = pl.BlockSpec(after_m.shape, lambda g, i: (0, 0), pipeline_mode=pl.Buffered(1))
    in_specs = [pl.BlockSpec((tq, wide), lambda g, i: (i, g)),
                pl.BlockSpec((T, wide), lambda g, i: (0, groups + g), pipeline_mode=pl.Buffered(1)),
                pl.BlockSpec((T, wide), lambda g, i: (0, 2 * groups + g), pipeline_mode=pl.Buffered(1)),
                pl.BlockSpec((tq, wide), lambda g, i: (i, g)),
                pl.BlockSpec((2 * pairs, nkb, 1, tq), lambda g, i: (g, 0, 0, i)),
                pl.BlockSpec((1, 8, LANES), lambda g, i: (i, 0, 0)),
                mat, mat,
                pl.BlockSpec(causal_m.shape, lambda g, i: (0, 0, 0), pipeline_mode=pl.Buffered(1))]
    out_specs = [pl.BlockSpec((tq, wide), lambda g, i: (i, g)),
                 pl.BlockSpec((T, wide), lambda g, i: (0, g)),
                 pl.BlockSpec((T, wide), lambda g, i: (0, g))]
    out_shape = [jax.ShapeDtypeStruct((T, SB_WIDTH), BF16), jax.ShapeDtypeStruct((T, SB_WIDTH), F32),
                 jax.ShapeDtypeStruct((T, SB_WIDTH), F32)]
    args = (h_a, h_a, h_a, d_out, r_mat, walked_blocks, after_m, before_m, causal_m)
    scratch = [pltpu.VMEM((pairs, 2 * LANES, tq), F32), pltpu.VMEM((pairs, 2 * LANES, tq), F32),
               pltpu.VMEM((pairs, 4 * LANES, tq), BF16), pltpu.VMEM((pairs, 2 * LANES, tq), BF16),
               pltpu.VMEM((pairs, 2 * LANES, tq), F32), pltpu.VMEM((pairs, 2 * LANES, tq), BF16)]
    if rider is not None:
        in_specs, args = in_specs + rider.specs, args + tuple(rider.bufs)
        out_specs, out_shape = out_specs + rider.specs, out_shape + rider.out_shape
        scratch = scratch + rider.scratch
    outs = pl.pallas_call(
        body,
        name="sb_bwd",
        grid=(groups, nq),
        in_specs=in_specs,
        out_specs=out_specs,
        out_shape=out_shape,
        scratch_shapes=scratch,
        compiler_params=_params(("arbitrary", "arbitrary") if rider is not None else ("parallel", "arbitrary")),
    )(*args)
    return outs[0], outs[1], outs[2], list(outs[3:])


def _ret_tables(T):
    half = RET_QK_DIM // 2
    inv = 1.0 / (ROPE_BASE ** (jnp.arange(half, dtype=F32) / half))
    ang = jnp.arange(T, dtype=F32)[:, None] * inv[None, :]
    cos, sin = jnp.cos(ang), jnp.sin(ang)
    cos_t = jnp.concatenate([cos, cos], axis=1)
    sin_t = jnp.concatenate([-sin, sin], axis=1)
    log_gamma = jnp.log1p(-jnp.exp2(-5.0 - jnp.arange(RET_HEADS, dtype=F32)))
    idx = jnp.arange(RET_CHUNK, dtype=F32)
    rel = idx[:, None] - idx[None, :]
    decay = jnp.where(rel[None] >= 0, jnp.exp(log_gamma[:, None, None] * jnp.maximum(rel, 0.0)[None]), 0.0)
    k_decay = jnp.exp(log_gamma[None, :] * (RET_CHUNK - 1.0 - idx)[:, None])
    q_decay = jnp.exp(log_gamma[None, :] * (idx + 1.0)[:, None])
    chunk_decay = jnp.exp(log_gamma * RET_CHUNK)
    k_dec = jnp.broadcast_to(k_decay.T[:, :, None], (RET_HEADS, RET_CHUNK, LANES))
    q_dec = jnp.broadcast_to(q_decay.T[:, :, None], (RET_HEADS, RET_CHUNK, LANES))
    c_dec = jnp.broadcast_to(chunk_decay[:, None, None], (RET_HEADS, 8, LANES))
    return cos_t, sin_t, decay, k_dec, q_dec, c_dec


def _rotary(x, cos_t, sin_t):
    return x * cos_t + pltpu.roll(x, RET_QK_DIM // 2, 1) * sin_t


def _rotary_transpose(dy, cos_t, sin_t):
    return dy * cos_t + pltpu.roll(dy * sin_t, RET_QK_DIM // 2, 1)


def _head_norm(o):
    mu = jnp.mean(o, axis=1, keepdims=True)
    cen = o - mu
    var = jnp.mean(cen * cen, axis=1, keepdims=True)
    rstd = lax.rsqrt(var + LN_EPS)
    return cen * rstd, rstd


def _ret_specs(steps, per_step, reverse):
    def n_of(n):
        return (steps - 1 - n) if reverse else n

    rows = per_step * RET_CHUNK
    q_spec = pl.BlockSpec((rows, RET_QK_WIDTH), lambda n: (n_of(n), 0))
    k_spec = pl.BlockSpec((rows, RET_QK_WIDTH), lambda n: (n_of(n), 1))
    vv = pl.BlockSpec((rows, RET_V_WIDTH), lambda n: (n_of(n), 0))
    pos = pl.BlockSpec((rows, LANES), lambda n: (n_of(n), 0))
    per_head = pl.BlockSpec((RET_HEADS, RET_CHUNK, LANES), lambda n: (0, 0, 0))
    c_dec = pl.BlockSpec((RET_HEADS, 8, LANES), lambda n: (0, 0, 0))
    state = pl.BlockSpec((RET_HEADS, per_step, RET_QK_DIM, RET_V_DIM), lambda n: (0, n_of(n), 0, 0))
    return q_spec, k_spec, vv, pos, per_head, c_dec, state


def _qk_cols(h):
    return slice(h * RET_QK_DIM, (h + 1) * RET_QK_DIM)


def _v_cols(h):
    return slice(h * RET_V_DIM, (h + 1) * RET_V_DIM)


def _ret_fwd(h_b, h_c, h_d, tables):
    T = h_b.shape[0]
    nc = T // RET_CHUNK
    per_step = _pick(nc, (RET_STEP_CHUNKS, 1))
    steps = nc // per_step
    q_spec, k_spec, vv, pos, per_head, c_dec, state = _ret_specs(steps, per_step, False)

    def body(q_ref, k_ref, v_ref, g_ref, cos_ref, sin_ref, dec_ref, kd_ref, qd_ref, cd_ref,
             y_ref, o_ref, st_ref, state_ref):
        @pl.when(pl.program_id(0) == 0)
        def _():
            state_ref[...] = jnp.zeros_like(state_ref)

        for c in range(per_step):
            rows = pl.ds(c * RET_CHUNK, RET_CHUNK)
            cos_t, sin_t = cos_ref[rows, :], sin_ref[rows, :]
            for h in range(RET_HEADS):
                q = _rotary(q_ref[rows, _qk_cols(h)], cos_t, sin_t) * (RET_QK_DIM ** -0.5)
                k = _rotary(k_ref[rows, _qk_cols(h)], cos_t, sin_t)
                v = v_ref[rows, _v_cols(h)]
                prev = state_ref[h]
                scores = _dot(q.astype(BF16), k.astype(BF16), _NT) * dec_ref[h]
                inner = _dot(scores.astype(BF16), v, _NN)
                cross = _dot((q * qd_ref[h]).astype(BF16), prev.astype(BF16), _NN)
                o = inner + cross
                st_ref[h, c] = prev
                kv = _dot((k * kd_ref[h]).astype(BF16), v, _TN)
                state_ref[h] = prev * cd_ref[h, 0:1, 0:1] + kv
                o_ref[rows, _v_cols(h)] = o
                normed, _ = _head_norm(o)
                gate = g_ref[rows, _v_cols(h)]
                y_ref[rows, _v_cols(h)] = (gate * jax.nn.sigmoid(gate) * normed).astype(BF16)

    return pl.pallas_call(
        body,
        name="ret_fwd",
        grid=(steps,),
        in_specs=[q_spec, k_spec, vv, vv, pos, pos, per_head, per_head, per_head, c_dec],
        out_specs=[vv, vv, state],
        out_shape=[jax.ShapeDtypeStruct((T, RET_V_WIDTH), BF16),
                   jax.ShapeDtypeStruct((T, RET_V_WIDTH), F32),
                   jax.ShapeDtypeStruct((RET_HEADS, nc, RET_QK_DIM, RET_V_DIM), F32)],
        scratch_shapes=[pltpu.VMEM((RET_HEADS, RET_QK_DIM, RET_V_DIM), F32)],
        compiler_params=_params(("arbitrary",)),
    )(h_b, h_b, h_c, h_d, *tables)


def _ret_bwd(d_y, o_pre, states, h_b, h_c, h_d, tables, rider=None):
    T = h_b.shape[0]
    nc = T // RET_CHUNK
    per_step = _pick(nc, (RET_STEP_CHUNKS, 1))
    steps = nc // per_step
    q_spec, k_spec, vv, pos, per_head, c_dec, state = _ret_specs(steps, per_step, True)
    n_ride = rider.n if rider is not None else 0

    def body(*refs):
        (dy_ref, o_ref, st_ref, q_ref, k_ref, v_ref, g_ref, cos_ref, sin_ref, dec_ref, kd_ref, qd_ref,
         cd_ref) = refs[:13]
        dq_ref, dk_ref, dv_ref, dg_ref = refs[13 + n_ride:17 + n_ride]
        carry_ref = refs[17 + 2 * n_ride]
        ids = [pl.program_id(0)]
        if rider is not None:
            ride = (refs[13:13 + n_ride], refs[17 + n_ride:17 + 2 * n_ride], refs[-3:])
            rider.start_at_first(ids, ride)

        @pl.when(ids[0] == 0)
        def _():
            carry_ref[...] = jnp.zeros_like(carry_ref)

        scale = RET_QK_DIM ** -0.5
        for c in reversed(range(per_step)):
            rows = pl.ds(c * RET_CHUNK, RET_CHUNK)
            cos_t, sin_t = cos_ref[rows, :], sin_ref[rows, :]
            for h in range(RET_HEADS):
                q = _rotary(q_ref[rows, _qk_cols(h)], cos_t, sin_t) * scale
                k = _rotary(k_ref[rows, _qk_cols(h)], cos_t, sin_t)
                v = v_ref[rows, _v_cols(h)]
                decay, k_dec, q_dec = dec_ref[h], kd_ref[h], qd_ref[h]
                chunk_decay = cd_ref[h, 0:1, 0:1]
                state = st_ref[h, c].astype(BF16)
                later = carry_ref[h]
                later_b = later.astype(BF16)

                gate = g_ref[rows, _v_cols(h)]
                sig = jax.nn.sigmoid(gate)
                silu = gate * sig
                normed, rstd = _head_norm(o_ref[rows, _v_cols(h)])
                d_y = dy_ref[rows, _v_cols(h)]
                dg_ref[rows, _v_cols(h)] = (d_y * normed * (sig * (1.0 + gate * (1.0 - sig)))).astype(BF16)
                d_n = d_y * silu
                d_o = rstd * (d_n - jnp.mean(d_n, axis=1, keepdims=True)
                              - normed * jnp.mean(d_n * normed, axis=1, keepdims=True))
                d_ob = d_o.astype(BF16)

                qb, kb = q.astype(BF16), k.astype(BF16)
                qd_b, kd_b = (q * q_dec).astype(BF16), (k * k_dec).astype(BF16)
                scores = _dot(qb, kb, _NT) * decay
                d_scores = (_dot(d_ob, v, _NT) * decay).astype(BF16)
                dq = _dot(d_scores, kb, _NN) + _dot(d_ob, state, _NT) * q_dec
                dk = _dot(d_scores, qb, _TN) + _dot(v, later_b, _NT) * k_dec
                dv = _dot(scores.astype(BF16), d_ob, _TN) + _dot(kd_b, later_b, _NN)
                carry_ref[h] = _dot(qd_b, d_ob, _TN) + chunk_decay * later
                dq_ref[rows, _qk_cols(h)] = _rotary_transpose(dq * scale, cos_t, sin_t).astype(BF16)
                dk_ref[rows, _qk_cols(h)] = _rotary_transpose(dk, cos_t, sin_t).astype(BF16)
                dv_ref[rows, _v_cols(h)] = dv.astype(BF16)
        if rider is not None:
            rider.wait_at_last(ids, (steps,), ride)

    qk_out = pl.BlockSpec((per_step * RET_CHUNK, RET_QK_WIDTH), lambda n: (steps - 1 - n, 0))
    in_specs = [vv, vv, state, q_spec, k_spec, vv, vv, pos, pos, per_head, per_head, per_head, c_dec]
    out_specs = [qk_out, qk_out, vv, vv]
    out_shape = [jax.ShapeDtypeStruct((T, RET_QK_WIDTH), BF16), jax.ShapeDtypeStruct((T, RET_QK_WIDTH), BF16),
                 jax.ShapeDtypeStruct((T, RET_V_WIDTH), BF16), jax.ShapeDtypeStruct((T, RET_V_WIDTH), BF16)]
    args = (d_y, o_pre, states, h_b, h_b, h_c, h_d) + tuple(tables)
    scratch = [pltpu.VMEM((RET_HEADS, RET_QK_DIM, RET_V_DIM), F32)]
    if rider is not None:
        in_specs, args = in_specs + rider.specs, args + tuple(rider.bufs)
        out_specs, out_shape = out_specs + rider.specs, out_shape + rider.out_shape
        scratch = scratch + rider.scratch
    outs = pl.pallas_call(
        body,
        name="ret_bwd",
        grid=(steps,),
        in_specs=in_specs,
        out_specs=out_specs,
        out_shape=out_shape,
        scratch_shapes=scratch,
        compiler_params=_params(("arbitrary",)),
    )(*args)
    return outs[0], outs[1], outs[2], outs[3], list(outs[4:])


def _proj_tiles(h, x):
    return h[:, 0:1536], h[:, 1536:2560], h[:, 2560:3584], h[:, 3584:4608], h[:, 4608:6656], x


def _gate_mix_tiles(y_ret, h_e, b_gate, y_sb):
    gates = jax.nn.sigmoid(h_e + b_gate)
    return y_ret, gates[:, :D_MODEL] * y_sb + gates[:, D_MODEL:] * y_ret


def _gate_mix_grad_tiles(d_mix, h_e, b_gate, y_sb, y_ret):
    gates = jax.nn.sigmoid(h_e + b_gate)
    g0, g1 = gates[:, :D_MODEL], gates[:, D_MODEL:]
    d_e = jnp.concatenate([d_mix * y_sb * g0 * (1.0 - g0), d_mix * y_ret * g1 * (1.0 - g1)], axis=1)
    return d_mix * g0, d_mix * g1, d_e, d_e


def _ln_stats(u):
    mu = jnp.mean(u, axis=1, keepdims=True)
    cen = u - mu
    var = jnp.mean(cen * cen, axis=1, keepdims=True)
    rstd = lax.rsqrt(var + LN_EPS)
    return cen * rstd, rstd


def _ln_input_grad(d_out, gain, xhat, rstd):
    d_hat = d_out * gain
    return rstd * (d_hat - jnp.mean(d_hat, axis=1, keepdims=True)
                   - xhat * jnp.mean(d_hat * xhat, axis=1, keepdims=True))


def _ln_tiles(sub, x_prev, gain, bias):
    xhat, rstd = _ln_stats(DN_ALPHA * x_prev + sub)
    out = xhat * gain + bias
    return out, out, xhat, rstd


def _residual_tiles(d_sub, res):
    return (d_sub + DN_ALPHA * res,)


def _ln_grad_tiles(d_sub, res, xhat, rstd, gain):
    d_out = d_sub + DN_ALPHA * res
    du = _ln_input_grad(d_out, gain, xhat, rstd)
    return du, du, d_out * xhat, d_out


def _ln_loss_tiles(sub, x_prev, gain, bias, target):
    xhat, rstd = _ln_stats(DN_ALPHA * x_prev + sub)
    diff = xhat * gain + bias - target
    d_out = diff * (1.0 / D_MODEL)
    du = _ln_input_grad(d_out, gain, xhat, rstd)
    return du, du, diff * diff, d_out * xhat, d_out


def _mem_probs(q_h, k_h):
    s = _dot(q_h, k_h, _NT) * (MEM_HEAD_DIM ** -0.5)
    e = jnp.exp(s - jnp.max(s, axis=1, keepdims=True))
    return e / jnp.sum(e, axis=1, keepdims=True)


def _xattn_fwd(q, kv):
    T, mem_len = q.shape[0], kv.shape[0]
    tq = _pick(T, (512, 256, 128))

    def body(q_ref, kv_ref, o_ref):
        for h in range(MEM_HEADS):
            cols = slice(h * MEM_HEAD_DIM, (h + 1) * MEM_HEAD_DIM)
            vcols = slice(D_MODEL + h * MEM_HEAD_DIM, D_MODEL + (h + 1) * MEM_HEAD_DIM)
            p = _mem_probs(q_ref[:, cols], kv_ref[:, cols])
            o_ref[:, cols] = _dot(p.astype(BF16), kv_ref[:, vcols], _NN).astype(BF16)

    return pl.pallas_call(
        body,
        name="xattn_fwd",
        grid=(T // tq,),
        in_specs=[pl.BlockSpec((tq, D_MODEL), lambda i: (i, 0)),
                  pl.BlockSpec((mem_len, 2 * D_MODEL), lambda i: (0, 0))],
        out_specs=pl.BlockSpec((tq, D_MODEL), lambda i: (i, 0)),
        out_shape=jax.ShapeDtypeStruct((T, D_MODEL), BF16),
        compiler_params=_params(("parallel",)),
    )(q, kv)


def _xattn_bwd(q, kv, d_o):
    T, mem_len = q.shape[0], kv.shape[0]
    tq = _pick(T, (512, 256, 128))

    def body(q_ref, kv_ref, do_ref, dq_ref, dkv_ref):
        @pl.when(pl.program_id(0) == 0)
        def _():
            dkv_ref[...] = jnp.zeros_like(dkv_ref)

        for h in range(MEM_HEADS):
            cols = slice(h * MEM_HEAD_DIM, (h + 1) * MEM_HEAD_DIM)
            vcols = slice(D_MODEL + h * MEM_HEAD_DIM, D_MODEL + (h + 1) * MEM_HEAD_DIM)
            q_h, k_h, do_h = q_ref[:, cols], kv_ref[:, cols], do_ref[:, cols]
            p = _mem_probs(q_h, k_h)
            dp = _dot(do_h, kv_ref[:, vcols], _NT)
            ds = p * (dp - jnp.sum(dp * p, axis=1, keepdims=True))
            dsb = (ds * (MEM_HEAD_DIM ** -0.5)).astype(BF16)
            dq_ref[:, cols] = _dot(dsb, k_h, _NN).astype(BF16)
            dkv_ref[:, cols] += _dot(dsb, q_h, _TN)
            dkv_ref[:, vcols] += _dot(p.astype(BF16), do_h, _TN)

    row = pl.BlockSpec((tq, D_MODEL), lambda i: (i, 0))
    full = pl.BlockSpec((mem_len, 2 * D_MODEL), lambda i: (0, 0))
    return pl.pallas_call(
        body,
        name="xattn_bwd",
        grid=(T // tq,),
        in_specs=[row, full, row],
        out_specs=[row, full],
        out_shape=[jax.ShapeDtypeStruct((T, D_MODEL), BF16), jax.ShapeDtypeStruct((mem_len, 2 * D_MODEL), F32)],
        compiler_params=_params(("arbitrary",)),
    )(q, kv, d_o)


def _swiglu_tiles(f):
    a, b = f[:, :FFN_HIDDEN], f[:, FFN_HIDDEN:]
    return f, a * jax.nn.sigmoid(a) * b


def _swiglu_grad_tiles(d_hidden, f):
    a, b = f[:, :FFN_HIDDEN], f[:, FFN_HIDDEN:]
    sig = jax.nn.sigmoid(a)
    return (jnp.concatenate([d_hidden * b * (sig * (1.0 + a * (1.0 - sig))), d_hidden * (a * sig)], axis=1),)


def _local_step(x, mem, w_in, small, target, fetch_rest, ship):
    T = x.shape[0]
    tables = _ret_tables(T)
    memb = mem.astype(BF16)

    h_a, h_b, h_c, h_d, h_e, xb = _mm_fused(
        x, w_in, mode="nn", name="proj_in", extras=[], pass_a=True,
        outs=[(1536, BF16), (1024, F32), (1024, BF16), (1024, F32), (2048, F32), (D_MODEL, BF16)],
        epilogue=_proj_tiles, max_rows=256)
    (a_sb, r_mat, _), w = fetch_rest(lambda rider: _sb_fwd(h_a, rider))
    y_gated, o_pre, states = _ret_fwd(h_b, h_c, h_d, tables)
    y_sb = _mm(a_sb, w["w_sb_o"], mode="nn", out_dtype=F32, name="sb_out")
    row_f32, row_bf16 = (D_MODEL, F32), (D_MODEL, BF16)
    ln_outs = [row_f32, row_bf16, row_f32, (1, F32)]
    y_ret, mix_in = _mm_fused(y_gated, w["w_ret_o"], mode="nn", name="ret_out", extras=[h_e, small["b_gate"], y_sb],
                              outs=[row_f32, row_bf16], epilogue=_gate_mix_tiles)
    x1, x1b, xhat1, rstd1 = _mm_fused(mix_in, w["w_mix_o"], mode="nn", name="mix_out",
                                      extras=[x, small["ln1_g"], small["ln1_b"]], outs=ln_outs, epilogue=_ln_tiles)
    q_m = _mm(x1b, w["w_mem_q"], mode="nn", out_dtype=BF16, name="mem_q")
    kv_m = _mm(memb, w["w_mem_kv"], mode="nn", out_dtype=BF16, name="mem_kv")
    o_m = _xattn_fwd(q_m, kv_m)
    x2, x2b, xhat2, rstd2 = _mm_fused(o_m, w["w_mem_o"], mode="nn", name="mem_out",
                                      extras=[x1, small["ln2_g"], small["ln2_b"]], outs=ln_outs, epilogue=_ln_tiles)
    f, hidden = _mm_fused(x2b, w["w_ffn_in"], mode="nn", name="ffn_in", extras=[],
                          outs=[(2 * FFN_HIDDEN, F32), (FFN_HIDDEN, BF16)], epilogue=_swiglu_tiles)
    du_outs, col = [row_f32, row_bf16], D_MODEL
    du3, du3b, loss_cols, d_ln3_g, d_ln3_b = _mm_fused(
        hidden, w["w_ffn_out"], mode="nn", name="ffn_out", extras=[x2, small["ln3_g"], small["ln3_b"], target],
        outs=du_outs, sums=[col, col, col], epilogue=_ln_loss_tiles, max_rows=256)

    g_ffn_out = _mm(hidden, du3b, mode="tn", out_dtype=BF16, name="g_ffn_out")
    (d_f,) = _mm_fused(du3b, w["w_ffn_out"], mode="nt", name="d_hidden", extras=[f],
                       outs=[(2 * FFN_HIDDEN, BF16)], epilogue=_swiglu_grad_tiles)
    g_ffn_in = _mm(x2b, d_f, mode="tn", out_dtype=BF16, name="g_ffn_in")
    du2, du2b, d_ln2_g, d_ln2_b = ship(
        {"w_ffn_out": g_ffn_out},
        lambda rider: _as_host(rider, _mm_fused(
            d_f, w["w_ffn_in"], mode="nt", name="d_x2", extras=[du3, xhat2, rstd2, small["ln2_g"]], outs=du_outs,
            sums=[col, col], epilogue=_ln_grad_tiles, rider=rider, max_rows=256)))
    g_mem_o = _mm(o_m, du2b, mode="tn", out_dtype=BF16, name="g_mem_o")
    d_om = _mm(du2b, w["w_mem_o"], mode="nt", out_dtype=BF16, name="d_om")
    d_qm, d_kvm = _xattn_bwd(q_m, kv_m, d_om)
    g_mem_q = _mm(x1b, d_qm, mode="tn", out_dtype=BF16, name="g_mem_q")
    g_mem_kv = _mm(memb, d_kvm.astype(BF16), mode="tn", out_dtype=BF16, name="g_mem_kv")
    du1, du1b, d_ln1_g, d_ln1_b = _mm_fused(
        d_qm, w["w_mem_q"], mode="nt", name="d_x1", extras=[du2, xhat1, rstd1, small["ln1_g"]], outs=du_outs,
        sums=[col, col], epilogue=_ln_grad_tiles, max_rows=256)
    g_mix_o = _mm(mix_in, du1b, mode="tn", out_dtype=BF16, name="g_mix_o")
    d_ysb, d_yret, d_e, d_b_gate = _mm_fused(
        du1b, w["w_mix_o"], mode="nt", name="d_mix_in", extras=[h_e, small["b_gate"], y_sb, y_ret],
        outs=[row_bf16, row_bf16, (2 * D_MODEL, BF16)], sums=[2 * D_MODEL], epilogue=_gate_mix_grad_tiles,
        max_rows=256)
    g_sb_o = _mm(a_sb, d_ysb, mode="tn", out_dtype=BF16, name="g_sb_o")
    g_ret_o = _mm(y_gated, d_yret, mode="tn", out_dtype=BF16, name="g_ret_o")
    d_asb = _mm(d_ysb, w["w_sb_o"], mode="nt", out_dtype=BF16, name="d_asb")
    d_ygated = _mm(d_yret, w["w_ret_o"], mode="nt", out_dtype=F32, name="d_ygated")
    small_grads = {"b_gate": d_b_gate, "ln1_g": d_ln1_g, "ln1_b": d_ln1_b, "ln2_g": d_ln2_g, "ln2_b": d_ln2_b,
                   "ln3_g": d_ln3_g, "ln3_b": d_ln3_b, "loss_cols": loss_cols}
    d_rq, d_rk, d_c, d_d, _ = _ret_bwd(d_ygated, o_pre, states, h_b, h_c, h_d, tables)
    ready = {"w_ffn_in": g_ffn_in, "w_mem_kv": g_mem_kv, "w_mem_q": g_mem_q, "w_mem_o": g_mem_o, "w_mix_o": g_mix_o,
             "w_ret_o": g_ret_o, "w_sb_o": g_sb_o, "small": small_grads}
    d_q, d_k, d_v = ship(ready, lambda rider: _sb_bwd(h_a, d_asb, r_mat, rider))
    d_h = [("sb_q", d_q), ("sb_k", d_k), ("sb_v", d_v), ("ret_q", d_rq), ("ret_k", d_rk), ("ret_v", d_c),
           ("ret_g", d_d), ("gate", d_e)]
    g_in = jnp.concatenate([_mm(xb, piece, mode="tn", out_dtype=BF16, name="g_in_" + tag) for tag, piece in d_h],
                           axis=1)
    (d_x,) = ship({"w_in": g_in},
                  lambda rider: _as_host(rider, _mm_fused(
                      [piece for _, piece in d_h], w_in, mode="nt", name="d_x", extras=[du1], outs=[(D_MODEL, F32)],
                      epilogue=_residual_tiles, rider=rider, max_rows=256)))
    return d_x


def _adamw_math(w, g, m, v):
    m = ADAM_B1 * m + (1.0 - ADAM_B1) * g
    v = ADAM_B2 * v + (1.0 - ADAM_B2) * jnp.square(g)
    m_hat = m / (1.0 - ADAM_B1 ** ADAM_STEP)
    v_hat = v / (1.0 - ADAM_B2 ** ADAM_STEP)
    delta = -ADAM_LR * (m_hat / (jnp.sqrt(v_hat) + ADAM_EPS) + ADAM_WD * w)
    return delta, m, v


def _adamw(parts, w, m, v, name):
    R, C = w.shape
    tr = max(t for t in range(16, min(R, 256) + 1, 16) if R % t == 0) if R >= 16 else R

    def body(p_ref, w_ref, m_ref, v_ref, g_ref, d_ref, nm_ref, nv_ref):
        g = p_ref[0].astype(F32)
        for j in range(1, N_DEV):
            g = g + p_ref[j].astype(F32)
        delta, nm, nv = _adamw_math(w_ref[...], g, m_ref[...], v_ref[...])
        g_ref[...] = g
        d_ref[...] = delta
        nm_ref[...] = nm
        nv_ref[...] = nv

    blk = pl.BlockSpec((tr, C), lambda i: (i, 0))
    out = jax.ShapeDtypeStruct((R, C), F32)
    return pl.pallas_call(
        body,
        name=name,
        grid=(R // tr,),
        in_specs=[pl.BlockSpec((N_DEV, tr, C), lambda i: (0, i, 0)), blk, blk, blk],
        out_specs=[blk] * 4,
        out_shape=[out] * 4,
        compiler_params=_params(("parallel",)),
    )(parts, w, m, v)


_SHARD_AXIS = {"w_in": 1, "w_sb_o": 1, "w_ret_o": 0, "w_mix_o": 0, "w_mem_q": 0, "w_mem_kv": 1, "w_mem_o": 0,
               "w_ffn_in": 1, "w_ffn_out": 0}
_MATRICES = tuple(_SHARD_AXIS)
_SMALL = ("b_gate", "ln1_g", "ln1_b", "ln2_g", "ln2_b", "ln3_g", "ln3_b")
_WEIGHT_ORDER = ("w_in", "b_gate", "w_sb_o", "w_ret_o", "w_mix_o", "ln1_g", "ln1_b", "w_mem_q", "w_mem_kv", "w_mem_o",
                 "ln2_g", "ln2_b", "w_ffn_in", "w_ffn_out", "ln3_g", "ln3_b")


def _assemble(name, gathered):
    if _SHARD_AXIS[name] == 0:
        return gathered.reshape(-1, gathered.shape[2])
    return jnp.transpose(gathered, (1, 0, 2)).reshape(gathered.shape[1], -1)


def _to_slots(name, full):
    if _SHARD_AXIS[name] == 0:
        return full.reshape(N_DEV, full.shape[0] // N_DEV, full.shape[1])
    return jnp.transpose(full.reshape(full.shape[0], N_DEV, full.shape[1] // N_DEV), (1, 0, 2))


SMALL_ROWS = 16


def _pack_small(vals):
    return jnp.concatenate([vals["b_gate"].reshape(2, D_MODEL)] + [vals[n] for n in _SMALL[1:]], axis=0)


def _unpack_small(packed):
    out = {"b_gate": packed[0:2].reshape(1, 2 * D_MODEL)}
    for i, n in enumerate(_SMALL[1:]):
        out[n] = packed[2 + i:3 + i]
    return out


def kernel(x, mem, w_in, b_gate, w_sb_o, w_ret_o, w_mix_o, ln1_g, ln1_b, w_mem_q, w_mem_kv, w_mem_o, ln2_g, ln2_b, w_ffn_in, w_ffn_out, ln3_g, ln3_b, loss_target, m_w_in, m_b_gate, m_w_sb_o, m_w_ret_o, m_w_mix_o, m_ln1_g, m_ln1_b, m_w_mem_q, m_w_mem_kv, m_w_mem_o, m_ln2_g, m_ln2_b, m_w_ffn_in, m_w_ffn_out, m_ln3_g, m_ln3_b, v_w_in, v_b_gate, v_w_sb_o, v_w_ret_o, v_w_mix_o, v_ln1_g, v_ln1_b, v_w_mem_q, v_w_mem_kv, v_w_mem_o, v_ln2_g, v_ln2_b, v_w_ffn_in, v_w_ffn_out, v_ln3_g, v_ln3_b):
    weights = dict(w_in=w_in, b_gate=b_gate, w_sb_o=w_sb_o, w_ret_o=w_ret_o, w_mix_o=w_mix_o, ln1_g=ln1_g, ln1_b=ln1_b,
                   w_mem_q=w_mem_q, w_mem_kv=w_mem_kv, w_mem_o=w_mem_o, ln2_g=ln2_g, ln2_b=ln2_b, w_ffn_in=w_ffn_in,
                   w_ffn_out=w_ffn_out, ln3_g=ln3_g, ln3_b=ln3_b)
    mom1 = dict(w_in=m_w_in, b_gate=m_b_gate, w_sb_o=m_w_sb_o, w_ret_o=m_w_ret_o, w_mix_o=m_w_mix_o, ln1_g=m_ln1_g,
                ln1_b=m_ln1_b, w_mem_q=m_w_mem_q, w_mem_kv=m_w_mem_kv, w_mem_o=m_w_mem_o, ln2_g=m_ln2_g, ln2_b=m_ln2_b,
                w_ffn_in=m_w_ffn_in, w_ffn_out=m_w_ffn_out, ln3_g=m_ln3_g, ln3_b=m_ln3_b)
    mom2 = dict(w_in=v_w_in, b_gate=v_b_gate, w_sb_o=v_w_sb_o, w_ret_o=v_w_ret_o, w_mix_o=v_w_mix_o, ln1_g=v_ln1_g,
                ln1_b=v_ln1_b, w_mem_q=v_w_mem_q, w_mem_kv=v_w_mem_kv, w_mem_o=v_w_mem_o, ln2_g=v_ln2_g, ln2_b=v_ln2_b,
                w_ffn_in=v_w_ffn_in, w_ffn_out=v_w_ffn_out, ln3_g=v_ln3_g, ln3_b=v_ln3_b)

    (gathered_in,) = _exchange([weights["w_in"][0].astype(BF16)], False, "gather_w_in")
    rest = [n for n in _MATRICES if n != "w_in"]
    received = {}

    def fetch_rest(host):
        res = host(_Rider([weights[n][0].astype(BF16) for n in rest], False))
        return res, {n: _assemble(n, g) for n, g in zip(rest, res[-1])}

    def ship(grads, host):
        names = list(grads)
        bufs = []
        for n in names:
            if n == "small":
                part = jnp.concatenate([_pack_small(grads[n]), grads[n]["loss_cols"],
                                        jnp.zeros((SMALL_ROWS - 9, D_MODEL), F32)], axis=0)
                bufs.append(jnp.broadcast_to(part[None], (N_DEV,) + part.shape))
            else:
                bufs.append(_to_slots(n, grads[n]).astype(BF16))
        res = host(_Rider(bufs, True))
        received.update(zip(names, res[-1]))
        return res[:-1]

    small = {n: weights[n] for n in _SMALL}
    d_x = _local_step(x[0], mem[0], _assemble("w_in", gathered_in), small, loss_target[0], fetch_rest, ship)

    new = {}
    for n in _MATRICES:
        new[n] = _adamw(received[n], weights[n][0], mom1[n][0], mom2[n][0], "adamw_" + n)
    packed = _adamw(received["small"][:, :8], _pack_small({n: weights[n] for n in _SMALL}),
                    _pack_small({n: mom1[n] for n in _SMALL}), _pack_small({n: mom2[n] for n in _SMALL}), "adamw_small")
    small_new = [_unpack_small(p) for p in packed]
    loss = jnp.sum(received["small"][:, 8]) * (0.5 / D_MODEL)

    outs = [loss, d_x[None]]
    for slot in range(4):
        for n in _WEIGHT_ORDER:
            outs.append(new[n][slot][None] if n in new else small_new[slot][n])
    return tuple(outs)
```

```python
import functools
import math

import jax
import jax.numpy as jnp
from jax import lax
from jax.experimental import pallas as pl
from jax.experimental.pallas import tpu as pltpu

F32 = jnp.float32
BF16 = jnp.bfloat16

N_DEV = 8
D_MODEL = 1024
SB_HEAD_DIM = 64
SB_WIDTH = 512
RET_HEADS = 4
RET_QK_DIM = 128
RET_V_DIM = 256
RET_QK_WIDTH = 512
RET_V_WIDTH = 1024
RET_CHUNK = 128
RET_STEP_CHUNKS = 2
ROPE_BASE = 10000.0
MEM_HEADS = 4
MEM_HEAD_DIM = 256
FFN_HIDDEN = 2816
DN_ALPHA = 2.0 ** 0.25
LN_EPS = 1e-5
ADAM_LR = 0.001
ADAM_B1 = 0.9
ADAM_B2 = 0.999
ADAM_EPS = 1e-08
ADAM_WD = 0.01
ADAM_STEP = 10

VMEM_LIMIT_BYTES = 52 * 1024 * 1024
LANES = 128
SB_KEY_BLOCK = 128
SB_Q_BLOCK = 256
SB_DEAD_LOG = -105.0

MESH_AXES = ("x", "y", "c")


def _pick(dim, prefs):
    for p in prefs:
        if dim % p == 0:
            return p
    return dim


def _params(sem):
    return pltpu.CompilerParams(dimension_semantics=sem, vmem_limit_bytes=VMEM_LIMIT_BYTES)


def _dot(a, b, dims):
    return lax.dot_general(a, b, (dims, ((), ())), preferred_element_type=F32)


_NN = ((1,), (0,))
_NT = ((1,), (1,))
_TN = ((0,), (0,))


def _my_index():
    return 4 * lax.axis_index("x") + 2 * lax.axis_index("y") + lax.axis_index("c")


def _peer(k):
    x, y, c = lax.axis_index("x"), lax.axis_index("y"), lax.axis_index("c")
    bx, by, bc = (k >> 2) & 1, (k >> 1) & 1, k & 1
    px = (1 - x) if bx else x
    py = (1 - y) if by else y
    pc = (1 - c) if bc else c
    return (px, py, pc), 4 * px + 2 * py + pc


class _Rider:
    def __init__(self, bufs, scatter):
        self.bufs, self.scatter, self.n = list(bufs), scatter, len(bufs)
        self.specs = [pl.BlockSpec(memory_space=pl.ANY)] * self.n
        self.out_shape = [jax.ShapeDtypeStruct(b.shape if scatter else (N_DEV,) + b.shape, b.dtype) for b in self.bufs]
        self.scratch = [pltpu.SemaphoreType.DMA((self.n, N_DEV - 1)), pltpu.SemaphoreType.DMA((self.n, N_DEV - 1)),
                        pltpu.SemaphoreType.DMA((self.n,))]

    def _remote(self, ride, a, k, src_ref, slot, to):
        _, dst, (send_sems, recv_sems, _) = ride
        return pltpu.make_async_remote_copy(src_ref=src_ref, dst_ref=dst[a].at[slot], send_sem=send_sems.at[a, k],
                                            recv_sem=recv_sems.at[a, k], device_id=to,
                                            device_id_type=pl.DeviceIdType.MESH)

    def _local(self, ride, a):
        src, dst, (_, _, local_sems) = ride
        me = _my_index()
        return pltpu.make_async_copy(src[a].at[me] if self.scatter else src[a], dst[a].at[me], local_sems.at[a])

    def _direct(self, ride, a):
        src = ride[0]
        me = _my_index()
        out = []
        for k in range(1, N_DEV):
            peer, peer_idx = _peer(k)
            out.append(self._remote(ride, a, k - 1, src[a].at[peer_idx], me, peer))
        return out

    def _two_level(self, ride, a):
        src, dst = ride[0], ride[1]
        x, y, c = lax.axis_index("x"), lax.axis_index("y"), lax.axis_index("c")
        me, sibling = _my_index(), (x, y, 1 - c)
        chips = [(1 - x, y), (x, 1 - y), (1 - x, 1 - y)]
        first = [self._remote(ride, a, 0, src[a], me, sibling)]
        passed, landing = [], [self._remote(ride, a, 0, src[a], me + 1 - 2 * c, sibling)]
        for j, (px, py) in enumerate(chips):
            first.append(self._remote(ride, a, 1 + j, src[a], me, (px, py, c)))
            theirs = 4 * px + 2 * py + c
            passed.append(self._remote(ride, a, 4 + j, dst[a].at[theirs], theirs, sibling))
            landing.append(self._remote(ride, a, 1 + j, src[a], theirs, (px, py, c)))
        for j, (px, py) in enumerate(chips):
            landing.append(self._remote(ride, a, 4 + j, src[a], 4 * px + 2 * py + 1 - c, sibling))
        return first, passed, landing

    def start(self, ride):
        for a in range(self.n):
            self._local(ride, a).start()
            for cp in (self._direct(ride, a) if self.scatter else self._two_level(ride, a)[0]):
                cp.start()

    def finish(self, ride):
        if self.scatter:
            for a in range(self.n):
                for cp in self._direct(ride, a):
                    cp.wait()
                self._local(ride, a).wait()
            return
        levels = [self._two_level(ride, a) for a in range(self.n)]
        for first, passed, landing in levels:
            for j, cp in enumerate(passed):
                landing[1 + j].wait_recv()
                cp.start()
        for a, (first, passed, landing) in enumerate(levels):
            landing[0].wait_recv()
            for cp in landing[4:]:
                cp.wait_recv()
            for cp in first + passed:
                cp.wait_send()
            self._local(ride, a).wait()

    def start_at_first(self, ids, ride):
        first = functools.reduce(jnp.logical_and, [i == 0 for i in ids])

        @pl.when(first)
        def _():
            self.start(ride)

    def wait_at_last(self, ids, grid, ride):
        last = functools.reduce(jnp.logical_and, [i == g - 1 for i, g in zip(ids, grid)])

        @pl.when(last)
        def _():
            self.finish(ride)


def _exchange(bufs, scatter, name):
    rider = _Rider(bufs, scatter)

    def body(*refs):
        ride = (refs[:rider.n], refs[rider.n:2 * rider.n], refs[2 * rider.n:])
        rider.start(ride)
        rider.finish(ride)

    return pl.pallas_call(
        body,
        name=name,
        in_specs=rider.specs,
        out_specs=rider.specs,
        out_shape=rider.out_shape,
        scratch_shapes=rider.scratch,
    )(*rider.bufs)


MM_RESIDENT_B_BYTES = 14 * 1024 * 1024
MM_A_TILE_BYTES = 4 * 1024 * 1024
MM_OUT_TILE_BYTES = 6 * 1024 * 1024


def _mm_tiles(mode, M, N, K, a_bytes, out_bytes):
    if mode != "tn" and K * N * 2 <= MM_RESIDENT_B_BYTES:
        for tm in (1024, 512, 256, 128):
            if M % tm == 0 and tm * K * a_bytes <= MM_A_TILE_BYTES and tm * N * out_bytes <= MM_OUT_TILE_BYTES:
                return tm, N, K
    if mode == "tn":
        return (_pick(M, (1024, 1408, 512, 256, 128)), _pick(N, (1024, 1664, 1408, 512, 256, 128)),
                _pick(K, (2048, 1024, 512, 256, 128)))
    return _pick(M, (1024, 512, 256, 128)), _pick(N, (512, 256, 128)), _pick(K, (1024, 512, 256, 128))


def _mm(a, b, *, mode, out_dtype, name, res=None, res_scale=1.0, rider=None):
    if mode == "nn":
        (M, K), (K2, N) = a.shape, b.shape
    elif mode == "nt":
        (M, K), (N, K2) = a.shape, b.shape
    else:
        (K, M), (K2, N) = a.shape, b.shape
    assert K == K2, (a.shape, b.shape, mode)
    out_bytes = jnp.dtype(out_dtype).itemsize + (4 if res is not None else 0)
    tm, tn, tk = _mm_tiles(mode, M, N, K, a.dtype.itemsize, out_bytes)
    grid = (M // tm, N // tn, K // tk)
    nk = grid[2]
    dims = {"nn": _NN, "nt": _NT, "tn": _TN}[mode]
    n_in = 2 + (res is not None)
    n_ride = rider.n if rider is not None else 0

    def body(*refs):
        a_ref, b_ref = refs[:2]
        r_ref = refs[2] if res is not None else None
        o_ref = refs[n_in + n_ride]
        rest = refs[n_in + 2 * n_ride + 1:]
        acc_ref = rest[0] if nk > 1 else None
        ids = [pl.program_id(d) for d in range(3)]
        if rider is not None:
            ride = (refs[n_in:n_in + n_ride], refs[n_in + n_ride + 1:n_in + 2 * n_ride + 1], rest[-3:])
            rider.start_at_first(ids, ride)
        part = _dot(a_ref[...].astype(BF16), b_ref[...].astype(BF16), dims)

        def finish(total):
            if r_ref is not None:
                total = total + res_scale * r_ref[...]
            o_ref[...] = total.astype(out_dtype)

        if nk == 1:
            finish(part)
        else:
            k = ids[2]

            @pl.when(k == 0)
            def _():
                acc_ref[...] = part

            @pl.when(k > 0)
            def _():
                acc_ref[...] += part

            @pl.when(k == nk - 1)
            def _():
                finish(acc_ref[...])

        if rider is not None:
            rider.wait_at_last(ids, grid, ride)

    if mode == "nn":
        a_spec = pl.BlockSpec((tm, tk), lambda i, j, k: (i, k))
        b_spec = pl.BlockSpec((tk, tn), lambda i, j, k: (k, j))
    elif mode == "nt":
        a_spec = pl.BlockSpec((tm, tk), lambda i, j, k: (i, k))
        b_spec = pl.BlockSpec((tn, tk), lambda i, j, k: (j, k))
    else:
        a_spec = pl.BlockSpec((tk, tm), lambda i, j, k: (k, i))
        b_spec = pl.BlockSpec((tk, tn), lambda i, j, k: (k, j))
    o_spec = pl.BlockSpec((tm, tn), lambda i, j, k: (i, j))
    in_specs = [a_spec, b_spec] + ([o_spec] if res is not None else [])
    args = (a, b) + ((res,) if res is not None else ())
    out_specs, out_shape = [o_spec], [jax.ShapeDtypeStruct((M, N), out_dtype)]
    scratch = [pltpu.VMEM((tm, tn), F32)] if nk > 1 else []
    sem = ("parallel", "parallel", "arbitrary")
    if rider is not None:
        in_specs, args = in_specs + rider.specs, args + tuple(rider.bufs)
        out_specs, out_shape = out_specs + rider.specs, out_shape + rider.out_shape
        scratch = scratch + rider.scratch
        sem = ("arbitrary",) * 3
    outs = pl.pallas_call(
        body,
        name=name,
        grid=grid,
        in_specs=in_specs,
        out_specs=out_specs,
        out_shape=out_shape,
        scratch_shapes=scratch,
        compiler_params=_params(sem),
    )(*args)
    return outs[0] if rider is None else (outs[0], list(outs[1:]))


def _mm_host(a, b, *, rider, **kw):
    out = _mm(a, b, rider=rider, **kw)
    return out if rider is not None else (out, [])


def _as_host(rider, results):
    return results if rider is not None else tuple(results) + ([],)


MM_FUSED_MARGIN_BYTES = 10 * 1024 * 1024
MM_FUSED_MAX_ROWS = 512


def _col_sum_update(acc_ref, val, first):
    part = jnp.sum(val.reshape(val.shape[0] // 8, 8, val.shape[1]), axis=0)

    @pl.when(first)
    def _():
        acc_ref[...] = part

    @pl.when(jnp.logical_not(first))
    def _():
        acc_ref[...] += part


def _mm_fused(a, b, *, mode, name, extras, outs, epilogue, sums=(), rider=None, max_rows=MM_FUSED_MAX_ROWS,
              pass_a=False):
    parts = list(a) if isinstance(a, (list, tuple)) else [a]
    M, K = parts[0].shape[0], sum(p.shape[1] for p in parts)
    if mode == "nn":
        (K2, N), b_dims = b.shape, _NN
    else:
        (N, K2), b_dims = b.shape, _NT
    assert K == K2, (K, b.shape, mode)
    rows = parts + [e for e in extras if e.shape[0] == M]
    per_row = 2 * (sum(e.shape[1] * e.dtype.itemsize for e in rows)
                   + sum(c * jnp.dtype(d).itemsize for c, d in outs)) + 2 * N * 4
    budget = VMEM_LIMIT_BYTES - K * N * 2 - MM_FUSED_MARGIN_BYTES
    tm = next(t for t in (512, 256, 128, 64, 32, 16) if t <= max_rows and M % t == 0 and t * per_row <= budget)
    steps = M // tm
    n_a, n_x, n_o, n_s = len(parts), len(extras), len(outs), len(sums)
    n_ride = rider.n if rider is not None else 0

    def body(*refs):
        a_refs, b_ref = refs[:n_a], refs[n_a]
        x_refs = refs[n_a + 1:n_a + 1 + n_x]
        base = n_a + 1 + n_x + n_ride
        o_refs, s_refs = refs[base:base + n_o], refs[base + n_o:base + n_o + n_s]
        acc_refs = refs[base + n_o + n_s + n_ride:base + n_o + 2 * n_s + n_ride]
        ids = [pl.program_id(0)]
        if rider is not None:
            ride = (refs[n_a + 1 + n_x:base], refs[base + n_o + n_s:base + n_o + n_s + n_ride], refs[-3:])
            rider.start_at_first(ids, ride)
        a_tile = a_refs[0][...]
        a_bf16 = a_tile.astype(BF16) if n_a == 1 else jnp.concatenate([r[...].astype(BF16) for r in a_refs], axis=1)
        prod = _dot(a_bf16, b_ref[...], b_dims)
        tiles = epilogue(prod, *([a_tile] if pass_a else []), *[r[...] for r in x_refs])
        for o_ref, t in zip(o_refs, tiles[:n_o]):
            o_ref[...] = t.astype(o_ref.dtype)
        for acc_ref, t in zip(acc_refs, tiles[n_o:]):
            _col_sum_update(acc_ref, t, ids[0] == 0)
        if n_s:
            @pl.when(ids[0] == steps - 1)
            def _():
                for s_ref, acc_ref in zip(s_refs, acc_refs):
                    s_ref[...] = jnp.sum(acc_ref[...], axis=0, keepdims=True)
        if rider is not None:
            rider.wait_at_last(ids, (steps,), ride)

    in_specs = [pl.BlockSpec((tm, p.shape[1]), lambda i: (i, 0)) for p in parts]
    in_specs.append(pl.BlockSpec(b.shape, lambda i: (0, 0), pipeline_mode=pl.Buffered(1)))
    for e in extras:
        in_specs.append(pl.BlockSpec((tm, e.shape[1]), lambda i: (i, 0)) if e.shape[0] == M
                        else pl.BlockSpec(e.shape, lambda i: (0, 0)))
    out_specs = ([pl.BlockSpec((tm, c), lambda i: (i, 0)) for c, _ in outs]
                 + [pl.BlockSpec((1, c), lambda i: (0, 0)) for c in sums])
    out_shape = ([jax.ShapeDtypeStruct((M, c), d) for c, d in outs]
                 + [jax.ShapeDtypeStruct((1, c), F32) for c in sums])
    args = tuple(parts) + (b,) + tuple(extras)
    scratch = [pltpu.VMEM((8, c), F32) for c in sums]
    if rider is not None:
        in_specs, args = in_specs + rider.specs, args + tuple(rider.bufs)
        out_specs, out_shape = out_specs + rider.specs, out_shape + rider.out_shape
        scratch = scratch + rider.scratch
    res = pl.pallas_call(
        body,
        name=name,
        grid=(steps,),
        in_specs=in_specs,
        out_specs=out_specs,
        out_shape=out_shape,
        scratch_shapes=scratch,
        compiler_params=_params(("arbitrary",) if (n_s or rider is not None) else ("parallel",)),
    )(*args)
    return tuple(res[:n_o + n_s]) + ((list(res[n_o + n_s:]),) if rider is not None else ())


def _pair_rows(blk, lane_is_a):
    zero = jnp.zeros_like(blk)
    return jnp.concatenate([jnp.where(lane_is_a, blk, zero), jnp.where(lane_is_a, zero, blk)], axis=0)


SB_STRIP = 32
SB_FWD_PAIRS = 4
SB_BWD_PAIRS = 2
SB_GROUP = 2


def _pair_lanes(p):
    return slice(p * LANES, (p + 1) * LANES)


def _sb_scan_matrices():
    o = lax.broadcasted_iota(jnp.int32, (2 * LANES, 4 * LANES), 0)
    c = lax.broadcasted_iota(jnp.int32, (2 * LANES, 4 * LANES), 1) & (2 * LANES - 1)
    same = (o >= LANES) == (c >= LANES)
    oo, cc = o & (LANES - 1), c & (LANES - 1)
    return (jnp.where(same & (cc > oo), 1.0, 0.0).astype(BF16), jnp.where(same & (cc < oo), 1.0, 0.0).astype(BF16))


def _sb_causal_masks(tq):
    d = lax.broadcasted_iota(jnp.int32, (tq // SB_KEY_BLOCK, SB_KEY_BLOCK, tq), 0)
    k = lax.broadcasted_iota(jnp.int32, (tq // SB_KEY_BLOCK, SB_KEY_BLOCK, tq), 1)
    t = lax.broadcasted_iota(jnp.int32, (tq // SB_KEY_BLOCK, SB_KEY_BLOCK, tq), 2)
    return jnp.where(d * SB_KEY_BLOCK + k < t, 1.0, 0.0).astype(F32)


def _sb_log_terms(z):
    log_rem = -jnp.maximum(z, 0.0) - jnp.log(1.0 + jnp.exp(-jnp.abs(z)))
    return log_rem, log_rem + z


def _sb_store_split(ref, strip, val, cols):
    hi = val.astype(BF16)
    ref[pl.ds(strip * SB_STRIP, SB_STRIP), cols] = hi
    ref[pl.ds(2 * LANES + strip * SB_STRIP, SB_STRIP), cols] = (val - hi.astype(F32)).astype(BF16)


def _sb_lanes(tq, diag):
    if diag == "left":
        return 0, tq // 2
    first = 0 if diag is None else diag * SB_KEY_BLOCK
    return first, tq - first


def _lane_add(full, part, lanes):
    first, width = lanes
    pieces = [full[:, :first]] if first else []
    pieces.append(full[:, first:first + width] + part)
    if first + width < full.shape[1]:
        pieces.append(full[:, first + width:])
    return pieces[0] if len(pieces) == 1 else jnp.concatenate(pieces, axis=1)


def _sb_fwd(h_a, rider=None):
    assert SB_FWD_PAIRS == 4
    T = h_a.shape[0]
    tq = _pick(T, (SB_Q_BLOCK, SB_KEY_BLOCK))
    nq, per_q, nkb = T // tq, tq // SB_KEY_BLOCK, T // SB_KEY_BLOCK
    assert per_q % SB_GROUP == 0
    n_strips = 2 * LANES // SB_STRIP
    n_ride = rider.n if rider is not None else 0
    after_m, _ = _sb_scan_matrices()
    causal_m = _sb_causal_masks(tq)
    pairs = SB_FWD_PAIRS

    def body(*refs):
        q_ref, k_ref, v_ref, after_ref, causal_ref = refs[:5]
        a_ref, r_ref, n_ref = refs[5 + n_ride:8 + n_ride]
        z_ref, lb_ref, split_ref, w_ref = refs[8 + 2 * n_ride:12 + 2 * n_ride]
        ids = [pl.program_id(0)]
        if rider is not None:
            ride = (refs[5:5 + n_ride], refs[8 + n_ride:8 + 2 * n_ride], refs[-3:])
            rider.start_at_first(ids, ride)
        i = ids[0]
        q_t = [(q_ref[:, _pair_lanes(p)].astype(F32).T * (SB_HEAD_DIM ** -0.5)).astype(BF16) for p in range(pairs)]
        lane_is_a = lax.broadcasted_iota(jnp.int32, (SB_KEY_BLOCK, LANES), 1) < SB_HEAD_DIM

        def tiles(kbs, diags, carry):
            nb = len(kbs)
            lanes = [_sb_lanes(tq, d) for d in diags]
            cols = [slice(first, first + width) for first, width in lanes]
            acc_t, ra, rb = [list(c) for c in carry]
            ks = [pl.multiple_of(kb * SB_KEY_BLOCK, SB_KEY_BLOCK) for kb in kbs]
            slot = lambda p, b: p * nb + b

            def causal(b, s):
                return causal_ref[diags[b], pl.ds((s * SB_STRIP) % SB_KEY_BLOCK, SB_STRIP), cols[b]]

            vv = {}
            for b in range(nb):
                for p in range(pairs):
                    kk = _pair_rows(k_ref[pl.ds(ks[b], SB_KEY_BLOCK), _pair_lanes(p)], lane_is_a)
                    vv[p, b] = _pair_rows(v_ref[pl.ds(ks[b], SB_KEY_BLOCK), _pair_lanes(p)], lane_is_a)
                    z_ref[slot(p, b), :, cols[b]] = _dot(kk, q_t[p][:, cols[b]], _NN)
            sums = {}
            for b in range(nb):
                for p in range(pairs):
                    part = [jnp.zeros((8, lanes[b][1]), F32), jnp.zeros((8, lanes[b][1]), F32)]
                    for s in range(n_strips):
                        rows = pl.ds(s * SB_STRIP, SB_STRIP)
                        log_rem, log_beta = _sb_log_terms(z_ref[slot(p, b), rows, cols[b]])
                        lb_ref[slot(p, b), rows, cols[b]] = log_beta
                        if isinstance(diags[b], int):
                            log_rem = log_rem * causal(b, s)
                        _sb_store_split(split_ref.at[slot(p, b)], s, log_rem, cols[b])
                        head = (s * SB_STRIP) // SB_KEY_BLOCK
                        part[head] = part[head] + jnp.sum(log_rem.reshape(SB_STRIP // 8, 8, lanes[b][1]), axis=0)
                    sums[p, b] = part
            for b in range(nb):
                for p in range(pairs):
                    z_ref[slot(p, b), :, cols[b]] = _dot(after_ref[...], split_ref[slot(p, b), :, cols[b]], _NN)
            for b in range(nb):
                for p in range(pairs):
                    for s in range(n_strips):
                        rows = pl.ds(s * SB_STRIP, SB_STRIP)
                        start = (ra[p] if (s * SB_STRIP) < SB_KEY_BLOCK else rb[p])[:, cols[b]]
                        w = jnp.exp(lb_ref[slot(p, b), rows, cols[b]] + z_ref[slot(p, b), rows, cols[b]] + start)
                        if isinstance(diags[b], int):
                            w = w * causal(b, s)
                        w_ref[slot(p, b), rows, cols[b]] = w.astype(BF16)
                    r_ref[2 * p, kbs[b]] = ra[p]
                    r_ref[2 * p + 1, kbs[b]] = rb[p]
                    ra[p] = _lane_add(ra[p], jnp.sum(sums[p, b][0], axis=0, keepdims=True), lanes[b])
                    rb[p] = _lane_add(rb[p], jnp.sum(sums[p, b][1], axis=0, keepdims=True), lanes[b])
            for b in range(nb):
                for p in range(pairs):
                    acc_t[p] = _lane_add(acc_t[p], _dot(vv[p, b], w_ref[slot(p, b), :, cols[b]], _TN), lanes[b])
            return tuple(acc_t), tuple(ra), tuple(rb)

        carry = (tuple(jnp.zeros((LANES, tq), F32) for _ in range(pairs)),
                 tuple(jnp.zeros((1, tq), F32) for _ in range(pairs)),
                 tuple(jnp.zeros((1, tq), F32) for _ in range(pairs)))
        own = list(reversed(range(per_q)))
        carry = tiles([i * per_q + d for d in own], own, carry)
        n_full = i * per_q

        def top_of(sums_a, sums_b, first):
            return jnp.max(functools.reduce(jnp.maximum, [r[:, first:] for r in sums_a + sums_b]))

        def alive(c):
            return jnp.logical_and(c[0] < n_full, top_of(c[2], c[3], 0) > SB_DEAD_LOG)

        def step(c):
            kbs = [n_full - 1 - c[0] - b for b in range(SB_GROUP)]
            return (c[0] + SB_GROUP,) + lax.cond(
                top_of(c[2], c[3], tq // 2) > SB_DEAD_LOG,
                lambda cc: tiles(kbs, [None] * SB_GROUP, cc), lambda cc: tiles(kbs, ["left"] * SB_GROUP, cc), c[1:])

        walked, acc_t, _, _ = lax.while_loop(alive, step, (jnp.int32(0),) + carry)
        for p in range(pairs):
            a_ref[:, _pair_lanes(p)] = acc_t[p].T.astype(BF16)
        n_ref[...] = jnp.zeros(n_ref.shape, F32) + walked.astype(F32)
        if rider is not None:
            rider.wait_at_last(ids, (nq,), ride)

    wide = pairs * LANES
    in_specs = [pl.BlockSpec((tq, wide), lambda i: (i, 0)),
                pl.BlockSpec((T, wide), lambda i: (0, 1), pipeline_mode=pl.Buffered(1)),
                pl.BlockSpec((T, wide), lambda i: (0, 2), pipeline_mode=pl.Buffered(1)),
                pl.BlockSpec(after_m.shape, lambda i: (0, 0), pipeline_mode=pl.Buffered(1)),
                pl.BlockSpec(causal_m.shape, lambda i: (0, 0, 0), pipeline_mode=pl.Buffered(1))]
    out_specs = [pl.BlockSpec((tq, wide), lambda i: (i, 0)),
                 pl.BlockSpec((2 * pairs, nkb, 1, tq), lambda i: (0, 0, 0, i)),
                 pl.BlockSpec((1, 8, LANES), lambda i: (i, 0, 0))]
    out_shape = [jax.ShapeDtypeStruct((T, SB_WIDTH), BF16), jax.ShapeDtypeStruct((2 * pairs, nkb, 1, T), F32),
                 jax.ShapeDtypeStruct((nq, 8, LANES), F32)]
    args = (h_a, h_a, h_a, after_m, causal_m)
    slots = pairs * max(per_q, SB_GROUP)
    scratch = [pltpu.VMEM((slots, 2 * LANES, tq), F32), pltpu.VMEM((slots, 2 * LANES, tq), F32),
               pltpu.VMEM((slots, 4 * LANES, tq), BF16), pltpu.VMEM((slots, 2 * LANES, tq), BF16)]
    if rider is not None:
        in_specs, args = in_specs + rider.specs, args + tuple(rider.bufs)
        out_specs, out_shape = out_specs + rider.specs, out_shape + rider.out_shape
        scratch = scratch + rider.scratch
    outs = pl.pallas_call(
        body,
        name="sb_fwd",
        grid=(nq,),
        in_specs=in_specs,
        out_specs=out_specs,
        out_shape=out_shape,
        scratch_shapes=scratch,
        compiler_params=_params(("arbitrary",)),
    )(*args)
    return outs[0], (outs[1], outs[2]), list(outs[3:])


def _sb_bwd(h_a, d_out, saved, rider=None):
    r_mat, walked_blocks = saved
    T = h_a.shape[0]
    tq = _pick(T, (SB_Q_BLOCK, SB_KEY_BLOCK))
    nq, per_q, nkb = T // tq, tq // SB_KEY_BLOCK, T // SB_KEY_BLOCK
    n_strips = 2 * LANES // SB_STRIP
    after_m, before_m = _sb_scan_matrices()
    causal_m = _sb_causal_masks(tq)
    pairs = SB_BWD_PAIRS
    groups = 4 // pairs
    n_ride = rider.n if rider is not None else 0

    def body(*refs):
        q_ref, k_ref, v_ref, do_ref, r_ref, n_ref, after_ref, before_ref, causal_ref = refs[:9]
        dq_ref, dk_ref, dv_ref = refs[9 + n_ride:12 + n_ride]
        z_ref, lb_ref, split_ref, w_ref, da_ref, dz_ref = refs[12 + 2 * n_ride:18 + 2 * n_ride]
        ids = [pl.program_id(0), pl.program_id(1)]
        if rider is not None:
            ride = (refs[9:9 + n_ride], refs[12 + n_ride:12 + 2 * n_ride], refs[-3:])
            rider.start_at_first(ids, ride)
        i = ids[1]

        @pl.when(i == 0)
        def _():
            dk_ref[...] = jnp.zeros_like(dk_ref)
            dv_ref[...] = jnp.zeros_like(dv_ref)

        scale = SB_HEAD_DIM ** -0.5
        q = [q_ref[:, _pair_lanes(p)] for p in range(pairs)]
        d_o = [do_ref[:, _pair_lanes(p)] for p in range(pairs)]
        q_t = [(x.astype(F32).T * scale).astype(BF16) for x in q]
        do_t = [x.astype(F32).T.astype(BF16) for x in d_o]
        lane_is_a = lax.broadcasted_iota(jnp.int32, (SB_KEY_BLOCK, LANES), 1) < SB_HEAD_DIM

        def tiles(kbs, diags, carry):
            nb = len(kbs)
            lanes = [_sb_lanes(tq, d) for d in diags]
            cols = [slice(first, first + width) for first, width in lanes]
            dq_t, ca, cb = [list(c) for c in carry]
            ks = [pl.multiple_of(kb * SB_KEY_BLOCK, SB_KEY_BLOCK) for kb in kbs]
            slot = lambda p, b: p * nb + b

            def causal(b, s):
                return causal_ref[diags[b], pl.ds((s * SB_STRIP) % SB_KEY_BLOCK, SB_STRIP), cols[b]]

            kk, vv = {}, {}
            for b in range(nb):
                for p in range(pairs):
                    kk[p, b] = _pair_rows(k_ref[pl.ds(ks[b], SB_KEY_BLOCK), _pair_lanes(p)], lane_is_a)
                    vv[p, b] = _pair_rows(v_ref[pl.ds(ks[b], SB_KEY_BLOCK), _pair_lanes(p)], lane_is_a)
                    z_ref[slot(p, b), :, cols[b]] = _dot(kk[p, b], q_t[p][:, cols[b]], _NN)
            for b in range(nb):
                for p in range(pairs):
                    for s in range(n_strips):
                        rows = pl.ds(s * SB_STRIP, SB_STRIP)
                        log_rem, log_beta = _sb_log_terms(z_ref[slot(p, b), rows, cols[b]])
                        lb_ref[slot(p, b), rows, cols[b]] = log_beta
                        if isinstance(diags[b], int):
                            log_rem = log_rem * causal(b, s)
                        _sb_store_split(split_ref.at[slot(p, b)], s, log_rem, cols[b])
            for b in range(nb):
                for p in range(pairs):
                    z_ref[slot(p, b), :, cols[b]] = _dot(after_ref[...], split_ref[slot(p, b), :, cols[b]], _NN)
                    da_ref[slot(p, b), :, cols[b]] = _dot(vv[p, b], do_t[p][:, cols[b]], _NN)
            sums = {}
            for b in range(nb):
                for p in range(pairs):
                    part = [jnp.zeros((8, lanes[b][1]), F32), jnp.zeros((8, lanes[b][1]), F32)]
                    for s in range(n_strips):
                        rows = pl.ds(s * SB_STRIP, SB_STRIP)
                        start = r_ref[2 * p + (s * SB_STRIP) // SB_KEY_BLOCK, kbs[b]][:, cols[b]]
                        w = jnp.exp(lb_ref[slot(p, b), rows, cols[b]] + z_ref[slot(p, b), rows, cols[b]] + start)
                        if isinstance(diags[b], int):
                            w = w * causal(b, s)
                        w_ref[slot(p, b), rows, cols[b]] = w.astype(BF16)
                        da = da_ref[slot(p, b), rows, cols[b]] * w
                        da_ref[slot(p, b), rows, cols[b]] = da
                        _sb_store_split(split_ref.at[slot(p, b)], s, da, cols[b])
                        head = (s * SB_STRIP) // SB_KEY_BLOCK
                        part[head] = part[head] + jnp.sum(da.reshape(SB_STRIP // 8, 8, lanes[b][1]), axis=0)
                    sums[p, b] = part
            for b in range(nb):
                for p in range(pairs):
                    z_ref[slot(p, b), :, cols[b]] = _dot(before_ref[...], split_ref[slot(p, b), :, cols[b]], _NN)
            for b in range(nb):
                for p in range(pairs):
                    for s in range(n_strips):
                        rows = pl.ds(s * SB_STRIP, SB_STRIP)
                        base = (ca[p] if (s * SB_STRIP) < SB_KEY_BLOCK else cb[p])[:, cols[b]]
                        sig = jnp.exp(lb_ref[slot(p, b), rows, cols[b]])
                        dz = (da_ref[slot(p, b), rows, cols[b]] * (1.0 - sig)
                              - (z_ref[slot(p, b), rows, cols[b]] + base) * sig)
                        if isinstance(diags[b], int):
                            dz = dz * causal(b, s)
                        dz_ref[slot(p, b), rows, cols[b]] = (dz * scale).astype(BF16)
                    ca[p] = _lane_add(ca[p], jnp.sum(sums[p, b][0], axis=0, keepdims=True), lanes[b])
                    cb[p] = _lane_add(cb[p], jnp.sum(sums[p, b][1], axis=0, keepdims=True), lanes[b])
            for b in range(nb):
                for p in range(pairs):
                    dq_t[p] = _lane_add(dq_t[p], _dot(kk[p, b], dz_ref[slot(p, b), :, cols[b]], _TN), lanes[b])
                    dkk = _dot(dz_ref[slot(p, b), :, cols[b]], q[p][cols[b], :], _NN)
                    dvv = _dot(w_ref[slot(p, b), :, cols[b]], d_o[p][cols[b], :], _NN)
                    here = (pl.ds(ks[b], SB_KEY_BLOCK), _pair_lanes(p))
                    dk_ref[here] += jnp.where(lane_is_a, dkk[:SB_KEY_BLOCK], dkk[SB_KEY_BLOCK:])
                    dv_ref[here] += jnp.where(lane_is_a, dvv[:SB_KEY_BLOCK], dvv[SB_KEY_BLOCK:])
            return tuple(dq_t), tuple(ca), tuple(cb)

        n_full = i * per_q
        groups_walked = jnp.clip(jnp.max(n_ref[...]).astype(jnp.int32), 0, n_full) // SB_GROUP
        carry = (tuple(jnp.zeros((LANES, tq), F32) for _ in range(pairs)),
                 tuple(jnp.zeros((1, tq), F32) for _ in range(pairs)),
                 tuple(jnp.zeros((1, tq), F32) for _ in range(pairs)))

        def below(j, c):
            kbs = [n_full - (groups_walked - j) * SB_GROUP + b for b in range(SB_GROUP)]
            starts = [r_ref[h, kbs[-1]][:, tq // 2:] for h in range(2 * pairs)]
            reaches = jnp.max(functools.reduce(jnp.maximum, starts)) > SB_DEAD_LOG
            return lax.cond(reaches, lambda cc: tiles(kbs, [None] * SB_GROUP, cc),
                            lambda cc: tiles(kbs, ["left"] * SB_GROUP, cc), c)

        carry = lax.fori_loop(0, groups_walked, below, carry)
        own = list(range(per_q))
        carry = tiles([i * per_q + d for d in own], own, carry)
        for p in range(pairs):
            dq_ref[:, _pair_lanes(p)] = carry[0][p].T.astype(BF16)
        if rider is not None:
            rider.wait_at_last(ids, (groups, nq), ride)

    wide = pairs * LANES
    mat = pl.BlockSpec(after_m.shape, lambda g, i: (0, 0), pipeline_mode=pl.Buffered(1))
    in_specs = [pl.BlockSpec((tq, wide), lambda g, i: (i, g)),
                pl.BlockSpec((T, wide), lambda g, i: (0, groups + g), pipeline_mode=pl.Buffered(1)),
                pl.BlockSpec((T, wide), lambda g, i: (0, 2 * groups + g), pipeline_mode=pl.Buffered(1)),
                pl.BlockSpec((tq, wide), lambda g, i: (i, g)),
                pl.BlockSpec((2 * pairs, nkb, 1, tq), lambda g, i: (g, 0, 0, i)),
                pl.BlockSpec((1, 8, LANES), lambda g, i: (i, 0, 0)),
                mat, mat,
                pl.BlockSpec(causal_m.shape, lambda g, i: (0, 0, 0), pipeline_mode=pl.Buffered(1))]
    out_specs = [pl.BlockSpec((tq, wide), lambda g, i: (i, g)),
                 pl.BlockSpec((T, wide), lambda g, i: (0, g)),
                 pl.BlockSpec((T, wide), lambda g, i: (0, g))]
    out_shape = [jax.ShapeDtypeStruct((T, SB_WIDTH), BF16), jax.ShapeDtypeStruct((T, SB_WIDTH), F32),
                 jax.ShapeDtypeStruct((T, SB_WIDTH), F32)]
    args = (h_a, h_a, h_a, d_out, r_mat, walked_blocks, after_m, before_m, causal_m)
    slots = pairs * max(per_q, SB_GROUP)
    scratch = [pltpu.VMEM((slots, 2 * LANES, tq), F32), pltpu.VMEM((slots, 2 * LANES, tq), F32),
               pltpu.VMEM((slots, 4 * LANES, tq), BF16), pltpu.VMEM((slots, 2 * LANES, tq), BF16),
               pltpu.VMEM((slots, 2 * LANES, tq), F32), pltpu.VMEM((slots, 2 * LANES, tq), BF16)]
    if rider is not None:
        in_specs, args = in_specs + rider.specs, args + tuple(rider.bufs)
        out_specs, out_shape = out_specs + rider.specs, out_shape + rider.out_shape
        scratch = scratch + rider.scratch
    outs = pl.pallas_call(
        body,
        name="sb_bwd",
        grid=(groups, nq),
        in_specs=in_specs,
        out_specs=out_specs,
        out_shape=out_shape,
        scratch_shapes=scratch,
        compiler_params=_params(("arbitrary", "arbitrary") if rider is not None else ("parallel", "arbitrary")),
    )(*args)
    return outs[0], outs[1], outs[2], list(outs[3:])


def _ret_tables(T):
    half = RET_QK_DIM // 2
    inv = 1.0 / (ROPE_BASE ** (jnp.arange(half, dtype=F32) / half))
    ang = jnp.arange(T, dtype=F32)[:, None] * inv[None, :]
    cos, sin = jnp.cos(ang), jnp.sin(ang)
    cos_t = jnp.concatenate([cos, cos], axis=1)
    sin_t = jnp.concatenate([-sin, sin], axis=1)
    log_gamma = jnp.log1p(-jnp.exp2(-5.0 - jnp.arange(RET_HEADS, dtype=F32)))
    idx = jnp.arange(RET_CHUNK, dtype=F32)
    rel = idx[:, None] - idx[None, :]
    decay = jnp.where(rel[None] >= 0, jnp.exp(log_gamma[:, None, None] * jnp.maximum(rel, 0.0)[None]), 0.0)
    k_decay = jnp.exp(log_gamma[None, :] * (RET_CHUNK - 1.0 - idx)[:, None])
    q_decay = jnp.exp(log_gamma[None, :] * (idx + 1.0)[:, None])
    chunk_decay = jnp.exp(log_gamma * RET_CHUNK)
    k_dec = jnp.broadcast_to(k_decay.T[:, :, None], (RET_HEADS, RET_CHUNK, LANES))
    q_dec = jnp.broadcast_to(q_decay.T[:, :, None], (RET_HEADS, RET_CHUNK, LANES))
    c_dec = jnp.broadcast_to(chunk_decay[:, None, None], (RET_HEADS, 8, LANES))
    return cos_t, sin_t, decay, k_dec, q_dec, c_dec


def _rotary(x, cos_t, sin_t):
    return x * cos_t + pltpu.roll(x, RET_QK_DIM // 2, 1) * sin_t


def _rotary_transpose(dy, cos_t, sin_t):
    return dy * cos_t + pltpu.roll(dy * sin_t, RET_QK_DIM // 2, 1)


def _head_norm(o):
    mu = jnp.mean(o, axis=1, keepdims=True)
    cen = o - mu
    var = jnp.mean(cen * cen, axis=1, keepdims=True)
    rstd = lax.rsqrt(var + LN_EPS)
    return cen * rstd, rstd


def _ret_specs(steps, per_step, reverse):
    def n_of(n):
        return (steps - 1 - n) if reverse else n

    rows = per_step * RET_CHUNK
    q_spec = pl.BlockSpec((rows, RET_QK_WIDTH), lambda n: (n_of(n), 0))
    k_spec = pl.BlockSpec((rows, RET_QK_WIDTH), lambda n: (n_of(n), 1))
    vv = pl.BlockSpec((rows, RET_V_WIDTH), lambda n: (n_of(n), 0))
    pos = pl.BlockSpec((rows, LANES), lambda n: (n_of(n), 0))
    per_head = pl.BlockSpec((RET_HEADS, RET_CHUNK, LANES), lambda n: (0, 0, 0))
    c_dec = pl.BlockSpec((RET_HEADS, 8, LANES), lambda n: (0, 0, 0))
    state = pl.BlockSpec((RET_HEADS, per_step, RET_QK_DIM, RET_V_DIM), lambda n: (0, n_of(n), 0, 0))
    return q_spec, k_spec, vv, pos, per_head, c_dec, state


def _qk_cols(h):
    return slice(h * RET_QK_DIM, (h + 1) * RET_QK_DIM)


def _v_cols(h):
    return slice(h * RET_V_DIM, (h + 1) * RET_V_DIM)


def _ret_fwd(h_b, h_c, h_d, tables):
    T = h_b.shape[0]
    nc = T // RET_CHUNK
    per_step = _pick(nc, (RET_STEP_CHUNKS, 1))
    steps = nc // per_step
    q_spec, k_spec, vv, pos, per_head, c_dec, state = _ret_specs(steps, per_step, False)

    def body(q_ref, k_ref, v_ref, g_ref, cos_ref, sin_ref, dec_ref, kd_ref, qd_ref, cd_ref,
             y_ref, o_ref, st_ref, state_ref):
        @pl.when(pl.program_id(0) == 0)
        def _():
            state_ref[...] = jnp.zeros_like(state_ref)

        for c in range(per_step):
            rows = pl.ds(c * RET_CHUNK, RET_CHUNK)
            cos_t, sin_t = cos_ref[rows, :], sin_ref[rows, :]
            for h in range(RET_HEADS):
                q = _rotary(q_ref[rows, _qk_cols(h)], cos_t, sin_t) * (RET_QK_DIM ** -0.5)
                k = _rotary(k_ref[rows, _qk_cols(h)], cos_t, sin_t)
                v = v_ref[rows, _v_cols(h)]
                prev = state_ref[h]
                scores = _dot(q.astype(BF16), k.astype(BF16), _NT) * dec_ref[h]
                inner = _dot(scores.astype(BF16), v, _NN)
                cross = _dot((q * qd_ref[h]).astype(BF16), prev.astype(BF16), _NN)
                o = inner + cross
                st_ref[h, c] = prev
                kv = _dot((k * kd_ref[h]).astype(BF16), v, _TN)
                state_ref[h] = prev * cd_ref[h, 0:1, 0:1] + kv
                o_ref[rows, _v_cols(h)] = o
                normed, _ = _head_norm(o)
                gate = g_ref[rows, _v_cols(h)]
                y_ref[rows, _v_cols(h)] = (gate * jax.nn.sigmoid(gate) * normed).astype(BF16)

    return pl.pallas_call(
        body,
        name="ret_fwd",
        grid=(steps,),
        in_specs=[q_spec, k_spec, vv, vv, pos, pos, per_head, per_head, per_head, c_dec],
        out_specs=[vv, vv, state],
        out_shape=[jax.ShapeDtypeStruct((T, RET_V_WIDTH), BF16),
                   jax.ShapeDtypeStruct((T, RET_V_WIDTH), F32),
                   jax.ShapeDtypeStruct((RET_HEADS, nc, RET_QK_DIM, RET_V_DIM), F32)],
        scratch_shapes=[pltpu.VMEM((RET_HEADS, RET_QK_DIM, RET_V_DIM), F32)],
        compiler_params=_params(("arbitrary",)),
    )(h_b, h_b, h_c, h_d, *tables)


def _ret_bwd(d_y, o_pre, states, h_b, h_c, h_d, tables, rider=None):
    T = h_b.shape[0]
    nc = T // RET_CHUNK
    per_step = _pick(nc, (RET_STEP_CHUNKS, 1))
    steps = nc // per_step
    q_spec, k_spec, vv, pos, per_head, c_dec, state = _ret_specs(steps, per_step, True)
    n_ride = rider.n if rider is not None else 0

    def body(*refs):
        (dy_ref, o_ref, st_ref, q_ref, k_ref, v_ref, g_ref, cos_ref, sin_ref, dec_ref, kd_ref, qd_ref,
         cd_ref) = refs[:13]
        dq_ref, dk_ref, dv_ref, dg_ref = refs[13 + n_ride:17 + n_ride]
        carry_ref = refs[17 + 2 * n_ride]
        ids = [pl.program_id(0)]
        if rider is not None:
            ride = (refs[13:13 + n_ride], refs[17 + n_ride:17 + 2 * n_ride], refs[-3:])
            rider.start_at_first(ids, ride)

        @pl.when(ids[0] == 0)
        def _():
            carry_ref[...] = jnp.zeros_like(carry_ref)

        scale = RET_QK_DIM ** -0.5
        for c in reversed(range(per_step)):
            rows = pl.ds(c * RET_CHUNK, RET_CHUNK)
            cos_t, sin_t = cos_ref[rows, :], sin_ref[rows, :]
            for h in range(RET_HEADS):
                q = _rotary(q_ref[rows, _qk_cols(h)], cos_t, sin_t) * scale
                k = _rotary(k_ref[rows, _qk_cols(h)], cos_t, sin_t)
                v = v_ref[rows, _v_cols(h)]
                decay, k_dec, q_dec = dec_ref[h], kd_ref[h], qd_ref[h]
                chunk_decay = cd_ref[h, 0:1, 0:1]
                state = st_ref[h, c].astype(BF16)
                later = carry_ref[h]
                later_b = later.astype(BF16)

                gate = g_ref[rows, _v_cols(h)]
                sig = jax.nn.sigmoid(gate)
                silu = gate * sig
                normed, rstd = _head_norm(o_ref[rows, _v_cols(h)])
                d_y = dy_ref[rows, _v_cols(h)]
                dg_ref[rows, _v_cols(h)] = (d_y * normed * (sig * (1.0 + gate * (1.0 - sig)))).astype(BF16)
                d_n = d_y * silu
                d_o = rstd * (d_n - jnp.mean(d_n, axis=1, keepdims=True)
                              - normed * jnp.mean(d_n * normed, axis=1, keepdims=True))
                d_ob = d_o.astype(BF16)

                qb, kb = q.astype(BF16), k.astype(BF16)
                qd_b, kd_b = (q * q_dec).astype(BF16), (k * k_dec).astype(BF16)
                scores = _dot(qb, kb, _NT) * decay
                d_scores = (_dot(d_ob, v, _NT) * decay).astype(BF16)
                dq = _dot(d_scores, kb, _NN) + _dot(d_ob, state, _NT) * q_dec
                dk = _dot(d_scores, qb, _TN) + _dot(v, later_b, _NT) * k_dec
                dv = _dot(scores.astype(BF16), d_ob, _TN) + _dot(kd_b, later_b, _NN)
                carry_ref[h] = _dot(qd_b, d_ob, _TN) + chunk_decay * later
                dq_ref[rows, _qk_cols(h)] = _rotary_transpose(dq * scale, cos_t, sin_t).astype(BF16)
                dk_ref[rows, _qk_cols(h)] = _rotary_transpose(dk, cos_t, sin_t).astype(BF16)
                dv_ref[rows, _v_cols(h)] = dv.astype(BF16)
        if rider is not None:
            rider.wait_at_last(ids, (steps,), ride)

    qk_out = pl.BlockSpec((per_step * RET_CHUNK, RET_QK_WIDTH), lambda n: (steps - 1 - n, 0))
    in_specs = [vv, vv, state, q_spec, k_spec, vv, vv, pos, pos, per_head, per_head, per_head, c_dec]
    out_specs = [qk_out, qk_out, vv, vv]
    out_shape = [jax.ShapeDtypeStruct((T, RET_QK_WIDTH), BF16), jax.ShapeDtypeStruct((T, RET_QK_WIDTH), BF16),
                 jax.ShapeDtypeStruct((T, RET_V_WIDTH), BF16), jax.ShapeDtypeStruct((T, RET_V_WIDTH), BF16)]
    args = (d_y, o_pre, states, h_b, h_b, h_c, h_d) + tuple(tables)
    scratch = [pltpu.VMEM((RET_HEADS, RET_QK_DIM, RET_V_DIM), F32)]
    if rider is not None:
        in_specs, args = in_specs + rider.specs, args + tuple(rider.bufs)
        out_specs, out_shape = out_specs + rider.specs, out_shape + rider.out_shape
        scratch = scratch + rider.scratch
    outs = pl.pallas_call(
        body,
        name="ret_bwd",
        grid=(steps,),
        in_specs=in_specs,
        out_specs=out_specs,
        out_shape=out_shape,
        scratch_shapes=scratch,
        compiler_params=_params(("arbitrary",)),
    )(*args)
    return outs[0], outs[1], outs[2], outs[3], list(outs[4:])


def _proj_tiles(h, x):
    return h[:, 0:1536], h[:, 1536:2560], h[:, 2560:3584], h[:, 3584:4608], h[:, 4608:6656], x


def _gate_mix_tiles(y_ret, h_e, b_gate, y_sb):
    gates = jax.nn.sigmoid(h_e + b_gate)
    return y_ret, gates[:, :D_MODEL] * y_sb + gates[:, D_MODEL:] * y_ret


def _gate_mix_grad_tiles(d_mix, h_e, b_gate, y_sb, y_ret):
    gates = jax.nn.sigmoid(h_e + b_gate)
    g0, g1 = gates[:, :D_MODEL], gates[:, D_MODEL:]
    d_e = jnp.concatenate([d_mix * y_sb * g0 * (1.0 - g0), d_mix * y_ret * g1 * (1.0 - g1)], axis=1)
    return d_mix * g0, d_mix * g1, d_e, d_e


def _ln_stats(u):
    mu = jnp.mean(u, axis=1, keepdims=True)
    cen = u - mu
    var = jnp.mean(cen * cen, axis=1, keepdims=True)
    rstd = lax.rsqrt(var + LN_EPS)
    return cen * rstd, rstd


def _ln_input_grad(d_out, gain, xhat, rstd):
    d_hat = d_out * gain
    return rstd * (d_hat - jnp.mean(d_hat, axis=1, keepdims=True)
                   - xhat * jnp.mean(d_hat * xhat, axis=1, keepdims=True))


def _ln_tiles(sub, x_prev, gain, bias):
    xhat, rstd = _ln_stats(DN_ALPHA * x_prev + sub)
    out = xhat * gain + bias
    return out, out, xhat, rstd


def _residual_tiles(d_sub, res):
    return (d_sub + DN_ALPHA * res,)


def _ln_grad_tiles(d_sub, res, xhat, rstd, gain):
    d_out = d_sub + DN_ALPHA * res
    du = _ln_input_grad(d_out, gain, xhat, rstd)
    return du, du, d_out * xhat, d_out


def _ln_loss_tiles(sub, x_prev, gain, bias, target):
    xhat, rstd = _ln_stats(DN_ALPHA * x_prev + sub)
    diff = xhat * gain + bias - target
    d_out = diff * (1.0 / D_MODEL)
    du = _ln_input_grad(d_out, gain, xhat, rstd)
    return du, du, diff * diff, d_out * xhat, d_out


def _mem_probs(q_h, k_h):
    s = _dot(q_h, k_h, _NT) * (MEM_HEAD_DIM ** -0.5)
    e = jnp.exp(s - jnp.max(s, axis=1, keepdims=True))
    return e / jnp.sum(e, axis=1, keepdims=True)


def _xattn_fwd(q, kv):
    T, mem_len = q.shape[0], kv.shape[0]
    tq = _pick(T, (512, 256, 128))

    def body(q_ref, kv_ref, o_ref):
        for h in range(MEM_HEADS):
            cols = slice(h * MEM_HEAD_DIM, (h + 1) * MEM_HEAD_DIM)
            vcols = slice(D_MODEL + h * MEM_HEAD_DIM, D_MODEL + (h + 1) * MEM_HEAD_DIM)
            p = _mem_probs(q_ref[:, cols], kv_ref[:, cols])
            o_ref[:, cols] = _dot(p.astype(BF16), kv_ref[:, vcols], _NN).astype(BF16)

    return pl.pallas_call(
        body,
        name="xattn_fwd",
        grid=(T // tq,),
        in_specs=[pl.BlockSpec((tq, D_MODEL), lambda i: (i, 0)),
                  pl.BlockSpec((mem_len, 2 * D_MODEL), lambda i: (0, 0))],
        out_specs=pl.BlockSpec((tq, D_MODEL), lambda i: (i, 0)),
        out_shape=jax.ShapeDtypeStruct((T, D_MODEL), BF16),
        compiler_params=_params(("parallel",)),
    )(q, kv)


def _xattn_bwd(q, kv, d_o):
    T, mem_len = q.shape[0], kv.shape[0]
    tq = _pick(T, (512, 256, 128))

    def body(q_ref, kv_ref, do_ref, dq_ref, dkv_ref):
        @pl.when(pl.program_id(0) == 0)
        def _():
            dkv_ref[...] = jnp.zeros_like(dkv_ref)

        for h in range(MEM_HEADS):
            cols = slice(h * MEM_HEAD_DIM, (h + 1) * MEM_HEAD_DIM)
            vcols = slice(D_MODEL + h * MEM_HEAD_DIM, D_MODEL + (h + 1) * MEM_HEAD_DIM)
            q_h, k_h, do_h = q_ref[:, cols], kv_ref[:, cols], do_ref[:, cols]
            p = _mem_probs(q_h, k_h)
            dp = _dot(do_h, kv_ref[:, vcols], _NT)
            ds = p * (dp - jnp.sum(dp * p, axis=1, keepdims=True))
            dsb = (ds * (MEM_HEAD_DIM ** -0.5)).astype(BF16)
            dq_ref[:, cols] = _dot(dsb, k_h, _NN).astype(BF16)
            dkv_ref[:, cols] += _dot(dsb, q_h, _TN)
            dkv_ref[:, vcols] += _dot(p.astype(BF16), do_h, _TN)

    row = pl.BlockSpec((tq, D_MODEL), lambda i: (i, 0))
    full = pl.BlockSpec((mem_len, 2 * D_MODEL), lambda i: (0, 0))
    return pl.pallas_call(
        body,
        name="xattn_bwd",
        grid=(T // tq,),
        in_specs=[row, full, row],
        out_specs=[row, full],
        out_shape=[jax.ShapeDtypeStruct((T, D_MODEL), BF16), jax.ShapeDtypeStruct((mem_len, 2 * D_MODEL), F32)],
        compiler_params=_params(("arbitrary",)),
    )(q, kv, d_o)


def _swiglu_tiles(f):
    a, b = f[:, :FFN_HIDDEN], f[:, FFN_HIDDEN:]
    return f, a * jax.nn.sigmoid(a) * b


def _swiglu_grad_tiles(d_hidden, f):
    a, b = f[:, :FFN_HIDDEN], f[:, FFN_HIDDEN:]
    sig = jax.nn.sigmoid(a)
    return (jnp.concatenate([d_hidden * b * (sig * (1.0 + a * (1.0 - sig))), d_hidden * (a * sig)], axis=1),)


def _local_step(x, mem, w_in, small, target, fetch_rest, ship):
    T = x.shape[0]
    tables = _ret_tables(T)
    memb = mem.astype(BF16)

    h_a, h_b, h_c, h_d, h_e, xb = _mm_fused(
        x, w_in, mode="nn", name="proj_in", extras=[], pass_a=True,
        outs=[(1536, BF16), (1024, F32), (1024, BF16), (1024, F32), (2048, F32), (D_MODEL, BF16)],
        epilogue=_proj_tiles, max_rows=256)
    (a_sb, r_mat, _), w = fetch_rest(lambda rider: _sb_fwd(h_a, rider))
    y_gated, o_pre, states = _ret_fwd(h_b, h_c, h_d, tables)
    y_sb = _mm(a_sb, w["w_sb_o"], mode="nn", out_dtype=F32, name="sb_out")
    row_f32, row_bf16 = (D_MODEL, F32), (D_MODEL, BF16)
    ln_outs = [row_f32, row_bf16, row_f32, (1, F32)]
    y_ret, mix_in = _mm_fused(y_gated, w["w_ret_o"], mode="nn", name="ret_out", extras=[h_e, small["b_gate"], y_sb],
                              outs=[row_f32, row_bf16], epilogue=_gate_mix_tiles)
    x1, x1b, xhat1, rstd1 = _mm_fused(mix_in, w["w_mix_o"], mode="nn", name="mix_out",
                                      extras=[x, small["ln1_g"], small["ln1_b"]], outs=ln_outs, epilogue=_ln_tiles)
    q_m = _mm(x1b, w["w_mem_q"], mode="nn", out_dtype=BF16, name="mem_q")
    kv_m = _mm(memb, w["w_mem_kv"], mode="nn", out_dtype=BF16, name="mem_kv")
    o_m = _xattn_fwd(q_m, kv_m)
    x2, x2b, xhat2, rstd2 = _mm_fused(o_m, w["w_mem_o"], mode="nn", name="mem_out",
                                      extras=[x1, small["ln2_g"], small["ln2_b"]], outs=ln_outs, epilogue=_ln_tiles)
    f, hidden = _mm_fused(x2b, w["w_ffn_in"], mode="nn", name="ffn_in", extras=[],
                          outs=[(2 * FFN_HIDDEN, F32), (FFN_HIDDEN, BF16)], epilogue=_swiglu_tiles)
    du_outs, col = [row_f32, row_bf16], D_MODEL
    du3, du3b, loss_cols, d_ln3_g, d_ln3_b = _mm_fused(
        hidden, w["w_ffn_out"], mode="nn", name="ffn_out", extras=[x2, small["ln3_g"], small["ln3_b"], target],
        outs=du_outs, sums=[col, col, col], epilogue=_ln_loss_tiles, max_rows=256)

    g_ffn_out = _mm(hidden, du3b, mode="tn", out_dtype=BF16, name="g_ffn_out")
    (d_f,) = _mm_fused(du3b, w["w_ffn_out"], mode="nt", name="d_hidden", extras=[f],
                       outs=[(2 * FFN_HIDDEN, BF16)], epilogue=_swiglu_grad_tiles)
    g_ffn_in = _mm(x2b, d_f, mode="tn", out_dtype=BF16, name="g_ffn_in")
    du2, du2b, d_ln2_g, d_ln2_b = ship(
        {"w_ffn_out": g_ffn_out},
        lambda rider: _as_host(rider, _mm_fused(
            d_f, w["w_ffn_in"], mode="nt", name="d_x2", extras=[du3, xhat2, rstd2, small["ln2_g"]], outs=du_outs,
            sums=[col, col], epilogue=_ln_grad_tiles, rider=rider, max_rows=256)))
    g_mem_o = _mm(o_m, du2b, mode="tn", out_dtype=BF16, name="g_mem_o")
    d_om = _mm(du2b, w["w_mem_o"], mode="nt", out_dtype=BF16, name="d_om")
    d_qm, d_kvm = _xattn_bwd(q_m, kv_m, d_om)
    g_mem_q = _mm(x1b, d_qm, mode="tn", out_dtype=BF16, name="g_mem_q")
    g_mem_kv = _mm(memb, d_kvm.astype(BF16), mode="tn", out_dtype=BF16, name="g_mem_kv")
    du1, du1b, d_ln1_g, d_ln1_b = _mm_fused(
        d_qm, w["w_mem_q"], mode="nt", name="d_x1", extras=[du2, xhat1, rstd1, small["ln1_g"]], outs=du_outs,
        sums=[col, col], epilogue=_ln_grad_tiles, max_rows=256)
    g_mix_o = _mm(mix_in, du1b, mode="tn", out_dtype=BF16, name="g_mix_o")
    d_ysb, d_yret, d_e, d_b_gate = _mm_fused(
        du1b, w["w_mix_o"], mode="nt", name="d_mix_in", extras=[h_e, small["b_gate"], y_sb, y_ret],
        outs=[row_bf16, row_bf16, (2 * D_MODEL, BF16)], sums=[2 * D_MODEL], epilogue=_gate_mix_grad_tiles,
        max_rows=256)
    g_sb_o = _mm(a_sb, d_ysb, mode="tn", out_dtype=BF16, name="g_sb_o")
    g_ret_o = _mm(y_gated, d_yret, mode="tn", out_dtype=BF16, name="g_ret_o")
    d_asb = _mm(d_ysb, w["w_sb_o"], mode="nt", out_dtype=BF16, name="d_asb")
    d_ygated = _mm(d_yret, w["w_ret_o"], mode="nt", out_dtype=F32, name="d_ygated")
    small_grads = {"b_gate": d_b_gate, "ln1_g": d_ln1_g, "ln1_b": d_ln1_b, "ln2_g": d_ln2_g, "ln2_b": d_ln2_b,
                   "ln3_g": d_ln3_g, "ln3_b": d_ln3_b, "loss_cols": loss_cols}
    d_rq, d_rk, d_c, d_d, _ = _ret_bwd(d_ygated, o_pre, states, h_b, h_c, h_d, tables)
    ready = {"w_ffn_in": g_ffn_in, "w_mem_kv": g_mem_kv, "w_mem_q": g_mem_q, "w_mem_o": g_mem_o, "w_mix_o": g_mix_o,
             "w_ret_o": g_ret_o, "w_sb_o": g_sb_o, "small": small_grads}
    d_q, d_k, d_v = ship(ready, lambda rider: _sb_bwd(h_a, d_asb, r_mat, rider))
    d_h = [("sb_q", d_q), ("sb_k", d_k), ("sb_v", d_v), ("ret_q", d_rq), ("ret_k", d_rk), ("ret_v", d_c),
           ("ret_g", d_d), ("gate", d_e)]
    g_in = jnp.concatenate([_mm(xb, piece, mode="tn", out_dtype=BF16, name="g_in_" + tag) for tag, piece in d_h],
                           axis=1)
    (d_x,) = ship({"w_in": g_in},
                  lambda rider: _as_host(rider, _mm_fused(
                      [piece for _, piece in d_h], w_in, mode="nt", name="d_x", extras=[du1], outs=[(D_MODEL, F32)],
                      epilogue=_residual_tiles, rider=rider, max_rows=256)))
    return d_x


def _adamw_math(w, g, m, v):
    m = ADAM_B1 * m + (1.0 - ADAM_B1) * g
    v = ADAM_B2 * v + (1.0 - ADAM_B2) * jnp.square(g)
    m_hat = m / (1.0 - ADAM_B1 ** ADAM_STEP)
    v_hat = v / (1.0 - ADAM_B2 ** ADAM_STEP)
    delta = -ADAM_LR * (m_hat / (jnp.sqrt(v_hat) + ADAM_EPS) + ADAM_WD * w)
    return delta, m, v


def _adamw(parts, w, m, v, name):
    R, C = w.shape
    tr = max(t for t in range(16, min(R, 256) + 1, 16) if R % t == 0) if R >= 16 else R

    def body(p_ref, w_ref, m_ref, v_ref, g_ref, d_ref, nm_ref, nv_ref):
        g = p_ref[0].astype(F32)
        for j in range(1, N_DEV):
            g = g + p_ref[j].astype(F32)
        delta, nm, nv = _adamw_math(w_ref[...], g, m_ref[...], v_ref[...])
        g_ref[...] = g
        d_ref[...] = delta
        nm_ref[...] = nm
        nv_ref[...] = nv

    blk = pl.BlockSpec((tr, C), lambda i: (i, 0))
    out = jax.ShapeDtypeStruct((R, C), F32)
    return pl.pallas_call(
        body,
        name=name,
        grid=(R // tr,),
        in_specs=[pl.BlockSpec((N_DEV, tr, C), lambda i: (0, i, 0)), blk, blk, blk],
        out_specs=[blk] * 4,
        out_shape=[out] * 4,
        compiler_params=_params(("parallel",)),
    )(parts, w, m, v)


_SHARD_AXIS = {"w_in": 1, "w_sb_o": 1, "w_ret_o": 0, "w_mix_o": 0, "w_mem_q": 0, "w_mem_kv": 1, "w_mem_o": 0,
               "w_ffn_in": 1, "w_ffn_out": 0}
_MATRICES = tuple(_SHARD_AXIS)
_SMALL = ("b_gate", "ln1_g", "ln1_b", "ln2_g", "ln2_b", "ln3_g", "ln3_b")
_WEIGHT_ORDER = ("w_in", "b_gate", "w_sb_o", "w_ret_o", "w_mix_o", "ln1_g", "ln1_b", "w_mem_q", "w_mem_kv", "w_mem_o",
                 "ln2_g", "ln2_b", "w_ffn_in", "w_ffn_out", "ln3_g", "ln3_b")


def _assemble(name, gathered):
    if _SHARD_AXIS[name] == 0:
        return gathered.reshape(-1, gathered.shape[2])
    return jnp.transpose(gathered, (1, 0, 2)).reshape(gathered.shape[1], -1)


def _to_slots(name, full):
    if _SHARD_AXIS[name] == 0:
        return full.reshape(N_DEV, full.shape[0] // N_DEV, full.shape[1])
    return jnp.transpose(full.reshape(full.shape[0], N_DEV, full.shape[1] // N_DEV), (1, 0, 2))


SMALL_ROWS = 16


def _pack_small(vals):
    return jnp.concatenate([vals["b_gate"].reshape(2, D_MODEL)] + [vals[n] for n in _SMALL[1:]], axis=0)


def _unpack_small(packed):
    out = {"b_gate": packed[0:2].reshape(1, 2 * D_MODEL)}
    for i, n in enumerate(_SMALL[1:]):
        out[n] = packed[2 + i:3 + i]
    return out


def kernel(x, mem, w_in, b_gate, w_sb_o, w_ret_o, w_mix_o, ln1_g, ln1_b, w_mem_q, w_mem_kv, w_mem_o, ln2_g, ln2_b, w_ffn_in, w_ffn_out, ln3_g, ln3_b, loss_target, m_w_in, m_b_gate, m_w_sb_o, m_w_ret_o, m_w_mix_o, m_ln1_g, m_ln1_b, m_w_mem_q, m_w_mem_kv, m_w_mem_o, m_ln2_g, m_ln2_b, m_w_ffn_in, m_w_ffn_out, m_ln3_g, m_ln3_b, v_w_in, v_b_gate, v_w_sb_o, v_w_ret_o, v_w_mix_o, v_ln1_g, v_ln1_b, v_w_mem_q, v_w_mem_kv, v_w_mem_o, v_ln2_g, v_ln2_b, v_w_ffn_in, v_w_ffn_out, v_ln3_g, v_ln3_b):
    weights = dict(w_in=w_in, b_gate=b_gate, w_sb_o=w_sb_o, w_ret_o=w_ret_o, w_mix_o=w_mix_o, ln1_g=ln1_g, ln1_b=ln1_b,
                   w_mem_q=w_mem_q, w_mem_kv=w_mem_kv, w_mem_o=w_mem_o, ln2_g=ln2_g, ln2_b=ln2_b, w_ffn_in=w_ffn_in,
                   w_ffn_out=w_ffn_out, ln3_g=ln3_g, ln3_b=ln3_b)
    mom1 = dict(w_in=m_w_in, b_gate=m_b_gate, w_sb_o=m_w_sb_o, w_ret_o=m_w_ret_o, w_mix_o=m_w_mix_o, ln1_g=m_ln1_g,
                ln1_b=m_ln1_b, w_mem_q=m_w_mem_q, w_mem_kv=m_w_mem_kv, w_mem_o=m_w_mem_o, ln2_g=m_ln2_g, ln2_b=m_ln2_b,
                w_ffn_in=m_w_ffn_in, w_ffn_out=m_w_ffn_out, ln3_g=m_ln3_g, ln3_b=m_ln3_b)
    mom2 = dict(w_in=v_w_in, b_gate=v_b_gate, w_sb_o=v_w_sb_o, w_ret_o=v_w_ret_o, w_mix_o=v_w_mix_o, ln1_g=v_ln1_g,
                ln1_b=v_ln1_b, w_mem_q=v_w_mem_q, w_mem_kv=v_w_mem_kv, w_mem_o=v_w_mem_o, ln2_g=v_ln2_g, ln2_b=v_ln2_b,
                w_ffn_in=v_w_ffn_in, w_ffn_out=v_w_ffn_out, ln3_g=v_ln3_g, ln3_b=v_ln3_b)

    (gathered_in,) = _exchange([weights["w_in"][0].astype(BF16)], False, "gather_w_in")
    rest = [n for n in _MATRICES if n != "w_in"]
    received = {}

    def fetch_rest(host):
        res = host(_Rider([weights[n][0].astype(BF16) for n in rest], False))
        return res, {n: _assemble(n, g) for n, g in zip(rest, res[-1])}

    def ship(grads, host):
        names = list(grads)
        bufs = []
        for n in names:
            if n == "small":
                part = jnp.concatenate([_pack_small(grads[n]), grads[n]["loss_cols"],
                                        jnp.zeros((SMALL_ROWS - 9, D_MODEL), F32)], axis=0)
                bufs.append(jnp.broadcast_to(part[None], (N_DEV,) + part.shape))
            else:
                bufs.append(_to_slots(n, grads[n]).astype(BF16))
        res = host(_Rider(bufs, True))
        received.update(zip(names, res[-1]))
        return res[:-1]

    small = {n: weights[n] for n in _SMALL}
    d_x = _local_step(x[0], mem[0], _assemble("w_in", gathered_in), small, loss_target[0], fetch_rest, ship)

    new = {}
    for n in _MATRICES:
        new[n] = _adamw(received[n], weights[n][0], mom1[n][0], mom2[n][0], "adamw_" + n)
    packed = _adamw(received["small"][:, :8], _pack_small({n: weights[n] for n in _SMALL}),
                    _pack_small({n: mom1[n] for n in _SMALL}), _pack_small({n: mom2[n] for n in _SMALL}), "adamw_small")
    small_new = [_unpack_small(p) for p in packed]
    loss = jnp.sum(received["small"][:, 8]) * (0.5 / D_MODEL)

    outs = [loss, d_x[None]]
    for slot in range(4):
        for n in _WEIGHT_ORDER:
            outs.append(new[n][slot][None] if n in new else small_new[slot][n])
    return tuple(outs)
```

```python
import functools
import math

import jax
import jax.numpy as jnp
from jax import lax
from jax.experimental import pallas as pl
from jax.experimental.pallas import tpu as pltpu

F32 = jnp.float32
BF16 = jnp.bfloat16

N_DEV = 8
D_MODEL = 1024
SB_HEAD_DIM = 64
SB_WIDTH = 512
RET_HEADS = 4
RET_QK_DIM = 128
RET_V_DIM = 256
RET_QK_WIDTH = 512
RET_V_WIDTH = 1024
RET_CHUNK = 128
RET_STEP_CHUNKS = 2
ROPE_BASE = 10000.0
MEM_HEADS = 4
MEM_HEAD_DIM = 256
FFN_HIDDEN = 2816
DN_ALPHA = 2.0 ** 0.25
LN_EPS = 1e-5
ADAM_LR = 0.001
ADAM_B1 = 0.9
ADAM_B2 = 0.999
ADAM_EPS = 1e-08
ADAM_WD = 0.01
ADAM_STEP = 10

VMEM_LIMIT_BYTES = 52 * 1024 * 1024
LANES = 128
SB_KEY_BLOCK = 128
SB_Q_BLOCK = 256
SB_DEAD_LOG = -105.0

MESH_AXES = ("x", "y", "c")


def _pick(dim, prefs):
    for p in prefs:
        if dim % p == 0:
            return p
    return dim


def _params(sem):
    return pltpu.CompilerParams(dimension_semantics=sem, vmem_limit_bytes=VMEM_LIMIT_BYTES)


def _dot(a, b, dims):
    return lax.dot_general(a, b, (dims, ((), ())), preferred_element_type=F32)


_NN = ((1,), (0,))
_NT = ((1,), (1,))
_TN = ((0,), (0,))


def _my_index():
    return 4 * lax.axis_index("x") + 2 * lax.axis_index("y") + lax.axis_index("c")


def _peer(k):
    x, y, c = lax.axis_index("x"), lax.axis_index("y"), lax.axis_index("c")
    bx, by, bc = (k >> 2) & 1, (k >> 1) & 1, k & 1
    px = (1 - x) if bx else x
    py = (1 - y) if by else y
    pc = (1 - c) if bc else c
    return (px, py, pc), 4 * px + 2 * py + pc


class _Rider:
    def __init__(self, bufs, scatter):
        self.bufs, self.scatter, self.n = list(bufs), scatter, len(bufs)
        self.specs = [pl.BlockSpec(memory_space=pl.ANY)] * self.n
        self.out_shape = [jax.ShapeDtypeStruct(b.shape if scatter else (N_DEV,) + b.shape, b.dtype) for b in self.bufs]
        self.scratch = [pltpu.SemaphoreType.DMA((self.n, N_DEV - 1)), pltpu.SemaphoreType.DMA((self.n, N_DEV - 1)),
                        pltpu.SemaphoreType.DMA((self.n,))]

    def _remote(self, ride, a, k, src_ref, slot, to):
        _, dst, (send_sems, recv_sems, _) = ride
        return pltpu.make_async_remote_copy(src_ref=src_ref, dst_ref=dst[a].at[slot], send_sem=send_sems.at[a, k],
                                            recv_sem=recv_sems.at[a, k], device_id=to,
                                            device_id_type=pl.DeviceIdType.MESH)

    def _local(self, ride, a):
        src, dst, (_, _, local_sems) = ride
        me = _my_index()
        return pltpu.make_async_copy(src[a].at[me] if self.scatter else src[a], dst[a].at[me], local_sems.at[a])

    def _direct(self, ride, a):
        src = ride[0]
        me = _my_index()
        out = []
        for k in range(1, N_DEV):
            peer, peer_idx = _peer(k)
            out.append(self._remote(ride, a, k - 1, src[a].at[peer_idx], me, peer))
        return out

    def _two_level(self, ride, a):
        src, dst = ride[0], ride[1]
        x, y, c = lax.axis_index("x"), lax.axis_index("y"), lax.axis_index("c")
        me, sibling = _my_index(), (x, y, 1 - c)
        chips = [(1 - x, y), (x, 1 - y), (1 - x, 1 - y)]
        first = [self._remote(ride, a, 0, src[a], me, sibling)]
        passed, landing = [], [self._remote(ride, a, 0, src[a], me + 1 - 2 * c, sibling)]
        for j, (px, py) in enumerate(chips):
            first.append(self._remote(ride, a, 1 + j, src[a], me, (px, py, c)))
            theirs = 4 * px + 2 * py + c
            passed.append(self._remote(ride, a, 4 + j, dst[a].at[theirs], theirs, sibling))
            landing.append(self._remote(ride, a, 1 + j, src[a], theirs, (px, py, c)))
        for j, (px, py) in enumerate(chips):
            landing.append(self._remote(ride, a, 4 + j, src[a], 4 * px + 2 * py + 1 - c, sibling))
        return first, passed, landing

    def start(self, ride):
        for a in range(self.n):
            self._local(ride, a).start()
            for cp in (self._direct(ride, a) if self.scatter else self._two_level(ride, a)[0]):
                cp.start()

    def finish(self, ride):
        if self.scatter:
            for a in range(self.n):
                for cp in self._direct(ride, a):
                    cp.wait()
                self._local(ride, a).wait()
            return
        levels = [self._two_level(ride, a) for a in range(self.n)]
        for first, passed, landing in levels:
            for j, cp in enumerate(passed):
                landing[1 + j].wait_recv()
                cp.start()
        for a, (first, passed, landing) in enumerate(levels):
            landing[0].wait_recv()
            for cp in landing[4:]:
                cp.wait_recv()
            for cp in first + passed:
                cp.wait_send()
            self._local(ride, a).wait()

    def start_at_first(self, ids, ride):
        first = functools.reduce(jnp.logical_and, [i == 0 for i in ids])

        @pl.when(first)
        def _():
            self.start(ride)

    def wait_at_last(self, ids, grid, ride):
        last = functools.reduce(jnp.logical_and, [i == g - 1 for i, g in zip(ids, grid)])

        @pl.when(last)
        def _():
            self.finish(ride)


def _exchange(bufs, scatter, name):
    rider = _Rider(bufs, scatter)

    def body(*refs):
        ride = (refs[:rider.n], refs[rider.n:2 * rider.n], refs[2 * rider.n:])
        rider.start(ride)
        rider.finish(ride)

    return pl.pallas_call(
        body,
        name=name,
        in_specs=rider.specs,
        out_specs=rider.specs,
        out_shape=rider.out_shape,
        scratch_shapes=rider.scratch,
    )(*rider.bufs)


MM_RESIDENT_B_BYTES = 14 * 1024 * 1024
MM_A_TILE_BYTES = 4 * 1024 * 1024
MM_OUT_TILE_BYTES = 6 * 1024 * 1024


def _mm_tiles(mode, M, N, K, a_bytes, out_bytes):
    if mode != "tn" and K * N * 2 <= MM_RESIDENT_B_BYTES:
        for tm in (1024, 512, 256, 128):
            if M % tm == 0 and tm * K * a_bytes <= MM_A_TILE_BYTES and tm * N * out_bytes <= MM_OUT_TILE_BYTES:
                return tm, N, K
    if mode == "tn":
        return (_pick(M, (1024, 1408, 512, 256, 128)), _pick(N, (1024, 1664, 1408, 512, 256, 128)),
                _pick(K, (2048, 1024, 512, 256, 128)))
    return _pick(M, (1024, 512, 256, 128)), _pick(N, (512, 256, 128)), _pick(K, (1024, 512, 256, 128))


def _mm(a, b, *, mode, out_dtype, name, res=None, res_scale=1.0, rider=None):
    if mode == "nn":
        (M, K), (K2, N) = a.shape, b.shape
    elif mode == "nt":
        (M, K), (N, K2) = a.shape, b.shape
    else:
        (K, M), (K2, N) = a.shape, b.shape
    assert K == K2, (a.shape, b.shape, mode)
    out_bytes = jnp.dtype(out_dtype).itemsize + (4 if res is not None else 0)
    tm, tn, tk = _mm_tiles(mode, M, N, K, a.dtype.itemsize, out_bytes)
    grid = (M // tm, N // tn, K // tk)
    nk = grid[2]
    dims = {"nn": _NN, "nt": _NT, "tn": _TN}[mode]
    n_in = 2 + (res is not None)
    n_ride = rider.n if rider is not None else 0

    def body(*refs):
        a_ref, b_ref = refs[:2]
        r_ref = refs[2] if res is not None else None
        o_ref = refs[n_in + n_ride]
        rest = refs[n_in + 2 * n_ride + 1:]
        acc_ref = rest[0] if nk > 1 else None
        ids = [pl.program_id(d) for d in range(3)]
        if rider is not None:
            ride = (refs[n_in:n_in + n_ride], refs[n_in + n_ride + 1:n_in + 2 * n_ride + 1], rest[-3:])
            rider.start_at_first(ids, ride)
        part = _dot(a_ref[...].astype(BF16), b_ref[...].astype(BF16), dims)

        def finish(total):
            if r_ref is not None:
                total = total + res_scale * r_ref[...]
            o_ref[...] = total.astype(out_dtype)

        if nk == 1:
            finish(part)
        else:
            k = ids[2]

            @pl.when(k == 0)
            def _():
                acc_ref[...] = part

            @pl.when(k > 0)
            def _():
                acc_ref[...] += part

            @pl.when(k == nk - 1)
            def _():
                finish(acc_ref[...])

        if rider is not None:
            rider.wait_at_last(ids, grid, ride)

    if mode == "nn":
        a_spec = pl.BlockSpec((tm, tk), lambda i, j, k: (i, k))
        b_spec = pl.BlockSpec((tk, tn), lambda i, j, k: (k, j))
    elif mode == "nt":
        a_spec = pl.BlockSpec((tm, tk), lambda i, j, k: (i, k))
        b_spec = pl.BlockSpec((tn, tk), lambda i, j, k: (j, k))
    else:
        a_spec = pl.BlockSpec((tk, tm), lambda i, j, k: (k, i))
        b_spec = pl.BlockSpec((tk, tn), lambda i, j, k: (k, j))
    o_spec = pl.BlockSpec((tm, tn), lambda i, j, k: (i, j))
    in_specs = [a_spec, b_spec] + ([o_spec] if res is not None else [])
    args = (a, b) + ((res,) if res is not None else ())
    out_specs, out_shape = [o_spec], [jax.ShapeDtypeStruct((M, N), out_dtype)]
    scratch = [pltpu.VMEM((tm, tn), F32)] if nk > 1 else []
    sem = ("parallel", "parallel", "arbitrary")
    if rider is not None:
        in_specs, args = in_specs + rider.specs, args + tuple(rider.bufs)
        out_specs, out_shape = out_specs + rider.specs, out_shape + rider.out_shape
        scratch = scratch + rider.scratch
        sem = ("arbitrary",) * 3
    outs = pl.pallas_call(
        body,
        name=name,
        grid=grid,
        in_specs=in_specs,
        out_specs=out_specs,
        out_shape=out_shape,
        scratch_shapes=scratch,
        compiler_params=_params(sem),
    )(*args)
    return outs[0] if rider is None else (outs[0], list(outs[1:]))


def _mm_host(a, b, *, rider, **kw):
    out = _mm(a, b, rider=rider, **kw)
    return out if rider is not None else (out, [])


def _as_host(rider, results):
    return results if rider is not None else tuple(results) + ([],)


MM_FUSED_MARGIN_BYTES = 10 * 1024 * 1024
MM_FUSED_MAX_ROWS = 512


def _col_sum_update(acc_ref, val, first):
    part = jnp.sum(val.reshape(val.shape[0] // 8, 8, val.shape[1]), axis=0)

    @pl.when(first)
    def _():
        acc_ref[...] = part

    @pl.when(jnp.logical_not(first))
    def _():
        acc_ref[...] += part


def _mm_fused(a, b, *, mode, name, extras, outs, epilogue, sums=(), rider=None, max_rows=MM_FUSED_MAX_ROWS,
              pass_a=False):
    parts = list(a) if isinstance(a, (list, tuple)) else [a]
    M, K = parts[0].shape[0], sum(p.shape[1] for p in parts)
    if mode == "nn":
        (K2, N), b_dims = b.shape, _NN
    else:
        (N, K2), b_dims = b.shape, _NT
    assert K == K2, (K, b.shape, mode)
    rows = parts + [e for e in extras if e.shape[0] == M]
    per_row = 2 * (sum(e.shape[1] * e.dtype.itemsize for e in rows)
                   + sum(c * jnp.dtype(d).itemsize for c, d in outs)) + 2 * N * 4
    budget = VMEM_LIMIT_BYTES - K * N * 2 - MM_FUSED_MARGIN_BYTES
    tm = next(t for t in (512, 256, 128, 64, 32, 16) if t <= max_rows and M % t == 0 and t * per_row <= budget)
    steps = M // tm
    n_a, n_x, n_o, n_s = len(parts), len(extras), len(outs), len(sums)
    n_ride = rider.n if rider is not None else 0

    def body(*refs):
        a_refs, b_ref = refs[:n_a], refs[n_a]
        x_refs = refs[n_a + 1:n_a + 1 + n_x]
        base = n_a + 1 + n_x + n_ride
        o_refs, s_refs = refs[base:base + n_o], refs[base + n_o:base + n_o + n_s]
        acc_refs = refs[base + n_o + n_s + n_ride:base + n_o + 2 * n_s + n_ride]
        ids = [pl.program_id(0)]
        if rider is not None:
            ride = (refs[n_a + 1 + n_x:base], refs[base + n_o + n_s:base + n_o + n_s + n_ride], refs[-3:])
            rider.start_at_first(ids, ride)
        a_tile = a_refs[0][...]
        a_bf16 = a_tile.astype(BF16) if n_a == 1 else jnp.concatenate([r[...].astype(BF16) for r in a_refs], axis=1)
        prod = _dot(a_bf16, b_ref[...], b_dims)
        tiles = epilogue(prod, *([a_tile] if pass_a else []), *[r[...] for r in x_refs])
        for o_ref, t in zip(o_refs, tiles[:n_o]):
            o_ref[...] = t.astype(o_ref.dtype)
        for acc_ref, t in zip(acc_refs, tiles[n_o:]):
            _col_sum_update(acc_ref, t, ids[0] == 0)
        if n_s:
            @pl.when(ids[0] == steps - 1)
            def _():
                for s_ref, acc_ref in zip(s_refs, acc_refs):
                    s_ref[...] = jnp.sum(acc_ref[...], axis=0, keepdims=True)
        if rider is not None:
            rider.wait_at_last(ids, (steps,), ride)

    in_specs = [pl.BlockSpec((tm, p.shape[1]), lambda i: (i, 0)) for p in parts]
    in_specs.append(pl.BlockSpec(b.shape, lambda i: (0, 0), pipeline_mode=pl.Buffered(1)))
    for e in extras:
        in_specs.append(pl.BlockSpec((tm, e.shape[1]), lambda i: (i, 0)) if e.shape[0] == M
                        else pl.BlockSpec(e.shape, lambda i: (0, 0)))
    out_specs = ([pl.BlockSpec((tm, c), lambda i: (i, 0)) for c, _ in outs]
                 + [pl.BlockSpec((1, c), lambda i: (0, 0)) for c in sums])
    out_shape = ([jax.ShapeDtypeStruct((M, c), d) for c, d in outs]
                 + [jax.ShapeDtypeStruct((1, c), F32) for c in sums])
    args = tuple(parts) + (b,) + tuple(extras)
    scratch = [pltpu.VMEM((8, c), F32) for c in sums]
    if rider is not None:
        in_specs, args = in_specs + rider.specs, args + tuple(rider.bufs)
        out_specs, out_shape = out_specs + rider.specs, out_shape + rider.out_shape
        scratch = scratch + rider.scratch
    res = pl.pallas_call(
        body,
        name=name,
        grid=(steps,),
        in_specs=in_specs,
        out_specs=out_specs,
        out_shape=out_shape,
        scratch_shapes=scratch,
        compiler_params=_params(("arbitrary",) if (n_s or rider is not None) else ("parallel",)),
    )(*args)
    return tuple(res[:n_o + n_s]) + ((list(res[n_o + n_s:]),) if rider is not None else ())


def _pair_rows(blk, lane_is_a):
    zero = jnp.zeros_like(blk)
    return jnp.concatenate([jnp.where(lane_is_a, blk, zero), jnp.where(lane_is_a, zero, blk)], axis=0)


SB_STRIP = 32
SB_FWD_PAIRS = 4
SB_BWD_PAIRS = 2
SB_GROUP = 2


def _pair_lanes(p):
    return slice(p * LANES, (p + 1) * LANES)


def _sb_scan_matrices():
    o = lax.broadcasted_iota(jnp.int32, (2 * LANES, 4 * LANES), 0)
    c = lax.broadcasted_iota(jnp.int32, (2 * LANES, 4 * LANES), 1) & (2 * LANES - 1)
    same = (o >= LANES) == (c >= LANES)
    oo, cc = o & (LANES - 1), c & (LANES - 1)
    return (jnp.where(same & (cc > oo), 1.0, 0.0).astype(BF16), jnp.where(same & (cc < oo), 1.0, 0.0).astype(BF16))


def _sb_causal_masks(tq):
    d = lax.broadcasted_iota(jnp.int32, (tq // SB_KEY_BLOCK, SB_KEY_BLOCK, tq), 0)
    k = lax.broadcasted_iota(jnp.int32, (tq // SB_KEY_BLOCK, SB_KEY_BLOCK, tq), 1)
    t = lax.broadcasted_iota(jnp.int32, (tq // SB_KEY_BLOCK, SB_KEY_BLOCK, tq), 2)
    return jnp.where(d * SB_KEY_BLOCK + k < t, 1.0, 0.0).astype(F32)


def _sb_log_terms(z):
    log_rem = -jnp.maximum(z, 0.0) - jnp.log(1.0 + jnp.exp(-jnp.abs(z)))
    return log_rem, log_rem + z


def _sb_store_split(ref, strip, val, cols):
    hi = val.astype(BF16)
    ref[pl.ds(strip * SB_STRIP, SB_STRIP), cols] = hi
    ref[pl.ds(2 * LANES + strip * SB_STRIP, SB_STRIP), cols] = (val - hi.astype(F32)).astype(BF16)


def _sb_lanes(tq, diag):
    if diag == "left":
        return 0, tq // 2
    first = 0 if diag is None else diag * SB_KEY_BLOCK
    return first, tq - first


def _lane_add(full, part, lanes):
    first, width = lanes
    pieces = [full[:, :first]] if first else []
    pieces.append(full[:, first:first + width] + part)
    if first + width < full.shape[1]:
        pieces.append(full[:, first + width:])
    return pieces[0] if len(pieces) == 1 else jnp.concatenate(pieces, axis=1)


def _sb_fwd(h_a, rider=None):
    assert SB_FWD_PAIRS == 4
    T = h_a.shape[0]
    tq = _pick(T, (SB_Q_BLOCK, SB_KEY_BLOCK))
    nq, per_q, nkb = T // tq, tq // SB_KEY_BLOCK, T // SB_KEY_BLOCK
    assert per_q % SB_GROUP == 0
    n_strips = 2 * LANES // SB_STRIP
    n_ride = rider.n if rider is not None else 0
    after_m, _ = _sb_scan_matrices()
    causal_m = _sb_causal_masks(tq)
    pairs = SB_FWD_PAIRS

    def body(*refs):
        q_ref, k_ref, v_ref, after_ref, causal_ref = refs[:5]
        a_ref, r_ref, n_ref = refs[5 + n_ride:8 + n_ride]
        z_ref, lb_ref, split_ref, w_ref = refs[8 + 2 * n_ride:12 + 2 * n_ride]
        ids = [pl.program_id(0)]
        if rider is not None:
            ride = (refs[5:5 + n_ride], refs[8 + n_ride:8 + 2 * n_ride], refs[-3:])
            rider.start_at_first(ids, ride)
        i = ids[0]
        q_t = [(q_ref[:, _pair_lanes(p)].astype(F32).T * (SB_HEAD_DIM ** -0.5)).astype(BF16) for p in range(pairs)]
        lane_is_a = lax.broadcasted_iota(jnp.int32, (SB_KEY_BLOCK, LANES), 1) < SB_HEAD_DIM

        def tiles(kbs, diags, carry):
            nb = len(kbs)
            lanes = [_sb_lanes(tq, d) for d in diags]
            cols = [slice(first, first + width) for first, width in lanes]
            acc_t, ra, rb = [list(c) for c in carry]
            ks = [pl.multiple_of(kb * SB_KEY_BLOCK, SB_KEY_BLOCK) for kb in kbs]
            slot = lambda p, b: p * nb + b

            def causal(b, s):
                return causal_ref[diags[b], pl.ds((s * SB_STRIP) % SB_KEY_BLOCK, SB_STRIP), cols[b]]

            vv = {}
            for b in range(nb):
                for p in range(pairs):
                    kk = _pair_rows(k_ref[pl.ds(ks[b], SB_KEY_BLOCK), _pair_lanes(p)], lane_is_a)
                    vv[p, b] = _pair_rows(v_ref[pl.ds(ks[b], SB_KEY_BLOCK), _pair_lanes(p)], lane_is_a)
                    z_ref[slot(p, b), :, cols[b]] = _dot(kk, q_t[p][:, cols[b]], _NN)
            sums = {}
            for b in range(nb):
                for p in range(pairs):
                    part = [jnp.zeros((8, lanes[b][1]), F32), jnp.zeros((8, lanes[b][1]), F32)]
                    for s in range(n_strips):
                        rows = pl.ds(s * SB_STRIP, SB_STRIP)
                        log_rem, log_beta = _sb_log_terms(z_ref[slot(p, b), rows, cols[b]])
                        lb_ref[slot(p, b), rows, cols[b]] = log_beta
                        if isinstance(diags[b], int):
                            log_rem = log_rem * causal(b, s)
                        _sb_store_split(split_ref.at[slot(p, b)], s, log_rem, cols[b])
                        head = (s * SB_STRIP) // SB_KEY_BLOCK
                        part[head] = part[head] + jnp.sum(log_rem.reshape(SB_STRIP // 8, 8, lanes[b][1]), axis=0)
                    sums[p, b] = part
            for b in range(nb):
                for p in range(pairs):
                    z_ref[slot(p, b), :, cols[b]] = _dot(after_ref[...], split_ref[slot(p, b), :, cols[b]], _NN)
            for b in range(nb):
                for p in range(pairs):
                    for s in range(n_strips):
                        rows = pl.ds(s * SB_STRIP, SB_STRIP)
                        start = (ra[p] if (s * SB_STRIP) < SB_KEY_BLOCK else rb[p])[:, cols[b]]
                        w = jnp.exp(lb_ref[slot(p, b), rows, cols[b]] + z_ref[slot(p, b), rows, cols[b]] + start)
                        if isinstance(diags[b], int):
                            w = w * causal(b, s)
                        w_ref[slot(p, b), rows, cols[b]] = w.astype(BF16)
                    r_ref[2 * p, kbs[b]] = ra[p]
                    r_ref[2 * p + 1, kbs[b]] = rb[p]
                    ra[p] = _lane_add(ra[p], jnp.sum(sums[p, b][0], axis=0, keepdims=True), lanes[b])
                    rb[p] = _lane_add(rb[p], jnp.sum(sums[p, b][1], axis=0, keepdims=True), lanes[b])
            for b in range(nb):
                for p in range(pairs):
                    acc_t[p] = _lane_add(acc_t[p], _dot(vv[p, b], w_ref[slot(p, b), :, cols[b]], _TN), lanes[b])
            return tuple(acc_t), tuple(ra), tuple(rb)

        carry = (tuple(jnp.zeros((LANES, tq), F32) for _ in range(pairs)),
                 tuple(jnp.zeros((1, tq), F32) for _ in range(pairs)),
                 tuple(jnp.zeros((1, tq), F32) for _ in range(pairs)))
        own = list(reversed(range(per_q)))
        n_full = i * per_q
        carry = lax.cond(
            i > 0,
            lambda cc: tiles([n_full + d for d in own] + [n_full - 1 - b for b in range(SB_GROUP)],
                             own + [None] * SB_GROUP, cc),
            lambda cc: tiles([n_full + d for d in own], own, cc), carry)
        first_walked = jnp.where(i > 0, SB_GROUP, 0).astype(jnp.int32)

        def top_of(sums_a, sums_b, first):
            return jnp.max(functools.reduce(jnp.maximum, [r[:, first:] for r in sums_a + sums_b]))

        def alive(c):
            return jnp.logical_and(c[0] < n_full, top_of(c[2], c[3], 0) > SB_DEAD_LOG)

        def step(c):
            kbs = [n_full - 1 - c[0] - b for b in range(SB_GROUP)]
            return (c[0] + SB_GROUP,) + lax.cond(
                top_of(c[2], c[3], tq // 2) > SB_DEAD_LOG,
                lambda cc: tiles(kbs, [None] * SB_GROUP, cc), lambda cc: tiles(kbs, ["left"] * SB_GROUP, cc), c[1:])

        walked, acc_t, _, _ = lax.while_loop(alive, step, (first_walked,) + carry)
        for p in range(pairs):
            a_ref[:, _pair_lanes(p)] = acc_t[p].T.astype(BF16)
        n_ref[...] = jnp.zeros(n_ref.shape, F32) + walked.astype(F32)
        if rider is not None:
            rider.wait_at_last(ids, (nq,), ride)

    wide = pairs * LANES
    in_specs = [pl.BlockSpec((tq, wide), lambda i: (i, 0)),
                pl.BlockSpec((T, wide), lambda i: (0, 1), pipeline_mode=pl.Buffered(1)),
                pl.BlockSpec((T, wide), lambda i: (0, 2), pipeline_mode=pl.Buffered(1)),
                pl.BlockSpec(after_m.shape, lambda i: (0, 0), pipeline_mode=pl.Buffered(1)),
                pl.BlockSpec(causal_m.shape, lambda i: (0, 0, 0), pipeline_mode=pl.Buffered(1))]
    out_specs = [pl.BlockSpec((tq, wide), lambda i: (i, 0)),
                 pl.BlockSpec((2 * pairs, nkb, 1, tq), lambda i: (0, 0, 0, i)),
                 pl.BlockSpec((1, 8, LANES), lambda i: (i, 0, 0))]
    out_shape = [jax.ShapeDtypeStruct((T, SB_WIDTH), BF16), jax.ShapeDtypeStruct((2 * pairs, nkb, 1, T), F32),
                 jax.ShapeDtypeStruct((nq, 8, LANES), F32)]
    args = (h_a, h_a, h_a, after_m, causal_m)
    slots = pairs * (per_q + SB_GROUP)
    scratch = [pltpu.VMEM((slots, 2 * LANES, tq), F32), pltpu.VMEM((slots, 2 * LANES, tq), F32),
               pltpu.VMEM((slots, 4 * LANES, tq), BF16), pltpu.VMEM((slots, 2 * LANES, tq), BF16)]
    if rider is not None:
        in_specs, args = in_specs + rider.specs, args + tuple(rider.bufs)
        out_specs, out_shape = out_specs + rider.specs, out_shape + rider.out_shape
        scratch = scratch + rider.scratch
    outs = pl.pallas_call(
        body,
        name="sb_fwd",
        grid=(nq,),
        in_specs=in_specs,
        out_specs=out_specs,
        out_shape=out_shape,
        scratch_shapes=scratch,
        compiler_params=_params(("arbitrary",)),
    )(*args)
    return outs[0], (outs[1], outs[2]), list(outs[3:])


def _sb_bwd(h_a, d_out, saved, rider=None):
    r_mat, walked_blocks = saved
    T = h_a.shape[0]
    tq = _pick(T, (SB_Q_BLOCK, SB_KEY_BLOCK))
    nq, per_q, nkb = T // tq, tq // SB_KEY_BLOCK, T // SB_KEY_BLOCK
    n_strips = 2 * LANES // SB_STRIP
    after_m, before_m = _sb_scan_matrices()
    causal_m = _sb_causal_masks(tq)
    pairs = SB_BWD_PAIRS
    groups = 4 // pairs
    n_ride = rider.n if rider is not None else 0

    def body(*refs):
        q_ref, k_ref, v_ref, do_ref, r_ref, n_ref, after_ref, before_ref, causal_ref = refs[:9]
        dq_ref, dk_ref, dv_ref = refs[9 + n_ride:12 + n_ride]
        z_ref, lb_ref, split_ref, w_ref, da_ref, dz_ref = refs[12 + 2 * n_ride:18 + 2 * n_ride]
        ids = [pl.program_id(0), pl.program_id(1)]
        if rider is not None:
            ride = (refs[9:9 + n_ride], refs[12 + n_ride:12 + 2 * n_ride], refs[-3:])
            rider.start_at_first(ids, ride)
        i = ids[1]

        @pl.when(i == 0)
        def _():
            dk_ref[...] = jnp.zeros_like(dk_ref)
            dv_ref[...] = jnp.zeros_like(dv_ref)

        scale = SB_HEAD_DIM ** -0.5
        q = [q_ref[:, _pair_lanes(p)] for p in range(pairs)]
        d_o = [do_ref[:, _pair_lanes(p)] for p in range(pairs)]
        q_t = [(x.astype(F32).T * scale).astype(BF16) for x in q]
        do_t = [x.astype(F32).T.astype(BF16) for x in d_o]
        lane_is_a = lax.broadcasted_iota(jnp.int32, (SB_KEY_BLOCK, LANES), 1) < SB_HEAD_DIM

        def tiles(kbs, diags, carry):
            nb = len(kbs)
            lanes = [_sb_lanes(tq, d) for d in diags]
            cols = [slice(first, first + width) for first, width in lanes]
            dq_t, ca, cb = [list(c) for c in carry]
            ks = [pl.multiple_of(kb * SB_KEY_BLOCK, SB_KEY_BLOCK) for kb in kbs]
            slot = lambda p, b: p * nb + b

            def causal(b, s):
                return causal_ref[diags[b], pl.ds((s * SB_STRIP) % SB_KEY_BLOCK, SB_STRIP), cols[b]]

            kk, vv = {}, {}
            for b in range(nb):
                for p in range(pairs):
                    kk[p, b] = _pair_rows(k_ref[pl.ds(ks[b], SB_KEY_BLOCK), _pair_lanes(p)], lane_is_a)
                    vv[p, b] = _pair_rows(v_ref[pl.ds(ks[b], SB_KEY_BLOCK), _pair_lanes(p)], lane_is_a)
                    z_ref[slot(p, b), :, cols[b]] = _dot(kk[p, b], q_t[p][:, cols[b]], _NN)
            for b in range(nb):
                for p in range(pairs):
                    for s in range(n_strips):
                        rows = pl.ds(s * SB_STRIP, SB_STRIP)
                        log_rem, log_beta = _sb_log_terms(z_ref[slot(p, b), rows, cols[b]])
                        lb_ref[slot(p, b), rows, cols[b]] = log_beta
                        if isinstance(diags[b], int):
                            log_rem = log_rem * causal(b, s)
                        _sb_store_split(split_ref.at[slot(p, b)], s, log_rem, cols[b])
            for b in range(nb):
                for p in range(pairs):
                    z_ref[slot(p, b), :, cols[b]] = _dot(after_ref[...], split_ref[slot(p, b), :, cols[b]], _NN)
                    da_ref[slot(p, b), :, cols[b]] = _dot(vv[p, b], do_t[p][:, cols[b]], _NN)
            sums = {}
            for b in range(nb):
                for p in range(pairs):
                    part = [jnp.zeros((8, lanes[b][1]), F32), jnp.zeros((8, lanes[b][1]), F32)]
                    for s in range(n_strips):
                        rows = pl.ds(s * SB_STRIP, SB_STRIP)
                        start = r_ref[2 * p + (s * SB_STRIP) // SB_KEY_BLOCK, kbs[b]][:, cols[b]]
                        w = jnp.exp(lb_ref[slot(p, b), rows, cols[b]] + z_ref[slot(p, b), rows, cols[b]] + start)
                        if isinstance(diags[b], int):
                            w = w * causal(b, s)
                        w_ref[slot(p, b), rows, cols[b]] = w.astype(BF16)
                        da = da_ref[slot(p, b), rows, cols[b]] * w
                        da_ref[slot(p, b), rows, cols[b]] = da
                        _sb_store_split(split_ref.at[slot(p, b)], s, da, cols[b])
                        head = (s * SB_STRIP) // SB_KEY_BLOCK
                        part[head] = part[head] + jnp.sum(da.reshape(SB_STRIP // 8, 8, lanes[b][1]), axis=0)
                    sums[p, b] = part
            for b in range(nb):
                for p in range(pairs):
                    z_ref[slot(p, b), :, cols[b]] = _dot(before_ref[...], split_ref[slot(p, b), :, cols[b]], _NN)
            for b in range(nb):
                for p in range(pairs):
                    for s in range(n_strips):
                        rows = pl.ds(s * SB_STRIP, SB_STRIP)
                        base = (ca[p] if (s * SB_STRIP) < SB_KEY_BLOCK else cb[p])[:, cols[b]]
                        sig = jnp.exp(lb_ref[slot(p, b), rows, cols[b]])
                        dz = (da_ref[slot(p, b), rows, cols[b]] * (1.0 - sig)
                              - (z_ref[slot(p, b), rows, cols[b]] + base) * sig)
                        if isinstance(diags[b], int):
                            dz = dz * causal(b, s)
                        dz_ref[slot(p, b), rows, cols[b]] = (dz * scale).astype(BF16)
                    ca[p] = _lane_add(ca[p], jnp.sum(sums[p, b][0], axis=0, keepdims=True), lanes[b])
                    cb[p] = _lane_add(cb[p], jnp.sum(sums[p, b][1], axis=0, keepdims=True), lanes[b])
            for b in range(nb):
                for p in range(pairs):
                    dq_t[p] = _lane_add(dq_t[p], _dot(kk[p, b], dz_ref[slot(p, b), :, cols[b]], _TN), lanes[b])
                    dkk = _dot(dz_ref[slot(p, b), :, cols[b]], q[p][cols[b], :], _NN)
                    dvv = _dot(w_ref[slot(p, b), :, cols[b]], d_o[p][cols[b], :], _NN)
                    here = (pl.ds(ks[b], SB_KEY_BLOCK), _pair_lanes(p))
                    dk_ref[here] += jnp.where(lane_is_a, dkk[:SB_KEY_BLOCK], dkk[SB_KEY_BLOCK:])
                    dv_ref[here] += jnp.where(lane_is_a, dvv[:SB_KEY_BLOCK], dvv[SB_KEY_BLOCK:])
            return tuple(dq_t), tuple(ca), tuple(cb)

        n_full = i * per_q
        groups_walked = jnp.clip(jnp.max(n_ref[...]).astype(jnp.int32), 0, n_full) // SB_GROUP
        carry = (tuple(jnp.zeros((LANES, tq), F32) for _ in range(pairs)),
                 tuple(jnp.zeros((1, tq), F32) for _ in range(pairs)),
                 tuple(jnp.zeros((1, tq), F32) for _ in range(pairs)))

        def below(j, c):
            kbs = [n_full - (groups_walked - j) * SB_GROUP + b for b in range(SB_GROUP)]
            starts = [r_ref[h, kbs[-1]][:, tq // 2:] for h in range(2 * pairs)]
            reaches = jnp.max(functools.reduce(jnp.maximum, starts)) > SB_DEAD_LOG
            return lax.cond(reaches, lambda cc: tiles(kbs, [None] * SB_GROUP, cc),
                            lambda cc: tiles(kbs, ["left"] * SB_GROUP, cc), c)

        carry = lax.fori_loop(0, groups_walked, below, carry)
        own = list(range(per_q))
        carry = tiles([i * per_q + d for d in own], own, carry)
        for p in range(pairs):
            dq_ref[:, _pair_lanes(p)] = carry[0][p].T.astype(BF16)
        if rider is not None:
            rider.wait_at_last(ids, (groups, nq), ride)

    wide = pairs * LANES
    mat = pl.BlockSpec(after_m.shape, lambda g, i: (0, 0), pipeline_mode=pl.Buffered(1))
    in_specs = [pl.BlockSpec((tq, wide), lambda g, i: (i, g)),
                pl.BlockSpec((T, wide), lambda g, i: (0, groups + g), pipeline_mode=pl.Buffered(1)),
                pl.BlockSpec((T, wide), lambda g, i: (0, 2 * groups + g), pipeline_mode=pl.Buffered(1)),
                pl.BlockSpec((tq, wide), lambda g, i: (i, g)),
                pl.BlockSpec((2 * pairs, nkb, 1, tq), lambda g, i: (g, 0, 0, i)),
                pl.BlockSpec((1, 8, LANES), lambda g, i: (i, 0, 0)),
                mat, mat,
                pl.BlockSpec(causal_m.shape, lambda g, i: (0, 0, 0), pipeline_mode=pl.Buffered(1))]
    out_specs = [pl.BlockSpec((tq, wide), lambda g, i: (i, g)),
                 pl.BlockSpec((T, wide), lambda g, i: (0, g)),
                 pl.BlockSpec((T, wide), lambda g, i: (0, g))]
    out_shape = [jax.ShapeDtypeStruct((T, SB_WIDTH), BF16), jax.ShapeDtypeStruct((T, SB_WIDTH), F32),
                 jax.ShapeDtypeStruct((T, SB_WIDTH), F32)]
    args = (h_a, h_a, h_a, d_out, r_mat, walked_blocks, after_m, before_m, causal_m)
    slots = pairs * max(per_q, SB_GROUP)
    scratch = [pltpu.VMEM((slots, 2 * LANES, tq), F32), pltpu.VMEM((slots, 2 * LANES, tq), F32),
               pltpu.VMEM((slots, 4 * LANES, tq), BF16), pltpu.VMEM((slots, 2 * LANES, tq), BF16),
               pltpu.VMEM((slots, 2 * LANES, tq), F32), pltpu.VMEM((slots, 2 * LANES, tq), BF16)]
    if rider is not None:
        in_specs, args = in_specs + rider.specs, args + tuple(rider.bufs)
        out_specs, out_shape = out_specs + rider.specs, out_shape + rider.out_shape
        scratch = scratch + rider.scratch
    outs = pl.pallas_call(
        body,
        name="sb_bwd",
        grid=(groups, nq),
        in_specs=in_specs,
        out_specs=out_specs,
        out_shape=out_shape,
        scratch_shapes=scratch,
        compiler_params=_params(("arbitrary", "arbitrary") if rider is not None else ("parallel", "arbitrary")),
    )(*args)
    return outs[0], outs[1], outs[2], list(outs[3:])


def _ret_tables(T):
    half = RET_QK_DIM // 2
    inv = 1.0 / (ROPE_BASE ** (jnp.arange(half, dtype=F32) / half))
    ang = jnp.arange(T, dtype=F32)[:, None] * inv[None, :]
    cos, sin = jnp.cos(ang), jnp.sin(ang)
    cos_t = jnp.concatenate([cos, cos], axis=1)
    sin_t = jnp.concatenate([-sin, sin], axis=1)
    log_gamma = jnp.log1p(-jnp.exp2(-5.0 - jnp.arange(RET_HEADS, dtype=F32)))
    idx = jnp.arange(RET_CHUNK, dtype=F32)
    rel = idx[:, None] - idx[None, :]
    decay = jnp.where(rel[None] >= 0, jnp.exp(log_gamma[:, None, None] * jnp.maximum(rel, 0.0)[None]), 0.0)
    k_decay = jnp.exp(log_gamma[None, :] * (RET_CHUNK - 1.0 - idx)[:, None])
    q_decay = jnp.exp(log_gamma[None, :] * (idx + 1.0)[:, None])
    chunk_decay = jnp.exp(log_gamma * RET_CHUNK)
    k_dec = jnp.broadcast_to(k_decay.T[:, :, None], (RET_HEADS, RET_CHUNK, LANES))
    q_dec = jnp.broadcast_to(q_decay.T[:, :, None], (RET_HEADS, RET_CHUNK, LANES))
    c_dec = jnp.broadcast_to(chunk_decay[:, None, None], (RET_HEADS, 8, LANES))
    return cos_t, sin_t, decay, k_dec, q_dec, c_dec


def _rotary(x, cos_t, sin_t):
    return x * cos_t + pltpu.roll(x, RET_QK_DIM // 2, 1) * sin_t


def _rotary_transpose(dy, cos_t, sin_t):
    return dy * cos_t + pltpu.roll(dy * sin_t, RET_QK_DIM // 2, 1)


def _head_norm(o):
    mu = jnp.mean(o, axis=1, keepdims=True)
    cen = o - mu
    var = jnp.mean(cen * cen, axis=1, keepdims=True)
    rstd = lax.rsqrt(var + LN_EPS)
    return cen * rstd, rstd


def _ret_specs(steps, per_step, reverse):
    def n_of(n):
        return (steps - 1 - n) if reverse else n

    rows = per_step * RET_CHUNK
    q_spec = pl.BlockSpec((rows, RET_QK_WIDTH), lambda n: (n_of(n), 0))
    k_spec = pl.BlockSpec((rows, RET_QK_WIDTH), lambda n: (n_of(n), 1))
    vv = pl.BlockSpec((rows, RET_V_WIDTH), lambda n: (n_of(n), 0))
    pos = pl.BlockSpec((rows, LANES), lambda n: (n_of(n), 0))
    per_head = pl.BlockSpec((RET_HEADS, RET_CHUNK, LANES), lambda n: (0, 0, 0))
    c_dec = pl.BlockSpec((RET_HEADS, 8, LANES), lambda n: (0, 0, 0))
    state = pl.BlockSpec((RET_HEADS, per_step, RET_QK_DIM, RET_V_DIM), lambda n: (0, n_of(n), 0, 0))
    return q_spec, k_spec, vv, pos, per_head, c_dec, state


def _qk_cols(h):
    return slice(h * RET_QK_DIM, (h + 1) * RET_QK_DIM)


def _v_cols(h):
    return slice(h * RET_V_DIM, (h + 1) * RET_V_DIM)


def _ret_fwd(h_b, h_c, h_d, tables):
    T = h_b.shape[0]
    nc = T // RET_CHUNK
    per_step = _pick(nc, (RET_STEP_CHUNKS, 1))
    steps = nc // per_step
    q_spec, k_spec, vv, pos, per_head, c_dec, state = _ret_specs(steps, per_step, False)

    def body(q_ref, k_ref, v_ref, g_ref, cos_ref, sin_ref, dec_ref, kd_ref, qd_ref, cd_ref,
             y_ref, o_ref, st_ref, state_ref):
        @pl.when(pl.program_id(0) == 0)
        def _():
            state_ref[...] = jnp.zeros_like(state_ref)

        for c in range(per_step):
            rows = pl.ds(c * RET_CHUNK, RET_CHUNK)
            cos_t, sin_t = cos_ref[rows, :], sin_ref[rows, :]
            for h in range(RET_HEADS):
                q = _rotary(q_ref[rows, _qk_cols(h)], cos_t, sin_t) * (RET_QK_DIM ** -0.5)
                k = _rotary(k_ref[rows, _qk_cols(h)], cos_t, sin_t)
                v = v_ref[rows, _v_cols(h)]
                prev = state_ref[h]
                scores = _dot(q.astype(BF16), k.astype(BF16), _NT) * dec_ref[h]
                inner = _dot(scores.astype(BF16), v, _NN)
                cross = _dot((q * qd_ref[h]).astype(BF16), prev.astype(BF16), _NN)
                o = inner + cross
                st_ref[h, c] = prev
                kv = _dot((k * kd_ref[h]).astype(BF16), v, _TN)
                state_ref[h] = prev * cd_ref[h, 0:1, 0:1] + kv
                o_ref[rows, _v_cols(h)] = o
                normed, _ = _head_norm(o)
                gate = g_ref[rows, _v_cols(h)]
                y_ref[rows, _v_cols(h)] = (gate * jax.nn.sigmoid(gate) * normed).astype(BF16)

    return pl.pallas_call(
        body,
        name="ret_fwd",
        grid=(steps,),
        in_specs=[q_spec, k_spec, vv, vv, pos, pos, per_head, per_head, per_head, c_dec],
        out_specs=[vv, vv, state],
        out_shape=[jax.ShapeDtypeStruct((T, RET_V_WIDTH), BF16),
                   jax.ShapeDtypeStruct((T, RET_V_WIDTH), F32),
                   jax.ShapeDtypeStruct((RET_HEADS, nc, RET_QK_DIM, RET_V_DIM), F32)],
        scratch_shapes=[pltpu.VMEM((RET_HEADS, RET_QK_DIM, RET_V_DIM), F32)],
        compiler_params=_params(("arbitrary",)),
    )(h_b, h_b, h_c, h_d, *tables)


def _ret_bwd(d_y, o_pre, states, h_b, h_c, h_d, tables, rider=None):
    T = h_b.shape[0]
    nc = T // RET_CHUNK
    per_step = _pick(nc, (RET_STEP_CHUNKS, 1))
    steps = nc // per_step
    q_spec, k_spec, vv, pos, per_head, c_dec, state = _ret_specs(steps, per_step, True)
    n_ride = rider.n if rider is not None else 0

    def body(*refs):
        (dy_ref, o_ref, st_ref, q_ref, k_ref, v_ref, g_ref, cos_ref, sin_ref, dec_ref, kd_ref, qd_ref,
         cd_ref) = refs[:13]
        dq_ref, dk_ref, dv_ref, dg_ref = refs[13 + n_ride:17 + n_ride]
        carry_ref = refs[17 + 2 * n_ride]
        ids = [pl.program_id(0)]
        if rider is not None:
            ride = (refs[13:13 + n_ride], refs[17 + n_ride:17 + 2 * n_ride], refs[-3:])
            rider.start_at_first(ids, ride)

        @pl.when(ids[0] == 0)
        def _():
            carry_ref[...] = jnp.zeros_like(carry_ref)

        scale = RET_QK_DIM ** -0.5
        for c in reversed(range(per_step)):
            rows = pl.ds(c * RET_CHUNK, RET_CHUNK)
            cos_t, sin_t = cos_ref[rows, :], sin_ref[rows, :]
            for h in range(RET_HEADS):
                q = _rotary(q_ref[rows, _qk_cols(h)], cos_t, sin_t) * scale
                k = _rotary(k_ref[rows, _qk_cols(h)], cos_t, sin_t)
                v = v_ref[rows, _v_cols(h)]
                decay, k_dec, q_dec = dec_ref[h], kd_ref[h], qd_ref[h]
                chunk_decay = cd_ref[h, 0:1, 0:1]
                state = st_ref[h, c].astype(BF16)
                later = carry_ref[h]
                later_b = later.astype(BF16)

                gate = g_ref[rows, _v_cols(h)]
                sig = jax.nn.sigmoid(gate)
                silu = gate * sig
                normed, rstd = _head_norm(o_ref[rows, _v_cols(h)])
                d_y = dy_ref[rows, _v_cols(h)]
                dg_ref[rows, _v_cols(h)] = (d_y * normed * (sig * (1.0 + gate * (1.0 - sig)))).astype(BF16)
                d_n = d_y * silu
                d_o = rstd * (d_n - jnp.mean(d_n, axis=1, keepdims=True)
                              - normed * jnp.mean(d_n * normed, axis=1, keepdims=True))
                d_ob = d_o.astype(BF16)

                qb, kb = q.astype(BF16), k.astype(BF16)
                qd_b, kd_b = (q * q_dec).astype(BF16), (k * k_dec).astype(BF16)
                scores = _dot(qb, kb, _NT) * decay
                d_scores = (_dot(d_ob, v, _NT) * decay).astype(BF16)
                dq = _dot(d_scores, kb, _NN) + _dot(d_ob, state, _NT) * q_dec
                dk = _dot(d_scores, qb, _TN) + _dot(v, later_b, _NT) * k_dec
                dv = _dot(scores.astype(BF16), d_ob, _TN) + _dot(kd_b, later_b, _NN)
                carry_ref[h] = _dot(qd_b, d_ob, _TN) + chunk_decay * later
                dq_ref[rows, _qk_cols(h)] = _rotary_transpose(dq * scale, cos_t, sin_t).astype(BF16)
                dk_ref[rows, _qk_cols(h)] = _rotary_transpose(dk, cos_t, sin_t).astype(BF16)
                dv_ref[rows, _v_cols(h)] = dv.astype(BF16)
        if rider is not None:
            rider.wait_at_last(ids, (steps,), ride)

    qk_out = pl.BlockSpec((per_step * RET_CHUNK, RET_QK_WIDTH), lambda n: (steps - 1 - n, 0))
    in_specs = [vv, vv, state, q_spec, k_spec, vv, vv, pos, pos, per_head, per_head, per_head, c_dec]
    out_specs = [qk_out, qk_out, vv, vv]
    out_shape = [jax.ShapeDtypeStruct((T, RET_QK_WIDTH), BF16), jax.ShapeDtypeStruct((T, RET_QK_WIDTH), BF16),
                 jax.ShapeDtypeStruct((T, RET_V_WIDTH), BF16), jax.ShapeDtypeStruct((T, RET_V_WIDTH), BF16)]
    args = (d_y, o_pre, states, h_b, h_b, h_c, h_d) + tuple(tables)
    scratch = [pltpu.VMEM((RET_HEADS, RET_QK_DIM, RET_V_DIM), F32)]
    if rider is not None:
        in_specs, args = in_specs + rider.specs, args + tuple(rider.bufs)
        out_specs, out_shape = out_specs + rider.specs, out_shape + rider.out_shape
        scratch = scratch + rider.scratch
    outs = pl.pallas_call(
        body,
        name="ret_bwd",
        grid=(steps,),
        in_specs=in_specs,
        out_specs=out_specs,
        out_shape=out_shape,
        scratch_shapes=scratch,
        compiler_params=_params(("arbitrary",)),
    )(*args)
    return outs[0], outs[1], outs[2], outs[3], list(outs[4:])


def _proj_tiles(h, x):
    return h[:, 0:1536], h[:, 1536:2560], h[:, 2560:3584], h[:, 3584:4608], h[:, 4608:6656], x


def _gate_mix_tiles(y_ret, h_e, b_gate, y_sb):
    gates = jax.nn.sigmoid(h_e + b_gate)
    return y_ret, gates[:, :D_MODEL] * y_sb + gates[:, D_MODEL:] * y_ret


def _gate_mix_grad_tiles(d_mix, h_e, b_gate, y_sb, y_ret):
    gates = jax.nn.sigmoid(h_e + b_gate)
    g0, g1 = gates[:, :D_MODEL], gates[:, D_MODEL:]
    d_e = jnp.concatenate([d_mix * y_sb * g0 * (1.0 - g0), d_mix * y_ret * g1 * (1.0 - g1)], axis=1)
    return d_mix * g0, d_mix * g1, d_e, d_e


def _ln_stats(u):
    mu = jnp.mean(u, axis=1, keepdims=True)
    cen = u - mu
    var = jnp.mean(cen * cen, axis=1, keepdims=True)
    rstd = lax.rsqrt(var + LN_EPS)
    return cen * rstd, rstd


def _ln_input_grad(d_out, gain, xhat, rstd):
    d_hat = d_out * gain
    return rstd * (d_hat - jnp.mean(d_hat, axis=1, keepdims=True)
                   - xhat * jnp.mean(d_hat * xhat, axis=1, keepdims=True))


def _ln_tiles(sub, x_prev, gain, bias):
    xhat, rstd = _ln_stats(DN_ALPHA * x_prev + sub)
    out = xhat * gain + bias
    return out, out, xhat, rstd


def _residual_tiles(d_sub, res):
    return (d_sub + DN_ALPHA * res,)


def _ln_grad_tiles(d_sub, res, xhat, rstd, gain):
    d_out = d_sub + DN_ALPHA * res
    du = _ln_input_grad(d_out, gain, xhat, rstd)
    return du, du, d_out * xhat, d_out


def _ln_loss_tiles(sub, x_prev, gain, bias, target):
    xhat, rstd = _ln_stats(DN_ALPHA * x_prev + sub)
    diff = xhat * gain + bias - target
    d_out = diff * (1.0 / D_MODEL)
    du = _ln_input_grad(d_out, gain, xhat, rstd)
    return du, du, diff * diff, d_out * xhat, d_out


def _mem_probs(q_h, k_h):
    s = _dot(q_h, k_h, _NT) * (MEM_HEAD_DIM ** -0.5)
    e = jnp.exp(s - jnp.max(s, axis=1, keepdims=True))
    return e / jnp.sum(e, axis=1, keepdims=True)


def _xattn_fwd(q, kv):
    T, mem_len = q.shape[0], kv.shape[0]
    tq = _pick(T, (512, 256, 128))

    def body(q_ref, kv_ref, o_ref):
        for h in range(MEM_HEADS):
            cols = slice(h * MEM_HEAD_DIM, (h + 1) * MEM_HEAD_DIM)
            vcols = slice(D_MODEL + h * MEM_HEAD_DIM, D_MODEL + (h + 1) * MEM_HEAD_DIM)
            p = _mem_probs(q_ref[:, cols], kv_ref[:, cols])
            o_ref[:, cols] = _dot(p.astype(BF16), kv_ref[:, vcols], _NN).astype(BF16)

    return pl.pallas_call(
        body,
        name="xattn_fwd",
        grid=(T // tq,),
        in_specs=[pl.BlockSpec((tq, D_MODEL), lambda i: (i, 0)),
                  pl.BlockSpec((mem_len, 2 * D_MODEL), lambda i: (0, 0))],
        out_specs=pl.BlockSpec((tq, D_MODEL), lambda i: (i, 0)),
        out_shape=jax.ShapeDtypeStruct((T, D_MODEL), BF16),
        compiler_params=_params(("parallel",)),
    )(q, kv)


def _xattn_bwd(q, kv, d_o):
    T, mem_len = q.shape[0], kv.shape[0]
    tq = _pick(T, (512, 256, 128))

    def body(q_ref, kv_ref, do_ref, dq_ref, dkv_ref):
        @pl.when(pl.program_id(0) == 0)
        def _():
            dkv_ref[...] = jnp.zeros_like(dkv_ref)

        for h in range(MEM_HEADS):
            cols = slice(h * MEM_HEAD_DIM, (h + 1) * MEM_HEAD_DIM)
            vcols = slice(D_MODEL + h * MEM_HEAD_DIM, D_MODEL + (h + 1) * MEM_HEAD_DIM)
            q_h, k_h, do_h = q_ref[:, cols], kv_ref[:, cols], do_ref[:, cols]
            p = _mem_probs(q_h, k_h)
            dp = _dot(do_h, kv_ref[:, vcols], _NT)
            ds = p * (dp - jnp.sum(dp * p, axis=1, keepdims=True))
            dsb = (ds * (MEM_HEAD_DIM ** -0.5)).astype(BF16)
            dq_ref[:, cols] = _dot(dsb, k_h, _NN).astype(BF16)
            dkv_ref[:, cols] += _dot(dsb, q_h, _TN)
            dkv_ref[:, vcols] += _dot(p.astype(BF16), do_h, _TN)

    row = pl.BlockSpec((tq, D_MODEL), lambda i: (i, 0))
    full = pl.BlockSpec((mem_len, 2 * D_MODEL), lambda i: (0, 0))
    return pl.pallas_call(
        body,
        name="xattn_bwd",
        grid=(T // tq,),
        in_specs=[row, full, row],
        out_specs=[row, full],
        out_shape=[jax.ShapeDtypeStruct((T, D_MODEL), BF16), jax.ShapeDtypeStruct((mem_len, 2 * D_MODEL), F32)],
        compiler_params=_params(("arbitrary",)),
    )(q, kv, d_o)


def _swiglu_tiles(f):
    a, b = f[:, :FFN_HIDDEN], f[:, FFN_HIDDEN:]
    return f, a * jax.nn.sigmoid(a) * b


def _swiglu_grad_tiles(d_hidden, f):
    a, b = f[:, :FFN_HIDDEN], f[:, FFN_HIDDEN:]
    sig = jax.nn.sigmoid(a)
    return (jnp.concatenate([d_hidden * b * (sig * (1.0 + a * (1.0 - sig))), d_hidden * (a * sig)], axis=1),)


def _local_step(x, mem, w_in, small, target, fetch_rest, ship):
    T = x.shape[0]
    tables = _ret_tables(T)
    memb = mem.astype(BF16)

    h_a, h_b, h_c, h_d, h_e, xb = _mm_fused(
        x, w_in, mode="nn", name="proj_in", extras=[], pass_a=True,
        outs=[(1536, BF16), (1024, F32), (1024, BF16), (1024, F32), (2048, F32), (D_MODEL, BF16)],
        epilogue=_proj_tiles, max_rows=256)
    (a_sb, r_mat, _), w = fetch_rest(lambda rider: _sb_fwd(h_a, rider))
    y_gated, o_pre, states = _ret_fwd(h_b, h_c, h_d, tables)
    y_sb = _mm(a_sb, w["w_sb_o"], mode="nn", out_dtype=F32, name="sb_out")
    row_f32, row_bf16 = (D_MODEL, F32), (D_MODEL, BF16)
    ln_outs = [row_f32, row_bf16, row_f32, (1, F32)]
    y_ret, mix_in = _mm_fused(y_gated, w["w_ret_o"], mode="nn", name="ret_out", extras=[h_e, small["b_gate"], y_sb],
                              outs=[row_f32, row_bf16], epilogue=_gate_mix_tiles)
    x1, x1b, xhat1, rstd1 = _mm_fused(mix_in, w["w_mix_o"], mode="nn", name="mix_out",
                                      extras=[x, small["ln1_g"], small["ln1_b"]], outs=ln_outs, epilogue=_ln_tiles)
    q_m = _mm(x1b, w["w_mem_q"], mode="nn", out_dtype=BF16, name="mem_q")
    kv_m = _mm(memb, w["w_mem_kv"], mode="nn", out_dtype=BF16, name="mem_kv")
    o_m = _xattn_fwd(q_m, kv_m)
    x2, x2b, xhat2, rstd2 = _mm_fused(o_m, w["w_mem_o"], mode="nn", name="mem_out",
                                      extras=[x1, small["ln2_g"], small["ln2_b"]], outs=ln_outs, epilogue=_ln_tiles)
    f, hidden = _mm_fused(x2b, w["w_ffn_in"], mode="nn", name="ffn_in", extras=[],
                          outs=[(2 * FFN_HIDDEN, F32), (FFN_HIDDEN, BF16)], epilogue=_swiglu_tiles)
    du_outs, col = [row_f32, row_bf16], D_MODEL
    du3, du3b, loss_cols, d_ln3_g, d_ln3_b = _mm_fused(
        hidden, w["w_ffn_out"], mode="nn", name="ffn_out", extras=[x2, small["ln3_g"], small["ln3_b"], target],
        outs=du_outs, sums=[col, col, col], epilogue=_ln_loss_tiles, max_rows=256)

    g_ffn_out = _mm(hidden, du3b, mode="tn", out_dtype=BF16, name="g_ffn_out")
    (d_f,) = _mm_fused(du3b, w["w_ffn_out"], mode="nt", name="d_hidden", extras=[f],
                       outs=[(2 * FFN_HIDDEN, BF16)], epilogue=_swiglu_grad_tiles)
    g_ffn_in = _mm(x2b, d_f, mode="tn", out_dtype=BF16, name="g_ffn_in")
    du2, du2b, d_ln2_g, d_ln2_b = ship(
        {"w_ffn_out": g_ffn_out},
        lambda rider: _as_host(rider, _mm_fused(
            d_f, w["w_ffn_in"], mode="nt", name="d_x2", extras=[du3, xhat2, rstd2, small["ln2_g"]], outs=du_outs,
            sums=[col, col], epilogue=_ln_grad_tiles, rider=rider, max_rows=256)))
    g_mem_o = _mm(o_m, du2b, mode="tn", out_dtype=BF16, name="g_mem_o")
    d_om = _mm(du2b, w["w_mem_o"], mode="nt", out_dtype=BF16, name="d_om")
    d_qm, d_kvm = _xattn_bwd(q_m, kv_m, d_om)
    g_mem_q = _mm(x1b, d_qm, mode="tn", out_dtype=BF16, name="g_mem_q")
    g_mem_kv = _mm(memb, d_kvm.astype(BF16), mode="tn", out_dtype=BF16, name="g_mem_kv")
    du1, du1b, d_ln1_g, d_ln1_b = _mm_fused(
        d_qm, w["w_mem_q"], mode="nt", name="d_x1", extras=[du2, xhat1, rstd1, small["ln1_g"]], outs=du_outs,
        sums=[col, col], epilogue=_ln_grad_tiles, max_rows=256)
    g_mix_o = _mm(mix_in, du1b, mode="tn", out_dtype=BF16, name="g_mix_o")
    d_ysb, d_yret, d_e, d_b_gate = _mm_fused(
        du1b, w["w_mix_o"], mode="nt", name="d_mix_in", extras=[h_e, small["b_gate"], y_sb, y_ret],
        outs=[row_bf16, row_bf16, (2 * D_MODEL, BF16)], sums=[2 * D_MODEL], epilogue=_gate_mix_grad_tiles,
        max_rows=256)
    g_sb_o = _mm(a_sb, d_ysb, mode="tn", out_dtype=BF16, name="g_sb_o")
    g_ret_o = _mm(y_gated, d_yret, mode="tn", out_dtype=BF16, name="g_ret_o")
    d_asb = _mm(d_ysb, w["w_sb_o"], mode="nt", out_dtype=BF16, name="d_asb")
    d_ygated = _mm(d_yret, w["w_ret_o"], mode="nt", out_dtype=F32, name="d_ygated")
    small_grads = {"b_gate": d_b_gate, "ln1_g": d_ln1_g, "ln1_b": d_ln1_b, "ln2_g": d_ln2_g, "ln2_b": d_ln2_b,
                   "ln3_g": d_ln3_g, "ln3_b": d_ln3_b, "loss_cols": loss_cols}
    d_rq, d_rk, d_c, d_d = ship({"w_mem_kv": g_mem_kv, "w_mem_q": g_mem_q, "w_mem_o": g_mem_o, "w_mix_o": g_mix_o},
                                lambda rider: _ret_bwd(d_ygated, o_pre, states, h_b, h_c, h_d, tables, rider))
    d_q, d_k, d_v = ship({"w_ffn_in": g_ffn_in, "w_ret_o": g_ret_o, "w_sb_o": g_sb_o, "small": small_grads},
                         lambda rider: _sb_bwd(h_a, d_asb, r_mat, rider))
    d_h = [("sb_q", d_q), ("sb_k", d_k), ("sb_v", d_v), ("ret_q", d_rq), ("ret_k", d_rk), ("ret_v", d_c),
           ("ret_g", d_d), ("gate", d_e)]
    g_in = jnp.concatenate([_mm(xb, piece, mode="tn", out_dtype=BF16, name="g_in_" + tag) for tag, piece in d_h],
                           axis=1)
    (d_x,) = ship({"w_in": g_in},
                  lambda rider: _as_host(rider, _mm_fused(
                      [piece for _, piece in d_h], w_in, mode="nt", name="d_x", extras=[du1], outs=[(D_MODEL, F32)],
                      epilogue=_residual_tiles, rider=rider, max_rows=256)))
    return d_x


def _adamw_math(w, g, m, v):
    m = ADAM_B1 * m + (1.0 - ADAM_B1) * g
    v = ADAM_B2 * v + (1.0 - ADAM_B2) * jnp.square(g)
    m_hat = m / (1.0 - ADAM_B1 ** ADAM_STEP)
    v_hat = v / (1.0 - ADAM_B2 ** ADAM_STEP)
    delta = -ADAM_LR * (m_hat / (jnp.sqrt(v_hat) + ADAM_EPS) + ADAM_WD * w)
    return delta, m, v


def _adamw(parts, w, m, v, name):
    R, C = w.shape
    tr = max(t for t in range(16, min(R, 256) + 1, 16) if R % t == 0) if R >= 16 else R

    def body(p_ref, w_ref, m_ref, v_ref, g_ref, d_ref, nm_ref, nv_ref):
        g = p_ref[0].astype(F32)
        for j in range(1, N_DEV):
            g = g + p_ref[j].astype(F32)
        delta, nm, nv = _adamw_math(w_ref[...], g, m_ref[...], v_ref[...])
        g_ref[...] = g
        d_ref[...] = delta
        nm_ref[...] = nm
        nv_ref[...] = nv

    blk = pl.BlockSpec((tr, C), lambda i: (i, 0))
    out = jax.ShapeDtypeStruct((R, C), F32)
    return pl.pallas_call(
        body,
        name=name,
        grid=(R // tr,),
        in_specs=[pl.BlockSpec((N_DEV, tr, C), lambda i: (0, i, 0)), blk, blk, blk],
        out_specs=[blk] * 4,
        out_shape=[out] * 4,
        compiler_params=_params(("parallel",)),
    )(parts, w, m, v)


_SHARD_AXIS = {"w_in": 1, "w_sb_o": 1, "w_ret_o": 0, "w_mix_o": 0, "w_mem_q": 0, "w_mem_kv": 1, "w_mem_o": 0,
               "w_ffn_in": 1, "w_ffn_out": 0}
_MATRICES = tuple(_SHARD_AXIS)
_SMALL = ("b_gate", "ln1_g", "ln1_b", "ln2_g", "ln2_b", "ln3_g", "ln3_b")
_WEIGHT_ORDER = ("w_in", "b_gate", "w_sb_o", "w_ret_o", "w_mix_o", "ln1_g", "ln1_b", "w_mem_q", "w_mem_kv", "w_mem_o",
                 "ln2_g", "ln2_b", "w_ffn_in", "w_ffn_out", "ln3_g", "ln3_b")


def _assemble(name, gathered):
    if _SHARD_AXIS[name] == 0:
        return gathered.reshape(-1, gathered.shape[2])
    return jnp.transpose(gathered, (1, 0, 2)).reshape(gathered.shape[1], -1)


def _to_slots(name, full):
    if _SHARD_AXIS[name] == 0:
        return full.reshape(N_DEV, full.shape[0] // N_DEV, full.shape[1])
    return jnp.transpose(full.reshape(full.shape[0], N_DEV, full.shape[1] // N_DEV), (1, 0, 2))


SMALL_ROWS = 16


def _pack_small(vals):
    return jnp.concatenate([vals["b_gate"].reshape(2, D_MODEL)] + [vals[n] for n in _SMALL[1:]], axis=0)


def _unpack_small(packed):
    out = {"b_gate": packed[0:2].reshape(1, 2 * D_MODEL)}
    for i, n in enumerate(_SMALL[1:]):
        out[n] = packed[2 + i:3 + i]
    return out


def kernel(x, mem, w_in, b_gate, w_sb_o, w_ret_o, w_mix_o, ln1_g, ln1_b, w_mem_q, w_mem_kv, w_mem_o, ln2_g, ln2_b, w_ffn_in, w_ffn_out, ln3_g, ln3_b, loss_target, m_w_in, m_b_gate, m_w_sb_o, m_w_ret_o, m_w_mix_o, m_ln1_g, m_ln1_b, m_w_mem_q, m_w_mem_kv, m_w_mem_o, m_ln2_g, m_ln2_b, m_w_ffn_in, m_w_ffn_out, m_ln3_g, m_ln3_b, v_w_in, v_b_gate, v_w_sb_o, v_w_ret_o, v_w_mix_o, v_ln1_g, v_ln1_b, v_w_mem_q, v_w_mem_kv, v_w_mem_o, v_ln2_g, v_ln2_b, v_w_ffn_in, v_w_ffn_out, v_ln3_g, v_ln3_b):
    weights = dict(w_in=w_in, b_gate=b_gate, w_sb_o=w_sb_o, w_ret_o=w_ret_o, w_mix_o=w_mix_o, ln1_g=ln1_g, ln1_b=ln1_b,
                   w_mem_q=w_mem_q, w_mem_kv=w_mem_kv, w_mem_o=w_mem_o, ln2_g=ln2_g, ln2_b=ln2_b, w_ffn_in=w_ffn_in,
                   w_ffn_out=w_ffn_out, ln3_g=ln3_g, ln3_b=ln3_b)
    mom1 = dict(w_in=m_w_in, b_gate=m_b_gate, w_sb_o=m_w_sb_o, w_ret_o=m_w_ret_o, w_mix_o=m_w_mix_o, ln1_g=m_ln1_g,
                ln1_b=m_ln1_b, w_mem_q=m_w_mem_q, w_mem_kv=m_w_mem_kv, w_mem_o=m_w_mem_o, ln2_g=m_ln2_g, ln2_b=m_ln2_b,
                w_ffn_in=m_w_ffn_in, w_ffn_out=m_w_ffn_out, ln3_g=m_ln3_g, ln3_b=m_ln3_b)
    mom2 = dict(w_in=v_w_in, b_gate=v_b_gate, w_sb_o=v_w_sb_o, w_ret_o=v_w_ret_o, w_mix_o=v_w_mix_o, ln1_g=v_ln1_g,
                ln1_b=v_ln1_b, w_mem_q=v_w_mem_q, w_mem_kv=v_w_mem_kv, w_mem_o=v_w_mem_o, ln2_g=v_ln2_g, ln2_b=v_ln2_b,
                w_ffn_in=v_w_ffn_in, w_ffn_out=v_w_ffn_out, ln3_g=v_ln3_g, ln3_b=v_ln3_b)

    (gathered_in,) = _exchange([weights["w_in"][0].astype(BF16)], False, "gather_w_in")
    rest = [n for n in _MATRICES if n != "w_in"]
    received = {}

    def fetch_rest(host):
        res = host(_Rider([weights[n][0].astype(BF16) for n in rest], False))
        return res, {n: _assemble(n, g) for n, g in zip(rest, res[-1])}

    def ship(grads, host):
        names = list(grads)
        bufs = []
        for n in names:
            if n == "small":
                part = jnp.concatenate([_pack_small(grads[n]), grads[n]["loss_cols"],
                                        jnp.zeros((SMALL_ROWS - 9, D_MODEL), F32)], axis=0)
                bufs.append(jnp.broadcast_to(part[None], (N_DEV,) + part.shape))
            else:
                bufs.append(_to_slots(n, grads[n]).astype(BF16))
        res = host(_Rider(bufs, True))
        received.update(zip(names, res[-1]))
        return res[:-1]

    small = {n: weights[n] for n in _SMALL}
    d_x = _local_step(x[0], mem[0], _assemble("w_in", gathered_in), small, loss_target[0], fetch_rest, ship)

    new = {}
    for n in _MATRICES:
        new[n] = _adamw(received[n], weights[n][0], mom1[n][0], mom2[n][0], "adamw_" + n)
    packed = _adamw(received["small"][:, :8], _pack_small({n: weights[n] for n in _SMALL}),
                    _pack_small({n: mom1[n] for n in _SMALL}), _pack_small({n: mom2[n] for n in _SMALL}), "adamw_small")
    small_new = [_unpack_small(p) for p in packed]
    loss = jnp.sum(received["small"][:, 8]) * (0.5 / D_MODEL)

    outs = [loss, d_x[None]]
    for slot in range(4):
        for n in _WEIGHT_ORDER:
            outs.append(new[n][slot][None] if n in new else small_new[slot][n])
    return tuple(outs)
```

```python
import functools
import math

import jax
import jax.numpy as jnp
from jax import lax
from jax.experimental import pallas as pl
from jax.experimental.pallas import tpu as pltpu

F32 = jnp.float32
BF16 = jnp.bfloat16

N_DEV = 8
D_MODEL = 1024
SB_HEAD_DIM = 64
SB_WIDTH = 512
RET_HEADS = 4
RET_QK_DIM = 128
RET_V_DIM = 256
RET_QK_WIDTH = 512
RET_V_WIDTH = 1024
RET_CHUNK = 128
RET_STEP_CHUNKS = 2
ROPE_BASE = 10000.0
MEM_HEADS = 4
MEM_HEAD_DIM = 256
FFN_HIDDEN = 2816
DN_ALPHA = 2.0 ** 0.25
LN_EPS = 1e-5
ADAM_LR = 0.001
ADAM_B1 = 0.9
ADAM_B2 = 0.999
ADAM_EPS = 1e-08
ADAM_WD = 0.01
ADAM_STEP = 10

VMEM_LIMIT_BYTES = 52 * 1024 * 1024
LANES = 128
SB_KEY_BLOCK = 128
SB_Q_BLOCK = 256
SB_DEAD_LOG = -105.0

MESH_AXES = ("x", "y", "c")


def _pick(dim, prefs):
    for p in prefs:
        if dim % p == 0:
            return p
    return dim


def _params(sem):
    return pltpu.CompilerParams(dimension_semantics=sem, vmem_limit_bytes=VMEM_LIMIT_BYTES)


def _dot(a, b, dims):
    return lax.dot_general(a, b, (dims, ((), ())), preferred_element_type=F32)


_NN = ((1,), (0,))
_NT = ((1,), (1,))
_TN = ((0,), (0,))


def _my_index():
    return 4 * lax.axis_index("x") + 2 * lax.axis_index("y") + lax.axis_index("c")


def _peer(k):
    x, y, c = lax.axis_index("x"), lax.axis_index("y"), lax.axis_index("c")
    bx, by, bc = (k >> 2) & 1, (k >> 1) & 1, k & 1
    px = (1 - x) if bx else x
    py = (1 - y) if by else y
    pc = (1 - c) if bc else c
    return (px, py, pc), 4 * px + 2 * py + pc


class _Rider:
    def __init__(self, bufs, scatter):
        self.bufs, self.scatter, self.n = list(bufs), scatter, len(bufs)
        self.specs = [pl.BlockSpec(memory_space=pl.ANY)] * self.n
        self.out_shape = [jax.ShapeDtypeStruct(b.shape if scatter else (N_DEV,) + b.shape, b.dtype) for b in self.bufs]
        self.scratch = [pltpu.SemaphoreType.DMA((self.n, N_DEV - 1)), pltpu.SemaphoreType.DMA((self.n, N_DEV - 1)),
                        pltpu.SemaphoreType.DMA((self.n,))]

    def _remote(self, ride, a, k, src_ref, slot, to):
        _, dst, (send_sems, recv_sems, _) = ride
        return pltpu.make_async_remote_copy(src_ref=src_ref, dst_ref=dst[a].at[slot], send_sem=send_sems.at[a, k],
                                            recv_sem=recv_sems.at[a, k], device_id=to,
                                            device_id_type=pl.DeviceIdType.MESH)

    def _local(self, ride, a):
        src, dst, (_, _, local_sems) = ride
        me = _my_index()
        return pltpu.make_async_copy(src[a].at[me] if self.scatter else src[a], dst[a].at[me], local_sems.at[a])

    def _direct(self, ride, a):
        src = ride[0]
        me = _my_index()
        out = []
        for k in range(1, N_DEV):
            peer, peer_idx = _peer(k)
            out.append(self._remote(ride, a, k - 1, src[a].at[peer_idx], me, peer))
        return out

    def _two_level(self, ride, a):
        src, dst = ride[0], ride[1]
        x, y, c = lax.axis_index("x"), lax.axis_index("y"), lax.axis_index("c")
        me, sibling = _my_index(), (x, y, 1 - c)
        chips = [(1 - x, y), (x, 1 - y), (1 - x, 1 - y)]
        first = [self._remote(ride, a, 0, src[a], me, sibling)]
        passed, landing = [], [self._remote(ride, a, 0, src[a], me + 1 - 2 * c, sibling)]
        for j, (px, py) in enumerate(chips):
            first.append(self._remote(ride, a, 1 + j, src[a], me, (px, py, c)))
            theirs = 4 * px + 2 * py + c
            passed.append(self._remote(ride, a, 4 + j, dst[a].at[theirs], theirs, sibling))
            landing.append(self._remote(ride, a, 1 + j, src[a], theirs, (px, py, c)))
        for j, (px, py) in enumerate(chips):
            landing.append(self._remote(ride, a, 4 + j, src[a], 4 * px + 2 * py + 1 - c, sibling))
        return first, passed, landing

    def start(self, ride):
        for a in range(self.n):
            self._local(ride, a).start()
            for cp in (self._direct(ride, a) if self.scatter else self._two_level(ride, a)[0]):
                cp.start()

    def finish(self, ride):
        if self.scatter:
            for a in range(self.n):
                for cp in self._direct(ride, a):
                    cp.wait()
                self._local(ride, a).wait()
            return
        levels = [self._two_level(ride, a) for a in range(self.n)]
        for first, passed, landing in levels:
            for j, cp in enumerate(passed):
                landing[1 + j].wait_recv()
                cp.start()
        for a, (first, passed, landing) in enumerate(levels):
            landing[0].wait_recv()
            for cp in landing[4:]:
                cp.wait_recv()
            for cp in first + passed:
                cp.wait_send()
            self._local(ride, a).wait()

    def start_at_first(self, ids, ride):
        first = functools.reduce(jnp.logical_and, [i == 0 for i in ids])

        @pl.when(first)
        def _():
            self.start(ride)

    def wait_at_last(self, ids, grid, ride):
        last = functools.reduce(jnp.logical_and, [i == g - 1 for i, g in zip(ids, grid)])

        @pl.when(last)
        def _():
            self.finish(ride)


def _exchange(bufs, scatter, name):
    rider = _Rider(bufs, scatter)

    def body(*refs):
        ride = (refs[:rider.n], refs[rider.n:2 * rider.n], refs[2 * rider.n:])
        rider.start(ride)
        rider.finish(ride)

    return pl.pallas_call(
        body,
        name=name,
        in_specs=rider.specs,
        out_specs=rider.specs,
        out_shape=rider.out_shape,
        scratch_shapes=rider.scratch,
    )(*rider.bufs)


MM_RESIDENT_B_BYTES = 14 * 1024 * 1024
MM_A_TILE_BYTES = 4 * 1024 * 1024
MM_OUT_TILE_BYTES = 6 * 1024 * 1024


def _mm_tiles(mode, M, N, K, a_bytes, out_bytes):
    if mode != "tn" and K * N * 2 <= MM_RESIDENT_B_BYTES:
        for tm in (1024, 512, 256, 128):
            if M % tm == 0 and tm * K * a_bytes <= MM_A_TILE_BYTES and tm * N * out_bytes <= MM_OUT_TILE_BYTES:
                return tm, N, K
    if mode == "tn":
        return (_pick(M, (1024, 1408, 512, 256, 128)), _pick(N, (1024, 1664, 1408, 512, 256, 128)),
                _pick(K, (2048, 1024, 512, 256, 128)))
    return _pick(M, (1024, 512, 256, 128)), _pick(N, (512, 256, 128)), _pick(K, (1024, 512, 256, 128))


def _mm(a, b, *, mode, out_dtype, name, res=None, res_scale=1.0, rider=None):
    if mode == "nn":
        (M, K), (K2, N) = a.shape, b.shape
    elif mode == "nt":
        (M, K), (N, K2) = a.shape, b.shape
    else:
        (K, M), (K2, N) = a.shape, b.shape
    assert K == K2, (a.shape, b.shape, mode)
    out_bytes = jnp.dtype(out_dtype).itemsize + (4 if res is not None else 0)
    tm, tn, tk = _mm_tiles(mode, M, N, K, a.dtype.itemsize, out_bytes)
    grid = (M // tm, N // tn, K // tk)
    nk = grid[2]
    dims = {"nn": _NN, "nt": _NT, "tn": _TN}[mode]
    n_in = 2 + (res is not None)
    n_ride = rider.n if rider is not None else 0

    def body(*refs):
        a_ref, b_ref = refs[:2]
        r_ref = refs[2] if res is not None else None
        o_ref = refs[n_in + n_ride]
        rest = refs[n_in + 2 * n_ride + 1:]
        acc_ref = rest[0] if nk > 1 else None
        ids = [pl.program_id(d) for d in range(3)]
        if rider is not None:
            ride = (refs[n_in:n_in + n_ride], refs[n_in + n_ride + 1:n_in + 2 * n_ride + 1], rest[-3:])
            rider.start_at_first(ids, ride)
        part = _dot(a_ref[...].astype(BF16), b_ref[...].astype(BF16), dims)

        def finish(total):
            if r_ref is not None:
                total = total + res_scale * r_ref[...]
            o_ref[...] = total.astype(out_dtype)

        if nk == 1:
            finish(part)
        else:
            k = ids[2]

            @pl.when(k == 0)
            def _():
                acc_ref[...] = part

            @pl.when(k > 0)
            def _():
                acc_ref[...] += part

            @pl.when(k == nk - 1)
            def _():
                finish(acc_ref[...])

        if rider is not None:
            rider.wait_at_last(ids, grid, ride)

    if mode == "nn":
        a_spec = pl.BlockSpec((tm, tk), lambda i, j, k: (i, k))
        b_spec = pl.BlockSpec((tk, tn), lambda i, j, k: (k, j))
    elif mode == "nt":
        a_spec = pl.BlockSpec((tm, tk), lambda i, j, k: (i, k))
        b_spec = pl.BlockSpec((tn, tk), lambda i, j, k: (j, k))
    else:
        a_spec = pl.BlockSpec((tk, tm), lambda i, j, k: (k, i))
        b_spec = pl.BlockSpec((tk, tn), lambda i, j, k: (k, j))
    o_spec = pl.BlockSpec((tm, tn), lambda i, j, k: (i, j))
    in_specs = [a_spec, b_spec] + ([o_spec] if res is not None else [])
    args = (a, b) + ((res,) if res is not None else ())
    out_specs, out_shape = [o_spec], [jax.ShapeDtypeStruct((M, N), out_dtype)]
    scratch = [pltpu.VMEM((tm, tn), F32)] if nk > 1 else []
    sem = ("parallel", "parallel", "arbitrary")
    if rider is not None:
        in_specs, args = in_specs + rider.specs, args + tuple(rider.bufs)
        out_specs, out_shape = out_specs + rider.specs, out_shape + rider.out_shape
        scratch = scratch + rider.scratch
        sem = ("arbitrary",) * 3
    outs = pl.pallas_call(
        body,
        name=name,
        grid=grid,
        in_specs=in_specs,
        out_specs=out_specs,
        out_shape=out_shape,
        scratch_shapes=scratch,
        compiler_params=_params(sem),
    )(*args)
    return outs[0] if rider is None else (outs[0], list(outs[1:]))


def _mm_host(a, b, *, rider, **kw):
    out = _mm(a, b, rider=rider, **kw)
    return out if rider is not None else (out, [])


def _as_host(rider, results):
    return results if rider is not None else tuple(results) + ([],)


MM_FUSED_MARGIN_BYTES = 10 * 1024 * 1024
MM_FUSED_MAX_ROWS = 512


def _col_sum_update(acc_ref, val, first):
    part = jnp.sum(val.reshape(val.shape[0] // 8, 8, val.shape[1]), axis=0)

    @pl.when(first)
    def _():
        acc_ref[...] = part

    @pl.when(jnp.logical_not(first))
    def _():
        acc_ref[...] += part


def _mm_fused(a, b, *, mode, name, extras, outs, epilogue, sums=(), rider=None, max_rows=MM_FUSED_MAX_ROWS,
              pass_a=False):
    parts = list(a) if isinstance(a, (list, tuple)) else [a]
    M, K = parts[0].shape[0], sum(p.shape[1] for p in parts)
    if mode == "nn":
        (K2, N), b_dims = b.shape, _NN
    else:
        (N, K2), b_dims = b.shape, _NT
    assert K == K2, (K, b.shape, mode)
    rows = parts + [e for e in extras if e.shape[0] == M]
    per_row = 2 * (sum(e.shape[1] * e.dtype.itemsize for e in rows)
                   + sum(c * jnp.dtype(d).itemsize for c, d in outs)) + 2 * N * 4
    budget = VMEM_LIMIT_BYTES - K * N * 2 - MM_FUSED_MARGIN_BYTES
    tm = next(t for t in (512, 256, 128, 64, 32, 16) if t <= max_rows and M % t == 0 and t * per_row <= budget)
    steps = M // tm
    n_a, n_x, n_o, n_s = len(parts), len(extras), len(outs), len(sums)
    n_ride = rider.n if rider is not None else 0

    def body(*refs):
        a_refs, b_ref = refs[:n_a], refs[n_a]
        x_refs = refs[n_a + 1:n_a + 1 + n_x]
        base = n_a + 1 + n_x + n_ride
        o_refs, s_refs = refs[base:base + n_o], refs[base + n_o:base + n_o + n_s]
        acc_refs = refs[base + n_o + n_s + n_ride:base + n_o + 2 * n_s + n_ride]
        ids = [pl.program_id(0)]
        if rider is not None:
            ride = (refs[n_a + 1 + n_x:base], refs[base + n_o + n_s:base + n_o + n_s + n_ride], refs[-3:])
            rider.start_at_first(ids, ride)
        a_tile = a_refs[0][...]
        a_bf16 = a_tile.astype(BF16) if n_a == 1 else jnp.concatenate([r[...].astype(BF16) for r in a_refs], axis=1)
        prod = _dot(a_bf16, b_ref[...], b_dims)
        tiles = epilogue(prod, *([a_tile] if pass_a else []), *[r[...] for r in x_refs])
        for o_ref, t in zip(o_refs, tiles[:n_o]):
            o_ref[...] = t.astype(o_ref.dtype)
        for acc_ref, t in zip(acc_refs, tiles[n_o:]):
            _col_sum_update(acc_ref, t, ids[0] == 0)
        if n_s:
            @pl.when(ids[0] == steps - 1)
            def _():
                for s_ref, acc_ref in zip(s_refs, acc_refs):
                    s_ref[...] = jnp.sum(acc_ref[...], axis=0, keepdims=True)
        if rider is not None:
            rider.wait_at_last(ids, (steps,), ride)

    in_specs = [pl.BlockSpec((tm, p.shape[1]), lambda i: (i, 0)) for p in parts]
    in_specs.append(pl.BlockSpec(b.shape, lambda i: (0, 0), pipeline_mode=pl.Buffered(1)))
    for e in extras:
        in_specs.append(pl.BlockSpec((tm, e.shape[1]), lambda i: (i, 0)) if e.shape[0] == M
                        else pl.BlockSpec(e.shape, lambda i: (0, 0)))
    out_specs = ([pl.BlockSpec((tm, c), lambda i: (i, 0)) for c, _ in outs]
                 + [pl.BlockSpec((1, c), lambda i: (0, 0)) for c in sums])
    out_shape = ([jax.ShapeDtypeStruct((M, c), d) for c, d in outs]
                 + [jax.ShapeDtypeStruct((1, c), F32) for c in sums])
    args = tuple(parts) + (b,) + tuple(extras)
    scratch = [pltpu.VMEM((8, c), F32) for c in sums]
    if rider is not None:
        in_specs, args = in_specs + rider.specs, args + tuple(rider.bufs)
        out_specs, out_shape = out_specs + rider.specs, out_shape + rider.out_shape
        scratch = scratch + rider.scratch
    res = pl.pallas_call(
        body,
        name=name,
        grid=(steps,),
        in_specs=in_specs,
        out_specs=out_specs,
        out_shape=out_shape,
        scratch_shapes=scratch,
        compiler_params=_params(("arbitrary",) if (n_s or rider is not None) else ("parallel",)),
    )(*args)
    return tuple(res[:n_o + n_s]) + ((list(res[n_o + n_s:]),) if rider is not None else ())


def _pair_rows(blk, lane_is_a):
    zero = jnp.zeros_like(blk)
    return jnp.concatenate([jnp.where(lane_is_a, blk, zero), jnp.where(lane_is_a, zero, blk)], axis=0)


SB_STRIP = 32
SB_FWD_PAIRS = 4
SB_BWD_PAIRS = 2
SB_GROUP = 2


def _pair_lanes(p):
    return slice(p * LANES, (p + 1) * LANES)


def _sb_scan_matrices():
    o = lax.broadcasted_iota(jnp.int32, (2 * LANES, 4 * LANES), 0)
    c = lax.broadcasted_iota(jnp.int32, (2 * LANES, 4 * LANES), 1) & (2 * LANES - 1)
    same = (o >= LANES) == (c >= LANES)
    oo, cc = o & (LANES - 1), c & (LANES - 1)
    return (jnp.where(same & (cc > oo), 1.0, 0.0).astype(BF16), jnp.where(same & (cc < oo), 1.0, 0.0).astype(BF16))


def _sb_causal_masks(tq):
    d = lax.broadcasted_iota(jnp.int32, (tq // SB_KEY_BLOCK, SB_KEY_BLOCK, tq), 0)
    k = lax.broadcasted_iota(jnp.int32, (tq // SB_KEY_BLOCK, SB_KEY_BLOCK, tq), 1)
    t = lax.broadcasted_iota(jnp.int32, (tq // SB_KEY_BLOCK, SB_KEY_BLOCK, tq), 2)
    return jnp.where(d * SB_KEY_BLOCK + k < t, 1.0, 0.0).astype(F32)


def _sb_log_terms(z):
    log_rem = -jnp.maximum(z, 0.0) - jnp.log(1.0 + jnp.exp(-jnp.abs(z)))
    return log_rem, log_rem + z


def _sb_store_split(ref, strip, val, cols):
    hi = val.astype(BF16)
    ref[pl.ds(strip * SB_STRIP, SB_STRIP), cols] = hi
    ref[pl.ds(2 * LANES + strip * SB_STRIP, SB_STRIP), cols] = (val - hi.astype(F32)).astype(BF16)


def _sb_lanes(tq, diag):
    if diag == "left":
        return 0, tq // 2
    first = 0 if diag is None else diag * SB_KEY_BLOCK
    return first, tq - first


def _lane_add(full, part, lanes):
    first, width = lanes
    pieces = [full[:, :first]] if first else []
    pieces.append(full[:, first:first + width] + part)
    if first + width < full.shape[1]:
        pieces.append(full[:, first + width:])
    return pieces[0] if len(pieces) == 1 else jnp.concatenate(pieces, axis=1)


def _sb_fwd(h_a, rider=None):
    assert SB_FWD_PAIRS == 4
    T = h_a.shape[0]
    tq = _pick(T, (SB_Q_BLOCK, SB_KEY_BLOCK))
    nq, per_q, nkb = T // tq, tq // SB_KEY_BLOCK, T // SB_KEY_BLOCK
    assert per_q % SB_GROUP == 0
    n_strips = 2 * LANES // SB_STRIP
    n_ride = rider.n if rider is not None else 0
    after_m, _ = _sb_scan_matrices()
    causal_m = _sb_causal_masks(tq)
    pairs = SB_FWD_PAIRS

    def body(*refs):
        q_ref, k_ref, v_ref, after_ref, causal_ref = refs[:5]
        a_ref, r_ref, n_ref = refs[5 + n_ride:8 + n_ride]
        z_ref, lb_ref, split_ref, w_ref = refs[8 + 2 * n_ride:12 + 2 * n_ride]
        ids = [pl.program_id(0)]
        if rider is not None:
            ride = (refs[5:5 + n_ride], refs[8 + n_ride:8 + 2 * n_ride], refs[-3:])
            rider.start_at_first(ids, ride)
        i = ids[0]
        q_t = [(q_ref[:, _pair_lanes(p)].astype(F32).T * (SB_HEAD_DIM ** -0.5)).astype(BF16) for p in range(pairs)]
        lane_is_a = lax.broadcasted_iota(jnp.int32, (SB_KEY_BLOCK, LANES), 1) < SB_HEAD_DIM

        def tiles(kbs, diags, carry):
            nb = len(kbs)
            lanes = [_sb_lanes(tq, d) for d in diags]
            cols = [slice(first, first + width) for first, width in lanes]
            acc_t, ra, rb = [list(c) for c in carry]
            ks = [pl.multiple_of(kb * SB_KEY_BLOCK, SB_KEY_BLOCK) for kb in kbs]
            slot = lambda p, b: p * nb + b

            def causal(b, s):
                return causal_ref[diags[b], pl.ds((s * SB_STRIP) % SB_KEY_BLOCK, SB_STRIP), cols[b]]

            vv = {}
            for b in range(nb):
                for p in range(pairs):
                    kk = _pair_rows(k_ref[pl.ds(ks[b], SB_KEY_BLOCK), _pair_lanes(p)], lane_is_a)
                    vv[p, b] = _pair_rows(v_ref[pl.ds(ks[b], SB_KEY_BLOCK), _pair_lanes(p)], lane_is_a)
                    z_ref[slot(p, b), :, cols[b]] = _dot(kk, q_t[p][:, cols[b]], _NN)
            sums = {}
            for b in range(nb):
                for p in range(pairs):
                    part = [jnp.zeros((8, lanes[b][1]), F32), jnp.zeros((8, lanes[b][1]), F32)]
                    for s in range(n_strips):
                        rows = pl.ds(s * SB_STRIP, SB_STRIP)
                        log_rem, log_beta = _sb_log_terms(z_ref[slot(p, b), rows, cols[b]])
                        lb_ref[slot(p, b), rows, cols[b]] = log_beta
                        if isinstance(diags[b], int):
                            log_rem = log_rem * causal(b, s)
                        _sb_store_split(split_ref.at[slot(p, b)], s, log_rem, cols[b])
                        head = (s * SB_STRIP) // SB_KEY_BLOCK
                        part[head] = part[head] + jnp.sum(log_rem.reshape(SB_STRIP // 8, 8, lanes[b][1]), axis=0)
                    sums[p, b] = part
            for b in range(nb):
                for p in range(pairs):
                    z_ref[slot(p, b), :, cols[b]] = _dot(after_ref[...], split_ref[slot(p, b), :, cols[b]], _NN)
            for b in range(nb):
                for p in range(pairs):
                    for s in range(n_strips):
                        rows = pl.ds(s * SB_STRIP, SB_STRIP)
                        start = (ra[p] if (s * SB_STRIP) < SB_KEY_BLOCK else rb[p])[:, cols[b]]
                        w = jnp.exp(lb_ref[slot(p, b), rows, cols[b]] + z_ref[slot(p, b), rows, cols[b]] + start)
                        if isinstance(diags[b], int):
                            w = w * causal(b, s)
                        w_ref[slot(p, b), rows, cols[b]] = w.astype(BF16)
                    r_ref[2 * p, kbs[b]] = ra[p]
                    r_ref[2 * p + 1, kbs[b]] = rb[p]
                    ra[p] = _lane_add(ra[p], jnp.sum(sums[p, b][0], axis=0, keepdims=True), lanes[b])
                    rb[p] = _lane_add(rb[p], jnp.sum(sums[p, b][1], axis=0, keepdims=True), lanes[b])
            for b in range(nb):
                for p in range(pairs):
                    acc_t[p] = _lane_add(acc_t[p], _dot(vv[p, b], w_ref[slot(p, b), :, cols[b]], _TN), lanes[b])
            return tuple(acc_t), tuple(ra), tuple(rb)

        carry = (tuple(jnp.zeros((LANES, tq), F32) for _ in range(pairs)),
                 tuple(jnp.zeros((1, tq), F32) for _ in range(pairs)),
                 tuple(jnp.zeros((1, tq), F32) for _ in range(pairs)))
        own = list(reversed(range(per_q)))
        n_full = i * per_q
        carry = lax.cond(
            i > 0,
            lambda cc: tiles([n_full + d for d in own] + [n_full - 1 - b for b in range(SB_GROUP)],
                             own + [None] * SB_GROUP, cc),
            lambda cc: tiles([n_full + d for d in own], own, cc), carry)
        first_walked = jnp.where(i > 0, SB_GROUP, 0).astype(jnp.int32)

        def top_of(sums_a, sums_b, first):
            return jnp.max(functools.reduce(jnp.maximum, [r[:, first:] for r in sums_a + sums_b]))

        def alive(c):
            return jnp.logical_and(c[0] < n_full, top_of(c[2], c[3], 0) > SB_DEAD_LOG)

        def step(c):
            kbs = [n_full - 1 - c[0] - b for b in range(SB_GROUP)]
            return (c[0] + SB_GROUP,) + lax.cond(
                top_of(c[2], c[3], tq // 2) > SB_DEAD_LOG,
                lambda cc: tiles(kbs, [None] * SB_GROUP, cc), lambda cc: tiles(kbs, ["left"] * SB_GROUP, cc), c[1:])

        walked, acc_t, _, _ = lax.while_loop(alive, step, (first_walked,) + carry)
        for p in range(pairs):
            a_ref[:, _pair_lanes(p)] = acc_t[p].T.astype(BF16)
        n_ref[...] = jnp.zeros(n_ref.shape, F32) + walked.astype(F32)
        if rider is not None:
            rider.wait_at_last(ids, (nq,), ride)

    wide = pairs * LANES
    in_specs = [pl.BlockSpec((tq, wide), lambda i: (i, 0)),
                pl.BlockSpec((T, wide), lambda i: (0, 1), pipeline_mode=pl.Buffered(1)),
                pl.BlockSpec((T, wide), lambda i: (0, 2), pipeline_mode=pl.Buffered(1)),
                pl.BlockSpec(after_m.shape, lambda i: (0, 0), pipeline_mode=pl.Buffered(1)),
                pl.BlockSpec(causal_m.shape, lambda i: (0, 0, 0), pipeline_mode=pl.Buffered(1))]
    out_specs = [pl.BlockSpec((tq, wide), lambda i: (i, 0)),
                 pl.BlockSpec((2 * pairs, nkb, 1, tq), lambda i: (0, 0, 0, i)),
                 pl.BlockSpec((1, 8, LANES), lambda i: (i, 0, 0))]
    out_shape = [jax.ShapeDtypeStruct((T, SB_WIDTH), BF16), jax.ShapeDtypeStruct((2 * pairs, nkb, 1, T), F32),
                 jax.ShapeDtypeStruct((nq, 8, LANES), F32)]
    args = (h_a, h_a, h_a, after_m, causal_m)
    slots = pairs * (per_q + SB_GROUP)
    scratch = [pltpu.VMEM((slots, 2 * LANES, tq), F32), pltpu.VMEM((slots, 2 * LANES, tq), F32),
               pltpu.VMEM((slots, 4 * LANES, tq), BF16), pltpu.VMEM((slots, 2 * LANES, tq), BF16)]
    if rider is not None:
        in_specs, args = in_specs + rider.specs, args + tuple(rider.bufs)
        out_specs, out_shape = out_specs + rider.specs, out_shape + rider.out_shape
        scratch = scratch + rider.scratch
    outs = pl.pallas_call(
        body,
        name="sb_fwd",
        grid=(nq,),
        in_specs=in_specs,
        out_specs=out_specs,
        out_shape=out_shape,
        scratch_shapes=scratch,
        compiler_params=_params(("arbitrary",)),
    )(*args)
    return outs[0], (outs[1], outs[2]), list(outs[3:])


def _sb_bwd(h_a, d_out, saved, rider=None):
    r_mat, walked_blocks = saved
    T = h_a.shape[0]
    tq = _pick(T, (SB_Q_BLOCK, SB_KEY_BLOCK))
    nq, per_q, nkb = T // tq, tq // SB_KEY_BLOCK, T // SB_KEY_BLOCK
    n_strips = 2 * LANES // SB_STRIP
    after_m, before_m = _sb_scan_matrices()
    causal_m = _sb_causal_masks(tq)
    pairs = SB_BWD_PAIRS
    groups = 4 // pairs
    n_ride = rider.n if rider is not None else 0

    def body(*refs):
        q_ref, k_ref, v_ref, do_ref, r_ref, n_ref, after_ref, before_ref, causal_ref = refs[:9]
        dq_ref, dk_ref, dv_ref = refs[9 + n_ride:12 + n_ride]
        z_ref, lb_ref, split_ref, w_ref, da_ref, dz_ref = refs[12 + 2 * n_ride:18 + 2 * n_ride]
        ids = [pl.program_id(0), pl.program_id(1)]
        if rider is not None:
            ride = (refs[9:9 + n_ride], refs[12 + n_ride:12 + 2 * n_ride], refs[-3:])
            rider.start_at_first(ids, ride)
        i = ids[1]

        @pl.when(i == 0)
        def _():
            dk_ref[...] = jnp.zeros_like(dk_ref)
            dv_ref[...] = jnp.zeros_like(dv_ref)

        scale = SB_HEAD_DIM ** -0.5
        q = [q_ref[:, _pair_lanes(p)] for p in range(pairs)]
        d_o = [do_ref[:, _pair_lanes(p)] for p in range(pairs)]
        q_t = [(x.astype(F32).T * scale).astype(BF16) for x in q]
        do_t = [x.astype(F32).T.astype(BF16) for x in d_o]
        lane_is_a = lax.broadcasted_iota(jnp.int32, (SB_KEY_BLOCK, LANES), 1) < SB_HEAD_DIM

        def tiles(kbs, diags, carry):
            nb = len(kbs)
            lanes = [_sb_lanes(tq, d) for d in diags]
            cols = [slice(first, first + width) for first, width in lanes]
            dq_t, ca, cb = [list(c) for c in carry]
            ks = [pl.multiple_of(kb * SB_KEY_BLOCK, SB_KEY_BLOCK) for kb in kbs]
            slot = lambda p, b: p * nb + b

            def causal(b, s):
                return causal_ref[diags[b], pl.ds((s * SB_STRIP) % SB_KEY_BLOCK, SB_STRIP), cols[b]]

            kk, vv = {}, {}
            for b in range(nb):
                for p in range(pairs):
                    kk[p, b] = _pair_rows(k_ref[pl.ds(ks[b], SB_KEY_BLOCK), _pair_lanes(p)], lane_is_a)
                    vv[p, b] = _pair_rows(v_ref[pl.ds(ks[b], SB_KEY_BLOCK), _pair_lanes(p)], lane_is_a)
                    z_ref[slot(p, b), :, cols[b]] = _dot(kk[p, b], q_t[p][:, cols[b]], _NN)
            for b in range(nb):
                for p in range(pairs):
                    for s in range(n_strips):
                        rows = pl.ds(s * SB_STRIP, SB_STRIP)
                        log_rem, log_beta = _sb_log_terms(z_ref[slot(p, b), rows, cols[b]])
                        lb_ref[slot(p, b), rows, cols[b]] = log_beta
                        if isinstance(diags[b], int):
                            log_rem = log_rem * causal(b, s)
                        _sb_store_split(split_ref.at[slot(p, b)], s, log_rem, cols[b])
            for b in range(nb):
                for p in range(pairs):
                    z_ref[slot(p, b), :, cols[b]] = _dot(after_ref[...], split_ref[slot(p, b), :, cols[b]], _NN)
                    da_ref[slot(p, b), :, cols[b]] = _dot(vv[p, b], do_t[p][:, cols[b]], _NN)
            sums = {}
            for b in range(nb):
                for p in range(pairs):
                    part = [jnp.zeros((8, lanes[b][1]), F32), jnp.zeros((8, lanes[b][1]), F32)]
                    for s in range(n_strips):
                        rows = pl.ds(s * SB_STRIP, SB_STRIP)
                        start = r_ref[2 * p + (s * SB_STRIP) // SB_KEY_BLOCK, kbs[b]][:, cols[b]]
                        w = jnp.exp(lb_ref[slot(p, b), rows, cols[b]] + z_ref[slot(p, b), rows, cols[b]] + start)
                        if isinstance(diags[b], int):
                            w = w * causal(b, s)
                        w_ref[slot(p, b), rows, cols[b]] = w.astype(BF16)
                        da = da_ref[slot(p, b), rows, cols[b]] * w
                        da_ref[slot(p, b), rows, cols[b]] = da
                        _sb_store_split(split_ref.at[slot(p, b)], s, da, cols[b])
                        head = (s * SB_STRIP) // SB_KEY_BLOCK
                        part[head] = part[head] + jnp.sum(da.reshape(SB_STRIP // 8, 8, lanes[b][1]), axis=0)
                    sums[p, b] = part
            for b in range(nb):
                for p in range(pairs):
                    z_ref[slot(p, b), :, cols[b]] = _dot(before_ref[...], split_ref[slot(p, b), :, cols[b]], _NN)
            for b in range(nb):
                for p in range(pairs):
                    for s in range(n_strips):
                        rows = pl.ds(s * SB_STRIP, SB_STRIP)
                        base = (ca[p] if (s * SB_STRIP) < SB_KEY_BLOCK else cb[p])[:, cols[b]]
                        sig = jnp.exp(lb_ref[slot(p, b), rows, cols[b]])
                        dz = (da_ref[slot(p, b), rows, cols[b]] * (1.0 - sig)
                              - (z_ref[slot(p, b), rows, cols[b]] + base) * sig)
                        if isinstance(diags[b], int):
                            dz = dz * causal(b, s)
                        dz_ref[slot(p, b), rows, cols[b]] = (dz * scale).astype(BF16)
                    ca[p] = _lane_add(ca[p], jnp.sum(sums[p, b][0], axis=0, keepdims=True), lanes[b])
                    cb[p] = _lane_add(cb[p], jnp.sum(sums[p, b][1], axis=0, keepdims=True), lanes[b])
            for b in range(nb):
                for p in range(pairs):
                    dq_t[p] = _lane_add(dq_t[p], _dot(kk[p, b], dz_ref[slot(p, b), :, cols[b]], _TN), lanes[b])
                    dkk = _dot(dz_ref[slot(p, b), :, cols[b]], q[p][cols[b], :], _NN)
                    dvv = _dot(w_ref[slot(p, b), :, cols[b]], d_o[p][cols[b], :], _NN)
                    here = (pl.ds(ks[b], SB_KEY_BLOCK), _pair_lanes(p))
                    dk_ref[here] += jnp.where(lane_is_a, dkk[:SB_KEY_BLOCK], dkk[SB_KEY_BLOCK:])
                    dv_ref[here] += jnp.where(lane_is_a, dvv[:SB_KEY_BLOCK], dvv[SB_KEY_BLOCK:])
            return tuple(dq_t), tuple(ca), tuple(cb)

        n_full = i * per_q
        groups_walked = jnp.clip(jnp.max(n_ref[...]).astype(jnp.int32), 0, n_full) // SB_GROUP
        carry = (tuple(jnp.zeros((LANES, tq), F32) for _ in range(pairs)),
                 tuple(jnp.zeros((1, tq), F32) for _ in range(pairs)),
                 tuple(jnp.zeros((1, tq), F32) for _ in range(pairs)))

        def below(j, c):
            kbs = [n_full - (groups_walked - j) * SB_GROUP + b for b in range(SB_GROUP)]
            starts = [r_ref[h, kbs[-1]][:, tq // 2:] for h in range(2 * pairs)]
            reaches = jnp.max(functools.reduce(jnp.maximum, starts)) > SB_DEAD_LOG
            return lax.cond(reaches, lambda cc: tiles(kbs, [None] * SB_GROUP, cc),
                            lambda cc: tiles(kbs, ["left"] * SB_GROUP, cc), c)

        carry = lax.fori_loop(0, groups_walked, below, carry)
        own = list(range(per_q))
        carry = tiles([i * per_q + d for d in own], own, carry)
        for p in range(pairs):
            dq_ref[:, _pair_lanes(p)] = carry[0][p].T.astype(BF16)
        if rider is not None:
            rider.wait_at_last(ids, (groups, nq), ride)

    wide = pairs * LANES
    mat = pl.BlockSpec(after_m.shape, lambda g, i: (0, 0), pipeline_mode=pl.Buffered(1))
    in_specs = [pl.BlockSpec((tq, wide), lambda g, i: (i, g)),
                pl.BlockSpec((T, wide), lambda g, i: (0, groups + g), pipeline_mode=pl.Buffered(1)),
                pl.BlockSpec((T, wide), lambda g, i: (0, 2 * groups + g), pipeline_mode=pl.Buffered(1)),
                pl.BlockSpec((tq, wide), lambda g, i: (i, g)),
                pl.BlockSpec((2 * pairs, nkb, 1, tq), lambda g, i: (g, 0, 0, i)),
                pl.BlockSpec((1, 8, LANES), lambda g, i: (i, 0, 0)),
                mat, mat,
                pl.BlockSpec(causal_m.shape, lambda g, i: (0, 0, 0), pipeline_mode=pl.Buffered(1))]
    out_specs = [pl.BlockSpec((tq, wide), lambda g, i: (i, g)),
                 pl.BlockSpec((T, wide), lambda g, i: (0, g)),
                 pl.BlockSpec((T, wide), lambda g, i: (0, g))]
    out_shape = [jax.ShapeDtypeStruct((T, SB_WIDTH), BF16), jax.ShapeDtypeStruct((T, SB_WIDTH), F32),
                 jax.ShapeDtypeStruct((T, SB_WIDTH), F32)]
    args = (h_a, h_a, h_a, d_out, r_mat, walked_blocks, after_m, before_m, causal_m)
    slots = pairs * max(per_q, SB_GROUP)
    scratch = [pltpu.VMEM((slots, 2 * LANES, tq), F32), pltpu.VMEM((slots, 2 * LANES, tq), F32),
               pltpu.VMEM((slots, 4 * LANES, tq), BF16), pltpu.VMEM((slots, 2 * LANES, tq), BF16),
               pltpu.VMEM((slots, 2 * LANES, tq), F32), pltpu.VMEM((slots, 2 * LANES, tq), BF16)]
    if rider is not None:
        in_specs, args = in_specs + rider.specs, args + tuple(rider.bufs)
        out_specs, out_shape = out_specs + rider.specs, out_shape + rider.out_shape
        scratch = scratch + rider.scratch
    outs = pl.pallas_call(
        body,
        name="sb_bwd",
        grid=(groups, nq),
        in_specs=in_specs,
        out_specs=out_specs,
        out_shape=out_shape,
        scratch_shapes=scratch,
        compiler_params=_params(("arbitrary", "arbitrary") if rider is not None else ("parallel", "arbitrary")),
    )(*args)
    return outs[0], outs[1], outs[2], list(outs[3:])


def _ret_tables(T):
    half = RET_QK_DIM // 2
    inv = 1.0 / (ROPE_BASE ** (jnp.arange(half, dtype=F32) / half))
    ang = jnp.arange(T, dtype=F32)[:, None] * inv[None, :]
    cos, sin = jnp.cos(ang), jnp.sin(ang)
    cos_t = jnp.concatenate([cos, cos], axis=1)
    sin_t = jnp.concatenate([-sin, sin], axis=1)
    log_gamma = jnp.log1p(-jnp.exp2(-5.0 - jnp.arange(RET_HEADS, dtype=F32)))
    idx = jnp.arange(RET_CHUNK, dtype=F32)
    rel = idx[:, None] - idx[None, :]
    decay = jnp.where(rel[None] >= 0, jnp.exp(log_gamma[:, None, None] * jnp.maximum(rel, 0.0)[None]), 0.0)
    k_decay = jnp.exp(log_gamma[None, :] * (RET_CHUNK - 1.0 - idx)[:, None])
    q_decay = jnp.exp(log_gamma[None, :] * (idx + 1.0)[:, None])
    chunk_decay = jnp.exp(log_gamma * RET_CHUNK)
    k_dec = jnp.broadcast_to(k_decay.T[:, :, None], (RET_HEADS, RET_CHUNK, LANES))
    q_dec = jnp.broadcast_to(q_decay.T[:, :, None], (RET_HEADS, RET_CHUNK, LANES))
    c_dec = jnp.broadcast_to(chunk_decay[:, None, None], (RET_HEADS, 8, LANES))
    return cos_t, sin_t, decay, k_dec, q_dec, c_dec


def _rotary(x, cos_t, sin_t):
    return x * cos_t + pltpu.roll(x, RET_QK_DIM // 2, 1) * sin_t


def _rotary_transpose(dy, cos_t, sin_t):
    return dy * cos_t + pltpu.roll(dy * sin_t, RET_QK_DIM // 2, 1)


def _head_norm(o):
    mu = jnp.mean(o, axis=1, keepdims=True)
    cen = o - mu
    var = jnp.mean(cen * cen, axis=1, keepdims=True)
    rstd = lax.rsqrt(var + LN_EPS)
    return cen * rstd, rstd


def _ret_specs(steps, per_step, reverse):
    def n_of(n):
        return (steps - 1 - n) if reverse else n

    rows = per_step * RET_CHUNK
    q_spec = pl.BlockSpec((rows, RET_QK_WIDTH), lambda n: (n_of(n), 0))
    k_spec = pl.BlockSpec((rows, RET_QK_WIDTH), lambda n: (n_of(n), 1))
    vv = pl.BlockSpec((rows, RET_V_WIDTH), lambda n: (n_of(n), 0))
    pos = pl.BlockSpec((rows, LANES), lambda n: (n_of(n), 0))
    per_head = pl.BlockSpec((RET_HEADS, RET_CHUNK, LANES), lambda n: (0, 0, 0))
    c_dec = pl.BlockSpec((RET_HEADS, 8, LANES), lambda n: (0, 0, 0))
    state = pl.BlockSpec((RET_HEADS, per_step, RET_QK_DIM, RET_V_DIM), lambda n: (0, n_of(n), 0, 0))
    return q_spec, k_spec, vv, pos, per_head, c_dec, state


def _qk_cols(h):
    return slice(h * RET_QK_DIM, (h + 1) * RET_QK_DIM)


def _v_cols(h):
    return slice(h * RET_V_DIM, (h + 1) * RET_V_DIM)


def _ret_fwd(h_b, h_c, h_d, tables):
    T = h_b.shape[0]
    nc = T // RET_CHUNK
    per_step = _pick(nc, (RET_STEP_CHUNKS, 1))
    steps = nc // per_step
    q_spec, k_spec, vv, pos, per_head, c_dec, state = _ret_specs(steps, per_step, False)

    def body(q_ref, k_ref, v_ref, g_ref, cos_ref, sin_ref, dec_ref, kd_ref, qd_ref, cd_ref,
             y_ref, o_ref, st_ref, state_ref):
        @pl.when(pl.program_id(0) == 0)
        def _():
            state_ref[...] = jnp.zeros_like(state_ref)

        for c in range(per_step):
            rows = pl.ds(c * RET_CHUNK, RET_CHUNK)
            cos_t, sin_t = cos_ref[rows, :], sin_ref[rows, :]
            for h in range(RET_HEADS):
                q = _rotary(q_ref[rows, _qk_cols(h)], cos_t, sin_t) * (RET_QK_DIM ** -0.5)
                k = _rotary(k_ref[rows, _qk_cols(h)], cos_t, sin_t)
                v = v_ref[rows, _v_cols(h)]
                prev = state_ref[h]
                scores = _dot(q.astype(BF16), k.astype(BF16), _NT) * dec_ref[h]
                inner = _dot(scores.astype(BF16), v, _NN)
                cross = _dot((q * qd_ref[h]).astype(BF16), prev.astype(BF16), _NN)
                o = inner + cross
                st_ref[h, c] = prev
                kv = _dot((k * kd_ref[h]).astype(BF16), v, _TN)
                state_ref[h] = prev * cd_ref[h, 0:1, 0:1] + kv
                o_ref[rows, _v_cols(h)] = o
                normed, _ = _head_norm(o)
                gate = g_ref[rows, _v_cols(h)]
                y_ref[rows, _v_cols(h)] = (gate * jax.nn.sigmoid(gate) * normed).astype(BF16)

    return pl.pallas_call(
        body,
        name="ret_fwd",
        grid=(steps,),
        in_specs=[q_spec, k_spec, vv, vv, pos, pos, per_head, per_head, per_head, c_dec],
        out_specs=[vv, vv, state],
        out_shape=[jax.ShapeDtypeStruct((T, RET_V_WIDTH), BF16),
                   jax.ShapeDtypeStruct((T, RET_V_WIDTH), F32),
                   jax.ShapeDtypeStruct((RET_HEADS, nc, RET_QK_DIM, RET_V_DIM), F32)],
        scratch_shapes=[pltpu.VMEM((RET_HEADS, RET_QK_DIM, RET_V_DIM), F32)],
        compiler_params=_params(("arbitrary",)),
    )(h_b, h_b, h_c, h_d, *tables)


def _ret_bwd(d_y, o_pre, states, h_b, h_c, h_d, tables, rider=None):
    T = h_b.shape[0]
    nc = T // RET_CHUNK
    per_step = _pick(nc, (RET_STEP_CHUNKS, 1))
    steps = nc // per_step
    q_spec, k_spec, vv, pos, per_head, c_dec, state = _ret_specs(steps, per_step, True)
    n_ride = rider.n if rider is not None else 0

    def body(*refs):
        (dy_ref, o_ref, st_ref, q_ref, k_ref, v_ref, g_ref, cos_ref, sin_ref, dec_ref, kd_ref, qd_ref,
         cd_ref) = refs[:13]
        dq_ref, dk_ref, dv_ref, dg_ref = refs[13 + n_ride:17 + n_ride]
        carry_ref = refs[17 + 2 * n_ride]
        ids = [pl.program_id(0)]
        if rider is not None:
            ride = (refs[13:13 + n_ride], refs[17 + n_ride:17 + 2 * n_ride], refs[-3:])
            rider.start_at_first(ids, ride)

        @pl.when(ids[0] == 0)
        def _():
            carry_ref[...] = jnp.zeros_like(carry_ref)

        scale = RET_QK_DIM ** -0.5
        for c in reversed(range(per_step)):
            rows = pl.ds(c * RET_CHUNK, RET_CHUNK)
            cos_t, sin_t = cos_ref[rows, :], sin_ref[rows, :]
            for h in range(RET_HEADS):
                q = _rotary(q_ref[rows, _qk_cols(h)], cos_t, sin_t) * scale
                k = _rotary(k_ref[rows, _qk_cols(h)], cos_t, sin_t)
                v = v_ref[rows, _v_cols(h)]
                decay, k_dec, q_dec = dec_ref[h], kd_ref[h], qd_ref[h]
                chunk_decay = cd_ref[h, 0:1, 0:1]
                state = st_ref[h, c].astype(BF16)
                later = carry_ref[h]
                later_b = later.astype(BF16)

                gate = g_ref[rows, _v_cols(h)]
                sig = jax.nn.sigmoid(gate)
                silu = gate * sig
                normed, rstd = _head_norm(o_ref[rows, _v_cols(h)])
                d_y = dy_ref[rows, _v_cols(h)]
                dg_ref[rows, _v_cols(h)] = (d_y * normed * (sig * (1.0 + gate * (1.0 - sig)))).astype(BF16)
                d_n = d_y * silu
                d_o = rstd * (d_n - jnp.mean(d_n, axis=1, keepdims=True)
                              - normed * jnp.mean(d_n * normed, axis=1, keepdims=True))
                d_ob = d_o.astype(BF16)

                qb, kb = q.astype(BF16), k.astype(BF16)
                qd_b, kd_b = (q * q_dec).astype(BF16), (k * k_dec).astype(BF16)
                scores = _dot(qb, kb, _NT) * decay
                d_scores = (_dot(d_ob, v, _NT) * decay).astype(BF16)
                dq = _dot(d_scores, kb, _NN) + _dot(d_ob, state, _NT) * q_dec
                dk = _dot(d_scores, qb, _TN) + _dot(v, later_b, _NT) * k_dec
                dv = _dot(scores.astype(BF16), d_ob, _TN) + _dot(kd_b, later_b, _NN)
                carry_ref[h] = _dot(qd_b, d_ob, _TN) + chunk_decay * later
                dq_ref[rows, _qk_cols(h)] = _rotary_transpose(dq * scale, cos_t, sin_t).astype(BF16)
                dk_ref[rows, _qk_cols(h)] = _rotary_transpose(dk, cos_t, sin_t).astype(BF16)
                dv_ref[rows, _v_cols(h)] = dv.astype(BF16)
        if rider is not None:
            rider.wait_at_last(ids, (steps,), ride)

    qk_out = pl.BlockSpec((per_step * RET_CHUNK, RET_QK_WIDTH), lambda n: (steps - 1 - n, 0))
    in_specs = [vv, vv, state, q_spec, k_spec, vv, vv, pos, pos, per_head, per_head, per_head, c_dec]
    out_specs = [qk_out, qk_out, vv, vv]
    out_shape = [jax.ShapeDtypeStruct((T, RET_QK_WIDTH), BF16), jax.ShapeDtypeStruct((T, RET_QK_WIDTH), BF16),
                 jax.ShapeDtypeStruct((T, RET_V_WIDTH), BF16), jax.ShapeDtypeStruct((T, RET_V_WIDTH), BF16)]
    args = (d_y, o_pre, states, h_b, h_b, h_c, h_d) + tuple(tables)
    scratch = [pltpu.VMEM((RET_HEADS, RET_QK_DIM, RET_V_DIM), F32)]
    if rider is not None:
        in_specs, args = in_specs + rider.specs, args + tuple(rider.bufs)
        out_specs, out_shape = out_specs + rider.specs, out_shape + rider.out_shape
        scratch = scratch + rider.scratch
    outs = pl.pallas_call(
        body,
        name="ret_bwd",
        grid=(steps,),
        in_specs=in_specs,
        out_specs=out_specs,
        out_shape=out_shape,
        scratch_shapes=scratch,
        compiler_params=_params(("arbitrary",)),
    )(*args)
    return outs[0], outs[1], outs[2], outs[3], list(outs[4:])


def _proj_tiles(h, x):
    return h[:, 0:1536], h[:, 1536:2560], h[:, 2560:3584], h[:, 3584:4608], h[:, 4608:6656], x


def _gate_mix_tiles(y_ret, h_e, b_gate, y_sb):
    gates = jax.nn.sigmoid(h_e + b_gate)
    return y_ret, gates[:, :D_MODEL] * y_sb + gates[:, D_MODEL:] * y_ret


def _gate_mix_grad_tiles(d_mix, h_e, b_gate, y_sb, y_ret):
    gates = jax.nn.sigmoid(h_e + b_gate)
    g0, g1 = gates[:, :D_MODEL], gates[:, D_MODEL:]
    d_e = jnp.concatenate([d_mix * y_sb * g0 * (1.0 - g0), d_mix * y_ret * g1 * (1.0 - g1)], axis=1)
    return d_mix * g0, d_mix * g1, d_e, d_e


def _ln_stats(u):
    mu = jnp.mean(u, axis=1, keepdims=True)
    cen = u - mu
    var = jnp.mean(cen * cen, axis=1, keepdims=True)
    rstd = lax.rsqrt(var + LN_EPS)
    return cen * rstd, rstd


def _ln_input_grad(d_out, gain, xhat, rstd):
    d_hat = d_out * gain
    return rstd * (d_hat - jnp.mean(d_hat, axis=1, keepdims=True)
                   - xhat * jnp.mean(d_hat * xhat, axis=1, keepdims=True))


def _ln_tiles(sub, x_prev, gain, bias):
    xhat, rstd = _ln_stats(DN_ALPHA * x_prev + sub)
    out = xhat * gain + bias
    return out, out, xhat, rstd


def _residual_tiles(d_sub, res):
    return (d_sub + DN_ALPHA * res,)


def _ln_grad_tiles(d_sub, res, xhat, rstd, gain):
    d_out = d_sub + DN_ALPHA * res
    du = _ln_input_grad(d_out, gain, xhat, rstd)
    return du, du, d_out * xhat, d_out


def _ln_loss_tiles(sub, x_prev, gain, bias, target):
    xhat, rstd = _ln_stats(DN_ALPHA * x_prev + sub)
    diff = xhat * gain + bias - target
    d_out = diff * (1.0 / D_MODEL)
    du = _ln_input_grad(d_out, gain, xhat, rstd)
    return du, du, diff * diff, d_out * xhat, d_out


def _mem_probs(q_h, k_h):
    s = _dot(q_h, k_h, _NT) * (MEM_HEAD_DIM ** -0.5)
    e = jnp.exp(s - jnp.max(s, axis=1, keepdims=True))
    return e / jnp.sum(e, axis=1, keepdims=True)


def _xattn_fwd(q, kv):
    T, mem_len = q.shape[0], kv.shape[0]
    tq = _pick(T, (512, 256, 128))

    def body(q_ref, kv_ref, o_ref):
        for h in range(MEM_HEADS):
            cols = slice(h * MEM_HEAD_DIM, (h + 1) * MEM_HEAD_DIM)
            vcols = slice(D_MODEL + h * MEM_HEAD_DIM, D_MODEL + (h + 1) * MEM_HEAD_DIM)
            p = _mem_probs(q_ref[:, cols], kv_ref[:, cols])
            o_ref[:, cols] = _dot(p.astype(BF16), kv_ref[:, vcols], _NN).astype(BF16)

    return pl.pallas_call(
        body,
        name="xattn_fwd",
        grid=(T // tq,),
        in_specs=[pl.BlockSpec((tq, D_MODEL), lambda i: (i, 0)),
                  pl.BlockSpec((mem_len, 2 * D_MODEL), lambda i: (0, 0))],
        out_specs=pl.BlockSpec((tq, D_MODEL), lambda i: (i, 0)),
        out_shape=jax.ShapeDtypeStruct((T, D_MODEL), BF16),
        compiler_params=_params(("parallel",)),
    )(q, kv)


def _xattn_bwd(q, kv, d_o):
    T, mem_len = q.shape[0], kv.shape[0]
    tq = _pick(T, (512, 256, 128))

    def body(q_ref, kv_ref, do_ref, dq_ref, dkv_ref):
        @pl.when(pl.program_id(0) == 0)
        def _():
            dkv_ref[...] = jnp.zeros_like(dkv_ref)

        for h in range(MEM_HEADS):
            cols = slice(h * MEM_HEAD_DIM, (h + 1) * MEM_HEAD_DIM)
            vcols = slice(D_MODEL + h * MEM_HEAD_DIM, D_MODEL + (h + 1) * MEM_HEAD_DIM)
            q_h, k_h, do_h = q_ref[:, cols], kv_ref[:, cols], do_ref[:, cols]
            p = _mem_probs(q_h, k_h)
            dp = _dot(do_h, kv_ref[:, vcols], _NT)
            ds = p * (dp - jnp.sum(dp * p, axis=1, keepdims=True))
            dsb = (ds * (MEM_HEAD_DIM ** -0.5)).astype(BF16)
            dq_ref[:, cols] = _dot(dsb, k_h, _NN).astype(BF16)
            dkv_ref[:, cols] += _dot(dsb, q_h, _TN)
            dkv_ref[:, vcols] += _dot(p.astype(BF16), do_h, _TN)

    row = pl.BlockSpec((tq, D_MODEL), lambda i: (i, 0))
    full = pl.BlockSpec((mem_len, 2 * D_MODEL), lambda i: (0, 0))
    return pl.pallas_call(
        body,
        name="xattn_bwd",
        grid=(T // tq,),
        in_specs=[row, full, row],
        out_specs=[row, full],
        out_shape=[jax.ShapeDtypeStruct((T, D_MODEL), BF16), jax.ShapeDtypeStruct((mem_len, 2 * D_MODEL), F32)],
        compiler_params=_params(("arbitrary",)),
    )(q, kv, d_o)


def _swiglu_tiles(f):
    a, b = f[:, :FFN_HIDDEN], f[:, FFN_HIDDEN:]
    return f, a * jax.nn.sigmoid(a) * b


def _swiglu_grad_tiles(d_hidden, f):
    a, b = f[:, :FFN_HIDDEN], f[:, FFN_HIDDEN:]
    sig = jax.nn.sigmoid(a)
    return (jnp.concatenate([d_hidden * b * (sig * (1.0 + a * (1.0 - sig))), d_hidden * (a * sig)], axis=1),)


def _local_step(x, mem, w_in, small, target, fetch, ship):
    T = x.shape[0]
    tables = _ret_tables(T)
    memb = mem.astype(BF16)

    (h_a, h_b, h_c, h_d, h_e, xb), w_ffn = fetch(
        ("w_ffn_in", "w_ffn_out"),
        lambda rider: _as_host(rider, _mm_fused(
            x, w_in, mode="nn", name="proj_in", extras=[], pass_a=True,
            outs=[(1536, BF16), (1024, F32), (1024, BF16), (1024, F32), (2048, F32), (D_MODEL, BF16)],
            epilogue=_proj_tiles, max_rows=256, rider=rider)))
    (a_sb, r_mat), w_mix = fetch(("w_sb_o", "w_ret_o", "w_mix_o", "w_mem_q", "w_mem_kv", "w_mem_o"),
                                 lambda rider: _sb_fwd(h_a, rider))
    w = {**w_ffn, **w_mix}
    y_gated, o_pre, states = _ret_fwd(h_b, h_c, h_d, tables)
    y_sb = _mm(a_sb, w["w_sb_o"], mode="nn", out_dtype=F32, name="sb_out")
    row_f32, row_bf16 = (D_MODEL, F32), (D_MODEL, BF16)
    ln_outs = [row_f32, row_bf16, row_f32, (1, F32)]
    y_ret, mix_in = _mm_fused(y_gated, w["w_ret_o"], mode="nn", name="ret_out", extras=[h_e, small["b_gate"], y_sb],
                              outs=[row_f32, row_bf16], epilogue=_gate_mix_tiles)
    x1, x1b, xhat1, rstd1 = _mm_fused(mix_in, w["w_mix_o"], mode="nn", name="mix_out",
                                      extras=[x, small["ln1_g"], small["ln1_b"]], outs=ln_outs, epilogue=_ln_tiles)
    q_m = _mm(x1b, w["w_mem_q"], mode="nn", out_dtype=BF16, name="mem_q")
    kv_m = _mm(memb, w["w_mem_kv"], mode="nn", out_dtype=BF16, name="mem_kv")
    o_m = _xattn_fwd(q_m, kv_m)
    x2, x2b, xhat2, rstd2 = _mm_fused(o_m, w["w_mem_o"], mode="nn", name="mem_out",
                                      extras=[x1, small["ln2_g"], small["ln2_b"]], outs=ln_outs, epilogue=_ln_tiles)
    f, hidden = _mm_fused(x2b, w["w_ffn_in"], mode="nn", name="ffn_in", extras=[],
                          outs=[(2 * FFN_HIDDEN, F32), (FFN_HIDDEN, BF16)], epilogue=_swiglu_tiles)
    du_outs, col = [row_f32, row_bf16], D_MODEL
    du3, du3b, loss_cols, d_ln3_g, d_ln3_b = _mm_fused(
        hidden, w["w_ffn_out"], mode="nn", name="ffn_out", extras=[x2, small["ln3_g"], small["ln3_b"], target],
        outs=du_outs, sums=[col, col, col], epilogue=_ln_loss_tiles, max_rows=256)

    g_ffn_out = _mm(hidden, du3b, mode="tn", out_dtype=BF16, name="g_ffn_out")
    (d_f,) = _mm_fused(du3b, w["w_ffn_out"], mode="nt", name="d_hidden", extras=[f],
                       outs=[(2 * FFN_HIDDEN, BF16)], epilogue=_swiglu_grad_tiles)
    g_ffn_in = _mm(x2b, d_f, mode="tn", out_dtype=BF16, name="g_ffn_in")
    du2, du2b, d_ln2_g, d_ln2_b = ship(
        {"w_ffn_out": g_ffn_out},
        lambda rider: _as_host(rider, _mm_fused(
            d_f, w["w_ffn_in"], mode="nt", name="d_x2", extras=[du3, xhat2, rstd2, small["ln2_g"]], outs=du_outs,
            sums=[col, col], epilogue=_ln_grad_tiles, rider=rider, max_rows=256)))
    g_mem_o = _mm(o_m, du2b, mode="tn", out_dtype=BF16, name="g_mem_o")
    d_om = _mm(du2b, w["w_mem_o"], mode="nt", out_dtype=BF16, name="d_om")
    d_qm, d_kvm = _xattn_bwd(q_m, kv_m, d_om)
    g_mem_q = _mm(x1b, d_qm, mode="tn", out_dtype=BF16, name="g_mem_q")
    g_mem_kv = _mm(memb, d_kvm.astype(BF16), mode="tn", out_dtype=BF16, name="g_mem_kv")
    du1, du1b, d_ln1_g, d_ln1_b = _mm_fused(
        d_qm, w["w_mem_q"], mode="nt", name="d_x1", extras=[du2, xhat1, rstd1, small["ln1_g"]], outs=du_outs,
        sums=[col, col], epilogue=_ln_grad_tiles, max_rows=256)
    g_mix_o = _mm(mix_in, du1b, mode="tn", out_dtype=BF16, name="g_mix_o")
    d_ysb, d_yret, d_e, d_b_gate = _mm_fused(
        du1b, w["w_mix_o"], mode="nt", name="d_mix_in", extras=[h_e, small["b_gate"], y_sb, y_ret],
        outs=[row_bf16, row_bf16, (2 * D_MODEL, BF16)], sums=[2 * D_MODEL], epilogue=_gate_mix_grad_tiles,
        max_rows=256)
    g_sb_o = _mm(a_sb, d_ysb, mode="tn", out_dtype=BF16, name="g_sb_o")
    g_ret_o = _mm(y_gated, d_yret, mode="tn", out_dtype=BF16, name="g_ret_o")
    d_asb = _mm(d_ysb, w["w_sb_o"], mode="nt", out_dtype=BF16, name="d_asb")
    d_ygated = _mm(d_yret, w["w_ret_o"], mode="nt", out_dtype=F32, name="d_ygated")
    small_grads = {"b_gate": d_b_gate, "ln1_g": d_ln1_g, "ln1_b": d_ln1_b, "ln2_g": d_ln2_g, "ln2_b": d_ln2_b,
                   "ln3_g": d_ln3_g, "ln3_b": d_ln3_b, "loss_cols": loss_cols}
    d_rq, d_rk, d_c, d_d = ship({"w_mem_kv": g_mem_kv, "w_mem_q": g_mem_q, "w_mem_o": g_mem_o, "w_mix_o": g_mix_o},
                                lambda rider: _ret_bwd(d_ygated, o_pre, states, h_b, h_c, h_d, tables, rider))
    d_q, d_k, d_v = ship({"w_ffn_in": g_ffn_in, "w_ret_o": g_ret_o, "w_sb_o": g_sb_o, "small": small_grads},
                         lambda rider: _sb_bwd(h_a, d_asb, r_mat, rider))
    d_h = [("sb_q", d_q), ("sb_k", d_k), ("sb_v", d_v), ("ret_q", d_rq), ("ret_k", d_rk), ("ret_v", d_c),
           ("ret_g", d_d), ("gate", d_e)]
    g_in = jnp.concatenate([_mm(xb, piece, mode="tn", out_dtype=BF16, name="g_in_" + tag) for tag, piece in d_h],
                           axis=1)
    (d_x,) = ship({"w_in": g_in},
                  lambda rider: _as_host(rider, _mm_fused(
                      [piece for _, piece in d_h], w_in, mode="nt", name="d_x", extras=[du1], outs=[(D_MODEL, F32)],
                      epilogue=_residual_tiles, rider=rider, max_rows=256)))
    return d_x


def _adamw_math(w, g, m, v):
    m = ADAM_B1 * m + (1.0 - ADAM_B1) * g
    v = ADAM_B2 * v + (1.0 - ADAM_B2) * jnp.square(g)
    m_hat = m / (1.0 - ADAM_B1 ** ADAM_STEP)
    v_hat = v / (1.0 - ADAM_B2 ** ADAM_STEP)
    delta = -ADAM_LR * (m_hat / (jnp.sqrt(v_hat) + ADAM_EPS) + ADAM_WD * w)
    return delta, m, v


def _adamw(parts, w, m, v, name):
    R, C = w.shape
    tr = max(t for t in range(16, min(R, 256) + 1, 16) if R % t == 0) if R >= 16 else R

    def body(p_ref, w_ref, m_ref, v_ref, g_ref, d_ref, nm_ref, nv_ref):
        g = p_ref[0].astype(F32)
        for j in range(1, N_DEV):
            g = g + p_ref[j].astype(F32)
        delta, nm, nv = _adamw_math(w_ref[...], g, m_ref[...], v_ref[...])
        g_ref[...] = g
        d_ref[...] = delta
        nm_ref[...] = nm
        nv_ref[...] = nv

    blk = pl.BlockSpec((tr, C), lambda i: (i, 0))
    out = jax.ShapeDtypeStruct((R, C), F32)
    return pl.pallas_call(
        body,
        name=name,
        grid=(R // tr,),
        in_specs=[pl.BlockSpec((N_DEV, tr, C), lambda i: (0, i, 0)), blk, blk, blk],
        out_specs=[blk] * 4,
        out_shape=[out] * 4,
        compiler_params=_params(("parallel",)),
    )(parts, w, m, v)


_SHARD_AXIS = {"w_in": 1, "w_sb_o": 1, "w_ret_o": 0, "w_mix_o": 0, "w_mem_q": 0, "w_mem_kv": 1, "w_mem_o": 0,
               "w_ffn_in": 1, "w_ffn_out": 0}
_MATRICES = tuple(_SHARD_AXIS)
_SMALL = ("b_gate", "ln1_g", "ln1_b", "ln2_g", "ln2_b", "ln3_g", "ln3_b")
_WEIGHT_ORDER = ("w_in", "b_gate", "w_sb_o", "w_ret_o", "w_mix_o", "ln1_g", "ln1_b", "w_mem_q", "w_mem_kv", "w_mem_o",
                 "ln2_g", "ln2_b", "w_ffn_in", "w_ffn_out", "ln3_g", "ln3_b")


def _assemble(name, gathered):
    if _SHARD_AXIS[name] == 0:
        return gathered.reshape(-1, gathered.shape[2])
    return jnp.transpose(gathered, (1, 0, 2)).reshape(gathered.shape[1], -1)


def _to_slots(name, full):
    if _SHARD_AXIS[name] == 0:
        return full.reshape(N_DEV, full.shape[0] // N_DEV, full.shape[1])
    return jnp.transpose(full.reshape(full.shape[0], N_DEV, full.shape[1] // N_DEV), (1, 0, 2))


SMALL_ROWS = 16


def _pack_small(vals):
    return jnp.concatenate([vals["b_gate"].reshape(2, D_MODEL)] + [vals[n] for n in _SMALL[1:]], axis=0)


def _unpack_small(packed):
    out = {"b_gate": packed[0:2].reshape(1, 2 * D_MODEL)}
    for i, n in enumerate(_SMALL[1:]):
        out[n] = packed[2 + i:3 + i]
    return out


def kernel(x, mem, w_in, b_gate, w_sb_o, w_ret_o, w_mix_o, ln1_g, ln1_b, w_mem_q, w_mem_kv, w_mem_o, ln2_g, ln2_b, w_ffn_in, w_ffn_out, ln3_g, ln3_b, loss_target, m_w_in, m_b_gate, m_w_sb_o, m_w_ret_o, m_w_mix_o, m_ln1_g, m_ln1_b, m_w_mem_q, m_w_mem_kv, m_w_mem_o, m_ln2_g, m_ln2_b, m_w_ffn_in, m_w_ffn_out, m_ln3_g, m_ln3_b, v_w_in, v_b_gate, v_w_sb_o, v_w_ret_o, v_w_mix_o, v_ln1_g, v_ln1_b, v_w_mem_q, v_w_mem_kv, v_w_mem_o, v_ln2_g, v_ln2_b, v_w_ffn_in, v_w_ffn_out, v_ln3_g, v_ln3_b):
    weights = dict(w_in=w_in, b_gate=b_gate, w_sb_o=w_sb_o, w_ret_o=w_ret_o, w_mix_o=w_mix_o, ln1_g=ln1_g, ln1_b=ln1_b,
                   w_mem_q=w_mem_q, w_mem_kv=w_mem_kv, w_mem_o=w_mem_o, ln2_g=ln2_g, ln2_b=ln2_b, w_ffn_in=w_ffn_in,
                   w_ffn_out=w_ffn_out, ln3_g=ln3_g, ln3_b=ln3_b)
    mom1 = dict(w_in=m_w_in, b_gate=m_b_gate, w_sb_o=m_w_sb_o, w_ret_o=m_w_ret_o, w_mix_o=m_w_mix_o, ln1_g=m_ln1_g,
                ln1_b=m_ln1_b, w_mem_q=m_w_mem_q, w_mem_kv=m_w_mem_kv, w_mem_o=m_w_mem_o, ln2_g=m_ln2_g, ln2_b=m_ln2_b,
                w_ffn_in=m_w_ffn_in, w_ffn_out=m_w_ffn_out, ln3_g=m_ln3_g, ln3_b=m_ln3_b)
    mom2 = dict(w_in=v_w_in, b_gate=v_b_gate, w_sb_o=v_w_sb_o, w_ret_o=v_w_ret_o, w_mix_o=v_w_mix_o, ln1_g=v_ln1_g,
                ln1_b=v_ln1_b, w_mem_q=v_w_mem_q, w_mem_kv=v_w_mem_kv, w_mem_o=v_w_mem_o, ln2_g=v_ln2_g, ln2_b=v_ln2_b,
                w_ffn_in=v_w_ffn_in, w_ffn_out=v_w_ffn_out, ln3_g=v_ln3_g, ln3_b=v_ln3_b)

    (gathered_in,) = _exchange([weights["w_in"][0].astype(BF16)], False, "gather_w_in")
    received = {}

    def fetch(names, host):
        res = host(_Rider([weights[n][0].astype(BF16) for n in names], False))
        return res[:-1], {n: _assemble(n, g) for n, g in zip(names, res[-1])}

    def ship(grads, host):
        names = list(grads)
        bufs = []
        for n in names:
            if n == "small":
                part = jnp.concatenate([_pack_small(grads[n]), grads[n]["loss_cols"],
                                        jnp.zeros((SMALL_ROWS - 9, D_MODEL), F32)], axis=0)
                bufs.append(jnp.broadcast_to(part[None], (N_DEV,) + part.shape))
            else:
                bufs.append(_to_slots(n, grads[n]).astype(BF16))
        res = host(_Rider(bufs, True))
        received.update(zip(names, res[-1]))
        return res[:-1]

    small = {n: weights[n] for n in _SMALL}
    d_x = _local_step(x[0], mem[0], _assemble("w_in", gathered_in), small, loss_target[0], fetch, ship)

    new = {}
    for n in _MATRICES:
        new[n] = _adamw(received[n], weights[n][0], mom1[n][0], mom2[n][0], "adamw_" + n)
    packed = _adamw(received["small"][:, :8], _pack_small({n: weights[n] for n in _SMALL}),
                    _pack_small({n: mom1[n] for n in _SMALL}), _pack_small({n: mom2[n] for n in _SMALL}), "adamw_small")
    small_new = [_unpack_small(p) for p in packed]
    loss = jnp.sum(received["small"][:, 8]) * (0.5 / D_MODEL)

    outs = [loss, d_x[None]]
    for slot in range(4):
        for n in _WEIGHT_ORDER:
            outs.append(new[n][slot][None] if n in new else small_new[slot][n])
    return tuple(outs)
```

```python
import functools
import math

import jax
import jax.numpy as jnp
from jax import lax
from jax.experimental import pallas as pl
from jax.experimental.pallas import tpu as pltpu

F32 = jnp.float32
BF16 = jnp.bfloat16

N_DEV = 8
D_MODEL = 1024
SB_HEAD_DIM = 64
SB_WIDTH = 512
RET_HEADS = 4
RET_QK_DIM = 128
RET_V_DIM = 256
RET_QK_WIDTH = 512
RET_V_WIDTH = 1024
RET_CHUNK = 128
RET_STEP_CHUNKS = 2
ROPE_BASE = 10000.0
MEM_HEADS = 4
MEM_HEAD_DIM = 256
FFN_HIDDEN = 2816
DN_ALPHA = 2.0 ** 0.25
LN_EPS = 1e-5
ADAM_LR = 0.001
ADAM_B1 = 0.9
ADAM_B2 = 0.999
ADAM_EPS = 1e-08
ADAM_WD = 0.01
ADAM_STEP = 10

VMEM_LIMIT_BYTES = 52 * 1024 * 1024
LANES = 128
SB_KEY_BLOCK = 128
SB_Q_BLOCK = 256
SB_DEAD_LOG = -105.0

MESH_AXES = ("x", "y", "c")


def _pick(dim, prefs):
    for p in prefs:
        if dim % p == 0:
            return p
    return dim


def _params(sem):
    return pltpu.CompilerParams(dimension_semantics=sem, vmem_limit_bytes=VMEM_LIMIT_BYTES)


def _dot(a, b, dims):
    return lax.dot_general(a, b, (dims, ((), ())), preferred_element_type=F32)


_NN = ((1,), (0,))
_NT = ((1,), (1,))
_TN = ((0,), (0,))


def _my_index():
    return 4 * lax.axis_index("x") + 2 * lax.axis_index("y") + lax.axis_index("c")


def _peer(k):
    x, y, c = lax.axis_index("x"), lax.axis_index("y"), lax.axis_index("c")
    bx, by, bc = (k >> 2) & 1, (k >> 1) & 1, k & 1
    px = (1 - x) if bx else x
    py = (1 - y) if by else y
    pc = (1 - c) if bc else c
    return (px, py, pc), 4 * px + 2 * py + pc


class _Rider:
    def __init__(self, bufs, scatter):
        self.bufs, self.scatter, self.n = list(bufs), scatter, len(bufs)
        self.specs = [pl.BlockSpec(memory_space=pl.ANY)] * self.n
        self.out_shape = [jax.ShapeDtypeStruct(b.shape if scatter else (N_DEV,) + b.shape, b.dtype) for b in self.bufs]
        self.scratch = [pltpu.SemaphoreType.DMA((self.n, N_DEV - 1)), pltpu.SemaphoreType.DMA((self.n, N_DEV - 1)),
                        pltpu.SemaphoreType.DMA((self.n,))]

    def _remote(self, ride, a, k, src_ref, slot, to):
        _, dst, (send_sems, recv_sems, _) = ride
        return pltpu.make_async_remote_copy(src_ref=src_ref, dst_ref=dst[a].at[slot], send_sem=send_sems.at[a, k],
                                            recv_sem=recv_sems.at[a, k], device_id=to,
                                            device_id_type=pl.DeviceIdType.MESH)

    def _local(self, ride, a):
        src, dst, (_, _, local_sems) = ride
        me = _my_index()
        return pltpu.make_async_copy(src[a].at[me] if self.scatter else src[a], dst[a].at[me], local_sems.at[a])

    def _direct(self, ride, a):
        src = ride[0]
        me = _my_index()
        out = []
        for k in range(1, N_DEV):
            peer, peer_idx = _peer(k)
            out.append(self._remote(ride, a, k - 1, src[a].at[peer_idx], me, peer))
        return out

    def _two_level(self, ride, a):
        src, dst = ride[0], ride[1]
        x, y, c = lax.axis_index("x"), lax.axis_index("y"), lax.axis_index("c")
        me, sibling = _my_index(), (x, y, 1 - c)
        chips = [(1 - x, y), (x, 1 - y), (1 - x, 1 - y)]
        first = [self._remote(ride, a, 0, src[a], me, sibling)]
        passed, landing = [], [self._remote(ride, a, 0, src[a], me + 1 - 2 * c, sibling)]
        for j, (px, py) in enumerate(chips):
            first.append(self._remote(ride, a, 1 + j, src[a], me, (px, py, c)))
            theirs = 4 * px + 2 * py + c
            passed.append(self._remote(ride, a, 4 + j, dst[a].at[theirs], theirs, sibling))
            landing.append(self._remote(ride, a, 1 + j, src[a], theirs, (px, py, c)))
        for j, (px, py) in enumerate(chips):
            landing.append(self._remote(ride, a, 4 + j, src[a], 4 * px + 2 * py + 1 - c, sibling))
        return first, passed, landing

    def start(self, ride):
        for a in range(self.n):
            self._local(ride, a).start()
            for cp in (self._direct(ride, a) if self.scatter else self._two_level(ride, a)[0]):
                cp.start()

    def finish(self, ride):
        if self.scatter:
            for a in range(self.n):
                for cp in self._direct(ride, a):
                    cp.wait()
                self._local(ride, a).wait()
            return
        levels = [self._two_level(ride, a) for a in range(self.n)]
        for first, passed, landing in levels:
            for j, cp in enumerate(passed):
                landing[1 + j].wait_recv()
                cp.start()
        for a, (first, passed, landing) in enumerate(levels):
            landing[0].wait_recv()
            for cp in landing[4:]:
                cp.wait_recv()
            for cp in first + passed:
                cp.wait_send()
            self._local(ride, a).wait()

    def start_at_first(self, ids, ride):
        first = functools.reduce(jnp.logical_and, [i == 0 for i in ids])

        @pl.when(first)
        def _():
            self.start(ride)

    def wait_at_last(self, ids, grid, ride):
        last = functools.reduce(jnp.logical_and, [i == g - 1 for i, g in zip(ids, grid)])

        @pl.when(last)
        def _():
            self.finish(ride)


def _exchange(bufs, scatter, name):
    rider = _Rider(bufs, scatter)

    def body(*refs):
        ride = (refs[:rider.n], refs[rider.n:2 * rider.n], refs[2 * rider.n:])
        rider.start(ride)
        rider.finish(ride)

    return pl.pallas_call(
        body,
        name=name,
        in_specs=rider.specs,
        out_specs=rider.specs,
        out_shape=rider.out_shape,
        scratch_shapes=rider.scratch,
    )(*rider.bufs)


MM_RESIDENT_B_BYTES = 14 * 1024 * 1024
MM_A_TILE_BYTES = 4 * 1024 * 1024
MM_OUT_TILE_BYTES = 6 * 1024 * 1024


def _mm_tiles(mode, M, N, K, a_bytes, out_bytes):
    if mode != "tn" and K * N * 2 <= MM_RESIDENT_B_BYTES:
        for tm in (1024, 512, 256, 128):
            if M % tm == 0 and tm * K * a_bytes <= MM_A_TILE_BYTES and tm * N * out_bytes <= MM_OUT_TILE_BYTES:
                return tm, N, K
    if mode == "tn":
        return (_pick(M, (1024, 1408, 512, 256, 128)), _pick(N, (1024, 1664, 1408, 512, 256, 128)),
                _pick(K, (2048, 1024, 512, 256, 128)))
    return _pick(M, (1024, 512, 256, 128)), _pick(N, (512, 256, 128)), _pick(K, (1024, 512, 256, 128))


def _mm(a, b, *, mode, out_dtype, name, res=None, res_scale=1.0, rider=None):
    if mode == "nn":
        (M, K), (K2, N) = a.shape, b.shape
    elif mode == "nt":
        (M, K), (N, K2) = a.shape, b.shape
    else:
        (K, M), (K2, N) = a.shape, b.shape
    assert K == K2, (a.shape, b.shape, mode)
    out_bytes = jnp.dtype(out_dtype).itemsize + (4 if res is not None else 0)
    tm, tn, tk = _mm_tiles(mode, M, N, K, a.dtype.itemsize, out_bytes)
    grid = (M // tm, N // tn, K // tk)
    nk = grid[2]
    dims = {"nn": _NN, "nt": _NT, "tn": _TN}[mode]
    n_in = 2 + (res is not None)
    n_ride = rider.n if rider is not None else 0

    def body(*refs):
        a_ref, b_ref = refs[:2]
        r_ref = refs[2] if res is not None else None
        o_ref = refs[n_in + n_ride]
        rest = refs[n_in + 2 * n_ride + 1:]
        acc_ref = rest[0] if nk > 1 else None
        ids = [pl.program_id(d) for d in range(3)]
        if rider is not None:
            ride = (refs[n_in:n_in + n_ride], refs[n_in + n_ride + 1:n_in + 2 * n_ride + 1], rest[-3:])
            rider.start_at_first(ids, ride)
        part = _dot(a_ref[...].astype(BF16), b_ref[...].astype(BF16), dims)

        def finish(total):
            if r_ref is not None:
                total = total + res_scale * r_ref[...]
            o_ref[...] = total.astype(out_dtype)

        if nk == 1:
            finish(part)
        else:
            k = ids[2]

            @pl.when(k == 0)
            def _():
                acc_ref[...] = part

            @pl.when(k > 0)
            def _():
                acc_ref[...] += part

            @pl.when(k == nk - 1)
            def _():
                finish(acc_ref[...])

        if rider is not None:
            rider.wait_at_last(ids, grid, ride)

    if mode == "nn":
        a_spec = pl.BlockSpec((tm, tk), lambda i, j, k: (i, k))
        b_spec = pl.BlockSpec((tk, tn), lambda i, j, k: (k, j))
    elif mode == "nt":
        a_spec = pl.BlockSpec((tm, tk), lambda i, j, k: (i, k))
        b_spec = pl.BlockSpec((tn, tk), lambda i, j, k: (j, k))
    else:
        a_spec = pl.BlockSpec((tk, tm), lambda i, j, k: (k, i))
        b_spec = pl.BlockSpec((tk, tn), lambda i, j, k: (k, j))
    o_spec = pl.BlockSpec((tm, tn), lambda i, j, k: (i, j))
    in_specs = [a_spec, b_spec] + ([o_spec] if res is not None else [])
    args = (a, b) + ((res,) if res is not None else ())
    out_specs, out_shape = [o_spec], [jax.ShapeDtypeStruct((M, N), out_dtype)]
    scratch = [pltpu.VMEM((tm, tn), F32)] if nk > 1 else []
    sem = ("parallel", "parallel", "arbitrary")
    if rider is not None:
        in_specs, args = in_specs + rider.specs, args + tuple(rider.bufs)
        out_specs, out_shape = out_specs + rider.specs, out_shape + rider.out_shape
        scratch = scratch + rider.scratch
        sem = ("arbitrary",) * 3
    outs = pl.pallas_call(
        body,
        name=name,
        grid=grid,
        in_specs=in_specs,
        out_specs=out_specs,
        out_shape=out_shape,
        scratch_shapes=scratch,
        compiler_params=_params(sem),
    )(*args)
    return outs[0] if rider is None else (outs[0], list(outs[1:]))


def _mm_host(a, b, *, rider, **kw):
    out = _mm(a, b, rider=rider, **kw)
    return out if rider is not None else (out, [])


def _as_host(rider, results):
    return results if rider is not None else tuple(results) + ([],)


def _mm_pieces_tn(a, a_block, M, pieces, name, rider=None):
    T = a.shape[0]
    widths = [p.shape[1] for p in pieces]
    tk = _pick(T, (512, 256, 128))
    steps = T // tk
    n_p = len(pieces)
    n_ride = rider.n if rider is not None else 0

    def body(*refs):
        a_ref, p_refs = refs[0], refs[1:1 + n_p]
        o_ref = refs[1 + n_p + n_ride]
        acc_ref = refs[2 + n_p + 2 * n_ride]
        ids = [pl.program_id(0)]
        if rider is not None:
            ride = (refs[1 + n_p:1 + n_p + n_ride], refs[2 + n_p + n_ride:2 + n_p + 2 * n_ride], refs[-3:])
            rider.start_at_first(ids, ride)
        a_tile = a_ref[...].astype(BF16)
        first = 0
        for p_ref, width in zip(p_refs, widths):
            part = _dot(a_tile, p_ref[...].astype(BF16), _TN)
            cols = slice(first, first + width)

            @pl.when(ids[0] == 0)
            def _():
                acc_ref[:, cols] = part

            @pl.when(ids[0] > 0)
            def _():
                acc_ref[:, cols] += part

            first += width

        @pl.when(ids[0] == steps - 1)
        def _():
            o_ref[...] = acc_ref[...].astype(BF16)

        if rider is not None:
            rider.wait_at_last(ids, (steps,), ride)

    n_out = sum(widths)
    in_specs = ([pl.BlockSpec((tk, M), lambda k: (k, a_block))]
                + [pl.BlockSpec((tk, w), lambda k: (k, 0)) for w in widths])
    out_specs = [pl.BlockSpec((M, n_out), lambda k: (0, 0))]
    out_shape = [jax.ShapeDtypeStruct((M, n_out), BF16)]
    args = (a,) + tuple(pieces)
    scratch = [pltpu.VMEM((M, n_out), F32)]
    if rider is not None:
        in_specs, args = in_specs + rider.specs, args + tuple(rider.bufs)
        out_specs, out_shape = out_specs + rider.specs, out_shape + rider.out_shape
        scratch = scratch + rider.scratch
    outs = pl.pallas_call(
        body,
        name=name,
        grid=(steps,),
        in_specs=in_specs,
        out_specs=out_specs,
        out_shape=out_shape,
        scratch_shapes=scratch,
        compiler_params=_params(("arbitrary",)),
    )(*args)
    return outs[0], list(outs[1:])


MM_FUSED_MARGIN_BYTES = 10 * 1024 * 1024
MM_FUSED_MAX_ROWS = 512


def _col_sum_update(acc_ref, val, first):
    part = jnp.sum(val.reshape(val.shape[0] // 8, 8, val.shape[1]), axis=0)

    @pl.when(first)
    def _():
        acc_ref[...] = part

    @pl.when(jnp.logical_not(first))
    def _():
        acc_ref[...] += part


def _mm_fused(a, b, *, mode, name, extras, outs, epilogue, sums=(), rider=None, max_rows=MM_FUSED_MAX_ROWS,
              pass_a=False):
    parts = list(a) if isinstance(a, (list, tuple)) else [a]
    M, K = parts[0].shape[0], sum(p.shape[1] for p in parts)
    if mode == "nn":
        (K2, N), b_dims = b.shape, _NN
    else:
        (N, K2), b_dims = b.shape, _NT
    assert K == K2, (K, b.shape, mode)
    rows = parts + [e for e in extras if e.shape[0] == M]
    per_row = 2 * (sum(e.shape[1] * e.dtype.itemsize for e in rows)
                   + sum(c * jnp.dtype(d).itemsize for c, d in outs)) + 2 * N * 4
    budget = VMEM_LIMIT_BYTES - K * N * 2 - MM_FUSED_MARGIN_BYTES
    tm = next(t for t in (512, 256, 128, 64, 32, 16) if t <= max_rows and M % t == 0 and t * per_row <= budget)
    steps = M // tm
    n_a, n_x, n_o, n_s = len(parts), len(extras), len(outs), len(sums)
    n_ride = rider.n if rider is not None else 0

    def body(*refs):
        a_refs, b_ref = refs[:n_a], refs[n_a]
        x_refs = refs[n_a + 1:n_a + 1 + n_x]
        base = n_a + 1 + n_x + n_ride
        o_refs, s_refs = refs[base:base + n_o], refs[base + n_o:base + n_o + n_s]
        acc_refs = refs[base + n_o + n_s + n_ride:base + n_o + 2 * n_s + n_ride]
        ids = [pl.program_id(0)]
        if rider is not None:
            ride = (refs[n_a + 1 + n_x:base], refs[base + n_o + n_s:base + n_o + n_s + n_ride], refs[-3:])
            rider.start_at_first(ids, ride)
        a_tile = a_refs[0][...]
        a_bf16 = a_tile.astype(BF16) if n_a == 1 else jnp.concatenate([r[...].astype(BF16) for r in a_refs], axis=1)
        prod = _dot(a_bf16, b_ref[...], b_dims)
        tiles = epilogue(prod, *([a_tile] if pass_a else []), *[r[...] for r in x_refs])
        for o_ref, t in zip(o_refs, tiles[:n_o]):
            o_ref[...] = t.astype(o_ref.dtype)
        for acc_ref, t in zip(acc_refs, tiles[n_o:]):
            _col_sum_update(acc_ref, t, ids[0] == 0)
        if n_s:
            @pl.when(ids[0] == steps - 1)
            def _():
                for s_ref, acc_ref in zip(s_refs, acc_refs):
                    s_ref[...] = jnp.sum(acc_ref[...], axis=0, keepdims=True)
        if rider is not None:
            rider.wait_at_last(ids, (steps,), ride)

    in_specs = [pl.BlockSpec((tm, p.shape[1]), lambda i: (i, 0)) for p in parts]
    in_specs.append(pl.BlockSpec(b.shape, lambda i: (0, 0), pipeline_mode=pl.Buffered(1)))
    for e in extras:
        in_specs.append(pl.BlockSpec((tm, e.shape[1]), lambda i: (i, 0)) if e.shape[0] == M
                        else pl.BlockSpec(e.shape, lambda i: (0, 0)))
    out_specs = ([pl.BlockSpec((tm, c), lambda i: (i, 0)) for c, _ in outs]
                 + [pl.BlockSpec((1, c), lambda i: (0, 0)) for c in sums])
    out_shape = ([jax.ShapeDtypeStruct((M, c), d) for c, d in outs]
                 + [jax.ShapeDtypeStruct((1, c), F32) for c in sums])
    args = tuple(parts) + (b,) + tuple(extras)
    scratch = [pltpu.VMEM((8, c), F32) for c in sums]
    if rider is not None:
        in_specs, args = in_specs + rider.specs, args + tuple(rider.bufs)
        out_specs, out_shape = out_specs + rider.specs, out_shape + rider.out_shape
        scratch = scratch + rider.scratch
    res = pl.pallas_call(
        body,
        name=name,
        grid=(steps,),
        in_specs=in_specs,
        out_specs=out_specs,
        out_shape=out_shape,
        scratch_shapes=scratch,
        compiler_params=_params(("arbitrary",) if (n_s or rider is not None) else ("parallel",)),
    )(*args)
    return tuple(res[:n_o + n_s]) + ((list(res[n_o + n_s:]),) if rider is not None else ())


def _pair_rows(blk, lane_is_a):
    zero = jnp.zeros_like(blk)
    return jnp.concatenate([jnp.where(lane_is_a, blk, zero), jnp.where(lane_is_a, zero, blk)], axis=0)


SB_STRIP = 32
SB_FWD_PAIRS = 4
SB_BWD_PAIRS = 2
SB_GROUP = 2


def _pair_lanes(p):
    return slice(p * LANES, (p + 1) * LANES)


def _sb_scan_matrices():
    o = lax.broadcasted_iota(jnp.int32, (2 * LANES, 4 * LANES), 0)
    c = lax.broadcasted_iota(jnp.int32, (2 * LANES, 4 * LANES), 1) & (2 * LANES - 1)
    same = (o >= LANES) == (c >= LANES)
    oo, cc = o & (LANES - 1), c & (LANES - 1)
    return (jnp.where(same & (cc > oo), 1.0, 0.0).astype(BF16), jnp.where(same & (cc < oo), 1.0, 0.0).astype(BF16))


def _sb_causal_masks(tq):
    d = lax.broadcasted_iota(jnp.int32, (tq // SB_KEY_BLOCK, SB_KEY_BLOCK, tq), 0)
    k = lax.broadcasted_iota(jnp.int32, (tq // SB_KEY_BLOCK, SB_KEY_BLOCK, tq), 1)
    t = lax.broadcasted_iota(jnp.int32, (tq // SB_KEY_BLOCK, SB_KEY_BLOCK, tq), 2)
    return jnp.where(d * SB_KEY_BLOCK + k < t, 1.0, 0.0).astype(F32)


def _sb_log_terms(z):
    log_rem = -jnp.maximum(z, 0.0) - jnp.log(1.0 + jnp.exp(-jnp.abs(z)))
    return log_rem, log_rem + z


def _sb_store_split(ref, strip, val, cols):
    hi = val.astype(BF16)
    ref[pl.ds(strip * SB_STRIP, SB_STRIP), cols] = hi
    ref[pl.ds(2 * LANES + strip * SB_STRIP, SB_STRIP), cols] = (val - hi.astype(F32)).astype(BF16)


def _sb_lanes(tq, diag):
    if diag == "left":
        return 0, tq // 2
    first = 0 if diag is None else diag * SB_KEY_BLOCK
    return first, tq - first


def _lane_add(full, part, lanes):
    first, width = lanes
    pieces = [full[:, :first]] if first else []
    pieces.append(full[:, first:first + width] + part)
    if first + width < full.shape[1]:
        pieces.append(full[:, first + width:])
    return pieces[0] if len(pieces) == 1 else jnp.concatenate(pieces, axis=1)


def _sb_fwd(h_a, rider=None):
    assert SB_FWD_PAIRS == 4
    T = h_a.shape[0]
    tq = _pick(T, (SB_Q_BLOCK, SB_KEY_BLOCK))
    nq, per_q, nkb = T // tq, tq // SB_KEY_BLOCK, T // SB_KEY_BLOCK
    assert per_q % SB_GROUP == 0
    n_strips = 2 * LANES // SB_STRIP
    n_ride = rider.n if rider is not None else 0
    after_m, _ = _sb_scan_matrices()
    causal_m = _sb_causal_masks(tq)
    pairs = SB_FWD_PAIRS

    def body(*refs):
        q_ref, k_ref, v_ref, after_ref, causal_ref = refs[:5]
        a_ref, r_ref, n_ref = refs[5 + n_ride:8 + n_ride]
        z_ref, lb_ref, split_ref, w_ref = refs[8 + 2 * n_ride:12 + 2 * n_ride]
        ids = [pl.program_id(0)]
        if rider is not None:
            ride = (refs[5:5 + n_ride], refs[8 + n_ride:8 + 2 * n_ride], refs[-3:])
            rider.start_at_first(ids, ride)
        i = ids[0]
        q_t = [(q_ref[:, _pair_lanes(p)].astype(F32).T * (SB_HEAD_DIM ** -0.5)).astype(BF16) for p in range(pairs)]
        lane_is_a = lax.broadcasted_iota(jnp.int32, (SB_KEY_BLOCK, LANES), 1) < SB_HEAD_DIM

        def tiles(kbs, diags, carry):
            nb = len(kbs)
            lanes = [_sb_lanes(tq, d) for d in diags]
            cols = [slice(first, first + width) for first, width in lanes]
            acc_t, ra, rb = [list(c) for c in carry]
            ks = [pl.multiple_of(kb * SB_KEY_BLOCK, SB_KEY_BLOCK) for kb in kbs]
            slot = lambda p, b: p * nb + b

            def causal(b, s):
                return causal_ref[diags[b], pl.ds((s * SB_STRIP) % SB_KEY_BLOCK, SB_STRIP), cols[b]]

            vv = {}
            for b in range(nb):
                for p in range(pairs):
                    kk = _pair_rows(k_ref[pl.ds(ks[b], SB_KEY_BLOCK), _pair_lanes(p)], lane_is_a)
                    vv[p, b] = _pair_rows(v_ref[pl.ds(ks[b], SB_KEY_BLOCK), _pair_lanes(p)], lane_is_a)
                    z_ref[slot(p, b), :, cols[b]] = _dot(kk, q_t[p][:, cols[b]], _NN)
            sums = {}
            for b in range(nb):
                for p in range(pairs):
                    part = [jnp.zeros((8, lanes[b][1]), F32), jnp.zeros((8, lanes[b][1]), F32)]
                    for s in range(n_strips):
                        rows = pl.ds(s * SB_STRIP, SB_STRIP)
                        log_rem, log_beta = _sb_log_terms(z_ref[slot(p, b), rows, cols[b]])
                        lb_ref[slot(p, b), rows, cols[b]] = log_beta
                        if isinstance(diags[b], int):
                            log_rem = log_rem * causal(b, s)
                        _sb_store_split(split_ref.at[slot(p, b)], s, log_rem, cols[b])
                        head = (s * SB_STRIP) // SB_KEY_BLOCK
                        part[head] = part[head] + jnp.sum(log_rem.reshape(SB_STRIP // 8, 8, lanes[b][1]), axis=0)
                    sums[p, b] = part
            for b in range(nb):
                for p in range(pairs):
                    z_ref[slot(p, b), :, cols[b]] = _dot(after_ref[...], split_ref[slot(p, b), :, cols[b]], _NN)
            for b in range(nb):
                for p in range(pairs):
                    for s in range(n_strips):
                        rows = pl.ds(s * SB_STRIP, SB_STRIP)
                        start = (ra[p] if (s * SB_STRIP) < SB_KEY_BLOCK else rb[p])[:, cols[b]]
                        w = jnp.exp(lb_ref[slot(p, b), rows, cols[b]] + z_ref[slot(p, b), rows, cols[b]] + start)
                        if isinstance(diags[b], int):
                            w = w * causal(b, s)
                        w_ref[slot(p, b), rows, cols[b]] = w.astype(BF16)
                    r_ref[2 * p, kbs[b]] = ra[p]
                    r_ref[2 * p + 1, kbs[b]] = rb[p]
                    ra[p] = _lane_add(ra[p], jnp.sum(sums[p, b][0], axis=0, keepdims=True), lanes[b])
                    rb[p] = _lane_add(rb[p], jnp.sum(sums[p, b][1], axis=0, keepdims=True), lanes[b])
            for b in range(nb):
                for p in range(pairs):
                    acc_t[p] = _lane_add(acc_t[p], _dot(vv[p, b], w_ref[slot(p, b), :, cols[b]], _TN), lanes[b])
            return tuple(acc_t), tuple(ra), tuple(rb)

        carry = (tuple(jnp.zeros((LANES, tq), F32) for _ in range(pairs)),
                 tuple(jnp.zeros((1, tq), F32) for _ in range(pairs)),
                 tuple(jnp.zeros((1, tq), F32) for _ in range(pairs)))
        own = list(reversed(range(per_q)))
        n_full = i * per_q
        carry = lax.cond(
            i > 0,
            lambda cc: tiles([n_full + d for d in own] + [n_full - 1 - b for b in range(SB_GROUP)],
                             own + [None] * SB_GROUP, cc),
            lambda cc: tiles([n_full + d for d in own], own, cc), carry)
        first_walked = jnp.where(i > 0, SB_GROUP, 0).astype(jnp.int32)

        def top_of(sums_a, sums_b, first):
            return jnp.max(functools.reduce(jnp.maximum, [r[:, first:] for r in sums_a + sums_b]))

        def alive(c):
            return jnp.logical_and(c[0] < n_full, top_of(c[2], c[3], 0) > SB_DEAD_LOG)

        def step(c):
            kbs = [n_full - 1 - c[0] - b for b in range(SB_GROUP)]
            return (c[0] + SB_GROUP,) + lax.cond(
                top_of(c[2], c[3], tq // 2) > SB_DEAD_LOG,
                lambda cc: tiles(kbs, [None] * SB_GROUP, cc), lambda cc: tiles(kbs, ["left"] * SB_GROUP, cc), c[1:])

        walked, acc_t, _, _ = lax.while_loop(alive, step, (first_walked,) + carry)
        for p in range(pairs):
            a_ref[:, _pair_lanes(p)] = acc_t[p].T.astype(BF16)
        n_ref[...] = jnp.zeros(n_ref.shape, F32) + walked.astype(F32)
        if rider is not None:
            rider.wait_at_last(ids, (nq,), ride)

    wide = pairs * LANES
    in_specs = [pl.BlockSpec((tq, wide), lambda i: (i, 0)),
                pl.BlockSpec((T, wide), lambda i: (0, 1), pipeline_mode=pl.Buffered(1)),
                pl.BlockSpec((T, wide), lambda i: (0, 2), pipeline_mode=pl.Buffered(1)),
                pl.BlockSpec(after_m.shape, lambda i: (0, 0), pipeline_mode=pl.Buffered(1)),
                pl.BlockSpec(causal_m.shape, lambda i: (0, 0, 0), pipeline_mode=pl.Buffered(1))]
    out_specs = [pl.BlockSpec((tq, wide), lambda i: (i, 0)),
                 pl.BlockSpec((2 * pairs, nkb, 1, tq), lambda i: (0, 0, 0, i)),
                 pl.BlockSpec((1, 8, LANES), lambda i: (i, 0, 0))]
    out_shape = [jax.ShapeDtypeStruct((T, SB_WIDTH), BF16), jax.ShapeDtypeStruct((2 * pairs, nkb, 1, T), F32),
                 jax.ShapeDtypeStruct((nq, 8, LANES), F32)]
    args = (h_a, h_a, h_a, after_m, causal_m)
    slots = pairs * (per_q + SB_GROUP)
    scratch = [pltpu.VMEM((slots, 2 * LANES, tq), F32), pltpu.VMEM((slots, 2 * LANES, tq), F32),
               pltpu.VMEM((slots, 4 * LANES, tq), BF16), pltpu.VMEM((slots, 2 * LANES, tq), BF16)]
    if rider is not None:
        in_specs, args = in_specs + rider.specs, args + tuple(rider.bufs)
        out_specs, out_shape = out_specs + rider.specs, out_shape + rider.out_shape
        scratch = scratch + rider.scratch
    outs = pl.pallas_call(
        body,
        name="sb_fwd",
        grid=(nq,),
        in_specs=in_specs,
        out_specs=out_specs,
        out_shape=out_shape,
        scratch_shapes=scratch,
        compiler_params=_params(("arbitrary",)),
    )(*args)
    return outs[0], (outs[1], outs[2]), list(outs[3:])


def _sb_bwd(h_a, d_out, saved, rider=None):
    r_mat, walked_blocks = saved
    T = h_a.shape[0]
    tq = _pick(T, (SB_Q_BLOCK, SB_KEY_BLOCK))
    nq, per_q, nkb = T // tq, tq // SB_KEY_BLOCK, T // SB_KEY_BLOCK
    n_strips = 2 * LANES // SB_STRIP
    after_m, before_m = _sb_scan_matrices()
    causal_m = _sb_causal_masks(tq)
    pairs = SB_BWD_PAIRS
    groups = 4 // pairs
    n_ride = rider.n if rider is not None else 0

    def body(*refs):
        q_ref, k_ref, v_ref, do_ref, r_ref, n_ref, after_ref, before_ref, causal_ref = refs[:9]
        dq_ref, dk_ref, dv_ref = refs[9 + n_ride:12 + n_ride]
        z_ref, lb_ref, split_ref, w_ref, da_ref, dz_ref = refs[12 + 2 * n_ride:18 + 2 * n_ride]
        ids = [pl.program_id(0), pl.program_id(1)]
        if rider is not None:
            ride = (refs[9:9 + n_ride], refs[12 + n_ride:12 + 2 * n_ride], refs[-3:])
            rider.start_at_first(ids, ride)
        i = ids[1]

        @pl.when(i == 0)
        def _():
            dk_ref[...] = jnp.zeros_like(dk_ref)
            dv_ref[...] = jnp.zeros_like(dv_ref)

        scale = SB_HEAD_DIM ** -0.5
        q = [q_ref[:, _pair_lanes(p)] for p in range(pairs)]
        d_o = [do_ref[:, _pair_lanes(p)] for p in range(pairs)]
        q_t = [(x.astype(F32).T * scale).astype(BF16) for x in q]
        do_t = [x.astype(F32).T.astype(BF16) for x in d_o]
        lane_is_a = lax.broadcasted_iota(jnp.int32, (SB_KEY_BLOCK, LANES), 1) < SB_HEAD_DIM

        def tiles(kbs, diags, carry):
            nb = len(kbs)
            lanes = [_sb_lanes(tq, d) for d in diags]
            cols = [slice(first, first + width) for first, width in lanes]
            dq_t, ca, cb = [list(c) for c in carry]
            ks = [pl.multiple_of(kb * SB_KEY_BLOCK, SB_KEY_BLOCK) for kb in kbs]
            slot = lambda p, b: p * nb + b

            def causal(b, s):
                return causal_ref[diags[b], pl.ds((s * SB_STRIP) % SB_KEY_BLOCK, SB_STRIP), cols[b]]

            kk, vv = {}, {}
            for b in range(nb):
                for p in range(pairs):
                    kk[p, b] = _pair_rows(k_ref[pl.ds(ks[b], SB_KEY_BLOCK), _pair_lanes(p)], lane_is_a)
                    vv[p, b] = _pair_rows(v_ref[pl.ds(ks[b], SB_KEY_BLOCK), _pair_lanes(p)], lane_is_a)
                    z_ref[slot(p, b), :, cols[b]] = _dot(kk[p, b], q_t[p][:, cols[b]], _NN)
            for b in range(nb):
                for p in range(pairs):
                    for s in range(n_strips):
                        rows = pl.ds(s * SB_STRIP, SB_STRIP)
                        log_rem, log_beta = _sb_log_terms(z_ref[slot(p, b), rows, cols[b]])
                        lb_ref[slot(p, b), rows, cols[b]] = log_beta
                        if isinstance(diags[b], int):
                            log_rem = log_rem * causal(b, s)
                        _sb_store_split(split_ref.at[slot(p, b)], s, log_rem, cols[b])
            for b in range(nb):
                for p in range(pairs):
                    z_ref[slot(p, b), :, cols[b]] = _dot(after_ref[...], split_ref[slot(p, b), :, cols[b]], _NN)
                    da_ref[slot(p, b), :, cols[b]] = _dot(vv[p, b], do_t[p][:, cols[b]], _NN)
            sums = {}
            for b in range(nb):
                for p in range(pairs):
                    part = [jnp.zeros((8, lanes[b][1]), F32), jnp.zeros((8, lanes[b][1]), F32)]
                    for s in range(n_strips):
                        rows = pl.ds(s * SB_STRIP, SB_STRIP)
                        start = r_ref[2 * p + (s * SB_STRIP) // SB_KEY_BLOCK, kbs[b]][:, cols[b]]
                        w = jnp.exp(lb_ref[slot(p, b), rows, cols[b]] + z_ref[slot(p, b), rows, cols[b]] + start)
                        if isinstance(diags[b], int):
                            w = w * causal(b, s)
                        w_ref[slot(p, b), rows, cols[b]] = w.astype(BF16)
                        da = da_ref[slot(p, b), rows, cols[b]] * w
                        da_ref[slot(p, b), rows, cols[b]] = da
                        _sb_store_split(split_ref.at[slot(p, b)], s, da, cols[b])
                        head = (s * SB_STRIP) // SB_KEY_BLOCK
                        part[head] = part[head] + jnp.sum(da.reshape(SB_STRIP // 8, 8, lanes[b][1]), axis=0)
                    sums[p, b] = part
            for b in range(nb):
                for p in range(pairs):
                    z_ref[slot(p, b), :, cols[b]] = _dot(before_ref[...], split_ref[slot(p, b), :, cols[b]], _NN)
            for b in range(nb):
                for p in range(pairs):
                    for s in range(n_strips):
                        rows = pl.ds(s * SB_STRIP, SB_STRIP)
                        base = (ca[p] if (s * SB_STRIP) < SB_KEY_BLOCK else cb[p])[:, cols[b]]
                        sig = jnp.exp(lb_ref[slot(p, b), rows, cols[b]])
                        dz = (da_ref[slot(p, b), rows, cols[b]] * (1.0 - sig)
                              - (z_ref[slot(p, b), rows, cols[b]] + base) * sig)
                        if isinstance(diags[b], int):
                            dz = dz * causal(b, s)
                        dz_ref[slot(p, b), rows, cols[b]] = (dz * scale).astype(BF16)
                    ca[p] = _lane_add(ca[p], jnp.sum(sums[p, b][0], axis=0, keepdims=True), lanes[b])
                    cb[p] = _lane_add(cb[p], jnp.sum(sums[p, b][1], axis=0, keepdims=True), lanes[b])
            for b in range(nb):
                for p in range(pairs):
                    dq_t[p] = _lane_add(dq_t[p], _dot(kk[p, b], dz_ref[slot(p, b), :, cols[b]], _TN), lanes[b])
                    dkk = _dot(dz_ref[slot(p, b), :, cols[b]], q[p][cols[b], :], _NN)
                    dvv = _dot(w_ref[slot(p, b), :, cols[b]], d_o[p][cols[b], :], _NN)
                    here = (pl.ds(ks[b], SB_KEY_BLOCK), _pair_lanes(p))
                    dk_ref[here] += jnp.where(lane_is_a, dkk[:SB_KEY_BLOCK], dkk[SB_KEY_BLOCK:])
                    dv_ref[here] += jnp.where(lane_is_a, dvv[:SB_KEY_BLOCK], dvv[SB_KEY_BLOCK:])
            return tuple(dq_t), tuple(ca), tuple(cb)

        n_full = i * per_q
        groups_walked = jnp.clip(jnp.max(n_ref[...]).astype(jnp.int32), 0, n_full) // SB_GROUP
        carry = (tuple(jnp.zeros((LANES, tq), F32) for _ in range(pairs)),
                 tuple(jnp.zeros((1, tq), F32) for _ in range(pairs)),
                 tuple(jnp.zeros((1, tq), F32) for _ in range(pairs)))

        def below(j, c):
            kbs = [n_full - (groups_walked - j) * SB_GROUP + b for b in range(SB_GROUP)]
            starts = [r_ref[h, kbs[-1]][:, tq // 2:] for h in range(2 * pairs)]
            reaches = jnp.max(functools.reduce(jnp.maximum, starts)) > SB_DEAD_LOG
            return lax.cond(reaches, lambda cc: tiles(kbs, [None] * SB_GROUP, cc),
                            lambda cc: tiles(kbs, ["left"] * SB_GROUP, cc), c)

        carry = lax.fori_loop(0, groups_walked, below, carry)
        own = list(range(per_q))
        carry = tiles([i * per_q + d for d in own], own, carry)
        for p in range(pairs):
            dq_ref[:, _pair_lanes(p)] = carry[0][p].T.astype(BF16)
        if rider is not None:
            rider.wait_at_last(ids, (groups, nq), ride)

    wide = pairs * LANES
    mat = pl.BlockSpec(after_m.shape, lambda g, i: (0, 0), pipeline_mode=pl.Buffered(1))
    in_specs = [pl.BlockSpec((tq, wide), lambda g, i: (i, g)),
                pl.BlockSpec((T, wide), lambda g, i: (0, groups + g), pipeline_mode=pl.Buffered(1)),
                pl.BlockSpec((T, wide), lambda g, i: (0, 2 * groups + g), pipeline_mode=pl.Buffered(1)),
                pl.BlockSpec((tq, wide), lambda g, i: (i, g)),
                pl.BlockSpec((2 * pairs, nkb, 1, tq), lambda g, i: (g, 0, 0, i)),
                pl.BlockSpec((1, 8, LANES), lambda g, i: (i, 0, 0)),
                mat, mat,
                pl.BlockSpec(causal_m.shape, lambda g, i: (0, 0, 0), pipeline_mode=pl.Buffered(1))]
    out_specs = [pl.BlockSpec((tq, wide), lambda g, i: (i, g)),
                 pl.BlockSpec((T, wide), lambda g, i: (0, g)),
                 pl.BlockSpec((T, wide), lambda g, i: (0, g))]
    out_shape = [jax.ShapeDtypeStruct((T, SB_WIDTH), BF16), jax.ShapeDtypeStruct((T, SB_WIDTH), F32),
                 jax.ShapeDtypeStruct((T, SB_WIDTH), F32)]
    args = (h_a, h_a, h_a, d_out, r_mat, walked_blocks, after_m, before_m, causal_m)
    slots = pairs * max(per_q, SB_GROUP)
    scratch = [pltpu.VMEM((slots, 2 * LANES, tq), F32), pltpu.VMEM((slots, 2 * LANES, tq), F32),
               pltpu.VMEM((slots, 4 * LANES, tq), BF16), pltpu.VMEM((slots, 2 * LANES, tq), BF16),
               pltpu.VMEM((slots, 2 * LANES, tq), F32), pltpu.VMEM((slots, 2 * LANES, tq), BF16)]
    if rider is not None:
        in_specs, args = in_specs + rider.specs, args + tuple(rider.bufs)
        out_specs, out_shape = out_specs + rider.specs, out_shape + rider.out_shape
        scratch = scratch + rider.scratch
    outs = pl.pallas_call(
        body,
        name="sb_bwd",
        grid=(groups, nq),
        in_specs=in_specs,
        out_specs=out_specs,
        out_shape=out_shape,
        scratch_shapes=scratch,
        compiler_params=_params(("arbitrary", "arbitrary") if rider is not None else ("parallel", "arbitrary")),
    )(*args)
    return outs[0], outs[1], outs[2], list(outs[3:])


def _ret_tables(T):
    half = RET_QK_DIM // 2
    inv = 1.0 / (ROPE_BASE ** (jnp.arange(half, dtype=F32) / half))
    ang = jnp.arange(T, dtype=F32)[:, None] * inv[None, :]
    cos, sin = jnp.cos(ang), jnp.sin(ang)
    cos_t = jnp.concatenate([cos, cos], axis=1)
    sin_t = jnp.concatenate([-sin, sin], axis=1)
    log_gamma = jnp.log1p(-jnp.exp2(-5.0 - jnp.arange(RET_HEADS, dtype=F32)))
    idx = jnp.arange(RET_CHUNK, dtype=F32)
    rel = idx[:, None] - idx[None, :]
    decay = jnp.where(rel[None] >= 0, jnp.exp(log_gamma[:, None, None] * jnp.maximum(rel, 0.0)[None]), 0.0)
    k_decay = jnp.exp(log_gamma[None, :] * (RET_CHUNK - 1.0 - idx)[:, None])
    q_decay = jnp.exp(log_gamma[None, :] * (idx + 1.0)[:, None])
    chunk_decay = jnp.exp(log_gamma * RET_CHUNK)
    k_dec = jnp.broadcast_to(k_decay.T[:, :, None], (RET_HEADS, RET_CHUNK, LANES))
    q_dec = jnp.broadcast_to(q_decay.T[:, :, None], (RET_HEADS, RET_CHUNK, LANES))
    c_dec = jnp.broadcast_to(chunk_decay[:, None, None], (RET_HEADS, 8, LANES))
    return cos_t, sin_t, decay, k_dec, q_dec, c_dec


def _rotary(x, cos_t, sin_t):
    return x * cos_t + pltpu.roll(x, RET_QK_DIM // 2, 1) * sin_t


def _rotary_transpose(dy, cos_t, sin_t):
    return dy * cos_t + pltpu.roll(dy * sin_t, RET_QK_DIM // 2, 1)


def _head_norm(o):
    mu = jnp.mean(o, axis=1, keepdims=True)
    cen = o - mu
    var = jnp.mean(cen * cen, axis=1, keepdims=True)
    rstd = lax.rsqrt(var + LN_EPS)
    return cen * rstd, rstd


def _ret_specs(steps, per_step, reverse):
    def n_of(n):
        return (steps - 1 - n) if reverse else n

    rows = per_step * RET_CHUNK
    q_spec = pl.BlockSpec((rows, RET_QK_WIDTH), lambda n: (n_of(n), 0))
    k_spec = pl.BlockSpec((rows, RET_QK_WIDTH), lambda n: (n_of(n), 1))
    vv = pl.BlockSpec((rows, RET_V_WIDTH), lambda n: (n_of(n), 0))
    pos = pl.BlockSpec((rows, LANES), lambda n: (n_of(n), 0))
    per_head = pl.BlockSpec((RET_HEADS, RET_CHUNK, LANES), lambda n: (0, 0, 0))
    c_dec = pl.BlockSpec((RET_HEADS, 8, LANES), lambda n: (0, 0, 0))
    state = pl.BlockSpec((RET_HEADS, per_step, RET_QK_DIM, RET_V_DIM), lambda n: (0, n_of(n), 0, 0))
    return q_spec, k_spec, vv, pos, per_head, c_dec, state


def _qk_cols(h):
    return slice(h * RET_QK_DIM, (h + 1) * RET_QK_DIM)


def _v_cols(h):
    return slice(h * RET_V_DIM, (h + 1) * RET_V_DIM)


def _ret_fwd(h_b, h_c, h_d, tables):
    T = h_b.shape[0]
    nc = T // RET_CHUNK
    per_step = _pick(nc, (RET_STEP_CHUNKS, 1))
    steps = nc // per_step
    q_spec, k_spec, vv, pos, per_head, c_dec, state = _ret_specs(steps, per_step, False)

    def body(q_ref, k_ref, v_ref, g_ref, cos_ref, sin_ref, dec_ref, kd_ref, qd_ref, cd_ref,
             y_ref, o_ref, st_ref, state_ref):
        @pl.when(pl.program_id(0) == 0)
        def _():
            state_ref[...] = jnp.zeros_like(state_ref)

        for c in range(per_step):
            rows = pl.ds(c * RET_CHUNK, RET_CHUNK)
            cos_t, sin_t = cos_ref[rows, :], sin_ref[rows, :]
            for h in range(RET_HEADS):
                q = _rotary(q_ref[rows, _qk_cols(h)], cos_t, sin_t) * (RET_QK_DIM ** -0.5)
                k = _rotary(k_ref[rows, _qk_cols(h)], cos_t, sin_t)
                v = v_ref[rows, _v_cols(h)]
                prev = state_ref[h]
                scores = _dot(q.astype(BF16), k.astype(BF16), _NT) * dec_ref[h]
                inner = _dot(scores.astype(BF16), v, _NN)
                cross = _dot((q * qd_ref[h]).astype(BF16), prev.astype(BF16), _NN)
                o = inner + cross
                st_ref[h, c] = prev
                kv = _dot((k * kd_ref[h]).astype(BF16), v, _TN)
                state_ref[h] = prev * cd_ref[h, 0:1, 0:1] + kv
                o_ref[rows, _v_cols(h)] = o
                normed, _ = _head_norm(o)
                gate = g_ref[rows, _v_cols(h)]
                y_ref[rows, _v_cols(h)] = (gate * jax.nn.sigmoid(gate) * normed).astype(BF16)

    return pl.pallas_call(
        body,
        name="ret_fwd",
        grid=(steps,),
        in_specs=[q_spec, k_spec, vv, vv, pos, pos, per_head, per_head, per_head, c_dec],
        out_specs=[vv, vv, state],
        out_shape=[jax.ShapeDtypeStruct((T, RET_V_WIDTH), BF16),
                   jax.ShapeDtypeStruct((T, RET_V_WIDTH), F32),
                   jax.ShapeDtypeStruct((RET_HEADS, nc, RET_QK_DIM, RET_V_DIM), F32)],
        scratch_shapes=[pltpu.VMEM((RET_HEADS, RET_QK_DIM, RET_V_DIM), F32)],
        compiler_params=_params(("arbitrary",)),
    )(h_b, h_b, h_c, h_d, *tables)


def _ret_bwd(d_y, o_pre, states, h_b, h_c, h_d, tables, rider=None):
    T = h_b.shape[0]
    nc = T // RET_CHUNK
    per_step = _pick(nc, (RET_STEP_CHUNKS, 1))
    steps = nc // per_step
    q_spec, k_spec, vv, pos, per_head, c_dec, state = _ret_specs(steps, per_step, True)
    n_ride = rider.n if rider is not None else 0

    def body(*refs):
        (dy_ref, o_ref, st_ref, q_ref, k_ref, v_ref, g_ref, cos_ref, sin_ref, dec_ref, kd_ref, qd_ref,
         cd_ref) = refs[:13]
        dq_ref, dk_ref, dv_ref, dg_ref = refs[13 + n_ride:17 + n_ride]
        carry_ref = refs[17 + 2 * n_ride]
        ids = [pl.program_id(0)]
        if rider is not None:
            ride = (refs[13:13 + n_ride], refs[17 + n_ride:17 + 2 * n_ride], refs[-3:])
            rider.start_at_first(ids, ride)

        @pl.when(ids[0] == 0)
        def _():
            carry_ref[...] = jnp.zeros_like(carry_ref)

        scale = RET_QK_DIM ** -0.5
        for c in reversed(range(per_step)):
            rows = pl.ds(c * RET_CHUNK, RET_CHUNK)
            cos_t, sin_t = cos_ref[rows, :], sin_ref[rows, :]
            for h in range(RET_HEADS):
                q = _rotary(q_ref[rows, _qk_cols(h)], cos_t, sin_t) * scale
                k = _rotary(k_ref[rows, _qk_cols(h)], cos_t, sin_t)
                v = v_ref[rows, _v_cols(h)]
                decay, k_dec, q_dec = dec_ref[h], kd_ref[h], qd_ref[h]
                chunk_decay = cd_ref[h, 0:1, 0:1]
                state = st_ref[h, c].astype(BF16)
                later = carry_ref[h]
                later_b = later.astype(BF16)

                gate = g_ref[rows, _v_cols(h)]
                sig = jax.nn.sigmoid(gate)
                silu = gate * sig
                normed, rstd = _head_norm(o_ref[rows, _v_cols(h)])
                d_y = dy_ref[rows, _v_cols(h)]
                dg_ref[rows, _v_cols(h)] = (d_y * normed * (sig * (1.0 + gate * (1.0 - sig)))).astype(BF16)
                d_n = d_y * silu
                d_o = rstd * (d_n - jnp.mean(d_n, axis=1, keepdims=True)
                              - normed * jnp.mean(d_n * normed, axis=1, keepdims=True))
                d_ob = d_o.astype(BF16)

                qb, kb = q.astype(BF16), k.astype(BF16)
                qd_b, kd_b = (q * q_dec).astype(BF16), (k * k_dec).astype(BF16)
                scores = _dot(qb, kb, _NT) * decay
                d_scores = (_dot(d_ob, v, _NT) * decay).astype(BF16)
                dq = _dot(d_scores, kb, _NN) + _dot(d_ob, state, _NT) * q_dec
                dk = _dot(d_scores, qb, _TN) + _dot(v, later_b, _NT) * k_dec
                dv = _dot(scores.astype(BF16), d_ob, _TN) + _dot(kd_b, later_b, _NN)
                carry_ref[h] = _dot(qd_b, d_ob, _TN) + chunk_decay * later
                dq_ref[rows, _qk_cols(h)] = _rotary_transpose(dq * scale, cos_t, sin_t).astype(BF16)
                dk_ref[rows, _qk_cols(h)] = _rotary_transpose(dk, cos_t, sin_t).astype(BF16)
                dv_ref[rows, _v_cols(h)] = dv.astype(BF16)
        if rider is not None:
            rider.wait_at_last(ids, (steps,), ride)

    qk_out = pl.BlockSpec((per_step * RET_CHUNK, RET_QK_WIDTH), lambda n: (steps - 1 - n, 0))
    in_specs = [vv, vv, state, q_spec, k_spec, vv, vv, pos, pos, per_head, per_head, per_head, c_dec]
    out_specs = [qk_out, qk_out, vv, vv]
    out_shape = [jax.ShapeDtypeStruct((T, RET_QK_WIDTH), BF16), jax.ShapeDtypeStruct((T, RET_QK_WIDTH), BF16),
                 jax.ShapeDtypeStruct((T, RET_V_WIDTH), BF16), jax.ShapeDtypeStruct((T, RET_V_WIDTH), BF16)]
    args = (d_y, o_pre, states, h_b, h_b, h_c, h_d) + tuple(tables)
    scratch = [pltpu.VMEM((RET_HEADS, RET_QK_DIM, RET_V_DIM), F32)]
    if rider is not None:
        in_specs, args = in_specs + rider.specs, args + tuple(rider.bufs)
        out_specs, out_shape = out_specs + rider.specs, out_shape + rider.out_shape
        scratch = scratch + rider.scratch
    outs = pl.pallas_call(
        body,
        name="ret_bwd",
        grid=(steps,),
        in_specs=in_specs,
        out_specs=out_specs,
        out_shape=out_shape,
        scratch_shapes=scratch,
        compiler_params=_params(("arbitrary",)),
    )(*args)
    return outs[0], outs[1], outs[2], outs[3], list(outs[4:])


def _proj_tiles(h, x):
    return h[:, 0:1536], h[:, 1536:2560], h[:, 2560:3584], h[:, 3584:4608], h[:, 4608:6656], x


def _gate_mix_tiles(y_ret, h_e, b_gate, y_sb):
    gates = jax.nn.sigmoid(h_e + b_gate)
    return y_ret, gates[:, :D_MODEL] * y_sb + gates[:, D_MODEL:] * y_ret


def _gate_mix_grad_tiles(d_mix, h_e, b_gate, y_sb, y_ret):
    gates = jax.nn.sigmoid(h_e + b_gate)
    g0, g1 = gates[:, :D_MODEL], gates[:, D_MODEL:]
    d_e = jnp.concatenate([d_mix * y_sb * g0 * (1.0 - g0), d_mix * y_ret * g1 * (1.0 - g1)], axis=1)
    return d_mix * g0, d_mix * g1, d_e, d_e


def _ln_stats(u):
    mu = jnp.mean(u, axis=1, keepdims=True)
    cen = u - mu
    var = jnp.mean(cen * cen, axis=1, keepdims=True)
    rstd = lax.rsqrt(var + LN_EPS)
    return cen * rstd, rstd


def _ln_input_grad(d_out, gain, xhat, rstd):
    d_hat = d_out * gain
    return rstd * (d_hat - jnp.mean(d_hat, axis=1, keepdims=True)
                   - xhat * jnp.mean(d_hat * xhat, axis=1, keepdims=True))


def _ln_tiles(sub, x_prev, gain, bias):
    xhat, rstd = _ln_stats(DN_ALPHA * x_prev + sub)
    out = xhat * gain + bias
    return out, out, xhat, rstd


def _residual_tiles(d_sub, res):
    return (d_sub + DN_ALPHA * res,)


def _ln_grad_tiles(d_sub, res, xhat, rstd, gain):
    d_out = d_sub + DN_ALPHA * res
    du = _ln_input_grad(d_out, gain, xhat, rstd)
    return du, du, d_out * xhat, d_out


def _ln_loss_tiles(sub, x_prev, gain, bias, target):
    xhat, rstd = _ln_stats(DN_ALPHA * x_prev + sub)
    diff = xhat * gain + bias - target
    d_out = diff * (1.0 / D_MODEL)
    du = _ln_input_grad(d_out, gain, xhat, rstd)
    return du, du, diff * diff, d_out * xhat, d_out


def _mem_probs(q_h, k_h):
    s = _dot(q_h, k_h, _NT) * (MEM_HEAD_DIM ** -0.5)
    e = jnp.exp(s - jnp.max(s, axis=1, keepdims=True))
    return e / jnp.sum(e, axis=1, keepdims=True)


def _xattn_fwd(q, kv):
    T, mem_len = q.shape[0], kv.shape[0]
    tq = _pick(T, (512, 256, 128))

    def body(q_ref, kv_ref, o_ref):
        for h in range(MEM_HEADS):
            cols = slice(h * MEM_HEAD_DIM, (h + 1) * MEM_HEAD_DIM)
            vcols = slice(D_MODEL + h * MEM_HEAD_DIM, D_MODEL + (h + 1) * MEM_HEAD_DIM)
            p = _mem_probs(q_ref[:, cols], kv_ref[:, cols])
            o_ref[:, cols] = _dot(p.astype(BF16), kv_ref[:, vcols], _NN).astype(BF16)

    return pl.pallas_call(
        body,
        name="xattn_fwd",
        grid=(T // tq,),
        in_specs=[pl.BlockSpec((tq, D_MODEL), lambda i: (i, 0)),
                  pl.BlockSpec((mem_len, 2 * D_MODEL), lambda i: (0, 0))],
        out_specs=pl.BlockSpec((tq, D_MODEL), lambda i: (i, 0)),
        out_shape=jax.ShapeDtypeStruct((T, D_MODEL), BF16),
        compiler_params=_params(("parallel",)),
    )(q, kv)


def _xattn_bwd(q, kv, d_o):
    T, mem_len = q.shape[0], kv.shape[0]
    tq = _pick(T, (512, 256, 128))

    def body(q_ref, kv_ref, do_ref, dq_ref, dkv_ref):
        @pl.when(pl.program_id(0) == 0)
        def _():
            dkv_ref[...] = jnp.zeros_like(dkv_ref)

        for h in range(MEM_HEADS):
            cols = slice(h * MEM_HEAD_DIM, (h + 1) * MEM_HEAD_DIM)
            vcols = slice(D_MODEL + h * MEM_HEAD_DIM, D_MODEL + (h + 1) * MEM_HEAD_DIM)
            q_h, k_h, do_h = q_ref[:, cols], kv_ref[:, cols], do_ref[:, cols]
            p = _mem_probs(q_h, k_h)
            dp = _dot(do_h, kv_ref[:, vcols], _NT)
            ds = p * (dp - jnp.sum(dp * p, axis=1, keepdims=True))
            dsb = (ds * (MEM_HEAD_DIM ** -0.5)).astype(BF16)
            dq_ref[:, cols] = _dot(dsb, k_h, _NN).astype(BF16)
            dkv_ref[:, cols] += _dot(dsb, q_h, _TN)
            dkv_ref[:, vcols] += _dot(p.astype(BF16), do_h, _TN)

    row = pl.BlockSpec((tq, D_MODEL), lambda i: (i, 0))
    full = pl.BlockSpec((mem_len, 2 * D_MODEL), lambda i: (0, 0))
    return pl.pallas_call(
        body,
        name="xattn_bwd",
        grid=(T // tq,),
        in_specs=[row, full, row],
        out_specs=[row, full],
        out_shape=[jax.ShapeDtypeStruct((T, D_MODEL), BF16), jax.ShapeDtypeStruct((mem_len, 2 * D_MODEL), F32)],
        compiler_params=_params(("arbitrary",)),
    )(q, kv, d_o)


def _swiglu_tiles(f):
    a, b = f[:, :FFN_HIDDEN], f[:, FFN_HIDDEN:]
    return f, a * jax.nn.sigmoid(a) * b


def _swiglu_grad_tiles(d_hidden, f):
    a, b = f[:, :FFN_HIDDEN], f[:, FFN_HIDDEN:]
    sig = jax.nn.sigmoid(a)
    return (jnp.concatenate([d_hidden * b * (sig * (1.0 + a * (1.0 - sig))), d_hidden * (a * sig)], axis=1),)


def _local_step(x, mem, w_in, small, target, fetch, ship):
    T = x.shape[0]
    tables = _ret_tables(T)
    memb = mem.astype(BF16)

    (h_a, h_b, h_c, h_d, h_e, xb), w_ffn = fetch(
        ("w_ffn_in", "w_ffn_out"),
        lambda rider: _as_host(rider, _mm_fused(
            x, w_in, mode="nn", name="proj_in", extras=[], pass_a=True,
            outs=[(1536, BF16), (1024, F32), (1024, BF16), (1024, F32), (2048, F32), (D_MODEL, BF16)],
            epilogue=_proj_tiles, max_rows=256, rider=rider)))
    (a_sb, r_mat), w_mix = fetch(("w_sb_o", "w_ret_o", "w_mix_o", "w_mem_q", "w_mem_kv", "w_mem_o"),
                                 lambda rider: _sb_fwd(h_a, rider))
    w = {**w_ffn, **w_mix}
    y_gated, o_pre, states = _ret_fwd(h_b, h_c, h_d, tables)
    y_sb = _mm(a_sb, w["w_sb_o"], mode="nn", out_dtype=F32, name="sb_out")
    row_f32, row_bf16 = (D_MODEL, F32), (D_MODEL, BF16)
    ln_outs = [row_f32, row_bf16, row_f32, (1, F32)]
    y_ret, mix_in = _mm_fused(y_gated, w["w_ret_o"], mode="nn", name="ret_out", extras=[h_e, small["b_gate"], y_sb],
                              outs=[row_f32, row_bf16], epilogue=_gate_mix_tiles)
    x1, x1b, xhat1, rstd1 = _mm_fused(mix_in, w["w_mix_o"], mode="nn", name="mix_out",
                                      extras=[x, small["ln1_g"], small["ln1_b"]], outs=ln_outs, epilogue=_ln_tiles)
    q_m = _mm(x1b, w["w_mem_q"], mode="nn", out_dtype=BF16, name="mem_q")
    kv_m = _mm(memb, w["w_mem_kv"], mode="nn", out_dtype=BF16, name="mem_kv")
    o_m = _xattn_fwd(q_m, kv_m)
    x2, x2b, xhat2, rstd2 = _mm_fused(o_m, w["w_mem_o"], mode="nn", name="mem_out",
                                      extras=[x1, small["ln2_g"], small["ln2_b"]], outs=ln_outs, epilogue=_ln_tiles)
    f, hidden = _mm_fused(x2b, w["w_ffn_in"], mode="nn", name="ffn_in", extras=[],
                          outs=[(2 * FFN_HIDDEN, F32), (FFN_HIDDEN, BF16)], epilogue=_swiglu_tiles)
    du_outs, col = [row_f32, row_bf16], D_MODEL
    du3, du3b, loss_cols, d_ln3_g, d_ln3_b = _mm_fused(
        hidden, w["w_ffn_out"], mode="nn", name="ffn_out", extras=[x2, small["ln3_g"], small["ln3_b"], target],
        outs=du_outs, sums=[col, col, col], epilogue=_ln_loss_tiles, max_rows=256)

    g_ffn_out = _mm(hidden, du3b, mode="tn", out_dtype=BF16, name="g_ffn_out")
    (d_f,) = _mm_fused(du3b, w["w_ffn_out"], mode="nt", name="d_hidden", extras=[f],
                       outs=[(2 * FFN_HIDDEN, BF16)], epilogue=_swiglu_grad_tiles)
    g_ffn_in = _mm(x2b, d_f, mode="tn", out_dtype=BF16, name="g_ffn_in")
    du2, du2b, d_ln2_g, d_ln2_b = ship(
        {"w_ffn_out": g_ffn_out},
        lambda rider: _as_host(rider, _mm_fused(
            d_f, w["w_ffn_in"], mode="nt", name="d_x2", extras=[du3, xhat2, rstd2, small["ln2_g"]], outs=du_outs,
            sums=[col, col], epilogue=_ln_grad_tiles, rider=rider, max_rows=256)))
    g_mem_o = _mm(o_m, du2b, mode="tn", out_dtype=BF16, name="g_mem_o")
    d_om = _mm(du2b, w["w_mem_o"], mode="nt", out_dtype=BF16, name="d_om")
    d_qm, d_kvm = _xattn_bwd(q_m, kv_m, d_om)
    g_mem_q = _mm(x1b, d_qm, mode="tn", out_dtype=BF16, name="g_mem_q")
    g_mem_kv = _mm(memb, d_kvm.astype(BF16), mode="tn", out_dtype=BF16, name="g_mem_kv")
    du1, du1b, d_ln1_g, d_ln1_b = _mm_fused(
        d_qm, w["w_mem_q"], mode="nt", name="d_x1", extras=[du2, xhat1, rstd1, small["ln1_g"]], outs=du_outs,
        sums=[col, col], epilogue=_ln_grad_tiles, max_rows=256)
    g_mix_o = _mm(mix_in, du1b, mode="tn", out_dtype=BF16, name="g_mix_o")
    d_ysb, d_yret, d_e, d_b_gate = _mm_fused(
        du1b, w["w_mix_o"], mode="nt", name="d_mix_in", extras=[h_e, small["b_gate"], y_sb, y_ret],
        outs=[row_bf16, row_bf16, (2 * D_MODEL, BF16)], sums=[2 * D_MODEL], epilogue=_gate_mix_grad_tiles,
        max_rows=256)
    g_sb_o = _mm(a_sb, d_ysb, mode="tn", out_dtype=BF16, name="g_sb_o")
    g_ret_o = _mm(y_gated, d_yret, mode="tn", out_dtype=BF16, name="g_ret_o")
    d_asb = _mm(d_ysb, w["w_sb_o"], mode="nt", out_dtype=BF16, name="d_asb")
    d_ygated = _mm(d_yret, w["w_ret_o"], mode="nt", out_dtype=F32, name="d_ygated")
    small_grads = {"b_gate": d_b_gate, "ln1_g": d_ln1_g, "ln1_b": d_ln1_b, "ln2_g": d_ln2_g, "ln2_b": d_ln2_b,
                   "ln3_g": d_ln3_g, "ln3_b": d_ln3_b, "loss_cols": loss_cols}
    d_rq, d_rk, d_c, d_d = ship({"w_mem_kv": g_mem_kv, "w_mem_q": g_mem_q, "w_mem_o": g_mem_o, "w_mix_o": g_mix_o},
                                lambda rider: _ret_bwd(d_ygated, o_pre, states, h_b, h_c, h_d, tables, rider))
    d_q, d_k, d_v = ship({"w_ffn_in": g_ffn_in, "w_ret_o": g_ret_o, "w_sb_o": g_sb_o, "small": small_grads},
                         lambda rider: _sb_bwd(h_a, d_asb, r_mat, rider))
    d_h = [("sb_q", d_q), ("sb_k", d_k), ("sb_v", d_v), ("ret_q", d_rq), ("ret_k", d_rk), ("ret_v", d_c),
           ("ret_g", d_d), ("gate", d_e)]
    pieces = [piece for _, piece in d_h]
    half = D_MODEL // 2
    g_top, _ = _mm_pieces_tn(xb, 0, half, pieces, "g_in_top")
    (g_bottom,) = ship({"w_in@top": g_top}, lambda rider: _mm_pieces_tn(xb, 1, half, pieces, "g_in_bottom", rider))
    (d_x,) = ship({"w_in@bottom": g_bottom},
                  lambda rider: _as_host(rider, _mm_fused(
                      [piece for _, piece in d_h], w_in, mode="nt", name="d_x", extras=[du1], outs=[(D_MODEL, F32)],
                      epilogue=_residual_tiles, rider=rider, max_rows=256)))
    return d_x


def _adamw_math(w, g, m, v):
    m = ADAM_B1 * m + (1.0 - ADAM_B1) * g
    v = ADAM_B2 * v + (1.0 - ADAM_B2) * jnp.square(g)
    m_hat = m / (1.0 - ADAM_B1 ** ADAM_STEP)
    v_hat = v / (1.0 - ADAM_B2 ** ADAM_STEP)
    delta = -ADAM_LR * (m_hat / (jnp.sqrt(v_hat) + ADAM_EPS) + ADAM_WD * w)
    return delta, m, v


def _adamw(parts, w, m, v, name):
    R, C = w.shape
    tr = max(t for t in range(16, min(R, 256) + 1, 16) if R % t == 0) if R >= 16 else R

    def body(p_ref, w_ref, m_ref, v_ref, g_ref, d_ref, nm_ref, nv_ref):
        g = p_ref[0].astype(F32)
        for j in range(1, N_DEV):
            g = g + p_ref[j].astype(F32)
        delta, nm, nv = _adamw_math(w_ref[...], g, m_ref[...], v_ref[...])
        g_ref[...] = g
        d_ref[...] = delta
        nm_ref[...] = nm
        nv_ref[...] = nv

    blk = pl.BlockSpec((tr, C), lambda i: (i, 0))
    out = jax.ShapeDtypeStruct((R, C), F32)
    return pl.pallas_call(
        body,
        name=name,
        grid=(R // tr,),
        in_specs=[pl.BlockSpec((N_DEV, tr, C), lambda i: (0, i, 0)), blk, blk, blk],
        out_specs=[blk] * 4,
        out_shape=[out] * 4,
        compiler_params=_params(("parallel",)),
    )(parts, w, m, v)


_SHARD_AXIS = {"w_in": 1, "w_sb_o": 1, "w_ret_o": 0, "w_mix_o": 0, "w_mem_q": 0, "w_mem_kv": 1, "w_mem_o": 0,
               "w_ffn_in": 1, "w_ffn_out": 0}
_MATRICES = tuple(_SHARD_AXIS)
_SMALL = ("b_gate", "ln1_g", "ln1_b", "ln2_g", "ln2_b", "ln3_g", "ln3_b")
_WEIGHT_ORDER = ("w_in", "b_gate", "w_sb_o", "w_ret_o", "w_mix_o", "ln1_g", "ln1_b", "w_mem_q", "w_mem_kv", "w_mem_o",
                 "ln2_g", "ln2_b", "w_ffn_in", "w_ffn_out", "ln3_g", "ln3_b")


def _assemble(name, gathered):
    if _SHARD_AXIS[name] == 0:
        return gathered.reshape(-1, gathered.shape[2])
    return jnp.transpose(gathered, (1, 0, 2)).reshape(gathered.shape[1], -1)


def _to_slots(name, full):
    if _SHARD_AXIS[name] == 0:
        return full.reshape(N_DEV, full.shape[0] // N_DEV, full.shape[1])
    return jnp.transpose(full.reshape(full.shape[0], N_DEV, full.shape[1] // N_DEV), (1, 0, 2))


SMALL_ROWS = 16


def _pack_small(vals):
    return jnp.concatenate([vals["b_gate"].reshape(2, D_MODEL)] + [vals[n] for n in _SMALL[1:]], axis=0)


def _unpack_small(packed):
    out = {"b_gate": packed[0:2].reshape(1, 2 * D_MODEL)}
    for i, n in enumerate(_SMALL[1:]):
        out[n] = packed[2 + i:3 + i]
    return out


def kernel(x, mem, w_in, b_gate, w_sb_o, w_ret_o, w_mix_o, ln1_g, ln1_b, w_mem_q, w_mem_kv, w_mem_o, ln2_g, ln2_b, w_ffn_in, w_ffn_out, ln3_g, ln3_b, loss_target, m_w_in, m_b_gate, m_w_sb_o, m_w_ret_o, m_w_mix_o, m_ln1_g, m_ln1_b, m_w_mem_q, m_w_mem_kv, m_w_mem_o, m_ln2_g, m_ln2_b, m_w_ffn_in, m_w_ffn_out, m_ln3_g, m_ln3_b, v_w_in, v_b_gate, v_w_sb_o, v_w_ret_o, v_w_mix_o, v_ln1_g, v_ln1_b, v_w_mem_q, v_w_mem_kv, v_w_mem_o, v_ln2_g, v_ln2_b, v_w_ffn_in, v_w_ffn_out, v_ln3_g, v_ln3_b):
    weights = dict(w_in=w_in, b_gate=b_gate, w_sb_o=w_sb_o, w_ret_o=w_ret_o, w_mix_o=w_mix_o, ln1_g=ln1_g, ln1_b=ln1_b,
                   w_mem_q=w_mem_q, w_mem_kv=w_mem_kv, w_mem_o=w_mem_o, ln2_g=ln2_g, ln2_b=ln2_b, w_ffn_in=w_ffn_in,
                   w_ffn_out=w_ffn_out, ln3_g=ln3_g, ln3_b=ln3_b)
    mom1 = dict(w_in=m_w_in, b_gate=m_b_gate, w_sb_o=m_w_sb_o, w_ret_o=m_w_ret_o, w_mix_o=m_w_mix_o, ln1_g=m_ln1_g,
                ln1_b=m_ln1_b, w_mem_q=m_w_mem_q, w_mem_kv=m_w_mem_kv, w_mem_o=m_w_mem_o, ln2_g=m_ln2_g, ln2_b=m_ln2_b,
                w_ffn_in=m_w_ffn_in, w_ffn_out=m_w_ffn_out, ln3_g=m_ln3_g, ln3_b=m_ln3_b)
    mom2 = dict(w_in=v_w_in, b_gate=v_b_gate, w_sb_o=v_w_sb_o, w_ret_o=v_w_ret_o, w_mix_o=v_w_mix_o, ln1_g=v_ln1_g,
                ln1_b=v_ln1_b, w_mem_q=v_w_mem_q, w_mem_kv=v_w_mem_kv, w_mem_o=v_w_mem_o, ln2_g=v_ln2_g, ln2_b=v_ln2_b,
                w_ffn_in=v_w_ffn_in, w_ffn_out=v_w_ffn_out, ln3_g=v_ln3_g, ln3_b=v_ln3_b)

    (gathered_in,) = _exchange([weights["w_in"][0].astype(BF16)], False, "gather_w_in")
    received = {}

    def fetch(names, host):
        res = host(_Rider([weights[n][0].astype(BF16) for n in names], False))
        return res[:-1], {n: _assemble(n, g) for n, g in zip(names, res[-1])}

    def ship(grads, host):
        names = list(grads)
        bufs = []
        for n in names:
            if n == "small":
                part = jnp.concatenate([_pack_small(grads[n]), grads[n]["loss_cols"],
                                        jnp.zeros((SMALL_ROWS - 9, D_MODEL), F32)], axis=0)
                bufs.append(jnp.broadcast_to(part[None], (N_DEV,) + part.shape))
            else:
                bufs.append(_to_slots(n.split("@")[0], grads[n]).astype(BF16))
        res = host(_Rider(bufs, True))
        received.update(zip(names, res[-1]))
        return res[:-1]

    small = {n: weights[n] for n in _SMALL}
    d_x = _local_step(x[0], mem[0], _assemble("w_in", gathered_in), small, loss_target[0], fetch, ship)

    received["w_in"] = jnp.concatenate([received.pop("w_in@top"), received.pop("w_in@bottom")], axis=1)
    new = {}
    for n in _MATRICES:
        new[n] = _adamw(received[n], weights[n][0], mom1[n][0], mom2[n][0], "adamw_" + n)
    packed = _adamw(received["small"][:, :8], _pack_small({n: weights[n] for n in _SMALL}),
                    _pack_small({n: mom1[n] for n in _SMALL}), _pack_small({n: mom2[n] for n in _SMALL}), "adamw_small")
    small_new = [_unpack_small(p) for p in packed]
    loss = jnp.sum(received["small"][:, 8]) * (0.5 / D_MODEL)

    outs = [loss, d_x[None]]
    for slot in range(4):
        for n in _WEIGHT_ORDER:
            outs.append(new[n][slot][None] if n in new else small_new[slot][n])
    return tuple(outs)
```

```python
import functools
import math

import jax
import jax.numpy as jnp
from jax import lax
from jax.experimental import pallas as pl
from jax.experimental.pallas import tpu as pltpu

F32 = jnp.float32
BF16 = jnp.bfloat16

N_DEV = 8
D_MODEL = 1024
SB_HEAD_DIM = 64
SB_WIDTH = 512
RET_HEADS = 4
RET_QK_DIM = 128
RET_V_DIM = 256
RET_QK_WIDTH = 512
RET_V_WIDTH = 1024
RET_CHUNK = 128
RET_STEP_CHUNKS = 4
ROPE_BASE = 10000.0
MEM_HEADS = 4
MEM_HEAD_DIM = 256
FFN_HIDDEN = 2816
DN_ALPHA = 2.0 ** 0.25
LN_EPS = 1e-5
ADAM_LR = 0.001
ADAM_B1 = 0.9
ADAM_B2 = 0.999
ADAM_EPS = 1e-08
ADAM_WD = 0.01
ADAM_STEP = 10

VMEM_LIMIT_BYTES = 52 * 1024 * 1024
LANES = 128
SB_KEY_BLOCK = 128
SB_Q_BLOCK = 256
SB_DEAD_LOG = -105.0

MESH_AXES = ("x", "y", "c")


def _pick(dim, prefs):
    for p in prefs:
        if dim % p == 0:
            return p
    return dim


def _params(sem):
    return pltpu.CompilerParams(dimension_semantics=sem, vmem_limit_bytes=VMEM_LIMIT_BYTES)


def _dot(a, b, dims):
    return lax.dot_general(a, b, (dims, ((), ())), preferred_element_type=F32)


_NN = ((1,), (0,))
_NT = ((1,), (1,))
_TN = ((0,), (0,))


def _my_index():
    return 4 * lax.axis_index("x") + 2 * lax.axis_index("y") + lax.axis_index("c")


def _peer(k):
    x, y, c = lax.axis_index("x"), lax.axis_index("y"), lax.axis_index("c")
    bx, by, bc = (k >> 2) & 1, (k >> 1) & 1, k & 1
    px = (1 - x) if bx else x
    py = (1 - y) if by else y
    pc = (1 - c) if bc else c
    return (px, py, pc), 4 * px + 2 * py + pc


class _Rider:
    def __init__(self, bufs, scatter):
        self.bufs, self.scatter, self.n = list(bufs), scatter, len(bufs)
        self.specs = [pl.BlockSpec(memory_space=pl.ANY)] * self.n
        self.out_shape = [jax.ShapeDtypeStruct(b.shape if scatter else (N_DEV,) + b.shape, b.dtype) for b in self.bufs]
        self.scratch = [pltpu.SemaphoreType.DMA((self.n, N_DEV - 1)), pltpu.SemaphoreType.DMA((self.n, N_DEV - 1)),
                        pltpu.SemaphoreType.DMA((self.n,))]

    def _remote(self, ride, a, k, src_ref, slot, to):
        _, dst, (send_sems, recv_sems, _) = ride
        return pltpu.make_async_remote_copy(src_ref=src_ref, dst_ref=dst[a].at[slot], send_sem=send_sems.at[a, k],
                                            recv_sem=recv_sems.at[a, k], device_id=to,
                                            device_id_type=pl.DeviceIdType.MESH)

    def _local(self, ride, a):
        src, dst, (_, _, local_sems) = ride
        me = _my_index()
        return pltpu.make_async_copy(src[a].at[me] if self.scatter else src[a], dst[a].at[me], local_sems.at[a])

    def _direct(self, ride, a):
        src = ride[0]
        me = _my_index()
        out = []
        for k in range(1, N_DEV):
            peer, peer_idx = _peer(k)
            out.append(self._remote(ride, a, k - 1, src[a].at[peer_idx], me, peer))
        return out

    def _two_level(self, ride, a):
        src, dst = ride[0], ride[1]
        x, y, c = lax.axis_index("x"), lax.axis_index("y"), lax.axis_index("c")
        me, sibling = _my_index(), (x, y, 1 - c)
        chips = [(1 - x, y), (x, 1 - y), (1 - x, 1 - y)]
        first = [self._remote(ride, a, 0, src[a], me, sibling)]
        passed, landing = [], [self._remote(ride, a, 0, src[a], me + 1 - 2 * c, sibling)]
        for j, (px, py) in enumerate(chips):
            first.append(self._remote(ride, a, 1 + j, src[a], me, (px, py, c)))
            theirs = 4 * px + 2 * py + c
            passed.append(self._remote(ride, a, 4 + j, dst[a].at[theirs], theirs, sibling))
            landing.append(self._remote(ride, a, 1 + j, src[a], theirs, (px, py, c)))
        for j, (px, py) in enumerate(chips):
            landing.append(self._remote(ride, a, 4 + j, src[a], 4 * px + 2 * py + 1 - c, sibling))
        return first, passed, landing

    def start(self, ride):
        for a in range(self.n):
            self._local(ride, a).start()
            for cp in (self._direct(ride, a) if self.scatter else self._two_level(ride, a)[0]):
                cp.start()

    def finish(self, ride):
        if self.scatter:
            for a in range(self.n):
                for cp in self._direct(ride, a):
                    cp.wait()
                self._local(ride, a).wait()
            return
        levels = [self._two_level(ride, a) for a in range(self.n)]
        for first, passed, landing in levels:
            for j, cp in enumerate(passed):
                landing[1 + j].wait_recv()
                cp.start()
        for a, (first, passed, landing) in enumerate(levels):
            landing[0].wait_recv()
            for cp in landing[4:]:
                cp.wait_recv()
            for cp in first + passed:
                cp.wait_send()
            self._local(ride, a).wait()

    def start_at_first(self, ids, ride):
        first = functools.reduce(jnp.logical_and, [i == 0 for i in ids])

        @pl.when(first)
        def _():
            self.start(ride)

    def wait_at_last(self, ids, grid, ride):
        last = functools.reduce(jnp.logical_and, [i == g - 1 for i, g in zip(ids, grid)])

        @pl.when(last)
        def _():
            self.finish(ride)


def _exchange(bufs, scatter, name):
    rider = _Rider(bufs, scatter)

    def body(*refs):
        ride = (refs[:rider.n], refs[rider.n:2 * rider.n], refs[2 * rider.n:])
        rider.start(ride)
        rider.finish(ride)

    return pl.pallas_call(
        body,
        name=name,
        in_specs=rider.specs,
        out_specs=rider.specs,
        out_shape=rider.out_shape,
        scratch_shapes=rider.scratch,
    )(*rider.bufs)


MM_RESIDENT_B_BYTES = 14 * 1024 * 1024
MM_A_TILE_BYTES = 4 * 1024 * 1024
MM_OUT_TILE_BYTES = 6 * 1024 * 1024


def _mm_tiles(mode, M, N, K, a_bytes, out_bytes):
    if mode != "tn" and K * N * 2 <= MM_RESIDENT_B_BYTES:
        for tm in (1024, 512, 256, 128):
            if M % tm == 0 and tm * K * a_bytes <= MM_A_TILE_BYTES and tm * N * out_bytes <= MM_OUT_TILE_BYTES:
                return tm, N, K
    if mode == "tn":
        return (_pick(M, (1024, 1408, 512, 256, 128)), _pick(N, (1024, 1664, 1408, 512, 256, 128)),
                _pick(K, (2048, 1024, 512, 256, 128)))
    return _pick(M, (1024, 512, 256, 128)), _pick(N, (512, 256, 128)), _pick(K, (1024, 512, 256, 128))


def _mm(a, b, *, mode, out_dtype, name, res=None, res_scale=1.0, rider=None):
    if mode == "nn":
        (M, K), (K2, N) = a.shape, b.shape
    elif mode == "nt":
        (M, K), (N, K2) = a.shape, b.shape
    else:
        (K, M), (K2, N) = a.shape, b.shape
    assert K == K2, (a.shape, b.shape, mode)
    out_bytes = jnp.dtype(out_dtype).itemsize + (4 if res is not None else 0)
    tm, tn, tk = _mm_tiles(mode, M, N, K, a.dtype.itemsize, out_bytes)
    grid = (M // tm, N // tn, K // tk)
    nk = grid[2]
    dims = {"nn": _NN, "nt": _NT, "tn": _TN}[mode]
    n_in = 2 + (res is not None)
    n_ride = rider.n if rider is not None else 0

    def body(*refs):
        a_ref, b_ref = refs[:2]
        r_ref = refs[2] if res is not None else None
        o_ref = refs[n_in + n_ride]
        rest = refs[n_in + 2 * n_ride + 1:]
        acc_ref = rest[0] if nk > 1 else None
        ids = [pl.program_id(d) for d in range(3)]
        if rider is not None:
            ride = (refs[n_in:n_in + n_ride], refs[n_in + n_ride + 1:n_in + 2 * n_ride + 1], rest[-3:])
            rider.start_at_first(ids, ride)
        part = _dot(a_ref[...].astype(BF16), b_ref[...].astype(BF16), dims)

        def finish(total):
            if r_ref is not None:
                total = total + res_scale * r_ref[...]
            o_ref[...] = total.astype(out_dtype)

        if nk == 1:
            finish(part)
        else:
            k = ids[2]

            @pl.when(k == 0)
            def _():
                acc_ref[...] = part

            @pl.when(k > 0)
            def _():
                acc_ref[...] += part

            @pl.when(k == nk - 1)
            def _():
                finish(acc_ref[...])

        if rider is not None:
            rider.wait_at_last(ids, grid, ride)

    if mode == "nn":
        a_spec = pl.BlockSpec((tm, tk), lambda i, j, k: (i, k))
        b_spec = pl.BlockSpec((tk, tn), lambda i, j, k: (k, j))
    elif mode == "nt":
        a_spec = pl.BlockSpec((tm, tk), lambda i, j, k: (i, k))
        b_spec = pl.BlockSpec((tn, tk), lambda i, j, k: (j, k))
    else:
        a_spec = pl.BlockSpec((tk, tm), lambda i, j, k: (k, i))
        b_spec = pl.BlockSpec((tk, tn), lambda i, j, k: (k, j))
    o_spec = pl.BlockSpec((tm, tn), lambda i, j, k: (i, j))
    in_specs = [a_spec, b_spec] + ([o_spec] if res is not None else [])
    args = (a, b) + ((res,) if res is not None else ())
    out_specs, out_shape = [o_spec], [jax.ShapeDtypeStruct((M, N), out_dtype)]
    scratch = [pltpu.VMEM((tm, tn), F32)] if nk > 1 else []
    sem = ("parallel", "parallel", "arbitrary")
    if rider is not None:
        in_specs, args = in_specs + rider.specs, args + tuple(rider.bufs)
        out_specs, out_shape = out_specs + rider.specs, out_shape + rider.out_shape
        scratch = scratch + rider.scratch
        sem = ("arbitrary",) * 3
    outs = pl.pallas_call(
        body,
        name=name,
        grid=grid,
        in_specs=in_specs,
        out_specs=out_specs,
        out_shape=out_shape,
        scratch_shapes=scratch,
        compiler_params=_params(sem),
    )(*args)
    return outs[0] if rider is None else (outs[0], list(outs[1:]))


def _mm_host(a, b, *, rider, **kw):
    out = _mm(a, b, rider=rider, **kw)
    return out if rider is not None else (out, [])


def _as_host(rider, results):
    return results if rider is not None else tuple(results) + ([],)


MM_FUSED_MARGIN_BYTES = 10 * 1024 * 1024
MM_FUSED_MAX_ROWS = 512


def _col_sum_update(acc_ref, val, first):
    part = jnp.sum(val.reshape(val.shape[0] // 8, 8, val.shape[1]), axis=0)

    @pl.when(first)
    def _():
        acc_ref[...] = part

    @pl.when(jnp.logical_not(first))
    def _():
        acc_ref[...] += part


def _mm_fused(a, b, *, mode, name, extras, outs, epilogue, sums=(), rider=None, max_rows=MM_FUSED_MAX_ROWS,
              pass_a=False):
    parts = list(a) if isinstance(a, (list, tuple)) else [a]
    M, K = parts[0].shape[0], sum(p.shape[1] for p in parts)
    if mode == "nn":
        (K2, N), b_dims = b.shape, _NN
    else:
        (N, K2), b_dims = b.shape, _NT
    assert K == K2, (K, b.shape, mode)
    rows = parts + [e for e in extras if e.shape[0] == M]
    per_row = 2 * (sum(e.shape[1] * e.dtype.itemsize for e in rows)
                   + sum(c * jnp.dtype(d).itemsize for c, d in outs)) + 2 * N * 4
    budget = VMEM_LIMIT_BYTES - K * N * 2 - MM_FUSED_MARGIN_BYTES
    tm = next(t for t in (512, 256, 128, 64, 32, 16) if t <= max_rows and M % t == 0 and t * per_row <= budget)
    steps = M // tm
    n_a, n_x, n_o, n_s = len(parts), len(extras), len(outs), len(sums)
    n_ride = rider.n if rider is not None else 0

    def body(*refs):
        a_refs, b_ref = refs[:n_a], refs[n_a]
        x_refs = refs[n_a + 1:n_a + 1 + n_x]
        base = n_a + 1 + n_x + n_ride
        o_refs, s_refs = refs[base:base + n_o], refs[base + n_o:base + n_o + n_s]
        acc_refs = refs[base + n_o + n_s + n_ride:base + n_o + 2 * n_s + n_ride]
        ids = [pl.program_id(0)]
        if rider is not None:
            ride = (refs[n_a + 1 + n_x:base], refs[base + n_o + n_s:base + n_o + n_s + n_ride], refs[-3:])
            rider.start_at_first(ids, ride)
        a_tile = a_refs[0][...]
        a_bf16 = a_tile.astype(BF16) if n_a == 1 else jnp.concatenate([r[...].astype(BF16) for r in a_refs], axis=1)
        prod = _dot(a_bf16, b_ref[...], b_dims)
        tiles = epilogue(prod, *([a_tile] if pass_a else []), *[r[...] for r in x_refs])
        for o_ref, t in zip(o_refs, tiles[:n_o]):
            o_ref[...] = t.astype(o_ref.dtype)
        for acc_ref, t in zip(acc_refs, tiles[n_o:]):
            _col_sum_update(acc_ref, t, ids[0] == 0)
        if n_s:
            @pl.when(ids[0] == steps - 1)
            def _():
                for s_ref, acc_ref in zip(s_refs, acc_refs):
                    s_ref[...] = jnp.sum(acc_ref[...], axis=0, keepdims=True)
        if rider is not None:
            rider.wait_at_last(ids, (steps,), ride)

    in_specs = [pl.BlockSpec((tm, p.shape[1]), lambda i: (i, 0)) for p in parts]
    in_specs.append(pl.BlockSpec(b.shape, lambda i: (0, 0), pipeline_mode=pl.Buffered(1)))
    for e in extras:
        in_specs.append(pl.BlockSpec((tm, e.shape[1]), lambda i: (i, 0)) if e.shape[0] == M
                        else pl.BlockSpec(e.shape, lambda i: (0, 0)))
    out_specs = ([pl.BlockSpec((tm, c), lambda i: (i, 0)) for c, _ in outs]
                 + [pl.BlockSpec((1, c), lambda i: (0, 0)) for c in sums])
    out_shape = ([jax.ShapeDtypeStruct((M, c), d) for c, d in outs]
                 + [jax.ShapeDtypeStruct((1, c), F32) for c in sums])
    args = tuple(parts) + (b,) + tuple(extras)
    scratch = [pltpu.VMEM((8, c), F32) for c in sums]
    if rider is not None:
        in_specs, args = in_specs + rider.specs, args + tuple(rider.bufs)
        out_specs, out_shape = out_specs + rider.specs, out_shape + rider.out_shape
        scratch = scratch + rider.scratch
    res = pl.pallas_call(
        body,
        name=name,
        grid=(steps,),
        in_specs=in_specs,
        out_specs=out_specs,
        out_shape=out_shape,
        scratch_shapes=scratch,
        compiler_params=_params(("arbitrary",) if (n_s or rider is not None) else ("parallel",)),
    )(*args)
    return tuple(res[:n_o + n_s]) + ((list(res[n_o + n_s:]),) if rider is not None else ())


def _pair_rows(blk, lane_is_a):
    zero = jnp.zeros_like(blk)
    return jnp.concatenate([jnp.where(lane_is_a, blk, zero), jnp.where(lane_is_a, zero, blk)], axis=0)


SB_STRIP = 32
SB_FWD_PAIRS = 4
SB_BWD_PAIRS = 2
SB_GROUP = 2


def _pair_lanes(p):
    return slice(p * LANES, (p + 1) * LANES)


def _sb_scan_matrices():
    o = lax.broadcasted_iota(jnp.int32, (2 * LANES, 4 * LANES), 0)
    c = lax.broadcasted_iota(jnp.int32, (2 * LANES, 4 * LANES), 1) & (2 * LANES - 1)
    same = (o >= LANES) == (c >= LANES)
    oo, cc = o & (LANES - 1), c & (LANES - 1)
    return (jnp.where(same & (cc > oo), 1.0, 0.0).astype(BF16), jnp.where(same & (cc < oo), 1.0, 0.0).astype(BF16))


def _sb_causal_masks(tq):
    d = lax.broadcasted_iota(jnp.int32, (tq // SB_KEY_BLOCK, SB_KEY_BLOCK, tq), 0)
    k = lax.broadcasted_iota(jnp.int32, (tq // SB_KEY_BLOCK, SB_KEY_BLOCK, tq), 1)
    t = lax.broadcasted_iota(jnp.int32, (tq // SB_KEY_BLOCK, SB_KEY_BLOCK, tq), 2)
    return jnp.where(d * SB_KEY_BLOCK + k < t, 1.0, 0.0).astype(F32)


def _sb_log_terms(z):
    log_rem = -jnp.maximum(z, 0.0) - jnp.log(1.0 + jnp.exp(-jnp.abs(z)))
    return log_rem, log_rem + z


def _sb_store_split(ref, strip, val, cols):
    hi = val.astype(BF16)
    ref[pl.ds(strip * SB_STRIP, SB_STRIP), cols] = hi
    ref[pl.ds(2 * LANES + strip * SB_STRIP, SB_STRIP), cols] = (val - hi.astype(F32)).astype(BF16)


def _sb_lanes(tq, diag):
    if diag == "left":
        return 0, tq // 2
    first = 0 if diag is None else diag * SB_KEY_BLOCK
    return first, tq - first


def _lane_add(full, part, lanes):
    first, width = lanes
    pieces = [full[:, :first]] if first else []
    pieces.append(full[:, first:first + width] + part)
    if first + width < full.shape[1]:
        pieces.append(full[:, first + width:])
    return pieces[0] if len(pieces) == 1 else jnp.concatenate(pieces, axis=1)


def _sb_fwd(h_a, rider=None):
    assert SB_FWD_PAIRS == 4
    T = h_a.shape[0]
    tq = _pick(T, (SB_Q_BLOCK, SB_KEY_BLOCK))
    nq, per_q, nkb = T // tq, tq // SB_KEY_BLOCK, T // SB_KEY_BLOCK
    assert per_q % SB_GROUP == 0
    n_strips = 2 * LANES // SB_STRIP
    n_ride = rider.n if rider is not None else 0
    after_m, _ = _sb_scan_matrices()
    causal_m = _sb_causal_masks(tq)
    pairs = SB_FWD_PAIRS

    def body(*refs):
        q_ref, k_ref, v_ref, after_ref, causal_ref = refs[:5]
        a_ref, r_ref, n_ref = refs[5 + n_ride:8 + n_ride]
        z_ref, lb_ref, split_ref, w_ref = refs[8 + 2 * n_ride:12 + 2 * n_ride]
        ids = [pl.program_id(0)]
        if rider is not None:
            ride = (refs[5:5 + n_ride], refs[8 + n_ride:8 + 2 * n_ride], refs[-3:])
            rider.start_at_first(ids, ride)
        i = ids[0]
        q_t = [(q_ref[:, _pair_lanes(p)].astype(F32).T * (SB_HEAD_DIM ** -0.5)).astype(BF16) for p in range(pairs)]
        lane_is_a = lax.broadcasted_iota(jnp.int32, (SB_KEY_BLOCK, LANES), 1) < SB_HEAD_DIM

        def tiles(kbs, diags, carry):
            nb = len(kbs)
            lanes = [_sb_lanes(tq, d) for d in diags]
            cols = [slice(first, first + width) for first, width in lanes]
            acc_t, ra, rb = [list(c) for c in carry]
            ks = [pl.multiple_of(kb * SB_KEY_BLOCK, SB_KEY_BLOCK) for kb in kbs]
            slot = lambda p, b: p * nb + b

            def causal(b, s):
                return causal_ref[diags[b], pl.ds((s * SB_STRIP) % SB_KEY_BLOCK, SB_STRIP), cols[b]]

            vv = {}
            for b in range(nb):
                for p in range(pairs):
                    kk = _pair_rows(k_ref[pl.ds(ks[b], SB_KEY_BLOCK), _pair_lanes(p)], lane_is_a)
                    vv[p, b] = _pair_rows(v_ref[pl.ds(ks[b], SB_KEY_BLOCK), _pair_lanes(p)], lane_is_a)
                    z_ref[slot(p, b), :, cols[b]] = _dot(kk, q_t[p][:, cols[b]], _NN)
            sums = {}
            for b in range(nb):
                for p in range(pairs):
                    part = [jnp.zeros((8, lanes[b][1]), F32), jnp.zeros((8, lanes[b][1]), F32)]
                    for s in range(n_strips):
                        rows = pl.ds(s * SB_STRIP, SB_STRIP)
                        log_rem, log_beta = _sb_log_terms(z_ref[slot(p, b), rows, cols[b]])
                        lb_ref[slot(p, b), rows, cols[b]] = log_beta
                        if isinstance(diags[b], int):
                            log_rem = log_rem * causal(b, s)
                        _sb_store_split(split_ref.at[slot(p, b)], s, log_rem, cols[b])
                        head = (s * SB_STRIP) // SB_KEY_BLOCK
                        part[head] = part[head] + jnp.sum(log_rem.reshape(SB_STRIP // 8, 8, lanes[b][1]), axis=0)
                    sums[p, b] = part
            for b in range(nb):
                for p in range(pairs):
                    z_ref[slot(p, b), :, cols[b]] = _dot(after_ref[...], split_ref[slot(p, b), :, cols[b]], _NN)
            for b in range(nb):
                for p in range(pairs):
                    for s in range(n_strips):
                        rows = pl.ds(s * SB_STRIP, SB_STRIP)
                        start = (ra[p] if (s * SB_STRIP) < SB_KEY_BLOCK else rb[p])[:, cols[b]]
                        w = jnp.exp(lb_ref[slot(p, b), rows, cols[b]] + z_ref[slot(p, b), rows, cols[b]] + start)
                        if isinstance(diags[b], int):
                            w = w * causal(b, s)
                        w_ref[slot(p, b), rows, cols[b]] = w.astype(BF16)
                    r_ref[2 * p, kbs[b]] = ra[p]
                    r_ref[2 * p + 1, kbs[b]] = rb[p]
                    ra[p] = _lane_add(ra[p], jnp.sum(sums[p, b][0], axis=0, keepdims=True), lanes[b])
                    rb[p] = _lane_add(rb[p], jnp.sum(sums[p, b][1], axis=0, keepdims=True), lanes[b])
            for b in range(nb):
                for p in range(pairs):
                    acc_t[p] = _lane_add(acc_t[p], _dot(vv[p, b], w_ref[slot(p, b), :, cols[b]], _TN), lanes[b])
            return tuple(acc_t), tuple(ra), tuple(rb)

        carry = (tuple(jnp.zeros((LANES, tq), F32) for _ in range(pairs)),
                 tuple(jnp.zeros((1, tq), F32) for _ in range(pairs)),
                 tuple(jnp.zeros((1, tq), F32) for _ in range(pairs)))
        own = list(reversed(range(per_q)))
        n_full = i * per_q
        carry = lax.cond(
            i > 0,
            lambda cc: tiles([n_full + d for d in own] + [n_full - 1 - b for b in range(SB_GROUP)],
                             own + [None] * SB_GROUP, cc),
            lambda cc: tiles([n_full + d for d in own], own, cc), carry)
        first_walked = jnp.where(i > 0, SB_GROUP, 0).astype(jnp.int32)

        def top_of(sums_a, sums_b, first):
            return jnp.max(functools.reduce(jnp.maximum, [r[:, first:] for r in sums_a + sums_b]))

        def alive(c):
            return jnp.logical_and(c[0] < n_full, top_of(c[2], c[3], 0) > SB_DEAD_LOG)

        def step(c):
            kbs = [n_full - 1 - c[0] - b for b in range(SB_GROUP)]
            return (c[0] + SB_GROUP,) + lax.cond(
                top_of(c[2], c[3], tq // 2) > SB_DEAD_LOG,
                lambda cc: tiles(kbs, [None] * SB_GROUP, cc), lambda cc: tiles(kbs, ["left"] * SB_GROUP, cc), c[1:])

        walked, acc_t, _, _ = lax.while_loop(alive, step, (first_walked,) + carry)
        for p in range(pairs):
            a_ref[:, _pair_lanes(p)] = acc_t[p].T.astype(BF16)
        n_ref[...] = jnp.zeros(n_ref.shape, F32) + walked.astype(F32)
        if rider is not None:
            rider.wait_at_last(ids, (nq,), ride)

    wide = pairs * LANES
    in_specs = [pl.BlockSpec((tq, wide), lambda i: (i, 0)),
                pl.BlockSpec((T, wide), lambda i: (0, 1), pipeline_mode=pl.Buffered(1)),
                pl.BlockSpec((T, wide), lambda i: (0, 2), pipeline_mode=pl.Buffered(1)),
                pl.BlockSpec(after_m.shape, lambda i: (0, 0), pipeline_mode=pl.Buffered(1)),
                pl.BlockSpec(causal_m.shape, lambda i: (0, 0, 0), pipeline_mode=pl.Buffered(1))]
    out_specs = [pl.BlockSpec((tq, wide), lambda i: (i, 0)),
                 pl.BlockSpec((2 * pairs, nkb, 1, tq), lambda i: (0, 0, 0, i)),
                 pl.BlockSpec((1, 8, LANES), lambda i: (i, 0, 0))]
    out_shape = [jax.ShapeDtypeStruct((T, SB_WIDTH), BF16), jax.ShapeDtypeStruct((2 * pairs, nkb, 1, T), F32),
                 jax.ShapeDtypeStruct((nq, 8, LANES), F32)]
    args = (h_a, h_a, h_a, after_m, causal_m)
    slots = pairs * (per_q + SB_GROUP)
    scratch = [pltpu.VMEM((slots, 2 * LANES, tq), F32), pltpu.VMEM((slots, 2 * LANES, tq), F32),
               pltpu.VMEM((slots, 4 * LANES, tq), BF16), pltpu.VMEM((slots, 2 * LANES, tq), BF16)]
    if rider is not None:
        in_specs, args = in_specs + rider.specs, args + tuple(rider.bufs)
        out_specs, out_shape = out_specs + rider.specs, out_shape + rider.out_shape
        scratch = scratch + rider.scratch
    outs = pl.pallas_call(
        body,
        name="sb_fwd",
        grid=(nq,),
        in_specs=in_specs,
        out_specs=out_specs,
        out_shape=out_shape,
        scratch_shapes=scratch,
        compiler_params=_params(("arbitrary",)),
    )(*args)
    return outs[0], (outs[1], outs[2]), list(outs[3:])


def _sb_bwd(h_a, d_out, saved, rider=None):
    r_mat, walked_blocks = saved
    T = h_a.shape[0]
    tq = _pick(T, (SB_Q_BLOCK, SB_KEY_BLOCK))
    nq, per_q, nkb = T // tq, tq // SB_KEY_BLOCK, T // SB_KEY_BLOCK
    n_strips = 2 * LANES // SB_STRIP
    after_m, before_m = _sb_scan_matrices()
    causal_m = _sb_causal_masks(tq)
    pairs = SB_BWD_PAIRS
    groups = 4 // pairs
    n_ride = rider.n if rider is not None else 0

    def body(*refs):
        q_ref, k_ref, v_ref, do_ref, r_ref, n_ref, after_ref, before_ref, causal_ref = refs[:9]
        dq_ref, dk_ref, dv_ref = refs[9 + n_ride:12 + n_ride]
        z_ref, lb_ref, split_ref, w_ref, da_ref, dz_ref = refs[12 + 2 * n_ride:18 + 2 * n_ride]
        ids = [pl.program_id(0), pl.program_id(1)]
        if rider is not None:
            ride = (refs[9:9 + n_ride], refs[12 + n_ride:12 + 2 * n_ride], refs[-3:])
            rider.start_at_first(ids, ride)
        i = ids[1]

        @pl.when(i == 0)
        def _():
            dk_ref[...] = jnp.zeros_like(dk_ref)
            dv_ref[...] = jnp.zeros_like(dv_ref)

        scale = SB_HEAD_DIM ** -0.5
        q = [q_ref[:, _pair_lanes(p)] for p in range(pairs)]
        d_o = [do_ref[:, _pair_lanes(p)] for p in range(pairs)]
        q_t = [(x.astype(F32).T * scale).astype(BF16) for x in q]
        do_t = [x.astype(F32).T.astype(BF16) for x in d_o]
        lane_is_a = lax.broadcasted_iota(jnp.int32, (SB_KEY_BLOCK, LANES), 1) < SB_HEAD_DIM

        def tiles(kbs, diags, carry):
            nb = len(kbs)
            lanes = [_sb_lanes(tq, d) for d in diags]
            cols = [slice(first, first + width) for first, width in lanes]
            dq_t, ca, cb = [list(c) for c in carry]
            ks = [pl.multiple_of(kb * SB_KEY_BLOCK, SB_KEY_BLOCK) for kb in kbs]
            slot = lambda p, b: p * nb + b

            def causal(b, s):
                return causal_ref[diags[b], pl.ds((s * SB_STRIP) % SB_KEY_BLOCK, SB_STRIP), cols[b]]

            kk, vv = {}, {}
            for b in range(nb):
                for p in range(pairs):
                    kk[p, b] = _pair_rows(k_ref[pl.ds(ks[b], SB_KEY_BLOCK), _pair_lanes(p)], lane_is_a)
                    vv[p, b] = _pair_rows(v_ref[pl.ds(ks[b], SB_KEY_BLOCK), _pair_lanes(p)], lane_is_a)
                    z_ref[slot(p, b), :, cols[b]] = _dot(kk[p, b], q_t[p][:, cols[b]], _NN)
            for b in range(nb):
                for p in range(pairs):
                    for s in range(n_strips):
                        rows = pl.ds(s * SB_STRIP, SB_STRIP)
                        log_rem, log_beta = _sb_log_terms(z_ref[slot(p, b), rows, cols[b]])
                        lb_ref[slot(p, b), rows, cols[b]] = log_beta
                        if isinstance(diags[b], int):
                            log_rem = log_rem * causal(b, s)
                        _sb_store_split(split_ref.at[slot(p, b)], s, log_rem, cols[b])
            for b in range(nb):
                for p in range(pairs):
                    z_ref[slot(p, b), :, cols[b]] = _dot(after_ref[...], split_ref[slot(p, b), :, cols[b]], _NN)
                    da_ref[slot(p, b), :, cols[b]] = _dot(vv[p, b], do_t[p][:, cols[b]], _NN)
            sums = {}
            for b in range(nb):
                for p in range(pairs):
                    part = [jnp.zeros((8, lanes[b][1]), F32), jnp.zeros((8, lanes[b][1]), F32)]
                    for s in range(n_strips):
                        rows = pl.ds(s * SB_STRIP, SB_STRIP)
                        start = r_ref[2 * p + (s * SB_STRIP) // SB_KEY_BLOCK, kbs[b]][:, cols[b]]
                        w = jnp.exp(lb_ref[slot(p, b), rows, cols[b]] + z_ref[slot(p, b), rows, cols[b]] + start)
                        if isinstance(diags[b], int):
                            w = w * causal(b, s)
                        w_ref[slot(p, b), rows, cols[b]] = w.astype(BF16)
                        da = da_ref[slot(p, b), rows, cols[b]] * w
                        da_ref[slot(p, b), rows, cols[b]] = da
                        _sb_store_split(split_ref.at[slot(p, b)], s, da, cols[b])
                        head = (s * SB_STRIP) // SB_KEY_BLOCK
                        part[head] = part[head] + jnp.sum(da.reshape(SB_STRIP // 8, 8, lanes[b][1]), axis=0)
                    sums[p, b] = part
            for b in range(nb):
                for p in range(pairs):
                    z_ref[slot(p, b), :, cols[b]] = _dot(before_ref[...], split_ref[slot(p, b), :, cols[b]], _NN)
            for b in range(nb):
                for p in range(pairs):
                    for s in range(n_strips):
                        rows = pl.ds(s * SB_STRIP, SB_STRIP)
                        base = (ca[p] if (s * SB_STRIP) < SB_KEY_BLOCK else cb[p])[:, cols[b]]
                        sig = jnp.exp(lb_ref[slot(p, b), rows, cols[b]])
                        dz = (da_ref[slot(p, b), rows, cols[b]] * (1.0 - sig)
                              - (z_ref[slot(p, b), rows, cols[b]] + base) * sig)
                        if isinstance(diags[b], int):
                            dz = dz * causal(b, s)
                        dz_ref[slot(p, b), rows, cols[b]] = (dz * scale).astype(BF16)
                    ca[p] = _lane_add(ca[p], jnp.sum(sums[p, b][0], axis=0, keepdims=True), lanes[b])
                    cb[p] = _lane_add(cb[p], jnp.sum(sums[p, b][1], axis=0, keepdims=True), lanes[b])
            for b in range(nb):
                for p in range(pairs):
                    dq_t[p] = _lane_add(dq_t[p], _dot(kk[p, b], dz_ref[slot(p, b), :, cols[b]], _TN), lanes[b])
                    dkk = _dot(dz_ref[slot(p, b), :, cols[b]], q[p][cols[b], :], _NN)
                    dvv = _dot(w_ref[slot(p, b), :, cols[b]], d_o[p][cols[b], :], _NN)
                    here = (pl.ds(ks[b], SB_KEY_BLOCK), _pair_lanes(p))
                    dk_ref[here] += jnp.where(lane_is_a, dkk[:SB_KEY_BLOCK], dkk[SB_KEY_BLOCK:])
                    dv_ref[here] += jnp.where(lane_is_a, dvv[:SB_KEY_BLOCK], dvv[SB_KEY_BLOCK:])
            return tuple(dq_t), tuple(ca), tuple(cb)

        n_full = i * per_q
        groups_walked = jnp.clip(jnp.max(n_ref[...]).astype(jnp.int32), 0, n_full) // SB_GROUP
        carry = (tuple(jnp.zeros((LANES, tq), F32) for _ in range(pairs)),
                 tuple(jnp.zeros((1, tq), F32) for _ in range(pairs)),
                 tuple(jnp.zeros((1, tq), F32) for _ in range(pairs)))

        def below(j, c):
            kbs = [n_full - (groups_walked - j) * SB_GROUP + b for b in range(SB_GROUP)]
            starts = [r_ref[h, kbs[-1]][:, tq // 2:] for h in range(2 * pairs)]
            reaches = jnp.max(functools.reduce(jnp.maximum, starts)) > SB_DEAD_LOG
            return lax.cond(reaches, lambda cc: tiles(kbs, [None] * SB_GROUP, cc),
                            lambda cc: tiles(kbs, ["left"] * SB_GROUP, cc), c)

        carry = lax.fori_loop(0, groups_walked, below, carry)
        own = list(range(per_q))
        carry = tiles([i * per_q + d for d in own], own, carry)
        for p in range(pairs):
            dq_ref[:, _pair_lanes(p)] = carry[0][p].T.astype(BF16)
        if rider is not None:
            rider.wait_at_last(ids, (groups, nq), ride)

    wide = pairs * LANES
    mat = pl.BlockSpec(after_m.shape, lambda g, i: (0, 0), pipeline_mode=pl.Buffered(1))
    in_specs = [pl.BlockSpec((tq, wide), lambda g, i: (i, g)),
                pl.BlockSpec((T, wide), lambda g, i: (0, groups + g), pipeline_mode=pl.Buffered(1)),
                pl.BlockSpec((T, wide), lambda g, i: (0, 2 * groups + g), pipeline_mode=pl.Buffered(1)),
                pl.BlockSpec((tq, wide), lambda g, i: (i, g)),
                pl.BlockSpec((2 * pairs, nkb, 1, tq), lambda g, i: (g, 0, 0, i)),
                pl.BlockSpec((1, 8, LANES), lambda g, i: (i, 0, 0)),
                mat, mat,
                pl.BlockSpec(causal_m.shape, lambda g, i: (0, 0, 0), pipeline_mode=pl.Buffered(1))]
    out_specs = [pl.BlockSpec((tq, wide), lambda g, i: (i, g)),
                 pl.BlockSpec((T, wide), lambda g, i: (0, g)),
                 pl.BlockSpec((T, wide), lambda g, i: (0, g))]
    out_shape = [jax.ShapeDtypeStruct((T, SB_WIDTH), BF16), jax.ShapeDtypeStruct((T, SB_WIDTH), F32),
                 jax.ShapeDtypeStruct((T, SB_WIDTH), F32)]
    args = (h_a, h_a, h_a, d_out, r_mat, walked_blocks, after_m, before_m, causal_m)
    slots = pairs * max(per_q, SB_GROUP)
    scratch = [pltpu.VMEM((slots, 2 * LANES, tq), F32), pltpu.VMEM((slots, 2 * LANES, tq), F32),
               pltpu.VMEM((slots, 4 * LANES, tq), BF16), pltpu.VMEM((slots, 2 * LANES, tq), BF16),
               pltpu.VMEM((slots, 2 * LANES, tq), F32), pltpu.VMEM((slots, 2 * LANES, tq), BF16)]
    if rider is not None:
        in_specs, args = in_specs + rider.specs, args + tuple(rider.bufs)
        out_specs, out_shape = out_specs + rider.specs, out_shape + rider.out_shape
        scratch = scratch + rider.scratch
    outs = pl.pallas_call(
        body,
        name="sb_bwd",
        grid=(groups, nq),
        in_specs=in_specs,
        out_specs=out_specs,
        out_shape=out_shape,
        scratch_shapes=scratch,
        compiler_params=_params(("arbitrary", "arbitrary") if rider is not None else ("parallel", "arbitrary")),
    )(*args)
    return outs[0], outs[1], outs[2], list(outs[3:])


def _ret_tables(T):
    half = RET_QK_DIM // 2
    inv = 1.0 / (ROPE_BASE ** (jnp.arange(half, dtype=F32) / half))
    ang = jnp.arange(T, dtype=F32)[:, None] * inv[None, :]
    cos, sin = jnp.cos(ang), jnp.sin(ang)
    cos_t = jnp.concatenate([cos, cos], axis=1)
    sin_t = jnp.concatenate([-sin, sin], axis=1)
    log_gamma = jnp.log1p(-jnp.exp2(-5.0 - jnp.arange(RET_HEADS, dtype=F32)))
    idx = jnp.arange(RET_CHUNK, dtype=F32)
    rel = idx[:, None] - idx[None, :]
    decay = jnp.where(rel[None] >= 0, jnp.exp(log_gamma[:, None, None] * jnp.maximum(rel, 0.0)[None]), 0.0)
    k_decay = jnp.exp(log_gamma[None, :] * (RET_CHUNK - 1.0 - idx)[:, None])
    q_decay = jnp.exp(log_gamma[None, :] * (idx + 1.0)[:, None])
    chunk_decay = jnp.exp(log_gamma * RET_CHUNK)
    k_dec = jnp.broadcast_to(k_decay.T[:, :, None], (RET_HEADS, RET_CHUNK, LANES))
    q_dec = jnp.broadcast_to(q_decay.T[:, :, None], (RET_HEADS, RET_CHUNK, LANES))
    c_dec = jnp.broadcast_to(chunk_decay[:, None, None], (RET_HEADS, 8, LANES))
    return cos_t, sin_t, decay, k_dec, q_dec, c_dec


def _rotary(x, cos_t, sin_t):
    return x * cos_t + pltpu.roll(x, RET_QK_DIM // 2, 1) * sin_t


def _rotary_transpose(dy, cos_t, sin_t):
    return dy * cos_t + pltpu.roll(dy * sin_t, RET_QK_DIM // 2, 1)


def _head_norm(o):
    mu = jnp.mean(o, axis=1, keepdims=True)
    cen = o - mu
    var = jnp.mean(cen * cen, axis=1, keepdims=True)
    rstd = lax.rsqrt(var + LN_EPS)
    return cen * rstd, rstd


def _ret_specs(steps, per_step, reverse):
    def n_of(n):
        return (steps - 1 - n) if reverse else n

    rows = per_step * RET_CHUNK
    q_spec = pl.BlockSpec((rows, RET_QK_WIDTH), lambda n: (n_of(n), 0))
    k_spec = pl.BlockSpec((rows, RET_QK_WIDTH), lambda n: (n_of(n), 1))
    vv = pl.BlockSpec((rows, RET_V_WIDTH), lambda n: (n_of(n), 0))
    pos = pl.BlockSpec((rows, LANES), lambda n: (n_of(n), 0))
    per_head = pl.BlockSpec((RET_HEADS, RET_CHUNK, LANES), lambda n: (0, 0, 0))
    c_dec = pl.BlockSpec((RET_HEADS, 8, LANES), lambda n: (0, 0, 0))
    state = pl.BlockSpec((RET_HEADS, per_step, RET_QK_DIM, RET_V_DIM), lambda n: (0, n_of(n), 0, 0))
    return q_spec, k_spec, vv, pos, per_head, c_dec, state


def _qk_cols(h):
    return slice(h * RET_QK_DIM, (h + 1) * RET_QK_DIM)


def _v_cols(h):
    return slice(h * RET_V_DIM, (h + 1) * RET_V_DIM)


def _ret_fwd(h_b, h_c, h_d, tables):
    T = h_b.shape[0]
    nc = T // RET_CHUNK
    per_step = _pick(nc, (RET_STEP_CHUNKS, 1))
    steps = nc // per_step
    q_spec, k_spec, vv, pos, per_head, c_dec, state = _ret_specs(steps, per_step, False)

    def body(q_ref, k_ref, v_ref, g_ref, cos_ref, sin_ref, dec_ref, kd_ref, qd_ref, cd_ref,
             y_ref, o_ref, st_ref, state_ref):
        @pl.when(pl.program_id(0) == 0)
        def _():
            state_ref[...] = jnp.zeros_like(state_ref)

        for c in range(per_step):
            rows = pl.ds(c * RET_CHUNK, RET_CHUNK)
            cos_t, sin_t = cos_ref[rows, :], sin_ref[rows, :]
            for h in range(RET_HEADS):
                q = _rotary(q_ref[rows, _qk_cols(h)], cos_t, sin_t) * (RET_QK_DIM ** -0.5)
                k = _rotary(k_ref[rows, _qk_cols(h)], cos_t, sin_t)
                v = v_ref[rows, _v_cols(h)]
                prev = state_ref[h]
                scores = _dot(q.astype(BF16), k.astype(BF16), _NT) * dec_ref[h]
                inner = _dot(scores.astype(BF16), v, _NN)
                cross = _dot((q * qd_ref[h]).astype(BF16), prev.astype(BF16), _NN)
                o = inner + cross
                st_ref[h, c] = prev
                kv = _dot((k * kd_ref[h]).astype(BF16), v, _TN)
                state_ref[h] = prev * cd_ref[h, 0:1, 0:1] + kv
                o_ref[rows, _v_cols(h)] = o
                normed, _ = _head_norm(o)
                gate = g_ref[rows, _v_cols(h)]
                y_ref[rows, _v_cols(h)] = (gate * jax.nn.sigmoid(gate) * normed).astype(BF16)

    return pl.pallas_call(
        body,
        name="ret_fwd",
        grid=(steps,),
        in_specs=[q_spec, k_spec, vv, vv, pos, pos, per_head, per_head, per_head, c_dec],
        out_specs=[vv, vv, state],
        out_shape=[jax.ShapeDtypeStruct((T, RET_V_WIDTH), BF16),
                   jax.ShapeDtypeStruct((T, RET_V_WIDTH), F32),
                   jax.ShapeDtypeStruct((RET_HEADS, nc, RET_QK_DIM, RET_V_DIM), F32)],
        scratch_shapes=[pltpu.VMEM((RET_HEADS, RET_QK_DIM, RET_V_DIM), F32)],
        compiler_params=_params(("arbitrary",)),
    )(h_b, h_b, h_c, h_d, *tables)


def _ret_bwd(d_y, o_pre, states, h_b, h_c, h_d, tables, rider=None):
    T = h_b.shape[0]
    nc = T // RET_CHUNK
    per_step = _pick(nc, (RET_STEP_CHUNKS, 1))
    steps = nc // per_step
    q_spec, k_spec, vv, pos, per_head, c_dec, state = _ret_specs(steps, per_step, True)
    n_ride = rider.n if rider is not None else 0

    def body(*refs):
        (dy_ref, o_ref, st_ref, q_ref, k_ref, v_ref, g_ref, cos_ref, sin_ref, dec_ref, kd_ref, qd_ref,
         cd_ref) = refs[:13]
        dq_ref, dk_ref, dv_ref, dg_ref = refs[13 + n_ride:17 + n_ride]
        carry_ref = refs[17 + 2 * n_ride]
        ids = [pl.program_id(0)]
        if rider is not None:
            ride = (refs[13:13 + n_ride], refs[17 + n_ride:17 + 2 * n_ride], refs[-3:])
            rider.start_at_first(ids, ride)

        @pl.when(ids[0] == 0)
        def _():
            carry_ref[...] = jnp.zeros_like(carry_ref)

        scale = RET_QK_DIM ** -0.5
        for c in reversed(range(per_step)):
            rows = pl.ds(c * RET_CHUNK, RET_CHUNK)
            cos_t, sin_t = cos_ref[rows, :], sin_ref[rows, :]
            for h in range(RET_HEADS):
                q = _rotary(q_ref[rows, _qk_cols(h)], cos_t, sin_t) * scale
                k = _rotary(k_ref[rows, _qk_cols(h)], cos_t, sin_t)
                v = v_ref[rows, _v_cols(h)]
                decay, k_dec, q_dec = dec_ref[h], kd_ref[h], qd_ref[h]
                chunk_decay = cd_ref[h, 0:1, 0:1]
                state = st_ref[h, c].astype(BF16)
                later = carry_ref[h]
                later_b = later.astype(BF16)

                gate = g_ref[rows, _v_cols(h)]
                sig = jax.nn.sigmoid(gate)
                silu = gate * sig
                normed, rstd = _head_norm(o_ref[rows, _v_cols(h)])
                d_y = dy_ref[rows, _v_cols(h)]
                dg_ref[rows, _v_cols(h)] = (d_y * normed * (sig * (1.0 + gate * (1.0 - sig)))).astype(BF16)
                d_n = d_y * silu
                d_o = rstd * (d_n - jnp.mean(d_n, axis=1, keepdims=True)
                              - normed * jnp.mean(d_n * normed, axis=1, keepdims=True))
                d_ob = d_o.astype(BF16)

                qb, kb = q.astype(BF16), k.astype(BF16)
                qd_b, kd_b = (q * q_dec).astype(BF16), (k * k_dec).astype(BF16)
                scores = _dot(qb, kb, _NT) * decay
                d_scores = (_dot(d_ob, v, _NT) * decay).astype(BF16)
                dq = _dot(d_scores, kb, _NN) + _dot(d_ob, state, _NT) * q_dec
                dk = _dot(d_scores, qb, _TN) + _dot(v, later_b, _NT) * k_dec
                dv = _dot(scores.astype(BF16), d_ob, _TN) + _dot(kd_b, later_b, _NN)
                carry_ref[h] = _dot(qd_b, d_ob, _TN) + chunk_decay * later
                dq_ref[rows, _qk_cols(h)] = _rotary_transpose(dq * scale, cos_t, sin_t).astype(BF16)
                dk_ref[rows, _qk_cols(h)] = _rotary_transpose(dk, cos_t, sin_t).astype(BF16)
                dv_ref[rows, _v_cols(h)] = dv.astype(BF16)
        if rider is not None:
            rider.wait_at_last(ids, (steps,), ride)

    qk_out = pl.BlockSpec((per_step * RET_CHUNK, RET_QK_WIDTH), lambda n: (steps - 1 - n, 0))
    in_specs = [vv, vv, state, q_spec, k_spec, vv, vv, pos, pos, per_head, per_head, per_head, c_dec]
    out_specs = [qk_out, qk_out, vv, vv]
    out_shape = [jax.ShapeDtypeStruct((T, RET_QK_WIDTH), BF16), jax.ShapeDtypeStruct((T, RET_QK_WIDTH), BF16),
                 jax.ShapeDtypeStruct((T, RET_V_WIDTH), BF16), jax.ShapeDtypeStruct((T, RET_V_WIDTH), BF16)]
    args = (d_y, o_pre, states, h_b, h_b, h_c, h_d) + tuple(tables)
    scratch = [pltpu.VMEM((RET_HEADS, RET_QK_DIM, RET_V_DIM), F32)]
    if rider is not None:
        in_specs, args = in_specs + rider.specs, args + tuple(rider.bufs)
        out_specs, out_shape = out_specs + rider.specs, out_shape + rider.out_shape
        scratch = scratch + rider.scratch
    outs = pl.pallas_call(
        body,
        name="ret_bwd",
        grid=(steps,),
        in_specs=in_specs,
        out_specs=out_specs,
        out_shape=out_shape,
        scratch_shapes=scratch,
        compiler_params=_params(("arbitrary",)),
    )(*args)
    return outs[0], outs[1], outs[2], outs[3], list(outs[4:])


def _proj_tiles(h, x):
    return h[:, 0:1536], h[:, 1536:2560], h[:, 2560:3584], h[:, 3584:4608], h[:, 4608:6656], x


def _gate_mix_tiles(y_ret, h_e, b_gate, y_sb):
    gates = jax.nn.sigmoid(h_e + b_gate)
    return y_ret, gates[:, :D_MODEL] * y_sb + gates[:, D_MODEL:] * y_ret


def _gate_mix_grad_tiles(d_mix, h_e, b_gate, y_sb, y_ret):
    gates = jax.nn.sigmoid(h_e + b_gate)
    g0, g1 = gates[:, :D_MODEL], gates[:, D_MODEL:]
    d_e = jnp.concatenate([d_mix * y_sb * g0 * (1.0 - g0), d_mix * y_ret * g1 * (1.0 - g1)], axis=1)
    return d_mix * g0, d_mix * g1, d_e, d_e


def _ln_stats(u):
    mu = jnp.mean(u, axis=1, keepdims=True)
    cen = u - mu
    var = jnp.mean(cen * cen, axis=1, keepdims=True)
    rstd = lax.rsqrt(var + LN_EPS)
    return cen * rstd, rstd


def _ln_input_grad(d_out, gain, xhat, rstd):
    d_hat = d_out * gain
    return rstd * (d_hat - jnp.mean(d_hat, axis=1, keepdims=True)
                   - xhat * jnp.mean(d_hat * xhat, axis=1, keepdims=True))


def _ln_tiles(sub, x_prev, gain, bias):
    xhat, rstd = _ln_stats(DN_ALPHA * x_prev + sub)
    out = xhat * gain + bias
    return out, out, xhat, rstd


def _residual_tiles(d_sub, res):
    return (d_sub + DN_ALPHA * res,)


def _ln_grad_tiles(d_sub, res, xhat, rstd, gain):
    d_out = d_sub + DN_ALPHA * res
    du = _ln_input_grad(d_out, gain, xhat, rstd)
    return du, du, d_out * xhat, d_out


def _ln_loss_tiles(sub, x_prev, gain, bias, target):
    xhat, rstd = _ln_stats(DN_ALPHA * x_prev + sub)
    diff = xhat * gain + bias - target
    d_out = diff * (1.0 / D_MODEL)
    du = _ln_input_grad(d_out, gain, xhat, rstd)
    return du, du, diff * diff, d_out * xhat, d_out


def _mem_probs(q_h, k_h):
    s = _dot(q_h, k_h, _NT) * (MEM_HEAD_DIM ** -0.5)
    e = jnp.exp(s - jnp.max(s, axis=1, keepdims=True))
    return e / jnp.sum(e, axis=1, keepdims=True)


def _xattn_fwd(q, kv):
    T, mem_len = q.shape[0], kv.shape[0]
    tq = _pick(T, (512, 256, 128))

    def body(q_ref, kv_ref, o_ref):
        for h in range(MEM_HEADS):
            cols = slice(h * MEM_HEAD_DIM, (h + 1) * MEM_HEAD_DIM)
            vcols = slice(D_MODEL + h * MEM_HEAD_DIM, D_MODEL + (h + 1) * MEM_HEAD_DIM)
            p = _mem_probs(q_ref[:, cols], kv_ref[:, cols])
            o_ref[:, cols] = _dot(p.astype(BF16), kv_ref[:, vcols], _NN).astype(BF16)

    return pl.pallas_call(
        body,
        name="xattn_fwd",
        grid=(T // tq,),
        in_specs=[pl.BlockSpec((tq, D_MODEL), lambda i: (i, 0)),
                  pl.BlockSpec((mem_len, 2 * D_MODEL), lambda i: (0, 0))],
        out_specs=pl.BlockSpec((tq, D_MODEL), lambda i: (i, 0)),
        out_shape=jax.ShapeDtypeStruct((T, D_MODEL), BF16),
        compiler_params=_params(("parallel",)),
    )(q, kv)


def _xattn_bwd(q, kv, d_o):
    T, mem_len = q.shape[0], kv.shape[0]
    tq = _pick(T, (512, 256, 128))

    def body(q_ref, kv_ref, do_ref, dq_ref, dkv_ref):
        @pl.when(pl.program_id(0) == 0)
        def _():
            dkv_ref[...] = jnp.zeros_like(dkv_ref)

        for h in range(MEM_HEADS):
            cols = slice(h * MEM_HEAD_DIM, (h + 1) * MEM_HEAD_DIM)
            vcols = slice(D_MODEL + h * MEM_HEAD_DIM, D_MODEL + (h + 1) * MEM_HEAD_DIM)
            q_h, k_h, do_h = q_ref[:, cols], kv_ref[:, cols], do_ref[:, cols]
            p = _mem_probs(q_h, k_h)
            dp = _dot(do_h, kv_ref[:, vcols], _NT)
            ds = p * (dp - jnp.sum(dp * p, axis=1, keepdims=True))
            dsb = (ds * (MEM_HEAD_DIM ** -0.5)).astype(BF16)
            dq_ref[:, cols] = _dot(dsb, k_h, _NN).astype(BF16)
            dkv_ref[:, cols] += _dot(dsb, q_h, _TN)
            dkv_ref[:, vcols] += _dot(p.astype(BF16), do_h, _TN)

    row = pl.BlockSpec((tq, D_MODEL), lambda i: (i, 0))
    full = pl.BlockSpec((mem_len, 2 * D_MODEL), lambda i: (0, 0))
    return pl.pallas_call(
        body,
        name="xattn_bwd",
        grid=(T // tq,),
        in_specs=[row, full, row],
        out_specs=[row, full],
        out_shape=[jax.ShapeDtypeStruct((T, D_MODEL), BF16), jax.ShapeDtypeStruct((mem_len, 2 * D_MODEL), F32)],
        compiler_params=_params(("arbitrary",)),
    )(q, kv, d_o)


def _swiglu_tiles(f):
    a, b = f[:, :FFN_HIDDEN], f[:, FFN_HIDDEN:]
    return f, a * jax.nn.sigmoid(a) * b


def _swiglu_grad_tiles(d_hidden, f):
    a, b = f[:, :FFN_HIDDEN], f[:, FFN_HIDDEN:]
    sig = jax.nn.sigmoid(a)
    return (jnp.concatenate([d_hidden * b * (sig * (1.0 + a * (1.0 - sig))), d_hidden * (a * sig)], axis=1),)


def _local_step(x, mem, w_in, small, target, fetch, ship):
    T = x.shape[0]
    tables = _ret_tables(T)
    memb = mem.astype(BF16)

    (h_a, h_b, h_c, h_d, h_e, xb), w_ffn = fetch(
        ("w_ffn_in", "w_ffn_out"),
        lambda rider: _as_host(rider, _mm_fused(
            x, w_in, mode="nn", name="proj_in", extras=[], pass_a=True,
            outs=[(1536, BF16), (1024, F32), (1024, BF16), (1024, F32), (2048, F32), (D_MODEL, BF16)],
            epilogue=_proj_tiles, max_rows=256, rider=rider)))
    (a_sb, r_mat), w_mix = fetch(("w_sb_o", "w_ret_o", "w_mix_o", "w_mem_q", "w_mem_kv", "w_mem_o"),
                                 lambda rider: _sb_fwd(h_a, rider))
    w = {**w_ffn, **w_mix}
    y_gated, o_pre, states = _ret_fwd(h_b, h_c, h_d, tables)
    y_sb = _mm(a_sb, w["w_sb_o"], mode="nn", out_dtype=F32, name="sb_out")
    row_f32, row_bf16 = (D_MODEL, F32), (D_MODEL, BF16)
    ln_outs = [row_f32, row_bf16, row_f32, (1, F32)]
    y_ret, mix_in = _mm_fused(y_gated, w["w_ret_o"], mode="nn", name="ret_out", extras=[h_e, small["b_gate"], y_sb],
                              outs=[row_f32, row_bf16], epilogue=_gate_mix_tiles)
    x1, x1b, xhat1, rstd1 = _mm_fused(mix_in, w["w_mix_o"], mode="nn", name="mix_out",
                                      extras=[x, small["ln1_g"], small["ln1_b"]], outs=ln_outs, epilogue=_ln_tiles)
    q_m = _mm(x1b, w["w_mem_q"], mode="nn", out_dtype=BF16, name="mem_q")
    kv_m = _mm(memb, w["w_mem_kv"], mode="nn", out_dtype=BF16, name="mem_kv")
    o_m = _xattn_fwd(q_m, kv_m)
    x2, x2b, xhat2, rstd2 = _mm_fused(o_m, w["w_mem_o"], mode="nn", name="mem_out",
                                      extras=[x1, small["ln2_g"], small["ln2_b"]], outs=ln_outs, epilogue=_ln_tiles)
    f, hidden = _mm_fused(x2b, w["w_ffn_in"], mode="nn", name="ffn_in", extras=[],
                          outs=[(2 * FFN_HIDDEN, F32), (FFN_HIDDEN, BF16)], epilogue=_swiglu_tiles)
    du_outs, col = [row_f32, row_bf16], D_MODEL
    du3, du3b, loss_cols, d_ln3_g, d_ln3_b = _mm_fused(
        hidden, w["w_ffn_out"], mode="nn", name="ffn_out", extras=[x2, small["ln3_g"], small["ln3_b"], target],
        outs=du_outs, sums=[col, col, col], epilogue=_ln_loss_tiles)

    g_ffn_out = _mm(hidden, du3b, mode="tn", out_dtype=BF16, name="g_ffn_out")
    (d_f,) = _mm_fused(du3b, w["w_ffn_out"], mode="nt", name="d_hidden", extras=[f],
                       outs=[(2 * FFN_HIDDEN, BF16)], epilogue=_swiglu_grad_tiles)
    g_ffn_in = _mm(x2b, d_f, mode="tn", out_dtype=BF16, name="g_ffn_in")
    du2, du2b, d_ln2_g, d_ln2_b = ship(
        {"w_ffn_out": g_ffn_out},
        lambda rider: _as_host(rider, _mm_fused(
            d_f, w["w_ffn_in"], mode="nt", name="d_x2", extras=[du3, xhat2, rstd2, small["ln2_g"]], outs=du_outs,
            sums=[col, col], epilogue=_ln_grad_tiles, rider=rider, max_rows=256)))
    g_mem_o = _mm(o_m, du2b, mode="tn", out_dtype=BF16, name="g_mem_o")
    d_om = _mm(du2b, w["w_mem_o"], mode="nt", out_dtype=BF16, name="d_om")
    d_qm, d_kvm = _xattn_bwd(q_m, kv_m, d_om)
    g_mem_q = _mm(x1b, d_qm, mode="tn", out_dtype=BF16, name="g_mem_q")
    g_mem_kv = _mm(memb, d_kvm.astype(BF16), mode="tn", out_dtype=BF16, name="g_mem_kv")
    du1, du1b, d_ln1_g, d_ln1_b = _mm_fused(
        d_qm, w["w_mem_q"], mode="nt", name="d_x1", extras=[du2, xhat1, rstd1, small["ln1_g"]], outs=du_outs,
        sums=[col, col], epilogue=_ln_grad_tiles)
    g_mix_o = _mm(mix_in, du1b, mode="tn", out_dtype=BF16, name="g_mix_o")
    d_ysb, d_yret, d_e, d_b_gate = _mm_fused(
        du1b, w["w_mix_o"], mode="nt", name="d_mix_in", extras=[h_e, small["b_gate"], y_sb, y_ret],
        outs=[row_bf16, row_bf16, (2 * D_MODEL, BF16)], sums=[2 * D_MODEL], epilogue=_gate_mix_grad_tiles)
    g_sb_o = _mm(a_sb, d_ysb, mode="tn", out_dtype=BF16, name="g_sb_o")
    g_ret_o = _mm(y_gated, d_yret, mode="tn", out_dtype=BF16, name="g_ret_o")
    d_asb = _mm(d_ysb, w["w_sb_o"], mode="nt", out_dtype=BF16, name="d_asb")
    d_ygated = _mm(d_yret, w["w_ret_o"], mode="nt", out_dtype=F32, name="d_ygated")
    small_grads = {"b_gate": d_b_gate, "ln1_g": d_ln1_g, "ln1_b": d_ln1_b, "ln2_g": d_ln2_g, "ln2_b": d_ln2_b,
                   "ln3_g": d_ln3_g, "ln3_b": d_ln3_b, "loss_cols": loss_cols}
    d_rq, d_rk, d_c, d_d = ship({"w_mem_kv": g_mem_kv, "w_mem_q": g_mem_q, "w_mem_o": g_mem_o, "w_mix_o": g_mix_o},
                                lambda rider: _ret_bwd(d_ygated, o_pre, states, h_b, h_c, h_d, tables, rider))
    d_q, d_k, d_v = ship({"w_ffn_in": g_ffn_in, "w_ret_o": g_ret_o, "w_sb_o": g_sb_o, "small": small_grads},
                         lambda rider: _sb_bwd(h_a, d_asb, r_mat, rider))
    d_h = [("sb_q", d_q), ("sb_k", d_k), ("sb_v", d_v), ("ret_q", d_rq), ("ret_k", d_rk), ("ret_v", d_c),
           ("ret_g", d_d), ("gate", d_e)]
    g_in = jnp.concatenate([_mm(xb, piece, mode="tn", out_dtype=BF16, name="g_in_" + tag) for tag, piece in d_h],
                           axis=1)
    (d_x,) = ship({"w_in": g_in},
                  lambda rider: _as_host(rider, _mm_fused(
                      [piece for _, piece in d_h], w_in, mode="nt", name="d_x", extras=[du1], outs=[(D_MODEL, F32)],
                      epilogue=_residual_tiles, rider=rider, max_rows=256)))
    return d_x


def _adamw_math(w, g, m, v):
    m = ADAM_B1 * m + (1.0 - ADAM_B1) * g
    v = ADAM_B2 * v + (1.0 - ADAM_B2) * jnp.square(g)
    m_hat = m / (1.0 - ADAM_B1 ** ADAM_STEP)
    v_hat = v / (1.0 - ADAM_B2 ** ADAM_STEP)
    delta = -ADAM_LR * (m_hat / (jnp.sqrt(v_hat) + ADAM_EPS) + ADAM_WD * w)
    return delta, m, v


def _adamw(parts, w, m, v, name):
    R, C = w.shape
    tr = max(t for t in range(16, min(R, 256) + 1, 16) if R % t == 0) if R >= 16 else R

    def body(p_ref, w_ref, m_ref, v_ref, g_ref, d_ref, nm_ref, nv_ref):
        g = p_ref[0].astype(F32)
        for j in range(1, N_DEV):
            g = g + p_ref[j].astype(F32)
        delta, nm, nv = _adamw_math(w_ref[...], g, m_ref[...], v_ref[...])
        g_ref[...] = g
        d_ref[...] = delta
        nm_ref[...] = nm
        nv_ref[...] = nv

    blk = pl.BlockSpec((tr, C), lambda i: (i, 0))
    out = jax.ShapeDtypeStruct((R, C), F32)
    return pl.pallas_call(
        body,
        name=name,
        grid=(R // tr,),
        in_specs=[pl.BlockSpec((N_DEV, tr, C), lambda i: (0, i, 0)), blk, blk, blk],
        out_specs=[blk] * 4,
        out_shape=[out] * 4,
        compiler_params=_params(("parallel",)),
    )(parts, w, m, v)


_SHARD_AXIS = {"w_in": 1, "w_sb_o": 1, "w_ret_o": 0, "w_mix_o": 0, "w_mem_q": 0, "w_mem_kv": 1, "w_mem_o": 0,
               "w_ffn_in": 1, "w_ffn_out": 0}
_MATRICES = tuple(_SHARD_AXIS)
_SMALL = ("b_gate", "ln1_g", "ln1_b", "ln2_g", "ln2_b", "ln3_g", "ln3_b")
_WEIGHT_ORDER = ("w_in", "b_gate", "w_sb_o", "w_ret_o", "w_mix_o", "ln1_g", "ln1_b", "w_mem_q", "w_mem_kv", "w_mem_o",
                 "ln2_g", "ln2_b", "w_ffn_in", "w_ffn_out", "ln3_g", "ln3_b")


def _assemble(name, gathered):
    if _SHARD_AXIS[name] == 0:
        return gathered.reshape(-1, gathered.shape[2])
    return jnp.transpose(gathered, (1, 0, 2)).reshape(gathered.shape[1], -1)


def _to_slots(name, full):
    if _SHARD_AXIS[name] == 0:
        return full.reshape(N_DEV, full.shape[0] // N_DEV, full.shape[1])
    return jnp.transpose(full.reshape(full.shape[0], N_DEV, full.shape[1] // N_DEV), (1, 0, 2))


SMALL_ROWS = 16


def _pack_small(vals):
    return jnp.concatenate([vals["b_gate"].reshape(2, D_MODEL)] + [vals[n] for n in _SMALL[1:]], axis=0)


def _unpack_small(packed):
    out = {"b_gate": packed[0:2].reshape(1, 2 * D_MODEL)}
    for i, n in enumerate(_SMALL[1:]):
        out[n] = packed[2 + i:3 + i]
    return out


def kernel(x, mem, w_in, b_gate, w_sb_o, w_ret_o, w_mix_o, ln1_g, ln1_b, w_mem_q, w_mem_kv, w_mem_o, ln2_g, ln2_b, w_ffn_in, w_ffn_out, ln3_g, ln3_b, loss_target, m_w_in, m_b_gate, m_w_sb_o, m_w_ret_o, m_w_mix_o, m_ln1_g, m_ln1_b, m_w_mem_q, m_w_mem_kv, m_w_mem_o, m_ln2_g, m_ln2_b, m_w_ffn_in, m_w_ffn_out, m_ln3_g, m_ln3_b, v_w_in, v_b_gate, v_w_sb_o, v_w_ret_o, v_w_mix_o, v_ln1_g, v_ln1_b, v_w_mem_q, v_w_mem_kv, v_w_mem_o, v_ln2_g, v_ln2_b, v_w_ffn_in, v_w_ffn_out, v_ln3_g, v_ln3_b):
    weights = dict(w_in=w_in, b_gate=b_gate, w_sb_o=w_sb_o, w_ret_o=w_ret_o, w_mix_o=w_mix_o, ln1_g=ln1_g, ln1_b=ln1_b,
                   w_mem_q=w_mem_q, w_mem_kv=w_mem_kv, w_mem_o=w_mem_o, ln2_g=ln2_g, ln2_b=ln2_b, w_ffn_in=w_ffn_in,
                   w_ffn_out=w_ffn_out, ln3_g=ln3_g, ln3_b=ln3_b)
    mom1 = dict(w_in=m_w_in, b_gate=m_b_gate, w_sb_o=m_w_sb_o, w_ret_o=m_w_ret_o, w_mix_o=m_w_mix_o, ln1_g=m_ln1_g,
                ln1_b=m_ln1_b, w_mem_q=m_w_mem_q, w_mem_kv=m_w_mem_kv, w_mem_o=m_w_mem_o, ln2_g=m_ln2_g, ln2_b=m_ln2_b,
                w_ffn_in=m_w_ffn_in, w_ffn_out=m_w_ffn_out, ln3_g=m_ln3_g, ln3_b=m_ln3_b)
    mom2 = dict(w_in=v_w_in, b_gate=v_b_gate, w_sb_o=v_w_sb_o, w_ret_o=v_w_ret_o, w_mix_o=v_w_mix_o, ln1_g=v_ln1_g,
                ln1_b=v_ln1_b, w_mem_q=v_w_mem_q, w_mem_kv=v_w_mem_kv, w_mem_o=v_w_mem_o, ln2_g=v_ln2_g, ln2_b=v_ln2_b,
                w_ffn_in=v_w_ffn_in, w_ffn_out=v_w_ffn_out, ln3_g=v_ln3_g, ln3_b=v_ln3_b)

    (gathered_in,) = _exchange([weights["w_in"][0].astype(BF16)], False, "gather_w_in")
    received = {}

    def fetch(names, host):
        res = host(_Rider([weights[n][0].astype(BF16) for n in names], False))
        return res[:-1], {n: _assemble(n, g) for n, g in zip(names, res[-1])}

    def ship(grads, host):
        names = list(grads)
        bufs = []
        for n in names:
            if n == "small":
                part = jnp.concatenate([_pack_small(grads[n]), grads[n]["loss_cols"],
                                        jnp.zeros((SMALL_ROWS - 9, D_MODEL), F32)], axis=0)
                bufs.append(jnp.broadcast_to(part[None], (N_DEV,) + part.shape))
            else:
                bufs.append(_to_slots(n, grads[n]).astype(BF16))
        res = host(_Rider(bufs, True))
        received.update(zip(names, res[-1]))
        return res[:-1]

    small = {n: weights[n] for n in _SMALL}
    d_x = _local_step(x[0], mem[0], _assemble("w_in", gathered_in), small, loss_target[0], fetch, ship)

    new = {}
    for n in _MATRICES:
        new[n] = _adamw(received[n], weights[n][0], mom1[n][0], mom2[n][0], "adamw_" + n)
    packed = _adamw(received["small"][:, :8], _pack_small({n: weights[n] for n in _SMALL}),
                    _pack_small({n: mom1[n] for n in _SMALL}), _pack_small({n: mom2[n] for n in _SMALL}), "adamw_small")
    small_new = [_unpack_small(p) for p in packed]
    loss = jnp.sum(received["small"][:, 8]) * (0.5 / D_MODEL)

    outs = [loss, d_x[None]]
    for slot in range(4):
        for n in _WEIGHT_ORDER:
            outs.append(new[n][slot][None] if n in new else small_new[slot][n])
    return tuple(outs)
```

```python
import functools
import math

import jax
import jax.numpy as jnp
from jax import lax
from jax.experimental import pallas as pl
from jax.experimental.pallas import tpu as pltpu

F32 = jnp.float32
BF16 = jnp.bfloat16

N_DEV = 8
D_MODEL = 1024
SB_HEAD_DIM = 64
SB_WIDTH = 512
RET_HEADS = 4
RET_QK_DIM = 128
RET_V_DIM = 256
RET_QK_WIDTH = 512
RET_V_WIDTH = 1024
RET_CHUNK = 128
RET_STEP_CHUNKS = 4
ROPE_BASE = 10000.0
MEM_HEADS = 4
MEM_HEAD_DIM = 256
FFN_HIDDEN = 2816
DN_ALPHA = 2.0 ** 0.25
LN_EPS = 1e-5
ADAM_LR = 0.001
ADAM_B1 = 0.9
ADAM_B2 = 0.999
ADAM_EPS = 1e-08
ADAM_WD = 0.01
ADAM_STEP = 10

VMEM_LIMIT_BYTES = 56 * 1024 * 1024
LANES = 128
SB_KEY_BLOCK = 128
SB_Q_BLOCK = 256
SB_DEAD_LOG = -105.0

MESH_AXES = ("x", "y", "c")


def _pick(dim, prefs):
    for p in prefs:
        if dim % p == 0:
            return p
    return dim


def _params(sem):
    return pltpu.CompilerParams(dimension_semantics=sem, vmem_limit_bytes=VMEM_LIMIT_BYTES)


def _dot(a, b, dims):
    return lax.dot_general(a, b, (dims, ((), ())), preferred_element_type=F32)


_NN = ((1,), (0,))
_NT = ((1,), (1,))
_TN = ((0,), (0,))


def _my_index():
    return 4 * lax.axis_index("x") + 2 * lax.axis_index("y") + lax.axis_index("c")


def _peer(k):
    x, y, c = lax.axis_index("x"), lax.axis_index("y"), lax.axis_index("c")
    bx, by, bc = (k >> 2) & 1, (k >> 1) & 1, k & 1
    px = (1 - x) if bx else x
    py = (1 - y) if by else y
    pc = (1 - c) if bc else c
    return (px, py, pc), 4 * px + 2 * py + pc


class _Rider:
    def __init__(self, bufs, scatter):
        self.bufs, self.scatter, self.n = list(bufs), scatter, len(bufs)
        self.specs = [pl.BlockSpec(memory_space=pl.ANY)] * self.n
        self.out_shape = [jax.ShapeDtypeStruct(b.shape if scatter else (N_DEV,) + b.shape, b.dtype) for b in self.bufs]
        self.scratch = [pltpu.SemaphoreType.DMA((self.n, N_DEV - 1)), pltpu.SemaphoreType.DMA((self.n, N_DEV - 1)),
                        pltpu.SemaphoreType.DMA((self.n,))]

    def _remote(self, ride, a, k, src_ref, slot, to):
        _, dst, (send_sems, recv_sems, _) = ride
        return pltpu.make_async_remote_copy(src_ref=src_ref, dst_ref=dst[a].at[slot], send_sem=send_sems.at[a, k],
                                            recv_sem=recv_sems.at[a, k], device_id=to,
                                            device_id_type=pl.DeviceIdType.MESH)

    def _local(self, ride, a):
        src, dst, (_, _, local_sems) = ride
        me = _my_index()
        return pltpu.make_async_copy(src[a].at[me] if self.scatter else src[a], dst[a].at[me], local_sems.at[a])

    def _direct(self, ride, a):
        src = ride[0]
        me = _my_index()
        out = []
        for k in range(1, N_DEV):
            peer, peer_idx = _peer(k)
            out.append(self._remote(ride, a, k - 1, src[a].at[peer_idx], me, peer))
        return out

    def _two_level(self, ride, a):
        src, dst = ride[0], ride[1]
        x, y, c = lax.axis_index("x"), lax.axis_index("y"), lax.axis_index("c")
        me, sibling = _my_index(), (x, y, 1 - c)
        chips = [(1 - x, y), (x, 1 - y), (1 - x, 1 - y)]
        first = [self._remote(ride, a, 0, src[a], me, sibling)]
        passed, landing = [], [self._remote(ride, a, 0, src[a], me + 1 - 2 * c, sibling)]
        for j, (px, py) in enumerate(chips):
            first.append(self._remote(ride, a, 1 + j, src[a], me, (px, py, c)))
            theirs = 4 * px + 2 * py + c
            passed.append(self._remote(ride, a, 4 + j, dst[a].at[theirs], theirs, sibling))
            landing.append(self._remote(ride, a, 1 + j, src[a], theirs, (px, py, c)))
        for j, (px, py) in enumerate(chips):
            landing.append(self._remote(ride, a, 4 + j, src[a], 4 * px + 2 * py + 1 - c, sibling))
        return first, passed, landing

    def start(self, ride):
        for a in range(self.n):
            self._local(ride, a).start()
            for cp in (self._direct(ride, a) if self.scatter else self._two_level(ride, a)[0]):
                cp.start()

    def finish(self, ride):
        if self.scatter:
            for a in range(self.n):
                for cp in self._direct(ride, a):
                    cp.wait()
                self._local(ride, a).wait()
            return
        levels = [self._two_level(ride, a) for a in range(self.n)]
        for first, passed, landing in levels:
            for j, cp in enumerate(passed):
                landing[1 + j].wait_recv()
                cp.start()
        for a, (first, passed, landing) in enumerate(levels):
            landing[0].wait_recv()
            for cp in landing[4:]:
                cp.wait_recv()
            for cp in first + passed:
                cp.wait_send()
            self._local(ride, a).wait()

    def start_at_first(self, ids, ride):
        first = functools.reduce(jnp.logical_and, [i == 0 for i in ids])

        @pl.when(first)
        def _():
            self.start(ride)

    def wait_at_last(self, ids, grid, ride):
        last = functools.reduce(jnp.logical_and, [i == g - 1 for i, g in zip(ids, grid)])

        @pl.when(last)
        def _():
            self.finish(ride)


def _exchange(bufs, scatter, name):
    rider = _Rider(bufs, scatter)

    def body(*refs):
        ride = (refs[:rider.n], refs[rider.n:2 * rider.n], refs[2 * rider.n:])
        rider.start(ride)
        rider.finish(ride)

    return pl.pallas_call(
        body,
        name=name,
        in_specs=rider.specs,
        out_specs=rider.specs,
        out_shape=rider.out_shape,
        scratch_shapes=rider.scratch,
    )(*rider.bufs)


MM_RESIDENT_B_BYTES = 14 * 1024 * 1024
MM_A_TILE_BYTES = 4 * 1024 * 1024
MM_OUT_TILE_BYTES = 6 * 1024 * 1024


def _mm_tiles(mode, M, N, K, a_bytes, out_bytes):
    if mode != "tn" and K * N * 2 <= MM_RESIDENT_B_BYTES:
        for tm in (1024, 512, 256, 128):
            if M % tm == 0 and tm * K * a_bytes <= MM_A_TILE_BYTES and tm * N * out_bytes <= MM_OUT_TILE_BYTES:
                return tm, N, K
    if mode == "tn":
        return (_pick(M, (1024, 1408, 512, 256, 128)), _pick(N, (1024, 1664, 1408, 512, 256, 128)),
                _pick(K, (2048, 1024, 512, 256, 128)))
    return _pick(M, (1024, 512, 256, 128)), _pick(N, (512, 256, 128)), _pick(K, (1024, 512, 256, 128))


def _mm(a, b, *, mode, out_dtype, name, res=None, res_scale=1.0, rider=None):
    if mode == "nn":
        (M, K), (K2, N) = a.shape, b.shape
    elif mode == "nt":
        (M, K), (N, K2) = a.shape, b.shape
    else:
        (K, M), (K2, N) = a.shape, b.shape
    assert K == K2, (a.shape, b.shape, mode)
    out_bytes = jnp.dtype(out_dtype).itemsize + (4 if res is not None else 0)
    tm, tn, tk = _mm_tiles(mode, M, N, K, a.dtype.itemsize, out_bytes)
    grid = (M // tm, N // tn, K // tk)
    nk = grid[2]
    dims = {"nn": _NN, "nt": _NT, "tn": _TN}[mode]
    n_in = 2 + (res is not None)
    n_ride = rider.n if rider is not None else 0

    def body(*refs):
        a_ref, b_ref = refs[:2]
        r_ref = refs[2] if res is not None else None
        o_ref = refs[n_in + n_ride]
        rest = refs[n_in + 2 * n_ride + 1:]
        acc_ref = rest[0] if nk > 1 else None
        ids = [pl.program_id(d) for d in range(3)]
        if rider is not None:
            ride = (refs[n_in:n_in + n_ride], refs[n_in + n_ride + 1:n_in + 2 * n_ride + 1], rest[-3:])
            rider.start_at_first(ids, ride)
        part = _dot(a_ref[...].astype(BF16), b_ref[...].astype(BF16), dims)

        def finish(total):
            if r_ref is not None:
                total = total + res_scale * r_ref[...]
            o_ref[...] = total.astype(out_dtype)

        if nk == 1:
            finish(part)
        else:
            k = ids[2]

            @pl.when(k == 0)
            def _():
                acc_ref[...] = part

            @pl.when(k > 0)
            def _():
                acc_ref[...] += part

            @pl.when(k == nk - 1)
            def _():
                finish(acc_ref[...])

        if rider is not None:
            rider.wait_at_last(ids, grid, ride)

    if mode == "nn":
        a_spec = pl.BlockSpec((tm, tk), lambda i, j, k: (i, k))
        b_spec = pl.BlockSpec((tk, tn), lambda i, j, k: (k, j))
    elif mode == "nt":
        a_spec = pl.BlockSpec((tm, tk), lambda i, j, k: (i, k))
        b_spec = pl.BlockSpec((tn, tk), lambda i, j, k: (j, k))
    else:
        a_spec = pl.BlockSpec((tk, tm), lambda i, j, k: (k, i))
        b_spec = pl.BlockSpec((tk, tn), lambda i, j, k: (k, j))
    o_spec = pl.BlockSpec((tm, tn), lambda i, j, k: (i, j))
    in_specs = [a_spec, b_spec] + ([o_spec] if res is not None else [])
    args = (a, b) + ((res,) if res is not None else ())
    out_specs, out_shape = [o_spec], [jax.ShapeDtypeStruct((M, N), out_dtype)]
    scratch = [pltpu.VMEM((tm, tn), F32)] if nk > 1 else []
    sem = ("parallel", "parallel", "arbitrary")
    if rider is not None:
        in_specs, args = in_specs + rider.specs, args + tuple(rider.bufs)
        out_specs, out_shape = out_specs + rider.specs, out_shape + rider.out_shape
        scratch = scratch + rider.scratch
        sem = ("arbitrary",) * 3
    outs = pl.pallas_call(
        body,
        name=name,
        grid=grid,
        in_specs=in_specs,
        out_specs=out_specs,
        out_shape=out_shape,
        scratch_shapes=scratch,
        compiler_params=_params(sem),
    )(*args)
    return outs[0] if rider is None else (outs[0], list(outs[1:]))


def _mm_host(a, b, *, rider, **kw):
    out = _mm(a, b, rider=rider, **kw)
    return out if rider is not None else (out, [])


def _as_host(rider, results):
    return results if rider is not None else tuple(results) + ([],)


MM_FUSED_MARGIN_BYTES = 10 * 1024 * 1024
MM_FUSED_MAX_ROWS = 512


def _col_sum_update(acc_ref, val, first):
    part = jnp.sum(val.reshape(val.shape[0] // 8, 8, val.shape[1]), axis=0)

    @pl.when(first)
    def _():
        acc_ref[...] = part

    @pl.when(jnp.logical_not(first))
    def _():
        acc_ref[...] += part


def _mm_fused(a, b, *, mode, name, extras, outs, epilogue, sums=(), rider=None, max_rows=MM_FUSED_MAX_ROWS,
              pass_a=False):
    parts = list(a) if isinstance(a, (list, tuple)) else [a]
    M, K = parts[0].shape[0], sum(p.shape[1] for p in parts)
    if mode == "nn":
        (K2, N), b_dims = b.shape, _NN
    else:
        (N, K2), b_dims = b.shape, _NT
    assert K == K2, (K, b.shape, mode)
    rows = parts + [e for e in extras if e.shape[0] == M]
    per_row = 2 * (sum(e.shape[1] * e.dtype.itemsize for e in rows)
                   + sum(c * jnp.dtype(d).itemsize for c, d in outs)) + 2 * N * 4
    budget = VMEM_LIMIT_BYTES - K * N * 2 - MM_FUSED_MARGIN_BYTES
    tm = next(t for t in (512, 256, 128, 64, 32, 16) if t <= max_rows and M % t == 0 and t * per_row <= budget)
    steps = M // tm
    n_a, n_x, n_o, n_s = len(parts), len(extras), len(outs), len(sums)
    n_ride = rider.n if rider is not None else 0

    def body(*refs):
        a_refs, b_ref = refs[:n_a], refs[n_a]
        x_refs = refs[n_a + 1:n_a + 1 + n_x]
        base = n_a + 1 + n_x + n_ride
        o_refs, s_refs = refs[base:base + n_o], refs[base + n_o:base + n_o + n_s]
        acc_refs = refs[base + n_o + n_s + n_ride:base + n_o + 2 * n_s + n_ride]
        ids = [pl.program_id(0)]
        if rider is not None:
            ride = (refs[n_a + 1 + n_x:base], refs[base + n_o + n_s:base + n_o + n_s + n_ride], refs[-3:])
            rider.start_at_first(ids, ride)
        a_tile = a_refs[0][...]
        a_bf16 = a_tile.astype(BF16) if n_a == 1 else jnp.concatenate([r[...].astype(BF16) for r in a_refs], axis=1)
        prod = _dot(a_bf16, b_ref[...], b_dims)
        tiles = epilogue(prod, *([a_tile] if pass_a else []), *[r[...] for r in x_refs])
        for o_ref, t in zip(o_refs, tiles[:n_o]):
            o_ref[...] = t.astype(o_ref.dtype)
        for acc_ref, t in zip(acc_refs, tiles[n_o:]):
            _col_sum_update(acc_ref, t, ids[0] == 0)
        if n_s:
            @pl.when(ids[0] == steps - 1)
            def _():
                for s_ref, acc_ref in zip(s_refs, acc_refs):
                    s_ref[...] = jnp.sum(acc_ref[...], axis=0, keepdims=True)
        if rider is not None:
            rider.wait_at_last(ids, (steps,), ride)

    in_specs = [pl.BlockSpec((tm, p.shape[1]), lambda i: (i, 0)) for p in parts]
    in_specs.append(pl.BlockSpec(b.shape, lambda i: (0, 0), pipeline_mode=pl.Buffered(1)))
    for e in extras:
        in_specs.append(pl.BlockSpec((tm, e.shape[1]), lambda i: (i, 0)) if e.shape[0] == M
                        else pl.BlockSpec(e.shape, lambda i: (0, 0)))
    out_specs = ([pl.BlockSpec((tm, c), lambda i: (i, 0)) for c, _ in outs]
                 + [pl.BlockSpec((1, c), lambda i: (0, 0)) for c in sums])
    out_shape = ([jax.ShapeDtypeStruct((M, c), d) for c, d in outs]
                 + [jax.ShapeDtypeStruct((1, c), F32) for c in sums])
    args = tuple(parts) + (b,) + tuple(extras)
    scratch = [pltpu.VMEM((8, c), F32) for c in sums]
    if rider is not None:
        in_specs, args = in_specs + rider.specs, args + tuple(rider.bufs)
        out_specs, out_shape = out_specs + rider.specs, out_shape + rider.out_shape
        scratch = scratch + rider.scratch
    res = pl.pallas_call(
        body,
        name=name,
        grid=(steps,),
        in_specs=in_specs,
        out_specs=out_specs,
        out_shape=out_shape,
        scratch_shapes=scratch,
        compiler_params=_params(("arbitrary",) if (n_s or rider is not None) else ("parallel",)),
    )(*args)
    return tuple(res[:n_o + n_s]) + ((list(res[n_o + n_s:]),) if rider is not None else ())


def _pair_rows(blk, lane_is_a):
    zero = jnp.zeros_like(blk)
    return jnp.concatenate([jnp.where(lane_is_a, blk, zero), jnp.where(lane_is_a, zero, blk)], axis=0)


SB_STRIP = 32
SB_FWD_PAIRS = 4
SB_BWD_PAIRS = 2
SB_GROUP = 2


def _pair_lanes(p):
    return slice(p * LANES, (p + 1) * LANES)


def _sb_scan_matrices():
    o = lax.broadcasted_iota(jnp.int32, (2 * LANES, 4 * LANES), 0)
    c = lax.broadcasted_iota(jnp.int32, (2 * LANES, 4 * LANES), 1) & (2 * LANES - 1)
    same = (o >= LANES) == (c >= LANES)
    oo, cc = o & (LANES - 1), c & (LANES - 1)
    return (jnp.where(same & (cc > oo), 1.0, 0.0).astype(BF16), jnp.where(same & (cc < oo), 1.0, 0.0).astype(BF16))


def _sb_causal_masks(tq):
    d = lax.broadcasted_iota(jnp.int32, (tq // SB_KEY_BLOCK, SB_KEY_BLOCK, tq), 0)
    k = lax.broadcasted_iota(jnp.int32, (tq // SB_KEY_BLOCK, SB_KEY_BLOCK, tq), 1)
    t = lax.broadcasted_iota(jnp.int32, (tq // SB_KEY_BLOCK, SB_KEY_BLOCK, tq), 2)
    return jnp.where(d * SB_KEY_BLOCK + k < t, 1.0, 0.0).astype(F32)


def _sb_log_terms(z):
    log_rem = -jnp.maximum(z, 0.0) - jnp.log(1.0 + jnp.exp(-jnp.abs(z)))
    return log_rem, log_rem + z


def _sb_store_split(ref, strip, val, cols):
    hi = val.astype(BF16)
    ref[pl.ds(strip * SB_STRIP, SB_STRIP), cols] = hi
    ref[pl.ds(2 * LANES + strip * SB_STRIP, SB_STRIP), cols] = (val - hi.astype(F32)).astype(BF16)


def _sb_lanes(tq, diag):
    if diag == "left":
        return 0, tq // 2
    first = 0 if diag is None else diag * SB_KEY_BLOCK
    return first, tq - first


def _lane_add(full, part, lanes):
    first, width = lanes
    pieces = [full[:, :first]] if first else []
    pieces.append(full[:, first:first + width] + part)
    if first + width < full.shape[1]:
        pieces.append(full[:, first + width:])
    return pieces[0] if len(pieces) == 1 else jnp.concatenate(pieces, axis=1)


def _sb_fwd(h_a, rider=None):
    assert SB_FWD_PAIRS == 4
    T = h_a.shape[0]
    tq = _pick(T, (SB_Q_BLOCK, SB_KEY_BLOCK))
    nq, per_q, nkb = T // tq, tq // SB_KEY_BLOCK, T // SB_KEY_BLOCK
    assert per_q % SB_GROUP == 0
    n_strips = 2 * LANES // SB_STRIP
    n_ride = rider.n if rider is not None else 0
    after_m, _ = _sb_scan_matrices()
    causal_m = _sb_causal_masks(tq)
    pairs = SB_FWD_PAIRS

    def body(*refs):
        q_ref, k_ref, v_ref, after_ref, causal_ref = refs[:5]
        a_ref, r_ref, n_ref = refs[5 + n_ride:8 + n_ride]
        z_ref, lb_ref, split_ref, w_ref = refs[8 + 2 * n_ride:12 + 2 * n_ride]
        ids = [pl.program_id(0)]
        if rider is not None:
            ride = (refs[5:5 + n_ride], refs[8 + n_ride:8 + 2 * n_ride], refs[-3:])
            rider.start_at_first(ids, ride)
        i = ids[0]
        q_t = [(q_ref[:, _pair_lanes(p)].astype(F32).T * (SB_HEAD_DIM ** -0.5)).astype(BF16) for p in range(pairs)]
        lane_is_a = lax.broadcasted_iota(jnp.int32, (SB_KEY_BLOCK, LANES), 1) < SB_HEAD_DIM

        def tiles(kbs, diags, carry):
            nb = len(kbs)
            lanes = [_sb_lanes(tq, d) for d in diags]
            cols = [slice(first, first + width) for first, width in lanes]
            acc_t, ra, rb = [list(c) for c in carry]
            ks = [pl.multiple_of(kb * SB_KEY_BLOCK, SB_KEY_BLOCK) for kb in kbs]
            slot = lambda p, b: p * nb + b

            def causal(b, s):
                return causal_ref[diags[b], pl.ds((s * SB_STRIP) % SB_KEY_BLOCK, SB_STRIP), cols[b]]

            vv = {}
            for b in range(nb):
                for p in range(pairs):
                    kk = _pair_rows(k_ref[pl.ds(ks[b], SB_KEY_BLOCK), _pair_lanes(p)], lane_is_a)
                    vv[p, b] = _pair_rows(v_ref[pl.ds(ks[b], SB_KEY_BLOCK), _pair_lanes(p)], lane_is_a)
                    z_ref[slot(p, b), :, cols[b]] = _dot(kk, q_t[p][:, cols[b]], _NN)
            sums = {}
            for b in range(nb):
                for p in range(pairs):
                    part = [jnp.zeros((8, lanes[b][1]), F32), jnp.zeros((8, lanes[b][1]), F32)]
                    for s in range(n_strips):
                        rows = pl.ds(s * SB_STRIP, SB_STRIP)
                        log_rem, log_beta = _sb_log_terms(z_ref[slot(p, b), rows, cols[b]])
                        lb_ref[slot(p, b), rows, cols[b]] = log_beta
                        if isinstance(diags[b], int):
                            log_rem = log_rem * causal(b, s)
                        _sb_store_split(split_ref.at[slot(p, b)], s, log_rem, cols[b])
                        head = (s * SB_STRIP) // SB_KEY_BLOCK
                        part[head] = part[head] + jnp.sum(log_rem.reshape(SB_STRIP // 8, 8, lanes[b][1]), axis=0)
                    sums[p, b] = part
            for b in range(nb):
                for p in range(pairs):
                    z_ref[slot(p, b), :, cols[b]] = _dot(after_ref[...], split_ref[slot(p, b), :, cols[b]], _NN)
            for b in range(nb):
                for p in range(pairs):
                    for s in range(n_strips):
                        rows = pl.ds(s * SB_STRIP, SB_STRIP)
                        start = (ra[p] if (s * SB_STRIP) < SB_KEY_BLOCK else rb[p])[:, cols[b]]
                        w = jnp.exp(lb_ref[slot(p, b), rows, cols[b]] + z_ref[slot(p, b), rows, cols[b]] + start)
                        if isinstance(diags[b], int):
                            w = w * causal(b, s)
                        w_ref[slot(p, b), rows, cols[b]] = w.astype(BF16)
                    r_ref[2 * p, kbs[b]] = ra[p]
                    r_ref[2 * p + 1, kbs[b]] = rb[p]
                    ra[p] = _lane_add(ra[p], jnp.sum(sums[p, b][0], axis=0, keepdims=True), lanes[b])
                    rb[p] = _lane_add(rb[p], jnp.sum(sums[p, b][1], axis=0, keepdims=True), lanes[b])
            for b in range(nb):
                for p in range(pairs):
                    acc_t[p] = _lane_add(acc_t[p], _dot(vv[p, b], w_ref[slot(p, b), :, cols[b]], _TN), lanes[b])
            return tuple(acc_t), tuple(ra), tuple(rb)

        carry = (tuple(jnp.zeros((LANES, tq), F32) for _ in range(pairs)),
                 tuple(jnp.zeros((1, tq), F32) for _ in range(pairs)),
                 tuple(jnp.zeros((1, tq), F32) for _ in range(pairs)))
        own = list(reversed(range(per_q)))
        n_full = i * per_q
        carry = lax.cond(
            i > 0,
            lambda cc: tiles([n_full + d for d in own] + [n_full - 1 - b for b in range(SB_GROUP)],
                             own + [None] * SB_GROUP, cc),
            lambda cc: tiles([n_full + d for d in own], own, cc), carry)
        first_walked = jnp.where(i > 0, SB_GROUP, 0).astype(jnp.int32)

        def top_of(sums_a, sums_b, first):
            return jnp.max(functools.reduce(jnp.maximum, [r[:, first:] for r in sums_a + sums_b]))

        def alive(c):
            return jnp.logical_and(c[0] < n_full, top_of(c[2], c[3], 0) > SB_DEAD_LOG)

        def step(c):
            kbs = [n_full - 1 - c[0] - b for b in range(SB_GROUP)]
            return (c[0] + SB_GROUP,) + lax.cond(
                top_of(c[2], c[3], tq // 2) > SB_DEAD_LOG,
                lambda cc: tiles(kbs, [None] * SB_GROUP, cc), lambda cc: tiles(kbs, ["left"] * SB_GROUP, cc), c[1:])

        walked, acc_t, _, _ = lax.while_loop(alive, step, (first_walked,) + carry)
        for p in range(pairs):
            a_ref[:, _pair_lanes(p)] = acc_t[p].T.astype(BF16)
        n_ref[...] = jnp.zeros(n_ref.shape, F32) + walked.astype(F32)
        if rider is not None:
            rider.wait_at_last(ids, (nq,), ride)

    wide = pairs * LANES
    in_specs = [pl.BlockSpec((tq, wide), lambda i: (i, 0)),
                pl.BlockSpec((T, wide), lambda i: (0, 1), pipeline_mode=pl.Buffered(1)),
                pl.BlockSpec((T, wide), lambda i: (0, 2), pipeline_mode=pl.Buffered(1)),
                pl.BlockSpec(after_m.shape, lambda i: (0, 0), pipeline_mode=pl.Buffered(1)),
                pl.BlockSpec(causal_m.shape, lambda i: (0, 0, 0), pipeline_mode=pl.Buffered(1))]
    out_specs = [pl.BlockSpec((tq, wide), lambda i: (i, 0)),
                 pl.BlockSpec((2 * pairs, nkb, 1, tq), lambda i: (0, 0, 0, i)),
                 pl.BlockSpec((1, 8, LANES), lambda i: (i, 0, 0))]
    out_shape = [jax.ShapeDtypeStruct((T, SB_WIDTH), BF16), jax.ShapeDtypeStruct((2 * pairs, nkb, 1, T), F32),
                 jax.ShapeDtypeStruct((nq, 8, LANES), F32)]
    args = (h_a, h_a, h_a, after_m, causal_m)
    slots = pairs * (per_q + SB_GROUP)
    scratch = [pltpu.VMEM((slots, 2 * LANES, tq), F32), pltpu.VMEM((slots, 2 * LANES, tq), F32),
               pltpu.VMEM((slots, 4 * LANES, tq), BF16), pltpu.VMEM((slots, 2 * LANES, tq), BF16)]
    if rider is not None:
        in_specs, args = in_specs + rider.specs, args + tuple(rider.bufs)
        out_specs, out_shape = out_specs + rider.specs, out_shape + rider.out_shape
        scratch = scratch + rider.scratch
    outs = pl.pallas_call(
        body,
        name="sb_fwd",
        grid=(nq,),
        in_specs=in_specs,
        out_specs=out_specs,
        out_shape=out_shape,
        scratch_shapes=scratch,
        compiler_params=_params(("arbitrary",)),
    )(*args)
    return outs[0], (outs[1], outs[2]), list(outs[3:])


def _sb_bwd(h_a, d_out, saved, rider=None):
    r_mat, walked_blocks = saved
    T = h_a.shape[0]
    tq = _pick(T, (SB_Q_BLOCK, SB_KEY_BLOCK))
    nq, per_q, nkb = T // tq, tq // SB_KEY_BLOCK, T // SB_KEY_BLOCK
    n_strips = 2 * LANES // SB_STRIP
    after_m, before_m = _sb_scan_matrices()
    causal_m = _sb_causal_masks(tq)
    pairs = SB_BWD_PAIRS
    groups = 4 // pairs
    n_ride = rider.n if rider is not None else 0

    def body(*refs):
        q_ref, k_ref, v_ref, do_ref, r_ref, n_ref, after_ref, before_ref, causal_ref = refs[:9]
        dq_ref, dk_ref, dv_ref = refs[9 + n_ride:12 + n_ride]
        z_ref, lb_ref, split_ref, w_ref, da_ref, dz_ref = refs[12 + 2 * n_ride:18 + 2 * n_ride]
        ids = [pl.program_id(0), pl.program_id(1)]
        if rider is not None:
            ride = (refs[9:9 + n_ride], refs[12 + n_ride:12 + 2 * n_ride], refs[-3:])
            rider.start_at_first(ids, ride)
        i = ids[1]

        @pl.when(i == 0)
        def _():
            dk_ref[...] = jnp.zeros_like(dk_ref)
            dv_ref[...] = jnp.zeros_like(dv_ref)

        scale = SB_HEAD_DIM ** -0.5
        q = [q_ref[:, _pair_lanes(p)] for p in range(pairs)]
        d_o = [do_ref[:, _pair_lanes(p)] for p in range(pairs)]
        q_t = [(x.astype(F32).T * scale).astype(BF16) for x in q]
        do_t = [x.astype(F32).T.astype(BF16) for x in d_o]
        lane_is_a = lax.broadcasted_iota(jnp.int32, (SB_KEY_BLOCK, LANES), 1) < SB_HEAD_DIM

        def tiles(kbs, diags, carry):
            nb = len(kbs)
            lanes = [_sb_lanes(tq, d) for d in diags]
            cols = [slice(first, first + width) for first, width in lanes]
            dq_t, ca, cb = [list(c) for c in carry]
            ks = [pl.multiple_of(kb * SB_KEY_BLOCK, SB_KEY_BLOCK) for kb in kbs]
            slot = lambda p, b: p * nb + b

            def causal(b, s):
                return causal_ref[diags[b], pl.ds((s * SB_STRIP) % SB_KEY_BLOCK, SB_STRIP), cols[b]]

            kk, vv = {}, {}
            for b in range(nb):
                for p in range(pairs):
                    kk[p, b] = _pair_rows(k_ref[pl.ds(ks[b], SB_KEY_BLOCK), _pair_lanes(p)], lane_is_a)
                    vv[p, b] = _pair_rows(v_ref[pl.ds(ks[b], SB_KEY_BLOCK), _pair_lanes(p)], lane_is_a)
                    z_ref[slot(p, b), :, cols[b]] = _dot(kk[p, b], q_t[p][:, cols[b]], _NN)
            for b in range(nb):
                for p in range(pairs):
                    for s in range(n_strips):
                        rows = pl.ds(s * SB_STRIP, SB_STRIP)
                        log_rem, log_beta = _sb_log_terms(z_ref[slot(p, b), rows, cols[b]])
                        lb_ref[slot(p, b), rows, cols[b]] = log_beta
                        if isinstance(diags[b], int):
                            log_rem = log_rem * causal(b, s)
                        _sb_store_split(split_ref.at[slot(p, b)], s, log_rem, cols[b])
            for b in range(nb):
                for p in range(pairs):
                    z_ref[slot(p, b), :, cols[b]] = _dot(after_ref[...], split_ref[slot(p, b), :, cols[b]], _NN)
                    da_ref[slot(p, b), :, cols[b]] = _dot(vv[p, b], do_t[p][:, cols[b]], _NN)
            sums = {}
            for b in range(nb):
                for p in range(pairs):
                    part = [jnp.zeros((8, lanes[b][1]), F32), jnp.zeros((8, lanes[b][1]), F32)]
                    for s in range(n_strips):
                        rows = pl.ds(s * SB_STRIP, SB_STRIP)
                        start = r_ref[2 * p + (s * SB_STRIP) // SB_KEY_BLOCK, kbs[b]][:, cols[b]]
                        w = jnp.exp(lb_ref[slot(p, b), rows, cols[b]] + z_ref[slot(p, b), rows, cols[b]] + start)
                        if isinstance(diags[b], int):
                            w = w * causal(b, s)
                        w_ref[slot(p, b), rows, cols[b]] = w.astype(BF16)
                        da = da_ref[slot(p, b), rows, cols[b]] * w
                        da_ref[slot(p, b), rows, cols[b]] = da
                        _sb_store_split(split_ref.at[slot(p, b)], s, da, cols[b])
                        head = (s * SB_STRIP) // SB_KEY_BLOCK
                        part[head] = part[head] + jnp.sum(da.reshape(SB_STRIP // 8, 8, lanes[b][1]), axis=0)
                    sums[p, b] = part
            for b in range(nb):
                for p in range(pairs):
                    z_ref[slot(p, b), :, cols[b]] = _dot(before_ref[...], split_ref[slot(p, b), :, cols[b]], _NN)
            for b in range(nb):
                for p in range(pairs):
                    for s in range(n_strips):
                        rows = pl.ds(s * SB_STRIP, SB_STRIP)
                        base = (ca[p] if (s * SB_STRIP) < SB_KEY_BLOCK else cb[p])[:, cols[b]]
                        sig = jnp.exp(lb_ref[slot(p, b), rows, cols[b]])
                        dz = (da_ref[slot(p, b), rows, cols[b]] * (1.0 - sig)
                              - (z_ref[slot(p, b), rows, cols[b]] + base) * sig)
                        if isinstance(diags[b], int):
                            dz = dz * causal(b, s)
                        dz_ref[slot(p, b), rows, cols[b]] = (dz * scale).astype(BF16)
                    ca[p] = _lane_add(ca[p], jnp.sum(sums[p, b][0], axis=0, keepdims=True), lanes[b])
                    cb[p] = _lane_add(cb[p], jnp.sum(sums[p, b][1], axis=0, keepdims=True), lanes[b])
            for b in range(nb):
                for p in range(pairs):
                    dq_t[p] = _lane_add(dq_t[p], _dot(kk[p, b], dz_ref[slot(p, b), :, cols[b]], _TN), lanes[b])
                    dkk = _dot(dz_ref[slot(p, b), :, cols[b]], q[p][cols[b], :], _NN)
                    dvv = _dot(w_ref[slot(p, b), :, cols[b]], d_o[p][cols[b], :], _NN)
                    here = (pl.ds(ks[b], SB_KEY_BLOCK), _pair_lanes(p))
                    dk_ref[here] += jnp.where(lane_is_a, dkk[:SB_KEY_BLOCK], dkk[SB_KEY_BLOCK:])
                    dv_ref[here] += jnp.where(lane_is_a, dvv[:SB_KEY_BLOCK], dvv[SB_KEY_BLOCK:])
            return tuple(dq_t), tuple(ca), tuple(cb)

        n_full = i * per_q
        groups_walked = jnp.clip(jnp.max(n_ref[...]).astype(jnp.int32), 0, n_full) // SB_GROUP
        carry = (tuple(jnp.zeros((LANES, tq), F32) for _ in range(pairs)),
                 tuple(jnp.zeros((1, tq), F32) for _ in range(pairs)),
                 tuple(jnp.zeros((1, tq), F32) for _ in range(pairs)))

        def below(j, c):
            kbs = [n_full - (groups_walked - j) * SB_GROUP + b for b in range(SB_GROUP)]
            starts = [r_ref[h, kbs[-1]][:, tq // 2:] for h in range(2 * pairs)]
            reaches = jnp.max(functools.reduce(jnp.maximum, starts)) > SB_DEAD_LOG
            return lax.cond(reaches, lambda cc: tiles(kbs, [None] * SB_GROUP, cc),
                            lambda cc: tiles(kbs, ["left"] * SB_GROUP, cc), c)

        carry = lax.fori_loop(0, groups_walked, below, carry)
        own = list(range(per_q))
        carry = tiles([i * per_q + d for d in own], own, carry)
        for p in range(pairs):
            dq_ref[:, _pair_lanes(p)] = carry[0][p].T.astype(BF16)
        if rider is not None:
            rider.wait_at_last(ids, (groups, nq), ride)

    wide = pairs * LANES
    mat = pl.BlockSpec(after_m.shape, lambda g, i: (0, 0), pipeline_mode=pl.Buffered(1))
    in_specs = [pl.BlockSpec((tq, wide), lambda g, i: (i, g)),
                pl.BlockSpec((T, wide), lambda g, i: (0, groups + g), pipeline_mode=pl.Buffered(1)),
                pl.BlockSpec((T, wide), lambda g, i: (0, 2 * groups + g), pipeline_mode=pl.Buffered(1)),
                pl.BlockSpec((tq, wide), lambda g, i: (i, g)),
                pl.BlockSpec((2 * pairs, nkb, 1, tq), lambda g, i: (g, 0, 0, i)),
                pl.BlockSpec((1, 8, LANES), lambda g, i: (i, 0, 0)),
                mat, mat,
                pl.BlockSpec(causal_m.shape, lambda g, i: (0, 0, 0), pipeline_mode=pl.Buffered(1))]
    out_specs = [pl.BlockSpec((tq, wide), lambda g, i: (i, g)),
                 pl.BlockSpec((T, wide), lambda g, i: (0, g)),
                 pl.BlockSpec((T, wide), lambda g, i: (0, g))]
    out_shape = [jax.ShapeDtypeStruct((T, SB_WIDTH), BF16), jax.ShapeDtypeStruct((T, SB_WIDTH), F32),
                 jax.ShapeDtypeStruct((T, SB_WIDTH), F32)]
    args = (h_a, h_a, h_a, d_out, r_mat, walked_blocks, after_m, before_m, causal_m)
    slots = pairs * max(per_q, SB_GROUP)
    scratch = [pltpu.VMEM((slots, 2 * LANES, tq), F32), pltpu.VMEM((slots, 2 * LANES, tq), F32),
               pltpu.VMEM((slots, 4 * LANES, tq), BF16), pltpu.VMEM((slots, 2 * LANES, tq), BF16),
               pltpu.VMEM((slots, 2 * LANES, tq), F32), pltpu.VMEM((slots, 2 * LANES, tq), BF16)]
    if rider is not None:
        in_specs, args = in_specs + rider.specs, args + tuple(rider.bufs)
        out_specs, out_shape = out_specs + rider.specs, out_shape + rider.out_shape
        scratch = scratch + rider.scratch
    outs = pl.pallas_call(
        body,
        name="sb_bwd",
        grid=(groups, nq),
        in_specs=in_specs,
        out_specs=out_specs,
        out_shape=out_shape,
        scratch_shapes=scratch,
        compiler_params=_params(("arbitrary", "arbitrary") if rider is not None else ("parallel", "arbitrary")),
    )(*args)
    return outs[0], outs[1], outs[2], list(outs[3:])


def _ret_tables(T):
    half = RET_QK_DIM // 2
    inv = 1.0 / (ROPE_BASE ** (jnp.arange(half, dtype=F32) / half))
    ang = jnp.arange(T, dtype=F32)[:, None] * inv[None, :]
    cos, sin = jnp.cos(ang), jnp.sin(ang)
    cos_t = jnp.concatenate([cos, cos], axis=1)
    sin_t = jnp.concatenate([-sin, sin], axis=1)
    log_gamma = jnp.log1p(-jnp.exp2(-5.0 - jnp.arange(RET_HEADS, dtype=F32)))
    idx = jnp.arange(RET_CHUNK, dtype=F32)
    rel = idx[:, None] - idx[None, :]
    decay = jnp.where(rel[None] >= 0, jnp.exp(log_gamma[:, None, None] * jnp.maximum(rel, 0.0)[None]), 0.0)
    k_decay = jnp.exp(log_gamma[None, :] * (RET_CHUNK - 1.0 - idx)[:, None])
    q_decay = jnp.exp(log_gamma[None, :] * (idx + 1.0)[:, None])
    chunk_decay = jnp.exp(log_gamma * RET_CHUNK)
    k_dec = jnp.broadcast_to(k_decay.T[:, :, None], (RET_HEADS, RET_CHUNK, LANES))
    q_dec = jnp.broadcast_to(q_decay.T[:, :, None], (RET_HEADS, RET_CHUNK, LANES))
    c_dec = jnp.broadcast_to(chunk_decay[:, None, None], (RET_HEADS, 8, LANES))
    return cos_t, sin_t, decay, k_dec, q_dec, c_dec


def _rotary(x, cos_t, sin_t):
    return x * cos_t + pltpu.roll(x, RET_QK_DIM // 2, 1) * sin_t


def _rotary_transpose(dy, cos_t, sin_t):
    return dy * cos_t + pltpu.roll(dy * sin_t, RET_QK_DIM // 2, 1)


def _head_norm(o):
    mu = jnp.mean(o, axis=1, keepdims=True)
    cen = o - mu
    var = jnp.mean(cen * cen, axis=1, keepdims=True)
    rstd = lax.rsqrt(var + LN_EPS)
    return cen * rstd, rstd


def _ret_specs(steps, per_step, reverse):
    def n_of(n):
        return (steps - 1 - n) if reverse else n

    rows = per_step * RET_CHUNK
    q_spec = pl.BlockSpec((rows, RET_QK_WIDTH), lambda n: (n_of(n), 0))
    k_spec = pl.BlockSpec((rows, RET_QK_WIDTH), lambda n: (n_of(n), 1))
    vv = pl.BlockSpec((rows, RET_V_WIDTH), lambda n: (n_of(n), 0))
    pos = pl.BlockSpec((rows, LANES), lambda n: (n_of(n), 0))
    per_head = pl.BlockSpec((RET_HEADS, RET_CHUNK, LANES), lambda n: (0, 0, 0))
    c_dec = pl.BlockSpec((RET_HEADS, 8, LANES), lambda n: (0, 0, 0))
    state = pl.BlockSpec((RET_HEADS, per_step, RET_QK_DIM, RET_V_DIM), lambda n: (0, n_of(n), 0, 0))
    return q_spec, k_spec, vv, pos, per_head, c_dec, state


def _qk_cols(h):
    return slice(h * RET_QK_DIM, (h + 1) * RET_QK_DIM)


def _v_cols(h):
    return slice(h * RET_V_DIM, (h + 1) * RET_V_DIM)


def _ret_fwd(h_b, h_c, h_d, tables):
    T = h_b.shape[0]
    nc = T // RET_CHUNK
    per_step = _pick(nc, (RET_STEP_CHUNKS, 1))
    steps = nc // per_step
    q_spec, k_spec, vv, pos, per_head, c_dec, state = _ret_specs(steps, per_step, False)

    def body(q_ref, k_ref, v_ref, g_ref, cos_ref, sin_ref, dec_ref, kd_ref, qd_ref, cd_ref,
             y_ref, o_ref, st_ref, state_ref):
        @pl.when(pl.program_id(0) == 0)
        def _():
            state_ref[...] = jnp.zeros_like(state_ref)

        for c in range(per_step):
            rows = pl.ds(c * RET_CHUNK, RET_CHUNK)
            cos_t, sin_t = cos_ref[rows, :], sin_ref[rows, :]
            for h in range(RET_HEADS):
                q = _rotary(q_ref[rows, _qk_cols(h)], cos_t, sin_t) * (RET_QK_DIM ** -0.5)
                k = _rotary(k_ref[rows, _qk_cols(h)], cos_t, sin_t)
                v = v_ref[rows, _v_cols(h)]
                prev = state_ref[h]
                scores = _dot(q.astype(BF16), k.astype(BF16), _NT) * dec_ref[h]
                inner = _dot(scores.astype(BF16), v, _NN)
                cross = _dot((q * qd_ref[h]).astype(BF16), prev.astype(BF16), _NN)
                o = inner + cross
                st_ref[h, c] = prev
                kv = _dot((k * kd_ref[h]).astype(BF16), v, _TN)
                state_ref[h] = prev * cd_ref[h, 0:1, 0:1] + kv
                o_ref[rows, _v_cols(h)] = o
                normed, _ = _head_norm(o)
                gate = g_ref[rows, _v_cols(h)]
                y_ref[rows, _v_cols(h)] = (gate * jax.nn.sigmoid(gate) * normed).astype(BF16)

    return pl.pallas_call(
        body,
        name="ret_fwd",
        grid=(steps,),
        in_specs=[q_spec, k_spec, vv, vv, pos, pos, per_head, per_head, per_head, c_dec],
        out_specs=[vv, vv, state],
        out_shape=[jax.ShapeDtypeStruct((T, RET_V_WIDTH), BF16),
                   jax.ShapeDtypeStruct((T, RET_V_WIDTH), F32),
                   jax.ShapeDtypeStruct((RET_HEADS, nc, RET_QK_DIM, RET_V_DIM), F32)],
        scratch_shapes=[pltpu.VMEM((RET_HEADS, RET_QK_DIM, RET_V_DIM), F32)],
        compiler_params=_params(("arbitrary",)),
    )(h_b, h_b, h_c, h_d, *tables)


def _ret_bwd(d_y, o_pre, states, h_b, h_c, h_d, tables, rider=None):
    T = h_b.shape[0]
    nc = T // RET_CHUNK
    per_step = _pick(nc, (RET_STEP_CHUNKS, 1))
    steps = nc // per_step
    q_spec, k_spec, vv, pos, per_head, c_dec, state = _ret_specs(steps, per_step, True)
    n_ride = rider.n if rider is not None else 0

    def body(*refs):
        (dy_ref, o_ref, st_ref, q_ref, k_ref, v_ref, g_ref, cos_ref, sin_ref, dec_ref, kd_ref, qd_ref,
         cd_ref) = refs[:13]
        dq_ref, dk_ref, dv_ref, dg_ref = refs[13 + n_ride:17 + n_ride]
        carry_ref = refs[17 + 2 * n_ride]
        ids = [pl.program_id(0)]
        if rider is not None:
            ride = (refs[13:13 + n_ride], refs[17 + n_ride:17 + 2 * n_ride], refs[-3:])
            rider.start_at_first(ids, ride)

        @pl.when(ids[0] == 0)
        def _():
            carry_ref[...] = jnp.zeros_like(carry_ref)

        scale = RET_QK_DIM ** -0.5
        for c in reversed(range(per_step)):
            rows = pl.ds(c * RET_CHUNK, RET_CHUNK)
            cos_t, sin_t = cos_ref[rows, :], sin_ref[rows, :]
            for h in range(RET_HEADS):
                q = _rotary(q_ref[rows, _qk_cols(h)], cos_t, sin_t) * scale
                k = _rotary(k_ref[rows, _qk_cols(h)], cos_t, sin_t)
                v = v_ref[rows, _v_cols(h)]
                decay, k_dec, q_dec = dec_ref[h], kd_ref[h], qd_ref[h]
                chunk_decay = cd_ref[h, 0:1, 0:1]
                state = st_ref[h, c].astype(BF16)
                later = carry_ref[h]
                later_b = later.astype(BF16)

                gate = g_ref[rows, _v_cols(h)]
                sig = jax.nn.sigmoid(gate)
                silu = gate * sig
                normed, rstd = _head_norm(o_ref[rows, _v_cols(h)])
                d_y = dy_ref[rows, _v_cols(h)]
                dg_ref[rows, _v_cols(h)] = (d_y * normed * (sig * (1.0 + gate * (1.0 - sig)))).astype(BF16)
                d_n = d_y * silu
                d_o = rstd * (d_n - jnp.mean(d_n, axis=1, keepdims=True)
                              - normed * jnp.mean(d_n * normed, axis=1, keepdims=True))
                d_ob = d_o.astype(BF16)

                qb, kb = q.astype(BF16), k.astype(BF16)
                qd_b, kd_b = (q * q_dec).astype(BF16), (k * k_dec).astype(BF16)
                scores = _dot(qb, kb, _NT) * decay
                d_scores = (_dot(d_ob, v, _NT) * decay).astype(BF16)
                dq = _dot(d_scores, kb, _NN) + _dot(d_ob, state, _NT) * q_dec
                dk = _dot(d_scores, qb, _TN) + _dot(v, later_b, _NT) * k_dec
                dv = _dot(scores.astype(BF16), d_ob, _TN) + _dot(kd_b, later_b, _NN)
                carry_ref[h] = _dot(qd_b, d_ob, _TN) + chunk_decay * later
                dq_ref[rows, _qk_cols(h)] = _rotary_transpose(dq * scale, cos_t, sin_t).astype(BF16)
                dk_ref[rows, _qk_cols(h)] = _rotary_transpose(dk, cos_t, sin_t).astype(BF16)
                dv_ref[rows, _v_cols(h)] = dv.astype(BF16)
        if rider is not None:
            rider.wait_at_last(ids, (steps,), ride)

    qk_out = pl.BlockSpec((per_step * RET_CHUNK, RET_QK_WIDTH), lambda n: (steps - 1 - n, 0))
    in_specs = [vv, vv, state, q_spec, k_spec, vv, vv, pos, pos, per_head, per_head, per_head, c_dec]
    out_specs = [qk_out, qk_out, vv, vv]
    out_shape = [jax.ShapeDtypeStruct((T, RET_QK_WIDTH), BF16), jax.ShapeDtypeStruct((T, RET_QK_WIDTH), BF16),
                 jax.ShapeDtypeStruct((T, RET_V_WIDTH), BF16), jax.ShapeDtypeStruct((T, RET_V_WIDTH), BF16)]
    args = (d_y, o_pre, states, h_b, h_b, h_c, h_d) + tuple(tables)
    scratch = [pltpu.VMEM((RET_HEADS, RET_QK_DIM, RET_V_DIM), F32)]
    if rider is not None:
        in_specs, args = in_specs + rider.specs, args + tuple(rider.bufs)
        out_specs, out_shape = out_specs + rider.specs, out_shape + rider.out_shape
        scratch = scratch + rider.scratch
    outs = pl.pallas_call(
        body,
        name="ret_bwd",
        grid=(steps,),
        in_specs=in_specs,
        out_specs=out_specs,
        out_shape=out_shape,
        scratch_shapes=scratch,
        compiler_params=_params(("arbitrary",)),
    )(*args)
    return outs[0], outs[1], outs[2], outs[3], list(outs[4:])


def _proj_tiles(h, x):
    return h[:, 0:1536], h[:, 1536:2560], h[:, 2560:3584], h[:, 3584:4608], h[:, 4608:6656], x


def _gate_mix_tiles(y_ret, h_e, b_gate, y_sb):
    gates = jax.nn.sigmoid(h_e + b_gate)
    return y_ret, gates[:, :D_MODEL] * y_sb + gates[:, D_MODEL:] * y_ret


def _gate_mix_grad_tiles(d_mix, h_e, b_gate, y_sb, y_ret):
    gates = jax.nn.sigmoid(h_e + b_gate)
    g0, g1 = gates[:, :D_MODEL], gates[:, D_MODEL:]
    d_e = jnp.concatenate([d_mix * y_sb * g0 * (1.0 - g0), d_mix * y_ret * g1 * (1.0 - g1)], axis=1)
    return d_mix * g0, d_mix * g1, d_e, d_e


def _ln_stats(u):
    mu = jnp.mean(u, axis=1, keepdims=True)
    cen = u - mu
    var = jnp.mean(cen * cen, axis=1, keepdims=True)
    rstd = lax.rsqrt(var + LN_EPS)
    return cen * rstd, rstd


def _ln_input_grad(d_out, gain, xhat, rstd):
    d_hat = d_out * gain
    return rstd * (d_hat - jnp.mean(d_hat, axis=1, keepdims=True)
                   - xhat * jnp.mean(d_hat * xhat, axis=1, keepdims=True))


def _ln_tiles(sub, x_prev, gain, bias):
    xhat, rstd = _ln_stats(DN_ALPHA * x_prev + sub)
    out = xhat * gain + bias
    return out, out, xhat, rstd


def _residual_tiles(d_sub, res):
    return (d_sub + DN_ALPHA * res,)


def _ln_grad_tiles(d_sub, res, xhat, rstd, gain):
    d_out = d_sub + DN_ALPHA * res
    du = _ln_input_grad(d_out, gain, xhat, rstd)
    return du, du, d_out * xhat, d_out


def _ln_loss_tiles(sub, x_prev, gain, bias, target):
    xhat, rstd = _ln_stats(DN_ALPHA * x_prev + sub)
    diff = xhat * gain + bias - target
    d_out = diff * (1.0 / D_MODEL)
    du = _ln_input_grad(d_out, gain, xhat, rstd)
    return du, du, diff * diff, d_out * xhat, d_out


def _mem_probs(q_h, k_h):
    s = _dot(q_h, k_h, _NT) * (MEM_HEAD_DIM ** -0.5)
    e = jnp.exp(s - jnp.max(s, axis=1, keepdims=True))
    return e / jnp.sum(e, axis=1, keepdims=True)


def _xattn_fwd(q, kv):
    T, mem_len = q.shape[0], kv.shape[0]
    tq = _pick(T, (512, 256, 128))

    def body(q_ref, kv_ref, o_ref):
        for h in range(MEM_HEADS):
            cols = slice(h * MEM_HEAD_DIM, (h + 1) * MEM_HEAD_DIM)
            vcols = slice(D_MODEL + h * MEM_HEAD_DIM, D_MODEL + (h + 1) * MEM_HEAD_DIM)
            p = _mem_probs(q_ref[:, cols], kv_ref[:, cols])
            o_ref[:, cols] = _dot(p.astype(BF16), kv_ref[:, vcols], _NN).astype(BF16)

    return pl.pallas_call(
        body,
        name="xattn_fwd",
        grid=(T // tq,),
        in_specs=[pl.BlockSpec((tq, D_MODEL), lambda i: (i, 0)),
                  pl.BlockSpec((mem_len, 2 * D_MODEL), lambda i: (0, 0))],
        out_specs=pl.BlockSpec((tq, D_MODEL), lambda i: (i, 0)),
        out_shape=jax.ShapeDtypeStruct((T, D_MODEL), BF16),
        compiler_params=_params(("parallel",)),
    )(q, kv)


def _xattn_bwd(q, kv, d_o):
    T, mem_len = q.shape[0], kv.shape[0]
    tq = _pick(T, (512, 256, 128))

    def body(q_ref, kv_ref, do_ref, dq_ref, dkv_ref):
        @pl.when(pl.program_id(0) == 0)
        def _():
            dkv_ref[...] = jnp.zeros_like(dkv_ref)

        for h in range(MEM_HEADS):
            cols = slice(h * MEM_HEAD_DIM, (h + 1) * MEM_HEAD_DIM)
            vcols = slice(D_MODEL + h * MEM_HEAD_DIM, D_MODEL + (h + 1) * MEM_HEAD_DIM)
            q_h, k_h, do_h = q_ref[:, cols], kv_ref[:, cols], do_ref[:, cols]
            p = _mem_probs(q_h, k_h)
            dp = _dot(do_h, kv_ref[:, vcols], _NT)
            ds = p * (dp - jnp.sum(dp * p, axis=1, keepdims=True))
            dsb = (ds * (MEM_HEAD_DIM ** -0.5)).astype(BF16)
            dq_ref[:, cols] = _dot(dsb, k_h, _NN).astype(BF16)
            dkv_ref[:, cols] += _dot(dsb, q_h, _TN)
            dkv_ref[:, vcols] += _dot(p.astype(BF16), do_h, _TN)

    row = pl.BlockSpec((tq, D_MODEL), lambda i: (i, 0))
    full = pl.BlockSpec((mem_len, 2 * D_MODEL), lambda i: (0, 0))
    return pl.pallas_call(
        body,
        name="xattn_bwd",
        grid=(T // tq,),
        in_specs=[row, full, row],
        out_specs=[row, full],
        out_shape=[jax.ShapeDtypeStruct((T, D_MODEL), BF16), jax.ShapeDtypeStruct((mem_len, 2 * D_MODEL), F32)],
        compiler_params=_params(("arbitrary",)),
    )(q, kv, d_o)


def _swiglu_tiles(f):
    a, b = f[:, :FFN_HIDDEN], f[:, FFN_HIDDEN:]
    return f, a * jax.nn.sigmoid(a) * b


def _swiglu_grad_tiles(d_hidden, f):
    a, b = f[:, :FFN_HIDDEN], f[:, FFN_HIDDEN:]
    sig = jax.nn.sigmoid(a)
    return (jnp.concatenate([d_hidden * b * (sig * (1.0 + a * (1.0 - sig))), d_hidden * (a * sig)], axis=1),)


def _local_step(x, mem, w_in, small, target, fetch, ship):
    T = x.shape[0]
    tables = _ret_tables(T)
    memb = mem.astype(BF16)

    (h_a, h_b, h_c, h_d, h_e, xb), w_ffn = fetch(
        ("w_ffn_in", "w_ffn_out"),
        lambda rider: _as_host(rider, _mm_fused(
            x, w_in, mode="nn", name="proj_in", extras=[], pass_a=True,
            outs=[(1536, BF16), (1024, F32), (1024, BF16), (1024, F32), (2048, F32), (D_MODEL, BF16)],
            epilogue=_proj_tiles, max_rows=256, rider=rider)))
    (a_sb, r_mat), w_mix = fetch(("w_sb_o", "w_ret_o", "w_mix_o", "w_mem_q", "w_mem_kv", "w_mem_o"),
                                 lambda rider: _sb_fwd(h_a, rider))
    w = {**w_ffn, **w_mix}
    y_gated, o_pre, states = _ret_fwd(h_b, h_c, h_d, tables)
    y_sb = _mm(a_sb, w["w_sb_o"], mode="nn", out_dtype=F32, name="sb_out")
    row_f32, row_bf16 = (D_MODEL, F32), (D_MODEL, BF16)
    ln_outs = [row_f32, row_bf16, row_f32, (1, F32)]
    y_ret, mix_in = _mm_fused(y_gated, w["w_ret_o"], mode="nn", name="ret_out", extras=[h_e, small["b_gate"], y_sb],
                              outs=[row_f32, row_bf16], epilogue=_gate_mix_tiles)
    x1, x1b, xhat1, rstd1 = _mm_fused(mix_in, w["w_mix_o"], mode="nn", name="mix_out",
                                      extras=[x, small["ln1_g"], small["ln1_b"]], outs=ln_outs, epilogue=_ln_tiles)
    q_m = _mm(x1b, w["w_mem_q"], mode="nn", out_dtype=BF16, name="mem_q")
    kv_m = _mm(memb, w["w_mem_kv"], mode="nn", out_dtype=BF16, name="mem_kv")
    o_m = _xattn_fwd(q_m, kv_m)
    x2, x2b, xhat2, rstd2 = _mm_fused(o_m, w["w_mem_o"], mode="nn", name="mem_out",
                                      extras=[x1, small["ln2_g"], small["ln2_b"]], outs=ln_outs, epilogue=_ln_tiles)
    f, hidden = _mm_fused(x2b, w["w_ffn_in"], mode="nn", name="ffn_in", extras=[],
                          outs=[(2 * FFN_HIDDEN, F32), (FFN_HIDDEN, BF16)], epilogue=_swiglu_tiles)
    du_outs, col = [row_f32, row_bf16], D_MODEL
    du3, du3b, loss_cols, d_ln3_g, d_ln3_b = _mm_fused(
        hidden, w["w_ffn_out"], mode="nn", name="ffn_out", extras=[x2, small["ln3_g"], small["ln3_b"], target],
        outs=du_outs, sums=[col, col, col], epilogue=_ln_loss_tiles)

    g_ffn_out = _mm(hidden, du3b, mode="tn", out_dtype=BF16, name="g_ffn_out")
    (d_f,) = _mm_fused(du3b, w["w_ffn_out"], mode="nt", name="d_hidden", extras=[f],
                       outs=[(2 * FFN_HIDDEN, BF16)], epilogue=_swiglu_grad_tiles)
    g_ffn_in = _mm(x2b, d_f, mode="tn", out_dtype=BF16, name="g_ffn_in")
    du2, du2b, d_ln2_g, d_ln2_b = ship(
        {"w_ffn_out": g_ffn_out},
        lambda rider: _as_host(rider, _mm_fused(
            d_f, w["w_ffn_in"], mode="nt", name="d_x2", extras=[du3, xhat2, rstd2, small["ln2_g"]], outs=du_outs,
            sums=[col, col], epilogue=_ln_grad_tiles, rider=rider)))
    g_mem_o = _mm(o_m, du2b, mode="tn", out_dtype=BF16, name="g_mem_o")
    d_om = _mm(du2b, w["w_mem_o"], mode="nt", out_dtype=BF16, name="d_om")
    d_qm, d_kvm = _xattn_bwd(q_m, kv_m, d_om)
    g_mem_q = _mm(x1b, d_qm, mode="tn", out_dtype=BF16, name="g_mem_q")
    g_mem_kv = _mm(memb, d_kvm.astype(BF16), mode="tn", out_dtype=BF16, name="g_mem_kv")
    du1, du1b, d_ln1_g, d_ln1_b = _mm_fused(
        d_qm, w["w_mem_q"], mode="nt", name="d_x1", extras=[du2, xhat1, rstd1, small["ln1_g"]], outs=du_outs,
        sums=[col, col], epilogue=_ln_grad_tiles)
    g_mix_o = _mm(mix_in, du1b, mode="tn", out_dtype=BF16, name="g_mix_o")
    d_ysb, d_yret, d_e, d_b_gate = _mm_fused(
        du1b, w["w_mix_o"], mode="nt", name="d_mix_in", extras=[h_e, small["b_gate"], y_sb, y_ret],
        outs=[row_bf16, row_bf16, (2 * D_MODEL, BF16)], sums=[2 * D_MODEL], epilogue=_gate_mix_grad_tiles)
    g_sb_o = _mm(a_sb, d_ysb, mode="tn", out_dtype=BF16, name="g_sb_o")
    g_ret_o = _mm(y_gated, d_yret, mode="tn", out_dtype=BF16, name="g_ret_o")
    d_asb = _mm(d_ysb, w["w_sb_o"], mode="nt", out_dtype=BF16, name="d_asb")
    d_ygated = _mm(d_yret, w["w_ret_o"], mode="nt", out_dtype=F32, name="d_ygated")
    small_grads = {"b_gate": d_b_gate, "ln1_g": d_ln1_g, "ln1_b": d_ln1_b, "ln2_g": d_ln2_g, "ln2_b": d_ln2_b,
                   "ln3_g": d_ln3_g, "ln3_b": d_ln3_b, "loss_cols": loss_cols}
    d_rq, d_rk, d_c, d_d = ship({"w_mem_kv": g_mem_kv, "w_mem_q": g_mem_q, "w_mem_o": g_mem_o, "w_mix_o": g_mix_o},
                                lambda rider: _ret_bwd(d_ygated, o_pre, states, h_b, h_c, h_d, tables, rider))
    d_q, d_k, d_v = ship({"w_ffn_in": g_ffn_in, "w_ret_o": g_ret_o, "w_sb_o": g_sb_o, "small": small_grads},
                         lambda rider: _sb_bwd(h_a, d_asb, r_mat, rider))
    d_h = [("sb_q", d_q), ("sb_k", d_k), ("sb_v", d_v), ("ret_q", d_rq), ("ret_k", d_rk), ("ret_v", d_c),
           ("ret_g", d_d), ("gate", d_e)]
    g_in = jnp.concatenate([_mm(xb, piece, mode="tn", out_dtype=BF16, name="g_in_" + tag) for tag, piece in d_h],
                           axis=1)
    (d_x,) = ship({"w_in": g_in},
                  lambda rider: _as_host(rider, _mm_fused(
                      [piece for _, piece in d_h], w_in, mode="nt", name="d_x", extras=[du1], outs=[(D_MODEL, F32)],
                      epilogue=_residual_tiles, rider=rider)))
    return d_x


def _adamw_math(w, g, m, v):
    m = ADAM_B1 * m + (1.0 - ADAM_B1) * g
    v = ADAM_B2 * v + (1.0 - ADAM_B2) * jnp.square(g)
    m_hat = m / (1.0 - ADAM_B1 ** ADAM_STEP)
    v_hat = v / (1.0 - ADAM_B2 ** ADAM_STEP)
    delta = -ADAM_LR * (m_hat / (jnp.sqrt(v_hat) + ADAM_EPS) + ADAM_WD * w)
    return delta, m, v


def _adamw(parts, w, m, v, name):
    R, C = w.shape
    tr = max(t for t in range(16, min(R, 256) + 1, 16) if R % t == 0) if R >= 16 else R

    def body(p_ref, w_ref, m_ref, v_ref, g_ref, d_ref, nm_ref, nv_ref):
        g = p_ref[0].astype(F32)
        for j in range(1, N_DEV):
            g = g + p_ref[j].astype(F32)
        delta, nm, nv = _adamw_math(w_ref[...], g, m_ref[...], v_ref[...])
        g_ref[...] = g
        d_ref[...] = delta
        nm_ref[...] = nm
        nv_ref[...] = nv

    blk = pl.BlockSpec((tr, C), lambda i: (i, 0))
    out = jax.ShapeDtypeStruct((R, C), F32)
    return pl.pallas_call(
        body,
        name=name,
        grid=(R // tr,),
        in_specs=[pl.BlockSpec((N_DEV, tr, C), lambda i: (0, i, 0)), blk, blk, blk],
        out_specs=[blk] * 4,
        out_shape=[out] * 4,
        compiler_params=_params(("parallel",)),
    )(parts, w, m, v)


_SHARD_AXIS = {"w_in": 1, "w_sb_o": 1, "w_ret_o": 0, "w_mix_o": 0, "w_mem_q": 0, "w_mem_kv": 1, "w_mem_o": 0,
               "w_ffn_in": 1, "w_ffn_out": 0}
_MATRICES = tuple(_SHARD_AXIS)
_SMALL = ("b_gate", "ln1_g", "ln1_b", "ln2_g", "ln2_b", "ln3_g", "ln3_b")
_WEIGHT_ORDER = ("w_in", "b_gate", "w_sb_o", "w_ret_o", "w_mix_o", "ln1_g", "ln1_b", "w_mem_q", "w_mem_kv", "w_mem_o",
                 "ln2_g", "ln2_b", "w_ffn_in", "w_ffn_out", "ln3_g", "ln3_b")


def _assemble(name, gathered):
    if _SHARD_AXIS[name] == 0:
        return gathered.reshape(-1, gathered.shape[2])
    return jnp.transpose(gathered, (1, 0, 2)).reshape(gathered.shape[1], -1)


def _to_slots(name, full):
    if _SHARD_AXIS[name] == 0:
        return full.reshape(N_DEV, full.shape[0] // N_DEV, full.shape[1])
    return jnp.transpose(full.reshape(full.shape[0], N_DEV, full.shape[1] // N_DEV), (1, 0, 2))


SMALL_ROWS = 16


def _pack_small(vals):
    return jnp.concatenate([vals["b_gate"].reshape(2, D_MODEL)] + [vals[n] for n in _SMALL[1:]], axis=0)


def _unpack_small(packed):
    out = {"b_gate": packed[0:2].reshape(1, 2 * D_MODEL)}
    for i, n in enumerate(_SMALL[1:]):
        out[n] = packed[2 + i:3 + i]
    return out


def kernel(x, mem, w_in, b_gate, w_sb_o, w_ret_o, w_mix_o, ln1_g, ln1_b, w_mem_q, w_mem_kv, w_mem_o, ln2_g, ln2_b, w_ffn_in, w_ffn_out, ln3_g, ln3_b, loss_target, m_w_in, m_b_gate, m_w_sb_o, m_w_ret_o, m_w_mix_o, m_ln1_g, m_ln1_b, m_w_mem_q, m_w_mem_kv, m_w_mem_o, m_ln2_g, m_ln2_b, m_w_ffn_in, m_w_ffn_out, m_ln3_g, m_ln3_b, v_w_in, v_b_gate, v_w_sb_o, v_w_ret_o, v_w_mix_o, v_ln1_g, v_ln1_b, v_w_mem_q, v_w_mem_kv, v_w_mem_o, v_ln2_g, v_ln2_b, v_w_ffn_in, v_w_ffn_out, v_ln3_g, v_ln3_b):
    weights = dict(w_in=w_in, b_gate=b_gate, w_sb_o=w_sb_o, w_ret_o=w_ret_o, w_mix_o=w_mix_o, ln1_g=ln1_g, ln1_b=ln1_b,
                   w_mem_q=w_mem_q, w_mem_kv=w_mem_kv, w_mem_o=w_mem_o, ln2_g=ln2_g, ln2_b=ln2_b, w_ffn_in=w_ffn_in,
                   w_ffn_out=w_ffn_out, ln3_g=ln3_g, ln3_b=ln3_b)
    mom1 = dict(w_in=m_w_in, b_gate=m_b_gate, w_sb_o=m_w_sb_o, w_ret_o=m_w_ret_o, w_mix_o=m_w_mix_o, ln1_g=m_ln1_g,
                ln1_b=m_ln1_b, w_mem_q=m_w_mem_q, w_mem_kv=m_w_mem_kv, w_mem_o=m_w_mem_o, ln2_g=m_ln2_g, ln2_b=m_ln2_b,
                w_ffn_in=m_w_ffn_in, w_ffn_out=m_w_ffn_out, ln3_g=m_ln3_g, ln3_b=m_ln3_b)
    mom2 = dict(w_in=v_w_in, b_gate=v_b_gate, w_sb_o=v_w_sb_o, w_ret_o=v_w_ret_o, w_mix_o=v_w_mix_o, ln1_g=v_ln1_g,
                ln1_b=v_ln1_b, w_mem_q=v_w_mem_q, w_mem_kv=v_w_mem_kv, w_mem_o=v_w_mem_o, ln2_g=v_ln2_g, ln2_b=v_ln2_b,
                w_ffn_in=v_w_ffn_in, w_ffn_out=v_w_ffn_out, ln3_g=v_ln3_g, ln3_b=v_ln3_b)

    (gathered_in,) = _exchange([weights["w_in"][0].astype(BF16)], False, "gather_w_in")
    received = {}

    def fetch(names, host):
        res = host(_Rider([weights[n][0].astype(BF16) for n in names], False))
        return res[:-1], {n: _assemble(n, g) for n, g in zip(names, res[-1])}

    def ship(grads, host):
        names = list(grads)
        bufs = []
        for n in names:
            if n == "small":
                part = jnp.concatenate([_pack_small(grads[n]), grads[n]["loss_cols"],
                                        jnp.zeros((SMALL_ROWS - 9, D_MODEL), F32)], axis=0)
                bufs.append(jnp.broadcast_to(part[None], (N_DEV,) + part.shape))
            else:
                bufs.append(_to_slots(n, grads[n]).astype(BF16))
        res = host(_Rider(bufs, True))
        received.update(zip(names, res[-1]))
        return res[:-1]

    small = {n: weights[n] for n in _SMALL}
    d_x = _local_step(x[0], mem[0], _assemble("w_in", gathered_in), small, loss_target[0], fetch, ship)

    new = {}
    for n in _MATRICES:
        new[n] = _adamw(received[n], weights[n][0], mom1[n][0], mom2[n][0], "adamw_" + n)
    packed = _adamw(received["small"][:, :8], _pack_small({n: weights[n] for n in _SMALL}),
                    _pack_small({n: mom1[n] for n in _SMALL}), _pack_small({n: mom2[n] for n in _SMALL}), "adamw_small")
    small_new = [_unpack_small(p) for p in packed]
    loss = jnp.sum(received["small"][:, 8]) * (0.5 / D_MODEL)

    outs = [loss, d_x[None]]
    for slot in range(4):
        for n in _WEIGHT_ORDER:
            outs.append(new[n][slot][None] if n in new else small_new[slot][n])
    return tuple(outs)
```

```python
import functools
import math

import jax
import jax.numpy as jnp
from jax import lax
from jax.experimental import pallas as pl
from jax.experimental.pallas import tpu as pltpu

F32 = jnp.float32
BF16 = jnp.bfloat16

N_DEV = 8
D_MODEL = 1024
SB_HEAD_DIM = 64
SB_WIDTH = 512
RET_HEADS = 4
RET_QK_DIM = 128
RET_V_DIM = 256
RET_QK_WIDTH = 512
RET_V_WIDTH = 1024
RET_CHUNK = 128
RET_STEP_CHUNKS = 4
ROPE_BASE = 10000.0
MEM_HEADS = 4
MEM_HEAD_DIM = 256
FFN_HIDDEN = 2816
DN_ALPHA = 2.0 ** 0.25
LN_EPS = 1e-5
ADAM_LR = 0.001
ADAM_B1 = 0.9
ADAM_B2 = 0.999
ADAM_EPS = 1e-08
ADAM_WD = 0.01
ADAM_STEP = 10

VMEM_LIMIT_BYTES = 52 * 1024 * 1024
LANES = 128
SB_KEY_BLOCK = 128
SB_Q_BLOCK = 256
SB_DEAD_LOG = -105.0

MESH_AXES = ("x", "y", "c")


def _pick(dim, prefs):
    for p in prefs:
        if dim % p == 0:
            return p
    return dim


def _params(sem):
    return pltpu.CompilerParams(dimension_semantics=sem, vmem_limit_bytes=VMEM_LIMIT_BYTES)


def _dot(a, b, dims):
    return lax.dot_general(a, b, (dims, ((), ())), preferred_element_type=F32)


_NN = ((1,), (0,))
_NT = ((1,), (1,))
_TN = ((0,), (0,))


def _my_index():
    return 4 * lax.axis_index("x") + 2 * lax.axis_index("y") + lax.axis_index("c")


def _peer(k):
    x, y, c = lax.axis_index("x"), lax.axis_index("y"), lax.axis_index("c")
    bx, by, bc = (k >> 2) & 1, (k >> 1) & 1, k & 1
    px = (1 - x) if bx else x
    py = (1 - y) if by else y
    pc = (1 - c) if bc else c
    return (px, py, pc), 4 * px + 2 * py + pc


class _Rider:
    def __init__(self, bufs, scatter):
        self.bufs, self.scatter, self.n = list(bufs), scatter, len(bufs)
        self.specs = [pl.BlockSpec(memory_space=pl.ANY)] * self.n
        self.out_shape = [jax.ShapeDtypeStruct(b.shape if scatter else (N_DEV,) + b.shape, b.dtype) for b in self.bufs]
        self.scratch = [pltpu.SemaphoreType.DMA((self.n, N_DEV - 1)), pltpu.SemaphoreType.DMA((self.n, N_DEV - 1)),
                        pltpu.SemaphoreType.DMA((self.n,))]

    def _remote(self, ride, a, k, src_ref, slot, to):
        _, dst, (send_sems, recv_sems, _) = ride
        return pltpu.make_async_remote_copy(src_ref=src_ref, dst_ref=dst[a].at[slot], send_sem=send_sems.at[a, k],
                                            recv_sem=recv_sems.at[a, k], device_id=to,
                                            device_id_type=pl.DeviceIdType.MESH)

    def _local(self, ride, a):
        src, dst, (_, _, local_sems) = ride
        me = _my_index()
        return pltpu.make_async_copy(src[a].at[me] if self.scatter else src[a], dst[a].at[me], local_sems.at[a])

    def _direct(self, ride, a):
        src = ride[0]
        me = _my_index()
        out = []
        for k in range(1, N_DEV):
            peer, peer_idx = _peer(k)
            out.append(self._remote(ride, a, k - 1, src[a].at[peer_idx], me, peer))
        return out

    def _two_level(self, ride, a):
        src, dst = ride[0], ride[1]
        x, y, c = lax.axis_index("x"), lax.axis_index("y"), lax.axis_index("c")
        me, sibling = _my_index(), (x, y, 1 - c)
        chips = [(1 - x, y), (x, 1 - y), (1 - x, 1 - y)]
        first = [self._remote(ride, a, 0, src[a], me, sibling)]
        passed, landing = [], [self._remote(ride, a, 0, src[a], me + 1 - 2 * c, sibling)]
        for j, (px, py) in enumerate(chips):
            first.append(self._remote(ride, a, 1 + j, src[a], me, (px, py, c)))
            theirs = 4 * px + 2 * py + c
            passed.append(self._remote(ride, a, 4 + j, dst[a].at[theirs], theirs, sibling))
            landing.append(self._remote(ride, a, 1 + j, src[a], theirs, (px, py, c)))
        for j, (px, py) in enumerate(chips):
            landing.append(self._remote(ride, a, 4 + j, src[a], 4 * px + 2 * py + 1 - c, sibling))
        return first, passed, landing

    def start(self, ride):
        for a in range(self.n):
            self._local(ride, a).start()
            for cp in (self._direct(ride, a) if self.scatter else self._two_level(ride, a)[0]):
                cp.start()

    def finish(self, ride):
        if self.scatter:
            for a in range(self.n):
                for cp in self._direct(ride, a):
                    cp.wait()
                self._local(ride, a).wait()
            return
        levels = [self._two_level(ride, a) for a in range(self.n)]
        for first, passed, landing in levels:
            for j, cp in enumerate(passed):
                landing[1 + j].wait_recv()
                cp.start()
        for a, (first, passed, landing) in enumerate(levels):
            landing[0].wait_recv()
            for cp in landing[4:]:
                cp.wait_recv()
            for cp in first + passed:
                cp.wait_send()
            self._local(ride, a).wait()

    def start_at_first(self, ids, ride):
        first = functools.reduce(jnp.logical_and, [i == 0 for i in ids])

        @pl.when(first)
        def _():
            self.start(ride)

    def wait_at_last(self, ids, grid, ride):
        last = functools.reduce(jnp.logical_and, [i == g - 1 for i, g in zip(ids, grid)])

        @pl.when(last)
        def _():
            self.finish(ride)


def _exchange(bufs, scatter, name):
    rider = _Rider(bufs, scatter)

    def body(*refs):
        ride = (refs[:rider.n], refs[rider.n:2 * rider.n], refs[2 * rider.n:])
        rider.start(ride)
        rider.finish(ride)

    return pl.pallas_call(
        body,
        name=name,
        in_specs=rider.specs,
        out_specs=rider.specs,
        out_shape=rider.out_shape,
        scratch_shapes=rider.scratch,
    )(*rider.bufs)


MM_RESIDENT_B_BYTES = 14 * 1024 * 1024
MM_A_TILE_BYTES = 4 * 1024 * 1024
MM_OUT_TILE_BYTES = 6 * 1024 * 1024


def _mm_tiles(mode, M, N, K, a_bytes, out_bytes):
    if mode != "tn" and K * N * 2 <= MM_RESIDENT_B_BYTES:
        for tm in (1024, 512, 256, 128):
            if M % tm == 0 and tm * K * a_bytes <= MM_A_TILE_BYTES and tm * N * out_bytes <= MM_OUT_TILE_BYTES:
                return tm, N, K
    if mode == "tn":
        return (_pick(M, (1024, 1408, 512, 256, 128)), _pick(N, (1024, 1664, 1408, 512, 256, 128)),
                _pick(K, (2048, 1024, 512, 256, 128)))
    return _pick(M, (1024, 512, 256, 128)), _pick(N, (512, 256, 128)), _pick(K, (1024, 512, 256, 128))


def _mm(a, b, *, mode, out_dtype, name, res=None, res_scale=1.0, rider=None):
    if mode == "nn":
        (M, K), (K2, N) = a.shape, b.shape
    elif mode == "nt":
        (M, K), (N, K2) = a.shape, b.shape
    else:
        (K, M), (K2, N) = a.shape, b.shape
    assert K == K2, (a.shape, b.shape, mode)
    out_bytes = jnp.dtype(out_dtype).itemsize + (4 if res is not None else 0)
    tm, tn, tk = _mm_tiles(mode, M, N, K, a.dtype.itemsize, out_bytes)
    grid = (M // tm, N // tn, K // tk)
    nk = grid[2]
    dims = {"nn": _NN, "nt": _NT, "tn": _TN}[mode]
    n_in = 2 + (res is not None)
    n_ride = rider.n if rider is not None else 0

    def body(*refs):
        a_ref, b_ref = refs[:2]
        r_ref = refs[2] if res is not None else None
        o_ref = refs[n_in + n_ride]
        rest = refs[n_in + 2 * n_ride + 1:]
        acc_ref = rest[0] if nk > 1 else None
        ids = [pl.program_id(d) for d in range(3)]
        if rider is not None:
            ride = (refs[n_in:n_in + n_ride], refs[n_in + n_ride + 1:n_in + 2 * n_ride + 1], rest[-3:])
            rider.start_at_first(ids, ride)
        part = _dot(a_ref[...].astype(BF16), b_ref[...].astype(BF16), dims)

        def finish(total):
            if r_ref is not None:
                total = total + res_scale * r_ref[...]
            o_ref[...] = total.astype(out_dtype)

        if nk == 1:
            finish(part)
        else:
            k = ids[2]

            @pl.when(k == 0)
            def _():
                acc_ref[...] = part

            @pl.when(k > 0)
            def _():
                acc_ref[...] += part

            @pl.when(k == nk - 1)
            def _():
                finish(acc_ref[...])

        if rider is not None:
            rider.wait_at_last(ids, grid, ride)

    if mode == "nn":
        a_spec = pl.BlockSpec((tm, tk), lambda i, j, k: (i, k))
        b_spec = pl.BlockSpec((tk, tn), lambda i, j, k: (k, j))
    elif mode == "nt":
        a_spec = pl.BlockSpec((tm, tk), lambda i, j, k: (i, k))
        b_spec = pl.BlockSpec((tn, tk), lambda i, j, k: (j, k))
    else:
        a_spec = pl.BlockSpec((tk, tm), lambda i, j, k: (k, i))
        b_spec = pl.BlockSpec((tk, tn), lambda i, j, k: (k, j))
    o_spec = pl.BlockSpec((tm, tn), lambda i, j, k: (i, j))
    in_specs = [a_spec, b_spec] + ([o_spec] if res is not None else [])
    args = (a, b) + ((res,) if res is not None else ())
    out_specs, out_shape = [o_spec], [jax.ShapeDtypeStruct((M, N), out_dtype)]
    scratch = [pltpu.VMEM((tm, tn), F32)] if nk > 1 else []
    sem = ("parallel", "parallel", "arbitrary")
    if rider is not None:
        in_specs, args = in_specs + rider.specs, args + tuple(rider.bufs)
        out_specs, out_shape = out_specs + rider.specs, out_shape + rider.out_shape
        scratch = scratch + rider.scratch
        sem = ("arbitrary",) * 3
    outs = pl.pallas_call(
        body,
        name=name,
        grid=grid,
        in_specs=in_specs,
        out_specs=out_specs,
        out_shape=out_shape,
        scratch_shapes=scratch,
        compiler_params=_params(sem),
    )(*args)
    return outs[0] if rider is None else (outs[0], list(outs[1:]))


def _mm_host(a, b, *, rider, **kw):
    out = _mm(a, b, rider=rider, **kw)
    return out if rider is not None else (out, [])


def _as_host(rider, results):
    return results if rider is not None else tuple(results) + ([],)


MM_FUSED_MARGIN_BYTES = 10 * 1024 * 1024
MM_FUSED_MAX_ROWS = 512


def _col_sum_update(acc_ref, val, first):
    part = jnp.sum(val.reshape(val.shape[0] // 8, 8, val.shape[1]), axis=0)

    @pl.when(first)
    def _():
        acc_ref[...] = part

    @pl.when(jnp.logical_not(first))
    def _():
        acc_ref[...] += part


def _mm_fused(a, b, *, mode, name, extras, outs, epilogue, sums=(), rider=None, max_rows=MM_FUSED_MAX_ROWS,
              pass_a=False):
    parts = list(a) if isinstance(a, (list, tuple)) else [a]
    M, K = parts[0].shape[0], sum(p.shape[1] for p in parts)
    if mode == "nn":
        (K2, N), b_dims = b.shape, _NN
    else:
        (N, K2), b_dims = b.shape, _NT
    assert K == K2, (K, b.shape, mode)
    rows = parts + [e for e in extras if e.shape[0] == M]
    per_row = 2 * (sum(e.shape[1] * e.dtype.itemsize for e in rows)
                   + sum(c * jnp.dtype(d).itemsize for c, d in outs)) + 2 * N * 4
    budget = VMEM_LIMIT_BYTES - K * N * 2 - MM_FUSED_MARGIN_BYTES
    tm = next(t for t in (512, 256, 128, 64, 32, 16) if t <= max_rows and M % t == 0 and t * per_row <= budget)
    steps = M // tm
    n_a, n_x, n_o, n_s = len(parts), len(extras), len(outs), len(sums)
    n_ride = rider.n if rider is not None else 0

    def body(*refs):
        a_refs, b_ref = refs[:n_a], refs[n_a]
        x_refs = refs[n_a + 1:n_a + 1 + n_x]
        base = n_a + 1 + n_x + n_ride
        o_refs, s_refs = refs[base:base + n_o], refs[base + n_o:base + n_o + n_s]
        acc_refs = refs[base + n_o + n_s + n_ride:base + n_o + 2 * n_s + n_ride]
        ids = [pl.program_id(0)]
        if rider is not None:
            ride = (refs[n_a + 1 + n_x:base], refs[base + n_o + n_s:base + n_o + n_s + n_ride], refs[-3:])
            rider.start_at_first(ids, ride)
        a_tile = a_refs[0][...]
        a_bf16 = a_tile.astype(BF16) if n_a == 1 else jnp.concatenate([r[...].astype(BF16) for r in a_refs], axis=1)
        prod = _dot(a_bf16, b_ref[...], b_dims)
        tiles = epilogue(prod, *([a_tile] if pass_a else []), *[r[...] for r in x_refs])
        for o_ref, t in zip(o_refs, tiles[:n_o]):
            o_ref[...] = t.astype(o_ref.dtype)
        for acc_ref, t in zip(acc_refs, tiles[n_o:]):
            _col_sum_update(acc_ref, t, ids[0] == 0)
        if n_s:
            @pl.when(ids[0] == steps - 1)
            def _():
                for s_ref, acc_ref in zip(s_refs, acc_refs):
                    s_ref[...] = jnp.sum(acc_ref[...], axis=0, keepdims=True)
        if rider is not None:
            rider.wait_at_last(ids, (steps,), ride)

    in_specs = [pl.BlockSpec((tm, p.shape[1]), lambda i: (i, 0)) for p in parts]
    in_specs.append(pl.BlockSpec(b.shape, lambda i: (0, 0), pipeline_mode=pl.Buffered(1)))
    for e in extras:
        in_specs.append(pl.BlockSpec((tm, e.shape[1]), lambda i: (i, 0)) if e.shape[0] == M
                        else pl.BlockSpec(e.shape, lambda i: (0, 0)))
    out_specs = ([pl.BlockSpec((tm, c), lambda i: (i, 0)) for c, _ in outs]
                 + [pl.BlockSpec((1, c), lambda i: (0, 0)) for c in sums])
    out_shape = ([jax.ShapeDtypeStruct((M, c), d) for c, d in outs]
                 + [jax.ShapeDtypeStruct((1, c), F32) for c in sums])
    args = tuple(parts) + (b,) + tuple(extras)
    scratch = [pltpu.VMEM((8, c), F32) for c in sums]
    if rider is not None:
        in_specs, args = in_specs + rider.specs, args + tuple(rider.bufs)
        out_specs, out_shape = out_specs + rider.specs, out_shape + rider.out_shape
        scratch = scratch + rider.scratch
    res = pl.pallas_call(
        body,
        name=name,
        grid=(steps,),
        in_specs=in_specs,
        out_specs=out_specs,
        out_shape=out_shape,
        scratch_shapes=scratch,
        compiler_params=_params(("arbitrary",) if (n_s or rider is not None) else ("parallel",)),
    )(*args)
    return tuple(res[:n_o + n_s]) + ((list(res[n_o + n_s:]),) if rider is not None else ())


def _pair_rows(blk, lane_is_a):
    zero = jnp.zeros_like(blk)
    return jnp.concatenate([jnp.where(lane_is_a, blk, zero), jnp.where(lane_is_a, zero, blk)], axis=0)


SB_STRIP = 32
SB_FWD_PAIRS = 4
SB_BWD_PAIRS = 2
SB_GROUP = 2


def _pair_lanes(p):
    return slice(p * LANES, (p + 1) * LANES)


def _sb_scan_matrices():
    o = lax.broadcasted_iota(jnp.int32, (2 * LANES, 4 * LANES), 0)
    c = lax.broadcasted_iota(jnp.int32, (2 * LANES, 4 * LANES), 1) & (2 * LANES - 1)
    same = (o >= LANES) == (c >= LANES)
    oo, cc = o & (LANES - 1), c & (LANES - 1)
    return (jnp.where(same & (cc > oo), -1.0, 0.0).astype(BF16), jnp.where(same & (cc < oo), 1.0, 0.0).astype(BF16))


def _sb_causal_masks(tq):
    d = lax.broadcasted_iota(jnp.int32, (tq // SB_KEY_BLOCK, SB_KEY_BLOCK, tq), 0)
    k = lax.broadcasted_iota(jnp.int32, (tq // SB_KEY_BLOCK, SB_KEY_BLOCK, tq), 1)
    t = lax.broadcasted_iota(jnp.int32, (tq // SB_KEY_BLOCK, SB_KEY_BLOCK, tq), 2)
    return jnp.where(d * SB_KEY_BLOCK + k < t, 1.0, 0.0).astype(F32)


def _sb_log_terms(z):
    minus_abs = lax.bitcast_convert_type(lax.bitcast_convert_type(z, jnp.uint32) | jnp.uint32(0x80000000), F32)
    spent = jnp.maximum(z, 0.0) + jnp.log(1.0 + jnp.exp(minus_abs))
    return spent, z - spent


def _sb_store_split(ref, strip, val, cols):
    hi = val.astype(BF16)
    ref[pl.ds(strip * SB_STRIP, SB_STRIP), cols] = hi
    ref[pl.ds(2 * LANES + strip * SB_STRIP, SB_STRIP), cols] = (val - hi.astype(F32)).astype(BF16)


def _sb_lanes(tq, diag):
    if diag == "left":
        return 0, tq // 2
    first = 0 if diag is None else diag * SB_KEY_BLOCK
    return first, tq - first


def _lane_add(full, part, lanes):
    first, width = lanes
    pieces = [full[:, :first]] if first else []
    pieces.append(full[:, first:first + width] + part)
    if first + width < full.shape[1]:
        pieces.append(full[:, first + width:])
    return pieces[0] if len(pieces) == 1 else jnp.concatenate(pieces, axis=1)


def _sb_fwd(h_a, rider=None):
    assert SB_FWD_PAIRS == 4
    T = h_a.shape[0]
    tq = _pick(T, (SB_Q_BLOCK, SB_KEY_BLOCK))
    nq, per_q, nkb = T // tq, tq // SB_KEY_BLOCK, T // SB_KEY_BLOCK
    assert per_q % SB_GROUP == 0
    n_strips = 2 * LANES // SB_STRIP
    n_ride = rider.n if rider is not None else 0
    after_m, _ = _sb_scan_matrices()
    causal_m = _sb_causal_masks(tq)
    pairs = SB_FWD_PAIRS

    def body(*refs):
        q_ref, k_ref, v_ref, after_ref, causal_ref = refs[:5]
        a_ref, r_ref, n_ref = refs[5 + n_ride:8 + n_ride]
        z_ref, lb_ref, split_ref, w_ref = refs[8 + 2 * n_ride:12 + 2 * n_ride]
        ids = [pl.program_id(0)]
        if rider is not None:
            ride = (refs[5:5 + n_ride], refs[8 + n_ride:8 + 2 * n_ride], refs[-3:])
            rider.start_at_first(ids, ride)
        i = ids[0]
        q_t = [(q_ref[:, _pair_lanes(p)].astype(F32).T * (SB_HEAD_DIM ** -0.5)).astype(BF16) for p in range(pairs)]
        lane_is_a = lax.broadcasted_iota(jnp.int32, (SB_KEY_BLOCK, LANES), 1) < SB_HEAD_DIM

        def tiles(kbs, diags, carry):
            nb = len(kbs)
            lanes = [_sb_lanes(tq, d) for d in diags]
            cols = [slice(first, first + width) for first, width in lanes]
            acc_t, ra, rb = [list(c) for c in carry]
            ks = [pl.multiple_of(kb * SB_KEY_BLOCK, SB_KEY_BLOCK) for kb in kbs]
            slot = lambda p, b: p * nb + b

            def causal(b, s):
                return causal_ref[diags[b], pl.ds((s * SB_STRIP) % SB_KEY_BLOCK, SB_STRIP), cols[b]]

            vv = {}
            for b in range(nb):
                for p in range(pairs):
                    kk = _pair_rows(k_ref[pl.ds(ks[b], SB_KEY_BLOCK), _pair_lanes(p)], lane_is_a)
                    vv[p, b] = _pair_rows(v_ref[pl.ds(ks[b], SB_KEY_BLOCK), _pair_lanes(p)], lane_is_a)
                    z_ref[slot(p, b), :, cols[b]] = _dot(kk, q_t[p][:, cols[b]], _NN)
            sums = {}
            for b in range(nb):
                for p in range(pairs):
                    part = [jnp.zeros((8, lanes[b][1]), F32), jnp.zeros((8, lanes[b][1]), F32)]
                    for s in range(n_strips):
                        rows = pl.ds(s * SB_STRIP, SB_STRIP)
                        spent, log_beta = _sb_log_terms(z_ref[slot(p, b), rows, cols[b]])
                        lb_ref[slot(p, b), rows, cols[b]] = log_beta
                        if isinstance(diags[b], int):
                            spent = spent * causal(b, s)
                        _sb_store_split(split_ref.at[slot(p, b)], s, spent, cols[b])
                        head = (s * SB_STRIP) // SB_KEY_BLOCK
                        part[head] = part[head] + jnp.sum(spent.reshape(SB_STRIP // 8, 8, lanes[b][1]), axis=0)
                    sums[p, b] = part
            for b in range(nb):
                for p in range(pairs):
                    z_ref[slot(p, b), :, cols[b]] = _dot(after_ref[...], split_ref[slot(p, b), :, cols[b]], _NN)
            for b in range(nb):
                for p in range(pairs):
                    for s in range(n_strips):
                        rows = pl.ds(s * SB_STRIP, SB_STRIP)
                        start = (ra[p] if (s * SB_STRIP) < SB_KEY_BLOCK else rb[p])[:, cols[b]]
                        w = jnp.exp(lb_ref[slot(p, b), rows, cols[b]] + z_ref[slot(p, b), rows, cols[b]] + start)
                        if isinstance(diags[b], int):
                            w = w * causal(b, s)
                        w_ref[slot(p, b), rows, cols[b]] = w.astype(BF16)
                    r_ref[2 * p, kbs[b]] = ra[p]
                    r_ref[2 * p + 1, kbs[b]] = rb[p]
                    ra[p] = _lane_add(ra[p], -jnp.sum(sums[p, b][0], axis=0, keepdims=True), lanes[b])
                    rb[p] = _lane_add(rb[p], -jnp.sum(sums[p, b][1], axis=0, keepdims=True), lanes[b])
            for b in range(nb):
                for p in range(pairs):
                    acc_t[p] = _lane_add(acc_t[p], _dot(vv[p, b], w_ref[slot(p, b), :, cols[b]], _TN), lanes[b])
            return tuple(acc_t), tuple(ra), tuple(rb)

        carry = (tuple(jnp.zeros((LANES, tq), F32) for _ in range(pairs)),
                 tuple(jnp.zeros((1, tq), F32) for _ in range(pairs)),
                 tuple(jnp.zeros((1, tq), F32) for _ in range(pairs)))
        own = list(reversed(range(per_q)))
        n_full = i * per_q
        carry = lax.cond(
            i > 0,
            lambda cc: tiles([n_full + d for d in own] + [n_full - 1 - b for b in range(SB_GROUP)],
                             own + [None] * SB_GROUP, cc),
            lambda cc: tiles([n_full + d for d in own], own, cc), carry)
        first_walked = jnp.where(i > 0, SB_GROUP, 0).astype(jnp.int32)

        def top_of(sums_a, sums_b, first):
            return jnp.max(functools.reduce(jnp.maximum, [r[:, first:] for r in sums_a + sums_b]))

        def alive(c):
            return jnp.logical_and(c[0] < n_full, top_of(c[2], c[3], 0) > SB_DEAD_LOG)

        def step(c):
            kbs = [n_full - 1 - c[0] - b for b in range(SB_GROUP)]
            return (c[0] + SB_GROUP,) + lax.cond(
                top_of(c[2], c[3], tq // 2) > SB_DEAD_LOG,
                lambda cc: tiles(kbs, [None] * SB_GROUP, cc), lambda cc: tiles(kbs, ["left"] * SB_GROUP, cc), c[1:])

        walked, acc_t, _, _ = lax.while_loop(alive, step, (first_walked,) + carry)
        for p in range(pairs):
            a_ref[:, _pair_lanes(p)] = acc_t[p].T.astype(BF16)
        n_ref[...] = jnp.zeros(n_ref.shape, F32) + walked.astype(F32)
        if rider is not None:
            rider.wait_at_last(ids, (nq,), ride)

    wide = pairs * LANES
    in_specs = [pl.BlockSpec((tq, wide), lambda i: (i, 0)),
                pl.BlockSpec((T, wide), lambda i: (0, 1), pipeline_mode=pl.Buffered(1)),
                pl.BlockSpec((T, wide), lambda i: (0, 2), pipeline_mode=pl.Buffered(1)),
                pl.BlockSpec(after_m.shape, lambda i: (0, 0), pipeline_mode=pl.Buffered(1)),
                pl.BlockSpec(causal_m.shape, lambda i: (0, 0, 0), pipeline_mode=pl.Buffered(1))]
    out_specs = [pl.BlockSpec((tq, wide), lambda i: (i, 0)),
                 pl.BlockSpec((2 * pairs, nkb, 1, tq), lambda i: (0, 0, 0, i)),
                 pl.BlockSpec((1, 8, LANES), lambda i: (i, 0, 0))]
    out_shape = [jax.ShapeDtypeStruct((T, SB_WIDTH), BF16), jax.ShapeDtypeStruct((2 * pairs, nkb, 1, T), F32),
                 jax.ShapeDtypeStruct((nq, 8, LANES), F32)]
    args = (h_a, h_a, h_a, after_m, causal_m)
    slots = pairs * (per_q + SB_GROUP)
    scratch = [pltpu.VMEM((slots, 2 * LANES, tq), F32), pltpu.VMEM((slots, 2 * LANES, tq), F32),
               pltpu.VMEM((slots, 4 * LANES, tq), BF16), pltpu.VMEM((slots, 2 * LANES, tq), BF16)]
    if rider is not None:
        in_specs, args = in_specs + rider.specs, args + tuple(rider.bufs)
        out_specs, out_shape = out_specs + rider.specs, out_shape + rider.out_shape
        scratch = scratch + rider.scratch
    outs = pl.pallas_call(
        body,
        name="sb_fwd",
        grid=(nq,),
        in_specs=in_specs,
        out_specs=out_specs,
        out_shape=out_shape,
        scratch_shapes=scratch,
        compiler_params=_params(("arbitrary",)),
    )(*args)
    return outs[0], (outs[1], outs[2]), list(outs[3:])


def _sb_bwd(h_a, d_out, saved, rider=None):
    r_mat, walked_blocks = saved
    T = h_a.shape[0]
    tq = _pick(T, (SB_Q_BLOCK, SB_KEY_BLOCK))
    nq, per_q, nkb = T // tq, tq // SB_KEY_BLOCK, T // SB_KEY_BLOCK
    n_strips = 2 * LANES // SB_STRIP
    after_m, before_m = _sb_scan_matrices()
    causal_m = _sb_causal_masks(tq)
    pairs = SB_BWD_PAIRS
    groups = 4 // pairs
    n_ride = rider.n if rider is not None else 0

    def body(*refs):
        q_ref, k_ref, v_ref, do_ref, r_ref, n_ref, after_ref, before_ref, causal_ref = refs[:9]
        dq_ref, dk_ref, dv_ref = refs[9 + n_ride:12 + n_ride]
        z_ref, lb_ref, split_ref, w_ref, da_ref, dz_ref = refs[12 + 2 * n_ride:18 + 2 * n_ride]
        ids = [pl.program_id(0), pl.program_id(1)]
        if rider is not None:
            ride = (refs[9:9 + n_ride], refs[12 + n_ride:12 + 2 * n_ride], refs[-3:])
            rider.start_at_first(ids, ride)
        i = ids[1]

        @pl.when(i == 0)
        def _():
            dk_ref[...] = jnp.zeros_like(dk_ref)
            dv_ref[...] = jnp.zeros_like(dv_ref)

        scale = SB_HEAD_DIM ** -0.5
        q = [q_ref[:, _pair_lanes(p)] for p in range(pairs)]
        d_o = [do_ref[:, _pair_lanes(p)] for p in range(pairs)]
        q_t = [(x.astype(F32).T * scale).astype(BF16) for x in q]
        do_t = [x.astype(F32).T.astype(BF16) for x in d_o]
        lane_is_a = lax.broadcasted_iota(jnp.int32, (SB_KEY_BLOCK, LANES), 1) < SB_HEAD_DIM

        def tiles(kbs, diags, carry):
            nb = len(kbs)
            lanes = [_sb_lanes(tq, d) for d in diags]
            cols = [slice(first, first + width) for first, width in lanes]
            dq_t, ca, cb = [list(c) for c in carry]
            ks = [pl.multiple_of(kb * SB_KEY_BLOCK, SB_KEY_BLOCK) for kb in kbs]
            slot = lambda p, b: p * nb + b

            def causal(b, s):
                return causal_ref[diags[b], pl.ds((s * SB_STRIP) % SB_KEY_BLOCK, SB_STRIP), cols[b]]

            kk, vv = {}, {}
            for b in range(nb):
                for p in range(pairs):
                    kk[p, b] = _pair_rows(k_ref[pl.ds(ks[b], SB_KEY_BLOCK), _pair_lanes(p)], lane_is_a)
                    vv[p, b] = _pair_rows(v_ref[pl.ds(ks[b], SB_KEY_BLOCK), _pair_lanes(p)], lane_is_a)
                    z_ref[slot(p, b), :, cols[b]] = _dot(kk[p, b], q_t[p][:, cols[b]], _NN)
            for b in range(nb):
                for p in range(pairs):
                    for s in range(n_strips):
                        rows = pl.ds(s * SB_STRIP, SB_STRIP)
                        spent, log_beta = _sb_log_terms(z_ref[slot(p, b), rows, cols[b]])
                        lb_ref[slot(p, b), rows, cols[b]] = log_beta
                        if isinstance(diags[b], int):
                            spent = spent * causal(b, s)
                        _sb_store_split(split_ref.at[slot(p, b)], s, spent, cols[b])
            for b in range(nb):
                for p in range(pairs):
                    z_ref[slot(p, b), :, cols[b]] = _dot(after_ref[...], split_ref[slot(p, b), :, cols[b]], _NN)
                    da_ref[slot(p, b), :, cols[b]] = _dot(vv[p, b], do_t[p][:, cols[b]], _NN)
            sums = {}
            for b in range(nb):
                for p in range(pairs):
                    part = [jnp.zeros((8, lanes[b][1]), F32), jnp.zeros((8, lanes[b][1]), F32)]
                    for s in range(n_strips):
                        rows = pl.ds(s * SB_STRIP, SB_STRIP)
                        start = r_ref[2 * p + (s * SB_STRIP) // SB_KEY_BLOCK, kbs[b]][:, cols[b]]
                        w = jnp.exp(lb_ref[slot(p, b), rows, cols[b]] + z_ref[slot(p, b), rows, cols[b]] + start)
                        if isinstance(diags[b], int):
                            w = w * causal(b, s)
                        w_ref[slot(p, b), rows, cols[b]] = w.astype(BF16)
                        da = da_ref[slot(p, b), rows, cols[b]] * w
                        da_ref[slot(p, b), rows, cols[b]] = da
                        _sb_store_split(split_ref.at[slot(p, b)], s, da, cols[b])
                        head = (s * SB_STRIP) // SB_KEY_BLOCK
                        part[head] = part[head] + jnp.sum(da.reshape(SB_STRIP // 8, 8, lanes[b][1]), axis=0)
                    sums[p, b] = part
            for b in range(nb):
                for p in range(pairs):
                    z_ref[slot(p, b), :, cols[b]] = _dot(before_ref[...], split_ref[slot(p, b), :, cols[b]], _NN)
            for b in range(nb):
                for p in range(pairs):
                    for s in range(n_strips):
                        rows = pl.ds(s * SB_STRIP, SB_STRIP)
                        base = (ca[p] if (s * SB_STRIP) < SB_KEY_BLOCK else cb[p])[:, cols[b]]
                        sig = jnp.exp(lb_ref[slot(p, b), rows, cols[b]])
                        dz = (da_ref[slot(p, b), rows, cols[b]] * (1.0 - sig)
                              - (z_ref[slot(p, b), rows, cols[b]] + base) * sig)
                        if isinstance(diags[b], int):
                            dz = dz * causal(b, s)
                        dz_ref[slot(p, b), rows, cols[b]] = (dz * scale).astype(BF16)
                    ca[p] = _lane_add(ca[p], jnp.sum(sums[p, b][0], axis=0, keepdims=True), lanes[b])
                    cb[p] = _lane_add(cb[p], jnp.sum(sums[p, b][1], axis=0, keepdims=True), lanes[b])
            for b in range(nb):
                for p in range(pairs):
                    dq_t[p] = _lane_add(dq_t[p], _dot(kk[p, b], dz_ref[slot(p, b), :, cols[b]], _TN), lanes[b])
                    dkk = _dot(dz_ref[slot(p, b), :, cols[b]], q[p][cols[b], :], _NN)
                    dvv = _dot(w_ref[slot(p, b), :, cols[b]], d_o[p][cols[b], :], _NN)
                    here = (pl.ds(ks[b], SB_KEY_BLOCK), _pair_lanes(p))
                    dk_ref[here] += jnp.where(lane_is_a, dkk[:SB_KEY_BLOCK], dkk[SB_KEY_BLOCK:])
                    dv_ref[here] += jnp.where(lane_is_a, dvv[:SB_KEY_BLOCK], dvv[SB_KEY_BLOCK:])
            return tuple(dq_t), tuple(ca), tuple(cb)

        n_full = i * per_q
        groups_walked = jnp.clip(jnp.max(n_ref[...]).astype(jnp.int32), 0, n_full) // SB_GROUP
        carry = (tuple(jnp.zeros((LANES, tq), F32) for _ in range(pairs)),
                 tuple(jnp.zeros((1, tq), F32) for _ in range(pairs)),
                 tuple(jnp.zeros((1, tq), F32) for _ in range(pairs)))

        def below(j, c):
            kbs = [n_full - (groups_walked - j) * SB_GROUP + b for b in range(SB_GROUP)]
            starts = [r_ref[h, kbs[-1]][:, tq // 2:] for h in range(2 * pairs)]
            reaches = jnp.max(functools.reduce(jnp.maximum, starts)) > SB_DEAD_LOG
            return lax.cond(reaches, lambda cc: tiles(kbs, [None] * SB_GROUP, cc),
                            lambda cc: tiles(kbs, ["left"] * SB_GROUP, cc), c)

        carry = lax.fori_loop(0, groups_walked, below, carry)
        own = list(range(per_q))
        carry = tiles([i * per_q + d for d in own], own, carry)
        for p in range(pairs):
            dq_ref[:, _pair_lanes(p)] = carry[0][p].T.astype(BF16)
        if rider is not None:
            rider.wait_at_last(ids, (groups, nq), ride)

    wide = pairs * LANES
    mat = pl.BlockSpec(after_m.shape, lambda g, i: (0, 0), pipeline_mode=pl.Buffered(1))
    in_specs = [pl.BlockSpec((tq, wide), lambda g, i: (i, g)),
                pl.BlockSpec((T, wide), lambda g, i: (0, groups + g), pipeline_mode=pl.Buffered(1)),
                pl.BlockSpec((T, wide), lambda g, i: (0, 2 * groups + g), pipeline_mode=pl.Buffered(1)),
                pl.BlockSpec((tq, wide), lambda g, i: (i, g)),
                pl.BlockSpec((2 * pairs, nkb, 1, tq), lambda g, i: (g, 0, 0, i)),
                pl.BlockSpec((1, 8, LANES), lambda g, i: (i, 0, 0)),
                mat, mat,
                pl.BlockSpec(causal_m.shape, lambda g, i: (0, 0, 0), pipeline_mode=pl.Buffered(1))]
    out_specs = [pl.BlockSpec((tq, wide), lambda g, i: (i, g)),
                 pl.BlockSpec((T, wide), lambda g, i: (0, g)),
                 pl.BlockSpec((T, wide), lambda g, i: (0, g))]
    out_shape = [jax.ShapeDtypeStruct((T, SB_WIDTH), BF16), jax.ShapeDtypeStruct((T, SB_WIDTH), F32),
                 jax.ShapeDtypeStruct((T, SB_WIDTH), F32)]
    args = (h_a, h_a, h_a, d_out, r_mat, walked_blocks, after_m, before_m, causal_m)
    slots = pairs * max(per_q, SB_GROUP)
    scratch = [pltpu.VMEM((slots, 2 * LANES, tq), F32), pltpu.VMEM((slots, 2 * LANES, tq), F32),
               pltpu.VMEM((slots, 4 * LANES, tq), BF16), pltpu.VMEM((slots, 2 * LANES, tq), BF16),
               pltpu.VMEM((slots, 2 * LANES, tq), F32), pltpu.VMEM((slots, 2 * LANES, tq), BF16)]
    if rider is not None:
        in_specs, args = in_specs + rider.specs, args + tuple(rider.bufs)
        out_specs, out_shape = out_specs + rider.specs, out_shape + rider.out_shape
        scratch = scratch + rider.scratch
    outs = pl.pallas_call(
        body,
        name="sb_bwd",
        grid=(groups, nq),
        in_specs=in_specs,
        out_specs=out_specs,
        out_shape=out_shape,
        scratch_shapes=scratch,
        compiler_params=_params(("arbitrary", "arbitrary") if rider is not None else ("parallel", "arbitrary")),
    )(*args)
    return outs[0], outs[1], outs[2], list(outs[3:])


def _ret_tables(T):
    half = RET_QK_DIM // 2
    inv = 1.0 / (ROPE_BASE ** (jnp.arange(half, dtype=F32) / half))
    ang = jnp.arange(T, dtype=F32)[:, None] * inv[None, :]
    cos, sin = jnp.cos(ang), jnp.sin(ang)
    cos_t = jnp.concatenate([cos, cos], axis=1)
    sin_t = jnp.concatenate([-sin, sin], axis=1)
    log_gamma = jnp.log1p(-jnp.exp2(-5.0 - jnp.arange(RET_HEADS, dtype=F32)))
    idx = jnp.arange(RET_CHUNK, dtype=F32)
    rel = idx[:, None] - idx[None, :]
    decay = jnp.where(rel[None] >= 0, jnp.exp(log_gamma[:, None, None] * jnp.maximum(rel, 0.0)[None]), 0.0)
    k_decay = jnp.exp(log_gamma[None, :] * (RET_CHUNK - 1.0 - idx)[:, None])
    q_decay = jnp.exp(log_gamma[None, :] * (idx + 1.0)[:, None])
    chunk_decay = jnp.exp(log_gamma * RET_CHUNK)
    k_dec = jnp.broadcast_to(k_decay.T[:, :, None], (RET_HEADS, RET_CHUNK, LANES))
    q_dec = jnp.broadcast_to(q_decay.T[:, :, None], (RET_HEADS, RET_CHUNK, LANES))
    c_dec = jnp.broadcast_to(chunk_decay[:, None, None], (RET_HEADS, 8, LANES))
    return cos_t, sin_t, decay, k_dec, q_dec, c_dec


def _rotary(x, cos_t, sin_t):
    return x * cos_t + pltpu.roll(x, RET_QK_DIM // 2, 1) * sin_t


def _rotary_transpose(dy, cos_t, sin_t):
    return dy * cos_t + pltpu.roll(dy * sin_t, RET_QK_DIM // 2, 1)


def _head_norm(o):
    mu = jnp.mean(o, axis=1, keepdims=True)
    cen = o - mu
    var = jnp.mean(cen * cen, axis=1, keepdims=True)
    rstd = lax.rsqrt(var + LN_EPS)
    return cen * rstd, rstd


def _ret_specs(steps, per_step, reverse):
    def n_of(n):
        return (steps - 1 - n) if reverse else n

    rows = per_step * RET_CHUNK
    q_spec = pl.BlockSpec((rows, RET_QK_WIDTH), lambda n: (n_of(n), 0))
    k_spec = pl.BlockSpec((rows, RET_QK_WIDTH), lambda n: (n_of(n), 1))
    vv = pl.BlockSpec((rows, RET_V_WIDTH), lambda n: (n_of(n), 0))
    pos = pl.BlockSpec((rows, LANES), lambda n: (n_of(n), 0))
    per_head = pl.BlockSpec((RET_HEADS, RET_CHUNK, LANES), lambda n: (0, 0, 0))
    c_dec = pl.BlockSpec((RET_HEADS, 8, LANES), lambda n: (0, 0, 0))
    state = pl.BlockSpec((RET_HEADS, per_step, RET_QK_DIM, RET_V_DIM), lambda n: (0, n_of(n), 0, 0))
    return q_spec, k_spec, vv, pos, per_head, c_dec, state


def _qk_cols(h):
    return slice(h * RET_QK_DIM, (h + 1) * RET_QK_DIM)


def _v_cols(h):
    return slice(h * RET_V_DIM, (h + 1) * RET_V_DIM)


def _ret_fwd(h_b, h_c, h_d, tables):
    T = h_b.shape[0]
    nc = T // RET_CHUNK
    per_step = _pick(nc, (RET_STEP_CHUNKS, 1))
    steps = nc // per_step
    q_spec, k_spec, vv, pos, per_head, c_dec, state = _ret_specs(steps, per_step, False)

    def body(q_ref, k_ref, v_ref, g_ref, cos_ref, sin_ref, dec_ref, kd_ref, qd_ref, cd_ref,
             y_ref, o_ref, st_ref, state_ref):
        @pl.when(pl.program_id(0) == 0)
        def _():
            state_ref[...] = jnp.zeros_like(state_ref)

        for c in range(per_step):
            rows = pl.ds(c * RET_CHUNK, RET_CHUNK)
            cos_t, sin_t = cos_ref[rows, :], sin_ref[rows, :]
            for h in range(RET_HEADS):
                q = _rotary(q_ref[rows, _qk_cols(h)], cos_t, sin_t) * (RET_QK_DIM ** -0.5)
                k = _rotary(k_ref[rows, _qk_cols(h)], cos_t, sin_t)
                v = v_ref[rows, _v_cols(h)]
                prev = state_ref[h]
                scores = _dot(q.astype(BF16), k.astype(BF16), _NT) * dec_ref[h]
                inner = _dot(scores.astype(BF16), v, _NN)
                cross = _dot((q * qd_ref[h]).astype(BF16), prev.astype(BF16), _NN)
                o = inner + cross
                st_ref[h, c] = prev
                kv = _dot((k * kd_ref[h]).astype(BF16), v, _TN)
                state_ref[h] = prev * cd_ref[h, 0:1, 0:1] + kv
                o_ref[rows, _v_cols(h)] = o
                normed, _ = _head_norm(o)
                gate = g_ref[rows, _v_cols(h)]
                y_ref[rows, _v_cols(h)] = (gate * jax.nn.sigmoid(gate) * normed).astype(BF16)

    return pl.pallas_call(
        body,
        name="ret_fwd",
        grid=(steps,),
        in_specs=[q_spec, k_spec, vv, vv, pos, pos, per_head, per_head, per_head, c_dec],
        out_specs=[vv, vv, state],
        out_shape=[jax.ShapeDtypeStruct((T, RET_V_WIDTH), BF16),
                   jax.ShapeDtypeStruct((T, RET_V_WIDTH), F32),
                   jax.ShapeDtypeStruct((RET_HEADS, nc, RET_QK_DIM, RET_V_DIM), F32)],
        scratch_shapes=[pltpu.VMEM((RET_HEADS, RET_QK_DIM, RET_V_DIM), F32)],
        compiler_params=_params(("arbitrary",)),
    )(h_b, h_b, h_c, h_d, *tables)


def _ret_bwd(d_y, o_pre, states, h_b, h_c, h_d, tables, rider=None):
    T = h_b.shape[0]
    nc = T // RET_CHUNK
    per_step = _pick(nc, (RET_STEP_CHUNKS, 1))
    steps = nc // per_step
    q_spec, k_spec, vv, pos, per_head, c_dec, state = _ret_specs(steps, per_step, True)
    n_ride = rider.n if rider is not None else 0

    def body(*refs):
        (dy_ref, o_ref, st_ref, q_ref, k_ref, v_ref, g_ref, cos_ref, sin_ref, dec_ref, kd_ref, qd_ref,
         cd_ref) = refs[:13]
        dq_ref, dk_ref, dv_ref, dg_ref = refs[13 + n_ride:17 + n_ride]
        carry_ref = refs[17 + 2 * n_ride]
        ids = [pl.program_id(0)]
        if rider is not None:
            ride = (refs[13:13 + n_ride], refs[17 + n_ride:17 + 2 * n_ride], refs[-3:])
            rider.start_at_first(ids, ride)

        @pl.when(ids[0] == 0)
        def _():
            carry_ref[...] = jnp.zeros_like(carry_ref)

        scale = RET_QK_DIM ** -0.5
        for c in reversed(range(per_step)):
            rows = pl.ds(c * RET_CHUNK, RET_CHUNK)
            cos_t, sin_t = cos_ref[rows, :], sin_ref[rows, :]
            for h in range(RET_HEADS):
                q = _rotary(q_ref[rows, _qk_cols(h)], cos_t, sin_t) * scale
                k = _rotary(k_ref[rows, _qk_cols(h)], cos_t, sin_t)
                v = v_ref[rows, _v_cols(h)]
                decay, k_dec, q_dec = dec_ref[h], kd_ref[h], qd_ref[h]
                chunk_decay = cd_ref[h, 0:1, 0:1]
                state = st_ref[h, c].astype(BF16)
                later = carry_ref[h]
                later_b = later.astype(BF16)

                gate = g_ref[rows, _v_cols(h)]
                sig = jax.nn.sigmoid(gate)
                silu = gate * sig
                normed, rstd = _head_norm(o_ref[rows, _v_cols(h)])
                d_y = dy_ref[rows, _v_cols(h)]
                dg_ref[rows, _v_cols(h)] = (d_y * normed * (sig * (1.0 + gate * (1.0 - sig)))).astype(BF16)
                d_n = d_y * silu
                d_o = rstd * (d_n - jnp.mean(d_n, axis=1, keepdims=True)
                              - normed * jnp.mean(d_n * normed, axis=1, keepdims=True))
                d_ob = d_o.astype(BF16)

                qb, kb = q.astype(BF16), k.astype(BF16)
                qd_b, kd_b = (q * q_dec).astype(BF16), (k * k_dec).astype(BF16)
                scores = _dot(qb, kb, _NT) * decay
                d_scores = (_dot(d_ob, v, _NT) * decay).astype(BF16)
                dq = _dot(d_scores, kb, _NN) + _dot(d_ob, state, _NT) * q_dec
                dk = _dot(d_scores, qb, _TN) + _dot(v, later_b, _NT) * k_dec
                dv = _dot(scores.astype(BF16), d_ob, _TN) + _dot(kd_b, later_b, _NN)
                carry_ref[h] = _dot(qd_b, d_ob, _TN) + chunk_decay * later
                dq_ref[rows, _qk_cols(h)] = _rotary_transpose(dq * scale, cos_t, sin_t).astype(BF16)
                dk_ref[rows, _qk_cols(h)] = _rotary_transpose(dk, cos_t, sin_t).astype(BF16)
                dv_ref[rows, _v_cols(h)] = dv.astype(BF16)
        if rider is not None:
            rider.wait_at_last(ids, (steps,), ride)

    qk_out = pl.BlockSpec((per_step * RET_CHUNK, RET_QK_WIDTH), lambda n: (steps - 1 - n, 0))
    in_specs = [vv, vv, state, q_spec, k_spec, vv, vv, pos, pos, per_head, per_head, per_head, c_dec]
    out_specs = [qk_out, qk_out, vv, vv]
    out_shape = [jax.ShapeDtypeStruct((T, RET_QK_WIDTH), BF16), jax.ShapeDtypeStruct((T, RET_QK_WIDTH), BF16),
                 jax.ShapeDtypeStruct((T, RET_V_WIDTH), BF16), jax.ShapeDtypeStruct((T, RET_V_WIDTH), BF16)]
    args = (d_y, o_pre, states, h_b, h_b, h_c, h_d) + tuple(tables)
    scratch = [pltpu.VMEM((RET_HEADS, RET_QK_DIM, RET_V_DIM), F32)]
    if rider is not None:
        in_specs, args = in_specs + rider.specs, args + tuple(rider.bufs)
        out_specs, out_shape = out_specs + rider.specs, out_shape + rider.out_shape
        scratch = scratch + rider.scratch
    outs = pl.pallas_call(
        body,
        name="ret_bwd",
        grid=(steps,),
        in_specs=in_specs,
        out_specs=out_specs,
        out_shape=out_shape,
        scratch_shapes=scratch,
        compiler_params=_params(("arbitrary",)),
    )(*args)
    return outs[0], outs[1], outs[2], outs[3], list(outs[4:])


def _proj_tiles(h, x):
    return h[:, 0:1536], h[:, 1536:2560], h[:, 2560:3584], h[:, 3584:4608], h[:, 4608:6656], x


def _gate_mix_tiles(y_ret, h_e, b_gate, y_sb):
    gates = jax.nn.sigmoid(h_e + b_gate)
    return y_ret, gates[:, :D_MODEL] * y_sb + gates[:, D_MODEL:] * y_ret


def _gate_mix_grad_tiles(d_mix, h_e, b_gate, y_sb, y_ret):
    gates = jax.nn.sigmoid(h_e + b_gate)
    g0, g1 = gates[:, :D_MODEL], gates[:, D_MODEL:]
    d_e = jnp.concatenate([d_mix * y_sb * g0 * (1.0 - g0), d_mix * y_ret * g1 * (1.0 - g1)], axis=1)
    return d_mix * g0, d_mix * g1, d_e, d_e


def _ln_stats(u):
    mu = jnp.mean(u, axis=1, keepdims=True)
    cen = u - mu
    var = jnp.mean(cen * cen, axis=1, keepdims=True)
    rstd = lax.rsqrt(var + LN_EPS)
    return cen * rstd, rstd


def _ln_input_grad(d_out, gain, xhat, rstd):
    d_hat = d_out * gain
    return rstd * (d_hat - jnp.mean(d_hat, axis=1, keepdims=True)
                   - xhat * jnp.mean(d_hat * xhat, axis=1, keepdims=True))


def _ln_tiles(sub, x_prev, gain, bias):
    xhat, rstd = _ln_stats(DN_ALPHA * x_prev + sub)
    out = xhat * gain + bias
    return out, out, xhat, rstd


def _residual_tiles(d_sub, res):
    return (d_sub + DN_ALPHA * res,)


def _ln_grad_tiles(d_sub, res, xhat, rstd, gain):
    d_out = d_sub + DN_ALPHA * res
    du = _ln_input_grad(d_out, gain, xhat, rstd)
    return du, du, d_out * xhat, d_out


def _ln_loss_tiles(sub, x_prev, gain, bias, target):
    xhat, rstd = _ln_stats(DN_ALPHA * x_prev + sub)
    diff = xhat * gain + bias - target
    d_out = diff * (1.0 / D_MODEL)
    du = _ln_input_grad(d_out, gain, xhat, rstd)
    return du, du, diff * diff, d_out * xhat, d_out


def _mem_probs(q_h, k_h):
    s = _dot(q_h, k_h, _NT) * (MEM_HEAD_DIM ** -0.5)
    e = jnp.exp(s - jnp.max(s, axis=1, keepdims=True))
    return e / jnp.sum(e, axis=1, keepdims=True)


def _xattn_fwd(q, kv):
    T, mem_len = q.shape[0], kv.shape[0]
    tq = _pick(T, (512, 256, 128))

    def body(q_ref, kv_ref, o_ref):
        for h in range(MEM_HEADS):
            cols = slice(h * MEM_HEAD_DIM, (h + 1) * MEM_HEAD_DIM)
            vcols = slice(D_MODEL + h * MEM_HEAD_DIM, D_MODEL + (h + 1) * MEM_HEAD_DIM)
            p = _mem_probs(q_ref[:, cols], kv_ref[:, cols])
            o_ref[:, cols] = _dot(p.astype(BF16), kv_ref[:, vcols], _NN).astype(BF16)

    return pl.pallas_call(
        body,
        name="xattn_fwd",
        grid=(T // tq,),
        in_specs=[pl.BlockSpec((tq, D_MODEL), lambda i: (i, 0)),
                  pl.BlockSpec((mem_len, 2 * D_MODEL), lambda i: (0, 0))],
        out_specs=pl.BlockSpec((tq, D_MODEL), lambda i: (i, 0)),
        out_shape=jax.ShapeDtypeStruct((T, D_MODEL), BF16),
        compiler_params=_params(("parallel",)),
    )(q, kv)


def _xattn_bwd(q, kv, d_o):
    T, mem_len = q.shape[0], kv.shape[0]
    tq = _pick(T, (512, 256, 128))

    def body(q_ref, kv_ref, do_ref, dq_ref, dkv_ref):
        @pl.when(pl.program_id(0) == 0)
        def _():
            dkv_ref[...] = jnp.zeros_like(dkv_ref)

        for h in range(MEM_HEADS):
            cols = slice(h * MEM_HEAD_DIM, (h + 1) * MEM_HEAD_DIM)
            vcols = slice(D_MODEL + h * MEM_HEAD_DIM, D_MODEL + (h + 1) * MEM_HEAD_DIM)
            q_h, k_h, do_h = q_ref[:, cols], kv_ref[:, cols], do_ref[:, cols]
            p = _mem_probs(q_h, k_h)
            dp = _dot(do_h, kv_ref[:, vcols], _NT)
            ds = p * (dp - jnp.sum(dp * p, axis=1, keepdims=True))
            dsb = (ds * (MEM_HEAD_DIM ** -0.5)).astype(BF16)
            dq_ref[:, cols] = _dot(dsb, k_h, _NN).astype(BF16)
            dkv_ref[:, cols] += _dot(dsb, q_h, _TN)
            dkv_ref[:, vcols] += _dot(p.astype(BF16), do_h, _TN)

    row = pl.BlockSpec((tq, D_MODEL), lambda i: (i, 0))
    full = pl.BlockSpec((mem_len, 2 * D_MODEL), lambda i: (0, 0))
    return pl.pallas_call(
        body,
        name="xattn_bwd",
        grid=(T // tq,),
        in_specs=[row, full, row],
        out_specs=[row, full],
        out_shape=[jax.ShapeDtypeStruct((T, D_MODEL), BF16), jax.ShapeDtypeStruct((mem_len, 2 * D_MODEL), F32)],
        compiler_params=_params(("arbitrary",)),
    )(q, kv, d_o)


def _swiglu_tiles(f):
    a, b = f[:, :FFN_HIDDEN], f[:, FFN_HIDDEN:]
    return f, a * jax.nn.sigmoid(a) * b


def _swiglu_grad_tiles(d_hidden, f):
    a, b = f[:, :FFN_HIDDEN], f[:, FFN_HIDDEN:]
    sig = jax.nn.sigmoid(a)
    return (jnp.concatenate([d_hidden * b * (sig * (1.0 + a * (1.0 - sig))), d_hidden * (a * sig)], axis=1),)


def _local_step(x, mem, w_in, small, target, fetch, ship):
    T = x.shape[0]
    tables = _ret_tables(T)
    memb = mem.astype(BF16)

    (h_a, h_b, h_c, h_d, h_e, xb), w_ffn = fetch(
        ("w_ffn_in", "w_ffn_out"),
        lambda rider: _as_host(rider, _mm_fused(
            x, w_in, mode="nn", name="proj_in", extras=[], pass_a=True,
            outs=[(1536, BF16), (1024, F32), (1024, BF16), (1024, F32), (2048, F32), (D_MODEL, BF16)],
            epilogue=_proj_tiles, max_rows=256, rider=rider)))
    (a_sb, r_mat), w_mix = fetch(("w_sb_o", "w_ret_o", "w_mix_o", "w_mem_q", "w_mem_kv", "w_mem_o"),
                                 lambda rider: _sb_fwd(h_a, rider))
    w = {**w_ffn, **w_mix}
    y_gated, o_pre, states = _ret_fwd(h_b, h_c, h_d, tables)
    y_sb = _mm(a_sb, w["w_sb_o"], mode="nn", out_dtype=F32, name="sb_out")
    row_f32, row_bf16 = (D_MODEL, F32), (D_MODEL, BF16)
    ln_outs = [row_f32, row_bf16, row_f32, (1, F32)]
    y_ret, mix_in = _mm_fused(y_gated, w["w_ret_o"], mode="nn", name="ret_out", extras=[h_e, small["b_gate"], y_sb],
                              outs=[row_f32, row_bf16], epilogue=_gate_mix_tiles)
    x1, x1b, xhat1, rstd1 = _mm_fused(mix_in, w["w_mix_o"], mode="nn", name="mix_out",
                                      extras=[x, small["ln1_g"], small["ln1_b"]], outs=ln_outs, epilogue=_ln_tiles)
    q_m = _mm(x1b, w["w_mem_q"], mode="nn", out_dtype=BF16, name="mem_q")
    kv_m = _mm(memb, w["w_mem_kv"], mode="nn", out_dtype=BF16, name="mem_kv")
    o_m = _xattn_fwd(q_m, kv_m)
    x2, x2b, xhat2, rstd2 = _mm_fused(o_m, w["w_mem_o"], mode="nn", name="mem_out",
                                      extras=[x1, small["ln2_g"], small["ln2_b"]], outs=ln_outs, epilogue=_ln_tiles)
    f, hidden = _mm_fused(x2b, w["w_ffn_in"], mode="nn", name="ffn_in", extras=[],
                          outs=[(2 * FFN_HIDDEN, F32), (FFN_HIDDEN, BF16)], epilogue=_swiglu_tiles)
    du_outs, col = [row_f32, row_bf16], D_MODEL
    du3, du3b, loss_cols, d_ln3_g, d_ln3_b = _mm_fused(
        hidden, w["w_ffn_out"], mode="nn", name="ffn_out", extras=[x2, small["ln3_g"], small["ln3_b"], target],
        outs=du_outs, sums=[col, col, col], epilogue=_ln_loss_tiles)

    g_ffn_out = _mm(hidden, du3b, mode="tn", out_dtype=BF16, name="g_ffn_out")
    (d_f,) = _mm_fused(du3b, w["w_ffn_out"], mode="nt", name="d_hidden", extras=[f],
                       outs=[(2 * FFN_HIDDEN, BF16)], epilogue=_swiglu_grad_tiles)
    g_ffn_in = _mm(x2b, d_f, mode="tn", out_dtype=BF16, name="g_ffn_in")
    du2, du2b, d_ln2_g, d_ln2_b = ship(
        {"w_ffn_out": g_ffn_out},
        lambda rider: _as_host(rider, _mm_fused(
            d_f, w["w_ffn_in"], mode="nt", name="d_x2", extras=[du3, xhat2, rstd2, small["ln2_g"]], outs=du_outs,
            sums=[col, col], epilogue=_ln_grad_tiles, rider=rider, max_rows=256)))
    g_mem_o = _mm(o_m, du2b, mode="tn", out_dtype=BF16, name="g_mem_o")
    d_om = _mm(du2b, w["w_mem_o"], mode="nt", out_dtype=BF16, name="d_om")
    d_qm, d_kvm = _xattn_bwd(q_m, kv_m, d_om)
    g_mem_q = _mm(x1b, d_qm, mode="tn", out_dtype=BF16, name="g_mem_q")
    g_mem_kv = _mm(memb, d_kvm.astype(BF16), mode="tn", out_dtype=BF16, name="g_mem_kv")
    du1, du1b, d_ln1_g, d_ln1_b = _mm_fused(
        d_qm, w["w_mem_q"], mode="nt", name="d_x1", extras=[du2, xhat1, rstd1, small["ln1_g"]], outs=du_outs,
        sums=[col, col], epilogue=_ln_grad_tiles)
    g_mix_o = _mm(mix_in, du1b, mode="tn", out_dtype=BF16, name="g_mix_o")
    d_ysb, d_yret, d_e, d_b_gate = _mm_fused(
        du1b, w["w_mix_o"], mode="nt", name="d_mix_in", extras=[h_e, small["b_gate"], y_sb, y_ret],
        outs=[row_bf16, row_bf16, (2 * D_MODEL, BF16)], sums=[2 * D_MODEL], epilogue=_gate_mix_grad_tiles)
    g_sb_o = _mm(a_sb, d_ysb, mode="tn", out_dtype=BF16, name="g_sb_o")
    g_ret_o = _mm(y_gated, d_yret, mode="tn", out_dtype=BF16, name="g_ret_o")
    d_asb = _mm(d_ysb, w["w_sb_o"], mode="nt", out_dtype=BF16, name="d_asb")
    d_ygated = _mm(d_yret, w["w_ret_o"], mode="nt", out_dtype=F32, name="d_ygated")
    small_grads = {"b_gate": d_b_gate, "ln1_g": d_ln1_g, "ln1_b": d_ln1_b, "ln2_g": d_ln2_g, "ln2_b": d_ln2_b,
                   "ln3_g": d_ln3_g, "ln3_b": d_ln3_b, "loss_cols": loss_cols}
    d_rq, d_rk, d_c, d_d = ship({"w_mem_kv": g_mem_kv, "w_mem_q": g_mem_q, "w_mem_o": g_mem_o, "w_mix_o": g_mix_o},
                                lambda rider: _ret_bwd(d_ygated, o_pre, states, h_b, h_c, h_d, tables, rider))
    d_q, d_k, d_v = ship({"w_ffn_in": g_ffn_in, "w_ret_o": g_ret_o, "w_sb_o": g_sb_o, "small": small_grads},
                         lambda rider: _sb_bwd(h_a, d_asb, r_mat, rider))
    d_h = [("sb_q", d_q), ("sb_k", d_k), ("sb_v", d_v), ("ret_q", d_rq), ("ret_k", d_rk), ("ret_v", d_c),
           ("ret_g", d_d), ("gate", d_e)]
    g_in = jnp.concatenate([_mm(xb, piece, mode="tn", out_dtype=BF16, name="g_in_" + tag) for tag, piece in d_h],
                           axis=1)
    (d_x,) = ship({"w_in": g_in},
                  lambda rider: _as_host(rider, _mm_fused(
                      [piece for _, piece in d_h], w_in, mode="nt", name="d_x", extras=[du1], outs=[(D_MODEL, F32)],
                      epilogue=_residual_tiles, rider=rider, max_rows=256)))
    return d_x


def _adamw_math(w, g, m, v):
    m = ADAM_B1 * m + (1.0 - ADAM_B1) * g
    v = ADAM_B2 * v + (1.0 - ADAM_B2) * jnp.square(g)
    m_hat = m / (1.0 - ADAM_B1 ** ADAM_STEP)
    v_hat = v / (1.0 - ADAM_B2 ** ADAM_STEP)
    delta = -ADAM_LR * (m_hat / (jnp.sqrt(v_hat) + ADAM_EPS) + ADAM_WD * w)
    return delta, m, v


def _adamw(parts, w, m, v, name):
    R, C = w.shape
    tr = max(t for t in range(16, min(R, 256) + 1, 16) if R % t == 0) if R >= 16 else R

    def body(p_ref, w_ref, m_ref, v_ref, g_ref, d_ref, nm_ref, nv_ref):
        g = p_ref[0].astype(F32)
        for j in range(1, N_DEV):
            g = g + p_ref[j].astype(F32)
        delta, nm, nv = _adamw_math(w_ref[...], g, m_ref[...], v_ref[...])
        g_ref[...] = g
        d_ref[...] = delta
        nm_ref[...] = nm
        nv_ref[...] = nv

    blk = pl.BlockSpec((tr, C), lambda i: (i, 0))
    out = jax.ShapeDtypeStruct((R, C), F32)
    return pl.pallas_call(
        body,
        name=name,
        grid=(R // tr,),
        in_specs=[pl.BlockSpec((N_DEV, tr, C), lambda i: (0, i, 0)), blk, blk, blk],
        out_specs=[blk] * 4,
        out_shape=[out] * 4,
        compiler_params=_params(("parallel",)),
    )(parts, w, m, v)


_SHARD_AXIS = {"w_in": 1, "w_sb_o": 1, "w_ret_o": 0, "w_mix_o": 0, "w_mem_q": 0, "w_mem_kv": 1, "w_mem_o": 0,
               "w_ffn_in": 1, "w_ffn_out": 0}
_MATRICES = tuple(_SHARD_AXIS)
_SMALL = ("b_gate", "ln1_g", "ln1_b", "ln2_g", "ln2_b", "ln3_g", "ln3_b")
_WEIGHT_ORDER = ("w_in", "b_gate", "w_sb_o", "w_ret_o", "w_mix_o", "ln1_g", "ln1_b", "w_mem_q", "w_mem_kv", "w_mem_o",
                 "ln2_g", "ln2_b", "w_ffn_in", "w_ffn_out", "ln3_g", "ln3_b")


def _assemble(name, gathered):
    if _SHARD_AXIS[name] == 0:
        return gathered.reshape(-1, gathered.shape[2])
    return jnp.transpose(gathered, (1, 0, 2)).reshape(gathered.shape[1], -1)


def _to_slots(name, full):
    if _SHARD_AXIS[name] == 0:
        return full.reshape(N_DEV, full.shape[0] // N_DEV, full.shape[1])
    return jnp.transpose(full.reshape(full.shape[0], N_DEV, full.shape[1] // N_DEV), (1, 0, 2))


SMALL_ROWS = 16


def _pack_small(vals):
    return jnp.concatenate([vals["b_gate"].reshape(2, D_MODEL)] + [vals[n] for n in _SMALL[1:]], axis=0)


def _unpack_small(packed):
    out = {"b_gate": packed[0:2].reshape(1, 2 * D_MODEL)}
    for i, n in enumerate(_SMALL[1:]):
        out[n] = packed[2 + i:3 + i]
    return out


def kernel(x, mem, w_in, b_gate, w_sb_o, w_ret_o, w_mix_o, ln1_g, ln1_b, w_mem_q, w_mem_kv, w_mem_o, ln2_g, ln2_b, w_ffn_in, w_ffn_out, ln3_g, ln3_b, loss_target, m_w_in, m_b_gate, m_w_sb_o, m_w_ret_o, m_w_mix_o, m_ln1_g, m_ln1_b, m_w_mem_q, m_w_mem_kv, m_w_mem_o, m_ln2_g, m_ln2_b, m_w_ffn_in, m_w_ffn_out, m_ln3_g, m_ln3_b, v_w_in, v_b_gate, v_w_sb_o, v_w_ret_o, v_w_mix_o, v_ln1_g, v_ln1_b, v_w_mem_q, v_w_mem_kv, v_w_mem_o, v_ln2_g, v_ln2_b, v_w_ffn_in, v_w_ffn_out, v_ln3_g, v_ln3_b):
    weights = dict(w_in=w_in, b_gate=b_gate, w_sb_o=w_sb_o, w_ret_o=w_ret_o, w_mix_o=w_mix_o, ln1_g=ln1_g, ln1_b=ln1_b,
                   w_mem_q=w_mem_q, w_mem_kv=w_mem_kv, w_mem_o=w_mem_o, ln2_g=ln2_g, ln2_b=ln2_b, w_ffn_in=w_ffn_in,
                   w_ffn_out=w_ffn_out, ln3_g=ln3_g, ln3_b=ln3_b)
    mom1 = dict(w_in=m_w_in, b_gate=m_b_gate, w_sb_o=m_w_sb_o, w_ret_o=m_w_ret_o, w_mix_o=m_w_mix_o, ln1_g=m_ln1_g,
                ln1_b=m_ln1_b, w_mem_q=m_w_mem_q, w_mem_kv=m_w_mem_kv, w_mem_o=m_w_mem_o, ln2_g=m_ln2_g, ln2_b=m_ln2_b,
                w_ffn_in=m_w_ffn_in, w_ffn_out=m_w_ffn_out, ln3_g=m_ln3_g, ln3_b=m_ln3_b)
    mom2 = dict(w_in=v_w_in, b_gate=v_b_gate, w_sb_o=v_w_sb_o, w_ret_o=v_w_ret_o, w_mix_o=v_w_mix_o, ln1_g=v_ln1_g,
                ln1_b=v_ln1_b, w_mem_q=v_w_mem_q, w_mem_kv=v_w_mem_kv, w_mem_o=v_w_mem_o, ln2_g=v_ln2_g, ln2_b=v_ln2_b,
                w_ffn_in=v_w_ffn_in, w_ffn_out=v_w_ffn_out, ln3_g=v_ln3_g, ln3_b=v_ln3_b)

    (gathered_in,) = _exchange([weights["w_in"][0].astype(BF16)], False, "gather_w_in")
    received = {}

    def fetch(names, host):
        res = host(_Rider([weights[n][0].astype(BF16) for n in names], False))
        return res[:-1], {n: _assemble(n, g) for n, g in zip(names, res[-1])}

    def ship(grads, host):
        names = list(grads)
        bufs = []
        for n in names:
            if n == "small":
                part = jnp.concatenate([_pack_small(grads[n]), grads[n]["loss_cols"],
                                        jnp.zeros((SMALL_ROWS - 9, D_MODEL), F32)], axis=0)
                bufs.append(jnp.broadcast_to(part[None], (N_DEV,) + part.shape))
            else:
                bufs.append(_to_slots(n, grads[n]).astype(BF16))
        res = host(_Rider(bufs, True))
        received.update(zip(names, res[-1]))
        return res[:-1]

    small = {n: weights[n] for n in _SMALL}
    d_x = _local_step(x[0], mem[0], _assemble("w_in", gathered_in), small, loss_target[0], fetch, ship)

    new = {}
    for n in _MATRICES:
        new[n] = _adamw(received[n], weights[n][0], mom1[n][0], mom2[n][0], "adamw_" + n)
    packed = _adamw(received["small"][:, :8], _pack_small({n: weights[n] for n in _SMALL}),
                    _pack_small({n: mom1[n] for n in _SMALL}), _pack_small({n: mom2[n] for n in _SMALL}), "adamw_small")
    small_new = [_unpack_small(p) for p in packed]
    loss = jnp.sum(received["small"][:, 8]) * (0.5 / D_MODEL)

    outs = [loss, d_x[None]]
    for slot in range(4):
        for n in _WEIGHT_ORDER:
            outs.append(new[n][slot][None] if n in new else small_new[slot][n])
    return tuple(outs)
```

```python
import functools
import math

import jax
import jax.numpy as jnp
from jax import lax
from jax.experimental import pallas as pl
from jax.experimental.pallas import tpu as pltpu

F32 = jnp.float32
BF16 = jnp.bfloat16

N_DEV = 8
D_MODEL = 1024
SB_HEAD_DIM = 64
SB_WIDTH = 512
RET_HEADS = 4
RET_QK_DIM = 128
RET_V_DIM = 256
RET_QK_WIDTH = 512
RET_V_WIDTH = 1024
RET_CHUNK = 128
RET_STEP_CHUNKS = 4
ROPE_BASE = 10000.0
MEM_HEADS = 4
MEM_HEAD_DIM = 256
FFN_HIDDEN = 2816
DN_ALPHA = 2.0 ** 0.25
LN_EPS = 1e-5
ADAM_LR = 0.001
ADAM_B1 = 0.9
ADAM_B2 = 0.999
ADAM_EPS = 1e-08
ADAM_WD = 0.01
ADAM_STEP = 10

VMEM_LIMIT_BYTES = 52 * 1024 * 1024
LANES = 128
SB_KEY_BLOCK = 128
SB_Q_BLOCK = 256
SB_DEAD_LOG = -105.0

MESH_AXES = ("x", "y", "c")


def _pick(dim, prefs):
    for p in prefs:
        if dim % p == 0:
            return p
    return dim


def _params(sem):
    return pltpu.CompilerParams(dimension_semantics=sem, vmem_limit_bytes=VMEM_LIMIT_BYTES)


def _dot(a, b, dims):
    return lax.dot_general(a, b, (dims, ((), ())), preferred_element_type=F32)


_NN = ((1,), (0,))
_NT = ((1,), (1,))
_TN = ((0,), (0,))


def _my_index():
    return 4 * lax.axis_index("x") + 2 * lax.axis_index("y") + lax.axis_index("c")


def _peer(k):
    x, y, c = lax.axis_index("x"), lax.axis_index("y"), lax.axis_index("c")
    bx, by, bc = (k >> 2) & 1, (k >> 1) & 1, k & 1
    px = (1 - x) if bx else x
    py = (1 - y) if by else y
    pc = (1 - c) if bc else c
    return (px, py, pc), 4 * px + 2 * py + pc


class _Rider:
    def __init__(self, bufs, scatter, ranges=None):
        self.bufs, self.scatter, self.n = list(bufs), scatter, len(bufs)
        self.ranges = list(ranges) if ranges is not None else [(0, N_DEV)] * self.n
        self.specs = [pl.BlockSpec(memory_space=pl.ANY)] * self.n
        self.out_shape = [jax.ShapeDtypeStruct((N_DEV,) + (b.shape[1:] if scatter else b.shape), b.dtype)
                          for b in self.bufs]
        self.scratch = [pltpu.SemaphoreType.DMA((self.n, N_DEV - 1)), pltpu.SemaphoreType.DMA((self.n, N_DEV - 1)),
                        pltpu.SemaphoreType.DMA((self.n,))]

    def _remote(self, ride, a, k, src_ref, slot, to):
        _, dst, (send_sems, recv_sems, _) = ride
        return pltpu.make_async_remote_copy(src_ref=src_ref, dst_ref=dst[a].at[slot], send_sem=send_sems.at[a, k],
                                            recv_sem=recv_sems.at[a, k], device_id=to,
                                            device_id_type=pl.DeviceIdType.MESH)

    def _local(self, ride, a):
        src, dst, (_, _, local_sems) = ride
        me = _my_index()
        lo, hi = self.ranges[a]
        own = src[a].at[jnp.clip(me - lo, 0, hi - lo - 1)] if self.scatter else src[a]
        return pltpu.make_async_copy(own, dst[a].at[me], local_sems.at[a])

    def _direct(self, ride, a):
        src = ride[0]
        me = _my_index()
        lo, hi = self.ranges[a]
        out = []
        for k in range(1, N_DEV):
            peer, peer_idx = _peer(k)
            slot = jnp.clip(peer_idx - lo, 0, hi - lo - 1)
            sends = jnp.logical_and(peer_idx >= lo, peer_idx < hi)
            out.append((self._remote(ride, a, k - 1, src[a].at[slot], me, peer), sends))
        return out

    def _receives(self, a):
        me = _my_index()
        lo, hi = self.ranges[a]
        return jnp.logical_and(me >= lo, me < hi)

    def _two_level(self, ride, a):
        src, dst = ride[0], ride[1]
        x, y, c = lax.axis_index("x"), lax.axis_index("y"), lax.axis_index("c")
        me, sibling = _my_index(), (x, y, 1 - c)
        chips = [(1 - x, y), (x, 1 - y), (1 - x, 1 - y)]
        first = [self._remote(ride, a, 0, src[a], me, sibling)]
        passed, landing = [], [self._remote(ride, a, 0, src[a], me + 1 - 2 * c, sibling)]
        for j, (px, py) in enumerate(chips):
            first.append(self._remote(ride, a, 1 + j, src[a], me, (px, py, c)))
            theirs = 4 * px + 2 * py + c
            passed.append(self._remote(ride, a, 4 + j, dst[a].at[theirs], theirs, sibling))
            landing.append(self._remote(ride, a, 1 + j, src[a], theirs, (px, py, c)))
        for j, (px, py) in enumerate(chips):
            landing.append(self._remote(ride, a, 4 + j, src[a], 4 * px + 2 * py + 1 - c, sibling))
        return first, passed, landing

    def start(self, ride):
        for a in range(self.n):
            if not self.scatter:
                self._local(ride, a).start()
                for cp in self._two_level(ride, a)[0]:
                    cp.start()
            elif self.ranges[a] == (0, N_DEV):
                self._local(ride, a).start()
                for cp, _ in self._direct(ride, a):
                    cp.start()
            else:
                pl.when(self._receives(a))(self._local(ride, a).start)
                for cp, sends in self._direct(ride, a):
                    pl.when(sends)(cp.start)

    def finish(self, ride):
        if self.scatter:
            for a in range(self.n):
                if self.ranges[a] == (0, N_DEV):
                    for cp, _ in self._direct(ride, a):
                        cp.wait()
                    self._local(ride, a).wait()
                else:
                    receives = self._receives(a)
                    for cp, sends in self._direct(ride, a):
                        pl.when(sends)(cp.wait_send)
                        pl.when(receives)(cp.wait_recv)
                    pl.when(receives)(self._local(ride, a).wait)
            return
        levels = [self._two_level(ride, a) for a in range(self.n)]
        for first, passed, landing in levels:
            for j, cp in enumerate(passed):
                landing[1 + j].wait_recv()
                cp.start()
        for a, (first, passed, landing) in enumerate(levels):
            landing[0].wait_recv()
            for cp in landing[4:]:
                cp.wait_recv()
            for cp in first + passed:
                cp.wait_send()
            self._local(ride, a).wait()

    def start_at_first(self, ids, ride):
        first = functools.reduce(jnp.logical_and, [i == 0 for i in ids])

        @pl.when(first)
        def _():
            self.start(ride)

    def wait_at_last(self, ids, grid, ride):
        last = functools.reduce(jnp.logical_and, [i == g - 1 for i, g in zip(ids, grid)])

        @pl.when(last)
        def _():
            self.finish(ride)


def _exchange(bufs, scatter, name):
    rider = _Rider(bufs, scatter)

    def body(*refs):
        ride = (refs[:rider.n], refs[rider.n:2 * rider.n], refs[2 * rider.n:])
        rider.start(ride)
        rider.finish(ride)

    return pl.pallas_call(
        body,
        name=name,
        in_specs=rider.specs,
        out_specs=rider.specs,
        out_shape=rider.out_shape,
        scratch_shapes=rider.scratch,
    )(*rider.bufs)


MM_RESIDENT_B_BYTES = 14 * 1024 * 1024
MM_A_TILE_BYTES = 4 * 1024 * 1024
MM_OUT_TILE_BYTES = 6 * 1024 * 1024


def _mm_tiles(mode, M, N, K, a_bytes, out_bytes):
    if mode != "tn" and K * N * 2 <= MM_RESIDENT_B_BYTES:
        for tm in (1024, 512, 256, 128):
            if M % tm == 0 and tm * K * a_bytes <= MM_A_TILE_BYTES and tm * N * out_bytes <= MM_OUT_TILE_BYTES:
                return tm, N, K
    if mode == "tn":
        return (_pick(M, (1024, 1408, 512, 256, 128)), _pick(N, (1024, 1664, 1408, 512, 256, 128)),
                _pick(K, (2048, 1024, 512, 256, 128)))
    return _pick(M, (1024, 512, 256, 128)), _pick(N, (512, 256, 128)), _pick(K, (1024, 512, 256, 128))


def _mm(a, b, *, mode, out_dtype, name, res=None, res_scale=1.0, rider=None):
    if mode == "nn":
        (M, K), (K2, N) = a.shape, b.shape
    elif mode == "nt":
        (M, K), (N, K2) = a.shape, b.shape
    else:
        (K, M), (K2, N) = a.shape, b.shape
    assert K == K2, (a.shape, b.shape, mode)
    out_bytes = jnp.dtype(out_dtype).itemsize + (4 if res is not None else 0)
    tm, tn, tk = _mm_tiles(mode, M, N, K, a.dtype.itemsize, out_bytes)
    grid = (M // tm, N // tn, K // tk)
    nk = grid[2]
    dims = {"nn": _NN, "nt": _NT, "tn": _TN}[mode]
    n_in = 2 + (res is not None)
    n_ride = rider.n if rider is not None else 0

    def body(*refs):
        a_ref, b_ref = refs[:2]
        r_ref = refs[2] if res is not None else None
        o_ref = refs[n_in + n_ride]
        rest = refs[n_in + 2 * n_ride + 1:]
        acc_ref = rest[0] if nk > 1 else None
        ids = [pl.program_id(d) for d in range(3)]
        if rider is not None:
            ride = (refs[n_in:n_in + n_ride], refs[n_in + n_ride + 1:n_in + 2 * n_ride + 1], rest[-3:])
            rider.start_at_first(ids, ride)
        part = _dot(a_ref[...].astype(BF16), b_ref[...].astype(BF16), dims)

        def finish(total):
            if r_ref is not None:
                total = total + res_scale * r_ref[...]
            o_ref[...] = total.astype(out_dtype)

        if nk == 1:
            finish(part)
        else:
            k = ids[2]

            @pl.when(k == 0)
            def _():
                acc_ref[...] = part

            @pl.when(k > 0)
            def _():
                acc_ref[...] += part

            @pl.when(k == nk - 1)
            def _():
                finish(acc_ref[...])

        if rider is not None:
            rider.wait_at_last(ids, grid, ride)

    if mode == "nn":
        a_spec = pl.BlockSpec((tm, tk), lambda i, j, k: (i, k))
        b_spec = pl.BlockSpec((tk, tn), lambda i, j, k: (k, j))
    elif mode == "nt":
        a_spec = pl.BlockSpec((tm, tk), lambda i, j, k: (i, k))
        b_spec = pl.BlockSpec((tn, tk), lambda i, j, k: (j, k))
    else:
        a_spec = pl.BlockSpec((tk, tm), lambda i, j, k: (k, i))
        b_spec = pl.BlockSpec((tk, tn), lambda i, j, k: (k, j))
    o_spec = pl.BlockSpec((tm, tn), lambda i, j, k: (i, j))
    in_specs = [a_spec, b_spec] + ([o_spec] if res is not None else [])
    args = (a, b) + ((res,) if res is not None else ())
    out_specs, out_shape = [o_spec], [jax.ShapeDtypeStruct((M, N), out_dtype)]
    scratch = [pltpu.VMEM((tm, tn), F32)] if nk > 1 else []
    sem = ("parallel", "parallel", "arbitrary")
    if rider is not None:
        in_specs, args = in_specs + rider.specs, args + tuple(rider.bufs)
        out_specs, out_shape = out_specs + rider.specs, out_shape + rider.out_shape
        scratch = scratch + rider.scratch
        sem = ("arbitrary",) * 3
    outs = pl.pallas_call(
        body,
        name=name,
        grid=grid,
        in_specs=in_specs,
        out_specs=out_specs,
        out_shape=out_shape,
        scratch_shapes=scratch,
        compiler_params=_params(sem),
    )(*args)
    return outs[0] if rider is None else (outs[0], list(outs[1:]))


def _mm_host(a, b, *, rider, **kw):
    out = _mm(a, b, rider=rider, **kw)
    return out if rider is not None else (out, [])


def _as_host(rider, results):
    return results if rider is not None else tuple(results) + ([],)


MM_FUSED_MARGIN_BYTES = 10 * 1024 * 1024
MM_FUSED_MAX_ROWS = 512


def _col_sum_update(acc_ref, val, first):
    part = jnp.sum(val.reshape(val.shape[0] // 8, 8, val.shape[1]), axis=0)

    @pl.when(first)
    def _():
        acc_ref[...] = part

    @pl.when(jnp.logical_not(first))
    def _():
        acc_ref[...] += part


def _mm_fused(a, b, *, mode, name, extras, outs, epilogue, sums=(), rider=None, max_rows=MM_FUSED_MAX_ROWS,
              pass_a=False):
    parts = list(a) if isinstance(a, (list, tuple)) else [a]
    M, K = parts[0].shape[0], sum(p.shape[1] for p in parts)
    if mode == "nn":
        (K2, N), b_dims = b.shape, _NN
    else:
        (N, K2), b_dims = b.shape, _NT
    assert K == K2, (K, b.shape, mode)
    rows = parts + [e for e in extras if e.shape[0] == M]
    per_row = 2 * (sum(e.shape[1] * e.dtype.itemsize for e in rows)
                   + sum(c * jnp.dtype(d).itemsize for c, d in outs)) + 2 * N * 4
    budget = VMEM_LIMIT_BYTES - K * N * 2 - MM_FUSED_MARGIN_BYTES
    tm = next(t for t in (512, 256, 128, 64, 32, 16) if t <= max_rows and M % t == 0 and t * per_row <= budget)
    steps = M // tm
    n_a, n_x, n_o, n_s = len(parts), len(extras), len(outs), len(sums)
    n_ride = rider.n if rider is not None else 0

    def body(*refs):
        a_refs, b_ref = refs[:n_a], refs[n_a]
        x_refs = refs[n_a + 1:n_a + 1 + n_x]
        base = n_a + 1 + n_x + n_ride
        o_refs, s_refs = refs[base:base + n_o], refs[base + n_o:base + n_o + n_s]
        acc_refs = refs[base + n_o + n_s + n_ride:base + n_o + 2 * n_s + n_ride]
        ids = [pl.program_id(0)]
        if rider is not None:
            ride = (refs[n_a + 1 + n_x:base], refs[base + n_o + n_s:base + n_o + n_s + n_ride], refs[-3:])
            rider.start_at_first(ids, ride)
        a_tile = a_refs[0][...]
        a_bf16 = a_tile.astype(BF16) if n_a == 1 else jnp.concatenate([r[...].astype(BF16) for r in a_refs], axis=1)
        prod = _dot(a_bf16, b_ref[...], b_dims)
        tiles = epilogue(prod, *([a_tile] if pass_a else []), *[r[...] for r in x_refs])
        for o_ref, t in zip(o_refs, tiles[:n_o]):
            o_ref[...] = t.astype(o_ref.dtype)
        for acc_ref, t in zip(acc_refs, tiles[n_o:]):
            _col_sum_update(acc_ref, t, ids[0] == 0)
        if n_s:
            @pl.when(ids[0] == steps - 1)
            def _():
                for s_ref, acc_ref in zip(s_refs, acc_refs):
                    s_ref[...] = jnp.sum(acc_ref[...], axis=0, keepdims=True)
        if rider is not None:
            rider.wait_at_last(ids, (steps,), ride)

    in_specs = [pl.BlockSpec((tm, p.shape[1]), lambda i: (i, 0)) for p in parts]
    in_specs.append(pl.BlockSpec(b.shape, lambda i: (0, 0), pipeline_mode=pl.Buffered(1)))
    for e in extras:
        in_specs.append(pl.BlockSpec((tm, e.shape[1]), lambda i: (i, 0)) if e.shape[0] == M
                        else pl.BlockSpec(e.shape, lambda i: (0, 0)))
    out_specs = ([pl.BlockSpec((tm, c), lambda i: (i, 0)) for c, _ in outs]
                 + [pl.BlockSpec((1, c), lambda i: (0, 0)) for c in sums])
    out_shape = ([jax.ShapeDtypeStruct((M, c), d) for c, d in outs]
                 + [jax.ShapeDtypeStruct((1, c), F32) for c in sums])
    args = tuple(parts) + (b,) + tuple(extras)
    scratch = [pltpu.VMEM((8, c), F32) for c in sums]
    if rider is not None:
        in_specs, args = in_specs + rider.specs, args + tuple(rider.bufs)
        out_specs, out_shape = out_specs + rider.specs, out_shape + rider.out_shape
        scratch = scratch + rider.scratch
    res = pl.pallas_call(
        body,
        name=name,
        grid=(steps,),
        in_specs=in_specs,
        out_specs=out_specs,
        out_shape=out_shape,
        scratch_shapes=scratch,
        compiler_params=_params(("arbitrary",) if (n_s or rider is not None) else ("parallel",)),
    )(*args)
    return tuple(res[:n_o + n_s]) + ((list(res[n_o + n_s:]),) if rider is not None else ())


def _pair_rows(blk, lane_is_a):
    zero = jnp.zeros_like(blk)
    return jnp.concatenate([jnp.where(lane_is_a, blk, zero), jnp.where(lane_is_a, zero, blk)], axis=0)


SB_STRIP = 32
SB_FWD_PAIRS = 4
SB_BWD_PAIRS = 2
SB_GROUP = 2
W_IN_EARLY_SHARDS = 2


def _pair_lanes(p):
    return slice(p * LANES, (p + 1) * LANES)


def _sb_scan_matrices():
    o = lax.broadcasted_iota(jnp.int32, (2 * LANES, 4 * LANES), 0)
    c = lax.broadcasted_iota(jnp.int32, (2 * LANES, 4 * LANES), 1) & (2 * LANES - 1)
    same = (o >= LANES) == (c >= LANES)
    oo, cc = o & (LANES - 1), c & (LANES - 1)
    return (jnp.where(same & (cc > oo), -1.0, 0.0).astype(BF16), jnp.where(same & (cc < oo), 1.0, 0.0).astype(BF16))


def _sb_causal_masks(tq):
    d = lax.broadcasted_iota(jnp.int32, (tq // SB_KEY_BLOCK, SB_KEY_BLOCK, tq), 0)
    k = lax.broadcasted_iota(jnp.int32, (tq // SB_KEY_BLOCK, SB_KEY_BLOCK, tq), 1)
    t = lax.broadcasted_iota(jnp.int32, (tq // SB_KEY_BLOCK, SB_KEY_BLOCK, tq), 2)
    return jnp.where(d * SB_KEY_BLOCK + k < t, 1.0, 0.0).astype(F32)


def _sb_log_terms(z):
    minus_abs = lax.bitcast_convert_type(lax.bitcast_convert_type(z, jnp.uint32) | jnp.uint32(0x80000000), F32)
    spent = jnp.maximum(z, 0.0) + jnp.log(1.0 + jnp.exp(minus_abs))
    return spent, z - spent


def _sb_store_split(ref, strip, val, cols):
    hi = val.astype(BF16)
    ref[pl.ds(strip * SB_STRIP, SB_STRIP), cols] = hi
    ref[pl.ds(2 * LANES + strip * SB_STRIP, SB_STRIP), cols] = (val - hi.astype(F32)).astype(BF16)


def _sb_lanes(tq, diag):
    if diag == "left":
        return 0, tq // 2
    first = 0 if diag is None else diag * SB_KEY_BLOCK
    return first, tq - first


def _lane_add(full, part, lanes):
    first, width = lanes
    pieces = [full[:, :first]] if first else []
    pieces.append(full[:, first:first + width] + part)
    if first + width < full.shape[1]:
        pieces.append(full[:, first + width:])
    return pieces[0] if len(pieces) == 1 else jnp.concatenate(pieces, axis=1)


def _sb_fwd(h_a, rider=None):
    assert SB_FWD_PAIRS == 4
    T = h_a.shape[0]
    tq = _pick(T, (SB_Q_BLOCK, SB_KEY_BLOCK))
    nq, per_q, nkb = T // tq, tq // SB_KEY_BLOCK, T // SB_KEY_BLOCK
    assert per_q % SB_GROUP == 0
    n_strips = 2 * LANES // SB_STRIP
    n_ride = rider.n if rider is not None else 0
    after_m, _ = _sb_scan_matrices()
    causal_m = _sb_causal_masks(tq)
    pairs = SB_FWD_PAIRS

    def body(*refs):
        q_ref, k_ref, v_ref, after_ref, causal_ref = refs[:5]
        a_ref, r_ref, n_ref = refs[5 + n_ride:8 + n_ride]
        z_ref, lb_ref, split_ref, w_ref = refs[8 + 2 * n_ride:12 + 2 * n_ride]
        ids = [pl.program_id(0)]
        if rider is not None:
            ride = (refs[5:5 + n_ride], refs[8 + n_ride:8 + 2 * n_ride], refs[-3:])
            rider.start_at_first(ids, ride)
        i = ids[0]
        q_t = [(q_ref[:, _pair_lanes(p)].astype(F32).T * (SB_HEAD_DIM ** -0.5)).astype(BF16) for p in range(pairs)]
        lane_is_a = lax.broadcasted_iota(jnp.int32, (SB_KEY_BLOCK, LANES), 1) < SB_HEAD_DIM

        def tiles(kbs, diags, carry):
            nb = len(kbs)
            lanes = [_sb_lanes(tq, d) for d in diags]
            cols = [slice(first, first + width) for first, width in lanes]
            acc_t, ra, rb = [list(c) for c in carry]
            ks = [pl.multiple_of(kb * SB_KEY_BLOCK, SB_KEY_BLOCK) for kb in kbs]
            slot = lambda p, b: p * nb + b

            def causal(b, s):
                return causal_ref[diags[b], pl.ds((s * SB_STRIP) % SB_KEY_BLOCK, SB_STRIP), cols[b]]

            vv = {}
            for b in range(nb):
                for p in range(pairs):
                    kk = _pair_rows(k_ref[pl.ds(ks[b], SB_KEY_BLOCK), _pair_lanes(p)], lane_is_a)
                    vv[p, b] = _pair_rows(v_ref[pl.ds(ks[b], SB_KEY_BLOCK), _pair_lanes(p)], lane_is_a)
                    z_ref[slot(p, b), :, cols[b]] = _dot(kk, q_t[p][:, cols[b]], _NN)
            sums = {}
            for b in range(nb):
                for p in range(pairs):
                    part = [jnp.zeros((8, lanes[b][1]), F32), jnp.zeros((8, lanes[b][1]), F32)]
                    for s in range(n_strips):
                        rows = pl.ds(s * SB_STRIP, SB_STRIP)
                        spent, log_beta = _sb_log_terms(z_ref[slot(p, b), rows, cols[b]])
                        lb_ref[slot(p, b), rows, cols[b]] = log_beta
                        if isinstance(diags[b], int):
                            spent = spent * causal(b, s)
                        _sb_store_split(split_ref.at[slot(p, b)], s, spent, cols[b])
                        head = (s * SB_STRIP) // SB_KEY_BLOCK
                        part[head] = part[head] + jnp.sum(spent.reshape(SB_STRIP // 8, 8, lanes[b][1]), axis=0)
                    sums[p, b] = part
            for b in range(nb):
                for p in range(pairs):
                    z_ref[slot(p, b), :, cols[b]] = _dot(after_ref[...], split_ref[slot(p, b), :, cols[b]], _NN)
            for b in range(nb):
                for p in range(pairs):
                    for s in range(n_strips):
                        rows = pl.ds(s * SB_STRIP, SB_STRIP)
                        start = (ra[p] if (s * SB_STRIP) < SB_KEY_BLOCK else rb[p])[:, cols[b]]
                        w = jnp.exp(lb_ref[slot(p, b), rows, cols[b]] + z_ref[slot(p, b), rows, cols[b]] + start)
                        if isinstance(diags[b], int):
                            w = w * causal(b, s)
                        w_ref[slot(p, b), rows, cols[b]] = w.astype(BF16)
                    r_ref[2 * p, kbs[b]] = ra[p]
                    r_ref[2 * p + 1, kbs[b]] = rb[p]
                    ra[p] = _lane_add(ra[p], -jnp.sum(sums[p, b][0], axis=0, keepdims=True), lanes[b])
                    rb[p] = _lane_add(rb[p], -jnp.sum(sums[p, b][1], axis=0, keepdims=True), lanes[b])
            for b in range(nb):
                for p in range(pairs):
                    acc_t[p] = _lane_add(acc_t[p], _dot(vv[p, b], w_ref[slot(p, b), :, cols[b]], _TN), lanes[b])
            return tuple(acc_t), tuple(ra), tuple(rb)

        carry = (tuple(jnp.zeros((LANES, tq), F32) for _ in range(pairs)),
                 tuple(jnp.zeros((1, tq), F32) for _ in range(pairs)),
                 tuple(jnp.zeros((1, tq), F32) for _ in range(pairs)))
        own = list(reversed(range(per_q)))
        n_full = i * per_q
        carry = lax.cond(
            i > 0,
            lambda cc: tiles([n_full + d for d in own] + [n_full - 1 - b for b in range(SB_GROUP)],
                             own + [None] * SB_GROUP, cc),
            lambda cc: tiles([n_full + d for d in own], own, cc), carry)
        first_walked = jnp.where(i > 0, SB_GROUP, 0).astype(jnp.int32)

        def top_of(sums_a, sums_b, first):
            return jnp.max(functools.reduce(jnp.maximum, [r[:, first:] for r in sums_a + sums_b]))

        def alive(c):
            return jnp.logical_and(c[0] < n_full, top_of(c[2], c[3], 0) > SB_DEAD_LOG)

        def step(c):
            kbs = [n_full - 1 - c[0] - b for b in range(SB_GROUP)]
            return (c[0] + SB_GROUP,) + lax.cond(
                top_of(c[2], c[3], tq // 2) > SB_DEAD_LOG,
                lambda cc: tiles(kbs, [None] * SB_GROUP, cc), lambda cc: tiles(kbs, ["left"] * SB_GROUP, cc), c[1:])

        walked, acc_t, _, _ = lax.while_loop(alive, step, (first_walked,) + carry)
        for p in range(pairs):
            a_ref[:, _pair_lanes(p)] = acc_t[p].T.astype(BF16)
        n_ref[...] = jnp.zeros(n_ref.shape, F32) + walked.astype(F32)
        if rider is not None:
            rider.wait_at_last(ids, (nq,), ride)

    wide = pairs * LANES
    in_specs = [pl.BlockSpec((tq, wide), lambda i: (i, 0)),
                pl.BlockSpec((T, wide), lambda i: (0, 1), pipeline_mode=pl.Buffered(1)),
                pl.BlockSpec((T, wide), lambda i: (0, 2), pipeline_mode=pl.Buffered(1)),
                pl.BlockSpec(after_m.shape, lambda i: (0, 0), pipeline_mode=pl.Buffered(1)),
                pl.BlockSpec(causal_m.shape, lambda i: (0, 0, 0), pipeline_mode=pl.Buffered(1))]
    out_specs = [pl.BlockSpec((tq, wide), lambda i: (i, 0)),
                 pl.BlockSpec((2 * pairs, nkb, 1, tq), lambda i: (0, 0, 0, i)),
                 pl.BlockSpec((1, 8, LANES), lambda i: (i, 0, 0))]
    out_shape = [jax.ShapeDtypeStruct((T, SB_WIDTH), BF16), jax.ShapeDtypeStruct((2 * pairs, nkb, 1, T), F32),
                 jax.ShapeDtypeStruct((nq, 8, LANES), F32)]
    args = (h_a, h_a, h_a, after_m, causal_m)
    slots = pairs * (per_q + SB_GROUP)
    scratch = [pltpu.VMEM((slots, 2 * LANES, tq), F32), pltpu.VMEM((slots, 2 * LANES, tq), F32),
               pltpu.VMEM((slots, 4 * LANES, tq), BF16), pltpu.VMEM((slots, 2 * LANES, tq), BF16)]
    if rider is not None:
        in_specs, args = in_specs + rider.specs, args + tuple(rider.bufs)
        out_specs, out_shape = out_specs + rider.specs, out_shape + rider.out_shape
        scratch = scratch + rider.scratch
    outs = pl.pallas_call(
        body,
        name="sb_fwd",
        grid=(nq,),
        in_specs=in_specs,
        out_specs=out_specs,
        out_shape=out_shape,
        scratch_shapes=scratch,
        compiler_params=_params(("arbitrary",)),
    )(*args)
    return outs[0], (outs[1], outs[2]), list(outs[3:])


def _sb_bwd(h_a, d_out, saved, rider=None):
    r_mat, walked_blocks = saved
    T = h_a.shape[0]
    tq = _pick(T, (SB_Q_BLOCK, SB_KEY_BLOCK))
    nq, per_q, nkb = T // tq, tq // SB_KEY_BLOCK, T // SB_KEY_BLOCK
    n_strips = 2 * LANES // SB_STRIP
    after_m, before_m = _sb_scan_matrices()
    causal_m = _sb_causal_masks(tq)
    pairs = SB_BWD_PAIRS
    groups = 4 // pairs
    n_ride = rider.n if rider is not None else 0

    def body(*refs):
        q_ref, k_ref, v_ref, do_ref, r_ref, n_ref, after_ref, before_ref, causal_ref = refs[:9]
        dq_ref, dk_ref, dv_ref = refs[9 + n_ride:12 + n_ride]
        z_ref, lb_ref, split_ref, w_ref, da_ref, dz_ref = refs[12 + 2 * n_ride:18 + 2 * n_ride]
        ids = [pl.program_id(0), pl.program_id(1)]
        if rider is not None:
            ride = (refs[9:9 + n_ride], refs[12 + n_ride:12 + 2 * n_ride], refs[-3:])
            rider.start_at_first(ids, ride)
        i = ids[1]

        @pl.when(i == 0)
        def _():
            dk_ref[...] = jnp.zeros_like(dk_ref)
            dv_ref[...] = jnp.zeros_like(dv_ref)

        scale = SB_HEAD_DIM ** -0.5
        q = [q_ref[:, _pair_lanes(p)] for p in range(pairs)]
        d_o = [do_ref[:, _pair_lanes(p)] for p in range(pairs)]
        q_t = [(x.astype(F32).T * scale).astype(BF16) for x in q]
        do_t = [x.astype(F32).T.astype(BF16) for x in d_o]
        lane_is_a = lax.broadcasted_iota(jnp.int32, (SB_KEY_BLOCK, LANES), 1) < SB_HEAD_DIM

        def tiles(kbs, diags, carry):
            nb = len(kbs)
            lanes = [_sb_lanes(tq, d) for d in diags]
            cols = [slice(first, first + width) for first, width in lanes]
            dq_t, ca, cb = [list(c) for c in carry]
            ks = [pl.multiple_of(kb * SB_KEY_BLOCK, SB_KEY_BLOCK) for kb in kbs]
            slot = lambda p, b: p * nb + b

            def causal(b, s):
                return causal_ref[diags[b], pl.ds((s * SB_STRIP) % SB_KEY_BLOCK, SB_STRIP), cols[b]]

            kk, vv = {}, {}
            for b in range(nb):
                for p in range(pairs):
                    kk[p, b] = _pair_rows(k_ref[pl.ds(ks[b], SB_KEY_BLOCK), _pair_lanes(p)], lane_is_a)
                    vv[p, b] = _pair_rows(v_ref[pl.ds(ks[b], SB_KEY_BLOCK), _pair_lanes(p)], lane_is_a)
                    z_ref[slot(p, b), :, cols[b]] = _dot(kk[p, b], q_t[p][:, cols[b]], _NN)
            for b in range(nb):
                for p in range(pairs):
                    for s in range(n_strips):
                        rows = pl.ds(s * SB_STRIP, SB_STRIP)
                        spent, log_beta = _sb_log_terms(z_ref[slot(p, b), rows, cols[b]])
                        lb_ref[slot(p, b), rows, cols[b]] = log_beta
                        if isinstance(diags[b], int):
                            spent = spent * causal(b, s)
                        _sb_store_split(split_ref.at[slot(p, b)], s, spent, cols[b])
            for b in range(nb):
                for p in range(pairs):
                    z_ref[slot(p, b), :, cols[b]] = _dot(after_ref[...], split_ref[slot(p, b), :, cols[b]], _NN)
                    da_ref[slot(p, b), :, cols[b]] = _dot(vv[p, b], do_t[p][:, cols[b]], _NN)
            sums = {}
            for b in range(nb):
                for p in range(pairs):
                    part = [jnp.zeros((8, lanes[b][1]), F32), jnp.zeros((8, lanes[b][1]), F32)]
                    for s in range(n_strips):
                        rows = pl.ds(s * SB_STRIP, SB_STRIP)
                        start = r_ref[2 * p + (s * SB_STRIP) // SB_KEY_BLOCK, kbs[b]][:, cols[b]]
                        w = jnp.exp(lb_ref[slot(p, b), rows, cols[b]] + z_ref[slot(p, b), rows, cols[b]] + start)
                        if isinstance(diags[b], int):
                            w = w * causal(b, s)
                        w_ref[slot(p, b), rows, cols[b]] = w.astype(BF16)
                        da = da_ref[slot(p, b), rows, cols[b]] * w
                        da_ref[slot(p, b), rows, cols[b]] = da
                        _sb_store_split(split_ref.at[slot(p, b)], s, da, cols[b])
                        head = (s * SB_STRIP) // SB_KEY_BLOCK
                        part[head] = part[head] + jnp.sum(da.reshape(SB_STRIP // 8, 8, lanes[b][1]), axis=0)
                    sums[p, b] = part
            for b in range(nb):
                for p in range(pairs):
                    z_ref[slot(p, b), :, cols[b]] = _dot(before_ref[...], split_ref[slot(p, b), :, cols[b]], _NN)
            for b in range(nb):
                for p in range(pairs):
                    for s in range(n_strips):
                        rows = pl.ds(s * SB_STRIP, SB_STRIP)
                        base = (ca[p] if (s * SB_STRIP) < SB_KEY_BLOCK else cb[p])[:, cols[b]]
                        sig = jnp.exp(lb_ref[slot(p, b), rows, cols[b]])
                        dz = (da_ref[slot(p, b), rows, cols[b]] * (1.0 - sig)
                              - (z_ref[slot(p, b), rows, cols[b]] + base) * sig)
                        if isinstance(diags[b], int):
                            dz = dz * causal(b, s)
                        dz_ref[slot(p, b), rows, cols[b]] = (dz * scale).astype(BF16)
                    ca[p] = _lane_add(ca[p], jnp.sum(sums[p, b][0], axis=0, keepdims=True), lanes[b])
                    cb[p] = _lane_add(cb[p], jnp.sum(sums[p, b][1], axis=0, keepdims=True), lanes[b])
            for b in range(nb):
                for p in range(pairs):
                    dq_t[p] = _lane_add(dq_t[p], _dot(kk[p, b], dz_ref[slot(p, b), :, cols[b]], _TN), lanes[b])
                    dkk = _dot(dz_ref[slot(p, b), :, cols[b]], q[p][cols[b], :], _NN)
                    dvv = _dot(w_ref[slot(p, b), :, cols[b]], d_o[p][cols[b], :], _NN)
                    here = (pl.ds(ks[b], SB_KEY_BLOCK), _pair_lanes(p))
                    dk_ref[here] += jnp.where(lane_is_a, dkk[:SB_KEY_BLOCK], dkk[SB_KEY_BLOCK:])
                    dv_ref[here] += jnp.where(lane_is_a, dvv[:SB_KEY_BLOCK], dvv[SB_KEY_BLOCK:])
            return tuple(dq_t), tuple(ca), tuple(cb)

        n_full = i * per_q
        groups_walked = jnp.clip(jnp.max(n_ref[...]).astype(jnp.int32), 0, n_full) // SB_GROUP
        carry = (tuple(jnp.zeros((LANES, tq), F32) for _ in range(pairs)),
                 tuple(jnp.zeros((1, tq), F32) for _ in range(pairs)),
                 tuple(jnp.zeros((1, tq), F32) for _ in range(pairs)))

        def below(j, c):
            kbs = [n_full - (groups_walked - j) * SB_GROUP + b for b in range(SB_GROUP)]
            starts = [r_ref[h, kbs[-1]][:, tq // 2:] for h in range(2 * pairs)]
            reaches = jnp.max(functools.reduce(jnp.maximum, starts)) > SB_DEAD_LOG
            return lax.cond(reaches, lambda cc: tiles(kbs, [None] * SB_GROUP, cc),
                            lambda cc: tiles(kbs, ["left"] * SB_GROUP, cc), c)

        carry = lax.fori_loop(0, groups_walked, below, carry)
        own = list(range(per_q))
        carry = tiles([i * per_q + d for d in own], own, carry)
        for p in range(pairs):
            dq_ref[:, _pair_lanes(p)] = carry[0][p].T.astype(BF16)
        if rider is not None:
            rider.wait_at_last(ids, (groups, nq), ride)

    wide = pairs * LANES
    mat = pl.BlockSpec(after_m.shape, lambda g, i: (0, 0), pipeline_mode=pl.Buffered(1))
    in_specs = [pl.BlockSpec((tq, wide), lambda g, i: (i, g)),
                pl.BlockSpec((T, wide), lambda g, i: (0, groups + g), pipeline_mode=pl.Buffered(1)),
                pl.BlockSpec((T, wide), lambda g, i: (0, 2 * groups + g), pipeline_mode=pl.Buffered(1)),
                pl.BlockSpec((tq, wide), lambda g, i: (i, g)),
                pl.BlockSpec((2 * pairs, nkb, 1, tq), lambda g, i: (g, 0, 0, i)),
                pl.BlockSpec((1, 8, LANES), lambda g, i: (i, 0, 0)),
                mat, mat,
                pl.BlockSpec(causal_m.shape, lambda g, i: (0, 0, 0), pipeline_mode=pl.Buffered(1))]
    out_specs = [pl.BlockSpec((tq, wide), lambda g, i: (i, g)),
                 pl.BlockSpec((T, wide), lambda g, i: (0, g)),
                 pl.BlockSpec((T, wide), lambda g, i: (0, g))]
    out_shape = [jax.ShapeDtypeStruct((T, SB_WIDTH), BF16), jax.ShapeDtypeStruct((T, SB_WIDTH), F32),
                 jax.ShapeDtypeStruct((T, SB_WIDTH), F32)]
    args = (h_a, h_a, h_a, d_out, r_mat, walked_blocks, after_m, before_m, causal_m)
    slots = pairs * max(per_q, SB_GROUP)
    scratch = [pltpu.VMEM((slots, 2 * LANES, tq), F32), pltpu.VMEM((slots, 2 * LANES, tq), F32),
               pltpu.VMEM((slots, 4 * LANES, tq), BF16), pltpu.VMEM((slots, 2 * LANES, tq), BF16),
               pltpu.VMEM((slots, 2 * LANES, tq), F32), pltpu.VMEM((slots, 2 * LANES, tq), BF16)]
    if rider is not None:
        in_specs, args = in_specs + rider.specs, args + tuple(rider.bufs)
        out_specs, out_shape = out_specs + rider.specs, out_shape + rider.out_shape
        scratch = scratch + rider.scratch
    outs = pl.pallas_call(
        body,
        name="sb_bwd",
        grid=(groups, nq),
        in_specs=in_specs,
        out_specs=out_specs,
        out_shape=out_shape,
        scratch_shapes=scratch,
        compiler_params=_params(("arbitrary", "arbitrary") if rider is not None else ("parallel", "arbitrary")),
    )(*args)
    return outs[0], outs[1], outs[2], list(outs[3:])


def _ret_tables(T):
    half = RET_QK_DIM // 2
    inv = 1.0 / (ROPE_BASE ** (jnp.arange(half, dtype=F32) / half))
    ang = jnp.arange(T, dtype=F32)[:, None] * inv[None, :]
    cos, sin = jnp.cos(ang), jnp.sin(ang)
    cos_t = jnp.concatenate([cos, cos], axis=1)
    sin_t = jnp.concatenate([-sin, sin], axis=1)
    log_gamma = jnp.log1p(-jnp.exp2(-5.0 - jnp.arange(RET_HEADS, dtype=F32)))
    idx = jnp.arange(RET_CHUNK, dtype=F32)
    rel = idx[:, None] - idx[None, :]
    decay = jnp.where(rel[None] >= 0, jnp.exp(log_gamma[:, None, None] * jnp.maximum(rel, 0.0)[None]), 0.0)
    k_decay = jnp.exp(log_gamma[None, :] * (RET_CHUNK - 1.0 - idx)[:, None])
    q_decay = jnp.exp(log_gamma[None, :] * (idx + 1.0)[:, None])
    chunk_decay = jnp.exp(log_gamma * RET_CHUNK)
    k_dec = jnp.broadcast_to(k_decay.T[:, :, None], (RET_HEADS, RET_CHUNK, LANES))
    q_dec = jnp.broadcast_to(q_decay.T[:, :, None], (RET_HEADS, RET_CHUNK, LANES))
    c_dec = jnp.broadcast_to(chunk_decay[:, None, None], (RET_HEADS, 8, LANES))
    return cos_t, sin_t, decay, k_dec, q_dec, c_dec


def _rotary(x, cos_t, sin_t):
    return x * cos_t + pltpu.roll(x, RET_QK_DIM // 2, 1) * sin_t


def _rotary_transpose(dy, cos_t, sin_t):
    return dy * cos_t + pltpu.roll(dy * sin_t, RET_QK_DIM // 2, 1)


def _head_norm(o):
    mu = jnp.mean(o, axis=1, keepdims=True)
    cen = o - mu
    var = jnp.mean(cen * cen, axis=1, keepdims=True)
    rstd = lax.rsqrt(var + LN_EPS)
    return cen * rstd, rstd


def _ret_specs(steps, per_step, reverse):
    def n_of(n):
        return (steps - 1 - n) if reverse else n

    rows = per_step * RET_CHUNK
    q_spec = pl.BlockSpec((rows, RET_QK_WIDTH), lambda n: (n_of(n), 0))
    k_spec = pl.BlockSpec((rows, RET_QK_WIDTH), lambda n: (n_of(n), 1))
    vv = pl.BlockSpec((rows, RET_V_WIDTH), lambda n: (n_of(n), 0))
    pos = pl.BlockSpec((rows, LANES), lambda n: (n_of(n), 0))
    per_head = pl.BlockSpec((RET_HEADS, RET_CHUNK, LANES), lambda n: (0, 0, 0))
    c_dec = pl.BlockSpec((RET_HEADS, 8, LANES), lambda n: (0, 0, 0))
    state = pl.BlockSpec((RET_HEADS, per_step, RET_QK_DIM, RET_V_DIM), lambda n: (0, n_of(n), 0, 0))
    return q_spec, k_spec, vv, pos, per_head, c_dec, state


def _qk_cols(h):
    return slice(h * RET_QK_DIM, (h + 1) * RET_QK_DIM)


def _v_cols(h):
    return slice(h * RET_V_DIM, (h + 1) * RET_V_DIM)


def _ret_fwd(h_b, h_c, h_d, tables):
    T = h_b.shape[0]
    nc = T // RET_CHUNK
    per_step = _pick(nc, (RET_STEP_CHUNKS, 1))
    steps = nc // per_step
    q_spec, k_spec, vv, pos, per_head, c_dec, state = _ret_specs(steps, per_step, False)

    def body(q_ref, k_ref, v_ref, g_ref, cos_ref, sin_ref, dec_ref, kd_ref, qd_ref, cd_ref,
             y_ref, o_ref, st_ref, state_ref):
        @pl.when(pl.program_id(0) == 0)
        def _():
            state_ref[...] = jnp.zeros_like(state_ref)

        for c in range(per_step):
            rows = pl.ds(c * RET_CHUNK, RET_CHUNK)
            cos_t, sin_t = cos_ref[rows, :], sin_ref[rows, :]
            for h in range(RET_HEADS):
                q = _rotary(q_ref[rows, _qk_cols(h)], cos_t, sin_t) * (RET_QK_DIM ** -0.5)
                k = _rotary(k_ref[rows, _qk_cols(h)], cos_t, sin_t)
                v = v_ref[rows, _v_cols(h)]
                prev = state_ref[h]
                scores = _dot(q.astype(BF16), k.astype(BF16), _NT) * dec_ref[h]
                inner = _dot(scores.astype(BF16), v, _NN)
                cross = _dot((q * qd_ref[h]).astype(BF16), prev.astype(BF16), _NN)
                o = inner + cross
                st_ref[h, c] = prev
                kv = _dot((k * kd_ref[h]).astype(BF16), v, _TN)
                state_ref[h] = prev * cd_ref[h, 0:1, 0:1] + kv
                o_ref[rows, _v_cols(h)] = o
                normed, _ = _head_norm(o)
                gate = g_ref[rows, _v_cols(h)]
                y_ref[rows, _v_cols(h)] = (gate * jax.nn.sigmoid(gate) * normed).astype(BF16)

    return pl.pallas_call(
        body,
        name="ret_fwd",
        grid=(steps,),
        in_specs=[q_spec, k_spec, vv, vv, pos, pos, per_head, per_head, per_head, c_dec],
        out_specs=[vv, vv, state],
        out_shape=[jax.ShapeDtypeStruct((T, RET_V_WIDTH), BF16),
                   jax.ShapeDtypeStruct((T, RET_V_WIDTH), F32),
                   jax.ShapeDtypeStruct((RET_HEADS, nc, RET_QK_DIM, RET_V_DIM), F32)],
        scratch_shapes=[pltpu.VMEM((RET_HEADS, RET_QK_DIM, RET_V_DIM), F32)],
        compiler_params=_params(("arbitrary",)),
    )(h_b, h_b, h_c, h_d, *tables)


def _ret_bwd(d_y, o_pre, states, h_b, h_c, h_d, tables, rider=None):
    T = h_b.shape[0]
    nc = T // RET_CHUNK
    per_step = _pick(nc, (RET_STEP_CHUNKS, 1))
    steps = nc // per_step
    q_spec, k_spec, vv, pos, per_head, c_dec, state = _ret_specs(steps, per_step, True)
    n_ride = rider.n if rider is not None else 0

    def body(*refs):
        (dy_ref, o_ref, st_ref, q_ref, k_ref, v_ref, g_ref, cos_ref, sin_ref, dec_ref, kd_ref, qd_ref,
         cd_ref) = refs[:13]
        dq_ref, dk_ref, dv_ref, dg_ref = refs[13 + n_ride:17 + n_ride]
        carry_ref = refs[17 + 2 * n_ride]
        ids = [pl.program_id(0)]
        if rider is not None:
            ride = (refs[13:13 + n_ride], refs[17 + n_ride:17 + 2 * n_ride], refs[-3:])
            rider.start_at_first(ids, ride)

        @pl.when(ids[0] == 0)
        def _():
            carry_ref[...] = jnp.zeros_like(carry_ref)

        scale = RET_QK_DIM ** -0.5
        for c in reversed(range(per_step)):
            rows = pl.ds(c * RET_CHUNK, RET_CHUNK)
            cos_t, sin_t = cos_ref[rows, :], sin_ref[rows, :]
            for h in range(RET_HEADS):
                q = _rotary(q_ref[rows, _qk_cols(h)], cos_t, sin_t) * scale
                k = _rotary(k_ref[rows, _qk_cols(h)], cos_t, sin_t)
                v = v_ref[rows, _v_cols(h)]
                decay, k_dec, q_dec = dec_ref[h], kd_ref[h], qd_ref[h]
                chunk_decay = cd_ref[h, 0:1, 0:1]
                state = st_ref[h, c].astype(BF16)
                later = carry_ref[h]
                later_b = later.astype(BF16)

                gate = g_ref[rows, _v_cols(h)]
                sig = jax.nn.sigmoid(gate)
                silu = gate * sig
                normed, rstd = _head_norm(o_ref[rows, _v_cols(h)])
                d_y = dy_ref[rows, _v_cols(h)]
                dg_ref[rows, _v_cols(h)] = (d_y * normed * (sig * (1.0 + gate * (1.0 - sig)))).astype(BF16)
                d_n = d_y * silu
                d_o = rstd * (d_n - jnp.mean(d_n, axis=1, keepdims=True)
                              - normed * jnp.mean(d_n * normed, axis=1, keepdims=True))
                d_ob = d_o.astype(BF16)

                qb, kb = q.astype(BF16), k.astype(BF16)
                qd_b, kd_b = (q * q_dec).astype(BF16), (k * k_dec).astype(BF16)
                scores = _dot(qb, kb, _NT) * decay
                d_scores = (_dot(d_ob, v, _NT) * decay).astype(BF16)
                dq = _dot(d_scores, kb, _NN) + _dot(d_ob, state, _NT) * q_dec
                dk = _dot(d_scores, qb, _TN) + _dot(v, later_b, _NT) * k_dec
                dv = _dot(scores.astype(BF16), d_ob, _TN) + _dot(kd_b, later_b, _NN)
                carry_ref[h] = _dot(qd_b, d_ob, _TN) + chunk_decay * later
                dq_ref[rows, _qk_cols(h)] = _rotary_transpose(dq * scale, cos_t, sin_t).astype(BF16)
                dk_ref[rows, _qk_cols(h)] = _rotary_transpose(dk, cos_t, sin_t).astype(BF16)
                dv_ref[rows, _v_cols(h)] = dv.astype(BF16)
        if rider is not None:
            rider.wait_at_last(ids, (steps,), ride)

    qk_out = pl.BlockSpec((per_step * RET_CHUNK, RET_QK_WIDTH), lambda n: (steps - 1 - n, 0))
    in_specs = [vv, vv, state, q_spec, k_spec, vv, vv, pos, pos, per_head, per_head, per_head, c_dec]
    out_specs = [qk_out, qk_out, vv, vv]
    out_shape = [jax.ShapeDtypeStruct((T, RET_QK_WIDTH), BF16), jax.ShapeDtypeStruct((T, RET_QK_WIDTH), BF16),
                 jax.ShapeDtypeStruct((T, RET_V_WIDTH), BF16), jax.ShapeDtypeStruct((T, RET_V_WIDTH), BF16)]
    args = (d_y, o_pre, states, h_b, h_b, h_c, h_d) + tuple(tables)
    scratch = [pltpu.VMEM((RET_HEADS, RET_QK_DIM, RET_V_DIM), F32)]
    if rider is not None:
        in_specs, args = in_specs + rider.specs, args + tuple(rider.bufs)
        out_specs, out_shape = out_specs + rider.specs, out_shape + rider.out_shape
        scratch = scratch + rider.scratch
    outs = pl.pallas_call(
        body,
        name="ret_bwd",
        grid=(steps,),
        in_specs=in_specs,
        out_specs=out_specs,
        out_shape=out_shape,
        scratch_shapes=scratch,
        compiler_params=_params(("arbitrary",)),
    )(*args)
    return outs[0], outs[1], outs[2], outs[3], list(outs[4:])


def _proj_tiles(h, x):
    return h[:, 0:1536], h[:, 1536:2560], h[:, 2560:3584], h[:, 3584:4608], h[:, 4608:6656], x


def _gate_mix_tiles(y_ret, h_e, b_gate, y_sb):
    gates = jax.nn.sigmoid(h_e + b_gate)
    return y_ret, gates[:, :D_MODEL] * y_sb + gates[:, D_MODEL:] * y_ret


def _gate_mix_grad_tiles(d_mix, h_e, b_gate, y_sb, y_ret):
    gates = jax.nn.sigmoid(h_e + b_gate)
    g0, g1 = gates[:, :D_MODEL], gates[:, D_MODEL:]
    d_e = jnp.concatenate([d_mix * y_sb * g0 * (1.0 - g0), d_mix * y_ret * g1 * (1.0 - g1)], axis=1)
    return d_mix * g0, d_mix * g1, d_e, d_e


def _ln_stats(u):
    mu = jnp.mean(u, axis=1, keepdims=True)
    cen = u - mu
    var = jnp.mean(cen * cen, axis=1, keepdims=True)
    rstd = lax.rsqrt(var + LN_EPS)
    return cen * rstd, rstd


def _ln_input_grad(d_out, gain, xhat, rstd):
    d_hat = d_out * gain
    return rstd * (d_hat - jnp.mean(d_hat, axis=1, keepdims=True)
                   - xhat * jnp.mean(d_hat * xhat, axis=1, keepdims=True))


def _ln_tiles(sub, x_prev, gain, bias):
    xhat, rstd = _ln_stats(DN_ALPHA * x_prev + sub)
    out = xhat * gain + bias
    return out, out, xhat, rstd


def _residual_tiles(d_sub, res):
    return (d_sub + DN_ALPHA * res,)


def _ln_grad_tiles(d_sub, res, xhat, rstd, gain):
    d_out = d_sub + DN_ALPHA * res
    du = _ln_input_grad(d_out, gain, xhat, rstd)
    return du, du, d_out * xhat, d_out


def _ln_loss_tiles(sub, x_prev, gain, bias, target):
    xhat, rstd = _ln_stats(DN_ALPHA * x_prev + sub)
    diff = xhat * gain + bias - target
    d_out = diff * (1.0 / D_MODEL)
    du = _ln_input_grad(d_out, gain, xhat, rstd)
    return du, du, diff * diff, d_out * xhat, d_out


def _mem_probs(q_h, k_h):
    s = _dot(q_h, k_h, _NT) * (MEM_HEAD_DIM ** -0.5)
    e = jnp.exp(s - jnp.max(s, axis=1, keepdims=True))
    return e / jnp.sum(e, axis=1, keepdims=True)


def _xattn_fwd(q, kv):
    T, mem_len = q.shape[0], kv.shape[0]
    tq = _pick(T, (512, 256, 128))

    def body(q_ref, kv_ref, o_ref):
        for h in range(MEM_HEADS):
            cols = slice(h * MEM_HEAD_DIM, (h + 1) * MEM_HEAD_DIM)
            vcols = slice(D_MODEL + h * MEM_HEAD_DIM, D_MODEL + (h + 1) * MEM_HEAD_DIM)
            p = _mem_probs(q_ref[:, cols], kv_ref[:, cols])
            o_ref[:, cols] = _dot(p.astype(BF16), kv_ref[:, vcols], _NN).astype(BF16)

    return pl.pallas_call(
        body,
        name="xattn_fwd",
        grid=(T // tq,),
        in_specs=[pl.BlockSpec((tq, D_MODEL), lambda i: (i, 0)),
                  pl.BlockSpec((mem_len, 2 * D_MODEL), lambda i: (0, 0))],
        out_specs=pl.BlockSpec((tq, D_MODEL), lambda i: (i, 0)),
        out_shape=jax.ShapeDtypeStruct((T, D_MODEL), BF16),
        compiler_params=_params(("parallel",)),
    )(q, kv)


def _xattn_bwd(q, kv, d_o):
    T, mem_len = q.shape[0], kv.shape[0]
    tq = _pick(T, (512, 256, 128))

    def body(q_ref, kv_ref, do_ref, dq_ref, dkv_ref):
        @pl.when(pl.program_id(0) == 0)
        def _():
            dkv_ref[...] = jnp.zeros_like(dkv_ref)

        for h in range(MEM_HEADS):
            cols = slice(h * MEM_HEAD_DIM, (h + 1) * MEM_HEAD_DIM)
            vcols = slice(D_MODEL + h * MEM_HEAD_DIM, D_MODEL + (h + 1) * MEM_HEAD_DIM)
            q_h, k_h, do_h = q_ref[:, cols], kv_ref[:, cols], do_ref[:, cols]
            p = _mem_probs(q_h, k_h)
            dp = _dot(do_h, kv_ref[:, vcols], _NT)
            ds = p * (dp - jnp.sum(dp * p, axis=1, keepdims=True))
            dsb = (ds * (MEM_HEAD_DIM ** -0.5)).astype(BF16)
            dq_ref[:, cols] = _dot(dsb, k_h, _NN).astype(BF16)
            dkv_ref[:, cols] += _dot(dsb, q_h, _TN)
            dkv_ref[:, vcols] += _dot(p.astype(BF16), do_h, _TN)

    row = pl.BlockSpec((tq, D_MODEL), lambda i: (i, 0))
    full = pl.BlockSpec((mem_len, 2 * D_MODEL), lambda i: (0, 0))
    return pl.pallas_call(
        body,
        name="xattn_bwd",
        grid=(T // tq,),
        in_specs=[row, full, row],
        out_specs=[row, full],
        out_shape=[jax.ShapeDtypeStruct((T, D_MODEL), BF16), jax.ShapeDtypeStruct((mem_len, 2 * D_MODEL), F32)],
        compiler_params=_params(("arbitrary",)),
    )(q, kv, d_o)


def _swiglu_tiles(f):
    a, b = f[:, :FFN_HIDDEN], f[:, FFN_HIDDEN:]
    return f, a * jax.nn.sigmoid(a) * b


def _swiglu_grad_tiles(d_hidden, f):
    a, b = f[:, :FFN_HIDDEN], f[:, FFN_HIDDEN:]
    sig = jax.nn.sigmoid(a)
    return (jnp.concatenate([d_hidden * b * (sig * (1.0 + a * (1.0 - sig))), d_hidden * (a * sig)], axis=1),)


def _local_step(x, mem, w_in, small, target, fetch, ship):
    T = x.shape[0]
    tables = _ret_tables(T)
    memb = mem.astype(BF16)

    (h_a, h_b, h_c, h_d, h_e, xb), w_ffn = fetch(
        ("w_ffn_in", "w_ffn_out"),
        lambda rider: _as_host(rider, _mm_fused(
            x, w_in, mode="nn", name="proj_in", extras=[], pass_a=True,
            outs=[(1536, BF16), (1024, F32), (1024, BF16), (1024, F32), (2048, F32), (D_MODEL, BF16)],
            epilogue=_proj_tiles, max_rows=256, rider=rider)))
    (a_sb, r_mat), w_mix = fetch(("w_sb_o", "w_ret_o", "w_mix_o", "w_mem_q", "w_mem_kv", "w_mem_o"),
                                 lambda rider: _sb_fwd(h_a, rider))
    w = {**w_ffn, **w_mix}
    y_gated, o_pre, states = _ret_fwd(h_b, h_c, h_d, tables)
    y_sb = _mm(a_sb, w["w_sb_o"], mode="nn", out_dtype=F32, name="sb_out")
    row_f32, row_bf16 = (D_MODEL, F32), (D_MODEL, BF16)
    ln_outs = [row_f32, row_bf16, row_f32, (1, F32)]
    y_ret, mix_in = _mm_fused(y_gated, w["w_ret_o"], mode="nn", name="ret_out", extras=[h_e, small["b_gate"], y_sb],
                              outs=[row_f32, row_bf16], epilogue=_gate_mix_tiles)
    x1, x1b, xhat1, rstd1 = _mm_fused(mix_in, w["w_mix_o"], mode="nn", name="mix_out",
                                      extras=[x, small["ln1_g"], small["ln1_b"]], outs=ln_outs, epilogue=_ln_tiles)
    q_m = _mm(x1b, w["w_mem_q"], mode="nn", out_dtype=BF16, name="mem_q")
    kv_m = _mm(memb, w["w_mem_kv"], mode="nn", out_dtype=BF16, name="mem_kv")
    o_m = _xattn_fwd(q_m, kv_m)
    x2, x2b, xhat2, rstd2 = _mm_fused(o_m, w["w_mem_o"], mode="nn", name="mem_out",
                                      extras=[x1, small["ln2_g"], small["ln2_b"]], outs=ln_outs, epilogue=_ln_tiles)
    f, hidden = _mm_fused(x2b, w["w_ffn_in"], mode="nn", name="ffn_in", extras=[],
                          outs=[(2 * FFN_HIDDEN, F32), (FFN_HIDDEN, BF16)], epilogue=_swiglu_tiles)
    du_outs, col = [row_f32, row_bf16], D_MODEL
    du3, du3b, loss_cols, d_ln3_g, d_ln3_b = _mm_fused(
        hidden, w["w_ffn_out"], mode="nn", name="ffn_out", extras=[x2, small["ln3_g"], small["ln3_b"], target],
        outs=du_outs, sums=[col, col, col], epilogue=_ln_loss_tiles)

    g_ffn_out = _mm(hidden, du3b, mode="tn", out_dtype=BF16, name="g_ffn_out")
    (d_f,) = _mm_fused(du3b, w["w_ffn_out"], mode="nt", name="d_hidden", extras=[f],
                       outs=[(2 * FFN_HIDDEN, BF16)], epilogue=_swiglu_grad_tiles)
    g_ffn_in = _mm(x2b, d_f, mode="tn", out_dtype=BF16, name="g_ffn_in")
    du2, du2b, d_ln2_g, d_ln2_b = ship(
        {"w_ffn_out": g_ffn_out},
        lambda rider: _as_host(rider, _mm_fused(
            d_f, w["w_ffn_in"], mode="nt", name="d_x2", extras=[du3, xhat2, rstd2, small["ln2_g"]], outs=du_outs,
            sums=[col, col], epilogue=_ln_grad_tiles, rider=rider, max_rows=256)))
    g_mem_o = _mm(o_m, du2b, mode="tn", out_dtype=BF16, name="g_mem_o")
    d_om = _mm(du2b, w["w_mem_o"], mode="nt", out_dtype=BF16, name="d_om")
    d_qm, d_kvm = _xattn_bwd(q_m, kv_m, d_om)
    g_mem_q = _mm(x1b, d_qm, mode="tn", out_dtype=BF16, name="g_mem_q")
    g_mem_kv = _mm(memb, d_kvm.astype(BF16), mode="tn", out_dtype=BF16, name="g_mem_kv")
    du1, du1b, d_ln1_g, d_ln1_b = _mm_fused(
        d_qm, w["w_mem_q"], mode="nt", name="d_x1", extras=[du2, xhat1, rstd1, small["ln1_g"]], outs=du_outs,
        sums=[col, col], epilogue=_ln_grad_tiles)
    g_mix_o = _mm(mix_in, du1b, mode="tn", out_dtype=BF16, name="g_mix_o")
    d_ysb, d_yret, d_e, d_b_gate = _mm_fused(
        du1b, w["w_mix_o"], mode="nt", name="d_mix_in", extras=[h_e, small["b_gate"], y_sb, y_ret],
        outs=[row_bf16, row_bf16, (2 * D_MODEL, BF16)], sums=[2 * D_MODEL], epilogue=_gate_mix_grad_tiles)
    g_sb_o = _mm(a_sb, d_ysb, mode="tn", out_dtype=BF16, name="g_sb_o")
    g_ret_o = _mm(y_gated, d_yret, mode="tn", out_dtype=BF16, name="g_ret_o")
    d_asb = _mm(d_ysb, w["w_sb_o"], mode="nt", out_dtype=BF16, name="d_asb")
    d_ygated = _mm(d_yret, w["w_ret_o"], mode="nt", out_dtype=F32, name="d_ygated")
    small_grads = {"b_gate": d_b_gate, "ln1_g": d_ln1_g, "ln1_b": d_ln1_b, "ln2_g": d_ln2_g, "ln2_b": d_ln2_b,
                   "ln3_g": d_ln3_g, "ln3_b": d_ln3_b, "loss_cols": loss_cols}
    d_rq, d_rk, d_c, d_d = ship({"w_mem_kv": g_mem_kv, "w_mem_q": g_mem_q, "w_mem_o": g_mem_o, "w_mix_o": g_mix_o},
                                lambda rider: _ret_bwd(d_ygated, o_pre, states, h_b, h_c, h_d, tables, rider))
    def g_in(named):
        return [_mm(xb, piece, mode="tn", out_dtype=BF16, name="g_in_" + tag) for tag, piece in named]

    late = [("ret_q", d_rq), ("ret_k", d_rk), ("ret_v", d_c), ("ret_g", d_d), ("gate", d_e)]
    g_late = jnp.concatenate(g_in(late), axis=1)
    split = W_IN_EARLY_SHARDS * (g_late.shape[1] + 3 * SB_WIDTH) // N_DEV - 3 * SB_WIDTH
    d_q, d_k, d_v = ship({"w_ffn_in": g_ffn_in, "w_ret_o": g_ret_o, "w_sb_o": g_sb_o, "small": small_grads,
                          "w_in@%d:%d" % (W_IN_EARLY_SHARDS, N_DEV): g_late[:, split:]},
                         lambda rider: _sb_bwd(h_a, d_asb, r_mat, rider))
    early = [("sb_q", d_q), ("sb_k", d_k), ("sb_v", d_v)]
    g_early = jnp.concatenate(g_in(early) + [g_late[:, :split]], axis=1)
    (d_x,) = ship({"w_in@0:%d" % W_IN_EARLY_SHARDS: g_early},
                  lambda rider: _as_host(rider, _mm_fused(
                      [piece for _, piece in early + late], w_in, mode="nt", name="d_x", extras=[du1],
                      outs=[(D_MODEL, F32)], epilogue=_residual_tiles, rider=rider, max_rows=256)))
    return d_x


def _adamw_math(w, g, m, v):
    m = ADAM_B1 * m + (1.0 - ADAM_B1) * g
    v = ADAM_B2 * v + (1.0 - ADAM_B2) * jnp.square(g)
    m_hat = m / (1.0 - ADAM_B1 ** ADAM_STEP)
    v_hat = v / (1.0 - ADAM_B2 ** ADAM_STEP)
    delta = -ADAM_LR * (m_hat / (jnp.sqrt(v_hat) + ADAM_EPS) + ADAM_WD * w)
    return delta, m, v


def _adamw(parts, w, m, v, name):
    R, C = w.shape
    tr = max(t for t in range(16, min(R, 256) + 1, 16) if R % t == 0) if R >= 16 else R

    def body(p_ref, w_ref, m_ref, v_ref, g_ref, d_ref, nm_ref, nv_ref):
        g = p_ref[0].astype(F32)
        for j in range(1, N_DEV):
            g = g + p_ref[j].astype(F32)
        delta, nm, nv = _adamw_math(w_ref[...], g, m_ref[...], v_ref[...])
        g_ref[...] = g
        d_ref[...] = delta
        nm_ref[...] = nm
        nv_ref[...] = nv

    blk = pl.BlockSpec((tr, C), lambda i: (i, 0))
    out = jax.ShapeDtypeStruct((R, C), F32)
    return pl.pallas_call(
        body,
        name=name,
        grid=(R // tr,),
        in_specs=[pl.BlockSpec((N_DEV, tr, C), lambda i: (0, i, 0)), blk, blk, blk],
        out_specs=[blk] * 4,
        out_shape=[out] * 4,
        compiler_params=_params(("parallel",)),
    )(parts, w, m, v)


_SHARD_AXIS = {"w_in": 1, "w_sb_o": 1, "w_ret_o": 0, "w_mix_o": 0, "w_mem_q": 0, "w_mem_kv": 1, "w_mem_o": 0,
               "w_ffn_in": 1, "w_ffn_out": 0}
_MATRICES = tuple(_SHARD_AXIS)
_SMALL = ("b_gate", "ln1_g", "ln1_b", "ln2_g", "ln2_b", "ln3_g", "ln3_b")
_WEIGHT_ORDER = ("w_in", "b_gate", "w_sb_o", "w_ret_o", "w_mix_o", "ln1_g", "ln1_b", "w_mem_q", "w_mem_kv", "w_mem_o",
                 "ln2_g", "ln2_b", "w_ffn_in", "w_ffn_out", "ln3_g", "ln3_b")


def _assemble(name, gathered):
    if _SHARD_AXIS[name] == 0:
        return gathered.reshape(-1, gathered.shape[2])
    return jnp.transpose(gathered, (1, 0, 2)).reshape(gathered.shape[1], -1)


def _to_slots(name, full, slots=N_DEV):
    if _SHARD_AXIS[name] == 0:
        return full.reshape(slots, full.shape[0] // slots, full.shape[1])
    return jnp.transpose(full.reshape(full.shape[0], slots, full.shape[1] // slots), (1, 0, 2))


SMALL_ROWS = 16


def _pack_small(vals):
    return jnp.concatenate([vals["b_gate"].reshape(2, D_MODEL)] + [vals[n] for n in _SMALL[1:]], axis=0)


def _unpack_small(packed):
    out = {"b_gate": packed[0:2].reshape(1, 2 * D_MODEL)}
    for i, n in enumerate(_SMALL[1:]):
        out[n] = packed[2 + i:3 + i]
    return out


def kernel(x, mem, w_in, b_gate, w_sb_o, w_ret_o, w_mix_o, ln1_g, ln1_b, w_mem_q, w_mem_kv, w_mem_o, ln2_g, ln2_b, w_ffn_in, w_ffn_out, ln3_g, ln3_b, loss_target, m_w_in, m_b_gate, m_w_sb_o, m_w_ret_o, m_w_mix_o, m_ln1_g, m_ln1_b, m_w_mem_q, m_w_mem_kv, m_w_mem_o, m_ln2_g, m_ln2_b, m_w_ffn_in, m_w_ffn_out, m_ln3_g, m_ln3_b, v_w_in, v_b_gate, v_w_sb_o, v_w_ret_o, v_w_mix_o, v_ln1_g, v_ln1_b, v_w_mem_q, v_w_mem_kv, v_w_mem_o, v_ln2_g, v_ln2_b, v_w_ffn_in, v_w_ffn_out, v_ln3_g, v_ln3_b):
    weights = dict(w_in=w_in, b_gate=b_gate, w_sb_o=w_sb_o, w_ret_o=w_ret_o, w_mix_o=w_mix_o, ln1_g=ln1_g, ln1_b=ln1_b,
                   w_mem_q=w_mem_q, w_mem_kv=w_mem_kv, w_mem_o=w_mem_o, ln2_g=ln2_g, ln2_b=ln2_b, w_ffn_in=w_ffn_in,
                   w_ffn_out=w_ffn_out, ln3_g=ln3_g, ln3_b=ln3_b)
    mom1 = dict(w_in=m_w_in, b_gate=m_b_gate, w_sb_o=m_w_sb_o, w_ret_o=m_w_ret_o, w_mix_o=m_w_mix_o, ln1_g=m_ln1_g,
                ln1_b=m_ln1_b, w_mem_q=m_w_mem_q, w_mem_kv=m_w_mem_kv, w_mem_o=m_w_mem_o, ln2_g=m_ln2_g, ln2_b=m_ln2_b,
                w_ffn_in=m_w_ffn_in, w_ffn_out=m_w_ffn_out, ln3_g=m_ln3_g, ln3_b=m_ln3_b)
    mom2 = dict(w_in=v_w_in, b_gate=v_b_gate, w_sb_o=v_w_sb_o, w_ret_o=v_w_ret_o, w_mix_o=v_w_mix_o, ln1_g=v_ln1_g,
                ln1_b=v_ln1_b, w_mem_q=v_w_mem_q, w_mem_kv=v_w_mem_kv, w_mem_o=v_w_mem_o, ln2_g=v_ln2_g, ln2_b=v_ln2_b,
                w_ffn_in=v_w_ffn_in, w_ffn_out=v_w_ffn_out, ln3_g=v_ln3_g, ln3_b=v_ln3_b)

    (gathered_in,) = _exchange([weights["w_in"][0].astype(BF16)], False, "gather_w_in")
    received = {}

    def fetch(names, host):
        res = host(_Rider([weights[n][0].astype(BF16) for n in names], False))
        return res[:-1], {n: _assemble(n, g) for n, g in zip(names, res[-1])}

    def ship(grads, host):
        names = list(grads)
        bufs, ranges = [], []
        for n in names:
            base, _, span = n.partition("@")
            lo, hi = (int(t) for t in span.split(":")) if span else (0, N_DEV)
            ranges.append((lo, hi))
            if n == "small":
                part = jnp.concatenate([_pack_small(grads[n]), grads[n]["loss_cols"],
                                        jnp.zeros((SMALL_ROWS - 9, D_MODEL), F32)], axis=0)
                bufs.append(jnp.broadcast_to(part[None], (N_DEV,) + part.shape))
            else:
                bufs.append(_to_slots(base, grads[n], hi - lo).astype(BF16))
        res = host(_Rider(bufs, True, ranges))
        received.update(zip(names, res[-1]))
        return res[:-1]

    small = {n: weights[n] for n in _SMALL}
    d_x = _local_step(x[0], mem[0], _assemble("w_in", gathered_in), small, loss_target[0], fetch, ship)

    received["w_in"] = jnp.where(_my_index() < W_IN_EARLY_SHARDS, received.pop("w_in@0:%d" % W_IN_EARLY_SHARDS),
                                 received.pop("w_in@%d:%d" % (W_IN_EARLY_SHARDS, N_DEV)))
    new = {}
    for n in _MATRICES:
        new[n] = _adamw(received[n], weights[n][0], mom1[n][0], mom2[n][0], "adamw_" + n)
    packed = _adamw(received["small"][:, :8], _pack_small({n: weights[n] for n in _SMALL}),
                    _pack_small({n: mom1[n] for n in _SMALL}), _pack_small({n: mom2[n] for n in _SMALL}), "adamw_small")
    small_new = [_unpack_small(p) for p in packed]
    loss = jnp.sum(received["small"][:, 8]) * (0.5 / D_MODEL)

    outs = [loss, d_x[None]]
    for slot in range(4):
        for n in _WEIGHT_ORDER:
            outs.append(new[n][slot][None] if n in new else small_new[slot][n])
    return tuple(outs)
```

```python
import functools
import math

import jax
import jax.numpy as jnp
from jax import lax
from jax.experimental import pallas as pl
from jax.experimental.pallas import tpu as pltpu

F32 = jnp.float32
BF16 = jnp.bfloat16

N_DEV = 8
D_MODEL = 1024
SB_HEAD_DIM = 64
SB_WIDTH = 512
RET_HEADS = 4
RET_QK_DIM = 128
RET_V_DIM = 256
RET_QK_WIDTH = 512
RET_V_WIDTH = 1024
RET_CHUNK = 128
RET_STEP_CHUNKS = 4
ROPE_BASE = 10000.0
MEM_HEADS = 4
MEM_HEAD_DIM = 256
FFN_HIDDEN = 2816
DN_ALPHA = 2.0 ** 0.25
LN_EPS = 1e-5
ADAM_LR = 0.001
ADAM_B1 = 0.9
ADAM_B2 = 0.999
ADAM_EPS = 1e-08
ADAM_WD = 0.01
ADAM_STEP = 10

VMEM_LIMIT_BYTES = 52 * 1024 * 1024
LANES = 128
SB_KEY_BLOCK = 128
SB_Q_BLOCK = 256
SB_DEAD_LOG = -105.0

MESH_AXES = ("x", "y", "c")


def _pick(dim, prefs):
    for p in prefs:
        if dim % p == 0:
            return p
    return dim


def _params(sem):
    return pltpu.CompilerParams(dimension_semantics=sem, vmem_limit_bytes=VMEM_LIMIT_BYTES)


def _dot(a, b, dims):
    return lax.dot_general(a, b, (dims, ((), ())), preferred_element_type=F32)


_NN = ((1,), (0,))
_NT = ((1,), (1,))
_TN = ((0,), (0,))


def _my_index():
    return 4 * lax.axis_index("x") + 2 * lax.axis_index("y") + lax.axis_index("c")


def _peer(k):
    x, y, c = lax.axis_index("x"), lax.axis_index("y"), lax.axis_index("c")
    bx, by, bc = (k >> 2) & 1, (k >> 1) & 1, k & 1
    px = (1 - x) if bx else x
    py = (1 - y) if by else y
    pc = (1 - c) if bc else c
    return (px, py, pc), 4 * px + 2 * py + pc


class _Rider:
    def __init__(self, bufs, scatter):
        self.bufs, self.scatter, self.n = list(bufs), scatter, len(bufs)
        self.specs = [pl.BlockSpec(memory_space=pl.ANY)] * self.n
        self.out_shape = [jax.ShapeDtypeStruct(b.shape if scatter else (N_DEV,) + b.shape, b.dtype) for b in self.bufs]
        self.scratch = [pltpu.SemaphoreType.DMA((self.n, N_DEV - 1)), pltpu.SemaphoreType.DMA((self.n, N_DEV - 1)),
                        pltpu.SemaphoreType.DMA((self.n,))]

    def _remote(self, ride, a, k, src_ref, slot, to):
        _, dst, (send_sems, recv_sems, _) = ride
        return pltpu.make_async_remote_copy(src_ref=src_ref, dst_ref=dst[a].at[slot], send_sem=send_sems.at[a, k],
                                            recv_sem=recv_sems.at[a, k], device_id=to,
                                            device_id_type=pl.DeviceIdType.MESH)

    def _local(self, ride, a):
        src, dst, (_, _, local_sems) = ride
        me = _my_index()
        return pltpu.make_async_copy(src[a].at[me] if self.scatter else src[a], dst[a].at[me], local_sems.at[a])

    def _direct(self, ride, a):
        src = ride[0]
        me = _my_index()
        out = []
        for k in range(1, N_DEV):
            peer, peer_idx = _peer(k)
            out.append(self._remote(ride, a, k - 1, src[a].at[peer_idx], me, peer))
        return out

    def _two_level(self, ride, a):
        src, dst = ride[0], ride[1]
        x, y, c = lax.axis_index("x"), lax.axis_index("y"), lax.axis_index("c")
        me, sibling = _my_index(), (x, y, 1 - c)
        chips = [(1 - x, y), (x, 1 - y), (1 - x, 1 - y)]
        first = [self._remote(ride, a, 0, src[a], me, sibling)]
        passed, landing = [], [self._remote(ride, a, 0, src[a], me + 1 - 2 * c, sibling)]
        for j, (px, py) in enumerate(chips):
            first.append(self._remote(ride, a, 1 + j, src[a], me, (px, py, c)))
            theirs = 4 * px + 2 * py + c
            passed.append(self._remote(ride, a, 4 + j, dst[a].at[theirs], theirs, sibling))
            landing.append(self._remote(ride, a, 1 + j, src[a], theirs, (px, py, c)))
        for j, (px, py) in enumerate(chips):
            landing.append(self._remote(ride, a, 4 + j, src[a], 4 * px + 2 * py + 1 - c, sibling))
        return first, passed, landing

    def start(self, ride):
        for a in range(self.n):
            self._local(ride, a).start()
            for cp in (self._direct(ride, a) if self.scatter else self._two_level(ride, a)[0]):
                cp.start()

    def finish(self, ride):
        if self.scatter:
            for a in range(self.n):
                for cp in self._direct(ride, a):
                    cp.wait()
                self._local(ride, a).wait()
            return
        levels = [self._two_level(ride, a) for a in range(self.n)]
        for first, passed, landing in levels:
            for j, cp in enumerate(passed):
                landing[1 + j].wait_recv()
                cp.start()
        for a, (first, passed, landing) in enumerate(levels):
            landing[0].wait_recv()
            for cp in landing[4:]:
                cp.wait_recv()
            for cp in first + passed:
                cp.wait_send()
            self._local(ride, a).wait()

    def start_at_first(self, ids, ride):
        first = functools.reduce(jnp.logical_and, [i == 0 for i in ids])

        @pl.when(first)
        def _():
            self.start(ride)

    def wait_at_last(self, ids, grid, ride):
        last = functools.reduce(jnp.logical_and, [i == g - 1 for i, g in zip(ids, grid)])

        @pl.when(last)
        def _():
            self.finish(ride)


def _exchange(bufs, scatter, name):
    rider = _Rider(bufs, scatter)

    def body(*refs):
        ride = (refs[:rider.n], refs[rider.n:2 * rider.n], refs[2 * rider.n:])
        rider.start(ride)
        rider.finish(ride)

    return pl.pallas_call(
        body,
        name=name,
        in_specs=rider.specs,
        out_specs=rider.specs,
        out_shape=rider.out_shape,
        scratch_shapes=rider.scratch,
    )(*rider.bufs)


MM_RESIDENT_B_BYTES = 14 * 1024 * 1024
MM_A_TILE_BYTES = 4 * 1024 * 1024
MM_OUT_TILE_BYTES = 6 * 1024 * 1024


def _mm_tiles(mode, M, N, K, a_bytes, out_bytes):
    if mode != "tn" and K * N * 2 <= MM_RESIDENT_B_BYTES:
        for tm in (1024, 512, 256, 128):
            if M % tm == 0 and tm * K * a_bytes <= MM_A_TILE_BYTES and tm * N * out_bytes <= MM_OUT_TILE_BYTES:
                return tm, N, K
    if mode == "tn":
        return (_pick(M, (1024, 1408, 512, 256, 128)), _pick(N, (1024, 1664, 1408, 512, 256, 128)),
                _pick(K, (2048, 1024, 512, 256, 128)))
    return _pick(M, (1024, 512, 256, 128)), _pick(N, (512, 256, 128)), _pick(K, (1024, 512, 256, 128))


def _mm(a, b, *, mode, out_dtype, name, res=None, res_scale=1.0, rider=None):
    if mode == "nn":
        (M, K), (K2, N) = a.shape, b.shape
    elif mode == "nt":
        (M, K), (N, K2) = a.shape, b.shape
    else:
        (K, M), (K2, N) = a.shape, b.shape
    assert K == K2, (a.shape, b.shape, mode)
    out_bytes = jnp.dtype(out_dtype).itemsize + (4 if res is not None else 0)
    tm, tn, tk = _mm_tiles(mode, M, N, K, a.dtype.itemsize, out_bytes)
    grid = (M // tm, N // tn, K // tk)
    nk = grid[2]
    dims = {"nn": _NN, "nt": _NT, "tn": _TN}[mode]
    n_in = 2 + (res is not None)
    n_ride = rider.n if rider is not None else 0

    def body(*refs):
        a_ref, b_ref = refs[:2]
        r_ref = refs[2] if res is not None else None
        o_ref = refs[n_in + n_ride]
        rest = refs[n_in + 2 * n_ride + 1:]
        acc_ref = rest[0] if nk > 1 else None
        ids = [pl.program_id(d) for d in range(3)]
        if rider is not None:
            ride = (refs[n_in:n_in + n_ride], refs[n_in + n_ride + 1:n_in + 2 * n_ride + 1], rest[-3:])
            rider.start_at_first(ids, ride)
        part = _dot(a_ref[...].astype(BF16), b_ref[...].astype(BF16), dims)

        def finish(total):
            if r_ref is not None:
                total = total + res_scale * r_ref[...]
            o_ref[...] = total.astype(out_dtype)

        if nk == 1:
            finish(part)
        else:
            k = ids[2]

            @pl.when(k == 0)
            def _():
                acc_ref[...] = part

            @pl.when(k > 0)
            def _():
                acc_ref[...] += part

            @pl.when(k == nk - 1)
            def _():
                finish(acc_ref[...])

        if rider is not None:
            rider.wait_at_last(ids, grid, ride)

    if mode == "nn":
        a_spec = pl.BlockSpec((tm, tk), lambda i, j, k: (i, k))
        b_spec = pl.BlockSpec((tk, tn), lambda i, j, k: (k, j))
    elif mode == "nt":
        a_spec = pl.BlockSpec((tm, tk), lambda i, j, k: (i, k))
        b_spec = pl.BlockSpec((tn, tk), lambda i, j, k: (j, k))
    else:
        a_spec = pl.BlockSpec((tk, tm), lambda i, j, k: (k, i))
        b_spec = pl.BlockSpec((tk, tn), lambda i, j, k: (k, j))
    o_spec = pl.BlockSpec((tm, tn), lambda i, j, k: (i, j))
    in_specs = [a_spec, b_spec] + ([o_spec] if res is not None else [])
    args = (a, b) + ((res,) if res is not None else ())
    out_specs, out_shape = [o_spec], [jax.ShapeDtypeStruct((M, N), out_dtype)]
    scratch = [pltpu.VMEM((tm, tn), F32)] if nk > 1 else []
    sem = ("parallel", "parallel", "arbitrary")
    if rider is not None:
        in_specs, args = in_specs + rider.specs, args + tuple(rider.bufs)
        out_specs, out_shape = out_specs + rider.specs, out_shape + rider.out_shape
        scratch = scratch + rider.scratch
        sem = ("arbitrary",) * 3
    outs = pl.pallas_call(
        body,
        name=name,
        grid=grid,
        in_specs=in_specs,
        out_specs=out_specs,
        out_shape=out_shape,
        scratch_shapes=scratch,
        compiler_params=_params(sem),
    )(*args)
    return outs[0] if rider is None else (outs[0], list(outs[1:]))


def _mm_host(a, b, *, rider, **kw):
    out = _mm(a, b, rider=rider, **kw)
    return out if rider is not None else (out, [])


def _as_host(rider, results):
    return results if rider is not None else tuple(results) + ([],)


MM_FUSED_MARGIN_BYTES = 10 * 1024 * 1024
MM_FUSED_MAX_ROWS = 512


def _col_sum_update(acc_ref, val, first):
    part = jnp.sum(val.reshape(val.shape[0] // 8, 8, val.shape[1]), axis=0)

    @pl.when(first)
    def _():
        acc_ref[...] = part

    @pl.when(jnp.logical_not(first))
    def _():
        acc_ref[...] += part


def _mm_fused(a, b, *, mode, name, extras, outs, epilogue, sums=(), rider=None, max_rows=MM_FUSED_MAX_ROWS,
              pass_a=False):
    parts = list(a) if isinstance(a, (list, tuple)) else [a]
    M, K = parts[0].shape[0], sum(p.shape[1] for p in parts)
    if mode == "nn":
        (K2, N), b_dims = b.shape, _NN
    else:
        (N, K2), b_dims = b.shape, _NT
    assert K == K2, (K, b.shape, mode)
    rows = parts + [e for e in extras if e.shape[0] == M]
    per_row = 2 * (sum(e.shape[1] * e.dtype.itemsize for e in rows)
                   + sum(c * jnp.dtype(d).itemsize for c, d in outs)) + 2 * N * 4
    budget = VMEM_LIMIT_BYTES - K * N * 2 - MM_FUSED_MARGIN_BYTES
    tm = next(t for t in (512, 256, 128, 64, 32, 16) if t <= max_rows and M % t == 0 and t * per_row <= budget)
    steps = M // tm
    n_a, n_x, n_o, n_s = len(parts), len(extras), len(outs), len(sums)
    n_ride = rider.n if rider is not None else 0

    def body(*refs):
        a_refs, b_ref = refs[:n_a], refs[n_a]
        x_refs = refs[n_a + 1:n_a + 1 + n_x]
        base = n_a + 1 + n_x + n_ride
        o_refs, s_refs = refs[base:base + n_o], refs[base + n_o:base + n_o + n_s]
        acc_refs = refs[base + n_o + n_s + n_ride:base + n_o + 2 * n_s + n_ride]
        ids = [pl.program_id(0)]
        if rider is not None:
            ride = (refs[n_a + 1 + n_x:base], refs[base + n_o + n_s:base + n_o + n_s + n_ride], refs[-3:])
            rider.start_at_first(ids, ride)
        a_tile = a_refs[0][...]
        a_bf16 = a_tile.astype(BF16) if n_a == 1 else jnp.concatenate([r[...].astype(BF16) for r in a_refs], axis=1)
        prod = _dot(a_bf16, b_ref[...], b_dims)
        tiles = epilogue(prod, *([a_tile] if pass_a else []), *[r[...] for r in x_refs])
        for o_ref, t in zip(o_refs, tiles[:n_o]):
            o_ref[...] = t.astype(o_ref.dtype)
        for acc_ref, t in zip(acc_refs, tiles[n_o:]):
            _col_sum_update(acc_ref, t, ids[0] == 0)
        if n_s:
            @pl.when(ids[0] == steps - 1)
            def _():
                for s_ref, acc_ref in zip(s_refs, acc_refs):
                    s_ref[...] = jnp.sum(acc_ref[...], axis=0, keepdims=True)
        if rider is not None:
            rider.wait_at_last(ids, (steps,), ride)

    in_specs = [pl.BlockSpec((tm, p.shape[1]), lambda i: (i, 0)) for p in parts]
    in_specs.append(pl.BlockSpec(b.shape, lambda i: (0, 0), pipeline_mode=pl.Buffered(1)))
    for e in extras:
        in_specs.append(pl.BlockSpec((tm, e.shape[1]), lambda i: (i, 0)) if e.shape[0] == M
                        else pl.BlockSpec(e.shape, lambda i: (0, 0)))
    out_specs = ([pl.BlockSpec((tm, c), lambda i: (i, 0)) for c, _ in outs]
                 + [pl.BlockSpec((1, c), lambda i: (0, 0)) for c in sums])
    out_shape = ([jax.ShapeDtypeStruct((M, c), d) for c, d in outs]
                 + [jax.ShapeDtypeStruct((1, c), F32) for c in sums])
    args = tuple(parts) + (b,) + tuple(extras)
    scratch = [pltpu.VMEM((8, c), F32) for c in sums]
    if rider is not None:
        in_specs, args = in_specs + rider.specs, args + tuple(rider.bufs)
        out_specs, out_shape = out_specs + rider.specs, out_shape + rider.out_shape
        scratch = scratch + rider.scratch
    res = pl.pallas_call(
        body,
        name=name,
        grid=(steps,),
        in_specs=in_specs,
        out_specs=out_specs,
        out_shape=out_shape,
        scratch_shapes=scratch,
        compiler_params=_params(("arbitrary",) if (n_s or rider is not None) else ("parallel",)),
    )(*args)
    return tuple(res[:n_o + n_s]) + ((list(res[n_o + n_s:]),) if rider is not None else ())


def _pair_rows(blk, lane_is_a):
    zero = jnp.zeros_like(blk)
    return jnp.concatenate([jnp.where(lane_is_a, blk, zero), jnp.where(lane_is_a, zero, blk)], axis=0)


SB_STRIP = 32
SB_FWD_PAIRS = 4
SB_BWD_PAIRS = 2
SB_GROUP = 2


def _pair_lanes(p):
    return slice(p * LANES, (p + 1) * LANES)


def _sb_scan_matrices():
    o = lax.broadcasted_iota(jnp.int32, (2 * LANES, 4 * LANES), 0)
    c = lax.broadcasted_iota(jnp.int32, (2 * LANES, 4 * LANES), 1) & (2 * LANES - 1)
    same = (o >= LANES) == (c >= LANES)
    oo, cc = o & (LANES - 1), c & (LANES - 1)
    return (jnp.where(same & (cc > oo), -1.0, 0.0).astype(BF16), jnp.where(same & (cc < oo), 1.0, 0.0).astype(BF16))


def _sb_causal_masks(tq):
    d = lax.broadcasted_iota(jnp.int32, (tq // SB_KEY_BLOCK, SB_KEY_BLOCK, tq), 0)
    k = lax.broadcasted_iota(jnp.int32, (tq // SB_KEY_BLOCK, SB_KEY_BLOCK, tq), 1)
    t = lax.broadcasted_iota(jnp.int32, (tq // SB_KEY_BLOCK, SB_KEY_BLOCK, tq), 2)
    return jnp.where(d * SB_KEY_BLOCK + k < t, 1.0, 0.0).astype(F32)


def _sb_log_terms(z):
    minus_abs = lax.bitcast_convert_type(lax.bitcast_convert_type(z, jnp.uint32) | jnp.uint32(0x80000000), F32)
    spent = jnp.maximum(z, 0.0) + jnp.log(1.0 + jnp.exp(minus_abs))
    return spent, z - spent


def _sb_store_split(ref, strip, val, cols):
    hi = val.astype(BF16)
    ref[pl.ds(strip * SB_STRIP, SB_STRIP), cols] = hi
    ref[pl.ds(2 * LANES + strip * SB_STRIP, SB_STRIP), cols] = (val - hi.astype(F32)).astype(BF16)


def _sb_lanes(tq, diag):
    if diag == "left":
        return 0, tq // 2
    first = 0 if diag is None else diag * SB_KEY_BLOCK
    return first, tq - first


def _lane_add(full, part, lanes):
    first, width = lanes
    pieces = [full[:, :first]] if first else []
    pieces.append(full[:, first:first + width] + part)
    if first + width < full.shape[1]:
        pieces.append(full[:, first + width:])
    return pieces[0] if len(pieces) == 1 else jnp.concatenate(pieces, axis=1)


def _sb_fwd(h_a, rider=None):
    assert SB_FWD_PAIRS == 4
    T = h_a.shape[0]
    tq = _pick(T, (SB_Q_BLOCK, SB_KEY_BLOCK))
    nq, per_q, nkb = T // tq, tq // SB_KEY_BLOCK, T // SB_KEY_BLOCK
    assert per_q % SB_GROUP == 0
    n_strips = 2 * LANES // SB_STRIP
    n_ride = rider.n if rider is not None else 0
    after_m, _ = _sb_scan_matrices()
    causal_m = _sb_causal_masks(tq)
    pairs = SB_FWD_PAIRS

    def body(*refs):
        q_ref, k_ref, v_ref, after_ref, causal_ref = refs[:5]
        a_ref, r_ref, n_ref = refs[5 + n_ride:8 + n_ride]
        z_ref, lb_ref, split_ref, w_ref = refs[8 + 2 * n_ride:12 + 2 * n_ride]
        ids = [pl.program_id(0)]
        if rider is not None:
            ride = (refs[5:5 + n_ride], refs[8 + n_ride:8 + 2 * n_ride], refs[-3:])
            rider.start_at_first(ids, ride)
        i = ids[0]
        q_t = [(q_ref[:, _pair_lanes(p)].astype(F32).T * (SB_HEAD_DIM ** -0.5)).astype(BF16) for p in range(pairs)]
        lane_is_a = lax.broadcasted_iota(jnp.int32, (SB_KEY_BLOCK, LANES), 1) < SB_HEAD_DIM

        def tiles(kbs, diags, carry):
            nb = len(kbs)
            lanes = [_sb_lanes(tq, d) for d in diags]
            cols = [slice(first, first + width) for first, width in lanes]
            acc_t, ra, rb = [list(c) for c in carry]
            ks = [pl.multiple_of(kb * SB_KEY_BLOCK, SB_KEY_BLOCK) for kb in kbs]
            slot = lambda p, b: p * nb + b

            def causal(b, s):
                return causal_ref[diags[b], pl.ds((s * SB_STRIP) % SB_KEY_BLOCK, SB_STRIP), cols[b]]

            vv = {}
            for b in range(nb):
                for p in range(pairs):
                    kk = _pair_rows(k_ref[pl.ds(ks[b], SB_KEY_BLOCK), _pair_lanes(p)], lane_is_a)
                    vv[p, b] = _pair_rows(v_ref[pl.ds(ks[b], SB_KEY_BLOCK), _pair_lanes(p)], lane_is_a)
                    z_ref[slot(p, b), :, cols[b]] = _dot(kk, q_t[p][:, cols[b]], _NN)
            sums = {}
            for b in range(nb):
                for p in range(pairs):
                    part = [jnp.zeros((8, lanes[b][1]), F32), jnp.zeros((8, lanes[b][1]), F32)]
                    for s in range(n_strips):
                        rows = pl.ds(s * SB_STRIP, SB_STRIP)
                        spent, log_beta = _sb_log_terms(z_ref[slot(p, b), rows, cols[b]])
                        lb_ref[slot(p, b), rows, cols[b]] = log_beta
                        if isinstance(diags[b], int):
                            spent = spent * causal(b, s)
                        _sb_store_split(split_ref.at[slot(p, b)], s, spent, cols[b])
                        head = (s * SB_STRIP) // SB_KEY_BLOCK
                        part[head] = part[head] + jnp.sum(spent.reshape(SB_STRIP // 8, 8, lanes[b][1]), axis=0)
                    sums[p, b] = part
            for b in range(nb):
                for p in range(pairs):
                    z_ref[slot(p, b), :, cols[b]] = _dot(after_ref[...], split_ref[slot(p, b), :, cols[b]], _NN)
            for b in range(nb):
                for p in range(pairs):
                    for s in range(n_strips):
                        rows = pl.ds(s * SB_STRIP, SB_STRIP)
                        start = (ra[p] if (s * SB_STRIP) < SB_KEY_BLOCK else rb[p])[:, cols[b]]
                        w = jnp.exp(lb_ref[slot(p, b), rows, cols[b]] + z_ref[slot(p, b), rows, cols[b]] + start)
                        if isinstance(diags[b], int):
                            w = w * causal(b, s)
                        w_ref[slot(p, b), rows, cols[b]] = w.astype(BF16)
                    r_ref[2 * p, kbs[b]] = ra[p]
                    r_ref[2 * p + 1, kbs[b]] = rb[p]
                    ra[p] = _lane_add(ra[p], -jnp.sum(sums[p, b][0], axis=0, keepdims=True), lanes[b])
                    rb[p] = _lane_add(rb[p], -jnp.sum(sums[p, b][1], axis=0, keepdims=True), lanes[b])
            for b in range(nb):
                for p in range(pairs):
                    acc_t[p] = _lane_add(acc_t[p], _dot(vv[p, b], w_ref[slot(p, b), :, cols[b]], _TN), lanes[b])
            return tuple(acc_t), tuple(ra), tuple(rb)

        carry = (tuple(jnp.zeros((LANES, tq), F32) for _ in range(pairs)),
                 tuple(jnp.zeros((1, tq), F32) for _ in range(pairs)),
                 tuple(jnp.zeros((1, tq), F32) for _ in range(pairs)))
        own = list(reversed(range(per_q)))
        n_full = i * per_q
        carry = lax.cond(
            i > 0,
            lambda cc: tiles([n_full + d for d in own] + [n_full - 1 - b for b in range(SB_GROUP)],
                             own + [None] * SB_GROUP, cc),
            lambda cc: tiles([n_full + d for d in own], own, cc), carry)
        first_walked = jnp.where(i > 0, SB_GROUP, 0).astype(jnp.int32)

        def top_of(sums_a, sums_b, first):
            return jnp.max(functools.reduce(jnp.maximum, [r[:, first:] for r in sums_a + sums_b]))

        def alive(c):
            return jnp.logical_and(c[0] < n_full, top_of(c[2], c[3], 0) > SB_DEAD_LOG)

        def step(c):
            kbs = [n_full - 1 - c[0] - b for b in range(SB_GROUP)]
            return (c[0] + SB_GROUP,) + lax.cond(
                top_of(c[2], c[3], tq // 2) > SB_DEAD_LOG,
                lambda cc: tiles(kbs, [None] * SB_GROUP, cc), lambda cc: tiles(kbs, ["left"] * SB_GROUP, cc), c[1:])

        walked, acc_t, _, _ = lax.while_loop(alive, step, (first_walked,) + carry)
        for p in range(pairs):
            a_ref[:, _pair_lanes(p)] = acc_t[p].T.astype(BF16)
        n_ref[...] = jnp.zeros(n_ref.shape, F32) + walked.astype(F32)
        if rider is not None:
            rider.wait_at_last(ids, (nq,), ride)

    wide = pairs * LANES
    in_specs = [pl.BlockSpec((tq, wide), lambda i: (i, 0)),
                pl.BlockSpec((T, wide), lambda i: (0, 1), pipeline_mode=pl.Buffered(1)),
                pl.BlockSpec((T, wide), lambda i: (0, 2), pipeline_mode=pl.Buffered(1)),
                pl.BlockSpec(after_m.shape, lambda i: (0, 0), pipeline_mode=pl.Buffered(1)),
                pl.BlockSpec(causal_m.shape, lambda i: (0, 0, 0), pipeline_mode=pl.Buffered(1))]
    out_specs = [pl.BlockSpec((tq, wide), lambda i: (i, 0)),
                 pl.BlockSpec((2 * pairs, nkb, 1, tq), lambda i: (0, 0, 0, i)),
                 pl.BlockSpec((1, 8, LANES), lambda i: (i, 0, 0))]
    out_shape = [jax.ShapeDtypeStruct((T, SB_WIDTH), BF16), jax.ShapeDtypeStruct((2 * pairs, nkb, 1, T), F32),
                 jax.ShapeDtypeStruct((nq, 8, LANES), F32)]
    args = (h_a, h_a, h_a, after_m, causal_m)
    slots = pairs * (per_q + SB_GROUP)
    scratch = [pltpu.VMEM((slots, 2 * LANES, tq), F32), pltpu.VMEM((slots, 2 * LANES, tq), F32),
               pltpu.VMEM((slots, 4 * LANES, tq), BF16), pltpu.VMEM((slots, 2 * LANES, tq), BF16)]
    if rider is not None:
        in_specs, args = in_specs + rider.specs, args + tuple(rider.bufs)
        out_specs, out_shape = out_specs + rider.specs, out_shape + rider.out_shape
        scratch = scratch + rider.scratch
    outs = pl.pallas_call(
        body,
        name="sb_fwd",
        grid=(nq,),
        in_specs=in_specs,
        out_specs=out_specs,
        out_shape=out_shape,
        scratch_shapes=scratch,
        compiler_params=_params(("arbitrary",)),
    )(*args)
    return outs[0], (outs[1], outs[2]), list(outs[3:])


def _sb_bwd(h_a, d_out, saved, rider=None):
    r_mat, walked_blocks = saved
    T = h_a.shape[0]
    tq = _pick(T, (SB_Q_BLOCK, SB_KEY_BLOCK))
    nq, per_q, nkb = T // tq, tq // SB_KEY_BLOCK, T // SB_KEY_BLOCK
    n_strips = 2 * LANES // SB_STRIP
    after_m, before_m = _sb_scan_matrices()
    causal_m = _sb_causal_masks(tq)
    pairs = SB_BWD_PAIRS
    groups = 4 // pairs
    n_ride = rider.n if rider is not None else 0

    def body(*refs):
        q_ref, k_ref, v_ref, do_ref, r_ref, n_ref, after_ref, before_ref, causal_ref = refs[:9]
        dq_ref, dk_ref, dv_ref = refs[9 + n_ride:12 + n_ride]
        z_ref, lb_ref, split_ref, w_ref, da_ref, dz_ref = refs[12 + 2 * n_ride:18 + 2 * n_ride]
        ids = [pl.program_id(0), pl.program_id(1)]
        if rider is not None:
            ride = (refs[9:9 + n_ride], refs[12 + n_ride:12 + 2 * n_ride], refs[-3:])
            rider.start_at_first(ids, ride)
        i = ids[1]

        @pl.when(i == 0)
        def _():
            dk_ref[...] = jnp.zeros_like(dk_ref)
            dv_ref[...] = jnp.zeros_like(dv_ref)

        scale = SB_HEAD_DIM ** -0.5
        q = [q_ref[:, _pair_lanes(p)] for p in range(pairs)]
        d_o = [do_ref[:, _pair_lanes(p)] for p in range(pairs)]
        q_t = [(x.astype(F32).T * scale).astype(BF16) for x in q]
        do_t = [(x.astype(F32).T * scale).astype(BF16) for x in d_o]
        lane_is_a = lax.broadcasted_iota(jnp.int32, (SB_KEY_BLOCK, LANES), 1) < SB_HEAD_DIM

        def tiles(kbs, diags, carry):
            nb = len(kbs)
            lanes = [_sb_lanes(tq, d) for d in diags]
            cols = [slice(first, first + width) for first, width in lanes]
            dq_t, ca, cb = [list(c) for c in carry]
            ks = [pl.multiple_of(kb * SB_KEY_BLOCK, SB_KEY_BLOCK) for kb in kbs]
            slot = lambda p, b: p * nb + b

            def causal(b, s):
                return causal_ref[diags[b], pl.ds((s * SB_STRIP) % SB_KEY_BLOCK, SB_STRIP), cols[b]]

            kk, vv = {}, {}
            for b in range(nb):
                for p in range(pairs):
                    kk[p, b] = _pair_rows(k_ref[pl.ds(ks[b], SB_KEY_BLOCK), _pair_lanes(p)], lane_is_a)
                    vv[p, b] = _pair_rows(v_ref[pl.ds(ks[b], SB_KEY_BLOCK), _pair_lanes(p)], lane_is_a)
                    z_ref[slot(p, b), :, cols[b]] = _dot(kk[p, b], q_t[p][:, cols[b]], _NN)
            for b in range(nb):
                for p in range(pairs):
                    for s in range(n_strips):
                        rows = pl.ds(s * SB_STRIP, SB_STRIP)
                        spent, log_beta = _sb_log_terms(z_ref[slot(p, b), rows, cols[b]])
                        lb_ref[slot(p, b), rows, cols[b]] = log_beta
                        if isinstance(diags[b], int):
                            spent = spent * causal(b, s)
                        _sb_store_split(split_ref.at[slot(p, b)], s, spent, cols[b])
            for b in range(nb):
                for p in range(pairs):
                    z_ref[slot(p, b), :, cols[b]] = _dot(after_ref[...], split_ref[slot(p, b), :, cols[b]], _NN)
                    da_ref[slot(p, b), :, cols[b]] = _dot(vv[p, b], do_t[p][:, cols[b]], _NN)
            sums = {}
            for b in range(nb):
                for p in range(pairs):
                    part = [jnp.zeros((8, lanes[b][1]), F32), jnp.zeros((8, lanes[b][1]), F32)]
                    for s in range(n_strips):
                        rows = pl.ds(s * SB_STRIP, SB_STRIP)
                        start = r_ref[2 * p + (s * SB_STRIP) // SB_KEY_BLOCK, kbs[b]][:, cols[b]]
                        w = jnp.exp(lb_ref[slot(p, b), rows, cols[b]] + z_ref[slot(p, b), rows, cols[b]] + start)
                        if isinstance(diags[b], int):
                            w = w * causal(b, s)
                        w_ref[slot(p, b), rows, cols[b]] = w.astype(BF16)
                        da = da_ref[slot(p, b), rows, cols[b]] * w
                        da_ref[slot(p, b), rows, cols[b]] = da
                        _sb_store_split(split_ref.at[slot(p, b)], s, da, cols[b])
                        head = (s * SB_STRIP) // SB_KEY_BLOCK
                        part[head] = part[head] + jnp.sum(da.reshape(SB_STRIP // 8, 8, lanes[b][1]), axis=0)
                    sums[p, b] = part
            for b in range(nb):
                for p in range(pairs):
                    z_ref[slot(p, b), :, cols[b]] = _dot(before_ref[...], split_ref[slot(p, b), :, cols[b]], _NN)
            for b in range(nb):
                for p in range(pairs):
                    for s in range(n_strips):
                        rows = pl.ds(s * SB_STRIP, SB_STRIP)
                        base = (ca[p] if (s * SB_STRIP) < SB_KEY_BLOCK else cb[p])[:, cols[b]]
                        sig = jnp.exp(lb_ref[slot(p, b), rows, cols[b]])
                        dz = (da_ref[slot(p, b), rows, cols[b]] * (1.0 - sig)
                              - (z_ref[slot(p, b), rows, cols[b]] + base) * sig)
                        if isinstance(diags[b], int):
                            dz = dz * causal(b, s)
                        dz_ref[slot(p, b), rows, cols[b]] = dz.astype(BF16)
                    ca[p] = _lane_add(ca[p], jnp.sum(sums[p, b][0], axis=0, keepdims=True), lanes[b])
                    cb[p] = _lane_add(cb[p], jnp.sum(sums[p, b][1], axis=0, keepdims=True), lanes[b])
            for b in range(nb):
                for p in range(pairs):
                    dq_t[p] = _lane_add(dq_t[p], _dot(kk[p, b], dz_ref[slot(p, b), :, cols[b]], _TN), lanes[b])
                    dkk = _dot(dz_ref[slot(p, b), :, cols[b]], q[p][cols[b], :], _NN)
                    dvv = _dot(w_ref[slot(p, b), :, cols[b]], d_o[p][cols[b], :], _NN)
                    here = (pl.ds(ks[b], SB_KEY_BLOCK), _pair_lanes(p))
                    dk_ref[here] += jnp.where(lane_is_a, dkk[:SB_KEY_BLOCK], dkk[SB_KEY_BLOCK:])
                    dv_ref[here] += jnp.where(lane_is_a, dvv[:SB_KEY_BLOCK], dvv[SB_KEY_BLOCK:])
            return tuple(dq_t), tuple(ca), tuple(cb)

        n_full = i * per_q
        groups_walked = jnp.clip(jnp.max(n_ref[...]).astype(jnp.int32), 0, n_full) // SB_GROUP
        carry = (tuple(jnp.zeros((LANES, tq), F32) for _ in range(pairs)),
                 tuple(jnp.zeros((1, tq), F32) for _ in range(pairs)),
                 tuple(jnp.zeros((1, tq), F32) for _ in range(pairs)))

        def below(j, c):
            kbs = [n_full - (groups_walked - j) * SB_GROUP + b for b in range(SB_GROUP)]
            starts = [r_ref[h, kbs[-1]][:, tq // 2:] for h in range(2 * pairs)]
            reaches = jnp.max(functools.reduce(jnp.maximum, starts)) > SB_DEAD_LOG
            return lax.cond(reaches, lambda cc: tiles(kbs, [None] * SB_GROUP, cc),
                            lambda cc: tiles(kbs, ["left"] * SB_GROUP, cc), c)

        carry = lax.fori_loop(0, groups_walked, below, carry)
        own = list(range(per_q))
        carry = tiles([i * per_q + d for d in own], own, carry)
        for p in range(pairs):
            dq_ref[:, _pair_lanes(p)] = carry[0][p].T.astype(BF16)
        if rider is not None:
            rider.wait_at_last(ids, (groups, nq), ride)

    wide = pairs * LANES
    mat = pl.BlockSpec(after_m.shape, lambda g, i: (0, 0), pipeline_mode=pl.Buffered(1))
    in_specs = [pl.BlockSpec((tq, wide), lambda g, i: (i, g)),
                pl.BlockSpec((T, wide), lambda g, i: (0, groups + g), pipeline_mode=pl.Buffered(1)),
                pl.BlockSpec((T, wide), lambda g, i: (0, 2 * groups + g), pipeline_mode=pl.Buffered(1)),
                pl.BlockSpec((tq, wide), lambda g, i: (i, g)),
                pl.BlockSpec((2 * pairs, nkb, 1, tq), lambda g, i: (g, 0, 0, i)),
                pl.BlockSpec((1, 8, LANES), lambda g, i: (i, 0, 0)),
                mat, mat,
                pl.BlockSpec(causal_m.shape, lambda g, i: (0, 0, 0), pipeline_mode=pl.Buffered(1))]
    out_specs = [pl.BlockSpec((tq, wide), lambda g, i: (i, g)),
                 pl.BlockSpec((T, wide), lambda g, i: (0, g)),
                 pl.BlockSpec((T, wide), lambda g, i: (0, g))]
    out_shape = [jax.ShapeDtypeStruct((T, SB_WIDTH), BF16), jax.ShapeDtypeStruct((T, SB_WIDTH), F32),
                 jax.ShapeDtypeStruct((T, SB_WIDTH), F32)]
    args = (h_a, h_a, h_a, d_out, r_mat, walked_blocks, after_m, before_m, causal_m)
    slots = pairs * max(per_q, SB_GROUP)
    scratch = [pltpu.VMEM((slots, 2 * LANES, tq), F32), pltpu.VMEM((slots, 2 * LANES, tq), F32),
               pltpu.VMEM((slots, 4 * LANES, tq), BF16), pltpu.VMEM((slots, 2 * LANES, tq), BF16),
               pltpu.VMEM((slots, 2 * LANES, tq), F32), pltpu.VMEM((slots, 2 * LANES, tq), BF16)]
    if rider is not None:
        in_specs, args = in_specs + rider.specs, args + tuple(rider.bufs)
        out_specs, out_shape = out_specs + rider.specs, out_shape + rider.out_shape
        scratch = scratch + rider.scratch
    outs = pl.pallas_call(
        body,
        name="sb_bwd",
        grid=(groups, nq),
        in_specs=in_specs,
        out_specs=out_specs,
        out_shape=out_shape,
        scratch_shapes=scratch,
        compiler_params=_params(("arbitrary", "arbitrary") if rider is not None else ("parallel", "arbitrary")),
    )(*args)
    return outs[0], outs[1], outs[2], list(outs[3:])


def _ret_tables(T):
    half = RET_QK_DIM // 2
    inv = 1.0 / (ROPE_BASE ** (jnp.arange(half, dtype=F32) / half))
    ang = jnp.arange(T, dtype=F32)[:, None] * inv[None, :]
    cos, sin = jnp.cos(ang), jnp.sin(ang)
    cos_t = jnp.concatenate([cos, cos], axis=1)
    sin_t = jnp.concatenate([-sin, sin], axis=1)
    log_gamma = jnp.log1p(-jnp.exp2(-5.0 - jnp.arange(RET_HEADS, dtype=F32)))
    idx = jnp.arange(RET_CHUNK, dtype=F32)
    rel = idx[:, None] - idx[None, :]
    decay = jnp.where(rel[None] >= 0, jnp.exp(log_gamma[:, None, None] * jnp.maximum(rel, 0.0)[None]), 0.0)
    k_decay = jnp.exp(log_gamma[None, :] * (RET_CHUNK - 1.0 - idx)[:, None])
    q_decay = jnp.exp(log_gamma[None, :] * (idx + 1.0)[:, None])
    chunk_decay = jnp.exp(log_gamma * RET_CHUNK)
    k_dec = jnp.broadcast_to(k_decay.T[:, :, None], (RET_HEADS, RET_CHUNK, LANES))
    q_dec = jnp.broadcast_to(q_decay.T[:, :, None], (RET_HEADS, RET_CHUNK, LANES))
    c_dec = jnp.broadcast_to(chunk_decay[:, None, None], (RET_HEADS, 8, LANES))
    return cos_t, sin_t, decay, k_dec, q_dec, c_dec


def _rotary(x, cos_t, sin_t):
    return x * cos_t + pltpu.roll(x, RET_QK_DIM // 2, 1) * sin_t


def _rotary_transpose(dy, cos_t, sin_t):
    return dy * cos_t + pltpu.roll(dy * sin_t, RET_QK_DIM // 2, 1)


def _head_norm(o):
    mu = jnp.mean(o, axis=1, keepdims=True)
    cen = o - mu
    var = jnp.mean(cen * cen, axis=1, keepdims=True)
    rstd = lax.rsqrt(var + LN_EPS)
    return cen * rstd, rstd


def _ret_specs(steps, per_step, reverse):
    def n_of(n):
        return (steps - 1 - n) if reverse else n

    rows = per_step * RET_CHUNK
    q_spec = pl.BlockSpec((rows, RET_QK_WIDTH), lambda n: (n_of(n), 0))
    k_spec = pl.BlockSpec((rows, RET_QK_WIDTH), lambda n: (n_of(n), 1))
    vv = pl.BlockSpec((rows, RET_V_WIDTH), lambda n: (n_of(n), 0))
    pos = pl.BlockSpec((rows, LANES), lambda n: (n_of(n), 0))
    per_head = pl.BlockSpec((RET_HEADS, RET_CHUNK, LANES), lambda n: (0, 0, 0))
    c_dec = pl.BlockSpec((RET_HEADS, 8, LANES), lambda n: (0, 0, 0))
    state = pl.BlockSpec((RET_HEADS, per_step, RET_QK_DIM, RET_V_DIM), lambda n: (0, n_of(n), 0, 0))
    return q_spec, k_spec, vv, pos, per_head, c_dec, state


def _qk_cols(h):
    return slice(h * RET_QK_DIM, (h + 1) * RET_QK_DIM)


def _v_cols(h):
    return slice(h * RET_V_DIM, (h + 1) * RET_V_DIM)


def _ret_fwd(h_b, h_c, h_d, tables):
    T = h_b.shape[0]
    nc = T // RET_CHUNK
    per_step = _pick(nc, (RET_STEP_CHUNKS, 1))
    steps = nc // per_step
    q_spec, k_spec, vv, pos, per_head, c_dec, state = _ret_specs(steps, per_step, False)

    def body(q_ref, k_ref, v_ref, g_ref, cos_ref, sin_ref, dec_ref, kd_ref, qd_ref, cd_ref,
             y_ref, o_ref, st_ref, state_ref):
        @pl.when(pl.program_id(0) == 0)
        def _():
            state_ref[...] = jnp.zeros_like(state_ref)

        for c in range(per_step):
            rows = pl.ds(c * RET_CHUNK, RET_CHUNK)
            cos_t, sin_t = cos_ref[rows, :], sin_ref[rows, :]
            for h in range(RET_HEADS):
                q = _rotary(q_ref[rows, _qk_cols(h)], cos_t, sin_t) * (RET_QK_DIM ** -0.5)
                k = _rotary(k_ref[rows, _qk_cols(h)], cos_t, sin_t)
                v = v_ref[rows, _v_cols(h)]
                prev = state_ref[h]
                scores = _dot(q.astype(BF16), k.astype(BF16), _NT) * dec_ref[h]
                inner = _dot(scores.astype(BF16), v, _NN)
                cross = _dot((q * qd_ref[h]).astype(BF16), prev.astype(BF16), _NN)
                o = inner + cross
                st_ref[h, c] = prev
                kv = _dot((k * kd_ref[h]).astype(BF16), v, _TN)
                state_ref[h] = prev * cd_ref[h, 0:1, 0:1] + kv
                o_ref[rows, _v_cols(h)] = o
                normed, _ = _head_norm(o)
                gate = g_ref[rows, _v_cols(h)]
                y_ref[rows, _v_cols(h)] = (gate * jax.nn.sigmoid(gate) * normed).astype(BF16)

    return pl.pallas_call(
        body,
        name="ret_fwd",
        grid=(steps,),
        in_specs=[q_spec, k_spec, vv, vv, pos, pos, per_head, per_head, per_head, c_dec],
        out_specs=[vv, vv, state],
        out_shape=[jax.ShapeDtypeStruct((T, RET_V_WIDTH), BF16),
                   jax.ShapeDtypeStruct((T, RET_V_WIDTH), F32),
                   jax.ShapeDtypeStruct((RET_HEADS, nc, RET_QK_DIM, RET_V_DIM), F32)],
        scratch_shapes=[pltpu.VMEM((RET_HEADS, RET_QK_DIM, RET_V_DIM), F32)],
        compiler_params=_params(("arbitrary",)),
    )(h_b, h_b, h_c, h_d, *tables)


def _ret_bwd(d_y, o_pre, states, h_b, h_c, h_d, tables, rider=None):
    T = h_b.shape[0]
    nc = T // RET_CHUNK
    per_step = _pick(nc, (RET_STEP_CHUNKS, 1))
    steps = nc // per_step
    q_spec, k_spec, vv, pos, per_head, c_dec, state = _ret_specs(steps, per_step, True)
    n_ride = rider.n if rider is not None else 0

    def body(*refs):
        (dy_ref, o_ref, st_ref, q_ref, k_ref, v_ref, g_ref, cos_ref, sin_ref, dec_ref, kd_ref, qd_ref,
         cd_ref) = refs[:13]
        dq_ref, dk_ref, dv_ref, dg_ref = refs[13 + n_ride:17 + n_ride]
        carry_ref = refs[17 + 2 * n_ride]
        ids = [pl.program_id(0)]
        if rider is not None:
            ride = (refs[13:13 + n_ride], refs[17 + n_ride:17 + 2 * n_ride], refs[-3:])
            rider.start_at_first(ids, ride)

        @pl.when(ids[0] == 0)
        def _():
            carry_ref[...] = jnp.zeros_like(carry_ref)

        scale = RET_QK_DIM ** -0.5
        for c in reversed(range(per_step)):
            rows = pl.ds(c * RET_CHUNK, RET_CHUNK)
            cos_t, sin_t = cos_ref[rows, :], sin_ref[rows, :]
            for h in range(RET_HEADS):
                q = _rotary(q_ref[rows, _qk_cols(h)], cos_t, sin_t) * scale
                k = _rotary(k_ref[rows, _qk_cols(h)], cos_t, sin_t)
                v = v_ref[rows, _v_cols(h)]
                decay, k_dec, q_dec = dec_ref[h], kd_ref[h], qd_ref[h]
                chunk_decay = cd_ref[h, 0:1, 0:1]
                state = st_ref[h, c].astype(BF16)
                later = carry_ref[h]
                later_b = later.astype(BF16)

                gate = g_ref[rows, _v_cols(h)]
                sig = jax.nn.sigmoid(gate)
                silu = gate * sig
                normed, rstd = _head_norm(o_ref[rows, _v_cols(h)])
                d_y = dy_ref[rows, _v_cols(h)]
                dg_ref[rows, _v_cols(h)] = (d_y * normed * (sig * (1.0 + gate * (1.0 - sig)))).astype(BF16)
                d_n = d_y * silu
                d_o = rstd * (d_n - jnp.mean(d_n, axis=1, keepdims=True)
                              - normed * jnp.mean(d_n * normed, axis=1, keepdims=True))
                d_ob = d_o.astype(BF16)

                qb, kb = q.astype(BF16), k.astype(BF16)
                qd_b, kd_b = (q * q_dec).astype(BF16), (k * k_dec).astype(BF16)
                scores = _dot(qb, kb, _NT) * decay
                d_scores = (_dot(d_ob, v, _NT) * decay).astype(BF16)
                dq = _dot(d_scores, kb, _NN) + _dot(d_ob, state, _NT) * q_dec
                dk = _dot(d_scores, qb, _TN) + _dot(v, later_b, _NT) * k_dec
                dv = _dot(scores.astype(BF16), d_ob, _TN) + _dot(kd_b, later_b, _NN)
                carry_ref[h] = _dot(qd_b, d_ob, _TN) + chunk_decay * later
                dq_ref[rows, _qk_cols(h)] = _rotary_transpose(dq * scale, cos_t, sin_t).astype(BF16)
                dk_ref[rows, _qk_cols(h)] = _rotary_transpose(dk, cos_t, sin_t).astype(BF16)
                dv_ref[rows, _v_cols(h)] = dv.astype(BF16)
        if rider is not None:
            rider.wait_at_last(ids, (steps,), ride)

    qk_out = pl.BlockSpec((per_step * RET_CHUNK, RET_QK_WIDTH), lambda n: (steps - 1 - n, 0))
    in_specs = [vv, vv, state, q_spec, k_spec, vv, vv, pos, pos, per_head, per_head, per_head, c_dec]
    out_specs = [qk_out, qk_out, vv, vv]
    out_shape = [jax.ShapeDtypeStruct((T, RET_QK_WIDTH), BF16), jax.ShapeDtypeStruct((T, RET_QK_WIDTH), BF16),
                 jax.ShapeDtypeStruct((T, RET_V_WIDTH), BF16), jax.ShapeDtypeStruct((T, RET_V_WIDTH), BF16)]
    args = (d_y, o_pre, states, h_b, h_b, h_c, h_d) + tuple(tables)
    scratch = [pltpu.VMEM((RET_HEADS, RET_QK_DIM, RET_V_DIM), F32)]
    if rider is not None:
        in_specs, args = in_specs + rider.specs, args + tuple(rider.bufs)
        out_specs, out_shape = out_specs + rider.specs, out_shape + rider.out_shape
        scratch = scratch + rider.scratch
    outs = pl.pallas_call(
        body,
        name="ret_bwd",
        grid=(steps,),
        in_specs=in_specs,
        out_specs=out_specs,
        out_shape=out_shape,
        scratch_shapes=scratch,
        compiler_params=_params(("arbitrary",)),
    )(*args)
    return outs[0], outs[1], outs[2], outs[3], list(outs[4:])


def _proj_tiles(h, x):
    return h[:, 0:1536], h[:, 1536:2560], h[:, 2560:3584], h[:, 3584:4608], h[:, 4608:6656], x


def _gate_mix_tiles(y_ret, h_e, b_gate, y_sb):
    gates = jax.nn.sigmoid(h_e + b_gate)
    return y_ret, gates[:, :D_MODEL] * y_sb + gates[:, D_MODEL:] * y_ret


def _gate_mix_grad_tiles(d_mix, h_e, b_gate, y_sb, y_ret):
    gates = jax.nn.sigmoid(h_e + b_gate)
    g0, g1 = gates[:, :D_MODEL], gates[:, D_MODEL:]
    d_e = jnp.concatenate([d_mix * y_sb * g0 * (1.0 - g0), d_mix * y_ret * g1 * (1.0 - g1)], axis=1)
    return d_mix * g0, d_mix * g1, d_e, d_e


def _ln_stats(u):
    mu = jnp.mean(u, axis=1, keepdims=True)
    cen = u - mu
    var = jnp.mean(cen * cen, axis=1, keepdims=True)
    rstd = lax.rsqrt(var + LN_EPS)
    return cen * rstd, rstd


def _ln_input_grad(d_out, gain, xhat, rstd):
    d_hat = d_out * gain
    return rstd * (d_hat - jnp.mean(d_hat, axis=1, keepdims=True)
                   - xhat * jnp.mean(d_hat * xhat, axis=1, keepdims=True))


def _ln_tiles(sub, x_prev, gain, bias):
    xhat, rstd = _ln_stats(DN_ALPHA * x_prev + sub)
    out = xhat * gain + bias
    return out, out, xhat, rstd


def _residual_tiles(d_sub, res):
    return (d_sub + DN_ALPHA * res,)


def _ln_grad_tiles(d_sub, res, xhat, rstd, gain):
    d_out = d_sub + DN_ALPHA * res
    du = _ln_input_grad(d_out, gain, xhat, rstd)
    return du, du, d_out * xhat, d_out


def _ln_loss_tiles(sub, x_prev, gain, bias, target):
    xhat, rstd = _ln_stats(DN_ALPHA * x_prev + sub)
    diff = xhat * gain + bias - target
    d_out = diff * (1.0 / D_MODEL)
    du = _ln_input_grad(d_out, gain, xhat, rstd)
    return du, du, diff * diff, d_out * xhat, d_out


def _mem_probs(q_h, k_h):
    s = _dot(q_h, k_h, _NT) * (MEM_HEAD_DIM ** -0.5)
    e = jnp.exp(s - jnp.max(s, axis=1, keepdims=True))
    return e / jnp.sum(e, axis=1, keepdims=True)


def _xattn_fwd(q, kv):
    T, mem_len = q.shape[0], kv.shape[0]
    tq = _pick(T, (512, 256, 128))

    def body(q_ref, kv_ref, o_ref):
        for h in range(MEM_HEADS):
            cols = slice(h * MEM_HEAD_DIM, (h + 1) * MEM_HEAD_DIM)
            vcols = slice(D_MODEL + h * MEM_HEAD_DIM, D_MODEL + (h + 1) * MEM_HEAD_DIM)
            p = _mem_probs(q_ref[:, cols], kv_ref[:, cols])
            o_ref[:, cols] = _dot(p.astype(BF16), kv_ref[:, vcols], _NN).astype(BF16)

    return pl.pallas_call(
        body,
        name="xattn_fwd",
        grid=(T // tq,),
        in_specs=[pl.BlockSpec((tq, D_MODEL), lambda i: (i, 0)),
                  pl.BlockSpec((mem_len, 2 * D_MODEL), lambda i: (0, 0))],
        out_specs=pl.BlockSpec((tq, D_MODEL), lambda i: (i, 0)),
        out_shape=jax.ShapeDtypeStruct((T, D_MODEL), BF16),
        compiler_params=_params(("parallel",)),
    )(q, kv)


def _xattn_bwd(q, kv, d_o):
    T, mem_len = q.shape[0], kv.shape[0]
    tq = _pick(T, (512, 256, 128))

    def body(q_ref, kv_ref, do_ref, dq_ref, dkv_ref):
        @pl.when(pl.program_id(0) == 0)
        def _():
            dkv_ref[...] = jnp.zeros_like(dkv_ref)

        for h in range(MEM_HEADS):
            cols = slice(h * MEM_HEAD_DIM, (h + 1) * MEM_HEAD_DIM)
            vcols = slice(D_MODEL + h * MEM_HEAD_DIM, D_MODEL + (h + 1) * MEM_HEAD_DIM)
            q_h, k_h, do_h = q_ref[:, cols], kv_ref[:, cols], do_ref[:, cols]
            p = _mem_probs(q_h, k_h)
            dp = _dot(do_h, kv_ref[:, vcols], _NT)
            ds = p * (dp - jnp.sum(dp * p, axis=1, keepdims=True))
            dsb = (ds * (MEM_HEAD_DIM ** -0.5)).astype(BF16)
            dq_ref[:, cols] = _dot(dsb, k_h, _NN).astype(BF16)
            dkv_ref[:, cols] += _dot(dsb, q_h, _TN)
            dkv_ref[:, vcols] += _dot(p.astype(BF16), do_h, _TN)

    row = pl.BlockSpec((tq, D_MODEL), lambda i: (i, 0))
    full = pl.BlockSpec((mem_len, 2 * D_MODEL), lambda i: (0, 0))
    return pl.pallas_call(
        body,
        name="xattn_bwd",
        grid=(T // tq,),
        in_specs=[row, full, row],
        out_specs=[row, full],
        out_shape=[jax.ShapeDtypeStruct((T, D_MODEL), BF16), jax.ShapeDtypeStruct((mem_len, 2 * D_MODEL), F32)],
        compiler_params=_params(("arbitrary",)),
    )(q, kv, d_o)


def _swiglu_tiles(f):
    a, b = f[:, :FFN_HIDDEN], f[:, FFN_HIDDEN:]
    return f, a * jax.nn.sigmoid(a) * b


def _swiglu_grad_tiles(d_hidden, f):
    a, b = f[:, :FFN_HIDDEN], f[:, FFN_HIDDEN:]
    sig = jax.nn.sigmoid(a)
    return (jnp.concatenate([d_hidden * b * (sig * (1.0 + a * (1.0 - sig))), d_hidden * (a * sig)], axis=1),)


def _local_step(x, mem, w_in, small, target, fetch, ship):
    T = x.shape[0]
    tables = _ret_tables(T)
    memb = mem.astype(BF16)

    (h_a, h_b, h_c, h_d, h_e, xb), w_ffn = fetch(
        ("w_ffn_in",),
        lambda rider: _as_host(rider, _mm_fused(
            x, w_in, mode="nn", name="proj_in", extras=[], pass_a=True,
            outs=[(1536, BF16), (1024, F32), (1024, BF16), (1024, F32), (2048, F32), (D_MODEL, BF16)],
            epilogue=_proj_tiles, max_rows=256, rider=rider)))
    (a_sb, r_mat), w_mix = fetch(("w_sb_o", "w_ret_o", "w_mix_o", "w_mem_q", "w_mem_kv", "w_mem_o", "w_ffn_out"),
                                 lambda rider: _sb_fwd(h_a, rider))
    w = {**w_ffn, **w_mix}
    y_gated, o_pre, states = _ret_fwd(h_b, h_c, h_d, tables)
    y_sb = _mm(a_sb, w["w_sb_o"], mode="nn", out_dtype=F32, name="sb_out")
    row_f32, row_bf16 = (D_MODEL, F32), (D_MODEL, BF16)
    ln_outs = [row_f32, row_bf16, row_f32, (1, F32)]
    y_ret, mix_in = _mm_fused(y_gated, w["w_ret_o"], mode="nn", name="ret_out", extras=[h_e, small["b_gate"], y_sb],
                              outs=[row_f32, row_bf16], epilogue=_gate_mix_tiles)
    x1, x1b, xhat1, rstd1 = _mm_fused(mix_in, w["w_mix_o"], mode="nn", name="mix_out",
                                      extras=[x, small["ln1_g"], small["ln1_b"]], outs=ln_outs, epilogue=_ln_tiles)
    q_m = _mm(x1b, w["w_mem_q"], mode="nn", out_dtype=BF16, name="mem_q")
    kv_m = _mm(memb, w["w_mem_kv"], mode="nn", out_dtype=BF16, name="mem_kv")
    o_m = _xattn_fwd(q_m, kv_m)
    x2, x2b, xhat2, rstd2 = _mm_fused(o_m, w["w_mem_o"], mode="nn", name="mem_out",
                                      extras=[x1, small["ln2_g"], small["ln2_b"]], outs=ln_outs, epilogue=_ln_tiles)
    f, hidden = _mm_fused(x2b, w["w_ffn_in"], mode="nn", name="ffn_in", extras=[],
                          outs=[(2 * FFN_HIDDEN, F32), (FFN_HIDDEN, BF16)], epilogue=_swiglu_tiles)
    du_outs, col = [row_f32, row_bf16], D_MODEL
    du3, du3b, loss_cols, d_ln3_g, d_ln3_b = _mm_fused(
        hidden, w["w_ffn_out"], mode="nn", name="ffn_out", extras=[x2, small["ln3_g"], small["ln3_b"], target],
        outs=du_outs, sums=[col, col, col], epilogue=_ln_loss_tiles)

    g_ffn_out = _mm(hidden, du3b, mode="tn", out_dtype=BF16, name="g_ffn_out")
    (d_f,) = _mm_fused(du3b, w["w_ffn_out"], mode="nt", name="d_hidden", extras=[f],
                       outs=[(2 * FFN_HIDDEN, BF16)], epilogue=_swiglu_grad_tiles)
    g_ffn_in = _mm(x2b, d_f, mode="tn", out_dtype=BF16, name="g_ffn_in")
    du2, du2b, d_ln2_g, d_ln2_b = ship(
        {"w_ffn_out": g_ffn_out},
        lambda rider: _as_host(rider, _mm_fused(
            d_f, w["w_ffn_in"], mode="nt", name="d_x2", extras=[du3, xhat2, rstd2, small["ln2_g"]], outs=du_outs,
            sums=[col, col], epilogue=_ln_grad_tiles, rider=rider, max_rows=256)))
    g_mem_o = _mm(o_m, du2b, mode="tn", out_dtype=BF16, name="g_mem_o")
    d_om = _mm(du2b, w["w_mem_o"], mode="nt", out_dtype=BF16, name="d_om")
    d_qm, d_kvm = _xattn_bwd(q_m, kv_m, d_om)
    g_mem_q = _mm(x1b, d_qm, mode="tn", out_dtype=BF16, name="g_mem_q")
    g_mem_kv = _mm(memb, d_kvm.astype(BF16), mode="tn", out_dtype=BF16, name="g_mem_kv")
    du1, du1b, d_ln1_g, d_ln1_b = _mm_fused(
        d_qm, w["w_mem_q"], mode="nt", name="d_x1", extras=[du2, xhat1, rstd1, small["ln1_g"]], outs=du_outs,
        sums=[col, col], epilogue=_ln_grad_tiles)
    g_mix_o = _mm(mix_in, du1b, mode="tn", out_dtype=BF16, name="g_mix_o")
    d_ysb, d_yret, d_e, d_b_gate = _mm_fused(
        du1b, w["w_mix_o"], mode="nt", name="d_mix_in", extras=[h_e, small["b_gate"], y_sb, y_ret],
        outs=[row_bf16, row_bf16, (2 * D_MODEL, BF16)], sums=[2 * D_MODEL], epilogue=_gate_mix_grad_tiles)
    g_sb_o = _mm(a_sb, d_ysb, mode="tn", out_dtype=BF16, name="g_sb_o")
    g_ret_o = _mm(y_gated, d_yret, mode="tn", out_dtype=BF16, name="g_ret_o")
    d_asb = _mm(d_ysb, w["w_sb_o"], mode="nt", out_dtype=BF16, name="d_asb")
    d_ygated = _mm(d_yret, w["w_ret_o"], mode="nt", out_dtype=F32, name="d_ygated")
    small_grads = {"b_gate": d_b_gate, "ln1_g": d_ln1_g, "ln1_b": d_ln1_b, "ln2_g": d_ln2_g, "ln2_b": d_ln2_b,
                   "ln3_g": d_ln3_g, "ln3_b": d_ln3_b, "loss_cols": loss_cols}
    d_rq, d_rk, d_c, d_d = ship({"w_mem_kv": g_mem_kv, "w_mem_q": g_mem_q, "w_mem_o": g_mem_o, "w_mix_o": g_mix_o},
                                lambda rider: _ret_bwd(d_ygated, o_pre, states, h_b, h_c, h_d, tables, rider))
    d_q, d_k, d_v = ship({"w_ffn_in": g_ffn_in, "w_ret_o": g_ret_o, "w_sb_o": g_sb_o, "small": small_grads},
                         lambda rider: _sb_bwd(h_a, d_asb, r_mat, rider))
    d_h = [("sb_q", d_q), ("sb_k", d_k), ("sb_v", d_v), ("ret_q", d_rq), ("ret_k", d_rk), ("ret_v", d_c),
           ("ret_g", d_d), ("gate", d_e)]
    g_in = jnp.concatenate([_mm(xb, piece, mode="tn", out_dtype=BF16, name="g_in_" + tag) for tag, piece in d_h],
                           axis=1)
    (d_x,) = ship({"w_in": g_in},
                  lambda rider: _as_host(rider, _mm_fused(
                      [piece for _, piece in d_h], w_in, mode="nt", name="d_x", extras=[du1], outs=[(D_MODEL, F32)],
                      epilogue=_residual_tiles, rider=rider, max_rows=256)))
    return d_x


def _adamw_math(w, g, m, v):
    m = ADAM_B1 * m + (1.0 - ADAM_B1) * g
    v = ADAM_B2 * v + (1.0 - ADAM_B2) * jnp.square(g)
    m_hat = m / (1.0 - ADAM_B1 ** ADAM_STEP)
    v_hat = v / (1.0 - ADAM_B2 ** ADAM_STEP)
    delta = -ADAM_LR * (m_hat / (jnp.sqrt(v_hat) + ADAM_EPS) + ADAM_WD * w)
    return delta, m, v


def _adamw(parts, w, m, v, name):
    R, C = w.shape
    tr = max(t for t in range(16, min(R, 256) + 1, 16) if R % t == 0) if R >= 16 else R

    def body(p_ref, w_ref, m_ref, v_ref, g_ref, d_ref, nm_ref, nv_ref):
        g = p_ref[0].astype(F32)
        for j in range(1, N_DEV):
            g = g + p_ref[j].astype(F32)
        delta, nm, nv = _adamw_math(w_ref[...], g, m_ref[...], v_ref[...])
        g_ref[...] = g
        d_ref[...] = delta
        nm_ref[...] = nm
        nv_ref[...] = nv

    blk = pl.BlockSpec((tr, C), lambda i: (i, 0))
    out = jax.ShapeDtypeStruct((R, C), F32)
    return pl.pallas_call(
        body,
        name=name,
        grid=(R // tr,),
        in_specs=[pl.BlockSpec((N_DEV, tr, C), lambda i: (0, i, 0)), blk, blk, blk],
        out_specs=[blk] * 4,
        out_shape=[out] * 4,
        compiler_params=_params(("parallel",)),
    )(parts, w, m, v)


_SHARD_AXIS = {"w_in": 1, "w_sb_o": 1, "w_ret_o": 0, "w_mix_o": 0, "w_mem_q": 0, "w_mem_kv": 1, "w_mem_o": 0,
               "w_ffn_in": 1, "w_ffn_out": 0}
_MATRICES = tuple(_SHARD_AXIS)
_SMALL = ("b_gate", "ln1_g", "ln1_b", "ln2_g", "ln2_b", "ln3_g", "ln3_b")
_WEIGHT_ORDER = ("w_in", "b_gate", "w_sb_o", "w_ret_o", "w_mix_o", "ln1_g", "ln1_b", "w_mem_q", "w_mem_kv", "w_mem_o",
                 "ln2_g", "ln2_b", "w_ffn_in", "w_ffn_out", "ln3_g", "ln3_b")


def _assemble(name, gathered):
    if _SHARD_AXIS[name] == 0:
        return gathered.reshape(-1, gathered.shape[2])
    return jnp.transpose(gathered, (1, 0, 2)).reshape(gathered.shape[1], -1)


def _to_slots(name, full):
    if _SHARD_AXIS[name] == 0:
        return full.reshape(N_DEV, full.shape[0] // N_DEV, full.shape[1])
    return jnp.transpose(full.reshape(full.shape[0], N_DEV, full.shape[1] // N_DEV), (1, 0, 2))


SMALL_ROWS = 16


def _pack_small(vals):
    return jnp.concatenate([vals["b_gate"].reshape(2, D_MODEL)] + [vals[n] for n in _SMALL[1:]], axis=0)


def _unpack_small(packed):
    out = {"b_gate": packed[0:2].reshape(1, 2 * D_MODEL)}
    for i, n in enumerate(_SMALL[1:]):
        out[n] = packed[2 + i:3 + i]
    return out


def kernel(x, mem, w_in, b_gate, w_sb_o, w_ret_o, w_mix_o, ln1_g, ln1_b, w_mem_q, w_mem_kv, w_mem_o, ln2_g, ln2_b, w_ffn_in, w_ffn_out, ln3_g, ln3_b, loss_target, m_w_in, m_b_gate, m_w_sb_o, m_w_ret_o, m_w_mix_o, m_ln1_g, m_ln1_b, m_w_mem_q, m_w_mem_kv, m_w_mem_o, m_ln2_g, m_ln2_b, m_w_ffn_in, m_w_ffn_out, m_ln3_g, m_ln3_b, v_w_in, v_b_gate, v_w_sb_o, v_w_ret_o, v_w_mix_o, v_ln1_g, v_ln1_b, v_w_mem_q, v_w_mem_kv, v_w_mem_o, v_ln2_g, v_ln2_b, v_w_ffn_in, v_w_ffn_out, v_ln3_g, v_ln3_b):
    weights = dict(w_in=w_in, b_gate=b_gate, w_sb_o=w_sb_o, w_ret_o=w_ret_o, w_mix_o=w_mix_o, ln1_g=ln1_g, ln1_b=ln1_b,
                   w_mem_q=w_mem_q, w_mem_kv=w_mem_kv, w_mem_o=w_mem_o, ln2_g=ln2_g, ln2_b=ln2_b, w_ffn_in=w_ffn_in,
                   w_ffn_out=w_ffn_out, ln3_g=ln3_g, ln3_b=ln3_b)
    mom1 = dict(w_in=m_w_in, b_gate=m_b_gate, w_sb_o=m_w_sb_o, w_ret_o=m_w_ret_o, w_mix_o=m_w_mix_o, ln1_g=m_ln1_g,
                ln1_b=m_ln1_b, w_mem_q=m_w_mem_q, w_mem_kv=m_w_mem_kv, w_mem_o=m_w_mem_o, ln2_g=m_ln2_g, ln2_b=m_ln2_b,
                w_ffn_in=m_w_ffn_in, w_ffn_out=m_w_ffn_out, ln3_g=m_ln3_g, ln3_b=m_ln3_b)
    mom2 = dict(w_in=v_w_in, b_gate=v_b_gate, w_sb_o=v_w_sb_o, w_ret_o=v_w_ret_o, w_mix_o=v_w_mix_o, ln1_g=v_ln1_g,
                ln1_b=v_ln1_b, w_mem_q=v_w_mem_q, w_mem_kv=v_w_mem_kv, w_mem_o=v_w_mem_o, ln2_g=v_ln2_g, ln2_b=v_ln2_b,
                w_ffn_in=v_w_ffn_in, w_ffn_out=v_w_ffn_out, ln3_g=v_ln3_g, ln3_b=v_ln3_b)

    (gathered_in,) = _exchange([weights["w_in"][0].astype(BF16)], False, "gather_w_in")
    received = {}

    def fetch(names, host):
        res = host(_Rider([weights[n][0].astype(BF16) for n in names], False))
        return res[:-1], {n: _assemble(n, g) for n, g in zip(names, res[-1])}

    def ship(grads, host):
        names = list(grads)
        bufs = []
        for n in names:
            if n == "small":
                part = jnp.concatenate([_pack_small(grads[n]), grads[n]["loss_cols"],
                                        jnp.zeros((SMALL_ROWS - 9, D_MODEL), F32)], axis=0)
                bufs.append(jnp.broadcast_to(part[None], (N_DEV,) + part.shape))
            else:
                bufs.append(_to_slots(n, grads[n]).astype(BF16))
        res = host(_Rider(bufs, True))
        received.update(zip(names, res[-1]))
        return res[:-1]

    small = {n: weights[n] for n in _SMALL}
    d_x = _local_step(x[0], mem[0], _assemble("w_in", gathered_in), small, loss_target[0], fetch, ship)

    new = {}
    for n in _MATRICES:
        new[n] = _adamw(received[n], weights[n][0], mom1[n][0], mom2[n][0], "adamw_" + n)
    packed = _adamw(received["small"][:, :8], _pack_small({n: weights[n] for n in _SMALL}),
                    _pack_small({n: mom1[n] for n in _SMALL}), _pack_small({n: mom2[n] for n in _SMALL}), "adamw_small")
    small_new = [_unpack_small(p) for p in packed]
    loss = jnp.sum(received["small"][:, 8]) * (0.5 / D_MODEL)

    outs = [loss, d_x[None]]
    for slot in range(4):
        for n in _WEIGHT_ORDER:
            outs.append(new[n][slot][None] if n in new else small_new[slot][n])
    return tuple(outs)
```

```python
import functools
import math

import jax
import jax.numpy as jnp
from jax import lax
from jax.experimental import pallas as pl
from jax.experimental.pallas import tpu as pltpu

F32 = jnp.float32
BF16 = jnp.bfloat16

N_DEV = 8
D_MODEL = 1024
SB_HEAD_DIM = 64
SB_WIDTH = 512
RET_HEADS = 4
RET_QK_DIM = 128
RET_V_DIM = 256
RET_QK_WIDTH = 512
RET_V_WIDTH = 1024
RET_CHUNK = 128
RET_STEP_CHUNKS = 4
ROPE_BASE = 10000.0
MEM_HEADS = 4
MEM_HEAD_DIM = 256
FFN_HIDDEN = 2816
DN_ALPHA = 2.0 ** 0.25
LN_EPS = 1e-5
ADAM_LR = 0.001
ADAM_B1 = 0.9
ADAM_B2 = 0.999
ADAM_EPS = 1e-08
ADAM_WD = 0.01
ADAM_STEP = 10

VMEM_LIMIT_BYTES = 52 * 1024 * 1024
LANES = 128
SB_KEY_BLOCK = 128
SB_Q_BLOCK = 256
SB_DEAD_LOG = -105.0

MESH_AXES = ("x", "y", "c")


def _pick(dim, prefs):
    for p in prefs:
        if dim % p == 0:
            return p
    return dim


def _params(sem):
    return pltpu.CompilerParams(dimension_semantics=sem, vmem_limit_bytes=VMEM_LIMIT_BYTES)


def _dot(a, b, dims):
    return lax.dot_general(a, b, (dims, ((), ())), preferred_element_type=F32)


_NN = ((1,), (0,))
_NT = ((1,), (1,))
_TN = ((0,), (0,))


def _my_index():
    return 4 * lax.axis_index("x") + 2 * lax.axis_index("y") + lax.axis_index("c")


def _peer(k):
    x, y, c = lax.axis_index("x"), lax.axis_index("y"), lax.axis_index("c")
    bx, by, bc = (k >> 2) & 1, (k >> 1) & 1, k & 1
    px = (1 - x) if bx else x
    py = (1 - y) if by else y
    pc = (1 - c) if bc else c
    return (px, py, pc), 4 * px + 2 * py + pc


class _Rider:
    def __init__(self, bufs, scatter):
        self.bufs, self.scatter, self.n = list(bufs), scatter, len(bufs)
        self.specs = [pl.BlockSpec(memory_space=pl.ANY)] * self.n
        self.out_shape = [jax.ShapeDtypeStruct(b.shape if scatter else (N_DEV,) + b.shape, b.dtype) for b in self.bufs]
        self.scratch = [pltpu.SemaphoreType.DMA((self.n, N_DEV - 1)), pltpu.SemaphoreType.DMA((self.n, N_DEV - 1)),
                        pltpu.SemaphoreType.DMA((self.n,))]

    def _remote(self, ride, a, k, src_ref, slot, to):
        _, dst, (send_sems, recv_sems, _) = ride
        return pltpu.make_async_remote_copy(src_ref=src_ref, dst_ref=dst[a].at[slot], send_sem=send_sems.at[a, k],
                                            recv_sem=recv_sems.at[a, k], device_id=to,
                                            device_id_type=pl.DeviceIdType.MESH)

    def _local(self, ride, a):
        src, dst, (_, _, local_sems) = ride
        me = _my_index()
        return pltpu.make_async_copy(src[a].at[me] if self.scatter else src[a], dst[a].at[me], local_sems.at[a])

    def _direct(self, ride, a):
        src = ride[0]
        me = _my_index()
        out = []
        for k in range(1, N_DEV):
            peer, peer_idx = _peer(k)
            out.append(self._remote(ride, a, k - 1, src[a].at[peer_idx], me, peer))
        return out

    def _two_level(self, ride, a):
        src, dst = ride[0], ride[1]
        x, y, c = lax.axis_index("x"), lax.axis_index("y"), lax.axis_index("c")
        me, sibling = _my_index(), (x, y, 1 - c)
        chips = [(1 - x, y), (x, 1 - y), (1 - x, 1 - y)]
        first = [self._remote(ride, a, 0, src[a], me, sibling)]
        passed, landing = [], [self._remote(ride, a, 0, src[a], me + 1 - 2 * c, sibling)]
        for j, (px, py) in enumerate(chips):
            first.append(self._remote(ride, a, 1 + j, src[a], me, (px, py, c)))
            theirs = 4 * px + 2 * py + c
            passed.append(self._remote(ride, a, 4 + j, dst[a].at[theirs], theirs, sibling))
            landing.append(self._remote(ride, a, 1 + j, src[a], theirs, (px, py, c)))
        for j, (px, py) in enumerate(chips):
            landing.append(self._remote(ride, a, 4 + j, src[a], 4 * px + 2 * py + 1 - c, sibling))
        return first, passed, landing

    def start(self, ride):
        for a in range(self.n):
            self._local(ride, a).start()
            for cp in (self._direct(ride, a) if self.scatter else self._two_level(ride, a)[0]):
                cp.start()

    def finish(self, ride):
        if self.scatter:
            for a in range(self.n):
                for cp in self._direct(ride, a):
                    cp.wait()
                self._local(ride, a).wait()
            return
        levels = [self._two_level(ride, a) for a in range(self.n)]
        for first, passed, landing in levels:
            for j, cp in enumerate(passed):
                landing[1 + j].wait_recv()
                cp.start()
        for a, (first, passed, landing) in enumerate(levels):
            landing[0].wait_recv()
            for cp in landing[4:]:
                cp.wait_recv()
            for cp in first + passed:
                cp.wait_send()
            self._local(ride, a).wait()

    def start_at_first(self, ids, ride):
        first = functools.reduce(jnp.logical_and, [i == 0 for i in ids])

        @pl.when(first)
        def _():
            self.start(ride)

    def wait_at_last(self, ids, grid, ride):
        last = functools.reduce(jnp.logical_and, [i == g - 1 for i, g in zip(ids, grid)])

        @pl.when(last)
        def _():
            self.finish(ride)


def _exchange(bufs, scatter, name):
    rider = _Rider(bufs, scatter)

    def body(*refs):
        ride = (refs[:rider.n], refs[rider.n:2 * rider.n], refs[2 * rider.n:])
        rider.start(ride)
        rider.finish(ride)

    return pl.pallas_call(
        body,
        name=name,
        in_specs=rider.specs,
        out_specs=rider.specs,
        out_shape=rider.out_shape,
        scratch_shapes=rider.scratch,
    )(*rider.bufs)


MM_RESIDENT_B_BYTES = 14 * 1024 * 1024
MM_A_TILE_BYTES = 4 * 1024 * 1024
MM_OUT_TILE_BYTES = 6 * 1024 * 1024


def _mm_tiles(mode, M, N, K, a_bytes, out_bytes):
    if mode != "tn" and K * N * 2 <= MM_RESIDENT_B_BYTES:
        for tm in (1024, 512, 256, 128):
            if M % tm == 0 and tm * K * a_bytes <= MM_A_TILE_BYTES and tm * N * out_bytes <= MM_OUT_TILE_BYTES:
                return tm, N, K
    if mode == "tn":
        return (_pick(M, (1024, 1408, 512, 256, 128)), _pick(N, (1024, 1664, 1408, 512, 256, 128)),
                _pick(K, (2048, 1024, 512, 256, 128)))
    return _pick(M, (1024, 512, 256, 128)), _pick(N, (512, 256, 128)), _pick(K, (1024, 512, 256, 128))


def _mm(a, b, *, mode, out_dtype, name, res=None, res_scale=1.0, rider=None):
    if mode == "nn":
        (M, K), (K2, N) = a.shape, b.shape
    elif mode == "nt":
        (M, K), (N, K2) = a.shape, b.shape
    else:
        (K, M), (K2, N) = a.shape, b.shape
    assert K == K2, (a.shape, b.shape, mode)
    out_bytes = jnp.dtype(out_dtype).itemsize + (4 if res is not None else 0)
    tm, tn, tk = _mm_tiles(mode, M, N, K, a.dtype.itemsize, out_bytes)
    grid = (M // tm, N // tn, K // tk)
    nk = grid[2]
    dims = {"nn": _NN, "nt": _NT, "tn": _TN}[mode]
    n_in = 2 + (res is not None)
    n_ride = rider.n if rider is not None else 0

    def body(*refs):
        a_ref, b_ref = refs[:2]
        r_ref = refs[2] if res is not None else None
        o_ref = refs[n_in + n_ride]
        rest = refs[n_in + 2 * n_ride + 1:]
        acc_ref = rest[0] if nk > 1 else None
        ids = [pl.program_id(d) for d in range(3)]
        if rider is not None:
            ride = (refs[n_in:n_in + n_ride], refs[n_in + n_ride + 1:n_in + 2 * n_ride + 1], rest[-3:])
            rider.start_at_first(ids, ride)
        part = _dot(a_ref[...].astype(BF16), b_ref[...].astype(BF16), dims)

        def finish(total):
            if r_ref is not None:
                total = total + res_scale * r_ref[...]
            o_ref[...] = total.astype(out_dtype)

        if nk == 1:
            finish(part)
        else:
            k = ids[2]

            @pl.when(k == 0)
            def _():
                acc_ref[...] = part

            @pl.when(k > 0)
            def _():
                acc_ref[...] += part

            @pl.when(k == nk - 1)
            def _():
                finish(acc_ref[...])

        if rider is not None:
            rider.wait_at_last(ids, grid, ride)

    if mode == "nn":
        a_spec = pl.BlockSpec((tm, tk), lambda i, j, k: (i, k))
        b_spec = pl.BlockSpec((tk, tn), lambda i, j, k: (k, j))
    elif mode == "nt":
        a_spec = pl.BlockSpec((tm, tk), lambda i, j, k: (i, k))
        b_spec = pl.BlockSpec((tn, tk), lambda i, j, k: (j, k))
    else:
        a_spec = pl.BlockSpec((tk, tm), lambda i, j, k: (k, i))
        b_spec = pl.BlockSpec((tk, tn), lambda i, j, k: (k, j))
    o_spec = pl.BlockSpec((tm, tn), lambda i, j, k: (i, j))
    in_specs = [a_spec, b_spec] + ([o_spec] if res is not None else [])
    args = (a, b) + ((res,) if res is not None else ())
    out_specs, out_shape = [o_spec], [jax.ShapeDtypeStruct((M, N), out_dtype)]
    scratch = [pltpu.VMEM((tm, tn), F32)] if nk > 1 else []
    sem = ("parallel", "parallel", "arbitrary")
    if rider is not None:
        in_specs, args = in_specs + rider.specs, args + tuple(rider.bufs)
        out_specs, out_shape = out_specs + rider.specs, out_shape + rider.out_shape
        scratch = scratch + rider.scratch
        sem = ("arbitrary",) * 3
    outs = pl.pallas_call(
        body,
        name=name,
        grid=grid,
        in_specs=in_specs,
        out_specs=out_specs,
        out_shape=out_shape,
        scratch_shapes=scratch,
        compiler_params=_params(sem),
    )(*args)
    return outs[0] if rider is None else (outs[0], list(outs[1:]))


def _mm_host(a, b, *, rider, **kw):
    out = _mm(a, b, rider=rider, **kw)
    return out if rider is not None else (out, [])


def _as_host(rider, results):
    return results if rider is not None else tuple(results) + ([],)


MM_FUSED_MARGIN_BYTES = 10 * 1024 * 1024
MM_FUSED_MAX_ROWS = 512


def _col_sum_update(acc_ref, val, first):
    part = jnp.sum(val.reshape(val.shape[0] // 8, 8, val.shape[1]), axis=0)

    @pl.when(first)
    def _():
        acc_ref[...] = part

    @pl.when(jnp.logical_not(first))
    def _():
        acc_ref[...] += part


def _mm_fused(a, b, *, mode, name, extras, outs, epilogue, sums=(), rider=None, max_rows=MM_FUSED_MAX_ROWS,
              pass_a=False):
    parts = list(a) if isinstance(a, (list, tuple)) else [a]
    M, K = parts[0].shape[0], sum(p.shape[1] for p in parts)
    if mode == "nn":
        (K2, N), b_dims = b.shape, _NN
    else:
        (N, K2), b_dims = b.shape, _NT
    assert K == K2, (K, b.shape, mode)
    rows = parts + [e for e in extras if e.shape[0] == M]
    per_row = 2 * (sum(e.shape[1] * e.dtype.itemsize for e in rows)
                   + sum(c * jnp.dtype(d).itemsize for c, d in outs)) + 2 * N * 4
    budget = VMEM_LIMIT_BYTES - K * N * 2 - MM_FUSED_MARGIN_BYTES
    tm = next(t for t in (512, 256, 128, 64, 32, 16) if t <= max_rows and M % t == 0 and t * per_row <= budget)
    steps = M // tm
    n_a, n_x, n_o, n_s = len(parts), len(extras), len(outs), len(sums)
    n_ride = rider.n if rider is not None else 0

    def body(*refs):
        a_refs, b_ref = refs[:n_a], refs[n_a]
        x_refs = refs[n_a + 1:n_a + 1 + n_x]
        base = n_a + 1 + n_x + n_ride
        o_refs, s_refs = refs[base:base + n_o], refs[base + n_o:base + n_o + n_s]
        acc_refs = refs[base + n_o + n_s + n_ride:base + n_o + 2 * n_s + n_ride]
        ids = [pl.program_id(0)]
        if rider is not None:
            ride = (refs[n_a + 1 + n_x:base], refs[base + n_o + n_s:base + n_o + n_s + n_ride], refs[-3:])
            rider.start_at_first(ids, ride)
        a_tile = a_refs[0][...]
        a_bf16 = a_tile.astype(BF16) if n_a == 1 else jnp.concatenate([r[...].astype(BF16) for r in a_refs], axis=1)
        prod = _dot(a_bf16, b_ref[...], b_dims)
        tiles = epilogue(prod, *([a_tile] if pass_a else []), *[r[...] for r in x_refs])
        for o_ref, t in zip(o_refs, tiles[:n_o]):
            o_ref[...] = t.astype(o_ref.dtype)
        for acc_ref, t in zip(acc_refs, tiles[n_o:]):
            _col_sum_update(acc_ref, t, ids[0] == 0)
        if n_s:
            @pl.when(ids[0] == steps - 1)
            def _():
                for s_ref, acc_ref in zip(s_refs, acc_refs):
                    s_ref[...] = jnp.sum(acc_ref[...], axis=0, keepdims=True)
        if rider is not None:
            rider.wait_at_last(ids, (steps,), ride)

    in_specs = [pl.BlockSpec((tm, p.shape[1]), lambda i: (i, 0)) for p in parts]
    in_specs.append(pl.BlockSpec(b.shape, lambda i: (0, 0), pipeline_mode=pl.Buffered(1)))
    for e in extras:
        in_specs.append(pl.BlockSpec((tm, e.shape[1]), lambda i: (i, 0)) if e.shape[0] == M
                        else pl.BlockSpec(e.shape, lambda i: (0, 0)))
    out_specs = ([pl.BlockSpec((tm, c), lambda i: (i, 0)) for c, _ in outs]
                 + [pl.BlockSpec((1, c), lambda i: (0, 0)) for c in sums])
    out_shape = ([jax.ShapeDtypeStruct((M, c), d) for c, d in outs]
                 + [jax.ShapeDtypeStruct((1, c), F32) for c in sums])
    args = tuple(parts) + (b,) + tuple(extras)
    scratch = [pltpu.VMEM((8, c), F32) for c in sums]
    if rider is not None:
        in_specs, args = in_specs + rider.specs, args + tuple(rider.bufs)
        out_specs, out_shape = out_specs + rider.specs, out_shape + rider.out_shape
        scratch = scratch + rider.scratch
    res = pl.pallas_call(
        body,
        name=name,
        grid=(steps,),
        in_specs=in_specs,
        out_specs=out_specs,
        out_shape=out_shape,
        scratch_shapes=scratch,
        compiler_params=_params(("arbitrary",) if (n_s or rider is not None) else ("parallel",)),
    )(*args)
    return tuple(res[:n_o + n_s]) + ((list(res[n_o + n_s:]),) if rider is not None else ())


def _pair_rows(blk, lane_is_a):
    zero = jnp.zeros_like(blk)
    return jnp.concatenate([jnp.where(lane_is_a, blk, zero), jnp.where(lane_is_a, zero, blk)], axis=0)


SB_STRIP = 32
SB_FWD_PAIRS = 4
SB_BWD_PAIRS = 2
SB_GROUP = 2


def _pair_lanes(p):
    return slice(p * LANES, (p + 1) * LANES)


def _sb_scan_matrices():
    o = lax.broadcasted_iota(jnp.int32, (2 * LANES, 4 * LANES), 0)
    c = lax.broadcasted_iota(jnp.int32, (2 * LANES, 4 * LANES), 1) & (2 * LANES - 1)
    same = (o >= LANES) == (c >= LANES)
    oo, cc = o & (LANES - 1), c & (LANES - 1)
    return (jnp.where(same & (cc > oo), -1.0, 0.0).astype(BF16), jnp.where(same & (cc < oo), 1.0, 0.0).astype(BF16))


def _sb_causal_masks(tq):
    d = lax.broadcasted_iota(jnp.int32, (tq // SB_KEY_BLOCK, SB_KEY_BLOCK, tq), 0)
    k = lax.broadcasted_iota(jnp.int32, (tq // SB_KEY_BLOCK, SB_KEY_BLOCK, tq), 1)
    t = lax.broadcasted_iota(jnp.int32, (tq // SB_KEY_BLOCK, SB_KEY_BLOCK, tq), 2)
    return jnp.where(d * SB_KEY_BLOCK + k < t, 1.0, 0.0).astype(F32)


def _sb_log_terms(z):
    minus_abs = lax.bitcast_convert_type(lax.bitcast_convert_type(z, jnp.uint32) | jnp.uint32(0x80000000), F32)
    spent = jnp.maximum(z, 0.0) + jnp.log(1.0 + jnp.exp(minus_abs))
    return spent, z - spent


def _sb_store_split(ref, strip, val, cols):
    hi = val.astype(BF16)
    ref[pl.ds(strip * SB_STRIP, SB_STRIP), cols] = hi
    ref[pl.ds(2 * LANES + strip * SB_STRIP, SB_STRIP), cols] = (val - hi.astype(F32)).astype(BF16)


def _sb_lanes(tq, diag):
    if diag == "left":
        return 0, tq // 2
    first = 0 if diag is None else diag * SB_KEY_BLOCK
    return first, tq - first


def _lane_add(full, part, lanes):
    first, width = lanes
    pieces = [full[:, :first]] if first else []
    pieces.append(full[:, first:first + width] + part)
    if first + width < full.shape[1]:
        pieces.append(full[:, first + width:])
    return pieces[0] if len(pieces) == 1 else jnp.concatenate(pieces, axis=1)


def _sb_fwd(h_a, rider=None):
    assert SB_FWD_PAIRS == 4
    T = h_a.shape[0]
    tq = _pick(T, (SB_Q_BLOCK, SB_KEY_BLOCK))
    nq, per_q, nkb = T // tq, tq // SB_KEY_BLOCK, T // SB_KEY_BLOCK
    assert per_q % SB_GROUP == 0
    n_strips = 2 * LANES // SB_STRIP
    n_ride = rider.n if rider is not None else 0
    after_m, _ = _sb_scan_matrices()
    causal_m = _sb_causal_masks(tq)
    pairs = SB_FWD_PAIRS

    def body(*refs):
        q_ref, k_ref, v_ref, after_ref, causal_ref = refs[:5]
        a_ref, r_ref, n_ref = refs[5 + n_ride:8 + n_ride]
        z_ref, lb_ref, split_ref, w_ref = refs[8 + 2 * n_ride:12 + 2 * n_ride]
        ids = [pl.program_id(0)]
        if rider is not None:
            ride = (refs[5:5 + n_ride], refs[8 + n_ride:8 + 2 * n_ride], refs[-3:])
            rider.start_at_first(ids, ride)
        i = ids[0]
        q_t = [(q_ref[:, _pair_lanes(p)].astype(F32).T * (SB_HEAD_DIM ** -0.5)).astype(BF16) for p in range(pairs)]
        lane_is_a = lax.broadcasted_iota(jnp.int32, (SB_KEY_BLOCK, LANES), 1) < SB_HEAD_DIM

        def tiles(kbs, diags, carry):
            nb = len(kbs)
            lanes = [_sb_lanes(tq, d) for d in diags]
            cols = [slice(first, first + width) for first, width in lanes]
            acc_t, ra, rb = [list(c) for c in carry]
            ks = [pl.multiple_of(kb * SB_KEY_BLOCK, SB_KEY_BLOCK) for kb in kbs]
            slot = lambda p, b: p * nb + b

            def causal(b, s):
                return causal_ref[diags[b], pl.ds((s * SB_STRIP) % SB_KEY_BLOCK, SB_STRIP), cols[b]]

            vv = {}
            for b in range(nb):
                for p in range(pairs):
                    kk = _pair_rows(k_ref[pl.ds(ks[b], SB_KEY_BLOCK), _pair_lanes(p)], lane_is_a)
                    vv[p, b] = _pair_rows(v_ref[pl.ds(ks[b], SB_KEY_BLOCK), _pair_lanes(p)], lane_is_a)
                    z_ref[slot(p, b), :, cols[b]] = _dot(kk, q_t[p][:, cols[b]], _NN)
            sums = {}
            for b in range(nb):
                for p in range(pairs):
                    part = [jnp.zeros((8, lanes[b][1]), F32), jnp.zeros((8, lanes[b][1]), F32)]
                    for s in range(n_strips):
                        rows = pl.ds(s * SB_STRIP, SB_STRIP)
                        spent, log_beta = _sb_log_terms(z_ref[slot(p, b), rows, cols[b]])
                        lb_ref[slot(p, b), rows, cols[b]] = log_beta
                        if isinstance(diags[b], int):
                            spent = spent * causal(b, s)
                        _sb_store_split(split_ref.at[slot(p, b)], s, spent, cols[b])
                        head = (s * SB_STRIP) // SB_KEY_BLOCK
                        part[head] = part[head] + jnp.sum(spent.reshape(SB_STRIP // 8, 8, lanes[b][1]), axis=0)
                    sums[p, b] = part
            for b in range(nb):
                for p in range(pairs):
                    z_ref[slot(p, b), :, cols[b]] = _dot(after_ref[...], split_ref[slot(p, b), :, cols[b]], _NN)
            for b in range(nb):
                for p in range(pairs):
                    for s in range(n_strips):
                        rows = pl.ds(s * SB_STRIP, SB_STRIP)
                        start = (ra[p] if (s * SB_STRIP) < SB_KEY_BLOCK else rb[p])[:, cols[b]]
                        w = jnp.exp(lb_ref[slot(p, b), rows, cols[b]] + z_ref[slot(p, b), rows, cols[b]] + start)
                        if isinstance(diags[b], int):
                            w = w * causal(b, s)
                        w_ref[slot(p, b), rows, cols[b]] = w.astype(BF16)
                    r_ref[2 * p, kbs[b]] = ra[p]
                    r_ref[2 * p + 1, kbs[b]] = rb[p]
                    ra[p] = _lane_add(ra[p], -jnp.sum(sums[p, b][0], axis=0, keepdims=True), lanes[b])
                    rb[p] = _lane_add(rb[p], -jnp.sum(sums[p, b][1], axis=0, keepdims=True), lanes[b])
            for b in range(nb):
                for p in range(pairs):
                    acc_t[p] = _lane_add(acc_t[p], _dot(vv[p, b], w_ref[slot(p, b), :, cols[b]], _TN), lanes[b])
            return tuple(acc_t), tuple(ra), tuple(rb)

        carry = (tuple(jnp.zeros((LANES, tq), F32) for _ in range(pairs)),
                 tuple(jnp.zeros((1, tq), F32) for _ in range(pairs)),
                 tuple(jnp.zeros((1, tq), F32) for _ in range(pairs)))
        own = list(reversed(range(per_q)))
        n_full = i * per_q
        carry = lax.cond(
            i > 0,
            lambda cc: tiles([n_full + d for d in own] + [n_full - 1 - b for b in range(SB_GROUP)],
                             own + [None] * SB_GROUP, cc),
            lambda cc: tiles([n_full + d for d in own], own, cc), carry)
        first_walked = jnp.where(i > 0, SB_GROUP, 0).astype(jnp.int32)

        def top_of(sums_a, sums_b, first):
            return jnp.max(functools.reduce(jnp.maximum, [r[:, first:] for r in sums_a + sums_b]))

        def alive(c):
            return jnp.logical_and(c[0] < n_full, top_of(c[2], c[3], 0) > SB_DEAD_LOG)

        def step(c):
            kbs = [n_full - 1 - c[0] - b for b in range(SB_GROUP)]
            return (c[0] + SB_GROUP,) + lax.cond(
                top_of(c[2], c[3], tq // 2) > SB_DEAD_LOG,
                lambda cc: tiles(kbs, [None] * SB_GROUP, cc), lambda cc: tiles(kbs, ["left"] * SB_GROUP, cc), c[1:])

        walked, acc_t, _, _ = lax.while_loop(alive, step, (first_walked,) + carry)
        for p in range(pairs):
            a_ref[:, _pair_lanes(p)] = acc_t[p].T.astype(BF16)
        n_ref[...] = jnp.zeros(n_ref.shape, F32) + walked.astype(F32)
        if rider is not None:
            rider.wait_at_last(ids, (nq,), ride)

    wide = pairs * LANES
    in_specs = [pl.BlockSpec((tq, wide), lambda i: (i, 0)),
                pl.BlockSpec((T, wide), lambda i: (0, 1), pipeline_mode=pl.Buffered(1)),
                pl.BlockSpec((T, wide), lambda i: (0, 2), pipeline_mode=pl.Buffered(1)),
                pl.BlockSpec(after_m.shape, lambda i: (0, 0), pipeline_mode=pl.Buffered(1)),
                pl.BlockSpec(causal_m.shape, lambda i: (0, 0, 0), pipeline_mode=pl.Buffered(1))]
    out_specs = [pl.BlockSpec((tq, wide), lambda i: (i, 0)),
                 pl.BlockSpec((2 * pairs, nkb, 1, tq), lambda i: (0, 0, 0, i)),
                 pl.BlockSpec((1, 8, LANES), lambda i: (i, 0, 0))]
    out_shape = [jax.ShapeDtypeStruct((T, SB_WIDTH), BF16), jax.ShapeDtypeStruct((2 * pairs, nkb, 1, T), F32),
                 jax.ShapeDtypeStruct((nq, 8, LANES), F32)]
    args = (h_a, h_a, h_a, after_m, causal_m)
    slots = pairs * (per_q + SB_GROUP)
    scratch = [pltpu.VMEM((slots, 2 * LANES, tq), F32), pltpu.VMEM((slots, 2 * LANES, tq), F32),
               pltpu.VMEM((slots, 4 * LANES, tq), BF16), pltpu.VMEM((slots, 2 * LANES, tq), BF16)]
    if rider is not None:
        in_specs, args = in_specs + rider.specs, args + tuple(rider.bufs)
        out_specs, out_shape = out_specs + rider.specs, out_shape + rider.out_shape
        scratch = scratch + rider.scratch
    outs = pl.pallas_call(
        body,
        name="sb_fwd",
        grid=(nq,),
        in_specs=in_specs,
        out_specs=out_specs,
        out_shape=out_shape,
        scratch_shapes=scratch,
        compiler_params=_params(("arbitrary",)),
    )(*args)
    return outs[0], (outs[1], outs[2]), list(outs[3:])


def _sb_bwd(h_a, d_out, saved, rider=None):
    r_mat, walked_blocks = saved
    T = h_a.shape[0]
    tq = _pick(T, (SB_Q_BLOCK, SB_KEY_BLOCK))
    nq, per_q, nkb = T // tq, tq // SB_KEY_BLOCK, T // SB_KEY_BLOCK
    n_strips = 2 * LANES // SB_STRIP
    after_m, before_m = _sb_scan_matrices()
    causal_m = _sb_causal_masks(tq)
    pairs = SB_BWD_PAIRS
    groups = 4 // pairs
    n_ride = rider.n if rider is not None else 0

    def body(*refs):
        q_ref, k_ref, v_ref, do_ref, r_ref, n_ref, after_ref, before_ref, causal_ref = refs[:9]
        dq_ref, dk_ref, dv_ref = refs[9 + n_ride:12 + n_ride]
        z_ref, lb_ref, split_ref, w_ref, da_ref, dz_ref = refs[12 + 2 * n_ride:18 + 2 * n_ride]
        ids = [pl.program_id(0), pl.program_id(1)]
        if rider is not None:
            ride = (refs[9:9 + n_ride], refs[12 + n_ride:12 + 2 * n_ride], refs[-3:])
            rider.start_at_first(ids, ride)
        i = ids[1]

        @pl.when(i == 0)
        def _():
            dk_ref[...] = jnp.zeros_like(dk_ref)
            dv_ref[...] = jnp.zeros_like(dv_ref)

        scale = SB_HEAD_DIM ** -0.5
        q = [q_ref[:, _pair_lanes(p)] for p in range(pairs)]
        d_o = [do_ref[:, _pair_lanes(p)] for p in range(pairs)]
        q_t = [(x.astype(F32).T * scale).astype(BF16) for x in q]
        do_t = [x.astype(F32).T.astype(BF16) for x in d_o]
        lane_is_a = lax.broadcasted_iota(jnp.int32, (SB_KEY_BLOCK, LANES), 1) < SB_HEAD_DIM

        def tiles(kbs, diags, carry):
            nb = len(kbs)
            lanes = [_sb_lanes(tq, d) for d in diags]
            cols = [slice(first, first + width) for first, width in lanes]
            dq_t, ca, cb = [list(c) for c in carry]
            ks = [pl.multiple_of(kb * SB_KEY_BLOCK, SB_KEY_BLOCK) for kb in kbs]
            slot = lambda p, b: p * nb + b

            def causal(b, s):
                return causal_ref[diags[b], pl.ds((s * SB_STRIP) % SB_KEY_BLOCK, SB_STRIP), cols[b]]

            kk, vv = {}, {}
            for b in range(nb):
                for p in range(pairs):
                    kk[p, b] = _pair_rows(k_ref[pl.ds(ks[b], SB_KEY_BLOCK), _pair_lanes(p)], lane_is_a)
                    vv[p, b] = _pair_rows(v_ref[pl.ds(ks[b], SB_KEY_BLOCK), _pair_lanes(p)], lane_is_a)
                    z_ref[slot(p, b), :, cols[b]] = _dot(kk[p, b], q_t[p][:, cols[b]], _NN)
            for b in range(nb):
                for p in range(pairs):
                    for s in range(n_strips):
                        rows = pl.ds(s * SB_STRIP, SB_STRIP)
                        spent, log_beta = _sb_log_terms(z_ref[slot(p, b), rows, cols[b]])
                        lb_ref[slot(p, b), rows, cols[b]] = log_beta
                        if isinstance(diags[b], int):
                            spent = spent * causal(b, s)
                        _sb_store_split(split_ref.at[slot(p, b)], s, spent, cols[b])
            for b in range(nb):
                for p in range(pairs):
                    z_ref[slot(p, b), :, cols[b]] = _dot(after_ref[...], split_ref[slot(p, b), :, cols[b]], _NN)
                    da_ref[slot(p, b), :, cols[b]] = _dot(vv[p, b], do_t[p][:, cols[b]], _NN)
            sums = {}
            for b in range(nb):
                for p in range(pairs):
                    part = [jnp.zeros((8, lanes[b][1]), F32), jnp.zeros((8, lanes[b][1]), F32)]
                    for s in range(n_strips):
                        rows = pl.ds(s * SB_STRIP, SB_STRIP)
                        start = r_ref[2 * p + (s * SB_STRIP) // SB_KEY_BLOCK, kbs[b]][:, cols[b]]
                        w = jnp.exp(lb_ref[slot(p, b), rows, cols[b]] + z_ref[slot(p, b), rows, cols[b]] + start)
                        if isinstance(diags[b], int):
                            w = w * causal(b, s)
                        w_ref[slot(p, b), rows, cols[b]] = w.astype(BF16)
                        da = da_ref[slot(p, b), rows, cols[b]] * w
                        da_ref[slot(p, b), rows, cols[b]] = da
                        _sb_store_split(split_ref.at[slot(p, b)], s, da, cols[b])
                        head = (s * SB_STRIP) // SB_KEY_BLOCK
                        part[head] = part[head] + jnp.sum(da.reshape(SB_STRIP // 8, 8, lanes[b][1]), axis=0)
                    sums[p, b] = part
            for b in range(nb):
                for p in range(pairs):
                    z_ref[slot(p, b), :, cols[b]] = _dot(before_ref[...], split_ref[slot(p, b), :, cols[b]], _NN)
            for b in range(nb):
                for p in range(pairs):
                    for s in range(n_strips):
                        rows = pl.ds(s * SB_STRIP, SB_STRIP)
                        base = (ca[p] if (s * SB_STRIP) < SB_KEY_BLOCK else cb[p])[:, cols[b]]
                        sig = jnp.exp(lb_ref[slot(p, b), rows, cols[b]])
                        dz = (da_ref[slot(p, b), rows, cols[b]] * (1.0 - sig)
                              - (z_ref[slot(p, b), rows, cols[b]] + base) * sig)
                        if isinstance(diags[b], int):
                            dz = dz * causal(b, s)
                        dz_ref[slot(p, b), rows, cols[b]] = (dz * scale).astype(BF16)
                    ca[p] = _lane_add(ca[p], jnp.sum(sums[p, b][0], axis=0, keepdims=True), lanes[b])
                    cb[p] = _lane_add(cb[p], jnp.sum(sums[p, b][1], axis=0, keepdims=True), lanes[b])
            for b in range(nb):
                for p in range(pairs):
                    dq_t[p] = _lane_add(dq_t[p], _dot(kk[p, b], dz_ref[slot(p, b), :, cols[b]], _TN), lanes[b])
                    dkk = _dot(dz_ref[slot(p, b), :, cols[b]], q[p][cols[b], :], _NN)
                    dvv = _dot(w_ref[slot(p, b), :, cols[b]], d_o[p][cols[b], :], _NN)
                    here = (pl.ds(ks[b], SB_KEY_BLOCK), _pair_lanes(p))
                    dk_ref[here] += jnp.where(lane_is_a, dkk[:SB_KEY_BLOCK], dkk[SB_KEY_BLOCK:])
                    dv_ref[here] += jnp.where(lane_is_a, dvv[:SB_KEY_BLOCK], dvv[SB_KEY_BLOCK:])
            return tuple(dq_t), tuple(ca), tuple(cb)

        n_full = i * per_q
        groups_walked = jnp.clip(jnp.max(n_ref[...]).astype(jnp.int32), 0, n_full) // SB_GROUP
        carry = (tuple(jnp.zeros((LANES, tq), F32) for _ in range(pairs)),
                 tuple(jnp.zeros((1, tq), F32) for _ in range(pairs)),
                 tuple(jnp.zeros((1, tq), F32) for _ in range(pairs)))

        def below(j, c):
            kbs = [n_full - (groups_walked - j) * SB_GROUP + b for b in range(SB_GROUP)]
            starts = [r_ref[h, kbs[-1]][:, tq // 2:] for h in range(2 * pairs)]
            reaches = jnp.max(functools.reduce(jnp.maximum, starts)) > SB_DEAD_LOG
            return lax.cond(reaches, lambda cc: tiles(kbs, [None] * SB_GROUP, cc),
                            lambda cc: tiles(kbs, ["left"] * SB_GROUP, cc), c)

        carry = lax.fori_loop(0, groups_walked, below, carry)
        own = list(range(per_q))
        carry = tiles([i * per_q + d for d in own], own, carry)
        for p in range(pairs):
            dq_ref[:, _pair_lanes(p)] = carry[0][p].T.astype(BF16)
        if rider is not None:
            rider.wait_at_last(ids, (groups, nq), ride)

    wide = pairs * LANES
    mat = pl.BlockSpec(after_m.shape, lambda g, i: (0, 0), pipeline_mode=pl.Buffered(1))
    in_specs = [pl.BlockSpec((tq, wide), lambda g, i: (i, g)),
                pl.BlockSpec((T, wide), lambda g, i: (0, groups + g), pipeline_mode=pl.Buffered(1)),
                pl.BlockSpec((T, wide), lambda g, i: (0, 2 * groups + g), pipeline_mode=pl.Buffered(1)),
                pl.BlockSpec((tq, wide), lambda g, i: (i, g)),
                pl.BlockSpec((2 * pairs, nkb, 1, tq), lambda g, i: (g, 0, 0, i)),
                pl.BlockSpec((1, 8, LANES), lambda g, i: (i, 0, 0)),
                mat, mat,
                pl.BlockSpec(causal_m.shape, lambda g, i: (0, 0, 0), pipeline_mode=pl.Buffered(1))]
    out_specs = [pl.BlockSpec((tq, wide), lambda g, i: (i, g)),
                 pl.BlockSpec((T, wide), lambda g, i: (0, g)),
                 pl.BlockSpec((T, wide), lambda g, i: (0, g))]
    out_shape = [jax.ShapeDtypeStruct((T, SB_WIDTH), BF16), jax.ShapeDtypeStruct((T, SB_WIDTH), F32),
                 jax.ShapeDtypeStruct((T, SB_WIDTH), F32)]
    args = (h_a, h_a, h_a, d_out, r_mat, walked_blocks, after_m, before_m, causal_m)
    slots = pairs * max(per_q, SB_GROUP)
    scratch = [pltpu.VMEM((slots, 2 * LANES, tq), F32), pltpu.VMEM((slots, 2 * LANES, tq), F32),
               pltpu.VMEM((slots, 4 * LANES, tq), BF16), pltpu.VMEM((slots, 2 * LANES, tq), BF16),
               pltpu.VMEM((slots, 2 * LANES, tq), F32), pltpu.VMEM((slots, 2 * LANES, tq), BF16)]
    if rider is not None:
        in_specs, args = in_specs + rider.specs, args + tuple(rider.bufs)
        out_specs, out_shape = out_specs + rider.specs, out_shape + rider.out_shape
        scratch = scratch + rider.scratch
    outs = pl.pallas_call(
        body,
        name="sb_bwd",
        grid=(groups, nq),
        in_specs=in_specs,
        out_specs=out_specs,
        out_shape=out_shape,
        scratch_shapes=scratch,
        compiler_params=_params(("arbitrary", "arbitrary") if rider is not None else ("parallel", "arbitrary")),
    )(*args)
    return outs[0], outs[1], outs[2], list(outs[3:])


def _ret_tables(T):
    half = RET_QK_DIM // 2
    inv = 1.0 / (ROPE_BASE ** (jnp.arange(half, dtype=F32) / half))
    ang = jnp.arange(T, dtype=F32)[:, None] * inv[None, :]
    cos, sin = jnp.cos(ang), jnp.sin(ang)
    cos_t = jnp.concatenate([cos, cos], axis=1)
    sin_t = jnp.concatenate([-sin, sin], axis=1)
    log_gamma = jnp.log1p(-jnp.exp2(-5.0 - jnp.arange(RET_HEADS, dtype=F32)))
    idx = jnp.arange(RET_CHUNK, dtype=F32)
    rel = idx[:, None] - idx[None, :]
    decay = jnp.where(rel[None] >= 0, jnp.exp(log_gamma[:, None, None] * jnp.maximum(rel, 0.0)[None]), 0.0)
    k_decay = jnp.exp(log_gamma[None, :] * (RET_CHUNK - 1.0 - idx)[:, None])
    q_decay = jnp.exp(log_gamma[None, :] * (idx + 1.0)[:, None])
    chunk_decay = jnp.exp(log_gamma * RET_CHUNK)
    k_dec = jnp.broadcast_to(k_decay.T[:, :, None], (RET_HEADS, RET_CHUNK, LANES))
    q_dec = jnp.broadcast_to(q_decay.T[:, :, None], (RET_HEADS, RET_CHUNK, LANES))
    c_dec = jnp.broadcast_to(chunk_decay[:, None, None], (RET_HEADS, 8, LANES))
    return cos_t, sin_t, decay, k_dec, q_dec, c_dec


def _rotary(x, cos_t, sin_t):
    return x * cos_t + pltpu.roll(x, RET_QK_DIM // 2, 1) * sin_t


def _rotary_transpose(dy, cos_t, sin_t):
    return dy * cos_t + pltpu.roll(dy * sin_t, RET_QK_DIM // 2, 1)


def _head_norm(o):
    mu = jnp.mean(o, axis=1, keepdims=True)
    cen = o - mu
    var = jnp.mean(cen * cen, axis=1, keepdims=True)
    rstd = lax.rsqrt(var + LN_EPS)
    return cen * rstd, rstd


def _ret_specs(steps, per_step, reverse):
    def n_of(n):
        return (steps - 1 - n) if reverse else n

    rows = per_step * RET_CHUNK
    q_spec = pl.BlockSpec((rows, RET_QK_WIDTH), lambda n: (n_of(n), 0))
    k_spec = pl.BlockSpec((rows, RET_QK_WIDTH), lambda n: (n_of(n), 1))
    vv = pl.BlockSpec((rows, RET_V_WIDTH), lambda n: (n_of(n), 0))
    pos = pl.BlockSpec((rows, LANES), lambda n: (n_of(n), 0))
    per_head = pl.BlockSpec((RET_HEADS, RET_CHUNK, LANES), lambda n: (0, 0, 0))
    c_dec = pl.BlockSpec((RET_HEADS, 8, LANES), lambda n: (0, 0, 0))
    state = pl.BlockSpec((RET_HEADS, per_step, RET_QK_DIM, RET_V_DIM), lambda n: (0, n_of(n), 0, 0))
    return q_spec, k_spec, vv, pos, per_head, c_dec, state


def _qk_cols(h):
    return slice(h * RET_QK_DIM, (h + 1) * RET_QK_DIM)


def _v_cols(h):
    return slice(h * RET_V_DIM, (h + 1) * RET_V_DIM)


def _ret_fwd(h_b, h_c, h_d, tables):
    T = h_b.shape[0]
    nc = T // RET_CHUNK
    per_step = _pick(nc, (RET_STEP_CHUNKS, 1))
    steps = nc // per_step
    q_spec, k_spec, vv, pos, per_head, c_dec, state = _ret_specs(steps, per_step, False)

    def body(q_ref, k_ref, v_ref, g_ref, cos_ref, sin_ref, dec_ref, kd_ref, qd_ref, cd_ref,
             y_ref, o_ref, st_ref, state_ref):
        @pl.when(pl.program_id(0) == 0)
        def _():
            state_ref[...] = jnp.zeros_like(state_ref)

        for c in range(per_step):
            rows = pl.ds(c * RET_CHUNK, RET_CHUNK)
            cos_t, sin_t = cos_ref[rows, :], sin_ref[rows, :]
            for h in range(RET_HEADS):
                q = _rotary(q_ref[rows, _qk_cols(h)], cos_t, sin_t) * (RET_QK_DIM ** -0.5)
                k = _rotary(k_ref[rows, _qk_cols(h)], cos_t, sin_t)
                v = v_ref[rows, _v_cols(h)]
                prev = state_ref[h]
                scores = _dot(q.astype(BF16), k.astype(BF16), _NT) * dec_ref[h]
                inner = _dot(scores.astype(BF16), v, _NN)
                cross = _dot((q * qd_ref[h]).astype(BF16), prev.astype(BF16), _NN)
                o = inner + cross
                st_ref[h, c] = prev
                kv = _dot((k * kd_ref[h]).astype(BF16), v, _TN)
                state_ref[h] = prev * cd_ref[h, 0:1, 0:1] + kv
                o_ref[rows, _v_cols(h)] = o
                normed, _ = _head_norm(o)
                gate = g_ref[rows, _v_cols(h)]
                y_ref[rows, _v_cols(h)] = (gate * jax.nn.sigmoid(gate) * normed).astype(BF16)

    return pl.pallas_call(
        body,
        name="ret_fwd",
        grid=(steps,),
        in_specs=[q_spec, k_spec, vv, vv, pos, pos, per_head, per_head, per_head, c_dec],
        out_specs=[vv, vv, state],
        out_shape=[jax.ShapeDtypeStruct((T, RET_V_WIDTH), BF16),
                   jax.ShapeDtypeStruct((T, RET_V_WIDTH), F32),
                   jax.ShapeDtypeStruct((RET_HEADS, nc, RET_QK_DIM, RET_V_DIM), F32)],
        scratch_shapes=[pltpu.VMEM((RET_HEADS, RET_QK_DIM, RET_V_DIM), F32)],
        compiler_params=_params(("arbitrary",)),
    )(h_b, h_b, h_c, h_d, *tables)


def _ret_bwd(d_y, o_pre, states, h_b, h_c, h_d, tables, rider=None):
    T = h_b.shape[0]
    nc = T // RET_CHUNK
    per_step = _pick(nc, (RET_STEP_CHUNKS, 1))
    steps = nc // per_step
    q_spec, k_spec, vv, pos, per_head, c_dec, state = _ret_specs(steps, per_step, True)
    n_ride = rider.n if rider is not None else 0

    def body(*refs):
        (dy_ref, o_ref, st_ref, q_ref, k_ref, v_ref, g_ref, cos_ref, sin_ref, dec_ref, kd_ref, qd_ref,
         cd_ref) = refs[:13]
        dq_ref, dk_ref, dv_ref, dg_ref = refs[13 + n_ride:17 + n_ride]
        carry_ref = refs[17 + 2 * n_ride]
        ids = [pl.program_id(0)]
        if rider is not None:
            ride = (refs[13:13 + n_ride], refs[17 + n_ride:17 + 2 * n_ride], refs[-3:])
            rider.start_at_first(ids, ride)

        @pl.when(ids[0] == 0)
        def _():
            carry_ref[...] = jnp.zeros_like(carry_ref)

        scale = RET_QK_DIM ** -0.5
        for c in reversed(range(per_step)):
            rows = pl.ds(c * RET_CHUNK, RET_CHUNK)
            cos_t, sin_t = cos_ref[rows, :], sin_ref[rows, :]
            for h in range(RET_HEADS):
                q = _rotary(q_ref[rows, _qk_cols(h)], cos_t, sin_t) * scale
                k = _rotary(k_ref[rows, _qk_cols(h)], cos_t, sin_t)
                v = v_ref[rows, _v_cols(h)]
                decay, k_dec, q_dec = dec_ref[h], kd_ref[h], qd_ref[h]
                chunk_decay = cd_ref[h, 0:1, 0:1]
                state = st_ref[h, c].astype(BF16)
                later = carry_ref[h]
                later_b = later.astype(BF16)

                gate = g_ref[rows, _v_cols(h)]
                sig = jax.nn.sigmoid(gate)
                silu = gate * sig
                normed, rstd = _head_norm(o_ref[rows, _v_cols(h)])
                d_y = dy_ref[rows, _v_cols(h)]
                dg_ref[rows, _v_cols(h)] = (d_y * normed * (sig * (1.0 + gate * (1.0 - sig)))).astype(BF16)
                d_n = d_y * silu
                d_o = rstd * (d_n - jnp.mean(d_n, axis=1, keepdims=True)
                              - normed * jnp.mean(d_n * normed, axis=1, keepdims=True))
                d_ob = d_o.astype(BF16)

                qb, kb = q.astype(BF16), k.astype(BF16)
                qd_b, kd_b = (q * q_dec).astype(BF16), (k * k_dec).astype(BF16)
                scores = _dot(qb, kb, _NT) * decay
                d_scores = (_dot(d_ob, v, _NT) * decay).astype(BF16)
                dq = _dot(d_scores, kb, _NN) + _dot(d_ob, state, _NT) * q_dec
                dk = _dot(d_scores, qb, _TN) + _dot(v, later_b, _NT) * k_dec
                dv = _dot(scores.astype(BF16), d_ob, _TN) + _dot(kd_b, later_b, _NN)
                carry_ref[h] = _dot(qd_b, d_ob, _TN) + chunk_decay * later
                dq_ref[rows, _qk_cols(h)] = _rotary_transpose(dq * scale, cos_t, sin_t).astype(BF16)
                dk_ref[rows, _qk_cols(h)] = _rotary_transpose(dk, cos_t, sin_t).astype(BF16)
                dv_ref[rows, _v_cols(h)] = dv.astype(BF16)
        if rider is not None:
            rider.wait_at_last(ids, (steps,), ride)

    qk_out = pl.BlockSpec((per_step * RET_CHUNK, RET_QK_WIDTH), lambda n: (steps - 1 - n, 0))
    in_specs = [vv, vv, state, q_spec, k_spec, vv, vv, pos, pos, per_head, per_head, per_head, c_dec]
    out_specs = [qk_out, qk_out, vv, vv]
    out_shape = [jax.ShapeDtypeStruct((T, RET_QK_WIDTH), BF16), jax.ShapeDtypeStruct((T, RET_QK_WIDTH), BF16),
                 jax.ShapeDtypeStruct((T, RET_V_WIDTH), BF16), jax.ShapeDtypeStruct((T, RET_V_WIDTH), BF16)]
    args = (d_y, o_pre, states, h_b, h_b, h_c, h_d) + tuple(tables)
    scratch = [pltpu.VMEM((RET_HEADS, RET_QK_DIM, RET_V_DIM), F32)]
    if rider is not None:
        in_specs, args = in_specs + rider.specs, args + tuple(rider.bufs)
        out_specs, out_shape = out_specs + rider.specs, out_shape + rider.out_shape
        scratch = scratch + rider.scratch
    outs = pl.pallas_call(
        body,
        name="ret_bwd",
        grid=(steps,),
        in_specs=in_specs,
        out_specs=out_specs,
        out_shape=out_shape,
        scratch_shapes=scratch,
        compiler_params=_params(("arbitrary",)),
    )(*args)
    return outs[0], outs[1], outs[2], outs[3], list(outs[4:])


def _proj_tiles(h, x):
    return h[:, 0:1536], h[:, 1536:2560], h[:, 2560:3584], h[:, 3584:4608], h[:, 4608:6656], x


def _gate_mix_tiles(y_ret, h_e, b_gate, y_sb):
    gates = jax.nn.sigmoid(h_e + b_gate)
    return y_ret, gates[:, :D_MODEL] * y_sb + gates[:, D_MODEL:] * y_ret


def _gate_mix_grad_tiles(d_mix, h_e, b_gate, y_sb, y_ret):
    gates = jax.nn.sigmoid(h_e + b_gate)
    g0, g1 = gates[:, :D_MODEL], gates[:, D_MODEL:]
    d_e = jnp.concatenate([d_mix * y_sb * g0 * (1.0 - g0), d_mix * y_ret * g1 * (1.0 - g1)], axis=1)
    return d_mix * g0, d_mix * g1, d_e, d_e


def _ln_stats(u):
    mu = jnp.mean(u, axis=1, keepdims=True)
    cen = u - mu
    var = jnp.mean(cen * cen, axis=1, keepdims=True)
    rstd = lax.rsqrt(var + LN_EPS)
    return cen * rstd, rstd


def _ln_input_grad(d_out, gain, xhat, rstd):
    d_hat = d_out * gain
    return rstd * (d_hat - jnp.mean(d_hat, axis=1, keepdims=True)
                   - xhat * jnp.mean(d_hat * xhat, axis=1, keepdims=True))


def _ln_tiles(sub, x_prev, gain, bias):
    xhat, rstd = _ln_stats(DN_ALPHA * x_prev + sub)
    out = xhat * gain + bias
    return out, out, xhat, rstd


def _residual_tiles(d_sub, res):
    return (d_sub + DN_ALPHA * res,)


def _ln_grad_tiles(d_sub, res, xhat, rstd, gain):
    d_out = d_sub + DN_ALPHA * res
    du = _ln_input_grad(d_out, gain, xhat, rstd)
    return du, du, d_out * xhat, d_out


def _ln_loss_tiles(sub, x_prev, gain, bias, target):
    xhat, rstd = _ln_stats(DN_ALPHA * x_prev + sub)
    diff = xhat * gain + bias - target
    d_out = diff * (1.0 / D_MODEL)
    du = _ln_input_grad(d_out, gain, xhat, rstd)
    return du, du, diff * diff, d_out * xhat, d_out


def _mem_probs(q_h, k_h):
    s = _dot(q_h, k_h, _NT) * (MEM_HEAD_DIM ** -0.5)
    e = jnp.exp(s - jnp.max(s, axis=1, keepdims=True))
    return e / jnp.sum(e, axis=1, keepdims=True)


def _xattn_fwd(q, kv):
    T, mem_len = q.shape[0], kv.shape[0]
    tq = _pick(T, (512, 256, 128))

    def body(q_ref, kv_ref, o_ref):
        for h in range(MEM_HEADS):
            cols = slice(h * MEM_HEAD_DIM, (h + 1) * MEM_HEAD_DIM)
            vcols = slice(D_MODEL + h * MEM_HEAD_DIM, D_MODEL + (h + 1) * MEM_HEAD_DIM)
            p = _mem_probs(q_ref[:, cols], kv_ref[:, cols])
            o_ref[:, cols] = _dot(p.astype(BF16), kv_ref[:, vcols], _NN).astype(BF16)

    return pl.pallas_call(
        body,
        name="xattn_fwd",
        grid=(T // tq,),
        in_specs=[pl.BlockSpec((tq, D_MODEL), lambda i: (i, 0)),
                  pl.BlockSpec((mem_len, 2 * D_MODEL), lambda i: (0, 0))],
        out_specs=pl.BlockSpec((tq, D_MODEL), lambda i: (i, 0)),
        out_shape=jax.ShapeDtypeStruct((T, D_MODEL), BF16),
        compiler_params=_params(("parallel",)),
    )(q, kv)


def _xattn_bwd(q, kv, d_o):
    T, mem_len = q.shape[0], kv.shape[0]
    tq = _pick(T, (512, 256, 128))

    def body(q_ref, kv_ref, do_ref, dq_ref, dkv_ref):
        @pl.when(pl.program_id(0) == 0)
        def _():
            dkv_ref[...] = jnp.zeros_like(dkv_ref)

        for h in range(MEM_HEADS):
            cols = slice(h * MEM_HEAD_DIM, (h + 1) * MEM_HEAD_DIM)
            vcols = slice(D_MODEL + h * MEM_HEAD_DIM, D_MODEL + (h + 1) * MEM_HEAD_DIM)
            q_h, k_h, do_h = q_ref[:, cols], kv_ref[:, cols], do_ref[:, cols]
            p = _mem_probs(q_h, k_h)
            dp = _dot(do_h, kv_ref[:, vcols], _NT)
            ds = p * (dp - jnp.sum(dp * p, axis=1, keepdims=True))
            dsb = (ds * (MEM_HEAD_DIM ** -0.5)).astype(BF16)
            dq_ref[:, cols] = _dot(dsb, k_h, _NN).astype(BF16)
            dkv_ref[:, cols] += _dot(dsb, q_h, _TN)
            dkv_ref[:, vcols] += _dot(p.astype(BF16), do_h, _TN)

    row = pl.BlockSpec((tq, D_MODEL), lambda i: (i, 0))
    full = pl.BlockSpec((mem_len, 2 * D_MODEL), lambda i: (0, 0))
    return pl.pallas_call(
        body,
        name="xattn_bwd",
        grid=(T // tq,),
        in_specs=[row, full, row],
        out_specs=[row, full],
        out_shape=[jax.ShapeDtypeStruct((T, D_MODEL), BF16), jax.ShapeDtypeStruct((mem_len, 2 * D_MODEL), F32)],
        compiler_params=_params(("arbitrary",)),
    )(q, kv, d_o)


def _swiglu_tiles(f):
    a, b = f[:, :FFN_HIDDEN], f[:, FFN_HIDDEN:]
    return f, a * jax.nn.sigmoid(a) * b


def _swiglu_grad_tiles(d_hidden, f):
    a, b = f[:, :FFN_HIDDEN], f[:, FFN_HIDDEN:]
    sig = jax.nn.sigmoid(a)
    return (jnp.concatenate([d_hidden * b * (sig * (1.0 + a * (1.0 - sig))), d_hidden * (a * sig)], axis=1),)


def _local_step(x, mem, w_in, small, target, fetch, ship):
    T = x.shape[0]
    tables = _ret_tables(T)
    memb = mem.astype(BF16)

    (h_a, h_b, h_c, h_d, h_e, xb), w_ffn = fetch(
        ("w_ffn_in", "w_ffn_out"),
        lambda rider: _as_host(rider, _mm_fused(
            x, w_in, mode="nn", name="proj_in", extras=[], pass_a=True,
            outs=[(1536, BF16), (1024, F32), (1024, BF16), (1024, F32), (2048, F32), (D_MODEL, BF16)],
            epilogue=_proj_tiles, max_rows=256, rider=rider)))
    (a_sb, r_mat), w_mix = fetch(("w_sb_o", "w_ret_o", "w_mix_o", "w_mem_q", "w_mem_kv", "w_mem_o"),
                                 lambda rider: _sb_fwd(h_a, rider))
    w = {**w_ffn, **w_mix}
    y_gated, o_pre, states = _ret_fwd(h_b, h_c, h_d, tables)
    y_sb = _mm(a_sb, w["w_sb_o"], mode="nn", out_dtype=F32, name="sb_out")
    row_f32, row_bf16 = (D_MODEL, F32), (D_MODEL, BF16)
    ln_outs = [row_f32, row_bf16, row_f32, (1, F32)]
    y_ret, mix_in = _mm_fused(y_gated, w["w_ret_o"], mode="nn", name="ret_out", extras=[h_e, small["b_gate"], y_sb],
                              outs=[row_f32, row_bf16], epilogue=_gate_mix_tiles)
    x1, x1b, xhat1, rstd1 = _mm_fused(mix_in, w["w_mix_o"], mode="nn", name="mix_out",
                                      extras=[x, small["ln1_g"], small["ln1_b"]], outs=ln_outs, epilogue=_ln_tiles)
    q_m = _mm(x1b, w["w_mem_q"], mode="nn", out_dtype=BF16, name="mem_q")
    kv_m = _mm(memb, w["w_mem_kv"], mode="nn", out_dtype=BF16, name="mem_kv")
    o_m = _xattn_fwd(q_m, kv_m)
    x2, x2b, xhat2, rstd2 = _mm_fused(o_m, w["w_mem_o"], mode="nn", name="mem_out",
                                      extras=[x1, small["ln2_g"], small["ln2_b"]], outs=ln_outs, epilogue=_ln_tiles)
    f, hidden = _mm_fused(x2b, w["w_ffn_in"], mode="nn", name="ffn_in", extras=[],
                          outs=[(2 * FFN_HIDDEN, F32), (FFN_HIDDEN, BF16)], epilogue=_swiglu_tiles)
    du_outs, col = [row_f32, row_bf16], D_MODEL
    du3, du3b, loss_cols, d_ln3_g, d_ln3_b = _mm_fused(
        hidden, w["w_ffn_out"], mode="nn", name="ffn_out", extras=[x2, small["ln3_g"], small["ln3_b"], target],
        outs=du_outs, sums=[col, col, col], epilogue=_ln_loss_tiles)

    g_ffn_out = _mm(hidden, du3b, mode="tn", out_dtype=BF16, name="g_ffn_out")
    (d_f,) = _mm_fused(du3b, w["w_ffn_out"], mode="nt", name="d_hidden", extras=[f],
                       outs=[(2 * FFN_HIDDEN, BF16)], epilogue=_swiglu_grad_tiles)
    g_ffn_in = _mm(x2b, d_f, mode="tn", out_dtype=BF16, name="g_ffn_in")
    du2, du2b, d_ln2_g, d_ln2_b = ship(
        {"w_ffn_out": g_ffn_out},
        lambda rider: _as_host(rider, _mm_fused(
            d_f, w["w_ffn_in"], mode="nt", name="d_x2", extras=[du3, xhat2, rstd2, small["ln2_g"]], outs=du_outs,
            sums=[col, col], epilogue=_ln_grad_tiles, rider=rider, max_rows=256)))
    g_mem_o = _mm(o_m, du2b, mode="tn", out_dtype=BF16, name="g_mem_o")
    d_om = _mm(du2b, w["w_mem_o"], mode="nt", out_dtype=BF16, name="d_om")
    d_qm, d_kvm = _xattn_bwd(q_m, kv_m, d_om)
    g_mem_q = _mm(x1b, d_qm, mode="tn", out_dtype=BF16, name="g_mem_q")
    g_mem_kv = _mm(memb, d_kvm.astype(BF16), mode="tn", out_dtype=BF16, name="g_mem_kv")
    du1, du1b, d_ln1_g, d_ln1_b = _mm_fused(
        d_qm, w["w_mem_q"], mode="nt", name="d_x1", extras=[du2, xhat1, rstd1, small["ln1_g"]], outs=du_outs,
        sums=[col, col], epilogue=_ln_grad_tiles)
    g_mix_o = _mm(mix_in, du1b, mode="tn", out_dtype=BF16, name="g_mix_o")
    d_ysb, d_yret, d_e, d_b_gate = _mm_fused(
        du1b, w["w_mix_o"], mode="nt", name="d_mix_in", extras=[h_e, small["b_gate"], y_sb, y_ret],
        outs=[row_bf16, row_bf16, (2 * D_MODEL, BF16)], sums=[2 * D_MODEL], epilogue=_gate_mix_grad_tiles)
    g_sb_o = _mm(a_sb, d_ysb, mode="tn", out_dtype=BF16, name="g_sb_o")
    g_ret_o = _mm(y_gated, d_yret, mode="tn", out_dtype=BF16, name="g_ret_o")
    d_asb = _mm(d_ysb, w["w_sb_o"], mode="nt", out_dtype=BF16, name="d_asb")
    d_ygated = _mm(d_yret, w["w_ret_o"], mode="nt", out_dtype=F32, name="d_ygated")
    small_grads = {"b_gate": d_b_gate, "ln1_g": d_ln1_g, "ln1_b": d_ln1_b, "ln2_g": d_ln2_g, "ln2_b": d_ln2_b,
                   "ln3_g": d_ln3_g, "ln3_b": d_ln3_b, "loss_cols": loss_cols}
    d_rq, d_rk, d_c, d_d = ship({"w_mem_kv": g_mem_kv, "w_mem_q": g_mem_q, "w_mem_o": g_mem_o, "w_mix_o": g_mix_o},
                                lambda rider: _ret_bwd(d_ygated, o_pre, states, h_b, h_c, h_d, tables, rider))
    d_q, d_k, d_v = ship({"w_ffn_in": g_ffn_in, "w_ret_o": g_ret_o, "w_sb_o": g_sb_o, "small": small_grads},
                         lambda rider: _sb_bwd(h_a, d_asb, r_mat, rider))
    d_h = [("sb_q", d_q), ("sb_k", d_k), ("sb_v", d_v), ("ret_q", d_rq), ("ret_k", d_rk), ("ret_v", d_c),
           ("ret_g", d_d), ("gate", d_e)]
    g_in = jnp.concatenate([_mm(xb, piece, mode="tn", out_dtype=BF16, name="g_in_" + tag) for tag, piece in d_h],
                           axis=1)
    (d_x,) = ship({"w_in": g_in},
                  lambda rider: _as_host(rider, _mm_fused(
                      [piece for _, piece in d_h], w_in, mode="nt", name="d_x", extras=[du1], outs=[(D_MODEL, F32)],
                      epilogue=_residual_tiles, rider=rider, max_rows=256)))
    return d_x


def _adamw_math(w, g, m, v):
    m = ADAM_B1 * m + (1.0 - ADAM_B1) * g
    v = ADAM_B2 * v + (1.0 - ADAM_B2) * jnp.square(g)
    m_hat = m / (1.0 - ADAM_B1 ** ADAM_STEP)
    v_hat = v / (1.0 - ADAM_B2 ** ADAM_STEP)
    delta = -ADAM_LR * (m_hat / (jnp.sqrt(v_hat) + ADAM_EPS) + ADAM_WD * w)
    return delta, m, v


def _adamw(parts, w, m, v, name):
    R, C = w.shape
    tr = max(t for t in range(16, max(32, R // 8) + 1, 16) if R % t == 0) if R >= 32 else R

    def body(p_ref, w_ref, m_ref, v_ref, g_ref, d_ref, nm_ref, nv_ref):
        g = p_ref[0].astype(F32)
        for j in range(1, N_DEV):
            g = g + p_ref[j].astype(F32)
        delta, nm, nv = _adamw_math(w_ref[...], g, m_ref[...], v_ref[...])
        g_ref[...] = g
        d_ref[...] = delta
        nm_ref[...] = nm
        nv_ref[...] = nv

    blk = pl.BlockSpec((tr, C), lambda i: (i, 0))
    out = jax.ShapeDtypeStruct((R, C), F32)
    return pl.pallas_call(
        body,
        name=name,
        grid=(R // tr,),
        in_specs=[pl.BlockSpec((N_DEV, tr, C), lambda i: (0, i, 0)), blk, blk, blk],
        out_specs=[blk] * 4,
        out_shape=[out] * 4,
        compiler_params=_params(("parallel",)),
    )(parts, w, m, v)


_SHARD_AXIS = {"w_in": 1, "w_sb_o": 1, "w_ret_o": 0, "w_mix_o": 0, "w_mem_q": 0, "w_mem_kv": 1, "w_mem_o": 0,
               "w_ffn_in": 1, "w_ffn_out": 0}
_MATRICES = tuple(_SHARD_AXIS)
_SMALL = ("b_gate", "ln1_g", "ln1_b", "ln2_g", "ln2_b", "ln3_g", "ln3_b")
_WEIGHT_ORDER = ("w_in", "b_gate", "w_sb_o", "w_ret_o", "w_mix_o", "ln1_g", "ln1_b", "w_mem_q", "w_mem_kv", "w_mem_o",
                 "ln2_g", "ln2_b", "w_ffn_in", "w_ffn_out", "ln3_g", "ln3_b")


def _assemble(name, gathered):
    if _SHARD_AXIS[name] == 0:
        return gathered.reshape(-1, gathered.shape[2])
    return jnp.transpose(gathered, (1, 0, 2)).reshape(gathered.shape[1], -1)


def _to_slots(name, full):
    if _SHARD_AXIS[name] == 0:
        return full.reshape(N_DEV, full.shape[0] // N_DEV, full.shape[1])
    return jnp.transpose(full.reshape(full.shape[0], N_DEV, full.shape[1] // N_DEV), (1, 0, 2))


SMALL_ROWS = 16


def _pack_small(vals):
    return jnp.concatenate([vals["b_gate"].reshape(2, D_MODEL)] + [vals[n] for n in _SMALL[1:]], axis=0)


def _unpack_small(packed):
    out = {"b_gate": packed[0:2].reshape(1, 2 * D_MODEL)}
    for i, n in enumerate(_SMALL[1:]):
        out[n] = packed[2 + i:3 + i]
    return out


def kernel(x, mem, w_in, b_gate, w_sb_o, w_ret_o, w_mix_o, ln1_g, ln1_b, w_mem_q, w_mem_kv, w_mem_o, ln2_g, ln2_b, w_ffn_in, w_ffn_out, ln3_g, ln3_b, loss_target, m_w_in, m_b_gate, m_w_sb_o, m_w_ret_o, m_w_mix_o, m_ln1_g, m_ln1_b, m_w_mem_q, m_w_mem_kv, m_w_mem_o, m_ln2_g, m_ln2_b, m_w_ffn_in, m_w_ffn_out, m_ln3_g, m_ln3_b, v_w_in, v_b_gate, v_w_sb_o, v_w_ret_o, v_w_mix_o, v_ln1_g, v_ln1_b, v_w_mem_q, v_w_mem_kv, v_w_mem_o, v_ln2_g, v_ln2_b, v_w_ffn_in, v_w_ffn_out, v_ln3_g, v_ln3_b):
    weights = dict(w_in=w_in, b_gate=b_gate, w_sb_o=w_sb_o, w_ret_o=w_ret_o, w_mix_o=w_mix_o, ln1_g=ln1_g, ln1_b=ln1_b,
                   w_mem_q=w_mem_q, w_mem_kv=w_mem_kv, w_mem_o=w_mem_o, ln2_g=ln2_g, ln2_b=ln2_b, w_ffn_in=w_ffn_in,
                   w_ffn_out=w_ffn_out, ln3_g=ln3_g, ln3_b=ln3_b)
    mom1 = dict(w_in=m_w_in, b_gate=m_b_gate, w_sb_o=m_w_sb_o, w_ret_o=m_w_ret_o, w_mix_o=m_w_mix_o, ln1_g=m_ln1_g,
                ln1_b=m_ln1_b, w_mem_q=m_w_mem_q, w_mem_kv=m_w_mem_kv, w_mem_o=m_w_mem_o, ln2_g=m_ln2_g, ln2_b=m_ln2_b,
                w_ffn_in=m_w_ffn_in, w_ffn_out=m_w_ffn_out, ln3_g=m_ln3_g, ln3_b=m_ln3_b)
    mom2 = dict(w_in=v_w_in, b_gate=v_b_gate, w_sb_o=v_w_sb_o, w_ret_o=v_w_ret_o, w_mix_o=v_w_mix_o, ln1_g=v_ln1_g,
                ln1_b=v_ln1_b, w_mem_q=v_w_mem_q, w_mem_kv=v_w_mem_kv, w_mem_o=v_w_mem_o, ln2_g=v_ln2_g, ln2_b=v_ln2_b,
                w_ffn_in=v_w_ffn_in, w_ffn_out=v_w_ffn_out, ln3_g=v_ln3_g, ln3_b=v_ln3_b)

    (gathered_in,) = _exchange([weights["w_in"][0].astype(BF16)], False, "gather_w_in")
    received = {}

    def fetch(names, host):
        res = host(_Rider([weights[n][0].astype(BF16) for n in names], False))
        return res[:-1], {n: _assemble(n, g) for n, g in zip(names, res[-1])}

    def ship(grads, host):
        names = list(grads)
        bufs = []
        for n in names:
            if n == "small":
                part = jnp.concatenate([_pack_small(grads[n]), grads[n]["loss_cols"],
                                        jnp.zeros((SMALL_ROWS - 9, D_MODEL), F32)], axis=0)
                bufs.append(jnp.broadcast_to(part[None], (N_DEV,) + part.shape))
            else:
                bufs.append(_to_slots(n, grads[n]).astype(BF16))
        res = host(_Rider(bufs, True))
        received.update(zip(names, res[-1]))
        return res[:-1]

    small = {n: weights[n] for n in _SMALL}
    d_x = _local_step(x[0], mem[0], _assemble("w_in", gathered_in), small, loss_target[0], fetch, ship)

    new = {}
    for n in _MATRICES:
        new[n] = _adamw(received[n], weights[n][0], mom1[n][0], mom2[n][0], "adamw_" + n)
    packed = _adamw(received["small"][:, :8], _pack_small({n: weights[n] for n in _SMALL}),
                    _pack_small({n: mom1[n] for n in _SMALL}), _pack_small({n: mom2[n] for n in _SMALL}), "adamw_small")
    small_new = [_unpack_small(p) for p in packed]
    loss = jnp.sum(received["small"][:, 8]) * (0.5 / D_MODEL)

    outs = [loss, d_x[None]]
    for slot in range(4):
        for n in _WEIGHT_ORDER:
            outs.append(new[n][slot][None] if n in new else small_new[slot][n])
    return tuple(outs)
```

```python
import functools
import math

import jax
import jax.numpy as jnp
from jax import lax
from jax.experimental import pallas as pl
from jax.experimental.pallas import tpu as pltpu

F32 = jnp.float32
BF16 = jnp.bfloat16

N_DEV = 8
D_MODEL = 1024
SB_HEAD_DIM = 64
SB_WIDTH = 512
RET_HEADS = 4
RET_QK_DIM = 128
RET_V_DIM = 256
RET_QK_WIDTH = 512
RET_V_WIDTH = 1024
RET_CHUNK = 128
RET_STEP_CHUNKS = 4
ROPE_BASE = 10000.0
MEM_HEADS = 4
MEM_HEAD_DIM = 256
FFN_HIDDEN = 2816
DN_ALPHA = 2.0 ** 0.25
LN_EPS = 1e-5
ADAM_LR = 0.001
ADAM_B1 = 0.9
ADAM_B2 = 0.999
ADAM_EPS = 1e-08
ADAM_WD = 0.01
ADAM_STEP = 10

VMEM_LIMIT_BYTES = 52 * 1024 * 1024
LANES = 128
SB_KEY_BLOCK = 128
SB_Q_BLOCK = 256
SB_DEAD_LOG = -105.0

MESH_AXES = ("x", "y", "c")


def _pick(dim, prefs):
    for p in prefs:
        if dim % p == 0:
            return p
    return dim


def _params(sem):
    return pltpu.CompilerParams(dimension_semantics=sem, vmem_limit_bytes=VMEM_LIMIT_BYTES)


def _dot(a, b, dims):
    return lax.dot_general(a, b, (dims, ((), ())), preferred_element_type=F32)


_NN = ((1,), (0,))
_NT = ((1,), (1,))
_TN = ((0,), (0,))


def _my_index():
    return 4 * lax.axis_index("x") + 2 * lax.axis_index("y") + lax.axis_index("c")


def _peer(k):
    x, y, c = lax.axis_index("x"), lax.axis_index("y"), lax.axis_index("c")
    bx, by, bc = (k >> 2) & 1, (k >> 1) & 1, k & 1
    px = (1 - x) if bx else x
    py = (1 - y) if by else y
    pc = (1 - c) if bc else c
    return (px, py, pc), 4 * px + 2 * py + pc


class _Rider:
    def __init__(self, bufs, scatter):
        self.bufs, self.scatter, self.n = list(bufs), scatter, len(bufs)
        self.specs = [pl.BlockSpec(memory_space=pl.ANY)] * self.n
        self.out_shape = [jax.ShapeDtypeStruct(b.shape if scatter else (N_DEV,) + b.shape, b.dtype) for b in self.bufs]
        self.scratch = [pltpu.SemaphoreType.DMA((self.n, N_DEV - 1)), pltpu.SemaphoreType.DMA((self.n, N_DEV - 1)),
                        pltpu.SemaphoreType.DMA((self.n,))]

    def _remote(self, ride, a, k, src_ref, slot, to):
        _, dst, (send_sems, recv_sems, _) = ride
        return pltpu.make_async_remote_copy(src_ref=src_ref, dst_ref=dst[a].at[slot], send_sem=send_sems.at[a, k],
                                            recv_sem=recv_sems.at[a, k], device_id=to,
                                            device_id_type=pl.DeviceIdType.MESH)

    def _local(self, ride, a):
        src, dst, (_, _, local_sems) = ride
        me = _my_index()
        return pltpu.make_async_copy(src[a].at[me] if self.scatter else src[a], dst[a].at[me], local_sems.at[a])

    def _direct(self, ride, a):
        src = ride[0]
        me = _my_index()
        out = []
        for k in range(1, N_DEV):
            peer, peer_idx = _peer(k)
            out.append(self._remote(ride, a, k - 1, src[a].at[peer_idx], me, peer))
        return out

    def _two_level(self, ride, a):
        src, dst = ride[0], ride[1]
        x, y, c = lax.axis_index("x"), lax.axis_index("y"), lax.axis_index("c")
        me, sibling = _my_index(), (x, y, 1 - c)
        chips = [(1 - x, y), (x, 1 - y), (1 - x, 1 - y)]
        first = [self._remote(ride, a, 0, src[a], me, sibling)]
        passed, landing = [], [self._remote(ride, a, 0, src[a], me + 1 - 2 * c, sibling)]
        for j, (px, py) in enumerate(chips):
            first.append(self._remote(ride, a, 1 + j, src[a], me, (px, py, c)))
            theirs = 4 * px + 2 * py + c
            passed.append(self._remote(ride, a, 4 + j, dst[a].at[theirs], theirs, sibling))
            landing.append(self._remote(ride, a, 1 + j, src[a], theirs, (px, py, c)))
        for j, (px, py) in enumerate(chips):
            landing.append(self._remote(ride, a, 4 + j, src[a], 4 * px + 2 * py + 1 - c, sibling))
        return first, passed, landing

    def start(self, ride):
        for a in range(self.n):
            self._local(ride, a).start()
            for cp in (self._direct(ride, a) if self.scatter else self._two_level(ride, a)[0]):
                cp.start()

    def finish(self, ride):
        if self.scatter:
            for a in range(self.n):
                for cp in self._direct(ride, a):
                    cp.wait()
                self._local(ride, a).wait()
            return
        levels = [self._two_level(ride, a) for a in range(self.n)]
        for first, passed, landing in levels:
            for j, cp in enumerate(passed):
                landing[1 + j].wait_recv()
                cp.start()
        for a, (first, passed, landing) in enumerate(levels):
            landing[0].wait_recv()
            for cp in landing[4:]:
                cp.wait_recv()
            for cp in first + passed:
                cp.wait_send()
            self._local(ride, a).wait()

    def start_at_first(self, ids, ride):
        first = functools.reduce(jnp.logical_and, [i == 0 for i in ids])

        @pl.when(first)
        def _():
            self.start(ride)

    def wait_at_last(self, ids, grid, ride):
        last = functools.reduce(jnp.logical_and, [i == g - 1 for i, g in zip(ids, grid)])

        @pl.when(last)
        def _():
            self.finish(ride)


def _exchange(bufs, scatter, name):
    rider = _Rider(bufs, scatter)

    def body(*refs):
        ride = (refs[:rider.n], refs[rider.n:2 * rider.n], refs[2 * rider.n:])
        rider.start(ride)
        rider.finish(ride)

    return pl.pallas_call(
        body,
        name=name,
        in_specs=rider.specs,
        out_specs=rider.specs,
        out_shape=rider.out_shape,
        scratch_shapes=rider.scratch,
    )(*rider.bufs)


MM_RESIDENT_B_BYTES = 14 * 1024 * 1024
MM_A_TILE_BYTES = 4 * 1024 * 1024
MM_OUT_TILE_BYTES = 6 * 1024 * 1024


def _mm_tiles(mode, M, N, K, a_bytes, out_bytes):
    if mode != "tn" and K * N * 2 <= MM_RESIDENT_B_BYTES:
        for tm in (1024, 512, 256, 128):
            if M % tm == 0 and tm * K * a_bytes <= MM_A_TILE_BYTES and tm * N * out_bytes <= MM_OUT_TILE_BYTES:
                return tm, N, K
    if mode == "tn":
        return (_pick(M, (1024, 1408, 512, 256, 128)), _pick(N, (1024, 1664, 1408, 512, 256, 128)),
                _pick(K, (2048, 1024, 512, 256, 128)))
    return _pick(M, (1024, 512, 256, 128)), _pick(N, (512, 256, 128)), _pick(K, (1024, 512, 256, 128))


def _mm(a, b, *, mode, out_dtype, name, res=None, res_scale=1.0, rider=None):
    if mode == "nn":
        (M, K), (K2, N) = a.shape, b.shape
    elif mode == "nt":
        (M, K), (N, K2) = a.shape, b.shape
    else:
        (K, M), (K2, N) = a.shape, b.shape
    assert K == K2, (a.shape, b.shape, mode)
    out_bytes = jnp.dtype(out_dtype).itemsize + (4 if res is not None else 0)
    tm, tn, tk = _mm_tiles(mode, M, N, K, a.dtype.itemsize, out_bytes)
    grid = (M // tm, N // tn, K // tk)
    nk = grid[2]
    dims = {"nn": _NN, "nt": _NT, "tn": _TN}[mode]
    n_in = 2 + (res is not None)
    n_ride = rider.n if rider is not None else 0

    def body(*refs):
        a_ref, b_ref = refs[:2]
        r_ref = refs[2] if res is not None else None
        o_ref = refs[n_in + n_ride]
        rest = refs[n_in + 2 * n_ride + 1:]
        acc_ref = rest[0] if nk > 1 else None
        ids = [pl.program_id(d) for d in range(3)]
        if rider is not None:
            ride = (refs[n_in:n_in + n_ride], refs[n_in + n_ride + 1:n_in + 2 * n_ride + 1], rest[-3:])
            rider.start_at_first(ids, ride)
        part = _dot(a_ref[...].astype(BF16), b_ref[...].astype(BF16), dims)

        def finish(total):
            if r_ref is not None:
                total = total + res_scale * r_ref[...]
            o_ref[...] = total.astype(out_dtype)

        if nk == 1:
            finish(part)
        else:
            k = ids[2]

            @pl.when(k == 0)
            def _():
                acc_ref[...] = part

            @pl.when(k > 0)
            def _():
                acc_ref[...] += part

            @pl.when(k == nk - 1)
            def _():
                finish(acc_ref[...])

        if rider is not None:
            rider.wait_at_last(ids, grid, ride)

    if mode == "nn":
        a_spec = pl.BlockSpec((tm, tk), lambda i, j, k: (i, k))
        b_spec = pl.BlockSpec((tk, tn), lambda i, j, k: (k, j))
    elif mode == "nt":
        a_spec = pl.BlockSpec((tm, tk), lambda i, j, k: (i, k))
        b_spec = pl.BlockSpec((tn, tk), lambda i, j, k: (j, k))
    else:
        a_spec = pl.BlockSpec((tk, tm), lambda i, j, k: (k, i))
        b_spec = pl.BlockSpec((tk, tn), lambda i, j, k: (k, j))
    o_spec = pl.BlockSpec((tm, tn), lambda i, j, k: (i, j))
    in_specs = [a_spec, b_spec] + ([o_spec] if res is not None else [])
    args = (a, b) + ((res,) if res is not None else ())
    out_specs, out_shape = [o_spec], [jax.ShapeDtypeStruct((M, N), out_dtype)]
    scratch = [pltpu.VMEM((tm, tn), F32)] if nk > 1 else []
    sem = ("parallel", "parallel", "arbitrary")
    if rider is not None:
        in_specs, args = in_specs + rider.specs, args + tuple(rider.bufs)
        out_specs, out_shape = out_specs + rider.specs, out_shape + rider.out_shape
        scratch = scratch + rider.scratch
        sem = ("arbitrary",) * 3
    outs = pl.pallas_call(
        body,
        name=name,
        grid=grid,
        in_specs=in_specs,
        out_specs=out_specs,
        out_shape=out_shape,
        scratch_shapes=scratch,
        compiler_params=_params(sem),
    )(*args)
    return outs[0] if rider is None else (outs[0], list(outs[1:]))


def _mm_host(a, b, *, rider, **kw):
    out = _mm(a, b, rider=rider, **kw)
    return out if rider is not None else (out, [])


def _as_host(rider, results):
    return results if rider is not None else tuple(results) + ([],)


MM_FUSED_MARGIN_BYTES = 10 * 1024 * 1024
MM_FUSED_MAX_ROWS = 512


def _col_sum_update(acc_ref, val, first):
    part = jnp.sum(val.reshape(val.shape[0] // 8, 8, val.shape[1]), axis=0)

    @pl.when(first)
    def _():
        acc_ref[...] = part

    @pl.when(jnp.logical_not(first))
    def _():
        acc_ref[...] += part


def _mm_fused(a, b, *, mode, name, extras, outs, epilogue, sums=(), rider=None, max_rows=MM_FUSED_MAX_ROWS,
              pass_a=False):
    parts = list(a) if isinstance(a, (list, tuple)) else [a]
    M, K = parts[0].shape[0], sum(p.shape[1] for p in parts)
    if mode == "nn":
        (K2, N), b_dims = b.shape, _NN
    else:
        (N, K2), b_dims = b.shape, _NT
    assert K == K2, (K, b.shape, mode)
    rows = parts + [e for e in extras if e.shape[0] == M]
    per_row = 2 * (sum(e.shape[1] * e.dtype.itemsize for e in rows)
                   + sum(c * jnp.dtype(d).itemsize for c, d in outs)) + 2 * N * 4
    budget = VMEM_LIMIT_BYTES - K * N * 2 - MM_FUSED_MARGIN_BYTES
    tm = next(t for t in (512, 256, 128, 64, 32, 16) if t <= max_rows and M % t == 0 and t * per_row <= budget)
    steps = M // tm
    n_a, n_x, n_o, n_s = len(parts), len(extras), len(outs), len(sums)
    n_ride = rider.n if rider is not None else 0

    def body(*refs):
        a_refs, b_ref = refs[:n_a], refs[n_a]
        x_refs = refs[n_a + 1:n_a + 1 + n_x]
        base = n_a + 1 + n_x + n_ride
        o_refs, s_refs = refs[base:base + n_o], refs[base + n_o:base + n_o + n_s]
        acc_refs = refs[base + n_o + n_s + n_ride:base + n_o + 2 * n_s + n_ride]
        ids = [pl.program_id(0)]
        if rider is not None:
            ride = (refs[n_a + 1 + n_x:base], refs[base + n_o + n_s:base + n_o + n_s + n_ride], refs[-3:])
            rider.start_at_first(ids, ride)
        a_tile = a_refs[0][...]
        a_bf16 = a_tile.astype(BF16) if n_a == 1 else jnp.concatenate([r[...].astype(BF16) for r in a_refs], axis=1)
        prod = _dot(a_bf16, b_ref[...], b_dims)
        tiles = epilogue(prod, *([a_tile] if pass_a else []), *[r[...] for r in x_refs])
        for o_ref, t in zip(o_refs, tiles[:n_o]):
            o_ref[...] = t.astype(o_ref.dtype)
        for acc_ref, t in zip(acc_refs, tiles[n_o:]):
            _col_sum_update(acc_ref, t, ids[0] == 0)
        if n_s:
            @pl.when(ids[0] == steps - 1)
            def _():
                for s_ref, acc_ref in zip(s_refs, acc_refs):
                    s_ref[...] = jnp.sum(acc_ref[...], axis=0, keepdims=True)
        if rider is not None:
            rider.wait_at_last(ids, (steps,), ride)

    in_specs = [pl.BlockSpec((tm, p.shape[1]), lambda i: (i, 0)) for p in parts]
    in_specs.append(pl.BlockSpec(b.shape, lambda i: (0, 0), pipeline_mode=pl.Buffered(1)))
    for e in extras:
        in_specs.append(pl.BlockSpec((tm, e.shape[1]), lambda i: (i, 0)) if e.shape[0] == M
                        else pl.BlockSpec(e.shape, lambda i: (0, 0)))
    out_specs = ([pl.BlockSpec((tm, c), lambda i: (i, 0)) for c, _ in outs]
                 + [pl.BlockSpec((1, c), lambda i: (0, 0)) for c in sums])
    out_shape = ([jax.ShapeDtypeStruct((M, c), d) for c, d in outs]
                 + [jax.ShapeDtypeStruct((1, c), F32) for c in sums])
    args = tuple(parts) + (b,) + tuple(extras)
    scratch = [pltpu.VMEM((8, c), F32) for c in sums]
    if rider is not None:
        in_specs, args = in_specs + rider.specs, args + tuple(rider.bufs)
        out_specs, out_shape = out_specs + rider.specs, out_shape + rider.out_shape
        scratch = scratch + rider.scratch
    res = pl.pallas_call(
        body,
        name=name,
        grid=(steps,),
        in_specs=in_specs,
        out_specs=out_specs,
        out_shape=out_shape,
        scratch_shapes=scratch,
        compiler_params=_params(("arbitrary",) if (n_s or rider is not None) else ("parallel",)),
    )(*args)
    return tuple(res[:n_o + n_s]) + ((list(res[n_o + n_s:]),) if rider is not None else ())


def _pair_rows(blk, lane_is_a):
    zero = jnp.zeros_like(blk)
    return jnp.concatenate([jnp.where(lane_is_a, blk, zero), jnp.where(lane_is_a, zero, blk)], axis=0)


SB_STRIP = 32
SB_FWD_PAIRS = 4
SB_BWD_PAIRS = 2
SB_GROUP = 2


def _pair_lanes(p):
    return slice(p * LANES, (p + 1) * LANES)


def _sb_scan_matrices():
    o = lax.broadcasted_iota(jnp.int32, (2 * LANES, 4 * LANES), 0)
    c = lax.broadcasted_iota(jnp.int32, (2 * LANES, 4 * LANES), 1) & (2 * LANES - 1)
    same = (o >= LANES) == (c >= LANES)
    oo, cc = o & (LANES - 1), c & (LANES - 1)
    return (jnp.where(same & (cc > oo), -1.0, 0.0).astype(BF16), jnp.where(same & (cc < oo), 1.0, 0.0).astype(BF16))


def _sb_causal_masks(tq):
    d = lax.broadcasted_iota(jnp.int32, (tq // SB_KEY_BLOCK, SB_KEY_BLOCK, tq), 0)
    k = lax.broadcasted_iota(jnp.int32, (tq // SB_KEY_BLOCK, SB_KEY_BLOCK, tq), 1)
    t = lax.broadcasted_iota(jnp.int32, (tq // SB_KEY_BLOCK, SB_KEY_BLOCK, tq), 2)
    return jnp.where(d * SB_KEY_BLOCK + k < t, 1.0, 0.0).astype(F32)


def _sb_log_terms(z):
    minus_abs = lax.bitcast_convert_type(lax.bitcast_convert_type(z, jnp.uint32) | jnp.uint32(0x80000000), F32)
    spent = jnp.maximum(z, 0.0) + jnp.log(1.0 + jnp.exp(minus_abs))
    return spent, z - spent


def _sb_store_split(ref, strip, val, cols):
    hi = val.astype(BF16)
    ref[pl.ds(strip * SB_STRIP, SB_STRIP), cols] = hi
    ref[pl.ds(2 * LANES + strip * SB_STRIP, SB_STRIP), cols] = (val - hi.astype(F32)).astype(BF16)


def _sb_lanes(tq, diag):
    if diag == "left":
        return 0, tq // 2
    first = 0 if diag is None else diag * SB_KEY_BLOCK
    return first, tq - first


def _lane_add(full, part, lanes):
    first, width = lanes
    pieces = [full[:, :first]] if first else []
    pieces.append(full[:, first:first + width] + part)
    if first + width < full.shape[1]:
        pieces.append(full[:, first + width:])
    return pieces[0] if len(pieces) == 1 else jnp.concatenate(pieces, axis=1)


def _sb_fwd(h_a, rider=None):
    assert SB_FWD_PAIRS == 4
    T = h_a.shape[0]
    tq = _pick(T, (SB_Q_BLOCK, SB_KEY_BLOCK))
    nq, per_q, nkb = T // tq, tq // SB_KEY_BLOCK, T // SB_KEY_BLOCK
    assert per_q % SB_GROUP == 0
    n_strips = 2 * LANES // SB_STRIP
    n_ride = rider.n if rider is not None else 0
    after_m, _ = _sb_scan_matrices()
    causal_m = _sb_causal_masks(tq)
    pairs = SB_FWD_PAIRS

    def body(*refs):
        q_ref, k_ref, v_ref, after_ref, causal_ref = refs[:5]
        a_ref, r_ref, n_ref = refs[5 + n_ride:8 + n_ride]
        z_ref, lb_ref, split_ref, w_ref = refs[8 + 2 * n_ride:12 + 2 * n_ride]
        ids = [pl.program_id(0)]
        if rider is not None:
            ride = (refs[5:5 + n_ride], refs[8 + n_ride:8 + 2 * n_ride], refs[-3:])
            rider.start_at_first(ids, ride)
        i = ids[0]
        q_t = [(q_ref[:, _pair_lanes(p)].astype(F32).T * (SB_HEAD_DIM ** -0.5)).astype(BF16) for p in range(pairs)]
        lane_is_a = lax.broadcasted_iota(jnp.int32, (SB_KEY_BLOCK, LANES), 1) < SB_HEAD_DIM

        def tiles(kbs, diags, carry):
            nb = len(kbs)
            lanes = [_sb_lanes(tq, d) for d in diags]
            cols = [slice(first, first + width) for first, width in lanes]
            acc_t, ra, rb = [list(c) for c in carry]
            ks = [pl.multiple_of(kb * SB_KEY_BLOCK, SB_KEY_BLOCK) for kb in kbs]
            slot = lambda p, b: p * nb + b

            def causal(b, s):
                return causal_ref[diags[b], pl.ds((s * SB_STRIP) % SB_KEY_BLOCK, SB_STRIP), cols[b]]

            vv = {}
            for b in range(nb):
                for p in range(pairs):
                    kk = _pair_rows(k_ref[pl.ds(ks[b], SB_KEY_BLOCK), _pair_lanes(p)], lane_is_a)
                    vv[p, b] = _pair_rows(v_ref[pl.ds(ks[b], SB_KEY_BLOCK), _pair_lanes(p)], lane_is_a)
                    z_ref[slot(p, b), :, cols[b]] = _dot(kk, q_t[p][:, cols[b]], _NN)
            sums = {}
            for b in range(nb):
                for p in range(pairs):
                    part = [jnp.zeros((8, lanes[b][1]), F32), jnp.zeros((8, lanes[b][1]), F32)]
                    for s in range(n_strips):
                        rows = pl.ds(s * SB_STRIP, SB_STRIP)
                        spent, log_beta = _sb_log_terms(z_ref[slot(p, b), rows, cols[b]])
                        lb_ref[slot(p, b), rows, cols[b]] = log_beta
                        if isinstance(diags[b], int):
                            spent = spent * causal(b, s)
                        _sb_store_split(split_ref.at[slot(p, b)], s, spent, cols[b])
                        head = (s * SB_STRIP) // SB_KEY_BLOCK
                        part[head] = part[head] + jnp.sum(spent.reshape(SB_STRIP // 8, 8, lanes[b][1]), axis=0)
                    sums[p, b] = part
            for b in range(nb):
                for p in range(pairs):
                    z_ref[slot(p, b), :, cols[b]] = _dot(after_ref[...], split_ref[slot(p, b), :, cols[b]], _NN)
            for b in range(nb):
                for p in range(pairs):
                    for s in range(n_strips):
                        rows = pl.ds(s * SB_STRIP, SB_STRIP)
                        start = (ra[p] if (s * SB_STRIP) < SB_KEY_BLOCK else rb[p])[:, cols[b]]
                        w = jnp.exp(lb_ref[slot(p, b), rows, cols[b]] + z_ref[slot(p, b), rows, cols[b]] + start)
                        if isinstance(diags[b], int):
                            w = w * causal(b, s)
                        w_ref[slot(p, b), rows, cols[b]] = w.astype(BF16)
                    r_ref[2 * p, kbs[b]] = ra[p]
                    r_ref[2 * p + 1, kbs[b]] = rb[p]
                    ra[p] = _lane_add(ra[p], -jnp.sum(sums[p, b][0], axis=0, keepdims=True), lanes[b])
                    rb[p] = _lane_add(rb[p], -jnp.sum(sums[p, b][1], axis=0, keepdims=True), lanes[b])
            for b in range(nb):
                for p in range(pairs):
                    acc_t[p] = _lane_add(acc_t[p], _dot(vv[p, b], w_ref[slot(p, b), :, cols[b]], _TN), lanes[b])
            return tuple(acc_t), tuple(ra), tuple(rb)

        carry = (tuple(jnp.zeros((LANES, tq), F32) for _ in range(pairs)),
                 tuple(jnp.zeros((1, tq), F32) for _ in range(pairs)),
                 tuple(jnp.zeros((1, tq), F32) for _ in range(pairs)))
        own = list(reversed(range(per_q)))
        n_full = i * per_q
        carry = lax.cond(
            i > 0,
            lambda cc: tiles([n_full + d for d in own] + [n_full - 1 - b for b in range(SB_GROUP)],
                             own + [None] * SB_GROUP, cc),
            lambda cc: tiles([n_full + d for d in own], own, cc), carry)
        first_walked = jnp.where(i > 0, SB_GROUP, 0).astype(jnp.int32)

        def top_of(sums_a, sums_b, first):
            return jnp.max(functools.reduce(jnp.maximum, [r[:, first:] for r in sums_a + sums_b]))

        def alive(c):
            return jnp.logical_and(c[0] < n_full, top_of(c[2], c[3], 0) > SB_DEAD_LOG)

        def step(c):
            kbs = [n_full - 1 - c[0] - b for b in range(SB_GROUP)]
            return (c[0] + SB_GROUP,) + lax.cond(
                top_of(c[2], c[3], tq // 2) > SB_DEAD_LOG,
                lambda cc: tiles(kbs, [None] * SB_GROUP, cc), lambda cc: tiles(kbs, ["left"] * SB_GROUP, cc), c[1:])

        walked, acc_t, _, _ = lax.while_loop(alive, step, (first_walked,) + carry)
        for p in range(pairs):
            a_ref[:, _pair_lanes(p)] = acc_t[p].T.astype(BF16)
        n_ref[...] = jnp.zeros(n_ref.shape, F32) + walked.astype(F32)
        if rider is not None:
            rider.wait_at_last(ids, (nq,), ride)

    wide = pairs * LANES
    in_specs = [pl.BlockSpec((tq, wide), lambda i: (i, 0)),
                pl.BlockSpec((T, wide), lambda i: (0, 1), pipeline_mode=pl.Buffered(1)),
                pl.BlockSpec((T, wide), lambda i: (0, 2), pipeline_mode=pl.Buffered(1)),
                pl.BlockSpec(after_m.shape, lambda i: (0, 0), pipeline_mode=pl.Buffered(1)),
                pl.BlockSpec(causal_m.shape, lambda i: (0, 0, 0), pipeline_mode=pl.Buffered(1))]
    out_specs = [pl.BlockSpec((tq, wide), lambda i: (i, 0)),
                 pl.BlockSpec((2 * pairs, nkb, 1, tq), lambda i: (0, 0, 0, i)),
                 pl.BlockSpec((1, 8, LANES), lambda i: (i, 0, 0))]
    out_shape = [jax.ShapeDtypeStruct((T, SB_WIDTH), BF16), jax.ShapeDtypeStruct((2 * pairs, nkb, 1, T), F32),
                 jax.ShapeDtypeStruct((nq, 8, LANES), F32)]
    args = (h_a, h_a, h_a, after_m, causal_m)
    slots = pairs * (per_q + SB_GROUP)
    scratch = [pltpu.VMEM((slots, 2 * LANES, tq), F32), pltpu.VMEM((slots, 2 * LANES, tq), F32),
               pltpu.VMEM((slots, 4 * LANES, tq), BF16), pltpu.VMEM((slots, 2 * LANES, tq), BF16)]
    if rider is not None:
        in_specs, args = in_specs + rider.specs, args + tuple(rider.bufs)
        out_specs, out_shape = out_specs + rider.specs, out_shape + rider.out_shape
        scratch = scratch + rider.scratch
    outs = pl.pallas_call(
        body,
        name="sb_fwd",
        grid=(nq,),
        in_specs=in_specs,
        out_specs=out_specs,
        out_shape=out_shape,
        scratch_shapes=scratch,
        compiler_params=_params(("arbitrary",)),
    )(*args)
    return outs[0], (outs[1], outs[2]), list(outs[3:])


def _sb_bwd(h_a, d_out, saved, rider=None):
    r_mat, walked_blocks = saved
    T = h_a.shape[0]
    tq = _pick(T, (SB_Q_BLOCK, SB_KEY_BLOCK))
    nq, per_q, nkb = T // tq, tq // SB_KEY_BLOCK, T // SB_KEY_BLOCK
    n_strips = 2 * LANES // SB_STRIP
    after_m, before_m = _sb_scan_matrices()
    causal_m = _sb_causal_masks(tq)
    pairs = SB_BWD_PAIRS
    groups = 4 // pairs
    n_ride = rider.n if rider is not None else 0

    def body(*refs):
        q_ref, k_ref, v_ref, do_ref, r_ref, n_ref, after_ref, before_ref, causal_ref = refs[:9]
        dq_ref, dk_ref, dv_ref = refs[9 + n_ride:12 + n_ride]
        z_ref, lb_ref, split_ref, w_ref, da_ref, dz_ref = refs[12 + 2 * n_ride:18 + 2 * n_ride]
        ids = [pl.program_id(0), pl.program_id(1)]
        if rider is not None:
            ride = (refs[9:9 + n_ride], refs[12 + n_ride:12 + 2 * n_ride], refs[-3:])
            rider.start_at_first(ids, ride)
        i = ids[1]

        @pl.when(i == 0)
        def _():
            dk_ref[...] = jnp.zeros_like(dk_ref)
            dv_ref[...] = jnp.zeros_like(dv_ref)

        scale = SB_HEAD_DIM ** -0.5
        q = [q_ref[:, _pair_lanes(p)] for p in range(pairs)]
        d_o = [do_ref[:, _pair_lanes(p)] for p in range(pairs)]
        q_t = [(x.astype(F32).T * scale).astype(BF16) for x in q]
        do_t = [x.astype(F32).T.astype(BF16) for x in d_o]
        lane_is_a = lax.broadcasted_iota(jnp.int32, (SB_KEY_BLOCK, LANES), 1) < SB_HEAD_DIM

        def tiles(kbs, diags, carry):
            nb = len(kbs)
            lanes = [_sb_lanes(tq, d) for d in diags]
            cols = [slice(first, first + width) for first, width in lanes]
            dq_t, ca, cb = [list(c) for c in carry]
            ks = [pl.multiple_of(kb * SB_KEY_BLOCK, SB_KEY_BLOCK) for kb in kbs]
            slot = lambda p, b: p * nb + b

            def causal(b, s):
                return causal_ref[diags[b], pl.ds((s * SB_STRIP) % SB_KEY_BLOCK, SB_STRIP), cols[b]]

            kk, vv = {}, {}
            for b in range(nb):
                for p in range(pairs):
                    kk[p, b] = _pair_rows(k_ref[pl.ds(ks[b], SB_KEY_BLOCK), _pair_lanes(p)], lane_is_a)
                    vv[p, b] = _pair_rows(v_ref[pl.ds(ks[b], SB_KEY_BLOCK), _pair_lanes(p)], lane_is_a)
                    z_ref[slot(p, b), :, cols[b]] = _dot(kk[p, b], q_t[p][:, cols[b]], _NN)
            for b in range(nb):
                for p in range(pairs):
                    for s in range(n_strips):
                        rows = pl.ds(s * SB_STRIP, SB_STRIP)
                        spent, log_beta = _sb_log_terms(z_ref[slot(p, b), rows, cols[b]])
                        lb_ref[slot(p, b), rows, cols[b]] = log_beta
                        if isinstance(diags[b], int):
                            spent = spent * causal(b, s)
                        _sb_store_split(split_ref.at[slot(p, b)], s, spent, cols[b])
            for b in range(nb):
                for p in range(pairs):
                    z_ref[slot(p, b), :, cols[b]] = _dot(after_ref[...], split_ref[slot(p, b), :, cols[b]], _NN)
                    da_ref[slot(p, b), :, cols[b]] = _dot(vv[p, b], do_t[p][:, cols[b]], _NN)
            sums = {}
            for b in range(nb):
                for p in range(pairs):
                    part = [jnp.zeros((8, lanes[b][1]), F32), jnp.zeros((8, lanes[b][1]), F32)]
                    for s in range(n_strips):
                        rows = pl.ds(s * SB_STRIP, SB_STRIP)
                        start = r_ref[2 * p + (s * SB_STRIP) // SB_KEY_BLOCK, kbs[b]][:, cols[b]]
                        w = jnp.exp(lb_ref[slot(p, b), rows, cols[b]] + z_ref[slot(p, b), rows, cols[b]] + start)
                        if isinstance(diags[b], int):
                            w = w * causal(b, s)
                        w_ref[slot(p, b), rows, cols[b]] = w.astype(BF16)
                        da = da_ref[slot(p, b), rows, cols[b]] * w
                        da_ref[slot(p, b), rows, cols[b]] = da
                        _sb_store_split(split_ref.at[slot(p, b)], s, da, cols[b])
                        head = (s * SB_STRIP) // SB_KEY_BLOCK
                        part[head] = part[head] + jnp.sum(da.reshape(SB_STRIP // 8, 8, lanes[b][1]), axis=0)
                    sums[p, b] = part
            for b in range(nb):
                for p in range(pairs):
                    z_ref[slot(p, b), :, cols[b]] = _dot(before_ref[...], split_ref[slot(p, b), :, cols[b]], _NN)
            for b in range(nb):
                for p in range(pairs):
                    for s in range(n_strips):
                        rows = pl.ds(s * SB_STRIP, SB_STRIP)
                        base = (ca[p] if (s * SB_STRIP) < SB_KEY_BLOCK else cb[p])[:, cols[b]]
                        sig = jnp.exp(lb_ref[slot(p, b), rows, cols[b]])
                        dz = (da_ref[slot(p, b), rows, cols[b]] * (1.0 - sig)
                              - (z_ref[slot(p, b), rows, cols[b]] + base) * sig)
                        if isinstance(diags[b], int):
                            dz = dz * causal(b, s)
                        dz_ref[slot(p, b), rows, cols[b]] = (dz * scale).astype(BF16)
                    ca[p] = _lane_add(ca[p], jnp.sum(sums[p, b][0], axis=0, keepdims=True), lanes[b])
                    cb[p] = _lane_add(cb[p], jnp.sum(sums[p, b][1], axis=0, keepdims=True), lanes[b])
            for b in range(nb):
                for p in range(pairs):
                    dq_t[p] = _lane_add(dq_t[p], _dot(kk[p, b], dz_ref[slot(p, b), :, cols[b]], _TN), lanes[b])
                    dkk = _dot(dz_ref[slot(p, b), :, cols[b]], q[p][cols[b], :], _NN)
                    dvv = _dot(w_ref[slot(p, b), :, cols[b]], d_o[p][cols[b], :], _NN)
                    here = (pl.ds(ks[b], SB_KEY_BLOCK), _pair_lanes(p))
                    dk_ref[here] += jnp.where(lane_is_a, dkk[:SB_KEY_BLOCK], dkk[SB_KEY_BLOCK:])
                    dv_ref[here] += jnp.where(lane_is_a, dvv[:SB_KEY_BLOCK], dvv[SB_KEY_BLOCK:])
            return tuple(dq_t), tuple(ca), tuple(cb)

        n_full = i * per_q
        groups_walked = jnp.clip(jnp.max(n_ref[...]).astype(jnp.int32), 0, n_full) // SB_GROUP
        carry = (tuple(jnp.zeros((LANES, tq), F32) for _ in range(pairs)),
                 tuple(jnp.zeros((1, tq), F32) for _ in range(pairs)),
                 tuple(jnp.zeros((1, tq), F32) for _ in range(pairs)))

        def below(j, c):
            kbs = [n_full - (groups_walked - j) * SB_GROUP + b for b in range(SB_GROUP)]
            starts = [r_ref[h, kbs[-1]][:, tq // 2:] for h in range(2 * pairs)]
            reaches = jnp.max(functools.reduce(jnp.maximum, starts)) > SB_DEAD_LOG
            return lax.cond(reaches, lambda cc: tiles(kbs, [None] * SB_GROUP, cc),
                            lambda cc: tiles(kbs, ["left"] * SB_GROUP, cc), c)

        carry = lax.fori_loop(0, groups_walked, below, carry)
        own = list(range(per_q))
        carry = tiles([i * per_q + d for d in own], own, carry)
        for p in range(pairs):
            dq_ref[:, _pair_lanes(p)] = carry[0][p].T.astype(BF16)
        if rider is not None:
            rider.wait_at_last(ids, (groups, nq), ride)

    wide = pairs * LANES
    mat = pl.BlockSpec(after_m.shape, lambda g, i: (0, 0), pipeline_mode=pl.Buffered(1))
    in_specs = [pl.BlockSpec((tq, wide), lambda g, i: (i, g)),
                pl.BlockSpec((T, wide), lambda g, i: (0, groups + g), pipeline_mode=pl.Buffered(1)),
                pl.BlockSpec((T, wide), lambda g, i: (0, 2 * groups + g), pipeline_mode=pl.Buffered(1)),
                pl.BlockSpec((tq, wide), lambda g, i: (i, g)),
                pl.BlockSpec((2 * pairs, nkb, 1, tq), lambda g, i: (g, 0, 0, i)),
                pl.BlockSpec((1, 8, LANES), lambda g, i: (i, 0, 0)),
                mat, mat,
                pl.BlockSpec(causal_m.shape, lambda g, i: (0, 0, 0), pipeline_mode=pl.Buffered(1))]
    out_specs = [pl.BlockSpec((tq, wide), lambda g, i: (i, g)),
                 pl.BlockSpec((T, wide), lambda g, i: (0, g)),
                 pl.BlockSpec((T, wide), lambda g, i: (0, g))]
    out_shape = [jax.ShapeDtypeStruct((T, SB_WIDTH), BF16), jax.ShapeDtypeStruct((T, SB_WIDTH), F32),
                 jax.ShapeDtypeStruct((T, SB_WIDTH), F32)]
    args = (h_a, h_a, h_a, d_out, r_mat, walked_blocks, after_m, before_m, causal_m)
    slots = pairs * max(per_q, SB_GROUP)
    scratch = [pltpu.VMEM((slots, 2 * LANES, tq), F32), pltpu.VMEM((slots, 2 * LANES, tq), F32),
               pltpu.VMEM((slots, 4 * LANES, tq), BF16), pltpu.VMEM((slots, 2 * LANES, tq), BF16),
               pltpu.VMEM((slots, 2 * LANES, tq), F32), pltpu.VMEM((slots, 2 * LANES, tq), BF16)]
    if rider is not None:
        in_specs, args = in_specs + rider.specs, args + tuple(rider.bufs)
        out_specs, out_shape = out_specs + rider.specs, out_shape + rider.out_shape
        scratch = scratch + rider.scratch
    outs = pl.pallas_call(
        body,
        name="sb_bwd",
        grid=(groups, nq),
        in_specs=in_specs,
        out_specs=out_specs,
        out_shape=out_shape,
        scratch_shapes=scratch,
        compiler_params=_params(("arbitrary", "arbitrary") if rider is not None else ("parallel", "arbitrary")),
    )(*args)
    return outs[0], outs[1], outs[2], list(outs[3:])


def _ret_tables(T):
    half = RET_QK_DIM // 2
    inv = 1.0 / (ROPE_BASE ** (jnp.arange(half, dtype=F32) / half))
    ang = jnp.arange(T, dtype=F32)[:, None] * inv[None, :]
    cos, sin = jnp.cos(ang), jnp.sin(ang)
    cos_t = jnp.concatenate([cos, cos], axis=1)
    sin_t = jnp.concatenate([-sin, sin], axis=1)
    log_gamma = jnp.log1p(-jnp.exp2(-5.0 - jnp.arange(RET_HEADS, dtype=F32)))
    idx = jnp.arange(RET_CHUNK, dtype=F32)
    rel = idx[:, None] - idx[None, :]
    decay = jnp.where(rel[None] >= 0, jnp.exp(log_gamma[:, None, None] * jnp.maximum(rel, 0.0)[None]), 0.0)
    k_decay = jnp.exp(log_gamma[None, :] * (RET_CHUNK - 1.0 - idx)[:, None])
    q_decay = jnp.exp(log_gamma[None, :] * (idx + 1.0)[:, None])
    chunk_decay = jnp.exp(log_gamma * RET_CHUNK)
    k_dec = jnp.broadcast_to(k_decay.T[:, :, None], (RET_HEADS, RET_CHUNK, LANES))
    q_dec = jnp.broadcast_to(q_decay.T[:, :, None], (RET_HEADS, RET_CHUNK, LANES))
    c_dec = jnp.broadcast_to(chunk_decay[:, None, None], (RET_HEADS, 8, LANES))
    return cos_t, sin_t, decay, k_dec, q_dec, c_dec


def _rotary(x, cos_t, sin_t):
    return x * cos_t + pltpu.roll(x, RET_QK_DIM // 2, 1) * sin_t


def _rotary_transpose(dy, cos_t, sin_t):
    return dy * cos_t + pltpu.roll(dy * sin_t, RET_QK_DIM // 2, 1)


def _head_norm(o):
    mu = jnp.mean(o, axis=1, keepdims=True)
    cen = o - mu
    var = jnp.mean(cen * cen, axis=1, keepdims=True)
    rstd = lax.rsqrt(var + LN_EPS)
    return cen * rstd, rstd


def _ret_specs(steps, per_step, reverse):
    def n_of(n):
        return (steps - 1 - n) if reverse else n

    rows = per_step * RET_CHUNK
    q_spec = pl.BlockSpec((rows, RET_QK_WIDTH), lambda n: (n_of(n), 0))
    k_spec = pl.BlockSpec((rows, RET_QK_WIDTH), lambda n: (n_of(n), 1))
    vv = pl.BlockSpec((rows, RET_V_WIDTH), lambda n: (n_of(n), 0))
    pos = pl.BlockSpec((rows, LANES), lambda n: (n_of(n), 0))
    per_head = pl.BlockSpec((RET_HEADS, RET_CHUNK, LANES), lambda n: (0, 0, 0))
    c_dec = pl.BlockSpec((RET_HEADS, 8, LANES), lambda n: (0, 0, 0))
    state = pl.BlockSpec((RET_HEADS, per_step, RET_QK_DIM, RET_V_DIM), lambda n: (0, n_of(n), 0, 0))
    return q_spec, k_spec, vv, pos, per_head, c_dec, state


def _qk_cols(h):
    return slice(h * RET_QK_DIM, (h + 1) * RET_QK_DIM)


def _v_cols(h):
    return slice(h * RET_V_DIM, (h + 1) * RET_V_DIM)


def _ret_fwd(h_b, h_c, h_d, tables):
    T = h_b.shape[0]
    nc = T // RET_CHUNK
    per_step = _pick(nc, (RET_STEP_CHUNKS, 1))
    steps = nc // per_step
    q_spec, k_spec, vv, pos, per_head, c_dec, state = _ret_specs(steps, per_step, False)

    def body(q_ref, k_ref, v_ref, g_ref, cos_ref, sin_ref, dec_ref, kd_ref, qd_ref, cd_ref,
             y_ref, o_ref, st_ref, state_ref):
        @pl.when(pl.program_id(0) == 0)
        def _():
            state_ref[...] = jnp.zeros_like(state_ref)

        for c in range(per_step):
            rows = pl.ds(c * RET_CHUNK, RET_CHUNK)
            cos_t, sin_t = cos_ref[rows, :], sin_ref[rows, :]
            for h in range(RET_HEADS):
                q = _rotary(q_ref[rows, _qk_cols(h)], cos_t, sin_t) * (RET_QK_DIM ** -0.5)
                k = _rotary(k_ref[rows, _qk_cols(h)], cos_t, sin_t)
                v = v_ref[rows, _v_cols(h)]
                prev = state_ref[h]
                scores = _dot(q.astype(BF16), k.astype(BF16), _NT) * dec_ref[h]
                inner = _dot(scores.astype(BF16), v, _NN)
                cross = _dot((q * qd_ref[h]).astype(BF16), prev.astype(BF16), _NN)
                o = inner + cross
                st_ref[h, c] = prev
                kv = _dot((k * kd_ref[h]).astype(BF16), v, _TN)
                state_ref[h] = prev * cd_ref[h, 0:1, 0:1] + kv
                o_ref[rows, _v_cols(h)] = o
                normed, _ = _head_norm(o)
                gate = g_ref[rows, _v_cols(h)]
                y_ref[rows, _v_cols(h)] = (gate * jax.nn.sigmoid(gate) * normed).astype(BF16)

    return pl.pallas_call(
        body,
        name="ret_fwd",
        grid=(steps,),
        in_specs=[q_spec, k_spec, vv, vv, pos, pos, per_head, per_head, per_head, c_dec],
        out_specs=[vv, vv, state],
        out_shape=[jax.ShapeDtypeStruct((T, RET_V_WIDTH), BF16),
                   jax.ShapeDtypeStruct((T, RET_V_WIDTH), F32),
                   jax.ShapeDtypeStruct((RET_HEADS, nc, RET_QK_DIM, RET_V_DIM), F32)],
        scratch_shapes=[pltpu.VMEM((RET_HEADS, RET_QK_DIM, RET_V_DIM), F32)],
        compiler_params=_params(("arbitrary",)),
    )(h_b, h_b, h_c, h_d, *tables)


def _ret_bwd(d_y, o_pre, states, h_b, h_c, h_d, tables, rider=None):
    T = h_b.shape[0]
    nc = T // RET_CHUNK
    per_step = _pick(nc, (RET_STEP_CHUNKS, 1))
    steps = nc // per_step
    q_spec, k_spec, vv, pos, per_head, c_dec, state = _ret_specs(steps, per_step, True)
    n_ride = rider.n if rider is not None else 0

    def body(*refs):
        (dy_ref, o_ref, st_ref, q_ref, k_ref, v_ref, g_ref, cos_ref, sin_ref, dec_ref, kd_ref, qd_ref,
         cd_ref) = refs[:13]
        dq_ref, dk_ref, dv_ref, dg_ref = refs[13 + n_ride:17 + n_ride]
        carry_ref = refs[17 + 2 * n_ride]
        ids = [pl.program_id(0)]
        if rider is not None:
            ride = (refs[13:13 + n_ride], refs[17 + n_ride:17 + 2 * n_ride], refs[-3:])
            rider.start_at_first(ids, ride)

        @pl.when(ids[0] == 0)
        def _():
            carry_ref[...] = jnp.zeros_like(carry_ref)

        scale = RET_QK_DIM ** -0.5
        for c in reversed(range(per_step)):
            rows = pl.ds(c * RET_CHUNK, RET_CHUNK)
            cos_t, sin_t = cos_ref[rows, :], sin_ref[rows, :]
            for h in range(RET_HEADS):
                q = _rotary(q_ref[rows, _qk_cols(h)], cos_t, sin_t) * scale
                k = _rotary(k_ref[rows, _qk_cols(h)], cos_t, sin_t)
                v = v_ref[rows, _v_cols(h)]
                decay, k_dec, q_dec = dec_ref[h], kd_ref[h], qd_ref[h]
                chunk_decay = cd_ref[h, 0:1, 0:1]
                state = st_ref[h, c].astype(BF16)
                later = carry_ref[h]
                later_b = later.astype(BF16)

                gate = g_ref[rows, _v_cols(h)]
                sig = jax.nn.sigmoid(gate)
                silu = gate * sig
                normed, rstd = _head_norm(o_ref[rows, _v_cols(h)])
                d_y = dy_ref[rows, _v_cols(h)]
                dg_ref[rows, _v_cols(h)] = (d_y * normed * (sig * (1.0 + gate * (1.0 - sig)))).astype(BF16)
                d_n = d_y * silu
                d_o = rstd * (d_n - jnp.mean(d_n, axis=1, keepdims=True)
                              - normed * jnp.mean(d_n * normed, axis=1, keepdims=True))
                d_ob = d_o.astype(BF16)

                qb, kb = q.astype(BF16), k.astype(BF16)
                qd_b, kd_b = (q * q_dec).astype(BF16), (k * k_dec).astype(BF16)
                scores = _dot(qb, kb, _NT) * decay
                d_scores = (_dot(d_ob, v, _NT) * decay).astype(BF16)
                dq = _dot(d_scores, kb, _NN) + _dot(d_ob, state, _NT) * q_dec
                dk = _dot(d_scores, qb, _TN) + _dot(v, later_b, _NT) * k_dec
                dv = _dot(scores.astype(BF16), d_ob, _TN) + _dot(kd_b, later_b, _NN)
                carry_ref[h] = _dot(qd_b, d_ob, _TN) + chunk_decay * later
                dq_ref[rows, _qk_cols(h)] = _rotary_transpose(dq * scale, cos_t, sin_t).astype(BF16)
                dk_ref[rows, _qk_cols(h)] = _rotary_transpose(dk, cos_t, sin_t).astype(BF16)
                dv_ref[rows, _v_cols(h)] = dv.astype(BF16)
        if rider is not None:
            rider.wait_at_last(ids, (steps,), ride)

    qk_out = pl.BlockSpec((per_step * RET_CHUNK, RET_QK_WIDTH), lambda n: (steps - 1 - n, 0))
    in_specs = [vv, vv, state, q_spec, k_spec, vv, vv, pos, pos, per_head, per_head, per_head, c_dec]
    out_specs = [qk_out, qk_out, vv, vv]
    out_shape = [jax.ShapeDtypeStruct((T, RET_QK_WIDTH), BF16), jax.ShapeDtypeStruct((T, RET_QK_WIDTH), BF16),
                 jax.ShapeDtypeStruct((T, RET_V_WIDTH), BF16), jax.ShapeDtypeStruct((T, RET_V_WIDTH), BF16)]
    args = (d_y, o_pre, states, h_b, h_b, h_c, h_d) + tuple(tables)
    scratch = [pltpu.VMEM((RET_HEADS, RET_QK_DIM, RET_V_DIM), F32)]
    if rider is not None:
        in_specs, args = in_specs + rider.specs, args + tuple(rider.bufs)
        out_specs, out_shape = out_specs + rider.specs, out_shape + rider.out_shape
        scratch = scratch + rider.scratch
    outs = pl.pallas_call(
        body,
        name="ret_bwd",
        grid=(steps,),
        in_specs=in_specs,
        out_specs=out_specs,
        out_shape=out_shape,
        scratch_shapes=scratch,
        compiler_params=_params(("arbitrary",)),
    )(*args)
    return outs[0], outs[1], outs[2], outs[3], list(outs[4:])


def _proj_tiles(h, x):
    return h[:, 0:1536], h[:, 1536:2560], h[:, 2560:3584], h[:, 3584:4608], h[:, 4608:6656], x


def _gate_mix_tiles(y_ret, h_e, b_gate, y_sb):
    gates = jax.nn.sigmoid(h_e + b_gate)
    return y_ret, gates[:, :D_MODEL] * y_sb + gates[:, D_MODEL:] * y_ret


def _gate_mix_grad_tiles(d_mix, h_e, b_gate, y_sb, y_ret):
    gates = jax.nn.sigmoid(h_e + b_gate)
    g0, g1 = gates[:, :D_MODEL], gates[:, D_MODEL:]
    d_e = jnp.concatenate([d_mix * y_sb * g0 * (1.0 - g0), d_mix * y_ret * g1 * (1.0 - g1)], axis=1)
    return d_mix * g0, d_mix * g1, d_e, d_e


def _ln_stats(u):
    mu = jnp.mean(u, axis=1, keepdims=True)
    cen = u - mu
    var = jnp.mean(cen * cen, axis=1, keepdims=True)
    rstd = lax.rsqrt(var + LN_EPS)
    return cen * rstd, rstd


def _ln_input_grad(d_out, gain, xhat, rstd):
    d_hat = d_out * gain
    return rstd * (d_hat - jnp.mean(d_hat, axis=1, keepdims=True)
                   - xhat * jnp.mean(d_hat * xhat, axis=1, keepdims=True))


def _ln_tiles(sub, x_prev, gain, bias):
    xhat, rstd = _ln_stats(DN_ALPHA * x_prev + sub)
    out = xhat * gain + bias
    return out, out, xhat, rstd


def _residual_tiles(d_sub, res):
    return (d_sub + DN_ALPHA * res,)


def _ln_grad_tiles(d_sub, res, xhat, rstd, gain):
    d_out = d_sub + DN_ALPHA * res
    du = _ln_input_grad(d_out, gain, xhat, rstd)
    return du, du, d_out * xhat, d_out


def _ln_loss_tiles(sub, x_prev, gain, bias, target):
    xhat, rstd = _ln_stats(DN_ALPHA * x_prev + sub)
    diff = xhat * gain + bias - target
    d_out = diff * (1.0 / D_MODEL)
    du = _ln_input_grad(d_out, gain, xhat, rstd)
    return du, du, diff * diff, d_out * xhat, d_out


def _mem_probs(q_h, k_h):
    s = _dot(q_h, k_h, _NT) * (MEM_HEAD_DIM ** -0.5)
    e = jnp.exp(s - jnp.max(s, axis=1, keepdims=True))
    return e / jnp.sum(e, axis=1, keepdims=True)


def _xattn_fwd(q, kv):
    T, mem_len = q.shape[0], kv.shape[0]
    tq = _pick(T, (512, 256, 128))

    def body(q_ref, kv_ref, o_ref):
        for h in range(MEM_HEADS):
            cols = slice(h * MEM_HEAD_DIM, (h + 1) * MEM_HEAD_DIM)
            vcols = slice(D_MODEL + h * MEM_HEAD_DIM, D_MODEL + (h + 1) * MEM_HEAD_DIM)
            p = _mem_probs(q_ref[:, cols], kv_ref[:, cols])
            o_ref[:, cols] = _dot(p.astype(BF16), kv_ref[:, vcols], _NN).astype(BF16)

    return pl.pallas_call(
        body,
        name="xattn_fwd",
        grid=(T // tq,),
        in_specs=[pl.BlockSpec((tq, D_MODEL), lambda i: (i, 0)),
                  pl.BlockSpec((mem_len, 2 * D_MODEL), lambda i: (0, 0))],
        out_specs=pl.BlockSpec((tq, D_MODEL), lambda i: (i, 0)),
        out_shape=jax.ShapeDtypeStruct((T, D_MODEL), BF16),
        compiler_params=_params(("parallel",)),
    )(q, kv)


def _xattn_bwd(q, kv, d_o):
    T, mem_len = q.shape[0], kv.shape[0]
    tq = _pick(T, (512, 256, 128))

    def body(q_ref, kv_ref, do_ref, dq_ref, dkv_ref):
        @pl.when(pl.program_id(0) == 0)
        def _():
            dkv_ref[...] = jnp.zeros_like(dkv_ref)

        for h in range(MEM_HEADS):
            cols = slice(h * MEM_HEAD_DIM, (h + 1) * MEM_HEAD_DIM)
            vcols = slice(D_MODEL + h * MEM_HEAD_DIM, D_MODEL + (h + 1) * MEM_HEAD_DIM)
            q_h, k_h, do_h = q_ref[:, cols], kv_ref[:, cols], do_ref[:, cols]
            p = _mem_probs(q_h, k_h)
            dp = _dot(do_h, kv_ref[:, vcols], _NT)
            ds = p * (dp - jnp.sum(dp * p, axis=1, keepdims=True))
            dsb = (ds * (MEM_HEAD_DIM ** -0.5)).astype(BF16)
            dq_ref[:, cols] = _dot(dsb, k_h, _NN).astype(BF16)
            dkv_ref[:, cols] += _dot(dsb, q_h, _TN)
            dkv_ref[:, vcols] += _dot(p.astype(BF16), do_h, _TN)

    row = pl.BlockSpec((tq, D_MODEL), lambda i: (i, 0))
    full = pl.BlockSpec((mem_len, 2 * D_MODEL), lambda i: (0, 0))
    return pl.pallas_call(
        body,
        name="xattn_bwd",
        grid=(T // tq,),
        in_specs=[row, full, row],
        out_specs=[row, full],
        out_shape=[jax.ShapeDtypeStruct((T, D_MODEL), BF16), jax.ShapeDtypeStruct((mem_len, 2 * D_MODEL), F32)],
        compiler_params=_params(("arbitrary",)),
    )(q, kv, d_o)


def _swiglu_tiles(f):
    a, b = f[:, :FFN_HIDDEN], f[:, FFN_HIDDEN:]
    return f, a * jax.nn.sigmoid(a) * b


def _swiglu_grad_tiles(d_hidden, f):
    f = f.astype(F32)
    a, b = f[:, :FFN_HIDDEN], f[:, FFN_HIDDEN:]
    sig = jax.nn.sigmoid(a)
    return (jnp.concatenate([d_hidden * b * (sig * (1.0 + a * (1.0 - sig))), d_hidden * (a * sig)], axis=1),)


def _local_step(x, mem, w_in, small, target, fetch, ship):
    T = x.shape[0]
    tables = _ret_tables(T)
    memb = mem.astype(BF16)

    (h_a, h_b, h_c, h_d, h_e, xb), w_ffn = fetch(
        ("w_ffn_in", "w_ffn_out"),
        lambda rider: _as_host(rider, _mm_fused(
            x, w_in, mode="nn", name="proj_in", extras=[], pass_a=True,
            outs=[(1536, BF16), (1024, F32), (1024, BF16), (1024, F32), (2048, F32), (D_MODEL, BF16)],
            epilogue=_proj_tiles, max_rows=256, rider=rider)))
    (a_sb, r_mat), w_mix = fetch(("w_sb_o", "w_ret_o", "w_mix_o", "w_mem_q", "w_mem_kv", "w_mem_o"),
                                 lambda rider: _sb_fwd(h_a, rider))
    w = {**w_ffn, **w_mix}
    y_gated, o_pre, states = _ret_fwd(h_b, h_c, h_d, tables)
    y_sb = _mm(a_sb, w["w_sb_o"], mode="nn", out_dtype=F32, name="sb_out")
    row_f32, row_bf16 = (D_MODEL, F32), (D_MODEL, BF16)
    ln_outs = [row_f32, row_bf16, row_f32, (1, F32)]
    y_ret, mix_in = _mm_fused(y_gated, w["w_ret_o"], mode="nn", name="ret_out", extras=[h_e, small["b_gate"], y_sb],
                              outs=[row_f32, row_bf16], epilogue=_gate_mix_tiles)
    x1, x1b, xhat1, rstd1 = _mm_fused(mix_in, w["w_mix_o"], mode="nn", name="mix_out",
                                      extras=[x, small["ln1_g"], small["ln1_b"]], outs=ln_outs, epilogue=_ln_tiles)
    q_m = _mm(x1b, w["w_mem_q"], mode="nn", out_dtype=BF16, name="mem_q")
    kv_m = _mm(memb, w["w_mem_kv"], mode="nn", out_dtype=BF16, name="mem_kv")
    o_m = _xattn_fwd(q_m, kv_m)
    x2, x2b, xhat2, rstd2 = _mm_fused(o_m, w["w_mem_o"], mode="nn", name="mem_out",
                                      extras=[x1, small["ln2_g"], small["ln2_b"]], outs=ln_outs, epilogue=_ln_tiles)
    f, hidden = _mm_fused(x2b, w["w_ffn_in"], mode="nn", name="ffn_in", extras=[],
                          outs=[(2 * FFN_HIDDEN, BF16), (FFN_HIDDEN, BF16)], epilogue=_swiglu_tiles)
    du_outs, col = [row_f32, row_bf16], D_MODEL
    du3, du3b, loss_cols, d_ln3_g, d_ln3_b = _mm_fused(
        hidden, w["w_ffn_out"], mode="nn", name="ffn_out", extras=[x2, small["ln3_g"], small["ln3_b"], target],
        outs=du_outs, sums=[col, col, col], epilogue=_ln_loss_tiles)

    g_ffn_out = _mm(hidden, du3b, mode="tn", out_dtype=BF16, name="g_ffn_out")
    (d_f,) = _mm_fused(du3b, w["w_ffn_out"], mode="nt", name="d_hidden", extras=[f],
                       outs=[(2 * FFN_HIDDEN, BF16)], epilogue=_swiglu_grad_tiles)
    g_ffn_in = _mm(x2b, d_f, mode="tn", out_dtype=BF16, name="g_ffn_in")
    du2, du2b, d_ln2_g, d_ln2_b = ship(
        {"w_ffn_out": g_ffn_out},
        lambda rider: _as_host(rider, _mm_fused(
            d_f, w["w_ffn_in"], mode="nt", name="d_x2", extras=[du3, xhat2, rstd2, small["ln2_g"]], outs=du_outs,
            sums=[col, col], epilogue=_ln_grad_tiles, rider=rider, max_rows=256)))
    g_mem_o = _mm(o_m, du2b, mode="tn", out_dtype=BF16, name="g_mem_o")
    d_om = _mm(du2b, w["w_mem_o"], mode="nt", out_dtype=BF16, name="d_om")
    d_qm, d_kvm = _xattn_bwd(q_m, kv_m, d_om)
    g_mem_q = _mm(x1b, d_qm, mode="tn", out_dtype=BF16, name="g_mem_q")
    g_mem_kv = _mm(memb, d_kvm.astype(BF16), mode="tn", out_dtype=BF16, name="g_mem_kv")
    du1, du1b, d_ln1_g, d_ln1_b = _mm_fused(
        d_qm, w["w_mem_q"], mode="nt", name="d_x1", extras=[du2, xhat1, rstd1, small["ln1_g"]], outs=du_outs,
        sums=[col, col], epilogue=_ln_grad_tiles)
    g_mix_o = _mm(mix_in, du1b, mode="tn", out_dtype=BF16, name="g_mix_o")
    d_ysb, d_yret, d_e, d_b_gate = _mm_fused(
        du1b, w["w_mix_o"], mode="nt", name="d_mix_in", extras=[h_e, small["b_gate"], y_sb, y_ret],
        outs=[row_bf16, row_bf16, (2 * D_MODEL, BF16)], sums=[2 * D_MODEL], epilogue=_gate_mix_grad_tiles)
    g_sb_o = _mm(a_sb, d_ysb, mode="tn", out_dtype=BF16, name="g_sb_o")
    g_ret_o = _mm(y_gated, d_yret, mode="tn", out_dtype=BF16, name="g_ret_o")
    d_asb = _mm(d_ysb, w["w_sb_o"], mode="nt", out_dtype=BF16, name="d_asb")
    d_ygated = _mm(d_yret, w["w_ret_o"], mode="nt", out_dtype=F32, name="d_ygated")
    small_grads = {"b_gate": d_b_gate, "ln1_g": d_ln1_g, "ln1_b": d_ln1_b, "ln2_g": d_ln2_g, "ln2_b": d_ln2_b,
                   "ln3_g": d_ln3_g, "ln3_b": d_ln3_b, "loss_cols": loss_cols}
    d_rq, d_rk, d_c, d_d = ship({"w_mem_kv": g_mem_kv, "w_mem_q": g_mem_q, "w_mem_o": g_mem_o, "w_mix_o": g_mix_o},
                                lambda rider: _ret_bwd(d_ygated, o_pre, states, h_b, h_c, h_d, tables, rider))
    d_q, d_k, d_v = ship({"w_ffn_in": g_ffn_in, "w_ret_o": g_ret_o, "w_sb_o": g_sb_o, "small": small_grads},
                         lambda rider: _sb_bwd(h_a, d_asb, r_mat, rider))
    d_h = [("sb_q", d_q), ("sb_k", d_k), ("sb_v", d_v), ("ret_q", d_rq), ("ret_k", d_rk), ("ret_v", d_c),
           ("ret_g", d_d), ("gate", d_e)]
    g_in = jnp.concatenate([_mm(xb, piece, mode="tn", out_dtype=BF16, name="g_in_" + tag) for tag, piece in d_h],
                           axis=1)
    (d_x,) = ship({"w_in": g_in},
                  lambda rider: _as_host(rider, _mm_fused(
                      [piece for _, piece in d_h], w_in, mode="nt", name="d_x", extras=[du1], outs=[(D_MODEL, F32)],
                      epilogue=_residual_tiles, rider=rider, max_rows=256)))
    return d_x


def _adamw_math(w, g, m, v):
    m = ADAM_B1 * m + (1.0 - ADAM_B1) * g
    v = ADAM_B2 * v + (1.0 - ADAM_B2) * jnp.square(g)
    m_hat = m / (1.0 - ADAM_B1 ** ADAM_STEP)
    v_hat = v / (1.0 - ADAM_B2 ** ADAM_STEP)
    delta = -ADAM_LR * (m_hat / (jnp.sqrt(v_hat) + ADAM_EPS) + ADAM_WD * w)
    return delta, m, v


def _adamw(parts, w, m, v, name):
    R, C = w.shape
    tr = max(t for t in range(16, min(R, 256) + 1, 16) if R % t == 0) if R >= 16 else R

    def body(p_ref, w_ref, m_ref, v_ref, g_ref, d_ref, nm_ref, nv_ref):
        g = p_ref[0].astype(F32)
        for j in range(1, N_DEV):
            g = g + p_ref[j].astype(F32)
        delta, nm, nv = _adamw_math(w_ref[...], g, m_ref[...], v_ref[...])
        g_ref[...] = g
        d_ref[...] = delta
        nm_ref[...] = nm
        nv_ref[...] = nv

    blk = pl.BlockSpec((tr, C), lambda i: (i, 0))
    out = jax.ShapeDtypeStruct((R, C), F32)
    return pl.pallas_call(
        body,
        name=name,
        grid=(R // tr,),
        in_specs=[pl.BlockSpec((N_DEV, tr, C), lambda i: (0, i, 0)), blk, blk, blk],
        out_specs=[blk] * 4,
        out_shape=[out] * 4,
        compiler_params=_params(("parallel",)),
    )(parts, w, m, v)


_SHARD_AXIS = {"w_in": 1, "w_sb_o": 1, "w_ret_o": 0, "w_mix_o": 0, "w_mem_q": 0, "w_mem_kv": 1, "w_mem_o": 0,
               "w_ffn_in": 1, "w_ffn_out": 0}
_MATRICES = tuple(_SHARD_AXIS)
_SMALL = ("b_gate", "ln1_g", "ln1_b", "ln2_g", "ln2_b", "ln3_g", "ln3_b")
_WEIGHT_ORDER = ("w_in", "b_gate", "w_sb_o", "w_ret_o", "w_mix_o", "ln1_g", "ln1_b", "w_mem_q", "w_mem_kv", "w_mem_o",
                 "ln2_g", "ln2_b", "w_ffn_in", "w_ffn_out", "ln3_g", "ln3_b")


def _assemble(name, gathered):
    if _SHARD_AXIS[name] == 0:
        return gathered.reshape(-1, gathered.shape[2])
    return jnp.transpose(gathered, (1, 0, 2)).reshape(gathered.shape[1], -1)


def _to_slots(name, full):
    if _SHARD_AXIS[name] == 0:
        return full.reshape(N_DEV, full.shape[0] // N_DEV, full.shape[1])
    return jnp.transpose(full.reshape(full.shape[0], N_DEV, full.shape[1] // N_DEV), (1, 0, 2))


SMALL_ROWS = 16


def _pack_small(vals):
    return jnp.concatenate([vals["b_gate"].reshape(2, D_MODEL)] + [vals[n] for n in _SMALL[1:]], axis=0)


def _unpack_small(packed):
    out = {"b_gate": packed[0:2].reshape(1, 2 * D_MODEL)}
    for i, n in enumerate(_SMALL[1:]):
        out[n] = packed[2 + i:3 + i]
    return out


def kernel(x, mem, w_in, b_gate, w_sb_o, w_ret_o, w_mix_o, ln1_g, ln1_b, w_mem_q, w_mem_kv, w_mem_o, ln2_g, ln2_b, w_ffn_in, w_ffn_out, ln3_g, ln3_b, loss_target, m_w_in, m_b_gate, m_w_sb_o, m_w_ret_o, m_w_mix_o, m_ln1_g, m_ln1_b, m_w_mem_q, m_w_mem_kv, m_w_mem_o, m_ln2_g, m_ln2_b, m_w_ffn_in, m_w_ffn_out, m_ln3_g, m_ln3_b, v_w_in, v_b_gate, v_w_sb_o, v_w_ret_o, v_w_mix_o, v_ln1_g, v_ln1_b, v_w_mem_q, v_w_mem_kv, v_w_mem_o, v_ln2_g, v_ln2_b, v_w_ffn_in, v_w_ffn_out, v_ln3_g, v_ln3_b):
    weights = dict(w_in=w_in, b_gate=b_gate, w_sb_o=w_sb_o, w_ret_o=w_ret_o, w_mix_o=w_mix_o, ln1_g=ln1_g, ln1_b=ln1_b,
                   w_mem_q=w_mem_q, w_mem_kv=w_mem_kv, w_mem_o=w_mem_o, ln2_g=ln2_g, ln2_b=ln2_b, w_ffn_in=w_ffn_in,
                   w_ffn_out=w_ffn_out, ln3_g=ln3_g, ln3_b=ln3_b)
    mom1 = dict(w_in=m_w_in, b_gate=m_b_gate, w_sb_o=m_w_sb_o, w_ret_o=m_w_ret_o, w_mix_o=m_w_mix_o, ln1_g=m_ln1_g,
                ln1_b=m_ln1_b, w_mem_q=m_w_mem_q, w_mem_kv=m_w_mem_kv, w_mem_o=m_w_mem_o, ln2_g=m_ln2_g, ln2_b=m_ln2_b,
                w_ffn_in=m_w_ffn_in, w_ffn_out=m_w_ffn_out, ln3_g=m_ln3_g, ln3_b=m_ln3_b)
    mom2 = dict(w_in=v_w_in, b_gate=v_b_gate, w_sb_o=v_w_sb_o, w_ret_o=v_w_ret_o, w_mix_o=v_w_mix_o, ln1_g=v_ln1_g,
                ln1_b=v_ln1_b, w_mem_q=v_w_mem_q, w_mem_kv=v_w_mem_kv, w_mem_o=v_w_mem_o, ln2_g=v_ln2_g, ln2_b=v_ln2_b,
                w_ffn_in=v_w_ffn_in, w_ffn_out=v_w_ffn_out, ln3_g=v_ln3_g, ln3_b=v_ln3_b)

    (gathered_in,) = _exchange([weights["w_in"][0].astype(BF16)], False, "gather_w_in")
    received = {}

    def fetch(names, host):
        res = host(_Rider([weights[n][0].astype(BF16) for n in names], False))
        return res[:-1], {n: _assemble(n, g) for n, g in zip(names, res[-1])}

    def ship(grads, host):
        names = list(grads)
        bufs = []
        for n in names:
            if n == "small":
                part = jnp.concatenate([_pack_small(grads[n]), grads[n]["loss_cols"],
                                        jnp.zeros((SMALL_ROWS - 9, D_MODEL), F32)], axis=0)
                bufs.append(jnp.broadcast_to(part[None], (N_DEV,) + part.shape))
            else:
                bufs.append(_to_slots(n, grads[n]).astype(BF16))
        res = host(_Rider(bufs, True))
        received.update(zip(names, res[-1]))
        return res[:-1]

    small = {n: weights[n] for n in _SMALL}
    d_x = _local_step(x[0], mem[0], _assemble("w_in", gathered_in), small, loss_target[0], fetch, ship)

    new = {}
    for n in _MATRICES:
        new[n] = _adamw(received[n], weights[n][0], mom1[n][0], mom2[n][0], "adamw_" + n)
    packed = _adamw(received["small"][:, :8], _pack_small({n: weights[n] for n in _SMALL}),
                    _pack_small({n: mom1[n] for n in _SMALL}), _pack_small({n: mom2[n] for n in _SMALL}), "adamw_small")
    small_new = [_unpack_small(p) for p in packed]
    loss = jnp.sum(received["small"][:, 8]) * (0.5 / D_MODEL)

    outs = [loss, d_x[None]]
    for slot in range(4):
        for n in _WEIGHT_ORDER:
            outs.append(new[n][slot][None] if n in new else small_new[slot][n])
    return tuple(outs)
```

```python
import functools
import math

import jax
import jax.numpy as jnp
from jax import lax
from jax.experimental import pallas as pl
from jax.experimental.pallas import tpu as pltpu

F32 = jnp.float32
BF16 = jnp.bfloat16

N_DEV = 8
D_MODEL = 1024
SB_HEAD_DIM = 64
SB_WIDTH = 512
RET_HEADS = 4
RET_QK_DIM = 128
RET_V_DIM = 256
RET_QK_WIDTH = 512
RET_V_WIDTH = 1024
RET_CHUNK = 128
RET_STEP_CHUNKS = 4
ROPE_BASE = 10000.0
MEM_HEADS = 4
MEM_HEAD_DIM = 256
FFN_HIDDEN = 2816
DN_ALPHA = 2.0 ** 0.25
LN_EPS = 1e-5
ADAM_LR = 0.001
ADAM_B1 = 0.9
ADAM_B2 = 0.999
ADAM_EPS = 1e-08
ADAM_WD = 0.01
ADAM_STEP = 10

VMEM_LIMIT_BYTES = 52 * 1024 * 1024
LANES = 128
SB_KEY_BLOCK = 128
SB_Q_BLOCK = 256
SB_DEAD_LOG = -105.0

MESH_AXES = ("x", "y", "c")


def _pick(dim, prefs):
    for p in prefs:
        if dim % p == 0:
            return p
    return dim


def _params(sem):
    return pltpu.CompilerParams(dimension_semantics=sem, vmem_limit_bytes=VMEM_LIMIT_BYTES)


def _dot(a, b, dims):
    return lax.dot_general(a, b, (dims, ((), ())), preferred_element_type=F32)


_NN = ((1,), (0,))
_NT = ((1,), (1,))
_TN = ((0,), (0,))


def _my_index():
    return 4 * lax.axis_index("x") + 2 * lax.axis_index("y") + lax.axis_index("c")


def _peer(k):
    x, y, c = lax.axis_index("x"), lax.axis_index("y"), lax.axis_index("c")
    bx, by, bc = (k >> 2) & 1, (k >> 1) & 1, k & 1
    px = (1 - x) if bx else x
    py = (1 - y) if by else y
    pc = (1 - c) if bc else c
    return (px, py, pc), 4 * px + 2 * py + pc


class _Rider:
    def __init__(self, bufs, scatter):
        self.bufs, self.scatter, self.n = list(bufs), scatter, len(bufs)
        self.specs = [pl.BlockSpec(memory_space=pl.ANY)] * self.n
        self.out_shape = [jax.ShapeDtypeStruct(b.shape if scatter else (N_DEV,) + b.shape, b.dtype) for b in self.bufs]
        self.scratch = [pltpu.SemaphoreType.DMA((self.n, N_DEV - 1)), pltpu.SemaphoreType.DMA((self.n, N_DEV - 1)),
                        pltpu.SemaphoreType.DMA((self.n,))]

    def _remote(self, ride, a, k, src_ref, slot, to):
        _, dst, (send_sems, recv_sems, _) = ride
        return pltpu.make_async_remote_copy(src_ref=src_ref, dst_ref=dst[a].at[slot], send_sem=send_sems.at[a, k],
                                            recv_sem=recv_sems.at[a, k], device_id=to,
                                            device_id_type=pl.DeviceIdType.MESH)

    def _local(self, ride, a):
        src, dst, (_, _, local_sems) = ride
        me = _my_index()
        return pltpu.make_async_copy(src[a].at[me] if self.scatter else src[a], dst[a].at[me], local_sems.at[a])

    def _direct(self, ride, a):
        src = ride[0]
        me = _my_index()
        out = []
        for k in range(1, N_DEV):
            peer, peer_idx = _peer(k)
            out.append(self._remote(ride, a, k - 1, src[a].at[peer_idx], me, peer))
        return out

    def _two_level(self, ride, a):
        src, dst = ride[0], ride[1]
        x, y, c = lax.axis_index("x"), lax.axis_index("y"), lax.axis_index("c")
        me, sibling = _my_index(), (x, y, 1 - c)
        chips = [(1 - x, y), (x, 1 - y), (1 - x, 1 - y)]
        first = [self._remote(ride, a, 0, src[a], me, sibling)]
        passed, landing = [], [self._remote(ride, a, 0, src[a], me + 1 - 2 * c, sibling)]
        for j, (px, py) in enumerate(chips):
            first.append(self._remote(ride, a, 1 + j, src[a], me, (px, py, c)))
            theirs = 4 * px + 2 * py + c
            passed.append(self._remote(ride, a, 4 + j, dst[a].at[theirs], theirs, sibling))
            landing.append(self._remote(ride, a, 1 + j, src[a], theirs, (px, py, c)))
        for j, (px, py) in enumerate(chips):
            landing.append(self._remote(ride, a, 4 + j, src[a], 4 * px + 2 * py + 1 - c, sibling))
        return first, passed, landing

    def start(self, ride):
        for a in range(self.n):
            self._local(ride, a).start()
            for cp in (self._direct(ride, a) if self.scatter else self._two_level(ride, a)[0]):
                cp.start()

    def finish(self, ride):
        if self.scatter:
            for a in range(self.n):
                for cp in self._direct(ride, a):
                    cp.wait()
                self._local(ride, a).wait()
            return
        levels = [self._two_level(ride, a) for a in range(self.n)]
        for first, passed, landing in levels:
            for j, cp in enumerate(passed):
                landing[1 + j].wait_recv()
                cp.start()
        for a, (first, passed, landing) in enumerate(levels):
            landing[0].wait_recv()
            for cp in landing[4:]:
                cp.wait_recv()
            for cp in first + passed:
                cp.wait_send()
            self._local(ride, a).wait()

    def start_at_first(self, ids, ride):
        first = functools.reduce(jnp.logical_and, [i == 0 for i in ids])

        @pl.when(first)
        def _():
            self.start(ride)

    def wait_at_last(self, ids, grid, ride):
        last = functools.reduce(jnp.logical_and, [i == g - 1 for i, g in zip(ids, grid)])

        @pl.when(last)
        def _():
            self.finish(ride)


def _exchange(bufs, scatter, name):
    rider = _Rider(bufs, scatter)

    def body(*refs):
        ride = (refs[:rider.n], refs[rider.n:2 * rider.n], refs[2 * rider.n:])
        rider.start(ride)
        rider.finish(ride)

    return pl.pallas_call(
        body,
        name=name,
        in_specs=rider.specs,
        out_specs=rider.specs,
        out_shape=rider.out_shape,
        scratch_shapes=rider.scratch,
    )(*rider.bufs)


MM_RESIDENT_B_BYTES = 14 * 1024 * 1024
MM_A_TILE_BYTES = 4 * 1024 * 1024
MM_OUT_TILE_BYTES = 6 * 1024 * 1024


def _mm_tiles(mode, M, N, K, a_bytes, out_bytes):
    if mode != "tn" and K * N * 2 <= MM_RESIDENT_B_BYTES:
        for tm in (1024, 512, 256, 128):
            if M % tm == 0 and tm * K * a_bytes <= MM_A_TILE_BYTES and tm * N * out_bytes <= MM_OUT_TILE_BYTES:
                return tm, N, K
    if mode == "tn":
        return (_pick(M, (1024, 1408, 512, 256, 128)), _pick(N, (1024, 1664, 1408, 512, 256, 128)),
                _pick(K, (2048, 1024, 512, 256, 128)))
    return _pick(M, (1024, 512, 256, 128)), _pick(N, (512, 256, 128)), _pick(K, (1024, 512, 256, 128))


def _mm(a, b, *, mode, out_dtype, name, res=None, res_scale=1.0, rider=None):
    if mode == "nn":
        (M, K), (K2, N) = a.shape, b.shape
    elif mode == "nt":
        (M, K), (N, K2) = a.shape, b.shape
    else:
        (K, M), (K2, N) = a.shape, b.shape
    assert K == K2, (a.shape, b.shape, mode)
    out_bytes = jnp.dtype(out_dtype).itemsize + (4 if res is not None else 0)
    tm, tn, tk = _mm_tiles(mode, M, N, K, a.dtype.itemsize, out_bytes)
    grid = (M // tm, N // tn, K // tk)
    nk = grid[2]
    dims = {"nn": _NN, "nt": _NT, "tn": _TN}[mode]
    n_in = 2 + (res is not None)
    n_ride = rider.n if rider is not None else 0

    def body(*refs):
        a_ref, b_ref = refs[:2]
        r_ref = refs[2] if res is not None else None
        o_ref = refs[n_in + n_ride]
        rest = refs[n_in + 2 * n_ride + 1:]
        acc_ref = rest[0] if nk > 1 else None
        ids = [pl.program_id(d) for d in range(3)]
        if rider is not None:
            ride = (refs[n_in:n_in + n_ride], refs[n_in + n_ride + 1:n_in + 2 * n_ride + 1], rest[-3:])
            rider.start_at_first(ids, ride)
        part = _dot(a_ref[...].astype(BF16), b_ref[...].astype(BF16), dims)

        def finish(total):
            if r_ref is not None:
                total = total + res_scale * r_ref[...]
            o_ref[...] = total.astype(out_dtype)

        if nk == 1:
            finish(part)
        else:
            k = ids[2]

            @pl.when(k == 0)
            def _():
                acc_ref[...] = part

            @pl.when(k > 0)
            def _():
                acc_ref[...] += part

            @pl.when(k == nk - 1)
            def _():
                finish(acc_ref[...])

        if rider is not None:
            rider.wait_at_last(ids, grid, ride)

    if mode == "nn":
        a_spec = pl.BlockSpec((tm, tk), lambda i, j, k: (i, k))
        b_spec = pl.BlockSpec((tk, tn), lambda i, j, k: (k, j))
    elif mode == "nt":
        a_spec = pl.BlockSpec((tm, tk), lambda i, j, k: (i, k))
        b_spec = pl.BlockSpec((tn, tk), lambda i, j, k: (j, k))
    else:
        a_spec = pl.BlockSpec((tk, tm), lambda i, j, k: (k, i))
        b_spec = pl.BlockSpec((tk, tn), lambda i, j, k: (k, j))
    o_spec = pl.BlockSpec((tm, tn), lambda i, j, k: (i, j))
    in_specs = [a_spec, b_spec] + ([o_spec] if res is not None else [])
    args = (a, b) + ((res,) if res is not None else ())
    out_specs, out_shape = [o_spec], [jax.ShapeDtypeStruct((M, N), out_dtype)]
    scratch = [pltpu.VMEM((tm, tn), F32)] if nk > 1 else []
    sem = ("parallel", "parallel", "arbitrary")
    if rider is not None:
        in_specs, args = in_specs + rider.specs, args + tuple(rider.bufs)
        out_specs, out_shape = out_specs + rider.specs, out_shape + rider.out_shape
        scratch = scratch + rider.scratch
        sem = ("arbitrary",) * 3
    outs = pl.pallas_call(
        body,
        name=name,
        grid=grid,
        in_specs=in_specs,
        out_specs=out_specs,
        out_shape=out_shape,
        scratch_shapes=scratch,
        compiler_params=_params(sem),
    )(*args)
    return outs[0] if rider is None else (outs[0], list(outs[1:]))


def _mm_host(a, b, *, rider, **kw):
    out = _mm(a, b, rider=rider, **kw)
    return out if rider is not None else (out, [])


def _as_host(rider, results):
    return results if rider is not None else tuple(results) + ([],)


MM_FUSED_MARGIN_BYTES = 10 * 1024 * 1024
MM_FUSED_MAX_ROWS = 512


def _col_sum_update(acc_ref, val, first):
    part = jnp.sum(val.reshape(val.shape[0] // 8, 8, val.shape[1]), axis=0)

    @pl.when(first)
    def _():
        acc_ref[...] = part

    @pl.when(jnp.logical_not(first))
    def _():
        acc_ref[...] += part


def _mm_fused(a, b, *, mode, name, extras, outs, epilogue, sums=(), rider=None, max_rows=MM_FUSED_MAX_ROWS,
              pass_a=False):
    parts = list(a) if isinstance(a, (list, tuple)) else [a]
    M, K = parts[0].shape[0], sum(p.shape[1] for p in parts)
    if mode == "nn":
        (K2, N), b_dims = b.shape, _NN
    else:
        (N, K2), b_dims = b.shape, _NT
    assert K == K2, (K, b.shape, mode)
    rows = parts + [e for e in extras if e.shape[0] == M]
    per_row = 2 * (sum(e.shape[1] * e.dtype.itemsize for e in rows)
                   + sum(c * jnp.dtype(d).itemsize for c, d in outs)) + 2 * N * 4
    budget = VMEM_LIMIT_BYTES - K * N * 2 - MM_FUSED_MARGIN_BYTES
    tm = next(t for t in (512, 256, 128, 64, 32, 16) if t <= max_rows and M % t == 0 and t * per_row <= budget)
    steps = M // tm
    n_a, n_x, n_o, n_s = len(parts), len(extras), len(outs), len(sums)
    n_ride = rider.n if rider is not None else 0

    def body(*refs):
        a_refs, b_ref = refs[:n_a], refs[n_a]
        x_refs = refs[n_a + 1:n_a + 1 + n_x]
        base = n_a + 1 + n_x + n_ride
        o_refs, s_refs = refs[base:base + n_o], refs[base + n_o:base + n_o + n_s]
        acc_refs = refs[base + n_o + n_s + n_ride:base + n_o + 2 * n_s + n_ride]
        ids = [pl.program_id(0)]
        if rider is not None:
            ride = (refs[n_a + 1 + n_x:base], refs[base + n_o + n_s:base + n_o + n_s + n_ride], refs[-3:])
            rider.start_at_first(ids, ride)
        a_tile = a_refs[0][...]
        a_bf16 = a_tile.astype(BF16) if n_a == 1 else jnp.concatenate([r[...].astype(BF16) for r in a_refs], axis=1)
        prod = _dot(a_bf16, b_ref[...], b_dims)
        tiles = epilogue(prod, *([a_tile] if pass_a else []), *[r[...] for r in x_refs])
        for o_ref, t in zip(o_refs, tiles[:n_o]):
            o_ref[...] = t.astype(o_ref.dtype)
        for acc_ref, t in zip(acc_refs, tiles[n_o:]):
            _col_sum_update(acc_ref, t, ids[0] == 0)
        if n_s:
            @pl.when(ids[0] == steps - 1)
            def _():
                for s_ref, acc_ref in zip(s_refs, acc_refs):
                    s_ref[...] = jnp.sum(acc_ref[...], axis=0, keepdims=True)
        if rider is not None:
            rider.wait_at_last(ids, (steps,), ride)

    in_specs = [pl.BlockSpec((tm, p.shape[1]), lambda i: (i, 0)) for p in parts]
    in_specs.append(pl.BlockSpec(b.shape, lambda i: (0, 0), pipeline_mode=pl.Buffered(1)))
    for e in extras:
        in_specs.append(pl.BlockSpec((tm, e.shape[1]), lambda i: (i, 0)) if e.shape[0] == M
                        else pl.BlockSpec(e.shape, lambda i: (0, 0)))
    out_specs = ([pl.BlockSpec((tm, c), lambda i: (i, 0)) for c, _ in outs]
                 + [pl.BlockSpec((1, c), lambda i: (0, 0)) for c in sums])
    out_shape = ([jax.ShapeDtypeStruct((M, c), d) for c, d in outs]
                 + [jax.ShapeDtypeStruct((1, c), F32) for c in sums])
    args = tuple(parts) + (b,) + tuple(extras)
    scratch = [pltpu.VMEM((8, c), F32) for c in sums]
    if rider is not None:
        in_specs, args = in_specs + rider.specs, args + tuple(rider.bufs)
        out_specs, out_shape = out_specs + rider.specs, out_shape + rider.out_shape
        scratch = scratch + rider.scratch
    res = pl.pallas_call(
        body,
        name=name,
        grid=(steps,),
        in_specs=in_specs,
        out_specs=out_specs,
        out_shape=out_shape,
        scratch_shapes=scratch,
        compiler_params=_params(("arbitrary",) if (n_s or rider is not None) else ("parallel",)),
    )(*args)
    return tuple(res[:n_o + n_s]) + ((list(res[n_o + n_s:]),) if rider is not None else ())


def _pair_rows(blk, lane_is_a):
    zero = jnp.zeros_like(blk)
    return jnp.concatenate([jnp.where(lane_is_a, blk, zero), jnp.where(lane_is_a, zero, blk)], axis=0)


SB_STRIP = 32
SB_FWD_PAIRS = 4
SB_BWD_PAIRS = 2
SB_GROUP = 2


def _pair_lanes(p):
    return slice(p * LANES, (p + 1) * LANES)


def _sb_scan_matrices():
    o = lax.broadcasted_iota(jnp.int32, (2 * LANES, 4 * LANES), 0)
    c = lax.broadcasted_iota(jnp.int32, (2 * LANES, 4 * LANES), 1) & (2 * LANES - 1)
    same = (o >= LANES) == (c >= LANES)
    oo, cc = o & (LANES - 1), c & (LANES - 1)
    return (jnp.where(same & (cc > oo), -1.0, 0.0).astype(BF16), jnp.where(same & (cc < oo), 1.0, 0.0).astype(BF16))


def _sb_causal_masks(tq):
    d = lax.broadcasted_iota(jnp.int32, (tq // SB_KEY_BLOCK, SB_KEY_BLOCK, tq), 0)
    k = lax.broadcasted_iota(jnp.int32, (tq // SB_KEY_BLOCK, SB_KEY_BLOCK, tq), 1)
    t = lax.broadcasted_iota(jnp.int32, (tq // SB_KEY_BLOCK, SB_KEY_BLOCK, tq), 2)
    return jnp.where(d * SB_KEY_BLOCK + k < t, 1.0, 0.0).astype(F32)


def _sb_log_terms(z):
    minus_abs = lax.bitcast_convert_type(lax.bitcast_convert_type(z, jnp.uint32) | jnp.uint32(0x80000000), F32)
    spent = jnp.maximum(z, 0.0) + jnp.log(1.0 + jnp.exp(minus_abs))
    return spent, z - spent


def _sb_store_split(ref, strip, val, cols):
    hi = val.astype(BF16)
    ref[pl.ds(strip * SB_STRIP, SB_STRIP), cols] = hi
    ref[pl.ds(2 * LANES + strip * SB_STRIP, SB_STRIP), cols] = (val - hi.astype(F32)).astype(BF16)


def _sb_lanes(tq, diag):
    if diag == "left":
        return 0, tq // 2
    first = 0 if diag is None else diag * SB_KEY_BLOCK
    return first, tq - first


def _lane_add(full, part, lanes):
    first, width = lanes
    pieces = [full[:, :first]] if first else []
    pieces.append(full[:, first:first + width] + part)
    if first + width < full.shape[1]:
        pieces.append(full[:, first + width:])
    return pieces[0] if len(pieces) == 1 else jnp.concatenate(pieces, axis=1)


def _sb_fwd(h_a, rider=None):
    assert SB_FWD_PAIRS == 4
    T = h_a.shape[0]
    tq = _pick(T, (SB_Q_BLOCK, SB_KEY_BLOCK))
    nq, per_q, nkb = T // tq, tq // SB_KEY_BLOCK, T // SB_KEY_BLOCK
    assert per_q % SB_GROUP == 0
    n_strips = 2 * LANES // SB_STRIP
    n_ride = rider.n if rider is not None else 0
    after_m, _ = _sb_scan_matrices()
    causal_m = _sb_causal_masks(tq)
    pairs = SB_FWD_PAIRS

    def body(*refs):
        q_ref, k_ref, v_ref, after_ref, causal_ref = refs[:5]
        a_ref, r_ref, n_ref = refs[5 + n_ride:8 + n_ride]
        z_ref, lb_ref, split_ref, w_ref = refs[8 + 2 * n_ride:12 + 2 * n_ride]
        ids = [pl.program_id(0)]
        if rider is not None:
            ride = (refs[5:5 + n_ride], refs[8 + n_ride:8 + 2 * n_ride], refs[-3:])
            rider.start_at_first(ids, ride)
        i = ids[0]
        q_t = [(q_ref[:, _pair_lanes(p)].astype(F32).T * (SB_HEAD_DIM ** -0.5)).astype(BF16) for p in range(pairs)]
        lane_is_a = lax.broadcasted_iota(jnp.int32, (SB_KEY_BLOCK, LANES), 1) < SB_HEAD_DIM

        def tiles(kbs, diags, carry):
            nb = len(kbs)
            lanes = [_sb_lanes(tq, d) for d in diags]
            cols = [slice(first, first + width) for first, width in lanes]
            acc_t, ra, rb = [list(c) for c in carry]
            ks = [pl.multiple_of(kb * SB_KEY_BLOCK, SB_KEY_BLOCK) for kb in kbs]
            slot = lambda p, b: p * nb + b

            def causal(b, s):
                return causal_ref[diags[b], pl.ds((s * SB_STRIP) % SB_KEY_BLOCK, SB_STRIP), cols[b]]

            vv = {}
            for b in range(nb):
                for p in range(pairs):
                    kk = _pair_rows(k_ref[pl.ds(ks[b], SB_KEY_BLOCK), _pair_lanes(p)], lane_is_a)
                    vv[p, b] = _pair_rows(v_ref[pl.ds(ks[b], SB_KEY_BLOCK), _pair_lanes(p)], lane_is_a)
                    z_ref[slot(p, b), :, cols[b]] = _dot(kk, q_t[p][:, cols[b]], _NN)
            sums = {}
            for b in range(nb):
                for p in range(pairs):
                    part = [jnp.zeros((8, lanes[b][1]), F32), jnp.zeros((8, lanes[b][1]), F32)]
                    for s in range(n_strips):
                        rows = pl.ds(s * SB_STRIP, SB_STRIP)
                        spent, log_beta = _sb_log_terms(z_ref[slot(p, b), rows, cols[b]])
                        lb_ref[slot(p, b), rows, cols[b]] = log_beta
                        if isinstance(diags[b], int):
                            spent = spent * causal(b, s)
                        _sb_store_split(split_ref.at[slot(p, b)], s, spent, cols[b])
                        head = (s * SB_STRIP) // SB_KEY_BLOCK
                        part[head] = part[head] + jnp.sum(spent.reshape(SB_STRIP // 8, 8, lanes[b][1]), axis=0)
                    sums[p, b] = part
            for b in range(nb):
                for p in range(pairs):
                    z_ref[slot(p, b), :, cols[b]] = _dot(after_ref[...], split_ref[slot(p, b), :, cols[b]], _NN)
            for b in range(nb):
                for p in range(pairs):
                    for s in range(n_strips):
                        rows = pl.ds(s * SB_STRIP, SB_STRIP)
                        start = (ra[p] if (s * SB_STRIP) < SB_KEY_BLOCK else rb[p])[:, cols[b]]
                        w = jnp.exp(lb_ref[slot(p, b), rows, cols[b]] + z_ref[slot(p, b), rows, cols[b]] + start)
                        if isinstance(diags[b], int):
                            w = w * causal(b, s)
                        w_ref[slot(p, b), rows, cols[b]] = w.astype(BF16)
                    r_ref[2 * p, kbs[b]] = ra[p]
                    r_ref[2 * p + 1, kbs[b]] = rb[p]
                    ra[p] = _lane_add(ra[p], -jnp.sum(sums[p, b][0], axis=0, keepdims=True), lanes[b])
                    rb[p] = _lane_add(rb[p], -jnp.sum(sums[p, b][1], axis=0, keepdims=True), lanes[b])
            for b in range(nb):
                for p in range(pairs):
                    acc_t[p] = _lane_add(acc_t[p], _dot(vv[p, b], w_ref[slot(p, b), :, cols[b]], _TN), lanes[b])
            return tuple(acc_t), tuple(ra), tuple(rb)

        carry = (tuple(jnp.zeros((LANES, tq), F32) for _ in range(pairs)),
                 tuple(jnp.zeros((1, tq), F32) for _ in range(pairs)),
                 tuple(jnp.zeros((1, tq), F32) for _ in range(pairs)))
        own = list(reversed(range(per_q)))
        n_full = i * per_q
        carry = lax.cond(
            i > 0,
            lambda cc: tiles([n_full + d for d in own] + [n_full - 1 - b for b in range(SB_GROUP)],
                             own + [None] * SB_GROUP, cc),
            lambda cc: tiles([n_full + d for d in own], own, cc), carry)
        first_walked = jnp.where(i > 0, SB_GROUP, 0).astype(jnp.int32)

        def top_of(sums_a, sums_b, first):
            return jnp.max(functools.reduce(jnp.maximum, [r[:, first:] for r in sums_a + sums_b]))

        def alive(c):
            return jnp.logical_and(c[0] < n_full, top_of(c[2], c[3], 0) > SB_DEAD_LOG)

        def step(c):
            kbs = [n_full - 1 - c[0] - b for b in range(SB_GROUP)]
            return (c[0] + SB_GROUP,) + lax.cond(
                top_of(c[2], c[3], tq // 2) > SB_DEAD_LOG,
                lambda cc: tiles(kbs, [None] * SB_GROUP, cc), lambda cc: tiles(kbs, ["left"] * SB_GROUP, cc), c[1:])

        walked, acc_t, _, _ = lax.while_loop(alive, step, (first_walked,) + carry)
        for p in range(pairs):
            a_ref[:, _pair_lanes(p)] = acc_t[p].T.astype(BF16)
        n_ref[...] = jnp.zeros(n_ref.shape, F32) + walked.astype(F32)
        if rider is not None:
            rider.wait_at_last(ids, (nq,), ride)

    wide = pairs * LANES
    in_specs = [pl.BlockSpec((tq, wide), lambda i: (i, 0)),
                pl.BlockSpec((T, wide), lambda i: (0, 1), pipeline_mode=pl.Buffered(1)),
                pl.BlockSpec((T, wide), lambda i: (0, 2), pipeline_mode=pl.Buffered(1)),
                pl.BlockSpec(after_m.shape, lambda i: (0, 0), pipeline_mode=pl.Buffered(1)),
                pl.BlockSpec(causal_m.shape, lambda i: (0, 0, 0), pipeline_mode=pl.Buffered(1))]
    out_specs = [pl.BlockSpec((tq, wide), lambda i: (i, 0)),
                 pl.BlockSpec((2 * pairs, nkb, 1, tq), lambda i: (0, 0, 0, i)),
                 pl.BlockSpec((1, 8, LANES), lambda i: (i, 0, 0))]
    out_shape = [jax.ShapeDtypeStruct((T, SB_WIDTH), BF16), jax.ShapeDtypeStruct((2 * pairs, nkb, 1, T), F32),
                 jax.ShapeDtypeStruct((nq, 8, LANES), F32)]
    args = (h_a, h_a, h_a, after_m, causal_m)
    slots = pairs * (per_q + SB_GROUP)
    scratch = [pltpu.VMEM((slots, 2 * LANES, tq), F32), pltpu.VMEM((slots, 2 * LANES, tq), F32),
               pltpu.VMEM((slots, 4 * LANES, tq), BF16), pltpu.VMEM((slots, 2 * LANES, tq), BF16)]
    if rider is not None:
        in_specs, args = in_specs + rider.specs, args + tuple(rider.bufs)
        out_specs, out_shape = out_specs + rider.specs, out_shape + rider.out_shape
        scratch = scratch + rider.scratch
    outs = pl.pallas_call(
        body,
        name="sb_fwd",
        grid=(nq,),
        in_specs=in_specs,
        out_specs=out_specs,
        out_shape=out_shape,
        scratch_shapes=scratch,
        compiler_params=_params(("arbitrary",)),
    )(*args)
    return outs[0], (outs[1], outs[2]), list(outs[3:])


def _sb_bwd(h_a, d_out, saved, rider=None):
    r_mat, walked_blocks = saved
    T = h_a.shape[0]
    tq = _pick(T, (SB_Q_BLOCK, SB_KEY_BLOCK))
    nq, per_q, nkb = T // tq, tq // SB_KEY_BLOCK, T // SB_KEY_BLOCK
    n_strips = 2 * LANES // SB_STRIP
    after_m, before_m = _sb_scan_matrices()
    causal_m = _sb_causal_masks(tq)
    pairs = SB_BWD_PAIRS
    groups = 4 // pairs
    n_ride = rider.n if rider is not None else 0

    def body(*refs):
        q_ref, k_ref, v_ref, do_ref, r_ref, n_ref, after_ref, before_ref, causal_ref = refs[:9]
        dq_ref, dk_ref, dv_ref = refs[9 + n_ride:12 + n_ride]
        z_ref, lb_ref, split_ref, w_ref, da_ref, dz_ref = refs[12 + 2 * n_ride:18 + 2 * n_ride]
        ids = [pl.program_id(0), pl.program_id(1)]
        if rider is not None:
            ride = (refs[9:9 + n_ride], refs[12 + n_ride:12 + 2 * n_ride], refs[-3:])
            rider.start_at_first(ids, ride)
        i = ids[1]

        @pl.when(i == 0)
        def _():
            dk_ref[...] = jnp.zeros_like(dk_ref)
            dv_ref[...] = jnp.zeros_like(dv_ref)

        scale = SB_HEAD_DIM ** -0.5
        q = [q_ref[:, _pair_lanes(p)] for p in range(pairs)]
        d_o = [do_ref[:, _pair_lanes(p)] for p in range(pairs)]
        q_t = [(x.astype(F32).T * scale).astype(BF16) for x in q]
        do_t = [x.astype(F32).T.astype(BF16) for x in d_o]
        lane_is_a = lax.broadcasted_iota(jnp.int32, (SB_KEY_BLOCK, LANES), 1) < SB_HEAD_DIM

        def tiles(kbs, diags, carry):
            nb = len(kbs)
            lanes = [_sb_lanes(tq, d) for d in diags]
            cols = [slice(first, first + width) for first, width in lanes]
            dq_t, ca, cb = [list(c) for c in carry]
            ks = [pl.multiple_of(kb * SB_KEY_BLOCK, SB_KEY_BLOCK) for kb in kbs]
            slot = lambda p, b: p * nb + b

            def causal(b, s):
                return causal_ref[diags[b], pl.ds((s * SB_STRIP) % SB_KEY_BLOCK, SB_STRIP), cols[b]]

            kk, vv = {}, {}
            for b in range(nb):
                for p in range(pairs):
                    kk[p, b] = _pair_rows(k_ref[pl.ds(ks[b], SB_KEY_BLOCK), _pair_lanes(p)], lane_is_a)
                    vv[p, b] = _pair_rows(v_ref[pl.ds(ks[b], SB_KEY_BLOCK), _pair_lanes(p)], lane_is_a)
                    z_ref[slot(p, b), :, cols[b]] = _dot(kk[p, b], q_t[p][:, cols[b]], _NN)
            for b in range(nb):
                for p in range(pairs):
                    for s in range(n_strips):
                        rows = pl.ds(s * SB_STRIP, SB_STRIP)
                        spent, log_beta = _sb_log_terms(z_ref[slot(p, b), rows, cols[b]])
                        lb_ref[slot(p, b), rows, cols[b]] = log_beta
                        if isinstance(diags[b], int):
                            spent = spent * causal(b, s)
                        _sb_store_split(split_ref.at[slot(p, b)], s, spent, cols[b])
            for b in range(nb):
                for p in range(pairs):
                    z_ref[slot(p, b), :, cols[b]] = _dot(after_ref[...], split_ref[slot(p, b), :, cols[b]], _NN)
                    da_ref[slot(p, b), :, cols[b]] = _dot(vv[p, b], do_t[p][:, cols[b]], _NN)
            sums = {}
            for b in range(nb):
                for p in range(pairs):
                    part = [jnp.zeros((8, lanes[b][1]), F32), jnp.zeros((8, lanes[b][1]), F32)]
                    for s in range(n_strips):
                        rows = pl.ds(s * SB_STRIP, SB_STRIP)
                        start = r_ref[2 * p + (s * SB_STRIP) // SB_KEY_BLOCK, kbs[b]][:, cols[b]]
                        w = jnp.exp(lb_ref[slot(p, b), rows, cols[b]] + z_ref[slot(p, b), rows, cols[b]] + start)
                        if isinstance(diags[b], int):
                            w = w * causal(b, s)
                        w_ref[slot(p, b), rows, cols[b]] = w.astype(BF16)
                        da = da_ref[slot(p, b), rows, cols[b]] * w
                        da_ref[slot(p, b), rows, cols[b]] = da
                        _sb_store_split(split_ref.at[slot(p, b)], s, da, cols[b])
                        head = (s * SB_STRIP) // SB_KEY_BLOCK
                        part[head] = part[head] + jnp.sum(da.reshape(SB_STRIP // 8, 8, lanes[b][1]), axis=0)
                    sums[p, b] = part
            for b in range(nb):
                for p in range(pairs):
                    z_ref[slot(p, b), :, cols[b]] = _dot(before_ref[...], split_ref[slot(p, b), :, cols[b]], _NN)
            for b in range(nb):
                for p in range(pairs):
                    for s in range(n_strips):
                        rows = pl.ds(s * SB_STRIP, SB_STRIP)
                        base = (ca[p] if (s * SB_STRIP) < SB_KEY_BLOCK else cb[p])[:, cols[b]]
                        sig = jnp.exp(lb_ref[slot(p, b), rows, cols[b]])
                        dz = (da_ref[slot(p, b), rows, cols[b]] * (1.0 - sig)
                              - (z_ref[slot(p, b), rows, cols[b]] + base) * sig)
                        if isinstance(diags[b], int):
                            dz = dz * causal(b, s)
                        dz_ref[slot(p, b), rows, cols[b]] = (dz * scale).astype(BF16)
                    ca[p] = _lane_add(ca[p], jnp.sum(sums[p, b][0], axis=0, keepdims=True), lanes[b])
                    cb[p] = _lane_add(cb[p], jnp.sum(sums[p, b][1], axis=0, keepdims=True), lanes[b])
            for b in range(nb):
                for p in range(pairs):
                    dq_t[p] = _lane_add(dq_t[p], _dot(kk[p, b], dz_ref[slot(p, b), :, cols[b]], _TN), lanes[b])
                    dkk = _dot(dz_ref[slot(p, b), :, cols[b]], q[p][cols[b], :], _NN)
                    dvv = _dot(w_ref[slot(p, b), :, cols[b]], d_o[p][cols[b], :], _NN)
                    here = (pl.ds(ks[b], SB_KEY_BLOCK), _pair_lanes(p))
                    dk_ref[here] += jnp.where(lane_is_a, dkk[:SB_KEY_BLOCK], dkk[SB_KEY_BLOCK:])
                    dv_ref[here] += jnp.where(lane_is_a, dvv[:SB_KEY_BLOCK], dvv[SB_KEY_BLOCK:])
            return tuple(dq_t), tuple(ca), tuple(cb)

        n_full = i * per_q
        groups_walked = jnp.clip(jnp.max(n_ref[...]).astype(jnp.int32), 0, n_full) // SB_GROUP
        carry = (tuple(jnp.zeros((LANES, tq), F32) for _ in range(pairs)),
                 tuple(jnp.zeros((1, tq), F32) for _ in range(pairs)),
                 tuple(jnp.zeros((1, tq), F32) for _ in range(pairs)))

        def below(j, c):
            kbs = [n_full - (groups_walked - j) * SB_GROUP + b for b in range(SB_GROUP)]
            starts = [r_ref[h, kbs[-1]][:, tq // 2:] for h in range(2 * pairs)]
            reaches = jnp.max(functools.reduce(jnp.maximum, starts)) > SB_DEAD_LOG
            return lax.cond(reaches, lambda cc: tiles(kbs, [None] * SB_GROUP, cc),
                            lambda cc: tiles(kbs, ["left"] * SB_GROUP, cc), c)

        carry = lax.fori_loop(0, groups_walked, below, carry)
        own = list(range(per_q))
        carry = tiles([i * per_q + d for d in own], own, carry)
        for p in range(pairs):
            dq_ref[:, _pair_lanes(p)] = carry[0][p].T.astype(BF16)
        if rider is not None:
            rider.wait_at_last(ids, (groups, nq), ride)

    wide = pairs * LANES
    mat = pl.BlockSpec(after_m.shape, lambda g, i: (0, 0), pipeline_mode=pl.Buffered(1))
    in_specs = [pl.BlockSpec((tq, wide), lambda g, i: (i, g)),
                pl.BlockSpec((T, wide), lambda g, i: (0, groups + g), pipeline_mode=pl.Buffered(1)),
                pl.BlockSpec((T, wide), lambda g, i: (0, 2 * groups + g), pipeline_mode=pl.Buffered(1)),
                pl.BlockSpec((tq, wide), lambda g, i: (i, g)),
                pl.BlockSpec((2 * pairs, nkb, 1, tq), lambda g, i: (g, 0, 0, i)),
                pl.BlockSpec((1, 8, LANES), lambda g, i: (i, 0, 0)),
                mat, mat,
                pl.BlockSpec(causal_m.shape, lambda g, i: (0, 0, 0), pipeline_mode=pl.Buffered(1))]
    out_specs = [pl.BlockSpec((tq, wide), lambda g, i: (i, g)),
                 pl.BlockSpec((T, wide), lambda g, i: (0, g)),
                 pl.BlockSpec((T, wide), lambda g, i: (0, g))]
    out_shape = [jax.ShapeDtypeStruct((T, SB_WIDTH), BF16), jax.ShapeDtypeStruct((T, SB_WIDTH), F32),
                 jax.ShapeDtypeStruct((T, SB_WIDTH), F32)]
    args = (h_a, h_a, h_a, d_out, r_mat, walked_blocks, after_m, before_m, causal_m)
    slots = pairs * max(per_q, SB_GROUP)
    scratch = [pltpu.VMEM((slots, 2 * LANES, tq), F32), pltpu.VMEM((slots, 2 * LANES, tq), F32),
               pltpu.VMEM((slots, 4 * LANES, tq), BF16), pltpu.VMEM((slots, 2 * LANES, tq), BF16),
               pltpu.VMEM((slots, 2 * LANES, tq), F32), pltpu.VMEM((slots, 2 * LANES, tq), BF16)]
    if rider is not None:
        in_specs, args = in_specs + rider.specs, args + tuple(rider.bufs)
        out_specs, out_shape = out_specs + rider.specs, out_shape + rider.out_shape
        scratch = scratch + rider.scratch
    outs = pl.pallas_call(
        body,
        name="sb_bwd",
        grid=(groups, nq),
        in_specs=in_specs,
        out_specs=out_specs,
        out_shape=out_shape,
        scratch_shapes=scratch,
        compiler_params=_params(("arbitrary", "arbitrary") if rider is not None else ("parallel", "arbitrary")),
    )(*args)
    return outs[0], outs[1], outs[2], list(outs[3:])


def _ret_tables(T):
    half = RET_QK_DIM // 2
    inv = 1.0 / (ROPE_BASE ** (jnp.arange(half, dtype=F32) / half))
    ang = jnp.arange(T, dtype=F32)[:, None] * inv[None, :]
    cos, sin = jnp.cos(ang), jnp.sin(ang)
    cos_t = jnp.concatenate([cos, cos], axis=1)
    sin_t = jnp.concatenate([-sin, sin], axis=1)
    log_gamma = jnp.log1p(-jnp.exp2(-5.0 - jnp.arange(RET_HEADS, dtype=F32)))
    idx = jnp.arange(RET_CHUNK, dtype=F32)
    rel = idx[:, None] - idx[None, :]
    decay = jnp.where(rel[None] >= 0, jnp.exp(log_gamma[:, None, None] * jnp.maximum(rel, 0.0)[None]), 0.0)
    k_decay = jnp.exp(log_gamma[None, :] * (RET_CHUNK - 1.0 - idx)[:, None])
    q_decay = jnp.exp(log_gamma[None, :] * (idx + 1.0)[:, None])
    chunk_decay = jnp.exp(log_gamma * RET_CHUNK)
    k_dec = jnp.broadcast_to(k_decay.T[:, :, None], (RET_HEADS, RET_CHUNK, LANES))
    q_dec = jnp.broadcast_to(q_decay.T[:, :, None], (RET_HEADS, RET_CHUNK, LANES))
    c_dec = jnp.broadcast_to(chunk_decay[:, None, None], (RET_HEADS, 8, LANES))
    return cos_t, sin_t, decay, k_dec, q_dec, c_dec


def _rotary(x, cos_t, sin_t):
    return x * cos_t + pltpu.roll(x, RET_QK_DIM // 2, 1) * sin_t


def _rotary_transpose(dy, cos_t, sin_t):
    return dy * cos_t + pltpu.roll(dy * sin_t, RET_QK_DIM // 2, 1)


def _head_norm(o):
    mu = jnp.mean(o, axis=1, keepdims=True)
    cen = o - mu
    var = jnp.mean(cen * cen, axis=1, keepdims=True)
    rstd = lax.rsqrt(var + LN_EPS)
    return cen * rstd, rstd


def _ret_specs(steps, per_step, reverse):
    def n_of(n):
        return (steps - 1 - n) if reverse else n

    rows = per_step * RET_CHUNK
    q_spec = pl.BlockSpec((rows, RET_QK_WIDTH), lambda n: (n_of(n), 0))
    k_spec = pl.BlockSpec((rows, RET_QK_WIDTH), lambda n: (n_of(n), 1))
    vv = pl.BlockSpec((rows, RET_V_WIDTH), lambda n: (n_of(n), 0))
    pos = pl.BlockSpec((rows, LANES), lambda n: (n_of(n), 0))
    per_head = pl.BlockSpec((RET_HEADS, RET_CHUNK, LANES), lambda n: (0, 0, 0))
    c_dec = pl.BlockSpec((RET_HEADS, 8, LANES), lambda n: (0, 0, 0))
    state = pl.BlockSpec((RET_HEADS, per_step, RET_QK_DIM, RET_V_DIM), lambda n: (0, n_of(n), 0, 0))
    return q_spec, k_spec, vv, pos, per_head, c_dec, state


def _qk_cols(h):
    return slice(h * RET_QK_DIM, (h + 1) * RET_QK_DIM)


def _v_cols(h):
    return slice(h * RET_V_DIM, (h + 1) * RET_V_DIM)


def _ret_fwd(h_b, h_c, h_d, tables):
    T = h_b.shape[0]
    nc = T // RET_CHUNK
    per_step = _pick(nc, (RET_STEP_CHUNKS, 1))
    steps = nc // per_step
    q_spec, k_spec, vv, pos, per_head, c_dec, state = _ret_specs(steps, per_step, False)

    def body(q_ref, k_ref, v_ref, g_ref, cos_ref, sin_ref, dec_ref, kd_ref, qd_ref, cd_ref,
             y_ref, o_ref, st_ref, state_ref):
        @pl.when(pl.program_id(0) == 0)
        def _():
            state_ref[...] = jnp.zeros_like(state_ref)

        for c in range(per_step):
            rows = pl.ds(c * RET_CHUNK, RET_CHUNK)
            cos_t, sin_t = cos_ref[rows, :], sin_ref[rows, :]
            for h in range(RET_HEADS):
                q = _rotary(q_ref[rows, _qk_cols(h)], cos_t, sin_t) * (RET_QK_DIM ** -0.5)
                k = _rotary(k_ref[rows, _qk_cols(h)], cos_t, sin_t)
                v = v_ref[rows, _v_cols(h)]
                prev = state_ref[h]
                scores = _dot(q.astype(BF16), k.astype(BF16), _NT) * dec_ref[h]
                inner = _dot(scores.astype(BF16), v, _NN)
                cross = _dot((q * qd_ref[h]).astype(BF16), prev.astype(BF16), _NN)
                o = inner + cross
                st_ref[h, c] = prev
                kv = _dot((k * kd_ref[h]).astype(BF16), v, _TN)
                state_ref[h] = prev * cd_ref[h, 0:1, 0:1] + kv
                o_ref[rows, _v_cols(h)] = o
                normed, _ = _head_norm(o)
                gate = g_ref[rows, _v_cols(h)]
                y_ref[rows, _v_cols(h)] = (gate * jax.nn.sigmoid(gate) * normed).astype(BF16)

    return pl.pallas_call(
        body,
        name="ret_fwd",
        grid=(steps,),
        in_specs=[q_spec, k_spec, vv, vv, pos, pos, per_head, per_head, per_head, c_dec],
        out_specs=[vv, vv, state],
        out_shape=[jax.ShapeDtypeStruct((T, RET_V_WIDTH), BF16),
                   jax.ShapeDtypeStruct((T, RET_V_WIDTH), F32),
                   jax.ShapeDtypeStruct((RET_HEADS, nc, RET_QK_DIM, RET_V_DIM), F32)],
        scratch_shapes=[pltpu.VMEM((RET_HEADS, RET_QK_DIM, RET_V_DIM), F32)],
        compiler_params=_params(("arbitrary",)),
    )(h_b, h_b, h_c, h_d, *tables)


def _ret_bwd(d_y, o_pre, states, h_b, h_c, h_d, tables, rider=None):
    T = h_b.shape[0]
    nc = T // RET_CHUNK
    per_step = _pick(nc, (RET_STEP_CHUNKS, 1))
    steps = nc // per_step
    q_spec, k_spec, vv, pos, per_head, c_dec, state = _ret_specs(steps, per_step, True)
    n_ride = rider.n if rider is not None else 0

    def body(*refs):
        (dy_ref, o_ref, st_ref, q_ref, k_ref, v_ref, g_ref, cos_ref, sin_ref, dec_ref, kd_ref, qd_ref,
         cd_ref) = refs[:13]
        dq_ref, dk_ref, dv_ref, dg_ref = refs[13 + n_ride:17 + n_ride]
        carry_ref = refs[17 + 2 * n_ride]
        ids = [pl.program_id(0)]
        if rider is not None:
            ride = (refs[13:13 + n_ride], refs[17 + n_ride:17 + 2 * n_ride], refs[-3:])
            rider.start_at_first(ids, ride)

        @pl.when(ids[0] == 0)
        def _():
            carry_ref[...] = jnp.zeros_like(carry_ref)

        scale = RET_QK_DIM ** -0.5
        for c in reversed(range(per_step)):
            rows = pl.ds(c * RET_CHUNK, RET_CHUNK)
            cos_t, sin_t = cos_ref[rows, :], sin_ref[rows, :]
            for h in range(RET_HEADS):
                q = _rotary(q_ref[rows, _qk_cols(h)], cos_t, sin_t) * scale
                k = _rotary(k_ref[rows, _qk_cols(h)], cos_t, sin_t)
                v = v_ref[rows, _v_cols(h)]
                decay, k_dec, q_dec = dec_ref[h], kd_ref[h], qd_ref[h]
                chunk_decay = cd_ref[h, 0:1, 0:1]
                state = st_ref[h, c].astype(BF16)
                later = carry_ref[h]
                later_b = later.astype(BF16)

                gate = g_ref[rows, _v_cols(h)]
                sig = jax.nn.sigmoid(gate)
                silu = gate * sig
                normed, rstd = _head_norm(o_ref[rows, _v_cols(h)])
                d_y = dy_ref[rows, _v_cols(h)]
                dg_ref[rows, _v_cols(h)] = (d_y * normed * (sig * (1.0 + gate * (1.0 - sig)))).astype(BF16)
                d_n = d_y * silu
                d_o = rstd * (d_n - jnp.mean(d_n, axis=1, keepdims=True)
                              - normed * jnp.mean(d_n * normed, axis=1, keepdims=True))
                d_ob = d_o.astype(BF16)

                qb, kb = q.astype(BF16), k.astype(BF16)
                qd_b, kd_b = (q * q_dec).astype(BF16), (k * k_dec).astype(BF16)
                scores = _dot(qb, kb, _NT) * decay
                d_scores = (_dot(d_ob, v, _NT) * decay).astype(BF16)
                dq = _dot(d_scores, kb, _NN) + _dot(d_ob, state, _NT) * q_dec
                dk = _dot(d_scores, qb, _TN) + _dot(v, later_b, _NT) * k_dec
                dv = _dot(scores.astype(BF16), d_ob, _TN) + _dot(kd_b, later_b, _NN)
                carry_ref[h] = _dot(qd_b, d_ob, _TN) + chunk_decay * later
                dq_ref[rows, _qk_cols(h)] = _rotary_transpose(dq * scale, cos_t, sin_t).astype(BF16)
                dk_ref[rows, _qk_cols(h)] = _rotary_transpose(dk, cos_t, sin_t).astype(BF16)
                dv_ref[rows, _v_cols(h)] = dv.astype(BF16)
        if rider is not None:
            rider.wait_at_last(ids, (steps,), ride)

    qk_out = pl.BlockSpec((per_step * RET_CHUNK, RET_QK_WIDTH), lambda n: (steps - 1 - n, 0))
    in_specs = [vv, vv, state, q_spec, k_spec, vv, vv, pos, pos, per_head, per_head, per_head, c_dec]
    out_specs = [qk_out, qk_out, vv, vv]
    out_shape = [jax.ShapeDtypeStruct((T, RET_QK_WIDTH), BF16), jax.ShapeDtypeStruct((T, RET_QK_WIDTH), BF16),
                 jax.ShapeDtypeStruct((T, RET_V_WIDTH), BF16), jax.ShapeDtypeStruct((T, RET_V_WIDTH), BF16)]
    args = (d_y, o_pre, states, h_b, h_b, h_c, h_d) + tuple(tables)
    scratch = [pltpu.VMEM((RET_HEADS, RET_QK_DIM, RET_V_DIM), F32)]
    if rider is not None:
        in_specs, args = in_specs + rider.specs, args + tuple(rider.bufs)
        out_specs, out_shape = out_specs + rider.specs, out_shape + rider.out_shape
        scratch = scratch + rider.scratch
    outs = pl.pallas_call(
        body,
        name="ret_bwd",
        grid=(steps,),
        in_specs=in_specs,
        out_specs=out_specs,
        out_shape=out_shape,
        scratch_shapes=scratch,
        compiler_params=_params(("arbitrary",)),
    )(*args)
    return outs[0], outs[1], outs[2], outs[3], list(outs[4:])


def _proj_tiles(h, x):
    return h[:, 0:1536], h[:, 1536:2560], h[:, 2560:3584], h[:, 3584:4608], h[:, 4608:6656], x


def _gate_mix_tiles(y_ret, h_e, b_gate, y_sb):
    gates = jax.nn.sigmoid(h_e + b_gate)
    return y_ret, gates[:, :D_MODEL] * y_sb + gates[:, D_MODEL:] * y_ret


def _gate_mix_grad_tiles(d_mix, h_e, b_gate, y_sb, y_ret):
    gates = jax.nn.sigmoid(h_e + b_gate)
    g0, g1 = gates[:, :D_MODEL], gates[:, D_MODEL:]
    d_e = jnp.concatenate([d_mix * y_sb * g0 * (1.0 - g0), d_mix * y_ret * g1 * (1.0 - g1)], axis=1)
    return d_mix * g0, d_mix * g1, d_e, d_e


def _ln_stats(u):
    mu = jnp.mean(u, axis=1, keepdims=True)
    cen = u - mu
    var = jnp.mean(cen * cen, axis=1, keepdims=True)
    rstd = lax.rsqrt(var + LN_EPS)
    return cen * rstd, rstd


def _ln_input_grad(d_out, gain, xhat, rstd):
    d_hat = d_out * gain
    return rstd * (d_hat - jnp.mean(d_hat, axis=1, keepdims=True)
                   - xhat * jnp.mean(d_hat * xhat, axis=1, keepdims=True))


def _ln_tiles(sub, x_prev, gain, bias):
    xhat, rstd = _ln_stats(DN_ALPHA * x_prev + sub)
    return xhat * gain + bias, xhat, rstd


def _ln_after_ln_tiles(sub, prev_hat, prev_gain, prev_bias, gain, bias):
    return _ln_tiles(sub, prev_hat * prev_gain + prev_bias, gain, bias)


def _residual_tiles(d_sub, res):
    return (d_sub + DN_ALPHA * res,)


def _ln_grad_tiles(d_sub, res, xhat, rstd, gain):
    d_out = d_sub + DN_ALPHA * res
    du = _ln_input_grad(d_out, gain, xhat, rstd)
    return du, du, d_out * xhat, d_out


def _ln_loss_tiles(sub, prev_hat, prev_gain, prev_bias, gain, bias, target):
    xhat, rstd = _ln_stats(DN_ALPHA * (prev_hat * prev_gain + prev_bias) + sub)
    diff = xhat * gain + bias - target
    d_out = diff * (1.0 / D_MODEL)
    du = _ln_input_grad(d_out, gain, xhat, rstd)
    return du, du, diff * diff, d_out * xhat, d_out


def _mem_probs(q_h, k_h):
    s = _dot(q_h, k_h, _NT) * (MEM_HEAD_DIM ** -0.5)
    e = jnp.exp(s - jnp.max(s, axis=1, keepdims=True))
    return e / jnp.sum(e, axis=1, keepdims=True)


def _xattn_fwd(q, kv):
    T, mem_len = q.shape[0], kv.shape[0]
    tq = _pick(T, (512, 256, 128))

    def body(q_ref, kv_ref, o_ref):
        for h in range(MEM_HEADS):
            cols = slice(h * MEM_HEAD_DIM, (h + 1) * MEM_HEAD_DIM)
            vcols = slice(D_MODEL + h * MEM_HEAD_DIM, D_MODEL + (h + 1) * MEM_HEAD_DIM)
            p = _mem_probs(q_ref[:, cols], kv_ref[:, cols])
            o_ref[:, cols] = _dot(p.astype(BF16), kv_ref[:, vcols], _NN).astype(BF16)

    return pl.pallas_call(
        body,
        name="xattn_fwd",
        grid=(T // tq,),
        in_specs=[pl.BlockSpec((tq, D_MODEL), lambda i: (i, 0)),
                  pl.BlockSpec((mem_len, 2 * D_MODEL), lambda i: (0, 0))],
        out_specs=pl.BlockSpec((tq, D_MODEL), lambda i: (i, 0)),
        out_shape=jax.ShapeDtypeStruct((T, D_MODEL), BF16),
        compiler_params=_params(("parallel",)),
    )(q, kv)


def _xattn_bwd(q, kv, d_o):
    T, mem_len = q.shape[0], kv.shape[0]
    tq = _pick(T, (512, 256, 128))

    def body(q_ref, kv_ref, do_ref, dq_ref, dkv_ref):
        @pl.when(pl.program_id(0) == 0)
        def _():
            dkv_ref[...] = jnp.zeros_like(dkv_ref)

        for h in range(MEM_HEADS):
            cols = slice(h * MEM_HEAD_DIM, (h + 1) * MEM_HEAD_DIM)
            vcols = slice(D_MODEL + h * MEM_HEAD_DIM, D_MODEL + (h + 1) * MEM_HEAD_DIM)
            q_h, k_h, do_h = q_ref[:, cols], kv_ref[:, cols], do_ref[:, cols]
            p = _mem_probs(q_h, k_h)
            dp = _dot(do_h, kv_ref[:, vcols], _NT)
            ds = p * (dp - jnp.sum(dp * p, axis=1, keepdims=True))
            dsb = (ds * (MEM_HEAD_DIM ** -0.5)).astype(BF16)
            dq_ref[:, cols] = _dot(dsb, k_h, _NN).astype(BF16)
            dkv_ref[:, cols] += _dot(dsb, q_h, _TN)
            dkv_ref[:, vcols] += _dot(p.astype(BF16), do_h, _TN)

    row = pl.BlockSpec((tq, D_MODEL), lambda i: (i, 0))
    full = pl.BlockSpec((mem_len, 2 * D_MODEL), lambda i: (0, 0))
    return pl.pallas_call(
        body,
        name="xattn_bwd",
        grid=(T // tq,),
        in_specs=[row, full, row],
        out_specs=[row, full],
        out_shape=[jax.ShapeDtypeStruct((T, D_MODEL), BF16), jax.ShapeDtypeStruct((mem_len, 2 * D_MODEL), F32)],
        compiler_params=_params(("arbitrary",)),
    )(q, kv, d_o)


def _swiglu_tiles(f):
    a, b = f[:, :FFN_HIDDEN], f[:, FFN_HIDDEN:]
    return f, a * jax.nn.sigmoid(a) * b


def _swiglu_grad_tiles(d_hidden, f):
    f = f.astype(F32)
    a, b = f[:, :FFN_HIDDEN], f[:, FFN_HIDDEN:]
    sig = jax.nn.sigmoid(a)
    return (jnp.concatenate([d_hidden * b * (sig * (1.0 + a * (1.0 - sig))), d_hidden * (a * sig)], axis=1),)


def _local_step(x, mem, w_in, small, target, fetch, ship):
    T = x.shape[0]
    tables = _ret_tables(T)
    memb = mem.astype(BF16)

    (h_a, h_b, h_c, h_d, h_e, xb), w_ffn = fetch(
        ("w_ffn_in", "w_ffn_out"),
        lambda rider: _as_host(rider, _mm_fused(
            x, w_in, mode="nn", name="proj_in", extras=[], pass_a=True,
            outs=[(1536, BF16), (1024, F32), (1024, BF16), (1024, F32), (2048, F32), (D_MODEL, BF16)],
            epilogue=_proj_tiles, max_rows=256, rider=rider)))
    (a_sb, r_mat), w_mix = fetch(("w_sb_o", "w_ret_o", "w_mix_o", "w_mem_q", "w_mem_kv", "w_mem_o"),
                                 lambda rider: _sb_fwd(h_a, rider))
    w = {**w_ffn, **w_mix}
    y_gated, o_pre, states = _ret_fwd(h_b, h_c, h_d, tables)
    y_sb = _mm(a_sb, w["w_sb_o"], mode="nn", out_dtype=F32, name="sb_out")
    row_f32, row_bf16 = (D_MODEL, F32), (D_MODEL, BF16)
    ln_outs = [row_bf16, row_f32, (1, F32)]
    y_ret, mix_in = _mm_fused(y_gated, w["w_ret_o"], mode="nn", name="ret_out", extras=[h_e, small["b_gate"], y_sb],
                              outs=[row_f32, row_bf16], epilogue=_gate_mix_tiles)
    x1b, xhat1, rstd1 = _mm_fused(mix_in, w["w_mix_o"], mode="nn", name="mix_out",
                                  extras=[x, small["ln1_g"], small["ln1_b"]], outs=ln_outs, epilogue=_ln_tiles)
    q_m = _mm(x1b, w["w_mem_q"], mode="nn", out_dtype=BF16, name="mem_q")
    kv_m = _mm(memb, w["w_mem_kv"], mode="nn", out_dtype=BF16, name="mem_kv")
    o_m = _xattn_fwd(q_m, kv_m)
    x2b, xhat2, rstd2 = _mm_fused(
        o_m, w["w_mem_o"], mode="nn", name="mem_out", outs=ln_outs, epilogue=_ln_after_ln_tiles,
        extras=[xhat1, small["ln1_g"], small["ln1_b"], small["ln2_g"], small["ln2_b"]])
    f, hidden = _mm_fused(x2b, w["w_ffn_in"], mode="nn", name="ffn_in", extras=[],
                          outs=[(2 * FFN_HIDDEN, BF16), (FFN_HIDDEN, BF16)], epilogue=_swiglu_tiles)
    du_outs, col = [row_f32, row_bf16], D_MODEL
    du3, du3b, loss_cols, d_ln3_g, d_ln3_b = _mm_fused(
        hidden, w["w_ffn_out"], mode="nn", name="ffn_out", outs=du_outs, sums=[col, col, col], epilogue=_ln_loss_tiles,
        extras=[xhat2, small["ln2_g"], small["ln2_b"], small["ln3_g"], small["ln3_b"], target])

    g_ffn_out = _mm(hidden, du3b, mode="tn", out_dtype=BF16, name="g_ffn_out")
    (d_f,) = _mm_fused(du3b, w["w_ffn_out"], mode="nt", name="d_hidden", extras=[f],
                       outs=[(2 * FFN_HIDDEN, BF16)], epilogue=_swiglu_grad_tiles)
    g_ffn_in = _mm(x2b, d_f, mode="tn", out_dtype=BF16, name="g_ffn_in")
    du2, du2b, d_ln2_g, d_ln2_b = ship(
        {"w_ffn_out": g_ffn_out},
        lambda rider: _as_host(rider, _mm_fused(
            d_f, w["w_ffn_in"], mode="nt", name="d_x2", extras=[du3, xhat2, rstd2, small["ln2_g"]], outs=du_outs,
            sums=[col, col], epilogue=_ln_grad_tiles, rider=rider, max_rows=256)))
    g_mem_o = _mm(o_m, du2b, mode="tn", out_dtype=BF16, name="g_mem_o")
    d_om = _mm(du2b, w["w_mem_o"], mode="nt", out_dtype=BF16, name="d_om")
    d_qm, d_kvm = _xattn_bwd(q_m, kv_m, d_om)
    g_mem_q = _mm(x1b, d_qm, mode="tn", out_dtype=BF16, name="g_mem_q")
    g_mem_kv = _mm(memb, d_kvm.astype(BF16), mode="tn", out_dtype=BF16, name="g_mem_kv")
    du1, du1b, d_ln1_g, d_ln1_b = _mm_fused(
        d_qm, w["w_mem_q"], mode="nt", name="d_x1", extras=[du2, xhat1, rstd1, small["ln1_g"]], outs=du_outs,
        sums=[col, col], epilogue=_ln_grad_tiles)
    g_mix_o = _mm(mix_in, du1b, mode="tn", out_dtype=BF16, name="g_mix_o")
    d_ysb, d_yret, d_e, d_b_gate = _mm_fused(
        du1b, w["w_mix_o"], mode="nt", name="d_mix_in", extras=[h_e, small["b_gate"], y_sb, y_ret],
        outs=[row_bf16, row_bf16, (2 * D_MODEL, BF16)], sums=[2 * D_MODEL], epilogue=_gate_mix_grad_tiles)
    g_sb_o = _mm(a_sb, d_ysb, mode="tn", out_dtype=BF16, name="g_sb_o")
    g_ret_o = _mm(y_gated, d_yret, mode="tn", out_dtype=BF16, name="g_ret_o")
    d_asb = _mm(d_ysb, w["w_sb_o"], mode="nt", out_dtype=BF16, name="d_asb")
    d_ygated = _mm(d_yret, w["w_ret_o"], mode="nt", out_dtype=F32, name="d_ygated")
    small_grads = {"b_gate": d_b_gate, "ln1_g": d_ln1_g, "ln1_b": d_ln1_b, "ln2_g": d_ln2_g, "ln2_b": d_ln2_b,
                   "ln3_g": d_ln3_g, "ln3_b": d_ln3_b, "loss_cols": loss_cols}
    d_rq, d_rk, d_c, d_d = ship({"w_mem_kv": g_mem_kv, "w_mem_q": g_mem_q, "w_mem_o": g_mem_o, "w_mix_o": g_mix_o},
                                lambda rider: _ret_bwd(d_ygated, o_pre, states, h_b, h_c, h_d, tables, rider))
    d_q, d_k, d_v = ship({"w_ffn_in": g_ffn_in, "w_ret_o": g_ret_o, "w_sb_o": g_sb_o, "small": small_grads},
                         lambda rider: _sb_bwd(h_a, d_asb, r_mat, rider))
    d_h = [("sb_q", d_q), ("sb_k", d_k), ("sb_v", d_v), ("ret_q", d_rq), ("ret_k", d_rk), ("ret_v", d_c),
           ("ret_g", d_d), ("gate", d_e)]
    g_in = jnp.concatenate([_mm(xb, piece, mode="tn", out_dtype=BF16, name="g_in_" + tag) for tag, piece in d_h],
                           axis=1)
    (d_x,) = ship({"w_in": g_in},
                  lambda rider: _as_host(rider, _mm_fused(
                      [piece for _, piece in d_h], w_in, mode="nt", name="d_x", extras=[du1], outs=[(D_MODEL, F32)],
                      epilogue=_residual_tiles, rider=rider, max_rows=256)))
    return d_x


def _adamw_math(w, g, m, v):
    m = ADAM_B1 * m + (1.0 - ADAM_B1) * g
    v = ADAM_B2 * v + (1.0 - ADAM_B2) * jnp.square(g)
    m_hat = m / (1.0 - ADAM_B1 ** ADAM_STEP)
    v_hat = v / (1.0 - ADAM_B2 ** ADAM_STEP)
    delta = -ADAM_LR * (m_hat / (jnp.sqrt(v_hat) + ADAM_EPS) + ADAM_WD * w)
    return delta, m, v


def _adamw(parts, w, m, v, name):
    R, C = w.shape
    tr = max(t for t in range(16, min(R, 256) + 1, 16) if R % t == 0) if R >= 16 else R

    def body(p_ref, w_ref, m_ref, v_ref, g_ref, d_ref, nm_ref, nv_ref):
        g = p_ref[0].astype(F32)
        for j in range(1, N_DEV):
            g = g + p_ref[j].astype(F32)
        delta, nm, nv = _adamw_math(w_ref[...], g, m_ref[...], v_ref[...])
        g_ref[...] = g
        d_ref[...] = delta
        nm_ref[...] = nm
        nv_ref[...] = nv

    blk = pl.BlockSpec((tr, C), lambda i: (i, 0))
    out = jax.ShapeDtypeStruct((R, C), F32)
    return pl.pallas_call(
        body,
        name=name,
        grid=(R // tr,),
        in_specs=[pl.BlockSpec((N_DEV, tr, C), lambda i: (0, i, 0)), blk, blk, blk],
        out_specs=[blk] * 4,
        out_shape=[out] * 4,
        compiler_params=_params(("parallel",)),
    )(parts, w, m, v)


_SHARD_AXIS = {"w_in": 1, "w_sb_o": 1, "w_ret_o": 0, "w_mix_o": 0, "w_mem_q": 0, "w_mem_kv": 1, "w_mem_o": 0,
               "w_ffn_in": 1, "w_ffn_out": 0}
_MATRICES = tuple(_SHARD_AXIS)
_SMALL = ("b_gate", "ln1_g", "ln1_b", "ln2_g", "ln2_b", "ln3_g", "ln3_b")
_WEIGHT_ORDER = ("w_in", "b_gate", "w_sb_o", "w_ret_o", "w_mix_o", "ln1_g", "ln1_b", "w_mem_q", "w_mem_kv", "w_mem_o",
                 "ln2_g", "ln2_b", "w_ffn_in", "w_ffn_out", "ln3_g", "ln3_b")


def _assemble(name, gathered):
    if _SHARD_AXIS[name] == 0:
        return gathered.reshape(-1, gathered.shape[2])
    return jnp.transpose(gathered, (1, 0, 2)).reshape(gathered.shape[1], -1)


def _to_slots(name, full):
    if _SHARD_AXIS[name] == 0:
        return full.reshape(N_DEV, full.shape[0] // N_DEV, full.shape[1])
    return jnp.transpose(full.reshape(full.shape[0], N_DEV, full.shape[1] // N_DEV), (1, 0, 2))


SMALL_ROWS = 16


def _pack_small(vals):
    return jnp.concatenate([vals["b_gate"].reshape(2, D_MODEL)] + [vals[n] for n in _SMALL[1:]], axis=0)


def _unpack_small(packed):
    out = {"b_gate": packed[0:2].reshape(1, 2 * D_MODEL)}
    for i, n in enumerate(_SMALL[1:]):
        out[n] = packed[2 + i:3 + i]
    return out


def kernel(x, mem, w_in, b_gate, w_sb_o, w_ret_o, w_mix_o, ln1_g, ln1_b, w_mem_q, w_mem_kv, w_mem_o, ln2_g, ln2_b, w_ffn_in, w_ffn_out, ln3_g, ln3_b, loss_target, m_w_in, m_b_gate, m_w_sb_o, m_w_ret_o, m_w_mix_o, m_ln1_g, m_ln1_b, m_w_mem_q, m_w_mem_kv, m_w_mem_o, m_ln2_g, m_ln2_b, m_w_ffn_in, m_w_ffn_out, m_ln3_g, m_ln3_b, v_w_in, v_b_gate, v_w_sb_o, v_w_ret_o, v_w_mix_o, v_ln1_g, v_ln1_b, v_w_mem_q, v_w_mem_kv, v_w_mem_o, v_ln2_g, v_ln2_b, v_w_ffn_in, v_w_ffn_out, v_ln3_g, v_ln3_b):
    weights = dict(w_in=w_in, b_gate=b_gate, w_sb_o=w_sb_o, w_ret_o=w_ret_o, w_mix_o=w_mix_o, ln1_g=ln1_g, ln1_b=ln1_b,
                   w_mem_q=w_mem_q, w_mem_kv=w_mem_kv, w_mem_o=w_mem_o, ln2_g=ln2_g, ln2_b=ln2_b, w_ffn_in=w_ffn_in,
                   w_ffn_out=w_ffn_out, ln3_g=ln3_g, ln3_b=ln3_b)
    mom1 = dict(w_in=m_w_in, b_gate=m_b_gate, w_sb_o=m_w_sb_o, w_ret_o=m_w_ret_o, w_mix_o=m_w_mix_o, ln1_g=m_ln1_g,
                ln1_b=m_ln1_b, w_mem_q=m_w_mem_q, w_mem_kv=m_w_mem_kv, w_mem_o=m_w_mem_o, ln2_g=m_ln2_g, ln2_b=m_ln2_b,
                w_ffn_in=m_w_ffn_in, w_ffn_out=m_w_ffn_out, ln3_g=m_ln3_g, ln3_b=m_ln3_b)
    mom2 = dict(w_in=v_w_in, b_gate=v_b_gate, w_sb_o=v_w_sb_o, w_ret_o=v_w_ret_o, w_mix_o=v_w_mix_o, ln1_g=v_ln1_g,
                ln1_b=v_ln1_b, w_mem_q=v_w_mem_q, w_mem_kv=v_w_mem_kv, w_mem_o=v_w_mem_o, ln2_g=v_ln2_g, ln2_b=v_ln2_b,
                w_ffn_in=v_w_ffn_in, w_ffn_out=v_w_ffn_out, ln3_g=v_ln3_g, ln3_b=v_ln3_b)

    (gathered_in,) = _exchange([weights["w_in"][0].astype(BF16)], False, "gather_w_in")
    received = {}

    def fetch(names, host):
        res = host(_Rider([weights[n][0].astype(BF16) for n in names], False))
        return res[:-1], {n: _assemble(n, g) for n, g in zip(names, res[-1])}

    def ship(grads, host):
        names = list(grads)
        bufs = []
        for n in names:
            if n == "small":
                part = jnp.concatenate([_pack_small(grads[n]), grads[n]["loss_cols"],
                                        jnp.zeros((SMALL_ROWS - 9, D_MODEL), F32)], axis=0)
                bufs.append(jnp.broadcast_to(part[None], (N_DEV,) + part.shape))
            else:
                bufs.append(_to_slots(n, grads[n]).astype(BF16))
        res = host(_Rider(bufs, True))
        received.update(zip(names, res[-1]))
        return res[:-1]

    small = {n: weights[n] for n in _SMALL}
    d_x = _local_step(x[0], mem[0], _assemble("w_in", gathered_in), small, loss_target[0], fetch, ship)

    new = {}
    for n in _MATRICES:
        new[n] = _adamw(received[n], weights[n][0], mom1[n][0], mom2[n][0], "adamw_" + n)
    packed = _adamw(received["small"][:, :8], _pack_small({n: weights[n] for n in _SMALL}),
                    _pack_small({n: mom1[n] for n in _SMALL}), _pack_small({n: mom2[n] for n in _SMALL}), "adamw_small")
    small_new = [_unpack_small(p) for p in packed]
    loss = jnp.sum(received["small"][:, 8]) * (0.5 / D_MODEL)

    outs = [loss, d_x[None]]
    for slot in range(4):
        for n in _WEIGHT_ORDER:
            outs.append(new[n][slot][None] if n in new else small_new[slot][n])
    return tuple(outs)
```

```python
import functools
import math

import jax
import jax.numpy as jnp
from jax import lax
from jax.experimental import pallas as pl
from jax.experimental.pallas import tpu as pltpu

F32 = jnp.float32
BF16 = jnp.bfloat16

N_DEV = 8
D_MODEL = 1024
SB_HEAD_DIM = 64
SB_WIDTH = 512
RET_HEADS = 4
RET_QK_DIM = 128
RET_V_DIM = 256
RET_QK_WIDTH = 512
RET_V_WIDTH = 1024
RET_CHUNK = 128
RET_STEP_CHUNKS = 4
ROPE_BASE = 10000.0
MEM_HEADS = 4
MEM_HEAD_DIM = 256
FFN_HIDDEN = 2816
DN_ALPHA = 2.0 ** 0.25
LN_EPS = 1e-5
ADAM_LR = 0.001
ADAM_B1 = 0.9
ADAM_B2 = 0.999
ADAM_EPS = 1e-08
ADAM_WD = 0.01
ADAM_STEP = 10

VMEM_LIMIT_BYTES = 52 * 1024 * 1024
LANES = 128
SB_KEY_BLOCK = 128
SB_Q_BLOCK = 256
SB_DEAD_LOG = -105.0

MESH_AXES = ("x", "y", "c")


def _pick(dim, prefs):
    for p in prefs:
        if dim % p == 0:
            return p
    return dim


def _params(sem):
    return pltpu.CompilerParams(dimension_semantics=sem, vmem_limit_bytes=VMEM_LIMIT_BYTES)


def _dot(a, b, dims):
    return lax.dot_general(a, b, (dims, ((), ())), preferred_element_type=F32)


_NN = ((1,), (0,))
_NT = ((1,), (1,))
_TN = ((0,), (0,))


def _my_index():
    return 4 * lax.axis_index("x") + 2 * lax.axis_index("y") + lax.axis_index("c")


def _peer(k):
    x, y, c = lax.axis_index("x"), lax.axis_index("y"), lax.axis_index("c")
    bx, by, bc = (k >> 2) & 1, (k >> 1) & 1, k & 1
    px = (1 - x) if bx else x
    py = (1 - y) if by else y
    pc = (1 - c) if bc else c
    return (px, py, pc), 4 * px + 2 * py + pc


class _Rider:
    def __init__(self, bufs, scatter):
        self.bufs, self.scatter, self.n = list(bufs), scatter, len(bufs)
        self.specs = [pl.BlockSpec(memory_space=pl.ANY)] * self.n
        self.out_shape = [jax.ShapeDtypeStruct(b.shape if scatter else (N_DEV,) + b.shape, b.dtype) for b in self.bufs]
        self.scratch = [pltpu.SemaphoreType.DMA((self.n, N_DEV - 1)), pltpu.SemaphoreType.DMA((self.n, N_DEV - 1)),
                        pltpu.SemaphoreType.DMA((self.n,))]

    def _remote(self, ride, a, k, src_ref, slot, to):
        _, dst, (send_sems, recv_sems, _) = ride
        return pltpu.make_async_remote_copy(src_ref=src_ref, dst_ref=dst[a].at[slot], send_sem=send_sems.at[a, k],
                                            recv_sem=recv_sems.at[a, k], device_id=to,
                                            device_id_type=pl.DeviceIdType.MESH)

    def _local(self, ride, a):
        src, dst, (_, _, local_sems) = ride
        me = _my_index()
        return pltpu.make_async_copy(src[a].at[me] if self.scatter else src[a], dst[a].at[me], local_sems.at[a])

    def _direct(self, ride, a):
        src = ride[0]
        me = _my_index()
        out = []
        for k in range(1, N_DEV):
            peer, peer_idx = _peer(k)
            out.append(self._remote(ride, a, k - 1, src[a].at[peer_idx], me, peer))
        return out

    def _two_level(self, ride, a):
        src, dst = ride[0], ride[1]
        x, y, c = lax.axis_index("x"), lax.axis_index("y"), lax.axis_index("c")
        me, sibling = _my_index(), (x, y, 1 - c)
        chips = [(1 - x, y), (x, 1 - y), (1 - x, 1 - y)]
        first = [self._remote(ride, a, 0, src[a], me, sibling)]
        passed, landing = [], [self._remote(ride, a, 0, src[a], me + 1 - 2 * c, sibling)]
        for j, (px, py) in enumerate(chips):
            first.append(self._remote(ride, a, 1 + j, src[a], me, (px, py, c)))
            theirs = 4 * px + 2 * py + c
            passed.append(self._remote(ride, a, 4 + j, dst[a].at[theirs], theirs, sibling))
            landing.append(self._remote(ride, a, 1 + j, src[a], theirs, (px, py, c)))
        for j, (px, py) in enumerate(chips):
            landing.append(self._remote(ride, a, 4 + j, src[a], 4 * px + 2 * py + 1 - c, sibling))
        return first, passed, landing

    def start(self, ride):
        for a in range(self.n):
            self._local(ride, a).start()
            for cp in (self._direct(ride, a) if self.scatter else self._two_level(ride, a)[0]):
                cp.start()

    def finish(self, ride):
        if self.scatter:
            for a in range(self.n):
                for cp in self._direct(ride, a):
                    cp.wait()
                self._local(ride, a).wait()
            return
        levels = [self._two_level(ride, a) for a in range(self.n)]
        for first, passed, landing in levels:
            for j, cp in enumerate(passed):
                landing[1 + j].wait_recv()
                cp.start()
        for a, (first, passed, landing) in enumerate(levels):
            landing[0].wait_recv()
            for cp in landing[4:]:
                cp.wait_recv()
            for cp in first + passed:
                cp.wait_send()
            self._local(ride, a).wait()

    def start_at_first(self, ids, ride):
        first = functools.reduce(jnp.logical_and, [i == 0 for i in ids])

        @pl.when(first)
        def _():
            self.start(ride)

    def wait_at_last(self, ids, grid, ride):
        last = functools.reduce(jnp.logical_and, [i == g - 1 for i, g in zip(ids, grid)])

        @pl.when(last)
        def _():
            self.finish(ride)


def _exchange(bufs, scatter, name):
    rider = _Rider(bufs, scatter)

    def body(*refs):
        ride = (refs[:rider.n], refs[rider.n:2 * rider.n], refs[2 * rider.n:])
        rider.start(ride)
        rider.finish(ride)

    return pl.pallas_call(
        body,
        name=name,
        in_specs=rider.specs,
        out_specs=rider.specs,
        out_shape=rider.out_shape,
        scratch_shapes=rider.scratch,
    )(*rider.bufs)


MM_RESIDENT_B_BYTES = 14 * 1024 * 1024
MM_A_TILE_BYTES = 4 * 1024 * 1024
MM_OUT_TILE_BYTES = 6 * 1024 * 1024


def _mm_tiles(mode, M, N, K, a_bytes, out_bytes):
    if mode != "tn" and K * N * 2 <= MM_RESIDENT_B_BYTES:
        for tm in (1024, 512, 256, 128):
            if M % tm == 0 and tm * K * a_bytes <= MM_A_TILE_BYTES and tm * N * out_bytes <= MM_OUT_TILE_BYTES:
                return tm, N, K
    if mode == "tn":
        return (_pick(M, (1024, 1408, 512, 256, 128)), _pick(N, (1024, 1664, 1408, 512, 256, 128)),
                _pick(K, (2048, 1024, 512, 256, 128)))
    return _pick(M, (1024, 512, 256, 128)), _pick(N, (512, 256, 128)), _pick(K, (1024, 512, 256, 128))


def _mm(a, b, *, mode, out_dtype, name, res=None, res_scale=1.0, rider=None):
    if mode == "nn":
        (M, K), (K2, N) = a.shape, b.shape
    elif mode == "nt":
        (M, K), (N, K2) = a.shape, b.shape
    else:
        (K, M), (K2, N) = a.shape, b.shape
    assert K == K2, (a.shape, b.shape, mode)
    out_bytes = jnp.dtype(out_dtype).itemsize + (4 if res is not None else 0)
    tm, tn, tk = _mm_tiles(mode, M, N, K, a.dtype.itemsize, out_bytes)
    grid = (M // tm, N // tn, K // tk)
    nk = grid[2]
    dims = {"nn": _NN, "nt": _NT, "tn": _TN}[mode]
    n_in = 2 + (res is not None)
    n_ride = rider.n if rider is not None else 0

    def body(*refs):
        a_ref, b_ref = refs[:2]
        r_ref = refs[2] if res is not None else None
        o_ref = refs[n_in + n_ride]
        rest = refs[n_in + 2 * n_ride + 1:]
        acc_ref = rest[0] if nk > 1 else None
        ids = [pl.program_id(d) for d in range(3)]
        if rider is not None:
            ride = (refs[n_in:n_in + n_ride], refs[n_in + n_ride + 1:n_in + 2 * n_ride + 1], rest[-3:])
            rider.start_at_first(ids, ride)
        part = _dot(a_ref[...].astype(BF16), b_ref[...].astype(BF16), dims)

        def finish(total):
            if r_ref is not None:
                total = total + res_scale * r_ref[...]
            o_ref[...] = total.astype(out_dtype)

        if nk == 1:
            finish(part)
        else:
            k = ids[2]

            @pl.when(k == 0)
            def _():
                acc_ref[...] = part

            @pl.when(k > 0)
            def _():
                acc_ref[...] += part

            @pl.when(k == nk - 1)
            def _():
                finish(acc_ref[...])

        if rider is not None:
            rider.wait_at_last(ids, grid, ride)

    if mode == "nn":
        a_spec = pl.BlockSpec((tm, tk), lambda i, j, k: (i, k))
        b_spec = pl.BlockSpec((tk, tn), lambda i, j, k: (k, j))
    elif mode == "nt":
        a_spec = pl.BlockSpec((tm, tk), lambda i, j, k: (i, k))
        b_spec = pl.BlockSpec((tn, tk), lambda i, j, k: (j, k))
    else:
        a_spec = pl.BlockSpec((tk, tm), lambda i, j, k: (k, i))
        b_spec = pl.BlockSpec((tk, tn), lambda i, j, k: (k, j))
    o_spec = pl.BlockSpec((tm, tn), lambda i, j, k: (i, j))
    in_specs = [a_spec, b_spec] + ([o_spec] if res is not None else [])
    args = (a, b) + ((res,) if res is not None else ())
    out_specs, out_shape = [o_spec], [jax.ShapeDtypeStruct((M, N), out_dtype)]
    scratch = [pltpu.VMEM((tm, tn), F32)] if nk > 1 else []
    sem = ("parallel", "parallel", "arbitrary")
    if rider is not None:
        in_specs, args = in_specs + rider.specs, args + tuple(rider.bufs)
        out_specs, out_shape = out_specs + rider.specs, out_shape + rider.out_shape
        scratch = scratch + rider.scratch
        sem = ("arbitrary",) * 3
    outs = pl.pallas_call(
        body,
        name=name,
        grid=grid,
        in_specs=in_specs,
        out_specs=out_specs,
        out_shape=out_shape,
        scratch_shapes=scratch,
        compiler_params=_params(sem),
    )(*args)
    return outs[0] if rider is None else (outs[0], list(outs[1:]))


def _mm_host(a, b, *, rider, **kw):
    out = _mm(a, b, rider=rider, **kw)
    return out if rider is not None else (out, [])


def _as_host(rider, results):
    return results if rider is not None else tuple(results) + ([],)


MM_FUSED_MARGIN_BYTES = 10 * 1024 * 1024
MM_FUSED_MAX_ROWS = 512


def _col_sum_update(acc_ref, val, first):
    part = jnp.sum(val.reshape(val.shape[0] // 8, 8, val.shape[1]), axis=0)

    @pl.when(first)
    def _():
        acc_ref[...] = part

    @pl.when(jnp.logical_not(first))
    def _():
        acc_ref[...] += part


def _mm_fused(a, b, *, mode, name, extras, outs, epilogue, sums=(), rider=None, max_rows=MM_FUSED_MAX_ROWS,
              pass_a=False):
    parts = list(a) if isinstance(a, (list, tuple)) else [a]
    M, K = parts[0].shape[0], sum(p.shape[1] for p in parts)
    if mode == "nn":
        (K2, N), b_dims = b.shape, _NN
    else:
        (N, K2), b_dims = b.shape, _NT
    assert K == K2, (K, b.shape, mode)
    rows = parts + [e for e in extras if e.shape[0] == M]
    per_row = 2 * (sum(e.shape[1] * e.dtype.itemsize for e in rows)
                   + sum(c * jnp.dtype(d).itemsize for c, d in outs)) + 2 * N * 4
    budget = VMEM_LIMIT_BYTES - K * N * 2 - MM_FUSED_MARGIN_BYTES
    tm = next(t for t in (512, 256, 128, 64, 32, 16) if t <= max_rows and M % t == 0 and t * per_row <= budget)
    steps = M // tm
    n_a, n_x, n_o, n_s = len(parts), len(extras), len(outs), len(sums)
    n_ride = rider.n if rider is not None else 0

    def body(*refs):
        a_refs, b_ref = refs[:n_a], refs[n_a]
        x_refs = refs[n_a + 1:n_a + 1 + n_x]
        base = n_a + 1 + n_x + n_ride
        o_refs, s_refs = refs[base:base + n_o], refs[base + n_o:base + n_o + n_s]
        acc_refs = refs[base + n_o + n_s + n_ride:base + n_o + 2 * n_s + n_ride]
        ids = [pl.program_id(0)]
        if rider is not None:
            ride = (refs[n_a + 1 + n_x:base], refs[base + n_o + n_s:base + n_o + n_s + n_ride], refs[-3:])
            rider.start_at_first(ids, ride)
        a_tile = a_refs[0][...]
        a_bf16 = a_tile.astype(BF16) if n_a == 1 else jnp.concatenate([r[...].astype(BF16) for r in a_refs], axis=1)
        prod = _dot(a_bf16, b_ref[...], b_dims)
        tiles = epilogue(prod, *([a_tile] if pass_a else []), *[r[...] for r in x_refs])
        for o_ref, t in zip(o_refs, tiles[:n_o]):
            o_ref[...] = t.astype(o_ref.dtype)
        for acc_ref, t in zip(acc_refs, tiles[n_o:]):
            _col_sum_update(acc_ref, t, ids[0] == 0)
        if n_s:
            @pl.when(ids[0] == steps - 1)
            def _():
                for s_ref, acc_ref in zip(s_refs, acc_refs):
                    s_ref[...] = jnp.sum(acc_ref[...], axis=0, keepdims=True)
        if rider is not None:
            rider.wait_at_last(ids, (steps,), ride)

    in_specs = [pl.BlockSpec((tm, p.shape[1]), lambda i: (i, 0)) for p in parts]
    in_specs.append(pl.BlockSpec(b.shape, lambda i: (0, 0), pipeline_mode=pl.Buffered(1)))
    for e in extras:
        in_specs.append(pl.BlockSpec((tm, e.shape[1]), lambda i: (i, 0)) if e.shape[0] == M
                        else pl.BlockSpec(e.shape, lambda i: (0, 0)))
    out_specs = ([pl.BlockSpec((tm, c), lambda i: (i, 0)) for c, _ in outs]
                 + [pl.BlockSpec((1, c), lambda i: (0, 0)) for c in sums])
    out_shape = ([jax.ShapeDtypeStruct((M, c), d) for c, d in outs]
                 + [jax.ShapeDtypeStruct((1, c), F32) for c in sums])
    args = tuple(parts) + (b,) + tuple(extras)
    scratch = [pltpu.VMEM((8, c), F32) for c in sums]
    if rider is not None:
        in_specs, args = in_specs + rider.specs, args + tuple(rider.bufs)
        out_specs, out_shape = out_specs + rider.specs, out_shape + rider.out_shape
        scratch = scratch + rider.scratch
    res = pl.pallas_call(
        body,
        name=name,
        grid=(steps,),
        in_specs=in_specs,
        out_specs=out_specs,
        out_shape=out_shape,
        scratch_shapes=scratch,
        compiler_params=_params(("arbitrary",) if (n_s or rider is not None) else ("parallel",)),
    )(*args)
    return tuple(res[:n_o + n_s]) + ((list(res[n_o + n_s:]),) if rider is not None else ())


def _pair_rows(blk, lane_is_a):
    zero = jnp.zeros_like(blk)
    return jnp.concatenate([jnp.where(lane_is_a, blk, zero), jnp.where(lane_is_a, zero, blk)], axis=0)


SB_STRIP = 32
SB_FWD_PAIRS = 4
SB_BWD_PAIRS = 2
SB_GROUP = 2


def _pair_lanes(p):
    return slice(p * LANES, (p + 1) * LANES)


def _sb_scan_matrices():
    o = lax.broadcasted_iota(jnp.int32, (2 * LANES, 4 * LANES), 0)
    c = lax.broadcasted_iota(jnp.int32, (2 * LANES, 4 * LANES), 1) & (2 * LANES - 1)
    same = (o >= LANES) == (c >= LANES)
    oo, cc = o & (LANES - 1), c & (LANES - 1)
    return (jnp.where(same & (cc > oo), -1.0, 0.0).astype(BF16), jnp.where(same & (cc < oo), 1.0, 0.0).astype(BF16))


def _sb_causal_masks(tq):
    d = lax.broadcasted_iota(jnp.int32, (tq // SB_KEY_BLOCK, SB_KEY_BLOCK, tq), 0)
    k = lax.broadcasted_iota(jnp.int32, (tq // SB_KEY_BLOCK, SB_KEY_BLOCK, tq), 1)
    t = lax.broadcasted_iota(jnp.int32, (tq // SB_KEY_BLOCK, SB_KEY_BLOCK, tq), 2)
    return jnp.where(d * SB_KEY_BLOCK + k < t, 1.0, 0.0).astype(F32)


def _sb_log_terms(z):
    minus_abs = lax.bitcast_convert_type(lax.bitcast_convert_type(z, jnp.uint32) | jnp.uint32(0x80000000), F32)
    spent = jnp.maximum(z, 0.0) + jnp.log(1.0 + jnp.exp(minus_abs))
    return spent, z - spent


def _sb_store_split(ref, strip, val, cols):
    hi = val.astype(BF16)
    ref[pl.ds(strip * SB_STRIP, SB_STRIP), cols] = hi
    ref[pl.ds(2 * LANES + strip * SB_STRIP, SB_STRIP), cols] = (val - hi.astype(F32)).astype(BF16)


def _sb_lanes(tq, diag):
    if diag == "left":
        return 0, tq // 2
    first = 0 if diag is None else diag * SB_KEY_BLOCK
    return first, tq - first


def _lane_add(full, part, lanes):
    first, width = lanes
    pieces = [full[:, :first]] if first else []
    pieces.append(full[:, first:first + width] + part)
    if first + width < full.shape[1]:
        pieces.append(full[:, first + width:])
    return pieces[0] if len(pieces) == 1 else jnp.concatenate(pieces, axis=1)


def _sb_fwd(h_a, rider=None):
    assert SB_FWD_PAIRS == 4
    T = h_a.shape[0]
    tq = _pick(T, (SB_Q_BLOCK, SB_KEY_BLOCK))
    nq, per_q, nkb = T // tq, tq // SB_KEY_BLOCK, T // SB_KEY_BLOCK
    assert per_q % SB_GROUP == 0
    n_strips = 2 * LANES // SB_STRIP
    n_ride = rider.n if rider is not None else 0
    after_m, _ = _sb_scan_matrices()
    causal_m = _sb_causal_masks(tq)
    pairs = SB_FWD_PAIRS

    def body(*refs):
        q_ref, k_ref, v_ref, after_ref, causal_ref = refs[:5]
        a_ref, r_ref, n_ref = refs[5 + n_ride:8 + n_ride]
        z_ref, lb_ref, split_ref, w_ref = refs[8 + 2 * n_ride:12 + 2 * n_ride]
        ids = [pl.program_id(0)]
        if rider is not None:
            ride = (refs[5:5 + n_ride], refs[8 + n_ride:8 + 2 * n_ride], refs[-3:])
            rider.start_at_first(ids, ride)
        i = ids[0]
        q_t = [(q_ref[:, _pair_lanes(p)].astype(F32).T * (SB_HEAD_DIM ** -0.5)).astype(BF16) for p in range(pairs)]
        lane_is_a = lax.broadcasted_iota(jnp.int32, (SB_KEY_BLOCK, LANES), 1) < SB_HEAD_DIM

        def tiles(kbs, diags, carry):
            nb = len(kbs)
            lanes = [_sb_lanes(tq, d) for d in diags]
            cols = [slice(first, first + width) for first, width in lanes]
            acc_t, ra, rb = [list(c) for c in carry]
            ks = [pl.multiple_of(kb * SB_KEY_BLOCK, SB_KEY_BLOCK) for kb in kbs]
            slot = lambda p, b: p * nb + b

            def causal(b, s):
                return causal_ref[diags[b], pl.ds((s * SB_STRIP) % SB_KEY_BLOCK, SB_STRIP), cols[b]]

            vv = {}
            for b in range(nb):
                for p in range(pairs):
                    kk = _pair_rows(k_ref[pl.ds(ks[b], SB_KEY_BLOCK), _pair_lanes(p)], lane_is_a)
                    vv[p, b] = _pair_rows(v_ref[pl.ds(ks[b], SB_KEY_BLOCK), _pair_lanes(p)], lane_is_a)
                    z_ref[slot(p, b), :, cols[b]] = _dot(kk, q_t[p][:, cols[b]], _NN)
            sums = {}
            for b in range(nb):
                for p in range(pairs):
                    part = [jnp.zeros((8, lanes[b][1]), F32), jnp.zeros((8, lanes[b][1]), F32)]
                    for s in range(n_strips):
                        rows = pl.ds(s * SB_STRIP, SB_STRIP)
                        spent, log_beta = _sb_log_terms(z_ref[slot(p, b), rows, cols[b]])
                        lb_ref[slot(p, b), rows, cols[b]] = log_beta
                        if isinstance(diags[b], int):
                            spent = spent * causal(b, s)
                        _sb_store_split(split_ref.at[slot(p, b)], s, spent, cols[b])
                        head = (s * SB_STRIP) // SB_KEY_BLOCK
                        part[head] = part[head] + jnp.sum(spent.reshape(SB_STRIP // 8, 8, lanes[b][1]), axis=0)
                    sums[p, b] = part
            for b in range(nb):
                for p in range(pairs):
                    z_ref[slot(p, b), :, cols[b]] = _dot(after_ref[...], split_ref[slot(p, b), :, cols[b]], _NN)
            for b in range(nb):
                for p in range(pairs):
                    for s in range(n_strips):
                        rows = pl.ds(s * SB_STRIP, SB_STRIP)
                        start = (ra[p] if (s * SB_STRIP) < SB_KEY_BLOCK else rb[p])[:, cols[b]]
                        w = jnp.exp(lb_ref[slot(p, b), rows, cols[b]] + z_ref[slot(p, b), rows, cols[b]] + start)
                        if isinstance(diags[b], int):
                            w = w * causal(b, s)
                        w_ref[slot(p, b), rows, cols[b]] = w.astype(BF16)
                    r_ref[2 * p, kbs[b]] = ra[p]
                    r_ref[2 * p + 1, kbs[b]] = rb[p]
                    ra[p] = _lane_add(ra[p], -jnp.sum(sums[p, b][0], axis=0, keepdims=True), lanes[b])
                    rb[p] = _lane_add(rb[p], -jnp.sum(sums[p, b][1], axis=0, keepdims=True), lanes[b])
            for b in range(nb):
                for p in range(pairs):
                    acc_t[p] = _lane_add(acc_t[p], _dot(vv[p, b], w_ref[slot(p, b), :, cols[b]], _TN), lanes[b])
            return tuple(acc_t), tuple(ra), tuple(rb)

        carry = (tuple(jnp.zeros((LANES, tq), F32) for _ in range(pairs)),
                 tuple(jnp.zeros((1, tq), F32) for _ in range(pairs)),
                 tuple(jnp.zeros((1, tq), F32) for _ in range(pairs)))
        own = list(reversed(range(per_q)))
        n_full = i * per_q
        carry = lax.cond(
            i > 0,
            lambda cc: tiles([n_full + d for d in own] + [n_full - 1 - b for b in range(SB_GROUP)],
                             own + [None] * SB_GROUP, cc),
            lambda cc: tiles([n_full + d for d in own], own, cc), carry)
        first_walked = jnp.where(i > 0, SB_GROUP, 0).astype(jnp.int32)

        def top_of(sums_a, sums_b, first):
            return jnp.max(functools.reduce(jnp.maximum, [r[:, first:] for r in sums_a + sums_b]))

        def alive(c):
            return jnp.logical_and(c[0] < n_full, top_of(c[2], c[3], 0) > SB_DEAD_LOG)

        def step(c):
            kbs = [n_full - 1 - c[0] - b for b in range(SB_GROUP)]
            return (c[0] + SB_GROUP,) + lax.cond(
                top_of(c[2], c[3], tq // 2) > SB_DEAD_LOG,
                lambda cc: tiles(kbs, [None] * SB_GROUP, cc), lambda cc: tiles(kbs, ["left"] * SB_GROUP, cc), c[1:])

        walked, acc_t, _, _ = lax.while_loop(alive, step, (first_walked,) + carry)
        for p in range(pairs):
            a_ref[:, _pair_lanes(p)] = acc_t[p].T.astype(BF16)
        n_ref[...] = jnp.zeros(n_ref.shape, F32) + walked.astype(F32)
        if rider is not None:
            rider.wait_at_last(ids, (nq,), ride)

    wide = pairs * LANES
    in_specs = [pl.BlockSpec((tq, wide), lambda i: (i, 0)),
                pl.BlockSpec((T, wide), lambda i: (0, 1), pipeline_mode=pl.Buffered(1)),
                pl.BlockSpec((T, wide), lambda i: (0, 2), pipeline_mode=pl.Buffered(1)),
                pl.BlockSpec(after_m.shape, lambda i: (0, 0), pipeline_mode=pl.Buffered(1)),
                pl.BlockSpec(causal_m.shape, lambda i: (0, 0, 0), pipeline_mode=pl.Buffered(1))]
    out_specs = [pl.BlockSpec((tq, wide), lambda i: (i, 0)),
                 pl.BlockSpec((2 * pairs, nkb, 1, tq), lambda i: (0, 0, 0, i)),
                 pl.BlockSpec((1, 8, LANES), lambda i: (i, 0, 0))]
    out_shape = [jax.ShapeDtypeStruct((T, SB_WIDTH), BF16), jax.ShapeDtypeStruct((2 * pairs, nkb, 1, T), F32),
                 jax.ShapeDtypeStruct((nq, 8, LANES), F32)]
    args = (h_a, h_a, h_a, after_m, causal_m)
    slots = pairs * (per_q + SB_GROUP)
    scratch = [pltpu.VMEM((slots, 2 * LANES, tq), F32), pltpu.VMEM((slots, 2 * LANES, tq), F32),
               pltpu.VMEM((slots, 4 * LANES, tq), BF16), pltpu.VMEM((slots, 2 * LANES, tq), BF16)]
    if rider is not None:
        in_specs, args = in_specs + rider.specs, args + tuple(rider.bufs)
        out_specs, out_shape = out_specs + rider.specs, out_shape + rider.out_shape
        scratch = scratch + rider.scratch
    outs = pl.pallas_call(
        body,
        name="sb_fwd",
        grid=(nq,),
        in_specs=in_specs,
        out_specs=out_specs,
        out_shape=out_shape,
        scratch_shapes=scratch,
        compiler_params=_params(("arbitrary",)),
    )(*args)
    return outs[0], (outs[1], outs[2]), list(outs[3:])


def _sb_bwd(h_a, d_out, saved, rider=None):
    r_mat, walked_blocks = saved
    T = h_a.shape[0]
    tq = _pick(T, (SB_Q_BLOCK, SB_KEY_BLOCK))
    nq, per_q, nkb = T // tq, tq // SB_KEY_BLOCK, T // SB_KEY_BLOCK
    n_strips = 2 * LANES // SB_STRIP
    after_m, before_m = _sb_scan_matrices()
    causal_m = _sb_causal_masks(tq)
    pairs = SB_BWD_PAIRS
    groups = 4 // pairs
    n_ride = rider.n if rider is not None else 0

    def body(*refs):
        q_ref, k_ref, v_ref, do_ref, r_ref, n_ref, after_ref, before_ref, causal_ref = refs[:9]
        dq_ref, dk_ref, dv_ref = refs[9 + n_ride:12 + n_ride]
        z_ref, lb_ref, split_ref, w_ref, da_ref, dz_ref = refs[12 + 2 * n_ride:18 + 2 * n_ride]
        ids = [pl.program_id(0), pl.program_id(1)]
        if rider is not None:
            ride = (refs[9:9 + n_ride], refs[12 + n_ride:12 + 2 * n_ride], refs[-3:])
            rider.start_at_first(ids, ride)
        i = ids[1]

        @pl.when(i == 0)
        def _():
            dk_ref[...] = jnp.zeros_like(dk_ref)
            dv_ref[...] = jnp.zeros_like(dv_ref)

        scale = SB_HEAD_DIM ** -0.5
        q = [q_ref[:, _pair_lanes(p)] for p in range(pairs)]
        d_o = [do_ref[:, _pair_lanes(p)] for p in range(pairs)]
        q_t = [(x.astype(F32).T * scale).astype(BF16) for x in q]
        do_t = [x.astype(F32).T.astype(BF16) for x in d_o]
        lane_is_a = lax.broadcasted_iota(jnp.int32, (SB_KEY_BLOCK, LANES), 1) < SB_HEAD_DIM

        def tiles(kbs, diags, carry):
            nb = len(kbs)
            lanes = [_sb_lanes(tq, d) for d in diags]
            cols = [slice(first, first + width) for first, width in lanes]
            dq_t, ca, cb = [list(c) for c in carry]
            ks = [pl.multiple_of(kb * SB_KEY_BLOCK, SB_KEY_BLOCK) for kb in kbs]
            slot = lambda p, b: p * nb + b

            def causal(b, s):
                return causal_ref[diags[b], pl.ds((s * SB_STRIP) % SB_KEY_BLOCK, SB_STRIP), cols[b]]

            kk, vv = {}, {}
            for b in range(nb):
                for p in range(pairs):
                    kk[p, b] = _pair_rows(k_ref[pl.ds(ks[b], SB_KEY_BLOCK), _pair_lanes(p)], lane_is_a)
                    vv[p, b] = _pair_rows(v_ref[pl.ds(ks[b], SB_KEY_BLOCK), _pair_lanes(p)], lane_is_a)
                    z_ref[slot(p, b), :, cols[b]] = _dot(kk[p, b], q_t[p][:, cols[b]], _NN)
            for b in range(nb):
                for p in range(pairs):
                    for s in range(n_strips):
                        rows = pl.ds(s * SB_STRIP, SB_STRIP)
                        spent, log_beta = _sb_log_terms(z_ref[slot(p, b), rows, cols[b]])
                        lb_ref[slot(p, b), rows, cols[b]] = log_beta
                        if isinstance(diags[b], int):
                            spent = spent * causal(b, s)
                        _sb_store_split(split_ref.at[slot(p, b)], s, spent, cols[b])
            for b in range(nb):
                for p in range(pairs):
                    z_ref[slot(p, b), :, cols[b]] = _dot(after_ref[...], split_ref[slot(p, b), :, cols[b]], _NN)
                    da_ref[slot(p, b), :, cols[b]] = _dot(vv[p, b], do_t[p][:, cols[b]], _NN)
            sums = {}
            for b in range(nb):
                for p in range(pairs):
                    part = [jnp.zeros((8, lanes[b][1]), F32), jnp.zeros((8, lanes[b][1]), F32)]
                    for s in range(n_strips):
                        rows = pl.ds(s * SB_STRIP, SB_STRIP)
                        start = r_ref[2 * p + (s * SB_STRIP) // SB_KEY_BLOCK, kbs[b]][:, cols[b]]
                        w = jnp.exp(lb_ref[slot(p, b), rows, cols[b]] + z_ref[slot(p, b), rows, cols[b]] + start)
                        if isinstance(diags[b], int):
                            w = w * causal(b, s)
                        w_ref[slot(p, b), rows, cols[b]] = w.astype(BF16)
                        da = da_ref[slot(p, b), rows, cols[b]] * w
                        da_ref[slot(p, b), rows, cols[b]] = da
                        _sb_store_split(split_ref.at[slot(p, b)], s, da, cols[b])
                        head = (s * SB_STRIP) // SB_KEY_BLOCK
                        part[head] = part[head] + jnp.sum(da.reshape(SB_STRIP // 8, 8, lanes[b][1]), axis=0)
                    sums[p, b] = part
            for b in range(nb):
                for p in range(pairs):
                    z_ref[slot(p, b), :, cols[b]] = _dot(before_ref[...], split_ref[slot(p, b), :, cols[b]], _NN)
            for b in range(nb):
                for p in range(pairs):
                    for s in range(n_strips):
                        rows = pl.ds(s * SB_STRIP, SB_STRIP)
                        base = (ca[p] if (s * SB_STRIP) < SB_KEY_BLOCK else cb[p])[:, cols[b]]
                        sig = jnp.exp(lb_ref[slot(p, b), rows, cols[b]])
                        dz = (da_ref[slot(p, b), rows, cols[b]] * (1.0 - sig)
                              - (z_ref[slot(p, b), rows, cols[b]] + base) * sig)
                        if isinstance(diags[b], int):
                            dz = dz * causal(b, s)
                        dz_ref[slot(p, b), rows, cols[b]] = (dz * scale).astype(BF16)
                    ca[p] = _lane_add(ca[p], jnp.sum(sums[p, b][0], axis=0, keepdims=True), lanes[b])
                    cb[p] = _lane_add(cb[p], jnp.sum(sums[p, b][1], axis=0, keepdims=True), lanes[b])
            for b in range(nb):
                for p in range(pairs):
                    dq_t[p] = _lane_add(dq_t[p], _dot(kk[p, b], dz_ref[slot(p, b), :, cols[b]], _TN), lanes[b])
                    dkk = _dot(dz_ref[slot(p, b), :, cols[b]], q[p][cols[b], :], _NN)
                    dvv = _dot(w_ref[slot(p, b), :, cols[b]], d_o[p][cols[b], :], _NN)
                    here = (pl.ds(ks[b], SB_KEY_BLOCK), _pair_lanes(p))
                    dk_ref[here] += jnp.where(lane_is_a, dkk[:SB_KEY_BLOCK], dkk[SB_KEY_BLOCK:])
                    dv_ref[here] += jnp.where(lane_is_a, dvv[:SB_KEY_BLOCK], dvv[SB_KEY_BLOCK:])
            return tuple(dq_t), tuple(ca), tuple(cb)

        n_full = i * per_q
        groups_walked = jnp.clip(jnp.max(n_ref[...]).astype(jnp.int32), 0, n_full) // SB_GROUP
        carry = (tuple(jnp.zeros((LANES, tq), F32) for _ in range(pairs)),
                 tuple(jnp.zeros((1, tq), F32) for _ in range(pairs)),
                 tuple(jnp.zeros((1, tq), F32) for _ in range(pairs)))

        def below(j, c):
            kbs = [n_full - (groups_walked - j) * SB_GROUP + b for b in range(SB_GROUP)]
            starts = [r_ref[h, kbs[-1]][:, tq // 2:] for h in range(2 * pairs)]
            reaches = jnp.max(functools.reduce(jnp.maximum, starts)) > SB_DEAD_LOG
            return lax.cond(reaches, lambda cc: tiles(kbs, [None] * SB_GROUP, cc),
                            lambda cc: tiles(kbs, ["left"] * SB_GROUP, cc), c)

        carry = lax.fori_loop(0, groups_walked, below, carry)
        own = list(range(per_q))
        carry = tiles([i * per_q + d for d in own], own, carry)
        for p in range(pairs):
            dq_ref[:, _pair_lanes(p)] = carry[0][p].T.astype(BF16)
        if rider is not None:
            rider.wait_at_last(ids, (groups, nq), ride)

    wide = pairs * LANES
    mat = pl.BlockSpec(after_m.shape, lambda g, i: (0, 0), pipeline_mode=pl.Buffered(1))
    in_specs = [pl.BlockSpec((tq, wide), lambda g, i: (i, g)),
                pl.BlockSpec((T, wide), lambda g, i: (0, groups + g), pipeline_mode=pl.Buffered(1)),
                pl.BlockSpec((T, wide), lambda g, i: (0, 2 * groups + g), pipeline_mode=pl.Buffered(1)),
                pl.BlockSpec((tq, wide), lambda g, i: (i, g)),
                pl.BlockSpec((2 * pairs, nkb, 1, tq), lambda g, i: (g, 0, 0, i)),
                pl.BlockSpec((1, 8, LANES), lambda g, i: (i, 0, 0)),
                mat, mat,
                pl.BlockSpec(causal_m.shape, lambda g, i: (0, 0, 0), pipeline_mode=pl.Buffered(1))]
    out_specs = [pl.BlockSpec((tq, wide), lambda g, i: (i, g)),
                 pl.BlockSpec((T, wide), lambda g, i: (0, g)),
                 pl.BlockSpec((T, wide), lambda g, i: (0, g))]
    out_shape = [jax.ShapeDtypeStruct((T, SB_WIDTH), BF16), jax.ShapeDtypeStruct((T, SB_WIDTH), F32),
                 jax.ShapeDtypeStruct((T, SB_WIDTH), F32)]
    args = (h_a, h_a, h_a, d_out, r_mat, walked_blocks, after_m, before_m, causal_m)
    slots = pairs * max(per_q, SB_GROUP)
    scratch = [pltpu.VMEM((slots, 2 * LANES, tq), F32), pltpu.VMEM((slots, 2 * LANES, tq), F32),
               pltpu.VMEM((slots, 4 * LANES, tq), BF16), pltpu.VMEM((slots, 2 * LANES, tq), BF16),
               pltpu.VMEM((slots, 2 * LANES, tq), F32), pltpu.VMEM((slots, 2 * LANES, tq), BF16)]
    if rider is not None:
        in_specs, args = in_specs + rider.specs, args + tuple(rider.bufs)
        out_specs, out_shape = out_specs + rider.specs, out_shape + rider.out_shape
        scratch = scratch + rider.scratch
    outs = pl.pallas_call(
        body,
        name="sb_bwd",
        grid=(groups, nq),
        in_specs=in_specs,
        out_specs=out_specs,
        out_shape=out_shape,
        scratch_shapes=scratch,
        compiler_params=_params(("arbitrary", "arbitrary") if rider is not None else ("parallel", "arbitrary")),
    )(*args)
    return outs[0], outs[1], outs[2], list(outs[3:])


def _ret_tables(T):
    half = RET_QK_DIM // 2
    inv = 1.0 / (ROPE_BASE ** (jnp.arange(half, dtype=F32) / half))
    ang = jnp.arange(T, dtype=F32)[:, None] * inv[None, :]
    cos, sin = jnp.cos(ang), jnp.sin(ang)
    cos_t = jnp.concatenate([cos, cos], axis=1)
    sin_t = jnp.concatenate([-sin, sin], axis=1)
    log_gamma = jnp.log1p(-jnp.exp2(-5.0 - jnp.arange(RET_HEADS, dtype=F32)))
    idx = jnp.arange(RET_CHUNK, dtype=F32)
    rel = idx[:, None] - idx[None, :]
    decay = jnp.where(rel[None] >= 0, jnp.exp(log_gamma[:, None, None] * jnp.maximum(rel, 0.0)[None]), 0.0)
    k_decay = jnp.exp(log_gamma[None, :] * (RET_CHUNK - 1.0 - idx)[:, None])
    q_decay = jnp.exp(log_gamma[None, :] * (idx + 1.0)[:, None])
    chunk_decay = jnp.exp(log_gamma * RET_CHUNK)
    k_dec = jnp.broadcast_to(k_decay.T[:, :, None], (RET_HEADS, RET_CHUNK, LANES))
    q_dec = jnp.broadcast_to(q_decay.T[:, :, None], (RET_HEADS, RET_CHUNK, LANES))
    c_dec = jnp.broadcast_to(chunk_decay[:, None, None], (RET_HEADS, 8, LANES))
    return cos_t, sin_t, decay, k_dec, q_dec, c_dec


def _rotary(x, cos_t, sin_t):
    return x * cos_t + pltpu.roll(x, RET_QK_DIM // 2, 1) * sin_t


def _rotary_transpose(dy, cos_t, sin_t):
    return dy * cos_t + pltpu.roll(dy * sin_t, RET_QK_DIM // 2, 1)


def _head_norm(o):
    mu = jnp.mean(o, axis=1, keepdims=True)
    cen = o - mu
    var = jnp.mean(cen * cen, axis=1, keepdims=True)
    rstd = lax.rsqrt(var + LN_EPS)
    return cen * rstd, rstd


def _ret_specs(steps, per_step, reverse):
    def n_of(n):
        return (steps - 1 - n) if reverse else n

    rows = per_step * RET_CHUNK
    q_spec = pl.BlockSpec((rows, RET_QK_WIDTH), lambda n: (n_of(n), 0))
    k_spec = pl.BlockSpec((rows, RET_QK_WIDTH), lambda n: (n_of(n), 1))
    vv = pl.BlockSpec((rows, RET_V_WIDTH), lambda n: (n_of(n), 0))
    pos = pl.BlockSpec((rows, LANES), lambda n: (n_of(n), 0))
    per_head = pl.BlockSpec((RET_HEADS, RET_CHUNK, LANES), lambda n: (0, 0, 0))
    c_dec = pl.BlockSpec((RET_HEADS, 8, LANES), lambda n: (0, 0, 0))
    state = pl.BlockSpec((RET_HEADS, per_step, RET_QK_DIM, RET_V_DIM), lambda n: (0, n_of(n), 0, 0))
    return q_spec, k_spec, vv, pos, per_head, c_dec, state


def _qk_cols(h):
    return slice(h * RET_QK_DIM, (h + 1) * RET_QK_DIM)


def _v_cols(h):
    return slice(h * RET_V_DIM, (h + 1) * RET_V_DIM)


def _ret_fwd(h_b, h_c, h_d, tables):
    T = h_b.shape[0]
    nc = T // RET_CHUNK
    per_step = _pick(nc, (RET_STEP_CHUNKS, 1))
    steps = nc // per_step
    q_spec, k_spec, vv, pos, per_head, c_dec, state = _ret_specs(steps, per_step, False)

    def body(q_ref, k_ref, v_ref, g_ref, cos_ref, sin_ref, dec_ref, kd_ref, qd_ref, cd_ref,
             y_ref, o_ref, st_ref, state_ref):
        @pl.when(pl.program_id(0) == 0)
        def _():
            state_ref[...] = jnp.zeros_like(state_ref)

        for c in range(per_step):
            rows = pl.ds(c * RET_CHUNK, RET_CHUNK)
            cos_t, sin_t = cos_ref[rows, :], sin_ref[rows, :]
            for h in range(RET_HEADS):
                q = _rotary(q_ref[rows, _qk_cols(h)], cos_t, sin_t) * (RET_QK_DIM ** -0.5)
                k = _rotary(k_ref[rows, _qk_cols(h)], cos_t, sin_t)
                v = v_ref[rows, _v_cols(h)]
                prev = state_ref[h]
                scores = _dot(q.astype(BF16), k.astype(BF16), _NT) * dec_ref[h]
                inner = _dot(scores.astype(BF16), v, _NN)
                cross = _dot((q * qd_ref[h]).astype(BF16), prev.astype(BF16), _NN)
                o = inner + cross
                st_ref[h, c] = prev
                kv = _dot((k * kd_ref[h]).astype(BF16), v, _TN)
                state_ref[h] = prev * cd_ref[h, 0:1, 0:1] + kv
                o_ref[rows, _v_cols(h)] = o
                normed, _ = _head_norm(o)
                gate = g_ref[rows, _v_cols(h)]
                y_ref[rows, _v_cols(h)] = (gate * jax.nn.sigmoid(gate) * normed).astype(BF16)

    return pl.pallas_call(
        body,
        name="ret_fwd",
        grid=(steps,),
        in_specs=[q_spec, k_spec, vv, vv, pos, pos, per_head, per_head, per_head, c_dec],
        out_specs=[vv, vv, state],
        out_shape=[jax.ShapeDtypeStruct((T, RET_V_WIDTH), BF16),
                   jax.ShapeDtypeStruct((T, RET_V_WIDTH), F32),
                   jax.ShapeDtypeStruct((RET_HEADS, nc, RET_QK_DIM, RET_V_DIM), F32)],
        scratch_shapes=[pltpu.VMEM((RET_HEADS, RET_QK_DIM, RET_V_DIM), F32)],
        compiler_params=_params(("arbitrary",)),
    )(h_b, h_b, h_c, h_d, *tables)


def _ret_bwd(d_y, o_pre, states, h_b, h_c, h_d, tables, rider=None):
    T = h_b.shape[0]
    nc = T // RET_CHUNK
    per_step = _pick(nc, (RET_STEP_CHUNKS, 1))
    steps = nc // per_step
    q_spec, k_spec, vv, pos, per_head, c_dec, state = _ret_specs(steps, per_step, True)
    n_ride = rider.n if rider is not None else 0

    def body(*refs):
        (dy_ref, o_ref, st_ref, q_ref, k_ref, v_ref, g_ref, cos_ref, sin_ref, dec_ref, kd_ref, qd_ref,
         cd_ref) = refs[:13]
        dq_ref, dk_ref, dv_ref, dg_ref = refs[13 + n_ride:17 + n_ride]
        carry_ref = refs[17 + 2 * n_ride]
        ids = [pl.program_id(0)]
        if rider is not None:
            ride = (refs[13:13 + n_ride], refs[17 + n_ride:17 + 2 * n_ride], refs[-3:])
            rider.start_at_first(ids, ride)

        @pl.when(ids[0] == 0)
        def _():
            carry_ref[...] = jnp.zeros_like(carry_ref)

        scale = RET_QK_DIM ** -0.5
        for c in reversed(range(per_step)):
            rows = pl.ds(c * RET_CHUNK, RET_CHUNK)
            cos_t, sin_t = cos_ref[rows, :], sin_ref[rows, :]
            for h in range(RET_HEADS):
                q = _rotary(q_ref[rows, _qk_cols(h)], cos_t, sin_t) * scale
                k = _rotary(k_ref[rows, _qk_cols(h)], cos_t, sin_t)
                v = v_ref[rows, _v_cols(h)]
                decay, k_dec, q_dec = dec_ref[h], kd_ref[h], qd_ref[h]
                chunk_decay = cd_ref[h, 0:1, 0:1]
                state = st_ref[h, c].astype(BF16)
                later = carry_ref[h]
                later_b = later.astype(BF16)

                gate = g_ref[rows, _v_cols(h)]
                sig = jax.nn.sigmoid(gate)
                silu = gate * sig
                normed, rstd = _head_norm(o_ref[rows, _v_cols(h)])
                d_y = dy_ref[rows, _v_cols(h)]
                dg_ref[rows, _v_cols(h)] = (d_y * normed * (sig * (1.0 + gate * (1.0 - sig)))).astype(BF16)
                d_n = d_y * silu
                d_o = rstd * (d_n - jnp.mean(d_n, axis=1, keepdims=True)
                              - normed * jnp.mean(d_n * normed, axis=1, keepdims=True))
                d_ob = d_o.astype(BF16)

                qb, kb = q.astype(BF16), k.astype(BF16)
                qd_b, kd_b = (q * q_dec).astype(BF16), (k * k_dec).astype(BF16)
                scores = _dot(qb, kb, _NT) * decay
                d_scores = (_dot(d_ob, v, _NT) * decay).astype(BF16)
                dq = _dot(d_scores, kb, _NN) + _dot(d_ob, state, _NT) * q_dec
                dk = _dot(d_scores, qb, _TN) + _dot(v, later_b, _NT) * k_dec
                dv = _dot(scores.astype(BF16), d_ob, _TN) + _dot(kd_b, later_b, _NN)
                carry_ref[h] = _dot(qd_b, d_ob, _TN) + chunk_decay * later
                dq_ref[rows, _qk_cols(h)] = _rotary_transpose(dq * scale, cos_t, sin_t).astype(BF16)
                dk_ref[rows, _qk_cols(h)] = _rotary_transpose(dk, cos_t, sin_t).astype(BF16)
                dv_ref[rows, _v_cols(h)] = dv.astype(BF16)
        if rider is not None:
            rider.wait_at_last(ids, (steps,), ride)

    qk_out = pl.BlockSpec((per_step * RET_CHUNK, RET_QK_WIDTH), lambda n: (steps - 1 - n, 0))
    in_specs = [vv, vv, state, q_spec, k_spec, vv, vv, pos, pos, per_head, per_head, per_head, c_dec]
    out_specs = [qk_out, qk_out, vv, vv]
    out_shape = [jax.ShapeDtypeStruct((T, RET_QK_WIDTH), BF16), jax.ShapeDtypeStruct((T, RET_QK_WIDTH), BF16),
                 jax.ShapeDtypeStruct((T, RET_V_WIDTH), BF16), jax.ShapeDtypeStruct((T, RET_V_WIDTH), BF16)]
    args = (d_y, o_pre, states, h_b, h_b, h_c, h_d) + tuple(tables)
    scratch = [pltpu.VMEM((RET_HEADS, RET_QK_DIM, RET_V_DIM), F32)]
    if rider is not None:
        in_specs, args = in_specs + rider.specs, args + tuple(rider.bufs)
        out_specs, out_shape = out_specs + rider.specs, out_shape + rider.out_shape
        scratch = scratch + rider.scratch
    outs = pl.pallas_call(
        body,
        name="ret_bwd",
        grid=(steps,),
        in_specs=in_specs,
        out_specs=out_specs,
        out_shape=out_shape,
        scratch_shapes=scratch,
        compiler_params=_params(("arbitrary",)),
    )(*args)
    return outs[0], outs[1], outs[2], outs[3], list(outs[4:])


def _proj_tiles(h, x):
    return h[:, 0:1536], h[:, 1536:2560], h[:, 2560:3584], h[:, 3584:4608], h[:, 4608:6656], x


def _gate_mix_tiles(y_ret, h_e, b_gate, y_sb):
    gates = jax.nn.sigmoid(h_e + b_gate)
    return y_ret, gates[:, :D_MODEL] * y_sb + gates[:, D_MODEL:] * y_ret


def _gate_mix_grad_tiles(d_mix, h_e, b_gate, y_sb, y_ret):
    y_ret = y_ret.astype(F32)
    gates = jax.nn.sigmoid(h_e + b_gate)
    g0, g1 = gates[:, :D_MODEL], gates[:, D_MODEL:]
    d_e = jnp.concatenate([d_mix * y_sb * g0 * (1.0 - g0), d_mix * y_ret * g1 * (1.0 - g1)], axis=1)
    return d_mix * g0, d_mix * g1, d_e, d_e


def _ln_stats(u):
    mu = jnp.mean(u, axis=1, keepdims=True)
    cen = u - mu
    var = jnp.mean(cen * cen, axis=1, keepdims=True)
    rstd = lax.rsqrt(var + LN_EPS)
    return cen * rstd, rstd


def _ln_input_grad(d_out, gain, xhat, rstd):
    d_hat = d_out * gain
    return rstd * (d_hat - jnp.mean(d_hat, axis=1, keepdims=True)
                   - xhat * jnp.mean(d_hat * xhat, axis=1, keepdims=True))


def _ln_tiles(sub, x_prev, gain, bias):
    xhat, rstd = _ln_stats(DN_ALPHA * x_prev + sub)
    return xhat * gain + bias, xhat, rstd


def _ln_after_ln_tiles(sub, prev_hat, prev_gain, prev_bias, gain, bias):
    return _ln_tiles(sub, prev_hat * prev_gain + prev_bias, gain, bias)


def _residual_tiles(d_sub, res):
    return (d_sub + DN_ALPHA * res,)


def _ln_grad_tiles(d_sub, res, xhat, rstd, gain):
    d_out = d_sub + DN_ALPHA * res
    du = _ln_input_grad(d_out, gain, xhat, rstd)
    return du, du, d_out * xhat, d_out


def _ln_loss_tiles(sub, prev_hat, prev_gain, prev_bias, gain, bias, target):
    xhat, rstd = _ln_stats(DN_ALPHA * (prev_hat * prev_gain + prev_bias) + sub)
    diff = xhat * gain + bias - target
    d_out = diff * (1.0 / D_MODEL)
    du = _ln_input_grad(d_out, gain, xhat, rstd)
    return du, du, diff * diff, d_out * xhat, d_out


def _mem_probs(q_h, k_h):
    s = _dot(q_h, k_h, _NT) * (MEM_HEAD_DIM ** -0.5)
    e = jnp.exp(s - jnp.max(s, axis=1, keepdims=True))
    return e / jnp.sum(e, axis=1, keepdims=True)


def _xattn_fwd(q, kv):
    T, mem_len = q.shape[0], kv.shape[0]
    tq = _pick(T, (512, 256, 128))

    def body(q_ref, kv_ref, o_ref):
        for h in range(MEM_HEADS):
            cols = slice(h * MEM_HEAD_DIM, (h + 1) * MEM_HEAD_DIM)
            vcols = slice(D_MODEL + h * MEM_HEAD_DIM, D_MODEL + (h + 1) * MEM_HEAD_DIM)
            p = _mem_probs(q_ref[:, cols], kv_ref[:, cols])
            o_ref[:, cols] = _dot(p.astype(BF16), kv_ref[:, vcols], _NN).astype(BF16)

    return pl.pallas_call(
        body,
        name="xattn_fwd",
        grid=(T // tq,),
        in_specs=[pl.BlockSpec((tq, D_MODEL), lambda i: (i, 0)),
                  pl.BlockSpec((mem_len, 2 * D_MODEL), lambda i: (0, 0))],
        out_specs=pl.BlockSpec((tq, D_MODEL), lambda i: (i, 0)),
        out_shape=jax.ShapeDtypeStruct((T, D_MODEL), BF16),
        compiler_params=_params(("parallel",)),
    )(q, kv)


def _xattn_bwd(q, kv, d_o):
    T, mem_len = q.shape[0], kv.shape[0]
    tq = _pick(T, (512, 256, 128))

    def body(q_ref, kv_ref, do_ref, dq_ref, dkv_ref):
        @pl.when(pl.program_id(0) == 0)
        def _():
            dkv_ref[...] = jnp.zeros_like(dkv_ref)

        for h in range(MEM_HEADS):
            cols = slice(h * MEM_HEAD_DIM, (h + 1) * MEM_HEAD_DIM)
            vcols = slice(D_MODEL + h * MEM_HEAD_DIM, D_MODEL + (h + 1) * MEM_HEAD_DIM)
            q_h, k_h, do_h = q_ref[:, cols], kv_ref[:, cols], do_ref[:, cols]
            p = _mem_probs(q_h, k_h)
            dp = _dot(do_h, kv_ref[:, vcols], _NT)
            ds = p * (dp - jnp.sum(dp * p, axis=1, keepdims=True))
            dsb = (ds * (MEM_HEAD_DIM ** -0.5)).astype(BF16)
            dq_ref[:, cols] = _dot(dsb, k_h, _NN).astype(BF16)
            dkv_ref[:, cols] += _dot(dsb, q_h, _TN)
            dkv_ref[:, vcols] += _dot(p.astype(BF16), do_h, _TN)

    row = pl.BlockSpec((tq, D_MODEL), lambda i: (i, 0))
    full = pl.BlockSpec((mem_len, 2 * D_MODEL), lambda i: (0, 0))
    return pl.pallas_call(
        body,
        name="xattn_bwd",
        grid=(T // tq,),
        in_specs=[row, full, row],
        out_specs=[row, full],
        out_shape=[jax.ShapeDtypeStruct((T, D_MODEL), BF16), jax.ShapeDtypeStruct((mem_len, 2 * D_MODEL), F32)],
        compiler_params=_params(("arbitrary",)),
    )(q, kv, d_o)


def _swiglu_tiles(f):
    a, b = f[:, :FFN_HIDDEN], f[:, FFN_HIDDEN:]
    return f, a * jax.nn.sigmoid(a) * b


def _swiglu_grad_tiles(d_hidden, f):
    f = f.astype(F32)
    a, b = f[:, :FFN_HIDDEN], f[:, FFN_HIDDEN:]
    sig = jax.nn.sigmoid(a)
    return (jnp.concatenate([d_hidden * b * (sig * (1.0 + a * (1.0 - sig))), d_hidden * (a * sig)], axis=1),)


def _local_step(x, mem, w_in, small, target, fetch, ship):
    T = x.shape[0]
    tables = _ret_tables(T)
    memb = mem.astype(BF16)

    (h_a, h_b, h_c, h_d, h_e, xb), w_ffn = fetch(
        ("w_ffn_in", "w_ffn_out"),
        lambda rider: _as_host(rider, _mm_fused(
            x, w_in, mode="nn", name="proj_in", extras=[], pass_a=True,
            outs=[(1536, BF16), (1024, F32), (1024, BF16), (1024, F32), (2048, F32), (D_MODEL, BF16)],
            epilogue=_proj_tiles, max_rows=256, rider=rider)))
    (a_sb, r_mat), w_mix = fetch(("w_sb_o", "w_ret_o", "w_mix_o", "w_mem_q", "w_mem_kv", "w_mem_o"),
                                 lambda rider: _sb_fwd(h_a, rider))
    w = {**w_ffn, **w_mix}
    y_gated, o_pre, states = _ret_fwd(h_b, h_c, h_d, tables)
    y_sb = _mm(a_sb, w["w_sb_o"], mode="nn", out_dtype=F32, name="sb_out")
    row_f32, row_bf16 = (D_MODEL, F32), (D_MODEL, BF16)
    ln_outs = [row_bf16, row_f32, (1, F32)]
    y_ret, mix_in = _mm_fused(y_gated, w["w_ret_o"], mode="nn", name="ret_out", extras=[h_e, small["b_gate"], y_sb],
                              outs=[row_bf16, row_bf16], epilogue=_gate_mix_tiles)
    x1b, xhat1, rstd1 = _mm_fused(mix_in, w["w_mix_o"], mode="nn", name="mix_out",
                                  extras=[x, small["ln1_g"], small["ln1_b"]], outs=ln_outs, epilogue=_ln_tiles)
    q_m = _mm(x1b, w["w_mem_q"], mode="nn", out_dtype=BF16, name="mem_q")
    kv_m = _mm(memb, w["w_mem_kv"], mode="nn", out_dtype=BF16, name="mem_kv")
    o_m = _xattn_fwd(q_m, kv_m)
    x2b, xhat2, rstd2 = _mm_fused(
        o_m, w["w_mem_o"], mode="nn", name="mem_out", outs=ln_outs, epilogue=_ln_after_ln_tiles,
        extras=[xhat1, small["ln1_g"], small["ln1_b"], small["ln2_g"], small["ln2_b"]])
    f, hidden = _mm_fused(x2b, w["w_ffn_in"], mode="nn", name="ffn_in", extras=[],
                          outs=[(2 * FFN_HIDDEN, BF16), (FFN_HIDDEN, BF16)], epilogue=_swiglu_tiles)
    du_outs, col = [row_f32, row_bf16], D_MODEL
    du3, du3b, loss_cols, d_ln3_g, d_ln3_b = _mm_fused(
        hidden, w["w_ffn_out"], mode="nn", name="ffn_out", outs=du_outs, sums=[col, col, col], epilogue=_ln_loss_tiles,
        extras=[xhat2, small["ln2_g"], small["ln2_b"], small["ln3_g"], small["ln3_b"], target])

    g_ffn_out = _mm(hidden, du3b, mode="tn", out_dtype=BF16, name="g_ffn_out")
    (d_f,) = _mm_fused(du3b, w["w_ffn_out"], mode="nt", name="d_hidden", extras=[f],
                       outs=[(2 * FFN_HIDDEN, BF16)], epilogue=_swiglu_grad_tiles)
    g_ffn_in = _mm(x2b, d_f, mode="tn", out_dtype=BF16, name="g_ffn_in")
    du2, du2b, d_ln2_g, d_ln2_b = ship(
        {"w_ffn_out": g_ffn_out},
        lambda rider: _as_host(rider, _mm_fused(
            d_f, w["w_ffn_in"], mode="nt", name="d_x2", extras=[du3, xhat2, rstd2, small["ln2_g"]], outs=du_outs,
            sums=[col, col], epilogue=_ln_grad_tiles, rider=rider, max_rows=256)))
    g_mem_o = _mm(o_m, du2b, mode="tn", out_dtype=BF16, name="g_mem_o")
    d_om = _mm(du2b, w["w_mem_o"], mode="nt", out_dtype=BF16, name="d_om")
    d_qm, d_kvm = _xattn_bwd(q_m, kv_m, d_om)
    g_mem_q = _mm(x1b, d_qm, mode="tn", out_dtype=BF16, name="g_mem_q")
    g_mem_kv = _mm(memb, d_kvm.astype(BF16), mode="tn", out_dtype=BF16, name="g_mem_kv")
    du1, du1b, d_ln1_g, d_ln1_b = _mm_fused(
        d_qm, w["w_mem_q"], mode="nt", name="d_x1", extras=[du2, xhat1, rstd1, small["ln1_g"]], outs=du_outs,
        sums=[col, col], epilogue=_ln_grad_tiles)
    g_mix_o = _mm(mix_in, du1b, mode="tn", out_dtype=BF16, name="g_mix_o")
    d_ysb, d_yret, d_e, d_b_gate = _mm_fused(
        du1b, w["w_mix_o"], mode="nt", name="d_mix_in", extras=[h_e, small["b_gate"], y_sb, y_ret],
        outs=[row_bf16, row_bf16, (2 * D_MODEL, BF16)], sums=[2 * D_MODEL], epilogue=_gate_mix_grad_tiles)
    g_sb_o = _mm(a_sb, d_ysb, mode="tn", out_dtype=BF16, name="g_sb_o")
    g_ret_o = _mm(y_gated, d_yret, mode="tn", out_dtype=BF16, name="g_ret_o")
    d_asb = _mm(d_ysb, w["w_sb_o"], mode="nt", out_dtype=BF16, name="d_asb")
    d_ygated = _mm(d_yret, w["w_ret_o"], mode="nt", out_dtype=F32, name="d_ygated")
    small_grads = {"b_gate": d_b_gate, "ln1_g": d_ln1_g, "ln1_b": d_ln1_b, "ln2_g": d_ln2_g, "ln2_b": d_ln2_b,
                   "ln3_g": d_ln3_g, "ln3_b": d_ln3_b, "loss_cols": loss_cols}
    d_rq, d_rk, d_c, d_d = ship({"w_mem_kv": g_mem_kv, "w_mem_q": g_mem_q, "w_mem_o": g_mem_o, "w_mix_o": g_mix_o},
                                lambda rider: _ret_bwd(d_ygated, o_pre, states, h_b, h_c, h_d, tables, rider))
    d_q, d_k, d_v = ship({"w_ffn_in": g_ffn_in, "w_ret_o": g_ret_o, "w_sb_o": g_sb_o, "small": small_grads},
                         lambda rider: _sb_bwd(h_a, d_asb, r_mat, rider))
    d_h = [("sb_q", d_q), ("sb_k", d_k), ("sb_v", d_v), ("ret_q", d_rq), ("ret_k", d_rk), ("ret_v", d_c),
           ("ret_g", d_d), ("gate", d_e)]
    g_in = jnp.concatenate([_mm(xb, piece, mode="tn", out_dtype=BF16, name="g_in_" + tag) for tag, piece in d_h],
                           axis=1)
    (d_x,) = ship({"w_in": g_in},
                  lambda rider: _as_host(rider, _mm_fused(
                      [piece for _, piece in d_h], w_in, mode="nt", name="d_x", extras=[du1], outs=[(D_MODEL, F32)],
                      epilogue=_residual_tiles, rider=rider, max_rows=256)))
    return d_x


def _adamw_math(w, g, m, v):
    m = ADAM_B1 * m + (1.0 - ADAM_B1) * g
    v = ADAM_B2 * v + (1.0 - ADAM_B2) * jnp.square(g)
    m_hat = m / (1.0 - ADAM_B1 ** ADAM_STEP)
    v_hat = v / (1.0 - ADAM_B2 ** ADAM_STEP)
    delta = -ADAM_LR * (m_hat / (jnp.sqrt(v_hat) + ADAM_EPS) + ADAM_WD * w)
    return delta, m, v


def _adamw(parts, w, m, v, name):
    R, C = w.shape
    tr = max(t for t in range(16, min(R, 256) + 1, 16) if R % t == 0) if R >= 16 else R

    def body(p_ref, w_ref, m_ref, v_ref, g_ref, d_ref, nm_ref, nv_ref):
        g = p_ref[0].astype(F32)
        for j in range(1, N_DEV):
            g = g + p_ref[j].astype(F32)
        delta, nm, nv = _adamw_math(w_ref[...], g, m_ref[...], v_ref[...])
        g_ref[...] = g
        d_ref[...] = delta
        nm_ref[...] = nm
        nv_ref[...] = nv

    blk = pl.BlockSpec((tr, C), lambda i: (i, 0))
    out = jax.ShapeDtypeStruct((R, C), F32)
    return pl.pallas_call(
        body,
        name=name,
        grid=(R // tr,),
        in_specs=[pl.BlockSpec((N_DEV, tr, C), lambda i: (0, i, 0)), blk, blk, blk],
        out_specs=[blk] * 4,
        out_shape=[out] * 4,
        compiler_params=_params(("parallel",)),
    )(parts, w, m, v)


_SHARD_AXIS = {"w_in": 1, "w_sb_o": 1, "w_ret_o": 0, "w_mix_o": 0, "w_mem_q": 0, "w_mem_kv": 1, "w_mem_o": 0,
               "w_ffn_in": 1, "w_ffn_out": 0}
_MATRICES = tuple(_SHARD_AXIS)
_SMALL = ("b_gate", "ln1_g", "ln1_b", "ln2_g", "ln2_b", "ln3_g", "ln3_b")
_WEIGHT_ORDER = ("w_in", "b_gate", "w_sb_o", "w_ret_o", "w_mix_o", "ln1_g", "ln1_b", "w_mem_q", "w_mem_kv", "w_mem_o",
                 "ln2_g", "ln2_b", "w_ffn_in", "w_ffn_out", "ln3_g", "ln3_b")


def _assemble(name, gathered):
    if _SHARD_AXIS[name] == 0:
        return gathered.reshape(-1, gathered.shape[2])
    return jnp.transpose(gathered, (1, 0, 2)).reshape(gathered.shape[1], -1)


def _to_slots(name, full):
    if _SHARD_AXIS[name] == 0:
        return full.reshape(N_DEV, full.shape[0] // N_DEV, full.shape[1])
    return jnp.transpose(full.reshape(full.shape[0], N_DEV, full.shape[1] // N_DEV), (1, 0, 2))


SMALL_ROWS = 16


def _pack_small(vals):
    return jnp.concatenate([vals["b_gate"].reshape(2, D_MODEL)] + [vals[n] for n in _SMALL[1:]], axis=0)


def _unpack_small(packed):
    out = {"b_gate": packed[0:2].reshape(1, 2 * D_MODEL)}
    for i, n in enumerate(_SMALL[1:]):
        out[n] = packed[2 + i:3 + i]
    return out


def kernel(x, mem, w_in, b_gate, w_sb_o, w_ret_o, w_mix_o, ln1_g, ln1_b, w_mem_q, w_mem_kv, w_mem_o, ln2_g, ln2_b, w_ffn_in, w_ffn_out, ln3_g, ln3_b, loss_target, m_w_in, m_b_gate, m_w_sb_o, m_w_ret_o, m_w_mix_o, m_ln1_g, m_ln1_b, m_w_mem_q, m_w_mem_kv, m_w_mem_o, m_ln2_g, m_ln2_b, m_w_ffn_in, m_w_ffn_out, m_ln3_g, m_ln3_b, v_w_in, v_b_gate, v_w_sb_o, v_w_ret_o, v_w_mix_o, v_ln1_g, v_ln1_b, v_w_mem_q, v_w_mem_kv, v_w_mem_o, v_ln2_g, v_ln2_b, v_w_ffn_in, v_w_ffn_out, v_ln3_g, v_ln3_b):
    weights = dict(w_in=w_in, b_gate=b_gate, w_sb_o=w_sb_o, w_ret_o=w_ret_o, w_mix_o=w_mix_o, ln1_g=ln1_g, ln1_b=ln1_b,
                   w_mem_q=w_mem_q, w_mem_kv=w_mem_kv, w_mem_o=w_mem_o, ln2_g=ln2_g, ln2_b=ln2_b, w_ffn_in=w_ffn_in,
                   w_ffn_out=w_ffn_out, ln3_g=ln3_g, ln3_b=ln3_b)
    mom1 = dict(w_in=m_w_in, b_gate=m_b_gate, w_sb_o=m_w_sb_o, w_ret_o=m_w_ret_o, w_mix_o=m_w_mix_o, ln1_g=m_ln1_g,
                ln1_b=m_ln1_b, w_mem_q=m_w_mem_q, w_mem_kv=m_w_mem_kv, w_mem_o=m_w_mem_o, ln2_g=m_ln2_g, ln2_b=m_ln2_b,
                w_ffn_in=m_w_ffn_in, w_ffn_out=m_w_ffn_out, ln3_g=m_ln3_g, ln3_b=m_ln3_b)
    mom2 = dict(w_in=v_w_in, b_gate=v_b_gate, w_sb_o=v_w_sb_o, w_ret_o=v_w_ret_o, w_mix_o=v_w_mix_o, ln1_g=v_ln1_g,
                ln1_b=v_ln1_b, w_mem_q=v_w_mem_q, w_mem_kv=v_w_mem_kv, w_mem_o=v_w_mem_o, ln2_g=v_ln2_g, ln2_b=v_ln2_b,
                w_ffn_in=v_w_ffn_in, w_ffn_out=v_w_ffn_out, ln3_g=v_ln3_g, ln3_b=v_ln3_b)

    (gathered_in,) = _exchange([weights["w_in"][0].astype(BF16)], False, "gather_w_in")
    received = {}

    def fetch(names, host):
        res = host(_Rider([weights[n][0].astype(BF16) for n in names], False))
        return res[:-1], {n: _assemble(n, g) for n, g in zip(names, res[-1])}

    def ship(grads, host):
        names = list(grads)
        bufs = []
        for n in names:
            if n == "small":
                part = jnp.concatenate([_pack_small(grads[n]), grads[n]["loss_cols"],
                                        jnp.zeros((SMALL_ROWS - 9, D_MODEL), F32)], axis=0)
                bufs.append(jnp.broadcast_to(part[None], (N_DEV,) + part.shape))
            else:
                bufs.append(_to_slots(n, grads[n]).astype(BF16))
        res = host(_Rider(bufs, True))
        received.update(zip(names, res[-1]))
        return res[:-1]

    small = {n: weights[n] for n in _SMALL}
    d_x = _local_step(x[0], mem[0], _assemble("w_in", gathered_in), small, loss_target[0], fetch, ship)

    new = {}
    for n in _MATRICES:
        new[n] = _adamw(received[n], weights[n][0], mom1[n][0], mom2[n][0], "adamw_" + n)
    packed = _adamw(received["small"][:, :8], _pack_small({n: weights[n] for n in _SMALL}),
                    _pack_small({n: mom1[n] for n in _SMALL}), _pack_small({n: mom2[n] for n in _SMALL}), "adamw_small")
    small_new = [_unpack_small(p) for p in packed]
    loss = jnp.sum(received["small"][:, 8]) * (0.5 / D_MODEL)

    outs = [loss, d_x[None]]
    for slot in range(4):
        for n in _WEIGHT_ORDER:
            outs.append(new[n][slot][None] if n in new else small_new[slot][n])
    return tuple(outs)
```

```python
import functools
import math

import jax
import jax.numpy as jnp
from jax import lax
from jax.experimental import pallas as pl
from jax.experimental.pallas import tpu as pltpu

F32 = jnp.float32
BF16 = jnp.bfloat16

N_DEV = 8
D_MODEL = 1024
SB_HEAD_DIM = 64
SB_WIDTH = 512
RET_HEADS = 4
RET_QK_DIM = 128
RET_V_DIM = 256
RET_QK_WIDTH = 512
RET_V_WIDTH = 1024
RET_CHUNK = 128
RET_STEP_CHUNKS = 4
ROPE_BASE = 10000.0
MEM_HEADS = 4
MEM_HEAD_DIM = 256
FFN_HIDDEN = 2816
DN_ALPHA = 2.0 ** 0.25
LN_EPS = 1e-5
ADAM_LR = 0.001
ADAM_B1 = 0.9
ADAM_B2 = 0.999
ADAM_EPS = 1e-08
ADAM_WD = 0.01
ADAM_STEP = 10

VMEM_LIMIT_BYTES = 52 * 1024 * 1024
LANES = 128
SB_KEY_BLOCK = 128
SB_Q_BLOCK = 256
SB_DEAD_LOG = -105.0

MESH_AXES = ("x", "y", "c")


def _pick(dim, prefs):
    for p in prefs:
        if dim % p == 0:
            return p
    return dim


def _params(sem):
    return pltpu.CompilerParams(dimension_semantics=sem, vmem_limit_bytes=VMEM_LIMIT_BYTES)


def _dot(a, b, dims):
    return lax.dot_general(a, b, (dims, ((), ())), preferred_element_type=F32)


_NN = ((1,), (0,))
_NT = ((1,), (1,))
_TN = ((0,), (0,))


def _my_index():
    return 4 * lax.axis_index("x") + 2 * lax.axis_index("y") + lax.axis_index("c")


def _peer(k):
    x, y, c = lax.axis_index("x"), lax.axis_index("y"), lax.axis_index("c")
    bx, by, bc = (k >> 2) & 1, (k >> 1) & 1, k & 1
    px = (1 - x) if bx else x
    py = (1 - y) if by else y
    pc = (1 - c) if bc else c
    return (px, py, pc), 4 * px + 2 * py + pc


class _Rider:
    def __init__(self, bufs, scatter):
        self.bufs, self.scatter, self.n = list(bufs), scatter, len(bufs)
        self.specs = [pl.BlockSpec(memory_space=pl.ANY)] * self.n
        self.out_shape = [jax.ShapeDtypeStruct(b.shape if scatter else (N_DEV,) + b.shape, b.dtype) for b in self.bufs]
        self.scratch = [pltpu.SemaphoreType.DMA((self.n, N_DEV - 1)), pltpu.SemaphoreType.DMA((self.n, N_DEV - 1)),
                        pltpu.SemaphoreType.DMA((self.n,))]

    def _remote(self, ride, a, k, src_ref, slot, to):
        _, dst, (send_sems, recv_sems, _) = ride
        return pltpu.make_async_remote_copy(src_ref=src_ref, dst_ref=dst[a].at[slot], send_sem=send_sems.at[a, k],
                                            recv_sem=recv_sems.at[a, k], device_id=to,
                                            device_id_type=pl.DeviceIdType.MESH)

    def _local(self, ride, a):
        src, dst, (_, _, local_sems) = ride
        me = _my_index()
        return pltpu.make_async_copy(src[a].at[me] if self.scatter else src[a], dst[a].at[me], local_sems.at[a])

    def _direct(self, ride, a):
        src = ride[0]
        me = _my_index()
        out = []
        for k in range(1, N_DEV):
            peer, peer_idx = _peer(k)
            out.append(self._remote(ride, a, k - 1, src[a].at[peer_idx], me, peer))
        return out

    def _two_level(self, ride, a):
        src, dst = ride[0], ride[1]
        x, y, c = lax.axis_index("x"), lax.axis_index("y"), lax.axis_index("c")
        me, sibling = _my_index(), (x, y, 1 - c)
        chips = [(1 - x, y), (x, 1 - y), (1 - x, 1 - y)]
        first = [self._remote(ride, a, 0, src[a], me, sibling)]
        passed, landing = [], [self._remote(ride, a, 0, src[a], me + 1 - 2 * c, sibling)]
        for j, (px, py) in enumerate(chips):
            first.append(self._remote(ride, a, 1 + j, src[a], me, (px, py, c)))
            theirs = 4 * px + 2 * py + c
            passed.append(self._remote(ride, a, 4 + j, dst[a].at[theirs], theirs, sibling))
            landing.append(self._remote(ride, a, 1 + j, src[a], theirs, (px, py, c)))
        for j, (px, py) in enumerate(chips):
            landing.append(self._remote(ride, a, 4 + j, src[a], 4 * px + 2 * py + 1 - c, sibling))
        return first, passed, landing

    def start(self, ride):
        for a in range(self.n):
            self._local(ride, a).start()
            for cp in (self._direct(ride, a) if self.scatter else self._two_level(ride, a)[0]):
                cp.start()

    def finish(self, ride):
        if self.scatter:
            for a in range(self.n):
                for cp in self._direct(ride, a):
                    cp.wait()
                self._local(ride, a).wait()
            return
        levels = [self._two_level(ride, a) for a in range(self.n)]
        for first, passed, landing in levels:
            for j, cp in enumerate(passed):
                landing[1 + j].wait_recv()
                cp.start()
        for a, (first, passed, landing) in enumerate(levels):
            landing[0].wait_recv()
            for cp in landing[4:]:
                cp.wait_recv()
            for cp in first + passed:
                cp.wait_send()
            self._local(ride, a).wait()

    def start_at_first(self, ids, ride):
        first = functools.reduce(jnp.logical_and, [i == 0 for i in ids])

        @pl.when(first)
        def _():
            self.start(ride)

    def wait_at_last(self, ids, grid, ride):
        last = functools.reduce(jnp.logical_and, [i == g - 1 for i, g in zip(ids, grid)])

        @pl.when(last)
        def _():
            self.finish(ride)


def _exchange(bufs, scatter, name):
    rider = _Rider(bufs, scatter)

    def body(*refs):
        ride = (refs[:rider.n], refs[rider.n:2 * rider.n], refs[2 * rider.n:])
        rider.start(ride)
        rider.finish(ride)

    return pl.pallas_call(
        body,
        name=name,
        in_specs=rider.specs,
        out_specs=rider.specs,
        out_shape=rider.out_shape,
        scratch_shapes=rider.scratch,
    )(*rider.bufs)


MM_RESIDENT_B_BYTES = 14 * 1024 * 1024
MM_A_TILE_BYTES = 4 * 1024 * 1024
MM_OUT_TILE_BYTES = 6 * 1024 * 1024


def _mm_tiles(mode, M, N, K, a_bytes, out_bytes):
    if mode != "tn" and K * N * 2 <= MM_RESIDENT_B_BYTES:
        for tm in (1024, 512, 256, 128):
            if M % tm == 0 and tm * K * a_bytes <= MM_A_TILE_BYTES and tm * N * out_bytes <= MM_OUT_TILE_BYTES:
                return tm, N, K
    if mode == "tn":
        return (_pick(M, (1024, 1408, 512, 256, 128)), _pick(N, (1024, 1664, 1408, 512, 256, 128)),
                _pick(K, (2048, 1024, 512, 256, 128)))
    return _pick(M, (1024, 512, 256, 128)), _pick(N, (512, 256, 128)), _pick(K, (1024, 512, 256, 128))


def _mm(a, b, *, mode, out_dtype, name, res=None, res_scale=1.0, rider=None):
    if mode == "nn":
        (M, K), (K2, N) = a.shape, b.shape
    elif mode == "nt":
        (M, K), (N, K2) = a.shape, b.shape
    else:
        (K, M), (K2, N) = a.shape, b.shape
    assert K == K2, (a.shape, b.shape, mode)
    out_bytes = jnp.dtype(out_dtype).itemsize + (4 if res is not None else 0)
    tm, tn, tk = _mm_tiles(mode, M, N, K, a.dtype.itemsize, out_bytes)
    grid = (M // tm, N // tn, K // tk)
    nk = grid[2]
    dims = {"nn": _NN, "nt": _NT, "tn": _TN}[mode]
    n_in = 2 + (res is not None)
    n_ride = rider.n if rider is not None else 0

    def body(*refs):
        a_ref, b_ref = refs[:2]
        r_ref = refs[2] if res is not None else None
        o_ref = refs[n_in + n_ride]
        rest = refs[n_in + 2 * n_ride + 1:]
        acc_ref = rest[0] if nk > 1 else None
        ids = [pl.program_id(d) for d in range(3)]
        if rider is not None:
            ride = (refs[n_in:n_in + n_ride], refs[n_in + n_ride + 1:n_in + 2 * n_ride + 1], rest[-3:])
            rider.start_at_first(ids, ride)
        part = _dot(a_ref[...].astype(BF16), b_ref[...].astype(BF16), dims)

        def finish(total):
            if r_ref is not None:
                total = total + res_scale * r_ref[...]
            o_ref[...] = total.astype(out_dtype)

        if nk == 1:
            finish(part)
        else:
            k = ids[2]

            @pl.when(k == 0)
            def _():
                acc_ref[...] = part

            @pl.when(k > 0)
            def _():
                acc_ref[...] += part

            @pl.when(k == nk - 1)
            def _():
                finish(acc_ref[...])

        if rider is not None:
            rider.wait_at_last(ids, grid, ride)

    if mode == "nn":
        a_spec = pl.BlockSpec((tm, tk), lambda i, j, k: (i, k))
        b_spec = pl.BlockSpec((tk, tn), lambda i, j, k: (k, j))
    elif mode == "nt":
        a_spec = pl.BlockSpec((tm, tk), lambda i, j, k: (i, k))
        b_spec = pl.BlockSpec((tn, tk), lambda i, j, k: (j, k))
    else:
        a_spec = pl.BlockSpec((tk, tm), lambda i, j, k: (k, i))
        b_spec = pl.BlockSpec((tk, tn), lambda i, j, k: (k, j))
    o_spec = pl.BlockSpec((tm, tn), lambda i, j, k: (i, j))
    in_specs = [a_spec, b_spec] + ([o_spec] if res is not None else [])
    args = (a, b) + ((res,) if res is not None else ())
    out_specs, out_shape = [o_spec], [jax.ShapeDtypeStruct((M, N), out_dtype)]
    scratch = [pltpu.VMEM((tm, tn), F32)] if nk > 1 else []
    sem = ("parallel", "parallel", "arbitrary")
    if rider is not None:
        in_specs, args = in_specs + rider.specs, args + tuple(rider.bufs)
        out_specs, out_shape = out_specs + rider.specs, out_shape + rider.out_shape
        scratch = scratch + rider.scratch
        sem = ("arbitrary",) * 3
    outs = pl.pallas_call(
        body,
        name=name,
        grid=grid,
        in_specs=in_specs,
        out_specs=out_specs,
        out_shape=out_shape,
        scratch_shapes=scratch,
        compiler_params=_params(sem),
    )(*args)
    return outs[0] if rider is None else (outs[0], list(outs[1:]))


def _mm_host(a, b, *, rider, **kw):
    out = _mm(a, b, rider=rider, **kw)
    return out if rider is not None else (out, [])


def _as_host(rider, results):
    return results if rider is not None else tuple(results) + ([],)


MM_FUSED_MARGIN_BYTES = 10 * 1024 * 1024
MM_FUSED_MAX_ROWS = 512


def _col_sum_update(acc_ref, val, first):
    part = jnp.sum(val.reshape(val.shape[0] // 8, 8, val.shape[1]), axis=0)

    @pl.when(first)
    def _():
        acc_ref[...] = part

    @pl.when(jnp.logical_not(first))
    def _():
        acc_ref[...] += part


def _mm_fused(a, b, *, mode, name, extras, outs, epilogue, sums=(), rider=None, max_rows=MM_FUSED_MAX_ROWS,
              pass_a=False):
    parts = list(a) if isinstance(a, (list, tuple)) else [a]
    M, K = parts[0].shape[0], sum(p.shape[1] for p in parts)
    if mode == "nn":
        (K2, N), b_dims = b.shape, _NN
    else:
        (N, K2), b_dims = b.shape, _NT
    assert K == K2, (K, b.shape, mode)
    rows = parts + [e for e in extras if e.shape[0] == M]
    per_row = 2 * (sum(e.shape[1] * e.dtype.itemsize for e in rows)
                   + sum(c * jnp.dtype(d).itemsize for c, d in outs)) + 2 * N * 4
    budget = VMEM_LIMIT_BYTES - K * N * 2 - MM_FUSED_MARGIN_BYTES
    tm = next(t for t in (512, 256, 128, 64, 32, 16) if t <= max_rows and M % t == 0 and t * per_row <= budget)
    steps = M // tm
    n_a, n_x, n_o, n_s = len(parts), len(extras), len(outs), len(sums)
    n_ride = rider.n if rider is not None else 0

    def body(*refs):
        a_refs, b_ref = refs[:n_a], refs[n_a]
        x_refs = refs[n_a + 1:n_a + 1 + n_x]
        base = n_a + 1 + n_x + n_ride
        o_refs, s_refs = refs[base:base + n_o], refs[base + n_o:base + n_o + n_s]
        acc_refs = refs[base + n_o + n_s + n_ride:base + n_o + 2 * n_s + n_ride]
        ids = [pl.program_id(0)]
        if rider is not None:
            ride = (refs[n_a + 1 + n_x:base], refs[base + n_o + n_s:base + n_o + n_s + n_ride], refs[-3:])
            rider.start_at_first(ids, ride)
        a_tile = a_refs[0][...]
        a_bf16 = a_tile.astype(BF16) if n_a == 1 else jnp.concatenate([r[...].astype(BF16) for r in a_refs], axis=1)
        prod = _dot(a_bf16, b_ref[...], b_dims)
        tiles = epilogue(prod, *([a_tile] if pass_a else []), *[r[...] for r in x_refs])
        for o_ref, t in zip(o_refs, tiles[:n_o]):
            o_ref[...] = t.astype(o_ref.dtype)
        for acc_ref, t in zip(acc_refs, tiles[n_o:]):
            _col_sum_update(acc_ref, t, ids[0] == 0)
        if n_s:
            @pl.when(ids[0] == steps - 1)
            def _():
                for s_ref, acc_ref in zip(s_refs, acc_refs):
                    s_ref[...] = jnp.sum(acc_ref[...], axis=0, keepdims=True)
        if rider is not None:
            rider.wait_at_last(ids, (steps,), ride)

    in_specs = [pl.BlockSpec((tm, p.shape[1]), lambda i: (i, 0)) for p in parts]
    in_specs.append(pl.BlockSpec(b.shape, lambda i: (0, 0), pipeline_mode=pl.Buffered(1)))
    for e in extras:
        in_specs.append(pl.BlockSpec((tm, e.shape[1]), lambda i: (i, 0)) if e.shape[0] == M
                        else pl.BlockSpec(e.shape, lambda i: (0, 0)))
    out_specs = ([pl.BlockSpec((tm, c), lambda i: (i, 0)) for c, _ in outs]
                 + [pl.BlockSpec((1, c), lambda i: (0, 0)) for c in sums])
    out_shape = ([jax.ShapeDtypeStruct((M, c), d) for c, d in outs]
                 + [jax.ShapeDtypeStruct((1, c), F32) for c in sums])
    args = tuple(parts) + (b,) + tuple(extras)
    scratch = [pltpu.VMEM((8, c), F32) for c in sums]
    if rider is not None:
        in_specs, args = in_specs + rider.specs, args + tuple(rider.bufs)
        out_specs, out_shape = out_specs + rider.specs, out_shape + rider.out_shape
        scratch = scratch + rider.scratch
    res = pl.pallas_call(
        body,
        name=name,
        grid=(steps,),
        in_specs=in_specs,
        out_specs=out_specs,
        out_shape=out_shape,
        scratch_shapes=scratch,
        compiler_params=_params(("arbitrary",) if (n_s or rider is not None) else ("parallel",)),
    )(*args)
    return tuple(res[:n_o + n_s]) + ((list(res[n_o + n_s:]),) if rider is not None else ())


def _pair_rows(blk, lane_is_a):
    zero = jnp.zeros_like(blk)
    return jnp.concatenate([jnp.where(lane_is_a, blk, zero), jnp.where(lane_is_a, zero, blk)], axis=0)


SB_STRIP = 32
SB_FWD_PAIRS = 4
SB_BWD_PAIRS = 2
SB_GROUP = 2


def _pair_lanes(p):
    return slice(p * LANES, (p + 1) * LANES)


def _sb_scan_matrices():
    o = lax.broadcasted_iota(jnp.int32, (2 * LANES, 4 * LANES), 0)
    c = lax.broadcasted_iota(jnp.int32, (2 * LANES, 4 * LANES), 1) & (2 * LANES - 1)
    same = (o >= LANES) == (c >= LANES)
    oo, cc = o & (LANES - 1), c & (LANES - 1)
    return (jnp.where(same & (cc > oo), -1.0, 0.0).astype(BF16), jnp.where(same & (cc < oo), 1.0, 0.0).astype(BF16))


def _sb_causal_masks(tq):
    d = lax.broadcasted_iota(jnp.int32, (tq // SB_KEY_BLOCK, SB_KEY_BLOCK, tq), 0)
    k = lax.broadcasted_iota(jnp.int32, (tq // SB_KEY_BLOCK, SB_KEY_BLOCK, tq), 1)
    t = lax.broadcasted_iota(jnp.int32, (tq // SB_KEY_BLOCK, SB_KEY_BLOCK, tq), 2)
    return jnp.where(d * SB_KEY_BLOCK + k < t, 1.0, 0.0).astype(F32)


def _sb_log_terms(z):
    minus_abs = lax.bitcast_convert_type(lax.bitcast_convert_type(z, jnp.uint32) | jnp.uint32(0x80000000), F32)
    spent = jnp.maximum(z, 0.0) + jnp.log(1.0 + jnp.exp(minus_abs))
    return spent, z - spent


def _sb_store_split(ref, strip, val, cols):
    hi = val.astype(BF16)
    ref[pl.ds(strip * SB_STRIP, SB_STRIP), cols] = hi
    ref[pl.ds(2 * LANES + strip * SB_STRIP, SB_STRIP), cols] = (val - hi.astype(F32)).astype(BF16)


def _sb_lanes(tq, diag):
    if diag == "left":
        return 0, tq // 2
    first = 0 if diag is None else diag * SB_KEY_BLOCK
    return first, tq - first


def _lane_add(full, part, lanes):
    first, width = lanes
    pieces = [full[:, :first]] if first else []
    pieces.append(full[:, first:first + width] + part)
    if first + width < full.shape[1]:
        pieces.append(full[:, first + width:])
    return pieces[0] if len(pieces) == 1 else jnp.concatenate(pieces, axis=1)


def _sb_fwd(h_a, rider=None):
    assert SB_FWD_PAIRS == 4
    T = h_a.shape[0]
    tq = _pick(T, (SB_Q_BLOCK, SB_KEY_BLOCK))
    nq, per_q, nkb = T // tq, tq // SB_KEY_BLOCK, T // SB_KEY_BLOCK
    assert per_q % SB_GROUP == 0
    n_strips = 2 * LANES // SB_STRIP
    n_ride = rider.n if rider is not None else 0
    after_m, _ = _sb_scan_matrices()
    causal_m = _sb_causal_masks(tq)
    pairs = SB_FWD_PAIRS

    def body(*refs):
        q_ref, k_ref, v_ref, after_ref, causal_ref = refs[:5]
        a_ref, r_ref, n_ref = refs[5 + n_ride:8 + n_ride]
        z_ref, lb_ref, split_ref, w_ref = refs[8 + 2 * n_ride:12 + 2 * n_ride]
        ids = [pl.program_id(0)]
        if rider is not None:
            ride = (refs[5:5 + n_ride], refs[8 + n_ride:8 + 2 * n_ride], refs[-3:])
            rider.start_at_first(ids, ride)
        i = ids[0]
        q_t = [(q_ref[:, _pair_lanes(p)].astype(F32).T * (SB_HEAD_DIM ** -0.5)).astype(BF16) for p in range(pairs)]
        lane_is_a = lax.broadcasted_iota(jnp.int32, (SB_KEY_BLOCK, LANES), 1) < SB_HEAD_DIM

        def tiles(kbs, diags, carry):
            nb = len(kbs)
            lanes = [_sb_lanes(tq, d) for d in diags]
            cols = [slice(first, first + width) for first, width in lanes]
            acc_t, ra, rb = [list(c) for c in carry]
            ks = [pl.multiple_of(kb * SB_KEY_BLOCK, SB_KEY_BLOCK) for kb in kbs]
            slot = lambda p, b: p * nb + b

            def causal(b, s):
                return causal_ref[diags[b], pl.ds((s * SB_STRIP) % SB_KEY_BLOCK, SB_STRIP), cols[b]]

            vv = {}
            for b in range(nb):
                for p in range(pairs):
                    kk = _pair_rows(k_ref[pl.ds(ks[b], SB_KEY_BLOCK), _pair_lanes(p)], lane_is_a)
                    vv[p, b] = _pair_rows(v_ref[pl.ds(ks[b], SB_KEY_BLOCK), _pair_lanes(p)], lane_is_a)
                    z_ref[slot(p, b), :, cols[b]] = _dot(kk, q_t[p][:, cols[b]], _NN)
            sums = {}
            for b in range(nb):
                for p in range(pairs):
                    part = [jnp.zeros((8, lanes[b][1]), F32), jnp.zeros((8, lanes[b][1]), F32)]
                    for s in range(n_strips):
                        rows = pl.ds(s * SB_STRIP, SB_STRIP)
                        spent, log_beta = _sb_log_terms(z_ref[slot(p, b), rows, cols[b]])
                        lb_ref[slot(p, b), rows, cols[b]] = log_beta
                        if isinstance(diags[b], int):
                            spent = spent * causal(b, s)
                        _sb_store_split(split_ref.at[slot(p, b)], s, spent, cols[b])
                        head = (s * SB_STRIP) // SB_KEY_BLOCK
                        part[head] = part[head] + jnp.sum(spent.reshape(SB_STRIP // 8, 8, lanes[b][1]), axis=0)
                    sums[p, b] = part
            for b in range(nb):
                for p in range(pairs):
                    z_ref[slot(p, b), :, cols[b]] = _dot(after_ref[...], split_ref[slot(p, b), :, cols[b]], _NN)
            for b in range(nb):
                for p in range(pairs):
                    for s in range(n_strips):
                        rows = pl.ds(s * SB_STRIP, SB_STRIP)
                        start = (ra[p] if (s * SB_STRIP) < SB_KEY_BLOCK else rb[p])[:, cols[b]]
                        w = jnp.exp(lb_ref[slot(p, b), rows, cols[b]] + z_ref[slot(p, b), rows, cols[b]] + start)
                        if isinstance(diags[b], int):
                            w = w * causal(b, s)
                        w_ref[slot(p, b), rows, cols[b]] = w.astype(BF16)
                    r_ref[2 * p, kbs[b]] = ra[p]
                    r_ref[2 * p + 1, kbs[b]] = rb[p]
                    ra[p] = _lane_add(ra[p], -jnp.sum(sums[p, b][0], axis=0, keepdims=True), lanes[b])
                    rb[p] = _lane_add(rb[p], -jnp.sum(sums[p, b][1], axis=0, keepdims=True), lanes[b])
            for b in range(nb):
                for p in range(pairs):
                    acc_t[p] = _lane_add(acc_t[p], _dot(vv[p, b], w_ref[slot(p, b), :, cols[b]], _TN), lanes[b])
            return tuple(acc_t), tuple(ra), tuple(rb)

        carry = (tuple(jnp.zeros((LANES, tq), F32) for _ in range(pairs)),
                 tuple(jnp.zeros((1, tq), F32) for _ in range(pairs)),
                 tuple(jnp.zeros((1, tq), F32) for _ in range(pairs)))
        own = list(reversed(range(per_q)))
        n_full = i * per_q
        carry = lax.cond(
            i > 0,
            lambda cc: tiles([n_full + d for d in own] + [n_full - 1 - b for b in range(SB_GROUP)],
                             own + [None] * SB_GROUP, cc),
            lambda cc: tiles([n_full + d for d in own], own, cc), carry)
        first_walked = jnp.where(i > 0, SB_GROUP, 0).astype(jnp.int32)

        def top_of(sums_a, sums_b, first):
            return jnp.max(functools.reduce(jnp.maximum, [r[:, first:] for r in sums_a + sums_b]))

        def alive(c):
            return jnp.logical_and(c[0] < n_full, top_of(c[2], c[3], 0) > SB_DEAD_LOG)

        def step(c):
            kbs = [n_full - 1 - c[0] - b for b in range(SB_GROUP)]
            return (c[0] + SB_GROUP,) + lax.cond(
                top_of(c[2], c[3], tq // 2) > SB_DEAD_LOG,
                lambda cc: tiles(kbs, [None] * SB_GROUP, cc), lambda cc: tiles(kbs, ["left"] * SB_GROUP, cc), c[1:])

        walked, acc_t, _, _ = lax.while_loop(alive, step, (first_walked,) + carry)
        for p in range(pairs):
            a_ref[:, _pair_lanes(p)] = acc_t[p].T.astype(BF16)
        n_ref[...] = jnp.zeros(n_ref.shape, F32) + walked.astype(F32)
        if rider is not None:
            rider.wait_at_last(ids, (nq,), ride)

    wide = pairs * LANES
    in_specs = [pl.BlockSpec((tq, wide), lambda i: (i, 0)),
                pl.BlockSpec((T, wide), lambda i: (0, 1), pipeline_mode=pl.Buffered(1)),
                pl.BlockSpec((T, wide), lambda i: (0, 2), pipeline_mode=pl.Buffered(1)),
                pl.BlockSpec(after_m.shape, lambda i: (0, 0), pipeline_mode=pl.Buffered(1)),
                pl.BlockSpec(causal_m.shape, lambda i: (0, 0, 0), pipeline_mode=pl.Buffered(1))]
    out_specs = [pl.BlockSpec((tq, wide), lambda i: (i, 0)),
                 pl.BlockSpec((2 * pairs, nkb, 1, tq), lambda i: (0, 0, 0, i)),
                 pl.BlockSpec((1, 8, LANES), lambda i: (i, 0, 0))]
    out_shape = [jax.ShapeDtypeStruct((T, SB_WIDTH), BF16), jax.ShapeDtypeStruct((2 * pairs, nkb, 1, T), F32),
                 jax.ShapeDtypeStruct((nq, 8, LANES), F32)]
    args = (h_a, h_a, h_a, after_m, causal_m)
    slots = pairs * (per_q + SB_GROUP)
    scratch = [pltpu.VMEM((slots, 2 * LANES, tq), F32), pltpu.VMEM((slots, 2 * LANES, tq), F32),
               pltpu.VMEM((slots, 4 * LANES, tq), BF16), pltpu.VMEM((slots, 2 * LANES, tq), BF16)]
    if rider is not None:
        in_specs, args = in_specs + rider.specs, args + tuple(rider.bufs)
        out_specs, out_shape = out_specs + rider.specs, out_shape + rider.out_shape
        scratch = scratch + rider.scratch
    outs = pl.pallas_call(
        body,
        name="sb_fwd",
        grid=(nq,),
        in_specs=in_specs,
        out_specs=out_specs,
        out_shape=out_shape,
        scratch_shapes=scratch,
        compiler_params=_params(("arbitrary",)),
    )(*args)
    return outs[0], (outs[1], outs[2]), list(outs[3:])


def _sb_bwd(h_a, d_out, saved, rider=None):
    r_mat, walked_blocks = saved
    T = h_a.shape[0]
    tq = _pick(T, (SB_Q_BLOCK, SB_KEY_BLOCK))
    nq, per_q, nkb = T // tq, tq // SB_KEY_BLOCK, T // SB_KEY_BLOCK
    n_strips = 2 * LANES // SB_STRIP
    after_m, before_m = _sb_scan_matrices()
    causal_m = _sb_causal_masks(tq)
    pairs = SB_BWD_PAIRS
    groups = 4 // pairs
    n_ride = rider.n if rider is not None else 0

    def body(*refs):
        q_ref, k_ref, v_ref, do_ref, r_ref, n_ref, after_ref, before_ref, causal_ref = refs[:9]
        dq_ref, dk_ref, dv_ref = refs[9 + n_ride:12 + n_ride]
        z_ref, lb_ref, split_ref, w_ref, da_ref, dz_ref = refs[12 + 2 * n_ride:18 + 2 * n_ride]
        ids = [pl.program_id(0), pl.program_id(1)]
        if rider is not None:
            ride = (refs[9:9 + n_ride], refs[12 + n_ride:12 + 2 * n_ride], refs[-3:])
            rider.start_at_first(ids, ride)
        i = ids[1]

        @pl.when(i == 0)
        def _():
            dk_ref[...] = jnp.zeros_like(dk_ref)
            dv_ref[...] = jnp.zeros_like(dv_ref)

        scale = SB_HEAD_DIM ** -0.5
        q = [q_ref[:, _pair_lanes(p)] for p in range(pairs)]
        d_o = [do_ref[:, _pair_lanes(p)] for p in range(pairs)]
        q_t = [(x.astype(F32).T * scale).astype(BF16) for x in q]
        do_t = [x.astype(F32).T.astype(BF16) for x in d_o]
        lane_is_a = lax.broadcasted_iota(jnp.int32, (SB_KEY_BLOCK, LANES), 1) < SB_HEAD_DIM

        def tiles(kbs, diags, carry):
            nb = len(kbs)
            lanes = [_sb_lanes(tq, d) for d in diags]
            cols = [slice(first, first + width) for first, width in lanes]
            dq_t, ca, cb = [list(c) for c in carry]
            ks = [pl.multiple_of(kb * SB_KEY_BLOCK, SB_KEY_BLOCK) for kb in kbs]
            slot = lambda p, b: p * nb + b

            def causal(b, s):
                return causal_ref[diags[b], pl.ds((s * SB_STRIP) % SB_KEY_BLOCK, SB_STRIP), cols[b]]

            kk, vv = {}, {}
            for b in range(nb):
                for p in range(pairs):
                    kk[p, b] = _pair_rows(k_ref[pl.ds(ks[b], SB_KEY_BLOCK), _pair_lanes(p)], lane_is_a)
                    vv[p, b] = _pair_rows(v_ref[pl.ds(ks[b], SB_KEY_BLOCK), _pair_lanes(p)], lane_is_a)
                    z_ref[slot(p, b), :, cols[b]] = _dot(kk[p, b], q_t[p][:, cols[b]], _NN)
            for b in range(nb):
                for p in range(pairs):
                    for s in range(n_strips):
                        rows = pl.ds(s * SB_STRIP, SB_STRIP)
                        spent, log_beta = _sb_log_terms(z_ref[slot(p, b), rows, cols[b]])
                        lb_ref[slot(p, b), rows, cols[b]] = log_beta
                        if isinstance(diags[b], int):
                            spent = spent * causal(b, s)
                        _sb_store_split(split_ref.at[slot(p, b)], s, spent, cols[b])
            for b in range(nb):
                for p in range(pairs):
                    z_ref[slot(p, b), :, cols[b]] = _dot(after_ref[...], split_ref[slot(p, b), :, cols[b]], _NN)
                    da_ref[slot(p, b), :, cols[b]] = _dot(vv[p, b], do_t[p][:, cols[b]], _NN)
            sums = {}
            for b in range(nb):
                for p in range(pairs):
                    part = [jnp.zeros((8, lanes[b][1]), F32), jnp.zeros((8, lanes[b][1]), F32)]
                    for s in range(n_strips):
                        rows = pl.ds(s * SB_STRIP, SB_STRIP)
                        start = r_ref[2 * p + (s * SB_STRIP) // SB_KEY_BLOCK, kbs[b]][:, cols[b]]
                        w = jnp.exp(lb_ref[slot(p, b), rows, cols[b]] + z_ref[slot(p, b), rows, cols[b]] + start)
                        if isinstance(diags[b], int):
                            w = w * causal(b, s)
                        w_ref[slot(p, b), rows, cols[b]] = w.astype(BF16)
                        da = da_ref[slot(p, b), rows, cols[b]] * w
                        da_ref[slot(p, b), rows, cols[b]] = da
                        _sb_store_split(split_ref.at[slot(p, b)], s, da, cols[b])
                        head = (s * SB_STRIP) // SB_KEY_BLOCK
                        part[head] = part[head] + jnp.sum(da.reshape(SB_STRIP // 8, 8, lanes[b][1]), axis=0)
                    sums[p, b] = part
            for b in range(nb):
                for p in range(pairs):
                    z_ref[slot(p, b), :, cols[b]] = _dot(before_ref[...], split_ref[slot(p, b), :, cols[b]], _NN)
            for b in range(nb):
                for p in range(pairs):
                    for s in range(n_strips):
                        rows = pl.ds(s * SB_STRIP, SB_STRIP)
                        base = (ca[p] if (s * SB_STRIP) < SB_KEY_BLOCK else cb[p])[:, cols[b]]
                        sig = jnp.exp(lb_ref[slot(p, b), rows, cols[b]])
                        dz = (da_ref[slot(p, b), rows, cols[b]] * (1.0 - sig)
                              - (z_ref[slot(p, b), rows, cols[b]] + base) * sig)
                        if isinstance(diags[b], int):
                            dz = dz * causal(b, s)
                        dz_ref[slot(p, b), rows, cols[b]] = (dz * scale).astype(BF16)
                    ca[p] = _lane_add(ca[p], jnp.sum(sums[p, b][0], axis=0, keepdims=True), lanes[b])
                    cb[p] = _lane_add(cb[p], jnp.sum(sums[p, b][1], axis=0, keepdims=True), lanes[b])
            for b in range(nb):
                for p in range(pairs):
                    dq_t[p] = _lane_add(dq_t[p], _dot(kk[p, b], dz_ref[slot(p, b), :, cols[b]], _TN), lanes[b])
                    dkk = _dot(dz_ref[slot(p, b), :, cols[b]], q[p][cols[b], :], _NN)
                    dvv = _dot(w_ref[slot(p, b), :, cols[b]], d_o[p][cols[b], :], _NN)
                    here = (pl.ds(ks[b], SB_KEY_BLOCK), _pair_lanes(p))
                    dk_ref[here] += jnp.where(lane_is_a, dkk[:SB_KEY_BLOCK], dkk[SB_KEY_BLOCK:])
                    dv_ref[here] += jnp.where(lane_is_a, dvv[:SB_KEY_BLOCK], dvv[SB_KEY_BLOCK:])
            return tuple(dq_t), tuple(ca), tuple(cb)

        n_full = i * per_q
        groups_walked = jnp.clip(jnp.max(n_ref[...]).astype(jnp.int32), 0, n_full) // SB_GROUP
        carry = (tuple(jnp.zeros((LANES, tq), F32) for _ in range(pairs)),
                 tuple(jnp.zeros((1, tq), F32) for _ in range(pairs)),
                 tuple(jnp.zeros((1, tq), F32) for _ in range(pairs)))

        def below(j, c):
            kbs = [n_full - (groups_walked - j) * SB_GROUP + b for b in range(SB_GROUP)]
            starts = [r_ref[h, kbs[-1]][:, tq // 2:] for h in range(2 * pairs)]
            reaches = jnp.max(functools.reduce(jnp.maximum, starts)) > SB_DEAD_LOG
            return lax.cond(reaches, lambda cc: tiles(kbs, [None] * SB_GROUP, cc),
                            lambda cc: tiles(kbs, ["left"] * SB_GROUP, cc), c)

        carry = lax.fori_loop(0, groups_walked, below, carry)
        own = list(range(per_q))
        carry = tiles([i * per_q + d for d in own], own, carry)
        for p in range(pairs):
            dq_ref[:, _pair_lanes(p)] = carry[0][p].T.astype(BF16)
        if rider is not None:
            rider.wait_at_last(ids, (groups, nq), ride)

    wide = pairs * LANES
    mat = pl.BlockSpec(after_m.shape, lambda g, i: (0, 0), pipeline_mode=pl.Buffered(1))
    in_specs = [pl.BlockSpec((tq, wide), lambda g, i: (i, g)),
                pl.BlockSpec((T, wide), lambda g, i: (0, groups + g), pipeline_mode=pl.Buffered(1)),
                pl.BlockSpec((T, wide), lambda g, i: (0, 2 * groups + g), pipeline_mode=pl.Buffered(1)),
                pl.BlockSpec((tq, wide), lambda g, i: (i, g)),
                pl.BlockSpec((2 * pairs, nkb, 1, tq), lambda g, i: (g, 0, 0, i)),
                pl.BlockSpec((1, 8, LANES), lambda g, i: (i, 0, 0)),
                mat, mat,
                pl.BlockSpec(causal_m.shape, lambda g, i: (0, 0, 0), pipeline_mode=pl.Buffered(1))]
    out_specs = [pl.BlockSpec((tq, wide), lambda g, i: (i, g)),
                 pl.BlockSpec((T, wide), lambda g, i: (0, g)),
                 pl.BlockSpec((T, wide), lambda g, i: (0, g))]
    out_shape = [jax.ShapeDtypeStruct((T, SB_WIDTH), BF16), jax.ShapeDtypeStruct((T, SB_WIDTH), F32),
                 jax.ShapeDtypeStruct((T, SB_WIDTH), F32)]
    args = (h_a, h_a, h_a, d_out, r_mat, walked_blocks, after_m, before_m, causal_m)
    slots = pairs * max(per_q, SB_GROUP)
    scratch = [pltpu.VMEM((slots, 2 * LANES, tq), F32), pltpu.VMEM((slots, 2 * LANES, tq), F32),
               pltpu.VMEM((slots, 4 * LANES, tq), BF16), pltpu.VMEM((slots, 2 * LANES, tq), BF16),
               pltpu.VMEM((slots, 2 * LANES, tq), F32), pltpu.VMEM((slots, 2 * LANES, tq), BF16)]
    if rider is not None:
        in_specs, args = in_specs + rider.specs, args + tuple(rider.bufs)
        out_specs, out_shape = out_specs + rider.specs, out_shape + rider.out_shape
        scratch = scratch + rider.scratch
    outs = pl.pallas_call(
        body,
        name="sb_bwd",
        grid=(groups, nq),
        in_specs=in_specs,
        out_specs=out_specs,
        out_shape=out_shape,
        scratch_shapes=scratch,
        compiler_params=_params(("arbitrary", "arbitrary") if rider is not None else ("parallel", "arbitrary")),
    )(*args)
    return outs[0], outs[1], outs[2], list(outs[3:])


def _ret_tables(T):
    half = RET_QK_DIM // 2
    inv = 1.0 / (ROPE_BASE ** (jnp.arange(half, dtype=F32) / half))
    ang = jnp.arange(T, dtype=F32)[:, None] * inv[None, :]
    cos, sin = jnp.cos(ang), jnp.sin(ang)
    cos_t = jnp.concatenate([cos, cos], axis=1)
    sin_t = jnp.concatenate([-sin, sin], axis=1)
    log_gamma = jnp.log1p(-jnp.exp2(-5.0 - jnp.arange(RET_HEADS, dtype=F32)))
    idx = jnp.arange(RET_CHUNK, dtype=F32)
    rel = idx[:, None] - idx[None, :]
    decay = jnp.where(rel[None] >= 0, jnp.exp(log_gamma[:, None, None] * jnp.maximum(rel, 0.0)[None]), 0.0)
    k_decay = jnp.exp(log_gamma[None, :] * (RET_CHUNK - 1.0 - idx)[:, None])
    q_decay = jnp.exp(log_gamma[None, :] * (idx + 1.0)[:, None])
    chunk_decay = jnp.exp(log_gamma * RET_CHUNK)
    k_dec = jnp.broadcast_to(k_decay.T[:, :, None], (RET_HEADS, RET_CHUNK, LANES))
    q_dec = jnp.broadcast_to(q_decay.T[:, :, None], (RET_HEADS, RET_CHUNK, LANES))
    c_dec = jnp.broadcast_to(chunk_decay[:, None, None], (RET_HEADS, 8, LANES))
    return cos_t, sin_t, decay, k_dec, q_dec, c_dec


def _rotary(x, cos_t, sin_t):
    return x * cos_t + pltpu.roll(x, RET_QK_DIM // 2, 1) * sin_t


def _rotary_transpose(dy, cos_t, sin_t):
    return dy * cos_t + pltpu.roll(dy * sin_t, RET_QK_DIM // 2, 1)


def _head_norm(o):
    mu = jnp.mean(o, axis=1, keepdims=True)
    cen = o - mu
    var = jnp.mean(cen * cen, axis=1, keepdims=True)
    rstd = lax.rsqrt(var + LN_EPS)
    return cen * rstd, rstd


def _ret_specs(steps, per_step, reverse):
    def n_of(n):
        return (steps - 1 - n) if reverse else n

    rows = per_step * RET_CHUNK
    q_spec = pl.BlockSpec((rows, RET_QK_WIDTH), lambda n: (n_of(n), 0))
    k_spec = pl.BlockSpec((rows, RET_QK_WIDTH), lambda n: (n_of(n), 1))
    vv = pl.BlockSpec((rows, RET_V_WIDTH), lambda n: (n_of(n), 0))
    pos = pl.BlockSpec((rows, LANES), lambda n: (n_of(n), 0))
    per_head = pl.BlockSpec((RET_HEADS, RET_CHUNK, LANES), lambda n: (0, 0, 0))
    c_dec = pl.BlockSpec((RET_HEADS, 8, LANES), lambda n: (0, 0, 0))
    state = pl.BlockSpec((RET_HEADS, per_step, RET_QK_DIM, RET_V_DIM), lambda n: (0, n_of(n), 0, 0))
    return q_spec, k_spec, vv, pos, per_head, c_dec, state


def _qk_cols(h):
    return slice(h * RET_QK_DIM, (h + 1) * RET_QK_DIM)


def _v_cols(h):
    return slice(h * RET_V_DIM, (h + 1) * RET_V_DIM)


def _ret_fwd(h_b, h_c, h_d, tables):
    T = h_b.shape[0]
    nc = T // RET_CHUNK
    per_step = _pick(nc, (RET_STEP_CHUNKS, 1))
    steps = nc // per_step
    q_spec, k_spec, vv, pos, per_head, c_dec, state = _ret_specs(steps, per_step, False)

    def body(q_ref, k_ref, v_ref, g_ref, cos_ref, sin_ref, dec_ref, kd_ref, qd_ref, cd_ref,
             y_ref, o_ref, st_ref, state_ref):
        @pl.when(pl.program_id(0) == 0)
        def _():
            state_ref[...] = jnp.zeros_like(state_ref)

        for c in range(per_step):
            rows = pl.ds(c * RET_CHUNK, RET_CHUNK)
            cos_t, sin_t = cos_ref[rows, :], sin_ref[rows, :]
            for h in range(RET_HEADS):
                q = _rotary(q_ref[rows, _qk_cols(h)], cos_t, sin_t) * (RET_QK_DIM ** -0.5)
                k = _rotary(k_ref[rows, _qk_cols(h)], cos_t, sin_t)
                v = v_ref[rows, _v_cols(h)]
                prev = state_ref[h]
                scores = _dot(q.astype(BF16), k.astype(BF16), _NT) * dec_ref[h]
                inner = _dot(scores.astype(BF16), v, _NN)
                cross = _dot((q * qd_ref[h]).astype(BF16), prev.astype(BF16), _NN)
                o = inner + cross
                st_ref[h, c] = prev
                kv = _dot((k * kd_ref[h]).astype(BF16), v, _TN)
                state_ref[h] = prev * cd_ref[h, 0:1, 0:1] + kv
                o_ref[rows, _v_cols(h)] = o
                normed, _ = _head_norm(o)
                gate = g_ref[rows, _v_cols(h)]
                y_ref[rows, _v_cols(h)] = (gate * jax.nn.sigmoid(gate) * normed).astype(BF16)

    return pl.pallas_call(
        body,
        name="ret_fwd",
        grid=(steps,),
        in_specs=[q_spec, k_spec, vv, vv, pos, pos, per_head, per_head, per_head, c_dec],
        out_specs=[vv, vv, state],
        out_shape=[jax.ShapeDtypeStruct((T, RET_V_WIDTH), BF16),
                   jax.ShapeDtypeStruct((T, RET_V_WIDTH), F32),
                   jax.ShapeDtypeStruct((RET_HEADS, nc, RET_QK_DIM, RET_V_DIM), F32)],
        scratch_shapes=[pltpu.VMEM((RET_HEADS, RET_QK_DIM, RET_V_DIM), F32)],
        compiler_params=_params(("arbitrary",)),
    )(h_b, h_b, h_c, h_d, *tables)


def _ret_bwd(d_y, o_pre, states, h_b, h_c, h_d, tables, rider=None):
    T = h_b.shape[0]
    nc = T // RET_CHUNK
    per_step = _pick(nc, (RET_STEP_CHUNKS, 1))
    steps = nc // per_step
    q_spec, k_spec, vv, pos, per_head, c_dec, state = _ret_specs(steps, per_step, True)
    n_ride = rider.n if rider is not None else 0

    def body(*refs):
        (dy_ref, o_ref, st_ref, q_ref, k_ref, v_ref, g_ref, cos_ref, sin_ref, dec_ref, kd_ref, qd_ref,
         cd_ref) = refs[:13]
        dq_ref, dk_ref, dv_ref, dg_ref = refs[13 + n_ride:17 + n_ride]
        carry_ref = refs[17 + 2 * n_ride]
        ids = [pl.program_id(0)]
        if rider is not None:
            ride = (refs[13:13 + n_ride], refs[17 + n_ride:17 + 2 * n_ride], refs[-3:])
            rider.start_at_first(ids, ride)

        @pl.when(ids[0] == 0)
        def _():
            carry_ref[...] = jnp.zeros_like(carry_ref)

        scale = RET_QK_DIM ** -0.5
        for c in reversed(range(per_step)):
            rows = pl.ds(c * RET_CHUNK, RET_CHUNK)
            cos_t, sin_t = cos_ref[rows, :], sin_ref[rows, :]
            for h in range(RET_HEADS):
                q = _rotary(q_ref[rows, _qk_cols(h)], cos_t, sin_t) * scale
                k = _rotary(k_ref[rows, _qk_cols(h)], cos_t, sin_t)
                v = v_ref[rows, _v_cols(h)]
                decay, k_dec, q_dec = dec_ref[h], kd_ref[h], qd_ref[h]
                chunk_decay = cd_ref[h, 0:1, 0:1]
                state = st_ref[h, c].astype(BF16)
                later = carry_ref[h]
                later_b = later.astype(BF16)

                gate = g_ref[rows, _v_cols(h)]
                sig = jax.nn.sigmoid(gate)
                silu = gate * sig
                normed, rstd = _head_norm(o_ref[rows, _v_cols(h)])
                d_y = dy_ref[rows, _v_cols(h)]
                dg_ref[rows, _v_cols(h)] = (d_y * normed * (sig * (1.0 + gate * (1.0 - sig)))).astype(BF16)
                d_n = d_y * silu
                d_o = rstd * (d_n - jnp.mean(d_n, axis=1, keepdims=True)
                              - normed * jnp.mean(d_n * normed, axis=1, keepdims=True))
                d_ob = d_o.astype(BF16)

                qb, kb = q.astype(BF16), k.astype(BF16)
                qd_b, kd_b = (q * q_dec).astype(BF16), (k * k_dec).astype(BF16)
                scores = _dot(qb, kb, _NT) * decay
                d_scores = (_dot(d_ob, v, _NT) * decay).astype(BF16)
                dq = _dot(d_scores, kb, _NN) + _dot(d_ob, state, _NT) * q_dec
                dk = _dot(d_scores, qb, _TN) + _dot(v, later_b, _NT) * k_dec
                dv = _dot(scores.astype(BF16), d_ob, _TN) + _dot(kd_b, later_b, _NN)
                carry_ref[h] = _dot(qd_b, d_ob, _TN) + chunk_decay * later
                dq_ref[rows, _qk_cols(h)] = _rotary_transpose(dq * scale, cos_t, sin_t).astype(BF16)
                dk_ref[rows, _qk_cols(h)] = _rotary_transpose(dk, cos_t, sin_t).astype(BF16)
                dv_ref[rows, _v_cols(h)] = dv.astype(BF16)
        if rider is not None:
            rider.wait_at_last(ids, (steps,), ride)

    qk_out = pl.BlockSpec((per_step * RET_CHUNK, RET_QK_WIDTH), lambda n: (steps - 1 - n, 0))
    in_specs = [vv, vv, state, q_spec, k_spec, vv, vv, pos, pos, per_head, per_head, per_head, c_dec]
    out_specs = [qk_out, qk_out, vv, vv]
    out_shape = [jax.ShapeDtypeStruct((T, RET_QK_WIDTH), BF16), jax.ShapeDtypeStruct((T, RET_QK_WIDTH), BF16),
                 jax.ShapeDtypeStruct((T, RET_V_WIDTH), BF16), jax.ShapeDtypeStruct((T, RET_V_WIDTH), BF16)]
    args = (d_y, o_pre, states, h_b, h_b, h_c, h_d) + tuple(tables)
    scratch = [pltpu.VMEM((RET_HEADS, RET_QK_DIM, RET_V_DIM), F32)]
    if rider is not None:
        in_specs, args = in_specs + rider.specs, args + tuple(rider.bufs)
        out_specs, out_shape = out_specs + rider.specs, out_shape + rider.out_shape
        scratch = scratch + rider.scratch
    outs = pl.pallas_call(
        body,
        name="ret_bwd",
        grid=(steps,),
        in_specs=in_specs,
        out_specs=out_specs,
        out_shape=out_shape,
        scratch_shapes=scratch,
        compiler_params=_params(("arbitrary",)),
    )(*args)
    return outs[0], outs[1], outs[2], outs[3], list(outs[4:])


def _proj_tiles(h, x):
    return h[:, 0:1536], h[:, 1536:2560], h[:, 2560:3584], h[:, 3584:4608], h[:, 4608:6656], x


def _gate_mix_tiles(y_ret, h_e, b_gate, y_sb):
    gates = jax.nn.sigmoid(h_e + b_gate)
    return y_ret, gates[:, :D_MODEL] * y_sb + gates[:, D_MODEL:] * y_ret


def _gate_mix_grad_tiles(d_mix, h_e, b_gate, y_sb, y_ret):
    gates = jax.nn.sigmoid(h_e + b_gate)
    g0, g1 = gates[:, :D_MODEL], gates[:, D_MODEL:]
    d_e = jnp.concatenate([d_mix * y_sb * g0 * (1.0 - g0), d_mix * y_ret * g1 * (1.0 - g1)], axis=1)
    return d_mix * g0, d_mix * g1, d_e, d_e


def _ln_stats(u):
    mu = jnp.mean(u, axis=1, keepdims=True)
    cen = u - mu
    var = jnp.mean(cen * cen, axis=1, keepdims=True)
    rstd = lax.rsqrt(var + LN_EPS)
    return cen * rstd, rstd


def _ln_input_grad(d_out, gain, xhat, rstd):
    d_hat = d_out * gain
    return rstd * (d_hat - jnp.mean(d_hat, axis=1, keepdims=True)
                   - xhat * jnp.mean(d_hat * xhat, axis=1, keepdims=True))


def _ln_tiles(sub, x_prev, gain, bias):
    xhat, rstd = _ln_stats(DN_ALPHA * x_prev + sub)
    return xhat * gain + bias, xhat, rstd


def _ln_after_ln_tiles(sub, prev_hat, prev_gain, prev_bias, gain, bias):
    return _ln_tiles(sub, prev_hat * prev_gain + prev_bias, gain, bias)


def _residual_tiles(d_sub, res):
    return (d_sub + DN_ALPHA * res,)


def _ln_grad_tiles(d_sub, res, xhat, rstd, gain):
    d_out = d_sub + DN_ALPHA * res
    du = _ln_input_grad(d_out, gain, xhat, rstd)
    return du, du, d_out * xhat, d_out


def _ln_loss_tiles(sub, prev_hat, prev_gain, prev_bias, gain, bias, target):
    xhat, rstd = _ln_stats(DN_ALPHA * (prev_hat * prev_gain + prev_bias) + sub)
    diff = xhat * gain + bias - target
    d_out = diff * (1.0 / D_MODEL)
    du = _ln_input_grad(d_out, gain, xhat, rstd)
    return du, du, diff * diff, d_out * xhat, d_out


def _mem_probs(q_h, k_h):
    s = _dot(q_h, k_h, _NT) * (MEM_HEAD_DIM ** -0.5)
    e = jnp.exp(s - jnp.max(s, axis=1, keepdims=True))
    return e / jnp.sum(e, axis=1, keepdims=True)


def _xattn_fwd(q, kv):
    T, mem_len = q.shape[0], kv.shape[0]
    tq = _pick(T, (512, 256, 128))

    def body(q_ref, kv_ref, o_ref):
        for h in range(MEM_HEADS):
            cols = slice(h * MEM_HEAD_DIM, (h + 1) * MEM_HEAD_DIM)
            vcols = slice(D_MODEL + h * MEM_HEAD_DIM, D_MODEL + (h + 1) * MEM_HEAD_DIM)
            p = _mem_probs(q_ref[:, cols], kv_ref[:, cols])
            o_ref[:, cols] = _dot(p.astype(BF16), kv_ref[:, vcols], _NN).astype(BF16)

    return pl.pallas_call(
        body,
        name="xattn_fwd",
        grid=(T // tq,),
        in_specs=[pl.BlockSpec((tq, D_MODEL), lambda i: (i, 0)),
                  pl.BlockSpec((mem_len, 2 * D_MODEL), lambda i: (0, 0))],
        out_specs=pl.BlockSpec((tq, D_MODEL), lambda i: (i, 0)),
        out_shape=jax.ShapeDtypeStruct((T, D_MODEL), BF16),
        compiler_params=_params(("parallel",)),
    )(q, kv)


def _xattn_bwd(q, kv, d_o):
    T, mem_len = q.shape[0], kv.shape[0]
    tq = _pick(T, (512, 256, 128))

    def body(q_ref, kv_ref, do_ref, dq_ref, dkv_ref):
        @pl.when(pl.program_id(0) == 0)
        def _():
            dkv_ref[...] = jnp.zeros_like(dkv_ref)

        for h in range(MEM_HEADS):
            cols = slice(h * MEM_HEAD_DIM, (h + 1) * MEM_HEAD_DIM)
            vcols = slice(D_MODEL + h * MEM_HEAD_DIM, D_MODEL + (h + 1) * MEM_HEAD_DIM)
            q_h, k_h, do_h = q_ref[:, cols], kv_ref[:, cols], do_ref[:, cols]
            p = _mem_probs(q_h, k_h)
            dp = _dot(do_h, kv_ref[:, vcols], _NT)
            ds = p * (dp - jnp.sum(dp * p, axis=1, keepdims=True))
            dsb = (ds * (MEM_HEAD_DIM ** -0.5)).astype(BF16)
            dq_ref[:, cols] = _dot(dsb, k_h, _NN).astype(BF16)
            dkv_ref[:, cols] += _dot(dsb, q_h, _TN)
            dkv_ref[:, vcols] += _dot(p.astype(BF16), do_h, _TN)

    row = pl.BlockSpec((tq, D_MODEL), lambda i: (i, 0))
    full = pl.BlockSpec((mem_len, 2 * D_MODEL), lambda i: (0, 0))
    return pl.pallas_call(
        body,
        name="xattn_bwd",
        grid=(T // tq,),
        in_specs=[row, full, row],
        out_specs=[row, full],
        out_shape=[jax.ShapeDtypeStruct((T, D_MODEL), BF16), jax.ShapeDtypeStruct((mem_len, 2 * D_MODEL), F32)],
        compiler_params=_params(("arbitrary",)),
    )(q, kv, d_o)


def _swiglu_tiles(f):
    a, b = f[:, :FFN_HIDDEN], f[:, FFN_HIDDEN:]
    return f, a * jax.nn.sigmoid(a) * b


def _swiglu_grad_tiles(d_hidden, f):
    f = f.astype(F32)
    a, b = f[:, :FFN_HIDDEN], f[:, FFN_HIDDEN:]
    sig = jax.nn.sigmoid(a)
    return (jnp.concatenate([d_hidden * b * (sig * (1.0 + a * (1.0 - sig))), d_hidden * (a * sig)], axis=1),)


def _local_step(x, mem, w_in, small, target, fetch, ship):
    T = x.shape[0]
    tables = _ret_tables(T)
    memb = mem.astype(BF16)

    (h_a, h_b, h_c, h_d, h_e, xb), w_ffn = fetch(
        ("w_ffn_in", "w_ffn_out"),
        lambda rider: _as_host(rider, _mm_fused(
            x, w_in, mode="nn", name="proj_in", extras=[], pass_a=True,
            outs=[(1536, BF16), (1024, F32), (1024, BF16), (1024, F32), (2048, F32), (D_MODEL, BF16)],
            epilogue=_proj_tiles, max_rows=256, rider=rider)))
    (a_sb, r_mat), w_mix = fetch(("w_sb_o", "w_ret_o", "w_mix_o", "w_mem_q", "w_mem_kv", "w_mem_o"),
                                 lambda rider: _sb_fwd(h_a, rider))
    w = {**w_ffn, **w_mix}
    y_gated, o_pre, states = _ret_fwd(h_b, h_c, h_d, tables)
    y_sb = _mm(a_sb, w["w_sb_o"], mode="nn", out_dtype=F32, name="sb_out")
    row_f32, row_bf16 = (D_MODEL, F32), (D_MODEL, BF16)
    ln_outs = [row_bf16, row_f32, (1, F32)]
    y_ret, mix_in = _mm_fused(y_gated, w["w_ret_o"], mode="nn", name="ret_out", extras=[h_e, small["b_gate"], y_sb],
                              outs=[row_f32, row_bf16], epilogue=_gate_mix_tiles)
    x1b, xhat1, rstd1 = _mm_fused(mix_in, w["w_mix_o"], mode="nn", name="mix_out",
                                  extras=[x, small["ln1_g"], small["ln1_b"]], outs=ln_outs, epilogue=_ln_tiles)
    q_m = _mm(x1b, w["w_mem_q"], mode="nn", out_dtype=BF16, name="mem_q")
    kv_m = _mm(memb, w["w_mem_kv"], mode="nn", out_dtype=BF16, name="mem_kv")
    o_m = _xattn_fwd(q_m, kv_m)
    x2b, xhat2, rstd2 = _mm_fused(
        o_m, w["w_mem_o"], mode="nn", name="mem_out", outs=ln_outs, epilogue=_ln_after_ln_tiles,
        extras=[xhat1, small["ln1_g"], small["ln1_b"], small["ln2_g"], small["ln2_b"]])
    f, hidden = _mm_fused(x2b, w["w_ffn_in"], mode="nn", name="ffn_in", extras=[],
                          outs=[(2 * FFN_HIDDEN, BF16), (FFN_HIDDEN, BF16)], epilogue=_swiglu_tiles)
    du_outs, col = [row_f32, row_bf16], D_MODEL
    du3, du3b, loss_cols, d_ln3_g, d_ln3_b = _mm_fused(
        hidden, w["w_ffn_out"], mode="nn", name="ffn_out", outs=du_outs, sums=[col, col, col], epilogue=_ln_loss_tiles,
        extras=[xhat2, small["ln2_g"], small["ln2_b"], small["ln3_g"], small["ln3_b"], target])

    g_ffn_out = _mm(hidden, du3b, mode="tn", out_dtype=BF16, name="g_ffn_out")
    (d_f,) = _mm_fused(du3b, w["w_ffn_out"], mode="nt", name="d_hidden", extras=[f],
                       outs=[(2 * FFN_HIDDEN, BF16)], epilogue=_swiglu_grad_tiles)
    g_ffn_in = _mm(x2b, d_f, mode="tn", out_dtype=BF16, name="g_ffn_in")
    du2, du2b, d_ln2_g, d_ln2_b = ship(
        {"w_ffn_out": g_ffn_out},
        lambda rider: _as_host(rider, _mm_fused(
            d_f, w["w_ffn_in"], mode="nt", name="d_x2", extras=[du3, xhat2, rstd2, small["ln2_g"]], outs=du_outs,
            sums=[col, col], epilogue=_ln_grad_tiles, rider=rider, max_rows=256)))
    g_mem_o = _mm(o_m, du2b, mode="tn", out_dtype=BF16, name="g_mem_o")
    d_om = _mm(du2b, w["w_mem_o"], mode="nt", out_dtype=BF16, name="d_om")
    d_qm, d_kvm = _xattn_bwd(q_m, kv_m, d_om)
    g_mem_q = _mm(x1b, d_qm, mode="tn", out_dtype=BF16, name="g_mem_q")
    g_mem_kv = _mm(memb, d_kvm.astype(BF16), mode="tn", out_dtype=BF16, name="g_mem_kv")
    du1, du1b, d_ln1_g, d_ln1_b = _mm_fused(
        d_qm, w["w_mem_q"], mode="nt", name="d_x1", extras=[du2, xhat1, rstd1, small["ln1_g"]], outs=du_outs,
        sums=[col, col], epilogue=_ln_grad_tiles)
    g_mix_o = _mm(mix_in, du1b, mode="tn", out_dtype=BF16, name="g_mix_o")
    d_ysb, d_yret, d_e, d_b_gate = _mm_fused(
        du1b, w["w_mix_o"], mode="nt", name="d_mix_in", extras=[h_e, small["b_gate"], y_sb, y_ret],
        outs=[row_bf16, row_bf16, (2 * D_MODEL, BF16)], sums=[2 * D_MODEL], epilogue=_gate_mix_grad_tiles)
    g_sb_o = _mm(a_sb, d_ysb, mode="tn", out_dtype=BF16, name="g_sb_o")
    g_ret_o = _mm(y_gated, d_yret, mode="tn", out_dtype=BF16, name="g_ret_o")
    d_asb = _mm(d_ysb, w["w_sb_o"], mode="nt", out_dtype=BF16, name="d_asb")
    d_ygated = _mm(d_yret, w["w_ret_o"], mode="nt", out_dtype=F32, name="d_ygated")
    small_grads = {"b_gate": d_b_gate, "ln1_g": d_ln1_g, "ln1_b": d_ln1_b, "ln2_g": d_ln2_g, "ln2_b": d_ln2_b,
                   "ln3_g": d_ln3_g, "ln3_b": d_ln3_b, "loss_cols": loss_cols}
    d_rq, d_rk, d_c, d_d = ship({"w_mem_kv": g_mem_kv, "w_mem_q": g_mem_q, "w_mem_o": g_mem_o, "w_mix_o": g_mix_o},
                                lambda rider: _ret_bwd(d_ygated, o_pre, states, h_b, h_c, h_d, tables, rider))
    d_q, d_k, d_v = ship({"w_ffn_in": g_ffn_in, "w_ret_o": g_ret_o, "w_sb_o": g_sb_o, "small": small_grads},
                         lambda rider: _sb_bwd(h_a, d_asb, r_mat, rider))
    d_h = [("sb_q", d_q), ("sb_k", d_k), ("sb_v", d_v), ("ret_q", d_rq), ("ret_k", d_rk), ("ret_v", d_c),
           ("ret_g", d_d), ("gate", d_e)]
    g_in = jnp.concatenate([_mm(xb, piece, mode="tn", out_dtype=BF16, name="g_in_" + tag) for tag, piece in d_h],
                           axis=1)
    (d_x,) = ship({"w_in": g_in},
                  lambda rider: _as_host(rider, _mm_fused(
                      [piece for _, piece in d_h], w_in, mode="nt", name="d_x", extras=[du1], outs=[(D_MODEL, F32)],
                      epilogue=_residual_tiles, rider=rider, max_rows=256)))
    return d_x


def _adamw_math(w, g, m, v):
    m = ADAM_B1 * m + (1.0 - ADAM_B1) * g
    v = ADAM_B2 * v + (1.0 - ADAM_B2) * jnp.square(g)
    m_hat = m / (1.0 - ADAM_B1 ** ADAM_STEP)
    v_hat = v / (1.0 - ADAM_B2 ** ADAM_STEP)
    delta = -ADAM_LR * (m_hat / (jnp.sqrt(v_hat) + ADAM_EPS) + ADAM_WD * w)
    return delta, m, v


ADAMW_BLOCK_BYTES = 40 * 1024 * 1024


def _adamw(items, name):
    n = len(items)
    R, C = items[0][1].shape
    row_bytes = 2 * n * C * (N_DEV * items[0][0].dtype.itemsize + 7 * 4)
    most = min(R, 256, ADAMW_BLOCK_BYTES // row_bytes)
    tr = max(t for t in range(16, most + 1, 16) if R % t == 0) if R >= 16 else R

    def body(*refs):
        ins, outs = refs[:4 * n], refs[4 * n:]
        for k in range(n):
            p_ref, w_ref, m_ref, v_ref = ins[4 * k:4 * k + 4]
            g = p_ref[0].astype(F32)
            for j in range(1, N_DEV):
                g = g + p_ref[j].astype(F32)
            delta, nm, nv = _adamw_math(w_ref[...], g, m_ref[...], v_ref[...])
            for o_ref, val in zip(outs[4 * k:4 * k + 4], (g, delta, nm, nv)):
                o_ref[...] = val

    blk = pl.BlockSpec((tr, C), lambda i: (i, 0))
    out = jax.ShapeDtypeStruct((R, C), F32)
    res = pl.pallas_call(
        body,
        name=name,
        grid=(R // tr,),
        in_specs=[pl.BlockSpec((N_DEV, tr, C), lambda i: (0, i, 0)), blk, blk, blk] * n,
        out_specs=[blk] * (4 * n),
        out_shape=[out] * (4 * n),
        compiler_params=_params(("parallel",)),
    )(*[a for item in items for a in item])
    return [tuple(res[4 * k:4 * k + 4]) for k in range(n)]


_SHARD_AXIS = {"w_in": 1, "w_sb_o": 1, "w_ret_o": 0, "w_mix_o": 0, "w_mem_q": 0, "w_mem_kv": 1, "w_mem_o": 0,
               "w_ffn_in": 1, "w_ffn_out": 0}
_MATRICES = tuple(_SHARD_AXIS)
_SMALL = ("b_gate", "ln1_g", "ln1_b", "ln2_g", "ln2_b", "ln3_g", "ln3_b")
_WEIGHT_ORDER = ("w_in", "b_gate", "w_sb_o", "w_ret_o", "w_mix_o", "ln1_g", "ln1_b", "w_mem_q", "w_mem_kv", "w_mem_o",
                 "ln2_g", "ln2_b", "w_ffn_in", "w_ffn_out", "ln3_g", "ln3_b")


def _assemble(name, gathered):
    if _SHARD_AXIS[name] == 0:
        return gathered.reshape(-1, gathered.shape[2])
    return jnp.transpose(gathered, (1, 0, 2)).reshape(gathered.shape[1], -1)


def _to_slots(name, full):
    if _SHARD_AXIS[name] == 0:
        return full.reshape(N_DEV, full.shape[0] // N_DEV, full.shape[1])
    return jnp.transpose(full.reshape(full.shape[0], N_DEV, full.shape[1] // N_DEV), (1, 0, 2))


SMALL_ROWS = 16


def _pack_small(vals):
    return jnp.concatenate([vals["b_gate"].reshape(2, D_MODEL)] + [vals[n] for n in _SMALL[1:]], axis=0)


def _unpack_small(packed):
    out = {"b_gate": packed[0:2].reshape(1, 2 * D_MODEL)}
    for i, n in enumerate(_SMALL[1:]):
        out[n] = packed[2 + i:3 + i]
    return out


def kernel(x, mem, w_in, b_gate, w_sb_o, w_ret_o, w_mix_o, ln1_g, ln1_b, w_mem_q, w_mem_kv, w_mem_o, ln2_g, ln2_b, w_ffn_in, w_ffn_out, ln3_g, ln3_b, loss_target, m_w_in, m_b_gate, m_w_sb_o, m_w_ret_o, m_w_mix_o, m_ln1_g, m_ln1_b, m_w_mem_q, m_w_mem_kv, m_w_mem_o, m_ln2_g, m_ln2_b, m_w_ffn_in, m_w_ffn_out, m_ln3_g, m_ln3_b, v_w_in, v_b_gate, v_w_sb_o, v_w_ret_o, v_w_mix_o, v_ln1_g, v_ln1_b, v_w_mem_q, v_w_mem_kv, v_w_mem_o, v_ln2_g, v_ln2_b, v_w_ffn_in, v_w_ffn_out, v_ln3_g, v_ln3_b):
    weights = dict(w_in=w_in, b_gate=b_gate, w_sb_o=w_sb_o, w_ret_o=w_ret_o, w_mix_o=w_mix_o, ln1_g=ln1_g, ln1_b=ln1_b,
                   w_mem_q=w_mem_q, w_mem_kv=w_mem_kv, w_mem_o=w_mem_o, ln2_g=ln2_g, ln2_b=ln2_b, w_ffn_in=w_ffn_in,
                   w_ffn_out=w_ffn_out, ln3_g=ln3_g, ln3_b=ln3_b)
    mom1 = dict(w_in=m_w_in, b_gate=m_b_gate, w_sb_o=m_w_sb_o, w_ret_o=m_w_ret_o, w_mix_o=m_w_mix_o, ln1_g=m_ln1_g,
                ln1_b=m_ln1_b, w_mem_q=m_w_mem_q, w_mem_kv=m_w_mem_kv, w_mem_o=m_w_mem_o, ln2_g=m_ln2_g, ln2_b=m_ln2_b,
                w_ffn_in=m_w_ffn_in, w_ffn_out=m_w_ffn_out, ln3_g=m_ln3_g, ln3_b=m_ln3_b)
    mom2 = dict(w_in=v_w_in, b_gate=v_b_gate, w_sb_o=v_w_sb_o, w_ret_o=v_w_ret_o, w_mix_o=v_w_mix_o, ln1_g=v_ln1_g,
                ln1_b=v_ln1_b, w_mem_q=v_w_mem_q, w_mem_kv=v_w_mem_kv, w_mem_o=v_w_mem_o, ln2_g=v_ln2_g, ln2_b=v_ln2_b,
                w_ffn_in=v_w_ffn_in, w_ffn_out=v_w_ffn_out, ln3_g=v_ln3_g, ln3_b=v_ln3_b)

    (gathered_in,) = _exchange([weights["w_in"][0].astype(BF16)], False, "gather_w_in")
    received = {}

    def fetch(names, host):
        res = host(_Rider([weights[n][0].astype(BF16) for n in names], False))
        return res[:-1], {n: _assemble(n, g) for n, g in zip(names, res[-1])}

    def ship(grads, host):
        names = list(grads)
        bufs = []
        for n in names:
            if n == "small":
                part = jnp.concatenate([_pack_small(grads[n]), grads[n]["loss_cols"],
                                        jnp.zeros((SMALL_ROWS - 9, D_MODEL), F32)], axis=0)
                bufs.append(jnp.broadcast_to(part[None], (N_DEV,) + part.shape))
            else:
                bufs.append(_to_slots(n, grads[n]).astype(BF16))
        res = host(_Rider(bufs, True))
        received.update(zip(names, res[-1]))
        return res[:-1]

    small = {n: weights[n] for n in _SMALL}
    d_x = _local_step(x[0], mem[0], _assemble("w_in", gathered_in), small, loss_target[0], fetch, ship)

    same_shape = {}
    for n in _MATRICES:
        same_shape.setdefault(weights[n][0].shape, []).append(n)
    new = {}
    for names in same_shape.values():
        items = [(received[n], weights[n][0], mom1[n][0], mom2[n][0]) for n in names]
        new.update(zip(names, _adamw(items, "adamw_" + "_".join(names))))
    (packed,) = _adamw([(received["small"][:, :8], _pack_small({n: weights[n] for n in _SMALL}),
                         _pack_small({n: mom1[n] for n in _SMALL}), _pack_small({n: mom2[n] for n in _SMALL}))],
                       "adamw_small")
    small_new = [_unpack_small(p) for p in packed]
    loss = jnp.sum(received["small"][:, 8]) * (0.5 / D_MODEL)

    outs = [loss, d_x[None]]
    for slot in range(4):
        for n in _WEIGHT_ORDER:
            outs.append(new[n][slot][None] if n in new else small_new[slot][n])
    return tuple(outs)
```
